```python
import functools
import jax, jax.numpy as jnp
from jax import lax
import numpy as np

D_MODEL = 2048
BATCH = 8
SEQ = 2048
DEPTH = 2

MIX_WIDTH = D_MODEL
HEAD_DIM = 128
Q_BLOCK = 128
SB_WIDTH = MIX_WIDTH // 2
SB_HEADS = SB_WIDTH // HEAD_DIM
SC_WIDTH = MIX_WIDTH - SB_WIDTH
SC_GROUPS = SC_WIDTH // HEAD_DIM
CONV_WIDTH = 3
CHUNK = 128
SG_WIDTH = MIX_WIDTH // 2
SG_GROUP_DIM = 128
SG_GROUPS = SG_WIDTH // SG_GROUP_DIM
FOX_WIDTH = MIX_WIDTH - SG_WIDTH
FOX_HEADS = FOX_WIDTH // HEAD_DIM
IN_AB = 3 * SB_WIDTH + 3 * SC_WIDTH
IN_CD = 2 * SG_WIDTH + 3 * FOX_WIDTH + FOX_HEADS
D_FF = 5632
EPS = 1e-6

kernel_name = "hybrid_stickbreak_shortconv_chunkgmlp_fox_block"


def rmsnorm(x, g):
    xf = x.astype(jnp.float32)
    y = xf * lax.rsqrt(jnp.mean(xf * xf, axis=-1, keepdims=True) + EPS)
    return (y * g.astype(jnp.float32)).astype(x.dtype)


def layernorm(x, g):
    xf = x.astype(jnp.float32)
    mu = jnp.mean(xf, axis=-1, keepdims=True)
    xc = xf - mu
    y = xc * lax.rsqrt(jnp.mean(xc * xc, axis=-1, keepdims=True) + EPS)
    return (y * g.astype(jnp.float32)).astype(x.dtype)


def causal_dwconv(x, w):
    K = w.shape[0]
    S = x.shape[1]
    xp = jnp.pad(x, ((0, 0), (K - 1, 0), (0, 0)))
    y = xp[:, 0:S] * w[0]
    for j in range(1, K):
        y = y + xp[:, j:j + S] * w[j]
    return y


def split_heads(t, n_heads):
    return t.reshape(t.shape[0], t.shape[1], n_heads, -1)


def stick_breaking_attention(q, k, v):
    S = q.shape[1]
    scale = HEAD_DIM ** -0.5
    outs = []
    for i in range(S // Q_BLOCK):
        q0 = i * Q_BLOCK
        kend = q0 + Q_BLOCK
        qb = q[:, q0:kend].astype(jnp.float32)
        kb = k[:, :kend].astype(jnp.float32)
        vb = v[:, :kend].astype(jnp.float32)
        z = jnp.einsum('bqhd,bkhd->bhqk', qb, kb) * scale
        t_idx = q0 + jnp.arange(Q_BLOCK)[:, None]
        s_idx = jnp.arange(kend)[None, :]
        mask = s_idx < t_idx
        log_1mb = jnp.where(mask, jax.nn.log_sigmoid(-z), 0.0)
        later = lax.cumsum(log_1mb, axis=3, reverse=True) - log_1mb
        a = jnp.where(mask, jnp.exp(jax.nn.log_sigmoid(z) + later), 0.0)
        outs.append(jnp.einsum('bhqk,bkhd->bqhd', a, vb))
    return jnp.concatenate(outs, axis=1).astype(q.dtype)


def forgetting_attention(q, k, v, log_f):
    S = q.shape[1]
    scale = HEAD_DIM ** -0.5
    c = jnp.cumsum(log_f, axis=1).transpose(0, 2, 1)
    outs = []
    for i in range(S // Q_BLOCK):
        q0 = i * Q_BLOCK
        kend = q0 + Q_BLOCK
        qb = q[:, q0:kend].astype(jnp.float32)
        kb = k[:, :kend].astype(jnp.float32)
        vb = v[:, :kend].astype(jnp.float32)
        logits = jnp.einsum('bqhd,bkhd->bhqk', qb, kb) * scale
        logits = logits + c[:, :, q0:kend, None] - c[:, :, None, :kend]
        t_idx = q0 + jnp.arange(Q_BLOCK)[:, None]
        s_idx = jnp.arange(kend)[None, :]
        p = jax.nn.softmax(jnp.where(s_idx <= t_idx, logits, -jnp.inf), axis=-1)
        outs.append(jnp.einsum('bhqk,bkhd->bqhd', p, vb))
    return jnp.concatenate(outs, axis=1).astype(q.dtype)


def chunked_spatial_gate(u, v, w_s, b_s, g):
    B, S, W = v.shape
    v = layernorm(v, g)
    vc = v.reshape(B, S // CHUNK, CHUNK, SG_GROUPS, SG_GROUP_DIM)
    w = w_s * jnp.tril(jnp.ones((CHUNK, CHUNK), w_s.dtype))
    mixed = jnp.einsum('gts,bnsgc->bntgc', w, vc) + b_s.T[None, None, :, :, None]
    return u * mixed.reshape(B, S, W)


def mixer_ab(h, w_in, sc_conv_w, w_out):
    B, S, _ = h.shape
    p = h @ w_in
    q, k, v, gate_b, gate_c, hin = jnp.split(
        p, [SB_WIDTH, 2 * SB_WIDTH, 3 * SB_WIDTH,
            3 * SB_WIDTH + SC_WIDTH, 3 * SB_WIDTH + 2 * SC_WIDTH], axis=-1)
    a_out = stick_breaking_attention(split_heads(q, SB_HEADS), split_heads(k, SB_HEADS),
                                     split_heads(v, SB_HEADS)).reshape(B, S, SB_WIDTH)
    b_out = gate_b * causal_dwconv(gate_c * hin, sc_conv_w)
    return jnp.concatenate([a_out, b_out], axis=-1) @ w_out


def mixer_cd(h, w_in, fox_b_f, sg_w, sg_b, sg_norm_g, w_out):
    B, S, _ = h.shape
    p = h @ w_in
    u, v, q, k, vv, f = jnp.split(
        p, [SG_WIDTH, 2 * SG_WIDTH, 2 * SG_WIDTH + FOX_WIDTH,
            2 * SG_WIDTH + 2 * FOX_WIDTH, 2 * SG_WIDTH + 3 * FOX_WIDTH], axis=-1)
    c_out = chunked_spatial_gate(jax.nn.gelu(u), jax.nn.gelu(v), sg_w, sg_b, sg_norm_g)
    log_f = jax.nn.log_sigmoid(f.astype(jnp.float32) + fox_b_f.astype(jnp.float32))
    d_out = forgetting_attention(split_heads(q, FOX_HEADS), split_heads(k, FOX_HEADS),
                                 split_heads(vv, FOX_HEADS), log_f).reshape(B, S, FOX_WIDTH)
    return jnp.concatenate([c_out, d_out], axis=-1) @ w_out


def conv_ffn(h, w_up, conv_w, w_down):
    a = causal_dwconv(h @ w_up, conv_w)
    gate, up = jnp.split(a, 2, axis=-1)
    return (jax.nn.silu(gate) * up) @ w_down


def _fwd_setup_inputs(seed: int = 0) -> dict:
    key = jax.random.key(seed)
    ks = iter(jax.random.split(key, 32))
    f32 = jnp.float32

    def w(shape, fan_in):
        return jax.random.normal(next(ks), shape, f32) * (fan_in ** -0.5)

    def gain(n):
        return 1.0 + 0.02 * jax.random.normal(next(ks), (n,), f32)

    inp = {}
    inp["x"] = jax.random.normal(next(ks), (BATCH, SEQ, D_MODEL), f32)
    inp["l0_mix_norm_g"] = gain(D_MODEL)
    inp["l0_w_in"] = w((D_MODEL, IN_AB), D_MODEL)
    inp["l0_sc_conv_w"] = w((CONV_WIDTH, SC_WIDTH), CONV_WIDTH)
    inp["l0_w_out"] = w((MIX_WIDTH, D_MODEL), MIX_WIDTH)
    inp["l0_ffn_norm_g"] = gain(D_MODEL)
    inp["l0_ffn_up"] = w((D_MODEL, 2 * D_FF), D_MODEL)
    inp["l0_ffn_conv_w"] = w((CONV_WIDTH, 2 * D_FF), CONV_WIDTH)
    inp["l0_ffn_down"] = w((D_FF, D_MODEL), D_FF)
    inp["l1_mix_norm_g"] = gain(D_MODEL)
    inp["l1_w_in"] = w((D_MODEL, IN_CD), D_MODEL)
    inp["l1_fox_b_f"] = 1.0 + 0.5 * jax.random.normal(next(ks), (FOX_HEADS,), f32)
    inp["l1_sg_w"] = w((SG_GROUPS, CHUNK, CHUNK), CHUNK)
    inp["l1_sg_b"] = 1.0 + 0.1 * jax.random.normal(next(ks), (SG_GROUPS, CHUNK), f32)
    inp["l1_sg_norm_g"] = gain(SG_WIDTH)
    inp["l1_w_out"] = w((MIX_WIDTH, D_MODEL), MIX_WIDTH)
    inp["l1_ffn_norm_g"] = gain(D_MODEL)
    inp["l1_ffn_up"] = w((D_MODEL, 2 * D_FF), D_MODEL)
    inp["l1_ffn_conv_w"] = w((CONV_WIDTH, 2 * D_FF), CONV_WIDTH)
    inp["l1_ffn_down"] = w((D_FF, D_MODEL), D_FF)
    inp["final_norm_g"] = gain(D_MODEL)
    return inp


def _fwd_reference(x, l0_mix_norm_g, l0_w_in, l0_sc_conv_w, l0_w_out, l0_ffn_norm_g,
              l0_ffn_up, l0_ffn_conv_w, l0_ffn_down,
              l1_mix_norm_g, l1_w_in, l1_fox_b_f, l1_sg_w, l1_sg_b, l1_sg_norm_g,
              l1_w_out, l1_ffn_norm_g, l1_ffn_up, l1_ffn_conv_w, l1_ffn_down,
              final_norm_g):
    layers = [
        (l0_mix_norm_g,
         functools.partial(mixer_ab, w_in=l0_w_in, sc_conv_w=l0_sc_conv_w, w_out=l0_w_out),
         l0_ffn_norm_g, l0_ffn_up, l0_ffn_conv_w, l0_ffn_down),
        (l1_mix_norm_g,
         functools.partial(mixer_cd, w_in=l1_w_in, fox_b_f=l1_fox_b_f, sg_w=l1_sg_w,
                           sg_b=l1_sg_b, sg_norm_g=l1_sg_norm_g, w_out=l1_w_out),
         l1_ffn_norm_g, l1_ffn_up, l1_ffn_conv_w, l1_ffn_down),
    ]
    for i in range(DEPTH):
        mix_g, mixer, ffn_g, w_up, conv_w, w_down = layers[i]
        x = x + mixer(rmsnorm(x, mix_g))
        x = x + conv_ffn(rmsnorm(x, ffn_g), w_up, conv_w, w_down)
    return rmsnorm(x, final_norm_g)


import jax as _jax
import jax.numpy as _jnp

TWIN_FORMAT = 'train_step'
FWD_PARAMS = ['x', 'l0_mix_norm_g', 'l0_w_in', 'l0_sc_conv_w', 'l0_w_out', 'l0_ffn_norm_g', 'l0_ffn_up', 'l0_ffn_conv_w', 'l0_ffn_down', 'l1_mix_norm_g', 'l1_w_in', 'l1_fox_b_f', 'l1_sg_w', 'l1_sg_b', 'l1_sg_norm_g', 'l1_w_out', 'l1_ffn_norm_g', 'l1_ffn_up', 'l1_ffn_conv_w', 'l1_ffn_down', 'final_norm_g']
TWIN_WEIGHTS = ['l0_mix_norm_g', 'l0_w_in', 'l0_sc_conv_w', 'l0_w_out', 'l0_ffn_norm_g', 'l0_ffn_up', 'l0_ffn_conv_w', 'l0_ffn_down', 'l1_mix_norm_g', 'l1_w_in', 'l1_fox_b_f', 'l1_sg_w', 'l1_sg_b', 'l1_sg_norm_g', 'l1_w_out', 'l1_ffn_norm_g', 'l1_ffn_up', 'l1_ffn_conv_w', 'l1_ffn_down', 'final_norm_g']
TWIN_DIFF_INPUT = 'x'
TWIN_INPUTS = ['x', 'l0_mix_norm_g', 'l0_w_in', 'l0_sc_conv_w', 'l0_w_out', 'l0_ffn_norm_g', 'l0_ffn_up', 'l0_ffn_conv_w', 'l0_ffn_down', 'l1_mix_norm_g', 'l1_w_in', 'l1_fox_b_f', 'l1_sg_w', 'l1_sg_b', 'l1_sg_norm_g', 'l1_w_out', 'l1_ffn_norm_g', 'l1_ffn_up', 'l1_ffn_conv_w', 'l1_ffn_down', 'final_norm_g', 'loss_target', 'm_l0_mix_norm_g', 'm_l0_w_in', 'm_l0_sc_conv_w', 'm_l0_w_out', 'm_l0_ffn_norm_g', 'm_l0_ffn_up', 'm_l0_ffn_conv_w', 'm_l0_ffn_down', 'm_l1_mix_norm_g', 'm_l1_w_in', 'm_l1_fox_b_f', 'm_l1_sg_w', 'm_l1_sg_b', 'm_l1_sg_norm_g', 'm_l1_w_out', 'm_l1_ffn_norm_g', 'm_l1_ffn_up', 'm_l1_ffn_conv_w', 'm_l1_ffn_down', 'm_final_norm_g', 'v_l0_mix_norm_g', 'v_l0_w_in', 'v_l0_sc_conv_w', 'v_l0_w_out', 'v_l0_ffn_norm_g', 'v_l0_ffn_up', 'v_l0_ffn_conv_w', 'v_l0_ffn_down', 'v_l1_mix_norm_g', 'v_l1_w_in', 'v_l1_fox_b_f', 'v_l1_sg_w', 'v_l1_sg_b', 'v_l1_sg_norm_g', 'v_l1_w_out', 'v_l1_ffn_norm_g', 'v_l1_ffn_up', 'v_l1_ffn_conv_w', 'v_l1_ffn_down', 'v_final_norm_g']
TWIN_OUTPUTS = ['loss', 'grad_x', 'grad_l0_mix_norm_g', 'grad_l0_w_in', 'grad_l0_sc_conv_w', 'grad_l0_w_out', 'grad_l0_ffn_norm_g', 'grad_l0_ffn_up', 'grad_l0_ffn_conv_w', 'grad_l0_ffn_down', 'grad_l1_mix_norm_g', 'grad_l1_w_in', 'grad_l1_fox_b_f', 'grad_l1_sg_w', 'grad_l1_sg_b', 'grad_l1_sg_norm_g', 'grad_l1_w_out', 'grad_l1_ffn_norm_g', 'grad_l1_ffn_up', 'grad_l1_ffn_conv_w', 'grad_l1_ffn_down', 'grad_final_norm_g', 'delta_l0_mix_norm_g', 'delta_l0_w_in', 'delta_l0_sc_conv_w', 'delta_l0_w_out', 'delta_l0_ffn_norm_g', 'delta_l0_ffn_up', 'delta_l0_ffn_conv_w', 'delta_l0_ffn_down', 'delta_l1_mix_norm_g', 'delta_l1_w_in', 'delta_l1_fox_b_f', 'delta_l1_sg_w', 'delta_l1_sg_b', 'delta_l1_sg_norm_g', 'delta_l1_w_out', 'delta_l1_ffn_norm_g', 'delta_l1_ffn_up', 'delta_l1_ffn_conv_w', 'delta_l1_ffn_down', 'delta_final_norm_g', 'new_m_l0_mix_norm_g', 'new_m_l0_w_in', 'new_m_l0_sc_conv_w', 'new_m_l0_w_out', 'new_m_l0_ffn_norm_g', 'new_m_l0_ffn_up', 'new_m_l0_ffn_conv_w', 'new_m_l0_ffn_down', 'new_m_l1_mix_norm_g', 'new_m_l1_w_in', 'new_m_l1_fox_b_f', 'new_m_l1_sg_w', 'new_m_l1_sg_b', 'new_m_l1_sg_norm_g', 'new_m_l1_w_out', 'new_m_l1_ffn_norm_g', 'new_m_l1_ffn_up', 'new_m_l1_ffn_conv_w', 'new_m_l1_ffn_down', 'new_m_final_norm_g', 'new_v_l0_mix_norm_g', 'new_v_l0_w_in', 'new_v_l0_sc_conv_w', 'new_v_l0_w_out', 'new_v_l0_ffn_norm_g', 'new_v_l0_ffn_up', 'new_v_l0_ffn_conv_w', 'new_v_l0_ffn_down', 'new_v_l1_mix_norm_g', 'new_v_l1_w_in', 'new_v_l1_fox_b_f', 'new_v_l1_sg_w', 'new_v_l1_sg_b', 'new_v_l1_sg_norm_g', 'new_v_l1_w_out', 'new_v_l1_ffn_norm_g', 'new_v_l1_ffn_up', 'new_v_l1_ffn_conv_w', 'new_v_l1_ffn_down', 'new_v_final_norm_g']
TWIN_LEAF_KINDS = {'loss': 'loss', 'grad_x': 'grad_x', 'grad_l0_mix_norm_g': 'grad_w', 'grad_l0_w_in': 'grad_w', 'grad_l0_sc_conv_w': 'grad_w', 'grad_l0_w_out': 'grad_w', 'grad_l0_ffn_norm_g': 'grad_w', 'grad_l0_ffn_up': 'grad_w', 'grad_l0_ffn_conv_w': 'grad_w', 'grad_l0_ffn_down': 'grad_w', 'grad_l1_mix_norm_g': 'grad_w', 'grad_l1_w_in': 'grad_w', 'grad_l1_fox_b_f': 'grad_w', 'grad_l1_sg_w': 'grad_w', 'grad_l1_sg_b': 'grad_w', 'grad_l1_sg_norm_g': 'grad_w', 'grad_l1_w_out': 'grad_w', 'grad_l1_ffn_norm_g': 'grad_w', 'grad_l1_ffn_up': 'grad_w', 'grad_l1_ffn_conv_w': 'grad_w', 'grad_l1_ffn_down': 'grad_w', 'grad_final_norm_g': 'grad_w', 'delta_l0_mix_norm_g': 'delta_w', 'delta_l0_w_in': 'delta_w', 'delta_l0_sc_conv_w': 'delta_w', 'delta_l0_w_out': 'delta_w', 'delta_l0_ffn_norm_g': 'delta_w', 'delta_l0_ffn_up': 'delta_w', 'delta_l0_ffn_conv_w': 'delta_w', 'delta_l0_ffn_down': 'delta_w', 'delta_l1_mix_norm_g': 'delta_w', 'delta_l1_w_in': 'delta_w', 'delta_l1_fox_b_f': 'delta_w', 'delta_l1_sg_w': 'delta_w', 'delta_l1_sg_b': 'delta_w', 'delta_l1_sg_norm_g': 'delta_w', 'delta_l1_w_out': 'delta_w', 'delta_l1_ffn_norm_g': 'delta_w', 'delta_l1_ffn_up': 'delta_w', 'delta_l1_ffn_conv_w': 'delta_w', 'delta_l1_ffn_down': 'delta_w', 'delta_final_norm_g': 'delta_w', 'new_m_l0_mix_norm_g': 'new_m', 'new_m_l0_w_in': 'new_m', 'new_m_l0_sc_conv_w': 'new_m', 'new_m_l0_w_out': 'new_m', 'new_m_l0_ffn_norm_g': 'new_m', 'new_m_l0_ffn_up': 'new_m', 'new_m_l0_ffn_conv_w': 'new_m', 'new_m_l0_ffn_down': 'new_m', 'new_m_l1_mix_norm_g': 'new_m', 'new_m_l1_w_in': 'new_m', 'new_m_l1_fox_b_f': 'new_m', 'new_m_l1_sg_w': 'new_m', 'new_m_l1_sg_b': 'new_m', 'new_m_l1_sg_norm_g': 'new_m', 'new_m_l1_w_out': 'new_m', 'new_m_l1_ffn_norm_g': 'new_m', 'new_m_l1_ffn_up': 'new_m', 'new_m_l1_ffn_conv_w': 'new_m', 'new_m_l1_ffn_down': 'new_m', 'new_m_final_norm_g': 'new_m', 'new_v_l0_mix_norm_g': 'new_v', 'new_v_l0_w_in': 'new_v', 'new_v_l0_sc_conv_w': 'new_v', 'new_v_l0_w_out': 'new_v', 'new_v_l0_ffn_norm_g': 'new_v', 'new_v_l0_ffn_up': 'new_v', 'new_v_l0_ffn_conv_w': 'new_v', 'new_v_l0_ffn_down': 'new_v', 'new_v_l1_mix_norm_g': 'new_v', 'new_v_l1_w_in': 'new_v', 'new_v_l1_fox_b_f': 'new_v', 'new_v_l1_sg_w': 'new_v', 'new_v_l1_sg_b': 'new_v', 'new_v_l1_sg_norm_g': 'new_v', 'new_v_l1_w_out': 'new_v', 'new_v_l1_ffn_norm_g': 'new_v', 'new_v_l1_ffn_up': 'new_v', 'new_v_l1_ffn_conv_w': 'new_v', 'new_v_l1_ffn_down': 'new_v', 'new_v_final_norm_g': 'new_v'}


def _forward(args):
    return _fwd_reference(*[args[k] for k in FWD_PARAMS])


def _output_shape():
    out = _jax.eval_shape(lambda: _forward(_fwd_setup_inputs(0)))
    return out.shape, out.dtype

N_MICROBATCH = 1
ADAM_LR = 0.001
ADAM_B1 = 0.9
ADAM_B2 = 0.999
ADAM_EPS = 1e-08
ADAM_WD = 0.01
ADAM_STEP = 10
PER_EXAMPLE_BATCH_AXIS = {'x': 0, 'loss_target': 0}
SHARED_INPUTS = []
_WEIGHT_DTYPES = {'l0_mix_norm_g': _jnp.float32, 'l0_w_in': _jnp.float32, 'l0_sc_conv_w': _jnp.float32, 'l0_w_out': _jnp.float32, 'l0_ffn_norm_g': _jnp.float32, 'l0_ffn_up': _jnp.float32, 'l0_ffn_conv_w': _jnp.float32, 'l0_ffn_down': _jnp.float32, 'l1_mix_norm_g': _jnp.float32, 'l1_w_in': _jnp.float32, 'l1_fox_b_f': _jnp.float32, 'l1_sg_w': _jnp.float32, 'l1_sg_b': _jnp.float32, 'l1_sg_norm_g': _jnp.float32, 'l1_w_out': _jnp.float32, 'l1_ffn_norm_g': _jnp.float32, 'l1_ffn_up': _jnp.float32, 'l1_ffn_conv_w': _jnp.float32, 'l1_ffn_down': _jnp.float32, 'final_norm_g': _jnp.float32}
MOMENT_SCALE = {'l0_mix_norm_g': 8.062487e-02, 'l0_w_in': 4.518731e-02, 'l0_sc_conv_w': 5.961702e-02, 'l0_w_out': 4.921199e-02, 'l0_ffn_norm_g': 4.457060e-02, 'l0_ffn_up': 1.827617e-02, 'l0_ffn_conv_w': 1.866237e-02, 'l0_ffn_down': 2.980565e-02, 'l1_mix_norm_g': 3.776460e-02, 'l1_w_in': 2.317689e-02, 'l1_fox_b_f': 1.192407e-01, 'l1_sg_w': 1.936589e-02, 'l1_sg_b': 2.702228e-02, 'l1_sg_norm_g': 2.002751e-02, 'l1_w_out': 2.892064e-02, 'l1_ffn_norm_g': 3.229916e-02, 'l1_ffn_up': 1.354248e-02, 'l1_ffn_conv_w': 1.380064e-02, 'l1_ffn_down': 2.209723e-02, 'final_norm_g': 8.011163e+00}


def _to_microbatches(a, axis):
    t = _jnp.moveaxis(a, axis, 0)
    t = t.reshape((N_MICROBATCH, t.shape[0] // N_MICROBATCH) + t.shape[1:])
    return _jnp.moveaxis(t, 1, axis + 1)


def setup_inputs(seed: int = 0) -> dict:
    inp = _fwd_setup_inputs(seed)
    key = _jax.random.fold_in(_jax.random.key(seed), 7919)
    shape, _ = _output_shape()
    out = dict(inp)
    out["loss_target"] = _jax.random.normal(_jax.random.fold_in(key, 0), shape, _jnp.float32)
    for i, name in enumerate(TWIN_WEIGHTS):
        w = inp[name].astype(_jnp.float32)
        if MOMENT_SCALE is None:
            s = _jnp.sqrt(_jnp.mean(_jnp.square(w)) + 1e-30)
        else:
            s = MOMENT_SCALE[name]
        km, kv = _jax.random.split(_jax.random.fold_in(key, i + 1))
        out[name] = w
        out["m_" + name] = s * _jax.random.normal(km, w.shape, _jnp.float32)
        out["v_" + name] = (s * s) * _jax.random.uniform(kv, w.shape, _jnp.float32, 0.5, 1.5)
    if N_MICROBATCH > 1:
        for name, axis in PER_EXAMPLE_BATCH_AXIS.items():
            out[name] = _to_microbatches(out[name], axis)
    return {'x': out['x'], 'l0_mix_norm_g': out['l0_mix_norm_g'], 'l0_w_in': out['l0_w_in'], 'l0_sc_conv_w': out['l0_sc_conv_w'], 'l0_w_out': out['l0_w_out'], 'l0_ffn_norm_g': out['l0_ffn_norm_g'], 'l0_ffn_up': out['l0_ffn_up'], 'l0_ffn_conv_w': out['l0_ffn_conv_w'], 'l0_ffn_down': out['l0_ffn_down'], 'l1_mix_norm_g': out['l1_mix_norm_g'], 'l1_w_in': out['l1_w_in'], 'l1_fox_b_f': out['l1_fox_b_f'], 'l1_sg_w': out['l1_sg_w'], 'l1_sg_b': out['l1_sg_b'], 'l1_sg_norm_g': out['l1_sg_norm_g'], 'l1_w_out': out['l1_w_out'], 'l1_ffn_norm_g': out['l1_ffn_norm_g'], 'l1_ffn_up': out['l1_ffn_up'], 'l1_ffn_conv_w': out['l1_ffn_conv_w'], 'l1_ffn_down': out['l1_ffn_down'], 'final_norm_g': out['final_norm_g'], 'loss_target': out['loss_target'], 'm_l0_mix_norm_g': out['m_l0_mix_norm_g'], 'm_l0_w_in': out['m_l0_w_in'], 'm_l0_sc_conv_w': out['m_l0_sc_conv_w'], 'm_l0_w_out': out['m_l0_w_out'], 'm_l0_ffn_norm_g': out['m_l0_ffn_norm_g'], 'm_l0_ffn_up': out['m_l0_ffn_up'], 'm_l0_ffn_conv_w': out['m_l0_ffn_conv_w'], 'm_l0_ffn_down': out['m_l0_ffn_down'], 'm_l1_mix_norm_g': out['m_l1_mix_norm_g'], 'm_l1_w_in': out['m_l1_w_in'], 'm_l1_fox_b_f': out['m_l1_fox_b_f'], 'm_l1_sg_w': out['m_l1_sg_w'], 'm_l1_sg_b': out['m_l1_sg_b'], 'm_l1_sg_norm_g': out['m_l1_sg_norm_g'], 'm_l1_w_out': out['m_l1_w_out'], 'm_l1_ffn_norm_g': out['m_l1_ffn_norm_g'], 'm_l1_ffn_up': out['m_l1_ffn_up'], 'm_l1_ffn_conv_w': out['m_l1_ffn_conv_w'], 'm_l1_ffn_down': out['m_l1_ffn_down'], 'm_final_norm_g': out['m_final_norm_g'], 'v_l0_mix_norm_g': out['v_l0_mix_norm_g'], 'v_l0_w_in': out['v_l0_w_in'], 'v_l0_sc_conv_w': out['v_l0_sc_conv_w'], 'v_l0_w_out': out['v_l0_w_out'], 'v_l0_ffn_norm_g': out['v_l0_ffn_norm_g'], 'v_l0_ffn_up': out['v_l0_ffn_up'], 'v_l0_ffn_conv_w': out['v_l0_ffn_conv_w'], 'v_l0_ffn_down': out['v_l0_ffn_down'], 'v_l1_mix_norm_g': out['v_l1_mix_norm_g'], 'v_l1_w_in': out['v_l1_w_in'], 'v_l1_fox_b_f': out['v_l1_fox_b_f'], 'v_l1_sg_w': out['v_l1_sg_w'], 'v_l1_sg_b': out['v_l1_sg_b'], 'v_l1_sg_norm_g': out['v_l1_sg_norm_g'], 'v_l1_w_out': out['v_l1_w_out'], 'v_l1_ffn_norm_g': out['v_l1_ffn_norm_g'], 'v_l1_ffn_up': out['v_l1_ffn_up'], 'v_l1_ffn_conv_w': out['v_l1_ffn_conv_w'], 'v_l1_ffn_down': out['v_l1_ffn_down'], 'v_final_norm_g': out['v_final_norm_g']}


def _loss(weights, diff, rest, loss_target):
    with _jax.named_scope("forward"):
        args = {**rest, TWIN_DIFF_INPUT: diff, **{k: w.astype(_WEIGHT_DTYPES[k]) for k, w in weights.items()}}
        y = _forward(args)
    with _jax.named_scope("loss_head"):
        err = _jnp.square(y.astype(_jnp.float32) - loss_target)
        return 0.5 * _jnp.sum(_jnp.mean(err, axis=-1)) if err.ndim else 0.5 * err


def _adamw(w, g, m, v):
    m = ADAM_B1 * m + (1.0 - ADAM_B1) * g
    v = ADAM_B2 * v + (1.0 - ADAM_B2) * _jnp.square(g)
    m_hat = m / (1.0 - ADAM_B1 ** ADAM_STEP)
    v_hat = v / (1.0 - ADAM_B2 ** ADAM_STEP)
    delta = -ADAM_LR * (m_hat / (_jnp.sqrt(v_hat) + ADAM_EPS) + ADAM_WD * w)
    return delta, m, v


def reference(x, l0_mix_norm_g, l0_w_in, l0_sc_conv_w, l0_w_out, l0_ffn_norm_g, l0_ffn_up, l0_ffn_conv_w, l0_ffn_down, l1_mix_norm_g, l1_w_in, l1_fox_b_f, l1_sg_w, l1_sg_b, l1_sg_norm_g, l1_w_out, l1_ffn_norm_g, l1_ffn_up, l1_ffn_conv_w, l1_ffn_down, final_norm_g, loss_target, m_l0_mix_norm_g, m_l0_w_in, m_l0_sc_conv_w, m_l0_w_out, m_l0_ffn_norm_g, m_l0_ffn_up, m_l0_ffn_conv_w, m_l0_ffn_down, m_l1_mix_norm_g, m_l1_w_in, m_l1_fox_b_f, m_l1_sg_w, m_l1_sg_b, m_l1_sg_norm_g, m_l1_w_out, m_l1_ffn_norm_g, m_l1_ffn_up, m_l1_ffn_conv_w, m_l1_ffn_down, m_final_norm_g, v_l0_mix_norm_g, v_l0_w_in, v_l0_sc_conv_w, v_l0_w_out, v_l0_ffn_norm_g, v_l0_ffn_up, v_l0_ffn_conv_w, v_l0_ffn_down, v_l1_mix_norm_g, v_l1_w_in, v_l1_fox_b_f, v_l1_sg_w, v_l1_sg_b, v_l1_sg_norm_g, v_l1_w_out, v_l1_ffn_norm_g, v_l1_ffn_up, v_l1_ffn_conv_w, v_l1_ffn_down, v_final_norm_g):
    given = dict(x=x, l0_mix_norm_g=l0_mix_norm_g, l0_w_in=l0_w_in, l0_sc_conv_w=l0_sc_conv_w, l0_w_out=l0_w_out, l0_ffn_norm_g=l0_ffn_norm_g, l0_ffn_up=l0_ffn_up, l0_ffn_conv_w=l0_ffn_conv_w, l0_ffn_down=l0_ffn_down, l1_mix_norm_g=l1_mix_norm_g, l1_w_in=l1_w_in, l1_fox_b_f=l1_fox_b_f, l1_sg_w=l1_sg_w, l1_sg_b=l1_sg_b, l1_sg_norm_g=l1_sg_norm_g, l1_w_out=l1_w_out, l1_ffn_norm_g=l1_ffn_norm_g, l1_ffn_up=l1_ffn_up, l1_ffn_conv_w=l1_ffn_conv_w, l1_ffn_down=l1_ffn_down, final_norm_g=final_norm_g, loss_target=loss_target, m_l0_mix_norm_g=m_l0_mix_norm_g, m_l0_w_in=m_l0_w_in, m_l0_sc_conv_w=m_l0_sc_conv_w, m_l0_w_out=m_l0_w_out, m_l0_ffn_norm_g=m_l0_ffn_norm_g, m_l0_ffn_up=m_l0_ffn_up, m_l0_ffn_conv_w=m_l0_ffn_conv_w, m_l0_ffn_down=m_l0_ffn_down, m_l1_mix_norm_g=m_l1_mix_norm_g, m_l1_w_in=m_l1_w_in, m_l1_fox_b_f=m_l1_fox_b_f, m_l1_sg_w=m_l1_sg_w, m_l1_sg_b=m_l1_sg_b, m_l1_sg_norm_g=m_l1_sg_norm_g, m_l1_w_out=m_l1_w_out, m_l1_ffn_norm_g=m_l1_ffn_norm_g, m_l1_ffn_up=m_l1_ffn_up, m_l1_ffn_conv_w=m_l1_ffn_conv_w, m_l1_ffn_down=m_l1_ffn_down, m_final_norm_g=m_final_norm_g, v_l0_mix_norm_g=v_l0_mix_norm_g, v_l0_w_in=v_l0_w_in, v_l0_sc_conv_w=v_l0_sc_conv_w, v_l0_w_out=v_l0_w_out, v_l0_ffn_norm_g=v_l0_ffn_norm_g, v_l0_ffn_up=v_l0_ffn_up, v_l0_ffn_conv_w=v_l0_ffn_conv_w, v_l0_ffn_down=v_l0_ffn_down, v_l1_mix_norm_g=v_l1_mix_norm_g, v_l1_w_in=v_l1_w_in, v_l1_fox_b_f=v_l1_fox_b_f, v_l1_sg_w=v_l1_sg_w, v_l1_sg_b=v_l1_sg_b, v_l1_sg_norm_g=v_l1_sg_norm_g, v_l1_w_out=v_l1_w_out, v_l1_ffn_norm_g=v_l1_ffn_norm_g, v_l1_ffn_up=v_l1_ffn_up, v_l1_ffn_conv_w=v_l1_ffn_conv_w, v_l1_ffn_down=v_l1_ffn_down, v_final_norm_g=v_final_norm_g)
    weights = {n: given[n] for n in TWIN_WEIGHTS}
    shared = {n: given[n] for n in SHARED_INPUTS}
    per_example = {n: given[n] for n in ['x']}
    grad_fn = _jax.value_and_grad(_loss, argnums=(0, 1))

    def one_microbatch(ex, loss_target):
        ex = dict(ex)
        diff = ex.pop(TWIN_DIFF_INPUT)
        return grad_fn(weights, diff, {**shared, **ex}, loss_target)

    if N_MICROBATCH == 1:
        loss, (grad_w, grad_x) = one_microbatch(per_example, given["loss_target"])
    else:
        def body(carry, xs):
            loss_sum, grad_sum = carry
            l_k, (gw_k, gx_k) = one_microbatch(xs[0], xs[1])
            with _jax.named_scope("update"):
                return (loss_sum + l_k, _jax.tree.map(_jnp.add, grad_sum, gw_k)), gx_k

        init = (_jnp.zeros((), _jnp.float32), _jax.tree.map(_jnp.zeros_like, weights))
        (loss, grad_w), grad_x = _jax.lax.scan(body, init, (per_example, given["loss_target"]))
    with _jax.named_scope("update"):
        delta_w, new_m, new_v = {}, {}, {}
        for n in TWIN_WEIGHTS:
            delta_w[n], new_m[n], new_v[n] = _adamw(weights[n], grad_w[n], given["m_" + n], given["v_" + n])
    return (loss, grad_x, *[grad_w[n] for n in TWIN_WEIGHTS], *[delta_w[n] for n in TWIN_WEIGHTS],
            *[new_m[n] for n in TWIN_WEIGHTS], *[new_v[n] for n in TWIN_WEIGHTS])
```

```python
import functools
import math

import jax
import jax.numpy as jnp
from jax import lax
from jax.experimental import pallas as pl
from jax.experimental.pallas import tpu as pltpu

F32 = jnp.float32
BF16 = jnp.bfloat16
HD = 128
EPS = 1e-6
CONV_K = 3
VMEM_LIMIT_BYTES = 48 << 20
NDEV = 8
MESH = pl.DeviceIdType.MESH

ADAM_LR = 0.001
ADAM_B1 = 0.9
ADAM_B2 = 0.999
ADAM_EPS = 1e-08
ADAM_WD = 0.01
ADAM_STEP = 10


def _cp(sem):
    return pltpu.CompilerParams(dimension_semantics=sem, vmem_limit_bytes=VMEM_LIMIT_BYTES)


def _pick(n, prefs):
    for p in prefs:
        if n % p == 0:
            return p
    return n


def _dot(a, b):
    return jnp.dot(a, b, preferred_element_type=F32)


def _dot_nt(a, b):
    return lax.dot_general(a, b, (((1,), (1,)), ((), ())), preferred_element_type=F32)


def _dot_tn(a, b):
    return lax.dot_general(a, b, (((0,), (0,)), ((), ())), preferred_element_type=F32)


def _split3(x):
    hi = x.astype(BF16)
    r = x - hi.astype(F32)
    mid = r.astype(BF16)
    lo = (r - mid.astype(F32)).astype(BF16)
    return hi, mid, lo


def _dot_ones_right(x, ones_bf16):
    hi, mid, lo = _split3(x)
    return _dot(hi, ones_bf16) + _dot(mid, ones_bf16) + _dot(lo, ones_bf16)


def _dot_ones_left(ones_bf16, x):
    hi, mid, lo = _split3(x)
    return _dot(ones_bf16, hi) + _dot(ones_bf16, mid) + _dot(ones_bf16, lo)


def _iota2(shape, axis):
    return lax.broadcasted_iota(jnp.int32, shape, axis)


def mm_nn(a, w2d, nb, name, out_dtype=BF16, res=None, tm=None, tn=None, tk=None):
    M, K = a.shape
    n = w2d.shape[1]
    assert w2d.shape[0] == nb * K
    tm = tm or _pick(M, (1024, 512, 256, 128))
    tn = tn or _pick(n, (1408, 1024, 768, 512, 256, 128))
    tk = tk or (K if K <= 2048 else _pick(K, (1408, 1024, 512, 256, 128)))
    nk, nt = K // tk, n // tn
    has_res = res is not None

    def body(*refs):
        if has_res:
            a_ref, w_ref, r_ref, o_ref = refs[:4]
        else:
            a_ref, w_ref, o_ref = refs[:3]
            r_ref = None
        part = _dot(a_ref[...], w_ref[...])

        def finish(acc):
            if r_ref is not None:
                acc = acc + r_ref[...].astype(F32)
            o_ref[...] = acc.astype(o_ref.dtype)

        if nk == 1:
            finish(part)
        else:
            acc_ref = refs[-1]
            k = pl.program_id(3)

            @pl.when(k == 0)
            def _():
                acc_ref[...] = part

            @pl.when(k > 0)
            def _():
                acc_ref[...] += part

            @pl.when(k == nk - 1)
            def _():
                finish(acc_ref[...])

    in_specs = [
        pl.BlockSpec((tm, tk), lambda i, j, t, k: (i, k)),
        pl.BlockSpec((tk, tn), lambda i, j, t, k: (j * nk + k, t)),
    ]
    args = [a, w2d]
    out_spec = pl.BlockSpec((tm, tn), lambda i, j, t, k: (i, j * nt + t))
    if has_res:
        in_specs.append(out_spec)
        args.append(res)
    return pl.pallas_call(
        body,
        grid=(M // tm, nb, nt, nk),
        in_specs=in_specs,
        out_specs=out_spec,
        out_shape=jax.ShapeDtypeStruct((M, nb * n), out_dtype),
        scratch_shapes=[pltpu.VMEM((tm, tn), F32)] if nk > 1 else [],
        compiler_params=_cp(("parallel", "parallel", "parallel", "arbitrary")),
        name=name,
    )(*args)


def mm_nt(dy2d, w2d, nb, M, K, name, out_dtype=BF16, res=None, dy_map=None, tm=None, tko=None, tn=None):
    n = w2d.shape[1]
    assert w2d.shape[0] == nb * K
    tm = tm or _pick(M, (1024, 512, 256, 128))
    tko = tko or _pick(K, (1024, 512, 256, 128))
    tn = tn or _pick(n, (1408, 1024, 768, 512, 256, 128))
    nt, nko = n // tn, K // tko
    has_res = res is not None
    if dy_map is None:
        dy_map = lambda i, j, t: (i, j * nt + t)

    def body(*refs):
        if has_res:
            d_ref, w_ref, r_ref, o_ref, acc_ref = refs
        else:
            d_ref, w_ref, o_ref, acc_ref = refs
            r_ref = None
        j, t = pl.program_id(2), pl.program_id(3)
        part = _dot_nt(d_ref[...], w_ref[...])
        first = jnp.logical_and(j == 0, t == 0)
        last = jnp.logical_and(j == nb - 1, t == nt - 1)

        @pl.when(first)
        def _():
            acc_ref[...] = part

        @pl.when(jnp.logical_not(first))
        def _():
            acc_ref[...] += part

        @pl.when(last)
        def _():
            acc = acc_ref[...]
            if r_ref is not None:
                acc = acc + r_ref[...].astype(F32)
            o_ref[...] = acc.astype(o_ref.dtype)

    in_specs = [
        pl.BlockSpec((tm, tn), lambda i, ko, j, t: dy_map(i, j, t)),
        pl.BlockSpec((tko, tn), lambda i, ko, j, t: (j * nko + ko, t)),
    ]
    args = [dy2d, w2d]
    out_spec = pl.BlockSpec((tm, tko), lambda i, ko, j, t: (i, ko))
    if has_res:
        in_specs.append(out_spec)
        args.append(res)
    return pl.pallas_call(
        body,
        grid=(M // tm, nko, nb, nt),
        in_specs=in_specs,
        out_specs=out_spec,
        out_shape=jax.ShapeDtypeStruct((M, K), out_dtype),
        scratch_shapes=[pltpu.VMEM((tm, tko), F32)],
        compiler_params=_cp(("parallel", "parallel", "arbitrary", "arbitrary")),
        name=name,
    )(*args)


def mm_tn(x, dy2d, nb, n, name, out_dtype=BF16, dy_map=None, tko=None, tn=None):
    S, K = x.shape
    tko = tko or _pick(K, (512, 256, 128))
    tn = tn or _pick(n, (1408, 1024, 768, 512, 256, 128))
    nt, nko = n // tn, K // tko
    if dy_map is None:
        dy_map = lambda j, t: (0, j * nt + t)

    def body(x_ref, d_ref, o_ref):
        o_ref[...] = _dot_tn(x_ref[...], d_ref[...]).astype(o_ref.dtype)

    return pl.pallas_call(
        body,
        grid=(nko, nb, nt),
        in_specs=[
            pl.BlockSpec((S, tko), lambda ko, j, t: (0, ko)),
            pl.BlockSpec((S, tn), lambda ko, j, t: dy_map(j, t)),
        ],
        out_specs=pl.BlockSpec((tko, tn), lambda ko, j, t: (j * nko + ko, t)),
        out_shape=jax.ShapeDtypeStruct((nb * K, n), out_dtype),
        compiler_params=_cp(("parallel", "parallel", "parallel")),
        name=name,
    )(x, dy2d)


def rms_fwd(x, g, name):
    S, D = x.shape
    tm = _pick(S, (256, 128))

    def body(x_ref, g_ref, o_ref):
        xf = x_ref[...]
        r = lax.rsqrt(jnp.mean(xf * xf, axis=-1, keepdims=True) + EPS)
        o_ref[...] = (xf * r * g_ref[...]).astype(o_ref.dtype)

    return pl.pallas_call(
        body,
        grid=(S // tm,),
        in_specs=[pl.BlockSpec((tm, D), lambda i: (i, 0)), pl.BlockSpec((1, D), lambda i: (0, 0))],
        out_specs=pl.BlockSpec((tm, D), lambda i: (i, 0)),
        out_shape=jax.ShapeDtypeStruct((S, D), BF16),
        compiler_params=_cp(("parallel",)),
        name=name,
    )(x, g.reshape(1, D))


def rms_bwd(x, g, dh, dres, name):
    S, D = x.shape
    tm = _pick(S, (256, 128))

    def body(x_ref, g_ref, dh_ref, dr_ref, dx_ref, dg_ref):
        i = pl.program_id(0)
        xf = x_ref[...]
        dh = dh_ref[...].astype(F32)
        r = lax.rsqrt(jnp.mean(xf * xf, axis=-1, keepdims=True) + EPS)
        gy = dh * g_ref[...]
        proj = jnp.mean(gy * xf, axis=-1, keepdims=True)
        dx_ref[...] = dr_ref[...] + r * gy - xf * (r * r * r * proj)
        dg = jnp.sum(dh * (xf * r), axis=0, keepdims=True)

        @pl.when(i == 0)
        def _():
            dg_ref[...] = dg

        @pl.when(i > 0)
        def _():
            dg_ref[...] += dg

    row = pl.BlockSpec((tm, D), lambda i: (i, 0))
    vec = pl.BlockSpec((1, D), lambda i: (0, 0))
    return pl.pallas_call(
        body,
        grid=(S // tm,),
        in_specs=[row, vec, row, row],
        out_specs=[row, vec],
        out_shape=[jax.ShapeDtypeStruct((S, D), F32), jax.ShapeDtypeStruct((1, D), F32)],
        compiler_params=_cp(("arbitrary",)),
        name=name,
    )(x, g.reshape(1, D), dh, dres)


def loss_head(x, g, target, name):
    S, D = x.shape
    tm = _pick(S, (256, 128))

    def body(x_ref, g_ref, t_ref, dx_ref, dg_ref, loss_ref):
        i = pl.program_id(0)
        xf = x_ref[...]
        gg = g_ref[...]
        r = lax.rsqrt(jnp.mean(xf * xf, axis=-1, keepdims=True) + EPS)
        xh = xf * r
        err = xh * gg - t_ref[...]
        part = (0.5 / D) * jnp.sum(err * err)
        dy = err * (1.0 / D)
        gy = dy * gg
        proj = jnp.mean(gy * xf, axis=-1, keepdims=True)
        dx_ref[...] = r * gy - xf * (r * r * r * proj)
        dg = jnp.sum(dy * xh, axis=0, keepdims=True)
        lossb = jnp.full(loss_ref.shape, part, F32)

        @pl.when(i == 0)
        def _():
            dg_ref[...] = dg
            loss_ref[...] = lossb

        @pl.when(i > 0)
        def _():
            dg_ref[...] += dg
            loss_ref[...] += lossb

    row = pl.BlockSpec((tm, D), lambda i: (i, 0))
    vec = pl.BlockSpec((1, D), lambda i: (0, 0))
    return pl.pallas_call(
        body,
        grid=(S // tm,),
        in_specs=[row, vec, row],
        out_specs=[row, vec, pl.BlockSpec((8, 128), lambda i: (0, 0))],
        out_shape=[
            jax.ShapeDtypeStruct((S, D), F32),
            jax.ShapeDtypeStruct((1, D), F32),
            jax.ShapeDtypeStruct((8, 128), F32),
        ],
        compiler_params=_cp(("arbitrary",)),
        name=name,
    )(x, g.reshape(1, D), target)


def _shift_down(s, k):
    if k == 0:
        return s
    return jnp.where(_iota2(s.shape, 0) >= k, pltpu.roll(s, k, axis=0), 0.0)


def _shift_up(s, k):
    if k == 0:
        return s
    n = s.shape[0]
    return jnp.where(_iota2(s.shape, 0) < n - k, pltpu.roll(s, n - k, axis=0), 0.0)


def _conv(s, w):
    return w[0:1] * _shift_down(s, 2) + w[1:2] * _shift_down(s, 1) + w[2:3] * s


def _conv_t(d, w):
    return w[2:3] * d + w[1:2] * _shift_up(d, 1) + w[0:1] * _shift_up(d, 2)


def _conv_dw(d, s):
    return [jnp.sum(d * _shift_down(s, CONV_K - 1 - k), axis=0, keepdims=True) for k in range(CONV_K)]


def sc_fwd(p, convw, cat, W, name):
    S = p.shape[0]
    tc = _pick(W, (256, 128))
    nc = W // tc

    def body(gb_ref, gc_ref, hi_ref, w_ref, cat_ref, o_ref):
        s = gc_ref[...].astype(F32) * hi_ref[...].astype(F32)
        o_ref[...] = (gb_ref[...].astype(F32) * _conv(s, w_ref[...])).astype(o_ref.dtype)

    col = lambda part: pl.BlockSpec((S, tc), lambda c: (0, part * nc + c))
    return pl.pallas_call(
        body,
        grid=(nc,),
        in_specs=[col(3), col(4), col(5), pl.BlockSpec((CONV_K, tc), lambda c: (0, c)), pl.BlockSpec(memory_space=pl.ANY)],
        out_specs=col(1),
        out_shape=jax.ShapeDtypeStruct(cat.shape, cat.dtype),
        input_output_aliases={4: 0},
        compiler_params=_cp(("parallel",)),
        name=name,
    )(p, p, p, convw, cat)


def sc_bwd(p, convw, dcat, dp, W, name):
    S = p.shape[0]
    tc = _pick(W, (256, 128))
    nc = W // tc

    def body(gb_ref, gc_ref, hi_ref, w_ref, do_ref, dp_in_ref, dp_ref, dw_ref):
        gb = gb_ref[...].astype(F32)
        gc = gc_ref[...].astype(F32)
        hi = hi_ref[...].astype(F32)
        w = w_ref[...]
        do = do_ref[...].astype(F32)
        s = gc * hi
        dcs = do * gb
        ds = _conv_t(dcs, w)
        dp_ref[0] = (do * _conv(s, w)).astype(dp_ref.dtype)
        dp_ref[1] = (ds * hi).astype(dp_ref.dtype)
        dp_ref[2] = (ds * gc).astype(dp_ref.dtype)
        for k, row in enumerate(_conv_dw(dcs, s)):
            dw_ref[k : k + 1, :] = row

    col = lambda part: pl.BlockSpec((S, tc), lambda c: (0, part * nc + c))
    return pl.pallas_call(
        body,
        grid=(nc,),
        in_specs=[
            col(3), col(4), col(5),
            pl.BlockSpec((CONV_K, tc), lambda c: (0, c)),
            pl.BlockSpec((S, tc), lambda c: (0, nc + c)),
            pl.BlockSpec(memory_space=pl.ANY),
        ],
        out_specs=[pl.BlockSpec((3, S, tc), lambda c: (1, 0, c)), pl.BlockSpec((CONV_K, tc), lambda c: (0, c))],
        out_shape=[jax.ShapeDtypeStruct(dp.shape, dp.dtype), jax.ShapeDtypeStruct((CONV_K, W), F32)],
        input_output_aliases={5: 0},
        compiler_params=_cp(("parallel",)),
        name=name,
    )(p, p, p, convw, dcat, dp)


def _silu_parts(a):
    sig = 1.0 / (1.0 + jnp.exp(-a))
    return a * sig, sig


def ffn_act_fwd(u, convw, F, name):
    S = u.shape[0]
    tc = _pick(F, (256, 128))
    nc = F // tc

    def body(ug_ref, uu_ref, wg_ref, wu_ref, o_ref):
        ag = _conv(ug_ref[...].astype(F32), wg_ref[...])
        au = _conv(uu_ref[...].astype(F32), wu_ref[...])
        o_ref[...] = (_silu_parts(ag)[0] * au).astype(o_ref.dtype)

    col = lambda half: pl.BlockSpec((S, tc), lambda c: (0, half * nc + c))
    wcol = lambda half: pl.BlockSpec((CONV_K, tc), lambda c: (0, half * nc + c))
    return pl.pallas_call(
        body,
        grid=(nc,),
        in_specs=[col(0), col(1), wcol(0), wcol(1)],
        out_specs=pl.BlockSpec((S, tc), lambda c: (0, c)),
        out_shape=jax.ShapeDtypeStruct((S, F), BF16),
        compiler_params=_cp(("parallel",)),
        name=name,
    )(u, u, convw, convw)


def ffn_act_bwd(u, convw, dact, F, name):
    S = u.shape[0]
    tc = _pick(F, (256, 128))
    nc = F // tc

    def body(ug_ref, uu_ref, wg_ref, wu_ref, da_ref, du_ref, dw_ref):
        ug = ug_ref[...].astype(F32)
        uu = uu_ref[...].astype(F32)
        wg = wg_ref[...]
        wu = wu_ref[...]
        da = da_ref[...].astype(F32)
        ag = _conv(ug, wg)
        au = _conv(uu, wu)
        sl, sig = _silu_parts(ag)
        dag = da * au * (sig * (1.0 + ag * (1.0 - sig)))
        dau = da * sl
        du_ref[0] = _conv_t(dag, wg).astype(du_ref.dtype)
        du_ref[1] = _conv_t(dau, wu).astype(du_ref.dtype)
        for k, (rg, ru) in enumerate(zip(_conv_dw(dag, ug), _conv_dw(dau, uu))):
            dw_ref[0, k : k + 1, :] = rg
            dw_ref[1, k : k + 1, :] = ru

    col = lambda half: pl.BlockSpec((S, tc), lambda c: (0, half * nc + c))
    wcol = lambda half: pl.BlockSpec((CONV_K, tc), lambda c: (0, half * nc + c))
    return pl.pallas_call(
        body,
        grid=(nc,),
        in_specs=[col(0), col(1), wcol(0), wcol(1), pl.BlockSpec((S, tc), lambda c: (0, c))],
        out_specs=[pl.BlockSpec((2, S, tc), lambda c: (0, 0, c)), pl.BlockSpec((2, CONV_K, tc), lambda c: (0, 0, c))],
        out_shape=[jax.ShapeDtypeStruct((2, S, F), BF16), jax.ShapeDtypeStruct((2, CONV_K, F), F32)],
        compiler_params=_cp(("parallel",)),
        name=name,
    )(u, u, convw, convw, dact)


def _softplus(z):
    return jnp.maximum(z, 0.0) + jnp.log(1.0 + jnp.exp(-jnp.abs(z)))


def _sb_block(q, kj, i, j, run, su):
    z = _dot_nt(q, kj) * (HD ** -0.5)
    mask = (_iota2(z.shape, 1) + j * HD) < (_iota2(z.shape, 0) + i * HD)
    sp = _softplus(z)
    l = jnp.where(mask, -sp, 0.0)
    later = _dot_ones_right(l, su) + run
    a = jnp.where(mask, jnp.exp(z - sp + later), 0.0)
    return z, mask, l, a


def sb_fwd(p, W, name):
    S = p.shape[0]
    nh, nq = W // HD, S // HD

    def body(q_ref, k_ref, v_ref, o_ref):
        i = pl.program_id(1)
        q = q_ref[...]
        su = (_iota2((HD, HD), 0) > _iota2((HD, HD), 1)).astype(BF16)

        def step(jj, carry):
            acc, run = carry
            j = i - jj
            off = pl.multiple_of(j * HD, HD)
            _, _, l, a = _sb_block(q, k_ref[pl.ds(off, HD), :], i, j, run, su)
            acc = acc + _dot(a.astype(BF16), v_ref[pl.ds(off, HD), :])
            return acc, run + jnp.sum(l, axis=1, keepdims=True)

        acc, _ = lax.fori_loop(0, i + 1, step, (jnp.zeros((HD, HD), F32), jnp.zeros((HD, 1), F32)))
        o_ref[...] = acc.astype(o_ref.dtype)

    return pl.pallas_call(
        body,
        grid=(nh, nq),
        in_specs=[
            pl.BlockSpec((HD, HD), lambda h, i: (i, h)),
            pl.BlockSpec((S, HD), lambda h, i: (0, nh + h)),
            pl.BlockSpec((S, HD), lambda h, i: (0, 2 * nh + h)),
        ],
        out_specs=pl.BlockSpec((HD, HD), lambda h, i: (i, h)),
        out_shape=jax.ShapeDtypeStruct((S, 2 * W), BF16),
        compiler_params=_cp(("parallel", "arbitrary")),
        name=name,
    )(p, p, p)


def sb_bwd(p, dcat, W, name):
    S = p.shape[0]
    nh, nq = W // HD, S // HD
    scale = HD ** -0.5

    def body(q_ref, k_ref, v_ref, do_ref, dp_ref, dk_acc, dv_acc, e_scr, z_scr):
        i = pl.program_id(1)
        q = q_ref[...]
        do = do_ref[...]
        su = (_iota2((HD, HD), 0) > _iota2((HD, HD), 1)).astype(BF16)
        sl = (_iota2((HD, HD), 0) < _iota2((HD, HD), 1)).astype(BF16)

        @pl.when(i == 0)
        def _():
            dk_acc[...] = jnp.zeros_like(dk_acc)
            dv_acc[...] = jnp.zeros_like(dv_acc)

        def pass_a(jj, run):
            j = i - jj
            off = pl.multiple_of(j * HD, HD)
            z, _, l, a = _sb_block(q, k_ref[pl.ds(off, HD), :], i, j, run, su)
            e_scr[j] = a * _dot_nt(do, v_ref[pl.ds(off, HD), :])
            z_scr[j] = z
            dv_acc[pl.ds(off, HD), :] += _dot_tn(a.astype(BF16), do)
            return run + jnp.sum(l, axis=1, keepdims=True)

        lax.fori_loop(0, i + 1, pass_a, jnp.zeros((HD, 1), F32))

        def pass_b(j, carry):
            dq, run_e = carry
            off = pl.multiple_of(j * HD, HD)
            e = e_scr[j]
            z = z_scr[j]
            mask = (_iota2(z.shape, 1) + j * HD) < (_iota2(z.shape, 0) + i * HD)
            big_e = _dot_ones_right(e, sl) + run_e
            sig = 1.0 / (1.0 + jnp.exp(-z))
            dz = (jnp.where(mask, e * (1.0 - sig) - big_e * sig, 0.0) * scale).astype(BF16)
            dq = dq + _dot(dz, k_ref[pl.ds(off, HD), :])
            dk_acc[pl.ds(off, HD), :] += _dot_tn(dz, q)
            return dq, run_e + jnp.sum(e, axis=1, keepdims=True)

        dq, _ = lax.fori_loop(0, i + 1, pass_b, (jnp.zeros((HD, HD), F32), jnp.zeros((HD, 1), F32)))
        dp_ref[0, pl.ds(pl.multiple_of(i * HD, HD), HD), :] = dq.astype(dp_ref.dtype)

        @pl.when(i == nq - 1)
        def _():
            dp_ref[1] = dk_acc[...].astype(dp_ref.dtype)
            dp_ref[2] = dv_acc[...].astype(dp_ref.dtype)

    return pl.pallas_call(
        body,
        grid=(nh, nq),
        in_specs=[
            pl.BlockSpec((HD, HD), lambda h, i: (i, h)),
            pl.BlockSpec((S, HD), lambda h, i: (0, nh + h)),
            pl.BlockSpec((S, HD), lambda h, i: (0, 2 * nh + h)),
            pl.BlockSpec((HD, HD), lambda h, i: (i, h)),
        ],
        out_specs=pl.BlockSpec((3, S, HD), lambda h, i: (0, 0, h)),
        out_shape=jax.ShapeDtypeStruct((6, S, W), BF16),
        scratch_shapes=[
            pltpu.VMEM((S, HD), F32),
            pltpu.VMEM((S, HD), F32),
            pltpu.VMEM((nq, HD, HD), F32),
            pltpu.VMEM((nq, HD, HD), F32),
        ],
        compiler_params=_cp(("parallel", "arbitrary")),
        name=name,
    )(p, p, p, dcat)


def fox_gate_fwd(f, b, name):
    S = f.shape[0]
    nq = S // HD

    def body(f_ref, b_ref, c_ref, run):
        i = pl.program_id(0)

        @pl.when(i == 0)
        def _():
            run[...] = jnp.zeros_like(run)

        lf = -_softplus(-(f_ref[...] + b_ref[...]))
        tri = (_iota2((HD, HD), 0) >= _iota2((HD, HD), 1)).astype(BF16)
        c_ref[...] = _dot_ones_left(tri, lf) + run[...]
        run[...] += jnp.sum(lf, axis=0, keepdims=True)

    return pl.pallas_call(
        body,
        grid=(nq,),
        in_specs=[pl.BlockSpec((HD, 128), lambda i: (i, 0)), pl.BlockSpec((1, 128), lambda i: (0, 0))],
        out_specs=pl.BlockSpec((HD, 128), lambda i: (i, 0)),
        out_shape=jax.ShapeDtypeStruct((S, 128), F32),
        scratch_shapes=[pltpu.VMEM((1, 128), F32)],
        compiler_params=_cp(("arbitrary",)),
        name=name,
    )(f, b)


def fox_gate_bwd(f, b, dc, name):
    S = f.shape[0]
    nq = S // HD

    def body(f_ref, b_ref, dc_ref, df_ref, db_ref, run):
        i = pl.program_id(0)

        @pl.when(i == 0)
        def _():
            run[...] = jnp.zeros_like(run)

        dc = dc_ref[...]
        tri = (_iota2((HD, HD), 0) <= _iota2((HD, HD), 1)).astype(BF16)
        dlf = _dot_ones_left(tri, dc) + run[...]
        run[...] += jnp.sum(dc, axis=0, keepdims=True)
        x = f_ref[...] + b_ref[...]
        df = dlf * (1.0 / (1.0 + jnp.exp(x)))
        df_ref[...] = df
        db = jnp.sum(df, axis=0, keepdims=True)

        @pl.when(i == 0)
        def _():
            db_ref[...] = db

        @pl.when(i > 0)
        def _():
            db_ref[...] += db

    rev = pl.BlockSpec((HD, 128), lambda i: (nq - 1 - i, 0))
    vec = pl.BlockSpec((1, 128), lambda i: (0, 0))
    return pl.pallas_call(
        body,
        grid=(nq,),
        in_specs=[rev, vec, rev],
        out_specs=[rev, vec],
        out_shape=[jax.ShapeDtypeStruct((S, 128), F32), jax.ShapeDtypeStruct((1, 128), F32)],
        scratch_shapes=[pltpu.VMEM((1, 128), F32)],
        compiler_params=_cp(("arbitrary",)),
        name=name,
    )(f, b, dc)


def _fox_logits(q, kj, ct, cs, i, j):
    s = _dot_nt(q, kj) * (HD ** -0.5) + (ct - cs)
    mask = (_iota2(s.shape, 1) + j * HD) <= (_iota2(s.shape, 0) + i * HD)
    return jnp.where(mask, s, -1e30), mask


def fox_fwd(p, ccol, crow, cat, W, name):
    S = p.shape[0]
    nh, nq = W // HD, S // HD

    def body(q_ref, k_ref, v_ref, cc_ref, cr_ref, cat_ref, o_ref, lse_ref):
        i = pl.program_id(1)
        q = q_ref[...]
        ct = cc_ref[0]

        def step(j, carry):
            m, l, acc = carry
            off = pl.multiple_of(j * HD, HD)
            s, _ = _fox_logits(q, k_ref[pl.ds(off, HD), :], ct, cr_ref[0, pl.ds(j, 1), :], i, j)
            m_new = jnp.maximum(m, jnp.max(s, axis=1, keepdims=True))
            alpha = jnp.exp(m - m_new)
            pr = jnp.exp(s - m_new)
            l = alpha * l + jnp.sum(pr, axis=1, keepdims=True)
            acc = alpha * acc + _dot(pr.astype(BF16), v_ref[pl.ds(off, HD), :])
            return m_new, l, acc

        init = (jnp.full((HD, 1), -1e30, F32), jnp.zeros((HD, 1), F32), jnp.zeros((HD, HD), F32))
        m, l, acc = lax.fori_loop(0, i + 1, step, init)
        o_ref[...] = (acc / l).astype(o_ref.dtype)
        lse_ref[0] = m + jnp.log(l)

    return pl.pallas_call(
        body,
        grid=(nh, nq),
        in_specs=[
            pl.BlockSpec((HD, HD), lambda h, i: (i, 2 * nh + h)),
            pl.BlockSpec((S, HD), lambda h, i: (0, 3 * nh + h)),
            pl.BlockSpec((S, HD), lambda h, i: (0, 4 * nh + h)),
            pl.BlockSpec((1, HD, 1), lambda h, i: (h, i, 0)),
            pl.BlockSpec((1, nq, HD), lambda h, i: (h, 0, 0)),
            pl.BlockSpec(memory_space=pl.ANY),
        ],
        out_specs=[pl.BlockSpec((HD, HD), lambda h, i: (i, nh + h)), pl.BlockSpec((1, HD, 1), lambda h, i: (h, i, 0))],
        out_shape=[jax.ShapeDtypeStruct(cat.shape, cat.dtype), jax.ShapeDtypeStruct((nh, S, 1), F32)],
        input_output_aliases={5: 0},
        compiler_params=_cp(("parallel", "arbitrary")),
        name=name,
    )(p, p, p, ccol, crow, cat)


def fox_bwd(p, ccol, crow, cat, lse, dcat, dp, W, name):
    S = p.shape[0]
    nh, nq = W // HD, S // HD
    scale = HD ** -0.5

    def body(q_ref, k_ref, v_ref, cc_ref, cr_ref, o_ref, lse_ref, do_ref, dp_in_ref, dp_ref, dcs_ref, dct_ref, dk_acc, dv_acc):
        i = pl.program_id(1)
        q = q_ref[...]
        do = do_ref[...]
        ct = cc_ref[0]
        lse_i = lse_ref[0]
        delta = jnp.sum(do.astype(F32) * o_ref[...].astype(F32), axis=1, keepdims=True)

        @pl.when(i == 0)
        def _():
            dk_acc[...] = jnp.zeros_like(dk_acc)
            dv_acc[...] = jnp.zeros_like(dv_acc)
            dcs_ref[...] = jnp.zeros_like(dcs_ref)

        def step(j, carry):
            dq, dct = carry
            off = pl.multiple_of(j * HD, HD)
            kj = k_ref[pl.ds(off, HD), :]
            s, mask = _fox_logits(q, kj, ct, cr_ref[0, pl.ds(j, 1), :], i, j)
            pr = jnp.where(mask, jnp.exp(s - lse_i), 0.0)
            ds = pr * (_dot_nt(do, v_ref[pl.ds(off, HD), :]) - delta)
            dv_acc[pl.ds(off, HD), :] += _dot_tn(pr.astype(BF16), do)
            dsb = (ds * scale).astype(BF16)
            dk_acc[pl.ds(off, HD), :] += _dot_tn(dsb, q)
            dcs_ref[0, pl.ds(j, 1), :] += jnp.sum(ds, axis=0, keepdims=True)
            return dq + _dot(dsb, kj), dct + jnp.sum(ds, axis=1, keepdims=True)

        dq, dct = lax.fori_loop(0, i + 1, step, (jnp.zeros((HD, HD), F32), jnp.zeros((HD, 1), F32)))
        dp_ref[0, pl.ds(pl.multiple_of(i * HD, HD), HD), :] = dq.astype(dp_ref.dtype)
        dct_ref[0] = dct

        @pl.when(i == nq - 1)
        def _():
            dp_ref[1] = dk_acc[...].astype(dp_ref.dtype)
            dp_ref[2] = dv_acc[...].astype(dp_ref.dtype)

    return pl.pallas_call(
        body,
        grid=(nh, nq),
        in_specs=[
            pl.BlockSpec((HD, HD), lambda h, i: (i, 2 * nh + h)),
            pl.BlockSpec((S, HD), lambda h, i: (0, 3 * nh + h)),
            pl.BlockSpec((S, HD), lambda h, i: (0, 4 * nh + h)),
            pl.BlockSpec((1, HD, 1), lambda h, i: (h, i, 0)),
            pl.BlockSpec((1, nq, HD), lambda h, i: (h, 0, 0)),
            pl.BlockSpec((HD, HD), lambda h, i: (i, nh + h)),
            pl.BlockSpec((1, HD, 1), lambda h, i: (h, i, 0)),
            pl.BlockSpec((HD, HD), lambda h, i: (i, nh + h)),
            pl.BlockSpec(memory_space=pl.ANY),
        ],
        out_specs=[
            pl.BlockSpec((3, S, HD), lambda h, i: (1, 0, h)),
            pl.BlockSpec((1, nq, HD), lambda h, i: (h, 0, 0)),
            pl.BlockSpec((1, HD, 1), lambda h, i: (h, i, 0)),
        ],
        out_shape=[
            jax.ShapeDtypeStruct(dp.shape, dp.dtype),
            jax.ShapeDtypeStruct((nh, nq, HD), F32),
            jax.ShapeDtypeStruct((nh, S, 1), F32),
        ],
        input_output_aliases={8: 0},
        scratch_shapes=[pltpu.VMEM((S, HD), F32), pltpu.VMEM((S, HD), F32)],
        compiler_params=_cp(("parallel", "arbitrary")),
        name=name,
    )(p, p, p, ccol, crow, cat, lse, dcat, dp)


_GELU_K = math.sqrt(2.0 / math.pi)
_GELU_C = 0.044715


def _gelu(x):
    return 0.5 * x * (1.0 + jnp.tanh(_GELU_K * (x + _GELU_C * x * x * x)))


def _gelu_grad(x):
    t = jnp.tanh(_GELU_K * (x + _GELU_C * x * x * x))
    return 0.5 * (1.0 + t) + 0.5 * x * (1.0 - t * t) * (_GELU_K * (1.0 + 3.0 * _GELU_C * x * x))


def _layernorm_parts(gv):
    xc = gv - jnp.mean(gv, axis=-1, keepdims=True)
    r = lax.rsqrt(jnp.mean(xc * xc, axis=-1, keepdims=True) + EPS)
    return xc * r, r


def sg_fwd(p, sg_w, sg_bt, sg_g, W, name):
    S = p.shape[0]
    G, nq = W // HD, S // HD

    def body(u_ref, v_ref, w_ref, bt_ref, g_ref, o_ref):
        xh, _ = _layernorm_parts(_gelu(v_ref[...].astype(F32)))
        vn = (xh * g_ref[...]).astype(BF16)
        tri = _iota2((HD, HD), 0) >= _iota2((HD, HD), 1)
        for gi in range(G):
            cols = slice(gi * HD, (gi + 1) * HD)
            wt = jnp.where(tri, w_ref[gi], 0.0).astype(BF16)
            mixed = _dot(wt, vn[:, cols]) + bt_ref[:, gi : gi + 1]
            o_ref[:, cols] = (_gelu(u_ref[:, cols].astype(F32)) * mixed).astype(o_ref.dtype)

    return pl.pallas_call(
        body,
        grid=(nq,),
        in_specs=[
            pl.BlockSpec((HD, W), lambda i: (i, 0)),
            pl.BlockSpec((HD, W), lambda i: (i, 1)),
            pl.BlockSpec((G, HD, HD), lambda i: (0, 0, 0)),
            pl.BlockSpec((HD, G), lambda i: (0, 0)),
            pl.BlockSpec((1, W), lambda i: (0, 0)),
        ],
        out_specs=pl.BlockSpec((HD, W), lambda i: (i, 0)),
        out_shape=jax.ShapeDtypeStruct((S, 2 * W), BF16),
        compiler_params=_cp(("parallel",)),
        name=name,
    )(p, p, sg_w, sg_bt, sg_g.reshape(1, W))


def sg_bwd(p, sg_w, sg_bt, sg_g, dcat, W, name):
    S = p.shape[0]
    G, nq = W // HD, S // HD

    def body(u_ref, v_ref, w_ref, bt_ref, g_ref, do_ref, dp_ref, dw_ref, dbt_ref, dg_ref, dvn_scr):
        i = pl.program_id(0)

        @pl.when(i == 0)
        def _():
            dw_ref[...] = jnp.zeros_like(dw_ref)
            dbt_ref[...] = jnp.zeros_like(dbt_ref)
            dg_ref[...] = jnp.zeros_like(dg_ref)

        v = v_ref[...].astype(F32)
        xh, r = _layernorm_parts(_gelu(v))
        gg = g_ref[...]
        vn = (xh * gg).astype(BF16)
        tri = _iota2((HD, HD), 0) >= _iota2((HD, HD), 1)
        for gi in range(G):
            cols = slice(gi * HD, (gi + 1) * HD)
            wt = jnp.where(tri, w_ref[gi], 0.0).astype(BF16)
            mixed = _dot(wt, vn[:, cols]) + bt_ref[:, gi : gi + 1]
            u = u_ref[:, cols].astype(F32)
            do = do_ref[:, cols].astype(F32)
            dp_ref[0, :, cols] = (do * mixed * _gelu_grad(u)).astype(dp_ref.dtype)
            dmix = do * _gelu(u)
            dmb = dmix.astype(BF16)
            dw_ref[gi] += jnp.where(tri, _dot_nt(dmb, vn[:, cols]), 0.0)
            dbt_ref[:, gi : gi + 1] += jnp.sum(dmix, axis=1, keepdims=True)
            dvn_scr[:, cols] = _dot_tn(wt, dmb)
        dvn = dvn_scr[...]
        dg_ref[...] += jnp.sum(dvn * xh, axis=0, keepdims=True)
        dxh = dvn * gg
        dgv = r * (dxh - jnp.mean(dxh, axis=-1, keepdims=True) - xh * jnp.mean(dxh * xh, axis=-1, keepdims=True))
        dp_ref[1] = (dgv * _gelu_grad(v)).astype(dp_ref.dtype)

    return pl.pallas_call(
        body,
        grid=(nq,),
        in_specs=[
            pl.BlockSpec((HD, W), lambda i: (i, 0)),
            pl.BlockSpec((HD, W), lambda i: (i, 1)),
            pl.BlockSpec((G, HD, HD), lambda i: (0, 0, 0)),
            pl.BlockSpec((HD, G), lambda i: (0, 0)),
            pl.BlockSpec((1, W), lambda i: (0, 0)),
            pl.BlockSpec((HD, W), lambda i: (i, 0)),
        ],
        out_specs=[
            pl.BlockSpec((2, HD, W), lambda i: (0, i, 0)),
            pl.BlockSpec((G, HD, HD), lambda i: (0, 0, 0)),
            pl.BlockSpec((HD, G), lambda i: (0, 0)),
            pl.BlockSpec((1, W), lambda i: (0, 0)),
        ],
        out_shape=[
            jax.ShapeDtypeStruct((6, S, W), BF16),
            jax.ShapeDtypeStruct((G, HD, HD), F32),
            jax.ShapeDtypeStruct((HD, G), F32),
            jax.ShapeDtypeStruct((1, W), F32),
        ],
        scratch_shapes=[pltpu.VMEM((HD, W), F32)],
        compiler_params=_cp(("arbitrary",)),
        name=name,
    )(p, p, sg_w, sg_bt, sg_g.reshape(1, W), dcat)


def _parts_map(parts_of, ntile_per_part, nrow_blocks):
    def to_block(i, vb):
        return (parts_of(vb // ntile_per_part) * nrow_blocks + i, vb % ntile_per_part)
    return to_block


def local_step(x, target, wts):
    S, D = x.shape
    W = D // 2
    nb = wts["nb"]
    F = wts["l0_ffn_down"].shape[0]
    g = {}

    def ffn_fwd(xin, l):
        h = rms_fwd(xin, wts[f"{l}_ffn_norm_g"], f"{l}_ffn_rms")
        u = mm_nn(h, wts[f"{l}_ffn_up"], nb, f"{l}_ffn_up_mm")
        act = ffn_act_fwd(u, wts[f"{l}_ffn_conv_w"], F, f"{l}_ffn_act")
        xout = mm_nn(act, wts[f"{l}_ffn_down"], 1, f"{l}_ffn_down_mm", out_dtype=F32, res=xin)
        return xout, (xin, h, u, act)

    def ffn_bwd(dxout, saved, l):
        xin, h, u, act = saved
        dact = mm_nt(dxout.astype(BF16), wts[f"{l}_ffn_down"], 1, S, F, f"{l}_ffn_down_dx")
        g[f"{l}_ffn_down"] = mm_tn(act, dxout.astype(BF16), 1, D, f"{l}_ffn_down_dw")
        du, dcw = ffn_act_bwd(u, wts[f"{l}_ffn_conv_w"], dact, F, f"{l}_ffn_act_bwd")
        g[f"{l}_ffn_conv_w"] = jnp.concatenate([dcw[0], dcw[1]], axis=1)
        du2 = du.reshape(2 * S, F)
        n = wts[f"{l}_ffn_up"].shape[1]
        tn = _pick(n, (1408, 1024, 768, 512, 256, 128))
        per_half = F // tn
        nt = n // tn

        def up_block(i, j, t):
            vb = j * nt + t
            return vb // per_half, vb % per_half

        tm = _pick(S, (1024, 512, 256, 128))

        def nt_map(i, j, t):
            half, cb = up_block(i, j, t)
            return (half * (S // tm) + i, cb)

        def tn_map(j, t):
            half, cb = up_block(0, j, t)
            return (half, cb)

        dh = mm_nt(du2, wts[f"{l}_ffn_up"], nb, S, D, f"{l}_ffn_up_dx", dy_map=nt_map, tm=tm, tn=tn)
        g[f"{l}_ffn_up"] = mm_tn(h, du2, nb, n, f"{l}_ffn_up_dw", dy_map=tn_map, tn=tn)
        dxin, dg = rms_bwd(xin, wts[f"{l}_ffn_norm_g"], dh, dxout, f"{l}_ffn_rms_bwd")
        g[f"{l}_ffn_norm_g"] = dg
        return dxin

    h0 = rms_fwd(x, wts["l0_mix_norm_g"], "l0_mix_rms")
    p0 = mm_nn(h0, wts["l0_w_in"], nb, "l0_w_in_mm")
    cat0 = sb_fwd(p0, W, "l0_sb_fwd")
    cat0 = sc_fwd(p0, wts["l0_sc_conv_w"], cat0, W, "l0_sc_fwd")
    x1 = mm_nn(cat0, wts["l0_w_out"], 1, "l0_w_out_mm", out_dtype=F32, res=x)
    x2, ffn0_saved = ffn_fwd(x1, "l0")

    nh = W // HD
    h2 = rms_fwd(x2, wts["l1_mix_norm_g"], "l1_mix_rms")
    p1 = mm_nn(h2, wts["l1_w_in_main"], 1, "l1_w_in_mm")
    f = mm_nn(h2, wts["l1_w_in_f"], 1, "l1_w_f_mm", out_dtype=F32)
    bf = jnp.zeros((1, 128), F32).at[0, :nh].set(wts["l1_fox_b_f"])
    c = fox_gate_fwd(f, bf, "l1_fox_gate")
    c_heads = c[:, :nh].T
    ccol = c_heads[:, :, None]
    crow = c_heads.reshape(nh, S // HD, HD)
    sg_bt = wts["l1_sg_b"].T
    cat1 = sg_fwd(p1, wts["l1_sg_w"], sg_bt, wts["l1_sg_norm_g"], W, "l1_sg_fwd")
    cat1, lse = fox_fwd(p1, ccol, crow, cat1, W, "l1_fox_fwd")
    x3 = mm_nn(cat1, wts["l1_w_out"], 1, "l1_w_out_mm", out_dtype=F32, res=x2)
    x4, ffn1_saved = ffn_fwd(x3, "l1")

    dx4, dgf, loss = loss_head(x4, wts["final_norm_g"], target, "loss_head")
    g["final_norm_g"] = dgf

    dx3 = ffn_bwd(dx4, ffn1_saved, "l1")
    dx3b = dx3.astype(BF16)
    dcat1 = mm_nt(dx3b, wts["l1_w_out"], 1, S, D, "l1_w_out_dx")
    g["l1_w_out"] = mm_tn(cat1, dx3b, 1, D, "l1_w_out_dw")
    dp1, dsgw, dsgbt, dsgg = sg_bwd(p1, wts["l1_sg_w"], sg_bt, wts["l1_sg_norm_g"], dcat1, W, "l1_sg_bwd")
    dp1, dcs, dct = fox_bwd(p1, ccol, crow, cat1, lse, dcat1, dp1, W, "l1_fox_bwd")
    g["l1_sg_w"], g["l1_sg_b"], g["l1_sg_norm_g"] = dsgw, dsgbt.T, dsgg
    dc = jnp.zeros((S, 128), F32).at[:, :nh].set((dct[:, :, 0] - dcs.reshape(nh, S)).T)
    df, dbf = fox_gate_bwd(f, bf, dc, "l1_fox_gate_bwd")
    g["l1_fox_b_f"] = dbf[0, :nh]
    dfb = df.astype(BF16)
    tn1 = _pick(W, (512, 256, 128))
    tm1 = _pick(S, (1024, 512, 256, 128))
    per_part = W // tn1
    part_of = lambda pt: pt + pt // 2 - pt // 4

    def nt_map1(i, j, t):
        return (part_of(t // per_part) * (S // tm1) + i, t % per_part)

    def tn_map1(j, t):
        return (part_of(t // per_part), t % per_part)

    dp1_2d = dp1.reshape(6 * S, W)
    dh2 = mm_nt(dfb, wts["l1_w_in_f"], 1, S, D, "l1_w_f_dx", out_dtype=F32)
    dh2 = mm_nt(dp1_2d, wts["l1_w_in_main"], 1, S, D, "l1_w_in_dx", res=dh2, dy_map=nt_map1, tm=tm1, tn=tn1)
    g["l1_w_in_main"] = mm_tn(h2, dp1_2d, 1, 5 * W, "l1_w_in_dw", dy_map=tn_map1, tn=tn1)
    g["l1_w_in_f"] = mm_tn(h2, dfb, 1, 128, "l1_w_f_dw")
    dx2, dg = rms_bwd(x2, wts["l1_mix_norm_g"], dh2, dx3, "l1_mix_rms_bwd")
    g["l1_mix_norm_g"] = dg

    dx1 = ffn_bwd(dx2, ffn0_saved, "l0")
    dx1b = dx1.astype(BF16)
    dcat0 = mm_nt(dx1b, wts["l0_w_out"], 1, S, D, "l0_w_out_dx")
    g["l0_w_out"] = mm_tn(cat0, dx1b, 1, D, "l0_w_out_dw")
    dp0 = sb_bwd(p0, dcat0, W, "l0_sb_bwd")
    dp0, dscw = sc_bwd(p0, wts["l0_sc_conv_w"], dcat0, dp0, W, "l0_sc_bwd")
    g["l0_sc_conv_w"] = dscw
    n0 = wts["l0_w_in"].shape[1]
    tn0 = math.gcd(_pick(n0, (768, 512, 256, 128)), _pick(W, (1024, 512, 256, 128)))
    tm0 = _pick(S, (1024, 512, 256, 128))
    per_part0 = W // tn0
    nt0 = n0 // tn0

    def nt_map0(i, j, t):
        vb = j * nt0 + t
        return ((vb // per_part0) * (S // tm0) + i, vb % per_part0)

    def tn_map0(j, t):
        vb = j * nt0 + t
        return (vb // per_part0, vb % per_part0)

    dp0_2d = dp0.reshape(6 * S, W)
    dh0 = mm_nt(dp0_2d, wts["l0_w_in"], nb, S, D, "l0_w_in_dx", dy_map=nt_map0, tm=tm0, tn=tn0)
    g["l0_w_in"] = mm_tn(h0, dp0_2d, nb, n0, "l0_w_in_dw", dy_map=tn_map0, tn=tn0)
    dx0, dg = rms_bwd(x, wts["l0_mix_norm_g"], dh0, dx1, "l0_mix_rms_bwd")
    g["l0_mix_norm_g"] = dg
    return loss, dx0, g


_HBM = pl.BlockSpec(memory_space=pl.ANY)


def _place():
    return lax.axis_index("x"), lax.axis_index("y"), lax.axis_index("c")


def _other_chips(x, y):
    return [(x, 1 - y), (1 - x, y), (1 - x, 1 - y)]


def all_gather(arrs, name):
    n = len(arrs)

    def body(*refs):
        xs, outs = refs[:n], refs[n : 2 * n]
        send_sems, recv_sems, local_sems = refs[2 * n :]
        x, y, c = _place()
        me, sibling = (x, y, c), (x, y, 1 - c)
        chips = _other_chips(x, y)

        def copy(a, k, block, to, src=None):
            px, py, pc = block
            dst = outs[a].at[4 * px + 2 * py + pc]
            return pltpu.make_async_remote_copy(
                src_ref=dst if src is None else src, dst_ref=dst,
                send_sem=send_sems.at[a, k], recv_sem=recv_sems.at[a, k], device_id=to, device_id_type=MESH,
            )

        mine = [pltpu.make_async_copy(xs[a], outs[a].at[4 * x + 2 * y + c], local_sems.at[a]) for a in range(n)]
        for cp in mine:
            cp.start()
        first = []
        for a in range(n):
            first.append(copy(a, 0, me, sibling, src=xs[a]))
            first += [copy(a, 1 + j, me, (*chip, c), src=xs[a]) for j, chip in enumerate(chips)]
        for cp in first:
            cp.start()
        passed = []
        for a in range(n):
            for j, chip in enumerate(chips):
                copy(a, 1 + j, (*chip, c), me).wait_recv()
                cp = copy(a, 4 + j, (*chip, c), sibling)
                cp.start()
                passed.append(cp)
        for a in range(n):
            copy(a, 0, sibling, me).wait_recv()
            for j, chip in enumerate(chips):
                copy(a, 4 + j, (*chip, 1 - c), me).wait_recv()
        for cp in first + passed:
            cp.wait_send()
        for cp in mine:
            cp.wait()

    return pl.pallas_call(
        body,
        in_specs=[_HBM] * n,
        out_specs=[_HBM] * n,
        out_shape=[jax.ShapeDtypeStruct((NDEV,) + a.shape, a.dtype) for a in arrs],
        scratch_shapes=[pltpu.SemaphoreType.DMA((n, 7)), pltpu.SemaphoreType.DMA((n, 7)), pltpu.SemaphoreType.DMA((n,))],
        name=name,
    )(*arrs)


def exchange_pair(arrs, name):
    n = len(arrs)

    def body(*refs):
        xs, lands = refs[:n], refs[n : 2 * n]
        send_sems, recv_sems = refs[2 * n :]
        x, y, c = _place()
        copies = [
            pltpu.make_async_remote_copy(
                src_ref=xs[a].at[k, 1 - c], dst_ref=lands[a].at[k],
                send_sem=send_sems.at[a, k], recv_sem=recv_sems.at[a, k], device_id=(x, y, 1 - c), device_id_type=MESH,
            )
            for a in range(n)
            for k in range(4)
        ]
        for cp in copies:
            cp.start()
        for cp in copies:
            cp.wait()

    return pl.pallas_call(
        body,
        in_specs=[_HBM] * n,
        out_specs=[_HBM] * n,
        out_shape=[jax.ShapeDtypeStruct((4,) + a.shape[2:], a.dtype) for a in arrs],
        scratch_shapes=[pltpu.SemaphoreType.DMA((n, 4)), pltpu.SemaphoreType.DMA((n, 4))],
        name=name,
    )(*arrs)


def exchange_chips(arrs, name):
    n = len(arrs)

    def body(*refs):
        xs, lands = refs[:n], refs[n : 2 * n]
        send_sems, recv_sems, local_sems = refs[2 * n :]
        x, y, c = _place()
        my_chip = 2 * x + y
        chips = _other_chips(x, y)
        mine = [pltpu.make_async_copy(xs[a].at[my_chip], lands[a].at[my_chip], local_sems.at[a]) for a in range(n)]
        for cp in mine:
            cp.start()
        sends, recvs = [], []
        for a in range(n):
            for j, (px, py) in enumerate(chips):
                peer = 2 * px + py
                sends.append(pltpu.make_async_remote_copy(
                    src_ref=xs[a].at[peer], dst_ref=lands[a].at[my_chip],
                    send_sem=send_sems.at[a, j], recv_sem=recv_sems.at[a, j], device_id=(px, py, c), device_id_type=MESH,
                ))
                recvs.append(pltpu.make_async_remote_copy(
                    src_ref=xs[a].at[peer], dst_ref=lands[a].at[peer],
                    send_sem=send_sems.at[a, j], recv_sem=recv_sems.at[a, j], device_id=(px, py, c), device_id_type=MESH,
                ))
        for cp in sends:
            cp.start()
        for cp in recvs:
            cp.wait_recv()
        for cp in sends:
            cp.wait_send()
        for cp in mine:
            cp.wait()

    return pl.pallas_call(
        body,
        in_specs=[_HBM] * n,
        out_specs=[_HBM] * n,
        out_shape=[jax.ShapeDtypeStruct(a.shape, a.dtype) for a in arrs],
        scratch_shapes=[pltpu.SemaphoreType.DMA((n, 3)), pltpu.SemaphoreType.DMA((n, 3)), pltpu.SemaphoreType.DMA((n,))],
        name=name,
    )(*arrs)


def pair_sum(a42, land4, core, name):
    _, _, R, C = a42.shape
    tr = _pick(R, (512, 256, 128, 64, 32, 16, 8))

    def body(core_ref, a_ref, l_ref, o_ref):
        o_ref[...] = (a_ref[0].astype(F32) + l_ref[...].astype(F32)).astype(o_ref.dtype)

    return pl.pallas_call(
        body,
        grid_spec=pltpu.PrefetchScalarGridSpec(
            num_scalar_prefetch=1,
            grid=(4, R // tr),
            in_specs=[
                pl.BlockSpec((1, 1, tr, C), lambda k, r, core_ref: (k, core_ref[0], r, 0)),
                pl.BlockSpec((1, tr, C), lambda k, r, core_ref: (k, r, 0)),
            ],
            out_specs=pl.BlockSpec((1, tr, C), lambda k, r, core_ref: (k, r, 0)),
        ),
        out_shape=jax.ShapeDtypeStruct((4, R, C), BF16),
        compiler_params=_cp(("parallel", "parallel")),
        name=name,
    )(core, a42, land4)


def sum_slots(parts, name):
    P, R, C = parts.shape

    def body(p_ref, o_ref):
        acc = p_ref[0].astype(F32)
        for k in range(1, P):
            acc = acc + p_ref[k].astype(F32)
        o_ref[...] = acc

    tr = _pick(R, (512, 256, 128, 64, 32, 16, 8))
    return pl.pallas_call(
        body,
        grid=(R // tr,),
        in_specs=[pl.BlockSpec((P, tr, C), lambda r: (0, r, 0))],
        out_specs=pl.BlockSpec((tr, C), lambda r: (r, 0)),
        out_shape=jax.ShapeDtypeStruct((R, C), F32),
        compiler_params=_cp(("parallel",)),
        name=name,
    )(parts)


def adamw(w, m, v, parts, name):
    R, C = w.shape
    P = parts.shape[0]
    tr = _pick(R, (256, 128, 64, 32, 16, 8))
    c1 = 1.0 - ADAM_B1 ** ADAM_STEP
    c2 = 1.0 - ADAM_B2 ** ADAM_STEP

    def body(w_ref, m_ref, v_ref, p_ref, g_ref, d_ref, nm_ref, nv_ref):
        g = p_ref[0].astype(F32)
        for k in range(1, P):
            g = g + p_ref[k].astype(F32)
        nm = ADAM_B1 * m_ref[...] + (1.0 - ADAM_B1) * g
        nv = ADAM_B2 * v_ref[...] + (1.0 - ADAM_B2) * (g * g)
        g_ref[...] = g
        nm_ref[...] = nm
        nv_ref[...] = nv
        d_ref[...] = -ADAM_LR * ((nm / c1) / (jnp.sqrt(nv / c2) + ADAM_EPS) + ADAM_WD * w_ref[...])

    blk = pl.BlockSpec((tr, C), lambda r: (r, 0))
    shp = jax.ShapeDtypeStruct((R, C), F32)
    return pl.pallas_call(
        body,
        grid=(R // tr,),
        in_specs=[blk, blk, blk, pl.BlockSpec((P, tr, C), lambda r: (0, r, 0))],
        out_specs=[blk, blk, blk, blk],
        out_shape=[shp, shp, shp, shp],
        compiler_params=_cp(("parallel",)),
        name=name,
    )(w, m, v, parts)


_WEIGHTS = [
    "l0_mix_norm_g", "l0_w_in", "l0_sc_conv_w", "l0_w_out", "l0_ffn_norm_g", "l0_ffn_up", "l0_ffn_conv_w", "l0_ffn_down",
    "l1_mix_norm_g", "l1_w_in", "l1_fox_b_f", "l1_sg_w", "l1_sg_b", "l1_sg_norm_g", "l1_w_out", "l1_ffn_norm_g",
    "l1_ffn_up", "l1_ffn_conv_w", "l1_ffn_down", "final_norm_g",
]
_COL_SHARDED = ["l0_w_in", "l0_ffn_up", "l1_w_in", "l1_ffn_up"]
_ROW_SHARDED = ["l0_w_out", "l0_ffn_down", "l1_w_out", "l1_ffn_down"]
_BIG = ["l0_w_in", "l0_w_out", "l0_ffn_up", "l0_ffn_down", "l1_w_in", "l1_w_out", "l1_ffn_up", "l1_ffn_down"]
_CONV = ["l0_sc_conv_w", "l0_ffn_conv_w", "l1_ffn_conv_w"]
_SMALL = [n for n in _WEIGHTS if n not in _BIG]
_PACK_ROWS = 8


def _pack(arrs):
    flat = []
    for a in arrs:
        v = a.reshape(-1).astype(F32)
        pad = (-v.shape[0]) % (_PACK_ROWS * 128)
        flat.append(jnp.pad(v, (0, pad)))
    return jnp.concatenate(flat).reshape(-1, 128)


def _unpack(packed, shapes):
    out, off = [], 0
    flat = packed.reshape(-1)
    for shp in shapes:
        size = math.prod(shp)
        out.append(flat[off : off + size].reshape(shp))
        off += size + (-size) % (_PACK_ROWS * 128)
    return out


def kernel(x, l0_mix_norm_g, l0_w_in, l0_sc_conv_w, l0_w_out, l0_ffn_norm_g, l0_ffn_up, l0_ffn_conv_w, l0_ffn_down, l1_mix_norm_g, l1_w_in, l1_fox_b_f, l1_sg_w, l1_sg_b, l1_sg_norm_g, l1_w_out, l1_ffn_norm_g, l1_ffn_up, l1_ffn_conv_w, l1_ffn_down, final_norm_g, loss_target, m_l0_mix_norm_g, m_l0_w_in, m_l0_sc_conv_w, m_l0_w_out, m_l0_ffn_norm_g, m_l0_ffn_up, m_l0_ffn_conv_w, m_l0_ffn_down, m_l1_mix_norm_g, m_l1_w_in, m_l1_fox_b_f, m_l1_sg_w, m_l1_sg_b, m_l1_sg_norm_g, m_l1_w_out, m_l1_ffn_norm_g, m_l1_ffn_up, m_l1_ffn_conv_w, m_l1_ffn_down, m_final_norm_g, v_l0_mix_norm_g, v_l0_w_in, v_l0_sc_conv_w, v_l0_w_out, v_l0_ffn_norm_g, v_l0_ffn_up, v_l0_ffn_conv_w, v_l0_ffn_down, v_l1_mix_norm_g, v_l1_w_in, v_l1_fox_b_f, v_l1_sg_w, v_l1_sg_b, v_l1_sg_norm_g, v_l1_w_out, v_l1_ffn_norm_g, v_l1_ffn_up, v_l1_ffn_conv_w, v_l1_ffn_down, v_final_norm_g):
    given = dict(locals())
    w = {n: given[n] for n in _WEIGHTS}
    mom = {n: given["m_" + n] for n in _WEIGHTS}
    var = {n: given["v_" + n] for n in _WEIGHTS}
    xs, target = x[0], loss_target[0]
    S, D = xs.shape
    W = D // 2
    nh = W // HD
    cx, cy, cc = _place()
    me = 4 * cx + 2 * cy + cc

    gathered = all_gather([w[n].astype(BF16) for n in _BIG] + [w[n] for n in _CONV], "gather_weights")
    full = dict(zip(_BIG + _CONV, gathered))
    wts = {"nb": NDEV}
    for n in ("l0_w_in", "l0_ffn_up", "l1_ffn_up"):
        wts[n] = full[n].reshape(NDEV * D, -1)
    for n in _ROW_SHARDED:
        wts[n] = full[n].reshape(-1, D)
    w_in1 = full["l1_w_in"].transpose(1, 0, 2).reshape(D, -1)
    wts["l1_w_in_main"] = w_in1[:, : 5 * W]
    wts["l1_w_in_f"] = jnp.pad(w_in1[:, 5 * W :], ((0, 0), (0, 128 - nh)))
    for n in _CONV:
        wts[n] = full[n].transpose(1, 0, 2).reshape(CONV_K, -1)
    for n in _SMALL:
        if n not in _CONV:
            wts[n] = w[n]

    loss_tile, dx, g = local_step(xs, target, wts)
    loss = lax.psum(loss_tile[0, 0], ("x", "y", "c"))

    dw_in1 = jnp.concatenate([g["l1_w_in_main"], g["l1_w_in_f"][:, :nh]], axis=1)
    terms = {
        "l0_w_in": g["l0_w_in"].reshape(NDEV, D, -1),
        "l0_ffn_up": g["l0_ffn_up"].reshape(NDEV, D, -1),
        "l1_ffn_up": g["l1_ffn_up"].reshape(NDEV, D, -1),
        "l1_w_in": dw_in1.reshape(D, NDEV, -1).transpose(1, 0, 2),
    }
    for n in _ROW_SHARDED:
        terms[n] = g[n].reshape(NDEV, -1, D)
    by_pair = [terms[n].reshape((4, 2) + terms[n].shape[1:]) for n in _BIG]
    landed = exchange_pair(by_pair, "reduce_pair")
    core = jnp.reshape(cc, (1,)).astype(jnp.int32)
    chip_terms = [pair_sum(a, l, core, f"pair_sum_{n}") for n, a, l in zip(_BIG, by_pair, landed)]
    by_chip = exchange_chips(chip_terms, "reduce_chips")
    out_g, out_d, out_m, out_v = {}, {}, {}, {}
    for n, parts in zip(_BIG, by_chip):
        out_g[n], out_d[n], out_m[n], out_v[n] = adamw(w[n], mom[n], var[n], parts, f"adamw_{n}")

    small_terms = [g[n] for n in _SMALL]
    small_shapes = [tuple(t.shape) for t in small_terms]
    packed = _pack(small_terms)
    all_terms = all_gather([packed], "gather_small_grads")[0]
    small_sum = _unpack(sum_slots(all_terms, "sum_small_grads"), small_shapes)
    small_g = {}
    for n, t in zip(_SMALL, small_sum):
        if n in _CONV:
            cols = w[n].shape[1]
            t = lax.dynamic_slice_in_dim(t, me * cols, cols, axis=1)
        small_g[n] = t.reshape(w[n].shape)
    shapes = [w[n].shape for n in _SMALL]
    res = adamw(
        _pack([w[n] for n in _SMALL]), _pack([mom[n] for n in _SMALL]), _pack([var[n] for n in _SMALL]),
        _pack([small_g[n] for n in _SMALL])[None], "adamw_small",
    )
    for dst, packed_out in zip((out_g, out_d, out_m, out_v), res):
        for n, t in zip(_SMALL, _unpack(packed_out, shapes)):
            dst[n] = t

    return (loss, dx[None], *[out_g[n] for n in _WEIGHTS], *[out_d[n] for n in _WEIGHTS],
            *[out_m[n] for n in _WEIGHTS], *[out_v[n] for n in _WEIGHTS])
```

```python
import functools
import math

import jax
import jax.numpy as jnp
from jax import lax
from jax.experimental import pallas as pl
from jax.experimental.pallas import tpu as pltpu

F32 = jnp.float32
BF16 = jnp.bfloat16
HD = 128
EPS = 1e-6
CONV_K = 3
VMEM_LIMIT_BYTES = 48 << 20
NDEV = 8
MESH = pl.DeviceIdType.MESH

ADAM_LR = 0.001
ADAM_B1 = 0.9
ADAM_B2 = 0.999
ADAM_EPS = 1e-08
ADAM_WD = 0.01
ADAM_STEP = 10


def _cp(sem):
    return pltpu.CompilerParams(dimension_semantics=sem, vmem_limit_bytes=VMEM_LIMIT_BYTES)


def _pick(n, prefs):
    for p in prefs:
        if n % p == 0:
            return p
    return n


def _dot(a, b):
    return jnp.dot(a, b, preferred_element_type=F32)


def _dot_nt(a, b):
    return lax.dot_general(a, b, (((1,), (1,)), ((), ())), preferred_element_type=F32)


def _dot_tn(a, b):
    return lax.dot_general(a, b, (((0,), (0,)), ((), ())), preferred_element_type=F32)


def _split3(x):
    hi = x.astype(BF16)
    r = x - hi.astype(F32)
    mid = r.astype(BF16)
    lo = (r - mid.astype(F32)).astype(BF16)
    return hi, mid, lo


def _dot_ones_right(x, ones_bf16):
    hi, mid, lo = _split3(x)
    return _dot(hi, ones_bf16) + _dot(mid, ones_bf16) + _dot(lo, ones_bf16)


def _dot_ones_left(ones_bf16, x):
    hi, mid, lo = _split3(x)
    return _dot(ones_bf16, hi) + _dot(ones_bf16, mid) + _dot(ones_bf16, lo)


def _iota2(shape, axis):
    return lax.broadcasted_iota(jnp.int32, shape, axis)


def mm_nn(a, w2d, nb, name, out_dtype=BF16, res=None, tm=None, tn=None, tk=None):
    M, K = a.shape
    n = w2d.shape[1]
    assert w2d.shape[0] == nb * K
    tm = tm or _pick(M, (1024, 512, 256, 128))
    tn = tn or _pick(n, (1408, 1024, 768, 512, 256, 128))
    tk = tk or (K if K <= 2048 else _pick(K, (1408, 1024, 512, 256, 128)))
    nk, nt = K // tk, n // tn
    has_res = res is not None

    def body(*refs):
        if has_res:
            a_ref, w_ref, r_ref, o_ref = refs[:4]
        else:
            a_ref, w_ref, o_ref = refs[:3]
            r_ref = None
        part = _dot(a_ref[...], w_ref[...])

        def finish(acc):
            if r_ref is not None:
                acc = acc + r_ref[...].astype(F32)
            o_ref[...] = acc.astype(o_ref.dtype)

        if nk == 1:
            finish(part)
        else:
            acc_ref = refs[-1]
            k = pl.program_id(3)

            @pl.when(k == 0)
            def _():
                acc_ref[...] = part

            @pl.when(k > 0)
            def _():
                acc_ref[...] += part

            @pl.when(k == nk - 1)
            def _():
                finish(acc_ref[...])

    in_specs = [
        pl.BlockSpec((tm, tk), lambda i, j, t, k: (i, k)),
        pl.BlockSpec((tk, tn), lambda i, j, t, k: (j * nk + k, t)),
    ]
    args = [a, w2d]
    out_spec = pl.BlockSpec((tm, tn), lambda i, j, t, k: (i, j * nt + t))
    if has_res:
        in_specs.append(out_spec)
        args.append(res)
    return pl.pallas_call(
        body,
        grid=(M // tm, nb, nt, nk),
        in_specs=in_specs,
        out_specs=out_spec,
        out_shape=jax.ShapeDtypeStruct((M, nb * n), out_dtype),
        scratch_shapes=[pltpu.VMEM((tm, tn), F32)] if nk > 1 else [],
        compiler_params=_cp(("parallel", "parallel", "parallel", "arbitrary")),
        name=name,
    )(*args)


def mm_nt(dy2d, w2d, nb, M, K, name, out_dtype=BF16, res=None, dy_maps=None, tm=None, tko=None, tn=None):
    n = w2d.shape[1]
    assert w2d.shape[0] == nb * K
    tm = tm or _pick(M, (1024, 512, 256, 128))
    tko = tko or _pick(K, (1024, 512, 256, 128))
    tn = tn or _pick(n, (1408, 1024, 768, 512, 256, 128))
    nt, nko = n // tn, K // tko
    has_res = res is not None
    if dy_maps is None:
        dy_maps = [lambda i, j, t: (i, j * nt + t)]
    nd = len(dy_maps)
    td = tn // nd

    def body(*refs):
        d_refs, w_ref = refs[:nd], refs[nd]
        r_ref = refs[nd + 1] if has_res else None
        o_ref, acc_ref = refs[-2], refs[-1]
        j, t = pl.program_id(2), pl.program_id(3)
        d = d_refs[0][...] if nd == 1 else jnp.concatenate([r[...] for r in d_refs], axis=1)
        part = _dot_nt(d, w_ref[...])
        first = jnp.logical_and(j == 0, t == 0)
        last = jnp.logical_and(j == nb - 1, t == nt - 1)

        @pl.when(first)
        def _():
            acc_ref[...] = part

        @pl.when(jnp.logical_not(first))
        def _():
            acc_ref[...] += part

        @pl.when(last)
        def _():
            acc = acc_ref[...]
            if r_ref is not None:
                acc = acc + r_ref[...].astype(F32)
            o_ref[...] = acc.astype(o_ref.dtype)

    in_specs = [pl.BlockSpec((tm, td), functools.partial(lambda f, i, ko, j, t: f(i, j, t), f)) for f in dy_maps]
    in_specs.append(pl.BlockSpec((tko, tn), lambda i, ko, j, t: (j * nko + ko, t)))
    args = [dy2d] * nd + [w2d]
    out_spec = pl.BlockSpec((tm, tko), lambda i, ko, j, t: (i, ko))
    if has_res:
        in_specs.append(out_spec)
        args.append(res)
    return pl.pallas_call(
        body,
        grid=(M // tm, nko, nb, nt),
        in_specs=in_specs,
        out_specs=out_spec,
        out_shape=jax.ShapeDtypeStruct((M, K), out_dtype),
        scratch_shapes=[pltpu.VMEM((tm, tko), F32)],
        compiler_params=_cp(("parallel", "parallel", "arbitrary", "arbitrary")),
        name=name,
    )(*args)


def mm_tn(x, dy2d, nb, n, name, out_dtype=BF16, dy_maps=None, tko=None, tn=None):
    S, K = x.shape
    tko = tko or _pick(K, (512, 256, 128))
    tn = tn or _pick(n, (1408, 1024, 768, 512, 256, 128))
    nt, nko = n // tn, K // tko
    if dy_maps is None:
        dy_maps = [lambda j, t: (0, j * nt + t)]
    nd = len(dy_maps)
    td = tn // nd

    def body(*refs):
        x_ref, d_refs, o_ref = refs[0], refs[1 : 1 + nd], refs[-1]
        d = d_refs[0][...] if nd == 1 else jnp.concatenate([r[...] for r in d_refs], axis=1)
        o_ref[...] = _dot_tn(x_ref[...], d).astype(o_ref.dtype)

    in_specs = [pl.BlockSpec((S, tko), lambda ko, j, t: (0, ko))]
    in_specs += [pl.BlockSpec((S, td), functools.partial(lambda f, ko, j, t: f(j, t), f)) for f in dy_maps]
    return pl.pallas_call(
        body,
        grid=(nko, nb, nt),
        in_specs=in_specs,
        out_specs=pl.BlockSpec((tko, tn), lambda ko, j, t: (j * nko + ko, t)),
        out_shape=jax.ShapeDtypeStruct((nb * K, n), out_dtype),
        compiler_params=_cp(("parallel", "parallel", "parallel")),
        name=name,
    )(x, *([dy2d] * nd))


def rms_fwd(x, g, name):
    S, D = x.shape
    tm = _pick(S, (256, 128))

    def body(x_ref, g_ref, o_ref):
        xf = x_ref[...]
        r = lax.rsqrt(jnp.mean(xf * xf, axis=-1, keepdims=True) + EPS)
        o_ref[...] = (xf * r * g_ref[...]).astype(o_ref.dtype)

    return pl.pallas_call(
        body,
        grid=(S // tm,),
        in_specs=[pl.BlockSpec((tm, D), lambda i: (i, 0)), pl.BlockSpec((1, D), lambda i: (0, 0))],
        out_specs=pl.BlockSpec((tm, D), lambda i: (i, 0)),
        out_shape=jax.ShapeDtypeStruct((S, D), BF16),
        compiler_params=_cp(("parallel",)),
        name=name,
    )(x, g.reshape(1, D))


def rms_bwd(x, g, dh, dres, name):
    S, D = x.shape
    tm = _pick(S, (256, 128))

    def body(x_ref, g_ref, dh_ref, dr_ref, dx_ref, dxb_ref, dg_ref):
        i = pl.program_id(0)
        xf = x_ref[...]
        dh = dh_ref[...].astype(F32)
        r = lax.rsqrt(jnp.mean(xf * xf, axis=-1, keepdims=True) + EPS)
        gy = dh * g_ref[...]
        proj = jnp.mean(gy * xf, axis=-1, keepdims=True)
        dx = dr_ref[...] + r * gy - xf * (r * r * r * proj)
        dx_ref[...] = dx
        dxb_ref[...] = dx.astype(BF16)
        dg = jnp.sum(dh * (xf * r), axis=0, keepdims=True)

        @pl.when(i == 0)
        def _():
            dg_ref[...] = dg

        @pl.when(i > 0)
        def _():
            dg_ref[...] += dg

    row = pl.BlockSpec((tm, D), lambda i: (i, 0))
    vec = pl.BlockSpec((1, D), lambda i: (0, 0))
    return pl.pallas_call(
        body,
        grid=(S // tm,),
        in_specs=[row, vec, row, row],
        out_specs=[row, row, vec],
        out_shape=[jax.ShapeDtypeStruct((S, D), F32), jax.ShapeDtypeStruct((S, D), BF16), jax.ShapeDtypeStruct((1, D), F32)],
        compiler_params=_cp(("arbitrary",)),
        name=name,
    )(x, g.reshape(1, D), dh, dres)


def loss_head(x, g, target, name):
    S, D = x.shape
    tm = _pick(S, (256, 128))

    def body(x_ref, g_ref, t_ref, dx_ref, dxb_ref, dg_ref, loss_ref):
        i = pl.program_id(0)
        xf = x_ref[...]
        gg = g_ref[...]
        r = lax.rsqrt(jnp.mean(xf * xf, axis=-1, keepdims=True) + EPS)
        xh = xf * r
        err = xh * gg - t_ref[...]
        part = (0.5 / D) * jnp.sum(err * err)
        dy = err * (1.0 / D)
        gy = dy * gg
        proj = jnp.mean(gy * xf, axis=-1, keepdims=True)
        dx = r * gy - xf * (r * r * r * proj)
        dx_ref[...] = dx
        dxb_ref[...] = dx.astype(BF16)
        dg = jnp.sum(dy * xh, axis=0, keepdims=True)
        lossb = jnp.full(loss_ref.shape, part, F32)

        @pl.when(i == 0)
        def _():
            dg_ref[...] = dg
            loss_ref[...] = lossb

        @pl.when(i > 0)
        def _():
            dg_ref[...] += dg
            loss_ref[...] += lossb

    row = pl.BlockSpec((tm, D), lambda i: (i, 0))
    vec = pl.BlockSpec((1, D), lambda i: (0, 0))
    return pl.pallas_call(
        body,
        grid=(S // tm,),
        in_specs=[row, vec, row],
        out_specs=[row, row, vec, pl.BlockSpec((8, 128), lambda i: (0, 0))],
        out_shape=[
            jax.ShapeDtypeStruct((S, D), F32),
            jax.ShapeDtypeStruct((S, D), BF16),
            jax.ShapeDtypeStruct((1, D), F32),
            jax.ShapeDtypeStruct((8, 128), F32),
        ],
        compiler_params=_cp(("arbitrary",)),
        name=name,
    )(x, g.reshape(1, D), target)


def _shift_down(s, k):
    if k == 0:
        return s
    return jnp.where(_iota2(s.shape, 0) >= k, pltpu.roll(s, k, axis=0), 0.0)


def _shift_up(s, k):
    if k == 0:
        return s
    n = s.shape[0]
    return jnp.where(_iota2(s.shape, 0) < n - k, pltpu.roll(s, n - k, axis=0), 0.0)


def _conv(s, w):
    return w[0:1] * _shift_down(s, 2) + w[1:2] * _shift_down(s, 1) + w[2:3] * s


def _conv_t(d, w):
    return w[2:3] * d + w[1:2] * _shift_up(d, 1) + w[0:1] * _shift_up(d, 2)


def _conv_dw(d, s):
    return [jnp.sum(d * _shift_down(s, CONV_K - 1 - k), axis=0, keepdims=True) for k in range(CONV_K)]


def sc_fwd(p, convw, cat, W, name):
    S = p.shape[0]
    tc = _pick(W, (256, 128))
    nc = W // tc

    def body(gb_ref, gc_ref, hi_ref, w_ref, cat_ref, o_ref):
        s = gc_ref[...].astype(F32) * hi_ref[...].astype(F32)
        o_ref[...] = (gb_ref[...].astype(F32) * _conv(s, w_ref[...])).astype(o_ref.dtype)

    col = lambda part: pl.BlockSpec((S, tc), lambda c: (0, part * nc + c))
    return pl.pallas_call(
        body,
        grid=(nc,),
        in_specs=[col(3), col(4), col(5), pl.BlockSpec((CONV_K, tc), lambda c: (0, c)), pl.BlockSpec(memory_space=pl.ANY)],
        out_specs=col(1),
        out_shape=jax.ShapeDtypeStruct(cat.shape, cat.dtype),
        input_output_aliases={4: 0},
        compiler_params=_cp(("parallel",)),
        name=name,
    )(p, p, p, convw, cat)


def sc_bwd(p, convw, dcat, dp, W, name):
    S = p.shape[0]
    tc = _pick(W, (256, 128))
    nc = W // tc

    def body(gb_ref, gc_ref, hi_ref, w_ref, do_ref, dp_in_ref, dp_ref, dw_ref):
        gb = gb_ref[...].astype(F32)
        gc = gc_ref[...].astype(F32)
        hi = hi_ref[...].astype(F32)
        w = w_ref[...]
        do = do_ref[...].astype(F32)
        s = gc * hi
        dcs = do * gb
        ds = _conv_t(dcs, w)
        dp_ref[0] = (do * _conv(s, w)).astype(dp_ref.dtype)
        dp_ref[1] = (ds * hi).astype(dp_ref.dtype)
        dp_ref[2] = (ds * gc).astype(dp_ref.dtype)
        for k, row in enumerate(_conv_dw(dcs, s)):
            dw_ref[k : k + 1, :] = row

    col = lambda part: pl.BlockSpec((S, tc), lambda c: (0, part * nc + c))
    return pl.pallas_call(
        body,
        grid=(nc,),
        in_specs=[
            col(3), col(4), col(5),
            pl.BlockSpec((CONV_K, tc), lambda c: (0, c)),
            pl.BlockSpec((S, tc), lambda c: (0, nc + c)),
            pl.BlockSpec(memory_space=pl.ANY),
        ],
        out_specs=[pl.BlockSpec((3, S, tc), lambda c: (1, 0, c)), pl.BlockSpec((CONV_K, tc), lambda c: (0, c))],
        out_shape=[jax.ShapeDtypeStruct(dp.shape, dp.dtype), jax.ShapeDtypeStruct((CONV_K, W), F32)],
        input_output_aliases={5: 0},
        compiler_params=_cp(("parallel",)),
        name=name,
    )(p, p, p, convw, dcat, dp)


def _silu_parts(a):
    sig = 1.0 / (1.0 + jnp.exp(-a))
    return a * sig, sig


def ffn_act_fwd(u, convw, F, name):
    S = u.shape[0]
    tc = _pick(F, (256, 128))
    nc = F // tc

    def body(ug_ref, uu_ref, wg_ref, wu_ref, o_ref):
        ag = _conv(ug_ref[...].astype(F32), wg_ref[...])
        au = _conv(uu_ref[...].astype(F32), wu_ref[...])
        o_ref[...] = (_silu_parts(ag)[0] * au).astype(o_ref.dtype)

    col = lambda half: pl.BlockSpec((S, tc), lambda c: (0, half * nc + c))
    wcol = lambda half: pl.BlockSpec((CONV_K, tc), lambda c: (0, half * nc + c))
    return pl.pallas_call(
        body,
        grid=(nc,),
        in_specs=[col(0), col(1), wcol(0), wcol(1)],
        out_specs=pl.BlockSpec((S, tc), lambda c: (0, c)),
        out_shape=jax.ShapeDtypeStruct((S, F), BF16),
        compiler_params=_cp(("parallel",)),
        name=name,
    )(u, u, convw, convw)


def ffn_act_bwd(u, convw, dact, F, name):
    S = u.shape[0]
    tc = _pick(F, (256, 128))
    nc = F // tc

    def body(ug_ref, uu_ref, wg_ref, wu_ref, da_ref, du_ref, dw_ref):
        ug = ug_ref[...].astype(F32)
        uu = uu_ref[...].astype(F32)
        wg = wg_ref[...]
        wu = wu_ref[...]
        da = da_ref[...].astype(F32)
        ag = _conv(ug, wg)
        au = _conv(uu, wu)
        sl, sig = _silu_parts(ag)
        dag = da * au * (sig * (1.0 + ag * (1.0 - sig)))
        dau = da * sl
        du_ref[0] = _conv_t(dag, wg).astype(du_ref.dtype)
        du_ref[1] = _conv_t(dau, wu).astype(du_ref.dtype)
        for k, (rg, ru) in enumerate(zip(_conv_dw(dag, ug), _conv_dw(dau, uu))):
            dw_ref[0, k : k + 1, :] = rg
            dw_ref[1, k : k + 1, :] = ru

    col = lambda half: pl.BlockSpec((S, tc), lambda c: (0, half * nc + c))
    wcol = lambda half: pl.BlockSpec((CONV_K, tc), lambda c: (0, half * nc + c))
    return pl.pallas_call(
        body,
        grid=(nc,),
        in_specs=[col(0), col(1), wcol(0), wcol(1), pl.BlockSpec((S, tc), lambda c: (0, c))],
        out_specs=[pl.BlockSpec((2, S, tc), lambda c: (0, 0, c)), pl.BlockSpec((2, CONV_K, tc), lambda c: (0, 0, c))],
        out_shape=[jax.ShapeDtypeStruct((2, S, F), BF16), jax.ShapeDtypeStruct((2, CONV_K, F), F32)],
        compiler_params=_cp(("parallel",)),
        name=name,
    )(u, u, convw, convw, dact)


def _softplus(z):
    return jnp.maximum(z, 0.0) + jnp.log(1.0 + jnp.exp(-jnp.abs(z)))


def _key_strip(S):
    return _pick(S, (512, 256, 128))


def _split2(x):
    hi = x.astype(BF16)
    return hi, (x - hi.astype(F32)).astype(BF16)


def _block_sums(x, ones_bf16):
    hi, lo = _split2(x)
    return [
        _dot(hi[:, b * HD : (b + 1) * HD], ones_bf16) + _dot(lo[:, b * HD : (b + 1) * HD], ones_bf16)
        for b in range(x.shape[1] // HD)
    ]


def _strip_mask(shape, i, off, strict):
    cols, rows = _iota2(shape, 1) + off, _iota2(shape, 0) + i * HD
    return cols < rows if strict else cols <= rows


def _sb_strip(q, ks, i, off, run, su):
    z = _dot_nt(q, ks) * (HD ** -0.5)
    mask = _strip_mask(z.shape, i, off, True)
    sp = _softplus(z)
    l = jnp.where(mask, -sp, 0.0)
    within = _block_sums(l, su)
    later = [None] * len(within)
    for b in reversed(range(len(within))):
        later[b] = within[b] + run
        run = run + jnp.sum(l[:, b * HD : (b + 1) * HD], axis=1, keepdims=True)
    a = jnp.where(mask, jnp.exp(z - sp + jnp.concatenate(later, axis=1)), 0.0)
    return z, mask, a, run


def sb_fwd(p, W, name):
    S = p.shape[0]
    nh, nq = W // HD, S // HD
    TK = _key_strip(S)

    def body(q_ref, k_ref, v_ref, o_ref):
        i = pl.program_id(1)
        q = q_ref[...]
        su = (_iota2((HD, HD), 0) > _iota2((HD, HD), 1)).astype(BF16)
        last = (i * HD) // TK

        def step(gg, carry):
            acc, run = carry
            off = pl.multiple_of((last - gg) * TK, TK)
            _, _, a, run = _sb_strip(q, k_ref[pl.ds(off, TK), :], i, off, run, su)
            return acc + _dot(a.astype(BF16), v_ref[pl.ds(off, TK), :]), run

        acc, _ = lax.fori_loop(0, last + 1, step, (jnp.zeros((HD, HD), F32), jnp.zeros((HD, 1), F32)))
        o_ref[...] = acc.astype(o_ref.dtype)

    return pl.pallas_call(
        body,
        grid=(nh, nq),
        in_specs=[
            pl.BlockSpec((HD, HD), lambda h, i: (i, h)),
            pl.BlockSpec((S, HD), lambda h, i: (0, nh + h)),
            pl.BlockSpec((S, HD), lambda h, i: (0, 2 * nh + h)),
        ],
        out_specs=pl.BlockSpec((HD, HD), lambda h, i: (i, h)),
        out_shape=jax.ShapeDtypeStruct((S, 2 * W), BF16),
        compiler_params=_cp(("parallel", "arbitrary")),
        name=name,
    )(p, p, p)


def sb_bwd(p, dcat, W, name):
    S = p.shape[0]
    nh, nq = W // HD, S // HD
    TK = _key_strip(S)
    scale = HD ** -0.5

    def body(q_ref, k_ref, v_ref, do_ref, dp_ref, dk_acc, dv_acc, e_scr, z_scr):
        i = pl.program_id(1)
        q = q_ref[...]
        do = do_ref[...]
        su = (_iota2((HD, HD), 0) > _iota2((HD, HD), 1)).astype(BF16)
        sl = (_iota2((HD, HD), 0) < _iota2((HD, HD), 1)).astype(BF16)
        last = (i * HD) // TK

        @pl.when(i == 0)
        def _():
            dk_acc[...] = jnp.zeros_like(dk_acc)
            dv_acc[...] = jnp.zeros_like(dv_acc)

        def pass_a(gg, run):
            g = last - gg
            off = pl.multiple_of(g * TK, TK)
            z, _, a, run = _sb_strip(q, k_ref[pl.ds(off, TK), :], i, off, run, su)
            e_scr[g] = a * _dot_nt(do, v_ref[pl.ds(off, TK), :])
            z_scr[g] = z
            dv_acc[pl.ds(off, TK), :] += _dot_tn(a.astype(BF16), do)
            return run

        lax.fori_loop(0, last + 1, pass_a, jnp.zeros((HD, 1), F32))

        def pass_b(g, carry):
            dq, run_e = carry
            off = pl.multiple_of(g * TK, TK)
            e = e_scr[g]
            z = z_scr[g]
            mask = _strip_mask(z.shape, i, off, True)
            within = _block_sums(e, sl)
            before = []
            for b in range(len(within)):
                before.append(within[b] + run_e)
                run_e = run_e + jnp.sum(e[:, b * HD : (b + 1) * HD], axis=1, keepdims=True)
            sig = 1.0 / (1.0 + jnp.exp(-z))
            dz = jnp.where(mask, e * (1.0 - sig) - jnp.concatenate(before, axis=1) * sig, 0.0)
            dz = (dz * scale).astype(BF16)
            dq = dq + _dot(dz, k_ref[pl.ds(off, TK), :])
            dk_acc[pl.ds(off, TK), :] += _dot_tn(dz, q)
            return dq, run_e

        dq, _ = lax.fori_loop(0, last + 1, pass_b, (jnp.zeros((HD, HD), F32), jnp.zeros((HD, 1), F32)))
        dp_ref[0, pl.ds(pl.multiple_of(i * HD, HD), HD), :] = dq.astype(dp_ref.dtype)

        @pl.when(i == nq - 1)
        def _():
            dp_ref[1] = dk_acc[...].astype(dp_ref.dtype)
            dp_ref[2] = dv_acc[...].astype(dp_ref.dtype)

    return pl.pallas_call(
        body,
        grid=(nh, nq),
        in_specs=[
            pl.BlockSpec((HD, HD), lambda h, i: (i, h)),
            pl.BlockSpec((S, HD), lambda h, i: (0, nh + h)),
            pl.BlockSpec((S, HD), lambda h, i: (0, 2 * nh + h)),
            pl.BlockSpec((HD, HD), lambda h, i: (i, h)),
        ],
        out_specs=pl.BlockSpec((3, S, HD), lambda h, i: (0, 0, h)),
        out_shape=jax.ShapeDtypeStruct((6, S, W), BF16),
        scratch_shapes=[
            pltpu.VMEM((S, HD), F32),
            pltpu.VMEM((S, HD), F32),
            pltpu.VMEM((S // TK, HD, TK), F32),
            pltpu.VMEM((S // TK, HD, TK), F32),
        ],
        compiler_params=_cp(("parallel", "arbitrary")),
        name=name,
    )(p, p, p, dcat)


def fox_gate_fwd(f, b, name):
    S = f.shape[0]
    nq = S // HD

    def body(f_ref, b_ref, c_ref, run):
        i = pl.program_id(0)

        @pl.when(i == 0)
        def _():
            run[...] = jnp.zeros_like(run)

        lf = -_softplus(-(f_ref[...] + b_ref[...]))
        tri = (_iota2((HD, HD), 0) >= _iota2((HD, HD), 1)).astype(BF16)
        c_ref[...] = _dot_ones_left(tri, lf) + run[...]
        run[...] += jnp.sum(lf, axis=0, keepdims=True)

    return pl.pallas_call(
        body,
        grid=(nq,),
        in_specs=[pl.BlockSpec((HD, 128), lambda i: (i, 0)), pl.BlockSpec((1, 128), lambda i: (0, 0))],
        out_specs=pl.BlockSpec((HD, 128), lambda i: (i, 0)),
        out_shape=jax.ShapeDtypeStruct((S, 128), F32),
        scratch_shapes=[pltpu.VMEM((1, 128), F32)],
        compiler_params=_cp(("arbitrary",)),
        name=name,
    )(f, b)


def fox_gate_bwd(f, b, dc, name):
    S = f.shape[0]
    nq = S // HD

    def body(f_ref, b_ref, dc_ref, df_ref, db_ref, run):
        i = pl.program_id(0)

        @pl.when(i == 0)
        def _():
            run[...] = jnp.zeros_like(run)

        dc = dc_ref[...]
        tri = (_iota2((HD, HD), 0) <= _iota2((HD, HD), 1)).astype(BF16)
        dlf = _dot_ones_left(tri, dc) + run[...]
        run[...] += jnp.sum(dc, axis=0, keepdims=True)
        x = f_ref[...] + b_ref[...]
        df = dlf * (1.0 / (1.0 + jnp.exp(x)))
        df_ref[...] = df
        db = jnp.sum(df, axis=0, keepdims=True)

        @pl.when(i == 0)
        def _():
            db_ref[...] = db

        @pl.when(i > 0)
        def _():
            db_ref[...] += db

    rev = pl.BlockSpec((HD, 128), lambda i: (nq - 1 - i, 0))
    vec = pl.BlockSpec((1, 128), lambda i: (0, 0))
    return pl.pallas_call(
        body,
        grid=(nq,),
        in_specs=[rev, vec, rev],
        out_specs=[rev, vec],
        out_shape=[jax.ShapeDtypeStruct((S, 128), F32), jax.ShapeDtypeStruct((1, 128), F32)],
        scratch_shapes=[pltpu.VMEM((1, 128), F32)],
        compiler_params=_cp(("arbitrary",)),
        name=name,
    )(f, b, dc)


def _fox_logits(q, ks, ct, cs, i, off):
    s = _dot_nt(q, ks) * (HD ** -0.5) + (ct - cs)
    mask = _strip_mask(s.shape, i, off, False)
    return jnp.where(mask, s, -1e30), mask


def fox_fwd(p, ccol, crow, cat, W, name):
    S = p.shape[0]
    nh, nq = W // HD, S // HD
    TK = _key_strip(S)

    def body(q_ref, k_ref, v_ref, cc_ref, cr_ref, cat_ref, o_ref, lse_ref):
        i = pl.program_id(1)
        q = q_ref[...]
        ct = cc_ref[0]

        def step(g, carry):
            m, l, acc = carry
            off = pl.multiple_of(g * TK, TK)
            s, _ = _fox_logits(q, k_ref[pl.ds(off, TK), :], ct, cr_ref[0, pl.ds(g, 1), :], i, off)
            m_new = jnp.maximum(m, jnp.max(s, axis=1, keepdims=True))
            alpha = jnp.exp(m - m_new)
            pr = jnp.exp(s - m_new)
            l = alpha * l + jnp.sum(pr, axis=1, keepdims=True)
            acc = alpha * acc + _dot(pr.astype(BF16), v_ref[pl.ds(off, TK), :])
            return m_new, l, acc

        init = (jnp.full((HD, 1), -1e30, F32), jnp.zeros((HD, 1), F32), jnp.zeros((HD, HD), F32))
        m, l, acc = lax.fori_loop(0, (i * HD) // TK + 1, step, init)
        o_ref[...] = (acc / l).astype(o_ref.dtype)
        lse_ref[0] = m + jnp.log(l)

    return pl.pallas_call(
        body,
        grid=(nh, nq),
        in_specs=[
            pl.BlockSpec((HD, HD), lambda h, i: (i, 2 * nh + h)),
            pl.BlockSpec((S, HD), lambda h, i: (0, 3 * nh + h)),
            pl.BlockSpec((S, HD), lambda h, i: (0, 4 * nh + h)),
            pl.BlockSpec((1, HD, 1), lambda h, i: (h, i, 0)),
            pl.BlockSpec((1, S // TK, TK), lambda h, i: (h, 0, 0)),
            pl.BlockSpec(memory_space=pl.ANY),
        ],
        out_specs=[pl.BlockSpec((HD, HD), lambda h, i: (i, nh + h)), pl.BlockSpec((1, HD, 1), lambda h, i: (h, i, 0))],
        out_shape=[jax.ShapeDtypeStruct(cat.shape, cat.dtype), jax.ShapeDtypeStruct((nh, S, 1), F32)],
        input_output_aliases={5: 0},
        compiler_params=_cp(("parallel", "arbitrary")),
        name=name,
    )(p, p, p, ccol, crow, cat)


def fox_bwd(p, ccol, crow, cat, lse, dcat, dp, W, name):
    S = p.shape[0]
    nh, nq = W // HD, S // HD
    TK = _key_strip(S)
    scale = HD ** -0.5

    def body(q_ref, k_ref, v_ref, cc_ref, cr_ref, o_ref, lse_ref, do_ref, dp_in_ref, dp_ref, dcs_ref, dct_ref, dk_acc, dv_acc):
        i = pl.program_id(1)
        q = q_ref[...]
        do = do_ref[...]
        ct = cc_ref[0]
        lse_i = lse_ref[0]
        delta = jnp.sum(do.astype(F32) * o_ref[...].astype(F32), axis=1, keepdims=True)

        @pl.when(i == 0)
        def _():
            dk_acc[...] = jnp.zeros_like(dk_acc)
            dv_acc[...] = jnp.zeros_like(dv_acc)
            dcs_ref[...] = jnp.zeros_like(dcs_ref)

        def step(g, carry):
            dq, dct = carry
            off = pl.multiple_of(g * TK, TK)
            ks = k_ref[pl.ds(off, TK), :]
            s, mask = _fox_logits(q, ks, ct, cr_ref[0, pl.ds(g, 1), :], i, off)
            pr = jnp.where(mask, jnp.exp(s - lse_i), 0.0)
            ds = pr * (_dot_nt(do, v_ref[pl.ds(off, TK), :]) - delta)
            dv_acc[pl.ds(off, TK), :] += _dot_tn(pr.astype(BF16), do)
            dsb = (ds * scale).astype(BF16)
            dk_acc[pl.ds(off, TK), :] += _dot_tn(dsb, q)
            dcs_ref[0, pl.ds(g, 1), :] += jnp.sum(ds, axis=0, keepdims=True)
            return dq + _dot(dsb, ks), dct + jnp.sum(ds, axis=1, keepdims=True)

        dq, dct = lax.fori_loop(0, (i * HD) // TK + 1, step, (jnp.zeros((HD, HD), F32), jnp.zeros((HD, 1), F32)))
        dp_ref[0, pl.ds(pl.multiple_of(i * HD, HD), HD), :] = dq.astype(dp_ref.dtype)
        dct_ref[0] = dct

        @pl.when(i == nq - 1)
        def _():
            dp_ref[1] = dk_acc[...].astype(dp_ref.dtype)
            dp_ref[2] = dv_acc[...].astype(dp_ref.dtype)

    return pl.pallas_call(
        body,
        grid=(nh, nq),
        in_specs=[
            pl.BlockSpec((HD, HD), lambda h, i: (i, 2 * nh + h)),
            pl.BlockSpec((S, HD), lambda h, i: (0, 3 * nh + h)),
            pl.BlockSpec((S, HD), lambda h, i: (0, 4 * nh + h)),
            pl.BlockSpec((1, HD, 1), lambda h, i: (h, i, 0)),
            pl.BlockSpec((1, S // TK, TK), lambda h, i: (h, 0, 0)),
            pl.BlockSpec((HD, HD), lambda h, i: (i, nh + h)),
            pl.BlockSpec((1, HD, 1), lambda h, i: (h, i, 0)),
            pl.BlockSpec((HD, HD), lambda h, i: (i, nh + h)),
            pl.BlockSpec(memory_space=pl.ANY),
        ],
        out_specs=[
            pl.BlockSpec((3, S, HD), lambda h, i: (1, 0, h)),
            pl.BlockSpec((1, S // TK, TK), lambda h, i: (h, 0, 0)),
            pl.BlockSpec((1, HD, 1), lambda h, i: (h, i, 0)),
        ],
        out_shape=[
            jax.ShapeDtypeStruct(dp.shape, dp.dtype),
            jax.ShapeDtypeStruct((nh, S // TK, TK), F32),
            jax.ShapeDtypeStruct((nh, S, 1), F32),
        ],
        input_output_aliases={8: 0},
        scratch_shapes=[pltpu.VMEM((S, HD), F32), pltpu.VMEM((S, HD), F32)],
        compiler_params=_cp(("parallel", "arbitrary")),
        name=name,
    )(p, p, p, ccol, crow, cat, lse, dcat, dp)


_GELU_K = math.sqrt(2.0 / math.pi)
_GELU_C = 0.044715


def _gelu(x):
    return 0.5 * x * (1.0 + jnp.tanh(_GELU_K * (x + _GELU_C * x * x * x)))


def _gelu_grad(x):
    t = jnp.tanh(_GELU_K * (x + _GELU_C * x * x * x))
    return 0.5 * (1.0 + t) + 0.5 * x * (1.0 - t * t) * (_GELU_K * (1.0 + 3.0 * _GELU_C * x * x))


def _layernorm_parts(gv):
    xc = gv - jnp.mean(gv, axis=-1, keepdims=True)
    r = lax.rsqrt(jnp.mean(xc * xc, axis=-1, keepdims=True) + EPS)
    return xc * r, r


def sg_fwd(p, sg_w, sg_bt, sg_g, W, name):
    S = p.shape[0]
    G, nq = W // HD, S // HD

    def body(u_ref, v_ref, w_ref, bt_ref, g_ref, o_ref):
        xh, _ = _layernorm_parts(_gelu(v_ref[...].astype(F32)))
        vn = (xh * g_ref[...]).astype(BF16)
        tri = _iota2((HD, HD), 0) >= _iota2((HD, HD), 1)
        for gi in range(G):
            cols = slice(gi * HD, (gi + 1) * HD)
            wt = jnp.where(tri, w_ref[gi], 0.0).astype(BF16)
            mixed = _dot(wt, vn[:, cols]) + bt_ref[:, gi : gi + 1]
            o_ref[:, cols] = (_gelu(u_ref[:, cols].astype(F32)) * mixed).astype(o_ref.dtype)

    return pl.pallas_call(
        body,
        grid=(nq,),
        in_specs=[
            pl.BlockSpec((HD, W), lambda i: (i, 0)),
            pl.BlockSpec((HD, W), lambda i: (i, 1)),
            pl.BlockSpec((G, HD, HD), lambda i: (0, 0, 0)),
            pl.BlockSpec((HD, G), lambda i: (0, 0)),
            pl.BlockSpec((1, W), lambda i: (0, 0)),
        ],
        out_specs=pl.BlockSpec((HD, W), lambda i: (i, 0)),
        out_shape=jax.ShapeDtypeStruct((S, 2 * W), BF16),
        compiler_params=_cp(("parallel",)),
        name=name,
    )(p, p, sg_w, sg_bt, sg_g.reshape(1, W))


def sg_bwd(p, sg_w, sg_bt, sg_g, dcat, W, name):
    S = p.shape[0]
    G, nq = W // HD, S // HD

    def body(u_ref, v_ref, w_ref, bt_ref, g_ref, do_ref, dp_ref, dw_ref, dbt_ref, dg_ref, dvn_scr):
        i = pl.program_id(0)

        @pl.when(i == 0)
        def _():
            dw_ref[...] = jnp.zeros_like(dw_ref)
            dbt_ref[...] = jnp.zeros_like(dbt_ref)
            dg_ref[...] = jnp.zeros_like(dg_ref)

        v = v_ref[...].astype(F32)
        xh, r = _layernorm_parts(_gelu(v))
        gg = g_ref[...]
        vn = (xh * gg).astype(BF16)
        tri = _iota2((HD, HD), 0) >= _iota2((HD, HD), 1)
        for gi in range(G):
            cols = slice(gi * HD, (gi + 1) * HD)
            wt = jnp.where(tri, w_ref[gi], 0.0).astype(BF16)
            mixed = _dot(wt, vn[:, cols]) + bt_ref[:, gi : gi + 1]
            u = u_ref[:, cols].astype(F32)
            do = do_ref[:, cols].astype(F32)
            dp_ref[0, :, cols] = (do * mixed * _gelu_grad(u)).astype(dp_ref.dtype)
            dmix = do * _gelu(u)
            dmb = dmix.astype(BF16)
            dw_ref[gi] += jnp.where(tri, _dot_nt(dmb, vn[:, cols]), 0.0)
            dbt_ref[:, gi : gi + 1] += jnp.sum(dmix, axis=1, keepdims=True)
            dvn_scr[:, cols] = _dot_tn(wt, dmb)
        dvn = dvn_scr[...]
        dg_ref[...] += jnp.sum(dvn * xh, axis=0, keepdims=True)
        dxh = dvn * gg
        dgv = r * (dxh - jnp.mean(dxh, axis=-1, keepdims=True) - xh * jnp.mean(dxh * xh, axis=-1, keepdims=True))
        dp_ref[1] = (dgv * _gelu_grad(v)).astype(dp_ref.dtype)

    return pl.pallas_call(
        body,
        grid=(nq,),
        in_specs=[
            pl.BlockSpec((HD, W), lambda i: (i, 0)),
            pl.BlockSpec((HD, W), lambda i: (i, 1)),
            pl.BlockSpec((G, HD, HD), lambda i: (0, 0, 0)),
            pl.BlockSpec((HD, G), lambda i: (0, 0)),
            pl.BlockSpec((1, W), lambda i: (0, 0)),
            pl.BlockSpec((HD, W), lambda i: (i, 0)),
        ],
        out_specs=[
            pl.BlockSpec((2, HD, W), lambda i: (0, i, 0)),
            pl.BlockSpec((G, HD, HD), lambda i: (0, 0, 0)),
            pl.BlockSpec((HD, G), lambda i: (0, 0)),
            pl.BlockSpec((1, W), lambda i: (0, 0)),
        ],
        out_shape=[
            jax.ShapeDtypeStruct((6, S, W), BF16),
            jax.ShapeDtypeStruct((G, HD, HD), F32),
            jax.ShapeDtypeStruct((HD, G), F32),
            jax.ShapeDtypeStruct((1, W), F32),
        ],
        scratch_shapes=[pltpu.VMEM((HD, W), F32)],
        compiler_params=_cp(("arbitrary",)),
        name=name,
    )(p, p, sg_w, sg_bt, sg_g.reshape(1, W), dcat)


def local_step(x, target, wts):
    S, D = x.shape
    W = D // 2
    nb = wts["nb"]
    F = wts["l0_ffn_down"].shape[0]
    g = {}

    def ffn_fwd(xin, l):
        h = rms_fwd(xin, wts[f"{l}_ffn_norm_g"], f"{l}_ffn_rms")
        u = mm_nn(h, wts[f"{l}_ffn_up"], nb, f"{l}_ffn_up_mm")
        act = ffn_act_fwd(u, wts[f"{l}_ffn_conv_w"], F, f"{l}_ffn_act")
        xout = mm_nn(act, wts[f"{l}_ffn_down"], 1, f"{l}_ffn_down_mm", out_dtype=F32, res=xin)
        return xout, (xin, h, u, act)

    def ffn_bwd(dxout, dxoutb, saved, l):
        xin, h, u, act = saved
        dact = mm_nt(dxoutb, wts[f"{l}_ffn_down"], 1, S, F, f"{l}_ffn_down_dx")
        g[f"{l}_ffn_down"] = mm_tn(act, dxoutb, 1, D, f"{l}_ffn_down_dw")
        du, dcw = ffn_act_bwd(u, wts[f"{l}_ffn_conv_w"], dact, F, f"{l}_ffn_act_bwd")
        g[f"{l}_ffn_conv_w"] = jnp.concatenate([dcw[0], dcw[1]], axis=1)
        du2 = du.reshape(2 * S, F)
        n = wts[f"{l}_ffn_up"].shape[1]
        tn = _pick(n, (1408, 1024, 768, 512, 256, 128))
        per_half = F // tn
        nt = n // tn

        def up_block(i, j, t):
            vb = j * nt + t
            return vb // per_half, vb % per_half

        tm = _pick(S, (1024, 512, 256, 128))

        def nt_map(i, j, t):
            half, cb = up_block(i, j, t)
            return (half * (S // tm) + i, cb)

        def tn_map(j, t):
            half, cb = up_block(0, j, t)
            return (half, cb)

        dh = mm_nt(du2, wts[f"{l}_ffn_up"], nb, S, D, f"{l}_ffn_up_dx", dy_maps=[nt_map], tm=tm, tn=tn)
        g[f"{l}_ffn_up"] = mm_tn(h, du2, nb, n, f"{l}_ffn_up_dw", dy_maps=[tn_map], tn=tn)
        dxin, dxinb, dg = rms_bwd(xin, wts[f"{l}_ffn_norm_g"], dh, dxout, f"{l}_ffn_rms_bwd")
        g[f"{l}_ffn_norm_g"] = dg
        return dxin, dxinb

    h0 = rms_fwd(x, wts["l0_mix_norm_g"], "l0_mix_rms")
    p0 = mm_nn(h0, wts["l0_w_in"], nb, "l0_w_in_mm")
    cat0 = sb_fwd(p0, W, "l0_sb_fwd")
    cat0 = sc_fwd(p0, wts["l0_sc_conv_w"], cat0, W, "l0_sc_fwd")
    x1 = mm_nn(cat0, wts["l0_w_out"], 1, "l0_w_out_mm", out_dtype=F32, res=x)
    x2, ffn0_saved = ffn_fwd(x1, "l0")

    nh = W // HD
    h2 = rms_fwd(x2, wts["l1_mix_norm_g"], "l1_mix_rms")
    p1 = mm_nn(h2, wts["l1_w_in_main"], 1, "l1_w_in_mm")
    f = mm_nn(h2, wts["l1_w_in_f"], 1, "l1_w_f_mm", out_dtype=F32)
    bf = jnp.zeros((1, 128), F32).at[0, :nh].set(wts["l1_fox_b_f"])
    c = fox_gate_fwd(f, bf, "l1_fox_gate")
    c_heads = c[:, :nh].T
    ccol = c_heads[:, :, None]
    crow = c_heads.reshape(nh, S // _key_strip(S), _key_strip(S))
    sg_bt = wts["l1_sg_b"].T
    cat1 = sg_fwd(p1, wts["l1_sg_w"], sg_bt, wts["l1_sg_norm_g"], W, "l1_sg_fwd")
    cat1, lse = fox_fwd(p1, ccol, crow, cat1, W, "l1_fox_fwd")
    x3 = mm_nn(cat1, wts["l1_w_out"], 1, "l1_w_out_mm", out_dtype=F32, res=x2)
    x4, ffn1_saved = ffn_fwd(x3, "l1")

    dx4, dx4b, dgf, loss = loss_head(x4, wts["final_norm_g"], target, "loss_head")
    g["final_norm_g"] = dgf

    dx3, dx3b = ffn_bwd(dx4, dx4b, ffn1_saved, "l1")
    dcat1 = mm_nt(dx3b, wts["l1_w_out"], 1, S, D, "l1_w_out_dx")
    g["l1_w_out"] = mm_tn(cat1, dx3b, 1, D, "l1_w_out_dw")
    dp1, dsgw, dsgbt, dsgg = sg_bwd(p1, wts["l1_sg_w"], sg_bt, wts["l1_sg_norm_g"], dcat1, W, "l1_sg_bwd")
    dp1, dcs, dct = fox_bwd(p1, ccol, crow, cat1, lse, dcat1, dp1, W, "l1_fox_bwd")
    g["l1_sg_w"], g["l1_sg_b"], g["l1_sg_norm_g"] = dsgw, dsgbt.T, dsgg
    dc = jnp.zeros((S, 128), F32).at[:, :nh].set((dct[:, :, 0] - dcs.reshape(nh, S)).T)
    df, dbf = fox_gate_bwd(f, bf, dc, "l1_fox_gate_bwd")
    g["l1_fox_b_f"] = dbf[0, :nh]
    dfb = df.astype(BF16)
    tn1 = _pick(W, (1024, 512, 256, 128))
    tm1 = _pick(S, (1024, 512, 256, 128))
    per_part = W // tn1
    part_of = lambda pt: pt + pt // 2 - pt // 4

    def nt_map1(i, j, t):
        return (part_of(t // per_part) * (S // tm1) + i, t % per_part)

    def tn_map1(j, t):
        return (part_of(t // per_part), t % per_part)

    dp1_2d = dp1.reshape(6 * S, W)
    dh2 = mm_nt(dfb, wts["l1_w_in_f"], 1, S, D, "l1_w_f_dx", out_dtype=F32)
    dh2 = mm_nt(dp1_2d, wts["l1_w_in_main"], 1, S, D, "l1_w_in_dx", res=dh2, dy_maps=[nt_map1], tm=tm1, tn=tn1)
    g["l1_w_in_main"] = mm_tn(h2, dp1_2d, 1, 5 * W, "l1_w_in_dw", dy_maps=[tn_map1], tn=tn1)
    g["l1_w_in_f"] = mm_tn(h2, dfb, 1, 128, "l1_w_f_dw")
    dx2, dx2b, dg = rms_bwd(x2, wts["l1_mix_norm_g"], dh2, dx3, "l1_mix_rms_bwd")
    g["l1_mix_norm_g"] = dg

    dx1, dx1b = ffn_bwd(dx2, dx2b, ffn0_saved, "l0")
    dcat0 = mm_nt(dx1b, wts["l0_w_out"], 1, S, D, "l0_w_out_dx")
    g["l0_w_out"] = mm_tn(cat0, dx1b, 1, D, "l0_w_out_dw")
    dp0 = sb_bwd(p0, dcat0, W, "l0_sb_bwd")
    dp0, dscw = sc_bwd(p0, wts["l0_sc_conv_w"], dcat0, dp0, W, "l0_sc_bwd")
    g["l0_sc_conv_w"] = dscw
    n0 = wts["l0_w_in"].shape[1]
    td0 = math.gcd(n0, W)
    nd0 = n0 // td0
    tm0 = _pick(S, (1024, 512, 256, 128))
    per_part0 = W // td0

    def nt_maps0(k):
        def f(i, j, t):
            vb = j * nd0 + k
            return ((vb // per_part0) * (S // tm0) + i, vb % per_part0)
        return f

    def tn_maps0(k):
        def f(j, t):
            vb = j * nd0 + k
            return (vb // per_part0, vb % per_part0)
        return f

    dp0_2d = dp0.reshape(6 * S, W)
    dh0 = mm_nt(dp0_2d, wts["l0_w_in"], nb, S, D, "l0_w_in_dx", dy_maps=[nt_maps0(k) for k in range(nd0)], tm=tm0, tn=n0)
    g["l0_w_in"] = mm_tn(h0, dp0_2d, nb, n0, "l0_w_in_dw", dy_maps=[tn_maps0(k) for k in range(nd0)], tn=n0)
    dx0, _, dg = rms_bwd(x, wts["l0_mix_norm_g"], dh0, dx1, "l0_mix_rms_bwd")
    g["l0_mix_norm_g"] = dg
    return loss, dx0, g


_HBM = pl.BlockSpec(memory_space=pl.ANY)


def _place():
    return lax.axis_index("x"), lax.axis_index("y"), lax.axis_index("c")


def _other_chips(x, y):
    return [(x, 1 - y), (1 - x, y), (1 - x, 1 - y)]


def all_gather(arrs, name):
    n = len(arrs)

    def body(*refs):
        xs, outs = refs[:n], refs[n : 2 * n]
        send_sems, recv_sems, local_sems = refs[2 * n :]
        x, y, c = _place()
        me, sibling = (x, y, c), (x, y, 1 - c)
        chips = _other_chips(x, y)

        def copy(a, k, block, to, src=None):
            px, py, pc = block
            dst = outs[a].at[4 * px + 2 * py + pc]
            return pltpu.make_async_remote_copy(
                src_ref=dst if src is None else src, dst_ref=dst,
                send_sem=send_sems.at[a, k], recv_sem=recv_sems.at[a, k], device_id=to, device_id_type=MESH,
            )

        mine = [pltpu.make_async_copy(xs[a], outs[a].at[4 * x + 2 * y + c], local_sems.at[a]) for a in range(n)]
        for cp in mine:
            cp.start()
        first = []
        for a in range(n):
            first.append(copy(a, 0, me, sibling, src=xs[a]))
            first += [copy(a, 1 + j, me, (*chip, c), src=xs[a]) for j, chip in enumerate(chips)]
        for cp in first:
            cp.start()
        passed = []
        for a in range(n):
            for j, chip in enumerate(chips):
                copy(a, 1 + j, (*chip, c), me).wait_recv()
                cp = copy(a, 4 + j, (*chip, c), sibling)
                cp.start()
                passed.append(cp)
        for a in range(n):
            copy(a, 0, sibling, me).wait_recv()
            for j, chip in enumerate(chips):
                copy(a, 4 + j, (*chip, 1 - c), me).wait_recv()
        for cp in first + passed:
            cp.wait_send()
        for cp in mine:
            cp.wait()

    return pl.pallas_call(
        body,
        in_specs=[_HBM] * n,
        out_specs=[_HBM] * n,
        out_shape=[jax.ShapeDtypeStruct((NDEV,) + a.shape, a.dtype) for a in arrs],
        scratch_shapes=[pltpu.SemaphoreType.DMA((n, 7)), pltpu.SemaphoreType.DMA((n, 7)), pltpu.SemaphoreType.DMA((n,))],
        name=name,
    )(*arrs)


def exchange_pair(arrs, name):
    n = len(arrs)

    def body(*refs):
        xs, lands = refs[:n], refs[n : 2 * n]
        send_sems, recv_sems = refs[2 * n :]
        x, y, c = _place()
        copies = [
            pltpu.make_async_remote_copy(
                src_ref=xs[a].at[k, 1 - c], dst_ref=lands[a].at[k],
                send_sem=send_sems.at[a, k], recv_sem=recv_sems.at[a, k], device_id=(x, y, 1 - c), device_id_type=MESH,
            )
            for a in range(n)
            for k in range(4)
        ]
        for cp in copies:
            cp.start()
        for cp in copies:
            cp.wait()

    return pl.pallas_call(
        body,
        in_specs=[_HBM] * n,
        out_specs=[_HBM] * n,
        out_shape=[jax.ShapeDtypeStruct((4,) + a.shape[2:], a.dtype) for a in arrs],
        scratch_shapes=[pltpu.SemaphoreType.DMA((n, 4)), pltpu.SemaphoreType.DMA((n, 4))],
        name=name,
    )(*arrs)


def exchange_chips(arrs, name):
    n = len(arrs)

    def body(*refs):
        xs, lands = refs[:n], refs[n : 2 * n]
        send_sems, recv_sems, local_sems = refs[2 * n :]
        x, y, c = _place()
        my_chip = 2 * x + y
        chips = _other_chips(x, y)
        mine = [pltpu.make_async_copy(xs[a].at[my_chip], lands[a].at[my_chip], local_sems.at[a]) for a in range(n)]
        for cp in mine:
            cp.start()
        sends, recvs = [], []
        for a in range(n):
            for j, (px, py) in enumerate(chips):
                peer = 2 * px + py
                sends.append(pltpu.make_async_remote_copy(
                    src_ref=xs[a].at[peer], dst_ref=lands[a].at[my_chip],
                    send_sem=send_sems.at[a, j], recv_sem=recv_sems.at[a, j], device_id=(px, py, c), device_id_type=MESH,
                ))
                recvs.append(pltpu.make_async_remote_copy(
                    src_ref=xs[a].at[peer], dst_ref=lands[a].at[peer],
                    send_sem=send_sems.at[a, j], recv_sem=recv_sems.at[a, j], device_id=(px, py, c), device_id_type=MESH,
                ))
        for cp in sends:
            cp.start()
        for cp in recvs:
            cp.wait_recv()
        for cp in sends:
            cp.wait_send()
        for cp in mine:
            cp.wait()

    return pl.pallas_call(
        body,
        in_specs=[_HBM] * n,
        out_specs=[_HBM] * n,
        out_shape=[jax.ShapeDtypeStruct(a.shape, a.dtype) for a in arrs],
        scratch_shapes=[pltpu.SemaphoreType.DMA((n, 3)), pltpu.SemaphoreType.DMA((n, 3)), pltpu.SemaphoreType.DMA((n,))],
        name=name,
    )(*arrs)


def _row_tile(R, C, max_elems):
    if R * C <= max_elems:
        return R
    best = None
    for tr in range(16, R, 16):
        if R % tr == 0 and tr * C <= max_elems:
            best = tr
    return best or R


def pair_sum(a42, land4, core, name):
    _, _, R, C = a42.shape
    tr = _row_tile(R, C, 1 << 20)

    def body(core_ref, a_ref, l_ref, o_ref):
        o_ref[...] = (a_ref[0].astype(F32) + l_ref[...].astype(F32)).astype(o_ref.dtype)

    return pl.pallas_call(
        body,
        grid_spec=pltpu.PrefetchScalarGridSpec(
            num_scalar_prefetch=1,
            grid=(4, R // tr),
            in_specs=[
                pl.BlockSpec((1, 1, tr, C), lambda k, r, core_ref: (k, core_ref[0], r, 0)),
                pl.BlockSpec((1, tr, C), lambda k, r, core_ref: (k, r, 0)),
            ],
            out_specs=pl.BlockSpec((1, tr, C), lambda k, r, core_ref: (k, r, 0)),
        ),
        out_shape=jax.ShapeDtypeStruct((4, R, C), BF16),
        compiler_params=_cp(("parallel", "parallel")),
        name=name,
    )(core, a42, land4)


def sum_slots(parts, name):
    P, R, C = parts.shape

    def body(p_ref, o_ref):
        acc = p_ref[0].astype(F32)
        for k in range(1, P):
            acc = acc + p_ref[k].astype(F32)
        o_ref[...] = acc

    tr = _row_tile(R, P * C, 1 << 21)
    return pl.pallas_call(
        body,
        grid=(R // tr,),
        in_specs=[pl.BlockSpec((P, tr, C), lambda r: (0, r, 0))],
        out_specs=pl.BlockSpec((tr, C), lambda r: (r, 0)),
        out_shape=jax.ShapeDtypeStruct((R, C), F32),
        compiler_params=_cp(("parallel",)),
        name=name,
    )(parts)


def adamw(w, m, v, parts, name):
    R, C = w.shape
    P = parts.shape[0]
    tr = _pick(R, (256, 128, 64, 32, 16, 8))
    c1 = 1.0 - ADAM_B1 ** ADAM_STEP
    c2 = 1.0 - ADAM_B2 ** ADAM_STEP

    def body(w_ref, m_ref, v_ref, p_ref, g_ref, d_ref, nm_ref, nv_ref):
        g = p_ref[0].astype(F32)
        for k in range(1, P):
            g = g + p_ref[k].astype(F32)
        nm = ADAM_B1 * m_ref[...] + (1.0 - ADAM_B1) * g
        nv = ADAM_B2 * v_ref[...] + (1.0 - ADAM_B2) * (g * g)
        g_ref[...] = g
        nm_ref[...] = nm
        nv_ref[...] = nv
        d_ref[...] = -ADAM_LR * ((nm / c1) / (jnp.sqrt(nv / c2) + ADAM_EPS) + ADAM_WD * w_ref[...])

    blk = pl.BlockSpec((tr, C), lambda r: (r, 0))
    shp = jax.ShapeDtypeStruct((R, C), F32)
    return pl.pallas_call(
        body,
        grid=(R // tr,),
        in_specs=[blk, blk, blk, pl.BlockSpec((P, tr, C), lambda r: (0, r, 0))],
        out_specs=[blk, blk, blk, blk],
        out_shape=[shp, shp, shp, shp],
        compiler_params=_cp(("parallel",)),
        name=name,
    )(w, m, v, parts)


_WEIGHTS = [
    "l0_mix_norm_g", "l0_w_in", "l0_sc_conv_w", "l0_w_out", "l0_ffn_norm_g", "l0_ffn_up", "l0_ffn_conv_w", "l0_ffn_down",
    "l1_mix_norm_g", "l1_w_in", "l1_fox_b_f", "l1_sg_w", "l1_sg_b", "l1_sg_norm_g", "l1_w_out", "l1_ffn_norm_g",
    "l1_ffn_up", "l1_ffn_conv_w", "l1_ffn_down", "final_norm_g",
]
_COL_SHARDED = ["l0_w_in", "l0_ffn_up", "l1_w_in", "l1_ffn_up"]
_ROW_SHARDED = ["l0_w_out", "l0_ffn_down", "l1_w_out", "l1_ffn_down"]
_BIG = ["l0_w_in", "l0_w_out", "l0_ffn_up", "l0_ffn_down", "l1_w_in", "l1_w_out", "l1_ffn_up", "l1_ffn_down"]
_CONV = ["l0_sc_conv_w", "l0_ffn_conv_w", "l1_ffn_conv_w"]
_SMALL = [n for n in _WEIGHTS if n not in _BIG]
_PACK_ROWS = 8


def _pack(arrs):
    flat = []
    for a in arrs:
        v = a.reshape(-1).astype(F32)
        pad = (-v.shape[0]) % (_PACK_ROWS * 128)
        flat.append(jnp.pad(v, (0, pad)))
    return jnp.concatenate(flat).reshape(-1, 128)


def _unpack(packed, shapes):
    out, off = [], 0
    flat = packed.reshape(-1)
    for shp in shapes:
        size = math.prod(shp)
        out.append(flat[off : off + size].reshape(shp))
        off += size + (-size) % (_PACK_ROWS * 128)
    return out


def kernel(x, l0_mix_norm_g, l0_w_in, l0_sc_conv_w, l0_w_out, l0_ffn_norm_g, l0_ffn_up, l0_ffn_conv_w, l0_ffn_down, l1_mix_norm_g, l1_w_in, l1_fox_b_f, l1_sg_w, l1_sg_b, l1_sg_norm_g, l1_w_out, l1_ffn_norm_g, l1_ffn_up, l1_ffn_conv_w, l1_ffn_down, final_norm_g, loss_target, m_l0_mix_norm_g, m_l0_w_in, m_l0_sc_conv_w, m_l0_w_out, m_l0_ffn_norm_g, m_l0_ffn_up, m_l0_ffn_conv_w, m_l0_ffn_down, m_l1_mix_norm_g, m_l1_w_in, m_l1_fox_b_f, m_l1_sg_w, m_l1_sg_b, m_l1_sg_norm_g, m_l1_w_out, m_l1_ffn_norm_g, m_l1_ffn_up, m_l1_ffn_conv_w, m_l1_ffn_down, m_final_norm_g, v_l0_mix_norm_g, v_l0_w_in, v_l0_sc_conv_w, v_l0_w_out, v_l0_ffn_norm_g, v_l0_ffn_up, v_l0_ffn_conv_w, v_l0_ffn_down, v_l1_mix_norm_g, v_l1_w_in, v_l1_fox_b_f, v_l1_sg_w, v_l1_sg_b, v_l1_sg_norm_g, v_l1_w_out, v_l1_ffn_norm_g, v_l1_ffn_up, v_l1_ffn_conv_w, v_l1_ffn_down, v_final_norm_g):
    given = dict(locals())
    w = {n: given[n] for n in _WEIGHTS}
    mom = {n: given["m_" + n] for n in _WEIGHTS}
    var = {n: given["v_" + n] for n in _WEIGHTS}
    xs, target = x[0], loss_target[0]
    S, D = xs.shape
    W = D // 2
    nh = W // HD
    cx, cy, cc = _place()
    me = 4 * cx + 2 * cy + cc

    gathered = all_gather([w[n].astype(BF16) for n in _BIG] + [w[n] for n in _CONV], "gather_weights")
    full = dict(zip(_BIG + _CONV, gathered))
    wts = {"nb": NDEV}
    for n in ("l0_w_in", "l0_ffn_up", "l1_ffn_up"):
        wts[n] = full[n].reshape(NDEV * D, -1)
    for n in _ROW_SHARDED:
        wts[n] = full[n].reshape(-1, D)
    w_in1 = full["l1_w_in"].transpose(1, 0, 2).reshape(D, -1)
    wts["l1_w_in_main"] = w_in1[:, : 5 * W]
    wts["l1_w_in_f"] = jnp.pad(w_in1[:, 5 * W :], ((0, 0), (0, 128 - nh)))
    for n in _CONV:
        wts[n] = full[n].transpose(1, 0, 2).reshape(CONV_K, -1)
    for n in _SMALL:
        if n not in _CONV:
            wts[n] = w[n]

    loss_tile, dx, g = local_step(xs, target, wts)
    loss = lax.psum(loss_tile[0, 0], ("x", "y", "c"))

    dw_in1 = jnp.concatenate([g["l1_w_in_main"], g["l1_w_in_f"][:, :nh]], axis=1)
    terms = {
        "l0_w_in": g["l0_w_in"].reshape(NDEV, D, -1),
        "l0_ffn_up": g["l0_ffn_up"].reshape(NDEV, D, -1),
        "l1_ffn_up": g["l1_ffn_up"].reshape(NDEV, D, -1),
        "l1_w_in": dw_in1.reshape(D, NDEV, -1).transpose(1, 0, 2),
    }
    for n in _ROW_SHARDED:
        terms[n] = g[n].reshape(NDEV, -1, D)
    by_pair = [terms[n].reshape((4, 2) + terms[n].shape[1:]) for n in _BIG]
    landed = exchange_pair(by_pair, "reduce_pair")
    core = jnp.reshape(cc, (1,)).astype(jnp.int32)
    chip_terms = [pair_sum(a, l, core, f"pair_sum_{n}") for n, a, l in zip(_BIG, by_pair, landed)]
    by_chip = exchange_chips(chip_terms, "reduce_chips")
    out_g, out_d, out_m, out_v = {}, {}, {}, {}
    for n, parts in zip(_BIG, by_chip):
        out_g[n], out_d[n], out_m[n], out_v[n] = adamw(w[n], mom[n], var[n], parts, f"adamw_{n}")

    small_terms = [g[n] for n in _SMALL]
    small_shapes = [tuple(t.shape) for t in small_terms]
    packed = _pack(small_terms)
    all_terms = all_gather([packed], "gather_small_grads")[0]
    small_sum = _unpack(sum_slots(all_terms, "sum_small_grads"), small_shapes)
    small_g = {}
    for n, t in zip(_SMALL, small_sum):
        if n in _CONV:
            cols = w[n].shape[1]
            t = lax.dynamic_slice_in_dim(t, me * cols, cols, axis=1)
        small_g[n] = t.reshape(w[n].shape)
    shapes = [w[n].shape for n in _SMALL]
    res = adamw(
        _pack([w[n] for n in _SMALL]), _pack([mom[n] for n in _SMALL]), _pack([var[n] for n in _SMALL]),
        _pack([small_g[n] for n in _SMALL])[None], "adamw_small",
    )
    for dst, packed_out in zip((out_g, out_d, out_m, out_v), res):
        for n, t in zip(_SMALL, _unpack(packed_out, shapes)):
            dst[n] = t

    return (loss, dx[None], *[out_g[n] for n in _WEIGHTS], *[out_d[n] for n in _WEIGHTS],
            *[out_m[n] for n in _WEIGHTS], *[out_v[n] for n in _WEIGHTS])
```

```python
import functools
import math

import jax
import jax.numpy as jnp
from jax import lax
from jax.experimental import pallas as pl
from jax.experimental.pallas import tpu as pltpu
from jax.experimental.pallas import tpu_sc as plsc

F32 = jnp.float32
BF16 = jnp.bfloat16
HD = 128
EPS = 1e-6
CONV_K = 3
VMEM_LIMIT_BYTES = 48 << 20
NDEV = 8
MESH = pl.DeviceIdType.MESH

ADAM_LR = 0.001
ADAM_B1 = 0.9
ADAM_B2 = 0.999
ADAM_EPS = 1e-08
ADAM_WD = 0.01
ADAM_STEP = 10


def _cp(sem):
    return pltpu.CompilerParams(dimension_semantics=sem, vmem_limit_bytes=VMEM_LIMIT_BYTES)


def _pick(n, prefs):
    for p in prefs:
        if n % p == 0:
            return p
    return n


def _dot(a, b):
    return jnp.dot(a, b, preferred_element_type=F32)


def _dot_nt(a, b):
    return lax.dot_general(a, b, (((1,), (1,)), ((), ())), preferred_element_type=F32)


def _dot_tn(a, b):
    return lax.dot_general(a, b, (((0,), (0,)), ((), ())), preferred_element_type=F32)


def _split3(x):
    hi = x.astype(BF16)
    r = x - hi.astype(F32)
    mid = r.astype(BF16)
    lo = (r - mid.astype(F32)).astype(BF16)
    return hi, mid, lo


def _dot_ones_right(x, ones_bf16):
    hi, mid, lo = _split3(x)
    return _dot(hi, ones_bf16) + _dot(mid, ones_bf16) + _dot(lo, ones_bf16)


def _dot_ones_left(ones_bf16, x):
    hi, mid, lo = _split3(x)
    return _dot(ones_bf16, hi) + _dot(ones_bf16, mid) + _dot(ones_bf16, lo)


def _iota2(shape, axis):
    return lax.broadcasted_iota(jnp.int32, shape, axis)


def mm_nn(a, w2d, nb, name, out_dtype=BF16, res=None, tm=None, tn=None, tk=None):
    M, K = a.shape
    n = w2d.shape[1]
    assert w2d.shape[0] == nb * K
    tm = tm or _pick(M, (1024, 512, 256, 128))
    tn = tn or _pick(n, (1408, 1024, 768, 512, 256, 128))
    tk = tk or (K if K <= 2048 else _pick(K, (1408, 1024, 512, 256, 128)))
    nk, nt = K // tk, n // tn
    has_res = res is not None

    def body(*refs):
        if has_res:
            a_ref, w_ref, r_ref, o_ref = refs[:4]
        else:
            a_ref, w_ref, o_ref = refs[:3]
            r_ref = None
        part = _dot(a_ref[...], w_ref[...])

        def finish(acc):
            if r_ref is not None:
                acc = acc + r_ref[...].astype(F32)
            o_ref[...] = acc.astype(o_ref.dtype)

        if nk == 1:
            finish(part)
        else:
            acc_ref = refs[-1]
            k = pl.program_id(3)

            @pl.when(k == 0)
            def _():
                acc_ref[...] = part

            @pl.when(k > 0)
            def _():
                acc_ref[...] += part

            @pl.when(k == nk - 1)
            def _():
                finish(acc_ref[...])

    in_specs = [
        pl.BlockSpec((tm, tk), lambda i, j, t, k: (i, k)),
        pl.BlockSpec((tk, tn), lambda i, j, t, k: (j * nk + k, t)),
    ]
    args = [a, w2d]
    out_spec = pl.BlockSpec((tm, tn), lambda i, j, t, k: (i, j * nt + t))
    if has_res:
        in_specs.append(out_spec)
        args.append(res)
    return pl.pallas_call(
        body,
        grid=(M // tm, nb, nt, nk),
        in_specs=in_specs,
        out_specs=out_spec,
        out_shape=jax.ShapeDtypeStruct((M, nb * n), out_dtype),
        scratch_shapes=[pltpu.VMEM((tm, tn), F32)] if nk > 1 else [],
        compiler_params=_cp(("parallel", "parallel", "parallel", "arbitrary")),
        name=name,
    )(*args)


def mm_nt(dy2d, w2d, nb, M, K, name, out_dtype=BF16, res=None, dy_maps=None, tm=None, tko=None, tn=None):
    n = w2d.shape[1]
    assert w2d.shape[0] == nb * K
    tm = tm or _pick(M, (1024, 512, 256, 128))
    tko = tko or _pick(K, (1024, 512, 256, 128))
    tn = tn or _pick(n, (1408, 1024, 768, 512, 256, 128))
    nt, nko = n // tn, K // tko
    has_res = res is not None
    if dy_maps is None:
        dy_maps = [lambda i, j, t: (i, j * nt + t)]
    nd = len(dy_maps)
    td = tn // nd

    def body(*refs):
        d_refs, w_ref = refs[:nd], refs[nd]
        r_ref = refs[nd + 1] if has_res else None
        o_ref, acc_ref = refs[-2], refs[-1]
        j, t = pl.program_id(2), pl.program_id(3)
        d = d_refs[0][...] if nd == 1 else jnp.concatenate([r[...] for r in d_refs], axis=1)
        part = _dot_nt(d, w_ref[...])
        first = jnp.logical_and(j == 0, t == 0)
        last = jnp.logical_and(j == nb - 1, t == nt - 1)

        @pl.when(first)
        def _():
            acc_ref[...] = part

        @pl.when(jnp.logical_not(first))
        def _():
            acc_ref[...] += part

        @pl.when(last)
        def _():
            acc = acc_ref[...]
            if r_ref is not None:
                acc = acc + r_ref[...].astype(F32)
            o_ref[...] = acc.astype(o_ref.dtype)

    in_specs = [pl.BlockSpec((tm, td), functools.partial(lambda f, i, ko, j, t: f(i, j, t), f)) for f in dy_maps]
    in_specs.append(pl.BlockSpec((tko, tn), lambda i, ko, j, t: (j * nko + ko, t)))
    args = [dy2d] * nd + [w2d]
    out_spec = pl.BlockSpec((tm, tko), lambda i, ko, j, t: (i, ko))
    if has_res:
        in_specs.append(out_spec)
        args.append(res)
    return pl.pallas_call(
        body,
        grid=(M // tm, nko, nb, nt),
        in_specs=in_specs,
        out_specs=out_spec,
        out_shape=jax.ShapeDtypeStruct((M, K), out_dtype),
        scratch_shapes=[pltpu.VMEM((tm, tko), F32)],
        compiler_params=_cp(("parallel", "parallel", "arbitrary", "arbitrary")),
        name=name,
    )(*args)


def mm_tn(x, dy2d, nb, n, name, out_dtype=BF16, dy_maps=None, tko=None, tn=None):
    S, K = x.shape
    tko = tko or _pick(K, (512, 256, 128))
    tn = tn or _pick(n, (1408, 1024, 768, 512, 256, 128))
    nt, nko = n // tn, K // tko
    if dy_maps is None:
        dy_maps = [lambda j, t: (0, j * nt + t)]
    nd = len(dy_maps)
    td = tn // nd

    def body(*refs):
        x_ref, d_refs, o_ref = refs[0], refs[1 : 1 + nd], refs[-1]
        d = d_refs[0][...] if nd == 1 else jnp.concatenate([r[...] for r in d_refs], axis=1)
        o_ref[...] = _dot_tn(x_ref[...], d).astype(o_ref.dtype)

    in_specs = [pl.BlockSpec((S, tko), lambda ko, j, t: (0, ko))]
    in_specs += [pl.BlockSpec((S, td), functools.partial(lambda f, ko, j, t: f(j, t), f)) for f in dy_maps]
    return pl.pallas_call(
        body,
        grid=(nko, nb, nt),
        in_specs=in_specs,
        out_specs=pl.BlockSpec((tko, tn), lambda ko, j, t: (j * nko + ko, t)),
        out_shape=jax.ShapeDtypeStruct((nb * K, n), out_dtype),
        compiler_params=_cp(("parallel", "parallel", "parallel")),
        name=name,
    )(x, *([dy2d] * nd))


def rms_fwd(x, g, name):
    S, D = x.shape
    tm = _pick(S, (256, 128))

    def body(x_ref, g_ref, o_ref):
        xf = x_ref[...]
        r = lax.rsqrt(jnp.mean(xf * xf, axis=-1, keepdims=True) + EPS)
        o_ref[...] = (xf * r * g_ref[...]).astype(o_ref.dtype)

    return pl.pallas_call(
        body,
        grid=(S // tm,),
        in_specs=[pl.BlockSpec((tm, D), lambda i: (i, 0)), pl.BlockSpec((1, D), lambda i: (0, 0))],
        out_specs=pl.BlockSpec((tm, D), lambda i: (i, 0)),
        out_shape=jax.ShapeDtypeStruct((S, D), BF16),
        compiler_params=_cp(("parallel",)),
        name=name,
    )(x, g.reshape(1, D))


def rms_bwd(x, g, dh, dres, name):
    S, D = x.shape
    tm = _pick(S, (256, 128))

    def body(x_ref, g_ref, dh_ref, dr_ref, dx_ref, dxb_ref, dg_ref):
        i = pl.program_id(0)
        xf = x_ref[...]
        dh = dh_ref[...].astype(F32)
        r = lax.rsqrt(jnp.mean(xf * xf, axis=-1, keepdims=True) + EPS)
        gy = dh * g_ref[...]
        proj = jnp.mean(gy * xf, axis=-1, keepdims=True)
        dx = dr_ref[...] + r * gy - xf * (r * r * r * proj)
        dx_ref[...] = dx
        dxb_ref[...] = dx.astype(BF16)
        dg = jnp.sum(dh * (xf * r), axis=0, keepdims=True)

        @pl.when(i == 0)
        def _():
            dg_ref[...] = dg

        @pl.when(i > 0)
        def _():
            dg_ref[...] += dg

    row = pl.BlockSpec((tm, D), lambda i: (i, 0))
    vec = pl.BlockSpec((1, D), lambda i: (0, 0))
    return pl.pallas_call(
        body,
        grid=(S // tm,),
        in_specs=[row, vec, row, row],
        out_specs=[row, row, vec],
        out_shape=[jax.ShapeDtypeStruct((S, D), F32), jax.ShapeDtypeStruct((S, D), BF16), jax.ShapeDtypeStruct((1, D), F32)],
        compiler_params=_cp(("arbitrary",)),
        name=name,
    )(x, g.reshape(1, D), dh, dres)


def loss_head(x, g, target, name):
    S, D = x.shape
    tm = _pick(S, (256, 128))

    def body(x_ref, g_ref, t_ref, dx_ref, dxb_ref, dg_ref, loss_ref):
        i = pl.program_id(0)
        xf = x_ref[...]
        gg = g_ref[...]
        r = lax.rsqrt(jnp.mean(xf * xf, axis=-1, keepdims=True) + EPS)
        xh = xf * r
        err = xh * gg - t_ref[...]
        part = (0.5 / D) * jnp.sum(err * err)
        dy = err * (1.0 / D)
        gy = dy * gg
        proj = jnp.mean(gy * xf, axis=-1, keepdims=True)
        dx = r * gy - xf * (r * r * r * proj)
        dx_ref[...] = dx
        dxb_ref[...] = dx.astype(BF16)
        dg = jnp.sum(dy * xh, axis=0, keepdims=True)
        lossb = jnp.full(loss_ref.shape, part, F32)

        @pl.when(i == 0)
        def _():
            dg_ref[...] = dg
            loss_ref[...] = lossb

        @pl.when(i > 0)
        def _():
            dg_ref[...] += dg
            loss_ref[...] += lossb

    row = pl.BlockSpec((tm, D), lambda i: (i, 0))
    vec = pl.BlockSpec((1, D), lambda i: (0, 0))
    return pl.pallas_call(
        body,
        grid=(S // tm,),
        in_specs=[row, vec, row],
        out_specs=[row, row, vec, pl.BlockSpec((8, 128), lambda i: (0, 0))],
        out_shape=[
            jax.ShapeDtypeStruct((S, D), F32),
            jax.ShapeDtypeStruct((S, D), BF16),
            jax.ShapeDtypeStruct((1, D), F32),
            jax.ShapeDtypeStruct((8, 128), F32),
        ],
        compiler_params=_cp(("arbitrary",)),
        name=name,
    )(x, g.reshape(1, D), target)


def _shift_down(s, k):
    if k == 0:
        return s
    return jnp.where(_iota2(s.shape, 0) >= k, pltpu.roll(s, k, axis=0), 0.0)


def _shift_up(s, k):
    if k == 0:
        return s
    n = s.shape[0]
    return jnp.where(_iota2(s.shape, 0) < n - k, pltpu.roll(s, n - k, axis=0), 0.0)


def _conv(s, w):
    return w[0:1] * _shift_down(s, 2) + w[1:2] * _shift_down(s, 1) + w[2:3] * s


def _conv_t(d, w):
    return w[2:3] * d + w[1:2] * _shift_up(d, 1) + w[0:1] * _shift_up(d, 2)


def _conv_dw(d, s):
    return [jnp.sum(d * _shift_down(s, CONV_K - 1 - k), axis=0, keepdims=True) for k in range(CONV_K)]


def sc_fwd(p, convw, cat, W, name):
    S = p.shape[0]
    tc = _pick(W, (256, 128))
    nc = W // tc

    def body(gb_ref, gc_ref, hi_ref, w_ref, cat_ref, o_ref):
        s = gc_ref[...].astype(F32) * hi_ref[...].astype(F32)
        o_ref[...] = (gb_ref[...].astype(F32) * _conv(s, w_ref[...])).astype(o_ref.dtype)

    col = lambda part: pl.BlockSpec((S, tc), lambda c: (0, part * nc + c))
    return pl.pallas_call(
        body,
        grid=(nc,),
        in_specs=[col(3), col(4), col(5), pl.BlockSpec((CONV_K, tc), lambda c: (0, c)), pl.BlockSpec(memory_space=pl.ANY)],
        out_specs=col(1),
        out_shape=jax.ShapeDtypeStruct(cat.shape, cat.dtype),
        input_output_aliases={4: 0},
        compiler_params=_cp(("parallel",)),
        name=name,
    )(p, p, p, convw, cat)


def sc_bwd(p, convw, dcat, dp, W, name):
    S = p.shape[0]
    tc = _pick(W, (256, 128))
    nc = W // tc

    def body(gb_ref, gc_ref, hi_ref, w_ref, do_ref, dp_in_ref, dp_ref, dw_ref):
        gb = gb_ref[...].astype(F32)
        gc = gc_ref[...].astype(F32)
        hi = hi_ref[...].astype(F32)
        w = w_ref[...]
        do = do_ref[...].astype(F32)
        s = gc * hi
        dcs = do * gb
        ds = _conv_t(dcs, w)
        dp_ref[0] = (do * _conv(s, w)).astype(dp_ref.dtype)
        dp_ref[1] = (ds * hi).astype(dp_ref.dtype)
        dp_ref[2] = (ds * gc).astype(dp_ref.dtype)
        for k, row in enumerate(_conv_dw(dcs, s)):
            dw_ref[k : k + 1, :] = row

    col = lambda part: pl.BlockSpec((S, tc), lambda c: (0, part * nc + c))
    return pl.pallas_call(
        body,
        grid=(nc,),
        in_specs=[
            col(3), col(4), col(5),
            pl.BlockSpec((CONV_K, tc), lambda c: (0, c)),
            pl.BlockSpec((S, tc), lambda c: (0, nc + c)),
            pl.BlockSpec(memory_space=pl.ANY),
        ],
        out_specs=[pl.BlockSpec((3, S, tc), lambda c: (1, 0, c)), pl.BlockSpec((CONV_K, tc), lambda c: (0, c))],
        out_shape=[jax.ShapeDtypeStruct(dp.shape, dp.dtype), jax.ShapeDtypeStruct((CONV_K, W), F32)],
        input_output_aliases={5: 0},
        compiler_params=_cp(("parallel",)),
        name=name,
    )(p, p, p, convw, dcat, dp)


def _silu_parts(a):
    sig = 1.0 / (1.0 + jnp.exp(-a))
    return a * sig, sig


def ffn_act_fwd(u, convw, F, name):
    S = u.shape[0]
    tc = _pick(F, (256, 128))
    nc = F // tc

    def body(ug_ref, uu_ref, wg_ref, wu_ref, o_ref):
        ag = _conv(ug_ref[...].astype(F32), wg_ref[...])
        au = _conv(uu_ref[...].astype(F32), wu_ref[...])
        o_ref[...] = (_silu_parts(ag)[0] * au).astype(o_ref.dtype)

    col = lambda half: pl.BlockSpec((S, tc), lambda c: (0, half * nc + c))
    wcol = lambda half: pl.BlockSpec((CONV_K, tc), lambda c: (0, half * nc + c))
    return pl.pallas_call(
        body,
        grid=(nc,),
        in_specs=[col(0), col(1), wcol(0), wcol(1)],
        out_specs=pl.BlockSpec((S, tc), lambda c: (0, c)),
        out_shape=jax.ShapeDtypeStruct((S, F), BF16),
        compiler_params=_cp(("parallel",)),
        name=name,
    )(u, u, convw, convw)


def ffn_act_bwd(u, convw, dact, F, name):
    S = u.shape[0]
    tc = _pick(F, (256, 128))
    nc = F // tc

    def body(ug_ref, uu_ref, wg_ref, wu_ref, da_ref, du_ref, dw_ref):
        ug = ug_ref[...].astype(F32)
        uu = uu_ref[...].astype(F32)
        wg = wg_ref[...]
        wu = wu_ref[...]
        da = da_ref[...].astype(F32)
        ag = _conv(ug, wg)
        au = _conv(uu, wu)
        sl, sig = _silu_parts(ag)
        dag = da * au * (sig * (1.0 + ag * (1.0 - sig)))
        dau = da * sl
        du_ref[0] = _conv_t(dag, wg).astype(du_ref.dtype)
        du_ref[1] = _conv_t(dau, wu).astype(du_ref.dtype)
        for k, (rg, ru) in enumerate(zip(_conv_dw(dag, ug), _conv_dw(dau, uu))):
            dw_ref[0, k : k + 1, :] = rg
            dw_ref[1, k : k + 1, :] = ru

    col = lambda half: pl.BlockSpec((S, tc), lambda c: (0, half * nc + c))
    wcol = lambda half: pl.BlockSpec((CONV_K, tc), lambda c: (0, half * nc + c))
    return pl.pallas_call(
        body,
        grid=(nc,),
        in_specs=[col(0), col(1), wcol(0), wcol(1), pl.BlockSpec((S, tc), lambda c: (0, c))],
        out_specs=[pl.BlockSpec((2, S, tc), lambda c: (0, 0, c)), pl.BlockSpec((2, CONV_K, tc), lambda c: (0, 0, c))],
        out_shape=[jax.ShapeDtypeStruct((2, S, F), BF16), jax.ShapeDtypeStruct((2, CONV_K, F), F32)],
        compiler_params=_cp(("parallel",)),
        name=name,
    )(u, u, convw, convw, dact)


def _softplus(z):
    return jnp.maximum(z, 0.0) + jnp.log(1.0 + jnp.exp(-jnp.abs(z)))


def _key_strip(S):
    return _pick(S, (512, 256, 128))


def _split2(x):
    hi = x.astype(BF16)
    return hi, (x - hi.astype(F32)).astype(BF16)


def _block_sums(x, ones_bf16):
    hi, lo = _split2(x)
    return [
        _dot(hi[:, b * HD : (b + 1) * HD], ones_bf16) + _dot(lo[:, b * HD : (b + 1) * HD], ones_bf16)
        for b in range(x.shape[1] // HD)
    ]


def _strip_mask(shape, i, off, strict):
    cols, rows = _iota2(shape, 1) + off, _iota2(shape, 0) + i * HD
    return cols < rows if strict else cols <= rows


def _sb_strip(q, ks, i, off, run, su):
    z = _dot_nt(q, ks) * (HD ** -0.5)
    mask = _strip_mask(z.shape, i, off, True)
    sp = _softplus(z)
    l = jnp.where(mask, -sp, 0.0)
    within = _block_sums(l, su)
    later = [None] * len(within)
    for b in reversed(range(len(within))):
        later[b] = within[b] + run
        run = run + jnp.sum(l[:, b * HD : (b + 1) * HD], axis=1, keepdims=True)
    a = jnp.where(mask, jnp.exp(z - sp + jnp.concatenate(later, axis=1)), 0.0)
    return z, mask, a, run


def sb_fwd(p, W, name):
    S = p.shape[0]
    nh, nq = W // HD, S // HD
    TK = _key_strip(S)

    def body(q_ref, k_ref, v_ref, o_ref):
        i = pl.program_id(1)
        q = q_ref[...]
        su = (_iota2((HD, HD), 0) > _iota2((HD, HD), 1)).astype(BF16)
        last = (i * HD) // TK

        def step(gg, carry):
            acc, run = carry
            off = pl.multiple_of((last - gg) * TK, TK)
            _, _, a, run = _sb_strip(q, k_ref[pl.ds(off, TK), :], i, off, run, su)
            return acc + _dot(a.astype(BF16), v_ref[pl.ds(off, TK), :]), run

        acc, _ = lax.fori_loop(0, last + 1, step, (jnp.zeros((HD, HD), F32), jnp.zeros((HD, 1), F32)))
        o_ref[...] = acc.astype(o_ref.dtype)

    return pl.pallas_call(
        body,
        grid=(nh, nq),
        in_specs=[
            pl.BlockSpec((HD, HD), lambda h, i: (i, h)),
            pl.BlockSpec((S, HD), lambda h, i: (0, nh + h)),
            pl.BlockSpec((S, HD), lambda h, i: (0, 2 * nh + h)),
        ],
        out_specs=pl.BlockSpec((HD, HD), lambda h, i: (i, h)),
        out_shape=jax.ShapeDtypeStruct((S, 2 * W), BF16),
        compiler_params=_cp(("parallel", "arbitrary")),
        name=name,
    )(p, p, p)


def sb_bwd(p, dcat, W, name):
    S = p.shape[0]
    nh, nq = W // HD, S // HD
    TK = _key_strip(S)
    scale = HD ** -0.5

    def body(q_ref, k_ref, v_ref, do_ref, dp_ref, dk_acc, dv_acc, e_scr, z_scr):
        i = pl.program_id(1)
        q = q_ref[...]
        do = do_ref[...]
        su = (_iota2((HD, HD), 0) > _iota2((HD, HD), 1)).astype(BF16)
        sl = (_iota2((HD, HD), 0) < _iota2((HD, HD), 1)).astype(BF16)
        last = (i * HD) // TK

        @pl.when(i == 0)
        def _():
            dk_acc[...] = jnp.zeros_like(dk_acc)
            dv_acc[...] = jnp.zeros_like(dv_acc)

        def pass_a(gg, run):
            g = last - gg
            off = pl.multiple_of(g * TK, TK)
            z, _, a, run = _sb_strip(q, k_ref[pl.ds(off, TK), :], i, off, run, su)
            e_scr[g] = a * _dot_nt(do, v_ref[pl.ds(off, TK), :])
            z_scr[g] = z
            dv_acc[pl.ds(off, TK), :] += _dot_tn(a.astype(BF16), do)
            return run

        lax.fori_loop(0, last + 1, pass_a, jnp.zeros((HD, 1), F32))

        def pass_b(g, carry):
            dq, run_e = carry
            off = pl.multiple_of(g * TK, TK)
            e = e_scr[g]
            z = z_scr[g]
            mask = _strip_mask(z.shape, i, off, True)
            within = _block_sums(e, sl)
            before = []
            for b in range(len(within)):
                before.append(within[b] + run_e)
                run_e = run_e + jnp.sum(e[:, b * HD : (b + 1) * HD], axis=1, keepdims=True)
            sig = 1.0 / (1.0 + jnp.exp(-z))
            dz = jnp.where(mask, e * (1.0 - sig) - jnp.concatenate(before, axis=1) * sig, 0.0)
            dz = (dz * scale).astype(BF16)
            dq = dq + _dot(dz, k_ref[pl.ds(off, TK), :])
            dk_acc[pl.ds(off, TK), :] += _dot_tn(dz, q)
            return dq, run_e

        dq, _ = lax.fori_loop(0, last + 1, pass_b, (jnp.zeros((HD, HD), F32), jnp.zeros((HD, 1), F32)))
        dp_ref[0, pl.ds(pl.multiple_of(i * HD, HD), HD), :] = dq.astype(dp_ref.dtype)

        @pl.when(i == nq - 1)
        def _():
            dp_ref[1] = dk_acc[...].astype(dp_ref.dtype)
            dp_ref[2] = dv_acc[...].astype(dp_ref.dtype)

    return pl.pallas_call(
        body,
        grid=(nh, nq),
        in_specs=[
            pl.BlockSpec((HD, HD), lambda h, i: (i, h)),
            pl.BlockSpec((S, HD), lambda h, i: (0, nh + h)),
            pl.BlockSpec((S, HD), lambda h, i: (0, 2 * nh + h)),
            pl.BlockSpec((HD, HD), lambda h, i: (i, h)),
        ],
        out_specs=pl.BlockSpec((3, S, HD), lambda h, i: (0, 0, h)),
        out_shape=jax.ShapeDtypeStruct((6, S, W), BF16),
        scratch_shapes=[
            pltpu.VMEM((S, HD), F32),
            pltpu.VMEM((S, HD), F32),
            pltpu.VMEM((S // TK, HD, TK), F32),
            pltpu.VMEM((S // TK, HD, TK), F32),
        ],
        compiler_params=_cp(("parallel", "arbitrary")),
        name=name,
    )(p, p, p, dcat)


def fox_gate_fwd(f, b, name):
    S = f.shape[0]
    nq = S // HD

    def body(f_ref, b_ref, c_ref, run):
        i = pl.program_id(0)

        @pl.when(i == 0)
        def _():
            run[...] = jnp.zeros_like(run)

        lf = -_softplus(-(f_ref[...] + b_ref[...]))
        tri = (_iota2((HD, HD), 0) >= _iota2((HD, HD), 1)).astype(BF16)
        c_ref[...] = _dot_ones_left(tri, lf) + run[...]
        run[...] += jnp.sum(lf, axis=0, keepdims=True)

    return pl.pallas_call(
        body,
        grid=(nq,),
        in_specs=[pl.BlockSpec((HD, 128), lambda i: (i, 0)), pl.BlockSpec((1, 128), lambda i: (0, 0))],
        out_specs=pl.BlockSpec((HD, 128), lambda i: (i, 0)),
        out_shape=jax.ShapeDtypeStruct((S, 128), F32),
        scratch_shapes=[pltpu.VMEM((1, 128), F32)],
        compiler_params=_cp(("arbitrary",)),
        name=name,
    )(f, b)


def fox_gate_bwd(f, b, dc, name):
    S = f.shape[0]
    nq = S // HD

    def body(f_ref, b_ref, dc_ref, df_ref, db_ref, run):
        i = pl.program_id(0)

        @pl.when(i == 0)
        def _():
            run[...] = jnp.zeros_like(run)

        dc = dc_ref[...]
        tri = (_iota2((HD, HD), 0) <= _iota2((HD, HD), 1)).astype(BF16)
        dlf = _dot_ones_left(tri, dc) + run[...]
        run[...] += jnp.sum(dc, axis=0, keepdims=True)
        x = f_ref[...] + b_ref[...]
        df = dlf * (1.0 / (1.0 + jnp.exp(x)))
        df_ref[...] = df
        db = jnp.sum(df, axis=0, keepdims=True)

        @pl.when(i == 0)
        def _():
            db_ref[...] = db

        @pl.when(i > 0)
        def _():
            db_ref[...] += db

    rev = pl.BlockSpec((HD, 128), lambda i: (nq - 1 - i, 0))
    vec = pl.BlockSpec((1, 128), lambda i: (0, 0))
    return pl.pallas_call(
        body,
        grid=(nq,),
        in_specs=[rev, vec, rev],
        out_specs=[rev, vec],
        out_shape=[jax.ShapeDtypeStruct((S, 128), F32), jax.ShapeDtypeStruct((1, 128), F32)],
        scratch_shapes=[pltpu.VMEM((1, 128), F32)],
        compiler_params=_cp(("arbitrary",)),
        name=name,
    )(f, b, dc)


def _fox_logits(q, ks, ct, cs, i, off):
    s = _dot_nt(q, ks) * (HD ** -0.5) + (ct - cs)
    mask = _strip_mask(s.shape, i, off, False)
    return jnp.where(mask, s, -1e30), mask


def fox_fwd(p, ccol, crow, cat, W, name):
    S = p.shape[0]
    nh, nq = W // HD, S // HD
    TK = _key_strip(S)

    def body(q_ref, k_ref, v_ref, cc_ref, cr_ref, cat_ref, o_ref, lse_ref):
        i = pl.program_id(1)
        q = q_ref[...]
        ct = cc_ref[0]

        def step(g, carry):
            m, l, acc = carry
            off = pl.multiple_of(g * TK, TK)
            s, _ = _fox_logits(q, k_ref[pl.ds(off, TK), :], ct, cr_ref[0, pl.ds(g, 1), :], i, off)
            m_new = jnp.maximum(m, jnp.max(s, axis=1, keepdims=True))
            alpha = jnp.exp(m - m_new)
            pr = jnp.exp(s - m_new)
            l = alpha * l + jnp.sum(pr, axis=1, keepdims=True)
            acc = alpha * acc + _dot(pr.astype(BF16), v_ref[pl.ds(off, TK), :])
            return m_new, l, acc

        init = (jnp.full((HD, 1), -1e30, F32), jnp.zeros((HD, 1), F32), jnp.zeros((HD, HD), F32))
        m, l, acc = lax.fori_loop(0, (i * HD) // TK + 1, step, init)
        o_ref[...] = (acc / l).astype(o_ref.dtype)
        lse_ref[0] = m + jnp.log(l)

    return pl.pallas_call(
        body,
        grid=(nh, nq),
        in_specs=[
            pl.BlockSpec((HD, HD), lambda h, i: (i, 2 * nh + h)),
            pl.BlockSpec((S, HD), lambda h, i: (0, 3 * nh + h)),
            pl.BlockSpec((S, HD), lambda h, i: (0, 4 * nh + h)),
            pl.BlockSpec((1, HD, 1), lambda h, i: (h, i, 0)),
            pl.BlockSpec((1, S // TK, TK), lambda h, i: (h, 0, 0)),
            pl.BlockSpec(memory_space=pl.ANY),
        ],
        out_specs=[pl.BlockSpec((HD, HD), lambda h, i: (i, nh + h)), pl.BlockSpec((1, HD, 1), lambda h, i: (h, i, 0))],
        out_shape=[jax.ShapeDtypeStruct(cat.shape, cat.dtype), jax.ShapeDtypeStruct((nh, S, 1), F32)],
        input_output_aliases={5: 0},
        compiler_params=_cp(("parallel", "arbitrary")),
        name=name,
    )(p, p, p, ccol, crow, cat)


def fox_bwd(p, ccol, crow, cat, lse, dcat, dp, W, name):
    S = p.shape[0]
    nh, nq = W // HD, S // HD
    TK = _key_strip(S)
    scale = HD ** -0.5

    def body(q_ref, k_ref, v_ref, cc_ref, cr_ref, o_ref, lse_ref, do_ref, dp_in_ref, dp_ref, dcs_ref, dct_ref, dk_acc, dv_acc):
        i = pl.program_id(1)
        q = q_ref[...]
        do = do_ref[...]
        ct = cc_ref[0]
        lse_i = lse_ref[0]
        delta = jnp.sum(do.astype(F32) * o_ref[...].astype(F32), axis=1, keepdims=True)

        @pl.when(i == 0)
        def _():
            dk_acc[...] = jnp.zeros_like(dk_acc)
            dv_acc[...] = jnp.zeros_like(dv_acc)
            dcs_ref[...] = jnp.zeros_like(dcs_ref)

        def step(g, carry):
            dq, dct = carry
            off = pl.multiple_of(g * TK, TK)
            ks = k_ref[pl.ds(off, TK), :]
            s, mask = _fox_logits(q, ks, ct, cr_ref[0, pl.ds(g, 1), :], i, off)
            pr = jnp.where(mask, jnp.exp(s - lse_i), 0.0)
            ds = pr * (_dot_nt(do, v_ref[pl.ds(off, TK), :]) - delta)
            dv_acc[pl.ds(off, TK), :] += _dot_tn(pr.astype(BF16), do)
            dsb = (ds * scale).astype(BF16)
            dk_acc[pl.ds(off, TK), :] += _dot_tn(dsb, q)
            dcs_ref[0, pl.ds(g, 1), :] += jnp.sum(ds, axis=0, keepdims=True)
            return dq + _dot(dsb, ks), dct + jnp.sum(ds, axis=1, keepdims=True)

        dq, dct = lax.fori_loop(0, (i * HD) // TK + 1, step, (jnp.zeros((HD, HD), F32), jnp.zeros((HD, 1), F32)))
        dp_ref[0, pl.ds(pl.multiple_of(i * HD, HD), HD), :] = dq.astype(dp_ref.dtype)
        dct_ref[0] = dct

        @pl.when(i == nq - 1)
        def _():
            dp_ref[1] = dk_acc[...].astype(dp_ref.dtype)
            dp_ref[2] = dv_acc[...].astype(dp_ref.dtype)

    return pl.pallas_call(
        body,
        grid=(nh, nq),
        in_specs=[
            pl.BlockSpec((HD, HD), lambda h, i: (i, 2 * nh + h)),
            pl.BlockSpec((S, HD), lambda h, i: (0, 3 * nh + h)),
            pl.BlockSpec((S, HD), lambda h, i: (0, 4 * nh + h)),
            pl.BlockSpec((1, HD, 1), lambda h, i: (h, i, 0)),
            pl.BlockSpec((1, S // TK, TK), lambda h, i: (h, 0, 0)),
            pl.BlockSpec((HD, HD), lambda h, i: (i, nh + h)),
            pl.BlockSpec((1, HD, 1), lambda h, i: (h, i, 0)),
            pl.BlockSpec((HD, HD), lambda h, i: (i, nh + h)),
            pl.BlockSpec(memory_space=pl.ANY),
        ],
        out_specs=[
            pl.BlockSpec((3, S, HD), lambda h, i: (1, 0, h)),
            pl.BlockSpec((1, S // TK, TK), lambda h, i: (h, 0, 0)),
            pl.BlockSpec((1, HD, 1), lambda h, i: (h, i, 0)),
        ],
        out_shape=[
            jax.ShapeDtypeStruct(dp.shape, dp.dtype),
            jax.ShapeDtypeStruct((nh, S // TK, TK), F32),
            jax.ShapeDtypeStruct((nh, S, 1), F32),
        ],
        input_output_aliases={8: 0},
        scratch_shapes=[pltpu.VMEM((S, HD), F32), pltpu.VMEM((S, HD), F32)],
        compiler_params=_cp(("parallel", "arbitrary")),
        name=name,
    )(p, p, p, ccol, crow, cat, lse, dcat, dp)


_GELU_K = math.sqrt(2.0 / math.pi)
_GELU_C = 0.044715


def _gelu(x):
    return 0.5 * x * (1.0 + jnp.tanh(_GELU_K * (x + _GELU_C * x * x * x)))


def _gelu_grad(x):
    t = jnp.tanh(_GELU_K * (x + _GELU_C * x * x * x))
    return 0.5 * (1.0 + t) + 0.5 * x * (1.0 - t * t) * (_GELU_K * (1.0 + 3.0 * _GELU_C * x * x))


def _layernorm_parts(gv):
    xc = gv - jnp.mean(gv, axis=-1, keepdims=True)
    r = lax.rsqrt(jnp.mean(xc * xc, axis=-1, keepdims=True) + EPS)
    return xc * r, r


def sg_fwd(p, sg_w, sg_bt, sg_g, W, name):
    S = p.shape[0]
    G, nq = W // HD, S // HD

    def body(u_ref, v_ref, w_ref, bt_ref, g_ref, o_ref):
        xh, _ = _layernorm_parts(_gelu(v_ref[...].astype(F32)))
        vn = (xh * g_ref[...]).astype(BF16)
        tri = _iota2((HD, HD), 0) >= _iota2((HD, HD), 1)
        for gi in range(G):
            cols = slice(gi * HD, (gi + 1) * HD)
            wt = jnp.where(tri, w_ref[gi], 0.0).astype(BF16)
            mixed = _dot(wt, vn[:, cols]) + bt_ref[:, gi : gi + 1]
            o_ref[:, cols] = (_gelu(u_ref[:, cols].astype(F32)) * mixed).astype(o_ref.dtype)

    return pl.pallas_call(
        body,
        grid=(nq,),
        in_specs=[
            pl.BlockSpec((HD, W), lambda i: (i, 0)),
            pl.BlockSpec((HD, W), lambda i: (i, 1)),
            pl.BlockSpec((G, HD, HD), lambda i: (0, 0, 0)),
            pl.BlockSpec((HD, G), lambda i: (0, 0)),
            pl.BlockSpec((1, W), lambda i: (0, 0)),
        ],
        out_specs=pl.BlockSpec((HD, W), lambda i: (i, 0)),
        out_shape=jax.ShapeDtypeStruct((S, 2 * W), BF16),
        compiler_params=_cp(("parallel",)),
        name=name,
    )(p, p, sg_w, sg_bt, sg_g.reshape(1, W))


def sg_bwd(p, sg_w, sg_bt, sg_g, dcat, W, name):
    S = p.shape[0]
    G, nq = W // HD, S // HD

    def body(u_ref, v_ref, w_ref, bt_ref, g_ref, do_ref, dp_ref, dw_ref, dbt_ref, dg_ref, dvn_scr):
        i = pl.program_id(0)

        @pl.when(i == 0)
        def _():
            dw_ref[...] = jnp.zeros_like(dw_ref)
            dbt_ref[...] = jnp.zeros_like(dbt_ref)
            dg_ref[...] = jnp.zeros_like(dg_ref)

        v = v_ref[...].astype(F32)
        xh, r = _layernorm_parts(_gelu(v))
        gg = g_ref[...]
        vn = (xh * gg).astype(BF16)
        tri = _iota2((HD, HD), 0) >= _iota2((HD, HD), 1)
        for gi in range(G):
            cols = slice(gi * HD, (gi + 1) * HD)
            wt = jnp.where(tri, w_ref[gi], 0.0).astype(BF16)
            mixed = _dot(wt, vn[:, cols]) + bt_ref[:, gi : gi + 1]
            u = u_ref[:, cols].astype(F32)
            do = do_ref[:, cols].astype(F32)
            dp_ref[0, :, cols] = (do * mixed * _gelu_grad(u)).astype(dp_ref.dtype)
            dmix = do * _gelu(u)
            dmb = dmix.astype(BF16)
            dw_ref[gi] += jnp.where(tri, _dot_nt(dmb, vn[:, cols]), 0.0)
            dbt_ref[:, gi : gi + 1] += jnp.sum(dmix, axis=1, keepdims=True)
            dvn_scr[:, cols] = _dot_tn(wt, dmb)
        dvn = dvn_scr[...]
        dg_ref[...] += jnp.sum(dvn * xh, axis=0, keepdims=True)
        dxh = dvn * gg
        dgv = r * (dxh - jnp.mean(dxh, axis=-1, keepdims=True) - xh * jnp.mean(dxh * xh, axis=-1, keepdims=True))
        dp_ref[1] = (dgv * _gelu_grad(v)).astype(dp_ref.dtype)

    return pl.pallas_call(
        body,
        grid=(nq,),
        in_specs=[
            pl.BlockSpec((HD, W), lambda i: (i, 0)),
            pl.BlockSpec((HD, W), lambda i: (i, 1)),
            pl.BlockSpec((G, HD, HD), lambda i: (0, 0, 0)),
            pl.BlockSpec((HD, G), lambda i: (0, 0)),
            pl.BlockSpec((1, W), lambda i: (0, 0)),
            pl.BlockSpec((HD, W), lambda i: (i, 0)),
        ],
        out_specs=[
            pl.BlockSpec((2, HD, W), lambda i: (0, i, 0)),
            pl.BlockSpec((G, HD, HD), lambda i: (0, 0, 0)),
            pl.BlockSpec((HD, G), lambda i: (0, 0)),
            pl.BlockSpec((1, W), lambda i: (0, 0)),
        ],
        out_shape=[
            jax.ShapeDtypeStruct((6, S, W), BF16),
            jax.ShapeDtypeStruct((G, HD, HD), F32),
            jax.ShapeDtypeStruct((HD, G), F32),
            jax.ShapeDtypeStruct((1, W), F32),
        ],
        scratch_shapes=[pltpu.VMEM((HD, W), F32)],
        compiler_params=_cp(("arbitrary",)),
        name=name,
    )(p, p, sg_w, sg_bt, sg_g.reshape(1, W), dcat)


def local_step(x, target, wts, on_grad):
    S, D = x.shape
    W = D // 2
    nb = wts["nb"]
    F = wts["l0_ffn_down"].shape[0]
    g = {}

    def ffn_fwd(xin, l):
        h = rms_fwd(xin, wts[f"{l}_ffn_norm_g"], f"{l}_ffn_rms")
        u = mm_nn(h, wts[f"{l}_ffn_up"], nb, f"{l}_ffn_up_mm")
        act = ffn_act_fwd(u, wts[f"{l}_ffn_conv_w"], F, f"{l}_ffn_act")
        xout = mm_nn(act, wts[f"{l}_ffn_down"], 1, f"{l}_ffn_down_mm", out_dtype=F32, res=xin)
        return xout, (xin, h, u, act)

    def ffn_bwd(dxout, dxoutb, saved, l):
        xin, h, u, act = saved
        dact = mm_nt(dxoutb, wts[f"{l}_ffn_down"], 1, S, F, f"{l}_ffn_down_dx")
        on_grad(f"{l}_ffn_down", mm_tn(act, dxoutb, 1, D, f"{l}_ffn_down_dw"))
        du, dcw = ffn_act_bwd(u, wts[f"{l}_ffn_conv_w"], dact, F, f"{l}_ffn_act_bwd")
        g[f"{l}_ffn_conv_w"] = jnp.concatenate([dcw[0], dcw[1]], axis=1)
        du2 = du.reshape(2 * S, F)
        n = wts[f"{l}_ffn_up"].shape[1]
        tn = _pick(n, (1408, 1024, 768, 512, 256, 128))
        per_half = F // tn
        nt = n // tn

        def up_block(i, j, t):
            vb = j * nt + t
            return vb // per_half, vb % per_half

        tm = _pick(S, (1024, 512, 256, 128))

        def nt_map(i, j, t):
            half, cb = up_block(i, j, t)
            return (half * (S // tm) + i, cb)

        def tn_map(j, t):
            half, cb = up_block(0, j, t)
            return (half, cb)

        on_grad(f"{l}_ffn_up", mm_tn(h, du2, nb, n, f"{l}_ffn_up_dw", dy_maps=[tn_map], tn=tn))
        dh = mm_nt(du2, wts[f"{l}_ffn_up"], nb, S, D, f"{l}_ffn_up_dx", dy_maps=[nt_map], tm=tm, tn=tn)
        dxin, dxinb, dg = rms_bwd(xin, wts[f"{l}_ffn_norm_g"], dh, dxout, f"{l}_ffn_rms_bwd")
        g[f"{l}_ffn_norm_g"] = dg
        return dxin, dxinb

    h0 = rms_fwd(x, wts["l0_mix_norm_g"], "l0_mix_rms")
    p0 = mm_nn(h0, wts["l0_w_in"], nb, "l0_w_in_mm")
    cat0 = sb_fwd(p0, W, "l0_sb_fwd")
    cat0 = sc_fwd(p0, wts["l0_sc_conv_w"], cat0, W, "l0_sc_fwd")
    x1 = mm_nn(cat0, wts["l0_w_out"], 1, "l0_w_out_mm", out_dtype=F32, res=x)
    x2, ffn0_saved = ffn_fwd(x1, "l0")

    nh = W // HD
    h2 = rms_fwd(x2, wts["l1_mix_norm_g"], "l1_mix_rms")
    p1 = mm_nn(h2, wts["l1_w_in_main"], 1, "l1_w_in_mm")
    f = mm_nn(h2, wts["l1_w_in_f"], 1, "l1_w_f_mm", out_dtype=F32)
    bf = jnp.zeros((1, 128), F32).at[0, :nh].set(wts["l1_fox_b_f"])
    c = fox_gate_fwd(f, bf, "l1_fox_gate")
    c_heads = c[:, :nh].T
    ccol = c_heads[:, :, None]
    crow = c_heads.reshape(nh, S // _key_strip(S), _key_strip(S))
    sg_bt = wts["l1_sg_b"].T
    cat1 = sg_fwd(p1, wts["l1_sg_w"], sg_bt, wts["l1_sg_norm_g"], W, "l1_sg_fwd")
    cat1, lse = fox_fwd(p1, ccol, crow, cat1, W, "l1_fox_fwd")
    x3 = mm_nn(cat1, wts["l1_w_out"], 1, "l1_w_out_mm", out_dtype=F32, res=x2)
    x4, ffn1_saved = ffn_fwd(x3, "l1")

    dx4, dx4b, dgf, loss = loss_head(x4, wts["final_norm_g"], target, "loss_head")
    g["final_norm_g"] = dgf

    dx3, dx3b = ffn_bwd(dx4, dx4b, ffn1_saved, "l1")
    on_grad("l1_w_out", mm_tn(cat1, dx3b, 1, D, "l1_w_out_dw"))
    dcat1 = mm_nt(dx3b, wts["l1_w_out"], 1, S, D, "l1_w_out_dx")
    dp1, dsgw, dsgbt, dsgg = sg_bwd(p1, wts["l1_sg_w"], sg_bt, wts["l1_sg_norm_g"], dcat1, W, "l1_sg_bwd")
    dp1, dcs, dct = fox_bwd(p1, ccol, crow, cat1, lse, dcat1, dp1, W, "l1_fox_bwd")
    g["l1_sg_w"], g["l1_sg_b"], g["l1_sg_norm_g"] = dsgw, dsgbt.T, dsgg
    dc = jnp.zeros((S, 128), F32).at[:, :nh].set((dct[:, :, 0] - dcs.reshape(nh, S)).T)
    df, dbf = fox_gate_bwd(f, bf, dc, "l1_fox_gate_bwd")
    g["l1_fox_b_f"] = dbf[0, :nh]
    dfb = df.astype(BF16)
    tn1 = _pick(W, (1024, 512, 256, 128))
    tm1 = _pick(S, (1024, 512, 256, 128))
    per_part = W // tn1
    part_of = lambda pt: pt + pt // 2 - pt // 4

    def nt_map1(i, j, t):
        return (part_of(t // per_part) * (S // tm1) + i, t % per_part)

    def tn_map1(j, t):
        return (part_of(t // per_part), t % per_part)

    dp1_2d = dp1.reshape(6 * S, W)
    dw_main = mm_tn(h2, dp1_2d, 1, 5 * W, "l1_w_in_dw", dy_maps=[tn_map1], tn=tn1)
    dw_f = mm_tn(h2, dfb, 1, 128, "l1_w_f_dw")
    on_grad("l1_w_in", jnp.concatenate([dw_main, dw_f[:, :nh]], axis=1))
    dh2 = mm_nt(dfb, wts["l1_w_in_f"], 1, S, D, "l1_w_f_dx", out_dtype=F32)
    dh2 = mm_nt(dp1_2d, wts["l1_w_in_main"], 1, S, D, "l1_w_in_dx", res=dh2, dy_maps=[nt_map1], tm=tm1, tn=tn1)
    dx2, dx2b, dg = rms_bwd(x2, wts["l1_mix_norm_g"], dh2, dx3, "l1_mix_rms_bwd")
    g["l1_mix_norm_g"] = dg

    dx1, dx1b = ffn_bwd(dx2, dx2b, ffn0_saved, "l0")
    on_grad("l0_w_out", mm_tn(cat0, dx1b, 1, D, "l0_w_out_dw"))
    dcat0 = mm_nt(dx1b, wts["l0_w_out"], 1, S, D, "l0_w_out_dx")
    dp0 = sb_bwd(p0, dcat0, W, "l0_sb_bwd")
    dp0, dscw = sc_bwd(p0, wts["l0_sc_conv_w"], dcat0, dp0, W, "l0_sc_bwd")
    g["l0_sc_conv_w"] = dscw
    n0 = wts["l0_w_in"].shape[1]
    td0 = math.gcd(n0, W)
    nd0 = n0 // td0
    tm0 = _pick(S, (1024, 512, 256, 128))
    per_part0 = W // td0

    def nt_maps0(k):
        def f(i, j, t):
            vb = j * nd0 + k
            return ((vb // per_part0) * (S // tm0) + i, vb % per_part0)
        return f

    def tn_maps0(k):
        def f(j, t):
            vb = j * nd0 + k
            return (vb // per_part0, vb % per_part0)
        return f

    dp0_2d = dp0.reshape(6 * S, W)
    on_grad("l0_w_in", mm_tn(h0, dp0_2d, nb, n0, "l0_w_in_dw", dy_maps=[tn_maps0(k) for k in range(nd0)], tn=n0))
    dh0 = mm_nt(dp0_2d, wts["l0_w_in"], nb, S, D, "l0_w_in_dx", dy_maps=[nt_maps0(k) for k in range(nd0)], tm=tm0, tn=n0)
    dx0, _, dg = rms_bwd(x, wts["l0_mix_norm_g"], dh0, dx1, "l0_mix_rms_bwd")
    g["l0_mix_norm_g"] = dg
    return loss, dx0, g


GATHER_ID, PAIR_ID, CHIPS_ID = 1, 2, 3


def _place():
    return lax.axis_index("x"), lax.axis_index("y"), lax.axis_index("c")


def _other_chips(x, y):
    return [(x, 1 - y), (1 - x, y), (1 - x, 1 - y)]


def _handshake(peers):
    barrier = pltpu.get_barrier_semaphore()
    for peer in peers:
        pl.semaphore_signal(barrier, inc=1, device_id=peer, device_id_type=MESH)
    pl.semaphore_wait(barrier, len(peers))


def _on_sequencer(body, out_type, scratch_types, collective_id, name):
    return pl.kernel(
        body,
        out_type=out_type,
        mesh=plsc.ScalarSubcoreMesh(axis_name="seq", num_cores=1),
        scratch_types=scratch_types,
        compiler_params=pltpu.CompilerParams(collective_id=collective_id),
        name=name,
    )


def all_gather(arrs, name):
    n = len(arrs)

    def body(*refs):
        xs, outs = refs[:n], refs[n : 2 * n]
        send_sems, recv_sems, local_sems = refs[2 * n :]
        x, y, c = _place()
        me, sibling = (x, y, c), (x, y, 1 - c)
        chips = _other_chips(x, y)
        _handshake([sibling] + [(*chip, c) for chip in chips])

        def copy(a, k, block, to, src=None):
            px, py, pc = block
            dst = outs[a].at[4 * px + 2 * py + pc]
            return pltpu.make_async_remote_copy(
                src_ref=dst if src is None else src, dst_ref=dst,
                send_sem=send_sems.at[7 * a + k], recv_sem=recv_sems.at[7 * a + k], device_id=to, device_id_type=MESH,
            )

        mine = [pltpu.make_async_copy(xs[a], outs[a].at[4 * x + 2 * y + c], local_sems.at[a]) for a in range(n)]
        for cp in mine:
            cp.start()
        first = []
        for a in range(n):
            first.append(copy(a, 0, me, sibling, src=xs[a]))
            first += [copy(a, 1 + j, me, (*chip, c), src=xs[a]) for j, chip in enumerate(chips)]
        for cp in first:
            cp.start()
        passed = []
        for a in range(n):
            for j, chip in enumerate(chips):
                copy(a, 1 + j, (*chip, c), me).wait_recv()
                cp = copy(a, 4 + j, (*chip, c), sibling)
                cp.start()
                passed.append(cp)
        for a in range(n):
            copy(a, 0, sibling, me).wait_recv()
            for j, chip in enumerate(chips):
                copy(a, 4 + j, (*chip, 1 - c), me).wait_recv()
        for cp in first + passed:
            cp.wait_send()
        for cp in mine:
            cp.wait()

    out_type = [jax.ShapeDtypeStruct((NDEV,) + a.shape, a.dtype) for a in arrs]
    sems = [pltpu.SemaphoreType.DMA((7 * n,)), pltpu.SemaphoreType.DMA((7 * n,)), pltpu.SemaphoreType.DMA((n,))]
    return _on_sequencer(body, out_type, sems, GATHER_ID, name)(*arrs)


def exchange_pair(arrs, name):
    n = len(arrs)

    def body(*refs):
        xs, lands = refs[:n], refs[n : 2 * n]
        send_sems, recv_sems = refs[2 * n :]
        x, y, c = _place()
        _handshake([(x, y, 1 - c)])
        copies = [
            pltpu.make_async_remote_copy(
                src_ref=xs[a].at[k, 1 - c], dst_ref=lands[a].at[k],
                send_sem=send_sems.at[4 * a + k], recv_sem=recv_sems.at[4 * a + k],
                device_id=(x, y, 1 - c), device_id_type=MESH,
            )
            for a in range(n)
            for k in range(4)
        ]
        for cp in copies:
            cp.start()
        for cp in copies:
            cp.wait()

    out_type = [jax.ShapeDtypeStruct((4,) + a.shape[2:], a.dtype) for a in arrs]
    sems = [pltpu.SemaphoreType.DMA((4 * n,)), pltpu.SemaphoreType.DMA((4 * n,))]
    return _on_sequencer(body, out_type, sems, PAIR_ID, name)(*arrs)


def exchange_chips(arrs, name):
    n = len(arrs)

    def body(*refs):
        xs, lands = refs[:n], refs[n : 2 * n]
        send_sems, recv_sems, local_sems = refs[2 * n :]
        x, y, c = _place()
        my_chip = 2 * x + y
        chips = _other_chips(x, y)
        _handshake([(*chip, c) for chip in chips])
        mine = [pltpu.make_async_copy(xs[a].at[my_chip], lands[a].at[my_chip], local_sems.at[a]) for a in range(n)]
        for cp in mine:
            cp.start()
        sends, recvs = [], []
        for a in range(n):
            for j, (px, py) in enumerate(chips):
                peer = 2 * px + py
                sends.append(pltpu.make_async_remote_copy(
                    src_ref=xs[a].at[peer], dst_ref=lands[a].at[my_chip],
                    send_sem=send_sems.at[3 * a + j], recv_sem=recv_sems.at[3 * a + j],
                    device_id=(px, py, c), device_id_type=MESH,
                ))
                recvs.append(pltpu.make_async_remote_copy(
                    src_ref=xs[a].at[peer], dst_ref=lands[a].at[peer],
                    send_sem=send_sems.at[3 * a + j], recv_sem=recv_sems.at[3 * a + j],
                    device_id=(px, py, c), device_id_type=MESH,
                ))
        for cp in sends:
            cp.start()
        for cp in recvs:
            cp.wait_recv()
        for cp in sends:
            cp.wait_send()
        for cp in mine:
            cp.wait()

    out_type = [jax.ShapeDtypeStruct(a.shape, a.dtype) for a in arrs]
    sems = [pltpu.SemaphoreType.DMA((3 * n,)), pltpu.SemaphoreType.DMA((3 * n,)), pltpu.SemaphoreType.DMA((n,))]
    return _on_sequencer(body, out_type, sems, CHIPS_ID, name)(*arrs)


def _row_tile(R, C, max_elems):
    if R * C <= max_elems:
        return R
    best = None
    for tr in range(16, R, 16):
        if R % tr == 0 and tr * C <= max_elems:
            best = tr
    return best or R


def pair_sum(a42, land4, core, name):
    _, _, R, C = a42.shape
    tr = _row_tile(R, C, 1 << 20)

    def body(core_ref, a_ref, l_ref, o_ref):
        o_ref[...] = (a_ref[0].astype(F32) + l_ref[...].astype(F32)).astype(o_ref.dtype)

    return pl.pallas_call(
        body,
        grid_spec=pltpu.PrefetchScalarGridSpec(
            num_scalar_prefetch=1,
            grid=(4, R // tr),
            in_specs=[
                pl.BlockSpec((1, 1, tr, C), lambda k, r, core_ref: (k, core_ref[0], r, 0)),
                pl.BlockSpec((1, tr, C), lambda k, r, core_ref: (k, r, 0)),
            ],
            out_specs=pl.BlockSpec((1, tr, C), lambda k, r, core_ref: (k, r, 0)),
        ),
        out_shape=jax.ShapeDtypeStruct((4, R, C), BF16),
        compiler_params=_cp(("parallel", "parallel")),
        name=name,
    )(core, a42, land4)


def sum_slots(parts, name):
    P, R, C = parts.shape

    def body(p_ref, o_ref):
        acc = p_ref[0].astype(F32)
        for k in range(1, P):
            acc = acc + p_ref[k].astype(F32)
        o_ref[...] = acc

    tr = _row_tile(R, P * C, 1 << 21)
    return pl.pallas_call(
        body,
        grid=(R // tr,),
        in_specs=[pl.BlockSpec((P, tr, C), lambda r: (0, r, 0))],
        out_specs=pl.BlockSpec((tr, C), lambda r: (r, 0)),
        out_shape=jax.ShapeDtypeStruct((R, C), F32),
        compiler_params=_cp(("parallel",)),
        name=name,
    )(parts)


def adamw(w, m, v, parts, name):
    R, C = w.shape
    P = parts.shape[0]
    tr = _pick(R, (256, 128, 64, 32, 16, 8))
    c1 = 1.0 - ADAM_B1 ** ADAM_STEP
    c2 = 1.0 - ADAM_B2 ** ADAM_STEP

    def body(w_ref, m_ref, v_ref, p_ref, g_ref, d_ref, nm_ref, nv_ref):
        g = p_ref[0].astype(F32)
        for k in range(1, P):
            g = g + p_ref[k].astype(F32)
        nm = ADAM_B1 * m_ref[...] + (1.0 - ADAM_B1) * g
        nv = ADAM_B2 * v_ref[...] + (1.0 - ADAM_B2) * (g * g)
        g_ref[...] = g
        nm_ref[...] = nm
        nv_ref[...] = nv
        d_ref[...] = -ADAM_LR * ((nm / c1) / (jnp.sqrt(nv / c2) + ADAM_EPS) + ADAM_WD * w_ref[...])

    blk = pl.BlockSpec((tr, C), lambda r: (r, 0))
    shp = jax.ShapeDtypeStruct((R, C), F32)
    return pl.pallas_call(
        body,
        grid=(R // tr,),
        in_specs=[blk, blk, blk, pl.BlockSpec((P, tr, C), lambda r: (0, r, 0))],
        out_specs=[blk, blk, blk, blk],
        out_shape=[shp, shp, shp, shp],
        compiler_params=_cp(("parallel",)),
        name=name,
    )(w, m, v, parts)


_WEIGHTS = [
    "l0_mix_norm_g", "l0_w_in", "l0_sc_conv_w", "l0_w_out", "l0_ffn_norm_g", "l0_ffn_up", "l0_ffn_conv_w", "l0_ffn_down",
    "l1_mix_norm_g", "l1_w_in", "l1_fox_b_f", "l1_sg_w", "l1_sg_b", "l1_sg_norm_g", "l1_w_out", "l1_ffn_norm_g",
    "l1_ffn_up", "l1_ffn_conv_w", "l1_ffn_down", "final_norm_g",
]
_COL_SHARDED = ["l0_w_in", "l0_ffn_up", "l1_w_in", "l1_ffn_up"]
_ROW_SHARDED = ["l0_w_out", "l0_ffn_down", "l1_w_out", "l1_ffn_down"]
_BIG = ["l0_w_in", "l0_w_out", "l0_ffn_up", "l0_ffn_down", "l1_w_in", "l1_w_out", "l1_ffn_up", "l1_ffn_down"]
_CONV = ["l0_sc_conv_w", "l0_ffn_conv_w", "l1_ffn_conv_w"]
_SMALL = [n for n in _WEIGHTS if n not in _BIG]
_PACK_ROWS = 8


def _pack(arrs):
    flat = []
    for a in arrs:
        v = a.reshape(-1).astype(F32)
        pad = (-v.shape[0]) % (_PACK_ROWS * 128)
        flat.append(jnp.pad(v, (0, pad)))
    return jnp.concatenate(flat).reshape(-1, 128)


def _unpack(packed, shapes):
    out, off = [], 0
    flat = packed.reshape(-1)
    for shp in shapes:
        size = math.prod(shp)
        out.append(flat[off : off + size].reshape(shp))
        off += size + (-size) % (_PACK_ROWS * 128)
    return out


def kernel(x, l0_mix_norm_g, l0_w_in, l0_sc_conv_w, l0_w_out, l0_ffn_norm_g, l0_ffn_up, l0_ffn_conv_w, l0_ffn_down, l1_mix_norm_g, l1_w_in, l1_fox_b_f, l1_sg_w, l1_sg_b, l1_sg_norm_g, l1_w_out, l1_ffn_norm_g, l1_ffn_up, l1_ffn_conv_w, l1_ffn_down, final_norm_g, loss_target, m_l0_mix_norm_g, m_l0_w_in, m_l0_sc_conv_w, m_l0_w_out, m_l0_ffn_norm_g, m_l0_ffn_up, m_l0_ffn_conv_w, m_l0_ffn_down, m_l1_mix_norm_g, m_l1_w_in, m_l1_fox_b_f, m_l1_sg_w, m_l1_sg_b, m_l1_sg_norm_g, m_l1_w_out, m_l1_ffn_norm_g, m_l1_ffn_up, m_l1_ffn_conv_w, m_l1_ffn_down, m_final_norm_g, v_l0_mix_norm_g, v_l0_w_in, v_l0_sc_conv_w, v_l0_w_out, v_l0_ffn_norm_g, v_l0_ffn_up, v_l0_ffn_conv_w, v_l0_ffn_down, v_l1_mix_norm_g, v_l1_w_in, v_l1_fox_b_f, v_l1_sg_w, v_l1_sg_b, v_l1_sg_norm_g, v_l1_w_out, v_l1_ffn_norm_g, v_l1_ffn_up, v_l1_ffn_conv_w, v_l1_ffn_down, v_final_norm_g):
    given = dict(locals())
    w = {n: given[n] for n in _WEIGHTS}
    mom = {n: given["m_" + n] for n in _WEIGHTS}
    var = {n: given["v_" + n] for n in _WEIGHTS}
    xs, target = x[0], loss_target[0]
    S, D = xs.shape
    W = D // 2
    nh = W // HD
    cx, cy, cc = _place()
    me = 4 * cx + 2 * cy + cc

    first = all_gather([w[_BIG[0]].astype(BF16)] + [w[n] for n in _CONV], f"gather_{_BIG[0]}")
    full = dict(zip([_BIG[0]] + _CONV, first))
    for n in _BIG[1:]:
        full[n] = all_gather([w[n].astype(BF16)], f"gather_{n}")[0]
    wts = {"nb": NDEV}
    for n in ("l0_w_in", "l0_ffn_up", "l1_ffn_up"):
        wts[n] = full[n].reshape(NDEV * D, -1)
    for n in _ROW_SHARDED:
        wts[n] = full[n].reshape(-1, D)
    w_in1 = full["l1_w_in"].transpose(1, 0, 2).reshape(D, -1)
    wts["l1_w_in_main"] = w_in1[:, : 5 * W]
    wts["l1_w_in_f"] = jnp.pad(w_in1[:, 5 * W :], ((0, 0), (0, 128 - nh)))
    for n in _CONV:
        wts[n] = full[n].transpose(1, 0, 2).reshape(CONV_K, -1)
    for n in _SMALL:
        if n not in _CONV:
            wts[n] = w[n]

    core = jnp.reshape(cc, (1,)).astype(jnp.int32)
    waiting, by_chip = [], {}

    def to_chips():
        n, term, landed = waiting.pop()
        by_chip[n] = exchange_chips([pair_sum(term, landed, core, f"pair_sum_{n}")], f"reduce_chips_{n}")[0]

    def on_grad(n, term):
        if n == "l1_w_in":
            term = term.reshape(D, NDEV, -1).transpose(1, 0, 2)
        elif n in _ROW_SHARDED:
            term = term.reshape(NDEV, -1, D)
        else:
            term = term.reshape(NDEV, D, -1)
        term = term.reshape((4, 2) + term.shape[1:])
        landed = exchange_pair([term], f"reduce_pair_{n}")[0]
        if waiting:
            to_chips()
        waiting.append((n, term, landed))

    loss_tile, dx, g = local_step(xs, target, wts, on_grad)
    to_chips()
    loss = lax.psum(loss_tile[0, 0], ("x", "y", "c"))
    out_g, out_d, out_m, out_v = {}, {}, {}, {}
    for n, parts in by_chip.items():
        out_g[n], out_d[n], out_m[n], out_v[n] = adamw(w[n], mom[n], var[n], parts, f"adamw_{n}")

    small_terms = [g[n] for n in _SMALL]
    small_shapes = [tuple(t.shape) for t in small_terms]
    packed = _pack(small_terms)
    all_terms = all_gather([packed], "gather_small_grads")[0]
    small_sum = _unpack(sum_slots(all_terms, "sum_small_grads"), small_shapes)
    small_g = {}
    for n, t in zip(_SMALL, small_sum):
        if n in _CONV:
            cols = w[n].shape[1]
            t = lax.dynamic_slice_in_dim(t, me * cols, cols, axis=1)
        small_g[n] = t.reshape(w[n].shape)
    shapes = [w[n].shape for n in _SMALL]
    res = adamw(
        _pack([w[n] for n in _SMALL]), _pack([mom[n] for n in _SMALL]), _pack([var[n] for n in _SMALL]),
        _pack([small_g[n] for n in _SMALL])[None], "adamw_small",
    )
    for dst, packed_out in zip((out_g, out_d, out_m, out_v), res):
        for n, t in zip(_SMALL, _unpack(packed_out, shapes)):
            dst[n] = t

    return (loss, dx[None], *[out_g[n] for n in _WEIGHTS], *[out_d[n] for n in _WEIGHTS],
            *[out_m[n] for n in _WEIGHTS], *[out_v[n] for n in _WEIGHTS])
```

```python
import functools
import math

import jax
import jax.numpy as jnp
from jax import lax
from jax.experimental import pallas as pl
from jax.experimental.pallas import tpu as pltpu
from jax.experimental.pallas import tpu_sc as plsc

F32 = jnp.float32
BF16 = jnp.bfloat16
HD = 128
EPS = 1e-6
CONV_K = 3
VMEM_LIMIT_BYTES = 48 << 20
NDEV = 8
MESH = pl.DeviceIdType.MESH

ADAM_LR = 0.001
ADAM_B1 = 0.9
ADAM_B2 = 0.999
ADAM_EPS = 1e-08
ADAM_WD = 0.01
ADAM_STEP = 10


def _cp(sem):
    return pltpu.CompilerParams(dimension_semantics=sem, vmem_limit_bytes=VMEM_LIMIT_BYTES)


def _pick(n, prefs):
    for p in prefs:
        if n % p == 0:
            return p
    return n


def _dot(a, b):
    return jnp.dot(a, b, preferred_element_type=F32)


def _dot_nt(a, b):
    return lax.dot_general(a, b, (((1,), (1,)), ((), ())), preferred_element_type=F32)


def _dot_tn(a, b):
    return lax.dot_general(a, b, (((0,), (0,)), ((), ())), preferred_element_type=F32)


def _split3(x):
    hi = x.astype(BF16)
    r = x - hi.astype(F32)
    mid = r.astype(BF16)
    lo = (r - mid.astype(F32)).astype(BF16)
    return hi, mid, lo


def _dot_ones_right(x, ones_bf16):
    hi, mid, lo = _split3(x)
    return _dot(hi, ones_bf16) + _dot(mid, ones_bf16) + _dot(lo, ones_bf16)


def _dot_ones_left(ones_bf16, x):
    hi, mid, lo = _split3(x)
    return _dot(ones_bf16, hi) + _dot(ones_bf16, mid) + _dot(ones_bf16, lo)


def _iota2(shape, axis):
    return lax.broadcasted_iota(jnp.int32, shape, axis)


def mm_nn(a, w2d, nb, name, out_dtype=BF16, res=None, tm=None, tn=None, tk=None):
    M, K = a.shape
    n = w2d.shape[1]
    assert w2d.shape[0] == nb * K
    tm = tm or _pick(M, (1024, 512, 256, 128))
    tn = tn or _pick(n, (1408, 1024, 768, 512, 256, 128))
    tk = tk or (K if K <= 2048 else _pick(K, (1408, 1024, 512, 256, 128)))
    nk, nt = K // tk, n // tn
    has_res = res is not None

    def body(*refs):
        if has_res:
            a_ref, w_ref, r_ref, o_ref = refs[:4]
        else:
            a_ref, w_ref, o_ref = refs[:3]
            r_ref = None
        part = _dot(a_ref[...], w_ref[...])

        def finish(acc):
            if r_ref is not None:
                acc = acc + r_ref[...].astype(F32)
            o_ref[...] = acc.astype(o_ref.dtype)

        if nk == 1:
            finish(part)
        else:
            acc_ref = refs[-1]
            k = pl.program_id(3)

            @pl.when(k == 0)
            def _():
                acc_ref[...] = part

            @pl.when(k > 0)
            def _():
                acc_ref[...] += part

            @pl.when(k == nk - 1)
            def _():
                finish(acc_ref[...])

    in_specs = [
        pl.BlockSpec((tm, tk), lambda i, j, t, k: (i, k)),
        pl.BlockSpec((tk, tn), lambda i, j, t, k: (j * nk + k, t)),
    ]
    args = [a, w2d]
    out_spec = pl.BlockSpec((tm, tn), lambda i, j, t, k: (i, j * nt + t))
    if has_res:
        in_specs.append(out_spec)
        args.append(res)
    return pl.pallas_call(
        body,
        grid=(M // tm, nb, nt, nk),
        in_specs=in_specs,
        out_specs=out_spec,
        out_shape=jax.ShapeDtypeStruct((M, nb * n), out_dtype),
        scratch_shapes=[pltpu.VMEM((tm, tn), F32)] if nk > 1 else [],
        compiler_params=_cp(("parallel", "parallel", "parallel", "arbitrary")),
        name=name,
    )(*args)


def mm_nt(dy2d, w2d, nb, M, K, name, out_dtype=BF16, res=None, dy_maps=None, tm=None, tko=None, tn=None):
    n = w2d.shape[1]
    assert w2d.shape[0] == nb * K
    tm = tm or _pick(M, (1024, 512, 256, 128))
    tko = tko or _pick(K, (1024, 512, 256, 128))
    tn = tn or _pick(n, (1408, 1024, 768, 512, 256, 128))
    nt, nko = n // tn, K // tko
    has_res = res is not None
    if dy_maps is None:
        dy_maps = [lambda i, j, t: (i, j * nt + t)]
    nd = len(dy_maps)
    td = tn // nd

    def body(*refs):
        d_refs, w_ref = refs[:nd], refs[nd]
        r_ref = refs[nd + 1] if has_res else None
        o_ref, acc_ref = refs[-2], refs[-1]
        j, t = pl.program_id(2), pl.program_id(3)
        d = d_refs[0][...] if nd == 1 else jnp.concatenate([r[...] for r in d_refs], axis=1)
        part = _dot_nt(d, w_ref[...])
        first = jnp.logical_and(j == 0, t == 0)
        last = jnp.logical_and(j == nb - 1, t == nt - 1)

        @pl.when(first)
        def _():
            acc_ref[...] = part

        @pl.when(jnp.logical_not(first))
        def _():
            acc_ref[...] += part

        @pl.when(last)
        def _():
            acc = acc_ref[...]
            if r_ref is not None:
                acc = acc + r_ref[...].astype(F32)
            o_ref[...] = acc.astype(o_ref.dtype)

    in_specs = [pl.BlockSpec((tm, td), functools.partial(lambda f, i, ko, j, t: f(i, j, t), f)) for f in dy_maps]
    in_specs.append(pl.BlockSpec((tko, tn), lambda i, ko, j, t: (j * nko + ko, t)))
    args = [dy2d] * nd + [w2d]
    out_spec = pl.BlockSpec((tm, tko), lambda i, ko, j, t: (i, ko))
    if has_res:
        in_specs.append(out_spec)
        args.append(res)
    return pl.pallas_call(
        body,
        grid=(M // tm, nko, nb, nt),
        in_specs=in_specs,
        out_specs=out_spec,
        out_shape=jax.ShapeDtypeStruct((M, K), out_dtype),
        scratch_shapes=[pltpu.VMEM((tm, tko), F32)],
        compiler_params=_cp(("parallel", "parallel", "arbitrary", "arbitrary")),
        name=name,
    )(*args)


def mm_tn(x, dy2d, nb, n, name, out_dtype=BF16, dy_maps=None, tko=None, tn=None):
    S, K = x.shape
    tko = tko or _pick(K, (512, 256, 128))
    tn = tn or _pick(n, (1408, 1024, 768, 512, 256, 128))
    nt, nko = n // tn, K // tko
    if dy_maps is None:
        dy_maps = [lambda j, t: (0, j * nt + t)]
    nd = len(dy_maps)
    td = tn // nd

    def body(*refs):
        x_ref, d_refs, o_ref = refs[0], refs[1 : 1 + nd], refs[-1]
        d = d_refs[0][...] if nd == 1 else jnp.concatenate([r[...] for r in d_refs], axis=1)
        o_ref[...] = _dot_tn(x_ref[...], d).astype(o_ref.dtype)

    in_specs = [pl.BlockSpec((S, tko), lambda ko, j, t: (0, ko))]
    in_specs += [pl.BlockSpec((S, td), functools.partial(lambda f, ko, j, t: f(j, t), f)) for f in dy_maps]
    return pl.pallas_call(
        body,
        grid=(nko, nb, nt),
        in_specs=in_specs,
        out_specs=pl.BlockSpec((tko, tn), lambda ko, j, t: (j * nko + ko, t)),
        out_shape=jax.ShapeDtypeStruct((nb * K, n), out_dtype),
        compiler_params=_cp(("parallel", "parallel", "parallel")),
        name=name,
    )(x, *([dy2d] * nd))


def rms_fwd(x, g, name):
    S, D = x.shape
    tm = _pick(S, (256, 128))

    def body(x_ref, g_ref, o_ref):
        xf = x_ref[...]
        r = lax.rsqrt(jnp.mean(xf * xf, axis=-1, keepdims=True) + EPS)
        o_ref[...] = (xf * r * g_ref[...]).astype(o_ref.dtype)

    return pl.pallas_call(
        body,
        grid=(S // tm,),
        in_specs=[pl.BlockSpec((tm, D), lambda i: (i, 0)), pl.BlockSpec((1, D), lambda i: (0, 0))],
        out_specs=pl.BlockSpec((tm, D), lambda i: (i, 0)),
        out_shape=jax.ShapeDtypeStruct((S, D), BF16),
        compiler_params=_cp(("parallel",)),
        name=name,
    )(x, g.reshape(1, D))


def rms_bwd(x, g, dh, dres, name):
    S, D = x.shape
    tm = _pick(S, (256, 128))

    def body(x_ref, g_ref, dh_ref, dr_ref, dx_ref, dxb_ref, dg_ref):
        i = pl.program_id(0)
        xf = x_ref[...]
        dh = dh_ref[...].astype(F32)
        r = lax.rsqrt(jnp.mean(xf * xf, axis=-1, keepdims=True) + EPS)
        gy = dh * g_ref[...]
        proj = jnp.mean(gy * xf, axis=-1, keepdims=True)
        dx = dr_ref[...] + r * gy - xf * (r * r * r * proj)
        dx_ref[...] = dx
        dxb_ref[...] = dx.astype(BF16)
        dg = jnp.sum(dh * (xf * r), axis=0, keepdims=True)

        @pl.when(i == 0)
        def _():
            dg_ref[...] = dg

        @pl.when(i > 0)
        def _():
            dg_ref[...] += dg

    row = pl.BlockSpec((tm, D), lambda i: (i, 0))
    vec = pl.BlockSpec((1, D), lambda i: (0, 0))
    return pl.pallas_call(
        body,
        grid=(S // tm,),
        in_specs=[row, vec, row, row],
        out_specs=[row, row, vec],
        out_shape=[jax.ShapeDtypeStruct((S, D), F32), jax.ShapeDtypeStruct((S, D), BF16), jax.ShapeDtypeStruct((1, D), F32)],
        compiler_params=_cp(("arbitrary",)),
        name=name,
    )(x, g.reshape(1, D), dh, dres)


def loss_head(x, g, target, name):
    S, D = x.shape
    tm = _pick(S, (256, 128))

    def body(x_ref, g_ref, t_ref, dx_ref, dxb_ref, dg_ref, loss_ref):
        i = pl.program_id(0)
        xf = x_ref[...]
        gg = g_ref[...]
        r = lax.rsqrt(jnp.mean(xf * xf, axis=-1, keepdims=True) + EPS)
        xh = xf * r
        err = xh * gg - t_ref[...]
        part = (0.5 / D) * jnp.sum(err * err)
        dy = err * (1.0 / D)
        gy = dy * gg
        proj = jnp.mean(gy * xf, axis=-1, keepdims=True)
        dx = r * gy - xf * (r * r * r * proj)
        dx_ref[...] = dx
        dxb_ref[...] = dx.astype(BF16)
        dg = jnp.sum(dy * xh, axis=0, keepdims=True)
        lossb = jnp.full(loss_ref.shape, part, F32)

        @pl.when(i == 0)
        def _():
            dg_ref[...] = dg
            loss_ref[...] = lossb

        @pl.when(i > 0)
        def _():
            dg_ref[...] += dg
            loss_ref[...] += lossb

    row = pl.BlockSpec((tm, D), lambda i: (i, 0))
    vec = pl.BlockSpec((1, D), lambda i: (0, 0))
    return pl.pallas_call(
        body,
        grid=(S // tm,),
        in_specs=[row, vec, row],
        out_specs=[row, row, vec, pl.BlockSpec((8, 128), lambda i: (0, 0))],
        out_shape=[
            jax.ShapeDtypeStruct((S, D), F32),
            jax.ShapeDtypeStruct((S, D), BF16),
            jax.ShapeDtypeStruct((1, D), F32),
            jax.ShapeDtypeStruct((8, 128), F32),
        ],
        compiler_params=_cp(("arbitrary",)),
        name=name,
    )(x, g.reshape(1, D), target)


def _shift_down(s, k):
    if k == 0:
        return s
    return jnp.where(_iota2(s.shape, 0) >= k, pltpu.roll(s, k, axis=0), 0.0)


def _shift_up(s, k):
    if k == 0:
        return s
    n = s.shape[0]
    return jnp.where(_iota2(s.shape, 0) < n - k, pltpu.roll(s, n - k, axis=0), 0.0)


def _conv(s, w):
    return w[0:1] * _shift_down(s, 2) + w[1:2] * _shift_down(s, 1) + w[2:3] * s


def _conv_t(d, w):
    return w[2:3] * d + w[1:2] * _shift_up(d, 1) + w[0:1] * _shift_up(d, 2)


def _conv_dw(d, s):
    return [jnp.sum(d * _shift_down(s, CONV_K - 1 - k), axis=0, keepdims=True) for k in range(CONV_K)]


def sc_fwd(p, convw, cat, W, name):
    S = p.shape[0]
    tc = _pick(W, (256, 128))
    nc = W // tc

    def body(gb_ref, gc_ref, hi_ref, w_ref, cat_ref, o_ref):
        s = gc_ref[...].astype(F32) * hi_ref[...].astype(F32)
        o_ref[...] = (gb_ref[...].astype(F32) * _conv(s, w_ref[...])).astype(o_ref.dtype)

    col = lambda part: pl.BlockSpec((S, tc), lambda c: (0, part * nc + c))
    return pl.pallas_call(
        body,
        grid=(nc,),
        in_specs=[col(3), col(4), col(5), pl.BlockSpec((CONV_K, tc), lambda c: (0, c)), pl.BlockSpec(memory_space=pl.ANY)],
        out_specs=col(1),
        out_shape=jax.ShapeDtypeStruct(cat.shape, cat.dtype),
        input_output_aliases={4: 0},
        compiler_params=_cp(("parallel",)),
        name=name,
    )(p, p, p, convw, cat)


def sc_bwd(p, convw, dcat, dp, W, name):
    S = p.shape[0]
    tc = _pick(W, (256, 128))
    nc = W // tc

    def body(gb_ref, gc_ref, hi_ref, w_ref, do_ref, dp_in_ref, dp_ref, dw_ref):
        gb = gb_ref[...].astype(F32)
        gc = gc_ref[...].astype(F32)
        hi = hi_ref[...].astype(F32)
        w = w_ref[...]
        do = do_ref[...].astype(F32)
        s = gc * hi
        dcs = do * gb
        ds = _conv_t(dcs, w)
        dp_ref[0] = (do * _conv(s, w)).astype(dp_ref.dtype)
        dp_ref[1] = (ds * hi).astype(dp_ref.dtype)
        dp_ref[2] = (ds * gc).astype(dp_ref.dtype)
        for k, row in enumerate(_conv_dw(dcs, s)):
            dw_ref[k : k + 1, :] = row

    col = lambda part: pl.BlockSpec((S, tc), lambda c: (0, part * nc + c))
    return pl.pallas_call(
        body,
        grid=(nc,),
        in_specs=[
            col(3), col(4), col(5),
            pl.BlockSpec((CONV_K, tc), lambda c: (0, c)),
            pl.BlockSpec((S, tc), lambda c: (0, nc + c)),
            pl.BlockSpec(memory_space=pl.ANY),
        ],
        out_specs=[pl.BlockSpec((3, S, tc), lambda c: (1, 0, c)), pl.BlockSpec((CONV_K, tc), lambda c: (0, c))],
        out_shape=[jax.ShapeDtypeStruct(dp.shape, dp.dtype), jax.ShapeDtypeStruct((CONV_K, W), F32)],
        input_output_aliases={5: 0},
        compiler_params=_cp(("parallel",)),
        name=name,
    )(p, p, p, convw, dcat, dp)


def _silu_parts(a):
    sig = 1.0 / (1.0 + jnp.exp(-a))
    return a * sig, sig


def ffn_act_fwd(u, convw, F, name):
    S = u.shape[0]
    tc = _pick(F, (256, 128))
    nc = F // tc

    def body(ug_ref, uu_ref, wg_ref, wu_ref, o_ref):
        ag = _conv(ug_ref[...].astype(F32), wg_ref[...])
        au = _conv(uu_ref[...].astype(F32), wu_ref[...])
        o_ref[...] = (_silu_parts(ag)[0] * au).astype(o_ref.dtype)

    col = lambda half: pl.BlockSpec((S, tc), lambda c: (0, half * nc + c))
    wcol = lambda half: pl.BlockSpec((CONV_K, tc), lambda c: (0, half * nc + c))
    return pl.pallas_call(
        body,
        grid=(nc,),
        in_specs=[col(0), col(1), wcol(0), wcol(1)],
        out_specs=pl.BlockSpec((S, tc), lambda c: (0, c)),
        out_shape=jax.ShapeDtypeStruct((S, F), BF16),
        compiler_params=_cp(("parallel",)),
        name=name,
    )(u, u, convw, convw)


def ffn_act_bwd(u, convw, dact, F, name):
    S = u.shape[0]
    tc = _pick(F, (256, 128))
    nc = F // tc

    def body(ug_ref, uu_ref, wg_ref, wu_ref, da_ref, du_ref, dw_ref):
        ug = ug_ref[...].astype(F32)
        uu = uu_ref[...].astype(F32)
        wg = wg_ref[...]
        wu = wu_ref[...]
        da = da_ref[...].astype(F32)
        ag = _conv(ug, wg)
        au = _conv(uu, wu)
        sl, sig = _silu_parts(ag)
        dag = da * au * (sig * (1.0 + ag * (1.0 - sig)))
        dau = da * sl
        du_ref[0] = _conv_t(dag, wg).astype(du_ref.dtype)
        du_ref[1] = _conv_t(dau, wu).astype(du_ref.dtype)
        for k, (rg, ru) in enumerate(zip(_conv_dw(dag, ug), _conv_dw(dau, uu))):
            dw_ref[0, k : k + 1, :] = rg
            dw_ref[1, k : k + 1, :] = ru

    col = lambda half: pl.BlockSpec((S, tc), lambda c: (0, half * nc + c))
    wcol = lambda half: pl.BlockSpec((CONV_K, tc), lambda c: (0, half * nc + c))
    return pl.pallas_call(
        body,
        grid=(nc,),
        in_specs=[col(0), col(1), wcol(0), wcol(1), pl.BlockSpec((S, tc), lambda c: (0, c))],
        out_specs=[pl.BlockSpec((2, S, tc), lambda c: (0, 0, c)), pl.BlockSpec((2, CONV_K, tc), lambda c: (0, 0, c))],
        out_shape=[jax.ShapeDtypeStruct((2, S, F), BF16), jax.ShapeDtypeStruct((2, CONV_K, F), F32)],
        compiler_params=_cp(("parallel",)),
        name=name,
    )(u, u, convw, convw, dact)


def _softplus(z):
    return jnp.maximum(z, 0.0) + jnp.log(1.0 + jnp.exp(-jnp.abs(z)))


def _key_strip(S):
    return _pick(S, (512, 256, 128))


def _split2(x):
    hi = x.astype(BF16)
    return hi, (x - hi.astype(F32)).astype(BF16)


def _block_sums(x, ones_bf16):
    hi, lo = _split2(x)
    return [
        _dot(hi[:, b * HD : (b + 1) * HD], ones_bf16) + _dot(lo[:, b * HD : (b + 1) * HD], ones_bf16)
        for b in range(x.shape[1] // HD)
    ]


def _strip_mask(shape, i, off, strict):
    cols, rows = _iota2(shape, 1) + off, _iota2(shape, 0) + i * HD
    return cols < rows if strict else cols <= rows


def _sb_strip(q, ks, i, off, run, su):
    z = _dot_nt(q, ks) * (HD ** -0.5)
    mask = _strip_mask(z.shape, i, off, True)
    sp = _softplus(z)
    l = jnp.where(mask, -sp, 0.0)
    within = _block_sums(l, su)
    later = [None] * len(within)
    for b in reversed(range(len(within))):
        later[b] = within[b] + run
        run = run + jnp.sum(l[:, b * HD : (b + 1) * HD], axis=1, keepdims=True)
    a = jnp.where(mask, jnp.exp(z - sp + jnp.concatenate(later, axis=1)), 0.0)
    return z, mask, a, run


def sb_fwd(p, W, name):
    S = p.shape[0]
    nh, nq = W // HD, S // HD
    TK = _key_strip(S)

    def body(q_ref, k_ref, v_ref, o_ref):
        i = pl.program_id(1)
        q = q_ref[...]
        su = (_iota2((HD, HD), 0) > _iota2((HD, HD), 1)).astype(BF16)
        last = (i * HD) // TK

        def step(gg, carry):
            acc, run = carry
            off = pl.multiple_of((last - gg) * TK, TK)
            _, _, a, run = _sb_strip(q, k_ref[pl.ds(off, TK), :], i, off, run, su)
            return acc + _dot(a.astype(BF16), v_ref[pl.ds(off, TK), :]), run

        acc, _ = lax.fori_loop(0, last + 1, step, (jnp.zeros((HD, HD), F32), jnp.zeros((HD, 1), F32)))
        o_ref[...] = acc.astype(o_ref.dtype)

    return pl.pallas_call(
        body,
        grid=(nh, nq),
        in_specs=[
            pl.BlockSpec((HD, HD), lambda h, i: (i, h)),
            pl.BlockSpec((S, HD), lambda h, i: (0, nh + h)),
            pl.BlockSpec((S, HD), lambda h, i: (0, 2 * nh + h)),
        ],
        out_specs=pl.BlockSpec((HD, HD), lambda h, i: (i, h)),
        out_shape=jax.ShapeDtypeStruct((S, 2 * W), BF16),
        compiler_params=_cp(("parallel", "arbitrary")),
        name=name,
    )(p, p, p)


def sb_bwd(p, dcat, W, name):
    S = p.shape[0]
    nh, nq = W // HD, S // HD
    TK = _key_strip(S)
    scale = HD ** -0.5

    def body(q_ref, k_ref, v_ref, do_ref, dp_ref, dk_acc, dv_acc, e_scr, z_scr):
        i = pl.program_id(1)
        q = q_ref[...]
        do = do_ref[...]
        su = (_iota2((HD, HD), 0) > _iota2((HD, HD), 1)).astype(BF16)
        sl = (_iota2((HD, HD), 0) < _iota2((HD, HD), 1)).astype(BF16)
        last = (i * HD) // TK

        @pl.when(i == 0)
        def _():
            dk_acc[...] = jnp.zeros_like(dk_acc)
            dv_acc[...] = jnp.zeros_like(dv_acc)

        def pass_a(gg, run):
            g = last - gg
            off = pl.multiple_of(g * TK, TK)
            z, _, a, run = _sb_strip(q, k_ref[pl.ds(off, TK), :], i, off, run, su)
            e_scr[g] = a * _dot_nt(do, v_ref[pl.ds(off, TK), :])
            z_scr[g] = z
            dv_acc[pl.ds(off, TK), :] += _dot_tn(a.astype(BF16), do)
            return run

        lax.fori_loop(0, last + 1, pass_a, jnp.zeros((HD, 1), F32))

        def pass_b(g, carry):
            dq, run_e = carry
            off = pl.multiple_of(g * TK, TK)
            e = e_scr[g]
            z = z_scr[g]
            mask = _strip_mask(z.shape, i, off, True)
            within = _block_sums(e, sl)
            before = []
            for b in range(len(within)):
                before.append(within[b] + run_e)
                run_e = run_e + jnp.sum(e[:, b * HD : (b + 1) * HD], axis=1, keepdims=True)
            sig = 1.0 / (1.0 + jnp.exp(-z))
            dz = jnp.where(mask, e * (1.0 - sig) - jnp.concatenate(before, axis=1) * sig, 0.0)
            dz = (dz * scale).astype(BF16)
            dq = dq + _dot(dz, k_ref[pl.ds(off, TK), :])
            dk_acc[pl.ds(off, TK), :] += _dot_tn(dz, q)
            return dq, run_e

        dq, _ = lax.fori_loop(0, last + 1, pass_b, (jnp.zeros((HD, HD), F32), jnp.zeros((HD, 1), F32)))
        dp_ref[0, pl.ds(pl.multiple_of(i * HD, HD), HD), :] = dq.astype(dp_ref.dtype)

        @pl.when(i == nq - 1)
        def _():
            dp_ref[1] = dk_acc[...].astype(dp_ref.dtype)
            dp_ref[2] = dv_acc[...].astype(dp_ref.dtype)

    return pl.pallas_call(
        body,
        grid=(nh, nq),
        in_specs=[
            pl.BlockSpec((HD, HD), lambda h, i: (i, h)),
            pl.BlockSpec((S, HD), lambda h, i: (0, nh + h)),
            pl.BlockSpec((S, HD), lambda h, i: (0, 2 * nh + h)),
            pl.BlockSpec((HD, HD), lambda h, i: (i, h)),
        ],
        out_specs=pl.BlockSpec((3, S, HD), lambda h, i: (0, 0, h)),
        out_shape=jax.ShapeDtypeStruct((6, S, W), BF16),
        scratch_shapes=[
            pltpu.VMEM((S, HD), F32),
            pltpu.VMEM((S, HD), F32),
            pltpu.VMEM((S // TK, HD, TK), F32),
            pltpu.VMEM((S // TK, HD, TK), F32),
        ],
        compiler_params=_cp(("parallel", "arbitrary")),
        name=name,
    )(p, p, p, dcat)


def fox_gate_fwd(f, b, name):
    S = f.shape[0]
    nq = S // HD

    def body(f_ref, b_ref, c_ref, run):
        i = pl.program_id(0)

        @pl.when(i == 0)
        def _():
            run[...] = jnp.zeros_like(run)

        lf = -_softplus(-(f_ref[...] + b_ref[...]))
        tri = (_iota2((HD, HD), 0) >= _iota2((HD, HD), 1)).astype(BF16)
        c_ref[...] = _dot_ones_left(tri, lf) + run[...]
        run[...] += jnp.sum(lf, axis=0, keepdims=True)

    return pl.pallas_call(
        body,
        grid=(nq,),
        in_specs=[pl.BlockSpec((HD, 128), lambda i: (i, 0)), pl.BlockSpec((1, 128), lambda i: (0, 0))],
        out_specs=pl.BlockSpec((HD, 128), lambda i: (i, 0)),
        out_shape=jax.ShapeDtypeStruct((S, 128), F32),
        scratch_shapes=[pltpu.VMEM((1, 128), F32)],
        compiler_params=_cp(("arbitrary",)),
        name=name,
    )(f, b)


def fox_gate_bwd(f, b, dc, name):
    S = f.shape[0]
    nq = S // HD

    def body(f_ref, b_ref, dc_ref, df_ref, db_ref, run):
        i = pl.program_id(0)

        @pl.when(i == 0)
        def _():
            run[...] = jnp.zeros_like(run)

        dc = dc_ref[...]
        tri = (_iota2((HD, HD), 0) <= _iota2((HD, HD), 1)).astype(BF16)
        dlf = _dot_ones_left(tri, dc) + run[...]
        run[...] += jnp.sum(dc, axis=0, keepdims=True)
        x = f_ref[...] + b_ref[...]
        df = dlf * (1.0 / (1.0 + jnp.exp(x)))
        df_ref[...] = df
        db = jnp.sum(df, axis=0, keepdims=True)

        @pl.when(i == 0)
        def _():
            db_ref[...] = db

        @pl.when(i > 0)
        def _():
            db_ref[...] += db

    rev = pl.BlockSpec((HD, 128), lambda i: (nq - 1 - i, 0))
    vec = pl.BlockSpec((1, 128), lambda i: (0, 0))
    return pl.pallas_call(
        body,
        grid=(nq,),
        in_specs=[rev, vec, rev],
        out_specs=[rev, vec],
        out_shape=[jax.ShapeDtypeStruct((S, 128), F32), jax.ShapeDtypeStruct((1, 128), F32)],
        scratch_shapes=[pltpu.VMEM((1, 128), F32)],
        compiler_params=_cp(("arbitrary",)),
        name=name,
    )(f, b, dc)


def _fox_logits(q, ks, ct, cs, i, off):
    s = _dot_nt(q, ks) * (HD ** -0.5) + (ct - cs)
    mask = _strip_mask(s.shape, i, off, False)
    return jnp.where(mask, s, -1e30), mask


def fox_fwd(p, ccol, crow, cat, W, name):
    S = p.shape[0]
    nh, nq = W // HD, S // HD
    TK = _key_strip(S)

    def body(q_ref, k_ref, v_ref, cc_ref, cr_ref, cat_ref, o_ref, lse_ref):
        i = pl.program_id(1)
        q = q_ref[...]
        ct = cc_ref[0]

        def step(g, carry):
            m, l, acc = carry
            off = pl.multiple_of(g * TK, TK)
            s, _ = _fox_logits(q, k_ref[pl.ds(off, TK), :], ct, cr_ref[0, pl.ds(g, 1), :], i, off)
            m_new = jnp.maximum(m, jnp.max(s, axis=1, keepdims=True))
            alpha = jnp.exp(m - m_new)
            pr = jnp.exp(s - m_new)
            l = alpha * l + jnp.sum(pr, axis=1, keepdims=True)
            acc = alpha * acc + _dot(pr.astype(BF16), v_ref[pl.ds(off, TK), :])
            return m_new, l, acc

        init = (jnp.full((HD, 1), -1e30, F32), jnp.zeros((HD, 1), F32), jnp.zeros((HD, HD), F32))
        m, l, acc = lax.fori_loop(0, (i * HD) // TK + 1, step, init)
        o_ref[...] = (acc / l).astype(o_ref.dtype)
        lse_ref[0] = m + jnp.log(l)

    return pl.pallas_call(
        body,
        grid=(nh, nq),
        in_specs=[
            pl.BlockSpec((HD, HD), lambda h, i: (i, 2 * nh + h)),
            pl.BlockSpec((S, HD), lambda h, i: (0, 3 * nh + h)),
            pl.BlockSpec((S, HD), lambda h, i: (0, 4 * nh + h)),
            pl.BlockSpec((1, HD, 1), lambda h, i: (h, i, 0)),
            pl.BlockSpec((1, S // TK, TK), lambda h, i: (h, 0, 0)),
            pl.BlockSpec(memory_space=pl.ANY),
        ],
        out_specs=[pl.BlockSpec((HD, HD), lambda h, i: (i, nh + h)), pl.BlockSpec((1, HD, 1), lambda h, i: (h, i, 0))],
        out_shape=[jax.ShapeDtypeStruct(cat.shape, cat.dtype), jax.ShapeDtypeStruct((nh, S, 1), F32)],
        input_output_aliases={5: 0},
        compiler_params=_cp(("parallel", "arbitrary")),
        name=name,
    )(p, p, p, ccol, crow, cat)


def fox_bwd(p, ccol, crow, cat, lse, dcat, dp, W, name):
    S = p.shape[0]
    nh, nq = W // HD, S // HD
    TK = _key_strip(S)
    scale = HD ** -0.5

    def body(q_ref, k_ref, v_ref, cc_ref, cr_ref, o_ref, lse_ref, do_ref, dp_in_ref, dp_ref, dcs_ref, dct_ref, dk_acc, dv_acc):
        i = pl.program_id(1)
        q = q_ref[...]
        do = do_ref[...]
        ct = cc_ref[0]
        lse_i = lse_ref[0]
        delta = jnp.sum(do.astype(F32) * o_ref[...].astype(F32), axis=1, keepdims=True)

        @pl.when(i == 0)
        def _():
            dk_acc[...] = jnp.zeros_like(dk_acc)
            dv_acc[...] = jnp.zeros_like(dv_acc)
            dcs_ref[...] = jnp.zeros_like(dcs_ref)

        def step(g, carry):
            dq, dct = carry
            off = pl.multiple_of(g * TK, TK)
            ks = k_ref[pl.ds(off, TK), :]
            s, mask = _fox_logits(q, ks, ct, cr_ref[0, pl.ds(g, 1), :], i, off)
            pr = jnp.where(mask, jnp.exp(s - lse_i), 0.0)
            ds = pr * (_dot_nt(do, v_ref[pl.ds(off, TK), :]) - delta)
            dv_acc[pl.ds(off, TK), :] += _dot_tn(pr.astype(BF16), do)
            dsb = (ds * scale).astype(BF16)
            dk_acc[pl.ds(off, TK), :] += _dot_tn(dsb, q)
            dcs_ref[0, pl.ds(g, 1), :] += jnp.sum(ds, axis=0, keepdims=True)
            return dq + _dot(dsb, ks), dct + jnp.sum(ds, axis=1, keepdims=True)

        dq, dct = lax.fori_loop(0, (i * HD) // TK + 1, step, (jnp.zeros((HD, HD), F32), jnp.zeros((HD, 1), F32)))
        dp_ref[0, pl.ds(pl.multiple_of(i * HD, HD), HD), :] = dq.astype(dp_ref.dtype)
        dct_ref[0] = dct

        @pl.when(i == nq - 1)
        def _():
            dp_ref[1] = dk_acc[...].astype(dp_ref.dtype)
            dp_ref[2] = dv_acc[...].astype(dp_ref.dtype)

    return pl.pallas_call(
        body,
        grid=(nh, nq),
        in_specs=[
            pl.BlockSpec((HD, HD), lambda h, i: (i, 2 * nh + h)),
            pl.BlockSpec((S, HD), lambda h, i: (0, 3 * nh + h)),
            pl.BlockSpec((S, HD), lambda h, i: (0, 4 * nh + h)),
            pl.BlockSpec((1, HD, 1), lambda h, i: (h, i, 0)),
            pl.BlockSpec((1, S // TK, TK), lambda h, i: (h, 0, 0)),
            pl.BlockSpec((HD, HD), lambda h, i: (i, nh + h)),
            pl.BlockSpec((1, HD, 1), lambda h, i: (h, i, 0)),
            pl.BlockSpec((HD, HD), lambda h, i: (i, nh + h)),
            pl.BlockSpec(memory_space=pl.ANY),
        ],
        out_specs=[
            pl.BlockSpec((3, S, HD), lambda h, i: (1, 0, h)),
            pl.BlockSpec((1, S // TK, TK), lambda h, i: (h, 0, 0)),
            pl.BlockSpec((1, HD, 1), lambda h, i: (h, i, 0)),
        ],
        out_shape=[
            jax.ShapeDtypeStruct(dp.shape, dp.dtype),
            jax.ShapeDtypeStruct((nh, S // TK, TK), F32),
            jax.ShapeDtypeStruct((nh, S, 1), F32),
        ],
        input_output_aliases={8: 0},
        scratch_shapes=[pltpu.VMEM((S, HD), F32), pltpu.VMEM((S, HD), F32)],
        compiler_params=_cp(("parallel", "arbitrary")),
        name=name,
    )(p, p, p, ccol, crow, cat, lse, dcat, dp)


_GELU_K = math.sqrt(2.0 / math.pi)
_GELU_C = 0.044715


def _gelu(x):
    return 0.5 * x * (1.0 + jnp.tanh(_GELU_K * (x + _GELU_C * x * x * x)))


def _gelu_grad(x):
    t = jnp.tanh(_GELU_K * (x + _GELU_C * x * x * x))
    return 0.5 * (1.0 + t) + 0.5 * x * (1.0 - t * t) * (_GELU_K * (1.0 + 3.0 * _GELU_C * x * x))


def _layernorm_parts(gv):
    xc = gv - jnp.mean(gv, axis=-1, keepdims=True)
    r = lax.rsqrt(jnp.mean(xc * xc, axis=-1, keepdims=True) + EPS)
    return xc * r, r


def sg_fwd(p, sg_w, sg_bt, sg_g, W, name):
    S = p.shape[0]
    G, nq = W // HD, S // HD

    def body(u_ref, v_ref, w_ref, bt_ref, g_ref, o_ref):
        xh, _ = _layernorm_parts(_gelu(v_ref[...].astype(F32)))
        vn = (xh * g_ref[...]).astype(BF16)
        tri = _iota2((HD, HD), 0) >= _iota2((HD, HD), 1)
        for gi in range(G):
            cols = slice(gi * HD, (gi + 1) * HD)
            wt = jnp.where(tri, w_ref[gi], 0.0).astype(BF16)
            mixed = _dot(wt, vn[:, cols]) + bt_ref[:, gi : gi + 1]
            o_ref[:, cols] = (_gelu(u_ref[:, cols].astype(F32)) * mixed).astype(o_ref.dtype)

    return pl.pallas_call(
        body,
        grid=(nq,),
        in_specs=[
            pl.BlockSpec((HD, W), lambda i: (i, 0)),
            pl.BlockSpec((HD, W), lambda i: (i, 1)),
            pl.BlockSpec((G, HD, HD), lambda i: (0, 0, 0)),
            pl.BlockSpec((HD, G), lambda i: (0, 0)),
            pl.BlockSpec((1, W), lambda i: (0, 0)),
        ],
        out_specs=pl.BlockSpec((HD, W), lambda i: (i, 0)),
        out_shape=jax.ShapeDtypeStruct((S, 2 * W), BF16),
        compiler_params=_cp(("parallel",)),
        name=name,
    )(p, p, sg_w, sg_bt, sg_g.reshape(1, W))


def sg_bwd(p, sg_w, sg_bt, sg_g, dcat, W, name):
    S = p.shape[0]
    G, nq = W // HD, S // HD

    def body(u_ref, v_ref, w_ref, bt_ref, g_ref, do_ref, dp_ref, dw_ref, dbt_ref, dg_ref, dvn_scr):
        i = pl.program_id(0)

        @pl.when(i == 0)
        def _():
            dw_ref[...] = jnp.zeros_like(dw_ref)
            dbt_ref[...] = jnp.zeros_like(dbt_ref)
            dg_ref[...] = jnp.zeros_like(dg_ref)

        v = v_ref[...].astype(F32)
        xh, r = _layernorm_parts(_gelu(v))
        gg = g_ref[...]
        vn = (xh * gg).astype(BF16)
        tri = _iota2((HD, HD), 0) >= _iota2((HD, HD), 1)
        for gi in range(G):
            cols = slice(gi * HD, (gi + 1) * HD)
            wt = jnp.where(tri, w_ref[gi], 0.0).astype(BF16)
            mixed = _dot(wt, vn[:, cols]) + bt_ref[:, gi : gi + 1]
            u = u_ref[:, cols].astype(F32)
            do = do_ref[:, cols].astype(F32)
            dp_ref[0, :, cols] = (do * mixed * _gelu_grad(u)).astype(dp_ref.dtype)
            dmix = do * _gelu(u)
            dmb = dmix.astype(BF16)
            dw_ref[gi] += jnp.where(tri, _dot_nt(dmb, vn[:, cols]), 0.0)
            dbt_ref[:, gi : gi + 1] += jnp.sum(dmix, axis=1, keepdims=True)
            dvn_scr[:, cols] = _dot_tn(wt, dmb)
        dvn = dvn_scr[...]
        dg_ref[...] += jnp.sum(dvn * xh, axis=0, keepdims=True)
        dxh = dvn * gg
        dgv = r * (dxh - jnp.mean(dxh, axis=-1, keepdims=True) - xh * jnp.mean(dxh * xh, axis=-1, keepdims=True))
        dp_ref[1] = (dgv * _gelu_grad(v)).astype(dp_ref.dtype)

    return pl.pallas_call(
        body,
        grid=(nq,),
        in_specs=[
            pl.BlockSpec((HD, W), lambda i: (i, 0)),
            pl.BlockSpec((HD, W), lambda i: (i, 1)),
            pl.BlockSpec((G, HD, HD), lambda i: (0, 0, 0)),
            pl.BlockSpec((HD, G), lambda i: (0, 0)),
            pl.BlockSpec((1, W), lambda i: (0, 0)),
            pl.BlockSpec((HD, W), lambda i: (i, 0)),
        ],
        out_specs=[
            pl.BlockSpec((2, HD, W), lambda i: (0, i, 0)),
            pl.BlockSpec((G, HD, HD), lambda i: (0, 0, 0)),
            pl.BlockSpec((HD, G), lambda i: (0, 0)),
            pl.BlockSpec((1, W), lambda i: (0, 0)),
        ],
        out_shape=[
            jax.ShapeDtypeStruct((6, S, W), BF16),
            jax.ShapeDtypeStruct((G, HD, HD), F32),
            jax.ShapeDtypeStruct((HD, G), F32),
            jax.ShapeDtypeStruct((1, W), F32),
        ],
        scratch_shapes=[pltpu.VMEM((HD, W), F32)],
        compiler_params=_cp(("arbitrary",)),
        name=name,
    )(p, p, sg_w, sg_bt, sg_g.reshape(1, W), dcat)


def local_step(x, target, wts, start_gather, on_grad):
    S, D = x.shape
    W = D // 2
    nb, F = wts["nb"], wts["F"]
    g = {}

    def ffn_fwd(xin, l, ahead=(None, None, None)):
        h = rms_fwd(xin, wts[f"{l}_ffn_norm_g"], f"{l}_ffn_rms")
        h = start_gather(ahead[0], h) if ahead[0] else h
        u = mm_nn(h, wts[f"{l}_ffn_up"], nb, f"{l}_ffn_up_mm")
        u = start_gather(ahead[1], u) if ahead[1] else u
        act = ffn_act_fwd(u, wts[f"{l}_ffn_conv_w"], F, f"{l}_ffn_act")
        act = start_gather(ahead[2], act) if ahead[2] else act
        xout = mm_nn(act, wts[f"{l}_ffn_down"], 1, f"{l}_ffn_down_mm", out_dtype=F32, res=xin)
        return xout, (xin, h, u, act)

    def ffn_bwd(dxout, dxoutb, saved, l):
        xin, h, u, act = saved
        dact = mm_nt(dxoutb, wts[f"{l}_ffn_down"], 1, S, F, f"{l}_ffn_down_dx")
        dact = on_grad(f"{l}_ffn_down", mm_tn(act, dxoutb, 1, D, f"{l}_ffn_down_dw"), dact)
        du, dcw = ffn_act_bwd(u, wts[f"{l}_ffn_conv_w"], dact, F, f"{l}_ffn_act_bwd")
        g[f"{l}_ffn_conv_w"] = jnp.concatenate([dcw[0], dcw[1]], axis=1)
        du2 = du.reshape(2 * S, F)
        n = wts[f"{l}_ffn_up"].shape[1]
        tn = _pick(n, (1408, 1024, 768, 512, 256, 128))
        per_half = F // tn
        nt = n // tn

        def up_block(i, j, t):
            vb = j * nt + t
            return vb // per_half, vb % per_half

        tm = _pick(S, (1024, 512, 256, 128))

        def nt_map(i, j, t):
            half, cb = up_block(i, j, t)
            return (half * (S // tm) + i, cb)

        def tn_map(j, t):
            half, cb = up_block(0, j, t)
            return (half, cb)

        dh = mm_nt(du2, wts[f"{l}_ffn_up"], nb, S, D, f"{l}_ffn_up_dx", dy_maps=[nt_map], tm=tm, tn=tn)
        dh = on_grad(f"{l}_ffn_up", mm_tn(h, du2, nb, n, f"{l}_ffn_up_dw", dy_maps=[tn_map], tn=tn), dh)
        dxin, dxinb, dg = rms_bwd(xin, wts[f"{l}_ffn_norm_g"], dh, dxout, f"{l}_ffn_rms_bwd")
        g[f"{l}_ffn_norm_g"] = dg
        return dxin, dxinb

    h0 = rms_fwd(x, wts["l0_mix_norm_g"], "l0_mix_rms")
    h0 = start_gather("l0_w_out", h0)
    p0 = mm_nn(h0, wts["l0_w_in"], nb, "l0_w_in_mm")
    p0 = start_gather("l0_ffn_up", p0)
    cat0 = sb_fwd(p0, W, "l0_sb_fwd")
    cat0 = sc_fwd(p0, wts["l0_sc_conv_w"], cat0, W, "l0_sc_fwd")
    cat0 = start_gather("l0_ffn_down", cat0)
    x1 = mm_nn(cat0, wts["l0_w_out"], 1, "l0_w_out_mm", out_dtype=F32, res=x)
    x1 = start_gather("l1_w_in", x1)
    x2, ffn0_saved = ffn_fwd(x1, "l0", ahead=("l1_w_out", "l1_ffn_up", "l1_ffn_down"))

    nh = W // HD
    h2 = rms_fwd(x2, wts["l1_mix_norm_g"], "l1_mix_rms")
    p1 = mm_nn(h2, wts["l1_w_in_main"], 1, "l1_w_in_mm")
    f = mm_nn(h2, wts["l1_w_in_f"], 1, "l1_w_f_mm", out_dtype=F32)
    bf = jnp.zeros((1, 128), F32).at[0, :nh].set(wts["l1_fox_b_f"])
    c = fox_gate_fwd(f, bf, "l1_fox_gate")
    c_heads = c[:, :nh].T
    ccol = c_heads[:, :, None]
    crow = c_heads.reshape(nh, S // _key_strip(S), _key_strip(S))
    sg_bt = wts["l1_sg_b"].T
    cat1 = sg_fwd(p1, wts["l1_sg_w"], sg_bt, wts["l1_sg_norm_g"], W, "l1_sg_fwd")
    cat1, lse = fox_fwd(p1, ccol, crow, cat1, W, "l1_fox_fwd")
    x3 = mm_nn(cat1, wts["l1_w_out"], 1, "l1_w_out_mm", out_dtype=F32, res=x2)
    x4, ffn1_saved = ffn_fwd(x3, "l1")

    dx4, dx4b, dgf, loss = loss_head(x4, wts["final_norm_g"], target, "loss_head")
    g["final_norm_g"] = dgf

    dx3, dx3b = ffn_bwd(dx4, dx4b, ffn1_saved, "l1")
    dcat1 = mm_nt(dx3b, wts["l1_w_out"], 1, S, D, "l1_w_out_dx")
    dcat1 = on_grad("l1_w_out", mm_tn(cat1, dx3b, 1, D, "l1_w_out_dw"), dcat1)
    dp1, dsgw, dsgbt, dsgg = sg_bwd(p1, wts["l1_sg_w"], sg_bt, wts["l1_sg_norm_g"], dcat1, W, "l1_sg_bwd")
    dp1, dcs, dct = fox_bwd(p1, ccol, crow, cat1, lse, dcat1, dp1, W, "l1_fox_bwd")
    g["l1_sg_w"], g["l1_sg_b"], g["l1_sg_norm_g"] = dsgw, dsgbt.T, dsgg
    dc = jnp.zeros((S, 128), F32).at[:, :nh].set((dct[:, :, 0] - dcs.reshape(nh, S)).T)
    df, dbf = fox_gate_bwd(f, bf, dc, "l1_fox_gate_bwd")
    g["l1_fox_b_f"] = dbf[0, :nh]
    dfb = df.astype(BF16)
    tn1 = _pick(W, (1024, 512, 256, 128))
    tm1 = _pick(S, (1024, 512, 256, 128))
    per_part = W // tn1
    part_of = lambda pt: pt + pt // 2 - pt // 4

    def nt_map1(i, j, t):
        return (part_of(t // per_part) * (S // tm1) + i, t % per_part)

    def tn_map1(j, t):
        return (part_of(t // per_part), t % per_part)

    dp1_2d = dp1.reshape(6 * S, W)
    dw_main = mm_tn(h2, dp1_2d, 1, 5 * W, "l1_w_in_dw", dy_maps=[tn_map1], tn=tn1)
    dw_f = mm_tn(h2, dfb, 1, 128, "l1_w_f_dw")
    dh2 = mm_nt(dfb, wts["l1_w_in_f"], 1, S, D, "l1_w_f_dx", out_dtype=F32)
    dh2 = mm_nt(dp1_2d, wts["l1_w_in_main"], 1, S, D, "l1_w_in_dx", res=dh2, dy_maps=[nt_map1], tm=tm1, tn=tn1)
    dh2 = on_grad("l1_w_in", jnp.concatenate([dw_main, dw_f[:, :nh]], axis=1), dh2)
    dx2, dx2b, dg = rms_bwd(x2, wts["l1_mix_norm_g"], dh2, dx3, "l1_mix_rms_bwd")
    g["l1_mix_norm_g"] = dg

    dx1, dx1b = ffn_bwd(dx2, dx2b, ffn0_saved, "l0")
    dcat0 = mm_nt(dx1b, wts["l0_w_out"], 1, S, D, "l0_w_out_dx")
    dcat0 = on_grad("l0_w_out", mm_tn(cat0, dx1b, 1, D, "l0_w_out_dw"), dcat0)
    dp0 = sb_bwd(p0, dcat0, W, "l0_sb_bwd")
    dp0, dscw = sc_bwd(p0, wts["l0_sc_conv_w"], dcat0, dp0, W, "l0_sc_bwd")
    g["l0_sc_conv_w"] = dscw
    n0 = wts["l0_w_in"].shape[1]
    td0 = math.gcd(n0, W)
    nd0 = n0 // td0
    tm0 = _pick(S, (1024, 512, 256, 128))
    per_part0 = W // td0

    def nt_maps0(k):
        def f(i, j, t):
            vb = j * nd0 + k
            return ((vb // per_part0) * (S // tm0) + i, vb % per_part0)
        return f

    def tn_maps0(k):
        def f(j, t):
            vb = j * nd0 + k
            return (vb // per_part0, vb % per_part0)
        return f

    dp0_2d = dp0.reshape(6 * S, W)
    dw0 = mm_tn(h0, dp0_2d, nb, n0, "l0_w_in_dw", dy_maps=[tn_maps0(k) for k in range(nd0)], tn=n0)
    dp0_2d = on_grad("l0_w_in", dw0, dp0_2d)
    dh0 = mm_nt(dp0_2d, wts["l0_w_in"], nb, S, D, "l0_w_in_dx", dy_maps=[nt_maps0(k) for k in range(nd0)], tm=tm0, tn=n0)
    dx0, _, dg = rms_bwd(x, wts["l0_mix_norm_g"], dh0, dx1, "l0_mix_rms_bwd")
    g["l0_mix_norm_g"] = dg
    return loss, dx0, g


GATHER_ID, PAIR_ID, CHIPS_ID = 1, 2, 3


def _place():
    return lax.axis_index("x"), lax.axis_index("y"), lax.axis_index("c")


def _other_chips(x, y):
    return [(x, 1 - y), (1 - x, y), (1 - x, 1 - y)]


def _handshake(peers):
    barrier = pltpu.get_barrier_semaphore()
    for peer in peers:
        pl.semaphore_signal(barrier, inc=1, device_id=peer, device_id_type=MESH)
    pl.semaphore_wait(barrier, len(peers))


def _on_sequencer(body, out_type, scratch_types, collective_id, name):
    return pl.kernel(
        body,
        out_type=out_type,
        mesh=plsc.ScalarSubcoreMesh(axis_name="seq", num_cores=1),
        scratch_types=scratch_types,
        compiler_params=pltpu.CompilerParams(collective_id=collective_id),
        name=name,
    )


def all_gather(arrs, name):
    n = len(arrs)

    def body(*refs):
        xs, outs = refs[:n], refs[n : 2 * n]
        send_sems, recv_sems, local_sems = refs[2 * n :]
        x, y, c = _place()
        me, sibling = (x, y, c), (x, y, 1 - c)
        chips = _other_chips(x, y)
        _handshake([sibling] + [(*chip, c) for chip in chips])

        def copy(a, k, block, to, src=None):
            px, py, pc = block
            dst = outs[a].at[4 * px + 2 * py + pc]
            return pltpu.make_async_remote_copy(
                src_ref=dst if src is None else src, dst_ref=dst,
                send_sem=send_sems.at[7 * a + k], recv_sem=recv_sems.at[7 * a + k], device_id=to, device_id_type=MESH,
            )

        mine = [pltpu.make_async_copy(xs[a], outs[a].at[4 * x + 2 * y + c], local_sems.at[a]) for a in range(n)]
        for cp in mine:
            cp.start()
        first = []
        for a in range(n):
            first.append(copy(a, 0, me, sibling, src=xs[a]))
            first += [copy(a, 1 + j, me, (*chip, c), src=xs[a]) for j, chip in enumerate(chips)]
        for cp in first:
            cp.start()
        passed = []
        for a in range(n):
            for j, chip in enumerate(chips):
                copy(a, 1 + j, (*chip, c), me).wait_recv()
                cp = copy(a, 4 + j, (*chip, c), sibling)
                cp.start()
                passed.append(cp)
        for a in range(n):
            copy(a, 0, sibling, me).wait_recv()
            for j, chip in enumerate(chips):
                copy(a, 4 + j, (*chip, 1 - c), me).wait_recv()
        for cp in first + passed:
            cp.wait_send()
        for cp in mine:
            cp.wait()

    out_type = [jax.ShapeDtypeStruct((NDEV,) + a.shape, a.dtype) for a in arrs]
    sems = [pltpu.SemaphoreType.DMA((7 * n,)), pltpu.SemaphoreType.DMA((7 * n,)), pltpu.SemaphoreType.DMA((n,))]
    return _on_sequencer(body, out_type, sems, GATHER_ID, name)(*arrs)


def exchange_pair(arrs, name):
    n = len(arrs)

    def body(*refs):
        xs, lands = refs[:n], refs[n : 2 * n]
        send_sems, recv_sems = refs[2 * n :]
        x, y, c = _place()
        _handshake([(x, y, 1 - c)])
        copies = [
            pltpu.make_async_remote_copy(
                src_ref=xs[a].at[k, 1 - c], dst_ref=lands[a].at[k],
                send_sem=send_sems.at[4 * a + k], recv_sem=recv_sems.at[4 * a + k],
                device_id=(x, y, 1 - c), device_id_type=MESH,
            )
            for a in range(n)
            for k in range(4)
        ]
        for cp in copies:
            cp.start()
        for cp in copies:
            cp.wait()

    out_type = [jax.ShapeDtypeStruct((4,) + a.shape[2:], a.dtype) for a in arrs]
    sems = [pltpu.SemaphoreType.DMA((4 * n,)), pltpu.SemaphoreType.DMA((4 * n,))]
    return _on_sequencer(body, out_type, sems, PAIR_ID, name)(*arrs)


def exchange_chips(arrs, name):
    n = len(arrs)

    def body(*refs):
        xs, lands = refs[:n], refs[n : 2 * n]
        send_sems, recv_sems, local_sems = refs[2 * n :]
        x, y, c = _place()
        my_chip = 2 * x + y
        chips = _other_chips(x, y)
        _handshake([(*chip, c) for chip in chips])
        mine = [pltpu.make_async_copy(xs[a].at[my_chip], lands[a].at[my_chip], local_sems.at[a]) for a in range(n)]
        for cp in mine:
            cp.start()
        sends, recvs = [], []
        for a in range(n):
            for j, (px, py) in enumerate(chips):
                peer = 2 * px + py
                sends.append(pltpu.make_async_remote_copy(
                    src_ref=xs[a].at[peer], dst_ref=lands[a].at[my_chip],
                    send_sem=send_sems.at[3 * a + j], recv_sem=recv_sems.at[3 * a + j],
                    device_id=(px, py, c), device_id_type=MESH,
                ))
                recvs.append(pltpu.make_async_remote_copy(
                    src_ref=xs[a].at[peer], dst_ref=lands[a].at[peer],
                    send_sem=send_sems.at[3 * a + j], recv_sem=recv_sems.at[3 * a + j],
                    device_id=(px, py, c), device_id_type=MESH,
                ))
        for cp in sends:
            cp.start()
        for cp in recvs:
            cp.wait_recv()
        for cp in sends:
            cp.wait_send()
        for cp in mine:
            cp.wait()

    out_type = [jax.ShapeDtypeStruct(a.shape, a.dtype) for a in arrs]
    sems = [pltpu.SemaphoreType.DMA((3 * n,)), pltpu.SemaphoreType.DMA((3 * n,)), pltpu.SemaphoreType.DMA((n,))]
    return _on_sequencer(body, out_type, sems, CHIPS_ID, name)(*arrs)


def _row_tile(R, C, max_elems):
    if R * C <= max_elems:
        return R
    best = None
    for tr in range(16, R, 16):
        if R % tr == 0 and tr * C <= max_elems:
            best = tr
    return best or R


def pair_sum(a42, land4, core, name):
    _, _, R, C = a42.shape
    tr = _row_tile(R, C, 1 << 20)

    def body(core_ref, a_ref, l_ref, o_ref):
        o_ref[...] = (a_ref[0].astype(F32) + l_ref[...].astype(F32)).astype(o_ref.dtype)

    return pl.pallas_call(
        body,
        grid_spec=pltpu.PrefetchScalarGridSpec(
            num_scalar_prefetch=1,
            grid=(4, R // tr),
            in_specs=[
                pl.BlockSpec((1, 1, tr, C), lambda k, r, core_ref: (k, core_ref[0], r, 0)),
                pl.BlockSpec((1, tr, C), lambda k, r, core_ref: (k, r, 0)),
            ],
            out_specs=pl.BlockSpec((1, tr, C), lambda k, r, core_ref: (k, r, 0)),
        ),
        out_shape=jax.ShapeDtypeStruct((4, R, C), BF16),
        compiler_params=_cp(("parallel", "parallel")),
        name=name,
    )(core, a42, land4)


def sum_slots(parts, name):
    P, R, C = parts.shape

    def body(p_ref, o_ref):
        acc = p_ref[0].astype(F32)
        for k in range(1, P):
            acc = acc + p_ref[k].astype(F32)
        o_ref[...] = acc

    tr = _row_tile(R, P * C, 1 << 21)
    return pl.pallas_call(
        body,
        grid=(R // tr,),
        in_specs=[pl.BlockSpec((P, tr, C), lambda r: (0, r, 0))],
        out_specs=pl.BlockSpec((tr, C), lambda r: (r, 0)),
        out_shape=jax.ShapeDtypeStruct((R, C), F32),
        compiler_params=_cp(("parallel",)),
        name=name,
    )(parts)


def adamw(w, m, v, parts, name):
    R, C = w.shape
    P = parts.shape[0]
    tr = _pick(R, (256, 128, 64, 32, 16, 8))
    c1 = 1.0 - ADAM_B1 ** ADAM_STEP
    c2 = 1.0 - ADAM_B2 ** ADAM_STEP

    def body(w_ref, m_ref, v_ref, p_ref, g_ref, d_ref, nm_ref, nv_ref):
        g = p_ref[0].astype(F32)
        for k in range(1, P):
            g = g + p_ref[k].astype(F32)
        nm = ADAM_B1 * m_ref[...] + (1.0 - ADAM_B1) * g
        nv = ADAM_B2 * v_ref[...] + (1.0 - ADAM_B2) * (g * g)
        g_ref[...] = g
        nm_ref[...] = nm
        nv_ref[...] = nv
        d_ref[...] = -ADAM_LR * ((nm / c1) / (jnp.sqrt(nv / c2) + ADAM_EPS) + ADAM_WD * w_ref[...])

    blk = pl.BlockSpec((tr, C), lambda r: (r, 0))
    shp = jax.ShapeDtypeStruct((R, C), F32)
    return pl.pallas_call(
        body,
        grid=(R // tr,),
        in_specs=[blk, blk, blk, pl.BlockSpec((P, tr, C), lambda r: (0, r, 0))],
        out_specs=[blk, blk, blk, blk],
        out_shape=[shp, shp, shp, shp],
        compiler_params=_cp(("parallel",)),
        name=name,
    )(w, m, v, parts)


_WEIGHTS = [
    "l0_mix_norm_g", "l0_w_in", "l0_sc_conv_w", "l0_w_out", "l0_ffn_norm_g", "l0_ffn_up", "l0_ffn_conv_w", "l0_ffn_down",
    "l1_mix_norm_g", "l1_w_in", "l1_fox_b_f", "l1_sg_w", "l1_sg_b", "l1_sg_norm_g", "l1_w_out", "l1_ffn_norm_g",
    "l1_ffn_up", "l1_ffn_conv_w", "l1_ffn_down", "final_norm_g",
]
_COL_SHARDED = ["l0_w_in", "l0_ffn_up", "l1_w_in", "l1_ffn_up"]
_ROW_SHARDED = ["l0_w_out", "l0_ffn_down", "l1_w_out", "l1_ffn_down"]
_BIG = ["l0_w_in", "l0_w_out", "l0_ffn_up", "l0_ffn_down", "l1_w_in", "l1_w_out", "l1_ffn_up", "l1_ffn_down"]
_CONV = ["l0_sc_conv_w", "l0_ffn_conv_w", "l1_ffn_conv_w"]
_SMALL = [n for n in _WEIGHTS if n not in _BIG]
_PACK_ROWS = 8


def _pack(arrs):
    flat = []
    for a in arrs:
        v = a.reshape(-1).astype(F32)
        pad = (-v.shape[0]) % (_PACK_ROWS * 128)
        flat.append(jnp.pad(v, (0, pad)))
    return jnp.concatenate(flat).reshape(-1, 128)


def _unpack(packed, shapes):
    out, off = [], 0
    flat = packed.reshape(-1)
    for shp in shapes:
        size = math.prod(shp)
        out.append(flat[off : off + size].reshape(shp))
        off += size + (-size) % (_PACK_ROWS * 128)
    return out


def kernel(x, l0_mix_norm_g, l0_w_in, l0_sc_conv_w, l0_w_out, l0_ffn_norm_g, l0_ffn_up, l0_ffn_conv_w, l0_ffn_down, l1_mix_norm_g, l1_w_in, l1_fox_b_f, l1_sg_w, l1_sg_b, l1_sg_norm_g, l1_w_out, l1_ffn_norm_g, l1_ffn_up, l1_ffn_conv_w, l1_ffn_down, final_norm_g, loss_target, m_l0_mix_norm_g, m_l0_w_in, m_l0_sc_conv_w, m_l0_w_out, m_l0_ffn_norm_g, m_l0_ffn_up, m_l0_ffn_conv_w, m_l0_ffn_down, m_l1_mix_norm_g, m_l1_w_in, m_l1_fox_b_f, m_l1_sg_w, m_l1_sg_b, m_l1_sg_norm_g, m_l1_w_out, m_l1_ffn_norm_g, m_l1_ffn_up, m_l1_ffn_conv_w, m_l1_ffn_down, m_final_norm_g, v_l0_mix_norm_g, v_l0_w_in, v_l0_sc_conv_w, v_l0_w_out, v_l0_ffn_norm_g, v_l0_ffn_up, v_l0_ffn_conv_w, v_l0_ffn_down, v_l1_mix_norm_g, v_l1_w_in, v_l1_fox_b_f, v_l1_sg_w, v_l1_sg_b, v_l1_sg_norm_g, v_l1_w_out, v_l1_ffn_norm_g, v_l1_ffn_up, v_l1_ffn_conv_w, v_l1_ffn_down, v_final_norm_g):
    given = dict(locals())
    w = {n: given[n] for n in _WEIGHTS}
    mom = {n: given["m_" + n] for n in _WEIGHTS}
    var = {n: given["v_" + n] for n in _WEIGHTS}
    xs, target = x[0], loss_target[0]
    S, D = xs.shape
    W = D // 2
    nh = W // HD
    cx, cy, cc = _place()
    me = 4 * cx + 2 * cy + cc

    wts = {"nb": NDEV, "F": l0_ffn_down.shape[0] * NDEV}
    for n in _SMALL:
        if n not in _CONV:
            wts[n] = w[n]

    def start_gather(n, after=None):
        src = [w[n].astype(BF16)] + ([w[c] for c in _CONV] if n == _BIG[0] else [])
        if after is not None:
            src, after = lax.optimization_barrier((src, after))
        got = all_gather(src, f"gather_{n}")
        if n == "l1_w_in":
            w_in1 = got[0].transpose(1, 0, 2).reshape(D, -1)
            wts["l1_w_in_main"] = w_in1[:, : 5 * W]
            wts["l1_w_in_f"] = jnp.pad(w_in1[:, 5 * W :], ((0, 0), (0, 128 - nh)))
        elif n in _ROW_SHARDED:
            wts[n] = got[0].reshape(-1, D)
        else:
            wts[n] = got[0].reshape(NDEV * D, -1)
        for c, taps in zip(_CONV, got[1:]):
            wts[c] = taps.transpose(1, 0, 2).reshape(CONV_K, -1)
        return after

    core = jnp.reshape(cc, (1,)).astype(jnp.int32)
    waiting, by_chip = [], {}

    def to_chips(after):
        n, term, landed = waiting.pop()
        summed = pair_sum(term, landed, core, f"pair_sum_{n}")
        if after is not None:
            summed, after = lax.optimization_barrier((summed, after))
        by_chip[n] = exchange_chips([summed], f"reduce_chips_{n}")[0]
        return after

    def on_grad(n, term, after):
        if n == "l1_w_in":
            term = term.reshape(D, NDEV, -1).transpose(1, 0, 2)
        elif n in _ROW_SHARDED:
            term = term.reshape(NDEV, -1, D)
        else:
            term = term.reshape(NDEV, D, -1)
        term = term.reshape((4, 2) + term.shape[1:])
        term, after = lax.optimization_barrier((term, after))
        landed = exchange_pair([term], f"reduce_pair_{n}")[0]
        if waiting:
            after = to_chips(after)
        waiting.append((n, term, landed))
        return after

    start_gather(_BIG[0])
    loss_tile, dx, g = local_step(xs, target, wts, start_gather, on_grad)
    to_chips(None)
    loss = lax.psum(loss_tile[0, 0], ("x", "y", "c"))
    out_g, out_d, out_m, out_v = {}, {}, {}, {}
    for n, parts in by_chip.items():
        out_g[n], out_d[n], out_m[n], out_v[n] = adamw(w[n], mom[n], var[n], parts, f"adamw_{n}")

    small_terms = [g[n] for n in _SMALL]
    small_shapes = [tuple(t.shape) for t in small_terms]
    packed = _pack(small_terms)
    all_terms = all_gather([packed], "gather_small_grads")[0]
    small_sum = _unpack(sum_slots(all_terms, "sum_small_grads"), small_shapes)
    small_g = {}
    for n, t in zip(_SMALL, small_sum):
        if n in _CONV:
            cols = w[n].shape[1]
            t = lax.dynamic_slice_in_dim(t, me * cols, cols, axis=1)
        small_g[n] = t.reshape(w[n].shape)
    shapes = [w[n].shape for n in _SMALL]
    res = adamw(
        _pack([w[n] for n in _SMALL]), _pack([mom[n] for n in _SMALL]), _pack([var[n] for n in _SMALL]),
        _pack([small_g[n] for n in _SMALL])[None], "adamw_small",
    )
    for dst, packed_out in zip((out_g, out_d, out_m, out_v), res):
        for n, t in zip(_SMALL, _unpack(packed_out, shapes)):
            dst[n] = t

    return (loss, dx[None], *[out_g[n] for n in _WEIGHTS], *[out_d[n] for n in _WEIGHTS],
            *[out_m[n] for n in _WEIGHTS], *[out_v[n] for n in _WEIGHTS])
```

```python
import functools
import math

import jax
import jax.numpy as jnp
from jax import lax
from jax.experimental import pallas as pl
from jax.experimental.pallas import tpu as pltpu
from jax.experimental.pallas import tpu_sc as plsc

F32 = jnp.float32
BF16 = jnp.bfloat16
HD = 128
EPS = 1e-6
CONV_K = 3
VMEM_LIMIT_BYTES = 48 << 20
NDEV = 8
MESH = pl.DeviceIdType.MESH

ADAM_LR = 0.001
ADAM_B1 = 0.9
ADAM_B2 = 0.999
ADAM_EPS = 1e-08
ADAM_WD = 0.01
ADAM_STEP = 10


def _cp(sem):
    return pltpu.CompilerParams(dimension_semantics=sem, vmem_limit_bytes=VMEM_LIMIT_BYTES)


def _pick(n, prefs):
    for p in prefs:
        if n % p == 0:
            return p
    return n


def _dot(a, b):
    return jnp.dot(a, b, preferred_element_type=F32)


def _dot_nt(a, b):
    return lax.dot_general(a, b, (((1,), (1,)), ((), ())), preferred_element_type=F32)


def _dot_tn(a, b):
    return lax.dot_general(a, b, (((0,), (0,)), ((), ())), preferred_element_type=F32)


def _split3(x):
    hi = x.astype(BF16)
    r = x - hi.astype(F32)
    mid = r.astype(BF16)
    lo = (r - mid.astype(F32)).astype(BF16)
    return hi, mid, lo


def _dot_ones_right(x, ones_bf16):
    hi, mid, lo = _split3(x)
    return _dot(hi, ones_bf16) + _dot(mid, ones_bf16) + _dot(lo, ones_bf16)


def _dot_ones_left(ones_bf16, x):
    hi, mid, lo = _split3(x)
    return _dot(ones_bf16, hi) + _dot(ones_bf16, mid) + _dot(ones_bf16, lo)


def _iota2(shape, axis):
    return lax.broadcasted_iota(jnp.int32, shape, axis)


def mm_nn(a, w2d, nb, name, out_dtype=BF16, res=None, tm=None, tn=None, tk=None):
    M, K = a.shape
    n = w2d.shape[1]
    assert w2d.shape[0] == nb * K
    tm = tm or _pick(M, (1024, 512, 256, 128))
    tn = tn or _pick(n, (1408, 1024, 768, 512, 256, 128))
    tk = tk or (K if K <= 2048 else _pick(K, (1408, 1024, 512, 256, 128)))
    nk, nt = K // tk, n // tn
    has_res = res is not None

    def body(*refs):
        if has_res:
            a_ref, w_ref, r_ref, o_ref = refs[:4]
        else:
            a_ref, w_ref, o_ref = refs[:3]
            r_ref = None
        part = _dot(a_ref[...], w_ref[...])

        def finish(acc):
            if r_ref is not None:
                acc = acc + r_ref[...].astype(F32)
            o_ref[...] = acc.astype(o_ref.dtype)

        if nk == 1:
            finish(part)
        else:
            acc_ref = refs[-1]
            k = pl.program_id(3)

            @pl.when(k == 0)
            def _():
                acc_ref[...] = part

            @pl.when(k > 0)
            def _():
                acc_ref[...] += part

            @pl.when(k == nk - 1)
            def _():
                finish(acc_ref[...])

    in_specs = [
        pl.BlockSpec((tm, tk), lambda i, j, t, k: (i, k)),
        pl.BlockSpec((tk, tn), lambda i, j, t, k: (j * nk + k, t)),
    ]
    args = [a, w2d]
    out_spec = pl.BlockSpec((tm, tn), lambda i, j, t, k: (i, j * nt + t))
    if has_res:
        in_specs.append(out_spec)
        args.append(res)
    return pl.pallas_call(
        body,
        grid=(M // tm, nb, nt, nk),
        in_specs=in_specs,
        out_specs=out_spec,
        out_shape=jax.ShapeDtypeStruct((M, nb * n), out_dtype),
        scratch_shapes=[pltpu.VMEM((tm, tn), F32)] if nk > 1 else [],
        compiler_params=_cp(("parallel", "parallel", "parallel", "arbitrary")),
        name=name,
    )(*args)


def mm_nt(dy2d, w2d, nb, M, K, name, out_dtype=BF16, res=None, dy_maps=None, tm=None, tko=None, tn=None):
    n = w2d.shape[1]
    assert w2d.shape[0] == nb * K
    tm = tm or _pick(M, (1024, 512, 256, 128))
    tko = tko or _pick(K, (1024, 512, 256, 128))
    tn = tn or _pick(n, (1408, 1024, 768, 512, 256, 128))
    nt, nko = n // tn, K // tko
    has_res = res is not None
    if dy_maps is None:
        dy_maps = [lambda i, j, t: (i, j * nt + t)]
    nd = len(dy_maps)
    td = tn // nd

    def body(*refs):
        d_refs, w_ref = refs[:nd], refs[nd]
        r_ref = refs[nd + 1] if has_res else None
        o_ref, acc_ref = refs[-2], refs[-1]
        j, t = pl.program_id(2), pl.program_id(3)
        d = d_refs[0][...] if nd == 1 else jnp.concatenate([r[...] for r in d_refs], axis=1)
        part = _dot_nt(d, w_ref[...])
        first = jnp.logical_and(j == 0, t == 0)
        last = jnp.logical_and(j == nb - 1, t == nt - 1)

        @pl.when(first)
        def _():
            acc_ref[...] = part

        @pl.when(jnp.logical_not(first))
        def _():
            acc_ref[...] += part

        @pl.when(last)
        def _():
            acc = acc_ref[...]
            if r_ref is not None:
                acc = acc + r_ref[...].astype(F32)
            o_ref[...] = acc.astype(o_ref.dtype)

    in_specs = [pl.BlockSpec((tm, td), functools.partial(lambda f, i, ko, j, t: f(i, j, t), f)) for f in dy_maps]
    in_specs.append(pl.BlockSpec((tko, tn), lambda i, ko, j, t: (j * nko + ko, t)))
    args = [dy2d] * nd + [w2d]
    out_spec = pl.BlockSpec((tm, tko), lambda i, ko, j, t: (i, ko))
    if has_res:
        in_specs.append(out_spec)
        args.append(res)
    return pl.pallas_call(
        body,
        grid=(M // tm, nko, nb, nt),
        in_specs=in_specs,
        out_specs=out_spec,
        out_shape=jax.ShapeDtypeStruct((M, K), out_dtype),
        scratch_shapes=[pltpu.VMEM((tm, tko), F32)],
        compiler_params=_cp(("parallel", "parallel", "arbitrary", "arbitrary")),
        name=name,
    )(*args)


def mm_tn(x, dy2d, nb, n, name, out_dtype=BF16, dy_maps=None, tko=None, tn=None):
    S, K = x.shape
    tko = tko or _pick(K, (512, 256, 128))
    tn = tn or _pick(n, (1408, 1024, 768, 512, 256, 128))
    nt, nko = n // tn, K // tko
    if dy_maps is None:
        dy_maps = [lambda j, t: (0, j * nt + t)]
    nd = len(dy_maps)
    td = tn // nd

    def body(*refs):
        x_ref, d_refs, o_ref = refs[0], refs[1 : 1 + nd], refs[-1]
        d = d_refs[0][...] if nd == 1 else jnp.concatenate([r[...] for r in d_refs], axis=1)
        o_ref[...] = _dot_tn(x_ref[...], d).astype(o_ref.dtype)

    in_specs = [pl.BlockSpec((S, tko), lambda ko, j, t: (0, ko))]
    in_specs += [pl.BlockSpec((S, td), functools.partial(lambda f, ko, j, t: f(j, t), f)) for f in dy_maps]
    return pl.pallas_call(
        body,
        grid=(nko, nb, nt),
        in_specs=in_specs,
        out_specs=pl.BlockSpec((tko, tn), lambda ko, j, t: (j * nko + ko, t)),
        out_shape=jax.ShapeDtypeStruct((nb * K, n), out_dtype),
        compiler_params=_cp(("parallel", "parallel", "parallel")),
        name=name,
    )(x, *([dy2d] * nd))


def rms_fwd(x, g, name):
    S, D = x.shape
    tm = _pick(S, (256, 128))

    def body(x_ref, g_ref, o_ref):
        xf = x_ref[...]
        r = lax.rsqrt(jnp.mean(xf * xf, axis=-1, keepdims=True) + EPS)
        o_ref[...] = (xf * r * g_ref[...]).astype(o_ref.dtype)

    return pl.pallas_call(
        body,
        grid=(S // tm,),
        in_specs=[pl.BlockSpec((tm, D), lambda i: (i, 0)), pl.BlockSpec((1, D), lambda i: (0, 0))],
        out_specs=pl.BlockSpec((tm, D), lambda i: (i, 0)),
        out_shape=jax.ShapeDtypeStruct((S, D), BF16),
        compiler_params=_cp(("parallel",)),
        name=name,
    )(x, g.reshape(1, D))


def rms_bwd(x, g, dh, dres, name):
    S, D = x.shape
    tm = _pick(S, (256, 128))

    def body(x_ref, g_ref, dh_ref, dr_ref, dx_ref, dxb_ref, dg_ref):
        i = pl.program_id(0)
        xf = x_ref[...]
        dh = dh_ref[...].astype(F32)
        r = lax.rsqrt(jnp.mean(xf * xf, axis=-1, keepdims=True) + EPS)
        gy = dh * g_ref[...]
        proj = jnp.mean(gy * xf, axis=-1, keepdims=True)
        dx = dr_ref[...] + r * gy - xf * (r * r * r * proj)
        dx_ref[...] = dx
        dxb_ref[...] = dx.astype(BF16)
        dg = jnp.sum(dh * (xf * r), axis=0, keepdims=True)

        @pl.when(i == 0)
        def _():
            dg_ref[...] = dg

        @pl.when(i > 0)
        def _():
            dg_ref[...] += dg

    row = pl.BlockSpec((tm, D), lambda i: (i, 0))
    vec = pl.BlockSpec((1, D), lambda i: (0, 0))
    return pl.pallas_call(
        body,
        grid=(S // tm,),
        in_specs=[row, vec, row, row],
        out_specs=[row, row, vec],
        out_shape=[jax.ShapeDtypeStruct((S, D), F32), jax.ShapeDtypeStruct((S, D), BF16), jax.ShapeDtypeStruct((1, D), F32)],
        compiler_params=_cp(("arbitrary",)),
        name=name,
    )(x, g.reshape(1, D), dh, dres)


def loss_head(x, g, target, name):
    S, D = x.shape
    tm = _pick(S, (256, 128))

    def body(x_ref, g_ref, t_ref, dx_ref, dxb_ref, dg_ref, loss_ref):
        i = pl.program_id(0)
        xf = x_ref[...]
        gg = g_ref[...]
        r = lax.rsqrt(jnp.mean(xf * xf, axis=-1, keepdims=True) + EPS)
        xh = xf * r
        err = xh * gg - t_ref[...]
        part = (0.5 / D) * jnp.sum(err * err)
        dy = err * (1.0 / D)
        gy = dy * gg
        proj = jnp.mean(gy * xf, axis=-1, keepdims=True)
        dx = r * gy - xf * (r * r * r * proj)
        dx_ref[...] = dx
        dxb_ref[...] = dx.astype(BF16)
        dg = jnp.sum(dy * xh, axis=0, keepdims=True)
        lossb = jnp.full(loss_ref.shape, part, F32)

        @pl.when(i == 0)
        def _():
            dg_ref[...] = dg
            loss_ref[...] = lossb

        @pl.when(i > 0)
        def _():
            dg_ref[...] += dg
            loss_ref[...] += lossb

    row = pl.BlockSpec((tm, D), lambda i: (i, 0))
    vec = pl.BlockSpec((1, D), lambda i: (0, 0))
    return pl.pallas_call(
        body,
        grid=(S // tm,),
        in_specs=[row, vec, row],
        out_specs=[row, row, vec, pl.BlockSpec((8, 128), lambda i: (0, 0))],
        out_shape=[
            jax.ShapeDtypeStruct((S, D), F32),
            jax.ShapeDtypeStruct((S, D), BF16),
            jax.ShapeDtypeStruct((1, D), F32),
            jax.ShapeDtypeStruct((8, 128), F32),
        ],
        compiler_params=_cp(("arbitrary",)),
        name=name,
    )(x, g.reshape(1, D), target)


def _shift_down(s, k):
    if k == 0:
        return s
    return jnp.where(_iota2(s.shape, 0) >= k, pltpu.roll(s, k, axis=0), 0.0)


def _shift_up(s, k):
    if k == 0:
        return s
    n = s.shape[0]
    return jnp.where(_iota2(s.shape, 0) < n - k, pltpu.roll(s, n - k, axis=0), 0.0)


def _conv(s, w):
    return w[0:1] * _shift_down(s, 2) + w[1:2] * _shift_down(s, 1) + w[2:3] * s


def _conv_t(d, w):
    return w[2:3] * d + w[1:2] * _shift_up(d, 1) + w[0:1] * _shift_up(d, 2)


def _conv_dw(d, s):
    return [jnp.sum(d * _shift_down(s, CONV_K - 1 - k), axis=0, keepdims=True) for k in range(CONV_K)]


def sc_fwd(p, convw, cat, W, name):
    S = p.shape[0]
    tc = _pick(W, (256, 128))
    nc = W // tc

    def body(gb_ref, gc_ref, hi_ref, w_ref, cat_ref, o_ref):
        s = gc_ref[...].astype(F32) * hi_ref[...].astype(F32)
        o_ref[...] = (gb_ref[...].astype(F32) * _conv(s, w_ref[...])).astype(o_ref.dtype)

    col = lambda part: pl.BlockSpec((S, tc), lambda c: (0, part * nc + c))
    return pl.pallas_call(
        body,
        grid=(nc,),
        in_specs=[col(3), col(4), col(5), pl.BlockSpec((CONV_K, tc), lambda c: (0, c)), pl.BlockSpec(memory_space=pl.ANY)],
        out_specs=col(1),
        out_shape=jax.ShapeDtypeStruct(cat.shape, cat.dtype),
        input_output_aliases={4: 0},
        compiler_params=_cp(("parallel",)),
        name=name,
    )(p, p, p, convw, cat)


def sc_bwd(p, convw, dcat, dp, W, name):
    S = p.shape[0]
    tc = _pick(W, (256, 128))
    nc = W // tc

    def body(gb_ref, gc_ref, hi_ref, w_ref, do_ref, dp_in_ref, dp_ref, dw_ref):
        gb = gb_ref[...].astype(F32)
        gc = gc_ref[...].astype(F32)
        hi = hi_ref[...].astype(F32)
        w = w_ref[...]
        do = do_ref[...].astype(F32)
        s = gc * hi
        dcs = do * gb
        ds = _conv_t(dcs, w)
        dp_ref[0] = (do * _conv(s, w)).astype(dp_ref.dtype)
        dp_ref[1] = (ds * hi).astype(dp_ref.dtype)
        dp_ref[2] = (ds * gc).astype(dp_ref.dtype)
        for k, row in enumerate(_conv_dw(dcs, s)):
            dw_ref[k : k + 1, :] = row

    col = lambda part: pl.BlockSpec((S, tc), lambda c: (0, part * nc + c))
    return pl.pallas_call(
        body,
        grid=(nc,),
        in_specs=[
            col(3), col(4), col(5),
            pl.BlockSpec((CONV_K, tc), lambda c: (0, c)),
            pl.BlockSpec((S, tc), lambda c: (0, nc + c)),
            pl.BlockSpec(memory_space=pl.ANY),
        ],
        out_specs=[pl.BlockSpec((3, S, tc), lambda c: (1, 0, c)), pl.BlockSpec((CONV_K, tc), lambda c: (0, c))],
        out_shape=[jax.ShapeDtypeStruct(dp.shape, dp.dtype), jax.ShapeDtypeStruct((CONV_K, W), F32)],
        input_output_aliases={5: 0},
        compiler_params=_cp(("parallel",)),
        name=name,
    )(p, p, p, convw, dcat, dp)


def _silu_parts(a):
    sig = 1.0 / (1.0 + jnp.exp(-a))
    return a * sig, sig


def ffn_act_fwd(u, convw, F, name):
    S = u.shape[0]
    tc = _pick(F, (256, 128))
    nc = F // tc

    def body(ug_ref, uu_ref, wg_ref, wu_ref, o_ref):
        ag = _conv(ug_ref[...].astype(F32), wg_ref[...])
        au = _conv(uu_ref[...].astype(F32), wu_ref[...])
        o_ref[...] = (_silu_parts(ag)[0] * au).astype(o_ref.dtype)

    col = lambda half: pl.BlockSpec((S, tc), lambda c: (0, half * nc + c))
    wcol = lambda half: pl.BlockSpec((CONV_K, tc), lambda c: (0, half * nc + c))
    return pl.pallas_call(
        body,
        grid=(nc,),
        in_specs=[col(0), col(1), wcol(0), wcol(1)],
        out_specs=pl.BlockSpec((S, tc), lambda c: (0, c)),
        out_shape=jax.ShapeDtypeStruct((S, F), BF16),
        compiler_params=_cp(("parallel",)),
        name=name,
    )(u, u, convw, convw)


def ffn_act_bwd(u, convw, dact, F, name):
    S = u.shape[0]
    tc = _pick(F, (256, 128))
    nc = F // tc

    def body(ug_ref, uu_ref, wg_ref, wu_ref, da_ref, du_ref, dw_ref):
        ug = ug_ref[...].astype(F32)
        uu = uu_ref[...].astype(F32)
        wg = wg_ref[...]
        wu = wu_ref[...]
        da = da_ref[...].astype(F32)
        ag = _conv(ug, wg)
        au = _conv(uu, wu)
        sl, sig = _silu_parts(ag)
        dag = da * au * (sig * (1.0 + ag * (1.0 - sig)))
        dau = da * sl
        du_ref[0] = _conv_t(dag, wg).astype(du_ref.dtype)
        du_ref[1] = _conv_t(dau, wu).astype(du_ref.dtype)
        for k, (rg, ru) in enumerate(zip(_conv_dw(dag, ug), _conv_dw(dau, uu))):
            dw_ref[0, k : k + 1, :] = rg
            dw_ref[1, k : k + 1, :] = ru

    col = lambda half: pl.BlockSpec((S, tc), lambda c: (0, half * nc + c))
    wcol = lambda half: pl.BlockSpec((CONV_K, tc), lambda c: (0, half * nc + c))
    return pl.pallas_call(
        body,
        grid=(nc,),
        in_specs=[col(0), col(1), wcol(0), wcol(1), pl.BlockSpec((S, tc), lambda c: (0, c))],
        out_specs=[pl.BlockSpec((2, S, tc), lambda c: (0, 0, c)), pl.BlockSpec((2, CONV_K, tc), lambda c: (0, 0, c))],
        out_shape=[jax.ShapeDtypeStruct((2, S, F), BF16), jax.ShapeDtypeStruct((2, CONV_K, F), F32)],
        compiler_params=_cp(("parallel",)),
        name=name,
    )(u, u, convw, convw, dact)


def _softplus(z):
    return jnp.maximum(z, 0.0) + jnp.log(1.0 + jnp.exp(-jnp.abs(z)))


def _key_strip(S):
    return _pick(S, (512, 256, 128))


def _split2(x):
    hi = x.astype(BF16)
    return hi, (x - hi.astype(F32)).astype(BF16)


def _block_sums(x, ones_bf16):
    hi, lo = _split2(x)
    return [
        _dot(hi[:, b * HD : (b + 1) * HD], ones_bf16) + _dot(lo[:, b * HD : (b + 1) * HD], ones_bf16)
        for b in range(x.shape[1] // HD)
    ]


def _strip_mask(shape, i, off, strict):
    cols, rows = _iota2(shape, 1) + off, _iota2(shape, 0) + i * HD
    return cols < rows if strict else cols <= rows


def _sb_strip(q, ks, i, off, run, su):
    z = _dot_nt(q, ks) * (HD ** -0.5)
    mask = _strip_mask(z.shape, i, off, True)
    sp = _softplus(z)
    l = jnp.where(mask, -sp, 0.0)
    within = _block_sums(l, su)
    later = [None] * len(within)
    for b in reversed(range(len(within))):
        later[b] = within[b] + run
        run = run + jnp.sum(l[:, b * HD : (b + 1) * HD], axis=1, keepdims=True)
    a = jnp.where(mask, jnp.exp(z - sp + jnp.concatenate(later, axis=1)), 0.0)
    return z, mask, a, run


def sb_fwd(p, W, name):
    S = p.shape[0]
    nh, nq = W // HD, S // HD
    TK = _key_strip(S)

    def body(q_ref, k_ref, v_ref, o_ref):
        i = pl.program_id(1)
        q = q_ref[...]
        su = (_iota2((HD, HD), 0) > _iota2((HD, HD), 1)).astype(BF16)
        last = (i * HD) // TK

        def step(gg, carry):
            acc, run = carry
            off = pl.multiple_of((last - gg) * TK, TK)
            _, _, a, run = _sb_strip(q, k_ref[pl.ds(off, TK), :], i, off, run, su)
            return acc + _dot(a.astype(BF16), v_ref[pl.ds(off, TK), :]), run

        acc, _ = lax.fori_loop(0, last + 1, step, (jnp.zeros((HD, HD), F32), jnp.zeros((HD, 1), F32)))
        o_ref[...] = acc.astype(o_ref.dtype)

    return pl.pallas_call(
        body,
        grid=(nh, nq),
        in_specs=[
            pl.BlockSpec((HD, HD), lambda h, i: (i, h)),
            pl.BlockSpec((S, HD), lambda h, i: (0, nh + h)),
            pl.BlockSpec((S, HD), lambda h, i: (0, 2 * nh + h)),
        ],
        out_specs=pl.BlockSpec((HD, HD), lambda h, i: (i, h)),
        out_shape=jax.ShapeDtypeStruct((S, 2 * W), BF16),
        compiler_params=_cp(("parallel", "arbitrary")),
        name=name,
    )(p, p, p)


def sb_bwd(p, dcat, W, name):
    S = p.shape[0]
    nh, nq = W // HD, S // HD
    TK = _key_strip(S)
    scale = HD ** -0.5

    def body(q_ref, k_ref, v_ref, do_ref, dp_ref, dk_acc, dv_acc, e_scr, z_scr):
        i = pl.program_id(1)
        q = q_ref[...]
        do = do_ref[...]
        su = (_iota2((HD, HD), 0) > _iota2((HD, HD), 1)).astype(BF16)
        sl = (_iota2((HD, HD), 0) < _iota2((HD, HD), 1)).astype(BF16)
        last = (i * HD) // TK

        @pl.when(i == 0)
        def _():
            dk_acc[...] = jnp.zeros_like(dk_acc)
            dv_acc[...] = jnp.zeros_like(dv_acc)

        def pass_a(gg, run):
            g = last - gg
            off = pl.multiple_of(g * TK, TK)
            z, _, a, run = _sb_strip(q, k_ref[pl.ds(off, TK), :], i, off, run, su)
            e_scr[g] = a * _dot_nt(do, v_ref[pl.ds(off, TK), :])
            z_scr[g] = z
            dv_acc[pl.ds(off, TK), :] += _dot_tn(a.astype(BF16), do)
            return run

        lax.fori_loop(0, last + 1, pass_a, jnp.zeros((HD, 1), F32))

        def pass_b(g, carry):
            dq, run_e = carry
            off = pl.multiple_of(g * TK, TK)
            e = e_scr[g]
            z = z_scr[g]
            mask = _strip_mask(z.shape, i, off, True)
            within = _block_sums(e, sl)
            before = []
            for b in range(len(within)):
                before.append(within[b] + run_e)
                run_e = run_e + jnp.sum(e[:, b * HD : (b + 1) * HD], axis=1, keepdims=True)
            sig = 1.0 / (1.0 + jnp.exp(-z))
            dz = jnp.where(mask, e * (1.0 - sig) - jnp.concatenate(before, axis=1) * sig, 0.0)
            dz = (dz * scale).astype(BF16)
            dq = dq + _dot(dz, k_ref[pl.ds(off, TK), :])
            dk_acc[pl.ds(off, TK), :] += _dot_tn(dz, q)
            return dq, run_e

        dq, _ = lax.fori_loop(0, last + 1, pass_b, (jnp.zeros((HD, HD), F32), jnp.zeros((HD, 1), F32)))
        dp_ref[0, pl.ds(pl.multiple_of(i * HD, HD), HD), :] = dq.astype(dp_ref.dtype)

        @pl.when(i == nq - 1)
        def _():
            dp_ref[1] = dk_acc[...].astype(dp_ref.dtype)
            dp_ref[2] = dv_acc[...].astype(dp_ref.dtype)

    return pl.pallas_call(
        body,
        grid=(nh, nq),
        in_specs=[
            pl.BlockSpec((HD, HD), lambda h, i: (i, h)),
            pl.BlockSpec((S, HD), lambda h, i: (0, nh + h)),
            pl.BlockSpec((S, HD), lambda h, i: (0, 2 * nh + h)),
            pl.BlockSpec((HD, HD), lambda h, i: (i, h)),
        ],
        out_specs=pl.BlockSpec((3, S, HD), lambda h, i: (0, 0, h)),
        out_shape=jax.ShapeDtypeStruct((6, S, W), BF16),
        scratch_shapes=[
            pltpu.VMEM((S, HD), F32),
            pltpu.VMEM((S, HD), F32),
            pltpu.VMEM((S // TK, HD, TK), F32),
            pltpu.VMEM((S // TK, HD, TK), F32),
        ],
        compiler_params=_cp(("parallel", "arbitrary")),
        name=name,
    )(p, p, p, dcat)


def fox_gate_fwd(f, b, name):
    S = f.shape[0]
    nq = S // HD

    def body(f_ref, b_ref, c_ref, run):
        i = pl.program_id(0)

        @pl.when(i == 0)
        def _():
            run[...] = jnp.zeros_like(run)

        lf = -_softplus(-(f_ref[...] + b_ref[...]))
        tri = (_iota2((HD, HD), 0) >= _iota2((HD, HD), 1)).astype(BF16)
        c_ref[...] = _dot_ones_left(tri, lf) + run[...]
        run[...] += jnp.sum(lf, axis=0, keepdims=True)

    return pl.pallas_call(
        body,
        grid=(nq,),
        in_specs=[pl.BlockSpec((HD, 128), lambda i: (i, 0)), pl.BlockSpec((1, 128), lambda i: (0, 0))],
        out_specs=pl.BlockSpec((HD, 128), lambda i: (i, 0)),
        out_shape=jax.ShapeDtypeStruct((S, 128), F32),
        scratch_shapes=[pltpu.VMEM((1, 128), F32)],
        compiler_params=_cp(("arbitrary",)),
        name=name,
    )(f, b)


def fox_gate_bwd(f, b, dc, name):
    S = f.shape[0]
    nq = S // HD

    def body(f_ref, b_ref, dc_ref, df_ref, db_ref, run):
        i = pl.program_id(0)

        @pl.when(i == 0)
        def _():
            run[...] = jnp.zeros_like(run)

        dc = dc_ref[...]
        tri = (_iota2((HD, HD), 0) <= _iota2((HD, HD), 1)).astype(BF16)
        dlf = _dot_ones_left(tri, dc) + run[...]
        run[...] += jnp.sum(dc, axis=0, keepdims=True)
        x = f_ref[...] + b_ref[...]
        df = dlf * (1.0 / (1.0 + jnp.exp(x)))
        df_ref[...] = df
        db = jnp.sum(df, axis=0, keepdims=True)

        @pl.when(i == 0)
        def _():
            db_ref[...] = db

        @pl.when(i > 0)
        def _():
            db_ref[...] += db

    rev = pl.BlockSpec((HD, 128), lambda i: (nq - 1 - i, 0))
    vec = pl.BlockSpec((1, 128), lambda i: (0, 0))
    return pl.pallas_call(
        body,
        grid=(nq,),
        in_specs=[rev, vec, rev],
        out_specs=[rev, vec],
        out_shape=[jax.ShapeDtypeStruct((S, 128), F32), jax.ShapeDtypeStruct((1, 128), F32)],
        scratch_shapes=[pltpu.VMEM((1, 128), F32)],
        compiler_params=_cp(("arbitrary",)),
        name=name,
    )(f, b, dc)


def _fox_logits(q, ks, ct, cs, i, off):
    s = _dot_nt(q, ks) * (HD ** -0.5) + (ct - cs)
    mask = _strip_mask(s.shape, i, off, False)
    return jnp.where(mask, s, -1e30), mask


def fox_fwd(p, ccol, crow, cat, W, name):
    S = p.shape[0]
    nh, nq = W // HD, S // HD
    TK = _key_strip(S)

    def body(q_ref, k_ref, v_ref, cc_ref, cr_ref, cat_ref, o_ref, lse_ref):
        i = pl.program_id(1)
        q = q_ref[...]
        ct = cc_ref[0]

        def step(g, carry):
            m, l, acc = carry
            off = pl.multiple_of(g * TK, TK)
            s, _ = _fox_logits(q, k_ref[pl.ds(off, TK), :], ct, cr_ref[0, pl.ds(g, 1), :], i, off)
            m_new = jnp.maximum(m, jnp.max(s, axis=1, keepdims=True))
            alpha = jnp.exp(m - m_new)
            pr = jnp.exp(s - m_new)
            l = alpha * l + jnp.sum(pr, axis=1, keepdims=True)
            acc = alpha * acc + _dot(pr.astype(BF16), v_ref[pl.ds(off, TK), :])
            return m_new, l, acc

        init = (jnp.full((HD, 1), -1e30, F32), jnp.zeros((HD, 1), F32), jnp.zeros((HD, HD), F32))
        m, l, acc = lax.fori_loop(0, (i * HD) // TK + 1, step, init)
        o_ref[...] = (acc / l).astype(o_ref.dtype)
        lse_ref[0] = m + jnp.log(l)

    return pl.pallas_call(
        body,
        grid=(nh, nq),
        in_specs=[
            pl.BlockSpec((HD, HD), lambda h, i: (i, 2 * nh + h)),
            pl.BlockSpec((S, HD), lambda h, i: (0, 3 * nh + h)),
            pl.BlockSpec((S, HD), lambda h, i: (0, 4 * nh + h)),
            pl.BlockSpec((1, HD, 1), lambda h, i: (h, i, 0)),
            pl.BlockSpec((1, S // TK, TK), lambda h, i: (h, 0, 0)),
            pl.BlockSpec(memory_space=pl.ANY),
        ],
        out_specs=[pl.BlockSpec((HD, HD), lambda h, i: (i, nh + h)), pl.BlockSpec((1, HD, 1), lambda h, i: (h, i, 0))],
        out_shape=[jax.ShapeDtypeStruct(cat.shape, cat.dtype), jax.ShapeDtypeStruct((nh, S, 1), F32)],
        input_output_aliases={5: 0},
        compiler_params=_cp(("parallel", "arbitrary")),
        name=name,
    )(p, p, p, ccol, crow, cat)


def fox_bwd(p, ccol, crow, cat, lse, dcat, dp, W, name):
    S = p.shape[0]
    nh, nq = W // HD, S // HD
    TK = _key_strip(S)
    scale = HD ** -0.5

    def body(q_ref, k_ref, v_ref, cc_ref, cr_ref, o_ref, lse_ref, do_ref, dp_in_ref, dp_ref, dcs_ref, dct_ref, dk_acc, dv_acc):
        i = pl.program_id(1)
        q = q_ref[...]
        do = do_ref[...]
        ct = cc_ref[0]
        lse_i = lse_ref[0]
        delta = jnp.sum(do.astype(F32) * o_ref[...].astype(F32), axis=1, keepdims=True)

        @pl.when(i == 0)
        def _():
            dk_acc[...] = jnp.zeros_like(dk_acc)
            dv_acc[...] = jnp.zeros_like(dv_acc)
            dcs_ref[...] = jnp.zeros_like(dcs_ref)

        def step(g, carry):
            dq, dct = carry
            off = pl.multiple_of(g * TK, TK)
            ks = k_ref[pl.ds(off, TK), :]
            s, mask = _fox_logits(q, ks, ct, cr_ref[0, pl.ds(g, 1), :], i, off)
            pr = jnp.where(mask, jnp.exp(s - lse_i), 0.0)
            ds = pr * (_dot_nt(do, v_ref[pl.ds(off, TK), :]) - delta)
            dv_acc[pl.ds(off, TK), :] += _dot_tn(pr.astype(BF16), do)
            dsb = (ds * scale).astype(BF16)
            dk_acc[pl.ds(off, TK), :] += _dot_tn(dsb, q)
            dcs_ref[0, pl.ds(g, 1), :] += jnp.sum(ds, axis=0, keepdims=True)
            return dq + _dot(dsb, ks), dct + jnp.sum(ds, axis=1, keepdims=True)

        dq, dct = lax.fori_loop(0, (i * HD) // TK + 1, step, (jnp.zeros((HD, HD), F32), jnp.zeros((HD, 1), F32)))
        dp_ref[0, pl.ds(pl.multiple_of(i * HD, HD), HD), :] = dq.astype(dp_ref.dtype)
        dct_ref[0] = dct

        @pl.when(i == nq - 1)
        def _():
            dp_ref[1] = dk_acc[...].astype(dp_ref.dtype)
            dp_ref[2] = dv_acc[...].astype(dp_ref.dtype)

    return pl.pallas_call(
        body,
        grid=(nh, nq),
        in_specs=[
            pl.BlockSpec((HD, HD), lambda h, i: (i, 2 * nh + h)),
            pl.BlockSpec((S, HD), lambda h, i: (0, 3 * nh + h)),
            pl.BlockSpec((S, HD), lambda h, i: (0, 4 * nh + h)),
            pl.BlockSpec((1, HD, 1), lambda h, i: (h, i, 0)),
            pl.BlockSpec((1, S // TK, TK), lambda h, i: (h, 0, 0)),
            pl.BlockSpec((HD, HD), lambda h, i: (i, nh + h)),
            pl.BlockSpec((1, HD, 1), lambda h, i: (h, i, 0)),
            pl.BlockSpec((HD, HD), lambda h, i: (i, nh + h)),
            pl.BlockSpec(memory_space=pl.ANY),
        ],
        out_specs=[
            pl.BlockSpec((3, S, HD), lambda h, i: (1, 0, h)),
            pl.BlockSpec((1, S // TK, TK), lambda h, i: (h, 0, 0)),
            pl.BlockSpec((1, HD, 1), lambda h, i: (h, i, 0)),
        ],
        out_shape=[
            jax.ShapeDtypeStruct(dp.shape, dp.dtype),
            jax.ShapeDtypeStruct((nh, S // TK, TK), F32),
            jax.ShapeDtypeStruct((nh, S, 1), F32),
        ],
        input_output_aliases={8: 0},
        scratch_shapes=[pltpu.VMEM((S, HD), F32), pltpu.VMEM((S, HD), F32)],
        compiler_params=_cp(("parallel", "arbitrary")),
        name=name,
    )(p, p, p, ccol, crow, cat, lse, dcat, dp)


_GELU_K = math.sqrt(2.0 / math.pi)
_GELU_C = 0.044715


def _gelu(x):
    return 0.5 * x * (1.0 + jnp.tanh(_GELU_K * (x + _GELU_C * x * x * x)))


def _gelu_grad(x):
    t = jnp.tanh(_GELU_K * (x + _GELU_C * x * x * x))
    return 0.5 * (1.0 + t) + 0.5 * x * (1.0 - t * t) * (_GELU_K * (1.0 + 3.0 * _GELU_C * x * x))


def _layernorm_parts(gv):
    xc = gv - jnp.mean(gv, axis=-1, keepdims=True)
    r = lax.rsqrt(jnp.mean(xc * xc, axis=-1, keepdims=True) + EPS)
    return xc * r, r


def sg_fwd(p, sg_w, sg_bt, sg_g, W, name):
    S = p.shape[0]
    G, nq = W // HD, S // HD

    def body(u_ref, v_ref, w_ref, bt_ref, g_ref, o_ref):
        xh, _ = _layernorm_parts(_gelu(v_ref[...].astype(F32)))
        vn = (xh * g_ref[...]).astype(BF16)
        tri = _iota2((HD, HD), 0) >= _iota2((HD, HD), 1)
        for gi in range(G):
            cols = slice(gi * HD, (gi + 1) * HD)
            wt = jnp.where(tri, w_ref[gi], 0.0).astype(BF16)
            mixed = _dot(wt, vn[:, cols]) + bt_ref[:, gi : gi + 1]
            o_ref[:, cols] = (_gelu(u_ref[:, cols].astype(F32)) * mixed).astype(o_ref.dtype)

    return pl.pallas_call(
        body,
        grid=(nq,),
        in_specs=[
            pl.BlockSpec((HD, W), lambda i: (i, 0)),
            pl.BlockSpec((HD, W), lambda i: (i, 1)),
            pl.BlockSpec((G, HD, HD), lambda i: (0, 0, 0)),
            pl.BlockSpec((HD, G), lambda i: (0, 0)),
            pl.BlockSpec((1, W), lambda i: (0, 0)),
        ],
        out_specs=pl.BlockSpec((HD, W), lambda i: (i, 0)),
        out_shape=jax.ShapeDtypeStruct((S, 2 * W), BF16),
        compiler_params=_cp(("parallel",)),
        name=name,
    )(p, p, sg_w, sg_bt, sg_g.reshape(1, W))


def sg_bwd(p, sg_w, sg_bt, sg_g, dcat, W, name):
    S = p.shape[0]
    G, nq = W // HD, S // HD

    def body(u_ref, v_ref, w_ref, bt_ref, g_ref, do_ref, dp_ref, dw_ref, dbt_ref, dg_ref, dvn_scr):
        i = pl.program_id(0)

        @pl.when(i == 0)
        def _():
            dw_ref[...] = jnp.zeros_like(dw_ref)
            dbt_ref[...] = jnp.zeros_like(dbt_ref)
            dg_ref[...] = jnp.zeros_like(dg_ref)

        v = v_ref[...].astype(F32)
        xh, r = _layernorm_parts(_gelu(v))
        gg = g_ref[...]
        vn = (xh * gg).astype(BF16)
        tri = _iota2((HD, HD), 0) >= _iota2((HD, HD), 1)
        for gi in range(G):
            cols = slice(gi * HD, (gi + 1) * HD)
            wt = jnp.where(tri, w_ref[gi], 0.0).astype(BF16)
            mixed = _dot(wt, vn[:, cols]) + bt_ref[:, gi : gi + 1]
            u = u_ref[:, cols].astype(F32)
            do = do_ref[:, cols].astype(F32)
            dp_ref[0, :, cols] = (do * mixed * _gelu_grad(u)).astype(dp_ref.dtype)
            dmix = do * _gelu(u)
            dmb = dmix.astype(BF16)
            dw_ref[gi] += jnp.where(tri, _dot_nt(dmb, vn[:, cols]), 0.0)
            dbt_ref[:, gi : gi + 1] += jnp.sum(dmix, axis=1, keepdims=True)
            dvn_scr[:, cols] = _dot_tn(wt, dmb)
        dvn = dvn_scr[...]
        dg_ref[...] += jnp.sum(dvn * xh, axis=0, keepdims=True)
        dxh = dvn * gg
        dgv = r * (dxh - jnp.mean(dxh, axis=-1, keepdims=True) - xh * jnp.mean(dxh * xh, axis=-1, keepdims=True))
        dp_ref[1] = (dgv * _gelu_grad(v)).astype(dp_ref.dtype)

    return pl.pallas_call(
        body,
        grid=(nq,),
        in_specs=[
            pl.BlockSpec((HD, W), lambda i: (i, 0)),
            pl.BlockSpec((HD, W), lambda i: (i, 1)),
            pl.BlockSpec((G, HD, HD), lambda i: (0, 0, 0)),
            pl.BlockSpec((HD, G), lambda i: (0, 0)),
            pl.BlockSpec((1, W), lambda i: (0, 0)),
            pl.BlockSpec((HD, W), lambda i: (i, 0)),
        ],
        out_specs=[
            pl.BlockSpec((2, HD, W), lambda i: (0, i, 0)),
            pl.BlockSpec((G, HD, HD), lambda i: (0, 0, 0)),
            pl.BlockSpec((HD, G), lambda i: (0, 0)),
            pl.BlockSpec((1, W), lambda i: (0, 0)),
        ],
        out_shape=[
            jax.ShapeDtypeStruct((6, S, W), BF16),
            jax.ShapeDtypeStruct((G, HD, HD), F32),
            jax.ShapeDtypeStruct((HD, G), F32),
            jax.ShapeDtypeStruct((1, W), F32),
        ],
        scratch_shapes=[pltpu.VMEM((HD, W), F32)],
        compiler_params=_cp(("arbitrary",)),
        name=name,
    )(p, p, sg_w, sg_bt, sg_g.reshape(1, W), dcat)


def local_step(x, target, wts, start_gather, on_grad):
    S, D = x.shape
    W = D // 2
    nb, F = wts["nb"], wts["F"]
    g = {}

    def ffn_fwd(xin, l, ahead=(None, None, None)):
        h = rms_fwd(xin, wts[f"{l}_ffn_norm_g"], f"{l}_ffn_rms")
        h = start_gather(ahead[0], h) if ahead[0] else h
        u = mm_nn(h, wts[f"{l}_ffn_up"], nb, f"{l}_ffn_up_mm")
        u = start_gather(ahead[1], u) if ahead[1] else u
        act = ffn_act_fwd(u, wts[f"{l}_ffn_conv_w"], F, f"{l}_ffn_act")
        act = start_gather(ahead[2], act) if ahead[2] else act
        xout = mm_nn(act, wts[f"{l}_ffn_down"], 1, f"{l}_ffn_down_mm", out_dtype=F32, res=xin)
        return xout, (xin, h, u, act)

    def ffn_bwd(dxout, dxoutb, saved, l):
        xin, h, u, act = saved
        dact = mm_nt(dxoutb, wts[f"{l}_ffn_down"], 1, S, F, f"{l}_ffn_down_dx")
        dact = on_grad(f"{l}_ffn_down", mm_tn(act, dxoutb, 1, D, f"{l}_ffn_down_dw"), dact)
        du, dcw = ffn_act_bwd(u, wts[f"{l}_ffn_conv_w"], dact, F, f"{l}_ffn_act_bwd")
        g[f"{l}_ffn_conv_w"] = jnp.concatenate([dcw[0], dcw[1]], axis=1)
        du2 = du.reshape(2 * S, F)
        n = wts[f"{l}_ffn_up"].shape[1]
        tn = _pick(n, (1408, 1024, 768, 512, 256, 128))
        per_half = F // tn
        nt = n // tn

        def up_block(i, j, t):
            vb = j * nt + t
            return vb // per_half, vb % per_half

        tm = _pick(S, (1024, 512, 256, 128))

        def nt_map(i, j, t):
            half, cb = up_block(i, j, t)
            return (half * (S // tm) + i, cb)

        def tn_map(j, t):
            half, cb = up_block(0, j, t)
            return (half, cb)

        dh = mm_nt(du2, wts[f"{l}_ffn_up"], nb, S, D, f"{l}_ffn_up_dx", dy_maps=[nt_map], tm=tm, tn=tn)
        dh = on_grad(f"{l}_ffn_up", mm_tn(h, du2, nb, n, f"{l}_ffn_up_dw", dy_maps=[tn_map], tn=tn), dh)
        dxin, dxinb, dg = rms_bwd(xin, wts[f"{l}_ffn_norm_g"], dh, dxout, f"{l}_ffn_rms_bwd")
        g[f"{l}_ffn_norm_g"] = dg
        return dxin, dxinb

    h0 = rms_fwd(x, wts["l0_mix_norm_g"], "l0_mix_rms")
    h0 = start_gather("l0_w_out", h0)
    p0 = mm_nn(h0, wts["l0_w_in"], nb, "l0_w_in_mm")
    p0 = start_gather("l0_ffn_up", p0)
    cat0 = sb_fwd(p0, W, "l0_sb_fwd")
    cat0 = sc_fwd(p0, wts["l0_sc_conv_w"], cat0, W, "l0_sc_fwd")
    cat0 = start_gather("l0_ffn_down", cat0)
    x1 = mm_nn(cat0, wts["l0_w_out"], 1, "l0_w_out_mm", out_dtype=F32, res=x)
    x2, ffn0_saved = ffn_fwd(x1, "l0", ahead=("l1_w_in", "l1_w_out", "l1_ffn_up"))
    x2 = start_gather("l1_ffn_down", x2)

    nh = W // HD
    h2 = rms_fwd(x2, wts["l1_mix_norm_g"], "l1_mix_rms")
    p1 = mm_nn(h2, wts["l1_w_in_main"], 1, "l1_w_in_mm")
    f = mm_nn(h2, wts["l1_w_in_f"], 1, "l1_w_f_mm", out_dtype=F32)
    bf = jnp.zeros((1, 128), F32).at[0, :nh].set(wts["l1_fox_b_f"])
    c = fox_gate_fwd(f, bf, "l1_fox_gate")
    c_heads = c[:, :nh].T
    ccol = c_heads[:, :, None]
    crow = c_heads.reshape(nh, S // _key_strip(S), _key_strip(S))
    sg_bt = wts["l1_sg_b"].T
    cat1 = sg_fwd(p1, wts["l1_sg_w"], sg_bt, wts["l1_sg_norm_g"], W, "l1_sg_fwd")
    cat1, lse = fox_fwd(p1, ccol, crow, cat1, W, "l1_fox_fwd")
    x3 = mm_nn(cat1, wts["l1_w_out"], 1, "l1_w_out_mm", out_dtype=F32, res=x2)
    x4, ffn1_saved = ffn_fwd(x3, "l1")

    dx4, dx4b, dgf, loss = loss_head(x4, wts["final_norm_g"], target, "loss_head")
    g["final_norm_g"] = dgf

    dx3, dx3b = ffn_bwd(dx4, dx4b, ffn1_saved, "l1")
    dcat1 = mm_nt(dx3b, wts["l1_w_out"], 1, S, D, "l1_w_out_dx")
    dcat1 = on_grad("l1_w_out", mm_tn(cat1, dx3b, 1, D, "l1_w_out_dw"), dcat1)
    dp1, dsgw, dsgbt, dsgg = sg_bwd(p1, wts["l1_sg_w"], sg_bt, wts["l1_sg_norm_g"], dcat1, W, "l1_sg_bwd")
    dp1, dcs, dct = fox_bwd(p1, ccol, crow, cat1, lse, dcat1, dp1, W, "l1_fox_bwd")
    g["l1_sg_w"], g["l1_sg_b"], g["l1_sg_norm_g"] = dsgw, dsgbt.T, dsgg
    dc = jnp.zeros((S, 128), F32).at[:, :nh].set((dct[:, :, 0] - dcs.reshape(nh, S)).T)
    df, dbf = fox_gate_bwd(f, bf, dc, "l1_fox_gate_bwd")
    g["l1_fox_b_f"] = dbf[0, :nh]
    dfb = df.astype(BF16)
    tn1 = _pick(W, (1024, 512, 256, 128))
    tm1 = _pick(S, (1024, 512, 256, 128))
    per_part = W // tn1
    part_of = lambda pt: pt + pt // 2 - pt // 4

    def nt_map1(i, j, t):
        return (part_of(t // per_part) * (S // tm1) + i, t % per_part)

    def tn_map1(j, t):
        return (part_of(t // per_part), t % per_part)

    dp1_2d = dp1.reshape(6 * S, W)
    dw_main = mm_tn(h2, dp1_2d, 1, 5 * W, "l1_w_in_dw", dy_maps=[tn_map1], tn=tn1)
    dw_f = mm_tn(h2, dfb, 1, 128, "l1_w_f_dw")
    dh2 = mm_nt(dfb, wts["l1_w_in_f"], 1, S, D, "l1_w_f_dx", out_dtype=F32)
    dh2 = mm_nt(dp1_2d, wts["l1_w_in_main"], 1, S, D, "l1_w_in_dx", res=dh2, dy_maps=[nt_map1], tm=tm1, tn=tn1)
    dh2 = on_grad("l1_w_in", jnp.concatenate([dw_main, dw_f[:, :nh]], axis=1), dh2)
    dx2, dx2b, dg = rms_bwd(x2, wts["l1_mix_norm_g"], dh2, dx3, "l1_mix_rms_bwd")
    g["l1_mix_norm_g"] = dg

    dx1, dx1b = ffn_bwd(dx2, dx2b, ffn0_saved, "l0")
    dcat0 = mm_nt(dx1b, wts["l0_w_out"], 1, S, D, "l0_w_out_dx")
    dcat0 = on_grad("l0_w_out", mm_tn(cat0, dx1b, 1, D, "l0_w_out_dw"), dcat0)
    dp0 = sb_bwd(p0, dcat0, W, "l0_sb_bwd")
    dp0, dscw = sc_bwd(p0, wts["l0_sc_conv_w"], dcat0, dp0, W, "l0_sc_bwd")
    g["l0_sc_conv_w"] = dscw
    n0 = wts["l0_w_in"].shape[1]
    td0 = math.gcd(n0, W)
    nd0 = n0 // td0
    tm0 = _pick(S, (1024, 512, 256, 128))
    per_part0 = W // td0

    def nt_maps0(k):
        def f(i, j, t):
            vb = j * nd0 + k
            return ((vb // per_part0) * (S // tm0) + i, vb % per_part0)
        return f

    def tn_maps0(k):
        def f(j, t):
            vb = j * nd0 + k
            return (vb // per_part0, vb % per_part0)
        return f

    dp0_2d = dp0.reshape(6 * S, W)
    dw0 = mm_tn(h0, dp0_2d, nb, n0, "l0_w_in_dw", dy_maps=[tn_maps0(k) for k in range(nd0)], tn=n0)
    dp0_2d = on_grad("l0_w_in", dw0, dp0_2d)
    dh0 = mm_nt(dp0_2d, wts["l0_w_in"], nb, S, D, "l0_w_in_dx", dy_maps=[nt_maps0(k) for k in range(nd0)], tm=tm0, tn=n0)
    dx0, _, dg = rms_bwd(x, wts["l0_mix_norm_g"], dh0, dx1, "l0_mix_rms_bwd")
    g["l0_mix_norm_g"] = dg
    return loss, dx0, g


GATHER_ID, PAIR_ID, CHIPS_ID = 1, 2, 3


def _place():
    return lax.axis_index("x"), lax.axis_index("y"), lax.axis_index("c")


def _other_chips(x, y):
    return [(x, 1 - y), (1 - x, y), (1 - x, 1 - y)]


def _handshake(peers):
    barrier = pltpu.get_barrier_semaphore()
    for peer in peers:
        pl.semaphore_signal(barrier, inc=1, device_id=peer, device_id_type=MESH)
    pl.semaphore_wait(barrier, len(peers))


UPDATE_LAG = 2


def _on_sequencer(body, out_type, scratch_types, collective_id, name):
    return pl.kernel(
        body,
        out_type=out_type,
        mesh=plsc.ScalarSubcoreMesh(axis_name="seq", num_cores=1),
        scratch_types=scratch_types,
        compiler_params=pltpu.CompilerParams(collective_id=collective_id),
        name=name,
    )


def all_gather(arrs, name):
    n = len(arrs)

    def body(*refs):
        xs, outs = refs[:n], refs[n : 2 * n]
        send_sems, recv_sems, local_sems = refs[2 * n :]
        x, y, c = _place()
        me, sibling = (x, y, c), (x, y, 1 - c)
        chips = _other_chips(x, y)
        _handshake([sibling] + [(*chip, c) for chip in chips])

        def copy(a, k, block, to, src=None):
            px, py, pc = block
            dst = outs[a].at[4 * px + 2 * py + pc]
            return pltpu.make_async_remote_copy(
                src_ref=dst if src is None else src, dst_ref=dst,
                send_sem=send_sems.at[7 * a + k], recv_sem=recv_sems.at[7 * a + k], device_id=to, device_id_type=MESH,
            )

        mine = [pltpu.make_async_copy(xs[a], outs[a].at[4 * x + 2 * y + c], local_sems.at[a]) for a in range(n)]
        for cp in mine:
            cp.start()
        first = []
        for a in range(n):
            first.append(copy(a, 0, me, sibling, src=xs[a]))
            first += [copy(a, 1 + j, me, (*chip, c), src=xs[a]) for j, chip in enumerate(chips)]
        for cp in first:
            cp.start()
        passed = []
        for a in range(n):
            for j, chip in enumerate(chips):
                copy(a, 1 + j, (*chip, c), me).wait_recv()
                cp = copy(a, 4 + j, (*chip, c), sibling)
                cp.start()
                passed.append(cp)
        for a in range(n):
            copy(a, 0, sibling, me).wait_recv()
            for j, chip in enumerate(chips):
                copy(a, 4 + j, (*chip, 1 - c), me).wait_recv()
        for cp in first + passed:
            cp.wait_send()
        for cp in mine:
            cp.wait()

    out_type = [jax.ShapeDtypeStruct((NDEV,) + a.shape, a.dtype) for a in arrs]
    sems = [pltpu.SemaphoreType.DMA((7 * n,)), pltpu.SemaphoreType.DMA((7 * n,)), pltpu.SemaphoreType.DMA((n,))]
    return _on_sequencer(body, out_type, sems, GATHER_ID, name)(*arrs)


_IN_HBM = pl.BlockSpec(memory_space=pltpu.HBM)
_IN_SEM = pl.BlockSpec(memory_space=pltpu.SEMAPHORE)
_EFFECT = pltpu.SideEffectType.DATAFLOW_SIDE_EFFECTING


def _split_start(make_copies, src, land_shape, nsem, name):
    def body(src_ref, land_ref, send_sems, recv_sems, src_thru, land_thru, token):
        for cp in make_copies(src_ref, land_ref, send_sems, recv_sems):
            cp.start()
        token[...] = jnp.zeros_like(token)

    return pl.pallas_call(
        body,
        name=name,
        out_shape=(
            pltpu.SemaphoreType.DMA((nsem,)), pltpu.SemaphoreType.DMA((nsem,)),
            pltpu.HBM(src.shape, src.dtype), pltpu.HBM(land_shape, src.dtype), jax.ShapeDtypeStruct((8, 128), F32),
        ),
        in_specs=(_IN_HBM, _IN_HBM),
        out_specs=(_IN_SEM, _IN_SEM, _IN_HBM, _IN_HBM, pl.BlockSpec(memory_space=pltpu.VMEM)),
        input_output_aliases={0: 2, 1: 3},
        compiler_params=pltpu.CompilerParams(has_side_effects=_EFFECT),
    )(pltpu.with_memory_space_constraint(src, pltpu.HBM), pltpu.with_memory_space_constraint(lax.empty(land_shape, src.dtype), pltpu.HBM))


def _split_wait(make_copies, send_sems, recv_sems, src_thru, land_thru, after, name):
    def body(src_ref, land_ref, send_sems, recv_sems, after_ref, src_dead, land_out):
        for cp in make_copies(src_ref, land_ref, send_sems, recv_sems):
            cp.wait_send()
            cp.wait_recv()

    return pl.pallas_call(
        body,
        name=name,
        out_shape=(pltpu.HBM(src_thru.shape, src_thru.dtype), pltpu.HBM(land_thru.shape, land_thru.dtype)),
        in_specs=(_IN_HBM, _IN_HBM, _IN_SEM, _IN_SEM, pl.BlockSpec(memory_space=pl.ANY)),
        out_specs=(_IN_HBM, _IN_HBM),
        input_output_aliases={0: 0, 1: 1},
        compiler_params=pltpu.CompilerParams(has_side_effects=_EFFECT),
    )(src_thru, land_thru, send_sems, recv_sems, after)[1]


def _pair_copies(src_ref, land_ref, send_sems, recv_sems):
    x, y, c = _place()
    return [
        pltpu.make_async_remote_copy(
            src_ref=src_ref.at[k, 1 - c], dst_ref=land_ref.at[k],
            send_sem=send_sems.at[k], recv_sem=recv_sems.at[k], device_id=(x, y, 1 - c), device_id_type=MESH,
        )
        for k in range(4)
    ]


def _chip_copies(src_ref, land_ref, send_sems, recv_sems):
    x, y, c = _place()
    return [
        pltpu.make_async_remote_copy(
            src_ref=src_ref.at[2 * px + py], dst_ref=land_ref.at[2 * x + y],
            send_sem=send_sems.at[j], recv_sem=recv_sems.at[j], device_id=(px, py, c), device_id_type=MESH,
        )
        for j, (px, py) in enumerate(_other_chips(x, y))
    ]


def _row_tile(R, C, max_elems):
    if R * C <= max_elems:
        return R
    best = None
    for tr in range(16, R, 16):
        if R % tr == 0 and tr * C <= max_elems:
            best = tr
    return best or R


def pair_sum(a42, land4, core, name):
    _, _, R, C = a42.shape
    tr = _row_tile(R, C, 1 << 20)

    def body(core_ref, a_ref, l_ref, o_ref):
        o_ref[...] = (a_ref[0].astype(F32) + l_ref[...].astype(F32)).astype(o_ref.dtype)

    return pl.pallas_call(
        body,
        grid_spec=pltpu.PrefetchScalarGridSpec(
            num_scalar_prefetch=1,
            grid=(4, R // tr),
            in_specs=[
                pl.BlockSpec((1, 1, tr, C), lambda k, r, core_ref: (k, core_ref[0], r, 0)),
                pl.BlockSpec((1, tr, C), lambda k, r, core_ref: (k, r, 0)),
            ],
            out_specs=pl.BlockSpec((1, tr, C), lambda k, r, core_ref: (k, r, 0)),
        ),
        out_shape=jax.ShapeDtypeStruct((4, R, C), BF16),
        compiler_params=_cp(("parallel", "parallel")),
        name=name,
    )(core, a42, land4)


def sum_slots(parts, name):
    P, R, C = parts.shape

    def body(p_ref, o_ref):
        acc = p_ref[0].astype(F32)
        for k in range(1, P):
            acc = acc + p_ref[k].astype(F32)
        o_ref[...] = acc

    tr = _row_tile(R, P * C, 1 << 21)
    return pl.pallas_call(
        body,
        grid=(R // tr,),
        in_specs=[pl.BlockSpec((P, tr, C), lambda r: (0, r, 0))],
        out_specs=pl.BlockSpec((tr, C), lambda r: (r, 0)),
        out_shape=jax.ShapeDtypeStruct((R, C), F32),
        compiler_params=_cp(("parallel",)),
        name=name,
    )(parts)


def adamw(w, m, v, parts, name):
    R, C = w.shape
    P = parts.shape[0]
    tr = _pick(R, (256, 128, 64, 32, 16, 8))
    c1 = 1.0 - ADAM_B1 ** ADAM_STEP
    c2 = 1.0 - ADAM_B2 ** ADAM_STEP

    def body(w_ref, m_ref, v_ref, p_ref, g_ref, d_ref, nm_ref, nv_ref):
        g = p_ref[0].astype(F32)
        for k in range(1, P):
            g = g + p_ref[k].astype(F32)
        nm = ADAM_B1 * m_ref[...] + (1.0 - ADAM_B1) * g
        nv = ADAM_B2 * v_ref[...] + (1.0 - ADAM_B2) * (g * g)
        g_ref[...] = g
        nm_ref[...] = nm
        nv_ref[...] = nv
        d_ref[...] = -ADAM_LR * ((nm / c1) / (jnp.sqrt(nv / c2) + ADAM_EPS) + ADAM_WD * w_ref[...])

    blk = pl.BlockSpec((tr, C), lambda r: (r, 0))
    shp = jax.ShapeDtypeStruct((R, C), F32)
    return pl.pallas_call(
        body,
        grid=(R // tr,),
        in_specs=[blk, blk, blk, pl.BlockSpec((P, tr, C), lambda r: (0, r, 0))],
        out_specs=[blk, blk, blk, blk],
        out_shape=[shp, shp, shp, shp],
        compiler_params=_cp(("parallel",)),
        name=name,
    )(w, m, v, parts)


def adamw_reduced(w, m, v, own, land, chip, name):
    R, C = w.shape
    tr = _pick(R, (256, 128, 64, 32, 16, 8))
    c1 = 1.0 - ADAM_B1 ** ADAM_STEP
    c2 = 1.0 - ADAM_B2 ** ADAM_STEP

    def body(chip_ref, w_ref, m_ref, v_ref, own_ref, land_ref, g_ref, d_ref, nm_ref, nv_ref):
        mine = own_ref[0].astype(F32)
        g = None
        for k in range(4):
            term = jnp.where(chip_ref[0] == k, mine, land_ref[k].astype(F32))
            g = term if g is None else g + term
        nm = ADAM_B1 * m_ref[...] + (1.0 - ADAM_B1) * g
        nv = ADAM_B2 * v_ref[...] + (1.0 - ADAM_B2) * (g * g)
        g_ref[...] = g
        nm_ref[...] = nm
        nv_ref[...] = nv
        d_ref[...] = -ADAM_LR * ((nm / c1) / (jnp.sqrt(nv / c2) + ADAM_EPS) + ADAM_WD * w_ref[...])

    blk = pl.BlockSpec((tr, C), lambda r, chip_ref: (r, 0))
    shp = jax.ShapeDtypeStruct((R, C), F32)
    return pl.pallas_call(
        body,
        grid_spec=pltpu.PrefetchScalarGridSpec(
            num_scalar_prefetch=1,
            grid=(R // tr,),
            in_specs=[
                blk, blk, blk,
                pl.BlockSpec((1, tr, C), lambda r, chip_ref: (chip_ref[0], r, 0)),
                pl.BlockSpec((4, tr, C), lambda r, chip_ref: (0, r, 0)),
            ],
            out_specs=[blk, blk, blk, blk],
        ),
        out_shape=[shp, shp, shp, shp],
        compiler_params=_cp(("parallel",)),
        name=name,
    )(chip, w, m, v, own, land)


_WEIGHTS = [
    "l0_mix_norm_g", "l0_w_in", "l0_sc_conv_w", "l0_w_out", "l0_ffn_norm_g", "l0_ffn_up", "l0_ffn_conv_w", "l0_ffn_down",
    "l1_mix_norm_g", "l1_w_in", "l1_fox_b_f", "l1_sg_w", "l1_sg_b", "l1_sg_norm_g", "l1_w_out", "l1_ffn_norm_g",
    "l1_ffn_up", "l1_ffn_conv_w", "l1_ffn_down", "final_norm_g",
]
_COL_SHARDED = ["l0_w_in", "l0_ffn_up", "l1_w_in", "l1_ffn_up"]
_ROW_SHARDED = ["l0_w_out", "l0_ffn_down", "l1_w_out", "l1_ffn_down"]
_BIG = ["l0_w_in", "l0_w_out", "l0_ffn_up", "l0_ffn_down", "l1_w_in", "l1_w_out", "l1_ffn_up", "l1_ffn_down"]
_CONV = ["l0_sc_conv_w", "l0_ffn_conv_w", "l1_ffn_conv_w"]
_SMALL = [n for n in _WEIGHTS if n not in _BIG]
_PACK_ROWS = 8


def _pack(arrs):
    flat = []
    for a in arrs:
        v = a.reshape(-1).astype(F32)
        pad = (-v.shape[0]) % (_PACK_ROWS * 128)
        flat.append(jnp.pad(v, (0, pad)))
    return jnp.concatenate(flat).reshape(-1, 128)


def _unpack(packed, shapes):
    out, off = [], 0
    flat = packed.reshape(-1)
    for shp in shapes:
        size = math.prod(shp)
        out.append(flat[off : off + size].reshape(shp))
        off += size + (-size) % (_PACK_ROWS * 128)
    return out


def kernel(x, l0_mix_norm_g, l0_w_in, l0_sc_conv_w, l0_w_out, l0_ffn_norm_g, l0_ffn_up, l0_ffn_conv_w, l0_ffn_down, l1_mix_norm_g, l1_w_in, l1_fox_b_f, l1_sg_w, l1_sg_b, l1_sg_norm_g, l1_w_out, l1_ffn_norm_g, l1_ffn_up, l1_ffn_conv_w, l1_ffn_down, final_norm_g, loss_target, m_l0_mix_norm_g, m_l0_w_in, m_l0_sc_conv_w, m_l0_w_out, m_l0_ffn_norm_g, m_l0_ffn_up, m_l0_ffn_conv_w, m_l0_ffn_down, m_l1_mix_norm_g, m_l1_w_in, m_l1_fox_b_f, m_l1_sg_w, m_l1_sg_b, m_l1_sg_norm_g, m_l1_w_out, m_l1_ffn_norm_g, m_l1_ffn_up, m_l1_ffn_conv_w, m_l1_ffn_down, m_final_norm_g, v_l0_mix_norm_g, v_l0_w_in, v_l0_sc_conv_w, v_l0_w_out, v_l0_ffn_norm_g, v_l0_ffn_up, v_l0_ffn_conv_w, v_l0_ffn_down, v_l1_mix_norm_g, v_l1_w_in, v_l1_fox_b_f, v_l1_sg_w, v_l1_sg_b, v_l1_sg_norm_g, v_l1_w_out, v_l1_ffn_norm_g, v_l1_ffn_up, v_l1_ffn_conv_w, v_l1_ffn_down, v_final_norm_g):
    given = dict(locals())
    w = {n: given[n] for n in _WEIGHTS}
    mom = {n: given["m_" + n] for n in _WEIGHTS}
    var = {n: given["v_" + n] for n in _WEIGHTS}
    xs, target = x[0], loss_target[0]
    S, D = xs.shape
    W = D // 2
    nh = W // HD
    cx, cy, cc = _place()
    me = 4 * cx + 2 * cy + cc

    wts = {"nb": NDEV, "F": l0_ffn_down.shape[0] * NDEV}
    for n in _SMALL:
        if n not in _CONV:
            wts[n] = w[n]

    def start_gather(n, after=None):
        src = [w[n].astype(BF16)] + ([w[c] for c in _CONV] if n == _BIG[0] else [])
        if after is not None:
            src, after = lax.optimization_barrier((src, after))
        got = all_gather(src, f"gather_{n}")
        if n == "l1_w_in":
            w_in1 = got[0].transpose(1, 0, 2).reshape(D, -1)
            wts["l1_w_in_main"] = w_in1[:, : 5 * W]
            wts["l1_w_in_f"] = jnp.pad(w_in1[:, 5 * W :], ((0, 0), (0, 128 - nh)))
        elif n in _ROW_SHARDED:
            wts[n] = got[0].reshape(-1, D)
        else:
            wts[n] = got[0].reshape(NDEV * D, -1)
        for c, taps in zip(_CONV, got[1:]):
            wts[c] = taps.transpose(1, 0, 2).reshape(CONV_K, -1)
        return after

    core = jnp.reshape(cc, (1,)).astype(jnp.int32)
    chip = jnp.reshape(2 * cx + cy, (1,)).astype(jnp.int32)
    pair_flying, chip_flying = [], []
    out_g, out_d, out_m, out_v = {}, {}, {}, {}

    def tie(value, after):
        if after is None:
            return value, None
        return lax.optimization_barrier((value, after))

    def to_chips(after):
        n, flying = pair_flying.pop()
        landed = _split_wait(_pair_copies, *flying, f"reduce_pair_wait_{n}")
        summed = pair_sum(flying[2], landed, core, f"pair_sum_{n}")
        *flying, token = _split_start(_chip_copies, summed, summed.shape, 3, f"reduce_chips_{n}")
        token, after = tie(token, after)
        chip_flying.append((n, flying + [token]))
        return after

    def update(after):
        n, flying = chip_flying.pop(0)
        landed = _split_wait(_chip_copies, *flying, f"reduce_chips_wait_{n}")
        res = adamw_reduced(w[n], mom[n], var[n], flying[2], landed, chip, f"adamw_{n}")
        res, after = tie(res, after)
        out_g[n], out_d[n], out_m[n], out_v[n] = res
        return after

    def on_grad(n, term, after):
        if n == "l1_w_in":
            term = term.reshape(D, NDEV, -1).transpose(1, 0, 2)
        elif n in _ROW_SHARDED:
            term = term.reshape(NDEV, -1, D)
        else:
            term = term.reshape(NDEV, D, -1)
        term = term.reshape((4, 2) + term.shape[1:])
        *flying, token = _split_start(_pair_copies, term, term.shape[:1] + term.shape[2:], 4, f"reduce_pair_{n}")
        token, after = tie(token, after)
        if len(chip_flying) == UPDATE_LAG:
            after = update(after)
        if pair_flying:
            after = to_chips(after)
        pair_flying.append((n, flying + [token]))
        return after

    for n in _BIG:
        start_gather(n)
    loss_tile, dx, g = local_step(xs, target, wts, lambda n, after: after, on_grad)
    to_chips(None)
    while chip_flying:
        update(None)
    loss = lax.psum(loss_tile[0, 0], ("x", "y", "c"))

    small_terms = [g[n] for n in _SMALL]
    small_shapes = [tuple(t.shape) for t in small_terms]
    packed = _pack(small_terms)
    all_terms = all_gather([packed], "gather_small_grads")[0]
    small_sum = _unpack(sum_slots(all_terms, "sum_small_grads"), small_shapes)
    small_g = {}
    for n, t in zip(_SMALL, small_sum):
        if n in _CONV:
            cols = w[n].shape[1]
            t = lax.dynamic_slice_in_dim(t, me * cols, cols, axis=1)
        small_g[n] = t.reshape(w[n].shape)
    shapes = [w[n].shape for n in _SMALL]
    res = adamw(
        _pack([w[n] for n in _SMALL]), _pack([mom[n] for n in _SMALL]), _pack([var[n] for n in _SMALL]),
        _pack([small_g[n] for n in _SMALL])[None], "adamw_small",
    )
    for dst, packed_out in zip((out_g, out_d, out_m, out_v), res):
        for n, t in zip(_SMALL, _unpack(packed_out, shapes)):
            dst[n] = t

    return (loss, dx[None], *[out_g[n] for n in _WEIGHTS], *[out_d[n] for n in _WEIGHTS],
            *[out_m[n] for n in _WEIGHTS], *[out_v[n] for n in _WEIGHTS])
```

```python
import functools
import math

import jax
import jax.numpy as jnp
from jax import lax
from jax.experimental import pallas as pl
from jax.experimental.pallas import tpu as pltpu
from jax.experimental.pallas import tpu_sc as plsc

F32 = jnp.float32
BF16 = jnp.bfloat16
HD = 128
EPS = 1e-6
CONV_K = 3
VMEM_LIMIT_BYTES = 48 << 20
NDEV = 8
MESH = pl.DeviceIdType.MESH

ADAM_LR = 0.001
ADAM_B1 = 0.9
ADAM_B2 = 0.999
ADAM_EPS = 1e-08
ADAM_WD = 0.01
ADAM_STEP = 10


def _cp(sem):
    return pltpu.CompilerParams(dimension_semantics=sem, vmem_limit_bytes=VMEM_LIMIT_BYTES)


def _pick(n, prefs):
    for p in prefs:
        if n % p == 0:
            return p
    return n


def _dot(a, b):
    return jnp.dot(a, b, preferred_element_type=F32)


def _dot_nt(a, b):
    return lax.dot_general(a, b, (((1,), (1,)), ((), ())), preferred_element_type=F32)


def _dot_tn(a, b):
    return lax.dot_general(a, b, (((0,), (0,)), ((), ())), preferred_element_type=F32)


def _split3(x):
    hi = x.astype(BF16)
    r = x - hi.astype(F32)
    mid = r.astype(BF16)
    lo = (r - mid.astype(F32)).astype(BF16)
    return hi, mid, lo


def _dot_ones_right(x, ones_bf16):
    hi, mid, lo = _split3(x)
    return _dot(hi, ones_bf16) + _dot(mid, ones_bf16) + _dot(lo, ones_bf16)


def _dot_ones_left(ones_bf16, x):
    hi, mid, lo = _split3(x)
    return _dot(ones_bf16, hi) + _dot(ones_bf16, mid) + _dot(ones_bf16, lo)


def _iota2(shape, axis):
    return lax.broadcasted_iota(jnp.int32, shape, axis)


def mm_nn(a, w2d, nb, name, out_dtype=BF16, res=None, tm=None, tn=None, tk=None):
    M, K = a.shape
    n = w2d.shape[1]
    assert w2d.shape[0] == nb * K
    tm = tm or _pick(M, (1024, 512, 256, 128))
    tn = tn or _pick(n, (1408, 1024, 768, 512, 256, 128))
    tk = tk or (K if K <= 2048 else _pick(K, (1408, 1024, 512, 256, 128)))
    nk, nt = K // tk, n // tn
    has_res = res is not None

    def body(*refs):
        if has_res:
            a_ref, w_ref, r_ref, o_ref = refs[:4]
        else:
            a_ref, w_ref, o_ref = refs[:3]
            r_ref = None
        part = _dot(a_ref[...], w_ref[...])

        def finish(acc):
            if r_ref is not None:
                acc = acc + r_ref[...].astype(F32)
            o_ref[...] = acc.astype(o_ref.dtype)

        if nk == 1:
            finish(part)
        else:
            acc_ref = refs[-1]
            k = pl.program_id(3)

            @pl.when(k == 0)
            def _():
                acc_ref[...] = part

            @pl.when(k > 0)
            def _():
                acc_ref[...] += part

            @pl.when(k == nk - 1)
            def _():
                finish(acc_ref[...])

    in_specs = [
        pl.BlockSpec((tm, tk), lambda i, j, t, k: (i, k)),
        pl.BlockSpec((tk, tn), lambda i, j, t, k: (j * nk + k, t)),
    ]
    args = [a, w2d]
    out_spec = pl.BlockSpec((tm, tn), lambda i, j, t, k: (i, j * nt + t))
    if has_res:
        in_specs.append(out_spec)
        args.append(res)
    return pl.pallas_call(
        body,
        grid=(M // tm, nb, nt, nk),
        in_specs=in_specs,
        out_specs=out_spec,
        out_shape=jax.ShapeDtypeStruct((M, nb * n), out_dtype),
        scratch_shapes=[pltpu.VMEM((tm, tn), F32)] if nk > 1 else [],
        compiler_params=_cp(("parallel", "parallel", "parallel", "arbitrary")),
        name=name,
    )(*args)


def mm_nt(dy2d, w2d, nb, M, K, name, out_dtype=BF16, res=None, dy_maps=None, tm=None, tko=None, tn=None):
    n = w2d.shape[1]
    assert w2d.shape[0] == nb * K
    tm = tm or _pick(M, (1024, 512, 256, 128))
    tko = tko or _pick(K, (1024, 512, 256, 128))
    tn = tn or _pick(n, (1408, 1024, 768, 512, 256, 128))
    nt, nko = n // tn, K // tko
    has_res = res is not None
    if dy_maps is None:
        dy_maps = [lambda i, j, t: (i, j * nt + t)]
    nd = len(dy_maps)
    td = tn // nd

    def body(*refs):
        d_refs, w_ref = refs[:nd], refs[nd]
        r_ref = refs[nd + 1] if has_res else None
        o_ref, acc_ref = refs[-2], refs[-1]
        j, t = pl.program_id(2), pl.program_id(3)
        d = d_refs[0][...] if nd == 1 else jnp.concatenate([r[...] for r in d_refs], axis=1)
        part = _dot_nt(d, w_ref[...])
        first = jnp.logical_and(j == 0, t == 0)
        last = jnp.logical_and(j == nb - 1, t == nt - 1)

        @pl.when(first)
        def _():
            acc_ref[...] = part

        @pl.when(jnp.logical_not(first))
        def _():
            acc_ref[...] += part

        @pl.when(last)
        def _():
            acc = acc_ref[...]
            if r_ref is not None:
                acc = acc + r_ref[...].astype(F32)
            o_ref[...] = acc.astype(o_ref.dtype)

    in_specs = [pl.BlockSpec((tm, td), functools.partial(lambda f, i, ko, j, t: f(i, j, t), f)) for f in dy_maps]
    in_specs.append(pl.BlockSpec((tko, tn), lambda i, ko, j, t: (j * nko + ko, t)))
    args = [dy2d] * nd + [w2d]
    out_spec = pl.BlockSpec((tm, tko), lambda i, ko, j, t: (i, ko))
    if has_res:
        in_specs.append(out_spec)
        args.append(res)
    return pl.pallas_call(
        body,
        grid=(M // tm, nko, nb, nt),
        in_specs=in_specs,
        out_specs=out_spec,
        out_shape=jax.ShapeDtypeStruct((M, K), out_dtype),
        scratch_shapes=[pltpu.VMEM((tm, tko), F32)],
        compiler_params=_cp(("parallel", "parallel", "arbitrary", "arbitrary")),
        name=name,
    )(*args)


def mm_tn(x, dy2d, nb, n, name, out_dtype=BF16, dy_maps=None, tko=None, tn=None):
    S, K = x.shape
    tko = tko or _pick(K, (512, 256, 128))
    tn = tn or _pick(n, (1408, 1024, 768, 512, 256, 128))
    nt, nko = n // tn, K // tko
    if dy_maps is None:
        dy_maps = [lambda j, t: (0, j * nt + t)]
    nd = len(dy_maps)
    td = tn // nd

    def body(*refs):
        x_ref, d_refs, o_ref = refs[0], refs[1 : 1 + nd], refs[-1]
        d = d_refs[0][...] if nd == 1 else jnp.concatenate([r[...] for r in d_refs], axis=1)
        o_ref[...] = _dot_tn(x_ref[...], d).astype(o_ref.dtype)

    in_specs = [pl.BlockSpec((S, tko), lambda ko, j, t: (0, ko))]
    in_specs += [pl.BlockSpec((S, td), functools.partial(lambda f, ko, j, t: f(j, t), f)) for f in dy_maps]
    return pl.pallas_call(
        body,
        grid=(nko, nb, nt),
        in_specs=in_specs,
        out_specs=pl.BlockSpec((tko, tn), lambda ko, j, t: (j * nko + ko, t)),
        out_shape=jax.ShapeDtypeStruct((nb * K, n), out_dtype),
        compiler_params=_cp(("parallel", "parallel", "parallel")),
        name=name,
    )(x, *([dy2d] * nd))


def rms_fwd(x, g, name):
    S, D = x.shape
    tm = _pick(S, (256, 128))

    def body(x_ref, g_ref, o_ref):
        xf = x_ref[...]
        r = lax.rsqrt(jnp.mean(xf * xf, axis=-1, keepdims=True) + EPS)
        o_ref[...] = (xf * r * g_ref[...]).astype(o_ref.dtype)

    return pl.pallas_call(
        body,
        grid=(S // tm,),
        in_specs=[pl.BlockSpec((tm, D), lambda i: (i, 0)), pl.BlockSpec((1, D), lambda i: (0, 0))],
        out_specs=pl.BlockSpec((tm, D), lambda i: (i, 0)),
        out_shape=jax.ShapeDtypeStruct((S, D), BF16),
        compiler_params=_cp(("parallel",)),
        name=name,
    )(x, g.reshape(1, D))


def rms_bwd(x, g, dh, dres, name):
    S, D = x.shape
    tm = _pick(S, (256, 128))

    def body(x_ref, g_ref, dh_ref, dr_ref, dx_ref, dxb_ref, dg_ref):
        i = pl.program_id(0)
        xf = x_ref[...]
        dh = dh_ref[...].astype(F32)
        r = lax.rsqrt(jnp.mean(xf * xf, axis=-1, keepdims=True) + EPS)
        gy = dh * g_ref[...]
        proj = jnp.mean(gy * xf, axis=-1, keepdims=True)
        dx = dr_ref[...] + r * gy - xf * (r * r * r * proj)
        dx_ref[...] = dx
        dxb_ref[...] = dx.astype(BF16)
        dg = jnp.sum(dh * (xf * r), axis=0, keepdims=True)

        @pl.when(i == 0)
        def _():
            dg_ref[...] = dg

        @pl.when(i > 0)
        def _():
            dg_ref[...] += dg

    row = pl.BlockSpec((tm, D), lambda i: (i, 0))
    vec = pl.BlockSpec((1, D), lambda i: (0, 0))
    return pl.pallas_call(
        body,
        grid=(S // tm,),
        in_specs=[row, vec, row, row],
        out_specs=[row, row, vec],
        out_shape=[jax.ShapeDtypeStruct((S, D), F32), jax.ShapeDtypeStruct((S, D), BF16), jax.ShapeDtypeStruct((1, D), F32)],
        compiler_params=_cp(("arbitrary",)),
        name=name,
    )(x, g.reshape(1, D), dh, dres)


def loss_head(x, g, target, name):
    S, D = x.shape
    tm = _pick(S, (256, 128))

    def body(x_ref, g_ref, t_ref, dx_ref, dxb_ref, dg_ref, loss_ref):
        i = pl.program_id(0)
        xf = x_ref[...]
        gg = g_ref[...]
        r = lax.rsqrt(jnp.mean(xf * xf, axis=-1, keepdims=True) + EPS)
        xh = xf * r
        err = xh * gg - t_ref[...]
        part = (0.5 / D) * jnp.sum(err * err)
        dy = err * (1.0 / D)
        gy = dy * gg
        proj = jnp.mean(gy * xf, axis=-1, keepdims=True)
        dx = r * gy - xf * (r * r * r * proj)
        dx_ref[...] = dx
        dxb_ref[...] = dx.astype(BF16)
        dg = jnp.sum(dy * xh, axis=0, keepdims=True)
        lossb = jnp.full(loss_ref.shape, part, F32)

        @pl.when(i == 0)
        def _():
            dg_ref[...] = dg
            loss_ref[...] = lossb

        @pl.when(i > 0)
        def _():
            dg_ref[...] += dg
            loss_ref[...] += lossb

    row = pl.BlockSpec((tm, D), lambda i: (i, 0))
    vec = pl.BlockSpec((1, D), lambda i: (0, 0))
    return pl.pallas_call(
        body,
        grid=(S // tm,),
        in_specs=[row, vec, row],
        out_specs=[row, row, vec, pl.BlockSpec((8, 128), lambda i: (0, 0))],
        out_shape=[
            jax.ShapeDtypeStruct((S, D), F32),
            jax.ShapeDtypeStruct((S, D), BF16),
            jax.ShapeDtypeStruct((1, D), F32),
            jax.ShapeDtypeStruct((8, 128), F32),
        ],
        compiler_params=_cp(("arbitrary",)),
        name=name,
    )(x, g.reshape(1, D), target)


def _shift_down(s, k):
    if k == 0:
        return s
    return jnp.where(_iota2(s.shape, 0) >= k, pltpu.roll(s, k, axis=0), 0.0)


def _shift_up(s, k):
    if k == 0:
        return s
    n = s.shape[0]
    return jnp.where(_iota2(s.shape, 0) < n - k, pltpu.roll(s, n - k, axis=0), 0.0)


def _conv(s, w):
    return w[0:1] * _shift_down(s, 2) + w[1:2] * _shift_down(s, 1) + w[2:3] * s


def _conv_t(d, w):
    return w[2:3] * d + w[1:2] * _shift_up(d, 1) + w[0:1] * _shift_up(d, 2)


def _conv_dw(d, s):
    return [jnp.sum(d * _shift_down(s, CONV_K - 1 - k), axis=0, keepdims=True) for k in range(CONV_K)]


def sc_fwd(p, convw, cat, W, name):
    S = p.shape[0]
    tc = _pick(W, (256, 128))
    nc = W // tc

    def body(gb_ref, gc_ref, hi_ref, w_ref, cat_ref, o_ref):
        s = gc_ref[...].astype(F32) * hi_ref[...].astype(F32)
        o_ref[...] = (gb_ref[...].astype(F32) * _conv(s, w_ref[...])).astype(o_ref.dtype)

    col = lambda part: pl.BlockSpec((S, tc), lambda c: (0, part * nc + c))
    return pl.pallas_call(
        body,
        grid=(nc,),
        in_specs=[col(3), col(4), col(5), pl.BlockSpec((CONV_K, tc), lambda c: (0, c)), pl.BlockSpec(memory_space=pl.ANY)],
        out_specs=col(1),
        out_shape=jax.ShapeDtypeStruct(cat.shape, cat.dtype),
        input_output_aliases={4: 0},
        compiler_params=_cp(("parallel",)),
        name=name,
    )(p, p, p, convw, cat)


def sc_bwd(p, convw, dcat, dp, W, name):
    S = p.shape[0]
    tc = _pick(W, (256, 128))
    nc = W // tc

    def body(gb_ref, gc_ref, hi_ref, w_ref, do_ref, dp_in_ref, dp_ref, dw_ref):
        gb = gb_ref[...].astype(F32)
        gc = gc_ref[...].astype(F32)
        hi = hi_ref[...].astype(F32)
        w = w_ref[...]
        do = do_ref[...].astype(F32)
        s = gc * hi
        dcs = do * gb
        ds = _conv_t(dcs, w)
        dp_ref[0] = (do * _conv(s, w)).astype(dp_ref.dtype)
        dp_ref[1] = (ds * hi).astype(dp_ref.dtype)
        dp_ref[2] = (ds * gc).astype(dp_ref.dtype)
        for k, row in enumerate(_conv_dw(dcs, s)):
            dw_ref[k : k + 1, :] = row

    col = lambda part: pl.BlockSpec((S, tc), lambda c: (0, part * nc + c))
    return pl.pallas_call(
        body,
        grid=(nc,),
        in_specs=[
            col(3), col(4), col(5),
            pl.BlockSpec((CONV_K, tc), lambda c: (0, c)),
            pl.BlockSpec((S, tc), lambda c: (0, nc + c)),
            pl.BlockSpec(memory_space=pl.ANY),
        ],
        out_specs=[pl.BlockSpec((3, S, tc), lambda c: (1, 0, c)), pl.BlockSpec((CONV_K, tc), lambda c: (0, c))],
        out_shape=[jax.ShapeDtypeStruct(dp.shape, dp.dtype), jax.ShapeDtypeStruct((CONV_K, W), F32)],
        input_output_aliases={5: 0},
        compiler_params=_cp(("parallel",)),
        name=name,
    )(p, p, p, convw, dcat, dp)


def _silu_parts(a):
    sig = 1.0 / (1.0 + jnp.exp(-a))
    return a * sig, sig


def ffn_act_fwd(u, convw, F, name):
    S = u.shape[0]
    tc = _pick(F, (256, 128))
    nc = F // tc

    def body(ug_ref, uu_ref, wg_ref, wu_ref, o_ref):
        ag = _conv(ug_ref[...].astype(F32), wg_ref[...])
        au = _conv(uu_ref[...].astype(F32), wu_ref[...])
        o_ref[...] = (_silu_parts(ag)[0] * au).astype(o_ref.dtype)

    col = lambda half: pl.BlockSpec((S, tc), lambda c: (0, half * nc + c))
    wcol = lambda half: pl.BlockSpec((CONV_K, tc), lambda c: (0, half * nc + c))
    return pl.pallas_call(
        body,
        grid=(nc,),
        in_specs=[col(0), col(1), wcol(0), wcol(1)],
        out_specs=pl.BlockSpec((S, tc), lambda c: (0, c)),
        out_shape=jax.ShapeDtypeStruct((S, F), BF16),
        compiler_params=_cp(("parallel",)),
        name=name,
    )(u, u, convw, convw)


def ffn_act_bwd(u, convw, dact, F, name):
    S = u.shape[0]
    tc = _pick(F, (256, 128))
    nc = F // tc

    def body(ug_ref, uu_ref, wg_ref, wu_ref, da_ref, du_ref, dw_ref):
        ug = ug_ref[...].astype(F32)
        uu = uu_ref[...].astype(F32)
        wg = wg_ref[...]
        wu = wu_ref[...]
        da = da_ref[...].astype(F32)
        ag = _conv(ug, wg)
        au = _conv(uu, wu)
        sl, sig = _silu_parts(ag)
        dag = da * au * (sig * (1.0 + ag * (1.0 - sig)))
        dau = da * sl
        du_ref[0] = _conv_t(dag, wg).astype(du_ref.dtype)
        du_ref[1] = _conv_t(dau, wu).astype(du_ref.dtype)
        for k, (rg, ru) in enumerate(zip(_conv_dw(dag, ug), _conv_dw(dau, uu))):
            dw_ref[0, k : k + 1, :] = rg
            dw_ref[1, k : k + 1, :] = ru

    col = lambda half: pl.BlockSpec((S, tc), lambda c: (0, half * nc + c))
    wcol = lambda half: pl.BlockSpec((CONV_K, tc), lambda c: (0, half * nc + c))
    return pl.pallas_call(
        body,
        grid=(nc,),
        in_specs=[col(0), col(1), wcol(0), wcol(1), pl.BlockSpec((S, tc), lambda c: (0, c))],
        out_specs=[pl.BlockSpec((2, S, tc), lambda c: (0, 0, c)), pl.BlockSpec((2, CONV_K, tc), lambda c: (0, 0, c))],
        out_shape=[jax.ShapeDtypeStruct((2, S, F), BF16), jax.ShapeDtypeStruct((2, CONV_K, F), F32)],
        compiler_params=_cp(("parallel",)),
        name=name,
    )(u, u, convw, convw, dact)


def _softplus(z):
    return jnp.maximum(z, 0.0) + jnp.log(1.0 + jnp.exp(-jnp.abs(z)))


def _key_strip(S):
    return _pick(S, (512, 256, 128))


def _split2(x):
    hi = x.astype(BF16)
    return hi, (x - hi.astype(F32)).astype(BF16)


def _block_sums(x, ones_bf16):
    hi, lo = _split2(x)
    return [
        _dot(hi[:, b * HD : (b + 1) * HD], ones_bf16) + _dot(lo[:, b * HD : (b + 1) * HD], ones_bf16)
        for b in range(x.shape[1] // HD)
    ]


def _strip_mask(shape, i, off, strict):
    cols, rows = _iota2(shape, 1) + off, _iota2(shape, 0) + i * HD
    return cols < rows if strict else cols <= rows


def _sb_strip(q, ks, i, off, run, su):
    z = _dot_nt(q, ks) * (HD ** -0.5)
    mask = _strip_mask(z.shape, i, off, True)
    sp = _softplus(z)
    l = jnp.where(mask, -sp, 0.0)
    within = _block_sums(l, su)
    later = [None] * len(within)
    for b in reversed(range(len(within))):
        later[b] = within[b] + run
        run = run + jnp.sum(l[:, b * HD : (b + 1) * HD], axis=1, keepdims=True)
    a = jnp.where(mask, jnp.exp(z - sp + jnp.concatenate(later, axis=1)), 0.0)
    return z, mask, a, run


def sb_fwd(p, W, name):
    S = p.shape[0]
    nh, nq = W // HD, S // HD
    TK = _key_strip(S)

    def body(q_ref, k_ref, v_ref, o_ref):
        i = pl.program_id(1)
        q = q_ref[...]
        su = (_iota2((HD, HD), 0) > _iota2((HD, HD), 1)).astype(BF16)
        last = (i * HD) // TK

        def step(gg, carry):
            acc, run = carry
            off = pl.multiple_of((last - gg) * TK, TK)
            _, _, a, run = _sb_strip(q, k_ref[pl.ds(off, TK), :], i, off, run, su)
            return acc + _dot(a.astype(BF16), v_ref[pl.ds(off, TK), :]), run

        acc, _ = lax.fori_loop(0, last + 1, step, (jnp.zeros((HD, HD), F32), jnp.zeros((HD, 1), F32)))
        o_ref[...] = acc.astype(o_ref.dtype)

    return pl.pallas_call(
        body,
        grid=(nh, nq),
        in_specs=[
            pl.BlockSpec((HD, HD), lambda h, i: (i, h)),
            pl.BlockSpec((S, HD), lambda h, i: (0, nh + h)),
            pl.BlockSpec((S, HD), lambda h, i: (0, 2 * nh + h)),
        ],
        out_specs=pl.BlockSpec((HD, HD), lambda h, i: (i, h)),
        out_shape=jax.ShapeDtypeStruct((S, 2 * W), BF16),
        compiler_params=_cp(("parallel", "arbitrary")),
        name=name,
    )(p, p, p)


def sb_bwd(p, dcat, W, name):
    S = p.shape[0]
    nh, nq = W // HD, S // HD
    TK = _key_strip(S)
    scale = HD ** -0.5

    def body(q_ref, k_ref, v_ref, do_ref, dp_ref, dk_acc, dv_acc, e_scr, z_scr):
        i = pl.program_id(1)
        q = q_ref[...]
        do = do_ref[...]
        su = (_iota2((HD, HD), 0) > _iota2((HD, HD), 1)).astype(BF16)
        sl = (_iota2((HD, HD), 0) < _iota2((HD, HD), 1)).astype(BF16)
        last = (i * HD) // TK

        @pl.when(i == 0)
        def _():
            dk_acc[...] = jnp.zeros_like(dk_acc)
            dv_acc[...] = jnp.zeros_like(dv_acc)

        def pass_a(gg, run):
            g = last - gg
            off = pl.multiple_of(g * TK, TK)
            z, _, a, run = _sb_strip(q, k_ref[pl.ds(off, TK), :], i, off, run, su)
            e_scr[g] = a * _dot_nt(do, v_ref[pl.ds(off, TK), :])
            z_scr[g] = z
            dv_acc[pl.ds(off, TK), :] += _dot_tn(a.astype(BF16), do)
            return run

        lax.fori_loop(0, last + 1, pass_a, jnp.zeros((HD, 1), F32))

        def pass_b(g, carry):
            dq, run_e = carry
            off = pl.multiple_of(g * TK, TK)
            e = e_scr[g]
            z = z_scr[g]
            mask = _strip_mask(z.shape, i, off, True)
            within = _block_sums(e, sl)
            before = []
            for b in range(len(within)):
                before.append(within[b] + run_e)
                run_e = run_e + jnp.sum(e[:, b * HD : (b + 1) * HD], axis=1, keepdims=True)
            sig = 1.0 / (1.0 + jnp.exp(-z))
            dz = jnp.where(mask, e * (1.0 - sig) - jnp.concatenate(before, axis=1) * sig, 0.0)
            dz = (dz * scale).astype(BF16)
            dq = dq + _dot(dz, k_ref[pl.ds(off, TK), :])
            dk_acc[pl.ds(off, TK), :] += _dot_tn(dz, q)
            return dq, run_e

        dq, _ = lax.fori_loop(0, last + 1, pass_b, (jnp.zeros((HD, HD), F32), jnp.zeros((HD, 1), F32)))
        dp_ref[0, pl.ds(pl.multiple_of(i * HD, HD), HD), :] = dq.astype(dp_ref.dtype)

        @pl.when(i == nq - 1)
        def _():
            dp_ref[1] = dk_acc[...].astype(dp_ref.dtype)
            dp_ref[2] = dv_acc[...].astype(dp_ref.dtype)

    return pl.pallas_call(
        body,
        grid=(nh, nq),
        in_specs=[
            pl.BlockSpec((HD, HD), lambda h, i: (i, h)),
            pl.BlockSpec((S, HD), lambda h, i: (0, nh + h)),
            pl.BlockSpec((S, HD), lambda h, i: (0, 2 * nh + h)),
            pl.BlockSpec((HD, HD), lambda h, i: (i, h)),
        ],
        out_specs=pl.BlockSpec((3, S, HD), lambda h, i: (0, 0, h)),
        out_shape=jax.ShapeDtypeStruct((6, S, W), BF16),
        scratch_shapes=[
            pltpu.VMEM((S, HD), F32),
            pltpu.VMEM((S, HD), F32),
            pltpu.VMEM((S // TK, HD, TK), F32),
            pltpu.VMEM((S // TK, HD, TK), F32),
        ],
        compiler_params=_cp(("parallel", "arbitrary")),
        name=name,
    )(p, p, p, dcat)


def fox_gate_fwd(f, b, name):
    S = f.shape[0]
    nq = S // HD

    def body(f_ref, b_ref, c_ref, run):
        i = pl.program_id(0)

        @pl.when(i == 0)
        def _():
            run[...] = jnp.zeros_like(run)

        lf = -_softplus(-(f_ref[...] + b_ref[...]))
        tri = (_iota2((HD, HD), 0) >= _iota2((HD, HD), 1)).astype(BF16)
        c_ref[...] = _dot_ones_left(tri, lf) + run[...]
        run[...] += jnp.sum(lf, axis=0, keepdims=True)

    return pl.pallas_call(
        body,
        grid=(nq,),
        in_specs=[pl.BlockSpec((HD, 128), lambda i: (i, 0)), pl.BlockSpec((1, 128), lambda i: (0, 0))],
        out_specs=pl.BlockSpec((HD, 128), lambda i: (i, 0)),
        out_shape=jax.ShapeDtypeStruct((S, 128), F32),
        scratch_shapes=[pltpu.VMEM((1, 128), F32)],
        compiler_params=_cp(("arbitrary",)),
        name=name,
    )(f, b)


def fox_gate_bwd(f, b, dc, name):
    S = f.shape[0]
    nq = S // HD

    def body(f_ref, b_ref, dc_ref, df_ref, db_ref, run):
        i = pl.program_id(0)

        @pl.when(i == 0)
        def _():
            run[...] = jnp.zeros_like(run)

        dc = dc_ref[...]
        tri = (_iota2((HD, HD), 0) <= _iota2((HD, HD), 1)).astype(BF16)
        dlf = _dot_ones_left(tri, dc) + run[...]
        run[...] += jnp.sum(dc, axis=0, keepdims=True)
        x = f_ref[...] + b_ref[...]
        df = dlf * (1.0 / (1.0 + jnp.exp(x)))
        df_ref[...] = df
        db = jnp.sum(df, axis=0, keepdims=True)

        @pl.when(i == 0)
        def _():
            db_ref[...] = db

        @pl.when(i > 0)
        def _():
            db_ref[...] += db

    rev = pl.BlockSpec((HD, 128), lambda i: (nq - 1 - i, 0))
    vec = pl.BlockSpec((1, 128), lambda i: (0, 0))
    return pl.pallas_call(
        body,
        grid=(nq,),
        in_specs=[rev, vec, rev],
        out_specs=[rev, vec],
        out_shape=[jax.ShapeDtypeStruct((S, 128), F32), jax.ShapeDtypeStruct((1, 128), F32)],
        scratch_shapes=[pltpu.VMEM((1, 128), F32)],
        compiler_params=_cp(("arbitrary",)),
        name=name,
    )(f, b, dc)


def _fox_logits(q, ks, ct, cs, i, off):
    s = _dot_nt(q, ks) * (HD ** -0.5) + (ct - cs)
    mask = _strip_mask(s.shape, i, off, False)
    return jnp.where(mask, s, -1e30), mask


def fox_fwd(p, ccol, crow, cat, W, name):
    S = p.shape[0]
    nh, nq = W // HD, S // HD
    TK = _key_strip(S)

    def body(q_ref, k_ref, v_ref, cc_ref, cr_ref, cat_ref, o_ref, lse_ref):
        i = pl.program_id(1)
        q = q_ref[...]
        ct = cc_ref[0]

        def step(g, carry):
            m, l, acc = carry
            off = pl.multiple_of(g * TK, TK)
            s, _ = _fox_logits(q, k_ref[pl.ds(off, TK), :], ct, cr_ref[0, pl.ds(g, 1), :], i, off)
            m_new = jnp.maximum(m, jnp.max(s, axis=1, keepdims=True))
            alpha = jnp.exp(m - m_new)
            pr = jnp.exp(s - m_new)
            l = alpha * l + jnp.sum(pr, axis=1, keepdims=True)
            acc = alpha * acc + _dot(pr.astype(BF16), v_ref[pl.ds(off, TK), :])
            return m_new, l, acc

        init = (jnp.full((HD, 1), -1e30, F32), jnp.zeros((HD, 1), F32), jnp.zeros((HD, HD), F32))
        m, l, acc = lax.fori_loop(0, (i * HD) // TK + 1, step, init)
        o_ref[...] = (acc / l).astype(o_ref.dtype)
        lse_ref[0] = m + jnp.log(l)

    return pl.pallas_call(
        body,
        grid=(nh, nq),
        in_specs=[
            pl.BlockSpec((HD, HD), lambda h, i: (i, 2 * nh + h)),
            pl.BlockSpec((S, HD), lambda h, i: (0, 3 * nh + h)),
            pl.BlockSpec((S, HD), lambda h, i: (0, 4 * nh + h)),
            pl.BlockSpec((1, HD, 1), lambda h, i: (h, i, 0)),
            pl.BlockSpec((1, S // TK, TK), lambda h, i: (h, 0, 0)),
            pl.BlockSpec(memory_space=pl.ANY),
        ],
        out_specs=[pl.BlockSpec((HD, HD), lambda h, i: (i, nh + h)), pl.BlockSpec((1, HD, 1), lambda h, i: (h, i, 0))],
        out_shape=[jax.ShapeDtypeStruct(cat.shape, cat.dtype), jax.ShapeDtypeStruct((nh, S, 1), F32)],
        input_output_aliases={5: 0},
        compiler_params=_cp(("parallel", "arbitrary")),
        name=name,
    )(p, p, p, ccol, crow, cat)


def fox_bwd(p, ccol, crow, cat, lse, dcat, dp, W, name):
    S = p.shape[0]
    nh, nq = W // HD, S // HD
    TK = _key_strip(S)
    scale = HD ** -0.5

    def body(q_ref, k_ref, v_ref, cc_ref, cr_ref, o_ref, lse_ref, do_ref, dp_in_ref, dp_ref, dcs_ref, dct_ref, dk_acc, dv_acc):
        i = pl.program_id(1)
        q = q_ref[...]
        do = do_ref[...]
        ct = cc_ref[0]
        lse_i = lse_ref[0]
        delta = jnp.sum(do.astype(F32) * o_ref[...].astype(F32), axis=1, keepdims=True)

        @pl.when(i == 0)
        def _():
            dk_acc[...] = jnp.zeros_like(dk_acc)
            dv_acc[...] = jnp.zeros_like(dv_acc)
            dcs_ref[...] = jnp.zeros_like(dcs_ref)

        def step(g, carry):
            dq, dct = carry
            off = pl.multiple_of(g * TK, TK)
            ks = k_ref[pl.ds(off, TK), :]
            s, mask = _fox_logits(q, ks, ct, cr_ref[0, pl.ds(g, 1), :], i, off)
            pr = jnp.where(mask, jnp.exp(s - lse_i), 0.0)
            ds = pr * (_dot_nt(do, v_ref[pl.ds(off, TK), :]) - delta)
            dv_acc[pl.ds(off, TK), :] += _dot_tn(pr.astype(BF16), do)
            dsb = (ds * scale).astype(BF16)
            dk_acc[pl.ds(off, TK), :] += _dot_tn(dsb, q)
            dcs_ref[0, pl.ds(g, 1), :] += jnp.sum(ds, axis=0, keepdims=True)
            return dq + _dot(dsb, ks), dct + jnp.sum(ds, axis=1, keepdims=True)

        dq, dct = lax.fori_loop(0, (i * HD) // TK + 1, step, (jnp.zeros((HD, HD), F32), jnp.zeros((HD, 1), F32)))
        dp_ref[0, pl.ds(pl.multiple_of(i * HD, HD), HD), :] = dq.astype(dp_ref.dtype)
        dct_ref[0] = dct

        @pl.when(i == nq - 1)
        def _():
            dp_ref[1] = dk_acc[...].astype(dp_ref.dtype)
            dp_ref[2] = dv_acc[...].astype(dp_ref.dtype)

    return pl.pallas_call(
        body,
        grid=(nh, nq),
        in_specs=[
            pl.BlockSpec((HD, HD), lambda h, i: (i, 2 * nh + h)),
            pl.BlockSpec((S, HD), lambda h, i: (0, 3 * nh + h)),
            pl.BlockSpec((S, HD), lambda h, i: (0, 4 * nh + h)),
            pl.BlockSpec((1, HD, 1), lambda h, i: (h, i, 0)),
            pl.BlockSpec((1, S // TK, TK), lambda h, i: (h, 0, 0)),
            pl.BlockSpec((HD, HD), lambda h, i: (i, nh + h)),
            pl.BlockSpec((1, HD, 1), lambda h, i: (h, i, 0)),
            pl.BlockSpec((HD, HD), lambda h, i: (i, nh + h)),
            pl.BlockSpec(memory_space=pl.ANY),
        ],
        out_specs=[
            pl.BlockSpec((3, S, HD), lambda h, i: (1, 0, h)),
            pl.BlockSpec((1, S // TK, TK), lambda h, i: (h, 0, 0)),
            pl.BlockSpec((1, HD, 1), lambda h, i: (h, i, 0)),
        ],
        out_shape=[
            jax.ShapeDtypeStruct(dp.shape, dp.dtype),
            jax.ShapeDtypeStruct((nh, S // TK, TK), F32),
            jax.ShapeDtypeStruct((nh, S, 1), F32),
        ],
        input_output_aliases={8: 0},
        scratch_shapes=[pltpu.VMEM((S, HD), F32), pltpu.VMEM((S, HD), F32)],
        compiler_params=_cp(("parallel", "arbitrary")),
        name=name,
    )(p, p, p, ccol, crow, cat, lse, dcat, dp)


_GELU_K = math.sqrt(2.0 / math.pi)
_GELU_C = 0.044715


def _gelu(x):
    return 0.5 * x * (1.0 + jnp.tanh(_GELU_K * (x + _GELU_C * x * x * x)))


def _gelu_grad(x):
    t = jnp.tanh(_GELU_K * (x + _GELU_C * x * x * x))
    return 0.5 * (1.0 + t) + 0.5 * x * (1.0 - t * t) * (_GELU_K * (1.0 + 3.0 * _GELU_C * x * x))


def _layernorm_parts(gv):
    xc = gv - jnp.mean(gv, axis=-1, keepdims=True)
    r = lax.rsqrt(jnp.mean(xc * xc, axis=-1, keepdims=True) + EPS)
    return xc * r, r


def sg_fwd(p, sg_w, sg_bt, sg_g, W, name):
    S = p.shape[0]
    G, nq = W // HD, S // HD

    def body(u_ref, v_ref, w_ref, bt_ref, g_ref, o_ref):
        xh, _ = _layernorm_parts(_gelu(v_ref[...].astype(F32)))
        vn = (xh * g_ref[...]).astype(BF16)
        tri = _iota2((HD, HD), 0) >= _iota2((HD, HD), 1)
        for gi in range(G):
            cols = slice(gi * HD, (gi + 1) * HD)
            wt = jnp.where(tri, w_ref[gi], 0.0).astype(BF16)
            mixed = _dot(wt, vn[:, cols]) + bt_ref[:, gi : gi + 1]
            o_ref[:, cols] = (_gelu(u_ref[:, cols].astype(F32)) * mixed).astype(o_ref.dtype)

    return pl.pallas_call(
        body,
        grid=(nq,),
        in_specs=[
            pl.BlockSpec((HD, W), lambda i: (i, 0)),
            pl.BlockSpec((HD, W), lambda i: (i, 1)),
            pl.BlockSpec((G, HD, HD), lambda i: (0, 0, 0)),
            pl.BlockSpec((HD, G), lambda i: (0, 0)),
            pl.BlockSpec((1, W), lambda i: (0, 0)),
        ],
        out_specs=pl.BlockSpec((HD, W), lambda i: (i, 0)),
        out_shape=jax.ShapeDtypeStruct((S, 2 * W), BF16),
        compiler_params=_cp(("parallel",)),
        name=name,
    )(p, p, sg_w, sg_bt, sg_g.reshape(1, W))


def sg_bwd(p, sg_w, sg_bt, sg_g, dcat, W, name):
    S = p.shape[0]
    G, nq = W // HD, S // HD

    def body(u_ref, v_ref, w_ref, bt_ref, g_ref, do_ref, dp_ref, dw_ref, dbt_ref, dg_ref, dvn_scr):
        i = pl.program_id(0)

        @pl.when(i == 0)
        def _():
            dw_ref[...] = jnp.zeros_like(dw_ref)
            dbt_ref[...] = jnp.zeros_like(dbt_ref)
            dg_ref[...] = jnp.zeros_like(dg_ref)

        v = v_ref[...].astype(F32)
        xh, r = _layernorm_parts(_gelu(v))
        gg = g_ref[...]
        vn = (xh * gg).astype(BF16)
        tri = _iota2((HD, HD), 0) >= _iota2((HD, HD), 1)
        for gi in range(G):
            cols = slice(gi * HD, (gi + 1) * HD)
            wt = jnp.where(tri, w_ref[gi], 0.0).astype(BF16)
            mixed = _dot(wt, vn[:, cols]) + bt_ref[:, gi : gi + 1]
            u = u_ref[:, cols].astype(F32)
            do = do_ref[:, cols].astype(F32)
            dp_ref[0, :, cols] = (do * mixed * _gelu_grad(u)).astype(dp_ref.dtype)
            dmix = do * _gelu(u)
            dmb = dmix.astype(BF16)
            dw_ref[gi] += jnp.where(tri, _dot_nt(dmb, vn[:, cols]), 0.0)
            dbt_ref[:, gi : gi + 1] += jnp.sum(dmix, axis=1, keepdims=True)
            dvn_scr[:, cols] = _dot_tn(wt, dmb)
        dvn = dvn_scr[...]
        dg_ref[...] += jnp.sum(dvn * xh, axis=0, keepdims=True)
        dxh = dvn * gg
        dgv = r * (dxh - jnp.mean(dxh, axis=-1, keepdims=True) - xh * jnp.mean(dxh * xh, axis=-1, keepdims=True))
        dp_ref[1] = (dgv * _gelu_grad(v)).astype(dp_ref.dtype)

    return pl.pallas_call(
        body,
        grid=(nq,),
        in_specs=[
            pl.BlockSpec((HD, W), lambda i: (i, 0)),
            pl.BlockSpec((HD, W), lambda i: (i, 1)),
            pl.BlockSpec((G, HD, HD), lambda i: (0, 0, 0)),
            pl.BlockSpec((HD, G), lambda i: (0, 0)),
            pl.BlockSpec((1, W), lambda i: (0, 0)),
            pl.BlockSpec((HD, W), lambda i: (i, 0)),
        ],
        out_specs=[
            pl.BlockSpec((2, HD, W), lambda i: (0, i, 0)),
            pl.BlockSpec((G, HD, HD), lambda i: (0, 0, 0)),
            pl.BlockSpec((HD, G), lambda i: (0, 0)),
            pl.BlockSpec((1, W), lambda i: (0, 0)),
        ],
        out_shape=[
            jax.ShapeDtypeStruct((6, S, W), BF16),
            jax.ShapeDtypeStruct((G, HD, HD), F32),
            jax.ShapeDtypeStruct((HD, G), F32),
            jax.ShapeDtypeStruct((1, W), F32),
        ],
        scratch_shapes=[pltpu.VMEM((HD, W), F32)],
        compiler_params=_cp(("arbitrary",)),
        name=name,
    )(p, p, sg_w, sg_bt, sg_g.reshape(1, W), dcat)


def local_step(x, target, wts, start_gather, on_grad):
    S, D = x.shape
    W = D // 2
    nb, F = wts["nb"], wts["F"]
    g = {}

    def ffn_fwd(xin, l, ahead=(None, None, None)):
        h = rms_fwd(xin, wts[f"{l}_ffn_norm_g"], f"{l}_ffn_rms")
        h = start_gather(ahead[0], h) if ahead[0] else h
        u = mm_nn(h, wts[f"{l}_ffn_up"], nb, f"{l}_ffn_up_mm")
        u = start_gather(ahead[1], u) if ahead[1] else u
        act = ffn_act_fwd(u, wts[f"{l}_ffn_conv_w"], F, f"{l}_ffn_act")
        act = start_gather(ahead[2], act) if ahead[2] else act
        xout = mm_nn(act, wts[f"{l}_ffn_down"], 1, f"{l}_ffn_down_mm", out_dtype=F32, res=xin)
        return xout, (xin, h, u, act)

    def ffn_bwd(dxout, dxoutb, saved, l):
        xin, h, u, act = saved
        dact = mm_nt(dxoutb, wts[f"{l}_ffn_down"], 1, S, F, f"{l}_ffn_down_dx")
        dact = on_grad(f"{l}_ffn_down", mm_tn(act, dxoutb, 1, D, f"{l}_ffn_down_dw"), dact)
        du, dcw = ffn_act_bwd(u, wts[f"{l}_ffn_conv_w"], dact, F, f"{l}_ffn_act_bwd")
        g[f"{l}_ffn_conv_w"] = jnp.concatenate([dcw[0], dcw[1]], axis=1)
        du2 = du.reshape(2 * S, F)
        n = wts[f"{l}_ffn_up"].shape[1]
        tn = _pick(n, (1408, 1024, 768, 512, 256, 128))
        per_half = F // tn
        nt = n // tn

        def up_block(i, j, t):
            vb = j * nt + t
            return vb // per_half, vb % per_half

        tm = _pick(S, (1024, 512, 256, 128))

        def nt_map(i, j, t):
            half, cb = up_block(i, j, t)
            return (half * (S // tm) + i, cb)

        def tn_map(j, t):
            half, cb = up_block(0, j, t)
            return (half, cb)

        dh = mm_nt(du2, wts[f"{l}_ffn_up"], nb, S, D, f"{l}_ffn_up_dx", dy_maps=[nt_map], tm=tm, tn=tn)
        dh = on_grad(f"{l}_ffn_up", mm_tn(h, du2, nb, n, f"{l}_ffn_up_dw", dy_maps=[tn_map], tn=tn), dh)
        dxin, dxinb, dg = rms_bwd(xin, wts[f"{l}_ffn_norm_g"], dh, dxout, f"{l}_ffn_rms_bwd")
        g[f"{l}_ffn_norm_g"] = dg
        return dxin, dxinb

    h0 = rms_fwd(x, wts["l0_mix_norm_g"], "l0_mix_rms")
    h0 = start_gather("l0_w_out", h0)
    p0 = mm_nn(h0, wts["l0_w_in"], nb, "l0_w_in_mm")
    p0 = start_gather("l0_ffn_up", p0)
    cat0 = sb_fwd(p0, W, "l0_sb_fwd")
    cat0 = sc_fwd(p0, wts["l0_sc_conv_w"], cat0, W, "l0_sc_fwd")
    cat0 = start_gather("l0_ffn_down", cat0)
    x1 = mm_nn(cat0, wts["l0_w_out"], 1, "l0_w_out_mm", out_dtype=F32, res=x)
    x2, ffn0_saved = ffn_fwd(x1, "l0", ahead=("l1_w_in", "l1_w_out", "l1_ffn_up"))
    x2 = start_gather("l1_ffn_down", x2)

    nh = W // HD
    h2 = rms_fwd(x2, wts["l1_mix_norm_g"], "l1_mix_rms")
    p1 = mm_nn(h2, wts["l1_w_in_main"], 1, "l1_w_in_mm")
    f = mm_nn(h2, wts["l1_w_in_f"], 1, "l1_w_f_mm", out_dtype=F32)
    bf = jnp.zeros((1, 128), F32).at[0, :nh].set(wts["l1_fox_b_f"])
    c = fox_gate_fwd(f, bf, "l1_fox_gate")
    c_heads = c[:, :nh].T
    ccol = c_heads[:, :, None]
    crow = c_heads.reshape(nh, S // _key_strip(S), _key_strip(S))
    sg_bt = wts["l1_sg_b"].T
    cat1 = sg_fwd(p1, wts["l1_sg_w"], sg_bt, wts["l1_sg_norm_g"], W, "l1_sg_fwd")
    cat1, lse = fox_fwd(p1, ccol, crow, cat1, W, "l1_fox_fwd")
    x3 = mm_nn(cat1, wts["l1_w_out"], 1, "l1_w_out_mm", out_dtype=F32, res=x2)
    x4, ffn1_saved = ffn_fwd(x3, "l1")

    dx4, dx4b, dgf, loss = loss_head(x4, wts["final_norm_g"], target, "loss_head")
    g["final_norm_g"] = dgf

    dx3, dx3b = ffn_bwd(dx4, dx4b, ffn1_saved, "l1")
    dcat1 = mm_nt(dx3b, wts["l1_w_out"], 1, S, D, "l1_w_out_dx")
    dcat1 = on_grad("l1_w_out", mm_tn(cat1, dx3b, 1, D, "l1_w_out_dw"), dcat1)
    dp1, dsgw, dsgbt, dsgg = sg_bwd(p1, wts["l1_sg_w"], sg_bt, wts["l1_sg_norm_g"], dcat1, W, "l1_sg_bwd")
    dp1, dcs, dct = fox_bwd(p1, ccol, crow, cat1, lse, dcat1, dp1, W, "l1_fox_bwd")
    g["l1_sg_w"], g["l1_sg_b"], g["l1_sg_norm_g"] = dsgw, dsgbt.T, dsgg
    dc = jnp.zeros((S, 128), F32).at[:, :nh].set((dct[:, :, 0] - dcs.reshape(nh, S)).T)
    df, dbf = fox_gate_bwd(f, bf, dc, "l1_fox_gate_bwd")
    g["l1_fox_b_f"] = dbf[0, :nh]
    dfb = df.astype(BF16)
    tn1 = _pick(W, (1024, 512, 256, 128))
    tm1 = _pick(S, (1024, 512, 256, 128))
    per_part = W // tn1
    part_of = lambda pt: pt + pt // 2 - pt // 4

    def nt_map1(i, j, t):
        return (part_of(t // per_part) * (S // tm1) + i, t % per_part)

    def tn_map1(j, t):
        return (part_of(t // per_part), t % per_part)

    dp1_2d = dp1.reshape(6 * S, W)
    dw_main = mm_tn(h2, dp1_2d, 1, 5 * W, "l1_w_in_dw", dy_maps=[tn_map1], tn=tn1)
    dw_f = mm_tn(h2, dfb, 1, 128, "l1_w_f_dw")
    dh2 = mm_nt(dfb, wts["l1_w_in_f"], 1, S, D, "l1_w_f_dx", out_dtype=F32)
    dh2 = mm_nt(dp1_2d, wts["l1_w_in_main"], 1, S, D, "l1_w_in_dx", res=dh2, dy_maps=[nt_map1], tm=tm1, tn=tn1)
    dh2 = on_grad("l1_w_in", jnp.concatenate([dw_main, dw_f[:, :nh]], axis=1), dh2)
    dx2, dx2b, dg = rms_bwd(x2, wts["l1_mix_norm_g"], dh2, dx3, "l1_mix_rms_bwd")
    g["l1_mix_norm_g"] = dg

    dx1, dx1b = ffn_bwd(dx2, dx2b, ffn0_saved, "l0")
    dcat0 = mm_nt(dx1b, wts["l0_w_out"], 1, S, D, "l0_w_out_dx")
    dcat0 = on_grad("l0_w_out", mm_tn(cat0, dx1b, 1, D, "l0_w_out_dw"), dcat0)
    dp0 = sb_bwd(p0, dcat0, W, "l0_sb_bwd")
    dp0, dscw = sc_bwd(p0, wts["l0_sc_conv_w"], dcat0, dp0, W, "l0_sc_bwd")
    g["l0_sc_conv_w"] = dscw
    n0 = wts["l0_w_in"].shape[1]
    td0 = math.gcd(n0, W)
    nd0 = n0 // td0
    tm0 = _pick(S, (1024, 512, 256, 128))
    per_part0 = W // td0

    def nt_maps0(k):
        def f(i, j, t):
            vb = j * nd0 + k
            return ((vb // per_part0) * (S // tm0) + i, vb % per_part0)
        return f

    def tn_maps0(k):
        def f(j, t):
            vb = j * nd0 + k
            return (vb // per_part0, vb % per_part0)
        return f

    dp0_2d = dp0.reshape(6 * S, W)
    dw0 = mm_tn(h0, dp0_2d, nb, n0, "l0_w_in_dw", dy_maps=[tn_maps0(k) for k in range(nd0)], tn=n0)
    dp0_2d = on_grad("l0_w_in", dw0, dp0_2d)
    dh0 = mm_nt(dp0_2d, wts["l0_w_in"], nb, S, D, "l0_w_in_dx", dy_maps=[nt_maps0(k) for k in range(nd0)], tm=tm0, tn=n0)
    dx0, _, dg = rms_bwd(x, wts["l0_mix_norm_g"], dh0, dx1, "l0_mix_rms_bwd")
    g["l0_mix_norm_g"] = dg
    return loss, dx0, g


GATHER_ID, PAIR_ID, CHIPS_ID = 1, 2, 3


def _place():
    return lax.axis_index("x"), lax.axis_index("y"), lax.axis_index("c")


def _other_chips(x, y):
    return [(x, 1 - y), (1 - x, y), (1 - x, 1 - y)]


def _handshake(peers):
    barrier = pltpu.get_barrier_semaphore()
    for peer in peers:
        pl.semaphore_signal(barrier, inc=1, device_id=peer, device_id_type=MESH)
    pl.semaphore_wait(barrier, len(peers))


UPDATE_LAG = 2


def _on_sequencer(body, out_type, scratch_types, collective_id, name):
    return pl.kernel(
        body,
        out_type=out_type,
        mesh=plsc.ScalarSubcoreMesh(axis_name="seq", num_cores=1),
        scratch_types=scratch_types,
        compiler_params=pltpu.CompilerParams(collective_id=collective_id),
        name=name,
    )


def all_gather(arrs, name):
    n = len(arrs)

    def body(*refs):
        xs, outs = refs[:n], refs[n : 2 * n]
        send_sems, recv_sems, local_sems = refs[2 * n :]
        x, y, c = _place()
        me, sibling = (x, y, c), (x, y, 1 - c)
        chips = _other_chips(x, y)
        _handshake([sibling] + [(*chip, c) for chip in chips])

        def copy(a, k, block, to, src=None):
            px, py, pc = block
            dst = outs[a].at[4 * px + 2 * py + pc]
            return pltpu.make_async_remote_copy(
                src_ref=dst if src is None else src, dst_ref=dst,
                send_sem=send_sems.at[7 * a + k], recv_sem=recv_sems.at[7 * a + k], device_id=to, device_id_type=MESH,
            )

        mine = [pltpu.make_async_copy(xs[a], outs[a].at[4 * x + 2 * y + c], local_sems.at[a]) for a in range(n)]
        for cp in mine:
            cp.start()
        first = []
        for a in range(n):
            first.append(copy(a, 0, me, sibling, src=xs[a]))
            first += [copy(a, 1 + j, me, (*chip, c), src=xs[a]) for j, chip in enumerate(chips)]
        for cp in first:
            cp.start()
        passed = []
        for a in range(n):
            for j, chip in enumerate(chips):
                copy(a, 1 + j, (*chip, c), me).wait_recv()
                cp = copy(a, 4 + j, (*chip, c), sibling)
                cp.start()
                passed.append(cp)
        for a in range(n):
            copy(a, 0, sibling, me).wait_recv()
            for j, chip in enumerate(chips):
                copy(a, 4 + j, (*chip, 1 - c), me).wait_recv()
        for cp in first + passed:
            cp.wait_send()
        for cp in mine:
            cp.wait()

    out_type = [jax.ShapeDtypeStruct((NDEV,) + a.shape, a.dtype) for a in arrs]
    sems = [pltpu.SemaphoreType.DMA((7 * n,)), pltpu.SemaphoreType.DMA((7 * n,)), pltpu.SemaphoreType.DMA((n,))]
    return _on_sequencer(body, out_type, sems, GATHER_ID, name)(*arrs)


_IN_HBM = pl.BlockSpec(memory_space=pltpu.HBM)
_IN_SEM = pl.BlockSpec(memory_space=pltpu.SEMAPHORE)
_EFFECT = pltpu.SideEffectType.DATAFLOW_SIDE_EFFECTING


def _split_start(make_copies, src, land_shape, nsem, name):
    def body(src_ref, land_ref, send_sems, recv_sems, land_thru, token):
        for cp in make_copies(src_ref, land_ref, send_sems, recv_sems):
            cp.start()
        token[...] = jnp.zeros_like(token)

    send_sems, recv_sems, land_thru, token = pl.pallas_call(
        body,
        name=name,
        out_shape=(
            pltpu.SemaphoreType.DMA((nsem,)), pltpu.SemaphoreType.DMA((nsem,)),
            pltpu.HBM(land_shape, src.dtype), jax.ShapeDtypeStruct((8, 128), F32),
        ),
        in_specs=(_IN_HBM, _IN_HBM),
        out_specs=(_IN_SEM, _IN_SEM, _IN_HBM, pl.BlockSpec(memory_space=pltpu.VMEM)),
        input_output_aliases={1: 2},
        compiler_params=pltpu.CompilerParams(has_side_effects=_EFFECT),
    )(src, pltpu.with_memory_space_constraint(lax.empty(land_shape, src.dtype), pltpu.HBM))
    return send_sems, recv_sems, src, land_thru, token


def _split_wait(make_copies, send_sems, recv_sems, src_thru, land_thru, after, name):
    def body(src_ref, land_ref, send_sems, recv_sems, after_ref, land_out):
        for cp in make_copies(src_ref, land_ref, send_sems, recv_sems):
            cp.wait_send()
            cp.wait_recv()

    return pl.pallas_call(
        body,
        name=name,
        out_shape=pltpu.HBM(land_thru.shape, land_thru.dtype),
        in_specs=(_IN_HBM, _IN_HBM, _IN_SEM, _IN_SEM, pl.BlockSpec(memory_space=pl.ANY)),
        out_specs=_IN_HBM,
        input_output_aliases={1: 0},
        compiler_params=pltpu.CompilerParams(has_side_effects=_EFFECT),
    )(src_thru, land_thru, send_sems, recv_sems, after)


def _pair_copies(src_ref, land_ref, send_sems, recv_sems):
    x, y, c = _place()
    return [
        pltpu.make_async_remote_copy(
            src_ref=src_ref.at[k, 1 - c], dst_ref=land_ref.at[k],
            send_sem=send_sems.at[k], recv_sem=recv_sems.at[k], device_id=(x, y, 1 - c), device_id_type=MESH,
        )
        for k in range(4)
    ]


def _chip_copies(src_ref, land_ref, send_sems, recv_sems):
    x, y, c = _place()
    return [
        pltpu.make_async_remote_copy(
            src_ref=src_ref.at[2 * px + py], dst_ref=land_ref.at[2 * x + y],
            send_sem=send_sems.at[j], recv_sem=recv_sems.at[j], device_id=(px, py, c), device_id_type=MESH,
        )
        for j, (px, py) in enumerate(_other_chips(x, y))
    ]


def _row_tile(R, C, max_elems):
    if R * C <= max_elems:
        return R
    best = None
    for tr in range(16, R, 16):
        if R % tr == 0 and tr * C <= max_elems:
            best = tr
    return best or R


def pair_sum(a42, land4, core, name):
    _, _, R, C = a42.shape
    tr = _row_tile(R, C, 1 << 20)

    def body(core_ref, a_ref, l_ref, o_ref):
        o_ref[...] = (a_ref[0].astype(F32) + l_ref[...].astype(F32)).astype(o_ref.dtype)

    return pl.pallas_call(
        body,
        grid_spec=pltpu.PrefetchScalarGridSpec(
            num_scalar_prefetch=1,
            grid=(4, R // tr),
            in_specs=[
                pl.BlockSpec((1, 1, tr, C), lambda k, r, core_ref: (k, core_ref[0], r, 0)),
                pl.BlockSpec((1, tr, C), lambda k, r, core_ref: (k, r, 0)),
            ],
            out_specs=pl.BlockSpec((1, tr, C), lambda k, r, core_ref: (k, r, 0)),
        ),
        out_shape=jax.ShapeDtypeStruct((4, R, C), BF16),
        compiler_params=_cp(("parallel", "parallel")),
        name=name,
    )(core, a42, land4)


def sum_slots(parts, name):
    P, R, C = parts.shape

    def body(p_ref, o_ref):
        acc = p_ref[0].astype(F32)
        for k in range(1, P):
            acc = acc + p_ref[k].astype(F32)
        o_ref[...] = acc

    tr = _row_tile(R, P * C, 1 << 21)
    return pl.pallas_call(
        body,
        grid=(R // tr,),
        in_specs=[pl.BlockSpec((P, tr, C), lambda r: (0, r, 0))],
        out_specs=pl.BlockSpec((tr, C), lambda r: (r, 0)),
        out_shape=jax.ShapeDtypeStruct((R, C), F32),
        compiler_params=_cp(("parallel",)),
        name=name,
    )(parts)


def adamw(w, m, v, parts, name):
    R, C = w.shape
    P = parts.shape[0]
    tr = _pick(R, (256, 128, 64, 32, 16, 8))
    c1 = 1.0 - ADAM_B1 ** ADAM_STEP
    c2 = 1.0 - ADAM_B2 ** ADAM_STEP

    def body(w_ref, m_ref, v_ref, p_ref, g_ref, d_ref, nm_ref, nv_ref):
        g = p_ref[0].astype(F32)
        for k in range(1, P):
            g = g + p_ref[k].astype(F32)
        nm = ADAM_B1 * m_ref[...] + (1.0 - ADAM_B1) * g
        nv = ADAM_B2 * v_ref[...] + (1.0 - ADAM_B2) * (g * g)
        g_ref[...] = g
        nm_ref[...] = nm
        nv_ref[...] = nv
        d_ref[...] = -ADAM_LR * ((nm / c1) / (jnp.sqrt(nv / c2) + ADAM_EPS) + ADAM_WD * w_ref[...])

    blk = pl.BlockSpec((tr, C), lambda r: (r, 0))
    shp = jax.ShapeDtypeStruct((R, C), F32)
    return pl.pallas_call(
        body,
        grid=(R // tr,),
        in_specs=[blk, blk, blk, pl.BlockSpec((P, tr, C), lambda r: (0, r, 0))],
        out_specs=[blk, blk, blk, blk],
        out_shape=[shp, shp, shp, shp],
        compiler_params=_cp(("parallel",)),
        name=name,
    )(w, m, v, parts)


def adamw_reduced(w, m, v, own, land, chip, name):
    R, C = w.shape
    tr = _pick(R, (256, 128, 64, 32, 16, 8))
    c1 = 1.0 - ADAM_B1 ** ADAM_STEP
    c2 = 1.0 - ADAM_B2 ** ADAM_STEP

    def body(chip_ref, w_ref, m_ref, v_ref, own_ref, land_ref, g_ref, d_ref, nm_ref, nv_ref):
        mine = own_ref[0].astype(F32)
        g = None
        for k in range(4):
            term = jnp.where(chip_ref[0] == k, mine, land_ref[k].astype(F32))
            g = term if g is None else g + term
        nm = ADAM_B1 * m_ref[...] + (1.0 - ADAM_B1) * g
        nv = ADAM_B2 * v_ref[...] + (1.0 - ADAM_B2) * (g * g)
        g_ref[...] = g
        nm_ref[...] = nm
        nv_ref[...] = nv
        d_ref[...] = -ADAM_LR * ((nm / c1) / (jnp.sqrt(nv / c2) + ADAM_EPS) + ADAM_WD * w_ref[...])

    blk = pl.BlockSpec((tr, C), lambda r, chip_ref: (r, 0))
    shp = jax.ShapeDtypeStruct((R, C), F32)
    return pl.pallas_call(
        body,
        grid_spec=pltpu.PrefetchScalarGridSpec(
            num_scalar_prefetch=1,
            grid=(R // tr,),
            in_specs=[
                blk, blk, blk,
                pl.BlockSpec((1, tr, C), lambda r, chip_ref: (chip_ref[0], r, 0)),
                pl.BlockSpec((4, tr, C), lambda r, chip_ref: (0, r, 0)),
            ],
            out_specs=[blk, blk, blk, blk],
        ),
        out_shape=[shp, shp, shp, shp],
        compiler_params=_cp(("parallel",)),
        name=name,
    )(chip, w, m, v, own, land)


_WEIGHTS = [
    "l0_mix_norm_g", "l0_w_in", "l0_sc_conv_w", "l0_w_out", "l0_ffn_norm_g", "l0_ffn_up", "l0_ffn_conv_w", "l0_ffn_down",
    "l1_mix_norm_g", "l1_w_in", "l1_fox_b_f", "l1_sg_w", "l1_sg_b", "l1_sg_norm_g", "l1_w_out", "l1_ffn_norm_g",
    "l1_ffn_up", "l1_ffn_conv_w", "l1_ffn_down", "final_norm_g",
]
_COL_SHARDED = ["l0_w_in", "l0_ffn_up", "l1_w_in", "l1_ffn_up"]
_ROW_SHARDED = ["l0_w_out", "l0_ffn_down", "l1_w_out", "l1_ffn_down"]
_BIG = ["l0_w_in", "l0_w_out", "l0_ffn_up", "l0_ffn_down", "l1_w_in", "l1_w_out", "l1_ffn_up", "l1_ffn_down"]
_CONV = ["l0_sc_conv_w", "l0_ffn_conv_w", "l1_ffn_conv_w"]
_SMALL = [n for n in _WEIGHTS if n not in _BIG]
_PACK_ROWS = 8


def _pack(arrs):
    flat = []
    for a in arrs:
        v = a.reshape(-1).astype(F32)
        pad = (-v.shape[0]) % (_PACK_ROWS * 128)
        flat.append(jnp.pad(v, (0, pad)))
    return jnp.concatenate(flat).reshape(-1, 128)


def _unpack(packed, shapes):
    out, off = [], 0
    flat = packed.reshape(-1)
    for shp in shapes:
        size = math.prod(shp)
        out.append(flat[off : off + size].reshape(shp))
        off += size + (-size) % (_PACK_ROWS * 128)
    return out


def kernel(x, l0_mix_norm_g, l0_w_in, l0_sc_conv_w, l0_w_out, l0_ffn_norm_g, l0_ffn_up, l0_ffn_conv_w, l0_ffn_down, l1_mix_norm_g, l1_w_in, l1_fox_b_f, l1_sg_w, l1_sg_b, l1_sg_norm_g, l1_w_out, l1_ffn_norm_g, l1_ffn_up, l1_ffn_conv_w, l1_ffn_down, final_norm_g, loss_target, m_l0_mix_norm_g, m_l0_w_in, m_l0_sc_conv_w, m_l0_w_out, m_l0_ffn_norm_g, m_l0_ffn_up, m_l0_ffn_conv_w, m_l0_ffn_down, m_l1_mix_norm_g, m_l1_w_in, m_l1_fox_b_f, m_l1_sg_w, m_l1_sg_b, m_l1_sg_norm_g, m_l1_w_out, m_l1_ffn_norm_g, m_l1_ffn_up, m_l1_ffn_conv_w, m_l1_ffn_down, m_final_norm_g, v_l0_mix_norm_g, v_l0_w_in, v_l0_sc_conv_w, v_l0_w_out, v_l0_ffn_norm_g, v_l0_ffn_up, v_l0_ffn_conv_w, v_l0_ffn_down, v_l1_mix_norm_g, v_l1_w_in, v_l1_fox_b_f, v_l1_sg_w, v_l1_sg_b, v_l1_sg_norm_g, v_l1_w_out, v_l1_ffn_norm_g, v_l1_ffn_up, v_l1_ffn_conv_w, v_l1_ffn_down, v_final_norm_g):
    given = dict(locals())
    w = {n: given[n] for n in _WEIGHTS}
    mom = {n: given["m_" + n] for n in _WEIGHTS}
    var = {n: given["v_" + n] for n in _WEIGHTS}
    xs, target = x[0], loss_target[0]
    S, D = xs.shape
    W = D // 2
    nh = W // HD
    cx, cy, cc = _place()
    me = 4 * cx + 2 * cy + cc

    wts = {"nb": NDEV, "F": l0_ffn_down.shape[0] * NDEV}
    for n in _SMALL:
        if n not in _CONV:
            wts[n] = w[n]

    def start_gather(n, after=None):
        src = [w[n].astype(BF16)] + ([w[c] for c in _CONV] if n == _BIG[0] else [])
        if after is not None:
            src, after = lax.optimization_barrier((src, after))
        got = all_gather(src, f"gather_{n}")
        if n == "l1_w_in":
            w_in1 = got[0].transpose(1, 0, 2).reshape(D, -1)
            wts["l1_w_in_main"] = w_in1[:, : 5 * W]
            wts["l1_w_in_f"] = jnp.pad(w_in1[:, 5 * W :], ((0, 0), (0, 128 - nh)))
        elif n in _ROW_SHARDED:
            wts[n] = got[0].reshape(-1, D)
        else:
            wts[n] = got[0].reshape(NDEV * D, -1)
        for c, taps in zip(_CONV, got[1:]):
            wts[c] = taps.transpose(1, 0, 2).reshape(CONV_K, -1)
        return after

    core = jnp.reshape(cc, (1,)).astype(jnp.int32)
    chip = jnp.reshape(2 * cx + cy, (1,)).astype(jnp.int32)
    pair_flying, chip_flying = [], []
    out_g, out_d, out_m, out_v = {}, {}, {}, {}

    def tie(value, after):
        if after is None:
            return value, None
        return lax.optimization_barrier((value, after))

    def to_chips(after):
        n, flying = pair_flying.pop()
        landed = _split_wait(_pair_copies, *flying, f"reduce_pair_wait_{n}")
        summed = pair_sum(flying[2], landed, core, f"pair_sum_{n}")
        *flying, token = _split_start(_chip_copies, summed, summed.shape, 3, f"reduce_chips_{n}")
        token, after = tie(token, after)
        chip_flying.append((n, flying + [token]))
        return after

    def update(after, behind=None):
        n, flying = chip_flying.pop(0)
        if behind is not None:
            flying[4] = behind
        landed = _split_wait(_chip_copies, *flying, f"reduce_chips_wait_{n}")
        res = adamw_reduced(w[n], mom[n], var[n], flying[2], landed, chip, f"adamw_{n}")
        res, after = tie(res, after)
        out_g[n], out_d[n], out_m[n], out_v[n] = res
        return after, res[0]

    def on_grad(n, term, after):
        if n == "l1_w_in":
            term = term.reshape(D, NDEV, -1).transpose(1, 0, 2)
        elif n in _ROW_SHARDED:
            term = term.reshape(NDEV, -1, D)
        else:
            term = term.reshape(NDEV, D, -1)
        term = term.reshape((4, 2) + term.shape[1:])
        *flying, token = _split_start(_pair_copies, term, term.shape[:1] + term.shape[2:], 4, f"reduce_pair_{n}")
        token, after = tie(token, after)
        if len(chip_flying) == UPDATE_LAG:
            after, _ = update(after)
        if pair_flying:
            after = to_chips(after)
        pair_flying.append((n, flying + [token]))
        return after

    for n in _BIG:
        start_gather(n)
    loss_tile, dx, g = local_step(xs, target, wts, lambda n, after: after, on_grad)
    to_chips(None)
    done = None
    while chip_flying:
        _, done = update(None, behind=done)
    loss = lax.psum(loss_tile[0, 0], ("x", "y", "c"))

    small_terms = [g[n] for n in _SMALL]
    small_shapes = [tuple(t.shape) for t in small_terms]
    packed = _pack(small_terms)
    all_terms = all_gather([packed], "gather_small_grads")[0]
    small_sum = _unpack(sum_slots(all_terms, "sum_small_grads"), small_shapes)
    small_g = {}
    for n, t in zip(_SMALL, small_sum):
        if n in _CONV:
            cols = w[n].shape[1]
            t = lax.dynamic_slice_in_dim(t, me * cols, cols, axis=1)
        small_g[n] = t.reshape(w[n].shape)
    shapes = [w[n].shape for n in _SMALL]
    res = adamw(
        _pack([w[n] for n in _SMALL]), _pack([mom[n] for n in _SMALL]), _pack([var[n] for n in _SMALL]),
        _pack([small_g[n] for n in _SMALL])[None], "adamw_small",
    )
    for dst, packed_out in zip((out_g, out_d, out_m, out_v), res):
        for n, t in zip(_SMALL, _unpack(packed_out, shapes)):
            dst[n] = t

    return (loss, dx[None], *[out_g[n] for n in _WEIGHTS], *[out_d[n] for n in _WEIGHTS],
            *[out_m[n] for n in _WEIGHTS], *[out_v[n] for n in _WEIGHTS])
```

```python
import functools
import math

import jax
import jax.numpy as jnp
from jax import lax
from jax.experimental import pallas as pl
from jax.experimental.pallas import tpu as pltpu
from jax.experimental.pallas import tpu_sc as plsc

F32 = jnp.float32
BF16 = jnp.bfloat16
HD = 128
EPS = 1e-6
CONV_K = 3
VMEM_LIMIT_BYTES = 48 << 20
NDEV = 8
MESH = pl.DeviceIdType.MESH

ADAM_LR = 0.001
ADAM_B1 = 0.9
ADAM_B2 = 0.999
ADAM_EPS = 1e-08
ADAM_WD = 0.01
ADAM_STEP = 10


def _cp(sem):
    return pltpu.CompilerParams(dimension_semantics=sem, vmem_limit_bytes=VMEM_LIMIT_BYTES)


def _pick(n, prefs):
    for p in prefs:
        if n % p == 0:
            return p
    return n


def _dot(a, b):
    return jnp.dot(a, b, preferred_element_type=F32)


def _dot_nt(a, b):
    return lax.dot_general(a, b, (((1,), (1,)), ((), ())), preferred_element_type=F32)


def _dot_tn(a, b):
    return lax.dot_general(a, b, (((0,), (0,)), ((), ())), preferred_element_type=F32)


def _split3(x):
    hi = x.astype(BF16)
    r = x - hi.astype(F32)
    mid = r.astype(BF16)
    lo = (r - mid.astype(F32)).astype(BF16)
    return hi, mid, lo


def _dot_ones_right(x, ones_bf16):
    hi, mid, lo = _split3(x)
    return _dot(hi, ones_bf16) + _dot(mid, ones_bf16) + _dot(lo, ones_bf16)


def _dot_ones_left(ones_bf16, x):
    hi, mid, lo = _split3(x)
    return _dot(ones_bf16, hi) + _dot(ones_bf16, mid) + _dot(ones_bf16, lo)


def _iota2(shape, axis):
    return lax.broadcasted_iota(jnp.int32, shape, axis)


def mm_nn(a, w2d, nb, name, out_dtype=BF16, res=None, tm=None, tn=None, tk=None):
    M, K = a.shape
    n = w2d.shape[1]
    assert w2d.shape[0] == nb * K
    tm = tm or _pick(M, (1024, 512, 256, 128))
    tn = tn or _pick(n, (1408, 1024, 768, 512, 256, 128))
    tk = tk or (K if K <= 2048 else _pick(K, (1408, 1024, 512, 256, 128)))
    nk, nt = K // tk, n // tn
    has_res = res is not None

    def body(*refs):
        if has_res:
            a_ref, w_ref, r_ref, o_ref = refs[:4]
        else:
            a_ref, w_ref, o_ref = refs[:3]
            r_ref = None
        part = _dot(a_ref[...], w_ref[...])

        def finish(acc):
            if r_ref is not None:
                acc = acc + r_ref[...].astype(F32)
            o_ref[...] = acc.astype(o_ref.dtype)

        if nk == 1:
            finish(part)
        else:
            acc_ref = refs[-1]
            k = pl.program_id(3)

            @pl.when(k == 0)
            def _():
                acc_ref[...] = part

            @pl.when(k > 0)
            def _():
                acc_ref[...] += part

            @pl.when(k == nk - 1)
            def _():
                finish(acc_ref[...])

    in_specs = [
        pl.BlockSpec((tm, tk), lambda i, j, t, k: (i, k)),
        pl.BlockSpec((tk, tn), lambda i, j, t, k: (j * nk + k, t)),
    ]
    args = [a, w2d]
    out_spec = pl.BlockSpec((tm, tn), lambda i, j, t, k: (i, j * nt + t))
    if has_res:
        in_specs.append(out_spec)
        args.append(res)
    return pl.pallas_call(
        body,
        grid=(M // tm, nb, nt, nk),
        in_specs=in_specs,
        out_specs=out_spec,
        out_shape=jax.ShapeDtypeStruct((M, nb * n), out_dtype),
        scratch_shapes=[pltpu.VMEM((tm, tn), F32)] if nk > 1 else [],
        compiler_params=_cp(("parallel", "parallel", "parallel", "arbitrary")),
        name=name,
    )(*args)


def mm_nt(dy2d, w2d, nb, M, K, name, out_dtype=BF16, res=None, dy_maps=None, tm=None, tko=None, tn=None):
    n = w2d.shape[1]
    assert w2d.shape[0] == nb * K
    tm = tm or _pick(M, (1024, 512, 256, 128))
    tko = tko or _pick(K, (1024, 512, 256, 128))
    tn = tn or _pick(n, (1408, 1024, 768, 512, 256, 128))
    nt, nko = n // tn, K // tko
    has_res = res is not None
    if dy_maps is None:
        dy_maps = [lambda i, j, t: (i, j * nt + t)]
    nd = len(dy_maps)
    td = tn // nd

    def body(*refs):
        d_refs, w_ref = refs[:nd], refs[nd]
        r_ref = refs[nd + 1] if has_res else None
        o_ref, acc_ref = refs[-2], refs[-1]
        j, t = pl.program_id(2), pl.program_id(3)
        d = d_refs[0][...] if nd == 1 else jnp.concatenate([r[...] for r in d_refs], axis=1)
        part = _dot_nt(d, w_ref[...])
        first = jnp.logical_and(j == 0, t == 0)
        last = jnp.logical_and(j == nb - 1, t == nt - 1)

        @pl.when(first)
        def _():
            acc_ref[...] = part

        @pl.when(jnp.logical_not(first))
        def _():
            acc_ref[...] += part

        @pl.when(last)
        def _():
            acc = acc_ref[...]
            if r_ref is not None:
                acc = acc + r_ref[...].astype(F32)
            o_ref[...] = acc.astype(o_ref.dtype)

    in_specs = [pl.BlockSpec((tm, td), functools.partial(lambda f, i, ko, j, t: f(i, j, t), f)) for f in dy_maps]
    in_specs.append(pl.BlockSpec((tko, tn), lambda i, ko, j, t: (j * nko + ko, t)))
    args = [dy2d] * nd + [w2d]
    out_spec = pl.BlockSpec((tm, tko), lambda i, ko, j, t: (i, ko))
    if has_res:
        in_specs.append(out_spec)
        args.append(res)
    return pl.pallas_call(
        body,
        grid=(M // tm, nko, nb, nt),
        in_specs=in_specs,
        out_specs=out_spec,
        out_shape=jax.ShapeDtypeStruct((M, K), out_dtype),
        scratch_shapes=[pltpu.VMEM((tm, tko), F32)],
        compiler_params=_cp(("parallel", "parallel", "arbitrary", "arbitrary")),
        name=name,
    )(*args)


def mm_tn(x, dy2d, nb, n, name, out_dtype=BF16, dy_maps=None, tko=None, tn=None):
    S, K = x.shape
    tko = tko or _pick(K, (512, 256, 128))
    tn = tn or _pick(n, (1408, 1024, 768, 512, 256, 128))
    nt, nko = n // tn, K // tko
    if dy_maps is None:
        dy_maps = [lambda j, t: (0, j * nt + t)]
    nd = len(dy_maps)
    td = tn // nd

    def body(*refs):
        x_ref, d_refs, o_ref = refs[0], refs[1 : 1 + nd], refs[-1]
        d = d_refs[0][...] if nd == 1 else jnp.concatenate([r[...] for r in d_refs], axis=1)
        o_ref[...] = _dot_tn(x_ref[...], d).astype(o_ref.dtype)

    in_specs = [pl.BlockSpec((S, tko), lambda ko, j, t: (0, ko))]
    in_specs += [pl.BlockSpec((S, td), functools.partial(lambda f, ko, j, t: f(j, t), f)) for f in dy_maps]
    return pl.pallas_call(
        body,
        grid=(nko, nb, nt),
        in_specs=in_specs,
        out_specs=pl.BlockSpec((tko, tn), lambda ko, j, t: (j * nko + ko, t)),
        out_shape=jax.ShapeDtypeStruct((nb * K, n), out_dtype),
        compiler_params=_cp(("parallel", "parallel", "parallel")),
        name=name,
    )(x, *([dy2d] * nd))


def rms_fwd(x, g, name):
    S, D = x.shape
    tm = _pick(S, (256, 128))

    def body(x_ref, g_ref, o_ref):
        xf = x_ref[...]
        r = lax.rsqrt(jnp.mean(xf * xf, axis=-1, keepdims=True) + EPS)
        o_ref[...] = (xf * r * g_ref[...]).astype(o_ref.dtype)

    return pl.pallas_call(
        body,
        grid=(S // tm,),
        in_specs=[pl.BlockSpec((tm, D), lambda i: (i, 0)), pl.BlockSpec((1, D), lambda i: (0, 0))],
        out_specs=pl.BlockSpec((tm, D), lambda i: (i, 0)),
        out_shape=jax.ShapeDtypeStruct((S, D), BF16),
        compiler_params=_cp(("parallel",)),
        name=name,
    )(x, g.reshape(1, D))


def rms_bwd(x, g, dh, dres, name):
    S, D = x.shape
    tm = _pick(S, (256, 128))

    def body(x_ref, g_ref, dh_ref, dr_ref, dx_ref, dxb_ref, dg_ref):
        i = pl.program_id(0)
        xf = x_ref[...]
        dh = dh_ref[...].astype(F32)
        r = lax.rsqrt(jnp.mean(xf * xf, axis=-1, keepdims=True) + EPS)
        gy = dh * g_ref[...]
        proj = jnp.mean(gy * xf, axis=-1, keepdims=True)
        dx = dr_ref[...] + r * gy - xf * (r * r * r * proj)
        dx_ref[...] = dx
        dxb_ref[...] = dx.astype(BF16)
        dg = jnp.sum(dh * (xf * r), axis=0, keepdims=True)

        @pl.when(i == 0)
        def _():
            dg_ref[...] = dg

        @pl.when(i > 0)
        def _():
            dg_ref[...] += dg

    row = pl.BlockSpec((tm, D), lambda i: (i, 0))
    vec = pl.BlockSpec((1, D), lambda i: (0, 0))
    return pl.pallas_call(
        body,
        grid=(S // tm,),
        in_specs=[row, vec, row, row],
        out_specs=[row, row, vec],
        out_shape=[jax.ShapeDtypeStruct((S, D), F32), jax.ShapeDtypeStruct((S, D), BF16), jax.ShapeDtypeStruct((1, D), F32)],
        compiler_params=_cp(("arbitrary",)),
        name=name,
    )(x, g.reshape(1, D), dh, dres)


def loss_head(x, g, target, name):
    S, D = x.shape
    tm = _pick(S, (256, 128))

    def body(x_ref, g_ref, t_ref, dx_ref, dxb_ref, dg_ref, loss_ref):
        i = pl.program_id(0)
        xf = x_ref[...]
        gg = g_ref[...]
        r = lax.rsqrt(jnp.mean(xf * xf, axis=-1, keepdims=True) + EPS)
        xh = xf * r
        err = xh * gg - t_ref[...]
        part = (0.5 / D) * jnp.sum(err * err)
        dy = err * (1.0 / D)
        gy = dy * gg
        proj = jnp.mean(gy * xf, axis=-1, keepdims=True)
        dx = r * gy - xf * (r * r * r * proj)
        dx_ref[...] = dx
        dxb_ref[...] = dx.astype(BF16)
        dg = jnp.sum(dy * xh, axis=0, keepdims=True)
        lossb = jnp.full(loss_ref.shape, part, F32)

        @pl.when(i == 0)
        def _():
            dg_ref[...] = dg
            loss_ref[...] = lossb

        @pl.when(i > 0)
        def _():
            dg_ref[...] += dg
            loss_ref[...] += lossb

    row = pl.BlockSpec((tm, D), lambda i: (i, 0))
    vec = pl.BlockSpec((1, D), lambda i: (0, 0))
    return pl.pallas_call(
        body,
        grid=(S // tm,),
        in_specs=[row, vec, row],
        out_specs=[row, row, vec, pl.BlockSpec((8, 128), lambda i: (0, 0))],
        out_shape=[
            jax.ShapeDtypeStruct((S, D), F32),
            jax.ShapeDtypeStruct((S, D), BF16),
            jax.ShapeDtypeStruct((1, D), F32),
            jax.ShapeDtypeStruct((8, 128), F32),
        ],
        compiler_params=_cp(("arbitrary",)),
        name=name,
    )(x, g.reshape(1, D), target)


def _shift_down(s, k):
    if k == 0:
        return s
    return jnp.where(_iota2(s.shape, 0) >= k, pltpu.roll(s, k, axis=0), 0.0)


def _shift_up(s, k):
    if k == 0:
        return s
    n = s.shape[0]
    return jnp.where(_iota2(s.shape, 0) < n - k, pltpu.roll(s, n - k, axis=0), 0.0)


def _conv(s, w):
    return w[0:1] * _shift_down(s, 2) + w[1:2] * _shift_down(s, 1) + w[2:3] * s


def _conv_t(d, w):
    return w[2:3] * d + w[1:2] * _shift_up(d, 1) + w[0:1] * _shift_up(d, 2)


def _conv_dw(d, s):
    return [jnp.sum(d * _shift_down(s, CONV_K - 1 - k), axis=0, keepdims=True) for k in range(CONV_K)]


def sc_fwd(p, convw, cat, W, name):
    S = p.shape[0]
    tc = _pick(W, (256, 128))
    nc = W // tc

    def body(gb_ref, gc_ref, hi_ref, w_ref, cat_ref, o_ref):
        s = gc_ref[...].astype(F32) * hi_ref[...].astype(F32)
        o_ref[...] = (gb_ref[...].astype(F32) * _conv(s, w_ref[...])).astype(o_ref.dtype)

    col = lambda part: pl.BlockSpec((S, tc), lambda c: (0, part * nc + c))
    return pl.pallas_call(
        body,
        grid=(nc,),
        in_specs=[col(3), col(4), col(5), pl.BlockSpec((CONV_K, tc), lambda c: (0, c)), pl.BlockSpec(memory_space=pl.ANY)],
        out_specs=col(1),
        out_shape=jax.ShapeDtypeStruct(cat.shape, cat.dtype),
        input_output_aliases={4: 0},
        compiler_params=_cp(("parallel",)),
        name=name,
    )(p, p, p, convw, cat)


def sc_bwd(p, convw, dcat, dp, W, name):
    S = p.shape[0]
    tc = _pick(W, (256, 128))
    nc = W // tc

    def body(gb_ref, gc_ref, hi_ref, w_ref, do_ref, dp_in_ref, dp_ref, dw_ref):
        gb = gb_ref[...].astype(F32)
        gc = gc_ref[...].astype(F32)
        hi = hi_ref[...].astype(F32)
        w = w_ref[...]
        do = do_ref[...].astype(F32)
        s = gc * hi
        dcs = do * gb
        ds = _conv_t(dcs, w)
        dp_ref[0] = (do * _conv(s, w)).astype(dp_ref.dtype)
        dp_ref[1] = (ds * hi).astype(dp_ref.dtype)
        dp_ref[2] = (ds * gc).astype(dp_ref.dtype)
        for k, row in enumerate(_conv_dw(dcs, s)):
            dw_ref[k : k + 1, :] = row

    col = lambda part: pl.BlockSpec((S, tc), lambda c: (0, part * nc + c))
    return pl.pallas_call(
        body,
        grid=(nc,),
        in_specs=[
            col(3), col(4), col(5),
            pl.BlockSpec((CONV_K, tc), lambda c: (0, c)),
            pl.BlockSpec((S, tc), lambda c: (0, nc + c)),
            pl.BlockSpec(memory_space=pl.ANY),
        ],
        out_specs=[pl.BlockSpec((3, S, tc), lambda c: (1, 0, c)), pl.BlockSpec((CONV_K, tc), lambda c: (0, c))],
        out_shape=[jax.ShapeDtypeStruct(dp.shape, dp.dtype), jax.ShapeDtypeStruct((CONV_K, W), F32)],
        input_output_aliases={5: 0},
        compiler_params=_cp(("parallel",)),
        name=name,
    )(p, p, p, convw, dcat, dp)


def _silu_parts(a):
    sig = 1.0 / (1.0 + jnp.exp(-a))
    return a * sig, sig


def ffn_act_fwd(u, convw, F, name):
    S = u.shape[0]
    tc = _pick(F, (256, 128))
    nc = F // tc

    def body(ug_ref, uu_ref, wg_ref, wu_ref, o_ref):
        ag = _conv(ug_ref[...].astype(F32), wg_ref[...])
        au = _conv(uu_ref[...].astype(F32), wu_ref[...])
        o_ref[...] = (_silu_parts(ag)[0] * au).astype(o_ref.dtype)

    col = lambda half: pl.BlockSpec((S, tc), lambda c: (0, half * nc + c))
    wcol = lambda half: pl.BlockSpec((CONV_K, tc), lambda c: (0, half * nc + c))
    return pl.pallas_call(
        body,
        grid=(nc,),
        in_specs=[col(0), col(1), wcol(0), wcol(1)],
        out_specs=pl.BlockSpec((S, tc), lambda c: (0, c)),
        out_shape=jax.ShapeDtypeStruct((S, F), BF16),
        compiler_params=_cp(("parallel",)),
        name=name,
    )(u, u, convw, convw)


def ffn_act_bwd(u, convw, dact, F, name):
    S = u.shape[0]
    tc = _pick(F, (256, 128))
    nc = F // tc

    def body(ug_ref, uu_ref, wg_ref, wu_ref, da_ref, du_ref, dw_ref):
        ug = ug_ref[...].astype(F32)
        uu = uu_ref[...].astype(F32)
        wg = wg_ref[...]
        wu = wu_ref[...]
        da = da_ref[...].astype(F32)
        ag = _conv(ug, wg)
        au = _conv(uu, wu)
        sl, sig = _silu_parts(ag)
        dag = da * au * (sig * (1.0 + ag * (1.0 - sig)))
        dau = da * sl
        du_ref[0] = _conv_t(dag, wg).astype(du_ref.dtype)
        du_ref[1] = _conv_t(dau, wu).astype(du_ref.dtype)
        for k, (rg, ru) in enumerate(zip(_conv_dw(dag, ug), _conv_dw(dau, uu))):
            dw_ref[0, k : k + 1, :] = rg
            dw_ref[1, k : k + 1, :] = ru

    col = lambda half: pl.BlockSpec((S, tc), lambda c: (0, half * nc + c))
    wcol = lambda half: pl.BlockSpec((CONV_K, tc), lambda c: (0, half * nc + c))
    return pl.pallas_call(
        body,
        grid=(nc,),
        in_specs=[col(0), col(1), wcol(0), wcol(1), pl.BlockSpec((S, tc), lambda c: (0, c))],
        out_specs=[pl.BlockSpec((2, S, tc), lambda c: (0, 0, c)), pl.BlockSpec((2, CONV_K, tc), lambda c: (0, 0, c))],
        out_shape=[jax.ShapeDtypeStruct((2, S, F), BF16), jax.ShapeDtypeStruct((2, CONV_K, F), F32)],
        compiler_params=_cp(("parallel",)),
        name=name,
    )(u, u, convw, convw, dact)


def _softplus(z):
    return jnp.maximum(z, 0.0) + jnp.log(1.0 + jnp.exp(-jnp.abs(z)))


def _key_strip(S):
    return _pick(S, (512, 256, 128))


def _query_rows(S):
    return _pick(S, (256, 128))


def _split2(x):
    hi = x.astype(BF16)
    return hi, (x - hi.astype(F32)).astype(BF16)


def _block_sums(x, ones_bf16):
    hi, lo = _split2(x)
    return [
        _dot(hi[:, b * HD : (b + 1) * HD], ones_bf16) + _dot(lo[:, b * HD : (b + 1) * HD], ones_bf16)
        for b in range(x.shape[1] // HD)
    ]


def _strip_mask(shape, row0, off, strict):
    cols, rows = _iota2(shape, 1) + off, _iota2(shape, 0) + row0
    return cols < rows if strict else cols <= rows


def _sb_strip(q, ks, row0, off, run, su):
    z = _dot_nt(q, ks) * (HD ** -0.5)
    mask = _strip_mask(z.shape, row0, off, True)
    sp = _softplus(z)
    l = jnp.where(mask, -sp, 0.0)
    within = _block_sums(l, su)
    later = [None] * len(within)
    for b in reversed(range(len(within))):
        later[b] = within[b] + run
        run = run + jnp.sum(l[:, b * HD : (b + 1) * HD], axis=1, keepdims=True)
    a = jnp.where(mask, jnp.exp(z - sp + jnp.concatenate(later, axis=1)), 0.0)
    return z, mask, a, run


def sb_fwd(p, W, name):
    S = p.shape[0]
    TQ, TK = _query_rows(S), _key_strip(S)
    nh, nq = W // HD, S // TQ

    def body(q_ref, k_ref, v_ref, o_ref):
        i = pl.program_id(1)
        q = q_ref[...]
        su = (_iota2((HD, HD), 0) > _iota2((HD, HD), 1)).astype(BF16)
        last = (i * TQ + TQ - 1) // TK

        def step(gg, carry):
            acc, run = carry
            off = pl.multiple_of((last - gg) * TK, TK)
            _, _, a, run = _sb_strip(q, k_ref[pl.ds(off, TK), :], i * TQ, off, run, su)
            return acc + _dot(a.astype(BF16), v_ref[pl.ds(off, TK), :]), run

        acc, _ = lax.fori_loop(0, last + 1, step, (jnp.zeros((TQ, HD), F32), jnp.zeros((TQ, 1), F32)))
        o_ref[...] = acc.astype(o_ref.dtype)

    return pl.pallas_call(
        body,
        grid=(nh, nq),
        in_specs=[
            pl.BlockSpec((TQ, HD), lambda h, i: (i, h)),
            pl.BlockSpec((S, HD), lambda h, i: (0, nh + h)),
            pl.BlockSpec((S, HD), lambda h, i: (0, 2 * nh + h)),
        ],
        out_specs=pl.BlockSpec((TQ, HD), lambda h, i: (i, h)),
        out_shape=jax.ShapeDtypeStruct((S, 2 * W), BF16),
        compiler_params=_cp(("parallel", "arbitrary")),
        name=name,
    )(p, p, p)


def sb_bwd(p, dcat, W, name):
    S = p.shape[0]
    TQ, TK = _query_rows(S), _key_strip(S)
    nh, nq = W // HD, S // TQ
    scale = HD ** -0.5

    def body(q_ref, k_ref, v_ref, do_ref, dp_ref, dk_acc, dv_acc, e_scr, z_scr):
        i = pl.program_id(1)
        q = q_ref[...]
        do = do_ref[...]
        su = (_iota2((HD, HD), 0) > _iota2((HD, HD), 1)).astype(BF16)
        sl = (_iota2((HD, HD), 0) < _iota2((HD, HD), 1)).astype(BF16)
        last = (i * TQ + TQ - 1) // TK

        @pl.when(i == 0)
        def _():
            dk_acc[...] = jnp.zeros_like(dk_acc)
            dv_acc[...] = jnp.zeros_like(dv_acc)

        def pass_a(gg, run):
            g = last - gg
            off = pl.multiple_of(g * TK, TK)
            z, _, a, run = _sb_strip(q, k_ref[pl.ds(off, TK), :], i * TQ, off, run, su)
            e_scr[g] = a * _dot_nt(do, v_ref[pl.ds(off, TK), :])
            z_scr[g] = z
            dv_acc[pl.ds(off, TK), :] += _dot_tn(a.astype(BF16), do)
            return run

        lax.fori_loop(0, last + 1, pass_a, jnp.zeros((TQ, 1), F32))

        def pass_b(g, carry):
            dq, run_e = carry
            off = pl.multiple_of(g * TK, TK)
            e = e_scr[g]
            z = z_scr[g]
            mask = _strip_mask(z.shape, i * TQ, off, True)
            within = _block_sums(e, sl)
            before = []
            for b in range(len(within)):
                before.append(within[b] + run_e)
                run_e = run_e + jnp.sum(e[:, b * HD : (b + 1) * HD], axis=1, keepdims=True)
            sig = 1.0 / (1.0 + jnp.exp(-z))
            dz = jnp.where(mask, e * (1.0 - sig) - jnp.concatenate(before, axis=1) * sig, 0.0)
            dz = (dz * scale).astype(BF16)
            dq = dq + _dot(dz, k_ref[pl.ds(off, TK), :])
            dk_acc[pl.ds(off, TK), :] += _dot_tn(dz, q)
            return dq, run_e

        dq, _ = lax.fori_loop(0, last + 1, pass_b, (jnp.zeros((TQ, HD), F32), jnp.zeros((TQ, 1), F32)))
        dp_ref[0, pl.ds(pl.multiple_of(i * TQ, TQ), TQ), :] = dq.astype(dp_ref.dtype)

        @pl.when(i == nq - 1)
        def _():
            dp_ref[1] = dk_acc[...].astype(dp_ref.dtype)
            dp_ref[2] = dv_acc[...].astype(dp_ref.dtype)

    return pl.pallas_call(
        body,
        grid=(nh, nq),
        in_specs=[
            pl.BlockSpec((TQ, HD), lambda h, i: (i, h)),
            pl.BlockSpec((S, HD), lambda h, i: (0, nh + h)),
            pl.BlockSpec((S, HD), lambda h, i: (0, 2 * nh + h)),
            pl.BlockSpec((TQ, HD), lambda h, i: (i, h)),
        ],
        out_specs=pl.BlockSpec((3, S, HD), lambda h, i: (0, 0, h)),
        out_shape=jax.ShapeDtypeStruct((6, S, W), BF16),
        scratch_shapes=[
            pltpu.VMEM((S, HD), F32),
            pltpu.VMEM((S, HD), F32),
            pltpu.VMEM((S // TK, TQ, TK), F32),
            pltpu.VMEM((S // TK, TQ, TK), F32),
        ],
        compiler_params=_cp(("parallel", "arbitrary")),
        name=name,
    )(p, p, p, dcat)


def fox_gate_fwd(f, b, name):
    S = f.shape[0]
    nq = S // HD

    def body(f_ref, b_ref, c_ref, run):
        i = pl.program_id(0)

        @pl.when(i == 0)
        def _():
            run[...] = jnp.zeros_like(run)

        lf = -_softplus(-(f_ref[...] + b_ref[...]))
        tri = (_iota2((HD, HD), 0) >= _iota2((HD, HD), 1)).astype(BF16)
        c_ref[...] = _dot_ones_left(tri, lf) + run[...]
        run[...] += jnp.sum(lf, axis=0, keepdims=True)

    return pl.pallas_call(
        body,
        grid=(nq,),
        in_specs=[pl.BlockSpec((HD, 128), lambda i: (i, 0)), pl.BlockSpec((1, 128), lambda i: (0, 0))],
        out_specs=pl.BlockSpec((HD, 128), lambda i: (i, 0)),
        out_shape=jax.ShapeDtypeStruct((S, 128), F32),
        scratch_shapes=[pltpu.VMEM((1, 128), F32)],
        compiler_params=_cp(("arbitrary",)),
        name=name,
    )(f, b)


def fox_gate_bwd(f, b, dc, name):
    S = f.shape[0]
    nq = S // HD

    def body(f_ref, b_ref, dc_ref, df_ref, db_ref, run):
        i = pl.program_id(0)

        @pl.when(i == 0)
        def _():
            run[...] = jnp.zeros_like(run)

        dc = dc_ref[...]
        tri = (_iota2((HD, HD), 0) <= _iota2((HD, HD), 1)).astype(BF16)
        dlf = _dot_ones_left(tri, dc) + run[...]
        run[...] += jnp.sum(dc, axis=0, keepdims=True)
        x = f_ref[...] + b_ref[...]
        df = dlf * (1.0 / (1.0 + jnp.exp(x)))
        df_ref[...] = df
        db = jnp.sum(df, axis=0, keepdims=True)

        @pl.when(i == 0)
        def _():
            db_ref[...] = db

        @pl.when(i > 0)
        def _():
            db_ref[...] += db

    rev = pl.BlockSpec((HD, 128), lambda i: (nq - 1 - i, 0))
    vec = pl.BlockSpec((1, 128), lambda i: (0, 0))
    return pl.pallas_call(
        body,
        grid=(nq,),
        in_specs=[rev, vec, rev],
        out_specs=[rev, vec],
        out_shape=[jax.ShapeDtypeStruct((S, 128), F32), jax.ShapeDtypeStruct((1, 128), F32)],
        scratch_shapes=[pltpu.VMEM((1, 128), F32)],
        compiler_params=_cp(("arbitrary",)),
        name=name,
    )(f, b, dc)


def _fox_logits(q, ks, ct, cs, row0, off):
    s = _dot_nt(q, ks) * (HD ** -0.5) + (ct - cs)
    mask = _strip_mask(s.shape, row0, off, False)
    return jnp.where(mask, s, -1e30), mask


def fox_fwd(p, ccol, crow, cat, W, name):
    S = p.shape[0]
    TQ, TK = _query_rows(S), _key_strip(S)
    nh, nq = W // HD, S // TQ

    def body(q_ref, k_ref, v_ref, cc_ref, cr_ref, cat_ref, o_ref, lse_ref):
        i = pl.program_id(1)
        q = q_ref[...]
        ct = cc_ref[0]

        def step(g, carry):
            m, l, acc = carry
            off = pl.multiple_of(g * TK, TK)
            s, _ = _fox_logits(q, k_ref[pl.ds(off, TK), :], ct, cr_ref[0, pl.ds(g, 1), :], i * TQ, off)
            m_new = jnp.maximum(m, jnp.max(s, axis=1, keepdims=True))
            alpha = jnp.exp(m - m_new)
            pr = jnp.exp(s - m_new)
            l = alpha * l + jnp.sum(pr, axis=1, keepdims=True)
            acc = alpha * acc + _dot(pr.astype(BF16), v_ref[pl.ds(off, TK), :])
            return m_new, l, acc

        init = (jnp.full((TQ, 1), -1e30, F32), jnp.zeros((TQ, 1), F32), jnp.zeros((TQ, HD), F32))
        m, l, acc = lax.fori_loop(0, (i * TQ + TQ - 1) // TK + 1, step, init)
        o_ref[...] = (acc / l).astype(o_ref.dtype)
        lse_ref[0] = m + jnp.log(l)

    return pl.pallas_call(
        body,
        grid=(nh, nq),
        in_specs=[
            pl.BlockSpec((TQ, HD), lambda h, i: (i, 2 * nh + h)),
            pl.BlockSpec((S, HD), lambda h, i: (0, 3 * nh + h)),
            pl.BlockSpec((S, HD), lambda h, i: (0, 4 * nh + h)),
            pl.BlockSpec((1, TQ, 1), lambda h, i: (h, i, 0)),
            pl.BlockSpec((1, S // TK, TK), lambda h, i: (h, 0, 0)),
            pl.BlockSpec(memory_space=pl.ANY),
        ],
        out_specs=[pl.BlockSpec((TQ, HD), lambda h, i: (i, nh + h)), pl.BlockSpec((1, TQ, 1), lambda h, i: (h, i, 0))],
        out_shape=[jax.ShapeDtypeStruct(cat.shape, cat.dtype), jax.ShapeDtypeStruct((nh, S, 1), F32)],
        input_output_aliases={5: 0},
        compiler_params=_cp(("parallel", "arbitrary")),
        name=name,
    )(p, p, p, ccol, crow, cat)


def fox_bwd(p, ccol, crow, cat, lse, dcat, dp, W, name):
    S = p.shape[0]
    TQ, TK = _query_rows(S), _key_strip(S)
    nh, nq = W // HD, S // TQ
    scale = HD ** -0.5

    def body(q_ref, k_ref, v_ref, cc_ref, cr_ref, o_ref, lse_ref, do_ref, dp_in_ref, dp_ref, dcs_ref, dct_ref, dk_acc, dv_acc):
        i = pl.program_id(1)
        q = q_ref[...]
        do = do_ref[...]
        ct = cc_ref[0]
        lse_i = lse_ref[0]
        delta = jnp.sum(do.astype(F32) * o_ref[...].astype(F32), axis=1, keepdims=True)

        @pl.when(i == 0)
        def _():
            dk_acc[...] = jnp.zeros_like(dk_acc)
            dv_acc[...] = jnp.zeros_like(dv_acc)
            dcs_ref[...] = jnp.zeros_like(dcs_ref)

        def step(g, carry):
            dq, dct = carry
            off = pl.multiple_of(g * TK, TK)
            ks = k_ref[pl.ds(off, TK), :]
            s, mask = _fox_logits(q, ks, ct, cr_ref[0, pl.ds(g, 1), :], i * TQ, off)
            pr = jnp.where(mask, jnp.exp(s - lse_i), 0.0)
            ds = pr * (_dot_nt(do, v_ref[pl.ds(off, TK), :]) - delta)
            dv_acc[pl.ds(off, TK), :] += _dot_tn(pr.astype(BF16), do)
            dsb = (ds * scale).astype(BF16)
            dk_acc[pl.ds(off, TK), :] += _dot_tn(dsb, q)
            dcs_ref[0, pl.ds(g, 1), :] += jnp.sum(ds, axis=0, keepdims=True)
            return dq + _dot(dsb, ks), dct + jnp.sum(ds, axis=1, keepdims=True)

        dq, dct = lax.fori_loop(0, (i * TQ + TQ - 1) // TK + 1, step, (jnp.zeros((TQ, HD), F32), jnp.zeros((TQ, 1), F32)))
        dp_ref[0, pl.ds(pl.multiple_of(i * TQ, TQ), TQ), :] = dq.astype(dp_ref.dtype)
        dct_ref[0] = dct

        @pl.when(i == nq - 1)
        def _():
            dp_ref[1] = dk_acc[...].astype(dp_ref.dtype)
            dp_ref[2] = dv_acc[...].astype(dp_ref.dtype)

    return pl.pallas_call(
        body,
        grid=(nh, nq),
        in_specs=[
            pl.BlockSpec((TQ, HD), lambda h, i: (i, 2 * nh + h)),
            pl.BlockSpec((S, HD), lambda h, i: (0, 3 * nh + h)),
            pl.BlockSpec((S, HD), lambda h, i: (0, 4 * nh + h)),
            pl.BlockSpec((1, TQ, 1), lambda h, i: (h, i, 0)),
            pl.BlockSpec((1, S // TK, TK), lambda h, i: (h, 0, 0)),
            pl.BlockSpec((TQ, HD), lambda h, i: (i, nh + h)),
            pl.BlockSpec((1, TQ, 1), lambda h, i: (h, i, 0)),
            pl.BlockSpec((TQ, HD), lambda h, i: (i, nh + h)),
            pl.BlockSpec(memory_space=pl.ANY),
        ],
        out_specs=[
            pl.BlockSpec((3, S, HD), lambda h, i: (1, 0, h)),
            pl.BlockSpec((1, S // TK, TK), lambda h, i: (h, 0, 0)),
            pl.BlockSpec((1, TQ, 1), lambda h, i: (h, i, 0)),
        ],
        out_shape=[
            jax.ShapeDtypeStruct(dp.shape, dp.dtype),
            jax.ShapeDtypeStruct((nh, S // TK, TK), F32),
            jax.ShapeDtypeStruct((nh, S, 1), F32),
        ],
        input_output_aliases={8: 0},
        scratch_shapes=[pltpu.VMEM((S, HD), F32), pltpu.VMEM((S, HD), F32)],
        compiler_params=_cp(("parallel", "arbitrary")),
        name=name,
    )(p, p, p, ccol, crow, cat, lse, dcat, dp)


_GELU_K = math.sqrt(2.0 / math.pi)
_GELU_C = 0.044715


def _gelu(x):
    return 0.5 * x * (1.0 + jnp.tanh(_GELU_K * (x + _GELU_C * x * x * x)))


def _gelu_grad(x):
    t = jnp.tanh(_GELU_K * (x + _GELU_C * x * x * x))
    return 0.5 * (1.0 + t) + 0.5 * x * (1.0 - t * t) * (_GELU_K * (1.0 + 3.0 * _GELU_C * x * x))


def _layernorm_parts(gv):
    xc = gv - jnp.mean(gv, axis=-1, keepdims=True)
    r = lax.rsqrt(jnp.mean(xc * xc, axis=-1, keepdims=True) + EPS)
    return xc * r, r


def sg_fwd(p, sg_w, sg_bt, sg_g, W, name):
    S = p.shape[0]
    G, nq = W // HD, S // HD

    def body(u_ref, v_ref, w_ref, bt_ref, g_ref, o_ref):
        xh, _ = _layernorm_parts(_gelu(v_ref[...].astype(F32)))
        vn = (xh * g_ref[...]).astype(BF16)
        tri = _iota2((HD, HD), 0) >= _iota2((HD, HD), 1)
        for gi in range(G):
            cols = slice(gi * HD, (gi + 1) * HD)
            wt = jnp.where(tri, w_ref[gi], 0.0).astype(BF16)
            mixed = _dot(wt, vn[:, cols]) + bt_ref[:, gi : gi + 1]
            o_ref[:, cols] = (_gelu(u_ref[:, cols].astype(F32)) * mixed).astype(o_ref.dtype)

    return pl.pallas_call(
        body,
        grid=(nq,),
        in_specs=[
            pl.BlockSpec((HD, W), lambda i: (i, 0)),
            pl.BlockSpec((HD, W), lambda i: (i, 1)),
            pl.BlockSpec((G, HD, HD), lambda i: (0, 0, 0)),
            pl.BlockSpec((HD, G), lambda i: (0, 0)),
            pl.BlockSpec((1, W), lambda i: (0, 0)),
        ],
        out_specs=pl.BlockSpec((HD, W), lambda i: (i, 0)),
        out_shape=jax.ShapeDtypeStruct((S, 2 * W), BF16),
        compiler_params=_cp(("parallel",)),
        name=name,
    )(p, p, sg_w, sg_bt, sg_g.reshape(1, W))


def sg_bwd(p, sg_w, sg_bt, sg_g, dcat, W, name):
    S = p.shape[0]
    G, nq = W // HD, S // HD

    def body(u_ref, v_ref, w_ref, bt_ref, g_ref, do_ref, dp_ref, dw_ref, dbt_ref, dg_ref, dvn_scr):
        i = pl.program_id(0)

        @pl.when(i == 0)
        def _():
            dw_ref[...] = jnp.zeros_like(dw_ref)
            dbt_ref[...] = jnp.zeros_like(dbt_ref)
            dg_ref[...] = jnp.zeros_like(dg_ref)

        v = v_ref[...].astype(F32)
        xh, r = _layernorm_parts(_gelu(v))
        gg = g_ref[...]
        vn = (xh * gg).astype(BF16)
        tri = _iota2((HD, HD), 0) >= _iota2((HD, HD), 1)
        for gi in range(G):
            cols = slice(gi * HD, (gi + 1) * HD)
            wt = jnp.where(tri, w_ref[gi], 0.0).astype(BF16)
            mixed = _dot(wt, vn[:, cols]) + bt_ref[:, gi : gi + 1]
            u = u_ref[:, cols].astype(F32)
            do = do_ref[:, cols].astype(F32)
            dp_ref[0, :, cols] = (do * mixed * _gelu_grad(u)).astype(dp_ref.dtype)
            dmix = do * _gelu(u)
            dmb = dmix.astype(BF16)
            dw_ref[gi] += jnp.where(tri, _dot_nt(dmb, vn[:, cols]), 0.0)
            dbt_ref[:, gi : gi + 1] += jnp.sum(dmix, axis=1, keepdims=True)
            dvn_scr[:, cols] = _dot_tn(wt, dmb)
        dvn = dvn_scr[...]
        dg_ref[...] += jnp.sum(dvn * xh, axis=0, keepdims=True)
        dxh = dvn * gg
        dgv = r * (dxh - jnp.mean(dxh, axis=-1, keepdims=True) - xh * jnp.mean(dxh * xh, axis=-1, keepdims=True))
        dp_ref[1] = (dgv * _gelu_grad(v)).astype(dp_ref.dtype)

    return pl.pallas_call(
        body,
        grid=(nq,),
        in_specs=[
            pl.BlockSpec((HD, W), lambda i: (i, 0)),
            pl.BlockSpec((HD, W), lambda i: (i, 1)),
            pl.BlockSpec((G, HD, HD), lambda i: (0, 0, 0)),
            pl.BlockSpec((HD, G), lambda i: (0, 0)),
            pl.BlockSpec((1, W), lambda i: (0, 0)),
            pl.BlockSpec((HD, W), lambda i: (i, 0)),
        ],
        out_specs=[
            pl.BlockSpec((2, HD, W), lambda i: (0, i, 0)),
            pl.BlockSpec((G, HD, HD), lambda i: (0, 0, 0)),
            pl.BlockSpec((HD, G), lambda i: (0, 0)),
            pl.BlockSpec((1, W), lambda i: (0, 0)),
        ],
        out_shape=[
            jax.ShapeDtypeStruct((6, S, W), BF16),
            jax.ShapeDtypeStruct((G, HD, HD), F32),
            jax.ShapeDtypeStruct((HD, G), F32),
            jax.ShapeDtypeStruct((1, W), F32),
        ],
        scratch_shapes=[pltpu.VMEM((HD, W), F32)],
        compiler_params=_cp(("arbitrary",)),
        name=name,
    )(p, p, sg_w, sg_bt, sg_g.reshape(1, W), dcat)


def local_step(x, target, wts, start_gather, on_grad):
    S, D = x.shape
    W = D // 2
    nb, F = wts["nb"], wts["F"]
    g = {}

    def ffn_fwd(xin, l, ahead=(None, None, None)):
        h = rms_fwd(xin, wts[f"{l}_ffn_norm_g"], f"{l}_ffn_rms")
        h = start_gather(ahead[0], h) if ahead[0] else h
        u = mm_nn(h, wts[f"{l}_ffn_up"], nb, f"{l}_ffn_up_mm")
        u = start_gather(ahead[1], u) if ahead[1] else u
        act = ffn_act_fwd(u, wts[f"{l}_ffn_conv_w"], F, f"{l}_ffn_act")
        act = start_gather(ahead[2], act) if ahead[2] else act
        xout = mm_nn(act, wts[f"{l}_ffn_down"], 1, f"{l}_ffn_down_mm", out_dtype=F32, res=xin)
        return xout, (xin, h, u, act)

    def ffn_bwd(dxout, dxoutb, saved, l):
        xin, h, u, act = saved
        dact = mm_nt(dxoutb, wts[f"{l}_ffn_down"], 1, S, F, f"{l}_ffn_down_dx")
        dact = on_grad(f"{l}_ffn_down", mm_tn(act, dxoutb, 1, D, f"{l}_ffn_down_dw"), dact)
        du, dcw = ffn_act_bwd(u, wts[f"{l}_ffn_conv_w"], dact, F, f"{l}_ffn_act_bwd")
        g[f"{l}_ffn_conv_w"] = jnp.concatenate([dcw[0], dcw[1]], axis=1)
        du2 = du.reshape(2 * S, F)
        n = wts[f"{l}_ffn_up"].shape[1]
        tn = _pick(n, (1408, 1024, 768, 512, 256, 128))
        per_half = F // tn
        nt = n // tn

        def up_block(i, j, t):
            vb = j * nt + t
            return vb // per_half, vb % per_half

        tm = _pick(S, (1024, 512, 256, 128))

        def nt_map(i, j, t):
            half, cb = up_block(i, j, t)
            return (half * (S // tm) + i, cb)

        def tn_map(j, t):
            half, cb = up_block(0, j, t)
            return (half, cb)

        dh = mm_nt(du2, wts[f"{l}_ffn_up"], nb, S, D, f"{l}_ffn_up_dx", dy_maps=[nt_map], tm=tm, tn=tn)
        dh = on_grad(f"{l}_ffn_up", mm_tn(h, du2, nb, n, f"{l}_ffn_up_dw", dy_maps=[tn_map], tn=tn), dh)
        dxin, dxinb, dg = rms_bwd(xin, wts[f"{l}_ffn_norm_g"], dh, dxout, f"{l}_ffn_rms_bwd")
        g[f"{l}_ffn_norm_g"] = dg
        return dxin, dxinb

    h0 = rms_fwd(x, wts["l0_mix_norm_g"], "l0_mix_rms")
    h0 = start_gather("l0_w_out", h0)
    p0 = mm_nn(h0, wts["l0_w_in"], nb, "l0_w_in_mm")
    p0 = start_gather("l0_ffn_up", p0)
    cat0 = sb_fwd(p0, W, "l0_sb_fwd")
    cat0 = sc_fwd(p0, wts["l0_sc_conv_w"], cat0, W, "l0_sc_fwd")
    cat0 = start_gather("l0_ffn_down", cat0)
    x1 = mm_nn(cat0, wts["l0_w_out"], 1, "l0_w_out_mm", out_dtype=F32, res=x)
    x2, ffn0_saved = ffn_fwd(x1, "l0", ahead=("l1_w_in", "l1_w_out", "l1_ffn_up"))
    x2 = start_gather("l1_ffn_down", x2)

    nh = W // HD
    h2 = rms_fwd(x2, wts["l1_mix_norm_g"], "l1_mix_rms")
    p1 = mm_nn(h2, wts["l1_w_in_main"], 1, "l1_w_in_mm")
    f = mm_nn(h2, wts["l1_w_in_f"], 1, "l1_w_f_mm", out_dtype=F32)
    bf = jnp.zeros((1, 128), F32).at[0, :nh].set(wts["l1_fox_b_f"])
    c = fox_gate_fwd(f, bf, "l1_fox_gate")
    c_heads = c[:, :nh].T
    ccol = c_heads[:, :, None]
    crow = c_heads.reshape(nh, S // _key_strip(S), _key_strip(S))
    sg_bt = wts["l1_sg_b"].T
    cat1 = sg_fwd(p1, wts["l1_sg_w"], sg_bt, wts["l1_sg_norm_g"], W, "l1_sg_fwd")
    cat1, lse = fox_fwd(p1, ccol, crow, cat1, W, "l1_fox_fwd")
    x3 = mm_nn(cat1, wts["l1_w_out"], 1, "l1_w_out_mm", out_dtype=F32, res=x2)
    x4, ffn1_saved = ffn_fwd(x3, "l1")

    dx4, dx4b, dgf, loss = loss_head(x4, wts["final_norm_g"], target, "loss_head")
    g["final_norm_g"] = dgf

    dx3, dx3b = ffn_bwd(dx4, dx4b, ffn1_saved, "l1")
    dcat1 = mm_nt(dx3b, wts["l1_w_out"], 1, S, D, "l1_w_out_dx")
    dcat1 = on_grad("l1_w_out", mm_tn(cat1, dx3b, 1, D, "l1_w_out_dw"), dcat1)
    dp1, dsgw, dsgbt, dsgg = sg_bwd(p1, wts["l1_sg_w"], sg_bt, wts["l1_sg_norm_g"], dcat1, W, "l1_sg_bwd")
    dp1, dcs, dct = fox_bwd(p1, ccol, crow, cat1, lse, dcat1, dp1, W, "l1_fox_bwd")
    g["l1_sg_w"], g["l1_sg_b"], g["l1_sg_norm_g"] = dsgw, dsgbt.T, dsgg
    dc = jnp.zeros((S, 128), F32).at[:, :nh].set((dct[:, :, 0] - dcs.reshape(nh, S)).T)
    df, dbf = fox_gate_bwd(f, bf, dc, "l1_fox_gate_bwd")
    g["l1_fox_b_f"] = dbf[0, :nh]
    dfb = df.astype(BF16)
    tn1 = _pick(W, (1024, 512, 256, 128))
    tm1 = _pick(S, (1024, 512, 256, 128))
    per_part = W // tn1
    part_of = lambda pt: pt + pt // 2 - pt // 4

    def nt_map1(i, j, t):
        return (part_of(t // per_part) * (S // tm1) + i, t % per_part)

    def tn_map1(j, t):
        return (part_of(t // per_part), t % per_part)

    dp1_2d = dp1.reshape(6 * S, W)
    dw_main = mm_tn(h2, dp1_2d, 1, 5 * W, "l1_w_in_dw", dy_maps=[tn_map1], tn=tn1)
    dw_f = mm_tn(h2, dfb, 1, 128, "l1_w_f_dw")
    dh2 = mm_nt(dfb, wts["l1_w_in_f"], 1, S, D, "l1_w_f_dx", out_dtype=F32)
    dh2 = mm_nt(dp1_2d, wts["l1_w_in_main"], 1, S, D, "l1_w_in_dx", res=dh2, dy_maps=[nt_map1], tm=tm1, tn=tn1)
    dh2 = on_grad("l1_w_in", jnp.concatenate([dw_main, dw_f[:, :nh]], axis=1), dh2)
    dx2, dx2b, dg = rms_bwd(x2, wts["l1_mix_norm_g"], dh2, dx3, "l1_mix_rms_bwd")
    g["l1_mix_norm_g"] = dg

    dx1, dx1b = ffn_bwd(dx2, dx2b, ffn0_saved, "l0")
    dcat0 = mm_nt(dx1b, wts["l0_w_out"], 1, S, D, "l0_w_out_dx")
    dcat0 = on_grad("l0_w_out", mm_tn(cat0, dx1b, 1, D, "l0_w_out_dw"), dcat0)
    dp0 = sb_bwd(p0, dcat0, W, "l0_sb_bwd")
    dp0, dscw = sc_bwd(p0, wts["l0_sc_conv_w"], dcat0, dp0, W, "l0_sc_bwd")
    g["l0_sc_conv_w"] = dscw
    n0 = wts["l0_w_in"].shape[1]
    td0 = math.gcd(n0, W)
    nd0 = n0 // td0
    tm0 = _pick(S, (1024, 512, 256, 128))
    per_part0 = W // td0

    def nt_maps0(k):
        def f(i, j, t):
            vb = j * nd0 + k
            return ((vb // per_part0) * (S // tm0) + i, vb % per_part0)
        return f

    def tn_maps0(k):
        def f(j, t):
            vb = j * nd0 + k
            return (vb // per_part0, vb % per_part0)
        return f

    dp0_2d = dp0.reshape(6 * S, W)
    dw0 = mm_tn(h0, dp0_2d, nb, n0, "l0_w_in_dw", dy_maps=[tn_maps0(k) for k in range(nd0)], tn=n0)
    dp0_2d = on_grad("l0_w_in", dw0, dp0_2d)
    dh0 = mm_nt(dp0_2d, wts["l0_w_in"], nb, S, D, "l0_w_in_dx", dy_maps=[nt_maps0(k) for k in range(nd0)], tm=tm0, tn=n0)
    dh0 = on_grad(None, None, dh0)
    dx0, _, dg = rms_bwd(x, wts["l0_mix_norm_g"], dh0, dx1, "l0_mix_rms_bwd")
    g["l0_mix_norm_g"] = dg
    return loss, dx0, g


GATHER_ID, PAIR_ID, CHIPS_ID = 1, 2, 3


def _place():
    return lax.axis_index("x"), lax.axis_index("y"), lax.axis_index("c")


def _other_chips(x, y):
    return [(x, 1 - y), (1 - x, y), (1 - x, 1 - y)]


def _handshake(peers):
    barrier = pltpu.get_barrier_semaphore()
    for peer in peers:
        pl.semaphore_signal(barrier, inc=1, device_id=peer, device_id_type=MESH)
    pl.semaphore_wait(barrier, len(peers))


UPDATE_LAG = 2


def _on_sequencer(body, out_type, scratch_types, collective_id, name):
    return pl.kernel(
        body,
        out_type=out_type,
        mesh=plsc.ScalarSubcoreMesh(axis_name="seq", num_cores=1),
        scratch_types=scratch_types,
        compiler_params=pltpu.CompilerParams(collective_id=collective_id),
        name=name,
    )


def all_gather(arrs, name):
    n = len(arrs)

    def body(*refs):
        xs, outs = refs[:n], refs[n : 2 * n]
        send_sems, recv_sems, local_sems = refs[2 * n :]
        x, y, c = _place()
        me, sibling = (x, y, c), (x, y, 1 - c)
        chips = _other_chips(x, y)
        _handshake([sibling] + [(*chip, c) for chip in chips])

        def copy(a, k, block, to, src=None):
            px, py, pc = block
            dst = outs[a].at[4 * px + 2 * py + pc]
            return pltpu.make_async_remote_copy(
                src_ref=dst if src is None else src, dst_ref=dst,
                send_sem=send_sems.at[7 * a + k], recv_sem=recv_sems.at[7 * a + k], device_id=to, device_id_type=MESH,
            )

        mine = [pltpu.make_async_copy(xs[a], outs[a].at[4 * x + 2 * y + c], local_sems.at[a]) for a in range(n)]
        for cp in mine:
            cp.start()
        first = []
        for a in range(n):
            first.append(copy(a, 0, me, sibling, src=xs[a]))
            first += [copy(a, 1 + j, me, (*chip, c), src=xs[a]) for j, chip in enumerate(chips)]
        for cp in first:
            cp.start()
        passed = []
        for a in range(n):
            for j, chip in enumerate(chips):
                copy(a, 1 + j, (*chip, c), me).wait_recv()
                cp = copy(a, 4 + j, (*chip, c), sibling)
                cp.start()
                passed.append(cp)
        for a in range(n):
            copy(a, 0, sibling, me).wait_recv()
            for j, chip in enumerate(chips):
                copy(a, 4 + j, (*chip, 1 - c), me).wait_recv()
        for cp in first + passed:
            cp.wait_send()
        for cp in mine:
            cp.wait()

    out_type = [jax.ShapeDtypeStruct((NDEV,) + a.shape, a.dtype) for a in arrs]
    sems = [pltpu.SemaphoreType.DMA((7 * n,)), pltpu.SemaphoreType.DMA((7 * n,)), pltpu.SemaphoreType.DMA((n,))]
    return _on_sequencer(body, out_type, sems, GATHER_ID, name)(*arrs)


_IN_HBM = pl.BlockSpec(memory_space=pltpu.HBM)
_IN_SEM = pl.BlockSpec(memory_space=pltpu.SEMAPHORE)
_EFFECT = pltpu.SideEffectType.DATAFLOW_SIDE_EFFECTING


def _split_start(make_copies, src, land_shape, nsem, name):
    def body(src_ref, land_ref, send_sems, recv_sems, land_thru, token):
        for cp in make_copies(src_ref, land_ref, send_sems, recv_sems):
            cp.start()
        token[...] = jnp.zeros_like(token)

    send_sems, recv_sems, land_thru, token = pl.pallas_call(
        body,
        name=name,
        out_shape=(
            pltpu.SemaphoreType.DMA((nsem,)), pltpu.SemaphoreType.DMA((nsem,)),
            pltpu.HBM(land_shape, src.dtype), jax.ShapeDtypeStruct((8, 128), F32),
        ),
        in_specs=(_IN_HBM, _IN_HBM),
        out_specs=(_IN_SEM, _IN_SEM, _IN_HBM, pl.BlockSpec(memory_space=pltpu.VMEM)),
        input_output_aliases={1: 2},
        compiler_params=pltpu.CompilerParams(has_side_effects=_EFFECT),
    )(src, pltpu.with_memory_space_constraint(lax.empty(land_shape, src.dtype), pltpu.HBM))
    return send_sems, recv_sems, src, land_thru, token


def _split_wait(make_copies, send_sems, recv_sems, src_thru, land_thru, after, name):
    def body(src_ref, land_ref, send_sems, recv_sems, after_ref, land_out):
        for cp in make_copies(src_ref, land_ref, send_sems, recv_sems):
            cp.wait_send()
            cp.wait_recv()

    return pl.pallas_call(
        body,
        name=name,
        out_shape=pltpu.HBM(land_thru.shape, land_thru.dtype),
        in_specs=(_IN_HBM, _IN_HBM, _IN_SEM, _IN_SEM, pl.BlockSpec(memory_space=pl.ANY)),
        out_specs=_IN_HBM,
        input_output_aliases={1: 0},
        compiler_params=pltpu.CompilerParams(has_side_effects=_EFFECT),
    )(src_thru, land_thru, send_sems, recv_sems, after)


def _pair_copies(src_ref, land_ref, send_sems, recv_sems):
    x, y, c = _place()
    return [
        pltpu.make_async_remote_copy(
            src_ref=src_ref.at[k, 1 - c], dst_ref=land_ref.at[k],
            send_sem=send_sems.at[k], recv_sem=recv_sems.at[k], device_id=(x, y, 1 - c), device_id_type=MESH,
        )
        for k in range(4)
    ]


def _chip_copies(src_ref, land_ref, send_sems, recv_sems):
    x, y, c = _place()
    return [
        pltpu.make_async_remote_copy(
            src_ref=src_ref.at[2 * px + py], dst_ref=land_ref.at[2 * x + y],
            send_sem=send_sems.at[j], recv_sem=recv_sems.at[j], device_id=(px, py, c), device_id_type=MESH,
        )
        for j, (px, py) in enumerate(_other_chips(x, y))
    ]


def _row_tile(R, C, max_elems):
    if R * C <= max_elems:
        return R
    best = None
    for tr in range(16, R, 16):
        if R % tr == 0 and tr * C <= max_elems:
            best = tr
    return best or R


def pair_sum(a42, land4, core, name):
    _, _, R, C = a42.shape
    tr = _row_tile(R, C, 1 << 20)

    def body(core_ref, a_ref, l_ref, o_ref):
        o_ref[...] = (a_ref[0].astype(F32) + l_ref[...].astype(F32)).astype(o_ref.dtype)

    return pl.pallas_call(
        body,
        grid_spec=pltpu.PrefetchScalarGridSpec(
            num_scalar_prefetch=1,
            grid=(4, R // tr),
            in_specs=[
                pl.BlockSpec((1, 1, tr, C), lambda k, r, core_ref: (k, core_ref[0], r, 0)),
                pl.BlockSpec((1, tr, C), lambda k, r, core_ref: (k, r, 0)),
            ],
            out_specs=pl.BlockSpec((1, tr, C), lambda k, r, core_ref: (k, r, 0)),
        ),
        out_shape=jax.ShapeDtypeStruct((4, R, C), BF16),
        compiler_params=_cp(("parallel", "parallel")),
        name=name,
    )(core, a42, land4)


def sum_slots(parts, name):
    P, R, C = parts.shape

    def body(p_ref, o_ref):
        acc = p_ref[0].astype(F32)
        for k in range(1, P):
            acc = acc + p_ref[k].astype(F32)
        o_ref[...] = acc

    tr = _row_tile(R, P * C, 1 << 21)
    return pl.pallas_call(
        body,
        grid=(R // tr,),
        in_specs=[pl.BlockSpec((P, tr, C), lambda r: (0, r, 0))],
        out_specs=pl.BlockSpec((tr, C), lambda r: (r, 0)),
        out_shape=jax.ShapeDtypeStruct((R, C), F32),
        compiler_params=_cp(("parallel",)),
        name=name,
    )(parts)


def adamw(w, m, v, parts, name):
    R, C = w.shape
    P = parts.shape[0]
    tr = _pick(R, (256, 128, 64, 32, 16, 8))
    c1 = 1.0 - ADAM_B1 ** ADAM_STEP
    c2 = 1.0 - ADAM_B2 ** ADAM_STEP

    def body(w_ref, m_ref, v_ref, p_ref, g_ref, d_ref, nm_ref, nv_ref):
        g = p_ref[0].astype(F32)
        for k in range(1, P):
            g = g + p_ref[k].astype(F32)
        nm = ADAM_B1 * m_ref[...] + (1.0 - ADAM_B1) * g
        nv = ADAM_B2 * v_ref[...] + (1.0 - ADAM_B2) * (g * g)
        g_ref[...] = g
        nm_ref[...] = nm
        nv_ref[...] = nv
        d_ref[...] = -ADAM_LR * ((nm / c1) / (jnp.sqrt(nv / c2) + ADAM_EPS) + ADAM_WD * w_ref[...])

    blk = pl.BlockSpec((tr, C), lambda r: (r, 0))
    shp = jax.ShapeDtypeStruct((R, C), F32)
    return pl.pallas_call(
        body,
        grid=(R // tr,),
        in_specs=[blk, blk, blk, pl.BlockSpec((P, tr, C), lambda r: (0, r, 0))],
        out_specs=[blk, blk, blk, blk],
        out_shape=[shp, shp, shp, shp],
        compiler_params=_cp(("parallel",)),
        name=name,
    )(w, m, v, parts)


def adamw_reduced(w, m, v, own, land, chip, name):
    R, C = w.shape
    tr = _pick(R, (256, 128, 64, 32, 16, 8))
    c1 = 1.0 - ADAM_B1 ** ADAM_STEP
    c2 = 1.0 - ADAM_B2 ** ADAM_STEP

    def body(chip_ref, w_ref, m_ref, v_ref, own_ref, land_ref, g_ref, d_ref, nm_ref, nv_ref):
        mine = own_ref[0].astype(F32)
        g = None
        for k in range(4):
            term = jnp.where(chip_ref[0] == k, mine, land_ref[k].astype(F32))
            g = term if g is None else g + term
        nm = ADAM_B1 * m_ref[...] + (1.0 - ADAM_B1) * g
        nv = ADAM_B2 * v_ref[...] + (1.0 - ADAM_B2) * (g * g)
        g_ref[...] = g
        nm_ref[...] = nm
        nv_ref[...] = nv
        d_ref[...] = -ADAM_LR * ((nm / c1) / (jnp.sqrt(nv / c2) + ADAM_EPS) + ADAM_WD * w_ref[...])

    blk = pl.BlockSpec((tr, C), lambda r, chip_ref: (r, 0))
    shp = jax.ShapeDtypeStruct((R, C), F32)
    return pl.pallas_call(
        body,
        grid_spec=pltpu.PrefetchScalarGridSpec(
            num_scalar_prefetch=1,
            grid=(R // tr,),
            in_specs=[
                blk, blk, blk,
                pl.BlockSpec((1, tr, C), lambda r, chip_ref: (chip_ref[0], r, 0)),
                pl.BlockSpec((4, tr, C), lambda r, chip_ref: (0, r, 0)),
            ],
            out_specs=[blk, blk, blk, blk],
        ),
        out_shape=[shp, shp, shp, shp],
        compiler_params=_cp(("parallel",)),
        name=name,
    )(chip, w, m, v, own, land)


_WEIGHTS = [
    "l0_mix_norm_g", "l0_w_in", "l0_sc_conv_w", "l0_w_out", "l0_ffn_norm_g", "l0_ffn_up", "l0_ffn_conv_w", "l0_ffn_down",
    "l1_mix_norm_g", "l1_w_in", "l1_fox_b_f", "l1_sg_w", "l1_sg_b", "l1_sg_norm_g", "l1_w_out", "l1_ffn_norm_g",
    "l1_ffn_up", "l1_ffn_conv_w", "l1_ffn_down", "final_norm_g",
]
_COL_SHARDED = ["l0_w_in", "l0_ffn_up", "l1_w_in", "l1_ffn_up"]
_ROW_SHARDED = ["l0_w_out", "l0_ffn_down", "l1_w_out", "l1_ffn_down"]
_BIG = ["l0_w_in", "l0_w_out", "l0_ffn_up", "l0_ffn_down", "l1_w_in", "l1_w_out", "l1_ffn_up", "l1_ffn_down"]
_CONV = ["l0_sc_conv_w", "l0_ffn_conv_w", "l1_ffn_conv_w"]
_SMALL = [n for n in _WEIGHTS if n not in _BIG]
_PACK_ROWS = 8


def _pack(arrs):
    flat = []
    for a in arrs:
        v = a.reshape(-1).astype(F32)
        pad = (-v.shape[0]) % (_PACK_ROWS * 128)
        flat.append(jnp.pad(v, (0, pad)))
    return jnp.concatenate(flat).reshape(-1, 128)


def _unpack(packed, shapes):
    out, off = [], 0
    flat = packed.reshape(-1)
    for shp in shapes:
        size = math.prod(shp)
        out.append(flat[off : off + size].reshape(shp))
        off += size + (-size) % (_PACK_ROWS * 128)
    return out


def kernel(x, l0_mix_norm_g, l0_w_in, l0_sc_conv_w, l0_w_out, l0_ffn_norm_g, l0_ffn_up, l0_ffn_conv_w, l0_ffn_down, l1_mix_norm_g, l1_w_in, l1_fox_b_f, l1_sg_w, l1_sg_b, l1_sg_norm_g, l1_w_out, l1_ffn_norm_g, l1_ffn_up, l1_ffn_conv_w, l1_ffn_down, final_norm_g, loss_target, m_l0_mix_norm_g, m_l0_w_in, m_l0_sc_conv_w, m_l0_w_out, m_l0_ffn_norm_g, m_l0_ffn_up, m_l0_ffn_conv_w, m_l0_ffn_down, m_l1_mix_norm_g, m_l1_w_in, m_l1_fox_b_f, m_l1_sg_w, m_l1_sg_b, m_l1_sg_norm_g, m_l1_w_out, m_l1_ffn_norm_g, m_l1_ffn_up, m_l1_ffn_conv_w, m_l1_ffn_down, m_final_norm_g, v_l0_mix_norm_g, v_l0_w_in, v_l0_sc_conv_w, v_l0_w_out, v_l0_ffn_norm_g, v_l0_ffn_up, v_l0_ffn_conv_w, v_l0_ffn_down, v_l1_mix_norm_g, v_l1_w_in, v_l1_fox_b_f, v_l1_sg_w, v_l1_sg_b, v_l1_sg_norm_g, v_l1_w_out, v_l1_ffn_norm_g, v_l1_ffn_up, v_l1_ffn_conv_w, v_l1_ffn_down, v_final_norm_g):
    given = dict(locals())
    w = {n: given[n] for n in _WEIGHTS}
    mom = {n: given["m_" + n] for n in _WEIGHTS}
    var = {n: given["v_" + n] for n in _WEIGHTS}
    xs, target = x[0], loss_target[0]
    S, D = xs.shape
    W = D // 2
    nh = W // HD
    cx, cy, cc = _place()
    me = 4 * cx + 2 * cy + cc

    wts = {"nb": NDEV, "F": l0_ffn_down.shape[0] * NDEV}
    for n in _SMALL:
        if n not in _CONV:
            wts[n] = w[n]

    def start_gather(n, after=None):
        src = [w[n].astype(BF16)] + ([w[c] for c in _CONV] if n == _BIG[0] else [])
        if after is not None:
            src, after = lax.optimization_barrier((src, after))
        got = all_gather(src, f"gather_{n}")
        if n == "l1_w_in":
            w_in1 = got[0].transpose(1, 0, 2).reshape(D, -1)
            wts["l1_w_in_main"] = w_in1[:, : 5 * W]
            wts["l1_w_in_f"] = jnp.pad(w_in1[:, 5 * W :], ((0, 0), (0, 128 - nh)))
        elif n in _ROW_SHARDED:
            wts[n] = got[0].reshape(-1, D)
        else:
            wts[n] = got[0].reshape(NDEV * D, -1)
        for c, taps in zip(_CONV, got[1:]):
            wts[c] = taps.transpose(1, 0, 2).reshape(CONV_K, -1)
        return after

    core = jnp.reshape(cc, (1,)).astype(jnp.int32)
    chip = jnp.reshape(2 * cx + cy, (1,)).astype(jnp.int32)
    pair_flying, chip_flying = [], []
    out_g, out_d, out_m, out_v = {}, {}, {}, {}

    def tie(value, after):
        if after is None:
            return value, None
        return lax.optimization_barrier((value, after))

    def to_chips(after):
        n, flying = pair_flying.pop()
        landed = _split_wait(_pair_copies, *flying, f"reduce_pair_wait_{n}")
        summed = pair_sum(flying[2], landed, core, f"pair_sum_{n}")
        *flying, token = _split_start(_chip_copies, summed, summed.shape, 3, f"reduce_chips_{n}")
        token, after = tie(token, after)
        chip_flying.append((n, flying + [token]))
        return after

    def update(after, behind=None):
        n, flying = chip_flying.pop(0)
        if behind is not None:
            flying[4], _ = lax.optimization_barrier((flying[4], behind))
        landed = _split_wait(_chip_copies, *flying, f"reduce_chips_wait_{n}")
        res = adamw_reduced(w[n], mom[n], var[n], flying[2], landed, chip, f"adamw_{n}")
        res, after = tie(res, after)
        out_g[n], out_d[n], out_m[n], out_v[n] = res
        return after, res[0]

    def on_grad(n, term, after):
        if n is None:
            return to_chips(after)
        if n == "l1_w_in":
            term = term.reshape(D, NDEV, -1).transpose(1, 0, 2)
        elif n in _ROW_SHARDED:
            term = term.reshape(NDEV, -1, D)
        else:
            term = term.reshape(NDEV, D, -1)
        term = term.reshape((4, 2) + term.shape[1:])
        *flying, token = _split_start(_pair_copies, term, term.shape[:1] + term.shape[2:], 4, f"reduce_pair_{n}")
        token, after = tie(token, after)
        if len(chip_flying) == UPDATE_LAG:
            after, _ = update(after)
        if pair_flying:
            after = to_chips(after)
        pair_flying.append((n, flying + [token]))
        return after

    for n in _BIG:
        start_gather(n)
    loss_tile, dx, g = local_step(xs, target, wts, lambda n, after: after, on_grad)
    done = None
    while chip_flying:
        _, done = update(None, behind=done)
    loss = lax.psum(loss_tile[0, 0], ("x", "y", "c"))

    small_terms = [g[n] for n in _SMALL]
    small_shapes = [tuple(t.shape) for t in small_terms]
    packed = _pack(small_terms)
    all_terms = all_gather([packed], "gather_small_grads")[0]
    small_sum = _unpack(sum_slots(all_terms, "sum_small_grads"), small_shapes)
    small_g = {}
    for n, t in zip(_SMALL, small_sum):
        if n in _CONV:
            cols = w[n].shape[1]
            t = lax.dynamic_slice_in_dim(t, me * cols, cols, axis=1)
        small_g[n] = t.reshape(w[n].shape)
    shapes = [w[n].shape for n in _SMALL]
    res = adamw(
        _pack([w[n] for n in _SMALL]), _pack([mom[n] for n in _SMALL]), _pack([var[n] for n in _SMALL]),
        _pack([small_g[n] for n in _SMALL])[None], "adamw_small",
    )
    for dst, packed_out in zip((out_g, out_d, out_m, out_v), res):
        for n, t in zip(_SMALL, _unpack(packed_out, shapes)):
            dst[n] = t

    return (loss, dx[None], *[out_g[n] for n in _WEIGHTS], *[out_d[n] for n in _WEIGHTS],
            *[out_m[n] for n in _WEIGHTS], *[out_v[n] for n in _WEIGHTS])
```

```python
import functools
import math

import jax
import jax.numpy as jnp
from jax import lax
from jax.experimental import pallas as pl
from jax.experimental.pallas import tpu as pltpu
from jax.experimental.pallas import tpu_sc as plsc

F32 = jnp.float32
BF16 = jnp.bfloat16
HD = 128
EPS = 1e-6
CONV_K = 3
VMEM_LIMIT_BYTES = 48 << 20
NDEV = 8
MESH = pl.DeviceIdType.MESH

ADAM_LR = 0.001
ADAM_B1 = 0.9
ADAM_B2 = 0.999
ADAM_EPS = 1e-08
ADAM_WD = 0.01
ADAM_STEP = 10


def _cp(sem):
    return pltpu.CompilerParams(dimension_semantics=sem, vmem_limit_bytes=VMEM_LIMIT_BYTES)


def _pick(n, prefs):
    for p in prefs:
        if n % p == 0:
            return p
    return n


def _dot(a, b):
    return jnp.dot(a, b, preferred_element_type=F32)


def _dot_nt(a, b):
    return lax.dot_general(a, b, (((1,), (1,)), ((), ())), preferred_element_type=F32)


def _dot_tn(a, b):
    return lax.dot_general(a, b, (((0,), (0,)), ((), ())), preferred_element_type=F32)


def _split3(x):
    hi = x.astype(BF16)
    r = x - hi.astype(F32)
    mid = r.astype(BF16)
    lo = (r - mid.astype(F32)).astype(BF16)
    return hi, mid, lo


def _dot_ones_right(x, ones_bf16):
    hi, mid, lo = _split3(x)
    return _dot(hi, ones_bf16) + _dot(mid, ones_bf16) + _dot(lo, ones_bf16)


def _dot_ones_left(ones_bf16, x):
    hi, mid, lo = _split3(x)
    return _dot(ones_bf16, hi) + _dot(ones_bf16, mid) + _dot(ones_bf16, lo)


def _iota2(shape, axis):
    return lax.broadcasted_iota(jnp.int32, shape, axis)


def mm_nn(a, w2d, nb, name, out_dtype=BF16, res=None, tm=None, tn=None, tk=None):
    M, K = a.shape
    n = w2d.shape[1]
    assert w2d.shape[0] == nb * K
    tm = tm or _pick(M, (1024, 512, 256, 128))
    tn = tn or _pick(n, (1408, 1024, 768, 512, 256, 128))
    tk = tk or (K if K <= 2048 else _pick(K, (1408, 1024, 512, 256, 128)))
    nk, nt = K // tk, n // tn
    has_res = res is not None

    def body(*refs):
        if has_res:
            a_ref, w_ref, r_ref, o_ref = refs[:4]
        else:
            a_ref, w_ref, o_ref = refs[:3]
            r_ref = None
        part = _dot(a_ref[...], w_ref[...])

        def finish(acc):
            if r_ref is not None:
                acc = acc + r_ref[...].astype(F32)
            o_ref[...] = acc.astype(o_ref.dtype)

        if nk == 1:
            finish(part)
        else:
            acc_ref = refs[-1]
            k = pl.program_id(3)

            @pl.when(k == 0)
            def _():
                acc_ref[...] = part

            @pl.when(k > 0)
            def _():
                acc_ref[...] += part

            @pl.when(k == nk - 1)
            def _():
                finish(acc_ref[...])

    in_specs = [
        pl.BlockSpec((tm, tk), lambda i, j, t, k: (i, k)),
        pl.BlockSpec((tk, tn), lambda i, j, t, k: (j * nk + k, t)),
    ]
    args = [a, w2d]
    out_spec = pl.BlockSpec((tm, tn), lambda i, j, t, k: (i, j * nt + t))
    if has_res:
        in_specs.append(out_spec)
        args.append(res)
    return pl.pallas_call(
        body,
        grid=(M // tm, nb, nt, nk),
        in_specs=in_specs,
        out_specs=out_spec,
        out_shape=jax.ShapeDtypeStruct((M, nb * n), out_dtype),
        scratch_shapes=[pltpu.VMEM((tm, tn), F32)] if nk > 1 else [],
        compiler_params=_cp(("parallel", "parallel", "parallel", "arbitrary")),
        name=name,
    )(*args)


def mm_nt(dy2d, w2d, nb, M, K, name, out_dtype=BF16, res=None, dy_maps=None, tm=None, tko=None, tn=None):
    n = w2d.shape[1]
    assert w2d.shape[0] == nb * K
    tm = tm or _pick(M, (1024, 512, 256, 128))
    tko = tko or _pick(K, (1024, 512, 256, 128))
    tn = tn or _pick(n, (1408, 1024, 768, 512, 256, 128))
    nt, nko = n // tn, K // tko
    has_res = res is not None
    if dy_maps is None:
        dy_maps = [lambda i, j, t: (i, j * nt + t)]
    nd = len(dy_maps)
    td = tn // nd

    def body(*refs):
        d_refs, w_ref = refs[:nd], refs[nd]
        r_ref = refs[nd + 1] if has_res else None
        o_ref, acc_ref = refs[-2], refs[-1]
        j, t = pl.program_id(2), pl.program_id(3)
        d = d_refs[0][...] if nd == 1 else jnp.concatenate([r[...] for r in d_refs], axis=1)
        part = _dot_nt(d, w_ref[...])
        first = jnp.logical_and(j == 0, t == 0)
        last = jnp.logical_and(j == nb - 1, t == nt - 1)

        @pl.when(first)
        def _():
            acc_ref[...] = part

        @pl.when(jnp.logical_not(first))
        def _():
            acc_ref[...] += part

        @pl.when(last)
        def _():
            acc = acc_ref[...]
            if r_ref is not None:
                acc = acc + r_ref[...].astype(F32)
            o_ref[...] = acc.astype(o_ref.dtype)

    in_specs = [pl.BlockSpec((tm, td), functools.partial(lambda f, i, ko, j, t: f(i, j, t), f)) for f in dy_maps]
    in_specs.append(pl.BlockSpec((tko, tn), lambda i, ko, j, t: (j * nko + ko, t)))
    args = [dy2d] * nd + [w2d]
    out_spec = pl.BlockSpec((tm, tko), lambda i, ko, j, t: (i, ko))
    if has_res:
        in_specs.append(out_spec)
        args.append(res)
    return pl.pallas_call(
        body,
        grid=(M // tm, nko, nb, nt),
        in_specs=in_specs,
        out_specs=out_spec,
        out_shape=jax.ShapeDtypeStruct((M, K), out_dtype),
        scratch_shapes=[pltpu.VMEM((tm, tko), F32)],
        compiler_params=_cp(("parallel", "parallel", "arbitrary", "arbitrary")),
        name=name,
    )(*args)


def mm_tn(x, dy2d, nb, n, name, out_dtype=BF16, dy_maps=None, tko=None, tn=None):
    S, K = x.shape
    tko = tko or _pick(K, (512, 256, 128))
    tn = tn or _pick(n, (1408, 1024, 768, 512, 256, 128))
    nt, nko = n // tn, K // tko
    if dy_maps is None:
        dy_maps = [lambda j, t: (0, j * nt + t)]
    nd = len(dy_maps)
    td = tn // nd

    def body(*refs):
        x_ref, d_refs, o_ref = refs[0], refs[1 : 1 + nd], refs[-1]
        d = d_refs[0][...] if nd == 1 else jnp.concatenate([r[...] for r in d_refs], axis=1)
        o_ref[...] = _dot_tn(x_ref[...], d).astype(o_ref.dtype)

    in_specs = [pl.BlockSpec((S, tko), lambda ko, j, t: (0, ko))]
    in_specs += [pl.BlockSpec((S, td), functools.partial(lambda f, ko, j, t: f(j, t), f)) for f in dy_maps]
    return pl.pallas_call(
        body,
        grid=(nko, nb, nt),
        in_specs=in_specs,
        out_specs=pl.BlockSpec((tko, tn), lambda ko, j, t: (j * nko + ko, t)),
        out_shape=jax.ShapeDtypeStruct((nb * K, n), out_dtype),
        compiler_params=_cp(("parallel", "parallel", "parallel")),
        name=name,
    )(x, *([dy2d] * nd))


def rms_fwd(x, g, name):
    S, D = x.shape
    tm = _pick(S, (256, 128))

    def body(x_ref, g_ref, o_ref):
        xf = x_ref[...]
        r = lax.rsqrt(jnp.mean(xf * xf, axis=-1, keepdims=True) + EPS)
        o_ref[...] = (xf * r * g_ref[...]).astype(o_ref.dtype)

    return pl.pallas_call(
        body,
        grid=(S // tm,),
        in_specs=[pl.BlockSpec((tm, D), lambda i: (i, 0)), pl.BlockSpec((1, D), lambda i: (0, 0))],
        out_specs=pl.BlockSpec((tm, D), lambda i: (i, 0)),
        out_shape=jax.ShapeDtypeStruct((S, D), BF16),
        compiler_params=_cp(("parallel",)),
        name=name,
    )(x, g.reshape(1, D))


def rms_bwd(x, g, dh, dres, name):
    S, D = x.shape
    tm = _pick(S, (256, 128))

    def body(x_ref, g_ref, dh_ref, dr_ref, dx_ref, dxb_ref, dg_ref):
        i = pl.program_id(0)
        xf = x_ref[...]
        dh = dh_ref[...].astype(F32)
        r = lax.rsqrt(jnp.mean(xf * xf, axis=-1, keepdims=True) + EPS)
        gy = dh * g_ref[...]
        proj = jnp.mean(gy * xf, axis=-1, keepdims=True)
        dx = dr_ref[...] + r * gy - xf * (r * r * r * proj)
        dx_ref[...] = dx
        dxb_ref[...] = dx.astype(BF16)
        dg = jnp.sum(dh * (xf * r), axis=0, keepdims=True)

        @pl.when(i == 0)
        def _():
            dg_ref[...] = dg

        @pl.when(i > 0)
        def _():
            dg_ref[...] += dg

    row = pl.BlockSpec((tm, D), lambda i: (i, 0))
    vec = pl.BlockSpec((1, D), lambda i: (0, 0))
    return pl.pallas_call(
        body,
        grid=(S // tm,),
        in_specs=[row, vec, row, row],
        out_specs=[row, row, vec],
        out_shape=[jax.ShapeDtypeStruct((S, D), F32), jax.ShapeDtypeStruct((S, D), BF16), jax.ShapeDtypeStruct((1, D), F32)],
        compiler_params=_cp(("arbitrary",)),
        name=name,
    )(x, g.reshape(1, D), dh, dres)


def loss_head(x, g, target, name):
    S, D = x.shape
    tm = _pick(S, (256, 128))

    def body(x_ref, g_ref, t_ref, dx_ref, dxb_ref, dg_ref, loss_ref):
        i = pl.program_id(0)
        xf = x_ref[...]
        gg = g_ref[...]
        r = lax.rsqrt(jnp.mean(xf * xf, axis=-1, keepdims=True) + EPS)
        xh = xf * r
        err = xh * gg - t_ref[...]
        part = (0.5 / D) * jnp.sum(err * err)
        dy = err * (1.0 / D)
        gy = dy * gg
        proj = jnp.mean(gy * xf, axis=-1, keepdims=True)
        dx = r * gy - xf * (r * r * r * proj)
        dx_ref[...] = dx
        dxb_ref[...] = dx.astype(BF16)
        dg = jnp.sum(dy * xh, axis=0, keepdims=True)
        lossb = jnp.full(loss_ref.shape, part, F32)

        @pl.when(i == 0)
        def _():
            dg_ref[...] = dg
            loss_ref[...] = lossb

        @pl.when(i > 0)
        def _():
            dg_ref[...] += dg
            loss_ref[...] += lossb

    row = pl.BlockSpec((tm, D), lambda i: (i, 0))
    vec = pl.BlockSpec((1, D), lambda i: (0, 0))
    return pl.pallas_call(
        body,
        grid=(S // tm,),
        in_specs=[row, vec, row],
        out_specs=[row, row, vec, pl.BlockSpec((8, 128), lambda i: (0, 0))],
        out_shape=[
            jax.ShapeDtypeStruct((S, D), F32),
            jax.ShapeDtypeStruct((S, D), BF16),
            jax.ShapeDtypeStruct((1, D), F32),
            jax.ShapeDtypeStruct((8, 128), F32),
        ],
        compiler_params=_cp(("arbitrary",)),
        name=name,
    )(x, g.reshape(1, D), target)


def _shift_down(s, k):
    if k == 0:
        return s
    return jnp.where(_iota2(s.shape, 0) >= k, pltpu.roll(s, k, axis=0), 0.0)


def _shift_up(s, k):
    if k == 0:
        return s
    n = s.shape[0]
    return jnp.where(_iota2(s.shape, 0) < n - k, pltpu.roll(s, n - k, axis=0), 0.0)


def _conv(s, w):
    return w[0:1] * _shift_down(s, 2) + w[1:2] * _shift_down(s, 1) + w[2:3] * s


def _conv_t(d, w):
    return w[2:3] * d + w[1:2] * _shift_up(d, 1) + w[0:1] * _shift_up(d, 2)


def _conv_dw(d, s):
    return [jnp.sum(d * _shift_down(s, CONV_K - 1 - k), axis=0, keepdims=True) for k in range(CONV_K)]


def sc_fwd(p, convw, cat, W, name):
    S = p.shape[0]
    tc = _pick(W, (256, 128))
    nc = W // tc

    def body(gb_ref, gc_ref, hi_ref, w_ref, cat_ref, o_ref):
        s = gc_ref[...].astype(F32) * hi_ref[...].astype(F32)
        o_ref[...] = (gb_ref[...].astype(F32) * _conv(s, w_ref[...])).astype(o_ref.dtype)

    col = lambda part: pl.BlockSpec((S, tc), lambda c: (0, part * nc + c))
    return pl.pallas_call(
        body,
        grid=(nc,),
        in_specs=[col(3), col(4), col(5), pl.BlockSpec((CONV_K, tc), lambda c: (0, c)), pl.BlockSpec(memory_space=pl.ANY)],
        out_specs=col(1),
        out_shape=jax.ShapeDtypeStruct(cat.shape, cat.dtype),
        input_output_aliases={4: 0},
        compiler_params=_cp(("parallel",)),
        name=name,
    )(p, p, p, convw, cat)


def sc_bwd(p, convw, dcat, dp, W, name):
    S = p.shape[0]
    tc = _pick(W, (256, 128))
    nc = W // tc

    def body(gb_ref, gc_ref, hi_ref, w_ref, do_ref, dp_in_ref, dp_ref, dw_ref):
        gb = gb_ref[...].astype(F32)
        gc = gc_ref[...].astype(F32)
        hi = hi_ref[...].astype(F32)
        w = w_ref[...]
        do = do_ref[...].astype(F32)
        s = gc * hi
        dcs = do * gb
        ds = _conv_t(dcs, w)
        dp_ref[0] = (do * _conv(s, w)).astype(dp_ref.dtype)
        dp_ref[1] = (ds * hi).astype(dp_ref.dtype)
        dp_ref[2] = (ds * gc).astype(dp_ref.dtype)
        for k, row in enumerate(_conv_dw(dcs, s)):
            dw_ref[k : k + 1, :] = row

    col = lambda part: pl.BlockSpec((S, tc), lambda c: (0, part * nc + c))
    return pl.pallas_call(
        body,
        grid=(nc,),
        in_specs=[
            col(3), col(4), col(5),
            pl.BlockSpec((CONV_K, tc), lambda c: (0, c)),
            pl.BlockSpec((S, tc), lambda c: (0, nc + c)),
            pl.BlockSpec(memory_space=pl.ANY),
        ],
        out_specs=[pl.BlockSpec((3, S, tc), lambda c: (1, 0, c)), pl.BlockSpec((CONV_K, tc), lambda c: (0, c))],
        out_shape=[jax.ShapeDtypeStruct(dp.shape, dp.dtype), jax.ShapeDtypeStruct((CONV_K, W), F32)],
        input_output_aliases={5: 0},
        compiler_params=_cp(("parallel",)),
        name=name,
    )(p, p, p, convw, dcat, dp)


def _silu_parts(a):
    sig = 1.0 / (1.0 + jnp.exp(-a))
    return a * sig, sig


def ffn_act_fwd(u, convw, F, name):
    S = u.shape[0]
    tc = _pick(F, (256, 128))
    nc = F // tc

    def body(ug_ref, uu_ref, wg_ref, wu_ref, o_ref):
        ag = _conv(ug_ref[...].astype(F32), wg_ref[...])
        au = _conv(uu_ref[...].astype(F32), wu_ref[...])
        o_ref[...] = (_silu_parts(ag)[0] * au).astype(o_ref.dtype)

    col = lambda half: pl.BlockSpec((S, tc), lambda c: (0, half * nc + c))
    wcol = lambda half: pl.BlockSpec((CONV_K, tc), lambda c: (0, half * nc + c))
    return pl.pallas_call(
        body,
        grid=(nc,),
        in_specs=[col(0), col(1), wcol(0), wcol(1)],
        out_specs=pl.BlockSpec((S, tc), lambda c: (0, c)),
        out_shape=jax.ShapeDtypeStruct((S, F), BF16),
        compiler_params=_cp(("parallel",)),
        name=name,
    )(u, u, convw, convw)


def ffn_act_bwd(u, convw, dact, F, name):
    S = u.shape[0]
    tc = _pick(F, (256, 128))
    nc = F // tc

    def body(ug_ref, uu_ref, wg_ref, wu_ref, da_ref, du_ref, dw_ref):
        ug = ug_ref[...].astype(F32)
        uu = uu_ref[...].astype(F32)
        wg = wg_ref[...]
        wu = wu_ref[...]
        da = da_ref[...].astype(F32)
        ag = _conv(ug, wg)
        au = _conv(uu, wu)
        sl, sig = _silu_parts(ag)
        dag = da * au * (sig * (1.0 + ag * (1.0 - sig)))
        dau = da * sl
        du_ref[0] = _conv_t(dag, wg).astype(du_ref.dtype)
        du_ref[1] = _conv_t(dau, wu).astype(du_ref.dtype)
        for k, (rg, ru) in enumerate(zip(_conv_dw(dag, ug), _conv_dw(dau, uu))):
            dw_ref[0, k : k + 1, :] = rg
            dw_ref[1, k : k + 1, :] = ru

    col = lambda half: pl.BlockSpec((S, tc), lambda c: (0, half * nc + c))
    wcol = lambda half: pl.BlockSpec((CONV_K, tc), lambda c: (0, half * nc + c))
    return pl.pallas_call(
        body,
        grid=(nc,),
        in_specs=[col(0), col(1), wcol(0), wcol(1), pl.BlockSpec((S, tc), lambda c: (0, c))],
        out_specs=[pl.BlockSpec((2, S, tc), lambda c: (0, 0, c)), pl.BlockSpec((2, CONV_K, tc), lambda c: (0, 0, c))],
        out_shape=[jax.ShapeDtypeStruct((2, S, F), BF16), jax.ShapeDtypeStruct((2, CONV_K, F), F32)],
        compiler_params=_cp(("parallel",)),
        name=name,
    )(u, u, convw, convw, dact)


def _softplus(z):
    return jnp.maximum(z, 0.0) + jnp.log(1.0 + jnp.exp(-jnp.abs(z)))


def _key_strip(S):
    return _pick(S, (512, 256, 128))


def _query_rows(S):
    return _pick(S, (256, 128))


def _split2(x):
    hi = x.astype(BF16)
    return hi, (x - hi.astype(F32)).astype(BF16)


def _block_sums(x, ones_bf16):
    hi, lo = _split2(x)
    return [
        _dot(hi[:, b * HD : (b + 1) * HD], ones_bf16) + _dot(lo[:, b * HD : (b + 1) * HD], ones_bf16)
        for b in range(x.shape[1] // HD)
    ]


def _strip_mask(shape, row0, off, strict):
    cols, rows = _iota2(shape, 1) + off, _iota2(shape, 0) + row0
    return cols < rows if strict else cols <= rows


def _sb_strip(q, ks, row0, off, run, su):
    z = _dot_nt(q, ks) * (HD ** -0.5)
    mask = _strip_mask(z.shape, row0, off, True)
    sp = _softplus(z)
    l = jnp.where(mask, -sp, 0.0)
    within = _block_sums(l, su)
    later = [None] * len(within)
    for b in reversed(range(len(within))):
        later[b] = within[b] + run
        run = run + jnp.sum(l[:, b * HD : (b + 1) * HD], axis=1, keepdims=True)
    a = jnp.where(mask, jnp.exp(z - sp + jnp.concatenate(later, axis=1)), 0.0)
    return z, mask, a, run


def sb_fwd(p, W, name):
    S = p.shape[0]
    TQ, TK = _query_rows(S), _key_strip(S)
    nh, nq = W // HD, S // TQ

    def body(q_ref, k_ref, v_ref, o_ref):
        i = pl.program_id(1)
        q = q_ref[...]
        su = (_iota2((HD, HD), 0) > _iota2((HD, HD), 1)).astype(BF16)
        last = (i * TQ + TQ - 1) // TK

        def step(gg, carry):
            acc, run = carry
            off = pl.multiple_of((last - gg) * TK, TK)
            _, _, a, run = _sb_strip(q, k_ref[pl.ds(off, TK), :], i * TQ, off, run, su)
            return acc + _dot(a.astype(BF16), v_ref[pl.ds(off, TK), :]), run

        acc, _ = lax.fori_loop(0, last + 1, step, (jnp.zeros((TQ, HD), F32), jnp.zeros((TQ, 1), F32)))
        o_ref[...] = acc.astype(o_ref.dtype)

    return pl.pallas_call(
        body,
        grid=(nh, nq),
        in_specs=[
            pl.BlockSpec((TQ, HD), lambda h, i: (i, h)),
            pl.BlockSpec((S, HD), lambda h, i: (0, nh + h)),
            pl.BlockSpec((S, HD), lambda h, i: (0, 2 * nh + h)),
        ],
        out_specs=pl.BlockSpec((TQ, HD), lambda h, i: (i, h)),
        out_shape=jax.ShapeDtypeStruct((S, 2 * W), BF16),
        compiler_params=_cp(("parallel", "arbitrary")),
        name=name,
    )(p, p, p)


def sb_bwd(p, dcat, W, name):
    S = p.shape[0]
    TQ, TK = _query_rows(S), _key_strip(S)
    nh, nq = W // HD, S // TQ
    scale = HD ** -0.5

    def body(q_ref, k_ref, v_ref, do_ref, dp_ref, dk_acc, dv_acc, e_scr, z_scr):
        i = pl.program_id(1)
        q = q_ref[...]
        do = do_ref[...]
        su = (_iota2((HD, HD), 0) > _iota2((HD, HD), 1)).astype(BF16)
        sl = (_iota2((HD, HD), 0) < _iota2((HD, HD), 1)).astype(BF16)
        last = (i * TQ + TQ - 1) // TK

        @pl.when(i == 0)
        def _():
            dk_acc[...] = jnp.zeros_like(dk_acc)
            dv_acc[...] = jnp.zeros_like(dv_acc)

        def pass_a(gg, run):
            g = last - gg
            off = pl.multiple_of(g * TK, TK)
            z, _, a, run = _sb_strip(q, k_ref[pl.ds(off, TK), :], i * TQ, off, run, su)
            e_scr[g] = a * _dot_nt(do, v_ref[pl.ds(off, TK), :])
            z_scr[g] = z
            dv_acc[pl.ds(off, TK), :] += _dot_tn(a.astype(BF16), do)
            return run

        lax.fori_loop(0, last + 1, pass_a, jnp.zeros((TQ, 1), F32))

        def pass_b(g, carry):
            dq, run_e = carry
            off = pl.multiple_of(g * TK, TK)
            e = e_scr[g]
            z = z_scr[g]
            mask = _strip_mask(z.shape, i * TQ, off, True)
            within = _block_sums(e, sl)
            before = []
            for b in range(len(within)):
                before.append(within[b] + run_e)
                run_e = run_e + jnp.sum(e[:, b * HD : (b + 1) * HD], axis=1, keepdims=True)
            sig = 1.0 / (1.0 + jnp.exp(-z))
            dz = jnp.where(mask, e * (1.0 - sig) - jnp.concatenate(before, axis=1) * sig, 0.0)
            dz = (dz * scale).astype(BF16)
            dq = dq + _dot(dz, k_ref[pl.ds(off, TK), :])
            dk_acc[pl.ds(off, TK), :] += _dot_tn(dz, q)
            return dq, run_e

        dq, _ = lax.fori_loop(0, last + 1, pass_b, (jnp.zeros((TQ, HD), F32), jnp.zeros((TQ, 1), F32)))
        dp_ref[0, pl.ds(pl.multiple_of(i * TQ, TQ), TQ), :] = dq.astype(dp_ref.dtype)

        @pl.when(i == nq - 1)
        def _():
            dp_ref[1] = dk_acc[...].astype(dp_ref.dtype)
            dp_ref[2] = dv_acc[...].astype(dp_ref.dtype)

    return pl.pallas_call(
        body,
        grid=(nh, nq),
        in_specs=[
            pl.BlockSpec((TQ, HD), lambda h, i: (i, h)),
            pl.BlockSpec((S, HD), lambda h, i: (0, nh + h)),
            pl.BlockSpec((S, HD), lambda h, i: (0, 2 * nh + h)),
            pl.BlockSpec((TQ, HD), lambda h, i: (i, h)),
        ],
        out_specs=pl.BlockSpec((3, S, HD), lambda h, i: (0, 0, h)),
        out_shape=jax.ShapeDtypeStruct((6, S, W), BF16),
        scratch_shapes=[
            pltpu.VMEM((S, HD), F32),
            pltpu.VMEM((S, HD), F32),
            pltpu.VMEM((S // TK, TQ, TK), F32),
            pltpu.VMEM((S // TK, TQ, TK), F32),
        ],
        compiler_params=_cp(("parallel", "arbitrary")),
        name=name,
    )(p, p, p, dcat)


def fox_gate_fwd(f, b, name):
    S = f.shape[0]
    nq = S // HD

    def body(f_ref, b_ref, c_ref, run):
        i = pl.program_id(0)

        @pl.when(i == 0)
        def _():
            run[...] = jnp.zeros_like(run)

        lf = -_softplus(-(f_ref[...] + b_ref[...]))
        tri = (_iota2((HD, HD), 0) >= _iota2((HD, HD), 1)).astype(BF16)
        c_ref[...] = _dot_ones_left(tri, lf) + run[...]
        run[...] += jnp.sum(lf, axis=0, keepdims=True)

    return pl.pallas_call(
        body,
        grid=(nq,),
        in_specs=[pl.BlockSpec((HD, 128), lambda i: (i, 0)), pl.BlockSpec((1, 128), lambda i: (0, 0))],
        out_specs=pl.BlockSpec((HD, 128), lambda i: (i, 0)),
        out_shape=jax.ShapeDtypeStruct((S, 128), F32),
        scratch_shapes=[pltpu.VMEM((1, 128), F32)],
        compiler_params=_cp(("arbitrary",)),
        name=name,
    )(f, b)


def fox_gate_bwd(f, b, dc, name):
    S = f.shape[0]
    nq = S // HD

    def body(f_ref, b_ref, dc_ref, df_ref, db_ref, run):
        i = pl.program_id(0)

        @pl.when(i == 0)
        def _():
            run[...] = jnp.zeros_like(run)

        dc = dc_ref[...]
        tri = (_iota2((HD, HD), 0) <= _iota2((HD, HD), 1)).astype(BF16)
        dlf = _dot_ones_left(tri, dc) + run[...]
        run[...] += jnp.sum(dc, axis=0, keepdims=True)
        x = f_ref[...] + b_ref[...]
        df = dlf * (1.0 / (1.0 + jnp.exp(x)))
        df_ref[...] = df
        db = jnp.sum(df, axis=0, keepdims=True)

        @pl.when(i == 0)
        def _():
            db_ref[...] = db

        @pl.when(i > 0)
        def _():
            db_ref[...] += db

    rev = pl.BlockSpec((HD, 128), lambda i: (nq - 1 - i, 0))
    vec = pl.BlockSpec((1, 128), lambda i: (0, 0))
    return pl.pallas_call(
        body,
        grid=(nq,),
        in_specs=[rev, vec, rev],
        out_specs=[rev, vec],
        out_shape=[jax.ShapeDtypeStruct((S, 128), F32), jax.ShapeDtypeStruct((1, 128), F32)],
        scratch_shapes=[pltpu.VMEM((1, 128), F32)],
        compiler_params=_cp(("arbitrary",)),
        name=name,
    )(f, b, dc)


def _fox_logits(q, ks, ct, cs, row0, off):
    s = _dot_nt(q, ks) * (HD ** -0.5) + (ct - cs)
    mask = _strip_mask(s.shape, row0, off, False)
    return jnp.where(mask, s, -1e30), mask


def fox_fwd(p, ccol, crow, cat, W, name):
    S = p.shape[0]
    TQ, TK = _query_rows(S), _key_strip(S)
    nh, nq = W // HD, S // TQ

    def body(q_ref, k_ref, v_ref, cc_ref, cr_ref, cat_ref, o_ref, lse_ref):
        i = pl.program_id(1)
        q = q_ref[...]
        ct = cc_ref[0]

        def step(g, carry):
            m, l, acc = carry
            off = pl.multiple_of(g * TK, TK)
            s, _ = _fox_logits(q, k_ref[pl.ds(off, TK), :], ct, cr_ref[0, pl.ds(g, 1), :], i * TQ, off)
            m_new = jnp.maximum(m, jnp.max(s, axis=1, keepdims=True))
            alpha = jnp.exp(m - m_new)
            pr = jnp.exp(s - m_new)
            l = alpha * l + jnp.sum(pr, axis=1, keepdims=True)
            acc = alpha * acc + _dot(pr.astype(BF16), v_ref[pl.ds(off, TK), :])
            return m_new, l, acc

        init = (jnp.full((TQ, 1), -1e30, F32), jnp.zeros((TQ, 1), F32), jnp.zeros((TQ, HD), F32))
        m, l, acc = lax.fori_loop(0, (i * TQ + TQ - 1) // TK + 1, step, init)
        o_ref[...] = (acc / l).astype(o_ref.dtype)
        lse_ref[0] = m + jnp.log(l)

    return pl.pallas_call(
        body,
        grid=(nh, nq),
        in_specs=[
            pl.BlockSpec((TQ, HD), lambda h, i: (i, 2 * nh + h)),
            pl.BlockSpec((S, HD), lambda h, i: (0, 3 * nh + h)),
            pl.BlockSpec((S, HD), lambda h, i: (0, 4 * nh + h)),
            pl.BlockSpec((1, TQ, 1), lambda h, i: (h, i, 0)),
            pl.BlockSpec((1, S // TK, TK), lambda h, i: (h, 0, 0)),
            pl.BlockSpec(memory_space=pl.ANY),
        ],
        out_specs=[pl.BlockSpec((TQ, HD), lambda h, i: (i, nh + h)), pl.BlockSpec((1, TQ, 1), lambda h, i: (h, i, 0))],
        out_shape=[jax.ShapeDtypeStruct(cat.shape, cat.dtype), jax.ShapeDtypeStruct((nh, S, 1), F32)],
        input_output_aliases={5: 0},
        compiler_params=_cp(("parallel", "arbitrary")),
        name=name,
    )(p, p, p, ccol, crow, cat)


def fox_bwd(p, ccol, crow, cat, lse, dcat, dp, W, name):
    S = p.shape[0]
    TQ, TK = _query_rows(S), _key_strip(S)
    nh, nq = W // HD, S // TQ
    scale = HD ** -0.5

    def body(q_ref, k_ref, v_ref, cc_ref, cr_ref, o_ref, lse_ref, do_ref, dp_in_ref, dp_ref, dcs_ref, dct_ref, dk_acc, dv_acc):
        i = pl.program_id(1)
        q = q_ref[...]
        do = do_ref[...]
        ct = cc_ref[0]
        lse_i = lse_ref[0]
        delta = jnp.sum(do.astype(F32) * o_ref[...].astype(F32), axis=1, keepdims=True)

        @pl.when(i == 0)
        def _():
            dk_acc[...] = jnp.zeros_like(dk_acc)
            dv_acc[...] = jnp.zeros_like(dv_acc)
            dcs_ref[...] = jnp.zeros_like(dcs_ref)

        def step(g, carry):
            dq, dct = carry
            off = pl.multiple_of(g * TK, TK)
            ks = k_ref[pl.ds(off, TK), :]
            s, mask = _fox_logits(q, ks, ct, cr_ref[0, pl.ds(g, 1), :], i * TQ, off)
            pr = jnp.where(mask, jnp.exp(s - lse_i), 0.0)
            ds = pr * (_dot_nt(do, v_ref[pl.ds(off, TK), :]) - delta)
            dv_acc[pl.ds(off, TK), :] += _dot_tn(pr.astype(BF16), do)
            dsb = (ds * scale).astype(BF16)
            dk_acc[pl.ds(off, TK), :] += _dot_tn(dsb, q)
            dcs_ref[0, pl.ds(g, 1), :] += jnp.sum(ds, axis=0, keepdims=True)
            return dq + _dot(dsb, ks), dct + jnp.sum(ds, axis=1, keepdims=True)

        dq, dct = lax.fori_loop(0, (i * TQ + TQ - 1) // TK + 1, step, (jnp.zeros((TQ, HD), F32), jnp.zeros((TQ, 1), F32)))
        dp_ref[0, pl.ds(pl.multiple_of(i * TQ, TQ), TQ), :] = dq.astype(dp_ref.dtype)
        dct_ref[0] = dct

        @pl.when(i == nq - 1)
        def _():
            dp_ref[1] = dk_acc[...].astype(dp_ref.dtype)
            dp_ref[2] = dv_acc[...].astype(dp_ref.dtype)

    return pl.pallas_call(
        body,
        grid=(nh, nq),
        in_specs=[
            pl.BlockSpec((TQ, HD), lambda h, i: (i, 2 * nh + h)),
            pl.BlockSpec((S, HD), lambda h, i: (0, 3 * nh + h)),
            pl.BlockSpec((S, HD), lambda h, i: (0, 4 * nh + h)),
            pl.BlockSpec((1, TQ, 1), lambda h, i: (h, i, 0)),
            pl.BlockSpec((1, S // TK, TK), lambda h, i: (h, 0, 0)),
            pl.BlockSpec((TQ, HD), lambda h, i: (i, nh + h)),
            pl.BlockSpec((1, TQ, 1), lambda h, i: (h, i, 0)),
            pl.BlockSpec((TQ, HD), lambda h, i: (i, nh + h)),
            pl.BlockSpec(memory_space=pl.ANY),
        ],
        out_specs=[
            pl.BlockSpec((3, S, HD), lambda h, i: (1, 0, h)),
            pl.BlockSpec((1, S // TK, TK), lambda h, i: (h, 0, 0)),
            pl.BlockSpec((1, TQ, 1), lambda h, i: (h, i, 0)),
        ],
        out_shape=[
            jax.ShapeDtypeStruct(dp.shape, dp.dtype),
            jax.ShapeDtypeStruct((nh, S // TK, TK), F32),
            jax.ShapeDtypeStruct((nh, S, 1), F32),
        ],
        input_output_aliases={8: 0},
        scratch_shapes=[pltpu.VMEM((S, HD), F32), pltpu.VMEM((S, HD), F32)],
        compiler_params=_cp(("parallel", "arbitrary")),
        name=name,
    )(p, p, p, ccol, crow, cat, lse, dcat, dp)


_GELU_K = math.sqrt(2.0 / math.pi)
_GELU_C = 0.044715


def _gelu(x):
    return 0.5 * x * (1.0 + jnp.tanh(_GELU_K * (x + _GELU_C * x * x * x)))


def _gelu_grad(x):
    t = jnp.tanh(_GELU_K * (x + _GELU_C * x * x * x))
    return 0.5 * (1.0 + t) + 0.5 * x * (1.0 - t * t) * (_GELU_K * (1.0 + 3.0 * _GELU_C * x * x))


def _layernorm_parts(gv):
    xc = gv - jnp.mean(gv, axis=-1, keepdims=True)
    r = lax.rsqrt(jnp.mean(xc * xc, axis=-1, keepdims=True) + EPS)
    return xc * r, r


def sg_fwd(p, sg_w, sg_bt, sg_g, W, name):
    S = p.shape[0]
    G, nq = W // HD, S // HD

    def body(u_ref, v_ref, w_ref, bt_ref, g_ref, o_ref):
        xh, _ = _layernorm_parts(_gelu(v_ref[...].astype(F32)))
        vn = (xh * g_ref[...]).astype(BF16)
        tri = _iota2((HD, HD), 0) >= _iota2((HD, HD), 1)
        for gi in range(G):
            cols = slice(gi * HD, (gi + 1) * HD)
            wt = jnp.where(tri, w_ref[gi], 0.0).astype(BF16)
            mixed = _dot(wt, vn[:, cols]) + bt_ref[:, gi : gi + 1]
            o_ref[:, cols] = (_gelu(u_ref[:, cols].astype(F32)) * mixed).astype(o_ref.dtype)

    return pl.pallas_call(
        body,
        grid=(nq,),
        in_specs=[
            pl.BlockSpec((HD, W), lambda i: (i, 0)),
            pl.BlockSpec((HD, W), lambda i: (i, 1)),
            pl.BlockSpec((G, HD, HD), lambda i: (0, 0, 0)),
            pl.BlockSpec((HD, G), lambda i: (0, 0)),
            pl.BlockSpec((1, W), lambda i: (0, 0)),
        ],
        out_specs=pl.BlockSpec((HD, W), lambda i: (i, 0)),
        out_shape=jax.ShapeDtypeStruct((S, 2 * W), BF16),
        compiler_params=_cp(("parallel",)),
        name=name,
    )(p, p, sg_w, sg_bt, sg_g.reshape(1, W))


def sg_bwd(p, sg_w, sg_bt, sg_g, dcat, W, name):
    S = p.shape[0]
    G, nq = W // HD, S // HD

    def body(u_ref, v_ref, w_ref, bt_ref, g_ref, do_ref, dp_ref, dw_ref, dbt_ref, dg_ref, dvn_scr):
        i = pl.program_id(0)

        @pl.when(i == 0)
        def _():
            dw_ref[...] = jnp.zeros_like(dw_ref)
            dbt_ref[...] = jnp.zeros_like(dbt_ref)
            dg_ref[...] = jnp.zeros_like(dg_ref)

        v = v_ref[...].astype(F32)
        xh, r = _layernorm_parts(_gelu(v))
        gg = g_ref[...]
        vn = (xh * gg).astype(BF16)
        tri = _iota2((HD, HD), 0) >= _iota2((HD, HD), 1)
        for gi in range(G):
            cols = slice(gi * HD, (gi + 1) * HD)
            wt = jnp.where(tri, w_ref[gi], 0.0).astype(BF16)
            mixed = _dot(wt, vn[:, cols]) + bt_ref[:, gi : gi + 1]
            u = u_ref[:, cols].astype(F32)
            do = do_ref[:, cols].astype(F32)
            dp_ref[0, :, cols] = (do * mixed * _gelu_grad(u)).astype(dp_ref.dtype)
            dmix = do * _gelu(u)
            dmb = dmix.astype(BF16)
            dw_ref[gi] += jnp.where(tri, _dot_nt(dmb, vn[:, cols]), 0.0)
            dbt_ref[:, gi : gi + 1] += jnp.sum(dmix, axis=1, keepdims=True)
            dvn_scr[:, cols] = _dot_tn(wt, dmb)
        dvn = dvn_scr[...]
        dg_ref[...] += jnp.sum(dvn * xh, axis=0, keepdims=True)
        dxh = dvn * gg
        dgv = r * (dxh - jnp.mean(dxh, axis=-1, keepdims=True) - xh * jnp.mean(dxh * xh, axis=-1, keepdims=True))
        dp_ref[1] = (dgv * _gelu_grad(v)).astype(dp_ref.dtype)

    return pl.pallas_call(
        body,
        grid=(nq,),
        in_specs=[
            pl.BlockSpec((HD, W), lambda i: (i, 0)),
            pl.BlockSpec((HD, W), lambda i: (i, 1)),
            pl.BlockSpec((G, HD, HD), lambda i: (0, 0, 0)),
            pl.BlockSpec((HD, G), lambda i: (0, 0)),
            pl.BlockSpec((1, W), lambda i: (0, 0)),
            pl.BlockSpec((HD, W), lambda i: (i, 0)),
        ],
        out_specs=[
            pl.BlockSpec((2, HD, W), lambda i: (0, i, 0)),
            pl.BlockSpec((G, HD, HD), lambda i: (0, 0, 0)),
            pl.BlockSpec((HD, G), lambda i: (0, 0)),
            pl.BlockSpec((1, W), lambda i: (0, 0)),
        ],
        out_shape=[
            jax.ShapeDtypeStruct((6, S, W), BF16),
            jax.ShapeDtypeStruct((G, HD, HD), F32),
            jax.ShapeDtypeStruct((HD, G), F32),
            jax.ShapeDtypeStruct((1, W), F32),
        ],
        scratch_shapes=[pltpu.VMEM((HD, W), F32)],
        compiler_params=_cp(("arbitrary",)),
        name=name,
    )(p, p, sg_w, sg_bt, sg_g.reshape(1, W), dcat)


def local_step(x, target, wts, at, on_grad):
    S, D = x.shape
    W = D // 2
    nb, F = wts["nb"], wts["F"]
    g = {}

    def ffn_fwd(xin, l):
        h = rms_fwd(xin, wts[f"{l}_ffn_norm_g"], f"{l}_ffn_rms")
        u = mm_nn(h, wts[f"{l}_ffn_up"], nb, f"{l}_ffn_up_mm")
        act = ffn_act_fwd(u, wts[f"{l}_ffn_conv_w"], F, f"{l}_ffn_act")
        xout = mm_nn(act, wts[f"{l}_ffn_down"], 1, f"{l}_ffn_down_mm", out_dtype=F32, res=xin)
        return xout, (xin, h, u, act)

    def ffn_bwd(dxout, dxoutb, saved, l):
        xin, h, u, act = saved
        dact = mm_nt(dxoutb, wts[f"{l}_ffn_down"], 1, S, F, f"{l}_ffn_down_dx")
        dact = on_grad(f"{l}_ffn_down", mm_tn(act, dxoutb, 1, D, f"{l}_ffn_down_dw"), dact)
        du, dcw = ffn_act_bwd(u, wts[f"{l}_ffn_conv_w"], dact, F, f"{l}_ffn_act_bwd")
        g[f"{l}_ffn_conv_w"] = jnp.concatenate([dcw[0], dcw[1]], axis=1)
        du2 = du.reshape(2 * S, F)
        n = wts[f"{l}_ffn_up"].shape[1]
        tn = _pick(n, (1408, 1024, 768, 512, 256, 128))
        per_half = F // tn
        nt = n // tn

        def up_block(i, j, t):
            vb = j * nt + t
            return vb // per_half, vb % per_half

        tm = _pick(S, (1024, 512, 256, 128))

        def nt_map(i, j, t):
            half, cb = up_block(i, j, t)
            return (half * (S // tm) + i, cb)

        def tn_map(j, t):
            half, cb = up_block(0, j, t)
            return (half, cb)

        dh = mm_nt(du2, wts[f"{l}_ffn_up"], nb, S, D, f"{l}_ffn_up_dx", dy_maps=[nt_map], tm=tm, tn=tn)
        dh = on_grad(f"{l}_ffn_up", mm_tn(h, du2, nb, n, f"{l}_ffn_up_dw", dy_maps=[tn_map], tn=tn), dh)
        dxin, dxinb, dg = rms_bwd(xin, wts[f"{l}_ffn_norm_g"], dh, dxout, f"{l}_ffn_rms_bwd")
        g[f"{l}_ffn_norm_g"] = dg
        return dxin, dxinb

    h0 = rms_fwd(x, wts["l0_mix_norm_g"], "l0_mix_rms")
    p0 = mm_nn(h0, wts["l0_w_in"], nb, "l0_w_in_mm")
    cat0 = sb_fwd(p0, W, "l0_sb_fwd")
    cat0 = sc_fwd(p0, wts["l0_sc_conv_w"], cat0, W, "l0_sc_fwd")
    x1 = mm_nn(cat0, wts["l0_w_out"], 1, "l0_w_out_mm", out_dtype=F32, res=x)
    x2, ffn0_saved = ffn_fwd(x1, "l0")

    x2 = at("l1_w_in", x2, None)
    nh = W // HD
    h2 = rms_fwd(x2, wts["l1_mix_norm_g"], "l1_mix_rms")
    p1 = mm_nn(h2, wts["l1_w_in_main"], 1, "l1_w_in_mm")
    f = mm_nn(h2, wts["l1_w_in_f"], 1, "l1_w_f_mm", out_dtype=F32)
    bf = jnp.zeros((1, 128), F32).at[0, :nh].set(wts["l1_fox_b_f"])
    c = fox_gate_fwd(f, bf, "l1_fox_gate")
    c_heads = c[:, :nh].T
    ccol = c_heads[:, :, None]
    crow = c_heads.reshape(nh, S // _key_strip(S), _key_strip(S))
    sg_bt = wts["l1_sg_b"].T
    cat1 = sg_fwd(p1, wts["l1_sg_w"], sg_bt, wts["l1_sg_norm_g"], W, "l1_sg_fwd")
    cat1, lse = fox_fwd(p1, ccol, crow, cat1, W, "l1_fox_fwd")
    x3 = mm_nn(cat1, wts["l1_w_out"], 1, "l1_w_out_mm", out_dtype=F32, res=x2)
    x4, ffn1_saved = ffn_fwd(x3, "l1")

    dx4, dx4b, dgf, loss = loss_head(x4, wts["final_norm_g"], target, "loss_head")
    dx4b = at("loss", dx4b, loss)
    g["final_norm_g"] = dgf

    dx3, dx3b = ffn_bwd(dx4, dx4b, ffn1_saved, "l1")
    dcat1 = mm_nt(dx3b, wts["l1_w_out"], 1, S, D, "l1_w_out_dx")
    dcat1 = on_grad("l1_w_out", mm_tn(cat1, dx3b, 1, D, "l1_w_out_dw"), dcat1)
    dp1, dsgw, dsgbt, dsgg = sg_bwd(p1, wts["l1_sg_w"], sg_bt, wts["l1_sg_norm_g"], dcat1, W, "l1_sg_bwd")
    dp1, dcs, dct = fox_bwd(p1, ccol, crow, cat1, lse, dcat1, dp1, W, "l1_fox_bwd")
    g["l1_sg_w"], g["l1_sg_b"], g["l1_sg_norm_g"] = dsgw, dsgbt.T, dsgg
    dc = jnp.zeros((S, 128), F32).at[:, :nh].set((dct[:, :, 0] - dcs.reshape(nh, S)).T)
    df, dbf = fox_gate_bwd(f, bf, dc, "l1_fox_gate_bwd")
    g["l1_fox_b_f"] = dbf[0, :nh]
    dfb = df.astype(BF16)
    tn1 = _pick(W, (1024, 512, 256, 128))
    tm1 = _pick(S, (1024, 512, 256, 128))
    per_part = W // tn1
    part_of = lambda pt: pt + pt // 2 - pt // 4

    def nt_map1(i, j, t):
        return (part_of(t // per_part) * (S // tm1) + i, t % per_part)

    def tn_map1(j, t):
        return (part_of(t // per_part), t % per_part)

    dp1_2d = dp1.reshape(6 * S, W)
    dw_main = mm_tn(h2, dp1_2d, 1, 5 * W, "l1_w_in_dw", dy_maps=[tn_map1], tn=tn1)
    dw_f = mm_tn(h2, dfb, 1, 128, "l1_w_f_dw")
    dh2 = mm_nt(dfb, wts["l1_w_in_f"], 1, S, D, "l1_w_f_dx", out_dtype=F32)
    dh2 = mm_nt(dp1_2d, wts["l1_w_in_main"], 1, S, D, "l1_w_in_dx", res=dh2, dy_maps=[nt_map1], tm=tm1, tn=tn1)
    dh2 = on_grad("l1_w_in", jnp.concatenate([dw_main, dw_f[:, :nh]], axis=1), dh2)
    dx2, dx2b, dg = rms_bwd(x2, wts["l1_mix_norm_g"], dh2, dx3, "l1_mix_rms_bwd")
    g["l1_mix_norm_g"] = dg

    dx1, dx1b = ffn_bwd(dx2, dx2b, ffn0_saved, "l0")
    dcat0 = mm_nt(dx1b, wts["l0_w_out"], 1, S, D, "l0_w_out_dx")
    dcat0 = on_grad("l0_w_out", mm_tn(cat0, dx1b, 1, D, "l0_w_out_dw"), dcat0)
    dp0 = sb_bwd(p0, dcat0, W, "l0_sb_bwd")
    dp0, dscw = sc_bwd(p0, wts["l0_sc_conv_w"], dcat0, dp0, W, "l0_sc_bwd")
    g["l0_sc_conv_w"] = dscw
    n0 = wts["l0_w_in"].shape[1]
    td0 = math.gcd(n0, W)
    nd0 = n0 // td0
    tm0 = _pick(S, (1024, 512, 256, 128))
    per_part0 = W // td0

    def nt_maps0(k):
        def f(i, j, t):
            vb = j * nd0 + k
            return ((vb // per_part0) * (S // tm0) + i, vb % per_part0)
        return f

    def tn_maps0(k):
        def f(j, t):
            vb = j * nd0 + k
            return (vb // per_part0, vb % per_part0)
        return f

    dp0_2d = dp0.reshape(6 * S, W)
    dw0 = mm_tn(h0, dp0_2d, nb, n0, "l0_w_in_dw", dy_maps=[tn_maps0(k) for k in range(nd0)], tn=n0)
    dp0_2d = on_grad("l0_w_in", dw0, dp0_2d)
    dh0 = mm_nt(dp0_2d, wts["l0_w_in"], nb, S, D, "l0_w_in_dx", dy_maps=[nt_maps0(k) for k in range(nd0)], tm=tm0, tn=n0)
    dh0 = on_grad(None, None, dh0)
    dx0, _, dg = rms_bwd(x, wts["l0_mix_norm_g"], dh0, dx1, "l0_mix_rms_bwd")
    g["l0_mix_norm_g"] = dg
    return dx0, g


GATHER_ID, PAIR_ID, CHIPS_ID = 1, 2, 3


def _place():
    return lax.axis_index("x"), lax.axis_index("y"), lax.axis_index("c")


def _other_chips(x, y):
    return [(x, 1 - y), (1 - x, y), (1 - x, 1 - y)]


def _handshake(peers):
    barrier = pltpu.get_barrier_semaphore()
    for peer in peers:
        pl.semaphore_signal(barrier, inc=1, device_id=peer, device_id_type=MESH)
    pl.semaphore_wait(barrier, len(peers))


UPDATE_LAG = 2


def _on_sequencer(body, out_type, scratch_types, collective_id, name):
    return pl.kernel(
        body,
        out_type=out_type,
        mesh=plsc.ScalarSubcoreMesh(axis_name="seq", num_cores=1),
        scratch_types=scratch_types,
        compiler_params=pltpu.CompilerParams(collective_id=collective_id),
        name=name,
    )


def all_gather(arrs, name):
    n = len(arrs)

    def body(*refs):
        xs, outs = refs[:n], refs[n : 2 * n]
        send_sems, recv_sems, local_sems = refs[2 * n :]
        x, y, c = _place()
        me, sibling = (x, y, c), (x, y, 1 - c)
        chips = _other_chips(x, y)
        _handshake([sibling] + [(*chip, c) for chip in chips])

        def copy(a, k, block, to, src=None):
            px, py, pc = block
            dst = outs[a].at[4 * px + 2 * py + pc]
            return pltpu.make_async_remote_copy(
                src_ref=dst if src is None else src, dst_ref=dst,
                send_sem=send_sems.at[7 * a + k], recv_sem=recv_sems.at[7 * a + k], device_id=to, device_id_type=MESH,
            )

        mine = [pltpu.make_async_copy(xs[a], outs[a].at[4 * x + 2 * y + c], local_sems.at[a]) for a in range(n)]
        for cp in mine:
            cp.start()
        first = []
        for a in range(n):
            first.append(copy(a, 0, me, sibling, src=xs[a]))
            first += [copy(a, 1 + j, me, (*chip, c), src=xs[a]) for j, chip in enumerate(chips)]
        for cp in first:
            cp.start()
        passed = []
        for a in range(n):
            for j, chip in enumerate(chips):
                copy(a, 1 + j, (*chip, c), me).wait_recv()
                cp = copy(a, 4 + j, (*chip, c), sibling)
                cp.start()
                passed.append(cp)
        for a in range(n):
            copy(a, 0, sibling, me).wait_recv()
            for j, chip in enumerate(chips):
                copy(a, 4 + j, (*chip, 1 - c), me).wait_recv()
        for cp in first + passed:
            cp.wait_send()
        for cp in mine:
            cp.wait()

    out_type = [jax.ShapeDtypeStruct((NDEV,) + a.shape, a.dtype) for a in arrs]
    sems = [pltpu.SemaphoreType.DMA((7 * n,)), pltpu.SemaphoreType.DMA((7 * n,)), pltpu.SemaphoreType.DMA((n,))]
    return _on_sequencer(body, out_type, sems, GATHER_ID, name)(*arrs)


_IN_HBM = pl.BlockSpec(memory_space=pltpu.HBM)
_IN_SEM = pl.BlockSpec(memory_space=pltpu.SEMAPHORE)
_EFFECT = pltpu.SideEffectType.DATAFLOW_SIDE_EFFECTING


def _split_start(make_copies, src, land_shape, nsem, name):
    def body(src_ref, land_ref, send_sems, recv_sems, land_thru, token):
        for cp in make_copies(src_ref, land_ref, send_sems, recv_sems):
            cp.start()
        token[...] = jnp.zeros_like(token)

    send_sems, recv_sems, land_thru, token = pl.pallas_call(
        body,
        name=name,
        out_shape=(
            pltpu.SemaphoreType.DMA((nsem,)), pltpu.SemaphoreType.DMA((nsem,)),
            pltpu.HBM(land_shape, src.dtype), jax.ShapeDtypeStruct((8, 128), F32),
        ),
        in_specs=(_IN_HBM, _IN_HBM),
        out_specs=(_IN_SEM, _IN_SEM, _IN_HBM, pl.BlockSpec(memory_space=pltpu.VMEM)),
        input_output_aliases={1: 2},
        compiler_params=pltpu.CompilerParams(has_side_effects=_EFFECT),
    )(src, pltpu.with_memory_space_constraint(lax.empty(land_shape, src.dtype), pltpu.HBM))
    return send_sems, recv_sems, src, land_thru, token


def _split_wait(make_copies, send_sems, recv_sems, src_thru, land_thru, after, name):
    def body(src_ref, land_ref, send_sems, recv_sems, after_ref, land_out):
        for cp in make_copies(src_ref, land_ref, send_sems, recv_sems):
            cp.wait_send()
            cp.wait_recv()

    return pl.pallas_call(
        body,
        name=name,
        out_shape=pltpu.HBM(land_thru.shape, land_thru.dtype),
        in_specs=(_IN_HBM, _IN_HBM, _IN_SEM, _IN_SEM, pl.BlockSpec(memory_space=pl.ANY)),
        out_specs=_IN_HBM,
        input_output_aliases={1: 0},
        compiler_params=pltpu.CompilerParams(has_side_effects=_EFFECT),
    )(src_thru, land_thru, send_sems, recv_sems, after)


def _pair_copies(src_ref, land_ref, send_sems, recv_sems):
    x, y, c = _place()
    return [
        pltpu.make_async_remote_copy(
            src_ref=src_ref.at[k, 1 - c], dst_ref=land_ref.at[k],
            send_sem=send_sems.at[k], recv_sem=recv_sems.at[k], device_id=(x, y, 1 - c), device_id_type=MESH,
        )
        for k in range(4)
    ]


def _chip_copies(src_ref, land_ref, send_sems, recv_sems):
    x, y, c = _place()
    return [
        pltpu.make_async_remote_copy(
            src_ref=src_ref.at[2 * px + py], dst_ref=land_ref.at[2 * x + y],
            send_sem=send_sems.at[j], recv_sem=recv_sems.at[j], device_id=(px, py, c), device_id_type=MESH,
        )
        for j, (px, py) in enumerate(_other_chips(x, y))
    ]


def _row_tile(R, C, max_elems):
    if R * C <= max_elems:
        return R
    best = None
    for tr in range(16, R, 16):
        if R % tr == 0 and tr * C <= max_elems:
            best = tr
    return best or R


def pair_sum(a42, land4, core, name):
    _, _, R, C = a42.shape
    tr = _row_tile(R, C, 1 << 20)

    def body(core_ref, a_ref, l_ref, o_ref):
        o_ref[...] = (a_ref[0].astype(F32) + l_ref[...].astype(F32)).astype(o_ref.dtype)

    return pl.pallas_call(
        body,
        grid_spec=pltpu.PrefetchScalarGridSpec(
            num_scalar_prefetch=1,
            grid=(4, R // tr),
            in_specs=[
                pl.BlockSpec((1, 1, tr, C), lambda k, r, core_ref: (k, core_ref[0], r, 0)),
                pl.BlockSpec((1, tr, C), lambda k, r, core_ref: (k, r, 0)),
            ],
            out_specs=pl.BlockSpec((1, tr, C), lambda k, r, core_ref: (k, r, 0)),
        ),
        out_shape=jax.ShapeDtypeStruct((4, R, C), BF16),
        compiler_params=_cp(("parallel", "parallel")),
        name=name,
    )(core, a42, land4)


def sum_slots(parts, name):
    P, R, C = parts.shape

    def body(p_ref, o_ref):
        acc = p_ref[0].astype(F32)
        for k in range(1, P):
            acc = acc + p_ref[k].astype(F32)
        o_ref[...] = acc

    tr = _row_tile(R, P * C, 1 << 21)
    return pl.pallas_call(
        body,
        grid=(R // tr,),
        in_specs=[pl.BlockSpec((P, tr, C), lambda r: (0, r, 0))],
        out_specs=pl.BlockSpec((tr, C), lambda r: (r, 0)),
        out_shape=jax.ShapeDtypeStruct((R, C), F32),
        compiler_params=_cp(("parallel",)),
        name=name,
    )(parts)


def adamw(w, m, v, parts, name):
    R, C = w.shape
    P = parts.shape[0]
    tr = _pick(R, (256, 128, 64, 32, 16, 8))
    c1 = 1.0 - ADAM_B1 ** ADAM_STEP
    c2 = 1.0 - ADAM_B2 ** ADAM_STEP

    def body(w_ref, m_ref, v_ref, p_ref, g_ref, d_ref, nm_ref, nv_ref):
        g = p_ref[0].astype(F32)
        for k in range(1, P):
            g = g + p_ref[k].astype(F32)
        nm = ADAM_B1 * m_ref[...] + (1.0 - ADAM_B1) * g
        nv = ADAM_B2 * v_ref[...] + (1.0 - ADAM_B2) * (g * g)
        g_ref[...] = g
        nm_ref[...] = nm
        nv_ref[...] = nv
        d_ref[...] = -ADAM_LR * ((nm / c1) / (jnp.sqrt(nv / c2) + ADAM_EPS) + ADAM_WD * w_ref[...])

    blk = pl.BlockSpec((tr, C), lambda r: (r, 0))
    shp = jax.ShapeDtypeStruct((R, C), F32)
    return pl.pallas_call(
        body,
        grid=(R // tr,),
        in_specs=[blk, blk, blk, pl.BlockSpec((P, tr, C), lambda r: (0, r, 0))],
        out_specs=[blk, blk, blk, blk],
        out_shape=[shp, shp, shp, shp],
        compiler_params=_cp(("parallel",)),
        name=name,
    )(w, m, v, parts)


def adamw_reduced(w, m, v, own, land, chip, name):
    R, C = w.shape
    tr = _pick(R, (256, 128, 64, 32, 16, 8))
    c1 = 1.0 - ADAM_B1 ** ADAM_STEP
    c2 = 1.0 - ADAM_B2 ** ADAM_STEP

    def body(chip_ref, w_ref, m_ref, v_ref, own_ref, land_ref, g_ref, d_ref, nm_ref, nv_ref):
        mine = own_ref[0].astype(F32)
        g = None
        for k in range(4):
            term = jnp.where(chip_ref[0] == k, mine, land_ref[k].astype(F32))
            g = term if g is None else g + term
        nm = ADAM_B1 * m_ref[...] + (1.0 - ADAM_B1) * g
        nv = ADAM_B2 * v_ref[...] + (1.0 - ADAM_B2) * (g * g)
        g_ref[...] = g
        nm_ref[...] = nm
        nv_ref[...] = nv
        d_ref[...] = -ADAM_LR * ((nm / c1) / (jnp.sqrt(nv / c2) + ADAM_EPS) + ADAM_WD * w_ref[...])

    blk = pl.BlockSpec((tr, C), lambda r, chip_ref: (r, 0))
    shp = jax.ShapeDtypeStruct((R, C), F32)
    return pl.pallas_call(
        body,
        grid_spec=pltpu.PrefetchScalarGridSpec(
            num_scalar_prefetch=1,
            grid=(R // tr,),
            in_specs=[
                blk, blk, blk,
                pl.BlockSpec((1, tr, C), lambda r, chip_ref: (chip_ref[0], r, 0)),
                pl.BlockSpec((4, tr, C), lambda r, chip_ref: (0, r, 0)),
            ],
            out_specs=[blk, blk, blk, blk],
        ),
        out_shape=[shp, shp, shp, shp],
        compiler_params=_cp(("parallel",)),
        name=name,
    )(chip, w, m, v, own, land)


_WEIGHTS = [
    "l0_mix_norm_g", "l0_w_in", "l0_sc_conv_w", "l0_w_out", "l0_ffn_norm_g", "l0_ffn_up", "l0_ffn_conv_w", "l0_ffn_down",
    "l1_mix_norm_g", "l1_w_in", "l1_fox_b_f", "l1_sg_w", "l1_sg_b", "l1_sg_norm_g", "l1_w_out", "l1_ffn_norm_g",
    "l1_ffn_up", "l1_ffn_conv_w", "l1_ffn_down", "final_norm_g",
]
_COL_SHARDED = ["l0_w_in", "l0_ffn_up", "l1_w_in", "l1_ffn_up"]
_ROW_SHARDED = ["l0_w_out", "l0_ffn_down", "l1_w_out", "l1_ffn_down"]
_BIG = ["l0_w_in", "l0_w_out", "l0_ffn_up", "l0_ffn_down", "l1_w_in", "l1_w_out", "l1_ffn_up", "l1_ffn_down"]
_CONV = ["l0_sc_conv_w", "l0_ffn_conv_w", "l1_ffn_conv_w"]
_SMALL = [n for n in _WEIGHTS if n not in _BIG]
_PACK_ROWS = 8


def _pack(arrs):
    flat = []
    for a in arrs:
        v = a.reshape(-1).astype(F32)
        pad = (-v.shape[0]) % (_PACK_ROWS * 128)
        flat.append(jnp.pad(v, (0, pad)))
    return jnp.concatenate(flat).reshape(-1, 128)


def _unpack(packed, shapes):
    out, off = [], 0
    flat = packed.reshape(-1)
    for shp in shapes:
        size = math.prod(shp)
        out.append(flat[off : off + size].reshape(shp))
        off += size + (-size) % (_PACK_ROWS * 128)
    return out


def kernel(x, l0_mix_norm_g, l0_w_in, l0_sc_conv_w, l0_w_out, l0_ffn_norm_g, l0_ffn_up, l0_ffn_conv_w, l0_ffn_down, l1_mix_norm_g, l1_w_in, l1_fox_b_f, l1_sg_w, l1_sg_b, l1_sg_norm_g, l1_w_out, l1_ffn_norm_g, l1_ffn_up, l1_ffn_conv_w, l1_ffn_down, final_norm_g, loss_target, m_l0_mix_norm_g, m_l0_w_in, m_l0_sc_conv_w, m_l0_w_out, m_l0_ffn_norm_g, m_l0_ffn_up, m_l0_ffn_conv_w, m_l0_ffn_down, m_l1_mix_norm_g, m_l1_w_in, m_l1_fox_b_f, m_l1_sg_w, m_l1_sg_b, m_l1_sg_norm_g, m_l1_w_out, m_l1_ffn_norm_g, m_l1_ffn_up, m_l1_ffn_conv_w, m_l1_ffn_down, m_final_norm_g, v_l0_mix_norm_g, v_l0_w_in, v_l0_sc_conv_w, v_l0_w_out, v_l0_ffn_norm_g, v_l0_ffn_up, v_l0_ffn_conv_w, v_l0_ffn_down, v_l1_mix_norm_g, v_l1_w_in, v_l1_fox_b_f, v_l1_sg_w, v_l1_sg_b, v_l1_sg_norm_g, v_l1_w_out, v_l1_ffn_norm_g, v_l1_ffn_up, v_l1_ffn_conv_w, v_l1_ffn_down, v_final_norm_g):
    given = dict(locals())
    w = {n: given[n] for n in _WEIGHTS}
    mom = {n: given["m_" + n] for n in _WEIGHTS}
    var = {n: given["v_" + n] for n in _WEIGHTS}
    xs, target = x[0], loss_target[0]
    S, D = xs.shape
    W = D // 2
    nh = W // HD
    cx, cy, cc = _place()
    me = 4 * cx + 2 * cy + cc

    wts = {"nb": NDEV, "F": l0_ffn_down.shape[0] * NDEV}
    for n in _SMALL:
        if n not in _CONV:
            wts[n] = w[n]
    gathered, loss_sum = {}, []

    def start_gather(n):
        got = all_gather([w[n].astype(BF16)] + ([w[c] for c in _CONV] if n == _BIG[0] else []), f"gather_{n}")
        if n == "l1_w_in":
            gathered[n] = got[0]
        elif n in _ROW_SHARDED:
            wts[n] = got[0].reshape(-1, D)
        else:
            wts[n] = got[0].reshape(NDEV * D, -1)
        for c, taps in zip(_CONV, got[1:]):
            wts[c] = taps.transpose(1, 0, 2).reshape(CONV_K, -1)

    def at(point, after, value):
        if point == "l1_w_in":
            got, after = lax.optimization_barrier((gathered[point], after))
            w_in1 = got.transpose(1, 0, 2).reshape(D, -1)
            wts["l1_w_in_main"] = w_in1[:, : 5 * W]
            wts["l1_w_in_f"] = jnp.pad(w_in1[:, 5 * W :], ((0, 0), (0, 128 - nh)))
        elif point == "loss":
            total, after = lax.optimization_barrier((lax.psum(value[0, 0], ("x", "y", "c")), after))
            loss_sum.append(total)
        return after

    core = jnp.reshape(cc, (1,)).astype(jnp.int32)
    chip = jnp.reshape(2 * cx + cy, (1,)).astype(jnp.int32)
    pair_flying, chip_flying = [], []
    out_g, out_d, out_m, out_v = {}, {}, {}, {}

    def tie(value, after):
        if after is None:
            return value, None
        return lax.optimization_barrier((value, after))

    def to_chips(after):
        n, flying = pair_flying.pop()
        landed = _split_wait(_pair_copies, *flying, f"reduce_pair_wait_{n}")
        summed = pair_sum(flying[2], landed, core, f"pair_sum_{n}")
        *flying, token = _split_start(_chip_copies, summed, summed.shape, 3, f"reduce_chips_{n}")
        token, after = tie(token, after)
        chip_flying.append((n, flying + [token]))
        return after

    def update(after, behind=None):
        n, flying = chip_flying.pop(0)
        if behind is not None:
            flying[4], _ = lax.optimization_barrier((flying[4], behind))
        landed = _split_wait(_chip_copies, *flying, f"reduce_chips_wait_{n}")
        res = adamw_reduced(w[n], mom[n], var[n], flying[2], landed, chip, f"adamw_{n}")
        res, after = tie(res, after)
        out_g[n], out_d[n], out_m[n], out_v[n] = res
        return after, res[0]

    def on_grad(n, term, after):
        if n is None:
            return to_chips(after)
        if n == "l1_w_in":
            term = term.reshape(D, NDEV, -1).transpose(1, 0, 2)
        elif n in _ROW_SHARDED:
            term = term.reshape(NDEV, -1, D)
        else:
            term = term.reshape(NDEV, D, -1)
        term = term.reshape((4, 2) + term.shape[1:])
        *flying, token = _split_start(_pair_copies, term, term.shape[:1] + term.shape[2:], 4, f"reduce_pair_{n}")
        token, after = tie(token, after)
        if len(chip_flying) == UPDATE_LAG:
            after, _ = update(after)
        if pair_flying:
            after = to_chips(after)
        pair_flying.append((n, flying + [token]))
        return after

    for n in _BIG:
        start_gather(n)
    dx, g = local_step(xs, target, wts, at, on_grad)
    done = None
    while chip_flying:
        _, done = update(None, behind=done)
    loss = loss_sum[0]

    small_terms = [g[n] for n in _SMALL]
    small_shapes = [tuple(t.shape) for t in small_terms]
    packed = _pack(small_terms)
    all_terms = all_gather([packed], "gather_small_grads")[0]
    small_sum = _unpack(sum_slots(all_terms, "sum_small_grads"), small_shapes)
    small_g = {}
    for n, t in zip(_SMALL, small_sum):
        if n in _CONV:
            cols = w[n].shape[1]
            t = lax.dynamic_slice_in_dim(t, me * cols, cols, axis=1)
        small_g[n] = t.reshape(w[n].shape)
    shapes = [w[n].shape for n in _SMALL]
    res = adamw(
        _pack([w[n] for n in _SMALL]), _pack([mom[n] for n in _SMALL]), _pack([var[n] for n in _SMALL]),
        _pack([small_g[n] for n in _SMALL])[None], "adamw_small",
    )
    for dst, packed_out in zip((out_g, out_d, out_m, out_v), res):
        for n, t in zip(_SMALL, _unpack(packed_out, shapes)):
            dst[n] = t

    return (loss, dx[None], *[out_g[n] for n in _WEIGHTS], *[out_d[n] for n in _WEIGHTS],
            *[out_m[n] for n in _WEIGHTS], *[out_v[n] for n in _WEIGHTS])
```

```python
import functools
import math

import jax
import jax.numpy as jnp
from jax import lax
from jax.experimental import pallas as pl
from jax.experimental.pallas import tpu as pltpu
from jax.experimental.pallas import tpu_sc as plsc

F32 = jnp.float32
BF16 = jnp.bfloat16
HD = 128
EPS = 1e-6
CONV_K = 3
VMEM_LIMIT_BYTES = 48 << 20
NDEV = 8
MESH = pl.DeviceIdType.MESH

ADAM_LR = 0.001
ADAM_B1 = 0.9
ADAM_B2 = 0.999
ADAM_EPS = 1e-08
ADAM_WD = 0.01
ADAM_STEP = 10


def _cp(sem):
    return pltpu.CompilerParams(dimension_semantics=sem, vmem_limit_bytes=VMEM_LIMIT_BYTES)


def _pick(n, prefs):
    for p in prefs:
        if n % p == 0:
            return p
    return n


def _dot(a, b):
    return jnp.dot(a, b, preferred_element_type=F32)


def _dot_nt(a, b):
    return lax.dot_general(a, b, (((1,), (1,)), ((), ())), preferred_element_type=F32)


def _dot_tn(a, b):
    return lax.dot_general(a, b, (((0,), (0,)), ((), ())), preferred_element_type=F32)


def _split3(x):
    hi = x.astype(BF16)
    r = x - hi.astype(F32)
    mid = r.astype(BF16)
    lo = (r - mid.astype(F32)).astype(BF16)
    return hi, mid, lo


def _dot_ones_right(x, ones_bf16):
    hi, mid, lo = _split3(x)
    return _dot(hi, ones_bf16) + _dot(mid, ones_bf16) + _dot(lo, ones_bf16)


def _dot_ones_left(ones_bf16, x):
    hi, mid, lo = _split3(x)
    return _dot(ones_bf16, hi) + _dot(ones_bf16, mid) + _dot(ones_bf16, lo)


def _iota2(shape, axis):
    return lax.broadcasted_iota(jnp.int32, shape, axis)


def mm_nn(a, w2d, nb, name, out_dtype=BF16, res=None, tm=None, tn=None, tk=None):
    M, K = a.shape
    n = w2d.shape[1]
    assert w2d.shape[0] == nb * K
    tm = tm or _pick(M, (1024, 512, 256, 128))
    tn = tn or _pick(n, (1408, 1024, 768, 512, 256, 128))
    tk = tk or (K if K <= 2048 else _pick(K, (1408, 1024, 512, 256, 128)))
    nk, nt = K // tk, n // tn
    has_res = res is not None

    def body(*refs):
        if has_res:
            a_ref, w_ref, r_ref, o_ref = refs[:4]
        else:
            a_ref, w_ref, o_ref = refs[:3]
            r_ref = None
        part = _dot(a_ref[...], w_ref[...])

        def finish(acc):
            if r_ref is not None:
                acc = acc + r_ref[...].astype(F32)
            o_ref[...] = acc.astype(o_ref.dtype)

        if nk == 1:
            finish(part)
        else:
            acc_ref = refs[-1]
            k = pl.program_id(3)

            @pl.when(k == 0)
            def _():
                acc_ref[...] = part

            @pl.when(k > 0)
            def _():
                acc_ref[...] += part

            @pl.when(k == nk - 1)
            def _():
                finish(acc_ref[...])

    in_specs = [
        pl.BlockSpec((tm, tk), lambda i, j, t, k: (i, k)),
        pl.BlockSpec((tk, tn), lambda i, j, t, k: (j * nk + k, t)),
    ]
    args = [a, w2d]
    out_spec = pl.BlockSpec((tm, tn), lambda i, j, t, k: (i, j * nt + t))
    if has_res:
        in_specs.append(out_spec)
        args.append(res)
    return pl.pallas_call(
        body,
        grid=(M // tm, nb, nt, nk),
        in_specs=in_specs,
        out_specs=out_spec,
        out_shape=jax.ShapeDtypeStruct((M, nb * n), out_dtype),
        scratch_shapes=[pltpu.VMEM((tm, tn), F32)] if nk > 1 else [],
        compiler_params=_cp(("parallel", "parallel", "parallel", "arbitrary")),
        name=name,
    )(*args)


def mm_nt(dy2d, w2d, nb, M, K, name, out_dtype=BF16, res=None, dy_maps=None, tm=None, tko=None, tn=None):
    n = w2d.shape[1]
    assert w2d.shape[0] == nb * K
    tm = tm or _pick(M, (1024, 512, 256, 128))
    tko = tko or _pick(K, (1024, 512, 256, 128))
    tn = tn or _pick(n, (1408, 1024, 768, 512, 256, 128))
    nt, nko = n // tn, K // tko
    has_res = res is not None
    if dy_maps is None:
        dy_maps = [lambda i, j, t: (i, j * nt + t)]
    nd = len(dy_maps)
    td = tn // nd

    one_step = nb * nt == 1

    def body(*refs):
        d_refs, w_ref = refs[:nd], refs[nd]
        r_ref = refs[nd + 1] if has_res else None
        d = d_refs[0][...] if nd == 1 else jnp.concatenate([r[...] for r in d_refs], axis=1)
        part = _dot_nt(d, w_ref[...])
        if one_step:
            o_ref = refs[-1]
            if r_ref is not None:
                part = part + r_ref[...].astype(F32)
            o_ref[...] = part.astype(o_ref.dtype)
            return
        o_ref, acc_ref = refs[-2], refs[-1]
        j, t = pl.program_id(2), pl.program_id(3)
        first = jnp.logical_and(j == 0, t == 0)
        last = jnp.logical_and(j == nb - 1, t == nt - 1)

        @pl.when(first)
        def _():
            acc_ref[...] = part

        @pl.when(jnp.logical_not(first))
        def _():
            acc_ref[...] += part

        @pl.when(last)
        def _():
            acc = acc_ref[...]
            if r_ref is not None:
                acc = acc + r_ref[...].astype(F32)
            o_ref[...] = acc.astype(o_ref.dtype)

    in_specs = [pl.BlockSpec((tm, td), functools.partial(lambda f, i, ko, j, t: f(i, j, t), f)) for f in dy_maps]
    in_specs.append(pl.BlockSpec((tko, tn), lambda i, ko, j, t: (j * nko + ko, t)))
    args = [dy2d] * nd + [w2d]
    out_spec = pl.BlockSpec((tm, tko), lambda i, ko, j, t: (i, ko))
    if has_res:
        in_specs.append(out_spec)
        args.append(res)
    return pl.pallas_call(
        body,
        grid=(M // tm, nko, nb, nt),
        in_specs=in_specs,
        out_specs=out_spec,
        out_shape=jax.ShapeDtypeStruct((M, K), out_dtype),
        scratch_shapes=[] if one_step else [pltpu.VMEM((tm, tko), F32)],
        compiler_params=_cp(("parallel", "parallel", "arbitrary", "arbitrary")),
        name=name,
    )(*args)


def mm_tn(x, dy2d, nb, n, name, out_dtype=BF16, dy_maps=None, tko=None, tn=None):
    S, K = x.shape
    tko = tko or _pick(K, (512, 256, 128))
    tn = tn or _pick(n, (1408, 1024, 768, 512, 256, 128))
    nt, nko = n // tn, K // tko
    if dy_maps is None:
        dy_maps = [lambda j, t: (0, j * nt + t)]
    nd = len(dy_maps)
    td = tn // nd

    def body(*refs):
        x_ref, d_refs, o_ref = refs[0], refs[1 : 1 + nd], refs[-1]
        d = d_refs[0][...] if nd == 1 else jnp.concatenate([r[...] for r in d_refs], axis=1)
        o_ref[...] = _dot_tn(x_ref[...], d).astype(o_ref.dtype)

    in_specs = [pl.BlockSpec((S, tko), lambda ko, j, t: (0, ko))]
    in_specs += [pl.BlockSpec((S, td), functools.partial(lambda f, ko, j, t: f(j, t), f)) for f in dy_maps]
    return pl.pallas_call(
        body,
        grid=(nko, nb, nt),
        in_specs=in_specs,
        out_specs=pl.BlockSpec((tko, tn), lambda ko, j, t: (j * nko + ko, t)),
        out_shape=jax.ShapeDtypeStruct((nb * K, n), out_dtype),
        compiler_params=_cp(("parallel", "parallel", "parallel")),
        name=name,
    )(x, *([dy2d] * nd))


def rms_fwd(x, g, name):
    S, D = x.shape
    tm = _pick(S, (256, 128))

    def body(x_ref, g_ref, o_ref):
        xf = x_ref[...]
        r = lax.rsqrt(jnp.mean(xf * xf, axis=-1, keepdims=True) + EPS)
        o_ref[...] = (xf * r * g_ref[...]).astype(o_ref.dtype)

    return pl.pallas_call(
        body,
        grid=(S // tm,),
        in_specs=[pl.BlockSpec((tm, D), lambda i: (i, 0)), pl.BlockSpec((1, D), lambda i: (0, 0))],
        out_specs=pl.BlockSpec((tm, D), lambda i: (i, 0)),
        out_shape=jax.ShapeDtypeStruct((S, D), BF16),
        compiler_params=_cp(("parallel",)),
        name=name,
    )(x, g.reshape(1, D))


def rms_bwd(x, g, dh, dres, name):
    S, D = x.shape
    tm = _pick(S, (256, 128))

    def body(x_ref, g_ref, dh_ref, dr_ref, dx_ref, dxb_ref, dg_ref):
        i = pl.program_id(0)
        xf = x_ref[...]
        dh = dh_ref[...].astype(F32)
        r = lax.rsqrt(jnp.mean(xf * xf, axis=-1, keepdims=True) + EPS)
        gy = dh * g_ref[...]
        proj = jnp.mean(gy * xf, axis=-1, keepdims=True)
        dx = dr_ref[...] + r * gy - xf * (r * r * r * proj)
        dx_ref[...] = dx
        dxb_ref[...] = dx.astype(BF16)
        dg = jnp.sum(dh * (xf * r), axis=0, keepdims=True)

        @pl.when(i == 0)
        def _():
            dg_ref[...] = dg

        @pl.when(i > 0)
        def _():
            dg_ref[...] += dg

    row = pl.BlockSpec((tm, D), lambda i: (i, 0))
    vec = pl.BlockSpec((1, D), lambda i: (0, 0))
    return pl.pallas_call(
        body,
        grid=(S // tm,),
        in_specs=[row, vec, row, row],
        out_specs=[row, row, vec],
        out_shape=[jax.ShapeDtypeStruct((S, D), F32), jax.ShapeDtypeStruct((S, D), BF16), jax.ShapeDtypeStruct((1, D), F32)],
        compiler_params=_cp(("arbitrary",)),
        name=name,
    )(x, g.reshape(1, D), dh, dres)


def loss_head(x, g, target, name):
    S, D = x.shape
    tm = _pick(S, (256, 128))

    def body(x_ref, g_ref, t_ref, dx_ref, dxb_ref, dg_ref, loss_ref):
        i = pl.program_id(0)
        xf = x_ref[...]
        gg = g_ref[...]
        r = lax.rsqrt(jnp.mean(xf * xf, axis=-1, keepdims=True) + EPS)
        xh = xf * r
        err = xh * gg - t_ref[...]
        part = (0.5 / D) * jnp.sum(err * err)
        dy = err * (1.0 / D)
        gy = dy * gg
        proj = jnp.mean(gy * xf, axis=-1, keepdims=True)
        dx = r * gy - xf * (r * r * r * proj)
        dx_ref[...] = dx
        dxb_ref[...] = dx.astype(BF16)
        dg = jnp.sum(dy * xh, axis=0, keepdims=True)
        lossb = jnp.full(loss_ref.shape, part, F32)

        @pl.when(i == 0)
        def _():
            dg_ref[...] = dg
            loss_ref[...] = lossb

        @pl.when(i > 0)
        def _():
            dg_ref[...] += dg
            loss_ref[...] += lossb

    row = pl.BlockSpec((tm, D), lambda i: (i, 0))
    vec = pl.BlockSpec((1, D), lambda i: (0, 0))
    return pl.pallas_call(
        body,
        grid=(S // tm,),
        in_specs=[row, vec, row],
        out_specs=[row, row, vec, pl.BlockSpec((8, 128), lambda i: (0, 0))],
        out_shape=[
            jax.ShapeDtypeStruct((S, D), F32),
            jax.ShapeDtypeStruct((S, D), BF16),
            jax.ShapeDtypeStruct((1, D), F32),
            jax.ShapeDtypeStruct((8, 128), F32),
        ],
        compiler_params=_cp(("arbitrary",)),
        name=name,
    )(x, g.reshape(1, D), target)


def _shift_down(s, k):
    if k == 0:
        return s
    return jnp.where(_iota2(s.shape, 0) >= k, pltpu.roll(s, k, axis=0), 0.0)


def _shift_up(s, k):
    if k == 0:
        return s
    n = s.shape[0]
    return jnp.where(_iota2(s.shape, 0) < n - k, pltpu.roll(s, n - k, axis=0), 0.0)


def _conv(s, w):
    return w[0:1] * _shift_down(s, 2) + w[1:2] * _shift_down(s, 1) + w[2:3] * s


def _conv_t(d, w):
    return w[2:3] * d + w[1:2] * _shift_up(d, 1) + w[0:1] * _shift_up(d, 2)


def _conv_dw(d, s):
    return [jnp.sum(d * _shift_down(s, CONV_K - 1 - k), axis=0, keepdims=True) for k in range(CONV_K)]


def sc_fwd(p, convw, cat, W, name):
    S = p.shape[0]
    tc = _pick(W, (256, 128))
    nc = W // tc

    def body(gb_ref, gc_ref, hi_ref, w_ref, cat_ref, o_ref):
        s = gc_ref[...].astype(F32) * hi_ref[...].astype(F32)
        o_ref[...] = (gb_ref[...].astype(F32) * _conv(s, w_ref[...])).astype(o_ref.dtype)

    col = lambda part: pl.BlockSpec((S, tc), lambda c: (0, part * nc + c))
    return pl.pallas_call(
        body,
        grid=(nc,),
        in_specs=[col(3), col(4), col(5), pl.BlockSpec((CONV_K, tc), lambda c: (0, c)), pl.BlockSpec(memory_space=pl.ANY)],
        out_specs=col(1),
        out_shape=jax.ShapeDtypeStruct(cat.shape, cat.dtype),
        input_output_aliases={4: 0},
        compiler_params=_cp(("parallel",)),
        name=name,
    )(p, p, p, convw, cat)


def sc_bwd(p, convw, dcat, dp, W, name):
    S = p.shape[0]
    tc = _pick(W, (256, 128))
    nc = W // tc

    def body(gb_ref, gc_ref, hi_ref, w_ref, do_ref, dp_in_ref, dp_ref, dw_ref):
        gb = gb_ref[...].astype(F32)
        gc = gc_ref[...].astype(F32)
        hi = hi_ref[...].astype(F32)
        w = w_ref[...]
        do = do_ref[...].astype(F32)
        s = gc * hi
        dcs = do * gb
        ds = _conv_t(dcs, w)
        dp_ref[0] = (do * _conv(s, w)).astype(dp_ref.dtype)
        dp_ref[1] = (ds * hi).astype(dp_ref.dtype)
        dp_ref[2] = (ds * gc).astype(dp_ref.dtype)
        for k, row in enumerate(_conv_dw(dcs, s)):
            dw_ref[k : k + 1, :] = row

    col = lambda part: pl.BlockSpec((S, tc), lambda c: (0, part * nc + c))
    return pl.pallas_call(
        body,
        grid=(nc,),
        in_specs=[
            col(3), col(4), col(5),
            pl.BlockSpec((CONV_K, tc), lambda c: (0, c)),
            pl.BlockSpec((S, tc), lambda c: (0, nc + c)),
            pl.BlockSpec(memory_space=pl.ANY),
        ],
        out_specs=[pl.BlockSpec((3, S, tc), lambda c: (1, 0, c)), pl.BlockSpec((CONV_K, tc), lambda c: (0, c))],
        out_shape=[jax.ShapeDtypeStruct(dp.shape, dp.dtype), jax.ShapeDtypeStruct((CONV_K, W), F32)],
        input_output_aliases={5: 0},
        compiler_params=_cp(("parallel",)),
        name=name,
    )(p, p, p, convw, dcat, dp)


def _silu_parts(a):
    sig = 1.0 / (1.0 + jnp.exp(-a))
    return a * sig, sig


def ffn_act_fwd(u, convw, F, name):
    S = u.shape[0]
    tc = _pick(F, (256, 128))
    nc = F // tc

    def body(ug_ref, uu_ref, wg_ref, wu_ref, o_ref):
        ag = _conv(ug_ref[...].astype(F32), wg_ref[...])
        au = _conv(uu_ref[...].astype(F32), wu_ref[...])
        o_ref[...] = (_silu_parts(ag)[0] * au).astype(o_ref.dtype)

    col = lambda half: pl.BlockSpec((S, tc), lambda c: (0, half * nc + c))
    wcol = lambda half: pl.BlockSpec((CONV_K, tc), lambda c: (0, half * nc + c))
    return pl.pallas_call(
        body,
        grid=(nc,),
        in_specs=[col(0), col(1), wcol(0), wcol(1)],
        out_specs=pl.BlockSpec((S, tc), lambda c: (0, c)),
        out_shape=jax.ShapeDtypeStruct((S, F), BF16),
        compiler_params=_cp(("parallel",)),
        name=name,
    )(u, u, convw, convw)


def ffn_act_bwd(u, convw, dact, F, name):
    S = u.shape[0]
    tc = _pick(F, (256, 128))
    nc = F // tc

    def body(ug_ref, uu_ref, wg_ref, wu_ref, da_ref, du_ref, dw_ref):
        ug = ug_ref[...].astype(F32)
        uu = uu_ref[...].astype(F32)
        wg = wg_ref[...]
        wu = wu_ref[...]
        da = da_ref[...].astype(F32)
        ag = _conv(ug, wg)
        au = _conv(uu, wu)
        sl, sig = _silu_parts(ag)
        dag = da * au * (sig * (1.0 + ag * (1.0 - sig)))
        dau = da * sl
        du_ref[0] = _conv_t(dag, wg).astype(du_ref.dtype)
        du_ref[1] = _conv_t(dau, wu).astype(du_ref.dtype)
        for k, (rg, ru) in enumerate(zip(_conv_dw(dag, ug), _conv_dw(dau, uu))):
            dw_ref[0, k : k + 1, :] = rg
            dw_ref[1, k : k + 1, :] = ru

    col = lambda half: pl.BlockSpec((S, tc), lambda c: (0, half * nc + c))
    wcol = lambda half: pl.BlockSpec((CONV_K, tc), lambda c: (0, half * nc + c))
    return pl.pallas_call(
        body,
        grid=(nc,),
        in_specs=[col(0), col(1), wcol(0), wcol(1), pl.BlockSpec((S, tc), lambda c: (0, c))],
        out_specs=[pl.BlockSpec((2, S, tc), lambda c: (0, 0, c)), pl.BlockSpec((2, CONV_K, tc), lambda c: (0, 0, c))],
        out_shape=[jax.ShapeDtypeStruct((2, S, F), BF16), jax.ShapeDtypeStruct((2, CONV_K, F), F32)],
        compiler_params=_cp(("parallel",)),
        name=name,
    )(u, u, convw, convw, dact)


def _softplus(z):
    return jnp.maximum(z, 0.0) + jnp.log(1.0 + jnp.exp(-jnp.abs(z)))


def _key_strip(S):
    return _pick(S, (512, 256, 128))


def _query_rows(S):
    return _pick(S, (256, 128))


def _split2(x):
    hi = x.astype(BF16)
    return hi, (x - hi.astype(F32)).astype(BF16)


def _block_sums(x, ones_bf16):
    hi, lo = _split2(x)
    return [
        _dot(hi[:, b * HD : (b + 1) * HD], ones_bf16) + _dot(lo[:, b * HD : (b + 1) * HD], ones_bf16)
        for b in range(x.shape[1] // HD)
    ]


def _strip_mask(shape, row0, off, strict):
    cols, rows = _iota2(shape, 1) + off, _iota2(shape, 0) + row0
    return cols < rows if strict else cols <= rows


def _sb_strip(q, ks, row0, off, run, su):
    z = _dot_nt(q, ks) * (HD ** -0.5)
    mask = _strip_mask(z.shape, row0, off, True)
    sp = _softplus(z)
    l = jnp.where(mask, -sp, 0.0)
    within = _block_sums(l, su)
    later = [None] * len(within)
    for b in reversed(range(len(within))):
        later[b] = within[b] + run
        run = run + jnp.sum(l[:, b * HD : (b + 1) * HD], axis=1, keepdims=True)
    a = jnp.where(mask, jnp.exp(z - sp + jnp.concatenate(later, axis=1)), 0.0)
    return z, mask, a, run


def sb_fwd(p, W, name):
    S = p.shape[0]
    TQ, TK = _query_rows(S), _key_strip(S)
    nh, nq = W // HD, S // TQ

    def body(q_ref, k_ref, v_ref, o_ref):
        i = pl.program_id(1)
        q = q_ref[...]
        su = (_iota2((HD, HD), 0) > _iota2((HD, HD), 1)).astype(BF16)
        last = (i * TQ + TQ - 1) // TK

        def step(gg, carry):
            acc, run = carry
            off = pl.multiple_of((last - gg) * TK, TK)
            _, _, a, run = _sb_strip(q, k_ref[pl.ds(off, TK), :], i * TQ, off, run, su)
            return acc + _dot(a.astype(BF16), v_ref[pl.ds(off, TK), :]), run

        acc, _ = lax.fori_loop(0, last + 1, step, (jnp.zeros((TQ, HD), F32), jnp.zeros((TQ, 1), F32)))
        o_ref[...] = acc.astype(o_ref.dtype)

    return pl.pallas_call(
        body,
        grid=(nh, nq),
        in_specs=[
            pl.BlockSpec((TQ, HD), lambda h, i: (i, h)),
            pl.BlockSpec((S, HD), lambda h, i: (0, nh + h)),
            pl.BlockSpec((S, HD), lambda h, i: (0, 2 * nh + h)),
        ],
        out_specs=pl.BlockSpec((TQ, HD), lambda h, i: (i, h)),
        out_shape=jax.ShapeDtypeStruct((S, 2 * W), BF16),
        compiler_params=_cp(("parallel", "arbitrary")),
        name=name,
    )(p, p, p)


def sb_bwd(p, dcat, W, name):
    S = p.shape[0]
    TQ, TK = _query_rows(S), _key_strip(S)
    nh, nq = W // HD, S // TQ
    scale = HD ** -0.5

    def body(q_ref, k_ref, v_ref, do_ref, dp_ref, dk_acc, dv_acc, e_scr, z_scr):
        i = pl.program_id(1)
        q = q_ref[...]
        do = do_ref[...]
        su = (_iota2((HD, HD), 0) > _iota2((HD, HD), 1)).astype(BF16)
        sl = (_iota2((HD, HD), 0) < _iota2((HD, HD), 1)).astype(BF16)
        last = (i * TQ + TQ - 1) // TK

        @pl.when(i == 0)
        def _():
            dk_acc[...] = jnp.zeros_like(dk_acc)
            dv_acc[...] = jnp.zeros_like(dv_acc)

        def pass_a(gg, run):
            g = last - gg
            off = pl.multiple_of(g * TK, TK)
            z, _, a, run = _sb_strip(q, k_ref[pl.ds(off, TK), :], i * TQ, off, run, su)
            e_scr[g] = a * _dot_nt(do, v_ref[pl.ds(off, TK), :])
            z_scr[g] = z
            dv_acc[pl.ds(off, TK), :] += _dot_tn(a.astype(BF16), do)
            return run

        lax.fori_loop(0, last + 1, pass_a, jnp.zeros((TQ, 1), F32))

        def pass_b(g, carry):
            dq, run_e = carry
            off = pl.multiple_of(g * TK, TK)
            e = e_scr[g]
            z = z_scr[g]
            mask = _strip_mask(z.shape, i * TQ, off, True)
            within = _block_sums(e, sl)
            before = []
            for b in range(len(within)):
                before.append(within[b] + run_e)
                run_e = run_e + jnp.sum(e[:, b * HD : (b + 1) * HD], axis=1, keepdims=True)
            sig = 1.0 / (1.0 + jnp.exp(-z))
            dz = jnp.where(mask, e * (1.0 - sig) - jnp.concatenate(before, axis=1) * sig, 0.0)
            dz = (dz * scale).astype(BF16)
            dq = dq + _dot(dz, k_ref[pl.ds(off, TK), :])
            dk_acc[pl.ds(off, TK), :] += _dot_tn(dz, q)
            return dq, run_e

        dq, _ = lax.fori_loop(0, last + 1, pass_b, (jnp.zeros((TQ, HD), F32), jnp.zeros((TQ, 1), F32)))
        dp_ref[0, pl.ds(pl.multiple_of(i * TQ, TQ), TQ), :] = dq.astype(dp_ref.dtype)

        @pl.when(i == nq - 1)
        def _():
            dp_ref[1] = dk_acc[...].astype(dp_ref.dtype)
            dp_ref[2] = dv_acc[...].astype(dp_ref.dtype)

    return pl.pallas_call(
        body,
        grid=(nh, nq),
        in_specs=[
            pl.BlockSpec((TQ, HD), lambda h, i: (i, h)),
            pl.BlockSpec((S, HD), lambda h, i: (0, nh + h)),
            pl.BlockSpec((S, HD), lambda h, i: (0, 2 * nh + h)),
            pl.BlockSpec((TQ, HD), lambda h, i: (i, h)),
        ],
        out_specs=pl.BlockSpec((3, S, HD), lambda h, i: (0, 0, h)),
        out_shape=jax.ShapeDtypeStruct((6, S, W), BF16),
        scratch_shapes=[
            pltpu.VMEM((S, HD), F32),
            pltpu.VMEM((S, HD), F32),
            pltpu.VMEM((S // TK, TQ, TK), F32),
            pltpu.VMEM((S // TK, TQ, TK), F32),
        ],
        compiler_params=_cp(("parallel", "arbitrary")),
        name=name,
    )(p, p, p, dcat)


def fox_gate_fwd(f, b, name):
    S = f.shape[0]
    nq = S // HD

    def body(f_ref, b_ref, c_ref, run):
        i = pl.program_id(0)

        @pl.when(i == 0)
        def _():
            run[...] = jnp.zeros_like(run)

        lf = -_softplus(-(f_ref[...] + b_ref[...]))
        tri = (_iota2((HD, HD), 0) >= _iota2((HD, HD), 1)).astype(BF16)
        c_ref[...] = _dot_ones_left(tri, lf) + run[...]
        run[...] += jnp.sum(lf, axis=0, keepdims=True)

    return pl.pallas_call(
        body,
        grid=(nq,),
        in_specs=[pl.BlockSpec((HD, 128), lambda i: (i, 0)), pl.BlockSpec((1, 128), lambda i: (0, 0))],
        out_specs=pl.BlockSpec((HD, 128), lambda i: (i, 0)),
        out_shape=jax.ShapeDtypeStruct((S, 128), F32),
        scratch_shapes=[pltpu.VMEM((1, 128), F32)],
        compiler_params=_cp(("arbitrary",)),
        name=name,
    )(f, b)


def fox_gate_bwd(f, b, dc, name):
    S = f.shape[0]
    nq = S // HD

    def body(f_ref, b_ref, dc_ref, df_ref, db_ref, run):
        i = pl.program_id(0)

        @pl.when(i == 0)
        def _():
            run[...] = jnp.zeros_like(run)

        dc = dc_ref[...]
        tri = (_iota2((HD, HD), 0) <= _iota2((HD, HD), 1)).astype(BF16)
        dlf = _dot_ones_left(tri, dc) + run[...]
        run[...] += jnp.sum(dc, axis=0, keepdims=True)
        x = f_ref[...] + b_ref[...]
        df = dlf * (1.0 / (1.0 + jnp.exp(x)))
        df_ref[...] = df
        db = jnp.sum(df, axis=0, keepdims=True)

        @pl.when(i == 0)
        def _():
            db_ref[...] = db

        @pl.when(i > 0)
        def _():
            db_ref[...] += db

    rev = pl.BlockSpec((HD, 128), lambda i: (nq - 1 - i, 0))
    vec = pl.BlockSpec((1, 128), lambda i: (0, 0))
    return pl.pallas_call(
        body,
        grid=(nq,),
        in_specs=[rev, vec, rev],
        out_specs=[rev, vec],
        out_shape=[jax.ShapeDtypeStruct((S, 128), F32), jax.ShapeDtypeStruct((1, 128), F32)],
        scratch_shapes=[pltpu.VMEM((1, 128), F32)],
        compiler_params=_cp(("arbitrary",)),
        name=name,
    )(f, b, dc)


def _fox_logits(q, ks, ct, cs, row0, off):
    s = _dot_nt(q, ks) * (HD ** -0.5) + (ct - cs)
    mask = _strip_mask(s.shape, row0, off, False)
    return jnp.where(mask, s, -1e30), mask


def fox_fwd(p, ccol, crow, cat, W, name):
    S = p.shape[0]
    TQ, TK = _query_rows(S), _key_strip(S)
    nh, nq = W // HD, S // TQ

    def body(q_ref, k_ref, v_ref, cc_ref, cr_ref, cat_ref, o_ref, lse_ref):
        i = pl.program_id(1)
        q = q_ref[...]
        ct = cc_ref[0]

        def step(g, carry):
            m, l, acc = carry
            off = pl.multiple_of(g * TK, TK)
            s, _ = _fox_logits(q, k_ref[pl.ds(off, TK), :], ct, cr_ref[0, pl.ds(g, 1), :], i * TQ, off)
            m_new = jnp.maximum(m, jnp.max(s, axis=1, keepdims=True))
            alpha = jnp.exp(m - m_new)
            pr = jnp.exp(s - m_new)
            l = alpha * l + jnp.sum(pr, axis=1, keepdims=True)
            acc = alpha * acc + _dot(pr.astype(BF16), v_ref[pl.ds(off, TK), :])
            return m_new, l, acc

        init = (jnp.full((TQ, 1), -1e30, F32), jnp.zeros((TQ, 1), F32), jnp.zeros((TQ, HD), F32))
        m, l, acc = lax.fori_loop(0, (i * TQ + TQ - 1) // TK + 1, step, init)
        o_ref[...] = (acc / l).astype(o_ref.dtype)
        lse_ref[0] = m + jnp.log(l)

    return pl.pallas_call(
        body,
        grid=(nh, nq),
        in_specs=[
            pl.BlockSpec((TQ, HD), lambda h, i: (i, 2 * nh + h)),
            pl.BlockSpec((S, HD), lambda h, i: (0, 3 * nh + h)),
            pl.BlockSpec((S, HD), lambda h, i: (0, 4 * nh + h)),
            pl.BlockSpec((1, TQ, 1), lambda h, i: (h, i, 0)),
            pl.BlockSpec((1, S // TK, TK), lambda h, i: (h, 0, 0)),
            pl.BlockSpec(memory_space=pl.ANY),
        ],
        out_specs=[pl.BlockSpec((TQ, HD), lambda h, i: (i, nh + h)), pl.BlockSpec((1, TQ, 1), lambda h, i: (h, i, 0))],
        out_shape=[jax.ShapeDtypeStruct(cat.shape, cat.dtype), jax.ShapeDtypeStruct((nh, S, 1), F32)],
        input_output_aliases={5: 0},
        compiler_params=_cp(("parallel", "arbitrary")),
        name=name,
    )(p, p, p, ccol, crow, cat)


def fox_bwd(p, ccol, crow, cat, lse, dcat, dp, W, name):
    S = p.shape[0]
    TQ, TK = _query_rows(S), _key_strip(S)
    nh, nq = W // HD, S // TQ
    scale = HD ** -0.5

    def body(q_ref, k_ref, v_ref, cc_ref, cr_ref, o_ref, lse_ref, do_ref, dp_in_ref, dp_ref, dcs_ref, dct_ref, dk_acc, dv_acc):
        i = pl.program_id(1)
        q = q_ref[...]
        do = do_ref[...]
        ct = cc_ref[0]
        lse_i = lse_ref[0]
        delta = jnp.sum(do.astype(F32) * o_ref[...].astype(F32), axis=1, keepdims=True)

        @pl.when(i == 0)
        def _():
            dk_acc[...] = jnp.zeros_like(dk_acc)
            dv_acc[...] = jnp.zeros_like(dv_acc)
            dcs_ref[...] = jnp.zeros_like(dcs_ref)

        def step(g, carry):
            dq, dct = carry
            off = pl.multiple_of(g * TK, TK)
            ks = k_ref[pl.ds(off, TK), :]
            s, mask = _fox_logits(q, ks, ct, cr_ref[0, pl.ds(g, 1), :], i * TQ, off)
            pr = jnp.where(mask, jnp.exp(s - lse_i), 0.0)
            ds = pr * (_dot_nt(do, v_ref[pl.ds(off, TK), :]) - delta)
            dv_acc[pl.ds(off, TK), :] += _dot_tn(pr.astype(BF16), do)
            dsb = (ds * scale).astype(BF16)
            dk_acc[pl.ds(off, TK), :] += _dot_tn(dsb, q)
            dcs_ref[0, pl.ds(g, 1), :] += jnp.sum(ds, axis=0, keepdims=True)
            return dq + _dot(dsb, ks), dct + jnp.sum(ds, axis=1, keepdims=True)

        dq, dct = lax.fori_loop(0, (i * TQ + TQ - 1) // TK + 1, step, (jnp.zeros((TQ, HD), F32), jnp.zeros((TQ, 1), F32)))
        dp_ref[0, pl.ds(pl.multiple_of(i * TQ, TQ), TQ), :] = dq.astype(dp_ref.dtype)
        dct_ref[0] = dct

        @pl.when(i == nq - 1)
        def _():
            dp_ref[1] = dk_acc[...].astype(dp_ref.dtype)
            dp_ref[2] = dv_acc[...].astype(dp_ref.dtype)

    return pl.pallas_call(
        body,
        grid=(nh, nq),
        in_specs=[
            pl.BlockSpec((TQ, HD), lambda h, i: (i, 2 * nh + h)),
            pl.BlockSpec((S, HD), lambda h, i: (0, 3 * nh + h)),
            pl.BlockSpec((S, HD), lambda h, i: (0, 4 * nh + h)),
            pl.BlockSpec((1, TQ, 1), lambda h, i: (h, i, 0)),
            pl.BlockSpec((1, S // TK, TK), lambda h, i: (h, 0, 0)),
            pl.BlockSpec((TQ, HD), lambda h, i: (i, nh + h)),
            pl.BlockSpec((1, TQ, 1), lambda h, i: (h, i, 0)),
            pl.BlockSpec((TQ, HD), lambda h, i: (i, nh + h)),
            pl.BlockSpec(memory_space=pl.ANY),
        ],
        out_specs=[
            pl.BlockSpec((3, S, HD), lambda h, i: (1, 0, h)),
            pl.BlockSpec((1, S // TK, TK), lambda h, i: (h, 0, 0)),
            pl.BlockSpec((1, TQ, 1), lambda h, i: (h, i, 0)),
        ],
        out_shape=[
            jax.ShapeDtypeStruct(dp.shape, dp.dtype),
            jax.ShapeDtypeStruct((nh, S // TK, TK), F32),
            jax.ShapeDtypeStruct((nh, S, 1), F32),
        ],
        input_output_aliases={8: 0},
        scratch_shapes=[pltpu.VMEM((S, HD), F32), pltpu.VMEM((S, HD), F32)],
        compiler_params=_cp(("parallel", "arbitrary")),
        name=name,
    )(p, p, p, ccol, crow, cat, lse, dcat, dp)


_GELU_K = math.sqrt(2.0 / math.pi)
_GELU_C = 0.044715


def _gelu(x):
    return 0.5 * x * (1.0 + jnp.tanh(_GELU_K * (x + _GELU_C * x * x * x)))


def _gelu_grad(x):
    t = jnp.tanh(_GELU_K * (x + _GELU_C * x * x * x))
    return 0.5 * (1.0 + t) + 0.5 * x * (1.0 - t * t) * (_GELU_K * (1.0 + 3.0 * _GELU_C * x * x))


def _layernorm_parts(gv):
    xc = gv - jnp.mean(gv, axis=-1, keepdims=True)
    r = lax.rsqrt(jnp.mean(xc * xc, axis=-1, keepdims=True) + EPS)
    return xc * r, r


def sg_fwd(p, sg_w, sg_bt, sg_g, W, name):
    S = p.shape[0]
    G, nq = W // HD, S // HD

    def body(u_ref, v_ref, w_ref, bt_ref, g_ref, o_ref):
        xh, _ = _layernorm_parts(_gelu(v_ref[...].astype(F32)))
        vn = (xh * g_ref[...]).astype(BF16)
        tri = _iota2((HD, HD), 0) >= _iota2((HD, HD), 1)
        for gi in range(G):
            cols = slice(gi * HD, (gi + 1) * HD)
            wt = jnp.where(tri, w_ref[gi], 0.0).astype(BF16)
            mixed = _dot(wt, vn[:, cols]) + bt_ref[:, gi : gi + 1]
            o_ref[:, cols] = (_gelu(u_ref[:, cols].astype(F32)) * mixed).astype(o_ref.dtype)

    return pl.pallas_call(
        body,
        grid=(nq,),
        in_specs=[
            pl.BlockSpec((HD, W), lambda i: (i, 0)),
            pl.BlockSpec((HD, W), lambda i: (i, 1)),
            pl.BlockSpec((G, HD, HD), lambda i: (0, 0, 0)),
            pl.BlockSpec((HD, G), lambda i: (0, 0)),
            pl.BlockSpec((1, W), lambda i: (0, 0)),
        ],
        out_specs=pl.BlockSpec((HD, W), lambda i: (i, 0)),
        out_shape=jax.ShapeDtypeStruct((S, 2 * W), BF16),
        compiler_params=_cp(("parallel",)),
        name=name,
    )(p, p, sg_w, sg_bt, sg_g.reshape(1, W))


def sg_bwd(p, sg_w, sg_bt, sg_g, dcat, W, name):
    S = p.shape[0]
    G, nq = W // HD, S // HD

    def body(u_ref, v_ref, w_ref, bt_ref, g_ref, do_ref, dp_ref, dw_ref, dbt_ref, dg_ref, dvn_scr):
        i = pl.program_id(0)

        @pl.when(i == 0)
        def _():
            dw_ref[...] = jnp.zeros_like(dw_ref)
            dbt_ref[...] = jnp.zeros_like(dbt_ref)
            dg_ref[...] = jnp.zeros_like(dg_ref)

        v = v_ref[...].astype(F32)
        xh, r = _layernorm_parts(_gelu(v))
        gg = g_ref[...]
        vn = (xh * gg).astype(BF16)
        tri = _iota2((HD, HD), 0) >= _iota2((HD, HD), 1)
        for gi in range(G):
            cols = slice(gi * HD, (gi + 1) * HD)
            wt = jnp.where(tri, w_ref[gi], 0.0).astype(BF16)
            mixed = _dot(wt, vn[:, cols]) + bt_ref[:, gi : gi + 1]
            u = u_ref[:, cols].astype(F32)
            do = do_ref[:, cols].astype(F32)
            dp_ref[0, :, cols] = (do * mixed * _gelu_grad(u)).astype(dp_ref.dtype)
            dmix = do * _gelu(u)
            dmb = dmix.astype(BF16)
            dw_ref[gi] += jnp.where(tri, _dot_nt(dmb, vn[:, cols]), 0.0)
            dbt_ref[:, gi : gi + 1] += jnp.sum(dmix, axis=1, keepdims=True)
            dvn_scr[:, cols] = _dot_tn(wt, dmb)
        dvn = dvn_scr[...]
        dg_ref[...] += jnp.sum(dvn * xh, axis=0, keepdims=True)
        dxh = dvn * gg
        dgv = r * (dxh - jnp.mean(dxh, axis=-1, keepdims=True) - xh * jnp.mean(dxh * xh, axis=-1, keepdims=True))
        dp_ref[1] = (dgv * _gelu_grad(v)).astype(dp_ref.dtype)

    return pl.pallas_call(
        body,
        grid=(nq,),
        in_specs=[
            pl.BlockSpec((HD, W), lambda i: (i, 0)),
            pl.BlockSpec((HD, W), lambda i: (i, 1)),
            pl.BlockSpec((G, HD, HD), lambda i: (0, 0, 0)),
            pl.BlockSpec((HD, G), lambda i: (0, 0)),
            pl.BlockSpec((1, W), lambda i: (0, 0)),
            pl.BlockSpec((HD, W), lambda i: (i, 0)),
        ],
        out_specs=[
            pl.BlockSpec((2, HD, W), lambda i: (0, i, 0)),
            pl.BlockSpec((G, HD, HD), lambda i: (0, 0, 0)),
            pl.BlockSpec((HD, G), lambda i: (0, 0)),
            pl.BlockSpec((1, W), lambda i: (0, 0)),
        ],
        out_shape=[
            jax.ShapeDtypeStruct((6, S, W), BF16),
            jax.ShapeDtypeStruct((G, HD, HD), F32),
            jax.ShapeDtypeStruct((HD, G), F32),
            jax.ShapeDtypeStruct((1, W), F32),
        ],
        scratch_shapes=[pltpu.VMEM((HD, W), F32)],
        compiler_params=_cp(("arbitrary",)),
        name=name,
    )(p, p, sg_w, sg_bt, sg_g.reshape(1, W), dcat)


def local_step(x, target, wts, at, on_grad):
    S, D = x.shape
    W = D // 2
    nb, F = wts["nb"], wts["F"]
    g = {}

    def ffn_fwd(xin, l):
        h = rms_fwd(xin, wts[f"{l}_ffn_norm_g"], f"{l}_ffn_rms")
        u = mm_nn(h, wts[f"{l}_ffn_up"], nb, f"{l}_ffn_up_mm")
        act = ffn_act_fwd(u, wts[f"{l}_ffn_conv_w"], F, f"{l}_ffn_act")
        xout = mm_nn(act, wts[f"{l}_ffn_down"], 1, f"{l}_ffn_down_mm", out_dtype=F32, res=xin)
        return xout, (xin, h, u, act)

    def ffn_bwd(dxout, dxoutb, saved, l):
        xin, h, u, act = saved
        dact = mm_nt(dxoutb, wts[f"{l}_ffn_down"], 1, S, F, f"{l}_ffn_down_dx", tko=_pick(F, (512, 256, 128)), tn=D)
        dact = on_grad(f"{l}_ffn_down", mm_tn(act, dxoutb, 1, D, f"{l}_ffn_down_dw"), dact)
        du, dcw = ffn_act_bwd(u, wts[f"{l}_ffn_conv_w"], dact, F, f"{l}_ffn_act_bwd")
        g[f"{l}_ffn_conv_w"] = jnp.concatenate([dcw[0], dcw[1]], axis=1)
        du2 = du.reshape(2 * S, F)
        n = wts[f"{l}_ffn_up"].shape[1]
        tn = _pick(n, (1408, 1024, 768, 512, 256, 128))
        per_half = F // tn
        nt = n // tn

        def up_block(i, j, t):
            vb = j * nt + t
            return vb // per_half, vb % per_half

        tm = _pick(S, (1024, 512, 256, 128))

        def nt_map(i, j, t):
            half, cb = up_block(i, j, t)
            return (half * (S // tm) + i, cb)

        def tn_map(j, t):
            half, cb = up_block(0, j, t)
            return (half, cb)

        dh = mm_nt(du2, wts[f"{l}_ffn_up"], nb, S, D, f"{l}_ffn_up_dx", dy_maps=[nt_map], tm=tm, tko=D, tn=tn)
        dh = on_grad(f"{l}_ffn_up", mm_tn(h, du2, nb, n, f"{l}_ffn_up_dw", dy_maps=[tn_map], tn=tn), dh)
        dxin, dxinb, dg = rms_bwd(xin, wts[f"{l}_ffn_norm_g"], dh, dxout, f"{l}_ffn_rms_bwd")
        g[f"{l}_ffn_norm_g"] = dg
        return dxin, dxinb

    h0 = rms_fwd(x, wts["l0_mix_norm_g"], "l0_mix_rms")
    p0 = mm_nn(h0, wts["l0_w_in"], nb, "l0_w_in_mm")
    cat0 = sb_fwd(p0, W, "l0_sb_fwd")
    cat0 = sc_fwd(p0, wts["l0_sc_conv_w"], cat0, W, "l0_sc_fwd")
    x1 = mm_nn(cat0, wts["l0_w_out"], 1, "l0_w_out_mm", out_dtype=F32, res=x, tm=S, tn=_pick(D, (512, 256, 128)))
    x2, ffn0_saved = ffn_fwd(x1, "l0")

    x2 = at("l1_w_in", x2, None)
    nh = W // HD
    h2 = rms_fwd(x2, wts["l1_mix_norm_g"], "l1_mix_rms")
    p1 = mm_nn(h2, wts["l1_w_in_main"], 1, "l1_w_in_mm")
    f = mm_nn(h2, wts["l1_w_in_f"], 1, "l1_w_f_mm", out_dtype=F32)
    bf = jnp.zeros((1, 128), F32).at[0, :nh].set(wts["l1_fox_b_f"])
    c = fox_gate_fwd(f, bf, "l1_fox_gate")
    c_heads = c[:, :nh].T
    ccol = c_heads[:, :, None]
    crow = c_heads.reshape(nh, S // _key_strip(S), _key_strip(S))
    sg_bt = wts["l1_sg_b"].T
    cat1 = sg_fwd(p1, wts["l1_sg_w"], sg_bt, wts["l1_sg_norm_g"], W, "l1_sg_fwd")
    cat1, lse = fox_fwd(p1, ccol, crow, cat1, W, "l1_fox_fwd")
    x3 = mm_nn(cat1, wts["l1_w_out"], 1, "l1_w_out_mm", out_dtype=F32, res=x2, tm=S, tn=_pick(D, (512, 256, 128)))
    x4, ffn1_saved = ffn_fwd(x3, "l1")

    dx4, dx4b, dgf, loss = loss_head(x4, wts["final_norm_g"], target, "loss_head")
    dx4b = at("loss", dx4b, loss)
    g["final_norm_g"] = dgf

    dx3, dx3b = ffn_bwd(dx4, dx4b, ffn1_saved, "l1")
    dcat1 = mm_nt(dx3b, wts["l1_w_out"], 1, S, D, "l1_w_out_dx", tn=D)
    dcat1 = on_grad("l1_w_out", mm_tn(cat1, dx3b, 1, D, "l1_w_out_dw"), dcat1)
    dp1, dsgw, dsgbt, dsgg = sg_bwd(p1, wts["l1_sg_w"], sg_bt, wts["l1_sg_norm_g"], dcat1, W, "l1_sg_bwd")
    dp1, dcs, dct = fox_bwd(p1, ccol, crow, cat1, lse, dcat1, dp1, W, "l1_fox_bwd")
    g["l1_sg_w"], g["l1_sg_b"], g["l1_sg_norm_g"] = dsgw, dsgbt.T, dsgg
    dc = jnp.zeros((S, 128), F32).at[:, :nh].set((dct[:, :, 0] - dcs.reshape(nh, S)).T)
    df, dbf = fox_gate_bwd(f, bf, dc, "l1_fox_gate_bwd")
    g["l1_fox_b_f"] = dbf[0, :nh]
    dfb = df.astype(BF16)
    tn1 = _pick(W, (1024, 512, 256, 128))
    tm1 = _pick(S, (1024, 512, 256, 128))
    per_part = W // tn1
    part_of = lambda pt: pt + pt // 2 - pt // 4

    def nt_map1(i, j, t):
        return (part_of(t // per_part) * (S // tm1) + i, t % per_part)

    def tn_map1(j, t):
        return (part_of(t // per_part), t % per_part)

    dp1_2d = dp1.reshape(6 * S, W)
    dw_main = mm_tn(h2, dp1_2d, 1, 5 * W, "l1_w_in_dw", dy_maps=[tn_map1], tn=tn1)
    dw_f = mm_tn(h2, dfb, 1, 128, "l1_w_f_dw")
    dh2 = mm_nt(dfb, wts["l1_w_in_f"], 1, S, D, "l1_w_f_dx", out_dtype=F32)
    dh2 = mm_nt(dp1_2d, wts["l1_w_in_main"], 1, S, D, "l1_w_in_dx", res=dh2, dy_maps=[nt_map1], tm=tm1, tn=tn1)
    dh2 = on_grad("l1_w_in", jnp.concatenate([dw_main, dw_f[:, :nh]], axis=1), dh2)
    dx2, dx2b, dg = rms_bwd(x2, wts["l1_mix_norm_g"], dh2, dx3, "l1_mix_rms_bwd")
    g["l1_mix_norm_g"] = dg

    dx1, dx1b = ffn_bwd(dx2, dx2b, ffn0_saved, "l0")
    dcat0 = mm_nt(dx1b, wts["l0_w_out"], 1, S, D, "l0_w_out_dx", tn=D)
    dcat0 = on_grad("l0_w_out", mm_tn(cat0, dx1b, 1, D, "l0_w_out_dw"), dcat0)
    dp0 = sb_bwd(p0, dcat0, W, "l0_sb_bwd")
    dp0, dscw = sc_bwd(p0, wts["l0_sc_conv_w"], dcat0, dp0, W, "l0_sc_bwd")
    g["l0_sc_conv_w"] = dscw
    n0 = wts["l0_w_in"].shape[1]
    td0 = math.gcd(n0, W)
    nd0 = n0 // td0
    half = S // 2
    tm0 = _pick(half, (1024, 512, 256, 128))
    per_part0 = W // td0

    def nt_maps0(k, first_block):
        def f(i, j, t):
            vb = j * nd0 + k
            return ((vb // per_part0) * (S // tm0) + first_block + i, vb % per_part0)
        return f

    def tn_maps0(k):
        def f(j, t):
            vb = j * nd0 + k
            return (vb // per_part0, vb % per_part0)
        return f

    dp0_2d = dp0.reshape(6 * S, W)
    dw0 = mm_tn(h0, dp0_2d, nb, n0, "l0_w_in_dw", dy_maps=[tn_maps0(k) for k in range(nd0)], tn=n0)
    dp0_2d = on_grad("l0_w_in", dw0, dp0_2d)
    halves = []
    for b, tag in enumerate("ab"):
        maps = [nt_maps0(k, b * (half // tm0)) for k in range(nd0)]
        halves.append(mm_nt(dp0_2d, wts["l0_w_in"], nb, half, D, f"l0_w_in_dx_{tag}", dy_maps=maps, tm=tm0, tn=n0))
        if b == 0:
            halves[0], dp0_2d = lax.optimization_barrier((halves[0], dp0_2d))
            dp0_2d = on_grad(None, None, dp0_2d)
    dh0 = jnp.concatenate(halves, axis=0)
    dx0, _, dg = rms_bwd(x, wts["l0_mix_norm_g"], dh0, dx1, "l0_mix_rms_bwd")
    g["l0_mix_norm_g"] = dg
    return dx0, g


GATHER_ID, PAIR_ID, CHIPS_ID = 1, 2, 3


def _place():
    return lax.axis_index("x"), lax.axis_index("y"), lax.axis_index("c")


def _other_chips(x, y):
    return [(x, 1 - y), (1 - x, y), (1 - x, 1 - y)]


def _handshake(peers):
    barrier = pltpu.get_barrier_semaphore()
    for peer in peers:
        pl.semaphore_signal(barrier, inc=1, device_id=peer, device_id_type=MESH)
    pl.semaphore_wait(barrier, len(peers))


UPDATE_LAG = 2


def _on_sequencer(body, out_type, scratch_types, collective_id, name):
    return pl.kernel(
        body,
        out_type=out_type,
        mesh=plsc.ScalarSubcoreMesh(axis_name="seq", num_cores=1),
        scratch_types=scratch_types,
        compiler_params=pltpu.CompilerParams(collective_id=collective_id),
        name=name,
    )


def all_gather(arrs, name):
    n = len(arrs)

    def body(*refs):
        xs, outs = refs[:n], refs[n : 2 * n]
        send_sems, recv_sems, local_sems = refs[2 * n :]
        x, y, c = _place()
        me, sibling = (x, y, c), (x, y, 1 - c)
        chips = _other_chips(x, y)
        _handshake([sibling] + [(*chip, c) for chip in chips])

        def copy(a, k, block, to, src=None):
            px, py, pc = block
            dst = outs[a].at[4 * px + 2 * py + pc]
            return pltpu.make_async_remote_copy(
                src_ref=dst if src is None else src, dst_ref=dst,
                send_sem=send_sems.at[7 * a + k], recv_sem=recv_sems.at[7 * a + k], device_id=to, device_id_type=MESH,
            )

        mine = [pltpu.make_async_copy(xs[a], outs[a].at[4 * x + 2 * y + c], local_sems.at[a]) for a in range(n)]
        for cp in mine:
            cp.start()
        first = []
        for a in range(n):
            first.append(copy(a, 0, me, sibling, src=xs[a]))
            first += [copy(a, 1 + j, me, (*chip, c), src=xs[a]) for j, chip in enumerate(chips)]
        for cp in first:
            cp.start()
        passed = []
        for a in range(n):
            for j, chip in enumerate(chips):
                copy(a, 1 + j, (*chip, c), me).wait_recv()
                cp = copy(a, 4 + j, (*chip, c), sibling)
                cp.start()
                passed.append(cp)
        for a in range(n):
            copy(a, 0, sibling, me).wait_recv()
            for j, chip in enumerate(chips):
                copy(a, 4 + j, (*chip, 1 - c), me).wait_recv()
        for cp in first + passed:
            cp.wait_send()
        for cp in mine:
            cp.wait()

    out_type = [jax.ShapeDtypeStruct((NDEV,) + a.shape, a.dtype) for a in arrs]
    sems = [pltpu.SemaphoreType.DMA((7 * n,)), pltpu.SemaphoreType.DMA((7 * n,)), pltpu.SemaphoreType.DMA((n,))]
    return _on_sequencer(body, out_type, sems, GATHER_ID, name)(*arrs)


_IN_HBM = pl.BlockSpec(memory_space=pltpu.HBM)
_IN_SEM = pl.BlockSpec(memory_space=pltpu.SEMAPHORE)
_EFFECT = pltpu.SideEffectType.DATAFLOW_SIDE_EFFECTING


def _split_start(make_copies, src, land_shape, nsem, name):
    def body(src_ref, land_ref, send_sems, recv_sems, land_thru, token):
        for cp in make_copies(src_ref, land_ref, send_sems, recv_sems):
            cp.start()
        token[...] = jnp.zeros_like(token)

    send_sems, recv_sems, land_thru, token = pl.pallas_call(
        body,
        name=name,
        out_shape=(
            pltpu.SemaphoreType.DMA((nsem,)), pltpu.SemaphoreType.DMA((nsem,)),
            pltpu.HBM(land_shape, src.dtype), jax.ShapeDtypeStruct((8, 128), F32),
        ),
        in_specs=(_IN_HBM, _IN_HBM),
        out_specs=(_IN_SEM, _IN_SEM, _IN_HBM, pl.BlockSpec(memory_space=pltpu.VMEM)),
        input_output_aliases={1: 2},
        compiler_params=pltpu.CompilerParams(has_side_effects=_EFFECT),
    )(src, pltpu.with_memory_space_constraint(lax.empty(land_shape, src.dtype), pltpu.HBM))
    return send_sems, recv_sems, src, land_thru, token


def _split_wait(make_copies, send_sems, recv_sems, src_thru, land_thru, after, name):
    def body(src_ref, land_ref, send_sems, recv_sems, after_ref, land_out):
        for cp in make_copies(src_ref, land_ref, send_sems, recv_sems):
            cp.wait_send()
            cp.wait_recv()

    return pl.pallas_call(
        body,
        name=name,
        out_shape=pltpu.HBM(land_thru.shape, land_thru.dtype),
        in_specs=(_IN_HBM, _IN_HBM, _IN_SEM, _IN_SEM, pl.BlockSpec(memory_space=pl.ANY)),
        out_specs=_IN_HBM,
        input_output_aliases={1: 0},
        compiler_params=pltpu.CompilerParams(has_side_effects=_EFFECT),
    )(src_thru, land_thru, send_sems, recv_sems, after)


def _pair_copies(src_ref, land_ref, send_sems, recv_sems):
    x, y, c = _place()
    return [
        pltpu.make_async_remote_copy(
            src_ref=src_ref.at[k, 1 - c], dst_ref=land_ref.at[k],
            send_sem=send_sems.at[k], recv_sem=recv_sems.at[k], device_id=(x, y, 1 - c), device_id_type=MESH,
        )
        for k in range(4)
    ]


def _chip_copies(src_ref, land_ref, send_sems, recv_sems):
    x, y, c = _place()
    return [
        pltpu.make_async_remote_copy(
            src_ref=src_ref.at[2 * px + py], dst_ref=land_ref.at[2 * x + y],
            send_sem=send_sems.at[j], recv_sem=recv_sems.at[j], device_id=(px, py, c), device_id_type=MESH,
        )
        for j, (px, py) in enumerate(_other_chips(x, y))
    ]


def _row_tile(R, C, max_elems):
    if R * C <= max_elems:
        return R
    best = None
    for tr in range(16, R, 16):
        if R % tr == 0 and tr * C <= max_elems:
            best = tr
    return best or R


def pair_sum(a42, land4, core, name):
    _, _, R, C = a42.shape
    tr = _row_tile(R, C, 1 << 20)

    def body(core_ref, a_ref, l_ref, o_ref):
        o_ref[...] = (a_ref[0].astype(F32) + l_ref[...].astype(F32)).astype(o_ref.dtype)

    return pl.pallas_call(
        body,
        grid_spec=pltpu.PrefetchScalarGridSpec(
            num_scalar_prefetch=1,
            grid=(4, R // tr),
            in_specs=[
                pl.BlockSpec((1, 1, tr, C), lambda k, r, core_ref: (k, core_ref[0], r, 0)),
                pl.BlockSpec((1, tr, C), lambda k, r, core_ref: (k, r, 0)),
            ],
            out_specs=pl.BlockSpec((1, tr, C), lambda k, r, core_ref: (k, r, 0)),
        ),
        out_shape=jax.ShapeDtypeStruct((4, R, C), BF16),
        compiler_params=_cp(("parallel", "parallel")),
        name=name,
    )(core, a42, land4)


def sum_slots(parts, name):
    P, R, C = parts.shape

    def body(p_ref, o_ref):
        acc = p_ref[0].astype(F32)
        for k in range(1, P):
            acc = acc + p_ref[k].astype(F32)
        o_ref[...] = acc

    tr = _row_tile(R, P * C, 1 << 21)
    return pl.pallas_call(
        body,
        grid=(R // tr,),
        in_specs=[pl.BlockSpec((P, tr, C), lambda r: (0, r, 0))],
        out_specs=pl.BlockSpec((tr, C), lambda r: (r, 0)),
        out_shape=jax.ShapeDtypeStruct((R, C), F32),
        compiler_params=_cp(("parallel",)),
        name=name,
    )(parts)


def adamw(w, m, v, parts, name):
    R, C = w.shape
    P = parts.shape[0]
    tr = _pick(R, (256, 128, 64, 32, 16, 8))
    c1 = 1.0 - ADAM_B1 ** ADAM_STEP
    c2 = 1.0 - ADAM_B2 ** ADAM_STEP

    def body(w_ref, m_ref, v_ref, p_ref, g_ref, d_ref, nm_ref, nv_ref):
        g = p_ref[0].astype(F32)
        for k in range(1, P):
            g = g + p_ref[k].astype(F32)
        nm = ADAM_B1 * m_ref[...] + (1.0 - ADAM_B1) * g
        nv = ADAM_B2 * v_ref[...] + (1.0 - ADAM_B2) * (g * g)
        g_ref[...] = g
        nm_ref[...] = nm
        nv_ref[...] = nv
        d_ref[...] = -ADAM_LR * ((nm / c1) / (jnp.sqrt(nv / c2) + ADAM_EPS) + ADAM_WD * w_ref[...])

    blk = pl.BlockSpec((tr, C), lambda r: (r, 0))
    shp = jax.ShapeDtypeStruct((R, C), F32)
    return pl.pallas_call(
        body,
        grid=(R // tr,),
        in_specs=[blk, blk, blk, pl.BlockSpec((P, tr, C), lambda r: (0, r, 0))],
        out_specs=[blk, blk, blk, blk],
        out_shape=[shp, shp, shp, shp],
        compiler_params=_cp(("parallel",)),
        name=name,
    )(w, m, v, parts)


def adamw_reduced(w, m, v, own, land, chip, name):
    R, C = w.shape
    tr = _pick(R, (256, 128, 64, 32, 16, 8))
    c1 = 1.0 - ADAM_B1 ** ADAM_STEP
    c2 = 1.0 - ADAM_B2 ** ADAM_STEP

    def body(chip_ref, w_ref, m_ref, v_ref, own_ref, land_ref, g_ref, d_ref, nm_ref, nv_ref):
        mine = own_ref[0].astype(F32)
        g = None
        for k in range(4):
            term = jnp.where(chip_ref[0] == k, mine, land_ref[k].astype(F32))
            g = term if g is None else g + term
        nm = ADAM_B1 * m_ref[...] + (1.0 - ADAM_B1) * g
        nv = ADAM_B2 * v_ref[...] + (1.0 - ADAM_B2) * (g * g)
        g_ref[...] = g
        nm_ref[...] = nm
        nv_ref[...] = nv
        d_ref[...] = -ADAM_LR * ((nm / c1) / (jnp.sqrt(nv / c2) + ADAM_EPS) + ADAM_WD * w_ref[...])

    blk = pl.BlockSpec((tr, C), lambda r, chip_ref: (r, 0))
    shp = jax.ShapeDtypeStruct((R, C), F32)
    return pl.pallas_call(
        body,
        grid_spec=pltpu.PrefetchScalarGridSpec(
            num_scalar_prefetch=1,
            grid=(R // tr,),
            in_specs=[
                blk, blk, blk,
                pl.BlockSpec((1, tr, C), lambda r, chip_ref: (chip_ref[0], r, 0)),
                pl.BlockSpec((4, tr, C), lambda r, chip_ref: (0, r, 0)),
            ],
            out_specs=[blk, blk, blk, blk],
        ),
        out_shape=[shp, shp, shp, shp],
        compiler_params=_cp(("parallel",)),
        name=name,
    )(chip, w, m, v, own, land)


_WEIGHTS = [
    "l0_mix_norm_g", "l0_w_in", "l0_sc_conv_w", "l0_w_out", "l0_ffn_norm_g", "l0_ffn_up", "l0_ffn_conv_w", "l0_ffn_down",
    "l1_mix_norm_g", "l1_w_in", "l1_fox_b_f", "l1_sg_w", "l1_sg_b", "l1_sg_norm_g", "l1_w_out", "l1_ffn_norm_g",
    "l1_ffn_up", "l1_ffn_conv_w", "l1_ffn_down", "final_norm_g",
]
_COL_SHARDED = ["l0_w_in", "l0_ffn_up", "l1_w_in", "l1_ffn_up"]
_ROW_SHARDED = ["l0_w_out", "l0_ffn_down", "l1_w_out", "l1_ffn_down"]
_BIG = ["l0_w_in", "l0_w_out", "l0_ffn_up", "l0_ffn_down", "l1_w_in", "l1_w_out", "l1_ffn_up", "l1_ffn_down"]
_CONV = ["l0_sc_conv_w", "l0_ffn_conv_w", "l1_ffn_conv_w"]
_SMALL = [n for n in _WEIGHTS if n not in _BIG]
_PACK_ROWS = 8


def _pack(arrs):
    flat = []
    for a in arrs:
        v = a.reshape(-1).astype(F32)
        pad = (-v.shape[0]) % (_PACK_ROWS * 128)
        flat.append(jnp.pad(v, (0, pad)))
    return jnp.concatenate(flat).reshape(-1, 128)


def _unpack(packed, shapes):
    out, off = [], 0
    flat = packed.reshape(-1)
    for shp in shapes:
        size = math.prod(shp)
        out.append(flat[off : off + size].reshape(shp))
        off += size + (-size) % (_PACK_ROWS * 128)
    return out


def kernel(x, l0_mix_norm_g, l0_w_in, l0_sc_conv_w, l0_w_out, l0_ffn_norm_g, l0_ffn_up, l0_ffn_conv_w, l0_ffn_down, l1_mix_norm_g, l1_w_in, l1_fox_b_f, l1_sg_w, l1_sg_b, l1_sg_norm_g, l1_w_out, l1_ffn_norm_g, l1_ffn_up, l1_ffn_conv_w, l1_ffn_down, final_norm_g, loss_target, m_l0_mix_norm_g, m_l0_w_in, m_l0_sc_conv_w, m_l0_w_out, m_l0_ffn_norm_g, m_l0_ffn_up, m_l0_ffn_conv_w, m_l0_ffn_down, m_l1_mix_norm_g, m_l1_w_in, m_l1_fox_b_f, m_l1_sg_w, m_l1_sg_b, m_l1_sg_norm_g, m_l1_w_out, m_l1_ffn_norm_g, m_l1_ffn_up, m_l1_ffn_conv_w, m_l1_ffn_down, m_final_norm_g, v_l0_mix_norm_g, v_l0_w_in, v_l0_sc_conv_w, v_l0_w_out, v_l0_ffn_norm_g, v_l0_ffn_up, v_l0_ffn_conv_w, v_l0_ffn_down, v_l1_mix_norm_g, v_l1_w_in, v_l1_fox_b_f, v_l1_sg_w, v_l1_sg_b, v_l1_sg_norm_g, v_l1_w_out, v_l1_ffn_norm_g, v_l1_ffn_up, v_l1_ffn_conv_w, v_l1_ffn_down, v_final_norm_g):
    given = dict(locals())
    w = {n: given[n] for n in _WEIGHTS}
    mom = {n: given["m_" + n] for n in _WEIGHTS}
    var = {n: given["v_" + n] for n in _WEIGHTS}
    xs, target = x[0], loss_target[0]
    S, D = xs.shape
    W = D // 2
    nh = W // HD
    cx, cy, cc = _place()
    me = 4 * cx + 2 * cy + cc

    wts = {"nb": NDEV, "F": l0_ffn_down.shape[0] * NDEV}
    for n in _SMALL:
        if n not in _CONV:
            wts[n] = w[n]
    gathered, loss_sum = {}, []

    def start_gather(n):
        got = all_gather([w[n].astype(BF16)] + ([w[c] for c in _CONV] if n == _BIG[0] else []), f"gather_{n}")
        if n == "l1_w_in":
            gathered[n] = got[0]
        elif n in _ROW_SHARDED:
            wts[n] = got[0].reshape(-1, D)
        else:
            wts[n] = got[0].reshape(NDEV * D, -1)
        for c, taps in zip(_CONV, got[1:]):
            wts[c] = taps.transpose(1, 0, 2).reshape(CONV_K, -1)

    def at(point, after, value):
        if point == "l1_w_in":
            got, after = lax.optimization_barrier((gathered[point], after))
            w_in1 = got.transpose(1, 0, 2).reshape(D, -1)
            wts["l1_w_in_main"] = w_in1[:, : 5 * W]
            wts["l1_w_in_f"] = jnp.pad(w_in1[:, 5 * W :], ((0, 0), (0, 128 - nh)))
        elif point == "loss":
            total, after = lax.optimization_barrier((lax.psum(value[0, 0], ("x", "y", "c")), after))
            loss_sum.append(total)
        return after

    core = jnp.reshape(cc, (1,)).astype(jnp.int32)
    chip = jnp.reshape(2 * cx + cy, (1,)).astype(jnp.int32)
    pair_flying, chip_flying = [], []
    out_g, out_d, out_m, out_v = {}, {}, {}, {}

    def tie(value, after):
        if after is None:
            return value, None
        return lax.optimization_barrier((value, after))

    def to_chips(after):
        n, flying = pair_flying.pop()
        landed = _split_wait(_pair_copies, *flying, f"reduce_pair_wait_{n}")
        summed = pair_sum(flying[2], landed, core, f"pair_sum_{n}")
        *flying, token = _split_start(_chip_copies, summed, summed.shape, 3, f"reduce_chips_{n}")
        token, after = tie(token, after)
        chip_flying.append((n, flying + [token]))
        return after

    def update(after, behind=None):
        n, flying = chip_flying.pop(0)
        if behind is not None:
            flying[4], _ = lax.optimization_barrier((flying[4], behind))
        landed = _split_wait(_chip_copies, *flying, f"reduce_chips_wait_{n}")
        res = adamw_reduced(w[n], mom[n], var[n], flying[2], landed, chip, f"adamw_{n}")
        res, after = tie(res, after)
        out_g[n], out_d[n], out_m[n], out_v[n] = res
        return after, res[0]

    def on_grad(n, term, after):
        if n is None:
            return to_chips(after)
        if n == "l1_w_in":
            term = term.reshape(D, NDEV, -1).transpose(1, 0, 2)
        elif n in _ROW_SHARDED:
            term = term.reshape(NDEV, -1, D)
        else:
            term = term.reshape(NDEV, D, -1)
        term = term.reshape((4, 2) + term.shape[1:])
        *flying, token = _split_start(_pair_copies, term, term.shape[:1] + term.shape[2:], 4, f"reduce_pair_{n}")
        token, after = tie(token, after)
        if len(chip_flying) == UPDATE_LAG:
            after, _ = update(after)
        if pair_flying:
            after = to_chips(after)
        pair_flying.append((n, flying + [token]))
        return after

    for n in _BIG:
        start_gather(n)
    dx, g = local_step(xs, target, wts, at, on_grad)
    done = None
    while chip_flying:
        _, done = update(None, behind=done)
    loss = loss_sum[0]

    small_terms = [g[n] for n in _SMALL]
    small_shapes = [tuple(t.shape) for t in small_terms]
    packed = _pack(small_terms)
    all_terms = all_gather([packed], "gather_small_grads")[0]
    small_sum = _unpack(sum_slots(all_terms, "sum_small_grads"), small_shapes)
    small_g = {}
    for n, t in zip(_SMALL, small_sum):
        if n in _CONV:
            cols = w[n].shape[1]
            t = lax.dynamic_slice_in_dim(t, me * cols, cols, axis=1)
        small_g[n] = t.reshape(w[n].shape)
    shapes = [w[n].shape for n in _SMALL]
    res = adamw(
        _pack([w[n] for n in _SMALL]), _pack([mom[n] for n in _SMALL]), _pack([var[n] for n in _SMALL]),
        _pack([small_g[n] for n in _SMALL])[None], "adamw_small",
    )
    for dst, packed_out in zip((out_g, out_d, out_m, out_v), res):
        for n, t in zip(_SMALL, _unpack(packed_out, shapes)):
            dst[n] = t

    return (loss, dx[None], *[out_g[n] for n in _WEIGHTS], *[out_d[n] for n in _WEIGHTS],
            *[out_m[n] for n in _WEIGHTS], *[out_v[n] for n in _WEIGHTS])
```

```python
import functools
import math

import jax
import jax.numpy as jnp
from jax import lax
from jax.experimental import pallas as pl
from jax.experimental.pallas import tpu as pltpu
from jax.experimental.pallas import tpu_sc as plsc

F32 = jnp.float32
BF16 = jnp.bfloat16
HD = 128
EPS = 1e-6
CONV_K = 3
VMEM_LIMIT_BYTES = 48 << 20
NDEV = 8
MESH = pl.DeviceIdType.MESH

ADAM_LR = 0.001
ADAM_B1 = 0.9
ADAM_B2 = 0.999
ADAM_EPS = 1e-08
ADAM_WD = 0.01
ADAM_STEP = 10


def _cp(sem):
    return pltpu.CompilerParams(dimension_semantics=sem, vmem_limit_bytes=VMEM_LIMIT_BYTES)


def _pick(n, prefs):
    for p in prefs:
        if n % p == 0:
            return p
    return n


def _dot(a, b):
    return jnp.dot(a, b, preferred_element_type=F32)


def _dot_nt(a, b):
    return lax.dot_general(a, b, (((1,), (1,)), ((), ())), preferred_element_type=F32)


def _dot_tn(a, b):
    return lax.dot_general(a, b, (((0,), (0,)), ((), ())), preferred_element_type=F32)


def _split3(x):
    hi = x.astype(BF16)
    r = x - hi.astype(F32)
    mid = r.astype(BF16)
    lo = (r - mid.astype(F32)).astype(BF16)
    return hi, mid, lo


def _dot_ones_right(x, ones_bf16):
    hi, mid, lo = _split3(x)
    return _dot(hi, ones_bf16) + _dot(mid, ones_bf16) + _dot(lo, ones_bf16)


def _dot_ones_left(ones_bf16, x):
    hi, mid, lo = _split3(x)
    return _dot(ones_bf16, hi) + _dot(ones_bf16, mid) + _dot(ones_bf16, lo)


def _iota2(shape, axis):
    return lax.broadcasted_iota(jnp.int32, shape, axis)


def mm_nn(a, w2d, nb, name, out_dtype=BF16, res=None, tm=None, tn=None, tk=None):
    M, K = a.shape
    n = w2d.shape[1]
    assert w2d.shape[0] == nb * K
    tm = tm or _pick(M, (1024, 512, 256, 128))
    tn = tn or _pick(n, (1408, 1024, 768, 512, 256, 128))
    tk = tk or (K if K <= 2048 else _pick(K, (1408, 1024, 512, 256, 128)))
    nk, nt = K // tk, n // tn
    has_res = res is not None

    def body(*refs):
        if has_res:
            a_ref, w_ref, r_ref, o_ref = refs[:4]
        else:
            a_ref, w_ref, o_ref = refs[:3]
            r_ref = None
        part = _dot(a_ref[...], w_ref[...])

        def finish(acc):
            if r_ref is not None:
                acc = acc + r_ref[...].astype(F32)
            o_ref[...] = acc.astype(o_ref.dtype)

        if nk == 1:
            finish(part)
        else:
            acc_ref = refs[-1]
            k = pl.program_id(3)

            @pl.when(k == 0)
            def _():
                acc_ref[...] = part

            @pl.when(k > 0)
            def _():
                acc_ref[...] += part

            @pl.when(k == nk - 1)
            def _():
                finish(acc_ref[...])

    in_specs = [
        pl.BlockSpec((tm, tk), lambda i, j, t, k: (i, k)),
        pl.BlockSpec((tk, tn), lambda i, j, t, k: (j * nk + k, t)),
    ]
    args = [a, w2d]
    out_spec = pl.BlockSpec((tm, tn), lambda i, j, t, k: (i, j * nt + t))
    if has_res:
        in_specs.append(out_spec)
        args.append(res)
    return pl.pallas_call(
        body,
        grid=(M // tm, nb, nt, nk),
        in_specs=in_specs,
        out_specs=out_spec,
        out_shape=jax.ShapeDtypeStruct((M, nb * n), out_dtype),
        scratch_shapes=[pltpu.VMEM((tm, tn), F32)] if nk > 1 else [],
        compiler_params=_cp(("parallel", "parallel", "parallel", "arbitrary")),
        name=name,
    )(*args)


def mm_nt(dy2d, w2d, nb, M, K, name, out_dtype=BF16, res=None, dy_maps=None, tm=None, tko=None, tn=None):
    n = w2d.shape[1]
    assert w2d.shape[0] == nb * K
    tm = tm or _pick(M, (1024, 512, 256, 128))
    tko = tko or _pick(K, (1024, 512, 256, 128))
    tn = tn or _pick(n, (1408, 1024, 768, 512, 256, 128))
    nt, nko = n // tn, K // tko
    has_res = res is not None
    if dy_maps is None:
        dy_maps = [lambda i, j, t: (i, j * nt + t)]
    nd = len(dy_maps)
    td = tn // nd

    one_step = nb * nt == 1

    def body(*refs):
        d_refs, w_ref = refs[:nd], refs[nd]
        r_ref = refs[nd + 1] if has_res else None
        d = d_refs[0][...] if nd == 1 else jnp.concatenate([r[...] for r in d_refs], axis=1)
        part = _dot_nt(d, w_ref[...])
        if one_step:
            o_ref = refs[-1]
            if r_ref is not None:
                part = part + r_ref[...].astype(F32)
            o_ref[...] = part.astype(o_ref.dtype)
            return
        o_ref, acc_ref = refs[-2], refs[-1]
        j, t = pl.program_id(2), pl.program_id(3)
        first = jnp.logical_and(j == 0, t == 0)
        last = jnp.logical_and(j == nb - 1, t == nt - 1)

        @pl.when(first)
        def _():
            acc_ref[...] = part

        @pl.when(jnp.logical_not(first))
        def _():
            acc_ref[...] += part

        @pl.when(last)
        def _():
            acc = acc_ref[...]
            if r_ref is not None:
                acc = acc + r_ref[...].astype(F32)
            o_ref[...] = acc.astype(o_ref.dtype)

    in_specs = [pl.BlockSpec((tm, td), functools.partial(lambda f, i, ko, j, t: f(i, j, t), f)) for f in dy_maps]
    in_specs.append(pl.BlockSpec((tko, tn), lambda i, ko, j, t: (j * nko + ko, t)))
    args = [dy2d] * nd + [w2d]
    out_spec = pl.BlockSpec((tm, tko), lambda i, ko, j, t: (i, ko))
    if has_res:
        in_specs.append(out_spec)
        args.append(res)
    return pl.pallas_call(
        body,
        grid=(M // tm, nko, nb, nt),
        in_specs=in_specs,
        out_specs=out_spec,
        out_shape=jax.ShapeDtypeStruct((M, K), out_dtype),
        scratch_shapes=[] if one_step else [pltpu.VMEM((tm, tko), F32)],
        compiler_params=_cp(("parallel", "parallel", "arbitrary", "arbitrary")),
        name=name,
    )(*args)


def mm_tn(x, dy2d, nb, n, name, out_dtype=BF16, dy_maps=None, tko=None, tn=None):
    S, K = x.shape
    tko = tko or _pick(K, (512, 256, 128))
    tn = tn or _pick(n, (1408, 1024, 768, 512, 256, 128))
    nt, nko = n // tn, K // tko
    if dy_maps is None:
        dy_maps = [lambda j, t: (0, j * nt + t)]
    nd = len(dy_maps)
    td = tn // nd

    def body(*refs):
        x_ref, d_refs, o_ref = refs[0], refs[1 : 1 + nd], refs[-1]
        d = d_refs[0][...] if nd == 1 else jnp.concatenate([r[...] for r in d_refs], axis=1)
        o_ref[...] = _dot_tn(x_ref[...], d).astype(o_ref.dtype)

    in_specs = [pl.BlockSpec((S, tko), lambda ko, j, t: (0, ko))]
    in_specs += [pl.BlockSpec((S, td), functools.partial(lambda f, ko, j, t: f(j, t), f)) for f in dy_maps]
    return pl.pallas_call(
        body,
        grid=(nko, nb, nt),
        in_specs=in_specs,
        out_specs=pl.BlockSpec((tko, tn), lambda ko, j, t: (j * nko + ko, t)),
        out_shape=jax.ShapeDtypeStruct((nb * K, n), out_dtype),
        compiler_params=_cp(("parallel", "parallel", "parallel")),
        name=name,
    )(x, *([dy2d] * nd))


def rms_fwd(x, g, name):
    S, D = x.shape
    tm = _pick(S, (256, 128))

    def body(x_ref, g_ref, o_ref):
        xf = x_ref[...]
        r = lax.rsqrt(jnp.mean(xf * xf, axis=-1, keepdims=True) + EPS)
        o_ref[...] = (xf * r * g_ref[...]).astype(o_ref.dtype)

    return pl.pallas_call(
        body,
        grid=(S // tm,),
        in_specs=[pl.BlockSpec((tm, D), lambda i: (i, 0)), pl.BlockSpec((1, D), lambda i: (0, 0))],
        out_specs=pl.BlockSpec((tm, D), lambda i: (i, 0)),
        out_shape=jax.ShapeDtypeStruct((S, D), BF16),
        compiler_params=_cp(("parallel",)),
        name=name,
    )(x, g.reshape(1, D))


def rms_bwd(x, g, dh, dres, name):
    S, D = x.shape
    tm = _pick(S, (256, 128))

    def body(x_ref, g_ref, dh_ref, dr_ref, dx_ref, dxb_ref, dg_ref):
        i = pl.program_id(0)
        xf = x_ref[...]
        dh = dh_ref[...].astype(F32)
        r = lax.rsqrt(jnp.mean(xf * xf, axis=-1, keepdims=True) + EPS)
        gy = dh * g_ref[...]
        proj = jnp.mean(gy * xf, axis=-1, keepdims=True)
        dx = dr_ref[...] + r * gy - xf * (r * r * r * proj)
        dx_ref[...] = dx
        dxb_ref[...] = dx.astype(BF16)
        dg = jnp.sum(dh * (xf * r), axis=0, keepdims=True)

        @pl.when(i == 0)
        def _():
            dg_ref[...] = dg

        @pl.when(i > 0)
        def _():
            dg_ref[...] += dg

    row = pl.BlockSpec((tm, D), lambda i: (i, 0))
    vec = pl.BlockSpec((1, D), lambda i: (0, 0))
    return pl.pallas_call(
        body,
        grid=(S // tm,),
        in_specs=[row, vec, row, row],
        out_specs=[row, row, vec],
        out_shape=[jax.ShapeDtypeStruct((S, D), F32), jax.ShapeDtypeStruct((S, D), BF16), jax.ShapeDtypeStruct((1, D), F32)],
        compiler_params=_cp(("arbitrary",)),
        name=name,
    )(x, g.reshape(1, D), dh, dres)


def loss_head(x, g, target, name):
    S, D = x.shape
    tm = _pick(S, (256, 128))

    def body(x_ref, g_ref, t_ref, dx_ref, dxb_ref, dg_ref, loss_ref):
        i = pl.program_id(0)
        xf = x_ref[...]
        gg = g_ref[...]
        r = lax.rsqrt(jnp.mean(xf * xf, axis=-1, keepdims=True) + EPS)
        xh = xf * r
        err = xh * gg - t_ref[...]
        part = (0.5 / D) * jnp.sum(err * err)
        dy = err * (1.0 / D)
        gy = dy * gg
        proj = jnp.mean(gy * xf, axis=-1, keepdims=True)
        dx = r * gy - xf * (r * r * r * proj)
        dx_ref[...] = dx
        dxb_ref[...] = dx.astype(BF16)
        dg = jnp.sum(dy * xh, axis=0, keepdims=True)
        lossb = jnp.full(loss_ref.shape, part, F32)

        @pl.when(i == 0)
        def _():
            dg_ref[...] = dg
            loss_ref[...] = lossb

        @pl.when(i > 0)
        def _():
            dg_ref[...] += dg
            loss_ref[...] += lossb

    row = pl.BlockSpec((tm, D), lambda i: (i, 0))
    vec = pl.BlockSpec((1, D), lambda i: (0, 0))
    return pl.pallas_call(
        body,
        grid=(S // tm,),
        in_specs=[row, vec, row],
        out_specs=[row, row, vec, pl.BlockSpec((8, 128), lambda i: (0, 0))],
        out_shape=[
            jax.ShapeDtypeStruct((S, D), F32),
            jax.ShapeDtypeStruct((S, D), BF16),
            jax.ShapeDtypeStruct((1, D), F32),
            jax.ShapeDtypeStruct((8, 128), F32),
        ],
        compiler_params=_cp(("arbitrary",)),
        name=name,
    )(x, g.reshape(1, D), target)


def _shift_down(s, k):
    if k == 0:
        return s
    return jnp.where(_iota2(s.shape, 0) >= k, pltpu.roll(s, k, axis=0), 0.0)


def _shift_up(s, k):
    if k == 0:
        return s
    n = s.shape[0]
    return jnp.where(_iota2(s.shape, 0) < n - k, pltpu.roll(s, n - k, axis=0), 0.0)


def _conv(s, w):
    return w[0:1] * _shift_down(s, 2) + w[1:2] * _shift_down(s, 1) + w[2:3] * s


def _conv_t(d, w):
    return w[2:3] * d + w[1:2] * _shift_up(d, 1) + w[0:1] * _shift_up(d, 2)


def _conv_dw(d, s):
    return [jnp.sum(d * _shift_down(s, CONV_K - 1 - k), axis=0, keepdims=True) for k in range(CONV_K)]


def sc_fwd(p, convw, cat, W, name):
    S = p.shape[0]
    tc = _pick(W, (256, 128))
    nc = W // tc

    def body(gb_ref, gc_ref, hi_ref, w_ref, cat_ref, o_ref):
        s = gc_ref[...].astype(F32) * hi_ref[...].astype(F32)
        o_ref[...] = (gb_ref[...].astype(F32) * _conv(s, w_ref[...])).astype(o_ref.dtype)

    col = lambda part: pl.BlockSpec((S, tc), lambda c: (0, part * nc + c))
    return pl.pallas_call(
        body,
        grid=(nc,),
        in_specs=[col(3), col(4), col(5), pl.BlockSpec((CONV_K, tc), lambda c: (0, c)), pl.BlockSpec(memory_space=pl.ANY)],
        out_specs=col(1),
        out_shape=jax.ShapeDtypeStruct(cat.shape, cat.dtype),
        input_output_aliases={4: 0},
        compiler_params=_cp(("parallel",)),
        name=name,
    )(p, p, p, convw, cat)


def sc_bwd(p, convw, dcat, dp, W, name):
    S = p.shape[0]
    tc = _pick(W, (256, 128))
    nc = W // tc

    def body(gb_ref, gc_ref, hi_ref, w_ref, do_ref, dp_in_ref, dp_ref, dw_ref):
        gb = gb_ref[...].astype(F32)
        gc = gc_ref[...].astype(F32)
        hi = hi_ref[...].astype(F32)
        w = w_ref[...]
        do = do_ref[...].astype(F32)
        s = gc * hi
        dcs = do * gb
        ds = _conv_t(dcs, w)
        dp_ref[0] = (do * _conv(s, w)).astype(dp_ref.dtype)
        dp_ref[1] = (ds * hi).astype(dp_ref.dtype)
        dp_ref[2] = (ds * gc).astype(dp_ref.dtype)
        for k, row in enumerate(_conv_dw(dcs, s)):
            dw_ref[k : k + 1, :] = row

    col = lambda part: pl.BlockSpec((S, tc), lambda c: (0, part * nc + c))
    return pl.pallas_call(
        body,
        grid=(nc,),
        in_specs=[
            col(3), col(4), col(5),
            pl.BlockSpec((CONV_K, tc), lambda c: (0, c)),
            pl.BlockSpec((S, tc), lambda c: (0, nc + c)),
            pl.BlockSpec(memory_space=pl.ANY),
        ],
        out_specs=[pl.BlockSpec((3, S, tc), lambda c: (1, 0, c)), pl.BlockSpec((CONV_K, tc), lambda c: (0, c))],
        out_shape=[jax.ShapeDtypeStruct(dp.shape, dp.dtype), jax.ShapeDtypeStruct((CONV_K, W), F32)],
        input_output_aliases={5: 0},
        compiler_params=_cp(("parallel",)),
        name=name,
    )(p, p, p, convw, dcat, dp)


def _silu_parts(a):
    sig = 1.0 / (1.0 + jnp.exp(-a))
    return a * sig, sig


def ffn_act_fwd(u, convw, F, name):
    S = u.shape[0]
    tc = _pick(F, (256, 128))
    nc = F // tc

    def body(ug_ref, uu_ref, wg_ref, wu_ref, o_ref):
        ag = _conv(ug_ref[...].astype(F32), wg_ref[...])
        au = _conv(uu_ref[...].astype(F32), wu_ref[...])
        o_ref[...] = (_silu_parts(ag)[0] * au).astype(o_ref.dtype)

    col = lambda half: pl.BlockSpec((S, tc), lambda c: (0, half * nc + c))
    wcol = lambda half: pl.BlockSpec((CONV_K, tc), lambda c: (0, half * nc + c))
    return pl.pallas_call(
        body,
        grid=(nc,),
        in_specs=[col(0), col(1), wcol(0), wcol(1)],
        out_specs=pl.BlockSpec((S, tc), lambda c: (0, c)),
        out_shape=jax.ShapeDtypeStruct((S, F), BF16),
        compiler_params=_cp(("parallel",)),
        name=name,
    )(u, u, convw, convw)


def ffn_act_bwd(u, convw, dact, F, name):
    S = u.shape[0]
    tc = _pick(F, (256, 128))
    nc = F // tc

    def body(ug_ref, uu_ref, wg_ref, wu_ref, da_ref, du_ref, dw_ref):
        ug = ug_ref[...].astype(F32)
        uu = uu_ref[...].astype(F32)
        wg = wg_ref[...]
        wu = wu_ref[...]
        da = da_ref[...].astype(F32)
        ag = _conv(ug, wg)
        au = _conv(uu, wu)
        sl, sig = _silu_parts(ag)
        dag = da * au * (sig * (1.0 + ag * (1.0 - sig)))
        dau = da * sl
        du_ref[0] = _conv_t(dag, wg).astype(du_ref.dtype)
        du_ref[1] = _conv_t(dau, wu).astype(du_ref.dtype)
        for k, (rg, ru) in enumerate(zip(_conv_dw(dag, ug), _conv_dw(dau, uu))):
            dw_ref[0, k : k + 1, :] = rg
            dw_ref[1, k : k + 1, :] = ru

    col = lambda half: pl.BlockSpec((S, tc), lambda c: (0, half * nc + c))
    wcol = lambda half: pl.BlockSpec((CONV_K, tc), lambda c: (0, half * nc + c))
    return pl.pallas_call(
        body,
        grid=(nc,),
        in_specs=[col(0), col(1), wcol(0), wcol(1), pl.BlockSpec((S, tc), lambda c: (0, c))],
        out_specs=[pl.BlockSpec((2, S, tc), lambda c: (0, 0, c)), pl.BlockSpec((2, CONV_K, tc), lambda c: (0, 0, c))],
        out_shape=[jax.ShapeDtypeStruct((2, S, F), BF16), jax.ShapeDtypeStruct((2, CONV_K, F), F32)],
        compiler_params=_cp(("parallel",)),
        name=name,
    )(u, u, convw, convw, dact)


def _softplus(z):
    return jnp.maximum(z, 0.0) + jnp.log(1.0 + jnp.exp(-jnp.abs(z)))


def _key_strip(S):
    return _pick(S, (512, 256, 128))


def _query_rows(S):
    return _pick(S, (512, 256, 128))


def _split2(x):
    hi = x.astype(BF16)
    return hi, (x - hi.astype(F32)).astype(BF16)


def _block_sums(x, ones_bf16):
    hi, lo = _split2(x)
    return [
        _dot(hi[:, b * HD : (b + 1) * HD], ones_bf16) + _dot(lo[:, b * HD : (b + 1) * HD], ones_bf16)
        for b in range(x.shape[1] // HD)
    ]


def _strip_mask(shape, row0, off, strict):
    cols, rows = _iota2(shape, 1) + off, _iota2(shape, 0) + row0
    return cols < rows if strict else cols <= rows


def _sb_strip(q, ks, row0, off, run, su):
    z = _dot_nt(q, ks) * (HD ** -0.5)
    mask = _strip_mask(z.shape, row0, off, True)
    sp = _softplus(z)
    l = jnp.where(mask, -sp, 0.0)
    within = _block_sums(l, su)
    later = [None] * len(within)
    for b in reversed(range(len(within))):
        later[b] = within[b] + run
        run = run + jnp.sum(l[:, b * HD : (b + 1) * HD], axis=1, keepdims=True)
    a = jnp.where(mask, jnp.exp(z - sp + jnp.concatenate(later, axis=1)), 0.0)
    return z, mask, a, run


def sb_fwd(p, W, name):
    S = p.shape[0]
    TQ, TK = _query_rows(S), _key_strip(S)
    nh, nq = W // HD, S // TQ

    def body(q_ref, k_ref, v_ref, o_ref):
        i = pl.program_id(1)
        q = q_ref[...]
        su = (_iota2((HD, HD), 0) > _iota2((HD, HD), 1)).astype(BF16)
        last = (i * TQ + TQ - 1) // TK

        def step(gg, carry):
            acc, run = carry
            off = pl.multiple_of((last - gg) * TK, TK)
            _, _, a, run = _sb_strip(q, k_ref[pl.ds(off, TK), :], i * TQ, off, run, su)
            return acc + _dot(a.astype(BF16), v_ref[pl.ds(off, TK), :]), run

        acc, _ = lax.fori_loop(0, last + 1, step, (jnp.zeros((TQ, HD), F32), jnp.zeros((TQ, 1), F32)))
        o_ref[...] = acc.astype(o_ref.dtype)

    return pl.pallas_call(
        body,
        grid=(nh, nq),
        in_specs=[
            pl.BlockSpec((TQ, HD), lambda h, i: (i, h)),
            pl.BlockSpec((S, HD), lambda h, i: (0, nh + h)),
            pl.BlockSpec((S, HD), lambda h, i: (0, 2 * nh + h)),
        ],
        out_specs=pl.BlockSpec((TQ, HD), lambda h, i: (i, h)),
        out_shape=jax.ShapeDtypeStruct((S, 2 * W), BF16),
        compiler_params=_cp(("parallel", "arbitrary")),
        name=name,
    )(p, p, p)


def sb_bwd(p, dcat, W, name):
    S = p.shape[0]
    TQ, TK = _query_rows(S), _key_strip(S)
    nh, nq = W // HD, S // TQ
    scale = HD ** -0.5

    def body(q_ref, k_ref, v_ref, do_ref, dp_ref, dk_acc, dv_acc, e_scr, z_scr):
        i = pl.program_id(1)
        q = q_ref[...]
        do = do_ref[...]
        su = (_iota2((HD, HD), 0) > _iota2((HD, HD), 1)).astype(BF16)
        sl = (_iota2((HD, HD), 0) < _iota2((HD, HD), 1)).astype(BF16)
        last = (i * TQ + TQ - 1) // TK

        @pl.when(i == 0)
        def _():
            dk_acc[...] = jnp.zeros_like(dk_acc)
            dv_acc[...] = jnp.zeros_like(dv_acc)

        def pass_a(gg, run):
            g = last - gg
            off = pl.multiple_of(g * TK, TK)
            z, _, a, run = _sb_strip(q, k_ref[pl.ds(off, TK), :], i * TQ, off, run, su)
            e_scr[g] = a * _dot_nt(do, v_ref[pl.ds(off, TK), :])
            z_scr[g] = z
            dv_acc[pl.ds(off, TK), :] += _dot_tn(a.astype(BF16), do)
            return run

        lax.fori_loop(0, last + 1, pass_a, jnp.zeros((TQ, 1), F32))

        def pass_b(g, carry):
            dq, run_e = carry
            off = pl.multiple_of(g * TK, TK)
            e = e_scr[g]
            z = z_scr[g]
            mask = _strip_mask(z.shape, i * TQ, off, True)
            within = _block_sums(e, sl)
            before = []
            for b in range(len(within)):
                before.append(within[b] + run_e)
                run_e = run_e + jnp.sum(e[:, b * HD : (b + 1) * HD], axis=1, keepdims=True)
            sig = 1.0 / (1.0 + jnp.exp(-z))
            dz = jnp.where(mask, e * (1.0 - sig) - jnp.concatenate(before, axis=1) * sig, 0.0)
            dz = (dz * scale).astype(BF16)
            dq = dq + _dot(dz, k_ref[pl.ds(off, TK), :])
            dk_acc[pl.ds(off, TK), :] += _dot_tn(dz, q)
            return dq, run_e

        dq, _ = lax.fori_loop(0, last + 1, pass_b, (jnp.zeros((TQ, HD), F32), jnp.zeros((TQ, 1), F32)))
        dp_ref[0, pl.ds(pl.multiple_of(i * TQ, TQ), TQ), :] = dq.astype(dp_ref.dtype)

        @pl.when(i == nq - 1)
        def _():
            dp_ref[1] = dk_acc[...].astype(dp_ref.dtype)
            dp_ref[2] = dv_acc[...].astype(dp_ref.dtype)

    return pl.pallas_call(
        body,
        grid=(nh, nq),
        in_specs=[
            pl.BlockSpec((TQ, HD), lambda h, i: (i, h)),
            pl.BlockSpec((S, HD), lambda h, i: (0, nh + h)),
            pl.BlockSpec((S, HD), lambda h, i: (0, 2 * nh + h)),
            pl.BlockSpec((TQ, HD), lambda h, i: (i, h)),
        ],
        out_specs=pl.BlockSpec((3, S, HD), lambda h, i: (0, 0, h)),
        out_shape=jax.ShapeDtypeStruct((6, S, W), BF16),
        scratch_shapes=[
            pltpu.VMEM((S, HD), F32),
            pltpu.VMEM((S, HD), F32),
            pltpu.VMEM((S // TK, TQ, TK), F32),
            pltpu.VMEM((S // TK, TQ, TK), F32),
        ],
        compiler_params=_cp(("parallel", "arbitrary")),
        name=name,
    )(p, p, p, dcat)


def fox_gate_fwd(f, b, name):
    S = f.shape[0]
    nq = S // HD

    def body(f_ref, b_ref, c_ref, run):
        i = pl.program_id(0)

        @pl.when(i == 0)
        def _():
            run[...] = jnp.zeros_like(run)

        lf = -_softplus(-(f_ref[...] + b_ref[...]))
        tri = (_iota2((HD, HD), 0) >= _iota2((HD, HD), 1)).astype(BF16)
        c_ref[...] = _dot_ones_left(tri, lf) + run[...]
        run[...] += jnp.sum(lf, axis=0, keepdims=True)

    return pl.pallas_call(
        body,
        grid=(nq,),
        in_specs=[pl.BlockSpec((HD, 128), lambda i: (i, 0)), pl.BlockSpec((1, 128), lambda i: (0, 0))],
        out_specs=pl.BlockSpec((HD, 128), lambda i: (i, 0)),
        out_shape=jax.ShapeDtypeStruct((S, 128), F32),
        scratch_shapes=[pltpu.VMEM((1, 128), F32)],
        compiler_params=_cp(("arbitrary",)),
        name=name,
    )(f, b)


def fox_gate_bwd(f, b, dc, name):
    S = f.shape[0]
    nq = S // HD

    def body(f_ref, b_ref, dc_ref, df_ref, db_ref, run):
        i = pl.program_id(0)

        @pl.when(i == 0)
        def _():
            run[...] = jnp.zeros_like(run)

        dc = dc_ref[...]
        tri = (_iota2((HD, HD), 0) <= _iota2((HD, HD), 1)).astype(BF16)
        dlf = _dot_ones_left(tri, dc) + run[...]
        run[...] += jnp.sum(dc, axis=0, keepdims=True)
        x = f_ref[...] + b_ref[...]
        df = dlf * (1.0 / (1.0 + jnp.exp(x)))
        df_ref[...] = df
        db = jnp.sum(df, axis=0, keepdims=True)

        @pl.when(i == 0)
        def _():
            db_ref[...] = db

        @pl.when(i > 0)
        def _():
            db_ref[...] += db

    rev = pl.BlockSpec((HD, 128), lambda i: (nq - 1 - i, 0))
    vec = pl.BlockSpec((1, 128), lambda i: (0, 0))
    return pl.pallas_call(
        body,
        grid=(nq,),
        in_specs=[rev, vec, rev],
        out_specs=[rev, vec],
        out_shape=[jax.ShapeDtypeStruct((S, 128), F32), jax.ShapeDtypeStruct((1, 128), F32)],
        scratch_shapes=[pltpu.VMEM((1, 128), F32)],
        compiler_params=_cp(("arbitrary",)),
        name=name,
    )(f, b, dc)


def _fox_logits(q, ks, ct, cs, row0, off):
    s = _dot_nt(q, ks) * (HD ** -0.5) + (ct - cs)
    mask = _strip_mask(s.shape, row0, off, False)
    return jnp.where(mask, s, -1e30), mask


def fox_fwd(p, ccol, crow, cat, W, name):
    S = p.shape[0]
    TQ, TK = _query_rows(S), _key_strip(S)
    nh, nq = W // HD, S // TQ

    def body(q_ref, k_ref, v_ref, cc_ref, cr_ref, cat_ref, o_ref, lse_ref):
        i = pl.program_id(1)
        q = q_ref[...]
        ct = cc_ref[0]

        def step(g, carry):
            m, l, acc = carry
            off = pl.multiple_of(g * TK, TK)
            s, _ = _fox_logits(q, k_ref[pl.ds(off, TK), :], ct, cr_ref[0, pl.ds(g, 1), :], i * TQ, off)
            m_new = jnp.maximum(m, jnp.max(s, axis=1, keepdims=True))
            alpha = jnp.exp(m - m_new)
            pr = jnp.exp(s - m_new)
            l = alpha * l + jnp.sum(pr, axis=1, keepdims=True)
            acc = alpha * acc + _dot(pr.astype(BF16), v_ref[pl.ds(off, TK), :])
            return m_new, l, acc

        init = (jnp.full((TQ, 1), -1e30, F32), jnp.zeros((TQ, 1), F32), jnp.zeros((TQ, HD), F32))
        m, l, acc = lax.fori_loop(0, (i * TQ + TQ - 1) // TK + 1, step, init)
        o_ref[...] = (acc / l).astype(o_ref.dtype)
        lse_ref[0] = m + jnp.log(l)

    return pl.pallas_call(
        body,
        grid=(nh, nq),
        in_specs=[
            pl.BlockSpec((TQ, HD), lambda h, i: (i, 2 * nh + h)),
            pl.BlockSpec((S, HD), lambda h, i: (0, 3 * nh + h)),
            pl.BlockSpec((S, HD), lambda h, i: (0, 4 * nh + h)),
            pl.BlockSpec((1, TQ, 1), lambda h, i: (h, i, 0)),
            pl.BlockSpec((1, S // TK, TK), lambda h, i: (h, 0, 0)),
            pl.BlockSpec(memory_space=pl.ANY),
        ],
        out_specs=[pl.BlockSpec((TQ, HD), lambda h, i: (i, nh + h)), pl.BlockSpec((1, TQ, 1), lambda h, i: (h, i, 0))],
        out_shape=[jax.ShapeDtypeStruct(cat.shape, cat.dtype), jax.ShapeDtypeStruct((nh, S, 1), F32)],
        input_output_aliases={5: 0},
        compiler_params=_cp(("parallel", "arbitrary")),
        name=name,
    )(p, p, p, ccol, crow, cat)


def fox_bwd(p, ccol, crow, cat, lse, dcat, dp, W, name):
    S = p.shape[0]
    TQ, TK = _query_rows(S), _key_strip(S)
    nh, nq = W // HD, S // TQ
    scale = HD ** -0.5

    def body(q_ref, k_ref, v_ref, cc_ref, cr_ref, o_ref, lse_ref, do_ref, dp_in_ref, dp_ref, dcs_ref, dct_ref, dk_acc, dv_acc):
        i = pl.program_id(1)
        q = q_ref[...]
        do = do_ref[...]
        ct = cc_ref[0]
        lse_i = lse_ref[0]
        delta = jnp.sum(do.astype(F32) * o_ref[...].astype(F32), axis=1, keepdims=True)

        @pl.when(i == 0)
        def _():
            dk_acc[...] = jnp.zeros_like(dk_acc)
            dv_acc[...] = jnp.zeros_like(dv_acc)
            dcs_ref[...] = jnp.zeros_like(dcs_ref)

        def step(g, carry):
            dq, dct = carry
            off = pl.multiple_of(g * TK, TK)
            ks = k_ref[pl.ds(off, TK), :]
            s, mask = _fox_logits(q, ks, ct, cr_ref[0, pl.ds(g, 1), :], i * TQ, off)
            pr = jnp.where(mask, jnp.exp(s - lse_i), 0.0)
            ds = pr * (_dot_nt(do, v_ref[pl.ds(off, TK), :]) - delta)
            dv_acc[pl.ds(off, TK), :] += _dot_tn(pr.astype(BF16), do)
            dsb = (ds * scale).astype(BF16)
            dk_acc[pl.ds(off, TK), :] += _dot_tn(dsb, q)
            dcs_ref[0, pl.ds(g, 1), :] += jnp.sum(ds, axis=0, keepdims=True)
            return dq + _dot(dsb, ks), dct + jnp.sum(ds, axis=1, keepdims=True)

        dq, dct = lax.fori_loop(0, (i * TQ + TQ - 1) // TK + 1, step, (jnp.zeros((TQ, HD), F32), jnp.zeros((TQ, 1), F32)))
        dp_ref[0, pl.ds(pl.multiple_of(i * TQ, TQ), TQ), :] = dq.astype(dp_ref.dtype)
        dct_ref[0] = dct

        @pl.when(i == nq - 1)
        def _():
            dp_ref[1] = dk_acc[...].astype(dp_ref.dtype)
            dp_ref[2] = dv_acc[...].astype(dp_ref.dtype)

    return pl.pallas_call(
        body,
        grid=(nh, nq),
        in_specs=[
            pl.BlockSpec((TQ, HD), lambda h, i: (i, 2 * nh + h)),
            pl.BlockSpec((S, HD), lambda h, i: (0, 3 * nh + h)),
            pl.BlockSpec((S, HD), lambda h, i: (0, 4 * nh + h)),
            pl.BlockSpec((1, TQ, 1), lambda h, i: (h, i, 0)),
            pl.BlockSpec((1, S // TK, TK), lambda h, i: (h, 0, 0)),
            pl.BlockSpec((TQ, HD), lambda h, i: (i, nh + h)),
            pl.BlockSpec((1, TQ, 1), lambda h, i: (h, i, 0)),
            pl.BlockSpec((TQ, HD), lambda h, i: (i, nh + h)),
            pl.BlockSpec(memory_space=pl.ANY),
        ],
        out_specs=[
            pl.BlockSpec((3, S, HD), lambda h, i: (1, 0, h)),
            pl.BlockSpec((1, S // TK, TK), lambda h, i: (h, 0, 0)),
            pl.BlockSpec((1, TQ, 1), lambda h, i: (h, i, 0)),
        ],
        out_shape=[
            jax.ShapeDtypeStruct(dp.shape, dp.dtype),
            jax.ShapeDtypeStruct((nh, S // TK, TK), F32),
            jax.ShapeDtypeStruct((nh, S, 1), F32),
        ],
        input_output_aliases={8: 0},
        scratch_shapes=[pltpu.VMEM((S, HD), F32), pltpu.VMEM((S, HD), F32)],
        compiler_params=_cp(("parallel", "arbitrary")),
        name=name,
    )(p, p, p, ccol, crow, cat, lse, dcat, dp)


_GELU_K = math.sqrt(2.0 / math.pi)
_GELU_C = 0.044715


def _gelu(x):
    return 0.5 * x * (1.0 + jnp.tanh(_GELU_K * (x + _GELU_C * x * x * x)))


def _gelu_grad(x):
    t = jnp.tanh(_GELU_K * (x + _GELU_C * x * x * x))
    return 0.5 * (1.0 + t) + 0.5 * x * (1.0 - t * t) * (_GELU_K * (1.0 + 3.0 * _GELU_C * x * x))


def _layernorm_parts(gv):
    xc = gv - jnp.mean(gv, axis=-1, keepdims=True)
    r = lax.rsqrt(jnp.mean(xc * xc, axis=-1, keepdims=True) + EPS)
    return xc * r, r


def sg_fwd(p, sg_w, sg_bt, sg_g, W, name):
    S = p.shape[0]
    G, nq = W // HD, S // HD

    def body(u_ref, v_ref, w_ref, bt_ref, g_ref, o_ref):
        xh, _ = _layernorm_parts(_gelu(v_ref[...].astype(F32)))
        vn = (xh * g_ref[...]).astype(BF16)
        tri = _iota2((HD, HD), 0) >= _iota2((HD, HD), 1)
        for gi in range(G):
            cols = slice(gi * HD, (gi + 1) * HD)
            wt = jnp.where(tri, w_ref[gi], 0.0).astype(BF16)
            mixed = _dot(wt, vn[:, cols]) + bt_ref[:, gi : gi + 1]
            o_ref[:, cols] = (_gelu(u_ref[:, cols].astype(F32)) * mixed).astype(o_ref.dtype)

    return pl.pallas_call(
        body,
        grid=(nq,),
        in_specs=[
            pl.BlockSpec((HD, W), lambda i: (i, 0)),
            pl.BlockSpec((HD, W), lambda i: (i, 1)),
            pl.BlockSpec((G, HD, HD), lambda i: (0, 0, 0)),
            pl.BlockSpec((HD, G), lambda i: (0, 0)),
            pl.BlockSpec((1, W), lambda i: (0, 0)),
        ],
        out_specs=pl.BlockSpec((HD, W), lambda i: (i, 0)),
        out_shape=jax.ShapeDtypeStruct((S, 2 * W), BF16),
        compiler_params=_cp(("parallel",)),
        name=name,
    )(p, p, sg_w, sg_bt, sg_g.reshape(1, W))


def sg_bwd(p, sg_w, sg_bt, sg_g, dcat, W, name):
    S = p.shape[0]
    G, nq = W // HD, S // HD

    def body(u_ref, v_ref, w_ref, bt_ref, g_ref, do_ref, dp_ref, dw_ref, dbt_ref, dg_ref, dvn_scr):
        i = pl.program_id(0)

        @pl.when(i == 0)
        def _():
            dw_ref[...] = jnp.zeros_like(dw_ref)
            dbt_ref[...] = jnp.zeros_like(dbt_ref)
            dg_ref[...] = jnp.zeros_like(dg_ref)

        v = v_ref[...].astype(F32)
        xh, r = _layernorm_parts(_gelu(v))
        gg = g_ref[...]
        vn = (xh * gg).astype(BF16)
        tri = _iota2((HD, HD), 0) >= _iota2((HD, HD), 1)
        for gi in range(G):
            cols = slice(gi * HD, (gi + 1) * HD)
            wt = jnp.where(tri, w_ref[gi], 0.0).astype(BF16)
            mixed = _dot(wt, vn[:, cols]) + bt_ref[:, gi : gi + 1]
            u = u_ref[:, cols].astype(F32)
            do = do_ref[:, cols].astype(F32)
            dp_ref[0, :, cols] = (do * mixed * _gelu_grad(u)).astype(dp_ref.dtype)
            dmix = do * _gelu(u)
            dmb = dmix.astype(BF16)
            dw_ref[gi] += jnp.where(tri, _dot_nt(dmb, vn[:, cols]), 0.0)
            dbt_ref[:, gi : gi + 1] += jnp.sum(dmix, axis=1, keepdims=True)
            dvn_scr[:, cols] = _dot_tn(wt, dmb)
        dvn = dvn_scr[...]
        dg_ref[...] += jnp.sum(dvn * xh, axis=0, keepdims=True)
        dxh = dvn * gg
        dgv = r * (dxh - jnp.mean(dxh, axis=-1, keepdims=True) - xh * jnp.mean(dxh * xh, axis=-1, keepdims=True))
        dp_ref[1] = (dgv * _gelu_grad(v)).astype(dp_ref.dtype)

    return pl.pallas_call(
        body,
        grid=(nq,),
        in_specs=[
            pl.BlockSpec((HD, W), lambda i: (i, 0)),
            pl.BlockSpec((HD, W), lambda i: (i, 1)),
            pl.BlockSpec((G, HD, HD), lambda i: (0, 0, 0)),
            pl.BlockSpec((HD, G), lambda i: (0, 0)),
            pl.BlockSpec((1, W), lambda i: (0, 0)),
            pl.BlockSpec((HD, W), lambda i: (i, 0)),
        ],
        out_specs=[
            pl.BlockSpec((2, HD, W), lambda i: (0, i, 0)),
            pl.BlockSpec((G, HD, HD), lambda i: (0, 0, 0)),
            pl.BlockSpec((HD, G), lambda i: (0, 0)),
            pl.BlockSpec((1, W), lambda i: (0, 0)),
        ],
        out_shape=[
            jax.ShapeDtypeStruct((6, S, W), BF16),
            jax.ShapeDtypeStruct((G, HD, HD), F32),
            jax.ShapeDtypeStruct((HD, G), F32),
            jax.ShapeDtypeStruct((1, W), F32),
        ],
        scratch_shapes=[pltpu.VMEM((HD, W), F32)],
        compiler_params=_cp(("arbitrary",)),
        name=name,
    )(p, p, sg_w, sg_bt, sg_g.reshape(1, W), dcat)


def local_step(x, target, wts, at, on_grad):
    S, D = x.shape
    W = D // 2
    nb, F = wts["nb"], wts["F"]
    g = {}

    def ffn_fwd(xin, l):
        h = rms_fwd(xin, wts[f"{l}_ffn_norm_g"], f"{l}_ffn_rms")
        u = mm_nn(h, wts[f"{l}_ffn_up"], nb, f"{l}_ffn_up_mm")
        act = ffn_act_fwd(u, wts[f"{l}_ffn_conv_w"], F, f"{l}_ffn_act")
        half_tile = _pick(S, (512, 256, 128))
        xout = mm_nn(act, wts[f"{l}_ffn_down"], 1, f"{l}_ffn_down_mm", out_dtype=F32, res=xin,
                     tm=half_tile, tn=_pick(D, (512, 256, 128)), tk=F)
        return xout, (xin, h, u, act)

    def ffn_bwd(dxout, dxoutb, saved, l):
        xin, h, u, act = saved
        dact = mm_nt(dxoutb, wts[f"{l}_ffn_down"], 1, S, F, f"{l}_ffn_down_dx", tko=_pick(F, (512, 256, 128)), tn=D)
        dact = on_grad(f"{l}_ffn_down", mm_tn(act, dxoutb, 1, D, f"{l}_ffn_down_dw", tn=D), dact)
        du, dcw = ffn_act_bwd(u, wts[f"{l}_ffn_conv_w"], dact, F, f"{l}_ffn_act_bwd")
        g[f"{l}_ffn_conv_w"] = jnp.concatenate([dcw[0], dcw[1]], axis=1)
        du2 = du.reshape(2 * S, F)
        n = wts[f"{l}_ffn_up"].shape[1]
        tn = _pick(n, (1408, 1024, 768, 512, 256, 128))
        per_half = F // tn
        nt = n // tn

        def up_block(i, j, t):
            vb = j * nt + t
            return vb // per_half, vb % per_half

        tm = _pick(S, (1024, 512, 256, 128))

        def nt_map(i, j, t):
            half, cb = up_block(i, j, t)
            return (half * (S // tm) + i, cb)

        def tn_map(j, t):
            half, cb = up_block(0, j, t)
            return (half, cb)

        dh = mm_nt(du2, wts[f"{l}_ffn_up"], nb, S, D, f"{l}_ffn_up_dx", dy_maps=[nt_map], tm=tm, tko=D, tn=tn)
        dh = on_grad(f"{l}_ffn_up", mm_tn(h, du2, nb, n, f"{l}_ffn_up_dw", dy_maps=[tn_map], tn=tn), dh)
        dxin, dxinb, dg = rms_bwd(xin, wts[f"{l}_ffn_norm_g"], dh, dxout, f"{l}_ffn_rms_bwd")
        g[f"{l}_ffn_norm_g"] = dg
        return dxin, dxinb

    h0 = rms_fwd(x, wts["l0_mix_norm_g"], "l0_mix_rms")
    p0 = mm_nn(h0, wts["l0_w_in"], nb, "l0_w_in_mm")
    cat0 = sb_fwd(p0, W, "l0_sb_fwd")
    cat0 = sc_fwd(p0, wts["l0_sc_conv_w"], cat0, W, "l0_sc_fwd")
    x1 = mm_nn(cat0, wts["l0_w_out"], 1, "l0_w_out_mm", out_dtype=F32, res=x, tm=S, tn=_pick(D, (512, 256, 128)))
    x2, ffn0_saved = ffn_fwd(x1, "l0")

    x2 = at("l1_w_in", x2, None)
    nh = W // HD
    h2 = rms_fwd(x2, wts["l1_mix_norm_g"], "l1_mix_rms")
    p1 = mm_nn(h2, wts["l1_w_in_main"], 1, "l1_w_in_mm")
    f = mm_nn(h2, wts["l1_w_in_f"], 1, "l1_w_f_mm", out_dtype=F32)
    bf = jnp.zeros((1, 128), F32).at[0, :nh].set(wts["l1_fox_b_f"])
    c = fox_gate_fwd(f, bf, "l1_fox_gate")
    c_heads = c[:, :nh].T
    ccol = c_heads[:, :, None]
    crow = c_heads.reshape(nh, S // _key_strip(S), _key_strip(S))
    sg_bt = wts["l1_sg_b"].T
    cat1 = sg_fwd(p1, wts["l1_sg_w"], sg_bt, wts["l1_sg_norm_g"], W, "l1_sg_fwd")
    cat1, lse = fox_fwd(p1, ccol, crow, cat1, W, "l1_fox_fwd")
    x3 = mm_nn(cat1, wts["l1_w_out"], 1, "l1_w_out_mm", out_dtype=F32, res=x2, tm=S, tn=_pick(D, (512, 256, 128)))
    x4, ffn1_saved = ffn_fwd(x3, "l1")

    dx4, dx4b, dgf, loss = loss_head(x4, wts["final_norm_g"], target, "loss_head")
    dx4b = at("loss", dx4b, loss)
    g["final_norm_g"] = dgf

    dx3, dx3b = ffn_bwd(dx4, dx4b, ffn1_saved, "l1")
    dcat1 = mm_nt(dx3b, wts["l1_w_out"], 1, S, D, "l1_w_out_dx", tn=D)
    dcat1 = on_grad("l1_w_out", mm_tn(cat1, dx3b, 1, D, "l1_w_out_dw", tn=D), dcat1)
    dp1, dsgw, dsgbt, dsgg = sg_bwd(p1, wts["l1_sg_w"], sg_bt, wts["l1_sg_norm_g"], dcat1, W, "l1_sg_bwd")
    dp1, dcs, dct = fox_bwd(p1, ccol, crow, cat1, lse, dcat1, dp1, W, "l1_fox_bwd")
    g["l1_sg_w"], g["l1_sg_b"], g["l1_sg_norm_g"] = dsgw, dsgbt.T, dsgg
    dc = jnp.zeros((S, 128), F32).at[:, :nh].set((dct[:, :, 0] - dcs.reshape(nh, S)).T)
    df, dbf = fox_gate_bwd(f, bf, dc, "l1_fox_gate_bwd")
    g["l1_fox_b_f"] = dbf[0, :nh]
    dfb = df.astype(BF16)
    tn1 = _pick(W, (1024, 512, 256, 128))
    tm1 = _pick(S, (1024, 512, 256, 128))
    per_part = W // tn1
    part_of = lambda pt: pt + pt // 2 - pt // 4

    def nt_map1(i, j, t):
        return (part_of(t // per_part) * (S // tm1) + i, t % per_part)

    def tn_map1(j, t):
        return (part_of(t // per_part), t % per_part)

    dp1_2d = dp1.reshape(6 * S, W)
    dw_main = mm_tn(h2, dp1_2d, 1, 5 * W, "l1_w_in_dw", dy_maps=[tn_map1], tn=tn1)
    dw_f = mm_tn(h2, dfb, 1, 128, "l1_w_f_dw")
    dh2 = mm_nt(dfb, wts["l1_w_in_f"], 1, S, D, "l1_w_f_dx", out_dtype=F32)
    dh2 = mm_nt(dp1_2d, wts["l1_w_in_main"], 1, S, D, "l1_w_in_dx", res=dh2, dy_maps=[nt_map1], tm=tm1, tn=tn1)
    dh2 = on_grad("l1_w_in", jnp.concatenate([dw_main, dw_f[:, :nh]], axis=1), dh2)
    dx2, dx2b, dg = rms_bwd(x2, wts["l1_mix_norm_g"], dh2, dx3, "l1_mix_rms_bwd")
    g["l1_mix_norm_g"] = dg

    dx1, dx1b = ffn_bwd(dx2, dx2b, ffn0_saved, "l0")
    dcat0 = mm_nt(dx1b, wts["l0_w_out"], 1, S, D, "l0_w_out_dx", tn=D)
    dcat0 = on_grad("l0_w_out", mm_tn(cat0, dx1b, 1, D, "l0_w_out_dw", tn=D), dcat0)
    dp0 = sb_bwd(p0, dcat0, W, "l0_sb_bwd")
    dp0, dscw = sc_bwd(p0, wts["l0_sc_conv_w"], dcat0, dp0, W, "l0_sc_bwd")
    g["l0_sc_conv_w"] = dscw
    n0 = wts["l0_w_in"].shape[1]
    td0 = math.gcd(n0, W)
    nd0 = n0 // td0
    half = S // 2
    tm0 = _pick(half, (1024, 512, 256, 128))
    per_part0 = W // td0

    def nt_maps0(k, first_block):
        def f(i, j, t):
            vb = j * nd0 + k
            return ((vb // per_part0) * (S // tm0) + first_block + i, vb % per_part0)
        return f

    def tn_maps0(k):
        def f(j, t):
            vb = j * nd0 + k
            return (vb // per_part0, vb % per_part0)
        return f

    dp0_2d = dp0.reshape(6 * S, W)
    dw0 = mm_tn(h0, dp0_2d, nb, n0, "l0_w_in_dw", dy_maps=[tn_maps0(k) for k in range(nd0)], tn=n0)
    dp0_2d = on_grad("l0_w_in", dw0, dp0_2d)
    halves = []
    for b, tag in enumerate("ab"):
        maps = [nt_maps0(k, b * (half // tm0)) for k in range(nd0)]
        halves.append(mm_nt(dp0_2d, wts["l0_w_in"], nb, half, D, f"l0_w_in_dx_{tag}", dy_maps=maps, tm=tm0, tn=n0))
        if b == 0:
            halves[0], dp0_2d = lax.optimization_barrier((halves[0], dp0_2d))
            dp0_2d = on_grad(None, None, dp0_2d)
    dh0 = jnp.concatenate(halves, axis=0)
    dx0, _, dg = rms_bwd(x, wts["l0_mix_norm_g"], dh0, dx1, "l0_mix_rms_bwd")
    g["l0_mix_norm_g"] = dg
    return dx0, g


GATHER_ID, PAIR_ID, CHIPS_ID = 1, 2, 3


def _place():
    return lax.axis_index("x"), lax.axis_index("y"), lax.axis_index("c")


def _other_chips(x, y):
    return [(x, 1 - y), (1 - x, y), (1 - x, 1 - y)]


def _handshake(peers):
    barrier = pltpu.get_barrier_semaphore()
    for peer in peers:
        pl.semaphore_signal(barrier, inc=1, device_id=peer, device_id_type=MESH)
    pl.semaphore_wait(barrier, len(peers))


UPDATE_LAG = 2


def _on_sequencer(body, out_type, scratch_types, collective_id, name):
    return pl.kernel(
        body,
        out_type=out_type,
        mesh=plsc.ScalarSubcoreMesh(axis_name="seq", num_cores=1),
        scratch_types=scratch_types,
        compiler_params=pltpu.CompilerParams(collective_id=collective_id),
        name=name,
    )


def all_gather(arrs, name):
    n = len(arrs)

    def body(*refs):
        xs, outs = refs[:n], refs[n : 2 * n]
        send_sems, recv_sems, local_sems = refs[2 * n :]
        x, y, c = _place()
        me, sibling = (x, y, c), (x, y, 1 - c)
        chips = _other_chips(x, y)
        _handshake([sibling] + [(*chip, c) for chip in chips])

        def copy(a, k, block, to, src=None):
            px, py, pc = block
            dst = outs[a].at[4 * px + 2 * py + pc]
            return pltpu.make_async_remote_copy(
                src_ref=dst if src is None else src, dst_ref=dst,
                send_sem=send_sems.at[7 * a + k], recv_sem=recv_sems.at[7 * a + k], device_id=to, device_id_type=MESH,
            )

        mine = [pltpu.make_async_copy(xs[a], outs[a].at[4 * x + 2 * y + c], local_sems.at[a]) for a in range(n)]
        for cp in mine:
            cp.start()
        first = []
        for a in range(n):
            first.append(copy(a, 0, me, sibling, src=xs[a]))
            first += [copy(a, 1 + j, me, (*chip, c), src=xs[a]) for j, chip in enumerate(chips)]
        for cp in first:
            cp.start()
        passed = []
        for a in range(n):
            for j, chip in enumerate(chips):
                copy(a, 1 + j, (*chip, c), me).wait_recv()
                cp = copy(a, 4 + j, (*chip, c), sibling)
                cp.start()
                passed.append(cp)
        for a in range(n):
            copy(a, 0, sibling, me).wait_recv()
            for j, chip in enumerate(chips):
                copy(a, 4 + j, (*chip, 1 - c), me).wait_recv()
        for cp in first + passed:
            cp.wait_send()
        for cp in mine:
            cp.wait()

    out_type = [jax.ShapeDtypeStruct((NDEV,) + a.shape, a.dtype) for a in arrs]
    sems = [pltpu.SemaphoreType.DMA((7 * n,)), pltpu.SemaphoreType.DMA((7 * n,)), pltpu.SemaphoreType.DMA((n,))]
    return _on_sequencer(body, out_type, sems, GATHER_ID, name)(*arrs)


_IN_HBM = pl.BlockSpec(memory_space=pltpu.HBM)
_IN_SEM = pl.BlockSpec(memory_space=pltpu.SEMAPHORE)
_EFFECT = pltpu.SideEffectType.DATAFLOW_SIDE_EFFECTING


def _split_start(make_copies, src, land_shape, nsem, name):
    def body(src_ref, land_ref, send_sems, recv_sems, land_thru, token):
        for cp in make_copies(src_ref, land_ref, send_sems, recv_sems):
            cp.start()
        token[...] = jnp.zeros_like(token)

    send_sems, recv_sems, land_thru, token = pl.pallas_call(
        body,
        name=name,
        out_shape=(
            pltpu.SemaphoreType.DMA((nsem,)), pltpu.SemaphoreType.DMA((nsem,)),
            pltpu.HBM(land_shape, src.dtype), jax.ShapeDtypeStruct((8, 128), F32),
        ),
        in_specs=(_IN_HBM, _IN_HBM),
        out_specs=(_IN_SEM, _IN_SEM, _IN_HBM, pl.BlockSpec(memory_space=pltpu.VMEM)),
        input_output_aliases={1: 2},
        compiler_params=pltpu.CompilerParams(has_side_effects=_EFFECT),
    )(src, pltpu.with_memory_space_constraint(lax.empty(land_shape, src.dtype), pltpu.HBM))
    return send_sems, recv_sems, src, land_thru, token


def _split_wait(make_copies, send_sems, recv_sems, src_thru, land_thru, after, name):
    def body(src_ref, land_ref, send_sems, recv_sems, after_ref, land_out):
        for cp in make_copies(src_ref, land_ref, send_sems, recv_sems):
            cp.wait_send()
            cp.wait_recv()

    return pl.pallas_call(
        body,
        name=name,
        out_shape=pltpu.HBM(land_thru.shape, land_thru.dtype),
        in_specs=(_IN_HBM, _IN_HBM, _IN_SEM, _IN_SEM, pl.BlockSpec(memory_space=pl.ANY)),
        out_specs=_IN_HBM,
        input_output_aliases={1: 0},
        compiler_params=pltpu.CompilerParams(has_side_effects=_EFFECT),
    )(src_thru, land_thru, send_sems, recv_sems, after)


def _pair_copies(src_ref, land_ref, send_sems, recv_sems):
    x, y, c = _place()
    return [
        pltpu.make_async_remote_copy(
            src_ref=src_ref.at[k, 1 - c], dst_ref=land_ref.at[k],
            send_sem=send_sems.at[k], recv_sem=recv_sems.at[k], device_id=(x, y, 1 - c), device_id_type=MESH,
        )
        for k in range(4)
    ]


def _chip_copies(src_ref, land_ref, send_sems, recv_sems):
    x, y, c = _place()
    return [
        pltpu.make_async_remote_copy(
            src_ref=src_ref.at[2 * px + py], dst_ref=land_ref.at[2 * x + y],
            send_sem=send_sems.at[j], recv_sem=recv_sems.at[j], device_id=(px, py, c), device_id_type=MESH,
        )
        for j, (px, py) in enumerate(_other_chips(x, y))
    ]


def _row_tile(R, C, max_elems):
    if R * C <= max_elems:
        return R
    best = None
    for tr in range(16, R, 16):
        if R % tr == 0 and tr * C <= max_elems:
            best = tr
    return best or R


def pair_sum(a42, land4, core, name):
    _, _, R, C = a42.shape
    tr = _row_tile(R, C, 1 << 20)

    def body(core_ref, a_ref, l_ref, o_ref):
        o_ref[...] = (a_ref[0].astype(F32) + l_ref[...].astype(F32)).astype(o_ref.dtype)

    return pl.pallas_call(
        body,
        grid_spec=pltpu.PrefetchScalarGridSpec(
            num_scalar_prefetch=1,
            grid=(4, R // tr),
            in_specs=[
                pl.BlockSpec((1, 1, tr, C), lambda k, r, core_ref: (k, core_ref[0], r, 0)),
                pl.BlockSpec((1, tr, C), lambda k, r, core_ref: (k, r, 0)),
            ],
            out_specs=pl.BlockSpec((1, tr, C), lambda k, r, core_ref: (k, r, 0)),
        ),
        out_shape=jax.ShapeDtypeStruct((4, R, C), BF16),
        compiler_params=_cp(("parallel", "parallel")),
        name=name,
    )(core, a42, land4)


def sum_slots(parts, name):
    P, R, C = parts.shape

    def body(p_ref, o_ref):
        acc = p_ref[0].astype(F32)
        for k in range(1, P):
            acc = acc + p_ref[k].astype(F32)
        o_ref[...] = acc

    tr = _row_tile(R, P * C, 1 << 21)
    return pl.pallas_call(
        body,
        grid=(R // tr,),
        in_specs=[pl.BlockSpec((P, tr, C), lambda r: (0, r, 0))],
        out_specs=pl.BlockSpec((tr, C), lambda r: (r, 0)),
        out_shape=jax.ShapeDtypeStruct((R, C), F32),
        compiler_params=_cp(("parallel",)),
        name=name,
    )(parts)


def adamw(w, m, v, parts, name):
    R, C = w.shape
    P = parts.shape[0]
    tr = _pick(R, (256, 128, 64, 32, 16, 8))
    c1 = 1.0 - ADAM_B1 ** ADAM_STEP
    c2 = 1.0 - ADAM_B2 ** ADAM_STEP

    def body(w_ref, m_ref, v_ref, p_ref, g_ref, d_ref, nm_ref, nv_ref):
        g = p_ref[0].astype(F32)
        for k in range(1, P):
            g = g + p_ref[k].astype(F32)
        nm = ADAM_B1 * m_ref[...] + (1.0 - ADAM_B1) * g
        nv = ADAM_B2 * v_ref[...] + (1.0 - ADAM_B2) * (g * g)
        g_ref[...] = g
        nm_ref[...] = nm
        nv_ref[...] = nv
        d_ref[...] = -ADAM_LR * ((nm / c1) / (jnp.sqrt(nv / c2) + ADAM_EPS) + ADAM_WD * w_ref[...])

    blk = pl.BlockSpec((tr, C), lambda r: (r, 0))
    shp = jax.ShapeDtypeStruct((R, C), F32)
    return pl.pallas_call(
        body,
        grid=(R // tr,),
        in_specs=[blk, blk, blk, pl.BlockSpec((P, tr, C), lambda r: (0, r, 0))],
        out_specs=[blk, blk, blk, blk],
        out_shape=[shp, shp, shp, shp],
        compiler_params=_cp(("parallel",)),
        name=name,
    )(w, m, v, parts)


def adamw_reduced(w, m, v, own, land, chip, name):
    R, C = w.shape
    tr = _pick(R, (256, 128, 64, 32, 16, 8))
    c1 = 1.0 - ADAM_B1 ** ADAM_STEP
    c2 = 1.0 - ADAM_B2 ** ADAM_STEP

    def body(chip_ref, w_ref, m_ref, v_ref, own_ref, land_ref, g_ref, d_ref, nm_ref, nv_ref):
        mine = own_ref[0].astype(F32)
        g = None
        for k in range(4):
            term = jnp.where(chip_ref[0] == k, mine, land_ref[k].astype(F32))
            g = term if g is None else g + term
        nm = ADAM_B1 * m_ref[...] + (1.0 - ADAM_B1) * g
        nv = ADAM_B2 * v_ref[...] + (1.0 - ADAM_B2) * (g * g)
        g_ref[...] = g
        nm_ref[...] = nm
        nv_ref[...] = nv
        d_ref[...] = -ADAM_LR * ((nm / c1) / (jnp.sqrt(nv / c2) + ADAM_EPS) + ADAM_WD * w_ref[...])

    blk = pl.BlockSpec((tr, C), lambda r, chip_ref: (r, 0))
    shp = jax.ShapeDtypeStruct((R, C), F32)
    return pl.pallas_call(
        body,
        grid_spec=pltpu.PrefetchScalarGridSpec(
            num_scalar_prefetch=1,
            grid=(R // tr,),
            in_specs=[
                blk, blk, blk,
                pl.BlockSpec((1, tr, C), lambda r, chip_ref: (chip_ref[0], r, 0)),
                pl.BlockSpec((4, tr, C), lambda r, chip_ref: (0, r, 0)),
            ],
            out_specs=[blk, blk, blk, blk],
        ),
        out_shape=[shp, shp, shp, shp],
        compiler_params=_cp(("parallel",)),
        name=name,
    )(chip, w, m, v, own, land)


_WEIGHTS = [
    "l0_mix_norm_g", "l0_w_in", "l0_sc_conv_w", "l0_w_out", "l0_ffn_norm_g", "l0_ffn_up", "l0_ffn_conv_w", "l0_ffn_down",
    "l1_mix_norm_g", "l1_w_in", "l1_fox_b_f", "l1_sg_w", "l1_sg_b", "l1_sg_norm_g", "l1_w_out", "l1_ffn_norm_g",
    "l1_ffn_up", "l1_ffn_conv_w", "l1_ffn_down", "final_norm_g",
]
_COL_SHARDED = ["l0_w_in", "l0_ffn_up", "l1_w_in", "l1_ffn_up"]
_ROW_SHARDED = ["l0_w_out", "l0_ffn_down", "l1_w_out", "l1_ffn_down"]
_BIG = ["l0_w_in", "l0_w_out", "l0_ffn_up", "l0_ffn_down", "l1_w_in", "l1_w_out", "l1_ffn_up", "l1_ffn_down"]
_CONV = ["l0_sc_conv_w", "l0_ffn_conv_w", "l1_ffn_conv_w"]
_SMALL = [n for n in _WEIGHTS if n not in _BIG]
_PACK_ROWS = 8


def _pack(arrs):
    flat = []
    for a in arrs:
        v = a.reshape(-1).astype(F32)
        pad = (-v.shape[0]) % (_PACK_ROWS * 128)
        flat.append(jnp.pad(v, (0, pad)))
    return jnp.concatenate(flat).reshape(-1, 128)


def _unpack(packed, shapes):
    out, off = [], 0
    flat = packed.reshape(-1)
    for shp in shapes:
        size = math.prod(shp)
        out.append(flat[off : off + size].reshape(shp))
        off += size + (-size) % (_PACK_ROWS * 128)
    return out


def kernel(x, l0_mix_norm_g, l0_w_in, l0_sc_conv_w, l0_w_out, l0_ffn_norm_g, l0_ffn_up, l0_ffn_conv_w, l0_ffn_down, l1_mix_norm_g, l1_w_in, l1_fox_b_f, l1_sg_w, l1_sg_b, l1_sg_norm_g, l1_w_out, l1_ffn_norm_g, l1_ffn_up, l1_ffn_conv_w, l1_ffn_down, final_norm_g, loss_target, m_l0_mix_norm_g, m_l0_w_in, m_l0_sc_conv_w, m_l0_w_out, m_l0_ffn_norm_g, m_l0_ffn_up, m_l0_ffn_conv_w, m_l0_ffn_down, m_l1_mix_norm_g, m_l1_w_in, m_l1_fox_b_f, m_l1_sg_w, m_l1_sg_b, m_l1_sg_norm_g, m_l1_w_out, m_l1_ffn_norm_g, m_l1_ffn_up, m_l1_ffn_conv_w, m_l1_ffn_down, m_final_norm_g, v_l0_mix_norm_g, v_l0_w_in, v_l0_sc_conv_w, v_l0_w_out, v_l0_ffn_norm_g, v_l0_ffn_up, v_l0_ffn_conv_w, v_l0_ffn_down, v_l1_mix_norm_g, v_l1_w_in, v_l1_fox_b_f, v_l1_sg_w, v_l1_sg_b, v_l1_sg_norm_g, v_l1_w_out, v_l1_ffn_norm_g, v_l1_ffn_up, v_l1_ffn_conv_w, v_l1_ffn_down, v_final_norm_g):
    given = dict(locals())
    w = {n: given[n] for n in _WEIGHTS}
    mom = {n: given["m_" + n] for n in _WEIGHTS}
    var = {n: given["v_" + n] for n in _WEIGHTS}
    xs, target = x[0], loss_target[0]
    S, D = xs.shape
    W = D // 2
    nh = W // HD
    cx, cy, cc = _place()
    me = 4 * cx + 2 * cy + cc

    wts = {"nb": NDEV, "F": l0_ffn_down.shape[0] * NDEV}
    for n in _SMALL:
        if n not in _CONV:
            wts[n] = w[n]
    gathered, loss_sum = {}, []

    def start_gather(n):
        got = all_gather([w[n].astype(BF16)] + ([w[c] for c in _CONV] if n == _BIG[0] else []), f"gather_{n}")
        if n == "l1_w_in":
            gathered[n] = got[0]
        elif n in _ROW_SHARDED:
            wts[n] = got[0].reshape(-1, D)
        else:
            wts[n] = got[0].reshape(NDEV * D, -1)
        for c, taps in zip(_CONV, got[1:]):
            wts[c] = taps.transpose(1, 0, 2).reshape(CONV_K, -1)

    def at(point, after, value):
        if point == "l1_w_in":
            got, after = lax.optimization_barrier((gathered[point], after))
            w_in1 = got.transpose(1, 0, 2).reshape(D, -1)
            wts["l1_w_in_main"] = w_in1[:, : 5 * W]
            wts["l1_w_in_f"] = jnp.pad(w_in1[:, 5 * W :], ((0, 0), (0, 128 - nh)))
        elif point == "loss":
            total, after = lax.optimization_barrier((lax.psum(value[0, 0], ("x", "y", "c")), after))
            loss_sum.append(total)
        return after

    core = jnp.reshape(cc, (1,)).astype(jnp.int32)
    chip = jnp.reshape(2 * cx + cy, (1,)).astype(jnp.int32)
    pair_flying, chip_flying = [], []
    out_g, out_d, out_m, out_v = {}, {}, {}, {}

    def tie(value, after):
        if after is None:
            return value, None
        return lax.optimization_barrier((value, after))

    def to_chips(after):
        n, flying = pair_flying.pop()
        landed = _split_wait(_pair_copies, *flying, f"reduce_pair_wait_{n}")
        summed = pair_sum(flying[2], landed, core, f"pair_sum_{n}")
        *flying, token = _split_start(_chip_copies, summed, summed.shape, 3, f"reduce_chips_{n}")
        token, after = tie(token, after)
        chip_flying.append((n, flying + [token]))
        return after

    def update(after, behind=None):
        n, flying = chip_flying.pop(0)
        if behind is not None:
            flying[4], _ = lax.optimization_barrier((flying[4], behind))
        landed = _split_wait(_chip_copies, *flying, f"reduce_chips_wait_{n}")
        res = adamw_reduced(w[n], mom[n], var[n], flying[2], landed, chip, f"adamw_{n}")
        res, after = tie(res, after)
        out_g[n], out_d[n], out_m[n], out_v[n] = res
        return after, res[0]

    def on_grad(n, term, after):
        if n is None:
            return to_chips(after)
        if n == "l1_w_in":
            term = term.reshape(D, NDEV, -1).transpose(1, 0, 2)
        elif n in _ROW_SHARDED:
            term = term.reshape(NDEV, -1, D)
        else:
            term = term.reshape(NDEV, D, -1)
        term = term.reshape((4, 2) + term.shape[1:])
        *flying, token = _split_start(_pair_copies, term, term.shape[:1] + term.shape[2:], 4, f"reduce_pair_{n}")
        token, after = tie(token, after)
        if len(chip_flying) == UPDATE_LAG:
            after, _ = update(after)
        if pair_flying:
            after = to_chips(after)
        pair_flying.append((n, flying + [token]))
        return after

    for n in _BIG:
        start_gather(n)
    dx, g = local_step(xs, target, wts, at, on_grad)
    done = None
    while chip_flying:
        _, done = update(None, behind=done)
    loss = loss_sum[0]

    small_terms = [g[n] for n in _SMALL]
    small_shapes = [tuple(t.shape) for t in small_terms]
    packed = _pack(small_terms)
    all_terms = all_gather([packed], "gather_small_grads")[0]
    small_sum = _unpack(sum_slots(all_terms, "sum_small_grads"), small_shapes)
    small_g = {}
    for n, t in zip(_SMALL, small_sum):
        if n in _CONV:
            cols = w[n].shape[1]
            t = lax.dynamic_slice_in_dim(t, me * cols, cols, axis=1)
        small_g[n] = t.reshape(w[n].shape)
    shapes = [w[n].shape for n in _SMALL]
    res = adamw(
        _pack([w[n] for n in _SMALL]), _pack([mom[n] for n in _SMALL]), _pack([var[n] for n in _SMALL]),
        _pack([small_g[n] for n in _SMALL])[None], "adamw_small",
    )
    for dst, packed_out in zip((out_g, out_d, out_m, out_v), res):
        for n, t in zip(_SMALL, _unpack(packed_out, shapes)):
            dst[n] = t

    return (loss, dx[None], *[out_g[n] for n in _WEIGHTS], *[out_d[n] for n in _WEIGHTS],
            *[out_m[n] for n in _WEIGHTS], *[out_v[n] for n in _WEIGHTS])
```

```python
import functools
import math

import jax
import jax.numpy as jnp
from jax import lax
from jax.experimental import pallas as pl
from jax.experimental.pallas import tpu as pltpu
from jax.experimental.pallas import tpu_sc as plsc

F32 = jnp.float32
BF16 = jnp.bfloat16
HD = 128
EPS = 1e-6
CONV_K = 3
VMEM_LIMIT_BYTES = 48 << 20
NDEV = 8
MESH = pl.DeviceIdType.MESH

ADAM_LR = 0.001
ADAM_B1 = 0.9
ADAM_B2 = 0.999
ADAM_EPS = 1e-08
ADAM_WD = 0.01
ADAM_STEP = 10


def _cp(sem):
    return pltpu.CompilerParams(dimension_semantics=sem, vmem_limit_bytes=VMEM_LIMIT_BYTES)


def _pick(n, prefs):
    for p in prefs:
        if n % p == 0:
            return p
    return n


def _dot(a, b):
    return jnp.dot(a, b, preferred_element_type=F32)


def _dot_nt(a, b):
    return lax.dot_general(a, b, (((1,), (1,)), ((), ())), preferred_element_type=F32)


def _dot_tn(a, b):
    return lax.dot_general(a, b, (((0,), (0,)), ((), ())), preferred_element_type=F32)


def _split3(x):
    hi = x.astype(BF16)
    r = x - hi.astype(F32)
    mid = r.astype(BF16)
    lo = (r - mid.astype(F32)).astype(BF16)
    return hi, mid, lo


def _dot_ones_right(x, ones_bf16):
    hi, mid, lo = _split3(x)
    return _dot(hi, ones_bf16) + _dot(mid, ones_bf16) + _dot(lo, ones_bf16)


def _dot_ones_left(ones_bf16, x):
    hi, mid, lo = _split3(x)
    return _dot(ones_bf16, hi) + _dot(ones_bf16, mid) + _dot(ones_bf16, lo)


def _iota2(shape, axis):
    return lax.broadcasted_iota(jnp.int32, shape, axis)


def mm_nn(a, w2d, nb, name, out_dtype=BF16, res=None, tm=None, tn=None, tk=None, a_map=None, a_shape=None):
    M, K = a_shape or a.shape
    n = w2d.shape[1]
    assert w2d.shape[0] == nb * K or (nb == 1 and w2d.shape[0] > K)
    a_map = a_map or (lambda i, k: (i, k))
    tm = tm or _pick(M, (1024, 512, 256, 128))
    tn = tn or _pick(n, (1408, 1024, 768, 512, 256, 128))
    tk = tk or (K if K <= 2048 else _pick(K, (1408, 1024, 512, 256, 128)))
    nk, nt = K // tk, n // tn
    has_res = res is not None

    def body(*refs):
        if has_res:
            a_ref, w_ref, r_ref, o_ref = refs[:4]
        else:
            a_ref, w_ref, o_ref = refs[:3]
            r_ref = None
        part = _dot(a_ref[...], w_ref[...])

        def finish(acc):
            if r_ref is not None:
                acc = acc + r_ref[...].astype(F32)
            o_ref[...] = acc.astype(o_ref.dtype)

        if nk == 1:
            finish(part)
        else:
            acc_ref = refs[-1]
            k = pl.program_id(3)

            @pl.when(k == 0)
            def _():
                acc_ref[...] = part

            @pl.when(k > 0)
            def _():
                acc_ref[...] += part

            @pl.when(k == nk - 1)
            def _():
                finish(acc_ref[...])

    in_specs = [
        pl.BlockSpec((tm, tk), lambda i, j, t, k: a_map(i, k)),
        pl.BlockSpec((tk, tn), lambda i, j, t, k: (j * nk + k, t)),
    ]
    args = [a, w2d]
    out_spec = pl.BlockSpec((tm, tn), lambda i, j, t, k: (i, j * nt + t))
    if has_res:
        in_specs.append(out_spec)
        args.append(res)
    return pl.pallas_call(
        body,
        grid=(M // tm, nb, nt, nk),
        in_specs=in_specs,
        out_specs=out_spec,
        out_shape=jax.ShapeDtypeStruct((M, nb * n), out_dtype),
        scratch_shapes=[pltpu.VMEM((tm, tn), F32)] if nk > 1 else [],
        compiler_params=_cp(("parallel", "parallel", "parallel", "arbitrary")),
        name=name,
    )(*args)


def mm_nt(dy2d, w2d, nb, M, K, name, out_dtype=BF16, res=None, dy_maps=None, tm=None, tko=None, tn=None):
    n = w2d.shape[1]
    assert w2d.shape[0] == nb * K or (nb == 1 and w2d.shape[0] > K)
    tm = tm or _pick(M, (1024, 512, 256, 128))
    tko = tko or _pick(K, (1024, 512, 256, 128))
    tn = tn or _pick(n, (1408, 1024, 768, 512, 256, 128))
    nt, nko = n // tn, K // tko
    has_res = res is not None
    if dy_maps is None:
        dy_maps = [lambda i, j, t: (i, j * nt + t)]
    nd = len(dy_maps)
    td = tn // nd

    one_step = nb * nt == 1

    def body(*refs):
        d_refs, w_ref = refs[:nd], refs[nd]
        r_ref = refs[nd + 1] if has_res else None
        d = d_refs[0][...] if nd == 1 else jnp.concatenate([r[...] for r in d_refs], axis=1)
        part = _dot_nt(d, w_ref[...])
        if one_step:
            o_ref = refs[-1]
            if r_ref is not None:
                part = part + r_ref[...].astype(F32)
            o_ref[...] = part.astype(o_ref.dtype)
            return
        o_ref, acc_ref = refs[-2], refs[-1]
        j, t = pl.program_id(2), pl.program_id(3)
        first = jnp.logical_and(j == 0, t == 0)
        last = jnp.logical_and(j == nb - 1, t == nt - 1)

        @pl.when(first)
        def _():
            acc_ref[...] = part

        @pl.when(jnp.logical_not(first))
        def _():
            acc_ref[...] += part

        @pl.when(last)
        def _():
            acc = acc_ref[...]
            if r_ref is not None:
                acc = acc + r_ref[...].astype(F32)
            o_ref[...] = acc.astype(o_ref.dtype)

    in_specs = [pl.BlockSpec((tm, td), functools.partial(lambda f, i, ko, j, t: f(i, j, t), f)) for f in dy_maps]
    in_specs.append(pl.BlockSpec((tko, tn), lambda i, ko, j, t: (j * nko + ko, t)))
    args = [dy2d] * nd + [w2d]
    out_spec = pl.BlockSpec((tm, tko), lambda i, ko, j, t: (i, ko))
    if has_res:
        in_specs.append(out_spec)
        args.append(res)
    return pl.pallas_call(
        body,
        grid=(M // tm, nko, nb, nt),
        in_specs=in_specs,
        out_specs=out_spec,
        out_shape=jax.ShapeDtypeStruct((M, K), out_dtype),
        scratch_shapes=[] if one_step else [pltpu.VMEM((tm, tko), F32)],
        compiler_params=_cp(("parallel", "parallel", "arbitrary", "arbitrary")),
        name=name,
    )(*args)


def mm_tn(x, dy2d, nb, n, name, out_dtype=BF16, dy_maps=None, tko=None, tn=None, x_map=None, x_shape=None):
    S, K = x_shape or x.shape
    x_map = x_map or (lambda ko: (0, ko))
    tko = tko or _pick(K, (512, 256, 128))
    tn = tn or _pick(n, (1408, 1024, 768, 512, 256, 128))
    nt, nko = n // tn, K // tko
    if dy_maps is None:
        dy_maps = [lambda j, t: (0, j * nt + t)]
    nd = len(dy_maps)
    td = tn // nd

    def body(*refs):
        x_ref, d_refs, o_ref = refs[0], refs[1 : 1 + nd], refs[-1]
        d = d_refs[0][...] if nd == 1 else jnp.concatenate([r[...] for r in d_refs], axis=1)
        o_ref[...] = _dot_tn(x_ref[...], d).astype(o_ref.dtype)

    in_specs = [pl.BlockSpec((S, tko), lambda ko, j, t: x_map(ko))]
    in_specs += [pl.BlockSpec((S, td), functools.partial(lambda f, ko, j, t: f(j, t), f)) for f in dy_maps]
    return pl.pallas_call(
        body,
        grid=(nko, nb, nt),
        in_specs=in_specs,
        out_specs=pl.BlockSpec((tko, tn), lambda ko, j, t: (j * nko + ko, t)),
        out_shape=jax.ShapeDtypeStruct((nb * K, n), out_dtype),
        compiler_params=_cp(("parallel", "parallel", "parallel")),
        name=name,
    )(x, *([dy2d] * nd))


def rms_fwd(x, g, name):
    S, D = x.shape
    tm = _pick(S, (256, 128))

    def body(x_ref, g_ref, o_ref):
        xf = x_ref[...]
        r = lax.rsqrt(jnp.mean(xf * xf, axis=-1, keepdims=True) + EPS)
        o_ref[...] = (xf * r * g_ref[...]).astype(o_ref.dtype)

    return pl.pallas_call(
        body,
        grid=(S // tm,),
        in_specs=[pl.BlockSpec((tm, D), lambda i: (i, 0)), pl.BlockSpec((1, D), lambda i: (0, 0))],
        out_specs=pl.BlockSpec((tm, D), lambda i: (i, 0)),
        out_shape=jax.ShapeDtypeStruct((S, D), BF16),
        compiler_params=_cp(("parallel",)),
        name=name,
    )(x, g.reshape(1, D))


def rms_bwd(x, g, dh, dres, name):
    S, D = x.shape
    tm = _pick(S, (256, 128))

    def body(x_ref, g_ref, dh_ref, dr_ref, dx_ref, dxb_ref, dg_ref):
        i = pl.program_id(0)
        xf = x_ref[...]
        dh = dh_ref[...].astype(F32)
        r = lax.rsqrt(jnp.mean(xf * xf, axis=-1, keepdims=True) + EPS)
        gy = dh * g_ref[...]
        proj = jnp.mean(gy * xf, axis=-1, keepdims=True)
        dx = dr_ref[...] + r * gy - xf * (r * r * r * proj)
        dx_ref[...] = dx
        dxb_ref[...] = dx.astype(BF16)
        dg = jnp.sum(dh * (xf * r), axis=0, keepdims=True)

        @pl.when(i == 0)
        def _():
            dg_ref[...] = dg

        @pl.when(i > 0)
        def _():
            dg_ref[...] += dg

    row = pl.BlockSpec((tm, D), lambda i: (i, 0))
    vec = pl.BlockSpec((1, D), lambda i: (0, 0))
    return pl.pallas_call(
        body,
        grid=(S // tm,),
        in_specs=[row, vec, row, row],
        out_specs=[row, row, vec],
        out_shape=[jax.ShapeDtypeStruct((S, D), F32), jax.ShapeDtypeStruct((S, D), BF16), jax.ShapeDtypeStruct((1, D), F32)],
        compiler_params=_cp(("arbitrary",)),
        name=name,
    )(x, g.reshape(1, D), dh, dres)


def loss_head(x, g, target, name):
    S, D = x.shape
    tm = _pick(S, (256, 128))

    def body(x_ref, g_ref, t_ref, dx_ref, dxb_ref, dg_ref, loss_ref):
        i = pl.program_id(0)
        xf = x_ref[...]
        gg = g_ref[...]
        r = lax.rsqrt(jnp.mean(xf * xf, axis=-1, keepdims=True) + EPS)
        xh = xf * r
        err = xh * gg - t_ref[...]
        part = (0.5 / D) * jnp.sum(err * err)
        dy = err * (1.0 / D)
        gy = dy * gg
        proj = jnp.mean(gy * xf, axis=-1, keepdims=True)
        dx = r * gy - xf * (r * r * r * proj)
        dx_ref[...] = dx
        dxb_ref[...] = dx.astype(BF16)
        dg = jnp.sum(dy * xh, axis=0, keepdims=True)
        lossb = jnp.full(loss_ref.shape, part, F32)

        @pl.when(i == 0)
        def _():
            dg_ref[...] = dg
            loss_ref[...] = lossb

        @pl.when(i > 0)
        def _():
            dg_ref[...] += dg
            loss_ref[...] += lossb

    row = pl.BlockSpec((tm, D), lambda i: (i, 0))
    vec = pl.BlockSpec((1, D), lambda i: (0, 0))
    return pl.pallas_call(
        body,
        grid=(S // tm,),
        in_specs=[row, vec, row],
        out_specs=[row, row, vec, pl.BlockSpec((8, 128), lambda i: (0, 0))],
        out_shape=[
            jax.ShapeDtypeStruct((S, D), F32),
            jax.ShapeDtypeStruct((S, D), BF16),
            jax.ShapeDtypeStruct((1, D), F32),
            jax.ShapeDtypeStruct((8, 128), F32),
        ],
        compiler_params=_cp(("arbitrary",)),
        name=name,
    )(x, g.reshape(1, D), target)


def _shift_down(s, k):
    if k == 0:
        return s
    return jnp.where(_iota2(s.shape, 0) >= k, pltpu.roll(s, k, axis=0), 0.0)


def _shift_up(s, k):
    if k == 0:
        return s
    n = s.shape[0]
    return jnp.where(_iota2(s.shape, 0) < n - k, pltpu.roll(s, n - k, axis=0), 0.0)


def _conv(s, w):
    return w[0:1] * _shift_down(s, 2) + w[1:2] * _shift_down(s, 1) + w[2:3] * s


def _conv_t(d, w):
    return w[2:3] * d + w[1:2] * _shift_up(d, 1) + w[0:1] * _shift_up(d, 2)


def _conv_dw(d, s):
    return [jnp.sum(d * _shift_down(s, CONV_K - 1 - k), axis=0, keepdims=True) for k in range(CONV_K)]


def sc_fwd(p, convw, cat, W, name):
    S = p.shape[0]
    tc = _pick(W, (256, 128))
    nc = W // tc

    def body(gb_ref, gc_ref, hi_ref, w_ref, cat_ref, o_ref):
        s = gc_ref[...].astype(F32) * hi_ref[...].astype(F32)
        o_ref[...] = (gb_ref[...].astype(F32) * _conv(s, w_ref[...])).astype(o_ref.dtype)

    col = lambda part: pl.BlockSpec((S, tc), lambda c: (0, part * nc + c))
    return pl.pallas_call(
        body,
        grid=(nc,),
        in_specs=[col(3), col(4), col(5), pl.BlockSpec((CONV_K, tc), lambda c: (0, c)), pl.BlockSpec(memory_space=pl.ANY)],
        out_specs=col(1),
        out_shape=jax.ShapeDtypeStruct(cat.shape, cat.dtype),
        input_output_aliases={4: 0},
        compiler_params=_cp(("parallel",)),
        name=name,
    )(p, p, p, convw, cat)


def sc_bwd(p, convw, dcat, dp, W, name):
    S = p.shape[0]
    tc = _pick(W, (256, 128))
    nc = W // tc

    def body(gb_ref, gc_ref, hi_ref, w_ref, do_ref, dp_in_ref, dp_ref, dw_ref):
        gb = gb_ref[...].astype(F32)
        gc = gc_ref[...].astype(F32)
        hi = hi_ref[...].astype(F32)
        w = w_ref[...]
        do = do_ref[...].astype(F32)
        s = gc * hi
        dcs = do * gb
        ds = _conv_t(dcs, w)
        dp_ref[0] = (do * _conv(s, w)).astype(dp_ref.dtype)
        dp_ref[1] = (ds * hi).astype(dp_ref.dtype)
        dp_ref[2] = (ds * gc).astype(dp_ref.dtype)
        for k, row in enumerate(_conv_dw(dcs, s)):
            dw_ref[k : k + 1, :] = row

    col = lambda part: pl.BlockSpec((S, tc), lambda c: (0, part * nc + c))
    return pl.pallas_call(
        body,
        grid=(nc,),
        in_specs=[
            col(3), col(4), col(5),
            pl.BlockSpec((CONV_K, tc), lambda c: (0, c)),
            pl.BlockSpec((S, tc), lambda c: (0, nc + c)),
            pl.BlockSpec(memory_space=pl.ANY),
        ],
        out_specs=[pl.BlockSpec((3, S, tc), lambda c: (1, 0, c)), pl.BlockSpec((CONV_K, tc), lambda c: (0, c))],
        out_shape=[jax.ShapeDtypeStruct(dp.shape, dp.dtype), jax.ShapeDtypeStruct((CONV_K, W), F32)],
        input_output_aliases={5: 0},
        compiler_params=_cp(("parallel",)),
        name=name,
    )(p, p, p, convw, dcat, dp)


def _silu_parts(a):
    sig = 1.0 / (1.0 + jnp.exp(-a))
    return a * sig, sig


def ffn_act_fwd(u, convw, F, name):
    S = u.shape[0]
    tc = _pick(F, (256, 128))
    nc = F // tc

    def body(ug_ref, uu_ref, wg_ref, wu_ref, o_ref):
        ag = _conv(ug_ref[...].astype(F32), wg_ref[...])
        au = _conv(uu_ref[...].astype(F32), wu_ref[...])
        o_ref[...] = (_silu_parts(ag)[0] * au).astype(o_ref.dtype)

    col = lambda half: pl.BlockSpec((S, tc), lambda c: (0, half * nc + c))
    wcol = lambda half: pl.BlockSpec((CONV_K, tc), lambda c: (0, half * nc + c))
    return pl.pallas_call(
        body,
        grid=(nc,),
        in_specs=[col(0), col(1), wcol(0), wcol(1)],
        out_specs=pl.BlockSpec((S, tc), lambda c: (0, c)),
        out_shape=jax.ShapeDtypeStruct((S, F), BF16),
        compiler_params=_cp(("parallel",)),
        name=name,
    )(u, u, convw, convw)


def ffn_act_bwd(u, convw, dact, F, name):
    S = u.shape[0]
    tc = _pick(F, (256, 128))
    nc = F // tc

    def body(ug_ref, uu_ref, wg_ref, wu_ref, da_ref, du_ref, dw_ref):
        ug = ug_ref[...].astype(F32)
        uu = uu_ref[...].astype(F32)
        wg = wg_ref[...]
        wu = wu_ref[...]
        da = da_ref[...].astype(F32)
        ag = _conv(ug, wg)
        au = _conv(uu, wu)
        sl, sig = _silu_parts(ag)
        dag = da * au * (sig * (1.0 + ag * (1.0 - sig)))
        dau = da * sl
        du_ref[0] = _conv_t(dag, wg).astype(du_ref.dtype)
        du_ref[1] = _conv_t(dau, wu).astype(du_ref.dtype)
        for k, (rg, ru) in enumerate(zip(_conv_dw(dag, ug), _conv_dw(dau, uu))):
            dw_ref[0, k : k + 1, :] = rg
            dw_ref[1, k : k + 1, :] = ru

    col = lambda half: pl.BlockSpec((S, tc), lambda c: (0, half * nc + c))
    wcol = lambda half: pl.BlockSpec((CONV_K, tc), lambda c: (0, half * nc + c))
    return pl.pallas_call(
        body,
        grid=(nc,),
        in_specs=[col(0), col(1), wcol(0), wcol(1), pl.BlockSpec((S, tc), lambda c: (0, c))],
        out_specs=[pl.BlockSpec((2, S, tc), lambda c: (0, 0, c)), pl.BlockSpec((2, CONV_K, tc), lambda c: (0, 0, c))],
        out_shape=[jax.ShapeDtypeStruct((2, S, F), BF16), jax.ShapeDtypeStruct((2, CONV_K, F), F32)],
        compiler_params=_cp(("parallel",)),
        name=name,
    )(u, u, convw, convw, dact)


def _softplus(z):
    return jnp.maximum(z, 0.0) + jnp.log(1.0 + jnp.exp(-jnp.abs(z)))


def _key_strip(S):
    return _pick(S, (512, 256, 128))


def _query_rows(S):
    return _pick(S, (512, 256, 128))


def _split2(x):
    hi = x.astype(BF16)
    return hi, (x - hi.astype(F32)).astype(BF16)


def _block_sums(x, ones_bf16):
    hi, lo = _split2(x)
    return [
        _dot(hi[:, b * HD : (b + 1) * HD], ones_bf16) + _dot(lo[:, b * HD : (b + 1) * HD], ones_bf16)
        for b in range(x.shape[1] // HD)
    ]


def _strip_mask(shape, row0, off, strict):
    cols, rows = _iota2(shape, 1) + off, _iota2(shape, 0) + row0
    return cols < rows if strict else cols <= rows


def _sb_strip(q, ks, row0, off, run, su):
    z = _dot_nt(q, ks) * (HD ** -0.5)
    mask = _strip_mask(z.shape, row0, off, True)
    sp = _softplus(z)
    l = jnp.where(mask, -sp, 0.0)
    within = _block_sums(l, su)
    later = [None] * len(within)
    for b in reversed(range(len(within))):
        later[b] = within[b] + run
        run = run + jnp.sum(l[:, b * HD : (b + 1) * HD], axis=1, keepdims=True)
    a = jnp.where(mask, jnp.exp(z - sp + jnp.concatenate(later, axis=1)), 0.0)
    return z, mask, a, run


def sb_fwd(p, W, name):
    S = p.shape[0]
    TQ, TK = _query_rows(S), _key_strip(S)
    nh, nq = W // HD, S // TQ

    def body(q_ref, k_ref, v_ref, o_ref):
        i = pl.program_id(1)
        q = q_ref[...]
        su = (_iota2((HD, HD), 0) > _iota2((HD, HD), 1)).astype(BF16)
        last = (i * TQ + TQ - 1) // TK

        def step(gg, carry):
            acc, run = carry
            off = pl.multiple_of((last - gg) * TK, TK)
            _, _, a, run = _sb_strip(q, k_ref[pl.ds(off, TK), :], i * TQ, off, run, su)
            return acc + _dot(a.astype(BF16), v_ref[pl.ds(off, TK), :]), run

        acc, _ = lax.fori_loop(0, last + 1, step, (jnp.zeros((TQ, HD), F32), jnp.zeros((TQ, 1), F32)))
        o_ref[...] = acc.astype(o_ref.dtype)

    return pl.pallas_call(
        body,
        grid=(nh, nq),
        in_specs=[
            pl.BlockSpec((TQ, HD), lambda h, i: (i, h)),
            pl.BlockSpec((S, HD), lambda h, i: (0, nh + h)),
            pl.BlockSpec((S, HD), lambda h, i: (0, 2 * nh + h)),
        ],
        out_specs=pl.BlockSpec((TQ, HD), lambda h, i: (i, h)),
        out_shape=jax.ShapeDtypeStruct((S, 2 * W), BF16),
        compiler_params=_cp(("parallel", "arbitrary")),
        name=name,
    )(p, p, p)


def sb_bwd(p, dcat, W, name):
    S = p.shape[0]
    TQ, TK = _query_rows(S), _key_strip(S)
    nh, nq = W // HD, S // TQ
    scale = HD ** -0.5

    def body(q_ref, k_ref, v_ref, do_ref, dp_ref, dk_acc, dv_acc, e_scr, z_scr):
        i = pl.program_id(1)
        q = q_ref[...]
        do = do_ref[...]
        su = (_iota2((HD, HD), 0) > _iota2((HD, HD), 1)).astype(BF16)
        sl = (_iota2((HD, HD), 0) < _iota2((HD, HD), 1)).astype(BF16)
        last = (i * TQ + TQ - 1) // TK

        @pl.when(i == 0)
        def _():
            dk_acc[...] = jnp.zeros_like(dk_acc)
            dv_acc[...] = jnp.zeros_like(dv_acc)

        def pass_a(gg, run):
            g = last - gg
            off = pl.multiple_of(g * TK, TK)
            z, _, a, run = _sb_strip(q, k_ref[pl.ds(off, TK), :], i * TQ, off, run, su)
            e_scr[g] = a * _dot_nt(do, v_ref[pl.ds(off, TK), :])
            z_scr[g] = z
            dv_acc[pl.ds(off, TK), :] += _dot_tn(a.astype(BF16), do)
            return run

        lax.fori_loop(0, last + 1, pass_a, jnp.zeros((TQ, 1), F32))

        def pass_b(g, carry):
            dq, run_e = carry
            off = pl.multiple_of(g * TK, TK)
            e = e_scr[g]
            z = z_scr[g]
            mask = _strip_mask(z.shape, i * TQ, off, True)
            within = _block_sums(e, sl)
            before = []
            for b in range(len(within)):
                before.append(within[b] + run_e)
                run_e = run_e + jnp.sum(e[:, b * HD : (b + 1) * HD], axis=1, keepdims=True)
            sig = 1.0 / (1.0 + jnp.exp(-z))
            dz = jnp.where(mask, e * (1.0 - sig) - jnp.concatenate(before, axis=1) * sig, 0.0)
            dz = (dz * scale).astype(BF16)
            dq = dq + _dot(dz, k_ref[pl.ds(off, TK), :])
            dk_acc[pl.ds(off, TK), :] += _dot_tn(dz, q)
            return dq, run_e

        dq, _ = lax.fori_loop(0, last + 1, pass_b, (jnp.zeros((TQ, HD), F32), jnp.zeros((TQ, 1), F32)))
        dp_ref[0, pl.ds(pl.multiple_of(i * TQ, TQ), TQ), :] = dq.astype(dp_ref.dtype)

        @pl.when(i == nq - 1)
        def _():
            dp_ref[1] = dk_acc[...].astype(dp_ref.dtype)
            dp_ref[2] = dv_acc[...].astype(dp_ref.dtype)

    return pl.pallas_call(
        body,
        grid=(nh, nq),
        in_specs=[
            pl.BlockSpec((TQ, HD), lambda h, i: (i, h)),
            pl.BlockSpec((S, HD), lambda h, i: (0, nh + h)),
            pl.BlockSpec((S, HD), lambda h, i: (0, 2 * nh + h)),
            pl.BlockSpec((TQ, HD), lambda h, i: (i, h)),
        ],
        out_specs=pl.BlockSpec((3, S, HD), lambda h, i: (0, 0, h)),
        out_shape=jax.ShapeDtypeStruct((6, S, W), BF16),
        scratch_shapes=[
            pltpu.VMEM((S, HD), F32),
            pltpu.VMEM((S, HD), F32),
            pltpu.VMEM((S // TK, TQ, TK), F32),
            pltpu.VMEM((S // TK, TQ, TK), F32),
        ],
        compiler_params=_cp(("parallel", "arbitrary")),
        name=name,
    )(p, p, p, dcat)


def fox_gate_fwd(f, b, name):
    S = f.shape[0]
    nq = S // HD

    def body(f_ref, b_ref, c_ref, run):
        i = pl.program_id(0)

        @pl.when(i == 0)
        def _():
            run[...] = jnp.zeros_like(run)

        lf = -_softplus(-(f_ref[...] + b_ref[...]))
        tri = (_iota2((HD, HD), 0) >= _iota2((HD, HD), 1)).astype(BF16)
        c_ref[...] = _dot_ones_left(tri, lf) + run[...]
        run[...] += jnp.sum(lf, axis=0, keepdims=True)

    return pl.pallas_call(
        body,
        grid=(nq,),
        in_specs=[pl.BlockSpec((HD, 128), lambda i: (i, 0)), pl.BlockSpec((1, 128), lambda i: (0, 0))],
        out_specs=pl.BlockSpec((HD, 128), lambda i: (i, 0)),
        out_shape=jax.ShapeDtypeStruct((S, 128), F32),
        scratch_shapes=[pltpu.VMEM((1, 128), F32)],
        compiler_params=_cp(("arbitrary",)),
        name=name,
    )(f, b)


def fox_gate_bwd(f, b, dc, name):
    S = f.shape[0]
    nq = S // HD

    def body(f_ref, b_ref, dc_ref, df_ref, db_ref, run):
        i = pl.program_id(0)

        @pl.when(i == 0)
        def _():
            run[...] = jnp.zeros_like(run)

        dc = dc_ref[...]
        tri = (_iota2((HD, HD), 0) <= _iota2((HD, HD), 1)).astype(BF16)
        dlf = _dot_ones_left(tri, dc) + run[...]
        run[...] += jnp.sum(dc, axis=0, keepdims=True)
        x = f_ref[...] + b_ref[...]
        df = dlf * (1.0 / (1.0 + jnp.exp(x)))
        df_ref[...] = df
        db = jnp.sum(df, axis=0, keepdims=True)

        @pl.when(i == 0)
        def _():
            db_ref[...] = db

        @pl.when(i > 0)
        def _():
            db_ref[...] += db

    rev = pl.BlockSpec((HD, 128), lambda i: (nq - 1 - i, 0))
    vec = pl.BlockSpec((1, 128), lambda i: (0, 0))
    return pl.pallas_call(
        body,
        grid=(nq,),
        in_specs=[rev, vec, rev],
        out_specs=[rev, vec],
        out_shape=[jax.ShapeDtypeStruct((S, 128), F32), jax.ShapeDtypeStruct((1, 128), F32)],
        scratch_shapes=[pltpu.VMEM((1, 128), F32)],
        compiler_params=_cp(("arbitrary",)),
        name=name,
    )(f, b, dc)


def _fox_logits(q, ks, ct, cs, row0, off):
    s = _dot_nt(q, ks) * (HD ** -0.5) + (ct - cs)
    mask = _strip_mask(s.shape, row0, off, False)
    return jnp.where(mask, s, -1e30), mask


def fox_fwd(p, ccol, crow, cat, W, name):
    S = p.shape[0]
    TQ, TK = _query_rows(S), _key_strip(S)
    nh, nq = W // HD, S // TQ

    def body(q_ref, k_ref, v_ref, cc_ref, cr_ref, cat_ref, o_ref, lse_ref):
        i = pl.program_id(1)
        q = q_ref[...]
        ct = cc_ref[0]

        def step(g, carry):
            m, l, acc = carry
            off = pl.multiple_of(g * TK, TK)
            s, _ = _fox_logits(q, k_ref[pl.ds(off, TK), :], ct, cr_ref[0, pl.ds(g, 1), :], i * TQ, off)
            m_new = jnp.maximum(m, jnp.max(s, axis=1, keepdims=True))
            alpha = jnp.exp(m - m_new)
            pr = jnp.exp(s - m_new)
            l = alpha * l + jnp.sum(pr, axis=1, keepdims=True)
            acc = alpha * acc + _dot(pr.astype(BF16), v_ref[pl.ds(off, TK), :])
            return m_new, l, acc

        init = (jnp.full((TQ, 1), -1e30, F32), jnp.zeros((TQ, 1), F32), jnp.zeros((TQ, HD), F32))
        m, l, acc = lax.fori_loop(0, (i * TQ + TQ - 1) // TK + 1, step, init)
        o_ref[...] = (acc / l).astype(o_ref.dtype)
        lse_ref[0] = m + jnp.log(l)

    return pl.pallas_call(
        body,
        grid=(nh, nq),
        in_specs=[
            pl.BlockSpec((TQ, HD), lambda h, i: (i, 2 * nh + h)),
            pl.BlockSpec((S, HD), lambda h, i: (0, 3 * nh + h)),
            pl.BlockSpec((S, HD), lambda h, i: (0, 4 * nh + h)),
            pl.BlockSpec((1, TQ, 1), lambda h, i: (h, i, 0)),
            pl.BlockSpec((1, S // TK, TK), lambda h, i: (h, 0, 0)),
            pl.BlockSpec(memory_space=pl.ANY),
        ],
        out_specs=[pl.BlockSpec((TQ, HD), lambda h, i: (i, nh + h)), pl.BlockSpec((1, TQ, 1), lambda h, i: (h, i, 0))],
        out_shape=[jax.ShapeDtypeStruct(cat.shape, cat.dtype), jax.ShapeDtypeStruct((nh, S, 1), F32)],
        input_output_aliases={5: 0},
        compiler_params=_cp(("parallel", "arbitrary")),
        name=name,
    )(p, p, p, ccol, crow, cat)


def fox_bwd(p, ccol, crow, cat, lse, dcat, dp, W, name):
    S = p.shape[0]
    TQ, TK = _query_rows(S), _key_strip(S)
    nh, nq = W // HD, S // TQ
    scale = HD ** -0.5

    def body(q_ref, k_ref, v_ref, cc_ref, cr_ref, o_ref, lse_ref, do_ref, dp_in_ref, dp_ref, dcs_ref, dct_ref, dk_acc, dv_acc):
        i = pl.program_id(1)
        q = q_ref[...]
        do = do_ref[...]
        ct = cc_ref[0]
        lse_i = lse_ref[0]
        delta = jnp.sum(do.astype(F32) * o_ref[...].astype(F32), axis=1, keepdims=True)

        @pl.when(i == 0)
        def _():
            dk_acc[...] = jnp.zeros_like(dk_acc)
            dv_acc[...] = jnp.zeros_like(dv_acc)
            dcs_ref[...] = jnp.zeros_like(dcs_ref)

        def step(g, carry):
            dq, dct = carry
            off = pl.multiple_of(g * TK, TK)
            ks = k_ref[pl.ds(off, TK), :]
            s, mask = _fox_logits(q, ks, ct, cr_ref[0, pl.ds(g, 1), :], i * TQ, off)
            pr = jnp.where(mask, jnp.exp(s - lse_i), 0.0)
            ds = pr * (_dot_nt(do, v_ref[pl.ds(off, TK), :]) - delta)
            dv_acc[pl.ds(off, TK), :] += _dot_tn(pr.astype(BF16), do)
            dsb = (ds * scale).astype(BF16)
            dk_acc[pl.ds(off, TK), :] += _dot_tn(dsb, q)
            dcs_ref[0, pl.ds(g, 1), :] += jnp.sum(ds, axis=0, keepdims=True)
            return dq + _dot(dsb, ks), dct + jnp.sum(ds, axis=1, keepdims=True)

        dq, dct = lax.fori_loop(0, (i * TQ + TQ - 1) // TK + 1, step, (jnp.zeros((TQ, HD), F32), jnp.zeros((TQ, 1), F32)))
        dp_ref[0, pl.ds(pl.multiple_of(i * TQ, TQ), TQ), :] = dq.astype(dp_ref.dtype)
        dct_ref[0] = dct

        @pl.when(i == nq - 1)
        def _():
            dp_ref[1] = dk_acc[...].astype(dp_ref.dtype)
            dp_ref[2] = dv_acc[...].astype(dp_ref.dtype)

    return pl.pallas_call(
        body,
        grid=(nh, nq),
        in_specs=[
            pl.BlockSpec((TQ, HD), lambda h, i: (i, 2 * nh + h)),
            pl.BlockSpec((S, HD), lambda h, i: (0, 3 * nh + h)),
            pl.BlockSpec((S, HD), lambda h, i: (0, 4 * nh + h)),
            pl.BlockSpec((1, TQ, 1), lambda h, i: (h, i, 0)),
            pl.BlockSpec((1, S // TK, TK), lambda h, i: (h, 0, 0)),
            pl.BlockSpec((TQ, HD), lambda h, i: (i, nh + h)),
            pl.BlockSpec((1, TQ, 1), lambda h, i: (h, i, 0)),
            pl.BlockSpec((TQ, HD), lambda h, i: (i, nh + h)),
            pl.BlockSpec(memory_space=pl.ANY),
        ],
        out_specs=[
            pl.BlockSpec((3, S, HD), lambda h, i: (1, 0, h)),
            pl.BlockSpec((1, S // TK, TK), lambda h, i: (h, 0, 0)),
            pl.BlockSpec((1, TQ, 1), lambda h, i: (h, i, 0)),
        ],
        out_shape=[
            jax.ShapeDtypeStruct(dp.shape, dp.dtype),
            jax.ShapeDtypeStruct((nh, S // TK, TK), F32),
            jax.ShapeDtypeStruct((nh, S, 1), F32),
        ],
        input_output_aliases={8: 0},
        scratch_shapes=[pltpu.VMEM((S, HD), F32), pltpu.VMEM((S, HD), F32)],
        compiler_params=_cp(("parallel", "arbitrary")),
        name=name,
    )(p, p, p, ccol, crow, cat, lse, dcat, dp)


_GELU_K = math.sqrt(2.0 / math.pi)
_GELU_C = 0.044715


def _gelu(x):
    return 0.5 * x * (1.0 + jnp.tanh(_GELU_K * (x + _GELU_C * x * x * x)))


def _gelu_grad(x):
    t = jnp.tanh(_GELU_K * (x + _GELU_C * x * x * x))
    return 0.5 * (1.0 + t) + 0.5 * x * (1.0 - t * t) * (_GELU_K * (1.0 + 3.0 * _GELU_C * x * x))


def _layernorm_parts(gv):
    xc = gv - jnp.mean(gv, axis=-1, keepdims=True)
    r = lax.rsqrt(jnp.mean(xc * xc, axis=-1, keepdims=True) + EPS)
    return xc * r, r


def sg_fwd(p, sg_w, sg_bt, sg_g, W, name):
    S = p.shape[0]
    G, nq = W // HD, S // HD

    def body(u_ref, v_ref, w_ref, bt_ref, g_ref, o_ref):
        xh, _ = _layernorm_parts(_gelu(v_ref[...].astype(F32)))
        vn = (xh * g_ref[...]).astype(BF16)
        tri = _iota2((HD, HD), 0) >= _iota2((HD, HD), 1)
        for gi in range(G):
            cols = slice(gi * HD, (gi + 1) * HD)
            wt = jnp.where(tri, w_ref[gi], 0.0).astype(BF16)
            mixed = _dot(wt, vn[:, cols]) + bt_ref[:, gi : gi + 1]
            o_ref[:, cols] = (_gelu(u_ref[:, cols].astype(F32)) * mixed).astype(o_ref.dtype)

    return pl.pallas_call(
        body,
        grid=(nq,),
        in_specs=[
            pl.BlockSpec((HD, W), lambda i: (i, 0)),
            pl.BlockSpec((HD, W), lambda i: (i, 1)),
            pl.BlockSpec((G, HD, HD), lambda i: (0, 0, 0)),
            pl.BlockSpec((HD, G), lambda i: (0, 0)),
            pl.BlockSpec((1, W), lambda i: (0, 0)),
        ],
        out_specs=pl.BlockSpec((HD, W), lambda i: (i, 0)),
        out_shape=jax.ShapeDtypeStruct((S, 2 * W), BF16),
        compiler_params=_cp(("parallel",)),
        name=name,
    )(p, p, sg_w, sg_bt, sg_g.reshape(1, W))


def sg_bwd(p, sg_w, sg_bt, sg_g, dcat, W, name):
    S = p.shape[0]
    G, nq = W // HD, S // HD

    def body(u_ref, v_ref, w_ref, bt_ref, g_ref, do_ref, dp_ref, dw_ref, dbt_ref, dg_ref, dvn_scr):
        i = pl.program_id(0)

        @pl.when(i == 0)
        def _():
            dw_ref[...] = jnp.zeros_like(dw_ref)
            dbt_ref[...] = jnp.zeros_like(dbt_ref)
            dg_ref[...] = jnp.zeros_like(dg_ref)

        v = v_ref[...].astype(F32)
        xh, r = _layernorm_parts(_gelu(v))
        gg = g_ref[...]
        vn = (xh * gg).astype(BF16)
        tri = _iota2((HD, HD), 0) >= _iota2((HD, HD), 1)
        for gi in range(G):
            cols = slice(gi * HD, (gi + 1) * HD)
            wt = jnp.where(tri, w_ref[gi], 0.0).astype(BF16)
            mixed = _dot(wt, vn[:, cols]) + bt_ref[:, gi : gi + 1]
            u = u_ref[:, cols].astype(F32)
            do = do_ref[:, cols].astype(F32)
            dp_ref[0, :, cols] = (do * mixed * _gelu_grad(u)).astype(dp_ref.dtype)
            dmix = do * _gelu(u)
            dmb = dmix.astype(BF16)
            dw_ref[gi] += jnp.where(tri, _dot_nt(dmb, vn[:, cols]), 0.0)
            dbt_ref[:, gi : gi + 1] += jnp.sum(dmix, axis=1, keepdims=True)
            dvn_scr[:, cols] = _dot_tn(wt, dmb)
        dvn = dvn_scr[...]
        dg_ref[...] += jnp.sum(dvn * xh, axis=0, keepdims=True)
        dxh = dvn * gg
        dgv = r * (dxh - jnp.mean(dxh, axis=-1, keepdims=True) - xh * jnp.mean(dxh * xh, axis=-1, keepdims=True))
        dp_ref[1] = (dgv * _gelu_grad(v)).astype(dp_ref.dtype)

    return pl.pallas_call(
        body,
        grid=(nq,),
        in_specs=[
            pl.BlockSpec((HD, W), lambda i: (i, 0)),
            pl.BlockSpec((HD, W), lambda i: (i, 1)),
            pl.BlockSpec((G, HD, HD), lambda i: (0, 0, 0)),
            pl.BlockSpec((HD, G), lambda i: (0, 0)),
            pl.BlockSpec((1, W), lambda i: (0, 0)),
            pl.BlockSpec((HD, W), lambda i: (i, 0)),
        ],
        out_specs=[
            pl.BlockSpec((2, HD, W), lambda i: (0, i, 0)),
            pl.BlockSpec((G, HD, HD), lambda i: (0, 0, 0)),
            pl.BlockSpec((HD, G), lambda i: (0, 0)),
            pl.BlockSpec((1, W), lambda i: (0, 0)),
        ],
        out_shape=[
            jax.ShapeDtypeStruct((6, S, W), BF16),
            jax.ShapeDtypeStruct((G, HD, HD), F32),
            jax.ShapeDtypeStruct((HD, G), F32),
            jax.ShapeDtypeStruct((1, W), F32),
        ],
        scratch_shapes=[pltpu.VMEM((HD, W), F32)],
        compiler_params=_cp(("arbitrary",)),
        name=name,
    )(p, p, sg_w, sg_bt, sg_g.reshape(1, W), dcat)


def local_step(x, target, wts, at, on_grad):
    S, D = x.shape
    W = D // 2
    nb, F = wts["nb"], wts["F"]
    g = {}

    def ffn_fwd(xin, l):
        h = rms_fwd(xin, wts[f"{l}_ffn_norm_g"], f"{l}_ffn_rms")
        u = mm_nn(h, wts[f"{l}_ffn_up"], nb, f"{l}_ffn_up_mm")
        act = ffn_act_fwd(u, wts[f"{l}_ffn_conv_w"], F, f"{l}_ffn_act")
        half_tile = _pick(S, (512, 256, 128))
        xout = mm_nn(act, wts[f"{l}_ffn_down"], 1, f"{l}_ffn_down_mm", out_dtype=F32, res=xin,
                     tm=half_tile, tn=_pick(D, (512, 256, 128)), tk=F)
        return xout, (xin, h, u, act)

    def ffn_bwd(dxout, dxoutb, saved, l):
        xin, h, u, act = saved
        dact = mm_nt(dxoutb, wts[f"{l}_ffn_down"], 1, S, F, f"{l}_ffn_down_dx", tko=_pick(F, (512, 256, 128)), tn=D)
        dact = on_grad(f"{l}_ffn_down", mm_tn(act, dxoutb, 1, D, f"{l}_ffn_down_dw", tn=D), dact)
        du, dcw = ffn_act_bwd(u, wts[f"{l}_ffn_conv_w"], dact, F, f"{l}_ffn_act_bwd")
        g[f"{l}_ffn_conv_w"] = jnp.concatenate([dcw[0], dcw[1]], axis=1)
        du2 = du.reshape(2 * S, F)
        n = wts[f"{l}_ffn_up"].shape[1]
        tn = _pick(n, (1408, 1024, 768, 512, 256, 128))
        per_half = F // tn
        nt = n // tn

        def up_block(i, j, t):
            vb = j * nt + t
            return vb // per_half, vb % per_half

        tm = _pick(S, (1024, 512, 256, 128))

        def nt_map(i, j, t):
            half, cb = up_block(i, j, t)
            return (half * (S // tm) + i, cb)

        def tn_map(j, t):
            half, cb = up_block(0, j, t)
            return (half, cb)

        dh = mm_nt(du2, wts[f"{l}_ffn_up"], nb, S, D, f"{l}_ffn_up_dx", dy_maps=[nt_map], tm=tm, tko=D, tn=tn)
        dh = on_grad(f"{l}_ffn_up", mm_tn(h, du2, nb, n, f"{l}_ffn_up_dw", dy_maps=[tn_map], tn=tn), dh)
        dxin, dxinb, dg = rms_bwd(xin, wts[f"{l}_ffn_norm_g"], dh, dxout, f"{l}_ffn_rms_bwd")
        g[f"{l}_ffn_norm_g"] = dg
        return dxin, dxinb

    h0 = rms_fwd(x, wts["l0_mix_norm_g"], "l0_mix_rms")
    p0 = mm_nn(h0, wts["l0_w_in"], nb, "l0_w_in_mm")
    cat0 = sb_fwd(p0, W, "l0_sb_fwd")
    cat0 = sc_fwd(p0, wts["l0_sc_conv_w"], cat0, W, "l0_sc_fwd")
    x1 = mm_nn(cat0, wts["l0_w_out"], 1, "l0_w_out_mm", out_dtype=F32, res=x, tm=S, tn=_pick(D, (512, 256, 128)))
    x2, ffn0_saved = ffn_fwd(x1, "l0")

    nh = W // HD
    h2 = rms_fwd(x2, wts["l1_mix_norm_g"], "l1_mix_rms")
    p1 = mm_nt(h2, wts["l1_w_in_t"], 1, S, 5 * W, "l1_w_in_mm", tn=D)
    f = mm_nt(h2, wts["l1_w_f_t"], 1, S, 128, "l1_w_f_mm", out_dtype=F32, tn=D)
    bf = jnp.zeros((1, 128), F32).at[0, :nh].set(wts["l1_fox_b_f"])
    c = fox_gate_fwd(f, bf, "l1_fox_gate")
    c_heads = c[:, :nh].T
    ccol = c_heads[:, :, None]
    crow = c_heads.reshape(nh, S // _key_strip(S), _key_strip(S))
    sg_bt = wts["l1_sg_b"].T
    cat1 = sg_fwd(p1, wts["l1_sg_w"], sg_bt, wts["l1_sg_norm_g"], W, "l1_sg_fwd")
    cat1, lse = fox_fwd(p1, ccol, crow, cat1, W, "l1_fox_fwd")
    x3 = mm_nn(cat1, wts["l1_w_out"], 1, "l1_w_out_mm", out_dtype=F32, res=x2, tm=S, tn=_pick(D, (512, 256, 128)))
    x4, ffn1_saved = ffn_fwd(x3, "l1")

    dx4, dx4b, dgf, loss = loss_head(x4, wts["final_norm_g"], target, "loss_head")
    dx4b = at("loss", dx4b, loss)
    g["final_norm_g"] = dgf

    dx3, dx3b = ffn_bwd(dx4, dx4b, ffn1_saved, "l1")
    dcat1 = mm_nt(dx3b, wts["l1_w_out"], 1, S, D, "l1_w_out_dx", tn=D)
    dcat1 = on_grad("l1_w_out", mm_tn(cat1, dx3b, 1, D, "l1_w_out_dw", tn=D), dcat1)
    dp1, dsgw, dsgbt, dsgg = sg_bwd(p1, wts["l1_sg_w"], sg_bt, wts["l1_sg_norm_g"], dcat1, W, "l1_sg_bwd")
    dp1, dcs, dct = fox_bwd(p1, ccol, crow, cat1, lse, dcat1, dp1, W, "l1_fox_bwd")
    g["l1_sg_w"], g["l1_sg_b"], g["l1_sg_norm_g"] = dsgw, dsgbt.T, dsgg
    dc = jnp.zeros((S, 128), F32).at[:, :nh].set((dct[:, :, 0] - dcs.reshape(nh, S)).T)
    df, dbf = fox_gate_bwd(f, bf, dc, "l1_fox_gate_bwd")
    g["l1_fox_b_f"] = dbf[0, :nh]
    dfb = df.astype(BF16)
    tk1 = _pick(W, (1024, 512, 256, 128))
    tx1 = _pick(W, (512, 256, 128))
    tm1 = _pick(S, (1024, 512, 256, 128))
    part_of = lambda pt: pt + pt // 2 - pt // 4

    def a_map1(i, k):
        return (part_of(k // (W // tk1)) * (S // tm1) + i, k % (W // tk1))

    def x_map1(ko):
        return (part_of(ko // (W // tx1)), ko % (W // tx1))

    dp1_2d = dp1.reshape(6 * S, W)
    dw_main = mm_tn(dp1_2d, h2, 1, D, "l1_w_in_dw", tko=tx1, tn=D, x_map=x_map1, x_shape=(S, 5 * W))
    dw_f = mm_tn(dfb, h2, 1, D, "l1_w_f_dw", tn=D)
    dh2 = mm_nn(dfb, wts["l1_w_f_t"], 1, "l1_w_f_dx", out_dtype=F32)
    dh2 = mm_nn(dp1_2d, wts["l1_w_in_t"], 1, "l1_w_in_dx", res=dh2, tm=tm1, tk=tk1, a_map=a_map1, a_shape=(S, 5 * W))
    dh2 = on_grad("l1_w_in", jnp.concatenate([dw_main, dw_f[:nh]], axis=0), dh2)
    dx2, dx2b, dg = rms_bwd(x2, wts["l1_mix_norm_g"], dh2, dx3, "l1_mix_rms_bwd")
    g["l1_mix_norm_g"] = dg

    dx1, dx1b = ffn_bwd(dx2, dx2b, ffn0_saved, "l0")
    dcat0 = mm_nt(dx1b, wts["l0_w_out"], 1, S, D, "l0_w_out_dx", tn=D)
    dcat0 = on_grad("l0_w_out", mm_tn(cat0, dx1b, 1, D, "l0_w_out_dw", tn=D), dcat0)
    dp0 = sb_bwd(p0, dcat0, W, "l0_sb_bwd")
    dp0, dscw = sc_bwd(p0, wts["l0_sc_conv_w"], dcat0, dp0, W, "l0_sc_bwd")
    g["l0_sc_conv_w"] = dscw
    n0 = wts["l0_w_in"].shape[1]
    td0 = math.gcd(n0, W)
    nd0 = n0 // td0
    half = S // 2
    tm0 = _pick(half, (1024, 512, 256, 128))
    per_part0 = W // td0

    def nt_maps0(k, first_block):
        def f(i, j, t):
            vb = j * nd0 + k
            return ((vb // per_part0) * (S // tm0) + first_block + i, vb % per_part0)
        return f

    def tn_maps0(k):
        def f(j, t):
            vb = j * nd0 + k
            return (vb // per_part0, vb % per_part0)
        return f

    dp0_2d = dp0.reshape(6 * S, W)
    dw0 = mm_tn(h0, dp0_2d, nb, n0, "l0_w_in_dw", dy_maps=[tn_maps0(k) for k in range(nd0)], tn=n0)
    dp0_2d = on_grad("l0_w_in", dw0, dp0_2d)
    halves = []
    for b, tag in enumerate("ab"):
        maps = [nt_maps0(k, b * (half // tm0)) for k in range(nd0)]
        halves.append(mm_nt(dp0_2d, wts["l0_w_in"], nb, half, D, f"l0_w_in_dx_{tag}", dy_maps=maps, tm=tm0, tn=n0))
        if b == 0:
            halves[0], dp0_2d = lax.optimization_barrier((halves[0], dp0_2d))
            dp0_2d = on_grad(None, None, dp0_2d)
    dh0 = jnp.concatenate(halves, axis=0)
    dx0, _, dg = rms_bwd(x, wts["l0_mix_norm_g"], dh0, dx1, "l0_mix_rms_bwd")
    g["l0_mix_norm_g"] = dg
    return dx0, g


GATHER_ID, PAIR_ID, CHIPS_ID = 1, 2, 3


def _place():
    return lax.axis_index("x"), lax.axis_index("y"), lax.axis_index("c")


def _other_chips(x, y):
    return [(x, 1 - y), (1 - x, y), (1 - x, 1 - y)]


def _handshake(peers):
    barrier = pltpu.get_barrier_semaphore()
    for peer in peers:
        pl.semaphore_signal(barrier, inc=1, device_id=peer, device_id_type=MESH)
    pl.semaphore_wait(barrier, len(peers))


UPDATE_LAG = 2


def _on_sequencer(body, out_type, scratch_types, collective_id, name):
    return pl.kernel(
        body,
        out_type=out_type,
        mesh=plsc.ScalarSubcoreMesh(axis_name="seq", num_cores=1),
        scratch_types=scratch_types,
        compiler_params=pltpu.CompilerParams(collective_id=collective_id),
        name=name,
    )


def all_gather(arrs, name):
    n = len(arrs)

    def body(*refs):
        xs, outs = refs[:n], refs[n : 2 * n]
        send_sems, recv_sems, local_sems = refs[2 * n :]
        x, y, c = _place()
        me, sibling = (x, y, c), (x, y, 1 - c)
        chips = _other_chips(x, y)
        _handshake([sibling] + [(*chip, c) for chip in chips])

        def copy(a, k, block, to, src=None):
            px, py, pc = block
            dst = outs[a].at[4 * px + 2 * py + pc]
            return pltpu.make_async_remote_copy(
                src_ref=dst if src is None else src, dst_ref=dst,
                send_sem=send_sems.at[7 * a + k], recv_sem=recv_sems.at[7 * a + k], device_id=to, device_id_type=MESH,
            )

        mine = [pltpu.make_async_copy(xs[a], outs[a].at[4 * x + 2 * y + c], local_sems.at[a]) for a in range(n)]
        for cp in mine:
            cp.start()
        first = []
        for a in range(n):
            first.append(copy(a, 0, me, sibling, src=xs[a]))
            first += [copy(a, 1 + j, me, (*chip, c), src=xs[a]) for j, chip in enumerate(chips)]
        for cp in first:
            cp.start()
        passed = []
        for a in range(n):
            for j, chip in enumerate(chips):
                copy(a, 1 + j, (*chip, c), me).wait_recv()
                cp = copy(a, 4 + j, (*chip, c), sibling)
                cp.start()
                passed.append(cp)
        for a in range(n):
            copy(a, 0, sibling, me).wait_recv()
            for j, chip in enumerate(chips):
                copy(a, 4 + j, (*chip, 1 - c), me).wait_recv()
        for cp in first + passed:
            cp.wait_send()
        for cp in mine:
            cp.wait()

    out_type = [jax.ShapeDtypeStruct((NDEV,) + a.shape, a.dtype) for a in arrs]
    sems = [pltpu.SemaphoreType.DMA((7 * n,)), pltpu.SemaphoreType.DMA((7 * n,)), pltpu.SemaphoreType.DMA((n,))]
    return _on_sequencer(body, out_type, sems, GATHER_ID, name)(*arrs)


_IN_HBM = pl.BlockSpec(memory_space=pltpu.HBM)
_IN_SEM = pl.BlockSpec(memory_space=pltpu.SEMAPHORE)
_EFFECT = pltpu.SideEffectType.DATAFLOW_SIDE_EFFECTING


def _split_start(make_copies, src, land_shape, nsem, name):
    def body(src_ref, land_ref, send_sems, recv_sems, land_thru, token):
        for cp in make_copies(src_ref, land_ref, send_sems, recv_sems):
            cp.start()
        token[...] = jnp.zeros_like(token)

    send_sems, recv_sems, land_thru, token = pl.pallas_call(
        body,
        name=name,
        out_shape=(
            pltpu.SemaphoreType.DMA((nsem,)), pltpu.SemaphoreType.DMA((nsem,)),
            pltpu.HBM(land_shape, src.dtype), jax.ShapeDtypeStruct((8, 128), F32),
        ),
        in_specs=(_IN_HBM, _IN_HBM),
        out_specs=(_IN_SEM, _IN_SEM, _IN_HBM, pl.BlockSpec(memory_space=pltpu.VMEM)),
        input_output_aliases={1: 2},
        compiler_params=pltpu.CompilerParams(has_side_effects=_EFFECT),
    )(src, pltpu.with_memory_space_constraint(lax.empty(land_shape, src.dtype), pltpu.HBM))
    return send_sems, recv_sems, src, land_thru, token


def _split_wait(make_copies, send_sems, recv_sems, src_thru, land_thru, after, name):
    def body(src_ref, land_ref, send_sems, recv_sems, after_ref, land_out):
        for cp in make_copies(src_ref, land_ref, send_sems, recv_sems):
            cp.wait_send()
            cp.wait_recv()

    return pl.pallas_call(
        body,
        name=name,
        out_shape=pltpu.HBM(land_thru.shape, land_thru.dtype),
        in_specs=(_IN_HBM, _IN_HBM, _IN_SEM, _IN_SEM, pl.BlockSpec(memory_space=pl.ANY)),
        out_specs=_IN_HBM,
        input_output_aliases={1: 0},
        compiler_params=pltpu.CompilerParams(has_side_effects=_EFFECT),
    )(src_thru, land_thru, send_sems, recv_sems, after)


def _pair_copies(src_ref, land_ref, send_sems, recv_sems):
    x, y, c = _place()
    return [
        pltpu.make_async_remote_copy(
            src_ref=src_ref.at[k, 1 - c], dst_ref=land_ref.at[k],
            send_sem=send_sems.at[k], recv_sem=recv_sems.at[k], device_id=(x, y, 1 - c), device_id_type=MESH,
        )
        for k in range(4)
    ]


def _chip_copies(src_ref, land_ref, send_sems, recv_sems):
    x, y, c = _place()
    return [
        pltpu.make_async_remote_copy(
            src_ref=src_ref.at[2 * px + py], dst_ref=land_ref.at[2 * x + y],
            send_sem=send_sems.at[j], recv_sem=recv_sems.at[j], device_id=(px, py, c), device_id_type=MESH,
        )
        for j, (px, py) in enumerate(_other_chips(x, y))
    ]


def _row_tile(R, C, max_elems):
    if R * C <= max_elems:
        return R
    best = None
    for tr in range(16, R, 16):
        if R % tr == 0 and tr * C <= max_elems:
            best = tr
    return best or R


def pair_sum(a42, land4, core, name):
    _, _, R, C = a42.shape
    tr = _row_tile(R, C, 1 << 20)

    def body(core_ref, a_ref, l_ref, o_ref):
        o_ref[...] = (a_ref[0].astype(F32) + l_ref[...].astype(F32)).astype(o_ref.dtype)

    return pl.pallas_call(
        body,
        grid_spec=pltpu.PrefetchScalarGridSpec(
            num_scalar_prefetch=1,
            grid=(4, R // tr),
            in_specs=[
                pl.BlockSpec((1, 1, tr, C), lambda k, r, core_ref: (k, core_ref[0], r, 0)),
                pl.BlockSpec((1, tr, C), lambda k, r, core_ref: (k, r, 0)),
            ],
            out_specs=pl.BlockSpec((1, tr, C), lambda k, r, core_ref: (k, r, 0)),
        ),
        out_shape=jax.ShapeDtypeStruct((4, R, C), BF16),
        compiler_params=_cp(("parallel", "parallel")),
        name=name,
    )(core, a42, land4)


def sum_slots(parts, name):
    P, R, C = parts.shape

    def body(p_ref, o_ref):
        acc = p_ref[0].astype(F32)
        for k in range(1, P):
            acc = acc + p_ref[k].astype(F32)
        o_ref[...] = acc

    tr = _row_tile(R, P * C, 1 << 21)
    return pl.pallas_call(
        body,
        grid=(R // tr,),
        in_specs=[pl.BlockSpec((P, tr, C), lambda r: (0, r, 0))],
        out_specs=pl.BlockSpec((tr, C), lambda r: (r, 0)),
        out_shape=jax.ShapeDtypeStruct((R, C), F32),
        compiler_params=_cp(("parallel",)),
        name=name,
    )(parts)


def adamw(w, m, v, parts, name):
    R, C = w.shape
    P = parts.shape[0]
    tr = _pick(R, (256, 128, 64, 32, 16, 8))
    c1 = 1.0 - ADAM_B1 ** ADAM_STEP
    c2 = 1.0 - ADAM_B2 ** ADAM_STEP

    def body(w_ref, m_ref, v_ref, p_ref, g_ref, d_ref, nm_ref, nv_ref):
        g = p_ref[0].astype(F32)
        for k in range(1, P):
            g = g + p_ref[k].astype(F32)
        nm = ADAM_B1 * m_ref[...] + (1.0 - ADAM_B1) * g
        nv = ADAM_B2 * v_ref[...] + (1.0 - ADAM_B2) * (g * g)
        g_ref[...] = g
        nm_ref[...] = nm
        nv_ref[...] = nv
        d_ref[...] = -ADAM_LR * ((nm / c1) / (jnp.sqrt(nv / c2) + ADAM_EPS) + ADAM_WD * w_ref[...])

    blk = pl.BlockSpec((tr, C), lambda r: (r, 0))
    shp = jax.ShapeDtypeStruct((R, C), F32)
    return pl.pallas_call(
        body,
        grid=(R // tr,),
        in_specs=[blk, blk, blk, pl.BlockSpec((P, tr, C), lambda r: (0, r, 0))],
        out_specs=[blk, blk, blk, blk],
        out_shape=[shp, shp, shp, shp],
        compiler_params=_cp(("parallel",)),
        name=name,
    )(w, m, v, parts)


def adamw_reduced(w, m, v, own, land, chip, name):
    R, C = w.shape
    if R % 8 == 0:
        tr, tc = _pick(R, (256, 128, 64, 32, 16, 8)), C
    else:
        tr, tc = R, _pick(C, (256, 128))
    c1 = 1.0 - ADAM_B1 ** ADAM_STEP
    c2 = 1.0 - ADAM_B2 ** ADAM_STEP

    def body(chip_ref, w_ref, m_ref, v_ref, own_ref, land_ref, g_ref, d_ref, nm_ref, nv_ref):
        mine = own_ref[0].astype(F32)
        g = None
        for k in range(4):
            term = jnp.where(chip_ref[0] == k, mine, land_ref[k].astype(F32))
            g = term if g is None else g + term
        nm = ADAM_B1 * m_ref[...] + (1.0 - ADAM_B1) * g
        nv = ADAM_B2 * v_ref[...] + (1.0 - ADAM_B2) * (g * g)
        g_ref[...] = g
        nm_ref[...] = nm
        nv_ref[...] = nv
        d_ref[...] = -ADAM_LR * ((nm / c1) / (jnp.sqrt(nv / c2) + ADAM_EPS) + ADAM_WD * w_ref[...])

    blk = pl.BlockSpec((tr, tc), lambda r, c, chip_ref: (r, c))
    shp = jax.ShapeDtypeStruct((R, C), F32)
    return pl.pallas_call(
        body,
        grid_spec=pltpu.PrefetchScalarGridSpec(
            num_scalar_prefetch=1,
            grid=(R // tr, C // tc),
            in_specs=[
                blk, blk, blk,
                pl.BlockSpec((1, tr, tc), lambda r, c, chip_ref: (chip_ref[0], r, c)),
                pl.BlockSpec((4, tr, tc), lambda r, c, chip_ref: (0, r, c)),
            ],
            out_specs=[blk, blk, blk, blk],
        ),
        out_shape=[shp, shp, shp, shp],
        compiler_params=_cp(("parallel", "parallel")),
        name=name,
    )(chip, w, m, v, own, land)


_WEIGHTS = [
    "l0_mix_norm_g", "l0_w_in", "l0_sc_conv_w", "l0_w_out", "l0_ffn_norm_g", "l0_ffn_up", "l0_ffn_conv_w", "l0_ffn_down",
    "l1_mix_norm_g", "l1_w_in", "l1_fox_b_f", "l1_sg_w", "l1_sg_b", "l1_sg_norm_g", "l1_w_out", "l1_ffn_norm_g",
    "l1_ffn_up", "l1_ffn_conv_w", "l1_ffn_down", "final_norm_g",
]
_COL_SHARDED = ["l0_w_in", "l0_ffn_up", "l1_w_in", "l1_ffn_up"]
_ROW_SHARDED = ["l0_w_out", "l0_ffn_down", "l1_w_out", "l1_ffn_down"]
_BIG = ["l0_w_in", "l0_w_out", "l0_ffn_up", "l0_ffn_down", "l1_w_in", "l1_w_out", "l1_ffn_up", "l1_ffn_down"]
_CONV = ["l0_sc_conv_w", "l0_ffn_conv_w", "l1_ffn_conv_w"]
_SMALL = [n for n in _WEIGHTS if n not in _BIG]
_PACK_ROWS = 8


def _pack(arrs):
    flat = []
    for a in arrs:
        v = a.reshape(-1).astype(F32)
        pad = (-v.shape[0]) % (_PACK_ROWS * 128)
        flat.append(jnp.pad(v, (0, pad)))
    return jnp.concatenate(flat).reshape(-1, 128)


def _unpack(packed, shapes):
    out, off = [], 0
    flat = packed.reshape(-1)
    for shp in shapes:
        size = math.prod(shp)
        out.append(flat[off : off + size].reshape(shp))
        off += size + (-size) % (_PACK_ROWS * 128)
    return out


def kernel(x, l0_mix_norm_g, l0_w_in, l0_sc_conv_w, l0_w_out, l0_ffn_norm_g, l0_ffn_up, l0_ffn_conv_w, l0_ffn_down, l1_mix_norm_g, l1_w_in, l1_fox_b_f, l1_sg_w, l1_sg_b, l1_sg_norm_g, l1_w_out, l1_ffn_norm_g, l1_ffn_up, l1_ffn_conv_w, l1_ffn_down, final_norm_g, loss_target, m_l0_mix_norm_g, m_l0_w_in, m_l0_sc_conv_w, m_l0_w_out, m_l0_ffn_norm_g, m_l0_ffn_up, m_l0_ffn_conv_w, m_l0_ffn_down, m_l1_mix_norm_g, m_l1_w_in, m_l1_fox_b_f, m_l1_sg_w, m_l1_sg_b, m_l1_sg_norm_g, m_l1_w_out, m_l1_ffn_norm_g, m_l1_ffn_up, m_l1_ffn_conv_w, m_l1_ffn_down, m_final_norm_g, v_l0_mix_norm_g, v_l0_w_in, v_l0_sc_conv_w, v_l0_w_out, v_l0_ffn_norm_g, v_l0_ffn_up, v_l0_ffn_conv_w, v_l0_ffn_down, v_l1_mix_norm_g, v_l1_w_in, v_l1_fox_b_f, v_l1_sg_w, v_l1_sg_b, v_l1_sg_norm_g, v_l1_w_out, v_l1_ffn_norm_g, v_l1_ffn_up, v_l1_ffn_conv_w, v_l1_ffn_down, v_final_norm_g):
    given = dict(locals())
    w = {n: given[n] for n in _WEIGHTS}
    mom = {n: given["m_" + n] for n in _WEIGHTS}
    var = {n: given["v_" + n] for n in _WEIGHTS}
    xs, target = x[0], loss_target[0]
    S, D = xs.shape
    W = D // 2
    nh = W // HD
    cx, cy, cc = _place()
    me = 4 * cx + 2 * cy + cc

    wts = {"nb": NDEV, "F": l0_ffn_down.shape[0] * NDEV}
    for n in _SMALL:
        if n not in _CONV:
            wts[n] = w[n]
    loss_sum = []

    def start_gather(n):
        src = w[n].T if n == "l1_w_in" else w[n]
        got = all_gather([src.astype(BF16)] + ([w[c] for c in _CONV] if n == _BIG[0] else []), f"gather_{n}")
        if n == "l1_w_in":
            wts["l1_w_in_t"] = got[0].reshape(-1, D)
            wts["l1_w_f_t"] = jnp.pad(wts["l1_w_in_t"][5 * W :], ((0, 128 - nh), (0, 0)))
        elif n in _ROW_SHARDED:
            wts[n] = got[0].reshape(-1, D)
        else:
            wts[n] = got[0].reshape(NDEV * D, -1)
        for c, taps in zip(_CONV, got[1:]):
            wts[c] = taps.transpose(1, 0, 2).reshape(CONV_K, -1)

    def at(point, after, value):
        if point == "loss":
            total, after = lax.optimization_barrier((lax.psum(value[0, 0], ("x", "y", "c")), after))
            loss_sum.append(total)
        return after

    core = jnp.reshape(cc, (1,)).astype(jnp.int32)
    chip = jnp.reshape(2 * cx + cy, (1,)).astype(jnp.int32)
    pair_flying, chip_flying = [], []
    out_g, out_d, out_m, out_v = {}, {}, {}, {}

    def tie(value, after):
        if after is None:
            return value, None
        return lax.optimization_barrier((value, after))

    def to_chips(after):
        n, flying = pair_flying.pop()
        landed = _split_wait(_pair_copies, *flying, f"reduce_pair_wait_{n}")
        summed = pair_sum(flying[2], landed, core, f"pair_sum_{n}")
        *flying, token = _split_start(_chip_copies, summed, summed.shape, 3, f"reduce_chips_{n}")
        token, after = tie(token, after)
        chip_flying.append((n, flying + [token]))
        return after

    def update(after, behind=None):
        n, flying = chip_flying.pop(0)
        if behind is not None:
            flying[4], _ = lax.optimization_barrier((flying[4], behind))
        landed = _split_wait(_chip_copies, *flying, f"reduce_chips_wait_{n}")
        turn = (lambda t: t.T) if n == "l1_w_in" else (lambda t: t)
        res = adamw_reduced(turn(w[n]), turn(mom[n]), turn(var[n]), flying[2], landed, chip, f"adamw_{n}")
        res, after = tie(res, after)
        out_g[n], out_d[n], out_m[n], out_v[n] = [turn(t) for t in res]
        return after, res[0]

    def on_grad(n, term, after):
        if n is None:
            return to_chips(after)
        if n in _ROW_SHARDED or n == "l1_w_in":
            term = term.reshape(NDEV, -1, D)
        else:
            term = term.reshape(NDEV, D, -1)
        term = term.reshape((4, 2) + term.shape[1:])
        *flying, token = _split_start(_pair_copies, term, term.shape[:1] + term.shape[2:], 4, f"reduce_pair_{n}")
        token, after = tie(token, after)
        if len(chip_flying) == UPDATE_LAG:
            after, _ = update(after)
        if pair_flying:
            after = to_chips(after)
        pair_flying.append((n, flying + [token]))
        return after

    for n in _BIG:
        start_gather(n)
    dx, g = local_step(xs, target, wts, at, on_grad)
    done = None
    while chip_flying:
        _, done = update(None, behind=done)
    loss = loss_sum[0]

    small_terms = [g[n] for n in _SMALL]
    small_shapes = [tuple(t.shape) for t in small_terms]
    packed = _pack(small_terms)
    all_terms = all_gather([packed], "gather_small_grads")[0]
    small_sum = _unpack(sum_slots(all_terms, "sum_small_grads"), small_shapes)
    small_g = {}
    for n, t in zip(_SMALL, small_sum):
        if n in _CONV:
            cols = w[n].shape[1]
            t = lax.dynamic_slice_in_dim(t, me * cols, cols, axis=1)
        small_g[n] = t.reshape(w[n].shape)
    shapes = [w[n].shape for n in _SMALL]
    res = adamw(
        _pack([w[n] for n in _SMALL]), _pack([mom[n] for n in _SMALL]), _pack([var[n] for n in _SMALL]),
        _pack([small_g[n] for n in _SMALL])[None], "adamw_small",
    )
    for dst, packed_out in zip((out_g, out_d, out_m, out_v), res):
        for n, t in zip(_SMALL, _unpack(packed_out, shapes)):
            dst[n] = t

    return (loss, dx[None], *[out_g[n] for n in _WEIGHTS], *[out_d[n] for n in _WEIGHTS],
            *[out_m[n] for n in _WEIGHTS], *[out_v[n] for n in _WEIGHTS])
```

```python
import functools
import math

import jax
import jax.numpy as jnp
from jax import lax
from jax.experimental import pallas as pl
from jax.experimental.pallas import tpu as pltpu
from jax.experimental.pallas import tpu_sc as plsc

F32 = jnp.float32
BF16 = jnp.bfloat16
HD = 128
EPS = 1e-6
CONV_K = 3
VMEM_LIMIT_BYTES = 48 << 20
NDEV = 8
MESH = pl.DeviceIdType.MESH

ADAM_LR = 0.001
ADAM_B1 = 0.9
ADAM_B2 = 0.999
ADAM_EPS = 1e-08
ADAM_WD = 0.01
ADAM_STEP = 10


def _cp(sem):
    return pltpu.CompilerParams(dimension_semantics=sem, vmem_limit_bytes=VMEM_LIMIT_BYTES)


def _pick(n, prefs):
    for p in prefs:
        if n % p == 0:
            return p
    return n


def _dot(a, b):
    return jnp.dot(a, b, preferred_element_type=F32)


def _dot_nt(a, b):
    return lax.dot_general(a, b, (((1,), (1,)), ((), ())), preferred_element_type=F32)


def _dot_tn(a, b):
    return lax.dot_general(a, b, (((0,), (0,)), ((), ())), preferred_element_type=F32)


def _split3(x):
    hi = x.astype(BF16)
    r = x - hi.astype(F32)
    mid = r.astype(BF16)
    lo = (r - mid.astype(F32)).astype(BF16)
    return hi, mid, lo


def _dot_ones_right(x, ones_bf16):
    hi, mid, lo = _split3(x)
    return _dot(hi, ones_bf16) + _dot(mid, ones_bf16) + _dot(lo, ones_bf16)


def _dot_ones_left(ones_bf16, x):
    hi, mid, lo = _split3(x)
    return _dot(ones_bf16, hi) + _dot(ones_bf16, mid) + _dot(ones_bf16, lo)


def _iota2(shape, axis):
    return lax.broadcasted_iota(jnp.int32, shape, axis)


def mm_nn(a, w2d, nb, name, out_dtype=BF16, res=None, tm=None, tn=None, tk=None, a_map=None, a_shape=None):
    M, K = a_shape or a.shape
    n = w2d.shape[1]
    assert w2d.shape[0] == nb * K or (nb == 1 and w2d.shape[0] > K)
    a_map = a_map or (lambda i, k: (i, k))
    tm = tm or _pick(M, (1024, 512, 256, 128))
    tn = tn or _pick(n, (1408, 1024, 768, 512, 256, 128))
    tk = tk or (K if K <= 2048 else _pick(K, (1408, 1024, 512, 256, 128)))
    nk, nt = K // tk, n // tn
    has_res = res is not None

    def body(*refs):
        if has_res:
            a_ref, w_ref, r_ref, o_ref = refs[:4]
        else:
            a_ref, w_ref, o_ref = refs[:3]
            r_ref = None
        part = _dot(a_ref[...], w_ref[...])

        def finish(acc):
            if r_ref is not None:
                acc = acc + r_ref[...].astype(F32)
            o_ref[...] = acc.astype(o_ref.dtype)

        if nk == 1:
            finish(part)
        else:
            acc_ref = refs[-1]
            k = pl.program_id(3)

            @pl.when(k == 0)
            def _():
                acc_ref[...] = part

            @pl.when(k > 0)
            def _():
                acc_ref[...] += part

            @pl.when(k == nk - 1)
            def _():
                finish(acc_ref[...])

    in_specs = [
        pl.BlockSpec((tm, tk), lambda i, j, t, k: a_map(i, k)),
        pl.BlockSpec((tk, tn), lambda i, j, t, k: (j * nk + k, t)),
    ]
    args = [a, w2d]
    out_spec = pl.BlockSpec((tm, tn), lambda i, j, t, k: (i, j * nt + t))
    if has_res:
        in_specs.append(out_spec)
        args.append(res)
    return pl.pallas_call(
        body,
        grid=(M // tm, nb, nt, nk),
        in_specs=in_specs,
        out_specs=out_spec,
        out_shape=jax.ShapeDtypeStruct((M, nb * n), out_dtype),
        scratch_shapes=[pltpu.VMEM((tm, tn), F32)] if nk > 1 else [],
        compiler_params=_cp(("parallel", "parallel", "parallel", "arbitrary")),
        name=name,
    )(*args)


def mm_nt(dy2d, w2d, nb, M, K, name, out_dtype=BF16, res=None, dy_maps=None, tm=None, tko=None, tn=None):
    n = w2d.shape[1]
    assert w2d.shape[0] == nb * K or (nb == 1 and w2d.shape[0] > K)
    tm = tm or _pick(M, (1024, 512, 256, 128))
    tko = tko or _pick(K, (1024, 512, 256, 128))
    tn = tn or _pick(n, (1408, 1024, 768, 512, 256, 128))
    nt, nko = n // tn, K // tko
    has_res = res is not None
    if dy_maps is None:
        dy_maps = [lambda i, j, t: (i, j * nt + t)]
    nd = len(dy_maps)
    td = tn // nd

    one_step = nb * nt == 1

    def body(*refs):
        d_refs, w_ref = refs[:nd], refs[nd]
        r_ref = refs[nd + 1] if has_res else None
        d = d_refs[0][...] if nd == 1 else jnp.concatenate([r[...] for r in d_refs], axis=1)
        part = _dot_nt(d, w_ref[...])
        if one_step:
            o_ref = refs[-1]
            if r_ref is not None:
                part = part + r_ref[...].astype(F32)
            o_ref[...] = part.astype(o_ref.dtype)
            return
        o_ref, acc_ref = refs[-2], refs[-1]
        j, t = pl.program_id(2), pl.program_id(3)
        first = jnp.logical_and(j == 0, t == 0)
        last = jnp.logical_and(j == nb - 1, t == nt - 1)

        @pl.when(first)
        def _():
            acc_ref[...] = part

        @pl.when(jnp.logical_not(first))
        def _():
            acc_ref[...] += part

        @pl.when(last)
        def _():
            acc = acc_ref[...]
            if r_ref is not None:
                acc = acc + r_ref[...].astype(F32)
            o_ref[...] = acc.astype(o_ref.dtype)

    in_specs = [pl.BlockSpec((tm, td), functools.partial(lambda f, i, ko, j, t: f(i, j, t), f)) for f in dy_maps]
    in_specs.append(pl.BlockSpec((tko, tn), lambda i, ko, j, t: (j * nko + ko, t)))
    args = [dy2d] * nd + [w2d]
    out_spec = pl.BlockSpec((tm, tko), lambda i, ko, j, t: (i, ko))
    if has_res:
        in_specs.append(out_spec)
        args.append(res)
    return pl.pallas_call(
        body,
        grid=(M // tm, nko, nb, nt),
        in_specs=in_specs,
        out_specs=out_spec,
        out_shape=jax.ShapeDtypeStruct((M, K), out_dtype),
        scratch_shapes=[] if one_step else [pltpu.VMEM((tm, tko), F32)],
        compiler_params=_cp(("parallel", "parallel", "arbitrary", "arbitrary")),
        name=name,
    )(*args)


def mm_tn(x, dy2d, nb, n, name, out_dtype=BF16, dy_maps=None, tko=None, tn=None, x_map=None, x_shape=None):
    S, K = x_shape or x.shape
    x_map = x_map or (lambda ko: (0, ko))
    tko = tko or _pick(K, (512, 256, 128))
    tn = tn or _pick(n, (1408, 1024, 768, 512, 256, 128))
    nt, nko = n // tn, K // tko
    if dy_maps is None:
        dy_maps = [lambda j, t: (0, j * nt + t)]
    nd = len(dy_maps)
    td = tn // nd

    def body(*refs):
        x_ref, d_refs, o_ref = refs[0], refs[1 : 1 + nd], refs[-1]
        d = d_refs[0][...] if nd == 1 else jnp.concatenate([r[...] for r in d_refs], axis=1)
        o_ref[...] = _dot_tn(x_ref[...], d).astype(o_ref.dtype)

    in_specs = [pl.BlockSpec((S, tko), lambda ko, j, t: x_map(ko))]
    in_specs += [pl.BlockSpec((S, td), functools.partial(lambda f, ko, j, t: f(j, t), f)) for f in dy_maps]
    return pl.pallas_call(
        body,
        grid=(nko, nb, nt),
        in_specs=in_specs,
        out_specs=pl.BlockSpec((tko, tn), lambda ko, j, t: (j * nko + ko, t)),
        out_shape=jax.ShapeDtypeStruct((nb * K, n), out_dtype),
        compiler_params=_cp(("parallel", "parallel", "parallel")),
        name=name,
    )(x, *([dy2d] * nd))


def rms_fwd(x, g, name):
    S, D = x.shape
    tm = _pick(S, (256, 128))

    def body(x_ref, g_ref, o_ref):
        xf = x_ref[...]
        r = lax.rsqrt(jnp.mean(xf * xf, axis=-1, keepdims=True) + EPS)
        o_ref[...] = (xf * r * g_ref[...]).astype(o_ref.dtype)

    return pl.pallas_call(
        body,
        grid=(S // tm,),
        in_specs=[pl.BlockSpec((tm, D), lambda i: (i, 0)), pl.BlockSpec((1, D), lambda i: (0, 0))],
        out_specs=pl.BlockSpec((tm, D), lambda i: (i, 0)),
        out_shape=jax.ShapeDtypeStruct((S, D), BF16),
        compiler_params=_cp(("parallel",)),
        name=name,
    )(x, g.reshape(1, D))


def rms_bwd(x, g, dh, dres, name):
    S, D = x.shape
    tm = _pick(S, (256, 128))

    def body(x_ref, g_ref, dh_ref, dr_ref, dx_ref, dxb_ref, dg_ref):
        i = pl.program_id(0)
        xf = x_ref[...]
        dh = dh_ref[...].astype(F32)
        r = lax.rsqrt(jnp.mean(xf * xf, axis=-1, keepdims=True) + EPS)
        gy = dh * g_ref[...]
        proj = jnp.mean(gy * xf, axis=-1, keepdims=True)
        dx = dr_ref[...] + r * gy - xf * (r * r * r * proj)
        dx_ref[...] = dx
        dxb_ref[...] = dx.astype(BF16)
        dg = jnp.sum(dh * (xf * r), axis=0, keepdims=True)

        @pl.when(i == 0)
        def _():
            dg_ref[...] = dg

        @pl.when(i > 0)
        def _():
            dg_ref[...] += dg

    row = pl.BlockSpec((tm, D), lambda i: (i, 0))
    vec = pl.BlockSpec((1, D), lambda i: (0, 0))
    return pl.pallas_call(
        body,
        grid=(S // tm,),
        in_specs=[row, vec, row, row],
        out_specs=[row, row, vec],
        out_shape=[jax.ShapeDtypeStruct((S, D), F32), jax.ShapeDtypeStruct((S, D), BF16), jax.ShapeDtypeStruct((1, D), F32)],
        compiler_params=_cp(("arbitrary",)),
        name=name,
    )(x, g.reshape(1, D), dh, dres)


def loss_head(x, g, target, name):
    S, D = x.shape
    tm = _pick(S, (256, 128))

    def body(x_ref, g_ref, t_ref, dx_ref, dxb_ref, dg_ref, loss_ref):
        i = pl.program_id(0)
        xf = x_ref[...]
        gg = g_ref[...]
        r = lax.rsqrt(jnp.mean(xf * xf, axis=-1, keepdims=True) + EPS)
        xh = xf * r
        err = xh * gg - t_ref[...]
        part = (0.5 / D) * jnp.sum(err * err)
        dy = err * (1.0 / D)
        gy = dy * gg
        proj = jnp.mean(gy * xf, axis=-1, keepdims=True)
        dx = r * gy - xf * (r * r * r * proj)
        dx_ref[...] = dx
        dxb_ref[...] = dx.astype(BF16)
        dg = jnp.sum(dy * xh, axis=0, keepdims=True)
        lossb = jnp.full(loss_ref.shape, part, F32)

        @pl.when(i == 0)
        def _():
            dg_ref[...] = dg
            loss_ref[...] = lossb

        @pl.when(i > 0)
        def _():
            dg_ref[...] += dg
            loss_ref[...] += lossb

    row = pl.BlockSpec((tm, D), lambda i: (i, 0))
    vec = pl.BlockSpec((1, D), lambda i: (0, 0))
    return pl.pallas_call(
        body,
        grid=(S // tm,),
        in_specs=[row, vec, row],
        out_specs=[row, row, vec, pl.BlockSpec((8, 128), lambda i: (0, 0))],
        out_shape=[
            jax.ShapeDtypeStruct((S, D), F32),
            jax.ShapeDtypeStruct((S, D), BF16),
            jax.ShapeDtypeStruct((1, D), F32),
            jax.ShapeDtypeStruct((8, 128), F32),
        ],
        compiler_params=_cp(("arbitrary",)),
        name=name,
    )(x, g.reshape(1, D), target)


def _shift_down(s, k):
    if k == 0:
        return s
    return jnp.where(_iota2(s.shape, 0) >= k, pltpu.roll(s, k, axis=0), 0.0)


def _shift_up(s, k):
    if k == 0:
        return s
    n = s.shape[0]
    return jnp.where(_iota2(s.shape, 0) < n - k, pltpu.roll(s, n - k, axis=0), 0.0)


def _conv(s, w):
    return w[0:1] * _shift_down(s, 2) + w[1:2] * _shift_down(s, 1) + w[2:3] * s


def _conv_t(d, w):
    return w[2:3] * d + w[1:2] * _shift_up(d, 1) + w[0:1] * _shift_up(d, 2)


def _conv_dw(d, s):
    return [jnp.sum(d * _shift_down(s, CONV_K - 1 - k), axis=0, keepdims=True) for k in range(CONV_K)]


def sc_fwd(p, convw, cat, W, name):
    S = p.shape[0]
    tc = _pick(W, (256, 128))
    nc = W // tc

    def body(gb_ref, gc_ref, hi_ref, w_ref, cat_ref, o_ref):
        s = gc_ref[...].astype(F32) * hi_ref[...].astype(F32)
        o_ref[...] = (gb_ref[...].astype(F32) * _conv(s, w_ref[...])).astype(o_ref.dtype)

    col = lambda part: pl.BlockSpec((S, tc), lambda c: (0, part * nc + c))
    return pl.pallas_call(
        body,
        grid=(nc,),
        in_specs=[col(3), col(4), col(5), pl.BlockSpec((CONV_K, tc), lambda c: (0, c)), pl.BlockSpec(memory_space=pl.ANY)],
        out_specs=col(1),
        out_shape=jax.ShapeDtypeStruct(cat.shape, cat.dtype),
        input_output_aliases={4: 0},
        compiler_params=_cp(("parallel",)),
        name=name,
    )(p, p, p, convw, cat)


def sc_bwd(p, convw, dcat, dp, W, name):
    S = p.shape[0]
    tc = _pick(W, (256, 128))
    nc = W // tc

    def body(gb_ref, gc_ref, hi_ref, w_ref, do_ref, dp_in_ref, dp_ref, dw_ref):
        gb = gb_ref[...].astype(F32)
        gc = gc_ref[...].astype(F32)
        hi = hi_ref[...].astype(F32)
        w = w_ref[...]
        do = do_ref[...].astype(F32)
        s = gc * hi
        dcs = do * gb
        ds = _conv_t(dcs, w)
        dp_ref[0] = (do * _conv(s, w)).astype(dp_ref.dtype)
        dp_ref[1] = (ds * hi).astype(dp_ref.dtype)
        dp_ref[2] = (ds * gc).astype(dp_ref.dtype)
        for k, row in enumerate(_conv_dw(dcs, s)):
            dw_ref[k : k + 1, :] = row

    col = lambda part: pl.BlockSpec((S, tc), lambda c: (0, part * nc + c))
    return pl.pallas_call(
        body,
        grid=(nc,),
        in_specs=[
            col(3), col(4), col(5),
            pl.BlockSpec((CONV_K, tc), lambda c: (0, c)),
            pl.BlockSpec((S, tc), lambda c: (0, nc + c)),
            pl.BlockSpec(memory_space=pl.ANY),
        ],
        out_specs=[pl.BlockSpec((3, S, tc), lambda c: (1, 0, c)), pl.BlockSpec((CONV_K, tc), lambda c: (0, c))],
        out_shape=[jax.ShapeDtypeStruct(dp.shape, dp.dtype), jax.ShapeDtypeStruct((CONV_K, W), F32)],
        input_output_aliases={5: 0},
        compiler_params=_cp(("parallel",)),
        name=name,
    )(p, p, p, convw, dcat, dp)


def _silu_parts(a):
    sig = 1.0 / (1.0 + jnp.exp(-a))
    return a * sig, sig


def ffn_act_fwd(u, convw, F, name):
    S = u.shape[0]
    tc = _pick(F, (256, 128))
    nc = F // tc

    def body(ug_ref, uu_ref, wg_ref, wu_ref, o_ref):
        ag = _conv(ug_ref[...].astype(F32), wg_ref[...])
        au = _conv(uu_ref[...].astype(F32), wu_ref[...])
        o_ref[...] = (_silu_parts(ag)[0] * au).astype(o_ref.dtype)

    col = lambda half: pl.BlockSpec((S, tc), lambda c: (0, half * nc + c))
    wcol = lambda half: pl.BlockSpec((CONV_K, tc), lambda c: (0, half * nc + c))
    return pl.pallas_call(
        body,
        grid=(nc,),
        in_specs=[col(0), col(1), wcol(0), wcol(1)],
        out_specs=pl.BlockSpec((S, tc), lambda c: (0, c)),
        out_shape=jax.ShapeDtypeStruct((S, F), BF16),
        compiler_params=_cp(("parallel",)),
        name=name,
    )(u, u, convw, convw)


def ffn_act_bwd(u, convw, dact, F, name):
    S = u.shape[0]
    tc = _pick(F, (256, 128))
    nc = F // tc

    def body(ug_ref, uu_ref, wg_ref, wu_ref, da_ref, du_ref, dw_ref):
        ug = ug_ref[...].astype(F32)
        uu = uu_ref[...].astype(F32)
        wg = wg_ref[...]
        wu = wu_ref[...]
        da = da_ref[...].astype(F32)
        ag = _conv(ug, wg)
        au = _conv(uu, wu)
        sl, sig = _silu_parts(ag)
        dag = da * au * (sig * (1.0 + ag * (1.0 - sig)))
        dau = da * sl
        du_ref[0] = _conv_t(dag, wg).astype(du_ref.dtype)
        du_ref[1] = _conv_t(dau, wu).astype(du_ref.dtype)
        for k, (rg, ru) in enumerate(zip(_conv_dw(dag, ug), _conv_dw(dau, uu))):
            dw_ref[0, k : k + 1, :] = rg
            dw_ref[1, k : k + 1, :] = ru

    col = lambda half: pl.BlockSpec((S, tc), lambda c: (0, half * nc + c))
    wcol = lambda half: pl.BlockSpec((CONV_K, tc), lambda c: (0, half * nc + c))
    return pl.pallas_call(
        body,
        grid=(nc,),
        in_specs=[col(0), col(1), wcol(0), wcol(1), pl.BlockSpec((S, tc), lambda c: (0, c))],
        out_specs=[pl.BlockSpec((2, S, tc), lambda c: (0, 0, c)), pl.BlockSpec((2, CONV_K, tc), lambda c: (0, 0, c))],
        out_shape=[jax.ShapeDtypeStruct((2, S, F), BF16), jax.ShapeDtypeStruct((2, CONV_K, F), F32)],
        compiler_params=_cp(("parallel",)),
        name=name,
    )(u, u, convw, convw, dact)


def _softplus(z):
    return jnp.maximum(z, 0.0) + jnp.log(1.0 + jnp.exp(-jnp.abs(z)))


def _key_strip(S):
    return _pick(S, (512, 256, 128))


def _query_rows(S):
    return _pick(S, (512, 256, 128))


def _split2(x):
    hi = x.astype(BF16)
    return hi, (x - hi.astype(F32)).astype(BF16)


def _block_sums(x, ones_bf16):
    hi, lo = _split2(x)
    return [
        _dot(hi[:, b * HD : (b + 1) * HD], ones_bf16) + _dot(lo[:, b * HD : (b + 1) * HD], ones_bf16)
        for b in range(x.shape[1] // HD)
    ]


def _strip_mask(shape, row0, off, strict):
    cols, rows = _iota2(shape, 1) + off, _iota2(shape, 0) + row0
    return cols < rows if strict else cols <= rows


def _sb_strip(q, ks, row0, off, run, su):
    z = _dot_nt(q, ks) * (HD ** -0.5)
    mask = _strip_mask(z.shape, row0, off, True)
    sp = _softplus(z)
    l = jnp.where(mask, -sp, 0.0)
    within = _block_sums(l, su)
    later = [None] * len(within)
    for b in reversed(range(len(within))):
        later[b] = within[b] + run
        run = run + jnp.sum(l[:, b * HD : (b + 1) * HD], axis=1, keepdims=True)
    a = jnp.where(mask, jnp.exp(z - sp + jnp.concatenate(later, axis=1)), 0.0)
    return z, mask, a, run


def sb_fwd(p, W, name):
    S = p.shape[0]
    TQ, TK = _query_rows(S), _key_strip(S)
    nh, nq = W // HD, S // TQ

    def body(q_ref, k_ref, v_ref, o_ref):
        i = pl.program_id(1)
        q = q_ref[...]
        su = (_iota2((HD, HD), 0) > _iota2((HD, HD), 1)).astype(BF16)
        last = (i * TQ + TQ - 1) // TK

        def step(gg, carry):
            acc, run = carry
            off = pl.multiple_of((last - gg) * TK, TK)
            _, _, a, run = _sb_strip(q, k_ref[pl.ds(off, TK), :], i * TQ, off, run, su)
            return acc + _dot(a.astype(BF16), v_ref[pl.ds(off, TK), :]), run

        acc, _ = lax.fori_loop(0, last + 1, step, (jnp.zeros((TQ, HD), F32), jnp.zeros((TQ, 1), F32)))
        o_ref[...] = acc.astype(o_ref.dtype)

    return pl.pallas_call(
        body,
        grid=(nh, nq),
        in_specs=[
            pl.BlockSpec((TQ, HD), lambda h, i: (i, h)),
            pl.BlockSpec((S, HD), lambda h, i: (0, nh + h)),
            pl.BlockSpec((S, HD), lambda h, i: (0, 2 * nh + h)),
        ],
        out_specs=pl.BlockSpec((TQ, HD), lambda h, i: (i, h)),
        out_shape=jax.ShapeDtypeStruct((S, 2 * W), BF16),
        compiler_params=_cp(("parallel", "arbitrary")),
        name=name,
    )(p, p, p)


def sb_bwd(p, dcat, W, name):
    S = p.shape[0]
    TQ, TK = _query_rows(S), _key_strip(S)
    nh, nq = W // HD, S // TQ
    scale = HD ** -0.5

    def body(q_ref, k_ref, v_ref, do_ref, dp_ref, dk_acc, dv_acc, e_scr, z_scr):
        i = pl.program_id(1)
        q = q_ref[...]
        do = do_ref[...]
        su = (_iota2((HD, HD), 0) > _iota2((HD, HD), 1)).astype(BF16)
        sl = (_iota2((HD, HD), 0) < _iota2((HD, HD), 1)).astype(BF16)
        last = (i * TQ + TQ - 1) // TK

        @pl.when(i == 0)
        def _():
            dk_acc[...] = jnp.zeros_like(dk_acc)
            dv_acc[...] = jnp.zeros_like(dv_acc)

        def pass_a(gg, run):
            g = last - gg
            off = pl.multiple_of(g * TK, TK)
            z, _, a, run = _sb_strip(q, k_ref[pl.ds(off, TK), :], i * TQ, off, run, su)
            e_scr[g] = a * _dot_nt(do, v_ref[pl.ds(off, TK), :])
            z_scr[g] = z
            dv_acc[pl.ds(off, TK), :] += _dot_tn(a.astype(BF16), do)
            return run

        lax.fori_loop(0, last + 1, pass_a, jnp.zeros((TQ, 1), F32))

        def pass_b(g, carry):
            dq, run_e = carry
            off = pl.multiple_of(g * TK, TK)
            e = e_scr[g]
            z = z_scr[g]
            mask = _strip_mask(z.shape, i * TQ, off, True)
            within = _block_sums(e, sl)
            before = []
            for b in range(len(within)):
                before.append(within[b] + run_e)
                run_e = run_e + jnp.sum(e[:, b * HD : (b + 1) * HD], axis=1, keepdims=True)
            sig = 1.0 / (1.0 + jnp.exp(-z))
            dz = jnp.where(mask, e * (1.0 - sig) - jnp.concatenate(before, axis=1) * sig, 0.0)
            dz = (dz * scale).astype(BF16)
            dq = dq + _dot(dz, k_ref[pl.ds(off, TK), :])
            dk_acc[pl.ds(off, TK), :] += _dot_tn(dz, q)
            return dq, run_e

        dq, _ = lax.fori_loop(0, last + 1, pass_b, (jnp.zeros((TQ, HD), F32), jnp.zeros((TQ, 1), F32)))
        dp_ref[0, pl.ds(pl.multiple_of(i * TQ, TQ), TQ), :] = dq.astype(dp_ref.dtype)

        @pl.when(i == nq - 1)
        def _():
            dp_ref[1] = dk_acc[...].astype(dp_ref.dtype)
            dp_ref[2] = dv_acc[...].astype(dp_ref.dtype)

    return pl.pallas_call(
        body,
        grid=(nh, nq),
        in_specs=[
            pl.BlockSpec((TQ, HD), lambda h, i: (i, h)),
            pl.BlockSpec((S, HD), lambda h, i: (0, nh + h)),
            pl.BlockSpec((S, HD), lambda h, i: (0, 2 * nh + h)),
            pl.BlockSpec((TQ, HD), lambda h, i: (i, h)),
        ],
        out_specs=pl.BlockSpec((3, S, HD), lambda h, i: (0, 0, h)),
        out_shape=jax.ShapeDtypeStruct((6, S, W), BF16),
        scratch_shapes=[
            pltpu.VMEM((S, HD), F32),
            pltpu.VMEM((S, HD), F32),
            pltpu.VMEM((S // TK, TQ, TK), F32),
            pltpu.VMEM((S // TK, TQ, TK), F32),
        ],
        compiler_params=_cp(("parallel", "arbitrary")),
        name=name,
    )(p, p, p, dcat)


def fox_gate_fwd(f, b, name):
    S = f.shape[0]
    nq = S // HD

    def body(f_ref, b_ref, c_ref, run):
        i = pl.program_id(0)

        @pl.when(i == 0)
        def _():
            run[...] = jnp.zeros_like(run)

        lf = -_softplus(-(f_ref[...] + b_ref[...]))
        tri = (_iota2((HD, HD), 0) >= _iota2((HD, HD), 1)).astype(BF16)
        c_ref[...] = _dot_ones_left(tri, lf) + run[...]
        run[...] += jnp.sum(lf, axis=0, keepdims=True)

    return pl.pallas_call(
        body,
        grid=(nq,),
        in_specs=[pl.BlockSpec((HD, 128), lambda i: (i, 0)), pl.BlockSpec((1, 128), lambda i: (0, 0))],
        out_specs=pl.BlockSpec((HD, 128), lambda i: (i, 0)),
        out_shape=jax.ShapeDtypeStruct((S, 128), F32),
        scratch_shapes=[pltpu.VMEM((1, 128), F32)],
        compiler_params=_cp(("arbitrary",)),
        name=name,
    )(f, b)


def fox_gate_bwd(f, b, dc, name):
    S = f.shape[0]
    nq = S // HD

    def body(f_ref, b_ref, dc_ref, df_ref, db_ref, run):
        i = pl.program_id(0)

        @pl.when(i == 0)
        def _():
            run[...] = jnp.zeros_like(run)

        dc = dc_ref[...]
        tri = (_iota2((HD, HD), 0) <= _iota2((HD, HD), 1)).astype(BF16)
        dlf = _dot_ones_left(tri, dc) + run[...]
        run[...] += jnp.sum(dc, axis=0, keepdims=True)
        x = f_ref[...] + b_ref[...]
        df = dlf * (1.0 / (1.0 + jnp.exp(x)))
        df_ref[...] = df
        db = jnp.sum(df, axis=0, keepdims=True)

        @pl.when(i == 0)
        def _():
            db_ref[...] = db

        @pl.when(i > 0)
        def _():
            db_ref[...] += db

    rev = pl.BlockSpec((HD, 128), lambda i: (nq - 1 - i, 0))
    vec = pl.BlockSpec((1, 128), lambda i: (0, 0))
    return pl.pallas_call(
        body,
        grid=(nq,),
        in_specs=[rev, vec, rev],
        out_specs=[rev, vec],
        out_shape=[jax.ShapeDtypeStruct((S, 128), F32), jax.ShapeDtypeStruct((1, 128), F32)],
        scratch_shapes=[pltpu.VMEM((1, 128), F32)],
        compiler_params=_cp(("arbitrary",)),
        name=name,
    )(f, b, dc)


def _fox_logits(q, ks, ct, cs, row0, off):
    s = _dot_nt(q, ks) * (HD ** -0.5) + (ct - cs)
    mask = _strip_mask(s.shape, row0, off, False)
    return jnp.where(mask, s, -1e30), mask


def fox_fwd(p, ccol, crow, cat, W, name):
    S = p.shape[0]
    TQ, TK = _query_rows(S), _key_strip(S)
    nh, nq = W // HD, S // TQ

    def body(q_ref, k_ref, v_ref, cc_ref, cr_ref, cat_ref, o_ref, lse_ref):
        i = pl.program_id(1)
        q = q_ref[...]
        ct = cc_ref[0]

        def step(g, carry):
            m, l, acc = carry
            off = pl.multiple_of(g * TK, TK)
            s, _ = _fox_logits(q, k_ref[pl.ds(off, TK), :], ct, cr_ref[0, pl.ds(g, 1), :], i * TQ, off)
            m_new = jnp.maximum(m, jnp.max(s, axis=1, keepdims=True))
            alpha = jnp.exp(m - m_new)
            pr = jnp.exp(s - m_new)
            l = alpha * l + jnp.sum(pr, axis=1, keepdims=True)
            acc = alpha * acc + _dot(pr.astype(BF16), v_ref[pl.ds(off, TK), :])
            return m_new, l, acc

        init = (jnp.full((TQ, 1), -1e30, F32), jnp.zeros((TQ, 1), F32), jnp.zeros((TQ, HD), F32))
        m, l, acc = lax.fori_loop(0, (i * TQ + TQ - 1) // TK + 1, step, init)
        o_ref[...] = (acc / l).astype(o_ref.dtype)
        lse_ref[0] = m + jnp.log(l)

    return pl.pallas_call(
        body,
        grid=(nh, nq),
        in_specs=[
            pl.BlockSpec((TQ, HD), lambda h, i: (i, 2 * nh + h)),
            pl.BlockSpec((S, HD), lambda h, i: (0, 3 * nh + h)),
            pl.BlockSpec((S, HD), lambda h, i: (0, 4 * nh + h)),
            pl.BlockSpec((1, TQ, 1), lambda h, i: (h, i, 0)),
            pl.BlockSpec((1, S // TK, TK), lambda h, i: (h, 0, 0)),
            pl.BlockSpec(memory_space=pl.ANY),
        ],
        out_specs=[pl.BlockSpec((TQ, HD), lambda h, i: (i, nh + h)), pl.BlockSpec((1, TQ, 1), lambda h, i: (h, i, 0))],
        out_shape=[jax.ShapeDtypeStruct(cat.shape, cat.dtype), jax.ShapeDtypeStruct((nh, S, 1), F32)],
        input_output_aliases={5: 0},
        compiler_params=_cp(("parallel", "arbitrary")),
        name=name,
    )(p, p, p, ccol, crow, cat)


def fox_bwd(p, ccol, crow, cat, lse, dcat, dp, W, name):
    S = p.shape[0]
    TQ, TK = _query_rows(S), _key_strip(S)
    nh, nq = W // HD, S // TQ
    scale = HD ** -0.5

    def body(q_ref, k_ref, v_ref, cc_ref, cr_ref, o_ref, lse_ref, do_ref, dp_in_ref, dp_ref, dcs_ref, dct_ref, dk_acc, dv_acc):
        i = pl.program_id(1)
        q = q_ref[...]
        do = do_ref[...]
        ct = cc_ref[0]
        lse_i = lse_ref[0]
        delta = jnp.sum(do.astype(F32) * o_ref[...].astype(F32), axis=1, keepdims=True)

        @pl.when(i == 0)
        def _():
            dk_acc[...] = jnp.zeros_like(dk_acc)
            dv_acc[...] = jnp.zeros_like(dv_acc)
            dcs_ref[...] = jnp.zeros_like(dcs_ref)

        def step(g, carry):
            dq, dct = carry
            off = pl.multiple_of(g * TK, TK)
            ks = k_ref[pl.ds(off, TK), :]
            s, mask = _fox_logits(q, ks, ct, cr_ref[0, pl.ds(g, 1), :], i * TQ, off)
            pr = jnp.where(mask, jnp.exp(s - lse_i), 0.0)
            ds = pr * (_dot_nt(do, v_ref[pl.ds(off, TK), :]) - delta)
            dv_acc[pl.ds(off, TK), :] += _dot_tn(pr.astype(BF16), do)
            dsb = (ds * scale).astype(BF16)
            dk_acc[pl.ds(off, TK), :] += _dot_tn(dsb, q)
            dcs_ref[0, pl.ds(g, 1), :] += jnp.sum(ds, axis=0, keepdims=True)
            return dq + _dot(dsb, ks), dct + jnp.sum(ds, axis=1, keepdims=True)

        dq, dct = lax.fori_loop(0, (i * TQ + TQ - 1) // TK + 1, step, (jnp.zeros((TQ, HD), F32), jnp.zeros((TQ, 1), F32)))
        dp_ref[0, pl.ds(pl.multiple_of(i * TQ, TQ), TQ), :] = dq.astype(dp_ref.dtype)
        dct_ref[0] = dct

        @pl.when(i == nq - 1)
        def _():
            dp_ref[1] = dk_acc[...].astype(dp_ref.dtype)
            dp_ref[2] = dv_acc[...].astype(dp_ref.dtype)

    return pl.pallas_call(
        body,
        grid=(nh, nq),
        in_specs=[
            pl.BlockSpec((TQ, HD), lambda h, i: (i, 2 * nh + h)),
            pl.BlockSpec((S, HD), lambda h, i: (0, 3 * nh + h)),
            pl.BlockSpec((S, HD), lambda h, i: (0, 4 * nh + h)),
            pl.BlockSpec((1, TQ, 1), lambda h, i: (h, i, 0)),
            pl.BlockSpec((1, S // TK, TK), lambda h, i: (h, 0, 0)),
            pl.BlockSpec((TQ, HD), lambda h, i: (i, nh + h)),
            pl.BlockSpec((1, TQ, 1), lambda h, i: (h, i, 0)),
            pl.BlockSpec((TQ, HD), lambda h, i: (i, nh + h)),
            pl.BlockSpec(memory_space=pl.ANY),
        ],
        out_specs=[
            pl.BlockSpec((3, S, HD), lambda h, i: (1, 0, h)),
            pl.BlockSpec((1, S // TK, TK), lambda h, i: (h, 0, 0)),
            pl.BlockSpec((1, TQ, 1), lambda h, i: (h, i, 0)),
        ],
        out_shape=[
            jax.ShapeDtypeStruct(dp.shape, dp.dtype),
            jax.ShapeDtypeStruct((nh, S // TK, TK), F32),
            jax.ShapeDtypeStruct((nh, S, 1), F32),
        ],
        input_output_aliases={8: 0},
        scratch_shapes=[pltpu.VMEM((S, HD), F32), pltpu.VMEM((S, HD), F32)],
        compiler_params=_cp(("parallel", "arbitrary")),
        name=name,
    )(p, p, p, ccol, crow, cat, lse, dcat, dp)


_GELU_K = math.sqrt(2.0 / math.pi)
_GELU_C = 0.044715


def _gelu(x):
    return 0.5 * x * (1.0 + jnp.tanh(_GELU_K * (x + _GELU_C * x * x * x)))


def _gelu_grad(x):
    t = jnp.tanh(_GELU_K * (x + _GELU_C * x * x * x))
    return 0.5 * (1.0 + t) + 0.5 * x * (1.0 - t * t) * (_GELU_K * (1.0 + 3.0 * _GELU_C * x * x))


def _layernorm_parts(gv):
    xc = gv - jnp.mean(gv, axis=-1, keepdims=True)
    r = lax.rsqrt(jnp.mean(xc * xc, axis=-1, keepdims=True) + EPS)
    return xc * r, r


def sg_fwd(p, sg_w, sg_bt, sg_g, W, name):
    S = p.shape[0]
    G, nq = W // HD, S // HD

    def body(u_ref, v_ref, w_ref, bt_ref, g_ref, o_ref):
        xh, _ = _layernorm_parts(_gelu(v_ref[...].astype(F32)))
        vn = (xh * g_ref[...]).astype(BF16)
        tri = _iota2((HD, HD), 0) >= _iota2((HD, HD), 1)
        for gi in range(G):
            cols = slice(gi * HD, (gi + 1) * HD)
            wt = jnp.where(tri, w_ref[gi], 0.0).astype(BF16)
            mixed = _dot(wt, vn[:, cols]) + bt_ref[:, gi : gi + 1]
            o_ref[:, cols] = (_gelu(u_ref[:, cols].astype(F32)) * mixed).astype(o_ref.dtype)

    return pl.pallas_call(
        body,
        grid=(nq,),
        in_specs=[
            pl.BlockSpec((HD, W), lambda i: (i, 0)),
            pl.BlockSpec((HD, W), lambda i: (i, 1)),
            pl.BlockSpec((G, HD, HD), lambda i: (0, 0, 0)),
            pl.BlockSpec((HD, G), lambda i: (0, 0)),
            pl.BlockSpec((1, W), lambda i: (0, 0)),
        ],
        out_specs=pl.BlockSpec((HD, W), lambda i: (i, 0)),
        out_shape=jax.ShapeDtypeStruct((S, 2 * W), BF16),
        compiler_params=_cp(("parallel",)),
        name=name,
    )(p, p, sg_w, sg_bt, sg_g.reshape(1, W))


def sg_bwd(p, sg_w, sg_bt, sg_g, dcat, W, name):
    S = p.shape[0]
    G, nq = W // HD, S // HD

    def body(u_ref, v_ref, w_ref, bt_ref, g_ref, do_ref, dp_ref, dw_ref, dbt_ref, dg_ref, dvn_scr):
        i = pl.program_id(0)

        @pl.when(i == 0)
        def _():
            dw_ref[...] = jnp.zeros_like(dw_ref)
            dbt_ref[...] = jnp.zeros_like(dbt_ref)
            dg_ref[...] = jnp.zeros_like(dg_ref)

        v = v_ref[...].astype(F32)
        xh, r = _layernorm_parts(_gelu(v))
        gg = g_ref[...]
        vn = (xh * gg).astype(BF16)
        tri = _iota2((HD, HD), 0) >= _iota2((HD, HD), 1)
        for gi in range(G):
            cols = slice(gi * HD, (gi + 1) * HD)
            wt = jnp.where(tri, w_ref[gi], 0.0).astype(BF16)
            mixed = _dot(wt, vn[:, cols]) + bt_ref[:, gi : gi + 1]
            u = u_ref[:, cols].astype(F32)
            do = do_ref[:, cols].astype(F32)
            dp_ref[0, :, cols] = (do * mixed * _gelu_grad(u)).astype(dp_ref.dtype)
            dmix = do * _gelu(u)
            dmb = dmix.astype(BF16)
            dw_ref[gi] += jnp.where(tri, _dot_nt(dmb, vn[:, cols]), 0.0)
            dbt_ref[:, gi : gi + 1] += jnp.sum(dmix, axis=1, keepdims=True)
            dvn_scr[:, cols] = _dot_tn(wt, dmb)
        dvn = dvn_scr[...]
        dg_ref[...] += jnp.sum(dvn * xh, axis=0, keepdims=True)
        dxh = dvn * gg
        dgv = r * (dxh - jnp.mean(dxh, axis=-1, keepdims=True) - xh * jnp.mean(dxh * xh, axis=-1, keepdims=True))
        dp_ref[1] = (dgv * _gelu_grad(v)).astype(dp_ref.dtype)

    return pl.pallas_call(
        body,
        grid=(nq,),
        in_specs=[
            pl.BlockSpec((HD, W), lambda i: (i, 0)),
            pl.BlockSpec((HD, W), lambda i: (i, 1)),
            pl.BlockSpec((G, HD, HD), lambda i: (0, 0, 0)),
            pl.BlockSpec((HD, G), lambda i: (0, 0)),
            pl.BlockSpec((1, W), lambda i: (0, 0)),
            pl.BlockSpec((HD, W), lambda i: (i, 0)),
        ],
        out_specs=[
            pl.BlockSpec((2, HD, W), lambda i: (0, i, 0)),
            pl.BlockSpec((G, HD, HD), lambda i: (0, 0, 0)),
            pl.BlockSpec((HD, G), lambda i: (0, 0)),
            pl.BlockSpec((1, W), lambda i: (0, 0)),
        ],
        out_shape=[
            jax.ShapeDtypeStruct((6, S, W), BF16),
            jax.ShapeDtypeStruct((G, HD, HD), F32),
            jax.ShapeDtypeStruct((HD, G), F32),
            jax.ShapeDtypeStruct((1, W), F32),
        ],
        scratch_shapes=[pltpu.VMEM((HD, W), F32)],
        compiler_params=_cp(("arbitrary",)),
        name=name,
    )(p, p, sg_w, sg_bt, sg_g.reshape(1, W), dcat)


def local_step(x, target, wts, at, on_grad):
    S, D = x.shape
    W = D // 2
    nb, F = wts["nb"], wts["F"]
    g = {}

    def ffn_fwd(xin, l):
        h = rms_fwd(xin, wts[f"{l}_ffn_norm_g"], f"{l}_ffn_rms")
        u = mm_nn(h, wts[f"{l}_ffn_up"], nb, f"{l}_ffn_up_mm")
        act = ffn_act_fwd(u, wts[f"{l}_ffn_conv_w"], F, f"{l}_ffn_act")
        half_tile = _pick(S, (512, 256, 128))
        xout = mm_nn(act, wts[f"{l}_ffn_down"], 1, f"{l}_ffn_down_mm", out_dtype=F32, res=xin,
                     tm=half_tile, tn=_pick(D, (512, 256, 128)), tk=F)
        return xout, (xin, h, u, act)

    def ffn_bwd(dxout, dxoutb, saved, l):
        xin, h, u, act = saved
        dact = mm_nt(dxoutb, wts[f"{l}_ffn_down"], 1, S, F, f"{l}_ffn_down_dx", tko=_pick(F, (512, 256, 128)), tn=D)
        dact = on_grad(f"{l}_ffn_down", mm_tn(act, dxoutb, 1, D, f"{l}_ffn_down_dw", tn=D), dact)
        du, dcw = ffn_act_bwd(u, wts[f"{l}_ffn_conv_w"], dact, F, f"{l}_ffn_act_bwd")
        g[f"{l}_ffn_conv_w"] = jnp.concatenate([dcw[0], dcw[1]], axis=1)
        du2 = du.reshape(2 * S, F)
        n = wts[f"{l}_ffn_up"].shape[1]
        tn = _pick(n, (1408, 1024, 768, 512, 256, 128))
        per_half = F // tn
        nt = n // tn

        def up_block(i, j, t):
            vb = j * nt + t
            return vb // per_half, vb % per_half

        tm = _pick(S, (1024, 512, 256, 128))

        def nt_map(i, j, t):
            half, cb = up_block(i, j, t)
            return (half * (S // tm) + i, cb)

        def tn_map(j, t):
            half, cb = up_block(0, j, t)
            return (half, cb)

        dh = mm_nt(du2, wts[f"{l}_ffn_up"], nb, S, D, f"{l}_ffn_up_dx", dy_maps=[nt_map], tm=tm, tko=D, tn=tn)
        dh = on_grad(f"{l}_ffn_up", mm_tn(h, du2, nb, n, f"{l}_ffn_up_dw", dy_maps=[tn_map], tn=tn), dh)
        dxin, dxinb, dg = rms_bwd(xin, wts[f"{l}_ffn_norm_g"], dh, dxout, f"{l}_ffn_rms_bwd")
        g[f"{l}_ffn_norm_g"] = dg
        return dxin, dxinb

    h0 = rms_fwd(x, wts["l0_mix_norm_g"], "l0_mix_rms")
    p0 = mm_nn(h0, wts["l0_w_in"], nb, "l0_w_in_mm")
    cat0 = sb_fwd(p0, W, "l0_sb_fwd")
    cat0 = sc_fwd(p0, wts["l0_sc_conv_w"], cat0, W, "l0_sc_fwd")
    x1 = mm_nn(cat0, wts["l0_w_out"], 1, "l0_w_out_mm", out_dtype=F32, res=x, tm=S, tn=_pick(D, (512, 256, 128)))
    x2, ffn0_saved = ffn_fwd(x1, "l0")

    x2 = at("l1_w_in", x2, None)
    nh = W // HD
    h2 = rms_fwd(x2, wts["l1_mix_norm_g"], "l1_mix_rms")
    p1 = mm_nt(h2, wts["l1_w_in_t"], 1, S, 5 * W, "l1_w_in_mm", tn=D)
    f = mm_nt(h2, wts["l1_w_f_t"], 1, S, 128, "l1_w_f_mm", out_dtype=F32, tn=D)
    bf = jnp.zeros((1, 128), F32).at[0, :nh].set(wts["l1_fox_b_f"])
    c = fox_gate_fwd(f, bf, "l1_fox_gate")
    c_heads = c[:, :nh].T
    ccol = c_heads[:, :, None]
    crow = c_heads.reshape(nh, S // _key_strip(S), _key_strip(S))
    sg_bt = wts["l1_sg_b"].T
    cat1 = sg_fwd(p1, wts["l1_sg_w"], sg_bt, wts["l1_sg_norm_g"], W, "l1_sg_fwd")
    cat1, lse = fox_fwd(p1, ccol, crow, cat1, W, "l1_fox_fwd")
    x3 = mm_nn(cat1, wts["l1_w_out"], 1, "l1_w_out_mm", out_dtype=F32, res=x2, tm=S, tn=_pick(D, (512, 256, 128)))
    x4, ffn1_saved = ffn_fwd(x3, "l1")

    dx4, dx4b, dgf, loss = loss_head(x4, wts["final_norm_g"], target, "loss_head")
    dx4b = at("loss", dx4b, loss)
    g["final_norm_g"] = dgf

    dx3, dx3b = ffn_bwd(dx4, dx4b, ffn1_saved, "l1")
    dcat1 = mm_nt(dx3b, wts["l1_w_out"], 1, S, D, "l1_w_out_dx", tn=D)
    dcat1 = on_grad("l1_w_out", mm_tn(cat1, dx3b, 1, D, "l1_w_out_dw", tn=D), dcat1)
    dp1, dsgw, dsgbt, dsgg = sg_bwd(p1, wts["l1_sg_w"], sg_bt, wts["l1_sg_norm_g"], dcat1, W, "l1_sg_bwd")
    dp1, dcs, dct = fox_bwd(p1, ccol, crow, cat1, lse, dcat1, dp1, W, "l1_fox_bwd")
    g["l1_sg_w"], g["l1_sg_b"], g["l1_sg_norm_g"] = dsgw, dsgbt.T, dsgg
    dc = jnp.zeros((S, 128), F32).at[:, :nh].set((dct[:, :, 0] - dcs.reshape(nh, S)).T)
    df, dbf = fox_gate_bwd(f, bf, dc, "l1_fox_gate_bwd")
    g["l1_fox_b_f"] = dbf[0, :nh]
    dfb = df.astype(BF16)
    tk1 = _pick(W, (1024, 512, 256, 128))
    tx1 = _pick(W, (512, 256, 128))
    tm1 = _pick(S, (1024, 512, 256, 128))
    part_of = lambda pt: pt + pt // 2 - pt // 4

    def a_map1(i, k):
        return (part_of(k // (W // tk1)) * (S // tm1) + i, k % (W // tk1))

    def x_map1(ko):
        return (part_of(ko // (W // tx1)), ko % (W // tx1))

    dp1_2d = dp1.reshape(6 * S, W)
    dw_main = mm_tn(dp1_2d, h2, 1, D, "l1_w_in_dw", tko=tx1, tn=D, x_map=x_map1, x_shape=(S, 5 * W))
    dw_f = mm_tn(dfb, h2, 1, D, "l1_w_f_dw", tn=D)
    dh2 = mm_nn(dfb, wts["l1_w_f_t"], 1, "l1_w_f_dx", out_dtype=F32)
    dh2 = mm_nn(dp1_2d, wts["l1_w_in_t"], 1, "l1_w_in_dx", res=dh2, tm=tm1, tk=tk1, a_map=a_map1, a_shape=(S, 5 * W))
    dh2 = on_grad("l1_w_in", jnp.concatenate([dw_main, dw_f[:nh]], axis=0), dh2)
    dx2, dx2b, dg = rms_bwd(x2, wts["l1_mix_norm_g"], dh2, dx3, "l1_mix_rms_bwd")
    g["l1_mix_norm_g"] = dg

    dx1, dx1b = ffn_bwd(dx2, dx2b, ffn0_saved, "l0")
    dcat0 = mm_nt(dx1b, wts["l0_w_out"], 1, S, D, "l0_w_out_dx", tn=D)
    dcat0 = on_grad("l0_w_out", mm_tn(cat0, dx1b, 1, D, "l0_w_out_dw", tn=D), dcat0)
    dp0 = sb_bwd(p0, dcat0, W, "l0_sb_bwd")
    dp0, dscw = sc_bwd(p0, wts["l0_sc_conv_w"], dcat0, dp0, W, "l0_sc_bwd")
    g["l0_sc_conv_w"] = dscw
    n0 = wts["l0_w_in"].shape[1]
    td0 = math.gcd(n0, W)
    nd0 = n0 // td0
    half = S // 2
    tm0 = _pick(half, (1024, 512, 256, 128))
    per_part0 = W // td0

    def nt_maps0(k, first_block):
        def f(i, j, t):
            vb = j * nd0 + k
            return ((vb // per_part0) * (S // tm0) + first_block + i, vb % per_part0)
        return f

    def tn_maps0(k):
        def f(j, t):
            vb = j * nd0 + k
            return (vb // per_part0, vb % per_part0)
        return f

    dp0_2d = dp0.reshape(6 * S, W)
    dw0 = mm_tn(h0, dp0_2d, nb, n0, "l0_w_in_dw", dy_maps=[tn_maps0(k) for k in range(nd0)], tn=n0)
    dp0_2d = on_grad("l0_w_in", dw0, dp0_2d)
    halves = []
    for b, tag in enumerate("ab"):
        maps = [nt_maps0(k, b * (half // tm0)) for k in range(nd0)]
        halves.append(mm_nt(dp0_2d, wts["l0_w_in"], nb, half, D, f"l0_w_in_dx_{tag}", dy_maps=maps, tm=tm0, tn=n0))
        if b == 0:
            halves[0], dp0_2d = lax.optimization_barrier((halves[0], dp0_2d))
            dp0_2d = on_grad(None, None, dp0_2d)
    dh0 = jnp.concatenate(halves, axis=0)
    dx0, _, dg = rms_bwd(x, wts["l0_mix_norm_g"], dh0, dx1, "l0_mix_rms_bwd")
    g["l0_mix_norm_g"] = dg
    return dx0, g


GATHER_ID, PAIR_ID, CHIPS_ID = 1, 2, 3


def _place():
    return lax.axis_index("x"), lax.axis_index("y"), lax.axis_index("c")


def _other_chips(x, y):
    return [(x, 1 - y), (1 - x, y), (1 - x, 1 - y)]


def _handshake(peers):
    barrier = pltpu.get_barrier_semaphore()
    for peer in peers:
        pl.semaphore_signal(barrier, inc=1, device_id=peer, device_id_type=MESH)
    pl.semaphore_wait(barrier, len(peers))


UPDATE_LAG = 2


def _on_sequencer(body, out_type, scratch_types, collective_id, name):
    return pl.kernel(
        body,
        out_type=out_type,
        mesh=plsc.ScalarSubcoreMesh(axis_name="seq", num_cores=1),
        scratch_types=scratch_types,
        compiler_params=pltpu.CompilerParams(collective_id=collective_id),
        name=name,
    )


def all_gather(arrs, name):
    n = len(arrs)

    def body(*refs):
        xs, outs = refs[:n], refs[n : 2 * n]
        send_sems, recv_sems, local_sems = refs[2 * n :]
        x, y, c = _place()
        me, sibling = (x, y, c), (x, y, 1 - c)
        chips = _other_chips(x, y)
        _handshake([sibling] + [(*chip, c) for chip in chips])

        def copy(a, k, block, to, src=None):
            px, py, pc = block
            dst = outs[a].at[4 * px + 2 * py + pc]
            return pltpu.make_async_remote_copy(
                src_ref=dst if src is None else src, dst_ref=dst,
                send_sem=send_sems.at[7 * a + k], recv_sem=recv_sems.at[7 * a + k], device_id=to, device_id_type=MESH,
            )

        mine = [pltpu.make_async_copy(xs[a], outs[a].at[4 * x + 2 * y + c], local_sems.at[a]) for a in range(n)]
        for cp in mine:
            cp.start()
        first = []
        for a in range(n):
            first.append(copy(a, 0, me, sibling, src=xs[a]))
            first += [copy(a, 1 + j, me, (*chip, c), src=xs[a]) for j, chip in enumerate(chips)]
        for cp in first:
            cp.start()
        passed = []
        for a in range(n):
            for j, chip in enumerate(chips):
                copy(a, 1 + j, (*chip, c), me).wait_recv()
                cp = copy(a, 4 + j, (*chip, c), sibling)
                cp.start()
                passed.append(cp)
        for a in range(n):
            copy(a, 0, sibling, me).wait_recv()
            for j, chip in enumerate(chips):
                copy(a, 4 + j, (*chip, 1 - c), me).wait_recv()
        for cp in first + passed:
            cp.wait_send()
        for cp in mine:
            cp.wait()

    out_type = [jax.ShapeDtypeStruct((NDEV,) + a.shape, a.dtype) for a in arrs]
    sems = [pltpu.SemaphoreType.DMA((7 * n,)), pltpu.SemaphoreType.DMA((7 * n,)), pltpu.SemaphoreType.DMA((n,))]
    return _on_sequencer(body, out_type, sems, GATHER_ID, name)(*arrs)


_IN_HBM = pl.BlockSpec(memory_space=pltpu.HBM)
_IN_SEM = pl.BlockSpec(memory_space=pltpu.SEMAPHORE)
_EFFECT = pltpu.SideEffectType.DATAFLOW_SIDE_EFFECTING


def _split_start(make_copies, src, land_shape, nsem, name):
    def body(src_ref, land_ref, send_sems, recv_sems, land_thru, token):
        for cp in make_copies(src_ref, land_ref, send_sems, recv_sems):
            cp.start()
        token[...] = jnp.zeros_like(token)

    send_sems, recv_sems, land_thru, token = pl.pallas_call(
        body,
        name=name,
        out_shape=(
            pltpu.SemaphoreType.DMA((nsem,)), pltpu.SemaphoreType.DMA((nsem,)),
            pltpu.HBM(land_shape, src.dtype), jax.ShapeDtypeStruct((8, 128), F32),
        ),
        in_specs=(_IN_HBM, _IN_HBM),
        out_specs=(_IN_SEM, _IN_SEM, _IN_HBM, pl.BlockSpec(memory_space=pltpu.VMEM)),
        input_output_aliases={1: 2},
        compiler_params=pltpu.CompilerParams(has_side_effects=_EFFECT),
    )(src, pltpu.with_memory_space_constraint(lax.empty(land_shape, src.dtype), pltpu.HBM))
    return send_sems, recv_sems, src, land_thru, token


def _split_wait(make_copies, send_sems, recv_sems, src_thru, land_thru, after, name):
    def body(src_ref, land_ref, send_sems, recv_sems, after_ref, land_out):
        for cp in make_copies(src_ref, land_ref, send_sems, recv_sems):
            cp.wait_send()
            cp.wait_recv()

    return pl.pallas_call(
        body,
        name=name,
        out_shape=pltpu.HBM(land_thru.shape, land_thru.dtype),
        in_specs=(_IN_HBM, _IN_HBM, _IN_SEM, _IN_SEM, pl.BlockSpec(memory_space=pl.ANY)),
        out_specs=_IN_HBM,
        input_output_aliases={1: 0},
        compiler_params=pltpu.CompilerParams(has_side_effects=_EFFECT),
    )(src_thru, land_thru, send_sems, recv_sems, after)


def _pair_copies(src_ref, land_ref, send_sems, recv_sems):
    x, y, c = _place()
    return [
        pltpu.make_async_remote_copy(
            src_ref=src_ref.at[k, 1 - c], dst_ref=land_ref.at[k],
            send_sem=send_sems.at[k], recv_sem=recv_sems.at[k], device_id=(x, y, 1 - c), device_id_type=MESH,
        )
        for k in range(4)
    ]


def _chip_copies(src_ref, land_ref, send_sems, recv_sems):
    x, y, c = _place()
    return [
        pltpu.make_async_remote_copy(
            src_ref=src_ref.at[2 * px + py], dst_ref=land_ref.at[2 * x + y],
            send_sem=send_sems.at[j], recv_sem=recv_sems.at[j], device_id=(px, py, c), device_id_type=MESH,
        )
        for j, (px, py) in enumerate(_other_chips(x, y))
    ]


def _row_tile(R, C, max_elems):
    if R * C <= max_elems:
        return R
    best = None
    for tr in range(16, R, 16):
        if R % tr == 0 and tr * C <= max_elems:
            best = tr
    return best or R


def pair_sum(a42, land4, core, name):
    _, _, R, C = a42.shape
    tr = _row_tile(R, C, 1 << 20)

    def body(core_ref, a_ref, l_ref, o_ref):
        o_ref[...] = (a_ref[0].astype(F32) + l_ref[...].astype(F32)).astype(o_ref.dtype)

    return pl.pallas_call(
        body,
        grid_spec=pltpu.PrefetchScalarGridSpec(
            num_scalar_prefetch=1,
            grid=(4, R // tr),
            in_specs=[
                pl.BlockSpec((1, 1, tr, C), lambda k, r, core_ref: (k, core_ref[0], r, 0)),
                pl.BlockSpec((1, tr, C), lambda k, r, core_ref: (k, r, 0)),
            ],
            out_specs=pl.BlockSpec((1, tr, C), lambda k, r, core_ref: (k, r, 0)),
        ),
        out_shape=jax.ShapeDtypeStruct((4, R, C), BF16),
        compiler_params=_cp(("parallel", "parallel")),
        name=name,
    )(core, a42, land4)


def sum_slots(parts, name):
    P, R, C = parts.shape

    def body(p_ref, o_ref):
        acc = p_ref[0].astype(F32)
        for k in range(1, P):
            acc = acc + p_ref[k].astype(F32)
        o_ref[...] = acc

    tr = _row_tile(R, P * C, 1 << 21)
    return pl.pallas_call(
        body,
        grid=(R // tr,),
        in_specs=[pl.BlockSpec((P, tr, C), lambda r: (0, r, 0))],
        out_specs=pl.BlockSpec((tr, C), lambda r: (r, 0)),
        out_shape=jax.ShapeDtypeStruct((R, C), F32),
        compiler_params=_cp(("parallel",)),
        name=name,
    )(parts)


def adamw(w, m, v, parts, name):
    R, C = w.shape
    P = parts.shape[0]
    tr = _pick(R, (256, 128, 64, 32, 16, 8))
    c1 = 1.0 - ADAM_B1 ** ADAM_STEP
    c2 = 1.0 - ADAM_B2 ** ADAM_STEP

    def body(w_ref, m_ref, v_ref, p_ref, g_ref, d_ref, nm_ref, nv_ref):
        g = p_ref[0].astype(F32)
        for k in range(1, P):
            g = g + p_ref[k].astype(F32)
        nm = ADAM_B1 * m_ref[...] + (1.0 - ADAM_B1) * g
        nv = ADAM_B2 * v_ref[...] + (1.0 - ADAM_B2) * (g * g)
        g_ref[...] = g
        nm_ref[...] = nm
        nv_ref[...] = nv
        d_ref[...] = -ADAM_LR * ((nm / c1) / (jnp.sqrt(nv / c2) + ADAM_EPS) + ADAM_WD * w_ref[...])

    blk = pl.BlockSpec((tr, C), lambda r: (r, 0))
    shp = jax.ShapeDtypeStruct((R, C), F32)
    return pl.pallas_call(
        body,
        grid=(R // tr,),
        in_specs=[blk, blk, blk, pl.BlockSpec((P, tr, C), lambda r: (0, r, 0))],
        out_specs=[blk, blk, blk, blk],
        out_shape=[shp, shp, shp, shp],
        compiler_params=_cp(("parallel",)),
        name=name,
    )(w, m, v, parts)


def adamw_reduced(w, m, v, own, land, chip, name):
    R, C = w.shape
    if R % 8 == 0:
        tr, tc = _pick(R, (256, 128, 64, 32, 16, 8)), C
    else:
        tr, tc = R, _pick(C, (256, 128))
    c1 = 1.0 - ADAM_B1 ** ADAM_STEP
    c2 = 1.0 - ADAM_B2 ** ADAM_STEP

    def body(chip_ref, w_ref, m_ref, v_ref, own_ref, land_ref, g_ref, d_ref, nm_ref, nv_ref):
        mine = own_ref[0].astype(F32)
        g = None
        for k in range(4):
            term = jnp.where(chip_ref[0] == k, mine, land_ref[k].astype(F32))
            g = term if g is None else g + term
        nm = ADAM_B1 * m_ref[...] + (1.0 - ADAM_B1) * g
        nv = ADAM_B2 * v_ref[...] + (1.0 - ADAM_B2) * (g * g)
        g_ref[...] = g
        nm_ref[...] = nm
        nv_ref[...] = nv
        d_ref[...] = -ADAM_LR * ((nm / c1) / (jnp.sqrt(nv / c2) + ADAM_EPS) + ADAM_WD * w_ref[...])

    blk = pl.BlockSpec((tr, tc), lambda r, c, chip_ref: (r, c))
    shp = jax.ShapeDtypeStruct((R, C), F32)
    return pl.pallas_call(
        body,
        grid_spec=pltpu.PrefetchScalarGridSpec(
            num_scalar_prefetch=1,
            grid=(R // tr, C // tc),
            in_specs=[
                blk, blk, blk,
                pl.BlockSpec((1, tr, tc), lambda r, c, chip_ref: (chip_ref[0], r, c)),
                pl.BlockSpec((4, tr, tc), lambda r, c, chip_ref: (0, r, c)),
            ],
            out_specs=[blk, blk, blk, blk],
        ),
        out_shape=[shp, shp, shp, shp],
        compiler_params=_cp(("parallel", "parallel")),
        name=name,
    )(chip, w, m, v, own, land)


_WEIGHTS = [
    "l0_mix_norm_g", "l0_w_in", "l0_sc_conv_w", "l0_w_out", "l0_ffn_norm_g", "l0_ffn_up", "l0_ffn_conv_w", "l0_ffn_down",
    "l1_mix_norm_g", "l1_w_in", "l1_fox_b_f", "l1_sg_w", "l1_sg_b", "l1_sg_norm_g", "l1_w_out", "l1_ffn_norm_g",
    "l1_ffn_up", "l1_ffn_conv_w", "l1_ffn_down", "final_norm_g",
]
_COL_SHARDED = ["l0_w_in", "l0_ffn_up", "l1_w_in", "l1_ffn_up"]
_ROW_SHARDED = ["l0_w_out", "l0_ffn_down", "l1_w_out", "l1_ffn_down"]
_BIG = ["l0_w_in", "l0_w_out", "l0_ffn_up", "l0_ffn_down", "l1_w_in", "l1_w_out", "l1_ffn_up", "l1_ffn_down"]
_CONV = ["l0_sc_conv_w", "l0_ffn_conv_w", "l1_ffn_conv_w"]
_SMALL = [n for n in _WEIGHTS if n not in _BIG]
_PACK_ROWS = 8


def _pack(arrs):
    flat = []
    for a in arrs:
        v = a.reshape(-1).astype(F32)
        pad = (-v.shape[0]) % (_PACK_ROWS * 128)
        flat.append(jnp.pad(v, (0, pad)))
    return jnp.concatenate(flat).reshape(-1, 128)


def _unpack(packed, shapes):
    out, off = [], 0
    flat = packed.reshape(-1)
    for shp in shapes:
        size = math.prod(shp)
        out.append(flat[off : off + size].reshape(shp))
        off += size + (-size) % (_PACK_ROWS * 128)
    return out


def kernel(x, l0_mix_norm_g, l0_w_in, l0_sc_conv_w, l0_w_out, l0_ffn_norm_g, l0_ffn_up, l0_ffn_conv_w, l0_ffn_down, l1_mix_norm_g, l1_w_in, l1_fox_b_f, l1_sg_w, l1_sg_b, l1_sg_norm_g, l1_w_out, l1_ffn_norm_g, l1_ffn_up, l1_ffn_conv_w, l1_ffn_down, final_norm_g, loss_target, m_l0_mix_norm_g, m_l0_w_in, m_l0_sc_conv_w, m_l0_w_out, m_l0_ffn_norm_g, m_l0_ffn_up, m_l0_ffn_conv_w, m_l0_ffn_down, m_l1_mix_norm_g, m_l1_w_in, m_l1_fox_b_f, m_l1_sg_w, m_l1_sg_b, m_l1_sg_norm_g, m_l1_w_out, m_l1_ffn_norm_g, m_l1_ffn_up, m_l1_ffn_conv_w, m_l1_ffn_down, m_final_norm_g, v_l0_mix_norm_g, v_l0_w_in, v_l0_sc_conv_w, v_l0_w_out, v_l0_ffn_norm_g, v_l0_ffn_up, v_l0_ffn_conv_w, v_l0_ffn_down, v_l1_mix_norm_g, v_l1_w_in, v_l1_fox_b_f, v_l1_sg_w, v_l1_sg_b, v_l1_sg_norm_g, v_l1_w_out, v_l1_ffn_norm_g, v_l1_ffn_up, v_l1_ffn_conv_w, v_l1_ffn_down, v_final_norm_g):
    given = dict(locals())
    w = {n: given[n] for n in _WEIGHTS}
    mom = {n: given["m_" + n] for n in _WEIGHTS}
    var = {n: given["v_" + n] for n in _WEIGHTS}
    xs, target = x[0], loss_target[0]
    S, D = xs.shape
    W = D // 2
    nh = W // HD
    cx, cy, cc = _place()
    me = 4 * cx + 2 * cy + cc

    wts = {"nb": NDEV, "F": l0_ffn_down.shape[0] * NDEV}
    for n in _SMALL:
        if n not in _CONV:
            wts[n] = w[n]
    gathered, loss_sum = {}, []

    def start_gather(n):
        src = w[n].T if n == "l1_w_in" else w[n]
        got = all_gather([src.astype(BF16)] + ([w[c] for c in _CONV] if n == _BIG[0] else []), f"gather_{n}")
        if n == "l1_w_in":
            gathered[n] = got[0]
        elif n in _ROW_SHARDED:
            wts[n] = got[0].reshape(-1, D)
        else:
            wts[n] = got[0].reshape(NDEV * D, -1)
        for c, taps in zip(_CONV, got[1:]):
            wts[c] = taps.transpose(1, 0, 2).reshape(CONV_K, -1)

    def at(point, after, value):
        if point == "l1_w_in":
            got, after = lax.optimization_barrier((gathered[point], after))
            wts["l1_w_in_t"] = got.reshape(-1, D)
            wts["l1_w_f_t"] = jnp.pad(wts["l1_w_in_t"][5 * W :], ((0, 128 - nh), (0, 0)))
        elif point == "loss":
            total, after = lax.optimization_barrier((lax.psum(value[0, 0], ("x", "y", "c")), after))
            loss_sum.append(total)
        return after

    core = jnp.reshape(cc, (1,)).astype(jnp.int32)
    chip = jnp.reshape(2 * cx + cy, (1,)).astype(jnp.int32)
    pair_flying, chip_flying = [], []
    out_g, out_d, out_m, out_v = {}, {}, {}, {}

    def tie(value, after):
        if after is None:
            return value, None
        return lax.optimization_barrier((value, after))

    def to_chips(after):
        n, flying = pair_flying.pop()
        landed = _split_wait(_pair_copies, *flying, f"reduce_pair_wait_{n}")
        summed = pair_sum(flying[2], landed, core, f"pair_sum_{n}")
        *flying, token = _split_start(_chip_copies, summed, summed.shape, 3, f"reduce_chips_{n}")
        token, after = tie(token, after)
        chip_flying.append((n, flying + [token]))
        return after

    def update(after, behind=None):
        n, flying = chip_flying.pop(0)
        if behind is not None:
            flying[4], _ = lax.optimization_barrier((flying[4], behind))
        landed = _split_wait(_chip_copies, *flying, f"reduce_chips_wait_{n}")
        turn = (lambda t: t.T) if n == "l1_w_in" else (lambda t: t)
        res = adamw_reduced(turn(w[n]), turn(mom[n]), turn(var[n]), flying[2], landed, chip, f"adamw_{n}")
        res, after = tie(res, after)
        out_g[n], out_d[n], out_m[n], out_v[n] = [turn(t) for t in res]
        return after, res[0]

    def on_grad(n, term, after):
        if n is None:
            return to_chips(after)
        if n in _ROW_SHARDED or n == "l1_w_in":
            term = term.reshape(NDEV, -1, D)
        else:
            term = term.reshape(NDEV, D, -1)
        term = term.reshape((4, 2) + term.shape[1:])
        *flying, token = _split_start(_pair_copies, term, term.shape[:1] + term.shape[2:], 4, f"reduce_pair_{n}")
        token, after = tie(token, after)
        if len(chip_flying) == UPDATE_LAG:
            after, _ = update(after)
        if pair_flying:
            after = to_chips(after)
        pair_flying.append((n, flying + [token]))
        return after

    for n in _BIG:
        start_gather(n)
    dx, g = local_step(xs, target, wts, at, on_grad)
    done = None
    while chip_flying:
        _, done = update(None, behind=done)
    loss = loss_sum[0]

    small_terms = [g[n] for n in _SMALL]
    small_shapes = [tuple(t.shape) for t in small_terms]
    packed = _pack(small_terms)
    all_terms = all_gather([packed], "gather_small_grads")[0]
    small_sum = _unpack(sum_slots(all_terms, "sum_small_grads"), small_shapes)
    small_g = {}
    for n, t in zip(_SMALL, small_sum):
        if n in _CONV:
            cols = w[n].shape[1]
            t = lax.dynamic_slice_in_dim(t, me * cols, cols, axis=1)
        small_g[n] = t.reshape(w[n].shape)
    shapes = [w[n].shape for n in _SMALL]
    res = adamw(
        _pack([w[n] for n in _SMALL]), _pack([mom[n] for n in _SMALL]), _pack([var[n] for n in _SMALL]),
        _pack([small_g[n] for n in _SMALL])[None], "adamw_small",
    )
    for dst, packed_out in zip((out_g, out_d, out_m, out_v), res):
        for n, t in zip(_SMALL, _unpack(packed_out, shapes)):
            dst[n] = t

    return (loss, dx[None], *[out_g[n] for n in _WEIGHTS], *[out_d[n] for n in _WEIGHTS],
            *[out_m[n] for n in _WEIGHTS], *[out_v[n] for n in _WEIGHTS])
```

```python
import functools
import math

import jax
import jax.numpy as jnp
from jax import lax
from jax.experimental import pallas as pl
from jax.experimental.pallas import tpu as pltpu
from jax.experimental.pallas import tpu_sc as plsc

F32 = jnp.float32
BF16 = jnp.bfloat16
HD = 128
EPS = 1e-6
CONV_K = 3
VMEM_LIMIT_BYTES = 48 << 20
NDEV = 8
MESH = pl.DeviceIdType.MESH

ADAM_LR = 0.001
ADAM_B1 = 0.9
ADAM_B2 = 0.999
ADAM_EPS = 1e-08
ADAM_WD = 0.01
ADAM_STEP = 10


def _cp(sem):
    return pltpu.CompilerParams(dimension_semantics=sem, vmem_limit_bytes=VMEM_LIMIT_BYTES)


def _pick(n, prefs):
    for p in prefs:
        if n % p == 0:
            return p
    return n


def _dot(a, b):
    return jnp.dot(a, b, preferred_element_type=F32)


def _dot_nt(a, b):
    return lax.dot_general(a, b, (((1,), (1,)), ((), ())), preferred_element_type=F32)


def _dot_tn(a, b):
    return lax.dot_general(a, b, (((0,), (0,)), ((), ())), preferred_element_type=F32)


def _split3(x):
    hi = x.astype(BF16)
    r = x - hi.astype(F32)
    mid = r.astype(BF16)
    lo = (r - mid.astype(F32)).astype(BF16)
    return hi, mid, lo


def _dot_ones_left(ones_bf16, x):
    hi, mid, lo = _split3(x)
    return _dot(ones_bf16, hi) + _dot(ones_bf16, mid) + _dot(ones_bf16, lo)


def _iota2(shape, axis):
    return lax.broadcasted_iota(jnp.int32, shape, axis)


def mm_nn(a, w2d, nb, name, out_dtype=BF16, res=None, tm=None, tn=None, tk=None, a_map=None, a_shape=None):
    M, K = a_shape or a.shape
    n = w2d.shape[1]
    assert w2d.shape[0] == nb * K or (nb == 1 and w2d.shape[0] > K)
    a_map = a_map or (lambda i, k: (i, k))
    tm = tm or _pick(M, (1024, 512, 256, 128))
    tn = tn or _pick(n, (1408, 1024, 768, 512, 256, 128))
    tk = tk or (K if K <= 2048 else _pick(K, (1408, 1024, 512, 256, 128)))
    nk, nt = K // tk, n // tn
    has_res = res is not None

    def body(*refs):
        if has_res:
            a_ref, w_ref, r_ref, o_ref = refs[:4]
        else:
            a_ref, w_ref, o_ref = refs[:3]
            r_ref = None
        part = _dot(a_ref[...], w_ref[...])

        def finish(acc):
            if r_ref is not None:
                acc = acc + r_ref[...].astype(F32)
            o_ref[...] = acc.astype(o_ref.dtype)

        if nk == 1:
            finish(part)
        else:
            acc_ref = refs[-1]
            k = pl.program_id(3)

            @pl.when(k == 0)
            def _():
                acc_ref[...] = part

            @pl.when(k > 0)
            def _():
                acc_ref[...] += part

            @pl.when(k == nk - 1)
            def _():
                finish(acc_ref[...])

    in_specs = [
        pl.BlockSpec((tm, tk), lambda i, j, t, k: a_map(i, k)),
        pl.BlockSpec((tk, tn), lambda i, j, t, k: (j * nk + k, t)),
    ]
    args = [a, w2d]
    out_spec = pl.BlockSpec((tm, tn), lambda i, j, t, k: (i, j * nt + t))
    if has_res:
        in_specs.append(out_spec)
        args.append(res)
    return pl.pallas_call(
        body,
        grid=(M // tm, nb, nt, nk),
        in_specs=in_specs,
        out_specs=out_spec,
        out_shape=jax.ShapeDtypeStruct((M, nb * n), out_dtype),
        scratch_shapes=[pltpu.VMEM((tm, tn), F32)] if nk > 1 else [],
        compiler_params=_cp(("parallel", "parallel", "parallel", "arbitrary")),
        name=name,
    )(*args)


def mm_nt(dy2d, w2d, nb, M, K, name, out_dtype=BF16, res=None, dy_maps=None, tm=None, tko=None, tn=None):
    n = w2d.shape[1]
    assert w2d.shape[0] == nb * K or (nb == 1 and w2d.shape[0] > K)
    tm = tm or _pick(M, (1024, 512, 256, 128))
    tko = tko or _pick(K, (1024, 512, 256, 128))
    tn = tn or _pick(n, (1408, 1024, 768, 512, 256, 128))
    nt, nko = n // tn, K // tko
    has_res = res is not None
    if dy_maps is None:
        dy_maps = [lambda i, j, t: (i, j * nt + t)]
    nd = len(dy_maps)
    td = tn // nd

    one_step = nb * nt == 1

    def body(*refs):
        d_refs, w_ref = refs[:nd], refs[nd]
        r_ref = refs[nd + 1] if has_res else None
        d = d_refs[0][...] if nd == 1 else jnp.concatenate([r[...] for r in d_refs], axis=1)
        part = _dot_nt(d, w_ref[...])
        if one_step:
            o_ref = refs[-1]
            if r_ref is not None:
                part = part + r_ref[...].astype(F32)
            o_ref[...] = part.astype(o_ref.dtype)
            return
        o_ref, acc_ref = refs[-2], refs[-1]
        j, t = pl.program_id(2), pl.program_id(3)
        first = jnp.logical_and(j == 0, t == 0)
        last = jnp.logical_and(j == nb - 1, t == nt - 1)

        @pl.when(first)
        def _():
            acc_ref[...] = part

        @pl.when(jnp.logical_not(first))
        def _():
            acc_ref[...] += part

        @pl.when(last)
        def _():
            acc = acc_ref[...]
            if r_ref is not None:
                acc = acc + r_ref[...].astype(F32)
            o_ref[...] = acc.astype(o_ref.dtype)

    in_specs = [pl.BlockSpec((tm, td), functools.partial(lambda f, i, ko, j, t: f(i, j, t), f)) for f in dy_maps]
    in_specs.append(pl.BlockSpec((tko, tn), lambda i, ko, j, t: (j * nko + ko, t)))
    args = [dy2d] * nd + [w2d]
    out_spec = pl.BlockSpec((tm, tko), lambda i, ko, j, t: (i, ko))
    if has_res:
        in_specs.append(out_spec)
        args.append(res)
    return pl.pallas_call(
        body,
        grid=(M // tm, nko, nb, nt),
        in_specs=in_specs,
        out_specs=out_spec,
        out_shape=jax.ShapeDtypeStruct((M, K), out_dtype),
        scratch_shapes=[] if one_step else [pltpu.VMEM((tm, tko), F32)],
        compiler_params=_cp(("parallel", "parallel", "arbitrary", "arbitrary")),
        name=name,
    )(*args)


def mm_tn(x, dy2d, nb, n, name, out_dtype=BF16, dy_maps=None, tko=None, tn=None, x_map=None, x_shape=None):
    S, K = x_shape or x.shape
    x_map = x_map or (lambda ko: (0, ko))
    tko = tko or _pick(K, (512, 256, 128))
    tn = tn or _pick(n, (1408, 1024, 768, 512, 256, 128))
    nt, nko = n // tn, K // tko
    if dy_maps is None:
        dy_maps = [lambda j, t: (0, j * nt + t)]
    nd = len(dy_maps)
    td = tn // nd

    def body(*refs):
        x_ref, d_refs, o_ref = refs[0], refs[1 : 1 + nd], refs[-1]
        d = d_refs[0][...] if nd == 1 else jnp.concatenate([r[...] for r in d_refs], axis=1)
        o_ref[...] = _dot_tn(x_ref[...], d).astype(o_ref.dtype)

    in_specs = [pl.BlockSpec((S, tko), lambda ko, j, t: x_map(ko))]
    in_specs += [pl.BlockSpec((S, td), functools.partial(lambda f, ko, j, t: f(j, t), f)) for f in dy_maps]
    return pl.pallas_call(
        body,
        grid=(nko, nb, nt),
        in_specs=in_specs,
        out_specs=pl.BlockSpec((tko, tn), lambda ko, j, t: (j * nko + ko, t)),
        out_shape=jax.ShapeDtypeStruct((nb * K, n), out_dtype),
        compiler_params=_cp(("parallel", "parallel", "parallel")),
        name=name,
    )(x, *([dy2d] * nd))


def rms_fwd(x, g, name):
    S, D = x.shape
    tm = _pick(S, (256, 128))

    def body(x_ref, g_ref, o_ref):
        xf = x_ref[...]
        r = lax.rsqrt(jnp.mean(xf * xf, axis=-1, keepdims=True) + EPS)
        o_ref[...] = (xf * r * g_ref[...]).astype(o_ref.dtype)

    return pl.pallas_call(
        body,
        grid=(S // tm,),
        in_specs=[pl.BlockSpec((tm, D), lambda i: (i, 0)), pl.BlockSpec((1, D), lambda i: (0, 0))],
        out_specs=pl.BlockSpec((tm, D), lambda i: (i, 0)),
        out_shape=jax.ShapeDtypeStruct((S, D), BF16),
        compiler_params=_cp(("parallel",)),
        name=name,
    )(x, g.reshape(1, D))


def rms_bwd(x, g, dh, dres, name):
    S, D = x.shape
    tm = _pick(S, (256, 128))

    def body(x_ref, g_ref, dh_ref, dr_ref, dx_ref, dxb_ref, dg_ref):
        i = pl.program_id(0)
        xf = x_ref[...]
        dh = dh_ref[...].astype(F32)
        r = lax.rsqrt(jnp.mean(xf * xf, axis=-1, keepdims=True) + EPS)
        gy = dh * g_ref[...]
        proj = jnp.mean(gy * xf, axis=-1, keepdims=True)
        dx = dr_ref[...] + r * gy - xf * (r * r * r * proj)
        dx_ref[...] = dx
        dxb_ref[...] = dx.astype(BF16)
        dg = jnp.sum(dh * (xf * r), axis=0, keepdims=True)

        @pl.when(i == 0)
        def _():
            dg_ref[...] = dg

        @pl.when(i > 0)
        def _():
            dg_ref[...] += dg

    row = pl.BlockSpec((tm, D), lambda i: (i, 0))
    vec = pl.BlockSpec((1, D), lambda i: (0, 0))
    return pl.pallas_call(
        body,
        grid=(S // tm,),
        in_specs=[row, vec, row, row],
        out_specs=[row, row, vec],
        out_shape=[jax.ShapeDtypeStruct((S, D), F32), jax.ShapeDtypeStruct((S, D), BF16), jax.ShapeDtypeStruct((1, D), F32)],
        compiler_params=_cp(("arbitrary",)),
        name=name,
    )(x, g.reshape(1, D), dh, dres)


def loss_head(x, g, target, name):
    S, D = x.shape
    tm = _pick(S, (256, 128))

    def body(x_ref, g_ref, t_ref, dx_ref, dxb_ref, dg_ref, loss_ref):
        i = pl.program_id(0)
        xf = x_ref[...]
        gg = g_ref[...]
        r = lax.rsqrt(jnp.mean(xf * xf, axis=-1, keepdims=True) + EPS)
        xh = xf * r
        err = xh * gg - t_ref[...]
        part = (0.5 / D) * jnp.sum(err * err)
        dy = err * (1.0 / D)
        gy = dy * gg
        proj = jnp.mean(gy * xf, axis=-1, keepdims=True)
        dx = r * gy - xf * (r * r * r * proj)
        dx_ref[...] = dx
        dxb_ref[...] = dx.astype(BF16)
        dg = jnp.sum(dy * xh, axis=0, keepdims=True)
        lossb = jnp.full(loss_ref.shape, part, F32)

        @pl.when(i == 0)
        def _():
            dg_ref[...] = dg
            loss_ref[...] = lossb

        @pl.when(i > 0)
        def _():
            dg_ref[...] += dg
            loss_ref[...] += lossb

    row = pl.BlockSpec((tm, D), lambda i: (i, 0))
    vec = pl.BlockSpec((1, D), lambda i: (0, 0))
    return pl.pallas_call(
        body,
        grid=(S // tm,),
        in_specs=[row, vec, row],
        out_specs=[row, row, vec, pl.BlockSpec((8, 128), lambda i: (0, 0))],
        out_shape=[
            jax.ShapeDtypeStruct((S, D), F32),
            jax.ShapeDtypeStruct((S, D), BF16),
            jax.ShapeDtypeStruct((1, D), F32),
            jax.ShapeDtypeStruct((8, 128), F32),
        ],
        compiler_params=_cp(("arbitrary",)),
        name=name,
    )(x, g.reshape(1, D), target)


def _shift_down(s, k):
    if k == 0:
        return s
    return jnp.where(_iota2(s.shape, 0) >= k, pltpu.roll(s, k, axis=0), 0.0)


def _shift_up(s, k):
    if k == 0:
        return s
    n = s.shape[0]
    return jnp.where(_iota2(s.shape, 0) < n - k, pltpu.roll(s, n - k, axis=0), 0.0)


def _conv(s, w):
    return w[0:1] * _shift_down(s, 2) + w[1:2] * _shift_down(s, 1) + w[2:3] * s


def _conv_t(d, w):
    return w[2:3] * d + w[1:2] * _shift_up(d, 1) + w[0:1] * _shift_up(d, 2)


def _conv_dw(d, s):
    return [jnp.sum(d * _shift_down(s, CONV_K - 1 - k), axis=0, keepdims=True) for k in range(CONV_K)]


def sc_fwd(p, convw, cat, W, name):
    S = p.shape[0]
    tc = _pick(W, (256, 128))
    nc = W // tc

    def body(gb_ref, gc_ref, hi_ref, w_ref, cat_ref, o_ref):
        s = gc_ref[...].astype(F32) * hi_ref[...].astype(F32)
        o_ref[...] = (gb_ref[...].astype(F32) * _conv(s, w_ref[...])).astype(o_ref.dtype)

    col = lambda part: pl.BlockSpec((S, tc), lambda c: (0, part * nc + c))
    return pl.pallas_call(
        body,
        grid=(nc,),
        in_specs=[col(3), col(4), col(5), pl.BlockSpec((CONV_K, tc), lambda c: (0, c)), pl.BlockSpec(memory_space=pl.ANY)],
        out_specs=col(1),
        out_shape=jax.ShapeDtypeStruct(cat.shape, cat.dtype),
        input_output_aliases={4: 0},
        compiler_params=_cp(("parallel",)),
        name=name,
    )(p, p, p, convw, cat)


def sc_bwd(p, convw, dcat, dp, W, name):
    S = p.shape[0]
    tc = _pick(W, (256, 128))
    nc = W // tc

    def body(gb_ref, gc_ref, hi_ref, w_ref, do_ref, dp_in_ref, dp_ref, dw_ref):
        gb = gb_ref[...].astype(F32)
        gc = gc_ref[...].astype(F32)
        hi = hi_ref[...].astype(F32)
        w = w_ref[...]
        do = do_ref[...].astype(F32)
        s = gc * hi
        dcs = do * gb
        ds = _conv_t(dcs, w)
        dp_ref[0] = (do * _conv(s, w)).astype(dp_ref.dtype)
        dp_ref[1] = (ds * hi).astype(dp_ref.dtype)
        dp_ref[2] = (ds * gc).astype(dp_ref.dtype)
        for k, row in enumerate(_conv_dw(dcs, s)):
            dw_ref[k : k + 1, :] = row

    col = lambda part: pl.BlockSpec((S, tc), lambda c: (0, part * nc + c))
    return pl.pallas_call(
        body,
        grid=(nc,),
        in_specs=[
            col(3), col(4), col(5),
            pl.BlockSpec((CONV_K, tc), lambda c: (0, c)),
            pl.BlockSpec((S, tc), lambda c: (0, nc + c)),
            pl.BlockSpec(memory_space=pl.ANY),
        ],
        out_specs=[pl.BlockSpec((3, S, tc), lambda c: (1, 0, c)), pl.BlockSpec((CONV_K, tc), lambda c: (0, c))],
        out_shape=[jax.ShapeDtypeStruct(dp.shape, dp.dtype), jax.ShapeDtypeStruct((CONV_K, W), F32)],
        input_output_aliases={5: 0},
        compiler_params=_cp(("parallel",)),
        name=name,
    )(p, p, p, convw, dcat, dp)


def _silu_parts(a):
    sig = 1.0 / (1.0 + jnp.exp(-a))
    return a * sig, sig


def ffn_act_fwd(u, convw, F, name):
    S = u.shape[0]
    tc = _pick(F, (256, 128))
    nc = F // tc

    def body(ug_ref, uu_ref, wg_ref, wu_ref, o_ref):
        ag = _conv(ug_ref[...].astype(F32), wg_ref[...])
        au = _conv(uu_ref[...].astype(F32), wu_ref[...])
        o_ref[...] = (_silu_parts(ag)[0] * au).astype(o_ref.dtype)

    col = lambda half: pl.BlockSpec((S, tc), lambda c: (0, half * nc + c))
    wcol = lambda half: pl.BlockSpec((CONV_K, tc), lambda c: (0, half * nc + c))
    return pl.pallas_call(
        body,
        grid=(nc,),
        in_specs=[col(0), col(1), wcol(0), wcol(1)],
        out_specs=pl.BlockSpec((S, tc), lambda c: (0, c)),
        out_shape=jax.ShapeDtypeStruct((S, F), BF16),
        compiler_params=_cp(("parallel",)),
        name=name,
    )(u, u, convw, convw)


def ffn_act_bwd(u, convw, dact, F, name):
    S = u.shape[0]
    tc = _pick(F, (256, 128))
    nc = F // tc

    def body(ug_ref, uu_ref, wg_ref, wu_ref, da_ref, du_ref, dw_ref):
        ug = ug_ref[...].astype(F32)
        uu = uu_ref[...].astype(F32)
        wg = wg_ref[...]
        wu = wu_ref[...]
        da = da_ref[...].astype(F32)
        ag = _conv(ug, wg)
        au = _conv(uu, wu)
        sl, sig = _silu_parts(ag)
        dag = da * au * (sig * (1.0 + ag * (1.0 - sig)))
        dau = da * sl
        du_ref[0] = _conv_t(dag, wg).astype(du_ref.dtype)
        du_ref[1] = _conv_t(dau, wu).astype(du_ref.dtype)
        for k, (rg, ru) in enumerate(zip(_conv_dw(dag, ug), _conv_dw(dau, uu))):
            dw_ref[0, k : k + 1, :] = rg
            dw_ref[1, k : k + 1, :] = ru

    col = lambda half: pl.BlockSpec((S, tc), lambda c: (0, half * nc + c))
    wcol = lambda half: pl.BlockSpec((CONV_K, tc), lambda c: (0, half * nc + c))
    return pl.pallas_call(
        body,
        grid=(nc,),
        in_specs=[col(0), col(1), wcol(0), wcol(1), pl.BlockSpec((S, tc), lambda c: (0, c))],
        out_specs=[pl.BlockSpec((2, S, tc), lambda c: (0, 0, c)), pl.BlockSpec((2, CONV_K, tc), lambda c: (0, 0, c))],
        out_shape=[jax.ShapeDtypeStruct((2, S, F), BF16), jax.ShapeDtypeStruct((2, CONV_K, F), F32)],
        compiler_params=_cp(("parallel",)),
        name=name,
    )(u, u, convw, convw, dact)


def _softplus(z):
    return jnp.maximum(z, 0.0) + jnp.log(1.0 + jnp.exp(-jnp.abs(z)))


def _key_strip(S):
    return _pick(S, (512, 256, 128))


def _query_rows(S):
    return _pick(S, (512, 256, 128))


def _split2(x):
    hi = x.astype(BF16)
    return hi, (x - hi.astype(F32)).astype(BF16)


def _block_sums(x, ones_bf16):
    hi, lo = _split2(x)
    return [
        _dot(hi[:, b * HD : (b + 1) * HD], ones_bf16) + _dot(lo[:, b * HD : (b + 1) * HD], ones_bf16)
        for b in range(x.shape[1] // HD)
    ]


def _strip_mask(shape, row0, off, strict):
    cols, rows = _iota2(shape, 1) + off, _iota2(shape, 0) + row0
    return cols < rows if strict else cols <= rows


def _sb_strip(q, ks, row0, off, run, su):
    z = _dot_nt(q, ks) * (HD ** -0.5)
    mask = _strip_mask(z.shape, row0, off, True)
    sp = _softplus(z)
    l = jnp.where(mask, -sp, 0.0)
    within = _block_sums(l, su)
    later = [None] * len(within)
    for b in reversed(range(len(within))):
        later[b] = within[b] + run
        run = run + jnp.sum(l[:, b * HD : (b + 1) * HD], axis=1, keepdims=True)
    a = jnp.where(mask, jnp.exp(z - sp + jnp.concatenate(later, axis=1)), 0.0)
    return z, mask, a, run


def sb_fwd(p, W, name):
    S = p.shape[0]
    TQ, TK = _query_rows(S), _key_strip(S)
    nh, nq = W // HD, S // TQ

    def body(q_ref, k_ref, v_ref, o_ref):
        i = pl.program_id(1)
        q = q_ref[...]
        su = (_iota2((HD, HD), 0) > _iota2((HD, HD), 1)).astype(BF16)
        last = (i * TQ + TQ - 1) // TK

        def step(gg, carry):
            acc, run = carry
            off = pl.multiple_of((last - gg) * TK, TK)
            _, _, a, run = _sb_strip(q, k_ref[pl.ds(off, TK), :], i * TQ, off, run, su)
            return acc + _dot(a.astype(BF16), v_ref[pl.ds(off, TK), :]), run

        acc, _ = lax.fori_loop(0, last + 1, step, (jnp.zeros((TQ, HD), F32), jnp.zeros((TQ, 1), F32)))
        o_ref[...] = acc.astype(o_ref.dtype)

    return pl.pallas_call(
        body,
        grid=(nh, nq),
        in_specs=[
            pl.BlockSpec((TQ, HD), lambda h, i: (i, h)),
            pl.BlockSpec((S, HD), lambda h, i: (0, nh + h)),
            pl.BlockSpec((S, HD), lambda h, i: (0, 2 * nh + h)),
        ],
        out_specs=pl.BlockSpec((TQ, HD), lambda h, i: (i, h)),
        out_shape=jax.ShapeDtypeStruct((S, 2 * W), BF16),
        compiler_params=_cp(("parallel", "arbitrary")),
        name=name,
    )(p, p, p)


def sb_bwd(p, dcat, W, name):
    S = p.shape[0]
    TQ, TK = _query_rows(S), _key_strip(S)
    nh, nq = W // HD, S // TQ
    scale = HD ** -0.5

    def body(q_ref, k_ref, v_ref, do_ref, dp_ref, dk_acc, dv_acc, e_scr, z_scr):
        i = pl.program_id(1)
        q = q_ref[...]
        do = do_ref[...]
        su = (_iota2((HD, HD), 0) > _iota2((HD, HD), 1)).astype(BF16)
        sl = (_iota2((HD, HD), 0) < _iota2((HD, HD), 1)).astype(BF16)
        last = (i * TQ + TQ - 1) // TK

        @pl.when(i == 0)
        def _():
            dk_acc[...] = jnp.zeros_like(dk_acc)
            dv_acc[...] = jnp.zeros_like(dv_acc)

        def pass_a(gg, run):
            g = last - gg
            off = pl.multiple_of(g * TK, TK)
            z, _, a, run = _sb_strip(q, k_ref[pl.ds(off, TK), :], i * TQ, off, run, su)
            e_scr[g] = a * _dot_nt(do, v_ref[pl.ds(off, TK), :])
            z_scr[g] = z
            dv_acc[pl.ds(off, TK), :] += _dot_tn(a.astype(BF16), do)
            return run

        lax.fori_loop(0, last + 1, pass_a, jnp.zeros((TQ, 1), F32))

        def pass_b(g, carry):
            dq, run_e = carry
            off = pl.multiple_of(g * TK, TK)
            e = e_scr[g]
            z = z_scr[g]
            mask = _strip_mask(z.shape, i * TQ, off, True)
            within = _block_sums(e, sl)
            before = []
            for b in range(len(within)):
                before.append(within[b] + run_e)
                run_e = run_e + jnp.sum(e[:, b * HD : (b + 1) * HD], axis=1, keepdims=True)
            sig = 1.0 / (1.0 + jnp.exp(-z))
            dz = jnp.where(mask, e * (1.0 - sig) - jnp.concatenate(before, axis=1) * sig, 0.0)
            dz = (dz * scale).astype(BF16)
            dq = dq + _dot(dz, k_ref[pl.ds(off, TK), :])
            dk_acc[pl.ds(off, TK), :] += _dot_tn(dz, q)
            return dq, run_e

        dq, _ = lax.fori_loop(0, last + 1, pass_b, (jnp.zeros((TQ, HD), F32), jnp.zeros((TQ, 1), F32)))
        dp_ref[0, pl.ds(pl.multiple_of(i * TQ, TQ), TQ), :] = dq.astype(dp_ref.dtype)

        @pl.when(i == nq - 1)
        def _():
            dp_ref[1] = dk_acc[...].astype(dp_ref.dtype)
            dp_ref[2] = dv_acc[...].astype(dp_ref.dtype)

    return pl.pallas_call(
        body,
        grid=(nh, nq),
        in_specs=[
            pl.BlockSpec((TQ, HD), lambda h, i: (i, h)),
            pl.BlockSpec((S, HD), lambda h, i: (0, nh + h)),
            pl.BlockSpec((S, HD), lambda h, i: (0, 2 * nh + h)),
            pl.BlockSpec((TQ, HD), lambda h, i: (i, h)),
        ],
        out_specs=pl.BlockSpec((3, S, HD), lambda h, i: (0, 0, h)),
        out_shape=jax.ShapeDtypeStruct((6, S, W), BF16),
        scratch_shapes=[
            pltpu.VMEM((S, HD), F32),
            pltpu.VMEM((S, HD), F32),
            pltpu.VMEM((S // TK, TQ, TK), F32),
            pltpu.VMEM((S // TK, TQ, TK), F32),
        ],
        compiler_params=_cp(("parallel", "arbitrary")),
        name=name,
    )(p, p, p, dcat)


def fox_gate_fwd(f, b, name):
    S = f.shape[0]
    nq = S // HD

    def body(f_ref, b_ref, c_ref, run):
        i = pl.program_id(0)

        @pl.when(i == 0)
        def _():
            run[...] = jnp.zeros_like(run)

        lf = -_softplus(-(f_ref[...] + b_ref[...]))
        tri = (_iota2((HD, HD), 0) >= _iota2((HD, HD), 1)).astype(BF16)
        c_ref[...] = _dot_ones_left(tri, lf) + run[...]
        run[...] += jnp.sum(lf, axis=0, keepdims=True)

    return pl.pallas_call(
        body,
        grid=(nq,),
        in_specs=[pl.BlockSpec((HD, 128), lambda i: (i, 0)), pl.BlockSpec((1, 128), lambda i: (0, 0))],
        out_specs=pl.BlockSpec((HD, 128), lambda i: (i, 0)),
        out_shape=jax.ShapeDtypeStruct((S, 128), F32),
        scratch_shapes=[pltpu.VMEM((1, 128), F32)],
        compiler_params=_cp(("arbitrary",)),
        name=name,
    )(f, b)


def fox_gate_bwd(f, b, dc, name):
    S = f.shape[0]
    nq = S // HD

    def body(f_ref, b_ref, dc_ref, df_ref, db_ref, run):
        i = pl.program_id(0)

        @pl.when(i == 0)
        def _():
            run[...] = jnp.zeros_like(run)

        dc = dc_ref[...]
        tri = (_iota2((HD, HD), 0) <= _iota2((HD, HD), 1)).astype(BF16)
        dlf = _dot_ones_left(tri, dc) + run[...]
        run[...] += jnp.sum(dc, axis=0, keepdims=True)
        x = f_ref[...] + b_ref[...]
        df = dlf * (1.0 / (1.0 + jnp.exp(x)))
        df_ref[...] = df
        db = jnp.sum(df, axis=0, keepdims=True)

        @pl.when(i == 0)
        def _():
            db_ref[...] = db

        @pl.when(i > 0)
        def _():
            db_ref[...] += db

    rev = pl.BlockSpec((HD, 128), lambda i: (nq - 1 - i, 0))
    vec = pl.BlockSpec((1, 128), lambda i: (0, 0))
    return pl.pallas_call(
        body,
        grid=(nq,),
        in_specs=[rev, vec, rev],
        out_specs=[rev, vec],
        out_shape=[jax.ShapeDtypeStruct((S, 128), F32), jax.ShapeDtypeStruct((1, 128), F32)],
        scratch_shapes=[pltpu.VMEM((1, 128), F32)],
        compiler_params=_cp(("arbitrary",)),
        name=name,
    )(f, b, dc)


def _fox_logits(q, ks, ct, cs, row0, off):
    s = _dot_nt(q, ks) * (HD ** -0.5) + (ct - cs)
    mask = _strip_mask(s.shape, row0, off, False)
    return jnp.where(mask, s, -1e30), mask


def fox_fwd(p, ccol, crow, cat, W, name):
    S = p.shape[0]
    TQ, TK = _query_rows(S), _key_strip(S)
    nh, nq = W // HD, S // TQ

    def body(q_ref, k_ref, v_ref, cc_ref, cr_ref, cat_ref, o_ref, lse_ref):
        i = pl.program_id(1)
        q = q_ref[...]
        ct = cc_ref[0]

        def step(g, carry):
            m, l, acc = carry
            off = pl.multiple_of(g * TK, TK)
            s, _ = _fox_logits(q, k_ref[pl.ds(off, TK), :], ct, cr_ref[0, pl.ds(g, 1), :], i * TQ, off)
            m_new = jnp.maximum(m, jnp.max(s, axis=1, keepdims=True))
            alpha = jnp.exp(m - m_new)
            pr = jnp.exp(s - m_new)
            l = alpha * l + jnp.sum(pr, axis=1, keepdims=True)
            acc = alpha * acc + _dot(pr.astype(BF16), v_ref[pl.ds(off, TK), :])
            return m_new, l, acc

        init = (jnp.full((TQ, 1), -1e30, F32), jnp.zeros((TQ, 1), F32), jnp.zeros((TQ, HD), F32))
        m, l, acc = lax.fori_loop(0, (i * TQ + TQ - 1) // TK + 1, step, init)
        o_ref[...] = (acc / l).astype(o_ref.dtype)
        lse_ref[0] = m + jnp.log(l)

    return pl.pallas_call(
        body,
        grid=(nh, nq),
        in_specs=[
            pl.BlockSpec((TQ, HD), lambda h, i: (i, 2 * nh + h)),
            pl.BlockSpec((S, HD), lambda h, i: (0, 3 * nh + h)),
            pl.BlockSpec((S, HD), lambda h, i: (0, 4 * nh + h)),
            pl.BlockSpec((1, TQ, 1), lambda h, i: (h, i, 0)),
            pl.BlockSpec((1, S // TK, TK), lambda h, i: (h, 0, 0)),
            pl.BlockSpec(memory_space=pl.ANY),
        ],
        out_specs=[pl.BlockSpec((TQ, HD), lambda h, i: (i, nh + h)), pl.BlockSpec((1, TQ, 1), lambda h, i: (h, i, 0))],
        out_shape=[jax.ShapeDtypeStruct(cat.shape, cat.dtype), jax.ShapeDtypeStruct((nh, S, 1), F32)],
        input_output_aliases={5: 0},
        compiler_params=_cp(("parallel", "arbitrary")),
        name=name,
    )(p, p, p, ccol, crow, cat)


def fox_bwd(p, ccol, crow, cat, lse, dcat, dp, W, name):
    S = p.shape[0]
    TQ, TK = _query_rows(S), _key_strip(S)
    nh, nq = W // HD, S // TQ
    scale = HD ** -0.5

    def body(q_ref, k_ref, v_ref, cc_ref, cr_ref, o_ref, lse_ref, do_ref, dp_in_ref, dp_ref, dcs_ref, dct_ref, dk_acc, dv_acc):
        i = pl.program_id(1)
        q = q_ref[...]
        do = do_ref[...]
        ct = cc_ref[0]
        lse_i = lse_ref[0]
        delta = jnp.sum(do.astype(F32) * o_ref[...].astype(F32), axis=1, keepdims=True)

        @pl.when(i == 0)
        def _():
            dk_acc[...] = jnp.zeros_like(dk_acc)
            dv_acc[...] = jnp.zeros_like(dv_acc)
            dcs_ref[...] = jnp.zeros_like(dcs_ref)

        def step(g, carry):
            dq, dct = carry
            off = pl.multiple_of(g * TK, TK)
            ks = k_ref[pl.ds(off, TK), :]
            s, mask = _fox_logits(q, ks, ct, cr_ref[0, pl.ds(g, 1), :], i * TQ, off)
            pr = jnp.where(mask, jnp.exp(s - lse_i), 0.0)
            ds = pr * (_dot_nt(do, v_ref[pl.ds(off, TK), :]) - delta)
            dv_acc[pl.ds(off, TK), :] += _dot_tn(pr.astype(BF16), do)
            dsb = (ds * scale).astype(BF16)
            dk_acc[pl.ds(off, TK), :] += _dot_tn(dsb, q)
            dcs_ref[0, pl.ds(g, 1), :] += jnp.sum(ds, axis=0, keepdims=True)
            return dq + _dot(dsb, ks), dct + jnp.sum(ds, axis=1, keepdims=True)

        dq, dct = lax.fori_loop(0, (i * TQ + TQ - 1) // TK + 1, step, (jnp.zeros((TQ, HD), F32), jnp.zeros((TQ, 1), F32)))
        dp_ref[0, pl.ds(pl.multiple_of(i * TQ, TQ), TQ), :] = dq.astype(dp_ref.dtype)
        dct_ref[0] = dct

        @pl.when(i == nq - 1)
        def _():
            dp_ref[1] = dk_acc[...].astype(dp_ref.dtype)
            dp_ref[2] = dv_acc[...].astype(dp_ref.dtype)

    return pl.pallas_call(
        body,
        grid=(nh, nq),
        in_specs=[
            pl.BlockSpec((TQ, HD), lambda h, i: (i, 2 * nh + h)),
            pl.BlockSpec((S, HD), lambda h, i: (0, 3 * nh + h)),
            pl.BlockSpec((S, HD), lambda h, i: (0, 4 * nh + h)),
            pl.BlockSpec((1, TQ, 1), lambda h, i: (h, i, 0)),
            pl.BlockSpec((1, S // TK, TK), lambda h, i: (h, 0, 0)),
            pl.BlockSpec((TQ, HD), lambda h, i: (i, nh + h)),
            pl.BlockSpec((1, TQ, 1), lambda h, i: (h, i, 0)),
            pl.BlockSpec((TQ, HD), lambda h, i: (i, nh + h)),
            pl.BlockSpec(memory_space=pl.ANY),
        ],
        out_specs=[
            pl.BlockSpec((3, S, HD), lambda h, i: (1, 0, h)),
            pl.BlockSpec((1, S // TK, TK), lambda h, i: (h, 0, 0)),
            pl.BlockSpec((1, TQ, 1), lambda h, i: (h, i, 0)),
        ],
        out_shape=[
            jax.ShapeDtypeStruct(dp.shape, dp.dtype),
            jax.ShapeDtypeStruct((nh, S // TK, TK), F32),
            jax.ShapeDtypeStruct((nh, S, 1), F32),
        ],
        input_output_aliases={8: 0},
        scratch_shapes=[pltpu.VMEM((S, HD), F32), pltpu.VMEM((S, HD), F32)],
        compiler_params=_cp(("parallel", "arbitrary")),
        name=name,
    )(p, p, p, ccol, crow, cat, lse, dcat, dp)


_GELU_K = math.sqrt(2.0 / math.pi)
_GELU_C = 0.044715


def _gelu(x):
    return 0.5 * x * (1.0 + jnp.tanh(_GELU_K * (x + _GELU_C * x * x * x)))


def _gelu_grad(x):
    t = jnp.tanh(_GELU_K * (x + _GELU_C * x * x * x))
    return 0.5 * (1.0 + t) + 0.5 * x * (1.0 - t * t) * (_GELU_K * (1.0 + 3.0 * _GELU_C * x * x))


def _layernorm_parts(gv):
    xc = gv - jnp.mean(gv, axis=-1, keepdims=True)
    r = lax.rsqrt(jnp.mean(xc * xc, axis=-1, keepdims=True) + EPS)
    return xc * r, r


def sg_fwd(p, sg_w, sg_bt, sg_g, W, name):
    S = p.shape[0]
    G, nq = W // HD, S // HD

    def body(u_ref, v_ref, w_ref, bt_ref, g_ref, o_ref):
        xh, _ = _layernorm_parts(_gelu(v_ref[...].astype(F32)))
        vn = (xh * g_ref[...]).astype(BF16)
        tri = _iota2((HD, HD), 0) >= _iota2((HD, HD), 1)
        for gi in range(G):
            cols = slice(gi * HD, (gi + 1) * HD)
            wt = jnp.where(tri, w_ref[gi], 0.0).astype(BF16)
            mixed = _dot(wt, vn[:, cols]) + bt_ref[:, gi : gi + 1]
            o_ref[:, cols] = (_gelu(u_ref[:, cols].astype(F32)) * mixed).astype(o_ref.dtype)

    return pl.pallas_call(
        body,
        grid=(nq,),
        in_specs=[
            pl.BlockSpec((HD, W), lambda i: (i, 0)),
            pl.BlockSpec((HD, W), lambda i: (i, 1)),
            pl.BlockSpec((G, HD, HD), lambda i: (0, 0, 0)),
            pl.BlockSpec((HD, G), lambda i: (0, 0)),
            pl.BlockSpec((1, W), lambda i: (0, 0)),
        ],
        out_specs=pl.BlockSpec((HD, W), lambda i: (i, 0)),
        out_shape=jax.ShapeDtypeStruct((S, 2 * W), BF16),
        compiler_params=_cp(("parallel",)),
        name=name,
    )(p, p, sg_w, sg_bt, sg_g.reshape(1, W))


def sg_bwd(p, sg_w, sg_bt, sg_g, dcat, W, name):
    S = p.shape[0]
    G, nq = W // HD, S // HD

    def body(u_ref, v_ref, w_ref, bt_ref, g_ref, do_ref, dp_ref, dw_ref, dbt_ref, dg_ref, dvn_scr):
        i = pl.program_id(0)

        @pl.when(i == 0)
        def _():
            dw_ref[...] = jnp.zeros_like(dw_ref)
            dbt_ref[...] = jnp.zeros_like(dbt_ref)
            dg_ref[...] = jnp.zeros_like(dg_ref)

        v = v_ref[...].astype(F32)
        xh, r = _layernorm_parts(_gelu(v))
        gg = g_ref[...]
        vn = (xh * gg).astype(BF16)
        tri = _iota2((HD, HD), 0) >= _iota2((HD, HD), 1)
        for gi in range(G):
            cols = slice(gi * HD, (gi + 1) * HD)
            wt = jnp.where(tri, w_ref[gi], 0.0).astype(BF16)
            mixed = _dot(wt, vn[:, cols]) + bt_ref[:, gi : gi + 1]
            u = u_ref[:, cols].astype(F32)
            do = do_ref[:, cols].astype(F32)
            dp_ref[0, :, cols] = (do * mixed * _gelu_grad(u)).astype(dp_ref.dtype)
            dmix = do * _gelu(u)
            dmb = dmix.astype(BF16)
            dw_ref[gi] += jnp.where(tri, _dot_nt(dmb, vn[:, cols]), 0.0)
            dbt_ref[:, gi : gi + 1] += jnp.sum(dmix, axis=1, keepdims=True)
            dvn_scr[:, cols] = _dot_tn(wt, dmb)
        dvn = dvn_scr[...]
        dg_ref[...] += jnp.sum(dvn * xh, axis=0, keepdims=True)
        dxh = dvn * gg
        dgv = r * (dxh - jnp.mean(dxh, axis=-1, keepdims=True) - xh * jnp.mean(dxh * xh, axis=-1, keepdims=True))
        dp_ref[1] = (dgv * _gelu_grad(v)).astype(dp_ref.dtype)

    return pl.pallas_call(
        body,
        grid=(nq,),
        in_specs=[
            pl.BlockSpec((HD, W), lambda i: (i, 0)),
            pl.BlockSpec((HD, W), lambda i: (i, 1)),
            pl.BlockSpec((G, HD, HD), lambda i: (0, 0, 0)),
            pl.BlockSpec((HD, G), lambda i: (0, 0)),
            pl.BlockSpec((1, W), lambda i: (0, 0)),
            pl.BlockSpec((HD, W), lambda i: (i, 0)),
        ],
        out_specs=[
            pl.BlockSpec((2, HD, W), lambda i: (0, i, 0)),
            pl.BlockSpec((G, HD, HD), lambda i: (0, 0, 0)),
            pl.BlockSpec((HD, G), lambda i: (0, 0)),
            pl.BlockSpec((1, W), lambda i: (0, 0)),
        ],
        out_shape=[
            jax.ShapeDtypeStruct((6, S, W), BF16),
            jax.ShapeDtypeStruct((G, HD, HD), F32),
            jax.ShapeDtypeStruct((HD, G), F32),
            jax.ShapeDtypeStruct((1, W), F32),
        ],
        scratch_shapes=[pltpu.VMEM((HD, W), F32)],
        compiler_params=_cp(("arbitrary",)),
        name=name,
    )(p, p, sg_w, sg_bt, sg_g.reshape(1, W), dcat)


def local_step(x, target, wts, at, on_grad):
    S, D = x.shape
    W = D // 2
    nb, F = wts["nb"], wts["F"]
    g = {}

    def ffn_fwd(xin, l):
        h = rms_fwd(xin, wts[f"{l}_ffn_norm_g"], f"{l}_ffn_rms")
        u = mm_nn(h, wts[f"{l}_ffn_up"], nb, f"{l}_ffn_up_mm")
        act = ffn_act_fwd(u, wts[f"{l}_ffn_conv_w"], F, f"{l}_ffn_act")
        half_tile = _pick(S, (512, 256, 128))
        xout = mm_nn(act, wts[f"{l}_ffn_down"], 1, f"{l}_ffn_down_mm", out_dtype=F32, res=xin,
                     tm=half_tile, tn=_pick(D, (512, 256, 128)), tk=F)
        return xout, (xin, h, u, act)

    def ffn_bwd(dxout, dxoutb, saved, l):
        xin, h, u, act = saved
        dact = mm_nt(dxoutb, wts[f"{l}_ffn_down"], 1, S, F, f"{l}_ffn_down_dx", tko=_pick(F, (512, 256, 128)), tn=D)
        dact = on_grad(f"{l}_ffn_down", mm_tn(act, dxoutb, 1, D, f"{l}_ffn_down_dw", tn=D), dact)
        du, dcw = ffn_act_bwd(u, wts[f"{l}_ffn_conv_w"], dact, F, f"{l}_ffn_act_bwd")
        g[f"{l}_ffn_conv_w"] = jnp.concatenate([dcw[0], dcw[1]], axis=1)
        du2 = du.reshape(2 * S, F)
        n = wts[f"{l}_ffn_up"].shape[1]
        tn = _pick(n, (1408, 1024, 768, 512, 256, 128))
        per_half = F // tn
        nt = n // tn

        def up_block(i, j, t):
            vb = j * nt + t
            return vb // per_half, vb % per_half

        tm = _pick(S, (1024, 512, 256, 128))

        def nt_map(i, j, t):
            half, cb = up_block(i, j, t)
            return (half * (S // tm) + i, cb)

        def tn_map(j, t):
            half, cb = up_block(0, j, t)
            return (half, cb)

        dh = mm_nt(du2, wts[f"{l}_ffn_up"], nb, S, D, f"{l}_ffn_up_dx", dy_maps=[nt_map], tm=tm, tko=D, tn=tn)
        dh = on_grad(f"{l}_ffn_up", mm_tn(h, du2, nb, n, f"{l}_ffn_up_dw", dy_maps=[tn_map], tn=tn), dh)
        dxin, dxinb, dg = rms_bwd(xin, wts[f"{l}_ffn_norm_g"], dh, dxout, f"{l}_ffn_rms_bwd")
        g[f"{l}_ffn_norm_g"] = dg
        return dxin, dxinb

    h0 = rms_fwd(x, wts["l0_mix_norm_g"], "l0_mix_rms")
    p0 = mm_nn(h0, wts["l0_w_in"], nb, "l0_w_in_mm")
    cat0 = sb_fwd(p0, W, "l0_sb_fwd")
    cat0 = sc_fwd(p0, wts["l0_sc_conv_w"], cat0, W, "l0_sc_fwd")
    x1 = mm_nn(cat0, wts["l0_w_out"], 1, "l0_w_out_mm", out_dtype=F32, res=x, tm=S, tn=_pick(D, (512, 256, 128)))
    x2, ffn0_saved = ffn_fwd(x1, "l0")

    x2 = at("l1_w_in", x2, None)
    nh = W // HD
    h2 = rms_fwd(x2, wts["l1_mix_norm_g"], "l1_mix_rms")
    p1 = mm_nt(h2, wts["l1_w_in_t"], 1, S, 5 * W, "l1_w_in_mm", tn=D)
    f = mm_nt(h2, wts["l1_w_f_t"], 1, S, 128, "l1_w_f_mm", out_dtype=F32, tn=D)
    bf = jnp.zeros((1, 128), F32).at[0, :nh].set(wts["l1_fox_b_f"])
    c = fox_gate_fwd(f, bf, "l1_fox_gate")
    c_heads = c[:, :nh].T
    ccol = c_heads[:, :, None]
    crow = c_heads.reshape(nh, S // _key_strip(S), _key_strip(S))
    sg_bt = wts["l1_sg_b"].T
    cat1 = sg_fwd(p1, wts["l1_sg_w"], sg_bt, wts["l1_sg_norm_g"], W, "l1_sg_fwd")
    cat1, lse = fox_fwd(p1, ccol, crow, cat1, W, "l1_fox_fwd")
    x3 = mm_nn(cat1, wts["l1_w_out"], 1, "l1_w_out_mm", out_dtype=F32, res=x2, tm=S, tn=_pick(D, (512, 256, 128)))
    x4, ffn1_saved = ffn_fwd(x3, "l1")

    dx4, dx4b, dgf, loss = loss_head(x4, wts["final_norm_g"], target, "loss_head")
    dx4b = at("loss", dx4b, loss)
    g["final_norm_g"] = dgf

    dx3, dx3b = ffn_bwd(dx4, dx4b, ffn1_saved, "l1")
    dcat1 = mm_nt(dx3b, wts["l1_w_out"], 1, S, D, "l1_w_out_dx", tn=D)
    dcat1 = on_grad("l1_w_out", mm_tn(cat1, dx3b, 1, D, "l1_w_out_dw", tn=D), dcat1)
    dp1, dsgw, dsgbt, dsgg = sg_bwd(p1, wts["l1_sg_w"], sg_bt, wts["l1_sg_norm_g"], dcat1, W, "l1_sg_bwd")
    dp1, dcs, dct = fox_bwd(p1, ccol, crow, cat1, lse, dcat1, dp1, W, "l1_fox_bwd")
    g["l1_sg_w"], g["l1_sg_b"], g["l1_sg_norm_g"] = dsgw, dsgbt.T, dsgg
    dc = jnp.zeros((S, 128), F32).at[:, :nh].set((dct[:, :, 0] - dcs.reshape(nh, S)).T)
    df, dbf = fox_gate_bwd(f, bf, dc, "l1_fox_gate_bwd")
    g["l1_fox_b_f"] = dbf[0, :nh]
    dfb = df.astype(BF16)
    tk1 = _pick(W, (1024, 512, 256, 128))
    tx1 = _pick(W, (512, 256, 128))
    tm1 = _pick(S, (1024, 512, 256, 128))
    part_of = lambda pt: pt + pt // 2 - pt // 4

    def a_map1(i, k):
        return (part_of(k // (W // tk1)) * (S // tm1) + i, k % (W // tk1))

    def x_map1(ko):
        return (part_of(ko // (W // tx1)), ko % (W // tx1))

    dp1_2d = dp1.reshape(6 * S, W)
    dw_main = mm_tn(dp1_2d, h2, 1, D, "l1_w_in_dw", tko=tx1, tn=D, x_map=x_map1, x_shape=(S, 5 * W))
    dw_f = mm_tn(dfb, h2, 1, D, "l1_w_f_dw", tn=D)
    dh2 = mm_nn(dfb, wts["l1_w_f_t"], 1, "l1_w_f_dx", out_dtype=F32)
    dh2 = mm_nn(dp1_2d, wts["l1_w_in_t"], 1, "l1_w_in_dx", res=dh2, tm=tm1, tk=tk1, a_map=a_map1, a_shape=(S, 5 * W))
    dh2 = on_grad("l1_w_in", jnp.concatenate([dw_main, dw_f[:nh]], axis=0), dh2)
    dx2, dx2b, dg = rms_bwd(x2, wts["l1_mix_norm_g"], dh2, dx3, "l1_mix_rms_bwd")
    g["l1_mix_norm_g"] = dg

    dx1, dx1b = ffn_bwd(dx2, dx2b, ffn0_saved, "l0")
    dcat0 = mm_nt(dx1b, wts["l0_w_out"], 1, S, D, "l0_w_out_dx", tn=D)
    dcat0 = on_grad("l0_w_out", mm_tn(cat0, dx1b, 1, D, "l0_w_out_dw", tn=D), dcat0)
    dp0 = sb_bwd(p0, dcat0, W, "l0_sb_bwd")
    dp0, dscw = sc_bwd(p0, wts["l0_sc_conv_w"], dcat0, dp0, W, "l0_sc_bwd")
    g["l0_sc_conv_w"] = dscw
    dp0 = at("small_ready", dp0, g)
    n0 = wts["l0_w_in"].shape[1]
    td0 = math.gcd(n0, W)
    nd0 = n0 // td0
    tm0 = _pick(S, (1024, 512, 256, 128))
    per_part0 = W // td0

    def nt_maps0(k):
        def f(i, j, t):
            vb = j * nd0 + k
            return ((vb // per_part0) * (S // tm0) + i, vb % per_part0)
        return f

    def tn_maps0(k):
        def f(j, t):
            vb = j * nd0 + k
            return (vb // per_part0, vb % per_part0)
        return f

    dp0_2d = dp0.reshape(6 * S, W)
    dw0 = mm_tn(h0, dp0_2d, nb, n0, "l0_w_in_dw", dy_maps=[tn_maps0(k) for k in range(nd0)], tn=n0)
    dp0_2d = on_grad("l0_w_in", dw0, dp0_2d)
    dp0_2d = on_grad(None, None, dp0_2d)
    dh0 = mm_nt(dp0_2d, wts["l0_w_in"], nb, S, D, "l0_w_in_dx", dy_maps=[nt_maps0(k) for k in range(nd0)], tm=tm0, tn=n0)
    dh0 = at("small_done", dh0, None)
    dx0, _, dg = rms_bwd(x, wts["l0_mix_norm_g"], dh0, dx1, "l0_mix_rms_bwd")
    g["l0_mix_norm_g"] = dg
    return dx0, g


GATHER_ID = 1


def _place():
    return lax.axis_index("x"), lax.axis_index("y"), lax.axis_index("c")


def _other_chips(x, y):
    return [(x, 1 - y), (1 - x, y), (1 - x, 1 - y)]


def _handshake(peers):
    barrier = pltpu.get_barrier_semaphore()
    for peer in peers:
        pl.semaphore_signal(barrier, inc=1, device_id=peer, device_id_type=MESH)
    pl.semaphore_wait(barrier, len(peers))


UPDATE_LAG = 2


def _on_sequencer(body, out_type, scratch_types, collective_id, name):
    return pl.kernel(
        body,
        out_type=out_type,
        mesh=plsc.ScalarSubcoreMesh(axis_name="seq", num_cores=1),
        scratch_types=scratch_types,
        compiler_params=pltpu.CompilerParams(collective_id=collective_id),
        name=name,
    )


def all_gather(arrs, name):
    n = len(arrs)

    def body(*refs):
        xs, outs = refs[:n], refs[n : 2 * n]
        send_sems, recv_sems, local_sems = refs[2 * n :]
        x, y, c = _place()
        me, sibling = (x, y, c), (x, y, 1 - c)
        chips = _other_chips(x, y)
        _handshake([sibling] + [(*chip, c) for chip in chips])

        def copy(a, k, block, to, src=None):
            px, py, pc = block
            dst = outs[a].at[4 * px + 2 * py + pc]
            return pltpu.make_async_remote_copy(
                src_ref=dst if src is None else src, dst_ref=dst,
                send_sem=send_sems.at[7 * a + k], recv_sem=recv_sems.at[7 * a + k], device_id=to, device_id_type=MESH,
            )

        mine = [pltpu.make_async_copy(xs[a], outs[a].at[4 * x + 2 * y + c], local_sems.at[a]) for a in range(n)]
        for cp in mine:
            cp.start()
        first = []
        for a in range(n):
            first.append(copy(a, 0, me, sibling, src=xs[a]))
            first += [copy(a, 1 + j, me, (*chip, c), src=xs[a]) for j, chip in enumerate(chips)]
        for cp in first:
            cp.start()
        passed = []
        for a in range(n):
            for j, chip in enumerate(chips):
                copy(a, 1 + j, (*chip, c), me).wait_recv()
                cp = copy(a, 4 + j, (*chip, c), sibling)
                cp.start()
                passed.append(cp)
        for a in range(n):
            copy(a, 0, sibling, me).wait_recv()
            for j, chip in enumerate(chips):
                copy(a, 4 + j, (*chip, 1 - c), me).wait_recv()
        for cp in first + passed:
            cp.wait_send()
        for cp in mine:
            cp.wait()

    out_type = [jax.ShapeDtypeStruct((NDEV,) + a.shape, a.dtype) for a in arrs]
    sems = [pltpu.SemaphoreType.DMA((7 * n,)), pltpu.SemaphoreType.DMA((7 * n,)), pltpu.SemaphoreType.DMA((n,))]
    return _on_sequencer(body, out_type, sems, GATHER_ID, name)(*arrs)


def all_gather_direct(arr, name):
    R, C = arr.shape

    def body(x_ref, out_ref, send_sems, recv_sems, local_sem):
        x, y, c = _place()
        me = 4 * x + 2 * y + c

        def copy(k, slot, to):
            return pltpu.make_async_remote_copy(
                src_ref=x_ref, dst_ref=out_ref.at[slot], send_sem=send_sems.at[k], recv_sem=recv_sems.at[k],
                device_id=(to // 4, (to // 2) % 2, to % 2), device_id_type=MESH,
            )

        mine = pltpu.make_async_copy(x_ref, out_ref.at[me], local_sem)
        mine.start()
        sends = [copy(k, me, (me + k + 1) % NDEV) for k in range(NDEV - 1)]
        for cp in sends:
            cp.start()
        for k in range(NDEV - 1):
            sender = (me + NDEV - k - 1) % NDEV
            copy(k, sender, sender).wait_recv()
        for cp in sends:
            cp.wait_send()
        mine.wait()

    vmem = pl.BlockSpec(memory_space=pltpu.VMEM)
    return pl.pallas_call(
        body,
        in_specs=[vmem],
        out_specs=vmem,
        out_shape=jax.ShapeDtypeStruct((NDEV, R, C), arr.dtype),
        scratch_shapes=[pltpu.SemaphoreType.DMA((NDEV - 1,)), pltpu.SemaphoreType.DMA((NDEV - 1,)), pltpu.SemaphoreType.DMA],
        name=name,
    )(arr)


_IN_HBM = pl.BlockSpec(memory_space=pltpu.HBM)
_IN_SEM = pl.BlockSpec(memory_space=pltpu.SEMAPHORE)
_EFFECT = pltpu.SideEffectType.DATAFLOW_SIDE_EFFECTING


def _split_start(make_copies, src, land_shape, nsem, name):
    def body(src_ref, land_ref, send_sems, recv_sems, land_thru, token):
        for cp in make_copies(src_ref, land_ref, send_sems, recv_sems):
            cp.start()
        token[...] = jnp.zeros_like(token)

    send_sems, recv_sems, land_thru, token = pl.pallas_call(
        body,
        name=name,
        out_shape=(
            pltpu.SemaphoreType.DMA((nsem,)), pltpu.SemaphoreType.DMA((nsem,)),
            pltpu.HBM(land_shape, src.dtype), jax.ShapeDtypeStruct((8, 128), F32),
        ),
        in_specs=(_IN_HBM, _IN_HBM),
        out_specs=(_IN_SEM, _IN_SEM, _IN_HBM, pl.BlockSpec(memory_space=pltpu.VMEM)),
        input_output_aliases={1: 2},
        compiler_params=pltpu.CompilerParams(has_side_effects=_EFFECT),
    )(src, pltpu.with_memory_space_constraint(lax.empty(land_shape, src.dtype), pltpu.HBM))
    return send_sems, recv_sems, src, land_thru, token


def _split_wait(make_copies, send_sems, recv_sems, src_thru, land_thru, after, name):
    def body(src_ref, land_ref, send_sems, recv_sems, after_ref, land_out):
        for cp in make_copies(src_ref, land_ref, send_sems, recv_sems):
            cp.wait_send()
            cp.wait_recv()

    return pl.pallas_call(
        body,
        name=name,
        out_shape=pltpu.HBM(land_thru.shape, land_thru.dtype),
        in_specs=(_IN_HBM, _IN_HBM, _IN_SEM, _IN_SEM, pl.BlockSpec(memory_space=pl.ANY)),
        out_specs=_IN_HBM,
        input_output_aliases={1: 0},
        compiler_params=pltpu.CompilerParams(has_side_effects=_EFFECT),
    )(src_thru, land_thru, send_sems, recv_sems, after)


def _pair_copies(src_ref, land_ref, send_sems, recv_sems):
    x, y, c = _place()
    return [
        pltpu.make_async_remote_copy(
            src_ref=src_ref.at[k, 1 - c], dst_ref=land_ref.at[k],
            send_sem=send_sems.at[k], recv_sem=recv_sems.at[k], device_id=(x, y, 1 - c), device_id_type=MESH,
        )
        for k in range(4)
    ]


def _chip_copies(src_ref, land_ref, send_sems, recv_sems):
    x, y, c = _place()
    return [
        pltpu.make_async_remote_copy(
            src_ref=src_ref.at[2 * px + py], dst_ref=land_ref.at[2 * x + y],
            send_sem=send_sems.at[j], recv_sem=recv_sems.at[j], device_id=(px, py, c), device_id_type=MESH,
        )
        for j, (px, py) in enumerate(_other_chips(x, y))
    ]


def _row_tile(R, C, max_elems):
    if R * C <= max_elems:
        return R
    best = None
    for tr in range(16, R, 16):
        if R % tr == 0 and tr * C <= max_elems:
            best = tr
    return best or R


def pair_sum(a42, land4, core, name):
    _, _, R, C = a42.shape
    tr = _row_tile(R, C, 1 << 20)

    def body(core_ref, a_ref, l_ref, o_ref):
        o_ref[...] = (a_ref[0].astype(F32) + l_ref[...].astype(F32)).astype(o_ref.dtype)

    return pl.pallas_call(
        body,
        grid_spec=pltpu.PrefetchScalarGridSpec(
            num_scalar_prefetch=1,
            grid=(4, R // tr),
            in_specs=[
                pl.BlockSpec((1, 1, tr, C), lambda k, r, core_ref: (k, core_ref[0], r, 0)),
                pl.BlockSpec((1, tr, C), lambda k, r, core_ref: (k, r, 0)),
            ],
            out_specs=pl.BlockSpec((1, tr, C), lambda k, r, core_ref: (k, r, 0)),
        ),
        out_shape=jax.ShapeDtypeStruct((4, R, C), BF16),
        compiler_params=_cp(("parallel", "parallel")),
        name=name,
    )(core, a42, land4)


def sum_slots(parts, name):
    P, R, C = parts.shape

    def body(p_ref, o_ref):
        acc = p_ref[0].astype(F32)
        for k in range(1, P):
            acc = acc + p_ref[k].astype(F32)
        o_ref[...] = acc

    tr = _row_tile(R, P * C, 1 << 21)
    return pl.pallas_call(
        body,
        grid=(R // tr,),
        in_specs=[pl.BlockSpec((P, tr, C), lambda r: (0, r, 0))],
        out_specs=pl.BlockSpec((tr, C), lambda r: (r, 0)),
        out_shape=jax.ShapeDtypeStruct((R, C), F32),
        compiler_params=_cp(("parallel",)),
        name=name,
    )(parts)


def adamw(w, m, v, parts, name):
    R, C = w.shape
    P = parts.shape[0]
    tr = _pick(R, (256, 128, 64, 32, 16, 8))
    c1 = 1.0 - ADAM_B1 ** ADAM_STEP
    c2 = 1.0 - ADAM_B2 ** ADAM_STEP

    def body(w_ref, m_ref, v_ref, p_ref, g_ref, d_ref, nm_ref, nv_ref):
        g = p_ref[0].astype(F32)
        for k in range(1, P):
            g = g + p_ref[k].astype(F32)
        nm = ADAM_B1 * m_ref[...] + (1.0 - ADAM_B1) * g
        nv = ADAM_B2 * v_ref[...] + (1.0 - ADAM_B2) * (g * g)
        g_ref[...] = g
        nm_ref[...] = nm
        nv_ref[...] = nv
        d_ref[...] = -ADAM_LR * ((nm / c1) / (jnp.sqrt(nv / c2) + ADAM_EPS) + ADAM_WD * w_ref[...])

    blk = pl.BlockSpec((tr, C), lambda r: (r, 0))
    shp = jax.ShapeDtypeStruct((R, C), F32)
    return pl.pallas_call(
        body,
        grid=(R // tr,),
        in_specs=[blk, blk, blk, pl.BlockSpec((P, tr, C), lambda r: (0, r, 0))],
        out_specs=[blk, blk, blk, blk],
        out_shape=[shp, shp, shp, shp],
        compiler_params=_cp(("parallel",)),
        name=name,
    )(w, m, v, parts)


def adamw_reduced(w, m, v, own, land, chip, name):
    R, C = w.shape
    if R % 8 == 0:
        tr, tc = _pick(R, (256, 128, 64, 32, 16, 8)), C
    else:
        tr, tc = R, _pick(C, (256, 128))
    c1 = 1.0 - ADAM_B1 ** ADAM_STEP
    c2 = 1.0 - ADAM_B2 ** ADAM_STEP

    def body(chip_ref, w_ref, m_ref, v_ref, own_ref, land_ref, g_ref, d_ref, nm_ref, nv_ref):
        mine = own_ref[0].astype(F32)
        g = None
        for k in range(4):
            term = jnp.where(chip_ref[0] == k, mine, land_ref[k].astype(F32))
            g = term if g is None else g + term
        nm = ADAM_B1 * m_ref[...] + (1.0 - ADAM_B1) * g
        nv = ADAM_B2 * v_ref[...] + (1.0 - ADAM_B2) * (g * g)
        g_ref[...] = g
        nm_ref[...] = nm
        nv_ref[...] = nv
        d_ref[...] = -ADAM_LR * ((nm / c1) / (jnp.sqrt(nv / c2) + ADAM_EPS) + ADAM_WD * w_ref[...])

    blk = pl.BlockSpec((tr, tc), lambda r, c, chip_ref: (r, c))
    shp = jax.ShapeDtypeStruct((R, C), F32)
    return pl.pallas_call(
        body,
        grid_spec=pltpu.PrefetchScalarGridSpec(
            num_scalar_prefetch=1,
            grid=(R // tr, C // tc),
            in_specs=[
                blk, blk, blk,
                pl.BlockSpec((1, tr, tc), lambda r, c, chip_ref: (chip_ref[0], r, c)),
                pl.BlockSpec((4, tr, tc), lambda r, c, chip_ref: (0, r, c)),
            ],
            out_specs=[blk, blk, blk, blk],
        ),
        out_shape=[shp, shp, shp, shp],
        compiler_params=_cp(("parallel", "parallel")),
        name=name,
    )(chip, w, m, v, own, land)


_WEIGHTS = [
    "l0_mix_norm_g", "l0_w_in", "l0_sc_conv_w", "l0_w_out", "l0_ffn_norm_g", "l0_ffn_up", "l0_ffn_conv_w", "l0_ffn_down",
    "l1_mix_norm_g", "l1_w_in", "l1_fox_b_f", "l1_sg_w", "l1_sg_b", "l1_sg_norm_g", "l1_w_out", "l1_ffn_norm_g",
    "l1_ffn_up", "l1_ffn_conv_w", "l1_ffn_down", "final_norm_g",
]
_ROW_SHARDED = ["l0_w_out", "l0_ffn_down", "l1_w_out", "l1_ffn_down"]
_BIG = ["l0_w_in", "l0_w_out", "l0_ffn_up", "l0_ffn_down", "l1_w_in", "l1_w_out", "l1_ffn_up", "l1_ffn_down"]
_CONV = ["l0_sc_conv_w", "l0_ffn_conv_w", "l1_ffn_conv_w"]
_SMALL = [n for n in _WEIGHTS if n not in _BIG]
_LAST_SMALL = "l0_mix_norm_g"
_PACK_ROWS = 8


def _pack(arrs):
    flat = []
    for a in arrs:
        v = a.reshape(-1).astype(F32)
        pad = (-v.shape[0]) % (_PACK_ROWS * 128)
        flat.append(jnp.pad(v, (0, pad)))
    return jnp.concatenate(flat).reshape(-1, 128)


def _unpack(packed, shapes):
    out, off = [], 0
    flat = packed.reshape(-1)
    for shp in shapes:
        size = math.prod(shp)
        out.append(flat[off : off + size].reshape(shp))
        off += size + (-size) % (_PACK_ROWS * 128)
    return out


def kernel(x, l0_mix_norm_g, l0_w_in, l0_sc_conv_w, l0_w_out, l0_ffn_norm_g, l0_ffn_up, l0_ffn_conv_w, l0_ffn_down, l1_mix_norm_g, l1_w_in, l1_fox_b_f, l1_sg_w, l1_sg_b, l1_sg_norm_g, l1_w_out, l1_ffn_norm_g, l1_ffn_up, l1_ffn_conv_w, l1_ffn_down, final_norm_g, loss_target, m_l0_mix_norm_g, m_l0_w_in, m_l0_sc_conv_w, m_l0_w_out, m_l0_ffn_norm_g, m_l0_ffn_up, m_l0_ffn_conv_w, m_l0_ffn_down, m_l1_mix_norm_g, m_l1_w_in, m_l1_fox_b_f, m_l1_sg_w, m_l1_sg_b, m_l1_sg_norm_g, m_l1_w_out, m_l1_ffn_norm_g, m_l1_ffn_up, m_l1_ffn_conv_w, m_l1_ffn_down, m_final_norm_g, v_l0_mix_norm_g, v_l0_w_in, v_l0_sc_conv_w, v_l0_w_out, v_l0_ffn_norm_g, v_l0_ffn_up, v_l0_ffn_conv_w, v_l0_ffn_down, v_l1_mix_norm_g, v_l1_w_in, v_l1_fox_b_f, v_l1_sg_w, v_l1_sg_b, v_l1_sg_norm_g, v_l1_w_out, v_l1_ffn_norm_g, v_l1_ffn_up, v_l1_ffn_conv_w, v_l1_ffn_down, v_final_norm_g):
    given = dict(locals())
    w = {n: given[n] for n in _WEIGHTS}
    mom = {n: given["m_" + n] for n in _WEIGHTS}
    var = {n: given["v_" + n] for n in _WEIGHTS}
    xs, target = x[0], loss_target[0]
    S, D = xs.shape
    W = D // 2
    nh = W // HD
    cx, cy, cc = _place()
    me = 4 * cx + 2 * cy + cc

    wts = {"nb": NDEV, "F": l0_ffn_down.shape[0] * NDEV}
    for n in _SMALL:
        if n not in _CONV:
            wts[n] = w[n]
    gathered, loss_sum = {}, []

    def start_gather(n):
        src = w[n].T if n == "l1_w_in" else w[n]
        got = all_gather([src.astype(BF16)] + ([w[c] for c in _CONV] if n == _BIG[0] else []), f"gather_{n}")
        if n == "l1_w_in":
            gathered[n] = got[0]
        elif n in _ROW_SHARDED:
            wts[n] = got[0].reshape(-1, D)
        else:
            wts[n] = got[0].reshape(NDEV * D, -1)
        for c, taps in zip(_CONV, got[1:]):
            wts[c] = taps.transpose(1, 0, 2).reshape(CONV_K, -1)

    def at(point, after, value):
        if point == "l1_w_in":
            got, after = lax.optimization_barrier((gathered[point], after))
            wts["l1_w_in_t"] = got.reshape(-1, D)
            wts["l1_w_f_t"] = jnp.pad(wts["l1_w_in_t"][5 * W :], ((0, 128 - nh), (0, 0)))
        elif point == "loss":
            total, after = lax.optimization_barrier((lax.psum(value[0, 0], ("x", "y", "c")), after))
            loss_sum.append(total)
        elif point == "small_ready":
            early = [n for n in _SMALL if n != _LAST_SMALL]
            gathered["small"] = all_gather([_pack([value[n] for n in early])], "gather_small_grads")[0]
        elif point == "small_done":
            after = update_small([n for n in _SMALL if n != _LAST_SMALL], gathered["small"], "small", after)
        return after

    out_g, out_d, out_m, out_v = {}, {}, {}, {}

    def update_small(names, all_terms, tag, after=None):
        shapes = [w[n].shape for n in names]
        full_shapes = [(CONV_K, NDEV * w[n].shape[1]) if n in _CONV else w[n].shape for n in names]
        grads = {}
        for n, t in zip(names, _unpack(sum_slots(all_terms, f"sum_{tag}_grads"), full_shapes)):
            if n in _CONV:
                cols = w[n].shape[1]
                t = lax.dynamic_slice_in_dim(t, me * cols, cols, axis=1)
            grads[n] = t
        res = adamw(
            _pack([w[n] for n in names]), _pack([mom[n] for n in names]), _pack([var[n] for n in names]),
            _pack([grads[n] for n in names])[None], f"adamw_{tag}",
        )
        if after is not None:
            res, after = lax.optimization_barrier((res, after))
        for dst, packed_out in zip((out_g, out_d, out_m, out_v), res):
            for n, t in zip(names, _unpack(packed_out, shapes)):
                dst[n] = t
        return after

    core = jnp.reshape(cc, (1,)).astype(jnp.int32)
    chip = jnp.reshape(2 * cx + cy, (1,)).astype(jnp.int32)
    pair_flying, chip_flying = [], []

    def tie(value, after):
        if after is None:
            return value, None
        return lax.optimization_barrier((value, after))

    def to_chips(after):
        n, flying = pair_flying.pop()
        landed = _split_wait(_pair_copies, *flying, f"reduce_pair_wait_{n}")
        summed = pair_sum(flying[2], landed, core, f"pair_sum_{n}")
        *flying, token = _split_start(_chip_copies, summed, summed.shape, 3, f"reduce_chips_{n}")
        token, after = tie(token, after)
        chip_flying.append((n, flying + [token]))
        return after

    def update(after, behind=None):
        n, flying = chip_flying.pop(0)
        if behind is not None:
            flying[4], _ = lax.optimization_barrier((flying[4], behind))
        landed = _split_wait(_chip_copies, *flying, f"reduce_chips_wait_{n}")
        turn = (lambda t: t.T) if n == "l1_w_in" else (lambda t: t)
        res = adamw_reduced(turn(w[n]), turn(mom[n]), turn(var[n]), flying[2], landed, chip, f"adamw_{n}")
        res, after = tie(res, after)
        out_g[n], out_d[n], out_m[n], out_v[n] = [turn(t) for t in res]
        return after, res[0]

    def on_grad(n, term, after):
        if n is None:
            return to_chips(after)
        if n in _ROW_SHARDED or n == "l1_w_in":
            term = term.reshape(NDEV, -1, D)
        else:
            term = term.reshape(NDEV, D, -1)
        term = term.reshape((4, 2) + term.shape[1:])
        *flying, token = _split_start(_pair_copies, term, term.shape[:1] + term.shape[2:], 4, f"reduce_pair_{n}")
        token, after = tie(token, after)
        if len(chip_flying) == UPDATE_LAG:
            after, _ = update(after)
        if pair_flying:
            after = to_chips(after)
        pair_flying.append((n, flying + [token]))
        return after

    for n in _BIG:
        start_gather(n)
    dx, g = local_step(xs, target, wts, at, on_grad)
    done = None
    while chip_flying:
        _, done = update(None, behind=done)
    loss = loss_sum[0]

    update_small([_LAST_SMALL], all_gather_direct(_pack([g[_LAST_SMALL]]), "gather_last_grad"), "last")

    return (loss, dx[None], *[out_g[n] for n in _WEIGHTS], *[out_d[n] for n in _WEIGHTS],
            *[out_m[n] for n in _WEIGHTS], *[out_v[n] for n in _WEIGHTS])
```

```python
import functools
import math

import jax
import jax.numpy as jnp
from jax import lax
from jax.experimental import pallas as pl
from jax.experimental.pallas import tpu as pltpu
from jax.experimental.pallas import tpu_sc as plsc

F32 = jnp.float32
BF16 = jnp.bfloat16
HD = 128
EPS = 1e-6
CONV_K = 3
VMEM_LIMIT_BYTES = 48 << 20
NDEV = 8
MESH = pl.DeviceIdType.MESH

ADAM_LR = 0.001
ADAM_B1 = 0.9
ADAM_B2 = 0.999
ADAM_EPS = 1e-08
ADAM_WD = 0.01
ADAM_STEP = 10


def _cp(sem):
    return pltpu.CompilerParams(dimension_semantics=sem, vmem_limit_bytes=VMEM_LIMIT_BYTES)


def _pick(n, prefs):
    for p in prefs:
        if n % p == 0:
            return p
    return n


def _dot(a, b):
    return jnp.dot(a, b, preferred_element_type=F32)


def _dot_nt(a, b):
    return lax.dot_general(a, b, (((1,), (1,)), ((), ())), preferred_element_type=F32)


def _dot_tn(a, b):
    return lax.dot_general(a, b, (((0,), (0,)), ((), ())), preferred_element_type=F32)


def _split3(x):
    hi = x.astype(BF16)
    r = x - hi.astype(F32)
    mid = r.astype(BF16)
    lo = (r - mid.astype(F32)).astype(BF16)
    return hi, mid, lo


def _dot_ones_left(ones_bf16, x):
    hi, mid, lo = _split3(x)
    return _dot(ones_bf16, hi) + _dot(ones_bf16, mid) + _dot(ones_bf16, lo)


def _iota2(shape, axis):
    return lax.broadcasted_iota(jnp.int32, shape, axis)


def mm_nn(a, w2d, nb, name, out_dtype=BF16, res=None, tm=None, tn=None, tk=None, a_map=None, a_shape=None):
    M, K = a_shape or a.shape
    n = w2d.shape[1]
    assert w2d.shape[0] == nb * K or (nb == 1 and w2d.shape[0] > K)
    a_map = a_map or (lambda i, k: (i, k))
    tm = tm or _pick(M, (1024, 512, 256, 128))
    tn = tn or _pick(n, (1408, 1024, 768, 512, 256, 128))
    tk = tk or (K if K <= 2048 else _pick(K, (1408, 1024, 512, 256, 128)))
    nk, nt = K // tk, n // tn
    has_res = res is not None

    def body(*refs):
        if has_res:
            a_ref, w_ref, r_ref, o_ref = refs[:4]
        else:
            a_ref, w_ref, o_ref = refs[:3]
            r_ref = None
        part = _dot(a_ref[...], w_ref[...])

        def finish(acc):
            if r_ref is not None:
                acc = acc + r_ref[...].astype(F32)
            o_ref[...] = acc.astype(o_ref.dtype)

        if nk == 1:
            finish(part)
        else:
            acc_ref = refs[-1]
            k = pl.program_id(3)

            @pl.when(k == 0)
            def _():
                acc_ref[...] = part

            @pl.when(k > 0)
            def _():
                acc_ref[...] += part

            @pl.when(k == nk - 1)
            def _():
                finish(acc_ref[...])

    in_specs = [
        pl.BlockSpec((tm, tk), lambda i, j, t, k: a_map(i, k)),
        pl.BlockSpec((tk, tn), lambda i, j, t, k: (j * nk + k, t)),
    ]
    args = [a, w2d]
    out_spec = pl.BlockSpec((tm, tn), lambda i, j, t, k: (i, j * nt + t))
    if has_res:
        in_specs.append(out_spec)
        args.append(res)
    return pl.pallas_call(
        body,
        grid=(M // tm, nb, nt, nk),
        in_specs=in_specs,
        out_specs=out_spec,
        out_shape=jax.ShapeDtypeStruct((M, nb * n), out_dtype),
        scratch_shapes=[pltpu.VMEM((tm, tn), F32)] if nk > 1 else [],
        compiler_params=_cp(("parallel", "parallel", "parallel", "arbitrary")),
        name=name,
    )(*args)


def mm_nt(dy2d, w2d, nb, M, K, name, out_dtype=BF16, res=None, dy_maps=None, tm=None, tko=None, tn=None):
    n = w2d.shape[1]
    assert w2d.shape[0] == nb * K or (nb == 1 and w2d.shape[0] > K)
    tm = tm or _pick(M, (1024, 512, 256, 128))
    tko = tko or _pick(K, (1024, 512, 256, 128))
    tn = tn or _pick(n, (1408, 1024, 768, 512, 256, 128))
    nt, nko = n // tn, K // tko
    has_res = res is not None
    if dy_maps is None:
        dy_maps = [lambda i, j, t: (i, j * nt + t)]
    nd = len(dy_maps)
    td = tn // nd

    one_step = nb * nt == 1

    def body(*refs):
        d_refs, w_ref = refs[:nd], refs[nd]
        r_ref = refs[nd + 1] if has_res else None
        d = d_refs[0][...] if nd == 1 else jnp.concatenate([r[...] for r in d_refs], axis=1)
        part = _dot_nt(d, w_ref[...])
        if one_step:
            o_ref = refs[-1]
            if r_ref is not None:
                part = part + r_ref[...].astype(F32)
            o_ref[...] = part.astype(o_ref.dtype)
            return
        o_ref, acc_ref = refs[-2], refs[-1]
        j, t = pl.program_id(2), pl.program_id(3)
        first = jnp.logical_and(j == 0, t == 0)
        last = jnp.logical_and(j == nb - 1, t == nt - 1)

        @pl.when(first)
        def _():
            acc_ref[...] = part

        @pl.when(jnp.logical_not(first))
        def _():
            acc_ref[...] += part

        @pl.when(last)
        def _():
            acc = acc_ref[...]
            if r_ref is not None:
                acc = acc + r_ref[...].astype(F32)
            o_ref[...] = acc.astype(o_ref.dtype)

    in_specs = [pl.BlockSpec((tm, td), functools.partial(lambda f, i, ko, j, t: f(i, j, t), f)) for f in dy_maps]
    in_specs.append(pl.BlockSpec((tko, tn), lambda i, ko, j, t: (j * nko + ko, t)))
    args = [dy2d] * nd + [w2d]
    out_spec = pl.BlockSpec((tm, tko), lambda i, ko, j, t: (i, ko))
    if has_res:
        in_specs.append(out_spec)
        args.append(res)
    return pl.pallas_call(
        body,
        grid=(M // tm, nko, nb, nt),
        in_specs=in_specs,
        out_specs=out_spec,
        out_shape=jax.ShapeDtypeStruct((M, K), out_dtype),
        scratch_shapes=[] if one_step else [pltpu.VMEM((tm, tko), F32)],
        compiler_params=_cp(("parallel", "parallel", "arbitrary", "arbitrary")),
        name=name,
    )(*args)


def mm_tn(x, dy2d, nb, n, name, out_dtype=BF16, dy_maps=None, tko=None, tn=None, x_map=None, x_shape=None):
    S, K = x_shape or x.shape
    x_map = x_map or (lambda ko: (0, ko))
    tko = tko or _pick(K, (512, 256, 128))
    tn = tn or _pick(n, (1408, 1024, 768, 512, 256, 128))
    nt, nko = n // tn, K // tko
    if dy_maps is None:
        dy_maps = [lambda j, t: (0, j * nt + t)]
    nd = len(dy_maps)
    td = tn // nd

    def body(*refs):
        x_ref, d_refs, o_ref = refs[0], refs[1 : 1 + nd], refs[-1]
        d = d_refs[0][...] if nd == 1 else jnp.concatenate([r[...] for r in d_refs], axis=1)
        o_ref[...] = _dot_tn(x_ref[...], d).astype(o_ref.dtype)

    in_specs = [pl.BlockSpec((S, tko), lambda ko, j, t: x_map(ko))]
    in_specs += [pl.BlockSpec((S, td), functools.partial(lambda f, ko, j, t: f(j, t), f)) for f in dy_maps]
    return pl.pallas_call(
        body,
        grid=(nko, nb, nt),
        in_specs=in_specs,
        out_specs=pl.BlockSpec((tko, tn), lambda ko, j, t: (j * nko + ko, t)),
        out_shape=jax.ShapeDtypeStruct((nb * K, n), out_dtype),
        compiler_params=_cp(("parallel", "parallel", "parallel")),
        name=name,
    )(x, *([dy2d] * nd))


def rms_fwd(x, g, name):
    S, D = x.shape
    tm = _pick(S, (256, 128))

    def body(x_ref, g_ref, o_ref):
        xf = x_ref[...]
        r = lax.rsqrt(jnp.mean(xf * xf, axis=-1, keepdims=True) + EPS)
        o_ref[...] = (xf * r * g_ref[...]).astype(o_ref.dtype)

    return pl.pallas_call(
        body,
        grid=(S // tm,),
        in_specs=[pl.BlockSpec((tm, D), lambda i: (i, 0)), pl.BlockSpec((1, D), lambda i: (0, 0))],
        out_specs=pl.BlockSpec((tm, D), lambda i: (i, 0)),
        out_shape=jax.ShapeDtypeStruct((S, D), BF16),
        compiler_params=_cp(("parallel",)),
        name=name,
    )(x, g.reshape(1, D))


def rms_bwd(x, g, dh, dres, name):
    S, D = x.shape
    tm = _pick(S, (256, 128))

    def body(x_ref, g_ref, dh_ref, dr_ref, dx_ref, dxb_ref, dg_ref):
        i = pl.program_id(0)
        xf = x_ref[...]
        dh = dh_ref[...].astype(F32)
        r = lax.rsqrt(jnp.mean(xf * xf, axis=-1, keepdims=True) + EPS)
        gy = dh * g_ref[...]
        proj = jnp.mean(gy * xf, axis=-1, keepdims=True)
        dx = dr_ref[...] + r * gy - xf * (r * r * r * proj)
        dx_ref[...] = dx
        dxb_ref[...] = dx.astype(BF16)
        dg = jnp.sum(dh * (xf * r), axis=0, keepdims=True)

        @pl.when(i == 0)
        def _():
            dg_ref[...] = dg

        @pl.when(i > 0)
        def _():
            dg_ref[...] += dg

    row = pl.BlockSpec((tm, D), lambda i: (i, 0))
    vec = pl.BlockSpec((1, D), lambda i: (0, 0))
    return pl.pallas_call(
        body,
        grid=(S // tm,),
        in_specs=[row, vec, row, row],
        out_specs=[row, row, vec],
        out_shape=[jax.ShapeDtypeStruct((S, D), F32), jax.ShapeDtypeStruct((S, D), BF16), jax.ShapeDtypeStruct((1, D), F32)],
        compiler_params=_cp(("arbitrary",)),
        name=name,
    )(x, g.reshape(1, D), dh, dres)


def loss_head(x, g, target, name):
    S, D = x.shape
    tm = _pick(S, (256, 128))

    def body(x_ref, g_ref, t_ref, dx_ref, dxb_ref, dg_ref, loss_ref):
        i = pl.program_id(0)
        xf = x_ref[...]
        gg = g_ref[...]
        r = lax.rsqrt(jnp.mean(xf * xf, axis=-1, keepdims=True) + EPS)
        xh = xf * r
        err = xh * gg - t_ref[...]
        part = (0.5 / D) * jnp.sum(err * err)
        dy = err * (1.0 / D)
        gy = dy * gg
        proj = jnp.mean(gy * xf, axis=-1, keepdims=True)
        dx = r * gy - xf * (r * r * r * proj)
        dx_ref[...] = dx
        dxb_ref[...] = dx.astype(BF16)
        dg = jnp.sum(dy * xh, axis=0, keepdims=True)
        lossb = jnp.full(loss_ref.shape, part, F32)

        @pl.when(i == 0)
        def _():
            dg_ref[...] = dg
            loss_ref[...] = lossb

        @pl.when(i > 0)
        def _():
            dg_ref[...] += dg
            loss_ref[...] += lossb

    row = pl.BlockSpec((tm, D), lambda i: (i, 0))
    vec = pl.BlockSpec((1, D), lambda i: (0, 0))
    return pl.pallas_call(
        body,
        grid=(S // tm,),
        in_specs=[row, vec, row],
        out_specs=[row, row, vec, pl.BlockSpec((8, 128), lambda i: (0, 0))],
        out_shape=[
            jax.ShapeDtypeStruct((S, D), F32),
            jax.ShapeDtypeStruct((S, D), BF16),
            jax.ShapeDtypeStruct((1, D), F32),
            jax.ShapeDtypeStruct((8, 128), F32),
        ],
        compiler_params=_cp(("arbitrary",)),
        name=name,
    )(x, g.reshape(1, D), target)


def _shift_down(s, k):
    if k == 0:
        return s
    return jnp.where(_iota2(s.shape, 0) >= k, pltpu.roll(s, k, axis=0), 0.0)


def _shift_up(s, k):
    if k == 0:
        return s
    n = s.shape[0]
    return jnp.where(_iota2(s.shape, 0) < n - k, pltpu.roll(s, n - k, axis=0), 0.0)


def _conv(s, w):
    return w[0:1] * _shift_down(s, 2) + w[1:2] * _shift_down(s, 1) + w[2:3] * s


def _conv_t(d, w):
    return w[2:3] * d + w[1:2] * _shift_up(d, 1) + w[0:1] * _shift_up(d, 2)


def _conv_dw(d, s):
    return [jnp.sum(d * _shift_down(s, CONV_K - 1 - k), axis=0, keepdims=True) for k in range(CONV_K)]


def sc_fwd(p, convw, cat, W, name):
    S = p.shape[0]
    tc = _pick(W, (256, 128))
    nc = W // tc

    def body(gb_ref, gc_ref, hi_ref, w_ref, cat_ref, o_ref):
        s = gc_ref[...].astype(F32) * hi_ref[...].astype(F32)
        o_ref[...] = (gb_ref[...].astype(F32) * _conv(s, w_ref[...])).astype(o_ref.dtype)

    col = lambda part: pl.BlockSpec((S, tc), lambda c: (0, part * nc + c))
    return pl.pallas_call(
        body,
        grid=(nc,),
        in_specs=[col(3), col(4), col(5), pl.BlockSpec((CONV_K, tc), lambda c: (0, c)), pl.BlockSpec(memory_space=pl.ANY)],
        out_specs=col(1),
        out_shape=jax.ShapeDtypeStruct(cat.shape, cat.dtype),
        input_output_aliases={4: 0},
        compiler_params=_cp(("parallel",)),
        name=name,
    )(p, p, p, convw, cat)


def sc_bwd(p, convw, dcat, dp, W, name):
    S = p.shape[0]
    tc = _pick(W, (256, 128))
    nc = W // tc

    def body(gb_ref, gc_ref, hi_ref, w_ref, do_ref, dp_in_ref, dp_ref, dw_ref):
        gb = gb_ref[...].astype(F32)
        gc = gc_ref[...].astype(F32)
        hi = hi_ref[...].astype(F32)
        w = w_ref[...]
        do = do_ref[...].astype(F32)
        s = gc * hi
        dcs = do * gb
        ds = _conv_t(dcs, w)
        dp_ref[0] = (do * _conv(s, w)).astype(dp_ref.dtype)
        dp_ref[1] = (ds * hi).astype(dp_ref.dtype)
        dp_ref[2] = (ds * gc).astype(dp_ref.dtype)
        for k, row in enumerate(_conv_dw(dcs, s)):
            dw_ref[k : k + 1, :] = row

    col = lambda part: pl.BlockSpec((S, tc), lambda c: (0, part * nc + c))
    return pl.pallas_call(
        body,
        grid=(nc,),
        in_specs=[
            col(3), col(4), col(5),
            pl.BlockSpec((CONV_K, tc), lambda c: (0, c)),
            pl.BlockSpec((S, tc), lambda c: (0, nc + c)),
            pl.BlockSpec(memory_space=pl.ANY),
        ],
        out_specs=[pl.BlockSpec((3, S, tc), lambda c: (1, 0, c)), pl.BlockSpec((CONV_K, tc), lambda c: (0, c))],
        out_shape=[jax.ShapeDtypeStruct(dp.shape, dp.dtype), jax.ShapeDtypeStruct((CONV_K, W), F32)],
        input_output_aliases={5: 0},
        compiler_params=_cp(("parallel",)),
        name=name,
    )(p, p, p, convw, dcat, dp)


def _silu_parts(a):
    sig = 1.0 / (1.0 + jnp.exp(-a))
    return a * sig, sig


def ffn_act_fwd(u, convw, F, name):
    S = u.shape[0]
    tc = _pick(F, (256, 128))
    nc = F // tc

    def body(ug_ref, uu_ref, wg_ref, wu_ref, o_ref):
        ag = _conv(ug_ref[...].astype(F32), wg_ref[...])
        au = _conv(uu_ref[...].astype(F32), wu_ref[...])
        o_ref[...] = (_silu_parts(ag)[0] * au).astype(o_ref.dtype)

    col = lambda half: pl.BlockSpec((S, tc), lambda c: (0, half * nc + c))
    wcol = lambda half: pl.BlockSpec((CONV_K, tc), lambda c: (0, half * nc + c))
    return pl.pallas_call(
        body,
        grid=(nc,),
        in_specs=[col(0), col(1), wcol(0), wcol(1)],
        out_specs=pl.BlockSpec((S, tc), lambda c: (0, c)),
        out_shape=jax.ShapeDtypeStruct((S, F), BF16),
        compiler_params=_cp(("parallel",)),
        name=name,
    )(u, u, convw, convw)


def ffn_act_bwd(u, convw, dact, F, name):
    S = u.shape[0]
    tc = _pick(F, (256, 128))
    nc = F // tc

    def body(ug_ref, uu_ref, wg_ref, wu_ref, da_ref, du_ref, dw_ref):
        ug = ug_ref[...].astype(F32)
        uu = uu_ref[...].astype(F32)
        wg = wg_ref[...]
        wu = wu_ref[...]
        da = da_ref[...].astype(F32)
        ag = _conv(ug, wg)
        au = _conv(uu, wu)
        sl, sig = _silu_parts(ag)
        dag = da * au * (sig * (1.0 + ag * (1.0 - sig)))
        dau = da * sl
        du_ref[0] = _conv_t(dag, wg).astype(du_ref.dtype)
        du_ref[1] = _conv_t(dau, wu).astype(du_ref.dtype)
        for k, (rg, ru) in enumerate(zip(_conv_dw(dag, ug), _conv_dw(dau, uu))):
            dw_ref[0, k : k + 1, :] = rg
            dw_ref[1, k : k + 1, :] = ru

    col = lambda half: pl.BlockSpec((S, tc), lambda c: (0, half * nc + c))
    wcol = lambda half: pl.BlockSpec((CONV_K, tc), lambda c: (0, half * nc + c))
    return pl.pallas_call(
        body,
        grid=(nc,),
        in_specs=[col(0), col(1), wcol(0), wcol(1), pl.BlockSpec((S, tc), lambda c: (0, c))],
        out_specs=[pl.BlockSpec((2, S, tc), lambda c: (0, 0, c)), pl.BlockSpec((2, CONV_K, tc), lambda c: (0, 0, c))],
        out_shape=[jax.ShapeDtypeStruct((2, S, F), BF16), jax.ShapeDtypeStruct((2, CONV_K, F), F32)],
        compiler_params=_cp(("parallel",)),
        name=name,
    )(u, u, convw, convw, dact)


def _softplus(z):
    return jnp.maximum(z, 0.0) + jnp.log(1.0 + jnp.exp(-jnp.abs(z)))


def _key_strip(S):
    return _pick(S, (512, 256, 128))


def _query_rows(S):
    return _pick(S, (512, 256, 128))


def _split2(x):
    hi = x.astype(BF16)
    return hi, (x - hi.astype(F32)).astype(BF16)


def _block_sums(x, ones_bf16):
    hi, lo = _split2(x)
    return [
        _dot(hi[:, b * HD : (b + 1) * HD], ones_bf16) + _dot(lo[:, b * HD : (b + 1) * HD], ones_bf16)
        for b in range(x.shape[1] // HD)
    ]


def _strip_mask(shape, row0, off, strict):
    cols, rows = _iota2(shape, 1) + off, _iota2(shape, 0) + row0
    return cols < rows if strict else cols <= rows


def _sb_strip(q, ks, row0, off, run, su):
    z = _dot_nt(q, ks) * (HD ** -0.5)
    mask = _strip_mask(z.shape, row0, off, True)
    sp = _softplus(z)
    l = jnp.where(mask, -sp, 0.0)
    within = _block_sums(l, su)
    later = [None] * len(within)
    for b in reversed(range(len(within))):
        later[b] = within[b] + run
        run = run + jnp.sum(l[:, b * HD : (b + 1) * HD], axis=1, keepdims=True)
    a = jnp.where(mask, jnp.exp(z - sp + jnp.concatenate(later, axis=1)), 0.0)
    return z, mask, a, run


def sb_fwd(p, W, name):
    S = p.shape[0]
    TQ, TK = _query_rows(S), _key_strip(S)
    nh, nq = W // HD, S // TQ

    def body(q_ref, k_ref, v_ref, o_ref):
        i = pl.program_id(1)
        q = q_ref[...]
        su = (_iota2((HD, HD), 0) > _iota2((HD, HD), 1)).astype(BF16)
        last = (i * TQ + TQ - 1) // TK

        def step(gg, carry):
            acc, run = carry
            off = pl.multiple_of((last - gg) * TK, TK)
            _, _, a, run = _sb_strip(q, k_ref[pl.ds(off, TK), :], i * TQ, off, run, su)
            return acc + _dot(a.astype(BF16), v_ref[pl.ds(off, TK), :]), run

        acc, _ = lax.fori_loop(0, last + 1, step, (jnp.zeros((TQ, HD), F32), jnp.zeros((TQ, 1), F32)))
        o_ref[...] = acc.astype(o_ref.dtype)

    return pl.pallas_call(
        body,
        grid=(nh, nq),
        in_specs=[
            pl.BlockSpec((TQ, HD), lambda h, i: (i, h)),
            pl.BlockSpec((S, HD), lambda h, i: (0, nh + h)),
            pl.BlockSpec((S, HD), lambda h, i: (0, 2 * nh + h)),
        ],
        out_specs=pl.BlockSpec((TQ, HD), lambda h, i: (i, h)),
        out_shape=jax.ShapeDtypeStruct((S, 2 * W), BF16),
        compiler_params=_cp(("parallel", "arbitrary")),
        name=name,
    )(p, p, p)


def sb_bwd(p, dcat, W, name):
    S = p.shape[0]
    TQ, TK = _query_rows(S), _key_strip(S)
    nh, nq = W // HD, S // TQ
    scale = HD ** -0.5

    def body(q_ref, k_ref, v_ref, do_ref, dp_ref, dk_acc, dv_acc, e_scr, z_scr):
        i = pl.program_id(1)
        q = q_ref[...]
        do = do_ref[...]
        su = (_iota2((HD, HD), 0) > _iota2((HD, HD), 1)).astype(BF16)
        sl = (_iota2((HD, HD), 0) < _iota2((HD, HD), 1)).astype(BF16)
        last = (i * TQ + TQ - 1) // TK

        @pl.when(i == 0)
        def _():
            dk_acc[...] = jnp.zeros_like(dk_acc)
            dv_acc[...] = jnp.zeros_like(dv_acc)

        def pass_a(gg, run):
            g = last - gg
            off = pl.multiple_of(g * TK, TK)
            z, _, a, run = _sb_strip(q, k_ref[pl.ds(off, TK), :], i * TQ, off, run, su)
            e_scr[g] = a * _dot_nt(do, v_ref[pl.ds(off, TK), :])
            z_scr[g] = z
            dv_acc[pl.ds(off, TK), :] += _dot_tn(a.astype(BF16), do)
            return run

        lax.fori_loop(0, last + 1, pass_a, jnp.zeros((TQ, 1), F32))

        def pass_b(g, carry):
            dq, run_e = carry
            off = pl.multiple_of(g * TK, TK)
            e = e_scr[g]
            z = z_scr[g]
            mask = _strip_mask(z.shape, i * TQ, off, True)
            within = _block_sums(e, sl)
            before = []
            for b in range(len(within)):
                before.append(within[b] + run_e)
                run_e = run_e + jnp.sum(e[:, b * HD : (b + 1) * HD], axis=1, keepdims=True)
            sig = 1.0 / (1.0 + jnp.exp(-z))
            dz = jnp.where(mask, e * (1.0 - sig) - jnp.concatenate(before, axis=1) * sig, 0.0)
            dz = (dz * scale).astype(BF16)
            dq = dq + _dot(dz, k_ref[pl.ds(off, TK), :])
            dk_acc[pl.ds(off, TK), :] += _dot_tn(dz, q)
            return dq, run_e

        dq, _ = lax.fori_loop(0, last + 1, pass_b, (jnp.zeros((TQ, HD), F32), jnp.zeros((TQ, 1), F32)))
        dp_ref[0, pl.ds(pl.multiple_of(i * TQ, TQ), TQ), :] = dq.astype(dp_ref.dtype)

        @pl.when(i == nq - 1)
        def _():
            dp_ref[1] = dk_acc[...].astype(dp_ref.dtype)
            dp_ref[2] = dv_acc[...].astype(dp_ref.dtype)

    return pl.pallas_call(
        body,
        grid=(nh, nq),
        in_specs=[
            pl.BlockSpec((TQ, HD), lambda h, i: (i, h)),
            pl.BlockSpec((S, HD), lambda h, i: (0, nh + h)),
            pl.BlockSpec((S, HD), lambda h, i: (0, 2 * nh + h)),
            pl.BlockSpec((TQ, HD), lambda h, i: (i, h)),
        ],
        out_specs=pl.BlockSpec((3, S, HD), lambda h, i: (0, 0, h)),
        out_shape=jax.ShapeDtypeStruct((6, S, W), BF16),
        scratch_shapes=[
            pltpu.VMEM((S, HD), F32),
            pltpu.VMEM((S, HD), F32),
            pltpu.VMEM((S // TK, TQ, TK), F32),
            pltpu.VMEM((S // TK, TQ, TK), F32),
        ],
        compiler_params=_cp(("parallel", "arbitrary")),
        name=name,
    )(p, p, p, dcat)


def fox_gate_fwd(f, b, name):
    S = f.shape[0]
    nq = S // HD

    def body(f_ref, b_ref, c_ref, run):
        i = pl.program_id(0)

        @pl.when(i == 0)
        def _():
            run[...] = jnp.zeros_like(run)

        lf = -_softplus(-(f_ref[...] + b_ref[...]))
        tri = (_iota2((HD, HD), 0) >= _iota2((HD, HD), 1)).astype(BF16)
        c_ref[...] = _dot_ones_left(tri, lf) + run[...]
        run[...] += jnp.sum(lf, axis=0, keepdims=True)

    return pl.pallas_call(
        body,
        grid=(nq,),
        in_specs=[pl.BlockSpec((HD, 128), lambda i: (i, 0)), pl.BlockSpec((1, 128), lambda i: (0, 0))],
        out_specs=pl.BlockSpec((HD, 128), lambda i: (i, 0)),
        out_shape=jax.ShapeDtypeStruct((S, 128), F32),
        scratch_shapes=[pltpu.VMEM((1, 128), F32)],
        compiler_params=_cp(("arbitrary",)),
        name=name,
    )(f, b)


def fox_gate_bwd(f, b, dc, name):
    S = f.shape[0]
    nq = S // HD

    def body(f_ref, b_ref, dc_ref, df_ref, db_ref, run):
        i = pl.program_id(0)

        @pl.when(i == 0)
        def _():
            run[...] = jnp.zeros_like(run)

        dc = dc_ref[...]
        tri = (_iota2((HD, HD), 0) <= _iota2((HD, HD), 1)).astype(BF16)
        dlf = _dot_ones_left(tri, dc) + run[...]
        run[...] += jnp.sum(dc, axis=0, keepdims=True)
        x = f_ref[...] + b_ref[...]
        df = dlf * (1.0 / (1.0 + jnp.exp(x)))
        df_ref[...] = df
        db = jnp.sum(df, axis=0, keepdims=True)

        @pl.when(i == 0)
        def _():
            db_ref[...] = db

        @pl.when(i > 0)
        def _():
            db_ref[...] += db

    rev = pl.BlockSpec((HD, 128), lambda i: (nq - 1 - i, 0))
    vec = pl.BlockSpec((1, 128), lambda i: (0, 0))
    return pl.pallas_call(
        body,
        grid=(nq,),
        in_specs=[rev, vec, rev],
        out_specs=[rev, vec],
        out_shape=[jax.ShapeDtypeStruct((S, 128), F32), jax.ShapeDtypeStruct((1, 128), F32)],
        scratch_shapes=[pltpu.VMEM((1, 128), F32)],
        compiler_params=_cp(("arbitrary",)),
        name=name,
    )(f, b, dc)


def _fox_logits(q, ks, ct, cs, row0, off):
    s = _dot_nt(q, ks) * (HD ** -0.5) + (ct - cs)
    mask = _strip_mask(s.shape, row0, off, False)
    return jnp.where(mask, s, -1e30), mask


def fox_fwd(p, ccol, crow, cat, W, name):
    S = p.shape[0]
    TQ, TK = _query_rows(S), _key_strip(S)
    nh, nq = W // HD, S // TQ

    def body(q_ref, k_ref, v_ref, cc_ref, cr_ref, cat_ref, o_ref, lse_ref):
        i = pl.program_id(1)
        q = q_ref[...]
        ct = cc_ref[0]

        def step(g, carry):
            m, l, acc = carry
            off = pl.multiple_of(g * TK, TK)
            s, _ = _fox_logits(q, k_ref[pl.ds(off, TK), :], ct, cr_ref[0, pl.ds(g, 1), :], i * TQ, off)
            m_new = jnp.maximum(m, jnp.max(s, axis=1, keepdims=True))
            alpha = jnp.exp(m - m_new)
            pr = jnp.exp(s - m_new)
            l = alpha * l + jnp.sum(pr, axis=1, keepdims=True)
            acc = alpha * acc + _dot(pr.astype(BF16), v_ref[pl.ds(off, TK), :])
            return m_new, l, acc

        init = (jnp.full((TQ, 1), -1e30, F32), jnp.zeros((TQ, 1), F32), jnp.zeros((TQ, HD), F32))
        m, l, acc = lax.fori_loop(0, (i * TQ + TQ - 1) // TK + 1, step, init)
        o_ref[...] = (acc / l).astype(o_ref.dtype)
        lse_ref[0] = m + jnp.log(l)

    return pl.pallas_call(
        body,
        grid=(nh, nq),
        in_specs=[
            pl.BlockSpec((TQ, HD), lambda h, i: (i, 2 * nh + h)),
            pl.BlockSpec((S, HD), lambda h, i: (0, 3 * nh + h)),
            pl.BlockSpec((S, HD), lambda h, i: (0, 4 * nh + h)),
            pl.BlockSpec((1, TQ, 1), lambda h, i: (h, i, 0)),
            pl.BlockSpec((1, S // TK, TK), lambda h, i: (h, 0, 0)),
            pl.BlockSpec(memory_space=pl.ANY),
        ],
        out_specs=[pl.BlockSpec((TQ, HD), lambda h, i: (i, nh + h)), pl.BlockSpec((1, TQ, 1), lambda h, i: (h, i, 0))],
        out_shape=[jax.ShapeDtypeStruct(cat.shape, cat.dtype), jax.ShapeDtypeStruct((nh, S, 1), F32)],
        input_output_aliases={5: 0},
        compiler_params=_cp(("parallel", "arbitrary")),
        name=name,
    )(p, p, p, ccol, crow, cat)


def fox_bwd(p, ccol, crow, cat, lse, dcat, dp, W, name):
    S = p.shape[0]
    TQ, TK = _query_rows(S), _key_strip(S)
    nh, nq = W // HD, S // TQ
    scale = HD ** -0.5

    def body(q_ref, k_ref, v_ref, cc_ref, cr_ref, o_ref, lse_ref, do_ref, dp_in_ref, dp_ref, dcs_ref, dct_ref, dk_acc, dv_acc):
        i = pl.program_id(1)
        q = q_ref[...]
        do = do_ref[...]
        ct = cc_ref[0]
        lse_i = lse_ref[0]
        delta = jnp.sum(do.astype(F32) * o_ref[...].astype(F32), axis=1, keepdims=True)

        @pl.when(i == 0)
        def _():
            dk_acc[...] = jnp.zeros_like(dk_acc)
            dv_acc[...] = jnp.zeros_like(dv_acc)
            dcs_ref[...] = jnp.zeros_like(dcs_ref)

        def step(g, carry):
            dq, dct = carry
            off = pl.multiple_of(g * TK, TK)
            ks = k_ref[pl.ds(off, TK), :]
            s, mask = _fox_logits(q, ks, ct, cr_ref[0, pl.ds(g, 1), :], i * TQ, off)
            pr = jnp.where(mask, jnp.exp(s - lse_i), 0.0)
            ds = pr * (_dot_nt(do, v_ref[pl.ds(off, TK), :]) - delta)
            dv_acc[pl.ds(off, TK), :] += _dot_tn(pr.astype(BF16), do)
            dsb = (ds * scale).astype(BF16)
            dk_acc[pl.ds(off, TK), :] += _dot_tn(dsb, q)
            dcs_ref[0, pl.ds(g, 1), :] += jnp.sum(ds, axis=0, keepdims=True)
            return dq + _dot(dsb, ks), dct + jnp.sum(ds, axis=1, keepdims=True)

        dq, dct = lax.fori_loop(0, (i * TQ + TQ - 1) // TK + 1, step, (jnp.zeros((TQ, HD), F32), jnp.zeros((TQ, 1), F32)))
        dp_ref[0, pl.ds(pl.multiple_of(i * TQ, TQ), TQ), :] = dq.astype(dp_ref.dtype)
        dct_ref[0] = dct

        @pl.when(i == nq - 1)
        def _():
            dp_ref[1] = dk_acc[...].astype(dp_ref.dtype)
            dp_ref[2] = dv_acc[...].astype(dp_ref.dtype)

    return pl.pallas_call(
        body,
        grid=(nh, nq),
        in_specs=[
            pl.BlockSpec((TQ, HD), lambda h, i: (i, 2 * nh + h)),
            pl.BlockSpec((S, HD), lambda h, i: (0, 3 * nh + h)),
            pl.BlockSpec((S, HD), lambda h, i: (0, 4 * nh + h)),
            pl.BlockSpec((1, TQ, 1), lambda h, i: (h, i, 0)),
            pl.BlockSpec((1, S // TK, TK), lambda h, i: (h, 0, 0)),
            pl.BlockSpec((TQ, HD), lambda h, i: (i, nh + h)),
            pl.BlockSpec((1, TQ, 1), lambda h, i: (h, i, 0)),
            pl.BlockSpec((TQ, HD), lambda h, i: (i, nh + h)),
            pl.BlockSpec(memory_space=pl.ANY),
        ],
        out_specs=[
            pl.BlockSpec((3, S, HD), lambda h, i: (1, 0, h)),
            pl.BlockSpec((1, S // TK, TK), lambda h, i: (h, 0, 0)),
            pl.BlockSpec((1, TQ, 1), lambda h, i: (h, i, 0)),
        ],
        out_shape=[
            jax.ShapeDtypeStruct(dp.shape, dp.dtype),
            jax.ShapeDtypeStruct((nh, S // TK, TK), F32),
            jax.ShapeDtypeStruct((nh, S, 1), F32),
        ],
        input_output_aliases={8: 0},
        scratch_shapes=[pltpu.VMEM((S, HD), F32), pltpu.VMEM((S, HD), F32)],
        compiler_params=_cp(("parallel", "arbitrary")),
        name=name,
    )(p, p, p, ccol, crow, cat, lse, dcat, dp)


_GELU_K = math.sqrt(2.0 / math.pi)
_GELU_C = 0.044715


def _gelu(x):
    return 0.5 * x * (1.0 + jnp.tanh(_GELU_K * (x + _GELU_C * x * x * x)))


def _gelu_grad(x):
    t = jnp.tanh(_GELU_K * (x + _GELU_C * x * x * x))
    return 0.5 * (1.0 + t) + 0.5 * x * (1.0 - t * t) * (_GELU_K * (1.0 + 3.0 * _GELU_C * x * x))


def _layernorm_parts(gv):
    xc = gv - jnp.mean(gv, axis=-1, keepdims=True)
    r = lax.rsqrt(jnp.mean(xc * xc, axis=-1, keepdims=True) + EPS)
    return xc * r, r


def sg_fwd(p, sg_w, sg_bt, sg_g, W, name):
    S = p.shape[0]
    G, nq = W // HD, S // HD

    def body(u_ref, v_ref, w_ref, bt_ref, g_ref, o_ref):
        xh, _ = _layernorm_parts(_gelu(v_ref[...].astype(F32)))
        vn = (xh * g_ref[...]).astype(BF16)
        tri = _iota2((HD, HD), 0) >= _iota2((HD, HD), 1)
        for gi in range(G):
            cols = slice(gi * HD, (gi + 1) * HD)
            wt = jnp.where(tri, w_ref[gi], 0.0).astype(BF16)
            mixed = _dot(wt, vn[:, cols]) + bt_ref[:, gi : gi + 1]
            o_ref[:, cols] = (_gelu(u_ref[:, cols].astype(F32)) * mixed).astype(o_ref.dtype)

    return pl.pallas_call(
        body,
        grid=(nq,),
        in_specs=[
            pl.BlockSpec((HD, W), lambda i: (i, 0)),
            pl.BlockSpec((HD, W), lambda i: (i, 1)),
            pl.BlockSpec((G, HD, HD), lambda i: (0, 0, 0)),
            pl.BlockSpec((HD, G), lambda i: (0, 0)),
            pl.BlockSpec((1, W), lambda i: (0, 0)),
        ],
        out_specs=pl.BlockSpec((HD, W), lambda i: (i, 0)),
        out_shape=jax.ShapeDtypeStruct((S, 2 * W), BF16),
        compiler_params=_cp(("parallel",)),
        name=name,
    )(p, p, sg_w, sg_bt, sg_g.reshape(1, W))


def sg_bwd(p, sg_w, sg_bt, sg_g, dcat, W, name):
    S = p.shape[0]
    G, nq = W // HD, S // HD

    def body(u_ref, v_ref, w_ref, bt_ref, g_ref, do_ref, dp_ref, dw_ref, dbt_ref, dg_ref, dvn_scr):
        i = pl.program_id(0)

        @pl.when(i == 0)
        def _():
            dw_ref[...] = jnp.zeros_like(dw_ref)
            dbt_ref[...] = jnp.zeros_like(dbt_ref)
            dg_ref[...] = jnp.zeros_like(dg_ref)

        v = v_ref[...].astype(F32)
        xh, r = _layernorm_parts(_gelu(v))
        gg = g_ref[...]
        vn = (xh * gg).astype(BF16)
        tri = _iota2((HD, HD), 0) >= _iota2((HD, HD), 1)
        for gi in range(G):
            cols = slice(gi * HD, (gi + 1) * HD)
            wt = jnp.where(tri, w_ref[gi], 0.0).astype(BF16)
            mixed = _dot(wt, vn[:, cols]) + bt_ref[:, gi : gi + 1]
            u = u_ref[:, cols].astype(F32)
            do = do_ref[:, cols].astype(F32)
            dp_ref[0, :, cols] = (do * mixed * _gelu_grad(u)).astype(dp_ref.dtype)
            dmix = do * _gelu(u)
            dmb = dmix.astype(BF16)
            dw_ref[gi] += jnp.where(tri, _dot_nt(dmb, vn[:, cols]), 0.0)
            dbt_ref[:, gi : gi + 1] += jnp.sum(dmix, axis=1, keepdims=True)
            dvn_scr[:, cols] = _dot_tn(wt, dmb)
        dvn = dvn_scr[...]
        dg_ref[...] += jnp.sum(dvn * xh, axis=0, keepdims=True)
        dxh = dvn * gg
        dgv = r * (dxh - jnp.mean(dxh, axis=-1, keepdims=True) - xh * jnp.mean(dxh * xh, axis=-1, keepdims=True))
        dp_ref[1] = (dgv * _gelu_grad(v)).astype(dp_ref.dtype)

    return pl.pallas_call(
        body,
        grid=(nq,),
        in_specs=[
            pl.BlockSpec((HD, W), lambda i: (i, 0)),
            pl.BlockSpec((HD, W), lambda i: (i, 1)),
            pl.BlockSpec((G, HD, HD), lambda i: (0, 0, 0)),
            pl.BlockSpec((HD, G), lambda i: (0, 0)),
            pl.BlockSpec((1, W), lambda i: (0, 0)),
            pl.BlockSpec((HD, W), lambda i: (i, 0)),
        ],
        out_specs=[
            pl.BlockSpec((2, HD, W), lambda i: (0, i, 0)),
            pl.BlockSpec((G, HD, HD), lambda i: (0, 0, 0)),
            pl.BlockSpec((HD, G), lambda i: (0, 0)),
            pl.BlockSpec((1, W), lambda i: (0, 0)),
        ],
        out_shape=[
            jax.ShapeDtypeStruct((6, S, W), BF16),
            jax.ShapeDtypeStruct((G, HD, HD), F32),
            jax.ShapeDtypeStruct((HD, G), F32),
            jax.ShapeDtypeStruct((1, W), F32),
        ],
        scratch_shapes=[pltpu.VMEM((HD, W), F32)],
        compiler_params=_cp(("arbitrary",)),
        name=name,
    )(p, p, sg_w, sg_bt, sg_g.reshape(1, W), dcat)


def local_step(x, target, wts, at, on_grad):
    S, D = x.shape
    W = D // 2
    nb, F = wts["nb"], wts["F"]
    g = {}

    def ffn_fwd(xin, l):
        h = rms_fwd(xin, wts[f"{l}_ffn_norm_g"], f"{l}_ffn_rms")
        u = mm_nn(h, wts[f"{l}_ffn_up"], nb, f"{l}_ffn_up_mm")
        act = ffn_act_fwd(u, wts[f"{l}_ffn_conv_w"], F, f"{l}_ffn_act")
        half_tile = _pick(S, (512, 256, 128))
        xout = mm_nn(act, wts[f"{l}_ffn_down"], 1, f"{l}_ffn_down_mm", out_dtype=F32, res=xin,
                     tm=half_tile, tn=_pick(D, (512, 256, 128)), tk=F)
        return xout, (xin, h, u, act)

    def ffn_bwd(dxout, dxoutb, saved, l):
        xin, h, u, act = saved
        dact = mm_nt(dxoutb, wts[f"{l}_ffn_down"], 1, S, F, f"{l}_ffn_down_dx", tko=_pick(F, (512, 256, 128)), tn=D)
        dact = on_grad(f"{l}_ffn_down", mm_tn(act, dxoutb, 1, D, f"{l}_ffn_down_dw", tn=D), dact)
        du, dcw = ffn_act_bwd(u, wts[f"{l}_ffn_conv_w"], dact, F, f"{l}_ffn_act_bwd")
        g[f"{l}_ffn_conv_w"] = jnp.concatenate([dcw[0], dcw[1]], axis=1)
        du2 = du.reshape(2 * S, F)
        n = wts[f"{l}_ffn_up"].shape[1]
        tn = _pick(n, (1408, 1024, 768, 512, 256, 128))
        per_half = F // tn
        nt = n // tn

        def up_block(i, j, t):
            vb = j * nt + t
            return vb // per_half, vb % per_half

        tm = _pick(S, (1024, 512, 256, 128))

        def nt_map(i, j, t):
            half, cb = up_block(i, j, t)
            return (half * (S // tm) + i, cb)

        def tn_map(j, t):
            half, cb = up_block(0, j, t)
            return (half, cb)

        dh = mm_nt(du2, wts[f"{l}_ffn_up"], nb, S, D, f"{l}_ffn_up_dx", dy_maps=[nt_map], tm=tm, tko=D, tn=tn)
        dh = on_grad(f"{l}_ffn_up", mm_tn(h, du2, nb, n, f"{l}_ffn_up_dw", dy_maps=[tn_map], tn=tn), dh)
        dxin, dxinb, dg = rms_bwd(xin, wts[f"{l}_ffn_norm_g"], dh, dxout, f"{l}_ffn_rms_bwd")
        g[f"{l}_ffn_norm_g"] = dg
        return dxin, dxinb

    h0 = rms_fwd(x, wts["l0_mix_norm_g"], "l0_mix_rms")
    p0 = mm_nn(h0, wts["l0_w_in"], nb, "l0_w_in_mm")
    cat0 = sb_fwd(p0, W, "l0_sb_fwd")
    cat0 = sc_fwd(p0, wts["l0_sc_conv_w"], cat0, W, "l0_sc_fwd")
    x1 = mm_nn(cat0, wts["l0_w_out"], 1, "l0_w_out_mm", out_dtype=F32, res=x, tm=S, tn=_pick(D, (512, 256, 128)))
    x2, ffn0_saved = ffn_fwd(x1, "l0")

    x2 = at("l1_w_in", x2, None)
    nh = W // HD
    h2 = rms_fwd(x2, wts["l1_mix_norm_g"], "l1_mix_rms")
    p1 = mm_nt(h2, wts["l1_w_in_t"], 1, S, 5 * W, "l1_w_in_mm", tn=D)
    f = mm_nt(h2, wts["l1_w_f_t"], 1, S, 128, "l1_w_f_mm", out_dtype=F32, tn=D)
    bf = jnp.zeros((1, 128), F32).at[0, :nh].set(wts["l1_fox_b_f"])
    c = fox_gate_fwd(f, bf, "l1_fox_gate")
    c_heads = c[:, :nh].T
    ccol = c_heads[:, :, None]
    crow = c_heads.reshape(nh, S // _key_strip(S), _key_strip(S))
    sg_bt = wts["l1_sg_b"].T
    cat1 = sg_fwd(p1, wts["l1_sg_w"], sg_bt, wts["l1_sg_norm_g"], W, "l1_sg_fwd")
    cat1, lse = fox_fwd(p1, ccol, crow, cat1, W, "l1_fox_fwd")
    x3 = mm_nn(cat1, wts["l1_w_out"], 1, "l1_w_out_mm", out_dtype=F32, res=x2, tm=S, tn=_pick(D, (512, 256, 128)))
    x4, ffn1_saved = ffn_fwd(x3, "l1")

    dx4, dx4b, dgf, loss = loss_head(x4, wts["final_norm_g"], target, "loss_head")
    dx4b = at("loss", dx4b, loss)
    g["final_norm_g"] = dgf

    dx3, dx3b = ffn_bwd(dx4, dx4b, ffn1_saved, "l1")
    dcat1 = mm_nt(dx3b, wts["l1_w_out"], 1, S, D, "l1_w_out_dx", tn=D)
    dcat1 = on_grad("l1_w_out", mm_tn(cat1, dx3b, 1, D, "l1_w_out_dw", tn=D), dcat1)
    dp1, dsgw, dsgbt, dsgg = sg_bwd(p1, wts["l1_sg_w"], sg_bt, wts["l1_sg_norm_g"], dcat1, W, "l1_sg_bwd")
    dp1, dcs, dct = fox_bwd(p1, ccol, crow, cat1, lse, dcat1, dp1, W, "l1_fox_bwd")
    g["l1_sg_w"], g["l1_sg_b"], g["l1_sg_norm_g"] = dsgw, dsgbt.T, dsgg
    dc = jnp.zeros((S, 128), F32).at[:, :nh].set((dct[:, :, 0] - dcs.reshape(nh, S)).T)
    df, dbf = fox_gate_bwd(f, bf, dc, "l1_fox_gate_bwd")
    g["l1_fox_b_f"] = dbf[0, :nh]
    dfb = df.astype(BF16)
    tk1 = _pick(W, (1024, 512, 256, 128))
    tx1 = _pick(W, (512, 256, 128))
    tm1 = _pick(S, (1024, 512, 256, 128))
    part_of = lambda pt: pt + pt // 2 - pt // 4

    def a_map1(i, k):
        return (part_of(k // (W // tk1)) * (S // tm1) + i, k % (W // tk1))

    def x_map1(ko):
        return (part_of(ko // (W // tx1)), ko % (W // tx1))

    dp1_2d = dp1.reshape(6 * S, W)
    dw_main = mm_tn(dp1_2d, h2, 1, D, "l1_w_in_dw", tko=tx1, tn=D, x_map=x_map1, x_shape=(S, 5 * W))
    dw_f = mm_tn(dfb, h2, 1, D, "l1_w_f_dw", tn=D)
    dh2 = mm_nn(dfb, wts["l1_w_f_t"], 1, "l1_w_f_dx", out_dtype=F32)
    dh2 = mm_nn(dp1_2d, wts["l1_w_in_t"], 1, "l1_w_in_dx", res=dh2, tm=tm1, tk=tk1, a_map=a_map1, a_shape=(S, 5 * W))
    dh2 = on_grad("l1_w_in", jnp.concatenate([dw_main, dw_f[:nh]], axis=0), dh2)
    dx2, dx2b, dg = rms_bwd(x2, wts["l1_mix_norm_g"], dh2, dx3, "l1_mix_rms_bwd")
    g["l1_mix_norm_g"] = dg

    dx1, dx1b = ffn_bwd(dx2, dx2b, ffn0_saved, "l0")
    dcat0 = mm_nt(dx1b, wts["l0_w_out"], 1, S, D, "l0_w_out_dx", tn=D)
    dcat0 = on_grad("l0_w_out", mm_tn(cat0, dx1b, 1, D, "l0_w_out_dw", tn=D), dcat0)
    dp0 = sb_bwd(p0, dcat0, W, "l0_sb_bwd")
    dp0, dscw = sc_bwd(p0, wts["l0_sc_conv_w"], dcat0, dp0, W, "l0_sc_bwd")
    g["l0_sc_conv_w"] = dscw
    dp0 = at("small_ready", dp0, g)
    n0 = wts["l0_w_in"].shape[1]
    td0 = math.gcd(n0, W)
    nd0 = n0 // td0
    tm0 = _pick(S, (1024, 512, 256, 128))
    per_part0 = W // td0

    def nt_maps0(k):
        def f(i, j, t):
            vb = j * nd0 + k
            return ((vb // per_part0) * (S // tm0) + i, vb % per_part0)
        return f

    def tn_maps0(k):
        def f(j, t):
            vb = j * nd0 + k
            return (vb // per_part0, vb % per_part0)
        return f

    dp0_2d = dp0.reshape(6 * S, W)
    dw0 = mm_tn(h0, dp0_2d, nb, n0, "l0_w_in_dw", dy_maps=[tn_maps0(k) for k in range(nd0)], tn=n0)
    dp0_2d = on_grad("l0_w_in", dw0, dp0_2d)
    dp0_2d = on_grad(None, None, dp0_2d)
    dh0 = mm_nt(dp0_2d, wts["l0_w_in"], nb, S, D, "l0_w_in_dx", dy_maps=[nt_maps0(k) for k in range(nd0)], tm=tm0, tn=n0)
    dh0 = at("small_done", dh0, None)
    dx0, _, dg = rms_bwd(x, wts["l0_mix_norm_g"], dh0, dx1, "l0_mix_rms_bwd")
    g["l0_mix_norm_g"] = dg
    return dx0, g


GATHER_ID = 1


def _place():
    return lax.axis_index("x"), lax.axis_index("y"), lax.axis_index("c")


def _other_chips(x, y):
    return [(x, 1 - y), (1 - x, y), (1 - x, 1 - y)]


def _handshake(peers):
    barrier = pltpu.get_barrier_semaphore()
    for peer in peers:
        pl.semaphore_signal(barrier, inc=1, device_id=peer, device_id_type=MESH)
    pl.semaphore_wait(barrier, len(peers))


UPDATE_LAG = 2


def _on_sequencer(body, out_type, scratch_types, collective_id, name):
    return pl.kernel(
        body,
        out_type=out_type,
        mesh=plsc.ScalarSubcoreMesh(axis_name="seq", num_cores=1),
        scratch_types=scratch_types,
        compiler_params=pltpu.CompilerParams(collective_id=collective_id),
        name=name,
    )


def all_gather(arrs, name):
    n = len(arrs)

    def body(*refs):
        xs, outs = refs[:n], refs[n : 2 * n]
        send_sems, recv_sems, local_sems = refs[2 * n :]
        x, y, c = _place()
        me, sibling = (x, y, c), (x, y, 1 - c)
        chips = _other_chips(x, y)
        _handshake([sibling] + [(*chip, c) for chip in chips])

        def copy(a, k, block, to, src=None):
            px, py, pc = block
            dst = outs[a].at[4 * px + 2 * py + pc]
            return pltpu.make_async_remote_copy(
                src_ref=dst if src is None else src, dst_ref=dst,
                send_sem=send_sems.at[7 * a + k], recv_sem=recv_sems.at[7 * a + k], device_id=to, device_id_type=MESH,
            )

        mine = [pltpu.make_async_copy(xs[a], outs[a].at[4 * x + 2 * y + c], local_sems.at[a]) for a in range(n)]
        for cp in mine:
            cp.start()
        first = []
        for a in range(n):
            first.append(copy(a, 0, me, sibling, src=xs[a]))
            first += [copy(a, 1 + j, me, (*chip, c), src=xs[a]) for j, chip in enumerate(chips)]
        for cp in first:
            cp.start()
        passed = []
        for a in range(n):
            for j, chip in enumerate(chips):
                copy(a, 1 + j, (*chip, c), me).wait_recv()
                cp = copy(a, 4 + j, (*chip, c), sibling)
                cp.start()
                passed.append(cp)
        for a in range(n):
            copy(a, 0, sibling, me).wait_recv()
            for j, chip in enumerate(chips):
                copy(a, 4 + j, (*chip, 1 - c), me).wait_recv()
        for cp in first + passed:
            cp.wait_send()
        for cp in mine:
            cp.wait()

    out_type = [jax.ShapeDtypeStruct((NDEV,) + a.shape, a.dtype) for a in arrs]
    sems = [pltpu.SemaphoreType.DMA((7 * n,)), pltpu.SemaphoreType.DMA((7 * n,)), pltpu.SemaphoreType.DMA((n,))]
    return _on_sequencer(body, out_type, sems, GATHER_ID, name)(*arrs)


def all_gather_direct(arr, name):
    R, C = arr.shape

    def body(x_ref, out_ref, send_sems, recv_sems, local_sem):
        x, y, c = _place()
        me = 4 * x + 2 * y + c

        def copy(k, slot, to):
            return pltpu.make_async_remote_copy(
                src_ref=x_ref, dst_ref=out_ref.at[slot], send_sem=send_sems.at[k], recv_sem=recv_sems.at[k],
                device_id=(to // 4, (to // 2) % 2, to % 2), device_id_type=MESH,
            )

        mine = pltpu.make_async_copy(x_ref, out_ref.at[me], local_sem)
        mine.start()
        sends = [copy(k, me, (me + k + 1) % NDEV) for k in range(NDEV - 1)]
        for cp in sends:
            cp.start()
        for k in range(NDEV - 1):
            sender = (me + NDEV - k - 1) % NDEV
            copy(k, sender, sender).wait_recv()
        for cp in sends:
            cp.wait_send()
        mine.wait()

    vmem = pl.BlockSpec(memory_space=pltpu.VMEM)
    return pl.pallas_call(
        body,
        in_specs=[vmem],
        out_specs=vmem,
        out_shape=jax.ShapeDtypeStruct((NDEV, R, C), arr.dtype),
        scratch_shapes=[pltpu.SemaphoreType.DMA((NDEV - 1,)), pltpu.SemaphoreType.DMA((NDEV - 1,)), pltpu.SemaphoreType.DMA],
        name=name,
    )(arr)


_IN_HBM = pl.BlockSpec(memory_space=pltpu.HBM)
_IN_SEM = pl.BlockSpec(memory_space=pltpu.SEMAPHORE)
_EFFECT = pltpu.SideEffectType.DATAFLOW_SIDE_EFFECTING


def _split_start(make_copies, src, land_shape, nsem, name):
    def body(src_ref, land_ref, send_sems, recv_sems, land_thru, token):
        for cp in make_copies(src_ref, land_ref, send_sems, recv_sems):
            cp.start()
        token[...] = jnp.zeros_like(token)

    send_sems, recv_sems, land_thru, token = pl.pallas_call(
        body,
        name=name,
        out_shape=(
            pltpu.SemaphoreType.DMA((nsem,)), pltpu.SemaphoreType.DMA((nsem,)),
            pltpu.HBM(land_shape, src.dtype), jax.ShapeDtypeStruct((8, 128), F32),
        ),
        in_specs=(_IN_HBM, _IN_HBM),
        out_specs=(_IN_SEM, _IN_SEM, _IN_HBM, pl.BlockSpec(memory_space=pltpu.VMEM)),
        input_output_aliases={1: 2},
        compiler_params=pltpu.CompilerParams(has_side_effects=_EFFECT),
    )(src, pltpu.with_memory_space_constraint(lax.empty(land_shape, src.dtype), pltpu.HBM))
    return send_sems, recv_sems, src, land_thru, token


def _split_wait(make_copies, send_sems, recv_sems, src_thru, land_thru, after, name):
    def body(src_ref, land_ref, send_sems, recv_sems, after_ref, land_out):
        for cp in make_copies(src_ref, land_ref, send_sems, recv_sems):
            cp.wait_send()
            cp.wait_recv()

    return pl.pallas_call(
        body,
        name=name,
        out_shape=pltpu.HBM(land_thru.shape, land_thru.dtype),
        in_specs=(_IN_HBM, _IN_HBM, _IN_SEM, _IN_SEM, pl.BlockSpec(memory_space=pl.ANY)),
        out_specs=_IN_HBM,
        input_output_aliases={1: 0},
        compiler_params=pltpu.CompilerParams(has_side_effects=_EFFECT),
    )(src_thru, land_thru, send_sems, recv_sems, after)


def _pair_copies(src_ref, land_ref, send_sems, recv_sems):
    x, y, c = _place()
    return [
        pltpu.make_async_remote_copy(
            src_ref=src_ref.at[k, 1 - c], dst_ref=land_ref.at[k],
            send_sem=send_sems.at[k], recv_sem=recv_sems.at[k], device_id=(x, y, 1 - c), device_id_type=MESH,
        )
        for k in range(4)
    ]


def _chip_copies(src_ref, land_ref, send_sems, recv_sems):
    x, y, c = _place()
    return [
        pltpu.make_async_remote_copy(
            src_ref=src_ref.at[2 * px + py], dst_ref=land_ref.at[2 * x + y],
            send_sem=send_sems.at[j], recv_sem=recv_sems.at[j], device_id=(px, py, c), device_id_type=MESH,
        )
        for j, (px, py) in enumerate(_other_chips(x, y))
    ]


def _row_tile(R, C, max_elems):
    if R * C <= max_elems:
        return R
    best = None
    for tr in range(16, R, 16):
        if R % tr == 0 and tr * C <= max_elems:
            best = tr
    return best or R


def pair_sum(a42, land4, core, name):
    _, _, R, C = a42.shape
    tr = _row_tile(R, C, 1 << 20)

    def body(core_ref, a_ref, l_ref, o_ref):
        o_ref[...] = (a_ref[0].astype(F32) + l_ref[...].astype(F32)).astype(o_ref.dtype)

    return pl.pallas_call(
        body,
        grid_spec=pltpu.PrefetchScalarGridSpec(
            num_scalar_prefetch=1,
            grid=(4, R // tr),
            in_specs=[
                pl.BlockSpec((1, 1, tr, C), lambda k, r, core_ref: (k, core_ref[0], r, 0)),
                pl.BlockSpec((1, tr, C), lambda k, r, core_ref: (k, r, 0)),
            ],
            out_specs=pl.BlockSpec((1, tr, C), lambda k, r, core_ref: (k, r, 0)),
        ),
        out_shape=jax.ShapeDtypeStruct((4, R, C), BF16),
        compiler_params=_cp(("parallel", "parallel")),
        name=name,
    )(core, a42, land4)


def sum_slots(parts, name):
    P, R, C = parts.shape

    def body(p_ref, o_ref):
        acc = p_ref[0].astype(F32)
        for k in range(1, P):
            acc = acc + p_ref[k].astype(F32)
        o_ref[...] = acc

    tr = _row_tile(R, P * C, 1 << 21)
    return pl.pallas_call(
        body,
        grid=(R // tr,),
        in_specs=[pl.BlockSpec((P, tr, C), lambda r: (0, r, 0))],
        out_specs=pl.BlockSpec((tr, C), lambda r: (r, 0)),
        out_shape=jax.ShapeDtypeStruct((R, C), F32),
        compiler_params=_cp(("parallel",)),
        name=name,
    )(parts)


def adamw(w, m, v, parts, name):
    R, C = w.shape
    P = parts.shape[0]
    tr = _pick(R, (256, 128, 64, 32, 16, 8))
    c1 = 1.0 - ADAM_B1 ** ADAM_STEP
    c2 = 1.0 - ADAM_B2 ** ADAM_STEP

    def body(w_ref, m_ref, v_ref, p_ref, g_ref, d_ref, nm_ref, nv_ref):
        g = p_ref[0].astype(F32)
        for k in range(1, P):
            g = g + p_ref[k].astype(F32)
        nm = ADAM_B1 * m_ref[...] + (1.0 - ADAM_B1) * g
        nv = ADAM_B2 * v_ref[...] + (1.0 - ADAM_B2) * (g * g)
        g_ref[...] = g
        nm_ref[...] = nm
        nv_ref[...] = nv
        d_ref[...] = -ADAM_LR * ((nm / c1) / (jnp.sqrt(nv / c2) + ADAM_EPS) + ADAM_WD * w_ref[...])

    blk = pl.BlockSpec((tr, C), lambda r: (r, 0))
    shp = jax.ShapeDtypeStruct((R, C), F32)
    return pl.pallas_call(
        body,
        grid=(R // tr,),
        in_specs=[blk, blk, blk, pl.BlockSpec((P, tr, C), lambda r: (0, r, 0))],
        out_specs=[blk, blk, blk, blk],
        out_shape=[shp, shp, shp, shp],
        compiler_params=_cp(("parallel",)),
        name=name,
    )(w, m, v, parts)


def adamw_reduced(w, m, v, own, land, chip, name):
    R, C = w.shape
    if R % 8 == 0:
        tr, tc = _pick(R, (256, 128, 64, 32, 16, 8)), C
    else:
        tr, tc = R, _pick(C, (256, 128))
    c1 = 1.0 - ADAM_B1 ** ADAM_STEP
    c2 = 1.0 - ADAM_B2 ** ADAM_STEP

    def body(chip_ref, w_ref, m_ref, v_ref, own_ref, land_ref, g_ref, d_ref, nm_ref, nv_ref):
        mine = own_ref[0].astype(F32)
        g = None
        for k in range(4):
            term = jnp.where(chip_ref[0] == k, mine, land_ref[k].astype(F32))
            g = term if g is None else g + term
        nm = ADAM_B1 * m_ref[...] + (1.0 - ADAM_B1) * g
        nv = ADAM_B2 * v_ref[...] + (1.0 - ADAM_B2) * (g * g)
        g_ref[...] = g
        nm_ref[...] = nm
        nv_ref[...] = nv
        d_ref[...] = -ADAM_LR * ((nm / c1) / (jnp.sqrt(nv / c2) + ADAM_EPS) + ADAM_WD * w_ref[...])

    blk = pl.BlockSpec((tr, tc), lambda r, c, chip_ref: (r, c))
    shp = jax.ShapeDtypeStruct((R, C), F32)
    return pl.pallas_call(
        body,
        grid_spec=pltpu.PrefetchScalarGridSpec(
            num_scalar_prefetch=1,
            grid=(R // tr, C // tc),
            in_specs=[
                blk, blk, blk,
                pl.BlockSpec((1, tr, tc), lambda r, c, chip_ref: (chip_ref[0], r, c)),
                pl.BlockSpec((4, tr, tc), lambda r, c, chip_ref: (0, r, c)),
            ],
            out_specs=[blk, blk, blk, blk],
        ),
        out_shape=[shp, shp, shp, shp],
        compiler_params=_cp(("parallel", "parallel")),
        name=name,
    )(chip, w, m, v, own, land)


_WEIGHTS = [
    "l0_mix_norm_g", "l0_w_in", "l0_sc_conv_w", "l0_w_out", "l0_ffn_norm_g", "l0_ffn_up", "l0_ffn_conv_w", "l0_ffn_down",
    "l1_mix_norm_g", "l1_w_in", "l1_fox_b_f", "l1_sg_w", "l1_sg_b", "l1_sg_norm_g", "l1_w_out", "l1_ffn_norm_g",
    "l1_ffn_up", "l1_ffn_conv_w", "l1_ffn_down", "final_norm_g",
]
_ROW_SHARDED = ["l0_w_out", "l0_ffn_down", "l1_w_out", "l1_ffn_down"]
_BIG = ["l0_w_in", "l0_w_out", "l0_ffn_up", "l0_ffn_down", "l1_w_in", "l1_w_out", "l1_ffn_up", "l1_ffn_down"]
_CONV = ["l0_sc_conv_w", "l0_ffn_conv_w", "l1_ffn_conv_w"]
_SMALL = [n for n in _WEIGHTS if n not in _BIG]
_LAST_SMALL = "l0_mix_norm_g"
_PACK_ROWS = 8


def _pack(arrs):
    flat = []
    for a in arrs:
        v = a.reshape(-1).astype(F32)
        pad = (-v.shape[0]) % (_PACK_ROWS * 128)
        flat.append(jnp.pad(v, (0, pad)))
    return jnp.concatenate(flat).reshape(-1, 128)


def _unpack(packed, shapes):
    out, off = [], 0
    flat = packed.reshape(-1)
    for shp in shapes:
        size = math.prod(shp)
        out.append(flat[off : off + size].reshape(shp))
        off += size + (-size) % (_PACK_ROWS * 128)
    return out


def kernel(x, l0_mix_norm_g, l0_w_in, l0_sc_conv_w, l0_w_out, l0_ffn_norm_g, l0_ffn_up, l0_ffn_conv_w, l0_ffn_down, l1_mix_norm_g, l1_w_in, l1_fox_b_f, l1_sg_w, l1_sg_b, l1_sg_norm_g, l1_w_out, l1_ffn_norm_g, l1_ffn_up, l1_ffn_conv_w, l1_ffn_down, final_norm_g, loss_target, m_l0_mix_norm_g, m_l0_w_in, m_l0_sc_conv_w, m_l0_w_out, m_l0_ffn_norm_g, m_l0_ffn_up, m_l0_ffn_conv_w, m_l0_ffn_down, m_l1_mix_norm_g, m_l1_w_in, m_l1_fox_b_f, m_l1_sg_w, m_l1_sg_b, m_l1_sg_norm_g, m_l1_w_out, m_l1_ffn_norm_g, m_l1_ffn_up, m_l1_ffn_conv_w, m_l1_ffn_down, m_final_norm_g, v_l0_mix_norm_g, v_l0_w_in, v_l0_sc_conv_w, v_l0_w_out, v_l0_ffn_norm_g, v_l0_ffn_up, v_l0_ffn_conv_w, v_l0_ffn_down, v_l1_mix_norm_g, v_l1_w_in, v_l1_fox_b_f, v_l1_sg_w, v_l1_sg_b, v_l1_sg_norm_g, v_l1_w_out, v_l1_ffn_norm_g, v_l1_ffn_up, v_l1_ffn_conv_w, v_l1_ffn_down, v_final_norm_g):
    given = dict(locals())
    w = {n: given[n] for n in _WEIGHTS}
    mom = {n: given["m_" + n] for n in _WEIGHTS}
    var = {n: given["v_" + n] for n in _WEIGHTS}
    xs, target = x[0], loss_target[0]
    S, D = xs.shape
    W = D // 2
    nh = W // HD
    cx, cy, cc = _place()
    me = 4 * cx + 2 * cy + cc

    wts = {"nb": NDEV, "F": l0_ffn_down.shape[0] * NDEV}
    for n in _SMALL:
        if n not in _CONV:
            wts[n] = w[n]
    gathered, loss_sum = {}, []

    def start_gather(n):
        src = w[n].T if n == "l1_w_in" else w[n]
        got = all_gather([src.astype(BF16)] + ([w[c] for c in _CONV] if n == _BIG[0] else []), f"gather_{n}")
        if n == "l1_w_in":
            gathered[n] = got[0]
        elif n in _ROW_SHARDED:
            wts[n] = got[0].reshape(-1, D)
        else:
            wts[n] = got[0].reshape(NDEV * D, -1)
        for c, taps in zip(_CONV, got[1:]):
            wts[c] = taps.transpose(1, 0, 2).reshape(CONV_K, -1)

    def at(point, after, value):
        if point == "l1_w_in":
            got, after = lax.optimization_barrier((gathered[point], after))
            wts["l1_w_in_t"] = got.reshape(-1, D)
            wts["l1_w_f_t"] = jnp.pad(wts["l1_w_in_t"][5 * W :], ((0, 128 - nh), (0, 0)))
        elif point == "loss":
            total, after = lax.optimization_barrier((lax.psum(value[0, 0], ("x", "y", "c")), after))
            loss_sum.append(total)
        elif point == "small_ready":
            early = [n for n in _SMALL if n != _LAST_SMALL]
            gathered["small"] = all_gather([_pack([value[n] for n in early])], "gather_small_grads")[0]
        elif point == "small_done":
            after = update_small([n for n in _SMALL if n != _LAST_SMALL], gathered["small"], "small", after)
        return after

    out_g, out_d, out_m, out_v = {}, {}, {}, {}

    def update_small(names, all_terms, tag, after=None):
        shapes = [w[n].shape for n in names]
        full_shapes = [(CONV_K, NDEV * w[n].shape[1]) if n in _CONV else w[n].shape for n in names]
        grads = {}
        for n, t in zip(names, _unpack(sum_slots(all_terms, f"sum_{tag}_grads"), full_shapes)):
            if n in _CONV:
                cols = w[n].shape[1]
                t = lax.dynamic_slice_in_dim(t, me * cols, cols, axis=1)
            grads[n] = t
        res = adamw(
            _pack([w[n] for n in names]), _pack([mom[n] for n in names]), _pack([var[n] for n in names]),
            _pack([grads[n] for n in names])[None], f"adamw_{tag}",
        )
        if after is not None:
            res, after = lax.optimization_barrier((res, after))
        for dst, packed_out in zip((out_g, out_d, out_m, out_v), res):
            for n, t in zip(names, _unpack(packed_out, shapes)):
                dst[n] = t
        return after

    core = jnp.reshape(cc, (1,)).astype(jnp.int32)
    chip = jnp.reshape(2 * cx + cy, (1,)).astype(jnp.int32)
    pair_flying, chip_flying = [], []

    def tie(value, after):
        if after is None:
            return value, None
        return lax.optimization_barrier((value, after))

    def to_chips(after):
        n, flying = pair_flying.pop()
        landed = _split_wait(_pair_copies, *flying, f"reduce_pair_wait_{n}")
        summed = pair_sum(flying[2], landed, core, f"pair_sum_{n}")
        *flying, token = _split_start(_chip_copies, summed, summed.shape, 3, f"reduce_chips_{n}")
        token, after = tie(token, after)
        chip_flying.append((n, flying + [token]))
        return after

    def update(after, behind=None):
        n, flying = chip_flying.pop(0)
        if behind is not None:
            flying[4], _ = lax.optimization_barrier((flying[4], behind))
        landed = _split_wait(_chip_copies, *flying, f"reduce_chips_wait_{n}")
        turn = (lambda t: t.T) if n == "l1_w_in" else (lambda t: t)
        res = adamw_reduced(turn(w[n]), turn(mom[n]), turn(var[n]), flying[2], landed, chip, f"adamw_{n}")
        res, after = tie(res, after)
        out_g[n], out_d[n], out_m[n], out_v[n] = [turn(t) for t in res]
        return after, res[0]

    def on_grad(n, term, after):
        if n is None:
            return to_chips(after)
        if n in _ROW_SHARDED or n == "l1_w_in":
            term = term.reshape(NDEV, -1, D)
        else:
            term = term.reshape(NDEV, D, -1)
        term = term.reshape((4, 2) + term.shape[1:])
        *flying, token = _split_start(_pair_copies, term, term.shape[:1] + term.shape[2:], 4, f"reduce_pair_{n}")
        token, after = tie(token, after)
        if len(chip_flying) == UPDATE_LAG:
            after, _ = update(after)
        if pair_flying:
            after = to_chips(after)
        pair_flying.append((n, flying + [token]))
        return after

    for n in _BIG:
        start_gather(n)
    dx, g = local_step(xs, target, wts, at, on_grad)
    update_small([_LAST_SMALL], all_gather_direct(_pack([g[_LAST_SMALL]]), "gather_last_grad"), "last")
    done = out_g[_LAST_SMALL]
    while chip_flying:
        _, done = update(None, behind=done)
    loss = loss_sum[0]

    return (loss, dx[None], *[out_g[n] for n in _WEIGHTS], *[out_d[n] for n in _WEIGHTS],
            *[out_m[n] for n in _WEIGHTS], *[out_v[n] for n in _WEIGHTS])
```

```python
import functools
import math

import jax
import jax.numpy as jnp
from jax import lax
from jax.experimental import pallas as pl
from jax.experimental.pallas import tpu as pltpu
from jax.experimental.pallas import tpu_sc as plsc

F32 = jnp.float32
BF16 = jnp.bfloat16
HD = 128
EPS = 1e-6
CONV_K = 3
VMEM_LIMIT_BYTES = 48 << 20
NDEV = 8
MESH = pl.DeviceIdType.MESH

ADAM_LR = 0.001
ADAM_B1 = 0.9
ADAM_B2 = 0.999
ADAM_EPS = 1e-08
ADAM_WD = 0.01
ADAM_STEP = 10


def _cp(sem):
    return pltpu.CompilerParams(dimension_semantics=sem, vmem_limit_bytes=VMEM_LIMIT_BYTES)


def _pick(n, prefs):
    for p in prefs:
        if n % p == 0:
            return p
    return n


def _dot(a, b):
    return jnp.dot(a, b, preferred_element_type=F32)


def _dot_nt(a, b):
    return lax.dot_general(a, b, (((1,), (1,)), ((), ())), preferred_element_type=F32)


def _dot_tn(a, b):
    return lax.dot_general(a, b, (((0,), (0,)), ((), ())), preferred_element_type=F32)


def _split3(x):
    hi = x.astype(BF16)
    r = x - hi.astype(F32)
    mid = r.astype(BF16)
    lo = (r - mid.astype(F32)).astype(BF16)
    return hi, mid, lo


def _dot_ones_left(ones_bf16, x):
    hi, mid, lo = _split3(x)
    return _dot(ones_bf16, hi) + _dot(ones_bf16, mid) + _dot(ones_bf16, lo)


def _iota2(shape, axis):
    return lax.broadcasted_iota(jnp.int32, shape, axis)


def mm_nn(a, w2d, nb, name, out_dtype=BF16, res=None, tm=None, tn=None, tk=None, a_map=None, a_shape=None):
    M, K = a_shape or a.shape
    n = w2d.shape[1]
    assert w2d.shape[0] == nb * K or (nb == 1 and w2d.shape[0] > K)
    a_map = a_map or (lambda i, k: (i, k))
    tm = tm or _pick(M, (1024, 512, 256, 128))
    tn = tn or _pick(n, (1408, 1024, 768, 512, 256, 128))
    tk = tk or (K if K <= 2048 else _pick(K, (1408, 1024, 512, 256, 128)))
    nk, nt = K // tk, n // tn
    has_res = res is not None

    def body(*refs):
        if has_res:
            a_ref, w_ref, r_ref, o_ref = refs[:4]
        else:
            a_ref, w_ref, o_ref = refs[:3]
            r_ref = None
        part = _dot(a_ref[...], w_ref[...])

        def finish(acc):
            if r_ref is not None:
                acc = acc + r_ref[...].astype(F32)
            o_ref[...] = acc.astype(o_ref.dtype)

        if nk == 1:
            finish(part)
        else:
            acc_ref = refs[-1]
            k = pl.program_id(3)

            @pl.when(k == 0)
            def _():
                acc_ref[...] = part

            @pl.when(k > 0)
            def _():
                acc_ref[...] += part

            @pl.when(k == nk - 1)
            def _():
                finish(acc_ref[...])

    in_specs = [
        pl.BlockSpec((tm, tk), lambda i, j, t, k: a_map(i, k)),
        pl.BlockSpec((tk, tn), lambda i, j, t, k: (j * nk + k, t)),
    ]
    args = [a, w2d]
    out_spec = pl.BlockSpec((tm, tn), lambda i, j, t, k: (i, j * nt + t))
    if has_res:
        in_specs.append(out_spec)
        args.append(res)
    return pl.pallas_call(
        body,
        grid=(M // tm, nb, nt, nk),
        in_specs=in_specs,
        out_specs=out_spec,
        out_shape=jax.ShapeDtypeStruct((M, nb * n), out_dtype),
        scratch_shapes=[pltpu.VMEM((tm, tn), F32)] if nk > 1 else [],
        compiler_params=_cp(("parallel", "parallel", "parallel", "arbitrary")),
        name=name,
    )(*args)


def mm_nt(dy2d, w2d, nb, M, K, name, out_dtype=BF16, res=None, dy_maps=None, tm=None, tko=None, tn=None):
    n = w2d.shape[1]
    assert w2d.shape[0] == nb * K or (nb == 1 and w2d.shape[0] > K)
    tm = tm or _pick(M, (1024, 512, 256, 128))
    tko = tko or _pick(K, (1024, 512, 256, 128))
    tn = tn or _pick(n, (1408, 1024, 768, 512, 256, 128))
    nt, nko = n // tn, K // tko
    has_res = res is not None
    if dy_maps is None:
        dy_maps = [lambda i, j, t: (i, j * nt + t)]
    nd = len(dy_maps)
    td = tn // nd

    one_step = nb * nt == 1

    def body(*refs):
        d_refs, w_ref = refs[:nd], refs[nd]
        r_ref = refs[nd + 1] if has_res else None
        d = d_refs[0][...] if nd == 1 else jnp.concatenate([r[...] for r in d_refs], axis=1)
        part = _dot_nt(d, w_ref[...])
        if one_step:
            o_ref = refs[-1]
            if r_ref is not None:
                part = part + r_ref[...].astype(F32)
            o_ref[...] = part.astype(o_ref.dtype)
            return
        o_ref, acc_ref = refs[-2], refs[-1]
        j, t = pl.program_id(2), pl.program_id(3)
        first = jnp.logical_and(j == 0, t == 0)
        last = jnp.logical_and(j == nb - 1, t == nt - 1)

        @pl.when(first)
        def _():
            acc_ref[...] = part

        @pl.when(jnp.logical_not(first))
        def _():
            acc_ref[...] += part

        @pl.when(last)
        def _():
            acc = acc_ref[...]
            if r_ref is not None:
                acc = acc + r_ref[...].astype(F32)
            o_ref[...] = acc.astype(o_ref.dtype)

    in_specs = [pl.BlockSpec((tm, td), functools.partial(lambda f, i, ko, j, t: f(i, j, t), f)) for f in dy_maps]
    in_specs.append(pl.BlockSpec((tko, tn), lambda i, ko, j, t: (j * nko + ko, t)))
    args = [dy2d] * nd + [w2d]
    out_spec = pl.BlockSpec((tm, tko), lambda i, ko, j, t: (i, ko))
    if has_res:
        in_specs.append(out_spec)
        args.append(res)
    return pl.pallas_call(
        body,
        grid=(M // tm, nko, nb, nt),
        in_specs=in_specs,
        out_specs=out_spec,
        out_shape=jax.ShapeDtypeStruct((M, K), out_dtype),
        scratch_shapes=[] if one_step else [pltpu.VMEM((tm, tko), F32)],
        compiler_params=_cp(("parallel", "parallel", "arbitrary", "arbitrary")),
        name=name,
    )(*args)


def mm_tn(x, dy2d, nb, n, name, out_dtype=BF16, dy_maps=None, tko=None, tn=None, x_map=None, x_shape=None):
    S, K = x_shape or x.shape
    x_map = x_map or (lambda ko: (0, ko))
    tko = tko or _pick(K, (512, 256, 128))
    tn = tn or _pick(n, (1408, 1024, 768, 512, 256, 128))
    nt, nko = n // tn, K // tko
    if dy_maps is None:
        dy_maps = [lambda j, t: (0, j * nt + t)]
    nd = len(dy_maps)
    td = tn // nd

    def body(*refs):
        x_ref, d_refs, o_ref = refs[0], refs[1 : 1 + nd], refs[-1]
        d = d_refs[0][...] if nd == 1 else jnp.concatenate([r[...] for r in d_refs], axis=1)
        o_ref[...] = _dot_tn(x_ref[...], d).astype(o_ref.dtype)

    in_specs = [pl.BlockSpec((S, tko), lambda ko, j, t: x_map(ko))]
    in_specs += [pl.BlockSpec((S, td), functools.partial(lambda f, ko, j, t: f(j, t), f)) for f in dy_maps]
    return pl.pallas_call(
        body,
        grid=(nko, nb, nt),
        in_specs=in_specs,
        out_specs=pl.BlockSpec((tko, tn), lambda ko, j, t: (j * nko + ko, t)),
        out_shape=jax.ShapeDtypeStruct((nb * K, n), out_dtype),
        compiler_params=_cp(("parallel", "parallel", "parallel")),
        name=name,
    )(x, *([dy2d] * nd))


def rms_fwd(x, g, name):
    S, D = x.shape
    tm = _pick(S, (256, 128))

    def body(x_ref, g_ref, o_ref):
        xf = x_ref[...]
        r = lax.rsqrt(jnp.mean(xf * xf, axis=-1, keepdims=True) + EPS)
        o_ref[...] = (xf * r * g_ref[...]).astype(o_ref.dtype)

    return pl.pallas_call(
        body,
        grid=(S // tm,),
        in_specs=[pl.BlockSpec((tm, D), lambda i: (i, 0)), pl.BlockSpec((1, D), lambda i: (0, 0))],
        out_specs=pl.BlockSpec((tm, D), lambda i: (i, 0)),
        out_shape=jax.ShapeDtypeStruct((S, D), BF16),
        compiler_params=_cp(("parallel",)),
        name=name,
    )(x, g.reshape(1, D))


def rms_bwd(x, g, dh, dres, name):
    S, D = x.shape
    tm = _pick(S, (256, 128))

    def body(x_ref, g_ref, dh_ref, dr_ref, dx_ref, dxb_ref, dg_ref):
        i = pl.program_id(0)
        xf = x_ref[...]
        dh = dh_ref[...].astype(F32)
        r = lax.rsqrt(jnp.mean(xf * xf, axis=-1, keepdims=True) + EPS)
        gy = dh * g_ref[...]
        proj = jnp.mean(gy * xf, axis=-1, keepdims=True)
        dx = dr_ref[...] + r * gy - xf * (r * r * r * proj)
        dx_ref[...] = dx
        dxb_ref[...] = dx.astype(BF16)
        dg = jnp.sum(dh * (xf * r), axis=0, keepdims=True)

        @pl.when(i == 0)
        def _():
            dg_ref[...] = dg

        @pl.when(i > 0)
        def _():
            dg_ref[...] += dg

    row = pl.BlockSpec((tm, D), lambda i: (i, 0))
    vec = pl.BlockSpec((1, D), lambda i: (0, 0))
    return pl.pallas_call(
        body,
        grid=(S // tm,),
        in_specs=[row, vec, row, row],
        out_specs=[row, row, vec],
        out_shape=[jax.ShapeDtypeStruct((S, D), F32), jax.ShapeDtypeStruct((S, D), BF16), jax.ShapeDtypeStruct((1, D), F32)],
        compiler_params=_cp(("arbitrary",)),
        name=name,
    )(x, g.reshape(1, D), dh, dres)


def loss_head(x, g, target, name):
    S, D = x.shape
    tm = _pick(S, (256, 128))

    def body(x_ref, g_ref, t_ref, dx_ref, dxb_ref, dg_ref, loss_ref):
        i = pl.program_id(0)
        xf = x_ref[...]
        gg = g_ref[...]
        r = lax.rsqrt(jnp.mean(xf * xf, axis=-1, keepdims=True) + EPS)
        xh = xf * r
        err = xh * gg - t_ref[...]
        part = (0.5 / D) * jnp.sum(err * err)
        dy = err * (1.0 / D)
        gy = dy * gg
        proj = jnp.mean(gy * xf, axis=-1, keepdims=True)
        dx = r * gy - xf * (r * r * r * proj)
        dx_ref[...] = dx
        dxb_ref[...] = dx.astype(BF16)
        dg = jnp.sum(dy * xh, axis=0, keepdims=True)
        lossb = jnp.full(loss_ref.shape, part, F32)

        @pl.when(i == 0)
        def _():
            dg_ref[...] = dg
            loss_ref[...] = lossb

        @pl.when(i > 0)
        def _():
            dg_ref[...] += dg
            loss_ref[...] += lossb

    row = pl.BlockSpec((tm, D), lambda i: (i, 0))
    vec = pl.BlockSpec((1, D), lambda i: (0, 0))
    return pl.pallas_call(
        body,
        grid=(S // tm,),
        in_specs=[row, vec, row],
        out_specs=[row, row, vec, pl.BlockSpec((8, 128), lambda i: (0, 0))],
        out_shape=[
            jax.ShapeDtypeStruct((S, D), F32),
            jax.ShapeDtypeStruct((S, D), BF16),
            jax.ShapeDtypeStruct((1, D), F32),
            jax.ShapeDtypeStruct((8, 128), F32),
        ],
        compiler_params=_cp(("arbitrary",)),
        name=name,
    )(x, g.reshape(1, D), target)


def _shift_down(s, k):
    if k == 0:
        return s
    return jnp.where(_iota2(s.shape, 0) >= k, pltpu.roll(s, k, axis=0), 0.0)


def _shift_up(s, k):
    if k == 0:
        return s
    n = s.shape[0]
    return jnp.where(_iota2(s.shape, 0) < n - k, pltpu.roll(s, n - k, axis=0), 0.0)


def _conv(s, w):
    return w[0:1] * _shift_down(s, 2) + w[1:2] * _shift_down(s, 1) + w[2:3] * s


def _conv_t(d, w):
    return w[2:3] * d + w[1:2] * _shift_up(d, 1) + w[0:1] * _shift_up(d, 2)


def _conv_dw(d, s):
    return [jnp.sum(d * _shift_down(s, CONV_K - 1 - k), axis=0, keepdims=True) for k in range(CONV_K)]


def sc_fwd(p, convw, cat, W, name):
    S = p.shape[0]
    tc = _pick(W, (256, 128))
    nc = W // tc

    def body(gb_ref, gc_ref, hi_ref, w_ref, cat_ref, o_ref):
        s = gc_ref[...].astype(F32) * hi_ref[...].astype(F32)
        o_ref[...] = (gb_ref[...].astype(F32) * _conv(s, w_ref[...])).astype(o_ref.dtype)

    col = lambda part: pl.BlockSpec((S, tc), lambda c: (0, part * nc + c))
    return pl.pallas_call(
        body,
        grid=(nc,),
        in_specs=[col(3), col(4), col(5), pl.BlockSpec((CONV_K, tc), lambda c: (0, c)), pl.BlockSpec(memory_space=pl.ANY)],
        out_specs=col(1),
        out_shape=jax.ShapeDtypeStruct(cat.shape, cat.dtype),
        input_output_aliases={4: 0},
        compiler_params=_cp(("parallel",)),
        name=name,
    )(p, p, p, convw, cat)


def sc_bwd(p, convw, dcat, dp, W, name):
    S = p.shape[0]
    tc = _pick(W, (256, 128))
    nc = W // tc

    def body(gb_ref, gc_ref, hi_ref, w_ref, do_ref, dp_in_ref, dp_ref, dw_ref):
        gb = gb_ref[...].astype(F32)
        gc = gc_ref[...].astype(F32)
        hi = hi_ref[...].astype(F32)
        w = w_ref[...]
        do = do_ref[...].astype(F32)
        s = gc * hi
        dcs = do * gb
        ds = _conv_t(dcs, w)
        dp_ref[0] = (do * _conv(s, w)).astype(dp_ref.dtype)
        dp_ref[1] = (ds * hi).astype(dp_ref.dtype)
        dp_ref[2] = (ds * gc).astype(dp_ref.dtype)
        for k, row in enumerate(_conv_dw(dcs, s)):
            dw_ref[k : k + 1, :] = row

    col = lambda part: pl.BlockSpec((S, tc), lambda c: (0, part * nc + c))
    return pl.pallas_call(
        body,
        grid=(nc,),
        in_specs=[
            col(3), col(4), col(5),
            pl.BlockSpec((CONV_K, tc), lambda c: (0, c)),
            pl.BlockSpec((S, tc), lambda c: (0, nc + c)),
            pl.BlockSpec(memory_space=pl.ANY),
        ],
        out_specs=[pl.BlockSpec((3, S, tc), lambda c: (1, 0, c)), pl.BlockSpec((CONV_K, tc), lambda c: (0, c))],
        out_shape=[jax.ShapeDtypeStruct(dp.shape, dp.dtype), jax.ShapeDtypeStruct((CONV_K, W), F32)],
        input_output_aliases={5: 0},
        compiler_params=_cp(("parallel",)),
        name=name,
    )(p, p, p, convw, dcat, dp)


def _silu_parts(a):
    sig = 1.0 / (1.0 + jnp.exp(-a))
    return a * sig, sig


def ffn_act_fwd(u, convw, F, name):
    S = u.shape[0]
    tc = _pick(F, (256, 128))
    nc = F // tc

    def body(ug_ref, uu_ref, wg_ref, wu_ref, o_ref):
        ag = _conv(ug_ref[...].astype(F32), wg_ref[...])
        au = _conv(uu_ref[...].astype(F32), wu_ref[...])
        o_ref[...] = (_silu_parts(ag)[0] * au).astype(o_ref.dtype)

    col = lambda half: pl.BlockSpec((S, tc), lambda c: (0, half * nc + c))
    wcol = lambda half: pl.BlockSpec((CONV_K, tc), lambda c: (0, half * nc + c))
    return pl.pallas_call(
        body,
        grid=(nc,),
        in_specs=[col(0), col(1), wcol(0), wcol(1)],
        out_specs=pl.BlockSpec((S, tc), lambda c: (0, c)),
        out_shape=jax.ShapeDtypeStruct((S, F), BF16),
        compiler_params=_cp(("parallel",)),
        name=name,
    )(u, u, convw, convw)


def ffn_act_bwd(u, convw, dact, F, name):
    S = u.shape[0]
    tc = _pick(F, (256, 128))
    nc = F // tc

    def body(ug_ref, uu_ref, wg_ref, wu_ref, da_ref, du_ref, dw_ref):
        ug = ug_ref[...].astype(F32)
        uu = uu_ref[...].astype(F32)
        wg = wg_ref[...]
        wu = wu_ref[...]
        da = da_ref[...].astype(F32)
        ag = _conv(ug, wg)
        au = _conv(uu, wu)
        sl, sig = _silu_parts(ag)
        dag = da * au * (sig * (1.0 + ag * (1.0 - sig)))
        dau = da * sl
        du_ref[0] = _conv_t(dag, wg).astype(du_ref.dtype)
        du_ref[1] = _conv_t(dau, wu).astype(du_ref.dtype)
        for k, (rg, ru) in enumerate(zip(_conv_dw(dag, ug), _conv_dw(dau, uu))):
            dw_ref[0, k : k + 1, :] = rg
            dw_ref[1, k : k + 1, :] = ru

    col = lambda half: pl.BlockSpec((S, tc), lambda c: (0, half * nc + c))
    wcol = lambda half: pl.BlockSpec((CONV_K, tc), lambda c: (0, half * nc + c))
    return pl.pallas_call(
        body,
        grid=(nc,),
        in_specs=[col(0), col(1), wcol(0), wcol(1), pl.BlockSpec((S, tc), lambda c: (0, c))],
        out_specs=[pl.BlockSpec((2, S, tc), lambda c: (0, 0, c)), pl.BlockSpec((2, CONV_K, tc), lambda c: (0, 0, c))],
        out_shape=[jax.ShapeDtypeStruct((2, S, F), BF16), jax.ShapeDtypeStruct((2, CONV_K, F), F32)],
        compiler_params=_cp(("parallel",)),
        name=name,
    )(u, u, convw, convw, dact)


def _softplus(z):
    return jnp.maximum(z, 0.0) + jnp.log(1.0 + jnp.exp(-jnp.abs(z)))


def _key_strip(S):
    return _pick(S, (512, 256, 128))


def _query_rows(S):
    tq = _pick(S, (512, 256, 128))
    assert _key_strip(S) % tq == 0
    return tq


def _split2(x):
    hi = x.astype(BF16)
    return hi, (x - hi.astype(F32)).astype(BF16)


def _block_sums(x, ones_bf16):
    hi, lo = _split2(x)
    return [
        _dot(hi[:, b * HD : (b + 1) * HD], ones_bf16) + _dot(lo[:, b * HD : (b + 1) * HD], ones_bf16)
        for b in range(x.shape[1] // HD)
    ]


def _strip_mask(shape, row0, off, strict):
    cols, rows = _iota2(shape, 1) + off, _iota2(shape, 0) + row0
    return cols < rows if strict else cols <= rows


def _sb_strip(q, ks, row0, off, run, su, masked):
    z = _dot_nt(q, ks) * (HD ** -0.5)
    sp = _softplus(z)
    mask = _strip_mask(z.shape, row0, off, True) if masked else None
    l = jnp.where(mask, -sp, 0.0) if masked else -sp
    within = _block_sums(l, su)
    later = [None] * len(within)
    for b in reversed(range(len(within))):
        later[b] = within[b] + run
        run = run + jnp.sum(l[:, b * HD : (b + 1) * HD], axis=1, keepdims=True)
    a = jnp.exp(z - sp + jnp.concatenate(later, axis=1))
    return z, (jnp.where(mask, a, 0.0) if masked else a), run


def sb_fwd(p, W, name):
    S = p.shape[0]
    TQ, TK = _query_rows(S), _key_strip(S)
    nh, nq = W // HD, S // TQ

    def body(q_ref, k_ref, v_ref, o_ref):
        i = pl.program_id(1)
        q = q_ref[...]
        su = (_iota2((HD, HD), 0) > _iota2((HD, HD), 1)).astype(BF16)
        last = (i * TQ + TQ - 1) // TK

        def strip(g, carry, masked):
            acc, run = carry
            off = pl.multiple_of(g * TK, TK)
            _, a, run = _sb_strip(q, k_ref[pl.ds(off, TK), :], i * TQ, off, run, su, masked)
            return acc + _dot(a.astype(BF16), v_ref[pl.ds(off, TK), :]), run

        carry = strip(last, (jnp.zeros((TQ, HD), F32), jnp.zeros((TQ, 1), F32)), True)
        acc, _ = lax.fori_loop(0, last, lambda gg, c: strip(last - 1 - gg, c, False), carry)
        o_ref[...] = acc.astype(o_ref.dtype)

    return pl.pallas_call(
        body,
        grid=(nh, nq),
        in_specs=[
            pl.BlockSpec((TQ, HD), lambda h, i: (i, h)),
            pl.BlockSpec((S, HD), lambda h, i: (0, nh + h)),
            pl.BlockSpec((S, HD), lambda h, i: (0, 2 * nh + h)),
        ],
        out_specs=pl.BlockSpec((TQ, HD), lambda h, i: (i, h)),
        out_shape=jax.ShapeDtypeStruct((S, 2 * W), BF16),
        compiler_params=_cp(("parallel", "arbitrary")),
        name=name,
    )(p, p, p)


def sb_bwd(p, dcat, W, name):
    S = p.shape[0]
    TQ, TK = _query_rows(S), _key_strip(S)
    nh, nq = W // HD, S // TQ
    scale = HD ** -0.5

    def body(q_ref, k_ref, v_ref, do_ref, dp_ref, dk_acc, dv_acc, e_scr, z_scr):
        i = pl.program_id(1)
        q = q_ref[...]
        do = do_ref[...]
        su = (_iota2((HD, HD), 0) > _iota2((HD, HD), 1)).astype(BF16)
        sl = (_iota2((HD, HD), 0) < _iota2((HD, HD), 1)).astype(BF16)
        last = (i * TQ + TQ - 1) // TK

        @pl.when(i == 0)
        def _():
            dk_acc[...] = jnp.zeros_like(dk_acc)
            dv_acc[...] = jnp.zeros_like(dv_acc)

        def pass_a(g, run, masked):
            off = pl.multiple_of(g * TK, TK)
            z, a, run = _sb_strip(q, k_ref[pl.ds(off, TK), :], i * TQ, off, run, su, masked)
            e_scr[g] = a * _dot_nt(do, v_ref[pl.ds(off, TK), :])
            z_scr[g] = z
            dv_acc[pl.ds(off, TK), :] += _dot_tn(a.astype(BF16), do)
            return run

        run = pass_a(last, jnp.zeros((TQ, 1), F32), True)
        lax.fori_loop(0, last, lambda gg, r: pass_a(last - 1 - gg, r, False), run)

        def pass_b(g, carry, masked):
            dq, run_e = carry
            off = pl.multiple_of(g * TK, TK)
            e = e_scr[g]
            z = z_scr[g]
            within = _block_sums(e, sl)
            before = []
            for b in range(len(within)):
                before.append(within[b] + run_e)
                run_e = run_e + jnp.sum(e[:, b * HD : (b + 1) * HD], axis=1, keepdims=True)
            sig = 1.0 / (1.0 + jnp.exp(-z))
            dz = e * (1.0 - sig) - jnp.concatenate(before, axis=1) * sig
            if masked:
                dz = jnp.where(_strip_mask(z.shape, i * TQ, off, True), dz, 0.0)
            dz = (dz * scale).astype(BF16)
            dq = dq + _dot(dz, k_ref[pl.ds(off, TK), :])
            dk_acc[pl.ds(off, TK), :] += _dot_tn(dz, q)
            return dq, run_e

        carry = lax.fori_loop(0, last, lambda g, c: pass_b(g, c, False), (jnp.zeros((TQ, HD), F32), jnp.zeros((TQ, 1), F32)))
        dq, _ = pass_b(last, carry, True)
        dp_ref[0, pl.ds(pl.multiple_of(i * TQ, TQ), TQ), :] = dq.astype(dp_ref.dtype)

        @pl.when(i == nq - 1)
        def _():
            dp_ref[1] = dk_acc[...].astype(dp_ref.dtype)
            dp_ref[2] = dv_acc[...].astype(dp_ref.dtype)

    return pl.pallas_call(
        body,
        grid=(nh, nq),
        in_specs=[
            pl.BlockSpec((TQ, HD), lambda h, i: (i, h)),
            pl.BlockSpec((S, HD), lambda h, i: (0, nh + h)),
            pl.BlockSpec((S, HD), lambda h, i: (0, 2 * nh + h)),
            pl.BlockSpec((TQ, HD), lambda h, i: (i, h)),
        ],
        out_specs=pl.BlockSpec((3, S, HD), lambda h, i: (0, 0, h)),
        out_shape=jax.ShapeDtypeStruct((6, S, W), BF16),
        scratch_shapes=[
            pltpu.VMEM((S, HD), F32),
            pltpu.VMEM((S, HD), F32),
            pltpu.VMEM((S // TK, TQ, TK), F32),
            pltpu.VMEM((S // TK, TQ, TK), F32),
        ],
        compiler_params=_cp(("parallel", "arbitrary")),
        name=name,
    )(p, p, p, dcat)


def fox_gate_fwd(f, b, name):
    S = f.shape[0]
    nq = S // HD

    def body(f_ref, b_ref, c_ref, run):
        i = pl.program_id(0)

        @pl.when(i == 0)
        def _():
            run[...] = jnp.zeros_like(run)

        lf = -_softplus(-(f_ref[...] + b_ref[...]))
        tri = (_iota2((HD, HD), 0) >= _iota2((HD, HD), 1)).astype(BF16)
        c_ref[...] = _dot_ones_left(tri, lf) + run[...]
        run[...] += jnp.sum(lf, axis=0, keepdims=True)

    return pl.pallas_call(
        body,
        grid=(nq,),
        in_specs=[pl.BlockSpec((HD, 128), lambda i: (i, 0)), pl.BlockSpec((1, 128), lambda i: (0, 0))],
        out_specs=pl.BlockSpec((HD, 128), lambda i: (i, 0)),
        out_shape=jax.ShapeDtypeStruct((S, 128), F32),
        scratch_shapes=[pltpu.VMEM((1, 128), F32)],
        compiler_params=_cp(("arbitrary",)),
        name=name,
    )(f, b)


def fox_gate_bwd(f, b, dc, name):
    S = f.shape[0]
    nq = S // HD

    def body(f_ref, b_ref, dc_ref, df_ref, db_ref, run):
        i = pl.program_id(0)

        @pl.when(i == 0)
        def _():
            run[...] = jnp.zeros_like(run)

        dc = dc_ref[...]
        tri = (_iota2((HD, HD), 0) <= _iota2((HD, HD), 1)).astype(BF16)
        dlf = _dot_ones_left(tri, dc) + run[...]
        run[...] += jnp.sum(dc, axis=0, keepdims=True)
        x = f_ref[...] + b_ref[...]
        df = dlf * (1.0 / (1.0 + jnp.exp(x)))
        df_ref[...] = df
        db = jnp.sum(df, axis=0, keepdims=True)

        @pl.when(i == 0)
        def _():
            db_ref[...] = db

        @pl.when(i > 0)
        def _():
            db_ref[...] += db

    rev = pl.BlockSpec((HD, 128), lambda i: (nq - 1 - i, 0))
    vec = pl.BlockSpec((1, 128), lambda i: (0, 0))
    return pl.pallas_call(
        body,
        grid=(nq,),
        in_specs=[rev, vec, rev],
        out_specs=[rev, vec],
        out_shape=[jax.ShapeDtypeStruct((S, 128), F32), jax.ShapeDtypeStruct((1, 128), F32)],
        scratch_shapes=[pltpu.VMEM((1, 128), F32)],
        compiler_params=_cp(("arbitrary",)),
        name=name,
    )(f, b, dc)


def _fox_logits(q, ks, ct, cs, row0, off, masked):
    s = _dot_nt(q, ks) * (HD ** -0.5) + (ct - cs)
    if not masked:
        return s, None
    mask = _strip_mask(s.shape, row0, off, False)
    return jnp.where(mask, s, -1e30), mask


def fox_fwd(p, ccol, crow, cat, W, name):
    S = p.shape[0]
    TQ, TK = _query_rows(S), _key_strip(S)
    nh, nq = W // HD, S // TQ

    def body(q_ref, k_ref, v_ref, cc_ref, cr_ref, cat_ref, o_ref, lse_ref):
        i = pl.program_id(1)
        q = q_ref[...]
        ct = cc_ref[0]

        def step(g, carry, masked):
            m, l, acc = carry
            off = pl.multiple_of(g * TK, TK)
            s, _ = _fox_logits(q, k_ref[pl.ds(off, TK), :], ct, cr_ref[0, pl.ds(g, 1), :], i * TQ, off, masked)
            m_new = jnp.maximum(m, jnp.max(s, axis=1, keepdims=True))
            alpha = jnp.exp(m - m_new)
            pr = jnp.exp(s - m_new)
            l = alpha * l + jnp.sum(pr, axis=1, keepdims=True)
            acc = alpha * acc + _dot(pr.astype(BF16), v_ref[pl.ds(off, TK), :])
            return m_new, l, acc

        init = (jnp.full((TQ, 1), -1e30, F32), jnp.zeros((TQ, 1), F32), jnp.zeros((TQ, HD), F32))
        last = (i * TQ + TQ - 1) // TK
        m, l, acc = step(last, lax.fori_loop(0, last, lambda g, c: step(g, c, False), init), True)
        o_ref[...] = (acc / l).astype(o_ref.dtype)
        lse_ref[0] = m + jnp.log(l)

    return pl.pallas_call(
        body,
        grid=(nh, nq),
        in_specs=[
            pl.BlockSpec((TQ, HD), lambda h, i: (i, 2 * nh + h)),
            pl.BlockSpec((S, HD), lambda h, i: (0, 3 * nh + h)),
            pl.BlockSpec((S, HD), lambda h, i: (0, 4 * nh + h)),
            pl.BlockSpec((1, TQ, 1), lambda h, i: (h, i, 0)),
            pl.BlockSpec((1, S // TK, TK), lambda h, i: (h, 0, 0)),
            pl.BlockSpec(memory_space=pl.ANY),
        ],
        out_specs=[pl.BlockSpec((TQ, HD), lambda h, i: (i, nh + h)), pl.BlockSpec((1, TQ, 1), lambda h, i: (h, i, 0))],
        out_shape=[jax.ShapeDtypeStruct(cat.shape, cat.dtype), jax.ShapeDtypeStruct((nh, S, 1), F32)],
        input_output_aliases={5: 0},
        compiler_params=_cp(("parallel", "arbitrary")),
        name=name,
    )(p, p, p, ccol, crow, cat)


def fox_bwd(p, ccol, crow, cat, lse, dcat, dp, W, name):
    S = p.shape[0]
    TQ, TK = _query_rows(S), _key_strip(S)
    nh, nq = W // HD, S // TQ
    scale = HD ** -0.5

    def body(q_ref, k_ref, v_ref, cc_ref, cr_ref, o_ref, lse_ref, do_ref, dp_in_ref, dp_ref, dcs_ref, dct_ref, dk_acc, dv_acc):
        i = pl.program_id(1)
        q = q_ref[...]
        do = do_ref[...]
        ct = cc_ref[0]
        lse_i = lse_ref[0]
        delta = jnp.sum(do.astype(F32) * o_ref[...].astype(F32), axis=1, keepdims=True)

        @pl.when(i == 0)
        def _():
            dk_acc[...] = jnp.zeros_like(dk_acc)
            dv_acc[...] = jnp.zeros_like(dv_acc)
            dcs_ref[...] = jnp.zeros_like(dcs_ref)

        def step(g, carry, masked):
            dq, dct = carry
            off = pl.multiple_of(g * TK, TK)
            ks = k_ref[pl.ds(off, TK), :]
            s, mask = _fox_logits(q, ks, ct, cr_ref[0, pl.ds(g, 1), :], i * TQ, off, masked)
            pr = jnp.where(mask, jnp.exp(s - lse_i), 0.0) if masked else jnp.exp(s - lse_i)
            ds = pr * (_dot_nt(do, v_ref[pl.ds(off, TK), :]) - delta)
            dv_acc[pl.ds(off, TK), :] += _dot_tn(pr.astype(BF16), do)
            dsb = (ds * scale).astype(BF16)
            dk_acc[pl.ds(off, TK), :] += _dot_tn(dsb, q)
            dcs_ref[0, pl.ds(g, 1), :] += jnp.sum(ds, axis=0, keepdims=True)
            return dq + _dot(dsb, ks), dct + jnp.sum(ds, axis=1, keepdims=True)

        last = (i * TQ + TQ - 1) // TK
        carry = lax.fori_loop(0, last, lambda g, c: step(g, c, False), (jnp.zeros((TQ, HD), F32), jnp.zeros((TQ, 1), F32)))
        dq, dct = step(last, carry, True)
        dp_ref[0, pl.ds(pl.multiple_of(i * TQ, TQ), TQ), :] = dq.astype(dp_ref.dtype)
        dct_ref[0] = dct

        @pl.when(i == nq - 1)
        def _():
            dp_ref[1] = dk_acc[...].astype(dp_ref.dtype)
            dp_ref[2] = dv_acc[...].astype(dp_ref.dtype)

    return pl.pallas_call(
        body,
        grid=(nh, nq),
        in_specs=[
            pl.BlockSpec((TQ, HD), lambda h, i: (i, 2 * nh + h)),
            pl.BlockSpec((S, HD), lambda h, i: (0, 3 * nh + h)),
            pl.BlockSpec((S, HD), lambda h, i: (0, 4 * nh + h)),
            pl.BlockSpec((1, TQ, 1), lambda h, i: (h, i, 0)),
            pl.BlockSpec((1, S // TK, TK), lambda h, i: (h, 0, 0)),
            pl.BlockSpec((TQ, HD), lambda h, i: (i, nh + h)),
            pl.BlockSpec((1, TQ, 1), lambda h, i: (h, i, 0)),
            pl.BlockSpec((TQ, HD), lambda h, i: (i, nh + h)),
            pl.BlockSpec(memory_space=pl.ANY),
        ],
        out_specs=[
            pl.BlockSpec((3, S, HD), lambda h, i: (1, 0, h)),
            pl.BlockSpec((1, S // TK, TK), lambda h, i: (h, 0, 0)),
            pl.BlockSpec((1, TQ, 1), lambda h, i: (h, i, 0)),
        ],
        out_shape=[
            jax.ShapeDtypeStruct(dp.shape, dp.dtype),
            jax.ShapeDtypeStruct((nh, S // TK, TK), F32),
            jax.ShapeDtypeStruct((nh, S, 1), F32),
        ],
        input_output_aliases={8: 0},
        scratch_shapes=[pltpu.VMEM((S, HD), F32), pltpu.VMEM((S, HD), F32)],
        compiler_params=_cp(("parallel", "arbitrary")),
        name=name,
    )(p, p, p, ccol, crow, cat, lse, dcat, dp)


_GELU_K = math.sqrt(2.0 / math.pi)
_GELU_C = 0.044715


def _gelu(x):
    return 0.5 * x * (1.0 + jnp.tanh(_GELU_K * (x + _GELU_C * x * x * x)))


def _gelu_grad(x):
    t = jnp.tanh(_GELU_K * (x + _GELU_C * x * x * x))
    return 0.5 * (1.0 + t) + 0.5 * x * (1.0 - t * t) * (_GELU_K * (1.0 + 3.0 * _GELU_C * x * x))


def _layernorm_parts(gv):
    xc = gv - jnp.mean(gv, axis=-1, keepdims=True)
    r = lax.rsqrt(jnp.mean(xc * xc, axis=-1, keepdims=True) + EPS)
    return xc * r, r


def sg_fwd(p, sg_w, sg_bt, sg_g, W, name):
    S = p.shape[0]
    G, nq = W // HD, S // HD

    def body(u_ref, v_ref, w_ref, bt_ref, g_ref, o_ref):
        xh, _ = _layernorm_parts(_gelu(v_ref[...].astype(F32)))
        vn = (xh * g_ref[...]).astype(BF16)
        tri = _iota2((HD, HD), 0) >= _iota2((HD, HD), 1)
        for gi in range(G):
            cols = slice(gi * HD, (gi + 1) * HD)
            wt = jnp.where(tri, w_ref[gi], 0.0).astype(BF16)
            mixed = _dot(wt, vn[:, cols]) + bt_ref[:, gi : gi + 1]
            o_ref[:, cols] = (_gelu(u_ref[:, cols].astype(F32)) * mixed).astype(o_ref.dtype)

    return pl.pallas_call(
        body,
        grid=(nq,),
        in_specs=[
            pl.BlockSpec((HD, W), lambda i: (i, 0)),
            pl.BlockSpec((HD, W), lambda i: (i, 1)),
            pl.BlockSpec((G, HD, HD), lambda i: (0, 0, 0)),
            pl.BlockSpec((HD, G), lambda i: (0, 0)),
            pl.BlockSpec((1, W), lambda i: (0, 0)),
        ],
        out_specs=pl.BlockSpec((HD, W), lambda i: (i, 0)),
        out_shape=jax.ShapeDtypeStruct((S, 2 * W), BF16),
        compiler_params=_cp(("parallel",)),
        name=name,
    )(p, p, sg_w, sg_bt, sg_g.reshape(1, W))


def sg_bwd(p, sg_w, sg_bt, sg_g, dcat, W, name):
    S = p.shape[0]
    G, nq = W // HD, S // HD

    def body(u_ref, v_ref, w_ref, bt_ref, g_ref, do_ref, dp_ref, dw_ref, dbt_ref, dg_ref, dvn_scr):
        i = pl.program_id(0)

        @pl.when(i == 0)
        def _():
            dw_ref[...] = jnp.zeros_like(dw_ref)
            dbt_ref[...] = jnp.zeros_like(dbt_ref)
            dg_ref[...] = jnp.zeros_like(dg_ref)

        v = v_ref[...].astype(F32)
        xh, r = _layernorm_parts(_gelu(v))
        gg = g_ref[...]
        vn = (xh * gg).astype(BF16)
        tri = _iota2((HD, HD), 0) >= _iota2((HD, HD), 1)
        for gi in range(G):
            cols = slice(gi * HD, (gi + 1) * HD)
            wt = jnp.where(tri, w_ref[gi], 0.0).astype(BF16)
            mixed = _dot(wt, vn[:, cols]) + bt_ref[:, gi : gi + 1]
            u = u_ref[:, cols].astype(F32)
            do = do_ref[:, cols].astype(F32)
            dp_ref[0, :, cols] = (do * mixed * _gelu_grad(u)).astype(dp_ref.dtype)
            dmix = do * _gelu(u)
            dmb = dmix.astype(BF16)
            dw_ref[gi] += jnp.where(tri, _dot_nt(dmb, vn[:, cols]), 0.0)
            dbt_ref[:, gi : gi + 1] += jnp.sum(dmix, axis=1, keepdims=True)
            dvn_scr[:, cols] = _dot_tn(wt, dmb)
        dvn = dvn_scr[...]
        dg_ref[...] += jnp.sum(dvn * xh, axis=0, keepdims=True)
        dxh = dvn * gg
        dgv = r * (dxh - jnp.mean(dxh, axis=-1, keepdims=True) - xh * jnp.mean(dxh * xh, axis=-1, keepdims=True))
        dp_ref[1] = (dgv * _gelu_grad(v)).astype(dp_ref.dtype)

    return pl.pallas_call(
        body,
        grid=(nq,),
        in_specs=[
            pl.BlockSpec((HD, W), lambda i: (i, 0)),
            pl.BlockSpec((HD, W), lambda i: (i, 1)),
            pl.BlockSpec((G, HD, HD), lambda i: (0, 0, 0)),
            pl.BlockSpec((HD, G), lambda i: (0, 0)),
            pl.BlockSpec((1, W), lambda i: (0, 0)),
            pl.BlockSpec((HD, W), lambda i: (i, 0)),
        ],
        out_specs=[
            pl.BlockSpec((2, HD, W), lambda i: (0, i, 0)),
            pl.BlockSpec((G, HD, HD), lambda i: (0, 0, 0)),
            pl.BlockSpec((HD, G), lambda i: (0, 0)),
            pl.BlockSpec((1, W), lambda i: (0, 0)),
        ],
        out_shape=[
            jax.ShapeDtypeStruct((6, S, W), BF16),
            jax.ShapeDtypeStruct((G, HD, HD), F32),
            jax.ShapeDtypeStruct((HD, G), F32),
            jax.ShapeDtypeStruct((1, W), F32),
        ],
        scratch_shapes=[pltpu.VMEM((HD, W), F32)],
        compiler_params=_cp(("arbitrary",)),
        name=name,
    )(p, p, sg_w, sg_bt, sg_g.reshape(1, W), dcat)


def local_step(x, target, wts, at, on_grad):
    S, D = x.shape
    W = D // 2
    nb, F = wts["nb"], wts["F"]
    g = {}

    def ffn_fwd(xin, l):
        h = rms_fwd(xin, wts[f"{l}_ffn_norm_g"], f"{l}_ffn_rms")
        u = mm_nn(h, wts[f"{l}_ffn_up"], nb, f"{l}_ffn_up_mm")
        act = ffn_act_fwd(u, wts[f"{l}_ffn_conv_w"], F, f"{l}_ffn_act")
        half_tile = _pick(S, (512, 256, 128))
        xout = mm_nn(act, wts[f"{l}_ffn_down"], 1, f"{l}_ffn_down_mm", out_dtype=F32, res=xin,
                     tm=half_tile, tn=_pick(D, (512, 256, 128)), tk=F)
        return xout, (xin, h, u, act)

    def ffn_bwd(dxout, dxoutb, saved, l):
        xin, h, u, act = saved
        dact = mm_nt(dxoutb, wts[f"{l}_ffn_down"], 1, S, F, f"{l}_ffn_down_dx", tko=_pick(F, (512, 256, 128)), tn=D)
        dact = on_grad(f"{l}_ffn_down", mm_tn(act, dxoutb, 1, D, f"{l}_ffn_down_dw", tn=D), dact)
        du, dcw = ffn_act_bwd(u, wts[f"{l}_ffn_conv_w"], dact, F, f"{l}_ffn_act_bwd")
        g[f"{l}_ffn_conv_w"] = jnp.concatenate([dcw[0], dcw[1]], axis=1)
        du2 = du.reshape(2 * S, F)
        n = wts[f"{l}_ffn_up"].shape[1]
        tn = _pick(n, (1408, 1024, 768, 512, 256, 128))
        per_half = F // tn
        nt = n // tn

        def up_block(i, j, t):
            vb = j * nt + t
            return vb // per_half, vb % per_half

        tm = _pick(S, (1024, 512, 256, 128))

        def nt_map(i, j, t):
            half, cb = up_block(i, j, t)
            return (half * (S // tm) + i, cb)

        def tn_map(j, t):
            half, cb = up_block(0, j, t)
            return (half, cb)

        dh = mm_nt(du2, wts[f"{l}_ffn_up"], nb, S, D, f"{l}_ffn_up_dx", dy_maps=[nt_map], tm=tm, tko=D, tn=tn)
        dh = on_grad(f"{l}_ffn_up", mm_tn(h, du2, nb, n, f"{l}_ffn_up_dw", dy_maps=[tn_map], tn=tn), dh)
        dxin, dxinb, dg = rms_bwd(xin, wts[f"{l}_ffn_norm_g"], dh, dxout, f"{l}_ffn_rms_bwd")
        g[f"{l}_ffn_norm_g"] = dg
        return dxin, dxinb

    h0 = rms_fwd(x, wts["l0_mix_norm_g"], "l0_mix_rms")
    p0 = mm_nn(h0, wts["l0_w_in"], nb, "l0_w_in_mm")
    cat0 = sb_fwd(p0, W, "l0_sb_fwd")
    cat0 = sc_fwd(p0, wts["l0_sc_conv_w"], cat0, W, "l0_sc_fwd")
    x1 = mm_nn(cat0, wts["l0_w_out"], 1, "l0_w_out_mm", out_dtype=F32, res=x, tm=S, tn=_pick(D, (512, 256, 128)))
    x2, ffn0_saved = ffn_fwd(x1, "l0")

    x2 = at("l1_w_in", x2, None)
    nh = W // HD
    h2 = rms_fwd(x2, wts["l1_mix_norm_g"], "l1_mix_rms")
    p1 = mm_nt(h2, wts["l1_w_in_t"], 1, S, 5 * W, "l1_w_in_mm", tn=D)
    f = mm_nt(h2, wts["l1_w_f_t"], 1, S, 128, "l1_w_f_mm", out_dtype=F32, tn=D)
    bf = jnp.zeros((1, 128), F32).at[0, :nh].set(wts["l1_fox_b_f"])
    c = fox_gate_fwd(f, bf, "l1_fox_gate")
    c_heads = c[:, :nh].T
    ccol = c_heads[:, :, None]
    crow = c_heads.reshape(nh, S // _key_strip(S), _key_strip(S))
    sg_bt = wts["l1_sg_b"].T
    cat1 = sg_fwd(p1, wts["l1_sg_w"], sg_bt, wts["l1_sg_norm_g"], W, "l1_sg_fwd")
    cat1, lse = fox_fwd(p1, ccol, crow, cat1, W, "l1_fox_fwd")
    x3 = mm_nn(cat1, wts["l1_w_out"], 1, "l1_w_out_mm", out_dtype=F32, res=x2, tm=S, tn=_pick(D, (512, 256, 128)))
    x4, ffn1_saved = ffn_fwd(x3, "l1")

    dx4, dx4b, dgf, loss = loss_head(x4, wts["final_norm_g"], target, "loss_head")
    dx4b = at("loss", dx4b, loss)
    g["final_norm_g"] = dgf

    dx3, dx3b = ffn_bwd(dx4, dx4b, ffn1_saved, "l1")
    dcat1 = mm_nt(dx3b, wts["l1_w_out"], 1, S, D, "l1_w_out_dx", tn=D)
    dcat1 = on_grad("l1_w_out", mm_tn(cat1, dx3b, 1, D, "l1_w_out_dw", tn=D), dcat1)
    dp1, dsgw, dsgbt, dsgg = sg_bwd(p1, wts["l1_sg_w"], sg_bt, wts["l1_sg_norm_g"], dcat1, W, "l1_sg_bwd")
    dp1, dcs, dct = fox_bwd(p1, ccol, crow, cat1, lse, dcat1, dp1, W, "l1_fox_bwd")
    g["l1_sg_w"], g["l1_sg_b"], g["l1_sg_norm_g"] = dsgw, dsgbt.T, dsgg
    dc = jnp.zeros((S, 128), F32).at[:, :nh].set((dct[:, :, 0] - dcs.reshape(nh, S)).T)
    df, dbf = fox_gate_bwd(f, bf, dc, "l1_fox_gate_bwd")
    g["l1_fox_b_f"] = dbf[0, :nh]
    dfb = df.astype(BF16)
    tk1 = _pick(W, (1024, 512, 256, 128))
    tx1 = _pick(W, (512, 256, 128))
    tm1 = _pick(S, (1024, 512, 256, 128))
    part_of = lambda pt: pt + pt // 2 - pt // 4

    def a_map1(i, k):
        return (part_of(k // (W // tk1)) * (S // tm1) + i, k % (W // tk1))

    def x_map1(ko):
        return (part_of(ko // (W // tx1)), ko % (W // tx1))

    dp1_2d = dp1.reshape(6 * S, W)
    dw_main = mm_tn(dp1_2d, h2, 1, D, "l1_w_in_dw", tko=tx1, tn=D, x_map=x_map1, x_shape=(S, 5 * W))
    dw_f = mm_tn(dfb, h2, 1, D, "l1_w_f_dw", tn=D)
    dh2 = mm_nn(dfb, wts["l1_w_f_t"], 1, "l1_w_f_dx", out_dtype=F32)
    dh2 = mm_nn(dp1_2d, wts["l1_w_in_t"], 1, "l1_w_in_dx", res=dh2, tm=tm1, tk=tk1, a_map=a_map1, a_shape=(S, 5 * W))
    dh2 = on_grad("l1_w_in", jnp.concatenate([dw_main, dw_f[:nh]], axis=0), dh2)
    dx2, dx2b, dg = rms_bwd(x2, wts["l1_mix_norm_g"], dh2, dx3, "l1_mix_rms_bwd")
    g["l1_mix_norm_g"] = dg

    dx1, dx1b = ffn_bwd(dx2, dx2b, ffn0_saved, "l0")
    dcat0 = mm_nt(dx1b, wts["l0_w_out"], 1, S, D, "l0_w_out_dx", tn=D)
    dcat0 = on_grad("l0_w_out", mm_tn(cat0, dx1b, 1, D, "l0_w_out_dw", tn=D), dcat0)
    dp0 = sb_bwd(p0, dcat0, W, "l0_sb_bwd")
    dp0, dscw = sc_bwd(p0, wts["l0_sc_conv_w"], dcat0, dp0, W, "l0_sc_bwd")
    g["l0_sc_conv_w"] = dscw
    dp0 = at("small_ready", dp0, g)
    n0 = wts["l0_w_in"].shape[1]
    td0 = math.gcd(n0, W)
    nd0 = n0 // td0
    tm0 = _pick(S, (1024, 512, 256, 128))
    per_part0 = W // td0

    def nt_maps0(k):
        def f(i, j, t):
            vb = j * nd0 + k
            return ((vb // per_part0) * (S // tm0) + i, vb % per_part0)
        return f

    def tn_maps0(k):
        def f(j, t):
            vb = j * nd0 + k
            return (vb // per_part0, vb % per_part0)
        return f

    dp0_2d = dp0.reshape(6 * S, W)
    dw0 = mm_tn(h0, dp0_2d, nb, n0, "l0_w_in_dw", dy_maps=[tn_maps0(k) for k in range(nd0)], tn=n0)
    dp0_2d = on_grad("l0_w_in", dw0, dp0_2d)
    dp0_2d = on_grad(None, None, dp0_2d)
    dh0 = mm_nt(dp0_2d, wts["l0_w_in"], nb, S, D, "l0_w_in_dx", dy_maps=[nt_maps0(k) for k in range(nd0)], tm=tm0, tn=n0)
    dh0 = at("small_done", dh0, None)
    dx0, _, dg = rms_bwd(x, wts["l0_mix_norm_g"], dh0, dx1, "l0_mix_rms_bwd")
    g["l0_mix_norm_g"] = dg
    return dx0, g


GATHER_ID = 1


def _place():
    return lax.axis_index("x"), lax.axis_index("y"), lax.axis_index("c")


def _other_chips(x, y):
    return [(x, 1 - y), (1 - x, y), (1 - x, 1 - y)]


def _handshake(peers):
    barrier = pltpu.get_barrier_semaphore()
    for peer in peers:
        pl.semaphore_signal(barrier, inc=1, device_id=peer, device_id_type=MESH)
    pl.semaphore_wait(barrier, len(peers))


UPDATE_LAG = 2


def _on_sequencer(body, out_type, scratch_types, collective_id, name):
    return pl.kernel(
        body,
        out_type=out_type,
        mesh=plsc.ScalarSubcoreMesh(axis_name="seq", num_cores=1),
        scratch_types=scratch_types,
        compiler_params=pltpu.CompilerParams(collective_id=collective_id),
        name=name,
    )


def all_gather(arrs, name):
    n = len(arrs)

    def body(*refs):
        xs, outs = refs[:n], refs[n : 2 * n]
        send_sems, recv_sems, local_sems = refs[2 * n :]
        x, y, c = _place()
        me, sibling = (x, y, c), (x, y, 1 - c)
        chips = _other_chips(x, y)
        _handshake([sibling] + [(*chip, c) for chip in chips])

        def copy(a, k, block, to, src=None):
            px, py, pc = block
            dst = outs[a].at[4 * px + 2 * py + pc]
            return pltpu.make_async_remote_copy(
                src_ref=dst if src is None else src, dst_ref=dst,
                send_sem=send_sems.at[7 * a + k], recv_sem=recv_sems.at[7 * a + k], device_id=to, device_id_type=MESH,
            )

        mine = [pltpu.make_async_copy(xs[a], outs[a].at[4 * x + 2 * y + c], local_sems.at[a]) for a in range(n)]
        for cp in mine:
            cp.start()
        first = []
        for a in range(n):
            first.append(copy(a, 0, me, sibling, src=xs[a]))
            first += [copy(a, 1 + j, me, (*chip, c), src=xs[a]) for j, chip in enumerate(chips)]
        for cp in first:
            cp.start()
        passed = []
        for a in range(n):
            for j, chip in enumerate(chips):
                copy(a, 1 + j, (*chip, c), me).wait_recv()
                cp = copy(a, 4 + j, (*chip, c), sibling)
                cp.start()
                passed.append(cp)
        for a in range(n):
            copy(a, 0, sibling, me).wait_recv()
            for j, chip in enumerate(chips):
                copy(a, 4 + j, (*chip, 1 - c), me).wait_recv()
        for cp in first + passed:
            cp.wait_send()
        for cp in mine:
            cp.wait()

    out_type = [jax.ShapeDtypeStruct((NDEV,) + a.shape, a.dtype) for a in arrs]
    sems = [pltpu.SemaphoreType.DMA((7 * n,)), pltpu.SemaphoreType.DMA((7 * n,)), pltpu.SemaphoreType.DMA((n,))]
    return _on_sequencer(body, out_type, sems, GATHER_ID, name)(*arrs)


def all_gather_direct(arr, name):
    R, C = arr.shape

    def body(x_ref, out_ref, send_sems, recv_sems, local_sem):
        x, y, c = _place()
        me = 4 * x + 2 * y + c

        def copy(k, slot, to):
            return pltpu.make_async_remote_copy(
                src_ref=x_ref, dst_ref=out_ref.at[slot], send_sem=send_sems.at[k], recv_sem=recv_sems.at[k],
                device_id=(to // 4, (to // 2) % 2, to % 2), device_id_type=MESH,
            )

        mine = pltpu.make_async_copy(x_ref, out_ref.at[me], local_sem)
        mine.start()
        sends = [copy(k, me, (me + k + 1) % NDEV) for k in range(NDEV - 1)]
        for cp in sends:
            cp.start()
        for k in range(NDEV - 1):
            sender = (me + NDEV - k - 1) % NDEV
            copy(k, sender, sender).wait_recv()
        for cp in sends:
            cp.wait_send()
        mine.wait()

    vmem = pl.BlockSpec(memory_space=pltpu.VMEM)
    return pl.pallas_call(
        body,
        in_specs=[vmem],
        out_specs=vmem,
        out_shape=jax.ShapeDtypeStruct((NDEV, R, C), arr.dtype),
        scratch_shapes=[pltpu.SemaphoreType.DMA((NDEV - 1,)), pltpu.SemaphoreType.DMA((NDEV - 1,)), pltpu.SemaphoreType.DMA],
        name=name,
    )(arr)


_IN_HBM = pl.BlockSpec(memory_space=pltpu.HBM)
_IN_SEM = pl.BlockSpec(memory_space=pltpu.SEMAPHORE)
_EFFECT = pltpu.SideEffectType.DATAFLOW_SIDE_EFFECTING


def _split_start(make_copies, src, land_shape, nsem, name):
    def body(src_ref, land_ref, send_sems, recv_sems, land_thru, token):
        for cp in make_copies(src_ref, land_ref, send_sems, recv_sems):
            cp.start()
        token[...] = jnp.zeros_like(token)

    send_sems, recv_sems, land_thru, token = pl.pallas_call(
        body,
        name=name,
        out_shape=(
            pltpu.SemaphoreType.DMA((nsem,)), pltpu.SemaphoreType.DMA((nsem,)),
            pltpu.HBM(land_shape, src.dtype), jax.ShapeDtypeStruct((8, 128), F32),
        ),
        in_specs=(_IN_HBM, _IN_HBM),
        out_specs=(_IN_SEM, _IN_SEM, _IN_HBM, pl.BlockSpec(memory_space=pltpu.VMEM)),
        input_output_aliases={1: 2},
        compiler_params=pltpu.CompilerParams(has_side_effects=_EFFECT),
    )(src, pltpu.with_memory_space_constraint(lax.empty(land_shape, src.dtype), pltpu.HBM))
    return send_sems, recv_sems, src, land_thru, token


def _split_wait(make_copies, send_sems, recv_sems, src_thru, land_thru, after, name):
    def body(src_ref, land_ref, send_sems, recv_sems, after_ref, land_out):
        for cp in make_copies(src_ref, land_ref, send_sems, recv_sems):
            cp.wait_send()
            cp.wait_recv()

    return pl.pallas_call(
        body,
        name=name,
        out_shape=pltpu.HBM(land_thru.shape, land_thru.dtype),
        in_specs=(_IN_HBM, _IN_HBM, _IN_SEM, _IN_SEM, pl.BlockSpec(memory_space=pl.ANY)),
        out_specs=_IN_HBM,
        input_output_aliases={1: 0},
        compiler_params=pltpu.CompilerParams(has_side_effects=_EFFECT),
    )(src_thru, land_thru, send_sems, recv_sems, after)


def _pair_copies(src_ref, land_ref, send_sems, recv_sems):
    x, y, c = _place()
    return [
        pltpu.make_async_remote_copy(
            src_ref=src_ref.at[k, 1 - c], dst_ref=land_ref.at[k],
            send_sem=send_sems.at[k], recv_sem=recv_sems.at[k], device_id=(x, y, 1 - c), device_id_type=MESH,
        )
        for k in range(4)
    ]


def _chip_copies(src_ref, land_ref, send_sems, recv_sems):
    x, y, c = _place()
    return [
        pltpu.make_async_remote_copy(
            src_ref=src_ref.at[2 * px + py], dst_ref=land_ref.at[2 * x + y],
            send_sem=send_sems.at[j], recv_sem=recv_sems.at[j], device_id=(px, py, c), device_id_type=MESH,
        )
        for j, (px, py) in enumerate(_other_chips(x, y))
    ]


def _row_tile(R, C, max_elems):
    if R * C <= max_elems:
        return R
    best = None
    for tr in range(16, R, 16):
        if R % tr == 0 and tr * C <= max_elems:
            best = tr
    return best or R


def pair_sum(a42, land4, core, name):
    _, _, R, C = a42.shape
    tr = _row_tile(R, C, 1 << 20)

    def body(core_ref, a_ref, l_ref, o_ref):
        o_ref[...] = (a_ref[0].astype(F32) + l_ref[...].astype(F32)).astype(o_ref.dtype)

    return pl.pallas_call(
        body,
        grid_spec=pltpu.PrefetchScalarGridSpec(
            num_scalar_prefetch=1,
            grid=(4, R // tr),
            in_specs=[
                pl.BlockSpec((1, 1, tr, C), lambda k, r, core_ref: (k, core_ref[0], r, 0)),
                pl.BlockSpec((1, tr, C), lambda k, r, core_ref: (k, r, 0)),
            ],
            out_specs=pl.BlockSpec((1, tr, C), lambda k, r, core_ref: (k, r, 0)),
        ),
        out_shape=jax.ShapeDtypeStruct((4, R, C), BF16),
        compiler_params=_cp(("parallel", "parallel")),
        name=name,
    )(core, a42, land4)


def sum_slots(parts, name):
    P, R, C = parts.shape

    def body(p_ref, o_ref):
        acc = p_ref[0].astype(F32)
        for k in range(1, P):
            acc = acc + p_ref[k].astype(F32)
        o_ref[...] = acc

    tr = _row_tile(R, P * C, 1 << 21)
    return pl.pallas_call(
        body,
        grid=(R // tr,),
        in_specs=[pl.BlockSpec((P, tr, C), lambda r: (0, r, 0))],
        out_specs=pl.BlockSpec((tr, C), lambda r: (r, 0)),
        out_shape=jax.ShapeDtypeStruct((R, C), F32),
        compiler_params=_cp(("parallel",)),
        name=name,
    )(parts)


def adamw(w, m, v, parts, name):
    R, C = w.shape
    P = parts.shape[0]
    tr = _pick(R, (256, 128, 64, 32, 16, 8))
    c1 = 1.0 - ADAM_B1 ** ADAM_STEP
    c2 = 1.0 - ADAM_B2 ** ADAM_STEP

    def body(w_ref, m_ref, v_ref, p_ref, g_ref, d_ref, nm_ref, nv_ref):
        g = p_ref[0].astype(F32)
        for k in range(1, P):
            g = g + p_ref[k].astype(F32)
        nm = ADAM_B1 * m_ref[...] + (1.0 - ADAM_B1) * g
        nv = ADAM_B2 * v_ref[...] + (1.0 - ADAM_B2) * (g * g)
        g_ref[...] = g
        nm_ref[...] = nm
        nv_ref[...] = nv
        d_ref[...] = -ADAM_LR * ((nm / c1) / (jnp.sqrt(nv / c2) + ADAM_EPS) + ADAM_WD * w_ref[...])

    blk = pl.BlockSpec((tr, C), lambda r: (r, 0))
    shp = jax.ShapeDtypeStruct((R, C), F32)
    return pl.pallas_call(
        body,
        grid=(R // tr,),
        in_specs=[blk, blk, blk, pl.BlockSpec((P, tr, C), lambda r: (0, r, 0))],
        out_specs=[blk, blk, blk, blk],
        out_shape=[shp, shp, shp, shp],
        compiler_params=_cp(("parallel",)),
        name=name,
    )(w, m, v, parts)


def adamw_reduced(w, m, v, own, land, chip, name):
    R, C = w.shape
    if R % 8 == 0:
        tr, tc = _pick(R, (256, 128, 64, 32, 16, 8)), C
    else:
        tr, tc = R, _pick(C, (256, 128))
    c1 = 1.0 - ADAM_B1 ** ADAM_STEP
    c2 = 1.0 - ADAM_B2 ** ADAM_STEP

    def body(chip_ref, w_ref, m_ref, v_ref, own_ref, land_ref, g_ref, d_ref, nm_ref, nv_ref):
        mine = own_ref[0].astype(F32)
        g = None
        for k in range(4):
            term = jnp.where(chip_ref[0] == k, mine, land_ref[k].astype(F32))
            g = term if g is None else g + term
        nm = ADAM_B1 * m_ref[...] + (1.0 - ADAM_B1) * g
        nv = ADAM_B2 * v_ref[...] + (1.0 - ADAM_B2) * (g * g)
        g_ref[...] = g
        nm_ref[...] = nm
        nv_ref[...] = nv
        d_ref[...] = -ADAM_LR * ((nm / c1) / (jnp.sqrt(nv / c2) + ADAM_EPS) + ADAM_WD * w_ref[...])

    blk = pl.BlockSpec((tr, tc), lambda r, c, chip_ref: (r, c))
    shp = jax.ShapeDtypeStruct((R, C), F32)
    return pl.pallas_call(
        body,
        grid_spec=pltpu.PrefetchScalarGridSpec(
            num_scalar_prefetch=1,
            grid=(R // tr, C // tc),
            in_specs=[
                blk, blk, blk,
                pl.BlockSpec((1, tr, tc), lambda r, c, chip_ref: (chip_ref[0], r, c)),
                pl.BlockSpec((4, tr, tc), lambda r, c, chip_ref: (0, r, c)),
            ],
            out_specs=[blk, blk, blk, blk],
        ),
        out_shape=[shp, shp, shp, shp],
        compiler_params=_cp(("parallel", "parallel")),
        name=name,
    )(chip, w, m, v, own, land)


_WEIGHTS = [
    "l0_mix_norm_g", "l0_w_in", "l0_sc_conv_w", "l0_w_out", "l0_ffn_norm_g", "l0_ffn_up", "l0_ffn_conv_w", "l0_ffn_down",
    "l1_mix_norm_g", "l1_w_in", "l1_fox_b_f", "l1_sg_w", "l1_sg_b", "l1_sg_norm_g", "l1_w_out", "l1_ffn_norm_g",
    "l1_ffn_up", "l1_ffn_conv_w", "l1_ffn_down", "final_norm_g",
]
_ROW_SHARDED = ["l0_w_out", "l0_ffn_down", "l1_w_out", "l1_ffn_down"]
_BIG = ["l0_w_in", "l0_w_out", "l0_ffn_up", "l0_ffn_down", "l1_w_in", "l1_w_out", "l1_ffn_up", "l1_ffn_down"]
_CONV = ["l0_sc_conv_w", "l0_ffn_conv_w", "l1_ffn_conv_w"]
_SMALL = [n for n in _WEIGHTS if n not in _BIG]
_LAST_SMALL = "l0_mix_norm_g"
_PACK_ROWS = 8


def _pack(arrs):
    flat = []
    for a in arrs:
        v = a.reshape(-1).astype(F32)
        pad = (-v.shape[0]) % (_PACK_ROWS * 128)
        flat.append(jnp.pad(v, (0, pad)))
    return jnp.concatenate(flat).reshape(-1, 128)


def _unpack(packed, shapes):
    out, off = [], 0
    flat = packed.reshape(-1)
    for shp in shapes:
        size = math.prod(shp)
        out.append(flat[off : off + size].reshape(shp))
        off += size + (-size) % (_PACK_ROWS * 128)
    return out


def kernel(x, l0_mix_norm_g, l0_w_in, l0_sc_conv_w, l0_w_out, l0_ffn_norm_g, l0_ffn_up, l0_ffn_conv_w, l0_ffn_down, l1_mix_norm_g, l1_w_in, l1_fox_b_f, l1_sg_w, l1_sg_b, l1_sg_norm_g, l1_w_out, l1_ffn_norm_g, l1_ffn_up, l1_ffn_conv_w, l1_ffn_down, final_norm_g, loss_target, m_l0_mix_norm_g, m_l0_w_in, m_l0_sc_conv_w, m_l0_w_out, m_l0_ffn_norm_g, m_l0_ffn_up, m_l0_ffn_conv_w, m_l0_ffn_down, m_l1_mix_norm_g, m_l1_w_in, m_l1_fox_b_f, m_l1_sg_w, m_l1_sg_b, m_l1_sg_norm_g, m_l1_w_out, m_l1_ffn_norm_g, m_l1_ffn_up, m_l1_ffn_conv_w, m_l1_ffn_down, m_final_norm_g, v_l0_mix_norm_g, v_l0_w_in, v_l0_sc_conv_w, v_l0_w_out, v_l0_ffn_norm_g, v_l0_ffn_up, v_l0_ffn_conv_w, v_l0_ffn_down, v_l1_mix_norm_g, v_l1_w_in, v_l1_fox_b_f, v_l1_sg_w, v_l1_sg_b, v_l1_sg_norm_g, v_l1_w_out, v_l1_ffn_norm_g, v_l1_ffn_up, v_l1_ffn_conv_w, v_l1_ffn_down, v_final_norm_g):
    given = dict(locals())
    w = {n: given[n] for n in _WEIGHTS}
    mom = {n: given["m_" + n] for n in _WEIGHTS}
    var = {n: given["v_" + n] for n in _WEIGHTS}
    xs, target = x[0], loss_target[0]
    S, D = xs.shape
    W = D // 2
    nh = W // HD
    cx, cy, cc = _place()
    me = 4 * cx + 2 * cy + cc

    wts = {"nb": NDEV, "F": l0_ffn_down.shape[0] * NDEV}
    for n in _SMALL:
        if n not in _CONV:
            wts[n] = w[n]
    gathered, loss_sum = {}, []

    def start_gather(n):
        src = w[n].T if n == "l1_w_in" else w[n]
        got = all_gather([src.astype(BF16)] + ([w[c] for c in _CONV] if n == _BIG[0] else []), f"gather_{n}")
        if n == "l1_w_in":
            gathered[n] = got[0]
        elif n in _ROW_SHARDED:
            wts[n] = got[0].reshape(-1, D)
        else:
            wts[n] = got[0].reshape(NDEV * D, -1)
        for c, taps in zip(_CONV, got[1:]):
            wts[c] = taps.transpose(1, 0, 2).reshape(CONV_K, -1)

    def at(point, after, value):
        if point == "l1_w_in":
            got, after = lax.optimization_barrier((gathered[point], after))
            wts["l1_w_in_t"] = got.reshape(-1, D)
            wts["l1_w_f_t"] = jnp.pad(wts["l1_w_in_t"][5 * W :], ((0, 128 - nh), (0, 0)))
        elif point == "loss":
            total, after = lax.optimization_barrier((lax.psum(value[0, 0], ("x", "y", "c")), after))
            loss_sum.append(total)
        elif point == "small_ready":
            early = [n for n in _SMALL if n != _LAST_SMALL]
            gathered["small"] = all_gather([_pack([value[n] for n in early])], "gather_small_grads")[0]
        elif point == "small_done":
            after = update_small([n for n in _SMALL if n != _LAST_SMALL], gathered["small"], "small", after)
        return after

    out_g, out_d, out_m, out_v = {}, {}, {}, {}

    def update_small(names, all_terms, tag, after=None):
        shapes = [w[n].shape for n in names]
        full_shapes = [(CONV_K, NDEV * w[n].shape[1]) if n in _CONV else w[n].shape for n in names]
        grads = {}
        for n, t in zip(names, _unpack(sum_slots(all_terms, f"sum_{tag}_grads"), full_shapes)):
            if n in _CONV:
                cols = w[n].shape[1]
                t = lax.dynamic_slice_in_dim(t, me * cols, cols, axis=1)
            grads[n] = t
        res = adamw(
            _pack([w[n] for n in names]), _pack([mom[n] for n in names]), _pack([var[n] for n in names]),
            _pack([grads[n] for n in names])[None], f"adamw_{tag}",
        )
        if after is not None:
            res, after = lax.optimization_barrier((res, after))
        for dst, packed_out in zip((out_g, out_d, out_m, out_v), res):
            for n, t in zip(names, _unpack(packed_out, shapes)):
                dst[n] = t
        return after

    core = jnp.reshape(cc, (1,)).astype(jnp.int32)
    chip = jnp.reshape(2 * cx + cy, (1,)).astype(jnp.int32)
    pair_flying, chip_flying = [], []

    def tie(value, after):
        if after is None:
            return value, None
        return lax.optimization_barrier((value, after))

    def to_chips(after):
        n, flying = pair_flying.pop()
        landed = _split_wait(_pair_copies, *flying, f"reduce_pair_wait_{n}")
        summed = pair_sum(flying[2], landed, core, f"pair_sum_{n}")
        *flying, token = _split_start(_chip_copies, summed, summed.shape, 3, f"reduce_chips_{n}")
        token, after = tie(token, after)
        chip_flying.append((n, flying + [token]))
        return after

    def update(after, behind=None):
        n, flying = chip_flying.pop(0)
        if behind is not None:
            flying[4], _ = lax.optimization_barrier((flying[4], behind))
        landed = _split_wait(_chip_copies, *flying, f"reduce_chips_wait_{n}")
        turn = (lambda t: t.T) if n == "l1_w_in" else (lambda t: t)
        res = adamw_reduced(turn(w[n]), turn(mom[n]), turn(var[n]), flying[2], landed, chip, f"adamw_{n}")
        res, after = tie(res, after)
        out_g[n], out_d[n], out_m[n], out_v[n] = [turn(t) for t in res]
        return after, res[0]

    def on_grad(n, term, after):
        if n is None:
            return to_chips(after)
        if n in _ROW_SHARDED or n == "l1_w_in":
            term = term.reshape(NDEV, -1, D)
        else:
            term = term.reshape(NDEV, D, -1)
        term = term.reshape((4, 2) + term.shape[1:])
        *flying, token = _split_start(_pair_copies, term, term.shape[:1] + term.shape[2:], 4, f"reduce_pair_{n}")
        token, after = tie(token, after)
        if len(chip_flying) == UPDATE_LAG:
            after, _ = update(after)
        if pair_flying:
            after = to_chips(after)
        pair_flying.append((n, flying + [token]))
        return after

    for n in _BIG:
        start_gather(n)
    dx, g = local_step(xs, target, wts, at, on_grad)
    update_small([_LAST_SMALL], all_gather_direct(_pack([g[_LAST_SMALL]]), "gather_last_grad"), "last")
    done = out_g[_LAST_SMALL]
    while chip_flying:
        _, done = update(None, behind=done)
    loss = loss_sum[0]

    return (loss, dx[None], *[out_g[n] for n in _WEIGHTS], *[out_d[n] for n in _WEIGHTS],
            *[out_m[n] for n in _WEIGHTS], *[out_v[n] for n in _WEIGHTS])
```

```python
import functools
import math

import jax
import jax.numpy as jnp
from jax import lax
from jax.experimental import pallas as pl
from jax.experimental.pallas import tpu as pltpu
from jax.experimental.pallas import tpu_sc as plsc

F32 = jnp.float32
BF16 = jnp.bfloat16
HD = 128
EPS = 1e-6
CONV_K = 3
VMEM_LIMIT_BYTES = 48 << 20
NDEV = 8
MESH = pl.DeviceIdType.MESH

ADAM_LR = 0.001
ADAM_B1 = 0.9
ADAM_B2 = 0.999
ADAM_EPS = 1e-08
ADAM_WD = 0.01
ADAM_STEP = 10


def _cp(sem):
    return pltpu.CompilerParams(dimension_semantics=sem, vmem_limit_bytes=VMEM_LIMIT_BYTES)


def _pick(n, prefs):
    for p in prefs:
        if n % p == 0:
            return p
    return n


def _dot(a, b):
    return jnp.dot(a, b, preferred_element_type=F32)


def _dot_nt(a, b):
    return lax.dot_general(a, b, (((1,), (1,)), ((), ())), preferred_element_type=F32)


def _dot_tn(a, b):
    return lax.dot_general(a, b, (((0,), (0,)), ((), ())), preferred_element_type=F32)


def _split3(x):
    hi = x.astype(BF16)
    r = x - hi.astype(F32)
    mid = r.astype(BF16)
    lo = (r - mid.astype(F32)).astype(BF16)
    return hi, mid, lo


def _dot_ones_left(ones_bf16, x):
    hi, mid, lo = _split3(x)
    return _dot(ones_bf16, hi) + _dot(ones_bf16, mid) + _dot(ones_bf16, lo)


def _iota2(shape, axis):
    return lax.broadcasted_iota(jnp.int32, shape, axis)


def mm_nn(a, w2d, nb, name, out_dtype=BF16, res=None, tm=None, tn=None, tk=None, a_map=None, a_shape=None):
    M, K = a_shape or a.shape
    n = w2d.shape[1]
    assert w2d.shape[0] == nb * K or (nb == 1 and w2d.shape[0] > K)
    a_map = a_map or (lambda i, k: (i, k))
    tm = tm or _pick(M, (1024, 512, 256, 128))
    tn = tn or _pick(n, (1408, 1024, 768, 512, 256, 128))
    tk = tk or (K if K <= 2048 else _pick(K, (1408, 1024, 512, 256, 128)))
    nk, nt = K // tk, n // tn
    has_res = res is not None

    def body(*refs):
        if has_res:
            a_ref, w_ref, r_ref, o_ref = refs[:4]
        else:
            a_ref, w_ref, o_ref = refs[:3]
            r_ref = None
        part = _dot(a_ref[...], w_ref[...])

        def finish(acc):
            if r_ref is not None:
                acc = acc + r_ref[...].astype(F32)
            o_ref[...] = acc.astype(o_ref.dtype)

        if nk == 1:
            finish(part)
        else:
            acc_ref = refs[-1]
            k = pl.program_id(3)

            @pl.when(k == 0)
            def _():
                acc_ref[...] = part

            @pl.when(k > 0)
            def _():
                acc_ref[...] += part

            @pl.when(k == nk - 1)
            def _():
                finish(acc_ref[...])

    in_specs = [
        pl.BlockSpec((tm, tk), lambda i, j, t, k: a_map(i, k)),
        pl.BlockSpec((tk, tn), lambda i, j, t, k: (j * nk + k, t)),
    ]
    args = [a, w2d]
    out_spec = pl.BlockSpec((tm, tn), lambda i, j, t, k: (i, j * nt + t))
    if has_res:
        in_specs.append(out_spec)
        args.append(res)
    return pl.pallas_call(
        body,
        grid=(M // tm, nb, nt, nk),
        in_specs=in_specs,
        out_specs=out_spec,
        out_shape=jax.ShapeDtypeStruct((M, nb * n), out_dtype),
        scratch_shapes=[pltpu.VMEM((tm, tn), F32)] if nk > 1 else [],
        compiler_params=_cp(("parallel", "parallel", "parallel", "arbitrary")),
        name=name,
    )(*args)


def mm_nt(dy2d, w2d, nb, M, K, name, out_dtype=BF16, res=None, dy_maps=None, tm=None, tko=None, tn=None):
    n = w2d.shape[1]
    assert w2d.shape[0] == nb * K or (nb == 1 and w2d.shape[0] > K)
    tm = tm or _pick(M, (1024, 512, 256, 128))
    tko = tko or _pick(K, (1024, 512, 256, 128))
    tn = tn or _pick(n, (1408, 1024, 768, 512, 256, 128))
    nt, nko = n // tn, K // tko
    has_res = res is not None
    if dy_maps is None:
        dy_maps = [lambda i, j, t: (i, j * nt + t)]
    nd = len(dy_maps)
    td = tn // nd

    one_step = nb * nt == 1

    def body(*refs):
        d_refs, w_ref = refs[:nd], refs[nd]
        r_ref = refs[nd + 1] if has_res else None
        d = d_refs[0][...] if nd == 1 else jnp.concatenate([r[...] for r in d_refs], axis=1)
        part = _dot_nt(d, w_ref[...])
        if one_step:
            o_ref = refs[-1]
            if r_ref is not None:
                part = part + r_ref[...].astype(F32)
            o_ref[...] = part.astype(o_ref.dtype)
            return
        o_ref, acc_ref = refs[-2], refs[-1]
        j, t = pl.program_id(2), pl.program_id(3)
        first = jnp.logical_and(j == 0, t == 0)
        last = jnp.logical_and(j == nb - 1, t == nt - 1)

        @pl.when(first)
        def _():
            acc_ref[...] = part

        @pl.when(jnp.logical_not(first))
        def _():
            acc_ref[...] += part

        @pl.when(last)
        def _():
            acc = acc_ref[...]
            if r_ref is not None:
                acc = acc + r_ref[...].astype(F32)
            o_ref[...] = acc.astype(o_ref.dtype)

    in_specs = [pl.BlockSpec((tm, td), functools.partial(lambda f, i, ko, j, t: f(i, j, t), f)) for f in dy_maps]
    in_specs.append(pl.BlockSpec((tko, tn), lambda i, ko, j, t: (j * nko + ko, t)))
    args = [dy2d] * nd + [w2d]
    out_spec = pl.BlockSpec((tm, tko), lambda i, ko, j, t: (i, ko))
    if has_res:
        in_specs.append(out_spec)
        args.append(res)
    return pl.pallas_call(
        body,
        grid=(M // tm, nko, nb, nt),
        in_specs=in_specs,
        out_specs=out_spec,
        out_shape=jax.ShapeDtypeStruct((M, K), out_dtype),
        scratch_shapes=[] if one_step else [pltpu.VMEM((tm, tko), F32)],
        compiler_params=_cp(("parallel", "parallel", "arbitrary", "arbitrary")),
        name=name,
    )(*args)


def mm_tn(x, dy2d, nb, n, name, out_dtype=BF16, dy_maps=None, tko=None, tn=None, x_map=None, x_shape=None):
    S, K = x_shape or x.shape
    x_map = x_map or (lambda ko: (0, ko))
    tko = tko or _pick(K, (512, 256, 128))
    tn = tn or _pick(n, (1408, 1024, 768, 512, 256, 128))
    nt, nko = n // tn, K // tko
    if dy_maps is None:
        dy_maps = [lambda j, t: (0, j * nt + t)]
    nd = len(dy_maps)
    td = tn // nd

    def body(*refs):
        x_ref, d_refs, o_ref = refs[0], refs[1 : 1 + nd], refs[-1]
        d = d_refs[0][...] if nd == 1 else jnp.concatenate([r[...] for r in d_refs], axis=1)
        o_ref[...] = _dot_tn(x_ref[...], d).astype(o_ref.dtype)

    in_specs = [pl.BlockSpec((S, tko), lambda ko, j, t: x_map(ko))]
    in_specs += [pl.BlockSpec((S, td), functools.partial(lambda f, ko, j, t: f(j, t), f)) for f in dy_maps]
    return pl.pallas_call(
        body,
        grid=(nko, nb, nt),
        in_specs=in_specs,
        out_specs=pl.BlockSpec((tko, tn), lambda ko, j, t: (j * nko + ko, t)),
        out_shape=jax.ShapeDtypeStruct((nb * K, n), out_dtype),
        compiler_params=_cp(("parallel", "parallel", "parallel")),
        name=name,
    )(x, *([dy2d] * nd))


def rms_fwd(x, g, name):
    S, D = x.shape
    tm = _pick(S, (256, 128))

    def body(x_ref, g_ref, o_ref):
        xf = x_ref[...]
        r = lax.rsqrt(jnp.mean(xf * xf, axis=-1, keepdims=True) + EPS)
        o_ref[...] = (xf * r * g_ref[...]).astype(o_ref.dtype)

    return pl.pallas_call(
        body,
        grid=(S // tm,),
        in_specs=[pl.BlockSpec((tm, D), lambda i: (i, 0)), pl.BlockSpec((1, D), lambda i: (0, 0))],
        out_specs=pl.BlockSpec((tm, D), lambda i: (i, 0)),
        out_shape=jax.ShapeDtypeStruct((S, D), BF16),
        compiler_params=_cp(("parallel",)),
        name=name,
    )(x, g.reshape(1, D))


def rms_bwd(x, g, dh, dres, name):
    S, D = x.shape
    tm = _pick(S, (256, 128))

    def body(x_ref, g_ref, dh_ref, dr_ref, dx_ref, dxb_ref, dg_ref):
        i = pl.program_id(0)
        xf = x_ref[...]
        dh = dh_ref[...].astype(F32)
        r = lax.rsqrt(jnp.mean(xf * xf, axis=-1, keepdims=True) + EPS)
        gy = dh * g_ref[...]
        proj = jnp.mean(gy * xf, axis=-1, keepdims=True)
        dx = dr_ref[...] + r * gy - xf * (r * r * r * proj)
        dx_ref[...] = dx
        dxb_ref[...] = dx.astype(BF16)
        dg = jnp.sum(dh * (xf * r), axis=0, keepdims=True)

        @pl.when(i == 0)
        def _():
            dg_ref[...] = dg

        @pl.when(i > 0)
        def _():
            dg_ref[...] += dg

    row = pl.BlockSpec((tm, D), lambda i: (i, 0))
    vec = pl.BlockSpec((1, D), lambda i: (0, 0))
    return pl.pallas_call(
        body,
        grid=(S // tm,),
        in_specs=[row, vec, row, row],
        out_specs=[row, row, vec],
        out_shape=[jax.ShapeDtypeStruct((S, D), F32), jax.ShapeDtypeStruct((S, D), BF16), jax.ShapeDtypeStruct((1, D), F32)],
        compiler_params=_cp(("arbitrary",)),
        name=name,
    )(x, g.reshape(1, D), dh, dres)


def loss_head(x, g, target, name):
    S, D = x.shape
    tm = _pick(S, (256, 128))

    def body(x_ref, g_ref, t_ref, dx_ref, dxb_ref, dg_ref, loss_ref):
        i = pl.program_id(0)
        xf = x_ref[...]
        gg = g_ref[...]
        r = lax.rsqrt(jnp.mean(xf * xf, axis=-1, keepdims=True) + EPS)
        xh = xf * r
        err = xh * gg - t_ref[...]
        part = (0.5 / D) * jnp.sum(err * err)
        dy = err * (1.0 / D)
        gy = dy * gg
        proj = jnp.mean(gy * xf, axis=-1, keepdims=True)
        dx = r * gy - xf * (r * r * r * proj)
        dx_ref[...] = dx
        dxb_ref[...] = dx.astype(BF16)
        dg = jnp.sum(dy * xh, axis=0, keepdims=True)
        lossb = jnp.full(loss_ref.shape, part, F32)

        @pl.when(i == 0)
        def _():
            dg_ref[...] = dg
            loss_ref[...] = lossb

        @pl.when(i > 0)
        def _():
            dg_ref[...] += dg
            loss_ref[...] += lossb

    row = pl.BlockSpec((tm, D), lambda i: (i, 0))
    vec = pl.BlockSpec((1, D), lambda i: (0, 0))
    return pl.pallas_call(
        body,
        grid=(S // tm,),
        in_specs=[row, vec, row],
        out_specs=[row, row, vec, pl.BlockSpec((8, 128), lambda i: (0, 0))],
        out_shape=[
            jax.ShapeDtypeStruct((S, D), F32),
            jax.ShapeDtypeStruct((S, D), BF16),
            jax.ShapeDtypeStruct((1, D), F32),
            jax.ShapeDtypeStruct((8, 128), F32),
        ],
        compiler_params=_cp(("arbitrary",)),
        name=name,
    )(x, g.reshape(1, D), target)


def _shift_down(s, k):
    if k == 0:
        return s
    return jnp.where(_iota2(s.shape, 0) >= k, pltpu.roll(s, k, axis=0), 0.0)


def _shift_up(s, k):
    if k == 0:
        return s
    n = s.shape[0]
    return jnp.where(_iota2(s.shape, 0) < n - k, pltpu.roll(s, n - k, axis=0), 0.0)


def _conv(s, w):
    return w[0:1] * _shift_down(s, 2) + w[1:2] * _shift_down(s, 1) + w[2:3] * s


def _conv_t(d, w):
    return w[2:3] * d + w[1:2] * _shift_up(d, 1) + w[0:1] * _shift_up(d, 2)


def _conv_dw(d, s):
    return [jnp.sum(d * _shift_down(s, CONV_K - 1 - k), axis=0, keepdims=True) for k in range(CONV_K)]


def sc_fwd(p, convw, cat, W, name):
    S = p.shape[0]
    tc = _pick(W, (256, 128))
    nc = W // tc

    def body(gb_ref, gc_ref, hi_ref, w_ref, cat_ref, o_ref):
        s = gc_ref[...].astype(F32) * hi_ref[...].astype(F32)
        o_ref[...] = (gb_ref[...].astype(F32) * _conv(s, w_ref[...])).astype(o_ref.dtype)

    col = lambda part: pl.BlockSpec((S, tc), lambda c: (0, part * nc + c))
    return pl.pallas_call(
        body,
        grid=(nc,),
        in_specs=[col(3), col(4), col(5), pl.BlockSpec((CONV_K, tc), lambda c: (0, c)), pl.BlockSpec(memory_space=pl.ANY)],
        out_specs=col(1),
        out_shape=jax.ShapeDtypeStruct(cat.shape, cat.dtype),
        input_output_aliases={4: 0},
        compiler_params=_cp(("parallel",)),
        name=name,
    )(p, p, p, convw, cat)


def sc_bwd(p, convw, dcat, dp, W, name):
    S = p.shape[0]
    tc = _pick(W, (256, 128))
    nc = W // tc

    def body(gb_ref, gc_ref, hi_ref, w_ref, do_ref, dp_in_ref, dp_ref, dw_ref):
        gb = gb_ref[...].astype(F32)
        gc = gc_ref[...].astype(F32)
        hi = hi_ref[...].astype(F32)
        w = w_ref[...]
        do = do_ref[...].astype(F32)
        s = gc * hi
        dcs = do * gb
        ds = _conv_t(dcs, w)
        dp_ref[0] = (do * _conv(s, w)).astype(dp_ref.dtype)
        dp_ref[1] = (ds * hi).astype(dp_ref.dtype)
        dp_ref[2] = (ds * gc).astype(dp_ref.dtype)
        for k, row in enumerate(_conv_dw(dcs, s)):
            dw_ref[k : k + 1, :] = row

    col = lambda part: pl.BlockSpec((S, tc), lambda c: (0, part * nc + c))
    return pl.pallas_call(
        body,
        grid=(nc,),
        in_specs=[
            col(3), col(4), col(5),
            pl.BlockSpec((CONV_K, tc), lambda c: (0, c)),
            pl.BlockSpec((S, tc), lambda c: (0, nc + c)),
            pl.BlockSpec(memory_space=pl.ANY),
        ],
        out_specs=[pl.BlockSpec((3, S, tc), lambda c: (1, 0, c)), pl.BlockSpec((CONV_K, tc), lambda c: (0, c))],
        out_shape=[jax.ShapeDtypeStruct(dp.shape, dp.dtype), jax.ShapeDtypeStruct((CONV_K, W), F32)],
        input_output_aliases={5: 0},
        compiler_params=_cp(("parallel",)),
        name=name,
    )(p, p, p, convw, dcat, dp)


def _silu_parts(a):
    sig = 1.0 / (1.0 + jnp.exp(-a))
    return a * sig, sig


def ffn_act_fwd(u, convw, F, name):
    S = u.shape[0]
    tc = _pick(F, (256, 128))
    nc = F // tc

    def body(ug_ref, uu_ref, wg_ref, wu_ref, o_ref):
        ag = _conv(ug_ref[...].astype(F32), wg_ref[...])
        au = _conv(uu_ref[...].astype(F32), wu_ref[...])
        o_ref[...] = (_silu_parts(ag)[0] * au).astype(o_ref.dtype)

    col = lambda half: pl.BlockSpec((S, tc), lambda c: (0, half * nc + c))
    wcol = lambda half: pl.BlockSpec((CONV_K, tc), lambda c: (0, half * nc + c))
    return pl.pallas_call(
        body,
        grid=(nc,),
        in_specs=[col(0), col(1), wcol(0), wcol(1)],
        out_specs=pl.BlockSpec((S, tc), lambda c: (0, c)),
        out_shape=jax.ShapeDtypeStruct((S, F), BF16),
        compiler_params=_cp(("parallel",)),
        name=name,
    )(u, u, convw, convw)


def ffn_act_bwd(u, convw, dact, F, name):
    S = u.shape[0]
    tc = _pick(F, (256, 128))
    nc = F // tc

    def body(ug_ref, uu_ref, wg_ref, wu_ref, da_ref, du_ref, dw_ref):
        ug = ug_ref[...].astype(F32)
        uu = uu_ref[...].astype(F32)
        wg = wg_ref[...]
        wu = wu_ref[...]
        da = da_ref[...].astype(F32)
        ag = _conv(ug, wg)
        au = _conv(uu, wu)
        sl, sig = _silu_parts(ag)
        dag = da * au * (sig * (1.0 + ag * (1.0 - sig)))
        dau = da * sl
        du_ref[0] = _conv_t(dag, wg).astype(du_ref.dtype)
        du_ref[1] = _conv_t(dau, wu).astype(du_ref.dtype)
        for k, (rg, ru) in enumerate(zip(_conv_dw(dag, ug), _conv_dw(dau, uu))):
            dw_ref[0, k : k + 1, :] = rg
            dw_ref[1, k : k + 1, :] = ru

    col = lambda half: pl.BlockSpec((S, tc), lambda c: (0, half * nc + c))
    wcol = lambda half: pl.BlockSpec((CONV_K, tc), lambda c: (0, half * nc + c))
    return pl.pallas_call(
        body,
        grid=(nc,),
        in_specs=[col(0), col(1), wcol(0), wcol(1), pl.BlockSpec((S, tc), lambda c: (0, c))],
        out_specs=[pl.BlockSpec((2, S, tc), lambda c: (0, 0, c)), pl.BlockSpec((2, CONV_K, tc), lambda c: (0, 0, c))],
        out_shape=[jax.ShapeDtypeStruct((2, S, F), BF16), jax.ShapeDtypeStruct((2, CONV_K, F), F32)],
        compiler_params=_cp(("parallel",)),
        name=name,
    )(u, u, convw, convw, dact)


def _softplus(z):
    return jnp.maximum(z, 0.0) + jnp.log(1.0 + jnp.exp(-jnp.abs(z)))


def _key_strip(S):
    return _pick(S, (512, 256, 128))


def _query_rows(S):
    tq = _pick(S, (512, 256, 128))
    assert _key_strip(S) % tq == 0
    return tq


def _split2(x):
    hi = x.astype(BF16)
    return hi, (x - hi.astype(F32)).astype(BF16)


def _block_sums(x, ones_bf16):
    hi, lo = _split2(x)
    return [
        _dot(hi[:, b * HD : (b + 1) * HD], ones_bf16) + _dot(lo[:, b * HD : (b + 1) * HD], ones_bf16)
        for b in range(x.shape[1] // HD)
    ]


def _strip_mask(shape, row0, off, strict):
    cols, rows = _iota2(shape, 1) + off, _iota2(shape, 0) + row0
    return cols < rows if strict else cols <= rows


def _sb_strip(q, ks, row0, off, run, su, masked):
    z = _dot_nt(q, ks) * (HD ** -0.5)
    sp = _softplus(z)
    mask = _strip_mask(z.shape, row0, off, True) if masked else None
    l = jnp.where(mask, -sp, 0.0) if masked else -sp
    within = _block_sums(l, su)
    later = [None] * len(within)
    for b in reversed(range(len(within))):
        later[b] = within[b] + run
        run = run + jnp.sum(l[:, b * HD : (b + 1) * HD], axis=1, keepdims=True)
    a = jnp.exp(z - sp + jnp.concatenate(later, axis=1))
    return z, (jnp.where(mask, a, 0.0) if masked else a), run


def sb_fwd(p, W, name):
    S = p.shape[0]
    TQ, TK = _query_rows(S), _key_strip(S)
    nh, nq = W // HD, S // TQ

    def body(q_ref, k_ref, v_ref, o_ref):
        i = pl.program_id(1)
        q = q_ref[...]
        su = (_iota2((HD, HD), 0) > _iota2((HD, HD), 1)).astype(BF16)
        last = (i * TQ + TQ - 1) // TK

        def strip(g, carry, masked):
            acc, run = carry
            off = pl.multiple_of(g * TK, TK)
            _, a, run = _sb_strip(q, k_ref[pl.ds(off, TK), :], i * TQ, off, run, su, masked)
            return acc + _dot(a.astype(BF16), v_ref[pl.ds(off, TK), :]), run

        carry = strip(last, (jnp.zeros((TQ, HD), F32), jnp.zeros((TQ, 1), F32)), True)
        acc, _ = lax.fori_loop(0, last, lambda gg, c: strip(last - 1 - gg, c, False), carry)
        o_ref[...] = acc.astype(o_ref.dtype)

    return pl.pallas_call(
        body,
        grid=(nh, nq),
        in_specs=[
            pl.BlockSpec((TQ, HD), lambda h, i: (i, h)),
            pl.BlockSpec((S, HD), lambda h, i: (0, nh + h)),
            pl.BlockSpec((S, HD), lambda h, i: (0, 2 * nh + h)),
        ],
        out_specs=pl.BlockSpec((TQ, HD), lambda h, i: (i, h)),
        out_shape=jax.ShapeDtypeStruct((S, 2 * W), BF16),
        compiler_params=_cp(("parallel", "arbitrary")),
        name=name,
    )(p, p, p)


def sb_bwd(p, dcat, W, name):
    S = p.shape[0]
    TQ, TK = _query_rows(S), _key_strip(S)
    nh, nq = W // HD, S // TQ
    scale = HD ** -0.5

    def body(q_ref, k_ref, v_ref, do_ref, dp_ref, dk_acc, dv_acc, e_scr, z_scr):
        i = pl.program_id(1)
        q = q_ref[...]
        do = do_ref[...]
        su = (_iota2((HD, HD), 0) > _iota2((HD, HD), 1)).astype(BF16)
        sl = (_iota2((HD, HD), 0) < _iota2((HD, HD), 1)).astype(BF16)
        last = (i * TQ + TQ - 1) // TK

        @pl.when(i == 0)
        def _():
            dk_acc[...] = jnp.zeros_like(dk_acc)
            dv_acc[...] = jnp.zeros_like(dv_acc)

        def pass_a(g, run, masked):
            off = pl.multiple_of(g * TK, TK)
            z, a, run = _sb_strip(q, k_ref[pl.ds(off, TK), :], i * TQ, off, run, su, masked)
            e_scr[g] = a * _dot_nt(do, v_ref[pl.ds(off, TK), :])
            z_scr[g] = z
            dv_acc[pl.ds(off, TK), :] += _dot_tn(a.astype(BF16), do)
            return run

        run = pass_a(last, jnp.zeros((TQ, 1), F32), True)
        lax.fori_loop(0, last, lambda gg, r: pass_a(last - 1 - gg, r, False), run)

        def pass_b(g, carry, masked):
            dq, run_e = carry
            off = pl.multiple_of(g * TK, TK)
            e = e_scr[g]
            z = z_scr[g]
            within = _block_sums(e, sl)
            before = []
            for b in range(len(within)):
                before.append(within[b] + run_e)
                run_e = run_e + jnp.sum(e[:, b * HD : (b + 1) * HD], axis=1, keepdims=True)
            sig = 1.0 / (1.0 + jnp.exp(-z))
            dz = e * (1.0 - sig) - jnp.concatenate(before, axis=1) * sig
            if masked:
                dz = jnp.where(_strip_mask(z.shape, i * TQ, off, True), dz, 0.0)
            dz = (dz * scale).astype(BF16)
            dq = dq + _dot(dz, k_ref[pl.ds(off, TK), :])
            dk_acc[pl.ds(off, TK), :] += _dot_tn(dz, q)
            return dq, run_e

        carry = lax.fori_loop(0, last, lambda g, c: pass_b(g, c, False), (jnp.zeros((TQ, HD), F32), jnp.zeros((TQ, 1), F32)))
        dq, _ = pass_b(last, carry, True)
        dp_ref[0, pl.ds(pl.multiple_of(i * TQ, TQ), TQ), :] = dq.astype(dp_ref.dtype)

        @pl.when(i == nq - 1)
        def _():
            dp_ref[1] = dk_acc[...].astype(dp_ref.dtype)
            dp_ref[2] = dv_acc[...].astype(dp_ref.dtype)

    return pl.pallas_call(
        body,
        grid=(nh, nq),
        in_specs=[
            pl.BlockSpec((TQ, HD), lambda h, i: (i, h)),
            pl.BlockSpec((S, HD), lambda h, i: (0, nh + h)),
            pl.BlockSpec((S, HD), lambda h, i: (0, 2 * nh + h)),
            pl.BlockSpec((TQ, HD), lambda h, i: (i, h)),
        ],
        out_specs=pl.BlockSpec((3, S, HD), lambda h, i: (0, 0, h)),
        out_shape=jax.ShapeDtypeStruct((6, S, W), BF16),
        scratch_shapes=[
            pltpu.VMEM((S, HD), F32),
            pltpu.VMEM((S, HD), F32),
            pltpu.VMEM((S // TK, TQ, TK), F32),
            pltpu.VMEM((S // TK, TQ, TK), F32),
        ],
        compiler_params=_cp(("parallel", "arbitrary")),
        name=name,
    )(p, p, p, dcat)


def fox_gate_fwd(f, b, name):
    S = f.shape[0]
    nq = S // HD

    def body(f_ref, b_ref, c_ref, run):
        i = pl.program_id(0)

        @pl.when(i == 0)
        def _():
            run[...] = jnp.zeros_like(run)

        lf = -_softplus(-(f_ref[...] + b_ref[...]))
        tri = (_iota2((HD, HD), 0) >= _iota2((HD, HD), 1)).astype(BF16)
        c_ref[...] = _dot_ones_left(tri, lf) + run[...]
        run[...] += jnp.sum(lf, axis=0, keepdims=True)

    return pl.pallas_call(
        body,
        grid=(nq,),
        in_specs=[pl.BlockSpec((HD, 128), lambda i: (i, 0)), pl.BlockSpec((1, 128), lambda i: (0, 0))],
        out_specs=pl.BlockSpec((HD, 128), lambda i: (i, 0)),
        out_shape=jax.ShapeDtypeStruct((S, 128), F32),
        scratch_shapes=[pltpu.VMEM((1, 128), F32)],
        compiler_params=_cp(("arbitrary",)),
        name=name,
    )(f, b)


def fox_gate_bwd(f, b, dc, name):
    S = f.shape[0]
    nq = S // HD

    def body(f_ref, b_ref, dc_ref, df_ref, db_ref, run):
        i = pl.program_id(0)

        @pl.when(i == 0)
        def _():
            run[...] = jnp.zeros_like(run)

        dc = dc_ref[...]
        tri = (_iota2((HD, HD), 0) <= _iota2((HD, HD), 1)).astype(BF16)
        dlf = _dot_ones_left(tri, dc) + run[...]
        run[...] += jnp.sum(dc, axis=0, keepdims=True)
        x = f_ref[...] + b_ref[...]
        df = dlf * (1.0 / (1.0 + jnp.exp(x)))
        df_ref[...] = df
        db = jnp.sum(df, axis=0, keepdims=True)

        @pl.when(i == 0)
        def _():
            db_ref[...] = db

        @pl.when(i > 0)
        def _():
            db_ref[...] += db

    rev = pl.BlockSpec((HD, 128), lambda i: (nq - 1 - i, 0))
    vec = pl.BlockSpec((1, 128), lambda i: (0, 0))
    return pl.pallas_call(
        body,
        grid=(nq,),
        in_specs=[rev, vec, rev],
        out_specs=[rev, vec],
        out_shape=[jax.ShapeDtypeStruct((S, 128), F32), jax.ShapeDtypeStruct((1, 128), F32)],
        scratch_shapes=[pltpu.VMEM((1, 128), F32)],
        compiler_params=_cp(("arbitrary",)),
        name=name,
    )(f, b, dc)


def _fox_logits(q, ks, ct, cs, row0, off, masked):
    s = _dot_nt(q, ks) * (HD ** -0.5) + (ct - cs)
    if not masked:
        return s, None
    mask = _strip_mask(s.shape, row0, off, False)
    return jnp.where(mask, s, -1e30), mask


def fox_fwd(p, ccol, crow, cat, W, name):
    S = p.shape[0]
    TQ, TK = _query_rows(S), _key_strip(S)
    nh, nq = W // HD, S // TQ

    def body(q_ref, k_ref, v_ref, cc_ref, cr_ref, cat_ref, o_ref, lse_ref):
        i = pl.program_id(1)
        q = q_ref[...]
        ct = cc_ref[0]

        def step(g, carry, masked):
            m, l, acc = carry
            off = pl.multiple_of(g * TK, TK)
            s, _ = _fox_logits(q, k_ref[pl.ds(off, TK), :], ct, cr_ref[0, pl.ds(g, 1), :], i * TQ, off, masked)
            m_new = jnp.maximum(m, jnp.max(s, axis=1, keepdims=True))
            alpha = jnp.exp(m - m_new)
            pr = jnp.exp(s - m_new)
            l = alpha * l + jnp.sum(pr, axis=1, keepdims=True)
            acc = alpha * acc + _dot(pr.astype(BF16), v_ref[pl.ds(off, TK), :])
            return m_new, l, acc

        init = (jnp.full((TQ, 1), -1e30, F32), jnp.zeros((TQ, 1), F32), jnp.zeros((TQ, HD), F32))
        last = (i * TQ + TQ - 1) // TK
        m, l, acc = step(last, lax.fori_loop(0, last, lambda g, c: step(g, c, False), init), True)
        o_ref[...] = (acc / l).astype(o_ref.dtype)
        lse_ref[0] = m + jnp.log(l)

    return pl.pallas_call(
        body,
        grid=(nh, nq),
        in_specs=[
            pl.BlockSpec((TQ, HD), lambda h, i: (i, 2 * nh + h)),
            pl.BlockSpec((S, HD), lambda h, i: (0, 3 * nh + h)),
            pl.BlockSpec((S, HD), lambda h, i: (0, 4 * nh + h)),
            pl.BlockSpec((1, TQ, 1), lambda h, i: (h, i, 0)),
            pl.BlockSpec((1, S // TK, TK), lambda h, i: (h, 0, 0)),
            pl.BlockSpec(memory_space=pl.ANY),
        ],
        out_specs=[pl.BlockSpec((TQ, HD), lambda h, i: (i, nh + h)), pl.BlockSpec((1, TQ, 1), lambda h, i: (h, i, 0))],
        out_shape=[jax.ShapeDtypeStruct(cat.shape, cat.dtype), jax.ShapeDtypeStruct((nh, S, 1), F32)],
        input_output_aliases={5: 0},
        compiler_params=_cp(("parallel", "arbitrary")),
        name=name,
    )(p, p, p, ccol, crow, cat)


def fox_bwd(p, ccol, crow, cat, lse, dcat, dp, W, name):
    S = p.shape[0]
    TQ, TK = _query_rows(S), _key_strip(S)
    nh, nq = W // HD, S // TQ
    scale = HD ** -0.5

    def body(q_ref, k_ref, v_ref, cc_ref, cr_ref, o_ref, lse_ref, do_ref, dp_in_ref, dp_ref, dcs_ref, dct_ref, dk_acc, dv_acc):
        i = pl.program_id(1)
        q = q_ref[...]
        do = do_ref[...]
        ct = cc_ref[0]
        lse_i = lse_ref[0]
        delta = jnp.sum(do.astype(F32) * o_ref[...].astype(F32), axis=1, keepdims=True)

        @pl.when(i == 0)
        def _():
            dk_acc[...] = jnp.zeros_like(dk_acc)
            dv_acc[...] = jnp.zeros_like(dv_acc)
            dcs_ref[...] = jnp.zeros_like(dcs_ref)

        def step(g, carry, masked):
            dq, dct = carry
            off = pl.multiple_of(g * TK, TK)
            ks = k_ref[pl.ds(off, TK), :]
            s, mask = _fox_logits(q, ks, ct, cr_ref[0, pl.ds(g, 1), :], i * TQ, off, masked)
            pr = jnp.where(mask, jnp.exp(s - lse_i), 0.0) if masked else jnp.exp(s - lse_i)
            ds = pr * (_dot_nt(do, v_ref[pl.ds(off, TK), :]) - delta)
            dv_acc[pl.ds(off, TK), :] += _dot_tn(pr.astype(BF16), do)
            dsb = (ds * scale).astype(BF16)
            dk_acc[pl.ds(off, TK), :] += _dot_tn(dsb, q)
            dcs_ref[0, pl.ds(g, 1), :] += jnp.sum(ds, axis=0, keepdims=True)
            return dq + _dot(dsb, ks), dct + jnp.sum(ds, axis=1, keepdims=True)

        last = (i * TQ + TQ - 1) // TK
        carry = lax.fori_loop(0, last, lambda g, c: step(g, c, False), (jnp.zeros((TQ, HD), F32), jnp.zeros((TQ, 1), F32)))
        dq, dct = step(last, carry, True)
        dp_ref[0, pl.ds(pl.multiple_of(i * TQ, TQ), TQ), :] = dq.astype(dp_ref.dtype)
        dct_ref[0] = dct

        @pl.when(i == nq - 1)
        def _():
            dp_ref[1] = dk_acc[...].astype(dp_ref.dtype)
            dp_ref[2] = dv_acc[...].astype(dp_ref.dtype)

    return pl.pallas_call(
        body,
        grid=(nh, nq),
        in_specs=[
            pl.BlockSpec((TQ, HD), lambda h, i: (i, 2 * nh + h)),
            pl.BlockSpec((S, HD), lambda h, i: (0, 3 * nh + h)),
            pl.BlockSpec((S, HD), lambda h, i: (0, 4 * nh + h)),
            pl.BlockSpec((1, TQ, 1), lambda h, i: (h, i, 0)),
            pl.BlockSpec((1, S // TK, TK), lambda h, i: (h, 0, 0)),
            pl.BlockSpec((TQ, HD), lambda h, i: (i, nh + h)),
            pl.BlockSpec((1, TQ, 1), lambda h, i: (h, i, 0)),
            pl.BlockSpec((TQ, HD), lambda h, i: (i, nh + h)),
            pl.BlockSpec(memory_space=pl.ANY),
        ],
        out_specs=[
            pl.BlockSpec((3, S, HD), lambda h, i: (1, 0, h)),
            pl.BlockSpec((1, S // TK, TK), lambda h, i: (h, 0, 0)),
            pl.BlockSpec((1, TQ, 1), lambda h, i: (h, i, 0)),
        ],
        out_shape=[
            jax.ShapeDtypeStruct(dp.shape, dp.dtype),
            jax.ShapeDtypeStruct((nh, S // TK, TK), F32),
            jax.ShapeDtypeStruct((nh, S, 1), F32),
        ],
        input_output_aliases={8: 0},
        scratch_shapes=[pltpu.VMEM((S, HD), F32), pltpu.VMEM((S, HD), F32)],
        compiler_params=_cp(("parallel", "arbitrary")),
        name=name,
    )(p, p, p, ccol, crow, cat, lse, dcat, dp)


_GELU_K = math.sqrt(2.0 / math.pi)
_GELU_C = 0.044715


def _gelu(x):
    return 0.5 * x * (1.0 + jnp.tanh(_GELU_K * (x + _GELU_C * x * x * x)))


def _gelu_grad(x):
    t = jnp.tanh(_GELU_K * (x + _GELU_C * x * x * x))
    return 0.5 * (1.0 + t) + 0.5 * x * (1.0 - t * t) * (_GELU_K * (1.0 + 3.0 * _GELU_C * x * x))


def _layernorm_parts(gv):
    xc = gv - jnp.mean(gv, axis=-1, keepdims=True)
    r = lax.rsqrt(jnp.mean(xc * xc, axis=-1, keepdims=True) + EPS)
    return xc * r, r


def sg_fwd(p, sg_w, sg_bt, sg_g, W, name):
    S = p.shape[0]
    G, nq = W // HD, S // HD

    def body(u_ref, v_ref, w_ref, bt_ref, g_ref, o_ref):
        xh, _ = _layernorm_parts(_gelu(v_ref[...].astype(F32)))
        vn = (xh * g_ref[...]).astype(BF16)
        tri = _iota2((HD, HD), 0) >= _iota2((HD, HD), 1)
        for gi in range(G):
            cols = slice(gi * HD, (gi + 1) * HD)
            wt = jnp.where(tri, w_ref[gi], 0.0).astype(BF16)
            mixed = _dot(wt, vn[:, cols]) + bt_ref[:, gi : gi + 1]
            o_ref[:, cols] = (_gelu(u_ref[:, cols].astype(F32)) * mixed).astype(o_ref.dtype)

    return pl.pallas_call(
        body,
        grid=(nq,),
        in_specs=[
            pl.BlockSpec((HD, W), lambda i: (i, 0)),
            pl.BlockSpec((HD, W), lambda i: (i, 1)),
            pl.BlockSpec((G, HD, HD), lambda i: (0, 0, 0)),
            pl.BlockSpec((HD, G), lambda i: (0, 0)),
            pl.BlockSpec((1, W), lambda i: (0, 0)),
        ],
        out_specs=pl.BlockSpec((HD, W), lambda i: (i, 0)),
        out_shape=jax.ShapeDtypeStruct((S, 2 * W), BF16),
        compiler_params=_cp(("parallel",)),
        name=name,
    )(p, p, sg_w, sg_bt, sg_g.reshape(1, W))


def sg_bwd(p, sg_w, sg_bt, sg_g, dcat, W, name):
    S = p.shape[0]
    G, nq = W // HD, S // HD

    def body(u_ref, v_ref, w_ref, bt_ref, g_ref, do_ref, dp_ref, dw_ref, dbt_ref, dg_ref, dvn_scr):
        i = pl.program_id(0)

        @pl.when(i == 0)
        def _():
            dw_ref[...] = jnp.zeros_like(dw_ref)
            dbt_ref[...] = jnp.zeros_like(dbt_ref)
            dg_ref[...] = jnp.zeros_like(dg_ref)

        v = v_ref[...].astype(F32)
        xh, r = _layernorm_parts(_gelu(v))
        gg = g_ref[...]
        vn = (xh * gg).astype(BF16)
        tri = _iota2((HD, HD), 0) >= _iota2((HD, HD), 1)
        for gi in range(G):
            cols = slice(gi * HD, (gi + 1) * HD)
            wt = jnp.where(tri, w_ref[gi], 0.0).astype(BF16)
            mixed = _dot(wt, vn[:, cols]) + bt_ref[:, gi : gi + 1]
            u = u_ref[:, cols].astype(F32)
            do = do_ref[:, cols].astype(F32)
            dp_ref[0, :, cols] = (do * mixed * _gelu_grad(u)).astype(dp_ref.dtype)
            dmix = do * _gelu(u)
            dmb = dmix.astype(BF16)
            dw_ref[gi] += jnp.where(tri, _dot_nt(dmb, vn[:, cols]), 0.0)
            dbt_ref[:, gi : gi + 1] += jnp.sum(dmix, axis=1, keepdims=True)
            dvn_scr[:, cols] = _dot_tn(wt, dmb)
        dvn = dvn_scr[...]
        dg_ref[...] += jnp.sum(dvn * xh, axis=0, keepdims=True)
        dxh = dvn * gg
        dgv = r * (dxh - jnp.mean(dxh, axis=-1, keepdims=True) - xh * jnp.mean(dxh * xh, axis=-1, keepdims=True))
        dp_ref[1] = (dgv * _gelu_grad(v)).astype(dp_ref.dtype)

    return pl.pallas_call(
        body,
        grid=(nq,),
        in_specs=[
            pl.BlockSpec((HD, W), lambda i: (i, 0)),
            pl.BlockSpec((HD, W), lambda i: (i, 1)),
            pl.BlockSpec((G, HD, HD), lambda i: (0, 0, 0)),
            pl.BlockSpec((HD, G), lambda i: (0, 0)),
            pl.BlockSpec((1, W), lambda i: (0, 0)),
            pl.BlockSpec((HD, W), lambda i: (i, 0)),
        ],
        out_specs=[
            pl.BlockSpec((2, HD, W), lambda i: (0, i, 0)),
            pl.BlockSpec((G, HD, HD), lambda i: (0, 0, 0)),
            pl.BlockSpec((HD, G), lambda i: (0, 0)),
            pl.BlockSpec((1, W), lambda i: (0, 0)),
        ],
        out_shape=[
            jax.ShapeDtypeStruct((6, S, W), BF16),
            jax.ShapeDtypeStruct((G, HD, HD), F32),
            jax.ShapeDtypeStruct((HD, G), F32),
            jax.ShapeDtypeStruct((1, W), F32),
        ],
        scratch_shapes=[pltpu.VMEM((HD, W), F32)],
        compiler_params=_cp(("arbitrary",)),
        name=name,
    )(p, p, sg_w, sg_bt, sg_g.reshape(1, W), dcat)


def local_step(x, target, wts, at, on_grad):
    S, D = x.shape
    W = D // 2
    nb, F = wts["nb"], wts["F"]
    g = {}

    def ffn_fwd(xin, l):
        h = rms_fwd(xin, wts[f"{l}_ffn_norm_g"], f"{l}_ffn_rms")
        u = mm_nn(h, wts[f"{l}_ffn_up"], nb, f"{l}_ffn_up_mm")
        act = ffn_act_fwd(u, wts[f"{l}_ffn_conv_w"], F, f"{l}_ffn_act")
        half_tile = _pick(S, (512, 256, 128))
        xout = mm_nn(act, wts[f"{l}_ffn_down"], 1, f"{l}_ffn_down_mm", out_dtype=F32, res=xin,
                     tm=half_tile, tn=_pick(D, (512, 256, 128)), tk=F)
        return xout, (xin, h, u, act)

    def ffn_bwd(dxout, dxoutb, saved, l):
        xin, h, u, act = saved
        dact = mm_nt(dxoutb, wts[f"{l}_ffn_down"], 1, S, F, f"{l}_ffn_down_dx", tko=_pick(F, (512, 256, 128)), tn=D)
        dact = on_grad(f"{l}_ffn_down", mm_tn(act, dxoutb, 1, D, f"{l}_ffn_down_dw", tn=D), dact)
        du, dcw = ffn_act_bwd(u, wts[f"{l}_ffn_conv_w"], dact, F, f"{l}_ffn_act_bwd")
        g[f"{l}_ffn_conv_w"] = jnp.concatenate([dcw[0], dcw[1]], axis=1)
        du2 = du.reshape(2 * S, F)
        n = wts[f"{l}_ffn_up"].shape[1]
        tn = _pick(n, (1408, 1024, 768, 512, 256, 128))
        per_half = F // tn
        nt = n // tn

        def up_block(i, j, t):
            vb = j * nt + t
            return vb // per_half, vb % per_half

        tm = _pick(S, (1024, 512, 256, 128))

        def nt_map(i, j, t):
            half, cb = up_block(i, j, t)
            return (half * (S // tm) + i, cb)

        def tn_map(j, t):
            half, cb = up_block(0, j, t)
            return (half, cb)

        dh = mm_nt(du2, wts[f"{l}_ffn_up"], nb, S, D, f"{l}_ffn_up_dx", dy_maps=[nt_map], tm=tm, tko=D, tn=tn)
        dh = on_grad(f"{l}_ffn_up", mm_tn(h, du2, nb, n, f"{l}_ffn_up_dw", dy_maps=[tn_map], tn=tn), dh)
        dxin, dxinb, dg = rms_bwd(xin, wts[f"{l}_ffn_norm_g"], dh, dxout, f"{l}_ffn_rms_bwd")
        g[f"{l}_ffn_norm_g"] = dg
        return dxin, dxinb

    h0 = rms_fwd(x, wts["l0_mix_norm_g"], "l0_mix_rms")
    p0 = mm_nn(h0, wts["l0_w_in"], nb, "l0_w_in_mm")
    cat0 = sb_fwd(p0, W, "l0_sb_fwd")
    cat0 = sc_fwd(p0, wts["l0_sc_conv_w"], cat0, W, "l0_sc_fwd")
    x1 = mm_nn(cat0, wts["l0_w_out"], 1, "l0_w_out_mm", out_dtype=F32, res=x, tm=S, tn=_pick(D, (512, 256, 128)))
    x2, ffn0_saved = ffn_fwd(x1, "l0")

    x2 = at("l1_w_in", x2, None)
    nh = W // HD
    h2 = rms_fwd(x2, wts["l1_mix_norm_g"], "l1_mix_rms")
    p1 = mm_nt(h2, wts["l1_w_in_t"], 1, S, 5 * W, "l1_w_in_mm", tn=D)
    f = mm_nt(h2, wts["l1_w_f_t"], 1, S, 128, "l1_w_f_mm", out_dtype=F32, tn=D)
    bf = jnp.zeros((1, 128), F32).at[0, :nh].set(wts["l1_fox_b_f"])
    c = fox_gate_fwd(f, bf, "l1_fox_gate")
    c_heads = c[:, :nh].T
    ccol = c_heads[:, :, None]
    crow = c_heads.reshape(nh, S // _key_strip(S), _key_strip(S))
    sg_bt = wts["l1_sg_b"].T
    cat1 = sg_fwd(p1, wts["l1_sg_w"], sg_bt, wts["l1_sg_norm_g"], W, "l1_sg_fwd")
    cat1, lse = fox_fwd(p1, ccol, crow, cat1, W, "l1_fox_fwd")
    x3 = mm_nn(cat1, wts["l1_w_out"], 1, "l1_w_out_mm", out_dtype=F32, res=x2, tm=S, tn=_pick(D, (512, 256, 128)))
    x4, ffn1_saved = ffn_fwd(x3, "l1")

    dx4, dx4b, dgf, loss = loss_head(x4, wts["final_norm_g"], target, "loss_head")
    dx4b = at("loss", dx4b, loss)
    g["final_norm_g"] = dgf

    dx3, dx3b = ffn_bwd(dx4, dx4b, ffn1_saved, "l1")
    dcat1 = mm_nt(dx3b, wts["l1_w_out"], 1, S, D, "l1_w_out_dx", tn=D)
    dcat1 = on_grad("l1_w_out", mm_tn(cat1, dx3b, 1, D, "l1_w_out_dw", tn=D), dcat1)
    dp1, dsgw, dsgbt, dsgg = sg_bwd(p1, wts["l1_sg_w"], sg_bt, wts["l1_sg_norm_g"], dcat1, W, "l1_sg_bwd")
    dp1, dcs, dct = fox_bwd(p1, ccol, crow, cat1, lse, dcat1, dp1, W, "l1_fox_bwd")
    g["l1_sg_w"], g["l1_sg_b"], g["l1_sg_norm_g"] = dsgw, dsgbt.T, dsgg
    dc = jnp.zeros((S, 128), F32).at[:, :nh].set((dct[:, :, 0] - dcs.reshape(nh, S)).T)
    df, dbf = fox_gate_bwd(f, bf, dc, "l1_fox_gate_bwd")
    g["l1_fox_b_f"] = dbf[0, :nh]
    dfb = df.astype(BF16)
    tk1 = _pick(W, (1024, 512, 256, 128))
    tx1 = _pick(W, (512, 256, 128))
    tm1 = _pick(S, (1024, 512, 256, 128))
    part_of = lambda pt: pt + pt // 2 - pt // 4

    def a_map1(i, k):
        return (part_of(k // (W // tk1)) * (S // tm1) + i, k % (W // tk1))

    def x_map1(ko):
        return (part_of(ko // (W // tx1)), ko % (W // tx1))

    dp1_2d = dp1.reshape(6 * S, W)
    dw_main = mm_tn(dp1_2d, h2, 1, D, "l1_w_in_dw", tko=tx1, tn=D, x_map=x_map1, x_shape=(S, 5 * W))
    dw_f = mm_tn(dfb, h2, 1, D, "l1_w_f_dw", tn=D)
    dh2 = mm_nn(dfb, wts["l1_w_f_t"], 1, "l1_w_f_dx", out_dtype=F32)
    dh2 = mm_nn(dp1_2d, wts["l1_w_in_t"], 1, "l1_w_in_dx", res=dh2, tm=tm1, tk=tk1, a_map=a_map1, a_shape=(S, 5 * W))
    dh2 = on_grad("l1_w_in", jnp.concatenate([dw_main, dw_f[:nh]], axis=0), dh2)
    dx2, dx2b, dg = rms_bwd(x2, wts["l1_mix_norm_g"], dh2, dx3, "l1_mix_rms_bwd")
    g["l1_mix_norm_g"] = dg

    dx1, dx1b = ffn_bwd(dx2, dx2b, ffn0_saved, "l0")
    dcat0 = mm_nt(dx1b, wts["l0_w_out"], 1, S, D, "l0_w_out_dx", tn=D)
    dcat0 = on_grad("l0_w_out", mm_tn(cat0, dx1b, 1, D, "l0_w_out_dw", tn=D), dcat0)
    dp0 = sb_bwd(p0, dcat0, W, "l0_sb_bwd")
    dp0, dscw = sc_bwd(p0, wts["l0_sc_conv_w"], dcat0, dp0, W, "l0_sc_bwd")
    g["l0_sc_conv_w"] = dscw
    dp0 = at("small_ready", dp0, g)
    n0 = wts["l0_w_in"].shape[1]
    td0 = math.gcd(n0, W)
    nd0 = n0 // td0
    tm0 = _pick(S, (1024, 512, 256, 128))
    per_part0 = W // td0

    def nt_maps0(k):
        def f(i, j, t):
            vb = j * nd0 + k
            return ((vb // per_part0) * (S // tm0) + i, vb % per_part0)
        return f

    def tn_maps0(k):
        def f(j, t):
            vb = j * nd0 + k
            return (vb // per_part0, vb % per_part0)
        return f

    dp0_2d = dp0.reshape(6 * S, W)
    dw0 = mm_tn(h0, dp0_2d, nb, n0, "l0_w_in_dw", dy_maps=[tn_maps0(k) for k in range(nd0)], tn=n0)
    dp0_2d = on_grad("l0_w_in", dw0, dp0_2d)
    dp0_2d = on_grad(None, None, dp0_2d)
    dh0 = mm_nt(dp0_2d, wts["l0_w_in"], nb, S, D, "l0_w_in_dx", dy_maps=[nt_maps0(k) for k in range(nd0)], tm=tm0, tn=n0)
    dh0 = at("small_done", dh0, None)
    dx0, _, dg = rms_bwd(x, wts["l0_mix_norm_g"], dh0, dx1, "l0_mix_rms_bwd")
    g["l0_mix_norm_g"] = dg
    return dx0, g


GATHER_ID = 1


def _place():
    return lax.axis_index("x"), lax.axis_index("y"), lax.axis_index("c")


def _other_chips(x, y):
    return [(x, 1 - y), (1 - x, y), (1 - x, 1 - y)]


def _handshake(peers):
    barrier = pltpu.get_barrier_semaphore()
    for peer in peers:
        pl.semaphore_signal(barrier, inc=1, device_id=peer, device_id_type=MESH)
    pl.semaphore_wait(barrier, len(peers))


UPDATE_LAG = 2


def _on_sequencer(body, out_type, scratch_types, collective_id, name):
    return pl.kernel(
        body,
        out_type=out_type,
        mesh=plsc.ScalarSubcoreMesh(axis_name="seq", num_cores=1),
        scratch_types=scratch_types,
        compiler_params=pltpu.CompilerParams(collective_id=collective_id),
        name=name,
    )


def all_gather(arrs, name):
    n = len(arrs)

    def body(*refs):
        xs, outs = refs[:n], refs[n : 2 * n]
        send_sems, recv_sems, local_sems = refs[2 * n :]
        x, y, c = _place()
        me, sibling = (x, y, c), (x, y, 1 - c)
        chips = _other_chips(x, y)
        _handshake([sibling] + [(*chip, c) for chip in chips])

        def copy(a, k, block, to, src=None):
            px, py, pc = block
            dst = outs[a].at[4 * px + 2 * py + pc]
            return pltpu.make_async_remote_copy(
                src_ref=dst if src is None else src, dst_ref=dst,
                send_sem=send_sems.at[7 * a + k], recv_sem=recv_sems.at[7 * a + k], device_id=to, device_id_type=MESH,
            )

        mine = [pltpu.make_async_copy(xs[a], outs[a].at[4 * x + 2 * y + c], local_sems.at[a]) for a in range(n)]
        for cp in mine:
            cp.start()
        first = []
        for a in range(n):
            first.append(copy(a, 0, me, sibling, src=xs[a]))
            first += [copy(a, 1 + j, me, (*chip, c), src=xs[a]) for j, chip in enumerate(chips)]
        for cp in first:
            cp.start()
        passed = []
        for a in range(n):
            for j, chip in enumerate(chips):
                copy(a, 1 + j, (*chip, c), me).wait_recv()
                cp = copy(a, 4 + j, (*chip, c), sibling)
                cp.start()
                passed.append(cp)
        for a in range(n):
            copy(a, 0, sibling, me).wait_recv()
            for j, chip in enumerate(chips):
                copy(a, 4 + j, (*chip, 1 - c), me).wait_recv()
        for cp in first + passed:
            cp.wait_send()
        for cp in mine:
            cp.wait()

    out_type = [jax.ShapeDtypeStruct((NDEV,) + a.shape, a.dtype) for a in arrs]
    sems = [pltpu.SemaphoreType.DMA((7 * n,)), pltpu.SemaphoreType.DMA((7 * n,)), pltpu.SemaphoreType.DMA((n,))]
    return _on_sequencer(body, out_type, sems, GATHER_ID, name)(*arrs)


_IN_HBM = pl.BlockSpec(memory_space=pltpu.HBM)
_IN_SEM = pl.BlockSpec(memory_space=pltpu.SEMAPHORE)
_EFFECT = pltpu.SideEffectType.DATAFLOW_SIDE_EFFECTING


def _split_start(make_copies, src, land_shape, nsem, name):
    def body(src_ref, land_ref, send_sems, recv_sems, land_thru, token):
        for cp in make_copies(src_ref, land_ref, send_sems, recv_sems):
            cp.start()
        token[...] = jnp.zeros_like(token)

    send_sems, recv_sems, land_thru, token = pl.pallas_call(
        body,
        name=name,
        out_shape=(
            pltpu.SemaphoreType.DMA((nsem,)), pltpu.SemaphoreType.DMA((nsem,)),
            pltpu.HBM(land_shape, src.dtype), jax.ShapeDtypeStruct((8, 128), F32),
        ),
        in_specs=(_IN_HBM, _IN_HBM),
        out_specs=(_IN_SEM, _IN_SEM, _IN_HBM, pl.BlockSpec(memory_space=pltpu.VMEM)),
        input_output_aliases={1: 2},
        compiler_params=pltpu.CompilerParams(has_side_effects=_EFFECT),
    )(src, pltpu.with_memory_space_constraint(lax.empty(land_shape, src.dtype), pltpu.HBM))
    return send_sems, recv_sems, src, land_thru, token


def _split_wait(make_copies, send_sems, recv_sems, src_thru, land_thru, after, name):
    def body(src_ref, land_ref, send_sems, recv_sems, after_ref, land_out):
        for cp in make_copies(src_ref, land_ref, send_sems, recv_sems):
            cp.wait_send()
            cp.wait_recv()

    return pl.pallas_call(
        body,
        name=name,
        out_shape=pltpu.HBM(land_thru.shape, land_thru.dtype),
        in_specs=(_IN_HBM, _IN_HBM, _IN_SEM, _IN_SEM, pl.BlockSpec(memory_space=pl.ANY)),
        out_specs=_IN_HBM,
        input_output_aliases={1: 0},
        compiler_params=pltpu.CompilerParams(has_side_effects=_EFFECT),
    )(src_thru, land_thru, send_sems, recv_sems, after)


def _pair_copies(src_ref, land_ref, send_sems, recv_sems):
    x, y, c = _place()
    return [
        pltpu.make_async_remote_copy(
            src_ref=src_ref.at[k, 1 - c], dst_ref=land_ref.at[k],
            send_sem=send_sems.at[k], recv_sem=recv_sems.at[k], device_id=(x, y, 1 - c), device_id_type=MESH,
        )
        for k in range(4)
    ]


def _direct_copies(src_ref, land_ref, send_sems, recv_sems):
    x, y, c = _place()
    me = 4 * x + 2 * y + c
    copies = []
    for k in range(NDEV - 1):
        to = (me + k + 1) % NDEV
        copies.append(pltpu.make_async_remote_copy(
            src_ref=src_ref, dst_ref=land_ref.at[me], send_sem=send_sems.at[k], recv_sem=recv_sems.at[k],
            device_id=(to // 4, (to // 2) % 2, to % 2), device_id_type=MESH,
        ))
    return copies


def _chip_copies(src_ref, land_ref, send_sems, recv_sems):
    x, y, c = _place()
    return [
        pltpu.make_async_remote_copy(
            src_ref=src_ref.at[2 * px + py], dst_ref=land_ref.at[2 * x + y],
            send_sem=send_sems.at[j], recv_sem=recv_sems.at[j], device_id=(px, py, c), device_id_type=MESH,
        )
        for j, (px, py) in enumerate(_other_chips(x, y))
    ]


def _row_tile(R, C, max_elems):
    if R * C <= max_elems:
        return R
    best = None
    for tr in range(16, R, 16):
        if R % tr == 0 and tr * C <= max_elems:
            best = tr
    return best or R


def pair_sum(a42, land4, core, name):
    _, _, R, C = a42.shape
    tr = _row_tile(R, C, 1 << 20)

    def body(core_ref, a_ref, l_ref, o_ref):
        o_ref[...] = (a_ref[0].astype(F32) + l_ref[...].astype(F32)).astype(o_ref.dtype)

    return pl.pallas_call(
        body,
        grid_spec=pltpu.PrefetchScalarGridSpec(
            num_scalar_prefetch=1,
            grid=(4, R // tr),
            in_specs=[
                pl.BlockSpec((1, 1, tr, C), lambda k, r, core_ref: (k, core_ref[0], r, 0)),
                pl.BlockSpec((1, tr, C), lambda k, r, core_ref: (k, r, 0)),
            ],
            out_specs=pl.BlockSpec((1, tr, C), lambda k, r, core_ref: (k, r, 0)),
        ),
        out_shape=jax.ShapeDtypeStruct((4, R, C), BF16),
        compiler_params=_cp(("parallel", "parallel")),
        name=name,
    )(core, a42, land4)


def sum_slots(parts, name):
    P, R, C = parts.shape

    def body(p_ref, o_ref):
        acc = p_ref[0].astype(F32)
        for k in range(1, P):
            acc = acc + p_ref[k].astype(F32)
        o_ref[...] = acc

    tr = _row_tile(R, P * C, 1 << 21)
    return pl.pallas_call(
        body,
        grid=(R // tr,),
        in_specs=[pl.BlockSpec((P, tr, C), lambda r: (0, r, 0))],
        out_specs=pl.BlockSpec((tr, C), lambda r: (r, 0)),
        out_shape=jax.ShapeDtypeStruct((R, C), F32),
        compiler_params=_cp(("parallel",)),
        name=name,
    )(parts)


def adamw(w, m, v, parts, name):
    R, C = w.shape
    P = parts.shape[0]
    tr = _pick(R, (256, 128, 64, 32, 16, 8))
    c1 = 1.0 - ADAM_B1 ** ADAM_STEP
    c2 = 1.0 - ADAM_B2 ** ADAM_STEP

    def body(w_ref, m_ref, v_ref, p_ref, g_ref, d_ref, nm_ref, nv_ref):
        g = p_ref[0].astype(F32)
        for k in range(1, P):
            g = g + p_ref[k].astype(F32)
        nm = ADAM_B1 * m_ref[...] + (1.0 - ADAM_B1) * g
        nv = ADAM_B2 * v_ref[...] + (1.0 - ADAM_B2) * (g * g)
        g_ref[...] = g
        nm_ref[...] = nm
        nv_ref[...] = nv
        d_ref[...] = -ADAM_LR * ((nm / c1) / (jnp.sqrt(nv / c2) + ADAM_EPS) + ADAM_WD * w_ref[...])

    blk = pl.BlockSpec((tr, C), lambda r: (r, 0))
    shp = jax.ShapeDtypeStruct((R, C), F32)
    return pl.pallas_call(
        body,
        grid=(R // tr,),
        in_specs=[blk, blk, blk, pl.BlockSpec((P, tr, C), lambda r: (0, r, 0))],
        out_specs=[blk, blk, blk, blk],
        out_shape=[shp, shp, shp, shp],
        compiler_params=_cp(("parallel",)),
        name=name,
    )(w, m, v, parts)


def adamw_reduced(w, m, v, own, land, chip, name):
    R, C = w.shape
    if R % 8 == 0:
        tr, tc = _pick(R, (256, 128, 64, 32, 16, 8)), C
    else:
        tr, tc = R, _pick(C, (256, 128))
    c1 = 1.0 - ADAM_B1 ** ADAM_STEP
    c2 = 1.0 - ADAM_B2 ** ADAM_STEP

    def body(chip_ref, w_ref, m_ref, v_ref, own_ref, land_ref, g_ref, d_ref, nm_ref, nv_ref):
        mine = own_ref[0].astype(F32)
        g = None
        for k in range(4):
            term = jnp.where(chip_ref[0] == k, mine, land_ref[k].astype(F32))
            g = term if g is None else g + term
        nm = ADAM_B1 * m_ref[...] + (1.0 - ADAM_B1) * g
        nv = ADAM_B2 * v_ref[...] + (1.0 - ADAM_B2) * (g * g)
        g_ref[...] = g
        nm_ref[...] = nm
        nv_ref[...] = nv
        d_ref[...] = -ADAM_LR * ((nm / c1) / (jnp.sqrt(nv / c2) + ADAM_EPS) + ADAM_WD * w_ref[...])

    blk = pl.BlockSpec((tr, tc), lambda r, c, chip_ref: (r, c))
    shp = jax.ShapeDtypeStruct((R, C), F32)
    return pl.pallas_call(
        body,
        grid_spec=pltpu.PrefetchScalarGridSpec(
            num_scalar_prefetch=1,
            grid=(R // tr, C // tc),
            in_specs=[
                blk, blk, blk,
                pl.BlockSpec((1, tr, tc), lambda r, c, chip_ref: (chip_ref[0], r, c)),
                pl.BlockSpec((4, tr, tc), lambda r, c, chip_ref: (0, r, c)),
            ],
            out_specs=[blk, blk, blk, blk],
        ),
        out_shape=[shp, shp, shp, shp],
        compiler_params=_cp(("parallel", "parallel")),
        name=name,
    )(chip, w, m, v, own, land)


_WEIGHTS = [
    "l0_mix_norm_g", "l0_w_in", "l0_sc_conv_w", "l0_w_out", "l0_ffn_norm_g", "l0_ffn_up", "l0_ffn_conv_w", "l0_ffn_down",
    "l1_mix_norm_g", "l1_w_in", "l1_fox_b_f", "l1_sg_w", "l1_sg_b", "l1_sg_norm_g", "l1_w_out", "l1_ffn_norm_g",
    "l1_ffn_up", "l1_ffn_conv_w", "l1_ffn_down", "final_norm_g",
]
_ROW_SHARDED = ["l0_w_out", "l0_ffn_down", "l1_w_out", "l1_ffn_down"]
_BIG = ["l0_w_in", "l0_w_out", "l0_ffn_up", "l0_ffn_down", "l1_w_in", "l1_w_out", "l1_ffn_up", "l1_ffn_down"]
_CONV = ["l0_sc_conv_w", "l0_ffn_conv_w", "l1_ffn_conv_w"]
_SMALL = [n for n in _WEIGHTS if n not in _BIG]
_LAST_SMALL = "l0_mix_norm_g"
_PACK_ROWS = 8


def _pack(arrs):
    flat = []
    for a in arrs:
        v = a.reshape(-1).astype(F32)
        pad = (-v.shape[0]) % (_PACK_ROWS * 128)
        flat.append(jnp.pad(v, (0, pad)))
    return jnp.concatenate(flat).reshape(-1, 128)


def _unpack(packed, shapes):
    out, off = [], 0
    flat = packed.reshape(-1)
    for shp in shapes:
        size = math.prod(shp)
        out.append(flat[off : off + size].reshape(shp))
        off += size + (-size) % (_PACK_ROWS * 128)
    return out


def kernel(x, l0_mix_norm_g, l0_w_in, l0_sc_conv_w, l0_w_out, l0_ffn_norm_g, l0_ffn_up, l0_ffn_conv_w, l0_ffn_down, l1_mix_norm_g, l1_w_in, l1_fox_b_f, l1_sg_w, l1_sg_b, l1_sg_norm_g, l1_w_out, l1_ffn_norm_g, l1_ffn_up, l1_ffn_conv_w, l1_ffn_down, final_norm_g, loss_target, m_l0_mix_norm_g, m_l0_w_in, m_l0_sc_conv_w, m_l0_w_out, m_l0_ffn_norm_g, m_l0_ffn_up, m_l0_ffn_conv_w, m_l0_ffn_down, m_l1_mix_norm_g, m_l1_w_in, m_l1_fox_b_f, m_l1_sg_w, m_l1_sg_b, m_l1_sg_norm_g, m_l1_w_out, m_l1_ffn_norm_g, m_l1_ffn_up, m_l1_ffn_conv_w, m_l1_ffn_down, m_final_norm_g, v_l0_mix_norm_g, v_l0_w_in, v_l0_sc_conv_w, v_l0_w_out, v_l0_ffn_norm_g, v_l0_ffn_up, v_l0_ffn_conv_w, v_l0_ffn_down, v_l1_mix_norm_g, v_l1_w_in, v_l1_fox_b_f, v_l1_sg_w, v_l1_sg_b, v_l1_sg_norm_g, v_l1_w_out, v_l1_ffn_norm_g, v_l1_ffn_up, v_l1_ffn_conv_w, v_l1_ffn_down, v_final_norm_g):
    given = dict(locals())
    w = {n: given[n] for n in _WEIGHTS}
    mom = {n: given["m_" + n] for n in _WEIGHTS}
    var = {n: given["v_" + n] for n in _WEIGHTS}
    xs, target = x[0], loss_target[0]
    S, D = xs.shape
    W = D // 2
    nh = W // HD
    cx, cy, cc = _place()
    me = 4 * cx + 2 * cy + cc

    wts = {"nb": NDEV, "F": l0_ffn_down.shape[0] * NDEV}
    for n in _SMALL:
        if n not in _CONV:
            wts[n] = w[n]
    gathered, loss_sum = {}, []

    def start_gather(n):
        src = w[n].T if n == "l1_w_in" else w[n]
        got = all_gather([src.astype(BF16)] + ([w[c] for c in _CONV] if n == _BIG[0] else []), f"gather_{n}")
        if n == "l1_w_in":
            gathered[n] = got[0]
        elif n in _ROW_SHARDED:
            wts[n] = got[0].reshape(-1, D)
        else:
            wts[n] = got[0].reshape(NDEV * D, -1)
        for c, taps in zip(_CONV, got[1:]):
            wts[c] = taps.transpose(1, 0, 2).reshape(CONV_K, -1)

    def at(point, after, value):
        if point == "l1_w_in":
            got, after = lax.optimization_barrier((gathered[point], after))
            wts["l1_w_in_t"] = got.reshape(-1, D)
            wts["l1_w_f_t"] = jnp.pad(wts["l1_w_in_t"][5 * W :], ((0, 128 - nh), (0, 0)))
        elif point == "loss":
            total, after = lax.optimization_barrier((lax.psum(value[0, 0], ("x", "y", "c")), after))
            loss_sum.append(total)
        elif point == "small_ready":
            early = [n for n in _SMALL if n != _LAST_SMALL]
            gathered["small"] = all_gather([_pack([value[n] for n in early])], "gather_small_grads")[0]
        elif point == "small_done":
            after = update_small([n for n in _SMALL if n != _LAST_SMALL], gathered["small"], "small", after)
        return after

    out_g, out_d, out_m, out_v = {}, {}, {}, {}

    def update_small(names, all_terms, tag, after=None):
        shapes = [w[n].shape for n in names]
        full_shapes = [(CONV_K, NDEV * w[n].shape[1]) if n in _CONV else w[n].shape for n in names]
        grads = {}
        for n, t in zip(names, _unpack(sum_slots(all_terms, f"sum_{tag}_grads"), full_shapes)):
            if n in _CONV:
                cols = w[n].shape[1]
                t = lax.dynamic_slice_in_dim(t, me * cols, cols, axis=1)
            grads[n] = t
        res = adamw(
            _pack([w[n] for n in names]), _pack([mom[n] for n in names]), _pack([var[n] for n in names]),
            _pack([grads[n] for n in names])[None], f"adamw_{tag}",
        )
        if after is not None:
            res, after = lax.optimization_barrier((res, after))
        for dst, packed_out in zip((out_g, out_d, out_m, out_v), res):
            for n, t in zip(names, _unpack(packed_out, shapes)):
                dst[n] = t
        return after

    core = jnp.reshape(cc, (1,)).astype(jnp.int32)
    chip = jnp.reshape(2 * cx + cy, (1,)).astype(jnp.int32)
    pair_flying, chip_flying = [], []

    def tie(value, after):
        if after is None:
            return value, None
        return lax.optimization_barrier((value, after))

    def to_chips(after):
        n, flying = pair_flying.pop()
        landed = _split_wait(_pair_copies, *flying, f"reduce_pair_wait_{n}")
        summed = pair_sum(flying[2], landed, core, f"pair_sum_{n}")
        *flying, token = _split_start(_chip_copies, summed, summed.shape, 3, f"reduce_chips_{n}")
        token, after = tie(token, after)
        chip_flying.append((n, flying + [token]))
        return after

    def update(after, behind=None):
        n, flying = chip_flying.pop(0)
        if behind is not None:
            flying[4], _ = lax.optimization_barrier((flying[4], behind))
        landed = _split_wait(_chip_copies, *flying, f"reduce_chips_wait_{n}")
        turn = (lambda t: t.T) if n == "l1_w_in" else (lambda t: t)
        res = adamw_reduced(turn(w[n]), turn(mom[n]), turn(var[n]), flying[2], landed, chip, f"adamw_{n}")
        res, after = tie(res, after)
        out_g[n], out_d[n], out_m[n], out_v[n] = [turn(t) for t in res]
        return after, res[0]

    def on_grad(n, term, after):
        if n is None:
            return to_chips(after)
        if n in _ROW_SHARDED or n == "l1_w_in":
            term = term.reshape(NDEV, -1, D)
        else:
            term = term.reshape(NDEV, D, -1)
        term = term.reshape((4, 2) + term.shape[1:])
        *flying, token = _split_start(_pair_copies, term, term.shape[:1] + term.shape[2:], 4, f"reduce_pair_{n}")
        token, after = tie(token, after)
        if len(chip_flying) == UPDATE_LAG:
            after, _ = update(after)
        if pair_flying:
            after = to_chips(after)
        pair_flying.append((n, flying + [token]))
        return after

    for n in _BIG:
        start_gather(n)
    dx, g = local_step(xs, target, wts, at, on_grad)
    last = _pack([g[_LAST_SMALL]])
    *flying, done = _split_start(_direct_copies, last, (NDEV,) + last.shape, NDEV - 1, "gather_last_grad")
    while len(chip_flying) > 1:
        _, done = update(None, behind=done)
    landed = _split_wait(_direct_copies, *flying, done, "gather_last_grad_wait")
    update_small([_LAST_SMALL], lax.dynamic_update_slice(landed, last[None], (me, 0, 0)), "last")
    update(None, behind=out_g[_LAST_SMALL])
    loss = loss_sum[0]

    return (loss, dx[None], *[out_g[n] for n in _WEIGHTS], *[out_d[n] for n in _WEIGHTS],
            *[out_m[n] for n in _WEIGHTS], *[out_v[n] for n in _WEIGHTS])
```

```python
import functools
import math

import jax
import jax.numpy as jnp
from jax import lax
from jax.experimental import pallas as pl
from jax.experimental.pallas import tpu as pltpu
from jax.experimental.pallas import tpu_sc as plsc

F32 = jnp.float32
BF16 = jnp.bfloat16
HD = 128
EPS = 1e-6
CONV_K = 3
VMEM_LIMIT_BYTES = 48 << 20
NDEV = 8
MESH = pl.DeviceIdType.MESH

ADAM_LR = 0.001
ADAM_B1 = 0.9
ADAM_B2 = 0.999
ADAM_EPS = 1e-08
ADAM_WD = 0.01
ADAM_STEP = 10


def _cp(sem):
    return pltpu.CompilerParams(dimension_semantics=sem, vmem_limit_bytes=VMEM_LIMIT_BYTES)


def _pick(n, prefs):
    for p in prefs:
        if n % p == 0:
            return p
    return n


def _dot(a, b):
    return jnp.dot(a, b, preferred_element_type=F32)


def _dot_nt(a, b):
    return lax.dot_general(a, b, (((1,), (1,)), ((), ())), preferred_element_type=F32)


def _dot_tn(a, b):
    return lax.dot_general(a, b, (((0,), (0,)), ((), ())), preferred_element_type=F32)


def _split3(x):
    hi = x.astype(BF16)
    r = x - hi.astype(F32)
    mid = r.astype(BF16)
    lo = (r - mid.astype(F32)).astype(BF16)
    return hi, mid, lo


def _dot_ones_left(ones_bf16, x):
    hi, mid, lo = _split3(x)
    return _dot(ones_bf16, hi) + _dot(ones_bf16, mid) + _dot(ones_bf16, lo)


def _iota2(shape, axis):
    return lax.broadcasted_iota(jnp.int32, shape, axis)


def mm_nn(a, w2d, nb, name, out_dtype=BF16, res=None, tm=None, tn=None, tk=None, a_map=None, a_shape=None):
    M, K = a_shape or a.shape
    n = w2d.shape[1]
    assert w2d.shape[0] == nb * K or (nb == 1 and w2d.shape[0] > K)
    a_map = a_map or (lambda i, k: (i, k))
    tm = tm or _pick(M, (1024, 512, 256, 128))
    tn = tn or _pick(n, (1408, 1024, 768, 512, 256, 128))
    tk = tk or (K if K <= 2048 else _pick(K, (1408, 1024, 512, 256, 128)))
    nk, nt = K // tk, n // tn
    has_res = res is not None

    def body(*refs):
        if has_res:
            a_ref, w_ref, r_ref, o_ref = refs[:4]
        else:
            a_ref, w_ref, o_ref = refs[:3]
            r_ref = None
        part = _dot(a_ref[...], w_ref[...])

        def finish(acc):
            if r_ref is not None:
                acc = acc + r_ref[...].astype(F32)
            o_ref[...] = acc.astype(o_ref.dtype)

        if nk == 1:
            finish(part)
        else:
            acc_ref = refs[-1]
            k = pl.program_id(3)

            @pl.when(k == 0)
            def _():
                acc_ref[...] = part

            @pl.when(k > 0)
            def _():
                acc_ref[...] += part

            @pl.when(k == nk - 1)
            def _():
                finish(acc_ref[...])

    in_specs = [
        pl.BlockSpec((tm, tk), lambda i, j, t, k: a_map(i, k)),
        pl.BlockSpec((tk, tn), lambda i, j, t, k: (j * nk + k, t)),
    ]
    args = [a, w2d]
    out_spec = pl.BlockSpec((tm, tn), lambda i, j, t, k: (i, j * nt + t))
    if has_res:
        in_specs.append(out_spec)
        args.append(res)
    return pl.pallas_call(
        body,
        grid=(M // tm, nb, nt, nk),
        in_specs=in_specs,
        out_specs=out_spec,
        out_shape=jax.ShapeDtypeStruct((M, nb * n), out_dtype),
        scratch_shapes=[pltpu.VMEM((tm, tn), F32)] if nk > 1 else [],
        compiler_params=_cp(("parallel", "parallel", "parallel", "arbitrary")),
        name=name,
    )(*args)


def mm_nt(dy2d, w2d, nb, M, K, name, out_dtype=BF16, res=None, dy_maps=None, tm=None, tko=None, tn=None):
    n = w2d.shape[1]
    assert w2d.shape[0] == nb * K or (nb == 1 and w2d.shape[0] > K)
    tm = tm or _pick(M, (1024, 512, 256, 128))
    tko = tko or _pick(K, (1024, 512, 256, 128))
    tn = tn or _pick(n, (1408, 1024, 768, 512, 256, 128))
    nt, nko = n // tn, K // tko
    has_res = res is not None
    if dy_maps is None:
        dy_maps = [lambda i, j, t: (i, j * nt + t)]
    nd = len(dy_maps)
    td = tn // nd

    one_step = nb * nt == 1

    def body(*refs):
        d_refs, w_ref = refs[:nd], refs[nd]
        r_ref = refs[nd + 1] if has_res else None
        d = d_refs[0][...] if nd == 1 else jnp.concatenate([r[...] for r in d_refs], axis=1)
        part = _dot_nt(d, w_ref[...])
        if one_step:
            o_ref = refs[-1]
            if r_ref is not None:
                part = part + r_ref[...].astype(F32)
            o_ref[...] = part.astype(o_ref.dtype)
            return
        o_ref, acc_ref = refs[-2], refs[-1]
        j, t = pl.program_id(2), pl.program_id(3)
        first = jnp.logical_and(j == 0, t == 0)
        last = jnp.logical_and(j == nb - 1, t == nt - 1)

        @pl.when(first)
        def _():
            acc_ref[...] = part

        @pl.when(jnp.logical_not(first))
        def _():
            acc_ref[...] += part

        @pl.when(last)
        def _():
            acc = acc_ref[...]
            if r_ref is not None:
                acc = acc + r_ref[...].astype(F32)
            o_ref[...] = acc.astype(o_ref.dtype)

    in_specs = [pl.BlockSpec((tm, td), functools.partial(lambda f, i, ko, j, t: f(i, j, t), f)) for f in dy_maps]
    in_specs.append(pl.BlockSpec((tko, tn), lambda i, ko, j, t: (j * nko + ko, t)))
    args = [dy2d] * nd + [w2d]
    out_spec = pl.BlockSpec((tm, tko), lambda i, ko, j, t: (i, ko))
    if has_res:
        in_specs.append(out_spec)
        args.append(res)
    return pl.pallas_call(
        body,
        grid=(M // tm, nko, nb, nt),
        in_specs=in_specs,
        out_specs=out_spec,
        out_shape=jax.ShapeDtypeStruct((M, K), out_dtype),
        scratch_shapes=[] if one_step else [pltpu.VMEM((tm, tko), F32)],
        compiler_params=_cp(("parallel", "parallel", "arbitrary", "arbitrary")),
        name=name,
    )(*args)


def mm_tn(x, dy2d, nb, n, name, out_dtype=BF16, dy_maps=None, tko=None, tn=None, x_map=None, x_shape=None):
    S, K = x_shape or x.shape
    x_map = x_map or (lambda ko: (0, ko))
    tko = tko or _pick(K, (512, 256, 128))
    tn = tn or _pick(n, (1408, 1024, 768, 512, 256, 128))
    nt, nko = n // tn, K // tko
    if dy_maps is None:
        dy_maps = [lambda j, t: (0, j * nt + t)]
    nd = len(dy_maps)
    td = tn // nd

    def body(*refs):
        x_ref, d_refs, o_ref = refs[0], refs[1 : 1 + nd], refs[-1]
        d = d_refs[0][...] if nd == 1 else jnp.concatenate([r[...] for r in d_refs], axis=1)
        o_ref[...] = _dot_tn(x_ref[...], d).astype(o_ref.dtype)

    in_specs = [pl.BlockSpec((S, tko), lambda ko, j, t: x_map(ko))]
    in_specs += [pl.BlockSpec((S, td), functools.partial(lambda f, ko, j, t: f(j, t), f)) for f in dy_maps]
    return pl.pallas_call(
        body,
        grid=(nko, nb, nt),
        in_specs=in_specs,
        out_specs=pl.BlockSpec((tko, tn), lambda ko, j, t: (j * nko + ko, t)),
        out_shape=jax.ShapeDtypeStruct((nb * K, n), out_dtype),
        compiler_params=_cp(("parallel", "parallel", "parallel")),
        name=name,
    )(x, *([dy2d] * nd))


def rms_fwd(x, g, name):
    S, D = x.shape
    tm = _pick(S, (256, 128))

    def body(x_ref, g_ref, o_ref):
        xf = x_ref[...]
        r = lax.rsqrt(jnp.mean(xf * xf, axis=-1, keepdims=True) + EPS)
        o_ref[...] = (xf * r * g_ref[...]).astype(o_ref.dtype)

    return pl.pallas_call(
        body,
        grid=(S // tm,),
        in_specs=[pl.BlockSpec((tm, D), lambda i: (i, 0)), pl.BlockSpec((1, D), lambda i: (0, 0))],
        out_specs=pl.BlockSpec((tm, D), lambda i: (i, 0)),
        out_shape=jax.ShapeDtypeStruct((S, D), BF16),
        compiler_params=_cp(("parallel",)),
        name=name,
    )(x, g.reshape(1, D))


def rms_bwd(x, g, dh, dres, name):
    S, D = x.shape
    tm = _pick(S, (256, 128))

    def body(x_ref, g_ref, dh_ref, dr_ref, dx_ref, dxb_ref, dg_ref):
        i = pl.program_id(0)
        xf = x_ref[...]
        dh = dh_ref[...].astype(F32)
        r = lax.rsqrt(jnp.mean(xf * xf, axis=-1, keepdims=True) + EPS)
        gy = dh * g_ref[...]
        proj = jnp.mean(gy * xf, axis=-1, keepdims=True)
        dx = dr_ref[...] + r * gy - xf * (r * r * r * proj)
        dx_ref[...] = dx
        dxb_ref[...] = dx.astype(BF16)
        dg = jnp.sum(dh * (xf * r), axis=0, keepdims=True)

        @pl.when(i == 0)
        def _():
            dg_ref[...] = dg

        @pl.when(i > 0)
        def _():
            dg_ref[...] += dg

    row = pl.BlockSpec((tm, D), lambda i: (i, 0))
    vec = pl.BlockSpec((1, D), lambda i: (0, 0))
    return pl.pallas_call(
        body,
        grid=(S // tm,),
        in_specs=[row, vec, row, row],
        out_specs=[row, row, vec],
        out_shape=[jax.ShapeDtypeStruct((S, D), F32), jax.ShapeDtypeStruct((S, D), BF16), jax.ShapeDtypeStruct((1, D), F32)],
        compiler_params=_cp(("arbitrary",)),
        name=name,
    )(x, g.reshape(1, D), dh, dres)


def loss_head(x, g, target, name):
    S, D = x.shape
    tm = _pick(S, (256, 128))

    def body(x_ref, g_ref, t_ref, dx_ref, dxb_ref, dg_ref, loss_ref):
        i = pl.program_id(0)
        xf = x_ref[...]
        gg = g_ref[...]
        r = lax.rsqrt(jnp.mean(xf * xf, axis=-1, keepdims=True) + EPS)
        xh = xf * r
        err = xh * gg - t_ref[...]
        part = (0.5 / D) * jnp.sum(err * err)
        dy = err * (1.0 / D)
        gy = dy * gg
        proj = jnp.mean(gy * xf, axis=-1, keepdims=True)
        dx = r * gy - xf * (r * r * r * proj)
        dx_ref[...] = dx
        dxb_ref[...] = dx.astype(BF16)
        dg = jnp.sum(dy * xh, axis=0, keepdims=True)
        lossb = jnp.full(loss_ref.shape, part, F32)

        @pl.when(i == 0)
        def _():
            dg_ref[...] = dg
            loss_ref[...] = lossb

        @pl.when(i > 0)
        def _():
            dg_ref[...] += dg
            loss_ref[...] += lossb

    row = pl.BlockSpec((tm, D), lambda i: (i, 0))
    vec = pl.BlockSpec((1, D), lambda i: (0, 0))
    return pl.pallas_call(
        body,
        grid=(S // tm,),
        in_specs=[row, vec, row],
        out_specs=[row, row, vec, pl.BlockSpec((8, 128), lambda i: (0, 0))],
        out_shape=[
            jax.ShapeDtypeStruct((S, D), F32),
            jax.ShapeDtypeStruct((S, D), BF16),
            jax.ShapeDtypeStruct((1, D), F32),
            jax.ShapeDtypeStruct((8, 128), F32),
        ],
        compiler_params=_cp(("arbitrary",)),
        name=name,
    )(x, g.reshape(1, D), target)


def _shift_down(s, k):
    if k == 0:
        return s
    return jnp.where(_iota2(s.shape, 0) >= k, pltpu.roll(s, k, axis=0), 0.0)


def _shift_up(s, k):
    if k == 0:
        return s
    n = s.shape[0]
    return jnp.where(_iota2(s.shape, 0) < n - k, pltpu.roll(s, n - k, axis=0), 0.0)


def _conv(s, w):
    return w[0:1] * _shift_down(s, 2) + w[1:2] * _shift_down(s, 1) + w[2:3] * s


def _conv_t(d, w):
    return w[2:3] * d + w[1:2] * _shift_up(d, 1) + w[0:1] * _shift_up(d, 2)


def _conv_dw(d, s):
    return [jnp.sum(d * _shift_down(s, CONV_K - 1 - k), axis=0, keepdims=True) for k in range(CONV_K)]


def sc_fwd(p, convw, cat, W, name):
    S = p.shape[0]
    tc = _pick(W, (256, 128))
    nc = W // tc

    def body(gb_ref, gc_ref, hi_ref, w_ref, cat_ref, o_ref):
        s = gc_ref[...].astype(F32) * hi_ref[...].astype(F32)
        o_ref[...] = (gb_ref[...].astype(F32) * _conv(s, w_ref[...])).astype(o_ref.dtype)

    col = lambda part: pl.BlockSpec((S, tc), lambda c: (0, part * nc + c))
    return pl.pallas_call(
        body,
        grid=(nc,),
        in_specs=[col(3), col(4), col(5), pl.BlockSpec((CONV_K, tc), lambda c: (0, c)), pl.BlockSpec(memory_space=pl.ANY)],
        out_specs=col(1),
        out_shape=jax.ShapeDtypeStruct(cat.shape, cat.dtype),
        input_output_aliases={4: 0},
        compiler_params=_cp(("parallel",)),
        name=name,
    )(p, p, p, convw, cat)


def sc_bwd(p, convw, dcat, dp, W, name):
    S = p.shape[0]
    tc = _pick(W, (256, 128))
    nc = W // tc

    def body(gb_ref, gc_ref, hi_ref, w_ref, do_ref, dp_in_ref, dp_ref, dw_ref):
        gb = gb_ref[...].astype(F32)
        gc = gc_ref[...].astype(F32)
        hi = hi_ref[...].astype(F32)
        w = w_ref[...]
        do = do_ref[...].astype(F32)
        s = gc * hi
        dcs = do * gb
        ds = _conv_t(dcs, w)
        dp_ref[0] = (do * _conv(s, w)).astype(dp_ref.dtype)
        dp_ref[1] = (ds * hi).astype(dp_ref.dtype)
        dp_ref[2] = (ds * gc).astype(dp_ref.dtype)
        for k, row in enumerate(_conv_dw(dcs, s)):
            dw_ref[k : k + 1, :] = row

    col = lambda part: pl.BlockSpec((S, tc), lambda c: (0, part * nc + c))
    return pl.pallas_call(
        body,
        grid=(nc,),
        in_specs=[
            col(3), col(4), col(5),
            pl.BlockSpec((CONV_K, tc), lambda c: (0, c)),
            pl.BlockSpec((S, tc), lambda c: (0, nc + c)),
            pl.BlockSpec(memory_space=pl.ANY),
        ],
        out_specs=[pl.BlockSpec((3, S, tc), lambda c: (1, 0, c)), pl.BlockSpec((CONV_K, tc), lambda c: (0, c))],
        out_shape=[jax.ShapeDtypeStruct(dp.shape, dp.dtype), jax.ShapeDtypeStruct((CONV_K, W), F32)],
        input_output_aliases={5: 0},
        compiler_params=_cp(("parallel",)),
        name=name,
    )(p, p, p, convw, dcat, dp)


def _silu_parts(a):
    sig = 1.0 / (1.0 + jnp.exp(-a))
    return a * sig, sig


def ffn_act_fwd(u, convw, F, name):
    S = u.shape[0]
    tc = _pick(F, (256, 128))
    nc = F // tc

    def body(ug_ref, uu_ref, wg_ref, wu_ref, o_ref):
        ag = _conv(ug_ref[...].astype(F32), wg_ref[...])
        au = _conv(uu_ref[...].astype(F32), wu_ref[...])
        o_ref[...] = (_silu_parts(ag)[0] * au).astype(o_ref.dtype)

    col = lambda half: pl.BlockSpec((S, tc), lambda c: (0, half * nc + c))
    wcol = lambda half: pl.BlockSpec((CONV_K, tc), lambda c: (0, half * nc + c))
    return pl.pallas_call(
        body,
        grid=(nc,),
        in_specs=[col(0), col(1), wcol(0), wcol(1)],
        out_specs=pl.BlockSpec((S, tc), lambda c: (0, c)),
        out_shape=jax.ShapeDtypeStruct((S, F), BF16),
        compiler_params=_cp(("parallel",)),
        name=name,
    )(u, u, convw, convw)


def ffn_act_bwd(u, convw, dact, F, name):
    S = u.shape[0]
    tc = _pick(F, (256, 128))
    nc = F // tc

    def body(ug_ref, uu_ref, wg_ref, wu_ref, da_ref, du_ref, dw_ref):
        ug = ug_ref[...].astype(F32)
        uu = uu_ref[...].astype(F32)
        wg = wg_ref[...]
        wu = wu_ref[...]
        da = da_ref[...].astype(F32)
        ag = _conv(ug, wg)
        au = _conv(uu, wu)
        sl, sig = _silu_parts(ag)
        dag = da * au * (sig * (1.0 + ag * (1.0 - sig)))
        dau = da * sl
        du_ref[0] = _conv_t(dag, wg).astype(du_ref.dtype)
        du_ref[1] = _conv_t(dau, wu).astype(du_ref.dtype)
        for k, (rg, ru) in enumerate(zip(_conv_dw(dag, ug), _conv_dw(dau, uu))):
            dw_ref[0, k : k + 1, :] = rg
            dw_ref[1, k : k + 1, :] = ru

    col = lambda half: pl.BlockSpec((S, tc), lambda c: (0, half * nc + c))
    wcol = lambda half: pl.BlockSpec((CONV_K, tc), lambda c: (0, half * nc + c))
    return pl.pallas_call(
        body,
        grid=(nc,),
        in_specs=[col(0), col(1), wcol(0), wcol(1), pl.BlockSpec((S, tc), lambda c: (0, c))],
        out_specs=[pl.BlockSpec((2, S, tc), lambda c: (0, 0, c)), pl.BlockSpec((2, CONV_K, tc), lambda c: (0, 0, c))],
        out_shape=[jax.ShapeDtypeStruct((2, S, F), BF16), jax.ShapeDtypeStruct((2, CONV_K, F), F32)],
        compiler_params=_cp(("parallel",)),
        name=name,
    )(u, u, convw, convw, dact)


def _softplus(z):
    return jnp.maximum(z, 0.0) + jnp.log(1.0 + jnp.exp(-jnp.abs(z)))


def _key_strip(S):
    return _pick(S, (512, 256, 128))


def _query_rows(S):
    tq = _pick(S, (512, 256, 128))
    assert _key_strip(S) % tq == 0
    return tq


def _split2(x):
    hi = x.astype(BF16)
    return hi, (x - hi.astype(F32)).astype(BF16)


def _block_sums(x, ones_bf16):
    hi, lo = _split2(x)
    return [
        _dot(hi[:, b * HD : (b + 1) * HD], ones_bf16) + _dot(lo[:, b * HD : (b + 1) * HD], ones_bf16)
        for b in range(x.shape[1] // HD)
    ]


def _strip_mask(shape, row0, off, strict):
    cols, rows = _iota2(shape, 1) + off, _iota2(shape, 0) + row0
    return cols < rows if strict else cols <= rows


def _sb_strip(q, ks, row0, off, run, su, masked):
    z = _dot_nt(q, ks) * (HD ** -0.5)
    sp = _softplus(z)
    mask = _strip_mask(z.shape, row0, off, True) if masked else None
    l = jnp.where(mask, -sp, 0.0) if masked else -sp
    within = _block_sums(l, su)
    later = [None] * len(within)
    for b in reversed(range(len(within))):
        later[b] = within[b] + run
        run = run + jnp.sum(l[:, b * HD : (b + 1) * HD], axis=1, keepdims=True)
    a = jnp.exp(z - sp + jnp.concatenate(later, axis=1))
    return z, (jnp.where(mask, a, 0.0) if masked else a), run


def sb_fwd(p, W, name):
    S = p.shape[0]
    TQ, TK = _query_rows(S), _key_strip(S)
    nh, nq = W // HD, S // TQ

    def body(q_ref, k_ref, v_ref, o_ref):
        i = pl.program_id(1)
        q = q_ref[...]
        su = (_iota2((HD, HD), 0) > _iota2((HD, HD), 1)).astype(BF16)
        last = (i * TQ + TQ - 1) // TK

        def strip(g, carry, masked):
            acc, run = carry
            off = pl.multiple_of(g * TK, TK)
            _, a, run = _sb_strip(q, k_ref[pl.ds(off, TK), :], i * TQ, off, run, su, masked)
            return acc + _dot(a.astype(BF16), v_ref[pl.ds(off, TK), :]), run

        carry = strip(last, (jnp.zeros((TQ, HD), F32), jnp.zeros((TQ, 1), F32)), True)
        acc, _ = lax.fori_loop(0, last, lambda gg, c: strip(last - 1 - gg, c, False), carry)
        o_ref[...] = acc.astype(o_ref.dtype)

    return pl.pallas_call(
        body,
        grid=(nh, nq),
        in_specs=[
            pl.BlockSpec((TQ, HD), lambda h, i: (i, h)),
            pl.BlockSpec((S, HD), lambda h, i: (0, nh + h)),
            pl.BlockSpec((S, HD), lambda h, i: (0, 2 * nh + h)),
        ],
        out_specs=pl.BlockSpec((TQ, HD), lambda h, i: (i, h)),
        out_shape=jax.ShapeDtypeStruct((S, 2 * W), BF16),
        compiler_params=_cp(("parallel", "arbitrary")),
        name=name,
    )(p, p, p)


def sb_bwd(p, dcat, W, name):
    S = p.shape[0]
    TQ, TK = _query_rows(S), _key_strip(S)
    nh, nq = W // HD, S // TQ
    scale = HD ** -0.5

    def body(q_ref, k_ref, v_ref, do_ref, dp_ref, dk_acc, dv_acc, e_scr, z_scr):
        i = pl.program_id(1)
        q = q_ref[...]
        do = do_ref[...]
        su = (_iota2((HD, HD), 0) > _iota2((HD, HD), 1)).astype(BF16)
        sl = (_iota2((HD, HD), 0) < _iota2((HD, HD), 1)).astype(BF16)
        last = (i * TQ + TQ - 1) // TK

        @pl.when(i == 0)
        def _():
            dk_acc[...] = jnp.zeros_like(dk_acc)
            dv_acc[...] = jnp.zeros_like(dv_acc)

        def pass_a(g, run, masked):
            off = pl.multiple_of(g * TK, TK)
            z, a, run = _sb_strip(q, k_ref[pl.ds(off, TK), :], i * TQ, off, run, su, masked)
            e_scr[g] = a * _dot_nt(do, v_ref[pl.ds(off, TK), :])
            z_scr[g] = z
            dv_acc[pl.ds(off, TK), :] += _dot_tn(a.astype(BF16), do)
            return run

        run = pass_a(last, jnp.zeros((TQ, 1), F32), True)
        lax.fori_loop(0, last, lambda gg, r: pass_a(last - 1 - gg, r, False), run)

        def pass_b(g, carry, masked):
            dq, run_e = carry
            off = pl.multiple_of(g * TK, TK)
            e = e_scr[g]
            z = z_scr[g]
            within = _block_sums(e, sl)
            before = []
            for b in range(len(within)):
                before.append(within[b] + run_e)
                run_e = run_e + jnp.sum(e[:, b * HD : (b + 1) * HD], axis=1, keepdims=True)
            sig = 1.0 / (1.0 + jnp.exp(-z))
            dz = e * (1.0 - sig) - jnp.concatenate(before, axis=1) * sig
            if masked:
                dz = jnp.where(_strip_mask(z.shape, i * TQ, off, True), dz, 0.0)
            dz = (dz * scale).astype(BF16)
            dq = dq + _dot(dz, k_ref[pl.ds(off, TK), :])
            dk_acc[pl.ds(off, TK), :] += _dot_tn(dz, q)
            return dq, run_e

        carry = lax.fori_loop(0, last, lambda g, c: pass_b(g, c, False), (jnp.zeros((TQ, HD), F32), jnp.zeros((TQ, 1), F32)))
        dq, _ = pass_b(last, carry, True)
        dp_ref[0, pl.ds(pl.multiple_of(i * TQ, TQ), TQ), :] = dq.astype(dp_ref.dtype)

        @pl.when(i == nq - 1)
        def _():
            dp_ref[1] = dk_acc[...].astype(dp_ref.dtype)
            dp_ref[2] = dv_acc[...].astype(dp_ref.dtype)

    return pl.pallas_call(
        body,
        grid=(nh, nq),
        in_specs=[
            pl.BlockSpec((TQ, HD), lambda h, i: (i, h)),
            pl.BlockSpec((S, HD), lambda h, i: (0, nh + h)),
            pl.BlockSpec((S, HD), lambda h, i: (0, 2 * nh + h)),
            pl.BlockSpec((TQ, HD), lambda h, i: (i, h)),
        ],
        out_specs=pl.BlockSpec((3, S, HD), lambda h, i: (0, 0, h)),
        out_shape=jax.ShapeDtypeStruct((6, S, W), BF16),
        scratch_shapes=[
            pltpu.VMEM((S, HD), F32),
            pltpu.VMEM((S, HD), F32),
            pltpu.VMEM((S // TK, TQ, TK), F32),
            pltpu.VMEM((S // TK, TQ, TK), F32),
        ],
        compiler_params=_cp(("parallel", "arbitrary")),
        name=name,
    )(p, p, p, dcat)


def fox_gate_fwd(f, b, name):
    S = f.shape[0]
    nq = S // HD

    def body(f_ref, b_ref, c_ref, run):
        i = pl.program_id(0)

        @pl.when(i == 0)
        def _():
            run[...] = jnp.zeros_like(run)

        lf = -_softplus(-(f_ref[...] + b_ref[...]))
        tri = (_iota2((HD, HD), 0) >= _iota2((HD, HD), 1)).astype(BF16)
        c_ref[...] = _dot_ones_left(tri, lf) + run[...]
        run[...] += jnp.sum(lf, axis=0, keepdims=True)

    return pl.pallas_call(
        body,
        grid=(nq,),
        in_specs=[pl.BlockSpec((HD, 128), lambda i: (i, 0)), pl.BlockSpec((1, 128), lambda i: (0, 0))],
        out_specs=pl.BlockSpec((HD, 128), lambda i: (i, 0)),
        out_shape=jax.ShapeDtypeStruct((S, 128), F32),
        scratch_shapes=[pltpu.VMEM((1, 128), F32)],
        compiler_params=_cp(("arbitrary",)),
        name=name,
    )(f, b)


def fox_gate_bwd(f, b, dc, name):
    S = f.shape[0]
    nq = S // HD

    def body(f_ref, b_ref, dc_ref, df_ref, db_ref, run):
        i = pl.program_id(0)

        @pl.when(i == 0)
        def _():
            run[...] = jnp.zeros_like(run)

        dc = dc_ref[...]
        tri = (_iota2((HD, HD), 0) <= _iota2((HD, HD), 1)).astype(BF16)
        dlf = _dot_ones_left(tri, dc) + run[...]
        run[...] += jnp.sum(dc, axis=0, keepdims=True)
        x = f_ref[...] + b_ref[...]
        df = dlf * (1.0 / (1.0 + jnp.exp(x)))
        df_ref[...] = df
        db = jnp.sum(df, axis=0, keepdims=True)

        @pl.when(i == 0)
        def _():
            db_ref[...] = db

        @pl.when(i > 0)
        def _():
            db_ref[...] += db

    rev = pl.BlockSpec((HD, 128), lambda i: (nq - 1 - i, 0))
    vec = pl.BlockSpec((1, 128), lambda i: (0, 0))
    return pl.pallas_call(
        body,
        grid=(nq,),
        in_specs=[rev, vec, rev],
        out_specs=[rev, vec],
        out_shape=[jax.ShapeDtypeStruct((S, 128), F32), jax.ShapeDtypeStruct((1, 128), F32)],
        scratch_shapes=[pltpu.VMEM((1, 128), F32)],
        compiler_params=_cp(("arbitrary",)),
        name=name,
    )(f, b, dc)


def _fox_logits(q, ks, ct, cs, row0, off, masked):
    s = _dot_nt(q, ks) * (HD ** -0.5) + (ct - cs)
    if not masked:
        return s, None
    mask = _strip_mask(s.shape, row0, off, False)
    return jnp.where(mask, s, -1e30), mask


def fox_fwd(p, ccol, crow, cat, W, name):
    S = p.shape[0]
    TQ, TK = _query_rows(S), _key_strip(S)
    nh, nq = W // HD, S // TQ

    def body(q_ref, k_ref, v_ref, cc_ref, cr_ref, cat_ref, o_ref, lse_ref):
        i = pl.program_id(1)
        q = q_ref[...]
        ct = cc_ref[0]

        def step(g, carry, masked):
            m, l, acc = carry
            off = pl.multiple_of(g * TK, TK)
            s, _ = _fox_logits(q, k_ref[pl.ds(off, TK), :], ct, cr_ref[0, pl.ds(g, 1), :], i * TQ, off, masked)
            m_new = jnp.maximum(m, jnp.max(s, axis=1, keepdims=True))
            alpha = jnp.exp(m - m_new)
            pr = jnp.exp(s - m_new)
            l = alpha * l + jnp.sum(pr, axis=1, keepdims=True)
            acc = alpha * acc + _dot(pr.astype(BF16), v_ref[pl.ds(off, TK), :])
            return m_new, l, acc

        init = (jnp.full((TQ, 1), -1e30, F32), jnp.zeros((TQ, 1), F32), jnp.zeros((TQ, HD), F32))
        last = (i * TQ + TQ - 1) // TK
        m, l, acc = step(last, lax.fori_loop(0, last, lambda g, c: step(g, c, False), init), True)
        o_ref[...] = (acc / l).astype(o_ref.dtype)
        lse_ref[0] = m + jnp.log(l)

    return pl.pallas_call(
        body,
        grid=(nh, nq),
        in_specs=[
            pl.BlockSpec((TQ, HD), lambda h, i: (i, 2 * nh + h)),
            pl.BlockSpec((S, HD), lambda h, i: (0, 3 * nh + h)),
            pl.BlockSpec((S, HD), lambda h, i: (0, 4 * nh + h)),
            pl.BlockSpec((1, TQ, 1), lambda h, i: (h, i, 0)),
            pl.BlockSpec((1, S // TK, TK), lambda h, i: (h, 0, 0)),
            pl.BlockSpec(memory_space=pl.ANY),
        ],
        out_specs=[pl.BlockSpec((TQ, HD), lambda h, i: (i, nh + h)), pl.BlockSpec((1, TQ, 1), lambda h, i: (h, i, 0))],
        out_shape=[jax.ShapeDtypeStruct(cat.shape, cat.dtype), jax.ShapeDtypeStruct((nh, S, 1), F32)],
        input_output_aliases={5: 0},
        compiler_params=_cp(("parallel", "arbitrary")),
        name=name,
    )(p, p, p, ccol, crow, cat)


def fox_bwd(p, ccol, crow, cat, lse, dcat, dp, W, name):
    S = p.shape[0]
    TQ, TK = _query_rows(S), _key_strip(S)
    nh, nq = W // HD, S // TQ
    scale = HD ** -0.5

    def body(q_ref, k_ref, v_ref, cc_ref, cr_ref, o_ref, lse_ref, do_ref, dp_in_ref, dp_ref, dcs_ref, dct_ref, dk_acc, dv_acc):
        i = pl.program_id(1)
        q = q_ref[...]
        do = do_ref[...]
        ct = cc_ref[0]
        lse_i = lse_ref[0]
        delta = jnp.sum(do.astype(F32) * o_ref[...].astype(F32), axis=1, keepdims=True)

        @pl.when(i == 0)
        def _():
            dk_acc[...] = jnp.zeros_like(dk_acc)
            dv_acc[...] = jnp.zeros_like(dv_acc)
            dcs_ref[...] = jnp.zeros_like(dcs_ref)

        def step(g, carry, masked):
            dq, dct = carry
            off = pl.multiple_of(g * TK, TK)
            ks = k_ref[pl.ds(off, TK), :]
            s, mask = _fox_logits(q, ks, ct, cr_ref[0, pl.ds(g, 1), :], i * TQ, off, masked)
            pr = jnp.where(mask, jnp.exp(s - lse_i), 0.0) if masked else jnp.exp(s - lse_i)
            ds = pr * (_dot_nt(do, v_ref[pl.ds(off, TK), :]) - delta)
            dv_acc[pl.ds(off, TK), :] += _dot_tn(pr.astype(BF16), do)
            dsb = (ds * scale).astype(BF16)
            dk_acc[pl.ds(off, TK), :] += _dot_tn(dsb, q)
            dcs_ref[0, pl.ds(g, 1), :] += jnp.sum(ds, axis=0, keepdims=True)
            return dq + _dot(dsb, ks), dct + jnp.sum(ds, axis=1, keepdims=True)

        last = (i * TQ + TQ - 1) // TK
        carry = lax.fori_loop(0, last, lambda g, c: step(g, c, False), (jnp.zeros((TQ, HD), F32), jnp.zeros((TQ, 1), F32)))
        dq, dct = step(last, carry, True)
        dp_ref[0, pl.ds(pl.multiple_of(i * TQ, TQ), TQ), :] = dq.astype(dp_ref.dtype)
        dct_ref[0] = dct

        @pl.when(i == nq - 1)
        def _():
            dp_ref[1] = dk_acc[...].astype(dp_ref.dtype)
            dp_ref[2] = dv_acc[...].astype(dp_ref.dtype)

    return pl.pallas_call(
        body,
        grid=(nh, nq),
        in_specs=[
            pl.BlockSpec((TQ, HD), lambda h, i: (i, 2 * nh + h)),
            pl.BlockSpec((S, HD), lambda h, i: (0, 3 * nh + h)),
            pl.BlockSpec((S, HD), lambda h, i: (0, 4 * nh + h)),
            pl.BlockSpec((1, TQ, 1), lambda h, i: (h, i, 0)),
            pl.BlockSpec((1, S // TK, TK), lambda h, i: (h, 0, 0)),
            pl.BlockSpec((TQ, HD), lambda h, i: (i, nh + h)),
            pl.BlockSpec((1, TQ, 1), lambda h, i: (h, i, 0)),
            pl.BlockSpec((TQ, HD), lambda h, i: (i, nh + h)),
            pl.BlockSpec(memory_space=pl.ANY),
        ],
        out_specs=[
            pl.BlockSpec((3, S, HD), lambda h, i: (1, 0, h)),
            pl.BlockSpec((1, S // TK, TK), lambda h, i: (h, 0, 0)),
            pl.BlockSpec((1, TQ, 1), lambda h, i: (h, i, 0)),
        ],
        out_shape=[
            jax.ShapeDtypeStruct(dp.shape, dp.dtype),
            jax.ShapeDtypeStruct((nh, S // TK, TK), F32),
            jax.ShapeDtypeStruct((nh, S, 1), F32),
        ],
        input_output_aliases={8: 0},
        scratch_shapes=[pltpu.VMEM((S, HD), F32), pltpu.VMEM((S, HD), F32)],
        compiler_params=_cp(("parallel", "arbitrary")),
        name=name,
    )(p, p, p, ccol, crow, cat, lse, dcat, dp)


_GELU_K = math.sqrt(2.0 / math.pi)
_GELU_C = 0.044715


def _gelu(x):
    return 0.5 * x * (1.0 + jnp.tanh(_GELU_K * (x + _GELU_C * x * x * x)))


def _gelu_grad(x):
    t = jnp.tanh(_GELU_K * (x + _GELU_C * x * x * x))
    return 0.5 * (1.0 + t) + 0.5 * x * (1.0 - t * t) * (_GELU_K * (1.0 + 3.0 * _GELU_C * x * x))


def _layernorm_parts(gv):
    xc = gv - jnp.mean(gv, axis=-1, keepdims=True)
    r = lax.rsqrt(jnp.mean(xc * xc, axis=-1, keepdims=True) + EPS)
    return xc * r, r


def sg_fwd(p, sg_w, sg_bt, sg_g, W, name):
    S = p.shape[0]
    G, nq = W // HD, S // HD

    def body(u_ref, v_ref, w_ref, bt_ref, g_ref, o_ref):
        xh, _ = _layernorm_parts(_gelu(v_ref[...].astype(F32)))
        vn = (xh * g_ref[...]).astype(BF16)
        tri = _iota2((HD, HD), 0) >= _iota2((HD, HD), 1)
        for gi in range(G):
            cols = slice(gi * HD, (gi + 1) * HD)
            wt = jnp.where(tri, w_ref[gi], 0.0).astype(BF16)
            mixed = _dot(wt, vn[:, cols]) + bt_ref[:, gi : gi + 1]
            o_ref[:, cols] = (_gelu(u_ref[:, cols].astype(F32)) * mixed).astype(o_ref.dtype)

    return pl.pallas_call(
        body,
        grid=(nq,),
        in_specs=[
            pl.BlockSpec((HD, W), lambda i: (i, 0)),
            pl.BlockSpec((HD, W), lambda i: (i, 1)),
            pl.BlockSpec((G, HD, HD), lambda i: (0, 0, 0)),
            pl.BlockSpec((HD, G), lambda i: (0, 0)),
            pl.BlockSpec((1, W), lambda i: (0, 0)),
        ],
        out_specs=pl.BlockSpec((HD, W), lambda i: (i, 0)),
        out_shape=jax.ShapeDtypeStruct((S, 2 * W), BF16),
        compiler_params=_cp(("parallel",)),
        name=name,
    )(p, p, sg_w, sg_bt, sg_g.reshape(1, W))


def sg_bwd(p, sg_w, sg_bt, sg_g, dcat, W, name):
    S = p.shape[0]
    G, nq = W // HD, S // HD

    def body(u_ref, v_ref, w_ref, bt_ref, g_ref, do_ref, dp_ref, dw_ref, dbt_ref, dg_ref, dvn_scr):
        i = pl.program_id(0)

        @pl.when(i == 0)
        def _():
            dw_ref[...] = jnp.zeros_like(dw_ref)
            dbt_ref[...] = jnp.zeros_like(dbt_ref)
            dg_ref[...] = jnp.zeros_like(dg_ref)

        v = v_ref[...].astype(F32)
        xh, r = _layernorm_parts(_gelu(v))
        gg = g_ref[...]
        vn = (xh * gg).astype(BF16)
        tri = _iota2((HD, HD), 0) >= _iota2((HD, HD), 1)
        for gi in range(G):
            cols = slice(gi * HD, (gi + 1) * HD)
            wt = jnp.where(tri, w_ref[gi], 0.0).astype(BF16)
            mixed = _dot(wt, vn[:, cols]) + bt_ref[:, gi : gi + 1]
            u = u_ref[:, cols].astype(F32)
            do = do_ref[:, cols].astype(F32)
            dp_ref[0, :, cols] = (do * mixed * _gelu_grad(u)).astype(dp_ref.dtype)
            dmix = do * _gelu(u)
            dmb = dmix.astype(BF16)
            dw_ref[gi] += jnp.where(tri, _dot_nt(dmb, vn[:, cols]), 0.0)
            dbt_ref[:, gi : gi + 1] += jnp.sum(dmix, axis=1, keepdims=True)
            dvn_scr[:, cols] = _dot_tn(wt, dmb)
        dvn = dvn_scr[...]
        dg_ref[...] += jnp.sum(dvn * xh, axis=0, keepdims=True)
        dxh = dvn * gg
        dgv = r * (dxh - jnp.mean(dxh, axis=-1, keepdims=True) - xh * jnp.mean(dxh * xh, axis=-1, keepdims=True))
        dp_ref[1] = (dgv * _gelu_grad(v)).astype(dp_ref.dtype)

    return pl.pallas_call(
        body,
        grid=(nq,),
        in_specs=[
            pl.BlockSpec((HD, W), lambda i: (i, 0)),
            pl.BlockSpec((HD, W), lambda i: (i, 1)),
            pl.BlockSpec((G, HD, HD), lambda i: (0, 0, 0)),
            pl.BlockSpec((HD, G), lambda i: (0, 0)),
            pl.BlockSpec((1, W), lambda i: (0, 0)),
            pl.BlockSpec((HD, W), lambda i: (i, 0)),
        ],
        out_specs=[
            pl.BlockSpec((2, HD, W), lambda i: (0, i, 0)),
            pl.BlockSpec((G, HD, HD), lambda i: (0, 0, 0)),
            pl.BlockSpec((HD, G), lambda i: (0, 0)),
            pl.BlockSpec((1, W), lambda i: (0, 0)),
        ],
        out_shape=[
            jax.ShapeDtypeStruct((6, S, W), BF16),
            jax.ShapeDtypeStruct((G, HD, HD), F32),
            jax.ShapeDtypeStruct((HD, G), F32),
            jax.ShapeDtypeStruct((1, W), F32),
        ],
        scratch_shapes=[pltpu.VMEM((HD, W), F32)],
        compiler_params=_cp(("arbitrary",)),
        name=name,
    )(p, p, sg_w, sg_bt, sg_g.reshape(1, W), dcat)


def local_step(x, target, wts, at, on_grad):
    S, D = x.shape
    W = D // 2
    nb, F = wts["nb"], wts["F"]
    g = {}

    def ffn_fwd(xin, l):
        h = rms_fwd(xin, wts[f"{l}_ffn_norm_g"], f"{l}_ffn_rms")
        u = mm_nn(h, wts[f"{l}_ffn_up"], nb, f"{l}_ffn_up_mm")
        act = ffn_act_fwd(u, wts[f"{l}_ffn_conv_w"], F, f"{l}_ffn_act")
        half_tile = _pick(S, (512, 256, 128))
        xout = mm_nn(act, wts[f"{l}_ffn_down"], 1, f"{l}_ffn_down_mm", out_dtype=F32, res=xin,
                     tm=half_tile, tn=_pick(D, (512, 256, 128)), tk=F)
        return xout, (xin, h, u, act)

    def ffn_bwd(dxout, dxoutb, saved, l):
        xin, h, u, act = saved
        dact = mm_nt(dxoutb, wts[f"{l}_ffn_down"], 1, S, F, f"{l}_ffn_down_dx", tko=_pick(F, (512, 256, 128)), tn=D)
        dact = on_grad(f"{l}_ffn_down", mm_tn(act, dxoutb, 1, D, f"{l}_ffn_down_dw", tn=D), dact)
        du, dcw = ffn_act_bwd(u, wts[f"{l}_ffn_conv_w"], dact, F, f"{l}_ffn_act_bwd")
        g[f"{l}_ffn_conv_w"] = jnp.concatenate([dcw[0], dcw[1]], axis=1)
        du2 = du.reshape(2 * S, F)
        n = wts[f"{l}_ffn_up"].shape[1]
        tn = _pick(n, (1408, 1024, 768, 512, 256, 128))
        per_half = F // tn
        nt = n // tn

        def up_block(i, j, t):
            vb = j * nt + t
            return vb // per_half, vb % per_half

        tm = _pick(S, (1024, 512, 256, 128))

        def nt_map(i, j, t):
            half, cb = up_block(i, j, t)
            return (half * (S // tm) + i, cb)

        def tn_map(j, t):
            half, cb = up_block(0, j, t)
            return (half, cb)

        dh = mm_nt(du2, wts[f"{l}_ffn_up"], nb, S, D, f"{l}_ffn_up_dx", dy_maps=[nt_map], tm=tm, tko=D, tn=tn)
        dh = on_grad(f"{l}_ffn_up", mm_tn(h, du2, nb, n, f"{l}_ffn_up_dw", dy_maps=[tn_map], tn=tn), dh)
        dxin, dxinb, dg = rms_bwd(xin, wts[f"{l}_ffn_norm_g"], dh, dxout, f"{l}_ffn_rms_bwd")
        g[f"{l}_ffn_norm_g"] = dg
        return dxin, dxinb

    h0 = rms_fwd(x, wts["l0_mix_norm_g"], "l0_mix_rms")
    p0 = mm_nn(h0, wts["l0_w_in"], nb, "l0_w_in_mm")
    cat0 = sb_fwd(p0, W, "l0_sb_fwd")
    cat0 = sc_fwd(p0, wts["l0_sc_conv_w"], cat0, W, "l0_sc_fwd")
    x1 = mm_nn(cat0, wts["l0_w_out"], 1, "l0_w_out_mm", out_dtype=F32, res=x, tm=S, tn=_pick(D, (512, 256, 128)))
    x2, ffn0_saved = ffn_fwd(x1, "l0")

    x2 = at("l1_w_in", x2, None)
    nh = W // HD
    h2 = rms_fwd(x2, wts["l1_mix_norm_g"], "l1_mix_rms")
    p1 = mm_nt(h2, wts["l1_w_in_t"], 1, S, 5 * W, "l1_w_in_mm", tn=D)
    f = mm_nt(h2, wts["l1_w_f_t"], 1, S, 128, "l1_w_f_mm", out_dtype=F32, tn=D)
    bf = jnp.zeros((1, 128), F32).at[0, :nh].set(wts["l1_fox_b_f"])
    c = fox_gate_fwd(f, bf, "l1_fox_gate")
    c_heads = c[:, :nh].T
    ccol = c_heads[:, :, None]
    crow = c_heads.reshape(nh, S // _key_strip(S), _key_strip(S))
    sg_bt = wts["l1_sg_b"].T
    cat1 = sg_fwd(p1, wts["l1_sg_w"], sg_bt, wts["l1_sg_norm_g"], W, "l1_sg_fwd")
    cat1, lse = fox_fwd(p1, ccol, crow, cat1, W, "l1_fox_fwd")
    x3 = mm_nn(cat1, wts["l1_w_out"], 1, "l1_w_out_mm", out_dtype=F32, res=x2, tm=S, tn=_pick(D, (512, 256, 128)))
    x4, ffn1_saved = ffn_fwd(x3, "l1")

    dx4, dx4b, dgf, loss = loss_head(x4, wts["final_norm_g"], target, "loss_head")
    dx4b = at("loss", dx4b, loss)
    g["final_norm_g"] = dgf

    dx3, dx3b = ffn_bwd(dx4, dx4b, ffn1_saved, "l1")
    dcat1 = mm_nt(dx3b, wts["l1_w_out"], 1, S, D, "l1_w_out_dx", tn=D)
    dcat1 = on_grad("l1_w_out", mm_tn(cat1, dx3b, 1, D, "l1_w_out_dw", tn=D), dcat1)
    dp1, dsgw, dsgbt, dsgg = sg_bwd(p1, wts["l1_sg_w"], sg_bt, wts["l1_sg_norm_g"], dcat1, W, "l1_sg_bwd")
    dp1, dcs, dct = fox_bwd(p1, ccol, crow, cat1, lse, dcat1, dp1, W, "l1_fox_bwd")
    g["l1_sg_w"], g["l1_sg_b"], g["l1_sg_norm_g"] = dsgw, dsgbt.T, dsgg
    dc = jnp.zeros((S, 128), F32).at[:, :nh].set((dct[:, :, 0] - dcs.reshape(nh, S)).T)
    df, dbf = fox_gate_bwd(f, bf, dc, "l1_fox_gate_bwd")
    g["l1_fox_b_f"] = dbf[0, :nh]
    dfb = df.astype(BF16)
    tk1 = _pick(W, (1024, 512, 256, 128))
    tx1 = _pick(W, (512, 256, 128))
    tm1 = _pick(S, (1024, 512, 256, 128))
    part_of = lambda pt: pt + pt // 2 - pt // 4

    def a_map1(i, k):
        return (part_of(k // (W // tk1)) * (S // tm1) + i, k % (W // tk1))

    def x_map1(ko):
        return (part_of(ko // (W // tx1)), ko % (W // tx1))

    dp1_2d = dp1.reshape(6 * S, W)
    dw_main = mm_tn(dp1_2d, h2, 1, D, "l1_w_in_dw", tko=tx1, tn=D, x_map=x_map1, x_shape=(S, 5 * W))
    dw_f = mm_tn(dfb, h2, 1, D, "l1_w_f_dw", tn=D)
    dh2 = mm_nn(dfb, wts["l1_w_f_t"], 1, "l1_w_f_dx", out_dtype=F32)
    dh2 = mm_nn(dp1_2d, wts["l1_w_in_t"], 1, "l1_w_in_dx", res=dh2, tm=tm1, tk=tk1, a_map=a_map1, a_shape=(S, 5 * W))
    dh2 = on_grad("l1_w_in", jnp.concatenate([dw_main, dw_f[:nh]], axis=0), dh2)
    dx2, dx2b, dg = rms_bwd(x2, wts["l1_mix_norm_g"], dh2, dx3, "l1_mix_rms_bwd")
    g["l1_mix_norm_g"] = dg

    dx1, dx1b = ffn_bwd(dx2, dx2b, ffn0_saved, "l0")
    dcat0 = mm_nt(dx1b, wts["l0_w_out"], 1, S, D, "l0_w_out_dx", tn=D)
    dcat0 = on_grad("l0_w_out", mm_tn(cat0, dx1b, 1, D, "l0_w_out_dw", tn=D), dcat0)
    dp0 = sb_bwd(p0, dcat0, W, "l0_sb_bwd")
    dp0, dscw = sc_bwd(p0, wts["l0_sc_conv_w"], dcat0, dp0, W, "l0_sc_bwd")
    g["l0_sc_conv_w"] = dscw
    dp0 = at("small_ready", dp0, g)
    n0 = wts["l0_w_in"].shape[1]
    td0 = math.gcd(n0, W)
    nd0 = n0 // td0
    tm0 = _pick(S, (1024, 512, 256, 128))
    per_part0 = W // td0

    def nt_maps0(k):
        def f(i, j, t):
            vb = j * nd0 + k
            return ((vb // per_part0) * (S // tm0) + i, vb % per_part0)
        return f

    def tn_maps0(k):
        def f(j, t):
            vb = j * nd0 + k
            return (vb // per_part0, vb % per_part0)
        return f

    dp0_2d = dp0.reshape(6 * S, W)
    dw0 = mm_tn(h0, dp0_2d, nb, n0, "l0_w_in_dw", dy_maps=[tn_maps0(k) for k in range(nd0)], tn=n0)
    dp0_2d = on_grad("l0_w_in", dw0, dp0_2d)
    dp0_2d = on_grad(None, None, dp0_2d)
    dh0 = mm_nt(dp0_2d, wts["l0_w_in"], nb, S, D, "l0_w_in_dx", dy_maps=[nt_maps0(k) for k in range(nd0)], tm=tm0, tko=D, tn=n0)
    dh0 = at("small_done", dh0, None)
    dx0, _, dg = rms_bwd(x, wts["l0_mix_norm_g"], dh0, dx1, "l0_mix_rms_bwd")
    g["l0_mix_norm_g"] = dg
    return dx0, g


GATHER_ID = 1


def _place():
    return lax.axis_index("x"), lax.axis_index("y"), lax.axis_index("c")


def _other_chips(x, y):
    return [(x, 1 - y), (1 - x, y), (1 - x, 1 - y)]


def _handshake(peers):
    barrier = pltpu.get_barrier_semaphore()
    for peer in peers:
        pl.semaphore_signal(barrier, inc=1, device_id=peer, device_id_type=MESH)
    pl.semaphore_wait(barrier, len(peers))


UPDATE_LAG = 2


def _on_sequencer(body, out_type, scratch_types, collective_id, name):
    return pl.kernel(
        body,
        out_type=out_type,
        mesh=plsc.ScalarSubcoreMesh(axis_name="seq", num_cores=1),
        scratch_types=scratch_types,
        compiler_params=pltpu.CompilerParams(collective_id=collective_id),
        name=name,
    )


def all_gather(arrs, name):
    n = len(arrs)

    def body(*refs):
        xs, outs = refs[:n], refs[n : 2 * n]
        send_sems, recv_sems, local_sems = refs[2 * n :]
        x, y, c = _place()
        me, sibling = (x, y, c), (x, y, 1 - c)
        chips = _other_chips(x, y)
        _handshake([sibling] + [(*chip, c) for chip in chips])

        def copy(a, k, block, to, src=None):
            px, py, pc = block
            dst = outs[a].at[4 * px + 2 * py + pc]
            return pltpu.make_async_remote_copy(
                src_ref=dst if src is None else src, dst_ref=dst,
                send_sem=send_sems.at[7 * a + k], recv_sem=recv_sems.at[7 * a + k], device_id=to, device_id_type=MESH,
            )

        mine = [pltpu.make_async_copy(xs[a], outs[a].at[4 * x + 2 * y + c], local_sems.at[a]) for a in range(n)]
        for cp in mine:
            cp.start()
        first = []
        for a in range(n):
            first.append(copy(a, 0, me, sibling, src=xs[a]))
            first += [copy(a, 1 + j, me, (*chip, c), src=xs[a]) for j, chip in enumerate(chips)]
        for cp in first:
            cp.start()
        passed = []
        for a in range(n):
            for j, chip in enumerate(chips):
                copy(a, 1 + j, (*chip, c), me).wait_recv()
                cp = copy(a, 4 + j, (*chip, c), sibling)
                cp.start()
                passed.append(cp)
        for a in range(n):
            copy(a, 0, sibling, me).wait_recv()
            for j, chip in enumerate(chips):
                copy(a, 4 + j, (*chip, 1 - c), me).wait_recv()
        for cp in first + passed:
            cp.wait_send()
        for cp in mine:
            cp.wait()

    out_type = [jax.ShapeDtypeStruct((NDEV,) + a.shape, a.dtype) for a in arrs]
    sems = [pltpu.SemaphoreType.DMA((7 * n,)), pltpu.SemaphoreType.DMA((7 * n,)), pltpu.SemaphoreType.DMA((n,))]
    return _on_sequencer(body, out_type, sems, GATHER_ID, name)(*arrs)


_IN_HBM = pl.BlockSpec(memory_space=pltpu.HBM)
_IN_SEM = pl.BlockSpec(memory_space=pltpu.SEMAPHORE)
_EFFECT = pltpu.SideEffectType.DATAFLOW_SIDE_EFFECTING


def _split_start(make_copies, src, land_shape, nsem, name):
    def body(src_ref, land_ref, send_sems, recv_sems, land_thru, token):
        for cp in make_copies(src_ref, land_ref, send_sems, recv_sems):
            cp.start()
        token[...] = jnp.zeros_like(token)

    send_sems, recv_sems, land_thru, token = pl.pallas_call(
        body,
        name=name,
        out_shape=(
            pltpu.SemaphoreType.DMA((nsem,)), pltpu.SemaphoreType.DMA((nsem,)),
            pltpu.HBM(land_shape, src.dtype), jax.ShapeDtypeStruct((8, 128), F32),
        ),
        in_specs=(_IN_HBM, _IN_HBM),
        out_specs=(_IN_SEM, _IN_SEM, _IN_HBM, pl.BlockSpec(memory_space=pltpu.VMEM)),
        input_output_aliases={1: 2},
        compiler_params=pltpu.CompilerParams(has_side_effects=_EFFECT),
    )(src, pltpu.with_memory_space_constraint(lax.empty(land_shape, src.dtype), pltpu.HBM))
    return send_sems, recv_sems, src, land_thru, token


def _split_wait(make_copies, send_sems, recv_sems, src_thru, land_thru, after, name):
    def body(src_ref, land_ref, send_sems, recv_sems, after_ref, land_out):
        for cp in make_copies(src_ref, land_ref, send_sems, recv_sems):
            cp.wait_send()
            cp.wait_recv()

    return pl.pallas_call(
        body,
        name=name,
        out_shape=pltpu.HBM(land_thru.shape, land_thru.dtype),
        in_specs=(_IN_HBM, _IN_HBM, _IN_SEM, _IN_SEM, pl.BlockSpec(memory_space=pl.ANY)),
        out_specs=_IN_HBM,
        input_output_aliases={1: 0},
        compiler_params=pltpu.CompilerParams(has_side_effects=_EFFECT),
    )(src_thru, land_thru, send_sems, recv_sems, after)


def _pair_copies(src_ref, land_ref, send_sems, recv_sems):
    x, y, c = _place()
    return [
        pltpu.make_async_remote_copy(
            src_ref=src_ref.at[k, 1 - c], dst_ref=land_ref.at[k],
            send_sem=send_sems.at[k], recv_sem=recv_sems.at[k], device_id=(x, y, 1 - c), device_id_type=MESH,
        )
        for k in range(4)
    ]


def _direct_copies(src_ref, land_ref, send_sems, recv_sems):
    x, y, c = _place()
    me = 4 * x + 2 * y + c
    copies = []
    for k in range(NDEV - 1):
        to = (me + k + 1) % NDEV
        copies.append(pltpu.make_async_remote_copy(
            src_ref=src_ref, dst_ref=land_ref.at[me], send_sem=send_sems.at[k], recv_sem=recv_sems.at[k],
            device_id=(to // 4, (to // 2) % 2, to % 2), device_id_type=MESH,
        ))
    return copies


def _chip_copies(src_ref, land_ref, send_sems, recv_sems):
    x, y, c = _place()
    return [
        pltpu.make_async_remote_copy(
            src_ref=src_ref.at[2 * px + py], dst_ref=land_ref.at[2 * x + y],
            send_sem=send_sems.at[j], recv_sem=recv_sems.at[j], device_id=(px, py, c), device_id_type=MESH,
        )
        for j, (px, py) in enumerate(_other_chips(x, y))
    ]


def _row_tile(R, C, max_elems):
    if R * C <= max_elems:
        return R
    best = None
    for tr in range(16, R, 16):
        if R % tr == 0 and tr * C <= max_elems:
            best = tr
    return best or R


def pair_sum(a42, land4, core, name):
    _, _, R, C = a42.shape
    tr = _row_tile(R, C, 1 << 20)

    def body(core_ref, a_ref, l_ref, o_ref):
        o_ref[...] = (a_ref[0].astype(F32) + l_ref[...].astype(F32)).astype(o_ref.dtype)

    return pl.pallas_call(
        body,
        grid_spec=pltpu.PrefetchScalarGridSpec(
            num_scalar_prefetch=1,
            grid=(4, R // tr),
            in_specs=[
                pl.BlockSpec((1, 1, tr, C), lambda k, r, core_ref: (k, core_ref[0], r, 0)),
                pl.BlockSpec((1, tr, C), lambda k, r, core_ref: (k, r, 0)),
            ],
            out_specs=pl.BlockSpec((1, tr, C), lambda k, r, core_ref: (k, r, 0)),
        ),
        out_shape=jax.ShapeDtypeStruct((4, R, C), BF16),
        compiler_params=_cp(("parallel", "parallel")),
        name=name,
    )(core, a42, land4)


def sum_slots(parts, name):
    P, R, C = parts.shape

    def body(p_ref, o_ref):
        acc = p_ref[0].astype(F32)
        for k in range(1, P):
            acc = acc + p_ref[k].astype(F32)
        o_ref[...] = acc

    tr = _row_tile(R, P * C, 1 << 21)
    return pl.pallas_call(
        body,
        grid=(R // tr,),
        in_specs=[pl.BlockSpec((P, tr, C), lambda r: (0, r, 0))],
        out_specs=pl.BlockSpec((tr, C), lambda r: (r, 0)),
        out_shape=jax.ShapeDtypeStruct((R, C), F32),
        compiler_params=_cp(("parallel",)),
        name=name,
    )(parts)


def adamw(w, m, v, parts, name):
    R, C = w.shape
    P = parts.shape[0]
    tr = _pick(R, (256, 128, 64, 32, 16, 8))
    c1 = 1.0 - ADAM_B1 ** ADAM_STEP
    c2 = 1.0 - ADAM_B2 ** ADAM_STEP

    def body(w_ref, m_ref, v_ref, p_ref, g_ref, d_ref, nm_ref, nv_ref):
        g = p_ref[0].astype(F32)
        for k in range(1, P):
            g = g + p_ref[k].astype(F32)
        nm = ADAM_B1 * m_ref[...] + (1.0 - ADAM_B1) * g
        nv = ADAM_B2 * v_ref[...] + (1.0 - ADAM_B2) * (g * g)
        g_ref[...] = g
        nm_ref[...] = nm
        nv_ref[...] = nv
        d_ref[...] = -ADAM_LR * ((nm / c1) / (jnp.sqrt(nv / c2) + ADAM_EPS) + ADAM_WD * w_ref[...])

    blk = pl.BlockSpec((tr, C), lambda r: (r, 0))
    shp = jax.ShapeDtypeStruct((R, C), F32)
    return pl.pallas_call(
        body,
        grid=(R // tr,),
        in_specs=[blk, blk, blk, pl.BlockSpec((P, tr, C), lambda r: (0, r, 0))],
        out_specs=[blk, blk, blk, blk],
        out_shape=[shp, shp, shp, shp],
        compiler_params=_cp(("parallel",)),
        name=name,
    )(w, m, v, parts)


def adamw_reduced(w, m, v, own, land, chip, name):
    R, C = w.shape
    if R % 8 == 0:
        tr, tc = _pick(R, (256, 128, 64, 32, 16, 8)), C
    else:
        tr, tc = R, _pick(C, (256, 128))
    c1 = 1.0 - ADAM_B1 ** ADAM_STEP
    c2 = 1.0 - ADAM_B2 ** ADAM_STEP

    def body(chip_ref, w_ref, m_ref, v_ref, own_ref, land_ref, g_ref, d_ref, nm_ref, nv_ref):
        mine = own_ref[0].astype(F32)
        g = None
        for k in range(4):
            term = jnp.where(chip_ref[0] == k, mine, land_ref[k].astype(F32))
            g = term if g is None else g + term
        nm = ADAM_B1 * m_ref[...] + (1.0 - ADAM_B1) * g
        nv = ADAM_B2 * v_ref[...] + (1.0 - ADAM_B2) * (g * g)
        g_ref[...] = g
        nm_ref[...] = nm
        nv_ref[...] = nv
        d_ref[...] = -ADAM_LR * ((nm / c1) / (jnp.sqrt(nv / c2) + ADAM_EPS) + ADAM_WD * w_ref[...])

    blk = pl.BlockSpec((tr, tc), lambda r, c, chip_ref: (r, c))
    shp = jax.ShapeDtypeStruct((R, C), F32)
    return pl.pallas_call(
        body,
        grid_spec=pltpu.PrefetchScalarGridSpec(
            num_scalar_prefetch=1,
            grid=(R // tr, C // tc),
            in_specs=[
                blk, blk, blk,
                pl.BlockSpec((1, tr, tc), lambda r, c, chip_ref: (chip_ref[0], r, c)),
                pl.BlockSpec((4, tr, tc), lambda r, c, chip_ref: (0, r, c)),
            ],
            out_specs=[blk, blk, blk, blk],
        ),
        out_shape=[shp, shp, shp, shp],
        compiler_params=_cp(("parallel", "parallel")),
        name=name,
    )(chip, w, m, v, own, land)


_WEIGHTS = [
    "l0_mix_norm_g", "l0_w_in", "l0_sc_conv_w", "l0_w_out", "l0_ffn_norm_g", "l0_ffn_up", "l0_ffn_conv_w", "l0_ffn_down",
    "l1_mix_norm_g", "l1_w_in", "l1_fox_b_f", "l1_sg_w", "l1_sg_b", "l1_sg_norm_g", "l1_w_out", "l1_ffn_norm_g",
    "l1_ffn_up", "l1_ffn_conv_w", "l1_ffn_down", "final_norm_g",
]
_ROW_SHARDED = ["l0_w_out", "l0_ffn_down", "l1_w_out", "l1_ffn_down"]
_BIG = ["l0_w_in", "l0_w_out", "l0_ffn_up", "l0_ffn_down", "l1_w_in", "l1_w_out", "l1_ffn_up", "l1_ffn_down"]
_CONV = ["l0_sc_conv_w", "l0_ffn_conv_w", "l1_ffn_conv_w"]
_SMALL = [n for n in _WEIGHTS if n not in _BIG]
_LAST_SMALL = "l0_mix_norm_g"
_PACK_ROWS = 8


def _pack(arrs):
    flat = []
    for a in arrs:
        v = a.reshape(-1).astype(F32)
        pad = (-v.shape[0]) % (_PACK_ROWS * 128)
        flat.append(jnp.pad(v, (0, pad)))
    return jnp.concatenate(flat).reshape(-1, 128)


def _unpack(packed, shapes):
    out, off = [], 0
    flat = packed.reshape(-1)
    for shp in shapes:
        size = math.prod(shp)
        out.append(flat[off : off + size].reshape(shp))
        off += size + (-size) % (_PACK_ROWS * 128)
    return out


def kernel(x, l0_mix_norm_g, l0_w_in, l0_sc_conv_w, l0_w_out, l0_ffn_norm_g, l0_ffn_up, l0_ffn_conv_w, l0_ffn_down, l1_mix_norm_g, l1_w_in, l1_fox_b_f, l1_sg_w, l1_sg_b, l1_sg_norm_g, l1_w_out, l1_ffn_norm_g, l1_ffn_up, l1_ffn_conv_w, l1_ffn_down, final_norm_g, loss_target, m_l0_mix_norm_g, m_l0_w_in, m_l0_sc_conv_w, m_l0_w_out, m_l0_ffn_norm_g, m_l0_ffn_up, m_l0_ffn_conv_w, m_l0_ffn_down, m_l1_mix_norm_g, m_l1_w_in, m_l1_fox_b_f, m_l1_sg_w, m_l1_sg_b, m_l1_sg_norm_g, m_l1_w_out, m_l1_ffn_norm_g, m_l1_ffn_up, m_l1_ffn_conv_w, m_l1_ffn_down, m_final_norm_g, v_l0_mix_norm_g, v_l0_w_in, v_l0_sc_conv_w, v_l0_w_out, v_l0_ffn_norm_g, v_l0_ffn_up, v_l0_ffn_conv_w, v_l0_ffn_down, v_l1_mix_norm_g, v_l1_w_in, v_l1_fox_b_f, v_l1_sg_w, v_l1_sg_b, v_l1_sg_norm_g, v_l1_w_out, v_l1_ffn_norm_g, v_l1_ffn_up, v_l1_ffn_conv_w, v_l1_ffn_down, v_final_norm_g):
    given = dict(locals())
    w = {n: given[n] for n in _WEIGHTS}
    mom = {n: given["m_" + n] for n in _WEIGHTS}
    var = {n: given["v_" + n] for n in _WEIGHTS}
    xs, target = x[0], loss_target[0]
    S, D = xs.shape
    W = D // 2
    nh = W // HD
    cx, cy, cc = _place()
    me = 4 * cx + 2 * cy + cc

    wts = {"nb": NDEV, "F": l0_ffn_down.shape[0] * NDEV}
    for n in _SMALL:
        if n not in _CONV:
            wts[n] = w[n]
    gathered, loss_sum = {}, []

    def start_gather(names):
        srcs = [(w[n].T if n == "l1_w_in" else w[n]).astype(BF16) for n in names]
        taps = [w[c] for c in _CONV] if names[0] == _BIG[0] else []
        got = all_gather(srcs + taps, "gather_" + "_".join(names))
        for n, full in zip(names, got):
            if n == "l1_w_in":
                gathered[n] = full
            elif n in _ROW_SHARDED:
                wts[n] = full.reshape(-1, D)
            else:
                wts[n] = full.reshape(NDEV * D, -1)
        for c, full in zip(_CONV, got[len(names):] if taps else []):
            wts[c] = full.transpose(1, 0, 2).reshape(CONV_K, -1)

    def at(point, after, value):
        if point == "l1_w_in":
            got, after = lax.optimization_barrier((gathered[point], after))
            wts["l1_w_in_t"] = got.reshape(-1, D)
            wts["l1_w_f_t"] = jnp.pad(wts["l1_w_in_t"][5 * W :], ((0, 128 - nh), (0, 0)))
        elif point == "loss":
            total, after = lax.optimization_barrier((lax.psum(value[0, 0], ("x", "y", "c")), after))
            loss_sum.append(total)
        elif point == "small_ready":
            early = [n for n in _SMALL if n != _LAST_SMALL]
            gathered["small"] = all_gather([_pack([value[n] for n in early])], "gather_small_grads")[0]
        elif point == "small_done":
            after = update_small([n for n in _SMALL if n != _LAST_SMALL], gathered["small"], "small", after)
        return after

    out_g, out_d, out_m, out_v = {}, {}, {}, {}

    def update_small(names, all_terms, tag, after=None):
        shapes = [w[n].shape for n in names]
        full_shapes = [(CONV_K, NDEV * w[n].shape[1]) if n in _CONV else w[n].shape for n in names]
        grads = {}
        for n, t in zip(names, _unpack(sum_slots(all_terms, f"sum_{tag}_grads"), full_shapes)):
            if n in _CONV:
                cols = w[n].shape[1]
                t = lax.dynamic_slice_in_dim(t, me * cols, cols, axis=1)
            grads[n] = t
        res = adamw(
            _pack([w[n] for n in names]), _pack([mom[n] for n in names]), _pack([var[n] for n in names]),
            _pack([grads[n] for n in names])[None], f"adamw_{tag}",
        )
        if after is not None:
            res, after = lax.optimization_barrier((res, after))
        for dst, packed_out in zip((out_g, out_d, out_m, out_v), res):
            for n, t in zip(names, _unpack(packed_out, shapes)):
                dst[n] = t
        return after

    core = jnp.reshape(cc, (1,)).astype(jnp.int32)
    chip = jnp.reshape(2 * cx + cy, (1,)).astype(jnp.int32)
    pair_flying, chip_flying = [], []

    def tie(value, after):
        if after is None:
            return value, None
        return lax.optimization_barrier((value, after))

    def to_chips(after):
        n, flying = pair_flying.pop()
        landed = _split_wait(_pair_copies, *flying, f"reduce_pair_wait_{n}")
        summed = pair_sum(flying[2], landed, core, f"pair_sum_{n}")
        *flying, token = _split_start(_chip_copies, summed, summed.shape, 3, f"reduce_chips_{n}")
        token, after = tie(token, after)
        chip_flying.append((n, flying + [token]))
        return after

    def update(after, behind=None):
        n, flying = chip_flying.pop(0)
        if behind is not None:
            flying[4], _ = lax.optimization_barrier((flying[4], behind))
        landed = _split_wait(_chip_copies, *flying, f"reduce_chips_wait_{n}")
        turn = (lambda t: t.T) if n == "l1_w_in" else (lambda t: t)
        res = adamw_reduced(turn(w[n]), turn(mom[n]), turn(var[n]), flying[2], landed, chip, f"adamw_{n}")
        res, after = tie(res, after)
        out_g[n], out_d[n], out_m[n], out_v[n] = [turn(t) for t in res]
        return after, res[0]

    def on_grad(n, term, after):
        if n is None:
            return to_chips(after)
        if n in _ROW_SHARDED or n == "l1_w_in":
            term = term.reshape(NDEV, -1, D)
        else:
            term = term.reshape(NDEV, D, -1)
        term = term.reshape((4, 2) + term.shape[1:])
        *flying, token = _split_start(_pair_copies, term, term.shape[:1] + term.shape[2:], 4, f"reduce_pair_{n}")
        token, after = tie(token, after)
        if len(chip_flying) == UPDATE_LAG:
            after, _ = update(after)
        if pair_flying:
            after = to_chips(after)
        pair_flying.append((n, flying + [token]))
        return after

    for n in _BIG:
        start_gather([n])
    dx, g = local_step(xs, target, wts, at, on_grad)
    last = _pack([g[_LAST_SMALL]])
    *flying, done = _split_start(_direct_copies, last, (NDEV,) + last.shape, NDEV - 1, "gather_last_grad")
    while len(chip_flying) > 1:
        _, done = update(None, behind=done)
    landed = _split_wait(_direct_copies, *flying, done, "gather_last_grad_wait")
    update_small([_LAST_SMALL], lax.dynamic_update_slice(landed, last[None], (me, 0, 0)), "last")
    update(None, behind=out_g[_LAST_SMALL])
    loss = loss_sum[0]

    return (loss, dx[None], *[out_g[n] for n in _WEIGHTS], *[out_d[n] for n in _WEIGHTS],
            *[out_m[n] for n in _WEIGHTS], *[out_v[n] for n in _WEIGHTS])
```

```python
import functools
import math

import jax
import jax.numpy as jnp
from jax import lax
from jax.experimental import pallas as pl
from jax.experimental.pallas import tpu as pltpu
from jax.experimental.pallas import tpu_sc as plsc

F32 = jnp.float32
BF16 = jnp.bfloat16
HD = 128
EPS = 1e-6
CONV_K = 3
VMEM_LIMIT_BYTES = 48 << 20
NDEV = 8
MESH = pl.DeviceIdType.MESH

ADAM_LR = 0.001
ADAM_B1 = 0.9
ADAM_B2 = 0.999
ADAM_EPS = 1e-08
ADAM_WD = 0.01
ADAM_STEP = 10


def _cp(sem):
    return pltpu.CompilerParams(dimension_semantics=sem, vmem_limit_bytes=VMEM_LIMIT_BYTES)


def _pick(n, prefs):
    for p in prefs:
        if n % p == 0:
            return p
    return n


def _dot(a, b):
    return jnp.dot(a, b, preferred_element_type=F32)


def _dot_nt(a, b):
    return lax.dot_general(a, b, (((1,), (1,)), ((), ())), preferred_element_type=F32)


def _dot_tn(a, b):
    return lax.dot_general(a, b, (((0,), (0,)), ((), ())), preferred_element_type=F32)


def _split3(x):
    hi = x.astype(BF16)
    r = x - hi.astype(F32)
    mid = r.astype(BF16)
    lo = (r - mid.astype(F32)).astype(BF16)
    return hi, mid, lo


def _dot_ones_left(ones_bf16, x):
    hi, mid, lo = _split3(x)
    return _dot(ones_bf16, hi) + _dot(ones_bf16, mid) + _dot(ones_bf16, lo)


def _iota2(shape, axis):
    return lax.broadcasted_iota(jnp.int32, shape, axis)


def mm_nn(a, w2d, nb, name, out_dtype=BF16, res=None, tm=None, tn=None, tk=None, a_map=None, a_shape=None):
    M, K = a_shape or a.shape
    n = w2d.shape[1]
    assert w2d.shape[0] == nb * K or (nb == 1 and w2d.shape[0] > K)
    a_map = a_map or (lambda i, k: (i, k))
    tm = tm or _pick(M, (1024, 512, 256, 128))
    tn = tn or _pick(n, (1408, 1024, 768, 512, 256, 128))
    tk = tk or (K if K <= 2048 else _pick(K, (1408, 1024, 512, 256, 128)))
    nk, nt = K // tk, n // tn
    has_res = res is not None

    def body(*refs):
        if has_res:
            a_ref, w_ref, r_ref, o_ref = refs[:4]
        else:
            a_ref, w_ref, o_ref = refs[:3]
            r_ref = None
        part = _dot(a_ref[...], w_ref[...])

        def finish(acc):
            if r_ref is not None:
                acc = acc + r_ref[...].astype(F32)
            o_ref[...] = acc.astype(o_ref.dtype)

        if nk == 1:
            finish(part)
        else:
            acc_ref = refs[-1]
            k = pl.program_id(3)

            @pl.when(k == 0)
            def _():
                acc_ref[...] = part

            @pl.when(k > 0)
            def _():
                acc_ref[...] += part

            @pl.when(k == nk - 1)
            def _():
                finish(acc_ref[...])

    in_specs = [
        pl.BlockSpec((tm, tk), lambda i, j, t, k: a_map(i, k)),
        pl.BlockSpec((tk, tn), lambda i, j, t, k: (j * nk + k, t)),
    ]
    args = [a, w2d]
    out_spec = pl.BlockSpec((tm, tn), lambda i, j, t, k: (i, j * nt + t))
    if has_res:
        in_specs.append(out_spec)
        args.append(res)
    return pl.pallas_call(
        body,
        grid=(M // tm, nb, nt, nk),
        in_specs=in_specs,
        out_specs=out_spec,
        out_shape=jax.ShapeDtypeStruct((M, nb * n), out_dtype),
        scratch_shapes=[pltpu.VMEM((tm, tn), F32)] if nk > 1 else [],
        compiler_params=_cp(("parallel", "parallel", "parallel", "arbitrary")),
        name=name,
    )(*args)


def mm_nt(dy2d, w2d, nb, M, K, name, out_dtype=BF16, res=None, dy_maps=None, tm=None, tko=None, tn=None):
    n = w2d.shape[1]
    assert w2d.shape[0] == nb * K or (nb == 1 and w2d.shape[0] > K)
    tm = tm or _pick(M, (1024, 512, 256, 128))
    tko = tko or _pick(K, (1024, 512, 256, 128))
    tn = tn or _pick(n, (1408, 1024, 768, 512, 256, 128))
    nt, nko = n // tn, K // tko
    has_res = res is not None
    if dy_maps is None:
        dy_maps = [lambda i, j, t: (i, j * nt + t)]
    nd = len(dy_maps)
    td = tn // nd

    one_step = nb * nt == 1

    def body(*refs):
        d_refs, w_ref = refs[:nd], refs[nd]
        r_ref = refs[nd + 1] if has_res else None
        d = d_refs[0][...] if nd == 1 else jnp.concatenate([r[...] for r in d_refs], axis=1)
        part = _dot_nt(d, w_ref[...])
        if one_step:
            o_ref = refs[-1]
            if r_ref is not None:
                part = part + r_ref[...].astype(F32)
            o_ref[...] = part.astype(o_ref.dtype)
            return
        o_ref, acc_ref = refs[-2], refs[-1]
        j, t = pl.program_id(2), pl.program_id(3)
        first = jnp.logical_and(j == 0, t == 0)
        last = jnp.logical_and(j == nb - 1, t == nt - 1)

        @pl.when(first)
        def _():
            acc_ref[...] = part

        @pl.when(jnp.logical_not(first))
        def _():
            acc_ref[...] += part

        @pl.when(last)
        def _():
            acc = acc_ref[...]
            if r_ref is not None:
                acc = acc + r_ref[...].astype(F32)
            o_ref[...] = acc.astype(o_ref.dtype)

    in_specs = [pl.BlockSpec((tm, td), functools.partial(lambda f, i, ko, j, t: f(i, j, t), f)) for f in dy_maps]
    in_specs.append(pl.BlockSpec((tko, tn), lambda i, ko, j, t: (j * nko + ko, t)))
    args = [dy2d] * nd + [w2d]
    out_spec = pl.BlockSpec((tm, tko), lambda i, ko, j, t: (i, ko))
    if has_res:
        in_specs.append(out_spec)
        args.append(res)
    return pl.pallas_call(
        body,
        grid=(M // tm, nko, nb, nt),
        in_specs=in_specs,
        out_specs=out_spec,
        out_shape=jax.ShapeDtypeStruct((M, K), out_dtype),
        scratch_shapes=[] if one_step else [pltpu.VMEM((tm, tko), F32)],
        compiler_params=_cp(("parallel", "parallel", "arbitrary", "arbitrary")),
        name=name,
    )(*args)


def mm_tn(x, dy2d, nb, n, name, out_dtype=BF16, dy_maps=None, tko=None, tn=None, x_map=None, x_shape=None):
    S, K = x_shape or x.shape
    x_map = x_map or (lambda ko: (0, ko))
    tko = tko or _pick(K, (512, 256, 128))
    tn = tn or _pick(n, (1408, 1024, 768, 512, 256, 128))
    nt, nko = n // tn, K // tko
    if dy_maps is None:
        dy_maps = [lambda j, t: (0, j * nt + t)]
    nd = len(dy_maps)
    td = tn // nd

    def body(*refs):
        x_ref, d_refs, o_ref = refs[0], refs[1 : 1 + nd], refs[-1]
        d = d_refs[0][...] if nd == 1 else jnp.concatenate([r[...] for r in d_refs], axis=1)
        o_ref[...] = _dot_tn(x_ref[...], d).astype(o_ref.dtype)

    in_specs = [pl.BlockSpec((S, tko), lambda ko, j, t: x_map(ko))]
    in_specs += [pl.BlockSpec((S, td), functools.partial(lambda f, ko, j, t: f(j, t), f)) for f in dy_maps]
    return pl.pallas_call(
        body,
        grid=(nko, nb, nt),
        in_specs=in_specs,
        out_specs=pl.BlockSpec((tko, tn), lambda ko, j, t: (j * nko + ko, t)),
        out_shape=jax.ShapeDtypeStruct((nb * K, n), out_dtype),
        compiler_params=_cp(("parallel", "parallel", "parallel")),
        name=name,
    )(x, *([dy2d] * nd))


def rms_fwd(x, g, name):
    S, D = x.shape
    tm = _pick(S, (256, 128))

    def body(x_ref, g_ref, o_ref):
        xf = x_ref[...]
        r = lax.rsqrt(jnp.mean(xf * xf, axis=-1, keepdims=True) + EPS)
        o_ref[...] = (xf * r * g_ref[...]).astype(o_ref.dtype)

    return pl.pallas_call(
        body,
        grid=(S // tm,),
        in_specs=[pl.BlockSpec((tm, D), lambda i: (i, 0)), pl.BlockSpec((1, D), lambda i: (0, 0))],
        out_specs=pl.BlockSpec((tm, D), lambda i: (i, 0)),
        out_shape=jax.ShapeDtypeStruct((S, D), BF16),
        compiler_params=_cp(("parallel",)),
        name=name,
    )(x, g.reshape(1, D))


def rms_bwd(x, g, dh, dres, name):
    S, D = x.shape
    tm = _pick(S, (256, 128))

    def body(x_ref, g_ref, dh_ref, dr_ref, dx_ref, dxb_ref, dg_ref):
        i = pl.program_id(0)
        xf = x_ref[...]
        dh = dh_ref[...].astype(F32)
        r = lax.rsqrt(jnp.mean(xf * xf, axis=-1, keepdims=True) + EPS)
        gy = dh * g_ref[...]
        proj = jnp.mean(gy * xf, axis=-1, keepdims=True)
        dx = dr_ref[...] + r * gy - xf * (r * r * r * proj)
        dx_ref[...] = dx
        dxb_ref[...] = dx.astype(BF16)
        dg = jnp.sum(dh * (xf * r), axis=0, keepdims=True)

        @pl.when(i == 0)
        def _():
            dg_ref[...] = dg

        @pl.when(i > 0)
        def _():
            dg_ref[...] += dg

    row = pl.BlockSpec((tm, D), lambda i: (i, 0))
    vec = pl.BlockSpec((1, D), lambda i: (0, 0))
    return pl.pallas_call(
        body,
        grid=(S // tm,),
        in_specs=[row, vec, row, row],
        out_specs=[row, row, vec],
        out_shape=[jax.ShapeDtypeStruct((S, D), F32), jax.ShapeDtypeStruct((S, D), BF16), jax.ShapeDtypeStruct((1, D), F32)],
        compiler_params=_cp(("arbitrary",)),
        name=name,
    )(x, g.reshape(1, D), dh, dres)


def loss_head(x, g, target, name):
    S, D = x.shape
    tm = _pick(S, (256, 128))

    def body(x_ref, g_ref, t_ref, dx_ref, dxb_ref, dg_ref, loss_ref):
        i = pl.program_id(0)
        xf = x_ref[...]
        gg = g_ref[...]
        r = lax.rsqrt(jnp.mean(xf * xf, axis=-1, keepdims=True) + EPS)
        xh = xf * r
        err = xh * gg - t_ref[...]
        part = (0.5 / D) * jnp.sum(err * err)
        dy = err * (1.0 / D)
        gy = dy * gg
        proj = jnp.mean(gy * xf, axis=-1, keepdims=True)
        dx = r * gy - xf * (r * r * r * proj)
        dx_ref[...] = dx
        dxb_ref[...] = dx.astype(BF16)
        dg = jnp.sum(dy * xh, axis=0, keepdims=True)
        lossb = jnp.full(loss_ref.shape, part, F32)

        @pl.when(i == 0)
        def _():
            dg_ref[...] = dg
            loss_ref[...] = lossb

        @pl.when(i > 0)
        def _():
            dg_ref[...] += dg
            loss_ref[...] += lossb

    row = pl.BlockSpec((tm, D), lambda i: (i, 0))
    vec = pl.BlockSpec((1, D), lambda i: (0, 0))
    return pl.pallas_call(
        body,
        grid=(S // tm,),
        in_specs=[row, vec, row],
        out_specs=[row, row, vec, pl.BlockSpec((8, 128), lambda i: (0, 0))],
        out_shape=[
            jax.ShapeDtypeStruct((S, D), F32),
            jax.ShapeDtypeStruct((S, D), BF16),
            jax.ShapeDtypeStruct((1, D), F32),
            jax.ShapeDtypeStruct((8, 128), F32),
        ],
        compiler_params=_cp(("arbitrary",)),
        name=name,
    )(x, g.reshape(1, D), target)


def _shift_down(s, k):
    if k == 0:
        return s
    return jnp.where(_iota2(s.shape, 0) >= k, pltpu.roll(s, k, axis=0), 0.0)


def _shift_up(s, k):
    if k == 0:
        return s
    n = s.shape[0]
    return jnp.where(_iota2(s.shape, 0) < n - k, pltpu.roll(s, n - k, axis=0), 0.0)


def _conv(s, w):
    return w[0:1] * _shift_down(s, 2) + w[1:2] * _shift_down(s, 1) + w[2:3] * s


def _conv_t(d, w):
    return w[2:3] * d + w[1:2] * _shift_up(d, 1) + w[0:1] * _shift_up(d, 2)


def _conv_dw(d, s):
    return [jnp.sum(d * _shift_down(s, CONV_K - 1 - k), axis=0, keepdims=True) for k in range(CONV_K)]


def sc_fwd(p, convw, cat, W, name):
    S = p.shape[0]
    tc = _pick(W, (256, 128))
    nc = W // tc

    def body(gb_ref, gc_ref, hi_ref, w_ref, cat_ref, o_ref):
        s = gc_ref[...].astype(F32) * hi_ref[...].astype(F32)
        o_ref[...] = (gb_ref[...].astype(F32) * _conv(s, w_ref[...])).astype(o_ref.dtype)

    col = lambda part: pl.BlockSpec((S, tc), lambda c: (0, part * nc + c))
    return pl.pallas_call(
        body,
        grid=(nc,),
        in_specs=[col(3), col(4), col(5), pl.BlockSpec((CONV_K, tc), lambda c: (0, c)), pl.BlockSpec(memory_space=pl.ANY)],
        out_specs=col(1),
        out_shape=jax.ShapeDtypeStruct(cat.shape, cat.dtype),
        input_output_aliases={4: 0},
        compiler_params=_cp(("parallel",)),
        name=name,
    )(p, p, p, convw, cat)


def sc_bwd(p, convw, dcat, dp, W, name):
    S = p.shape[0]
    tc = _pick(W, (256, 128))
    nc = W // tc

    def body(gb_ref, gc_ref, hi_ref, w_ref, do_ref, dp_in_ref, dp_ref, dw_ref):
        gb = gb_ref[...].astype(F32)
        gc = gc_ref[...].astype(F32)
        hi = hi_ref[...].astype(F32)
        w = w_ref[...]
        do = do_ref[...].astype(F32)
        s = gc * hi
        dcs = do * gb
        ds = _conv_t(dcs, w)
        dp_ref[0] = (do * _conv(s, w)).astype(dp_ref.dtype)
        dp_ref[1] = (ds * hi).astype(dp_ref.dtype)
        dp_ref[2] = (ds * gc).astype(dp_ref.dtype)
        for k, row in enumerate(_conv_dw(dcs, s)):
            dw_ref[k : k + 1, :] = row

    col = lambda part: pl.BlockSpec((S, tc), lambda c: (0, part * nc + c))
    return pl.pallas_call(
        body,
        grid=(nc,),
        in_specs=[
            col(3), col(4), col(5),
            pl.BlockSpec((CONV_K, tc), lambda c: (0, c)),
            pl.BlockSpec((S, tc), lambda c: (0, nc + c)),
            pl.BlockSpec(memory_space=pl.ANY),
        ],
        out_specs=[pl.BlockSpec((3, S, tc), lambda c: (1, 0, c)), pl.BlockSpec((CONV_K, tc), lambda c: (0, c))],
        out_shape=[jax.ShapeDtypeStruct(dp.shape, dp.dtype), jax.ShapeDtypeStruct((CONV_K, W), F32)],
        input_output_aliases={5: 0},
        compiler_params=_cp(("parallel",)),
        name=name,
    )(p, p, p, convw, dcat, dp)


def _silu_parts(a):
    sig = 1.0 / (1.0 + jnp.exp(-a))
    return a * sig, sig


def ffn_act_fwd(u, convw, F, name):
    S = u.shape[0]
    tc = _pick(F, (256, 128))
    nc = F // tc

    def body(ug_ref, uu_ref, wg_ref, wu_ref, o_ref):
        ag = _conv(ug_ref[...].astype(F32), wg_ref[...])
        au = _conv(uu_ref[...].astype(F32), wu_ref[...])
        o_ref[...] = (_silu_parts(ag)[0] * au).astype(o_ref.dtype)

    col = lambda half: pl.BlockSpec((S, tc), lambda c: (0, half * nc + c))
    wcol = lambda half: pl.BlockSpec((CONV_K, tc), lambda c: (0, half * nc + c))
    return pl.pallas_call(
        body,
        grid=(nc,),
        in_specs=[col(0), col(1), wcol(0), wcol(1)],
        out_specs=pl.BlockSpec((S, tc), lambda c: (0, c)),
        out_shape=jax.ShapeDtypeStruct((S, F), BF16),
        compiler_params=_cp(("parallel",)),
        name=name,
    )(u, u, convw, convw)


def ffn_act_bwd(u, convw, dact, F, name):
    S = u.shape[0]
    tc = _pick(F, (256, 128))
    nc = F // tc

    def body(ug_ref, uu_ref, wg_ref, wu_ref, da_ref, du_ref, dw_ref):
        ug = ug_ref[...].astype(F32)
        uu = uu_ref[...].astype(F32)
        wg = wg_ref[...]
        wu = wu_ref[...]
        da = da_ref[...].astype(F32)
        ag = _conv(ug, wg)
        au = _conv(uu, wu)
        sl, sig = _silu_parts(ag)
        dag = da * au * (sig * (1.0 + ag * (1.0 - sig)))
        dau = da * sl
        du_ref[0] = _conv_t(dag, wg).astype(du_ref.dtype)
        du_ref[1] = _conv_t(dau, wu).astype(du_ref.dtype)
        for k, (rg, ru) in enumerate(zip(_conv_dw(dag, ug), _conv_dw(dau, uu))):
            dw_ref[0, k : k + 1, :] = rg
            dw_ref[1, k : k + 1, :] = ru

    col = lambda half: pl.BlockSpec((S, tc), lambda c: (0, half * nc + c))
    wcol = lambda half: pl.BlockSpec((CONV_K, tc), lambda c: (0, half * nc + c))
    return pl.pallas_call(
        body,
        grid=(nc,),
        in_specs=[col(0), col(1), wcol(0), wcol(1), pl.BlockSpec((S, tc), lambda c: (0, c))],
        out_specs=[pl.BlockSpec((2, S, tc), lambda c: (0, 0, c)), pl.BlockSpec((2, CONV_K, tc), lambda c: (0, 0, c))],
        out_shape=[jax.ShapeDtypeStruct((2, S, F), BF16), jax.ShapeDtypeStruct((2, CONV_K, F), F32)],
        compiler_params=_cp(("parallel",)),
        name=name,
    )(u, u, convw, convw, dact)


def _softplus(z):
    return jnp.maximum(z, 0.0) + jnp.log(1.0 + jnp.exp(-jnp.abs(z)))


def _key_strip(S):
    return _pick(S, (512, 256, 128))


def _query_rows(S):
    tq = _pick(S, (512, 256, 128))
    assert _key_strip(S) % tq == 0
    return tq


def _split2(x):
    hi = x.astype(BF16)
    return hi, (x - hi.astype(F32)).astype(BF16)


def _block_sums(x, ones_bf16):
    hi, lo = _split2(x)
    return [
        _dot(hi[:, b * HD : (b + 1) * HD], ones_bf16) + _dot(lo[:, b * HD : (b + 1) * HD], ones_bf16)
        for b in range(x.shape[1] // HD)
    ]


def _strip_mask(shape, row0, off, strict):
    cols, rows = _iota2(shape, 1) + off, _iota2(shape, 0) + row0
    return cols < rows if strict else cols <= rows


def _sb_strip(q, ks, row0, off, run, su, masked):
    z = _dot_nt(q, ks) * (HD ** -0.5)
    sp = _softplus(z)
    mask = _strip_mask(z.shape, row0, off, True) if masked else None
    l = jnp.where(mask, -sp, 0.0) if masked else -sp
    within = _block_sums(l, su)
    later = [None] * len(within)
    for b in reversed(range(len(within))):
        later[b] = within[b] + run
        run = run + jnp.sum(l[:, b * HD : (b + 1) * HD], axis=1, keepdims=True)
    a = jnp.exp(z - sp + jnp.concatenate(later, axis=1))
    return z, (jnp.where(mask, a, 0.0) if masked else a), run


def sb_fwd(p, W, name):
    S = p.shape[0]
    TQ, TK = _query_rows(S), _key_strip(S)
    nh, nq = W // HD, S // TQ

    def body(q_ref, k_ref, v_ref, o_ref):
        i = pl.program_id(1)
        q = q_ref[...]
        su = (_iota2((HD, HD), 0) > _iota2((HD, HD), 1)).astype(BF16)
        last = (i * TQ + TQ - 1) // TK

        def strip(g, carry, masked):
            acc, run = carry
            off = pl.multiple_of(g * TK, TK)
            _, a, run = _sb_strip(q, k_ref[pl.ds(off, TK), :], i * TQ, off, run, su, masked)
            return acc + _dot(a.astype(BF16), v_ref[pl.ds(off, TK), :]), run

        carry = strip(last, (jnp.zeros((TQ, HD), F32), jnp.zeros((TQ, 1), F32)), True)
        acc, _ = lax.fori_loop(0, last, lambda gg, c: strip(last - 1 - gg, c, False), carry)
        o_ref[...] = acc.astype(o_ref.dtype)

    return pl.pallas_call(
        body,
        grid=(nh, nq),
        in_specs=[
            pl.BlockSpec((TQ, HD), lambda h, i: (i, h)),
            pl.BlockSpec((S, HD), lambda h, i: (0, nh + h)),
            pl.BlockSpec((S, HD), lambda h, i: (0, 2 * nh + h)),
        ],
        out_specs=pl.BlockSpec((TQ, HD), lambda h, i: (i, h)),
        out_shape=jax.ShapeDtypeStruct((S, 2 * W), BF16),
        compiler_params=_cp(("parallel", "arbitrary")),
        name=name,
    )(p, p, p)


def sb_bwd(p, dcat, W, name):
    S = p.shape[0]
    TQ, TK = _query_rows(S), _key_strip(S)
    nh, nq = W // HD, S // TQ
    scale = HD ** -0.5

    def body(q_ref, k_ref, v_ref, do_ref, dp_ref, dk_acc, dv_acc, e_scr, z_scr):
        i = pl.program_id(1)
        q = q_ref[...]
        do = do_ref[...]
        su = (_iota2((HD, HD), 0) > _iota2((HD, HD), 1)).astype(BF16)
        sl = (_iota2((HD, HD), 0) < _iota2((HD, HD), 1)).astype(BF16)
        last = (i * TQ + TQ - 1) // TK

        @pl.when(i == 0)
        def _():
            dk_acc[...] = jnp.zeros_like(dk_acc)
            dv_acc[...] = jnp.zeros_like(dv_acc)

        def pass_a(g, run, masked):
            off = pl.multiple_of(g * TK, TK)
            z, a, run = _sb_strip(q, k_ref[pl.ds(off, TK), :], i * TQ, off, run, su, masked)
            e_scr[g] = a * _dot_nt(do, v_ref[pl.ds(off, TK), :])
            z_scr[g] = z
            dv_acc[pl.ds(off, TK), :] += _dot_tn(a.astype(BF16), do)
            return run

        run = pass_a(last, jnp.zeros((TQ, 1), F32), True)
        lax.fori_loop(0, last, lambda gg, r: pass_a(last - 1 - gg, r, False), run)

        def pass_b(g, carry, masked):
            dq, run_e = carry
            off = pl.multiple_of(g * TK, TK)
            e = e_scr[g]
            z = z_scr[g]
            within = _block_sums(e, sl)
            before = []
            for b in range(len(within)):
                before.append(within[b] + run_e)
                run_e = run_e + jnp.sum(e[:, b * HD : (b + 1) * HD], axis=1, keepdims=True)
            sig = 1.0 / (1.0 + jnp.exp(-z))
            dz = e * (1.0 - sig) - jnp.concatenate(before, axis=1) * sig
            if masked:
                dz = jnp.where(_strip_mask(z.shape, i * TQ, off, True), dz, 0.0)
            dz = (dz * scale).astype(BF16)
            dq = dq + _dot(dz, k_ref[pl.ds(off, TK), :])
            dk_acc[pl.ds(off, TK), :] += _dot_tn(dz, q)
            return dq, run_e

        carry = lax.fori_loop(0, last, lambda g, c: pass_b(g, c, False), (jnp.zeros((TQ, HD), F32), jnp.zeros((TQ, 1), F32)))
        dq, _ = pass_b(last, carry, True)
        dp_ref[0, pl.ds(pl.multiple_of(i * TQ, TQ), TQ), :] = dq.astype(dp_ref.dtype)

        @pl.when(i == nq - 1)
        def _():
            dp_ref[1] = dk_acc[...].astype(dp_ref.dtype)
            dp_ref[2] = dv_acc[...].astype(dp_ref.dtype)

    return pl.pallas_call(
        body,
        grid=(nh, nq),
        in_specs=[
            pl.BlockSpec((TQ, HD), lambda h, i: (i, h)),
            pl.BlockSpec((S, HD), lambda h, i: (0, nh + h)),
            pl.BlockSpec((S, HD), lambda h, i: (0, 2 * nh + h)),
            pl.BlockSpec((TQ, HD), lambda h, i: (i, h)),
        ],
        out_specs=pl.BlockSpec((3, S, HD), lambda h, i: (0, 0, h)),
        out_shape=jax.ShapeDtypeStruct((6, S, W), BF16),
        scratch_shapes=[
            pltpu.VMEM((S, HD), F32),
            pltpu.VMEM((S, HD), F32),
            pltpu.VMEM((S // TK, TQ, TK), F32),
            pltpu.VMEM((S // TK, TQ, TK), F32),
        ],
        compiler_params=_cp(("parallel", "arbitrary")),
        name=name,
    )(p, p, p, dcat)


def fox_gate_fwd(f, b, name):
    S = f.shape[0]
    nq = S // HD

    def body(f_ref, b_ref, c_ref, run):
        i = pl.program_id(0)

        @pl.when(i == 0)
        def _():
            run[...] = jnp.zeros_like(run)

        lf = -_softplus(-(f_ref[...] + b_ref[...]))
        tri = (_iota2((HD, HD), 0) >= _iota2((HD, HD), 1)).astype(BF16)
        c_ref[...] = _dot_ones_left(tri, lf) + run[...]
        run[...] += jnp.sum(lf, axis=0, keepdims=True)

    return pl.pallas_call(
        body,
        grid=(nq,),
        in_specs=[pl.BlockSpec((HD, 128), lambda i: (i, 0)), pl.BlockSpec((1, 128), lambda i: (0, 0))],
        out_specs=pl.BlockSpec((HD, 128), lambda i: (i, 0)),
        out_shape=jax.ShapeDtypeStruct((S, 128), F32),
        scratch_shapes=[pltpu.VMEM((1, 128), F32)],
        compiler_params=_cp(("arbitrary",)),
        name=name,
    )(f, b)


def fox_gate_bwd(f, b, dc, name):
    S = f.shape[0]
    nq = S // HD

    def body(f_ref, b_ref, dc_ref, df_ref, db_ref, run):
        i = pl.program_id(0)

        @pl.when(i == 0)
        def _():
            run[...] = jnp.zeros_like(run)

        dc = dc_ref[...]
        tri = (_iota2((HD, HD), 0) <= _iota2((HD, HD), 1)).astype(BF16)
        dlf = _dot_ones_left(tri, dc) + run[...]
        run[...] += jnp.sum(dc, axis=0, keepdims=True)
        x = f_ref[...] + b_ref[...]
        df = dlf * (1.0 / (1.0 + jnp.exp(x)))
        df_ref[...] = df
        db = jnp.sum(df, axis=0, keepdims=True)

        @pl.when(i == 0)
        def _():
            db_ref[...] = db

        @pl.when(i > 0)
        def _():
            db_ref[...] += db

    rev = pl.BlockSpec((HD, 128), lambda i: (nq - 1 - i, 0))
    vec = pl.BlockSpec((1, 128), lambda i: (0, 0))
    return pl.pallas_call(
        body,
        grid=(nq,),
        in_specs=[rev, vec, rev],
        out_specs=[rev, vec],
        out_shape=[jax.ShapeDtypeStruct((S, 128), F32), jax.ShapeDtypeStruct((1, 128), F32)],
        scratch_shapes=[pltpu.VMEM((1, 128), F32)],
        compiler_params=_cp(("arbitrary",)),
        name=name,
    )(f, b, dc)


def _fox_logits(q, ks, ct, cs, row0, off, masked):
    s = _dot_nt(q, ks) * (HD ** -0.5) + (ct - cs)
    if not masked:
        return s, None
    mask = _strip_mask(s.shape, row0, off, False)
    return jnp.where(mask, s, -1e30), mask


def fox_fwd(p, ccol, crow, cat, W, name):
    S = p.shape[0]
    TQ, TK = _query_rows(S), _key_strip(S)
    nh, nq = W // HD, S // TQ

    def body(q_ref, k_ref, v_ref, cc_ref, cr_ref, cat_ref, o_ref, lse_ref):
        i = pl.program_id(1)
        q = q_ref[...]
        ct = cc_ref[0]

        def step(g, carry, masked):
            m, l, acc = carry
            off = pl.multiple_of(g * TK, TK)
            s, _ = _fox_logits(q, k_ref[pl.ds(off, TK), :], ct, cr_ref[0, pl.ds(g, 1), :], i * TQ, off, masked)
            m_new = jnp.maximum(m, jnp.max(s, axis=1, keepdims=True))
            alpha = jnp.exp(m - m_new)
            pr = jnp.exp(s - m_new)
            l = alpha * l + jnp.sum(pr, axis=1, keepdims=True)
            acc = alpha * acc + _dot(pr.astype(BF16), v_ref[pl.ds(off, TK), :])
            return m_new, l, acc

        init = (jnp.full((TQ, 1), -1e30, F32), jnp.zeros((TQ, 1), F32), jnp.zeros((TQ, HD), F32))
        last = (i * TQ + TQ - 1) // TK
        m, l, acc = step(last, lax.fori_loop(0, last, lambda g, c: step(g, c, False), init), True)
        o_ref[...] = (acc / l).astype(o_ref.dtype)
        lse_ref[0] = m + jnp.log(l)

    return pl.pallas_call(
        body,
        grid=(nh, nq),
        in_specs=[
            pl.BlockSpec((TQ, HD), lambda h, i: (i, 2 * nh + h)),
            pl.BlockSpec((S, HD), lambda h, i: (0, 3 * nh + h)),
            pl.BlockSpec((S, HD), lambda h, i: (0, 4 * nh + h)),
            pl.BlockSpec((1, TQ, 1), lambda h, i: (h, i, 0)),
            pl.BlockSpec((1, S // TK, TK), lambda h, i: (h, 0, 0)),
            pl.BlockSpec(memory_space=pl.ANY),
        ],
        out_specs=[pl.BlockSpec((TQ, HD), lambda h, i: (i, nh + h)), pl.BlockSpec((1, TQ, 1), lambda h, i: (h, i, 0))],
        out_shape=[jax.ShapeDtypeStruct(cat.shape, cat.dtype), jax.ShapeDtypeStruct((nh, S, 1), F32)],
        input_output_aliases={5: 0},
        compiler_params=_cp(("parallel", "arbitrary")),
        name=name,
    )(p, p, p, ccol, crow, cat)


def fox_bwd(p, ccol, crow, cat, lse, dcat, dp, W, name):
    S = p.shape[0]
    TQ, TK = _query_rows(S), _key_strip(S)
    nh, nq = W // HD, S // TQ
    scale = HD ** -0.5

    def body(q_ref, k_ref, v_ref, cc_ref, cr_ref, o_ref, lse_ref, do_ref, dp_in_ref, dp_ref, dcs_ref, dct_ref, dk_acc, dv_acc):
        i = pl.program_id(1)
        q = q_ref[...]
        do = do_ref[...]
        ct = cc_ref[0]
        lse_i = lse_ref[0]
        delta = jnp.sum(do.astype(F32) * o_ref[...].astype(F32), axis=1, keepdims=True)

        @pl.when(i == 0)
        def _():
            dk_acc[...] = jnp.zeros_like(dk_acc)
            dv_acc[...] = jnp.zeros_like(dv_acc)
            dcs_ref[...] = jnp.zeros_like(dcs_ref)

        def step(g, carry, masked):
            dq, dct = carry
            off = pl.multiple_of(g * TK, TK)
            ks = k_ref[pl.ds(off, TK), :]
            s, mask = _fox_logits(q, ks, ct, cr_ref[0, pl.ds(g, 1), :], i * TQ, off, masked)
            pr = jnp.where(mask, jnp.exp(s - lse_i), 0.0) if masked else jnp.exp(s - lse_i)
            ds = pr * (_dot_nt(do, v_ref[pl.ds(off, TK), :]) - delta)
            dv_acc[pl.ds(off, TK), :] += _dot_tn(pr.astype(BF16), do)
            dsb = (ds * scale).astype(BF16)
            dk_acc[pl.ds(off, TK), :] += _dot_tn(dsb, q)
            dcs_ref[0, pl.ds(g, 1), :] += jnp.sum(ds, axis=0, keepdims=True)
            return dq + _dot(dsb, ks), dct + jnp.sum(ds, axis=1, keepdims=True)

        last = (i * TQ + TQ - 1) // TK
        carry = lax.fori_loop(0, last, lambda g, c: step(g, c, False), (jnp.zeros((TQ, HD), F32), jnp.zeros((TQ, 1), F32)))
        dq, dct = step(last, carry, True)
        dp_ref[0, pl.ds(pl.multiple_of(i * TQ, TQ), TQ), :] = dq.astype(dp_ref.dtype)
        dct_ref[0] = dct

        @pl.when(i == nq - 1)
        def _():
            dp_ref[1] = dk_acc[...].astype(dp_ref.dtype)
            dp_ref[2] = dv_acc[...].astype(dp_ref.dtype)

    return pl.pallas_call(
        body,
        grid=(nh, nq),
        in_specs=[
            pl.BlockSpec((TQ, HD), lambda h, i: (i, 2 * nh + h)),
            pl.BlockSpec((S, HD), lambda h, i: (0, 3 * nh + h)),
            pl.BlockSpec((S, HD), lambda h, i: (0, 4 * nh + h)),
            pl.BlockSpec((1, TQ, 1), lambda h, i: (h, i, 0)),
            pl.BlockSpec((1, S // TK, TK), lambda h, i: (h, 0, 0)),
            pl.BlockSpec((TQ, HD), lambda h, i: (i, nh + h)),
            pl.BlockSpec((1, TQ, 1), lambda h, i: (h, i, 0)),
            pl.BlockSpec((TQ, HD), lambda h, i: (i, nh + h)),
            pl.BlockSpec(memory_space=pl.ANY),
        ],
        out_specs=[
            pl.BlockSpec((3, S, HD), lambda h, i: (1, 0, h)),
            pl.BlockSpec((1, S // TK, TK), lambda h, i: (h, 0, 0)),
            pl.BlockSpec((1, TQ, 1), lambda h, i: (h, i, 0)),
        ],
        out_shape=[
            jax.ShapeDtypeStruct(dp.shape, dp.dtype),
            jax.ShapeDtypeStruct((nh, S // TK, TK), F32),
            jax.ShapeDtypeStruct((nh, S, 1), F32),
        ],
        input_output_aliases={8: 0},
        scratch_shapes=[pltpu.VMEM((S, HD), F32), pltpu.VMEM((S, HD), F32)],
        compiler_params=_cp(("parallel", "arbitrary")),
        name=name,
    )(p, p, p, ccol, crow, cat, lse, dcat, dp)


_GELU_K = math.sqrt(2.0 / math.pi)
_GELU_C = 0.044715


def _gelu(x):
    return 0.5 * x * (1.0 + jnp.tanh(_GELU_K * (x + _GELU_C * x * x * x)))


def _gelu_grad(x):
    t = jnp.tanh(_GELU_K * (x + _GELU_C * x * x * x))
    return 0.5 * (1.0 + t) + 0.5 * x * (1.0 - t * t) * (_GELU_K * (1.0 + 3.0 * _GELU_C * x * x))


def _layernorm_parts(gv):
    xc = gv - jnp.mean(gv, axis=-1, keepdims=True)
    r = lax.rsqrt(jnp.mean(xc * xc, axis=-1, keepdims=True) + EPS)
    return xc * r, r


def sg_fwd(p, sg_w, sg_bt, sg_g, W, name):
    S = p.shape[0]
    G, nq = W // HD, S // HD

    def body(u_ref, v_ref, w_ref, bt_ref, g_ref, o_ref):
        xh, _ = _layernorm_parts(_gelu(v_ref[...].astype(F32)))
        vn = (xh * g_ref[...]).astype(BF16)
        tri = _iota2((HD, HD), 0) >= _iota2((HD, HD), 1)
        for gi in range(G):
            cols = slice(gi * HD, (gi + 1) * HD)
            wt = jnp.where(tri, w_ref[gi], 0.0).astype(BF16)
            mixed = _dot(wt, vn[:, cols]) + bt_ref[:, gi : gi + 1]
            o_ref[:, cols] = (_gelu(u_ref[:, cols].astype(F32)) * mixed).astype(o_ref.dtype)

    return pl.pallas_call(
        body,
        grid=(nq,),
        in_specs=[
            pl.BlockSpec((HD, W), lambda i: (i, 0)),
            pl.BlockSpec((HD, W), lambda i: (i, 1)),
            pl.BlockSpec((G, HD, HD), lambda i: (0, 0, 0)),
            pl.BlockSpec((HD, G), lambda i: (0, 0)),
            pl.BlockSpec((1, W), lambda i: (0, 0)),
        ],
        out_specs=pl.BlockSpec((HD, W), lambda i: (i, 0)),
        out_shape=jax.ShapeDtypeStruct((S, 2 * W), BF16),
        compiler_params=_cp(("parallel",)),
        name=name,
    )(p, p, sg_w, sg_bt, sg_g.reshape(1, W))


def sg_bwd(p, sg_w, sg_bt, sg_g, dcat, W, name):
    S = p.shape[0]
    G, nq = W // HD, S // HD

    def body(u_ref, v_ref, w_ref, bt_ref, g_ref, do_ref, dp_ref, dw_ref, dbt_ref, dg_ref, dvn_scr):
        i = pl.program_id(0)

        @pl.when(i == 0)
        def _():
            dw_ref[...] = jnp.zeros_like(dw_ref)
            dbt_ref[...] = jnp.zeros_like(dbt_ref)
            dg_ref[...] = jnp.zeros_like(dg_ref)

        v = v_ref[...].astype(F32)
        xh, r = _layernorm_parts(_gelu(v))
        gg = g_ref[...]
        vn = (xh * gg).astype(BF16)
        tri = _iota2((HD, HD), 0) >= _iota2((HD, HD), 1)
        for gi in range(G):
            cols = slice(gi * HD, (gi + 1) * HD)
            wt = jnp.where(tri, w_ref[gi], 0.0).astype(BF16)
            mixed = _dot(wt, vn[:, cols]) + bt_ref[:, gi : gi + 1]
            u = u_ref[:, cols].astype(F32)
            do = do_ref[:, cols].astype(F32)
            dp_ref[0, :, cols] = (do * mixed * _gelu_grad(u)).astype(dp_ref.dtype)
            dmix = do * _gelu(u)
            dmb = dmix.astype(BF16)
            dw_ref[gi] += jnp.where(tri, _dot_nt(dmb, vn[:, cols]), 0.0)
            dbt_ref[:, gi : gi + 1] += jnp.sum(dmix, axis=1, keepdims=True)
            dvn_scr[:, cols] = _dot_tn(wt, dmb)
        dvn = dvn_scr[...]
        dg_ref[...] += jnp.sum(dvn * xh, axis=0, keepdims=True)
        dxh = dvn * gg
        dgv = r * (dxh - jnp.mean(dxh, axis=-1, keepdims=True) - xh * jnp.mean(dxh * xh, axis=-1, keepdims=True))
        dp_ref[1] = (dgv * _gelu_grad(v)).astype(dp_ref.dtype)

    return pl.pallas_call(
        body,
        grid=(nq,),
        in_specs=[
            pl.BlockSpec((HD, W), lambda i: (i, 0)),
            pl.BlockSpec((HD, W), lambda i: (i, 1)),
            pl.BlockSpec((G, HD, HD), lambda i: (0, 0, 0)),
            pl.BlockSpec((HD, G), lambda i: (0, 0)),
            pl.BlockSpec((1, W), lambda i: (0, 0)),
            pl.BlockSpec((HD, W), lambda i: (i, 0)),
        ],
        out_specs=[
            pl.BlockSpec((2, HD, W), lambda i: (0, i, 0)),
            pl.BlockSpec((G, HD, HD), lambda i: (0, 0, 0)),
            pl.BlockSpec((HD, G), lambda i: (0, 0)),
            pl.BlockSpec((1, W), lambda i: (0, 0)),
        ],
        out_shape=[
            jax.ShapeDtypeStruct((6, S, W), BF16),
            jax.ShapeDtypeStruct((G, HD, HD), F32),
            jax.ShapeDtypeStruct((HD, G), F32),
            jax.ShapeDtypeStruct((1, W), F32),
        ],
        scratch_shapes=[pltpu.VMEM((HD, W), F32)],
        compiler_params=_cp(("arbitrary",)),
        name=name,
    )(p, p, sg_w, sg_bt, sg_g.reshape(1, W), dcat)


def local_step(x, target, wts, at, on_grad):
    S, D = x.shape
    W = D // 2
    nb, F = wts["nb"], wts["F"]
    g = {}

    def ffn_fwd(xin, l):
        h = rms_fwd(xin, wts[f"{l}_ffn_norm_g"], f"{l}_ffn_rms")
        u = mm_nn(h, wts[f"{l}_ffn_up"], nb, f"{l}_ffn_up_mm")
        act = ffn_act_fwd(u, wts[f"{l}_ffn_conv_w"], F, f"{l}_ffn_act")
        half_tile = _pick(S, (512, 256, 128))
        xout = mm_nn(act, wts[f"{l}_ffn_down"], 1, f"{l}_ffn_down_mm", out_dtype=F32, res=xin,
                     tm=half_tile, tn=_pick(D, (512, 256, 128)), tk=F)
        return xout, (xin, h, u, act)

    def ffn_bwd(dxout, dxoutb, saved, l):
        xin, h, u, act = saved
        dact = mm_nt(dxoutb, wts[f"{l}_ffn_down"], 1, S, F, f"{l}_ffn_down_dx", tko=_pick(F, (512, 256, 128)), tn=D)
        dact = on_grad(f"{l}_ffn_down", mm_tn(act, dxoutb, 1, D, f"{l}_ffn_down_dw", tn=D), dact)
        du, dcw = ffn_act_bwd(u, wts[f"{l}_ffn_conv_w"], dact, F, f"{l}_ffn_act_bwd")
        g[f"{l}_ffn_conv_w"] = jnp.concatenate([dcw[0], dcw[1]], axis=1)
        du2 = du.reshape(2 * S, F)
        n = wts[f"{l}_ffn_up"].shape[1]
        tn = _pick(n, (1408, 1024, 768, 512, 256, 128))
        per_half = F // tn
        nt = n // tn

        def up_block(i, j, t):
            vb = j * nt + t
            return vb // per_half, vb % per_half

        tm = _pick(S, (1024, 512, 256, 128))

        def nt_map(i, j, t):
            half, cb = up_block(i, j, t)
            return (half * (S // tm) + i, cb)

        def tn_map(j, t):
            half, cb = up_block(0, j, t)
            return (half, cb)

        dh = mm_nt(du2, wts[f"{l}_ffn_up"], nb, S, D, f"{l}_ffn_up_dx", dy_maps=[nt_map], tm=tm, tko=D, tn=tn)
        dh = on_grad(f"{l}_ffn_up", mm_tn(h, du2, nb, n, f"{l}_ffn_up_dw", dy_maps=[tn_map], tko=_pick(D, (1024, 512, 256, 128)), tn=tn), dh)
        dxin, dxinb, dg = rms_bwd(xin, wts[f"{l}_ffn_norm_g"], dh, dxout, f"{l}_ffn_rms_bwd")
        g[f"{l}_ffn_norm_g"] = dg
        return dxin, dxinb

    h0 = rms_fwd(x, wts["l0_mix_norm_g"], "l0_mix_rms")
    p0 = mm_nn(h0, wts["l0_w_in"], nb, "l0_w_in_mm")
    cat0 = sb_fwd(p0, W, "l0_sb_fwd")
    cat0 = sc_fwd(p0, wts["l0_sc_conv_w"], cat0, W, "l0_sc_fwd")
    x1 = mm_nn(cat0, wts["l0_w_out"], 1, "l0_w_out_mm", out_dtype=F32, res=x, tm=S, tn=_pick(D, (512, 256, 128)))
    x2, ffn0_saved = ffn_fwd(x1, "l0")

    x2 = at("l1_w_in", x2, None)
    nh = W // HD
    h2 = rms_fwd(x2, wts["l1_mix_norm_g"], "l1_mix_rms")
    p1 = mm_nt(h2, wts["l1_w_in_t"], 1, S, 5 * W, "l1_w_in_mm", tn=D)
    f = mm_nt(h2, wts["l1_w_f_t"], 1, S, 128, "l1_w_f_mm", out_dtype=F32, tn=D)
    bf = jnp.zeros((1, 128), F32).at[0, :nh].set(wts["l1_fox_b_f"])
    c = fox_gate_fwd(f, bf, "l1_fox_gate")
    c_heads = c[:, :nh].T
    ccol = c_heads[:, :, None]
    crow = c_heads.reshape(nh, S // _key_strip(S), _key_strip(S))
    sg_bt = wts["l1_sg_b"].T
    cat1 = sg_fwd(p1, wts["l1_sg_w"], sg_bt, wts["l1_sg_norm_g"], W, "l1_sg_fwd")
    cat1, lse = fox_fwd(p1, ccol, crow, cat1, W, "l1_fox_fwd")
    x3 = mm_nn(cat1, wts["l1_w_out"], 1, "l1_w_out_mm", out_dtype=F32, res=x2, tm=S, tn=_pick(D, (512, 256, 128)))
    x4, ffn1_saved = ffn_fwd(x3, "l1")

    dx4, dx4b, dgf, loss = loss_head(x4, wts["final_norm_g"], target, "loss_head")
    dx4b = at("loss", dx4b, loss)
    g["final_norm_g"] = dgf

    dx3, dx3b = ffn_bwd(dx4, dx4b, ffn1_saved, "l1")
    dcat1 = mm_nt(dx3b, wts["l1_w_out"], 1, S, D, "l1_w_out_dx", tn=D)
    dcat1 = on_grad("l1_w_out", mm_tn(cat1, dx3b, 1, D, "l1_w_out_dw", tn=D), dcat1)
    dp1, dsgw, dsgbt, dsgg = sg_bwd(p1, wts["l1_sg_w"], sg_bt, wts["l1_sg_norm_g"], dcat1, W, "l1_sg_bwd")
    dp1, dcs, dct = fox_bwd(p1, ccol, crow, cat1, lse, dcat1, dp1, W, "l1_fox_bwd")
    g["l1_sg_w"], g["l1_sg_b"], g["l1_sg_norm_g"] = dsgw, dsgbt.T, dsgg
    dc = jnp.zeros((S, 128), F32).at[:, :nh].set((dct[:, :, 0] - dcs.reshape(nh, S)).T)
    df, dbf = fox_gate_bwd(f, bf, dc, "l1_fox_gate_bwd")
    g["l1_fox_b_f"] = dbf[0, :nh]
    dfb = df.astype(BF16)
    tk1 = _pick(W, (1024, 512, 256, 128))
    tx1 = _pick(W, (512, 256, 128))
    tm1 = _pick(S, (1024, 512, 256, 128))
    part_of = lambda pt: pt + pt // 2 - pt // 4

    def a_map1(i, k):
        return (part_of(k // (W // tk1)) * (S // tm1) + i, k % (W // tk1))

    def x_map1(ko):
        return (part_of(ko // (W // tx1)), ko % (W // tx1))

    dp1_2d = dp1.reshape(6 * S, W)
    dw_main = mm_tn(dp1_2d, h2, 1, D, "l1_w_in_dw", tko=tx1, tn=D, x_map=x_map1, x_shape=(S, 5 * W))
    dw_f = mm_tn(dfb, h2, 1, D, "l1_w_f_dw", tn=D)
    dh2 = mm_nn(dfb, wts["l1_w_f_t"], 1, "l1_w_f_dx", out_dtype=F32)
    dh2 = mm_nn(dp1_2d, wts["l1_w_in_t"], 1, "l1_w_in_dx", res=dh2, tm=tm1, tk=tk1, a_map=a_map1, a_shape=(S, 5 * W))
    dh2 = on_grad("l1_w_in", jnp.concatenate([dw_main, dw_f[:nh]], axis=0), dh2)
    dx2, dx2b, dg = rms_bwd(x2, wts["l1_mix_norm_g"], dh2, dx3, "l1_mix_rms_bwd")
    g["l1_mix_norm_g"] = dg

    dx1, dx1b = ffn_bwd(dx2, dx2b, ffn0_saved, "l0")
    dcat0 = mm_nt(dx1b, wts["l0_w_out"], 1, S, D, "l0_w_out_dx", tn=D)
    dcat0 = on_grad("l0_w_out", mm_tn(cat0, dx1b, 1, D, "l0_w_out_dw", tn=D), dcat0)
    dp0 = sb_bwd(p0, dcat0, W, "l0_sb_bwd")
    dp0, dscw = sc_bwd(p0, wts["l0_sc_conv_w"], dcat0, dp0, W, "l0_sc_bwd")
    g["l0_sc_conv_w"] = dscw
    dp0 = at("small_ready", dp0, g)
    n0 = wts["l0_w_in"].shape[1]
    td0 = math.gcd(n0, W)
    nd0 = n0 // td0
    tm0 = _pick(S, (1024, 512, 256, 128))
    per_part0 = W // td0

    def nt_maps0(k):
        def f(i, j, t):
            vb = j * nd0 + k
            return ((vb // per_part0) * (S // tm0) + i, vb % per_part0)
        return f

    def tn_maps0(k):
        def f(j, t):
            vb = j * nd0 + k
            return (vb // per_part0, vb % per_part0)
        return f

    dp0_2d = dp0.reshape(6 * S, W)
    dw0 = mm_tn(h0, dp0_2d, nb, n0, "l0_w_in_dw", dy_maps=[tn_maps0(k) for k in range(nd0)], tko=_pick(D, (1024, 512, 256, 128)), tn=n0)
    dp0_2d = on_grad("l0_w_in", dw0, dp0_2d)
    dp0_2d = on_grad(None, None, dp0_2d)
    dh0 = mm_nt(dp0_2d, wts["l0_w_in"], nb, S, D, "l0_w_in_dx", dy_maps=[nt_maps0(k) for k in range(nd0)], tm=tm0, tko=D, tn=n0)
    dh0 = at("small_done", dh0, None)
    dx0, _, dg = rms_bwd(x, wts["l0_mix_norm_g"], dh0, dx1, "l0_mix_rms_bwd")
    g["l0_mix_norm_g"] = dg
    return dx0, g


GATHER_ID = 1


def _place():
    return lax.axis_index("x"), lax.axis_index("y"), lax.axis_index("c")


def _other_chips(x, y):
    return [(x, 1 - y), (1 - x, y), (1 - x, 1 - y)]


def _handshake(peers):
    barrier = pltpu.get_barrier_semaphore()
    for peer in peers:
        pl.semaphore_signal(barrier, inc=1, device_id=peer, device_id_type=MESH)
    pl.semaphore_wait(barrier, len(peers))


UPDATE_LAG = 2


def _on_sequencer(body, out_type, scratch_types, collective_id, name):
    return pl.kernel(
        body,
        out_type=out_type,
        mesh=plsc.ScalarSubcoreMesh(axis_name="seq", num_cores=1),
        scratch_types=scratch_types,
        compiler_params=pltpu.CompilerParams(collective_id=collective_id),
        name=name,
    )


def all_gather(arrs, name):
    n = len(arrs)

    def body(*refs):
        xs, outs = refs[:n], refs[n : 2 * n]
        send_sems, recv_sems, local_sems = refs[2 * n :]
        x, y, c = _place()
        me, sibling = (x, y, c), (x, y, 1 - c)
        chips = _other_chips(x, y)
        _handshake([sibling] + [(*chip, c) for chip in chips])

        def copy(a, k, block, to, src=None):
            px, py, pc = block
            dst = outs[a].at[4 * px + 2 * py + pc]
            return pltpu.make_async_remote_copy(
                src_ref=dst if src is None else src, dst_ref=dst,
                send_sem=send_sems.at[7 * a + k], recv_sem=recv_sems.at[7 * a + k], device_id=to, device_id_type=MESH,
            )

        mine = [pltpu.make_async_copy(xs[a], outs[a].at[4 * x + 2 * y + c], local_sems.at[a]) for a in range(n)]
        for cp in mine:
            cp.start()
        first = []
        for a in range(n):
            first.append(copy(a, 0, me, sibling, src=xs[a]))
            first += [copy(a, 1 + j, me, (*chip, c), src=xs[a]) for j, chip in enumerate(chips)]
        for cp in first:
            cp.start()
        passed = []
        for a in range(n):
            for j, chip in enumerate(chips):
                copy(a, 1 + j, (*chip, c), me).wait_recv()
                cp = copy(a, 4 + j, (*chip, c), sibling)
                cp.start()
                passed.append(cp)
        for a in range(n):
            copy(a, 0, sibling, me).wait_recv()
            for j, chip in enumerate(chips):
                copy(a, 4 + j, (*chip, 1 - c), me).wait_recv()
        for cp in first + passed:
            cp.wait_send()
        for cp in mine:
            cp.wait()

    out_type = [jax.ShapeDtypeStruct((NDEV,) + a.shape, a.dtype) for a in arrs]
    sems = [pltpu.SemaphoreType.DMA((7 * n,)), pltpu.SemaphoreType.DMA((7 * n,)), pltpu.SemaphoreType.DMA((n,))]
    return _on_sequencer(body, out_type, sems, GATHER_ID, name)(*arrs)


_IN_HBM = pl.BlockSpec(memory_space=pltpu.HBM)
_IN_SEM = pl.BlockSpec(memory_space=pltpu.SEMAPHORE)
_EFFECT = pltpu.SideEffectType.DATAFLOW_SIDE_EFFECTING


def _split_start(make_copies, src, land_shape, nsem, name):
    def body(src_ref, land_ref, send_sems, recv_sems, land_thru, token):
        for cp in make_copies(src_ref, land_ref, send_sems, recv_sems):
            cp.start()
        token[...] = jnp.zeros_like(token)

    send_sems, recv_sems, land_thru, token = pl.pallas_call(
        body,
        name=name,
        out_shape=(
            pltpu.SemaphoreType.DMA((nsem,)), pltpu.SemaphoreType.DMA((nsem,)),
            pltpu.HBM(land_shape, src.dtype), jax.ShapeDtypeStruct((8, 128), F32),
        ),
        in_specs=(_IN_HBM, _IN_HBM),
        out_specs=(_IN_SEM, _IN_SEM, _IN_HBM, pl.BlockSpec(memory_space=pltpu.VMEM)),
        input_output_aliases={1: 2},
        compiler_params=pltpu.CompilerParams(has_side_effects=_EFFECT),
    )(src, pltpu.with_memory_space_constraint(lax.empty(land_shape, src.dtype), pltpu.HBM))
    return send_sems, recv_sems, src, land_thru, token


def _split_wait(make_copies, send_sems, recv_sems, src_thru, land_thru, after, name):
    def body(src_ref, land_ref, send_sems, recv_sems, after_ref, land_out):
        for cp in make_copies(src_ref, land_ref, send_sems, recv_sems):
            cp.wait_send()
            cp.wait_recv()

    return pl.pallas_call(
        body,
        name=name,
        out_shape=pltpu.HBM(land_thru.shape, land_thru.dtype),
        in_specs=(_IN_HBM, _IN_HBM, _IN_SEM, _IN_SEM, pl.BlockSpec(memory_space=pl.ANY)),
        out_specs=_IN_HBM,
        input_output_aliases={1: 0},
        compiler_params=pltpu.CompilerParams(has_side_effects=_EFFECT),
    )(src_thru, land_thru, send_sems, recv_sems, after)


def _pair_copies(src_ref, land_ref, send_sems, recv_sems):
    x, y, c = _place()
    return [
        pltpu.make_async_remote_copy(
            src_ref=src_ref.at[k, 1 - c], dst_ref=land_ref.at[k],
            send_sem=send_sems.at[k], recv_sem=recv_sems.at[k], device_id=(x, y, 1 - c), device_id_type=MESH,
        )
        for k in range(4)
    ]


def _direct_copies(src_ref, land_ref, send_sems, recv_sems):
    x, y, c = _place()
    me = 4 * x + 2 * y + c
    copies = []
    for k in range(NDEV - 1):
        to = (me + k + 1) % NDEV
        copies.append(pltpu.make_async_remote_copy(
            src_ref=src_ref, dst_ref=land_ref.at[me], send_sem=send_sems.at[k], recv_sem=recv_sems.at[k],
            device_id=(to // 4, (to // 2) % 2, to % 2), device_id_type=MESH,
        ))
    return copies


def _chip_copies(src_ref, land_ref, send_sems, recv_sems):
    x, y, c = _place()
    return [
        pltpu.make_async_remote_copy(
            src_ref=src_ref.at[2 * px + py], dst_ref=land_ref.at[2 * x + y],
            send_sem=send_sems.at[j], recv_sem=recv_sems.at[j], device_id=(px, py, c), device_id_type=MESH,
        )
        for j, (px, py) in enumerate(_other_chips(x, y))
    ]


def _row_tile(R, C, max_elems):
    if R * C <= max_elems:
        return R
    best = None
    for tr in range(16, R, 16):
        if R % tr == 0 and tr * C <= max_elems:
            best = tr
    return best or R


def pair_sum(a42, land4, core, name):
    _, _, R, C = a42.shape
    tr = _row_tile(R, C, 1 << 20)

    def body(core_ref, a_ref, l_ref, o_ref):
        o_ref[...] = (a_ref[0].astype(F32) + l_ref[...].astype(F32)).astype(o_ref.dtype)

    return pl.pallas_call(
        body,
        grid_spec=pltpu.PrefetchScalarGridSpec(
            num_scalar_prefetch=1,
            grid=(4, R // tr),
            in_specs=[
                pl.BlockSpec((1, 1, tr, C), lambda k, r, core_ref: (k, core_ref[0], r, 0)),
                pl.BlockSpec((1, tr, C), lambda k, r, core_ref: (k, r, 0)),
            ],
            out_specs=pl.BlockSpec((1, tr, C), lambda k, r, core_ref: (k, r, 0)),
        ),
        out_shape=jax.ShapeDtypeStruct((4, R, C), BF16),
        compiler_params=_cp(("parallel", "parallel")),
        name=name,
    )(core, a42, land4)


def sum_slots(parts, name):
    P, R, C = parts.shape

    def body(p_ref, o_ref):
        acc = p_ref[0].astype(F32)
        for k in range(1, P):
            acc = acc + p_ref[k].astype(F32)
        o_ref[...] = acc

    tr = _row_tile(R, P * C, 1 << 21)
    return pl.pallas_call(
        body,
        grid=(R // tr,),
        in_specs=[pl.BlockSpec((P, tr, C), lambda r: (0, r, 0))],
        out_specs=pl.BlockSpec((tr, C), lambda r: (r, 0)),
        out_shape=jax.ShapeDtypeStruct((R, C), F32),
        compiler_params=_cp(("parallel",)),
        name=name,
    )(parts)


def adamw(w, m, v, parts, name):
    R, C = w.shape
    P = parts.shape[0]
    tr = _pick(R, (256, 128, 64, 32, 16, 8))
    c1 = 1.0 - ADAM_B1 ** ADAM_STEP
    c2 = 1.0 - ADAM_B2 ** ADAM_STEP

    def body(w_ref, m_ref, v_ref, p_ref, g_ref, d_ref, nm_ref, nv_ref):
        g = p_ref[0].astype(F32)
        for k in range(1, P):
            g = g + p_ref[k].astype(F32)
        nm = ADAM_B1 * m_ref[...] + (1.0 - ADAM_B1) * g
        nv = ADAM_B2 * v_ref[...] + (1.0 - ADAM_B2) * (g * g)
        g_ref[...] = g
        nm_ref[...] = nm
        nv_ref[...] = nv
        d_ref[...] = -ADAM_LR * ((nm / c1) / (jnp.sqrt(nv / c2) + ADAM_EPS) + ADAM_WD * w_ref[...])

    blk = pl.BlockSpec((tr, C), lambda r: (r, 0))
    shp = jax.ShapeDtypeStruct((R, C), F32)
    return pl.pallas_call(
        body,
        grid=(R // tr,),
        in_specs=[blk, blk, blk, pl.BlockSpec((P, tr, C), lambda r: (0, r, 0))],
        out_specs=[blk, blk, blk, blk],
        out_shape=[shp, shp, shp, shp],
        compiler_params=_cp(("parallel",)),
        name=name,
    )(w, m, v, parts)


def adamw_reduced(w, m, v, own, land, chip, name):
    R, C = w.shape
    if R % 8 == 0:
        tr, tc = _pick(R, (256, 128, 64, 32, 16, 8)), C
    else:
        tr, tc = R, _pick(C, (256, 128))
    c1 = 1.0 - ADAM_B1 ** ADAM_STEP
    c2 = 1.0 - ADAM_B2 ** ADAM_STEP

    def body(chip_ref, w_ref, m_ref, v_ref, own_ref, land_ref, g_ref, d_ref, nm_ref, nv_ref):
        mine = own_ref[0].astype(F32)
        g = None
        for k in range(4):
            term = jnp.where(chip_ref[0] == k, mine, land_ref[k].astype(F32))
            g = term if g is None else g + term
        nm = ADAM_B1 * m_ref[...] + (1.0 - ADAM_B1) * g
        nv = ADAM_B2 * v_ref[...] + (1.0 - ADAM_B2) * (g * g)
        g_ref[...] = g
        nm_ref[...] = nm
        nv_ref[...] = nv
        d_ref[...] = -ADAM_LR * ((nm / c1) / (jnp.sqrt(nv / c2) + ADAM_EPS) + ADAM_WD * w_ref[...])

    blk = pl.BlockSpec((tr, tc), lambda r, c, chip_ref: (r, c))
    shp = jax.ShapeDtypeStruct((R, C), F32)
    return pl.pallas_call(
        body,
        grid_spec=pltpu.PrefetchScalarGridSpec(
            num_scalar_prefetch=1,
            grid=(R // tr, C // tc),
            in_specs=[
                blk, blk, blk,
                pl.BlockSpec((1, tr, tc), lambda r, c, chip_ref: (chip_ref[0], r, c)),
                pl.BlockSpec((4, tr, tc), lambda r, c, chip_ref: (0, r, c)),
            ],
            out_specs=[blk, blk, blk, blk],
        ),
        out_shape=[shp, shp, shp, shp],
        compiler_params=_cp(("parallel", "parallel")),
        name=name,
    )(chip, w, m, v, own, land)


_WEIGHTS = [
    "l0_mix_norm_g", "l0_w_in", "l0_sc_conv_w", "l0_w_out", "l0_ffn_norm_g", "l0_ffn_up", "l0_ffn_conv_w", "l0_ffn_down",
    "l1_mix_norm_g", "l1_w_in", "l1_fox_b_f", "l1_sg_w", "l1_sg_b", "l1_sg_norm_g", "l1_w_out", "l1_ffn_norm_g",
    "l1_ffn_up", "l1_ffn_conv_w", "l1_ffn_down", "final_norm_g",
]
_ROW_SHARDED = ["l0_w_out", "l0_ffn_down", "l1_w_out", "l1_ffn_down"]
_BIG = ["l0_w_in", "l0_w_out", "l0_ffn_up", "l0_ffn_down", "l1_w_in", "l1_w_out", "l1_ffn_up", "l1_ffn_down"]
_CONV = ["l0_sc_conv_w", "l0_ffn_conv_w", "l1_ffn_conv_w"]
_SMALL = [n for n in _WEIGHTS if n not in _BIG]
_LAST_SMALL = "l0_mix_norm_g"
_PACK_ROWS = 8


def _pack(arrs):
    flat = []
    for a in arrs:
        v = a.reshape(-1).astype(F32)
        pad = (-v.shape[0]) % (_PACK_ROWS * 128)
        flat.append(jnp.pad(v, (0, pad)))
    return jnp.concatenate(flat).reshape(-1, 128)


def _unpack(packed, shapes):
    out, off = [], 0
    flat = packed.reshape(-1)
    for shp in shapes:
        size = math.prod(shp)
        out.append(flat[off : off + size].reshape(shp))
        off += size + (-size) % (_PACK_ROWS * 128)
    return out


def kernel(x, l0_mix_norm_g, l0_w_in, l0_sc_conv_w, l0_w_out, l0_ffn_norm_g, l0_ffn_up, l0_ffn_conv_w, l0_ffn_down, l1_mix_norm_g, l1_w_in, l1_fox_b_f, l1_sg_w, l1_sg_b, l1_sg_norm_g, l1_w_out, l1_ffn_norm_g, l1_ffn_up, l1_ffn_conv_w, l1_ffn_down, final_norm_g, loss_target, m_l0_mix_norm_g, m_l0_w_in, m_l0_sc_conv_w, m_l0_w_out, m_l0_ffn_norm_g, m_l0_ffn_up, m_l0_ffn_conv_w, m_l0_ffn_down, m_l1_mix_norm_g, m_l1_w_in, m_l1_fox_b_f, m_l1_sg_w, m_l1_sg_b, m_l1_sg_norm_g, m_l1_w_out, m_l1_ffn_norm_g, m_l1_ffn_up, m_l1_ffn_conv_w, m_l1_ffn_down, m_final_norm_g, v_l0_mix_norm_g, v_l0_w_in, v_l0_sc_conv_w, v_l0_w_out, v_l0_ffn_norm_g, v_l0_ffn_up, v_l0_ffn_conv_w, v_l0_ffn_down, v_l1_mix_norm_g, v_l1_w_in, v_l1_fox_b_f, v_l1_sg_w, v_l1_sg_b, v_l1_sg_norm_g, v_l1_w_out, v_l1_ffn_norm_g, v_l1_ffn_up, v_l1_ffn_conv_w, v_l1_ffn_down, v_final_norm_g):
    given = dict(locals())
    w = {n: given[n] for n in _WEIGHTS}
    mom = {n: given["m_" + n] for n in _WEIGHTS}
    var = {n: given["v_" + n] for n in _WEIGHTS}
    xs, target = x[0], loss_target[0]
    S, D = xs.shape
    W = D // 2
    nh = W // HD
    cx, cy, cc = _place()
    me = 4 * cx + 2 * cy + cc

    wts = {"nb": NDEV, "F": l0_ffn_down.shape[0] * NDEV}
    for n in _SMALL:
        if n not in _CONV:
            wts[n] = w[n]
    gathered, loss_sum = {}, []

    def start_gather(names):
        srcs = [(w[n].T if n == "l1_w_in" else w[n]).astype(BF16) for n in names]
        taps = [w[c] for c in _CONV] if names[0] == _BIG[0] else []
        got = all_gather(srcs + taps, "gather_" + "_".join(names))
        for n, full in zip(names, got):
            if n == "l1_w_in":
                gathered[n] = full
            elif n in _ROW_SHARDED:
                wts[n] = full.reshape(-1, D)
            else:
                wts[n] = full.reshape(NDEV * D, -1)
        for c, full in zip(_CONV, got[len(names):] if taps else []):
            wts[c] = full.transpose(1, 0, 2).reshape(CONV_K, -1)

    def at(point, after, value):
        if point == "l1_w_in":
            got, after = lax.optimization_barrier((gathered[point], after))
            wts["l1_w_in_t"] = got.reshape(-1, D)
            wts["l1_w_f_t"] = jnp.pad(wts["l1_w_in_t"][5 * W :], ((0, 128 - nh), (0, 0)))
        elif point == "loss":
            total, after = lax.optimization_barrier((lax.psum(value[0, 0], ("x", "y", "c")), after))
            loss_sum.append(total)
        elif point == "small_ready":
            early = [n for n in _SMALL if n != _LAST_SMALL]
            gathered["small"] = all_gather([_pack([value[n] for n in early])], "gather_small_grads")[0]
        elif point == "small_done":
            after = update_small([n for n in _SMALL if n != _LAST_SMALL], gathered["small"], "small", after)
        return after

    out_g, out_d, out_m, out_v = {}, {}, {}, {}

    def update_small(names, all_terms, tag, after=None):
        shapes = [w[n].shape for n in names]
        full_shapes = [(CONV_K, NDEV * w[n].shape[1]) if n in _CONV else w[n].shape for n in names]
        grads = {}
        for n, t in zip(names, _unpack(sum_slots(all_terms, f"sum_{tag}_grads"), full_shapes)):
            if n in _CONV:
                cols = w[n].shape[1]
                t = lax.dynamic_slice_in_dim(t, me * cols, cols, axis=1)
            grads[n] = t
        res = adamw(
            _pack([w[n] for n in names]), _pack([mom[n] for n in names]), _pack([var[n] for n in names]),
            _pack([grads[n] for n in names])[None], f"adamw_{tag}",
        )
        if after is not None:
            res, after = lax.optimization_barrier((res, after))
        for dst, packed_out in zip((out_g, out_d, out_m, out_v), res):
            for n, t in zip(names, _unpack(packed_out, shapes)):
                dst[n] = t
        return after

    core = jnp.reshape(cc, (1,)).astype(jnp.int32)
    chip = jnp.reshape(2 * cx + cy, (1,)).astype(jnp.int32)
    pair_flying, chip_flying = [], []

    def tie(value, after):
        if after is None:
            return value, None
        return lax.optimization_barrier((value, after))

    def to_chips(after):
        n, flying = pair_flying.pop()
        landed = _split_wait(_pair_copies, *flying, f"reduce_pair_wait_{n}")
        summed = pair_sum(flying[2], landed, core, f"pair_sum_{n}")
        *flying, token = _split_start(_chip_copies, summed, summed.shape, 3, f"reduce_chips_{n}")
        token, after = tie(token, after)
        chip_flying.append((n, flying + [token]))
        return after

    def update(after, behind=None):
        n, flying = chip_flying.pop(0)
        if behind is not None:
            flying[4], _ = lax.optimization_barrier((flying[4], behind))
        landed = _split_wait(_chip_copies, *flying, f"reduce_chips_wait_{n}")
        turn = (lambda t: t.T) if n == "l1_w_in" else (lambda t: t)
        res = adamw_reduced(turn(w[n]), turn(mom[n]), turn(var[n]), flying[2], landed, chip, f"adamw_{n}")
        res, after = tie(res, after)
        out_g[n], out_d[n], out_m[n], out_v[n] = [turn(t) for t in res]
        return after, res[0]

    def on_grad(n, term, after):
        if n is None:
            return to_chips(after)
        if n in _ROW_SHARDED or n == "l1_w_in":
            term = term.reshape(NDEV, -1, D)
        else:
            term = term.reshape(NDEV, D, -1)
        term = term.reshape((4, 2) + term.shape[1:])
        *flying, token = _split_start(_pair_copies, term, term.shape[:1] + term.shape[2:], 4, f"reduce_pair_{n}")
        token, after = tie(token, after)
        if len(chip_flying) == UPDATE_LAG:
            after, _ = update(after)
        if pair_flying:
            after = to_chips(after)
        pair_flying.append((n, flying + [token]))
        return after

    for n in _BIG:
        start_gather([n])
    dx, g = local_step(xs, target, wts, at, on_grad)
    last = _pack([g[_LAST_SMALL]])
    *flying, done = _split_start(_direct_copies, last, (NDEV,) + last.shape, NDEV - 1, "gather_last_grad")
    while len(chip_flying) > 1:
        _, done = update(None, behind=done)
    landed = _split_wait(_direct_copies, *flying, done, "gather_last_grad_wait")
    update_small([_LAST_SMALL], lax.dynamic_update_slice(landed, last[None], (me, 0, 0)), "last")
    update(None, behind=out_g[_LAST_SMALL])
    loss = loss_sum[0]

    return (loss, dx[None], *[out_g[n] for n in _WEIGHTS], *[out_d[n] for n in _WEIGHTS],
            *[out_m[n] for n in _WEIGHTS], *[out_v[n] for n in _WEIGHTS])
```

```python
import functools
import math

import jax
import jax.numpy as jnp
from jax import lax
from jax.experimental import pallas as pl
from jax.experimental.pallas import tpu as pltpu
from jax.experimental.pallas import tpu_sc as plsc

F32 = jnp.float32
BF16 = jnp.bfloat16
HD = 128
EPS = 1e-6
CONV_K = 3
VMEM_LIMIT_BYTES = 48 << 20
NDEV = 8
MESH = pl.DeviceIdType.MESH

ADAM_LR = 0.001
ADAM_B1 = 0.9
ADAM_B2 = 0.999
ADAM_EPS = 1e-08
ADAM_WD = 0.01
ADAM_STEP = 10


def _cp(sem):
    return pltpu.CompilerParams(dimension_semantics=sem, vmem_limit_bytes=VMEM_LIMIT_BYTES)


def _pick(n, prefs):
    for p in prefs:
        if n % p == 0:
            return p
    return n


def _dot(a, b):
    return jnp.dot(a, b, preferred_element_type=F32)


def _dot_nt(a, b):
    return lax.dot_general(a, b, (((1,), (1,)), ((), ())), preferred_element_type=F32)


def _dot_tn(a, b):
    return lax.dot_general(a, b, (((0,), (0,)), ((), ())), preferred_element_type=F32)


def _split3(x):
    hi = x.astype(BF16)
    r = x - hi.astype(F32)
    mid = r.astype(BF16)
    lo = (r - mid.astype(F32)).astype(BF16)
    return hi, mid, lo


def _dot_ones_left(ones_bf16, x):
    hi, mid, lo = _split3(x)
    return _dot(ones_bf16, hi) + _dot(ones_bf16, mid) + _dot(ones_bf16, lo)


def _iota2(shape, axis):
    return lax.broadcasted_iota(jnp.int32, shape, axis)


def mm_nn(a, w2d, nb, name, out_dtype=BF16, res=None, tm=None, tn=None, tk=None, a_map=None, a_shape=None):
    M, K = a_shape or a.shape
    n = w2d.shape[1]
    assert w2d.shape[0] == nb * K or (nb == 1 and w2d.shape[0] > K)
    a_map = a_map or (lambda i, k: (i, k))
    tm = tm or _pick(M, (1024, 512, 256, 128))
    tn = tn or _pick(n, (1408, 1024, 768, 512, 256, 128))
    tk = tk or (K if K <= 2048 else _pick(K, (1408, 1024, 512, 256, 128)))
    nk, nt = K // tk, n // tn
    has_res = res is not None

    def body(*refs):
        if has_res:
            a_ref, w_ref, r_ref, o_ref = refs[:4]
        else:
            a_ref, w_ref, o_ref = refs[:3]
            r_ref = None
        part = _dot(a_ref[...], w_ref[...])

        def finish(acc):
            if r_ref is not None:
                acc = acc + r_ref[...].astype(F32)
            o_ref[...] = acc.astype(o_ref.dtype)

        if nk == 1:
            finish(part)
        else:
            acc_ref = refs[-1]
            k = pl.program_id(3)

            @pl.when(k == 0)
            def _():
                acc_ref[...] = part

            @pl.when(k > 0)
            def _():
                acc_ref[...] += part

            @pl.when(k == nk - 1)
            def _():
                finish(acc_ref[...])

    in_specs = [
        pl.BlockSpec((tm, tk), lambda i, j, t, k: a_map(i, k)),
        pl.BlockSpec((tk, tn), lambda i, j, t, k: (j * nk + k, t)),
    ]
    args = [a, w2d]
    out_spec = pl.BlockSpec((tm, tn), lambda i, j, t, k: (i, j * nt + t))
    if has_res:
        in_specs.append(out_spec)
        args.append(res)
    return pl.pallas_call(
        body,
        grid=(M // tm, nb, nt, nk),
        in_specs=in_specs,
        out_specs=out_spec,
        out_shape=jax.ShapeDtypeStruct((M, nb * n), out_dtype),
        scratch_shapes=[pltpu.VMEM((tm, tn), F32)] if nk > 1 else [],
        compiler_params=_cp(("parallel", "parallel", "parallel", "arbitrary")),
        name=name,
    )(*args)


def mm_nt(dy2d, w2d, nb, M, K, name, out_dtype=BF16, res=None, dy_maps=None, tm=None, tko=None, tn=None):
    n = w2d.shape[1]
    assert w2d.shape[0] == nb * K or (nb == 1 and w2d.shape[0] > K)
    tm = tm or _pick(M, (1024, 512, 256, 128))
    tko = tko or _pick(K, (1024, 512, 256, 128))
    tn = tn or _pick(n, (1408, 1024, 768, 512, 256, 128))
    nt, nko = n // tn, K // tko
    has_res = res is not None
    if dy_maps is None:
        dy_maps = [lambda i, j, t: (i, j * nt + t)]
    nd = len(dy_maps)
    td = tn // nd

    one_step = nb * nt == 1

    def body(*refs):
        d_refs, w_ref = refs[:nd], refs[nd]
        r_ref = refs[nd + 1] if has_res else None
        d = d_refs[0][...] if nd == 1 else jnp.concatenate([r[...] for r in d_refs], axis=1)
        part = _dot_nt(d, w_ref[...])
        if one_step:
            o_ref = refs[-1]
            if r_ref is not None:
                part = part + r_ref[...].astype(F32)
            o_ref[...] = part.astype(o_ref.dtype)
            return
        o_ref, acc_ref = refs[-2], refs[-1]
        j, t = pl.program_id(2), pl.program_id(3)
        first = jnp.logical_and(j == 0, t == 0)
        last = jnp.logical_and(j == nb - 1, t == nt - 1)

        @pl.when(first)
        def _():
            acc_ref[...] = part

        @pl.when(jnp.logical_not(first))
        def _():
            acc_ref[...] += part

        @pl.when(last)
        def _():
            acc = acc_ref[...]
            if r_ref is not None:
                acc = acc + r_ref[...].astype(F32)
            o_ref[...] = acc.astype(o_ref.dtype)

    in_specs = [pl.BlockSpec((tm, td), functools.partial(lambda f, i, ko, j, t: f(i, j, t), f)) for f in dy_maps]
    in_specs.append(pl.BlockSpec((tko, tn), lambda i, ko, j, t: (j * nko + ko, t)))
    args = [dy2d] * nd + [w2d]
    out_spec = pl.BlockSpec((tm, tko), lambda i, ko, j, t: (i, ko))
    if has_res:
        in_specs.append(out_spec)
        args.append(res)
    return pl.pallas_call(
        body,
        grid=(M // tm, nko, nb, nt),
        in_specs=in_specs,
        out_specs=out_spec,
        out_shape=jax.ShapeDtypeStruct((M, K), out_dtype),
        scratch_shapes=[] if one_step else [pltpu.VMEM((tm, tko), F32)],
        compiler_params=_cp(("parallel", "parallel", "arbitrary", "arbitrary")),
        name=name,
    )(*args)


def mm_tn(x, dy2d, nb, n, name, out_dtype=BF16, dy_maps=None, tko=None, tn=None, x_map=None, x_shape=None):
    S, K = x_shape or x.shape
    x_map = x_map or (lambda ko: (0, ko))
    tko = tko or _pick(K, (512, 256, 128))
    tn = tn or _pick(n, (1408, 1024, 768, 512, 256, 128))
    nt, nko = n // tn, K // tko
    if dy_maps is None:
        dy_maps = [lambda j, t: (0, j * nt + t)]
    nd = len(dy_maps)
    td = tn // nd

    def body(*refs):
        x_ref, d_refs, o_ref = refs[0], refs[1 : 1 + nd], refs[-1]
        d = d_refs[0][...] if nd == 1 else jnp.concatenate([r[...] for r in d_refs], axis=1)
        o_ref[...] = _dot_tn(x_ref[...], d).astype(o_ref.dtype)

    in_specs = [pl.BlockSpec((S, tko), lambda ko, j, t: x_map(ko))]
    in_specs += [pl.BlockSpec((S, td), functools.partial(lambda f, ko, j, t: f(j, t), f)) for f in dy_maps]
    return pl.pallas_call(
        body,
        grid=(nko, nb, nt),
        in_specs=in_specs,
        out_specs=pl.BlockSpec((tko, tn), lambda ko, j, t: (j * nko + ko, t)),
        out_shape=jax.ShapeDtypeStruct((nb * K, n), out_dtype),
        compiler_params=_cp(("parallel", "parallel", "parallel")),
        name=name,
    )(x, *([dy2d] * nd))


def rms_fwd(x, g, name):
    S, D = x.shape
    tm = _pick(S, (512, 256, 128))

    def body(x_ref, g_ref, o_ref):
        xf = x_ref[...]
        r = lax.rsqrt(jnp.mean(xf * xf, axis=-1, keepdims=True) + EPS)
        o_ref[...] = (xf * r * g_ref[...]).astype(o_ref.dtype)

    return pl.pallas_call(
        body,
        grid=(S // tm,),
        in_specs=[pl.BlockSpec((tm, D), lambda i: (i, 0)), pl.BlockSpec((1, D), lambda i: (0, 0))],
        out_specs=pl.BlockSpec((tm, D), lambda i: (i, 0)),
        out_shape=jax.ShapeDtypeStruct((S, D), BF16),
        compiler_params=_cp(("parallel",)),
        name=name,
    )(x, g.reshape(1, D))


def rms_bwd(x, g, dh, dres, name):
    S, D = x.shape
    tm = _pick(S, (256, 128))

    def body(x_ref, g_ref, dh_ref, dr_ref, dx_ref, dxb_ref, dg_ref):
        i = pl.program_id(0)
        xf = x_ref[...]
        dh = dh_ref[...].astype(F32)
        r = lax.rsqrt(jnp.mean(xf * xf, axis=-1, keepdims=True) + EPS)
        gy = dh * g_ref[...]
        proj = jnp.mean(gy * xf, axis=-1, keepdims=True)
        dx = dr_ref[...] + r * gy - xf * (r * r * r * proj)
        dx_ref[...] = dx
        dxb_ref[...] = dx.astype(BF16)
        dg = jnp.sum(dh * (xf * r), axis=0, keepdims=True)

        @pl.when(i == 0)
        def _():
            dg_ref[...] = dg

        @pl.when(i > 0)
        def _():
            dg_ref[...] += dg

    row = pl.BlockSpec((tm, D), lambda i: (i, 0))
    vec = pl.BlockSpec((1, D), lambda i: (0, 0))
    return pl.pallas_call(
        body,
        grid=(S // tm,),
        in_specs=[row, vec, row, row],
        out_specs=[row, row, vec],
        out_shape=[jax.ShapeDtypeStruct((S, D), F32), jax.ShapeDtypeStruct((S, D), BF16), jax.ShapeDtypeStruct((1, D), F32)],
        compiler_params=_cp(("arbitrary",)),
        name=name,
    )(x, g.reshape(1, D), dh, dres)


def loss_head(x, g, target, name):
    S, D = x.shape
    tm = _pick(S, (256, 128))

    def body(x_ref, g_ref, t_ref, dx_ref, dxb_ref, dg_ref, loss_ref):
        i = pl.program_id(0)
        xf = x_ref[...]
        gg = g_ref[...]
        r = lax.rsqrt(jnp.mean(xf * xf, axis=-1, keepdims=True) + EPS)
        xh = xf * r
        err = xh * gg - t_ref[...]
        part = (0.5 / D) * jnp.sum(err * err)
        dy = err * (1.0 / D)
        gy = dy * gg
        proj = jnp.mean(gy * xf, axis=-1, keepdims=True)
        dx = r * gy - xf * (r * r * r * proj)
        dx_ref[...] = dx
        dxb_ref[...] = dx.astype(BF16)
        dg = jnp.sum(dy * xh, axis=0, keepdims=True)
        lossb = jnp.full(loss_ref.shape, part, F32)

        @pl.when(i == 0)
        def _():
            dg_ref[...] = dg
            loss_ref[...] = lossb

        @pl.when(i > 0)
        def _():
            dg_ref[...] += dg
            loss_ref[...] += lossb

    row = pl.BlockSpec((tm, D), lambda i: (i, 0))
    vec = pl.BlockSpec((1, D), lambda i: (0, 0))
    return pl.pallas_call(
        body,
        grid=(S // tm,),
        in_specs=[row, vec, row],
        out_specs=[row, row, vec, pl.BlockSpec((8, 128), lambda i: (0, 0))],
        out_shape=[
            jax.ShapeDtypeStruct((S, D), F32),
            jax.ShapeDtypeStruct((S, D), BF16),
            jax.ShapeDtypeStruct((1, D), F32),
            jax.ShapeDtypeStruct((8, 128), F32),
        ],
        compiler_params=_cp(("arbitrary",)),
        name=name,
    )(x, g.reshape(1, D), target)


def _shift_down(s, k):
    if k == 0:
        return s
    return jnp.where(_iota2(s.shape, 0) >= k, pltpu.roll(s, k, axis=0), 0.0)


def _shift_up(s, k):
    if k == 0:
        return s
    n = s.shape[0]
    return jnp.where(_iota2(s.shape, 0) < n - k, pltpu.roll(s, n - k, axis=0), 0.0)


def _conv(s, w):
    return w[0:1] * _shift_down(s, 2) + w[1:2] * _shift_down(s, 1) + w[2:3] * s


def _conv_t(d, w):
    return w[2:3] * d + w[1:2] * _shift_up(d, 1) + w[0:1] * _shift_up(d, 2)


def _conv_dw(d, s):
    return [jnp.sum(d * _shift_down(s, CONV_K - 1 - k), axis=0, keepdims=True) for k in range(CONV_K)]


def sc_fwd(p, convw, cat, W, name):
    S = p.shape[0]
    tc = _pick(W, (256, 128))
    nc = W // tc

    def body(gb_ref, gc_ref, hi_ref, w_ref, cat_ref, o_ref):
        s = gc_ref[...].astype(F32) * hi_ref[...].astype(F32)
        o_ref[...] = (gb_ref[...].astype(F32) * _conv(s, w_ref[...])).astype(o_ref.dtype)

    col = lambda part: pl.BlockSpec((S, tc), lambda c: (0, part * nc + c))
    return pl.pallas_call(
        body,
        grid=(nc,),
        in_specs=[col(3), col(4), col(5), pl.BlockSpec((CONV_K, tc), lambda c: (0, c)), pl.BlockSpec(memory_space=pl.ANY)],
        out_specs=col(1),
        out_shape=jax.ShapeDtypeStruct(cat.shape, cat.dtype),
        input_output_aliases={4: 0},
        compiler_params=_cp(("parallel",)),
        name=name,
    )(p, p, p, convw, cat)


def sc_bwd(p, convw, dcat, dp, W, name):
    S = p.shape[0]
    tc = _pick(W, (256, 128))
    nc = W // tc

    def body(gb_ref, gc_ref, hi_ref, w_ref, do_ref, dp_in_ref, dp_ref, dw_ref):
        gb = gb_ref[...].astype(F32)
        gc = gc_ref[...].astype(F32)
        hi = hi_ref[...].astype(F32)
        w = w_ref[...]
        do = do_ref[...].astype(F32)
        s = gc * hi
        dcs = do * gb
        ds = _conv_t(dcs, w)
        dp_ref[0] = (do * _conv(s, w)).astype(dp_ref.dtype)
        dp_ref[1] = (ds * hi).astype(dp_ref.dtype)
        dp_ref[2] = (ds * gc).astype(dp_ref.dtype)
        for k, row in enumerate(_conv_dw(dcs, s)):
            dw_ref[k : k + 1, :] = row

    col = lambda part: pl.BlockSpec((S, tc), lambda c: (0, part * nc + c))
    return pl.pallas_call(
        body,
        grid=(nc,),
        in_specs=[
            col(3), col(4), col(5),
            pl.BlockSpec((CONV_K, tc), lambda c: (0, c)),
            pl.BlockSpec((S, tc), lambda c: (0, nc + c)),
            pl.BlockSpec(memory_space=pl.ANY),
        ],
        out_specs=[pl.BlockSpec((3, S, tc), lambda c: (1, 0, c)), pl.BlockSpec((CONV_K, tc), lambda c: (0, c))],
        out_shape=[jax.ShapeDtypeStruct(dp.shape, dp.dtype), jax.ShapeDtypeStruct((CONV_K, W), F32)],
        input_output_aliases={5: 0},
        compiler_params=_cp(("parallel",)),
        name=name,
    )(p, p, p, convw, dcat, dp)


def _silu_parts(a):
    sig = 1.0 / (1.0 + jnp.exp(-a))
    return a * sig, sig


def ffn_act_fwd(u, convw, F, name):
    S = u.shape[0]
    tc = _pick(F, (256, 128))
    nc = F // tc

    def body(ug_ref, uu_ref, wg_ref, wu_ref, o_ref):
        ag = _conv(ug_ref[...].astype(F32), wg_ref[...])
        au = _conv(uu_ref[...].astype(F32), wu_ref[...])
        o_ref[...] = (_silu_parts(ag)[0] * au).astype(o_ref.dtype)

    col = lambda half: pl.BlockSpec((S, tc), lambda c: (0, half * nc + c))
    wcol = lambda half: pl.BlockSpec((CONV_K, tc), lambda c: (0, half * nc + c))
    return pl.pallas_call(
        body,
        grid=(nc,),
        in_specs=[col(0), col(1), wcol(0), wcol(1)],
        out_specs=pl.BlockSpec((S, tc), lambda c: (0, c)),
        out_shape=jax.ShapeDtypeStruct((S, F), BF16),
        compiler_params=_cp(("parallel",)),
        name=name,
    )(u, u, convw, convw)


def ffn_act_bwd(u, convw, dact, F, name):
    S = u.shape[0]
    tc = _pick(F, (256, 128))
    nc = F // tc

    def body(ug_ref, uu_ref, wg_ref, wu_ref, da_ref, du_ref, dw_ref):
        ug = ug_ref[...].astype(F32)
        uu = uu_ref[...].astype(F32)
        wg = wg_ref[...]
        wu = wu_ref[...]
        da = da_ref[...].astype(F32)
        ag = _conv(ug, wg)
        au = _conv(uu, wu)
        sl, sig = _silu_parts(ag)
        dag = da * au * (sig * (1.0 + ag * (1.0 - sig)))
        dau = da * sl
        du_ref[0] = _conv_t(dag, wg).astype(du_ref.dtype)
        du_ref[1] = _conv_t(dau, wu).astype(du_ref.dtype)
        for k, (rg, ru) in enumerate(zip(_conv_dw(dag, ug), _conv_dw(dau, uu))):
            dw_ref[0, k : k + 1, :] = rg
            dw_ref[1, k : k + 1, :] = ru

    col = lambda half: pl.BlockSpec((S, tc), lambda c: (0, half * nc + c))
    wcol = lambda half: pl.BlockSpec((CONV_K, tc), lambda c: (0, half * nc + c))
    return pl.pallas_call(
        body,
        grid=(nc,),
        in_specs=[col(0), col(1), wcol(0), wcol(1), pl.BlockSpec((S, tc), lambda c: (0, c))],
        out_specs=[pl.BlockSpec((2, S, tc), lambda c: (0, 0, c)), pl.BlockSpec((2, CONV_K, tc), lambda c: (0, 0, c))],
        out_shape=[jax.ShapeDtypeStruct((2, S, F), BF16), jax.ShapeDtypeStruct((2, CONV_K, F), F32)],
        compiler_params=_cp(("parallel",)),
        name=name,
    )(u, u, convw, convw, dact)


def _softplus(z):
    return jnp.maximum(z, 0.0) + jnp.log(1.0 + jnp.exp(-jnp.abs(z)))


def _key_strip(S):
    return _pick(S, (512, 256, 128))


def _query_rows(S):
    tq = _pick(S, (512, 256, 128))
    assert _key_strip(S) % tq == 0
    return tq


def _split2(x):
    hi = x.astype(BF16)
    return hi, (x - hi.astype(F32)).astype(BF16)


def _block_sums(x, ones_bf16):
    hi, lo = _split2(x)
    return [
        _dot(hi[:, b * HD : (b + 1) * HD], ones_bf16) + _dot(lo[:, b * HD : (b + 1) * HD], ones_bf16)
        for b in range(x.shape[1] // HD)
    ]


def _strip_mask(shape, row0, off, strict):
    cols, rows = _iota2(shape, 1) + off, _iota2(shape, 0) + row0
    return cols < rows if strict else cols <= rows


def _sb_strip(q, ks, row0, off, run, su, masked):
    z = _dot_nt(q, ks) * (HD ** -0.5)
    sp = _softplus(z)
    mask = _strip_mask(z.shape, row0, off, True) if masked else None
    l = jnp.where(mask, -sp, 0.0) if masked else -sp
    within = _block_sums(l, su)
    later = [None] * len(within)
    for b in reversed(range(len(within))):
        later[b] = within[b] + run
        run = run + jnp.sum(l[:, b * HD : (b + 1) * HD], axis=1, keepdims=True)
    a = jnp.exp(z - sp + jnp.concatenate(later, axis=1))
    return z, (jnp.where(mask, a, 0.0) if masked else a), run


def sb_fwd(p, W, name):
    S = p.shape[0]
    TQ, TK = _query_rows(S), _key_strip(S)
    nh, nq = W // HD, S // TQ

    def body(q_ref, k_ref, v_ref, o_ref):
        i = pl.program_id(1)
        q = q_ref[...]
        su = (_iota2((HD, HD), 0) > _iota2((HD, HD), 1)).astype(BF16)
        last = (i * TQ + TQ - 1) // TK

        def strip(g, carry, masked):
            acc, run = carry
            off = pl.multiple_of(g * TK, TK)
            _, a, run = _sb_strip(q, k_ref[pl.ds(off, TK), :], i * TQ, off, run, su, masked)
            return acc + _dot(a.astype(BF16), v_ref[pl.ds(off, TK), :]), run

        carry = strip(last, (jnp.zeros((TQ, HD), F32), jnp.zeros((TQ, 1), F32)), True)
        acc, _ = lax.fori_loop(0, last, lambda gg, c: strip(last - 1 - gg, c, False), carry)
        o_ref[...] = acc.astype(o_ref.dtype)

    return pl.pallas_call(
        body,
        grid=(nh, nq),
        in_specs=[
            pl.BlockSpec((TQ, HD), lambda h, i: (i, h)),
            pl.BlockSpec((S, HD), lambda h, i: (0, nh + h)),
            pl.BlockSpec((S, HD), lambda h, i: (0, 2 * nh + h)),
        ],
        out_specs=pl.BlockSpec((TQ, HD), lambda h, i: (i, h)),
        out_shape=jax.ShapeDtypeStruct((S, 2 * W), BF16),
        compiler_params=_cp(("parallel", "arbitrary")),
        name=name,
    )(p, p, p)


def sb_bwd(p, dcat, W, name):
    S = p.shape[0]
    TQ, TK = _query_rows(S), _key_strip(S)
    nh, nq = W // HD, S // TQ
    scale = HD ** -0.5

    def body(q_ref, k_ref, v_ref, do_ref, dp_ref, dk_acc, dv_acc, e_scr, z_scr):
        i = pl.program_id(1)
        q = q_ref[...]
        do = do_ref[...]
        su = (_iota2((HD, HD), 0) > _iota2((HD, HD), 1)).astype(BF16)
        sl = (_iota2((HD, HD), 0) < _iota2((HD, HD), 1)).astype(BF16)
        last = (i * TQ + TQ - 1) // TK

        @pl.when(i == 0)
        def _():
            dk_acc[...] = jnp.zeros_like(dk_acc)
            dv_acc[...] = jnp.zeros_like(dv_acc)

        def pass_a(g, run, masked):
            off = pl.multiple_of(g * TK, TK)
            z, a, run = _sb_strip(q, k_ref[pl.ds(off, TK), :], i * TQ, off, run, su, masked)
            e_scr[g] = a * _dot_nt(do, v_ref[pl.ds(off, TK), :])
            z_scr[g] = z
            dv_acc[pl.ds(off, TK), :] += _dot_tn(a.astype(BF16), do)
            return run

        run = pass_a(last, jnp.zeros((TQ, 1), F32), True)
        lax.fori_loop(0, last, lambda gg, r: pass_a(last - 1 - gg, r, False), run)

        def pass_b(g, carry, masked):
            dq, run_e = carry
            off = pl.multiple_of(g * TK, TK)
            e = e_scr[g]
            z = z_scr[g]
            within = _block_sums(e, sl)
            before = []
            for b in range(len(within)):
                before.append(within[b] + run_e)
                run_e = run_e + jnp.sum(e[:, b * HD : (b + 1) * HD], axis=1, keepdims=True)
            sig = 1.0 / (1.0 + jnp.exp(-z))
            dz = e * (1.0 - sig) - jnp.concatenate(before, axis=1) * sig
            if masked:
                dz = jnp.where(_strip_mask(z.shape, i * TQ, off, True), dz, 0.0)
            dz = (dz * scale).astype(BF16)
            dq = dq + _dot(dz, k_ref[pl.ds(off, TK), :])
            dk_acc[pl.ds(off, TK), :] += _dot_tn(dz, q)
            return dq, run_e

        carry = lax.fori_loop(0, last, lambda g, c: pass_b(g, c, False), (jnp.zeros((TQ, HD), F32), jnp.zeros((TQ, 1), F32)))
        dq, _ = pass_b(last, carry, True)
        dp_ref[0, pl.ds(pl.multiple_of(i * TQ, TQ), TQ), :] = dq.astype(dp_ref.dtype)

        @pl.when(i == nq - 1)
        def _():
            dp_ref[1] = dk_acc[...].astype(dp_ref.dtype)
            dp_ref[2] = dv_acc[...].astype(dp_ref.dtype)

    return pl.pallas_call(
        body,
        grid=(nh, nq),
        in_specs=[
            pl.BlockSpec((TQ, HD), lambda h, i: (i, h)),
            pl.BlockSpec((S, HD), lambda h, i: (0, nh + h)),
            pl.BlockSpec((S, HD), lambda h, i: (0, 2 * nh + h)),
            pl.BlockSpec((TQ, HD), lambda h, i: (i, h)),
        ],
        out_specs=pl.BlockSpec((3, S, HD), lambda h, i: (0, 0, h)),
        out_shape=jax.ShapeDtypeStruct((6, S, W), BF16),
        scratch_shapes=[
            pltpu.VMEM((S, HD), F32),
            pltpu.VMEM((S, HD), F32),
            pltpu.VMEM((S // TK, TQ, TK), F32),
            pltpu.VMEM((S // TK, TQ, TK), F32),
        ],
        compiler_params=_cp(("parallel", "arbitrary")),
        name=name,
    )(p, p, p, dcat)


def fox_gate_fwd(f, b, name):
    S = f.shape[0]
    nq = S // HD

    def body(f_ref, b_ref, c_ref, run):
        i = pl.program_id(0)

        @pl.when(i == 0)
        def _():
            run[...] = jnp.zeros_like(run)

        lf = -_softplus(-(f_ref[...] + b_ref[...]))
        tri = (_iota2((HD, HD), 0) >= _iota2((HD, HD), 1)).astype(BF16)
        c_ref[...] = _dot_ones_left(tri, lf) + run[...]
        run[...] += jnp.sum(lf, axis=0, keepdims=True)

    return pl.pallas_call(
        body,
        grid=(nq,),
        in_specs=[pl.BlockSpec((HD, 128), lambda i: (i, 0)), pl.BlockSpec((1, 128), lambda i: (0, 0))],
        out_specs=pl.BlockSpec((HD, 128), lambda i: (i, 0)),
        out_shape=jax.ShapeDtypeStruct((S, 128), F32),
        scratch_shapes=[pltpu.VMEM((1, 128), F32)],
        compiler_params=_cp(("arbitrary",)),
        name=name,
    )(f, b)


def fox_gate_bwd(f, b, dc, name):
    S = f.shape[0]
    nq = S // HD

    def body(f_ref, b_ref, dc_ref, df_ref, db_ref, run):
        i = pl.program_id(0)

        @pl.when(i == 0)
        def _():
            run[...] = jnp.zeros_like(run)

        dc = dc_ref[...]
        tri = (_iota2((HD, HD), 0) <= _iota2((HD, HD), 1)).astype(BF16)
        dlf = _dot_ones_left(tri, dc) + run[...]
        run[...] += jnp.sum(dc, axis=0, keepdims=True)
        x = f_ref[...] + b_ref[...]
        df = dlf * (1.0 / (1.0 + jnp.exp(x)))
        df_ref[...] = df
        db = jnp.sum(df, axis=0, keepdims=True)

        @pl.when(i == 0)
        def _():
            db_ref[...] = db

        @pl.when(i > 0)
        def _():
            db_ref[...] += db

    rev = pl.BlockSpec((HD, 128), lambda i: (nq - 1 - i, 0))
    vec = pl.BlockSpec((1, 128), lambda i: (0, 0))
    return pl.pallas_call(
        body,
        grid=(nq,),
        in_specs=[rev, vec, rev],
        out_specs=[rev, vec],
        out_shape=[jax.ShapeDtypeStruct((S, 128), F32), jax.ShapeDtypeStruct((1, 128), F32)],
        scratch_shapes=[pltpu.VMEM((1, 128), F32)],
        compiler_params=_cp(("arbitrary",)),
        name=name,
    )(f, b, dc)


def _fox_logits(q, ks, ct, cs, row0, off, masked):
    s = _dot_nt(q, ks) * (HD ** -0.5) + (ct - cs)
    if not masked:
        return s, None
    mask = _strip_mask(s.shape, row0, off, False)
    return jnp.where(mask, s, -1e30), mask


def fox_fwd(p, ccol, crow, cat, W, name):
    S = p.shape[0]
    TQ, TK = _query_rows(S), _key_strip(S)
    nh, nq = W // HD, S // TQ

    def body(q_ref, k_ref, v_ref, cc_ref, cr_ref, cat_ref, o_ref, lse_ref):
        i = pl.program_id(1)
        q = q_ref[...]
        ct = cc_ref[0]

        def step(g, carry, masked):
            m, l, acc = carry
            off = pl.multiple_of(g * TK, TK)
            s, _ = _fox_logits(q, k_ref[pl.ds(off, TK), :], ct, cr_ref[0, pl.ds(g, 1), :], i * TQ, off, masked)
            m_new = jnp.maximum(m, jnp.max(s, axis=1, keepdims=True))
            alpha = jnp.exp(m - m_new)
            pr = jnp.exp(s - m_new)
            l = alpha * l + jnp.sum(pr, axis=1, keepdims=True)
            acc = alpha * acc + _dot(pr.astype(BF16), v_ref[pl.ds(off, TK), :])
            return m_new, l, acc

        init = (jnp.full((TQ, 1), -1e30, F32), jnp.zeros((TQ, 1), F32), jnp.zeros((TQ, HD), F32))
        last = (i * TQ + TQ - 1) // TK
        m, l, acc = step(last, lax.fori_loop(0, last, lambda g, c: step(g, c, False), init), True)
        o_ref[...] = (acc / l).astype(o_ref.dtype)
        lse_ref[0] = m + jnp.log(l)

    return pl.pallas_call(
        body,
        grid=(nh, nq),
        in_specs=[
            pl.BlockSpec((TQ, HD), lambda h, i: (i, 2 * nh + h)),
            pl.BlockSpec((S, HD), lambda h, i: (0, 3 * nh + h)),
            pl.BlockSpec((S, HD), lambda h, i: (0, 4 * nh + h)),
            pl.BlockSpec((1, TQ, 1), lambda h, i: (h, i, 0)),
            pl.BlockSpec((1, S // TK, TK), lambda h, i: (h, 0, 0)),
            pl.BlockSpec(memory_space=pl.ANY),
        ],
        out_specs=[pl.BlockSpec((TQ, HD), lambda h, i: (i, nh + h)), pl.BlockSpec((1, TQ, 1), lambda h, i: (h, i, 0))],
        out_shape=[jax.ShapeDtypeStruct(cat.shape, cat.dtype), jax.ShapeDtypeStruct((nh, S, 1), F32)],
        input_output_aliases={5: 0},
        compiler_params=_cp(("parallel", "arbitrary")),
        name=name,
    )(p, p, p, ccol, crow, cat)


def fox_bwd(p, ccol, crow, cat, lse, dcat, dp, W, name):
    S = p.shape[0]
    TQ, TK = _query_rows(S), _key_strip(S)
    nh, nq = W // HD, S // TQ
    scale = HD ** -0.5

    def body(q_ref, k_ref, v_ref, cc_ref, cr_ref, o_ref, lse_ref, do_ref, dp_in_ref, dp_ref, dcs_ref, dct_ref, dk_acc, dv_acc):
        i = pl.program_id(1)
        q = q_ref[...]
        do = do_ref[...]
        ct = cc_ref[0]
        lse_i = lse_ref[0]
        delta = jnp.sum(do.astype(F32) * o_ref[...].astype(F32), axis=1, keepdims=True)

        @pl.when(i == 0)
        def _():
            dk_acc[...] = jnp.zeros_like(dk_acc)
            dv_acc[...] = jnp.zeros_like(dv_acc)
            dcs_ref[...] = jnp.zeros_like(dcs_ref)

        def step(g, carry, masked):
            dq, dct = carry
            off = pl.multiple_of(g * TK, TK)
            ks = k_ref[pl.ds(off, TK), :]
            s, mask = _fox_logits(q, ks, ct, cr_ref[0, pl.ds(g, 1), :], i * TQ, off, masked)
            pr = jnp.where(mask, jnp.exp(s - lse_i), 0.0) if masked else jnp.exp(s - lse_i)
            ds = pr * (_dot_nt(do, v_ref[pl.ds(off, TK), :]) - delta)
            dv_acc[pl.ds(off, TK), :] += _dot_tn(pr.astype(BF16), do)
            dsb = (ds * scale).astype(BF16)
            dk_acc[pl.ds(off, TK), :] += _dot_tn(dsb, q)
            dcs_ref[0, pl.ds(g, 1), :] += jnp.sum(ds, axis=0, keepdims=True)
            return dq + _dot(dsb, ks), dct + jnp.sum(ds, axis=1, keepdims=True)

        last = (i * TQ + TQ - 1) // TK
        carry = lax.fori_loop(0, last, lambda g, c: step(g, c, False), (jnp.zeros((TQ, HD), F32), jnp.zeros((TQ, 1), F32)))
        dq, dct = step(last, carry, True)
        dp_ref[0, pl.ds(pl.multiple_of(i * TQ, TQ), TQ), :] = dq.astype(dp_ref.dtype)
        dct_ref[0] = dct

        @pl.when(i == nq - 1)
        def _():
            dp_ref[1] = dk_acc[...].astype(dp_ref.dtype)
            dp_ref[2] = dv_acc[...].astype(dp_ref.dtype)

    return pl.pallas_call(
        body,
        grid=(nh, nq),
        in_specs=[
            pl.BlockSpec((TQ, HD), lambda h, i: (i, 2 * nh + h)),
            pl.BlockSpec((S, HD), lambda h, i: (0, 3 * nh + h)),
            pl.BlockSpec((S, HD), lambda h, i: (0, 4 * nh + h)),
            pl.BlockSpec((1, TQ, 1), lambda h, i: (h, i, 0)),
            pl.BlockSpec((1, S // TK, TK), lambda h, i: (h, 0, 0)),
            pl.BlockSpec((TQ, HD), lambda h, i: (i, nh + h)),
            pl.BlockSpec((1, TQ, 1), lambda h, i: (h, i, 0)),
            pl.BlockSpec((TQ, HD), lambda h, i: (i, nh + h)),
            pl.BlockSpec(memory_space=pl.ANY),
        ],
        out_specs=[
            pl.BlockSpec((3, S, HD), lambda h, i: (1, 0, h)),
            pl.BlockSpec((1, S // TK, TK), lambda h, i: (h, 0, 0)),
            pl.BlockSpec((1, TQ, 1), lambda h, i: (h, i, 0)),
        ],
        out_shape=[
            jax.ShapeDtypeStruct(dp.shape, dp.dtype),
            jax.ShapeDtypeStruct((nh, S // TK, TK), F32),
            jax.ShapeDtypeStruct((nh, S, 1), F32),
        ],
        input_output_aliases={8: 0},
        scratch_shapes=[pltpu.VMEM((S, HD), F32), pltpu.VMEM((S, HD), F32)],
        compiler_params=_cp(("parallel", "arbitrary")),
        name=name,
    )(p, p, p, ccol, crow, cat, lse, dcat, dp)


_GELU_K = math.sqrt(2.0 / math.pi)
_GELU_C = 0.044715


def _gelu(x):
    return 0.5 * x * (1.0 + jnp.tanh(_GELU_K * (x + _GELU_C * x * x * x)))


def _gelu_grad(x):
    t = jnp.tanh(_GELU_K * (x + _GELU_C * x * x * x))
    return 0.5 * (1.0 + t) + 0.5 * x * (1.0 - t * t) * (_GELU_K * (1.0 + 3.0 * _GELU_C * x * x))


def _layernorm_parts(gv):
    xc = gv - jnp.mean(gv, axis=-1, keepdims=True)
    r = lax.rsqrt(jnp.mean(xc * xc, axis=-1, keepdims=True) + EPS)
    return xc * r, r


def sg_fwd(p, sg_w, sg_bt, sg_g, W, name):
    S = p.shape[0]
    G, nq = W // HD, S // HD

    def body(u_ref, v_ref, w_ref, bt_ref, g_ref, o_ref):
        xh, _ = _layernorm_parts(_gelu(v_ref[...].astype(F32)))
        vn = (xh * g_ref[...]).astype(BF16)
        tri = _iota2((HD, HD), 0) >= _iota2((HD, HD), 1)
        for gi in range(G):
            cols = slice(gi * HD, (gi + 1) * HD)
            wt = jnp.where(tri, w_ref[gi], 0.0).astype(BF16)
            mixed = _dot(wt, vn[:, cols]) + bt_ref[:, gi : gi + 1]
            o_ref[:, cols] = (_gelu(u_ref[:, cols].astype(F32)) * mixed).astype(o_ref.dtype)

    return pl.pallas_call(
        body,
        grid=(nq,),
        in_specs=[
            pl.BlockSpec((HD, W), lambda i: (i, 0)),
            pl.BlockSpec((HD, W), lambda i: (i, 1)),
            pl.BlockSpec((G, HD, HD), lambda i: (0, 0, 0)),
            pl.BlockSpec((HD, G), lambda i: (0, 0)),
            pl.BlockSpec((1, W), lambda i: (0, 0)),
        ],
        out_specs=pl.BlockSpec((HD, W), lambda i: (i, 0)),
        out_shape=jax.ShapeDtypeStruct((S, 2 * W), BF16),
        compiler_params=_cp(("parallel",)),
        name=name,
    )(p, p, sg_w, sg_bt, sg_g.reshape(1, W))


def sg_bwd(p, sg_w, sg_bt, sg_g, dcat, W, name):
    S = p.shape[0]
    G, nq = W // HD, S // HD

    def body(u_ref, v_ref, w_ref, bt_ref, g_ref, do_ref, dp_ref, dw_ref, dbt_ref, dg_ref, dvn_scr):
        i = pl.program_id(0)

        @pl.when(i == 0)
        def _():
            dw_ref[...] = jnp.zeros_like(dw_ref)
            dbt_ref[...] = jnp.zeros_like(dbt_ref)
            dg_ref[...] = jnp.zeros_like(dg_ref)

        v = v_ref[...].astype(F32)
        xh, r = _layernorm_parts(_gelu(v))
        gg = g_ref[...]
        vn = (xh * gg).astype(BF16)
        tri = _iota2((HD, HD), 0) >= _iota2((HD, HD), 1)
        for gi in range(G):
            cols = slice(gi * HD, (gi + 1) * HD)
            wt = jnp.where(tri, w_ref[gi], 0.0).astype(BF16)
            mixed = _dot(wt, vn[:, cols]) + bt_ref[:, gi : gi + 1]
            u = u_ref[:, cols].astype(F32)
            do = do_ref[:, cols].astype(F32)
            dp_ref[0, :, cols] = (do * mixed * _gelu_grad(u)).astype(dp_ref.dtype)
            dmix = do * _gelu(u)
            dmb = dmix.astype(BF16)
            dw_ref[gi] += jnp.where(tri, _dot_nt(dmb, vn[:, cols]), 0.0)
            dbt_ref[:, gi : gi + 1] += jnp.sum(dmix, axis=1, keepdims=True)
            dvn_scr[:, cols] = _dot_tn(wt, dmb)
        dvn = dvn_scr[...]
        dg_ref[...] += jnp.sum(dvn * xh, axis=0, keepdims=True)
        dxh = dvn * gg
        dgv = r * (dxh - jnp.mean(dxh, axis=-1, keepdims=True) - xh * jnp.mean(dxh * xh, axis=-1, keepdims=True))
        dp_ref[1] = (dgv * _gelu_grad(v)).astype(dp_ref.dtype)

    return pl.pallas_call(
        body,
        grid=(nq,),
        in_specs=[
            pl.BlockSpec((HD, W), lambda i: (i, 0)),
            pl.BlockSpec((HD, W), lambda i: (i, 1)),
            pl.BlockSpec((G, HD, HD), lambda i: (0, 0, 0)),
            pl.BlockSpec((HD, G), lambda i: (0, 0)),
            pl.BlockSpec((1, W), lambda i: (0, 0)),
            pl.BlockSpec((HD, W), lambda i: (i, 0)),
        ],
        out_specs=[
            pl.BlockSpec((2, HD, W), lambda i: (0, i, 0)),
            pl.BlockSpec((G, HD, HD), lambda i: (0, 0, 0)),
            pl.BlockSpec((HD, G), lambda i: (0, 0)),
            pl.BlockSpec((1, W), lambda i: (0, 0)),
        ],
        out_shape=[
            jax.ShapeDtypeStruct((6, S, W), BF16),
            jax.ShapeDtypeStruct((G, HD, HD), F32),
            jax.ShapeDtypeStruct((HD, G), F32),
            jax.ShapeDtypeStruct((1, W), F32),
        ],
        scratch_shapes=[pltpu.VMEM((HD, W), F32)],
        compiler_params=_cp(("arbitrary",)),
        name=name,
    )(p, p, sg_w, sg_bt, sg_g.reshape(1, W), dcat)


def local_step(x, target, wts, at, on_grad):
    S, D = x.shape
    W = D // 2
    nb, F = wts["nb"], wts["F"]
    g = {}

    def ffn_fwd(xin, l):
        h = rms_fwd(xin, wts[f"{l}_ffn_norm_g"], f"{l}_ffn_rms")
        u = mm_nn(h, wts[f"{l}_ffn_up"], nb, f"{l}_ffn_up_mm")
        act = ffn_act_fwd(u, wts[f"{l}_ffn_conv_w"], F, f"{l}_ffn_act")
        half_tile = _pick(S, (512, 256, 128))
        xout = mm_nn(act, wts[f"{l}_ffn_down"], 1, f"{l}_ffn_down_mm", out_dtype=F32, res=xin,
                     tm=half_tile, tn=_pick(D, (512, 256, 128)), tk=F)
        return xout, (xin, h, u, act)

    def ffn_bwd(dxout, dxoutb, saved, l):
        xin, h, u, act = saved
        dact = mm_nt(dxoutb, wts[f"{l}_ffn_down"], 1, S, F, f"{l}_ffn_down_dx", tko=_pick(F, (512, 256, 128)), tn=D)
        dact = on_grad(f"{l}_ffn_down", mm_tn(act, dxoutb, 1, D, f"{l}_ffn_down_dw", tn=D), dact)
        du, dcw = ffn_act_bwd(u, wts[f"{l}_ffn_conv_w"], dact, F, f"{l}_ffn_act_bwd")
        g[f"{l}_ffn_conv_w"] = jnp.concatenate([dcw[0], dcw[1]], axis=1)
        du2 = du.reshape(2 * S, F)
        n = wts[f"{l}_ffn_up"].shape[1]
        tn = _pick(n, (1408, 1024, 768, 512, 256, 128))
        per_half = F // tn
        nt = n // tn

        def up_block(i, j, t):
            vb = j * nt + t
            return vb // per_half, vb % per_half

        tm = _pick(S, (1024, 512, 256, 128))

        def nt_map(i, j, t):
            half, cb = up_block(i, j, t)
            return (half * (S // tm) + i, cb)

        def tn_map(j, t):
            half, cb = up_block(0, j, t)
            return (half, cb)

        dh = mm_nt(du2, wts[f"{l}_ffn_up"], nb, S, D, f"{l}_ffn_up_dx", dy_maps=[nt_map], tm=tm, tko=D, tn=tn)
        dh = on_grad(f"{l}_ffn_up", mm_tn(h, du2, nb, n, f"{l}_ffn_up_dw", dy_maps=[tn_map], tko=_pick(D, (1024, 512, 256, 128)), tn=tn), dh)
        dxin, dxinb, dg = rms_bwd(xin, wts[f"{l}_ffn_norm_g"], dh, dxout, f"{l}_ffn_rms_bwd")
        g[f"{l}_ffn_norm_g"] = dg
        return dxin, dxinb

    h0 = rms_fwd(x, wts["l0_mix_norm_g"], "l0_mix_rms")
    p0 = mm_nn(h0, wts["l0_w_in"], nb, "l0_w_in_mm")
    cat0 = sb_fwd(p0, W, "l0_sb_fwd")
    cat0 = sc_fwd(p0, wts["l0_sc_conv_w"], cat0, W, "l0_sc_fwd")
    x1 = mm_nn(cat0, wts["l0_w_out"], 1, "l0_w_out_mm", out_dtype=F32, res=x, tm=S, tn=_pick(D, (512, 256, 128)))
    x2, ffn0_saved = ffn_fwd(x1, "l0")

    x2 = at("l1_w_in", x2, None)
    nh = W // HD
    h2 = rms_fwd(x2, wts["l1_mix_norm_g"], "l1_mix_rms")
    p1 = mm_nt(h2, wts["l1_w_in_t"], 1, S, 5 * W, "l1_w_in_mm", tn=D)
    f = mm_nt(h2, wts["l1_w_f_t"], 1, S, 128, "l1_w_f_mm", out_dtype=F32, tn=D)
    bf = jnp.zeros((1, 128), F32).at[0, :nh].set(wts["l1_fox_b_f"])
    c = fox_gate_fwd(f, bf, "l1_fox_gate")
    c_heads = c[:, :nh].T
    ccol = c_heads[:, :, None]
    crow = c_heads.reshape(nh, S // _key_strip(S), _key_strip(S))
    sg_bt = wts["l1_sg_b"].T
    cat1 = sg_fwd(p1, wts["l1_sg_w"], sg_bt, wts["l1_sg_norm_g"], W, "l1_sg_fwd")
    cat1, lse = fox_fwd(p1, ccol, crow, cat1, W, "l1_fox_fwd")
    x3 = mm_nn(cat1, wts["l1_w_out"], 1, "l1_w_out_mm", out_dtype=F32, res=x2, tm=S, tn=_pick(D, (512, 256, 128)))
    x4, ffn1_saved = ffn_fwd(x3, "l1")

    dx4, dx4b, dgf, loss = loss_head(x4, wts["final_norm_g"], target, "loss_head")
    dx4b = at("loss", dx4b, loss)
    g["final_norm_g"] = dgf

    dx3, dx3b = ffn_bwd(dx4, dx4b, ffn1_saved, "l1")
    dcat1 = mm_nt(dx3b, wts["l1_w_out"], 1, S, D, "l1_w_out_dx", tn=D)
    dcat1 = on_grad("l1_w_out", mm_tn(cat1, dx3b, 1, D, "l1_w_out_dw", tn=D), dcat1)
    dp1, dsgw, dsgbt, dsgg = sg_bwd(p1, wts["l1_sg_w"], sg_bt, wts["l1_sg_norm_g"], dcat1, W, "l1_sg_bwd")
    dp1, dcs, dct = fox_bwd(p1, ccol, crow, cat1, lse, dcat1, dp1, W, "l1_fox_bwd")
    g["l1_sg_w"], g["l1_sg_b"], g["l1_sg_norm_g"] = dsgw, dsgbt.T, dsgg
    dc = jnp.zeros((S, 128), F32).at[:, :nh].set((dct[:, :, 0] - dcs.reshape(nh, S)).T)
    df, dbf = fox_gate_bwd(f, bf, dc, "l1_fox_gate_bwd")
    g["l1_fox_b_f"] = dbf[0, :nh]
    dfb = df.astype(BF16)
    tk1 = _pick(W, (1024, 512, 256, 128))
    tx1 = _pick(W, (512, 256, 128))
    tm1 = _pick(S, (1024, 512, 256, 128))
    part_of = lambda pt: pt + pt // 2 - pt // 4

    def a_map1(i, k):
        return (part_of(k // (W // tk1)) * (S // tm1) + i, k % (W // tk1))

    def x_map1(ko):
        return (part_of(ko // (W // tx1)), ko % (W // tx1))

    dp1_2d = dp1.reshape(6 * S, W)
    dw_main = mm_tn(dp1_2d, h2, 1, D, "l1_w_in_dw", tko=tx1, tn=D, x_map=x_map1, x_shape=(S, 5 * W))
    dw_f = mm_tn(dfb, h2, 1, D, "l1_w_f_dw", tn=D)
    dh2 = mm_nn(dfb, wts["l1_w_f_t"], 1, "l1_w_f_dx", out_dtype=F32)
    dh2 = mm_nn(dp1_2d, wts["l1_w_in_t"], 1, "l1_w_in_dx", res=dh2, tm=tm1, tk=tk1, a_map=a_map1, a_shape=(S, 5 * W))
    dh2 = on_grad("l1_w_in", jnp.concatenate([dw_main, dw_f[:nh]], axis=0), dh2)
    dx2, dx2b, dg = rms_bwd(x2, wts["l1_mix_norm_g"], dh2, dx3, "l1_mix_rms_bwd")
    g["l1_mix_norm_g"] = dg

    dx1, dx1b = ffn_bwd(dx2, dx2b, ffn0_saved, "l0")
    dcat0 = mm_nt(dx1b, wts["l0_w_out"], 1, S, D, "l0_w_out_dx", tn=D)
    dcat0 = on_grad("l0_w_out", mm_tn(cat0, dx1b, 1, D, "l0_w_out_dw", tn=D), dcat0)
    dp0 = sb_bwd(p0, dcat0, W, "l0_sb_bwd")
    dp0, dscw = sc_bwd(p0, wts["l0_sc_conv_w"], dcat0, dp0, W, "l0_sc_bwd")
    g["l0_sc_conv_w"] = dscw
    dp0 = at("small_ready", dp0, g)
    n0 = wts["l0_w_in"].shape[1]
    td0 = math.gcd(n0, W)
    nd0 = n0 // td0
    tm0 = _pick(S, (1024, 512, 256, 128))
    per_part0 = W // td0

    def nt_maps0(k):
        def f(i, j, t):
            vb = j * nd0 + k
            return ((vb // per_part0) * (S // tm0) + i, vb % per_part0)
        return f

    def tn_maps0(k):
        def f(j, t):
            vb = j * nd0 + k
            return (vb // per_part0, vb % per_part0)
        return f

    dp0_2d = dp0.reshape(6 * S, W)
    dw0 = mm_tn(h0, dp0_2d, nb, n0, "l0_w_in_dw", dy_maps=[tn_maps0(k) for k in range(nd0)], tko=_pick(D, (1024, 512, 256, 128)), tn=n0)
    dp0_2d = on_grad("l0_w_in", dw0, dp0_2d)
    dp0_2d = on_grad(None, None, dp0_2d)
    dh0 = mm_nt(dp0_2d, wts["l0_w_in"], nb, S, D, "l0_w_in_dx", dy_maps=[nt_maps0(k) for k in range(nd0)], tm=tm0, tko=D, tn=n0)
    dh0 = at("small_done", dh0, None)
    dx0, _, dg = rms_bwd(x, wts["l0_mix_norm_g"], dh0, dx1, "l0_mix_rms_bwd")
    g["l0_mix_norm_g"] = dg
    return dx0, g


GATHER_ID = 1


def _place():
    return lax.axis_index("x"), lax.axis_index("y"), lax.axis_index("c")


def _other_chips(x, y):
    return [(x, 1 - y), (1 - x, y), (1 - x, 1 - y)]


def _handshake(peers):
    barrier = pltpu.get_barrier_semaphore()
    for peer in peers:
        pl.semaphore_signal(barrier, inc=1, device_id=peer, device_id_type=MESH)
    pl.semaphore_wait(barrier, len(peers))


UPDATE_LAG = 2


def _on_sequencer(body, out_type, scratch_types, collective_id, name):
    return pl.kernel(
        body,
        out_type=out_type,
        mesh=plsc.ScalarSubcoreMesh(axis_name="seq", num_cores=1),
        scratch_types=scratch_types,
        compiler_params=pltpu.CompilerParams(collective_id=collective_id),
        name=name,
    )


def all_gather(arrs, name):
    n = len(arrs)

    def body(*refs):
        xs, outs = refs[:n], refs[n : 2 * n]
        send_sems, recv_sems, local_sems = refs[2 * n :]
        x, y, c = _place()
        me, sibling = (x, y, c), (x, y, 1 - c)
        chips = _other_chips(x, y)
        _handshake([sibling] + [(*chip, c) for chip in chips])

        def copy(a, k, block, to, src=None):
            px, py, pc = block
            dst = outs[a].at[4 * px + 2 * py + pc]
            return pltpu.make_async_remote_copy(
                src_ref=dst if src is None else src, dst_ref=dst,
                send_sem=send_sems.at[7 * a + k], recv_sem=recv_sems.at[7 * a + k], device_id=to, device_id_type=MESH,
            )

        mine = [pltpu.make_async_copy(xs[a], outs[a].at[4 * x + 2 * y + c], local_sems.at[a]) for a in range(n)]
        for cp in mine:
            cp.start()
        first = []
        for a in range(n):
            first.append(copy(a, 0, me, sibling, src=xs[a]))
            first += [copy(a, 1 + j, me, (*chip, c), src=xs[a]) for j, chip in enumerate(chips)]
        for cp in first:
            cp.start()
        passed = []
        for a in range(n):
            for j, chip in enumerate(chips):
                copy(a, 1 + j, (*chip, c), me).wait_recv()
                cp = copy(a, 4 + j, (*chip, c), sibling)
                cp.start()
                passed.append(cp)
        for a in range(n):
            copy(a, 0, sibling, me).wait_recv()
            for j, chip in enumerate(chips):
                copy(a, 4 + j, (*chip, 1 - c), me).wait_recv()
        for cp in first + passed:
            cp.wait_send()
        for cp in mine:
            cp.wait()

    out_type = [jax.ShapeDtypeStruct((NDEV,) + a.shape, a.dtype) for a in arrs]
    sems = [pltpu.SemaphoreType.DMA((7 * n,)), pltpu.SemaphoreType.DMA((7 * n,)), pltpu.SemaphoreType.DMA((n,))]
    return _on_sequencer(body, out_type, sems, GATHER_ID, name)(*arrs)


_IN_HBM = pl.BlockSpec(memory_space=pltpu.HBM)
_IN_SEM = pl.BlockSpec(memory_space=pltpu.SEMAPHORE)
_EFFECT = pltpu.SideEffectType.DATAFLOW_SIDE_EFFECTING


def _split_start(make_copies, src, land_shape, nsem, name):
    def body(src_ref, land_ref, send_sems, recv_sems, land_thru, token):
        for cp in make_copies(src_ref, land_ref, send_sems, recv_sems):
            cp.start()
        token[...] = jnp.zeros_like(token)

    send_sems, recv_sems, land_thru, token = pl.pallas_call(
        body,
        name=name,
        out_shape=(
            pltpu.SemaphoreType.DMA((nsem,)), pltpu.SemaphoreType.DMA((nsem,)),
            pltpu.HBM(land_shape, src.dtype), jax.ShapeDtypeStruct((8, 128), F32),
        ),
        in_specs=(_IN_HBM, _IN_HBM),
        out_specs=(_IN_SEM, _IN_SEM, _IN_HBM, pl.BlockSpec(memory_space=pltpu.VMEM)),
        input_output_aliases={1: 2},
        compiler_params=pltpu.CompilerParams(has_side_effects=_EFFECT),
    )(src, pltpu.with_memory_space_constraint(lax.empty(land_shape, src.dtype), pltpu.HBM))
    return send_sems, recv_sems, src, land_thru, token


def _split_wait(make_copies, send_sems, recv_sems, src_thru, land_thru, after, name):
    def body(src_ref, land_ref, send_sems, recv_sems, after_ref, land_out):
        for cp in make_copies(src_ref, land_ref, send_sems, recv_sems):
            cp.wait_send()
            cp.wait_recv()

    return pl.pallas_call(
        body,
        name=name,
        out_shape=pltpu.HBM(land_thru.shape, land_thru.dtype),
        in_specs=(_IN_HBM, _IN_HBM, _IN_SEM, _IN_SEM, pl.BlockSpec(memory_space=pl.ANY)),
        out_specs=_IN_HBM,
        input_output_aliases={1: 0},
        compiler_params=pltpu.CompilerParams(has_side_effects=_EFFECT),
    )(src_thru, land_thru, send_sems, recv_sems, after)


def _pair_copies(src_ref, land_ref, send_sems, recv_sems):
    x, y, c = _place()
    return [
        pltpu.make_async_remote_copy(
            src_ref=src_ref.at[k, 1 - c], dst_ref=land_ref.at[k],
            send_sem=send_sems.at[k], recv_sem=recv_sems.at[k], device_id=(x, y, 1 - c), device_id_type=MESH,
        )
        for k in range(4)
    ]


def _direct_copies(src_ref, land_ref, send_sems, recv_sems):
    x, y, c = _place()
    me = 4 * x + 2 * y + c
    copies = []
    for k in range(NDEV - 1):
        to = (me + k + 1) % NDEV
        copies.append(pltpu.make_async_remote_copy(
            src_ref=src_ref, dst_ref=land_ref.at[me], send_sem=send_sems.at[k], recv_sem=recv_sems.at[k],
            device_id=(to // 4, (to // 2) % 2, to % 2), device_id_type=MESH,
        ))
    return copies


def _chip_copies(src_ref, land_ref, send_sems, recv_sems):
    x, y, c = _place()
    return [
        pltpu.make_async_remote_copy(
            src_ref=src_ref.at[2 * px + py], dst_ref=land_ref.at[2 * x + y],
            send_sem=send_sems.at[j], recv_sem=recv_sems.at[j], device_id=(px, py, c), device_id_type=MESH,
        )
        for j, (px, py) in enumerate(_other_chips(x, y))
    ]


def _row_tile(R, C, max_elems):
    if R * C <= max_elems:
        return R
    best = None
    for tr in range(16, R, 16):
        if R % tr == 0 and tr * C <= max_elems:
            best = tr
    return best or R


def pair_sum(a42, land4, core, name):
    _, _, R, C = a42.shape
    tr = _row_tile(R, C, 1 << 20)

    def body(core_ref, a_ref, l_ref, o_ref):
        o_ref[...] = (a_ref[0].astype(F32) + l_ref[...].astype(F32)).astype(o_ref.dtype)

    return pl.pallas_call(
        body,
        grid_spec=pltpu.PrefetchScalarGridSpec(
            num_scalar_prefetch=1,
            grid=(4, R // tr),
            in_specs=[
                pl.BlockSpec((1, 1, tr, C), lambda k, r, core_ref: (k, core_ref[0], r, 0)),
                pl.BlockSpec((1, tr, C), lambda k, r, core_ref: (k, r, 0)),
            ],
            out_specs=pl.BlockSpec((1, tr, C), lambda k, r, core_ref: (k, r, 0)),
        ),
        out_shape=jax.ShapeDtypeStruct((4, R, C), BF16),
        compiler_params=_cp(("parallel", "parallel")),
        name=name,
    )(core, a42, land4)


def sum_slots(parts, name):
    P, R, C = parts.shape

    def body(p_ref, o_ref):
        acc = p_ref[0].astype(F32)
        for k in range(1, P):
            acc = acc + p_ref[k].astype(F32)
        o_ref[...] = acc

    tr = _row_tile(R, P * C, 1 << 21)
    return pl.pallas_call(
        body,
        grid=(R // tr,),
        in_specs=[pl.BlockSpec((P, tr, C), lambda r: (0, r, 0))],
        out_specs=pl.BlockSpec((tr, C), lambda r: (r, 0)),
        out_shape=jax.ShapeDtypeStruct((R, C), F32),
        compiler_params=_cp(("parallel",)),
        name=name,
    )(parts)


def adamw(w, m, v, parts, name):
    R, C = w.shape
    P = parts.shape[0]
    tr = _pick(R, (256, 128, 64, 32, 16, 8))
    c1 = 1.0 - ADAM_B1 ** ADAM_STEP
    c2 = 1.0 - ADAM_B2 ** ADAM_STEP

    def body(w_ref, m_ref, v_ref, p_ref, g_ref, d_ref, nm_ref, nv_ref):
        g = p_ref[0].astype(F32)
        for k in range(1, P):
            g = g + p_ref[k].astype(F32)
        nm = ADAM_B1 * m_ref[...] + (1.0 - ADAM_B1) * g
        nv = ADAM_B2 * v_ref[...] + (1.0 - ADAM_B2) * (g * g)
        g_ref[...] = g
        nm_ref[...] = nm
        nv_ref[...] = nv
        d_ref[...] = -ADAM_LR * ((nm / c1) / (jnp.sqrt(nv / c2) + ADAM_EPS) + ADAM_WD * w_ref[...])

    blk = pl.BlockSpec((tr, C), lambda r: (r, 0))
    shp = jax.ShapeDtypeStruct((R, C), F32)
    return pl.pallas_call(
        body,
        grid=(R // tr,),
        in_specs=[blk, blk, blk, pl.BlockSpec((P, tr, C), lambda r: (0, r, 0))],
        out_specs=[blk, blk, blk, blk],
        out_shape=[shp, shp, shp, shp],
        compiler_params=_cp(("parallel",)),
        name=name,
    )(w, m, v, parts)


def adamw_reduced(w, m, v, own, land, chip, name):
    R, C = w.shape
    if R % 8 == 0:
        tr, tc = _pick(R, (256, 128, 64, 32, 16, 8)), C
    else:
        tr, tc = R, _pick(C, (256, 128))
    c1 = 1.0 - ADAM_B1 ** ADAM_STEP
    c2 = 1.0 - ADAM_B2 ** ADAM_STEP

    def body(chip_ref, w_ref, m_ref, v_ref, own_ref, land_ref, g_ref, d_ref, nm_ref, nv_ref):
        mine = own_ref[0].astype(F32)
        g = None
        for k in range(4):
            term = jnp.where(chip_ref[0] == k, mine, land_ref[k].astype(F32))
            g = term if g is None else g + term
        nm = ADAM_B1 * m_ref[...] + (1.0 - ADAM_B1) * g
        nv = ADAM_B2 * v_ref[...] + (1.0 - ADAM_B2) * (g * g)
        g_ref[...] = g
        nm_ref[...] = nm
        nv_ref[...] = nv
        d_ref[...] = -ADAM_LR * ((nm / c1) / (jnp.sqrt(nv / c2) + ADAM_EPS) + ADAM_WD * w_ref[...])

    blk = pl.BlockSpec((tr, tc), lambda r, c, chip_ref: (r, c))
    shp = jax.ShapeDtypeStruct((R, C), F32)
    return pl.pallas_call(
        body,
        grid_spec=pltpu.PrefetchScalarGridSpec(
            num_scalar_prefetch=1,
            grid=(R // tr, C // tc),
            in_specs=[
                blk, blk, blk,
                pl.BlockSpec((1, tr, tc), lambda r, c, chip_ref: (chip_ref[0], r, c)),
                pl.BlockSpec((4, tr, tc), lambda r, c, chip_ref: (0, r, c)),
            ],
            out_specs=[blk, blk, blk, blk],
        ),
        out_shape=[shp, shp, shp, shp],
        compiler_params=_cp(("parallel", "parallel")),
        name=name,
    )(chip, w, m, v, own, land)


_WEIGHTS = [
    "l0_mix_norm_g", "l0_w_in", "l0_sc_conv_w", "l0_w_out", "l0_ffn_norm_g", "l0_ffn_up", "l0_ffn_conv_w", "l0_ffn_down",
    "l1_mix_norm_g", "l1_w_in", "l1_fox_b_f", "l1_sg_w", "l1_sg_b", "l1_sg_norm_g", "l1_w_out", "l1_ffn_norm_g",
    "l1_ffn_up", "l1_ffn_conv_w", "l1_ffn_down", "final_norm_g",
]
_ROW_SHARDED = ["l0_w_out", "l0_ffn_down", "l1_w_out", "l1_ffn_down"]
_BIG = ["l0_w_in", "l0_w_out", "l0_ffn_up", "l0_ffn_down", "l1_w_in", "l1_w_out", "l1_ffn_up", "l1_ffn_down"]
_CONV = ["l0_sc_conv_w", "l0_ffn_conv_w", "l1_ffn_conv_w"]
_SMALL = [n for n in _WEIGHTS if n not in _BIG]
_LAST_SMALL = "l0_mix_norm_g"
_PACK_ROWS = 8


def _pack(arrs):
    flat = []
    for a in arrs:
        v = a.reshape(-1).astype(F32)
        pad = (-v.shape[0]) % (_PACK_ROWS * 128)
        flat.append(jnp.pad(v, (0, pad)))
    return jnp.concatenate(flat).reshape(-1, 128)


def _unpack(packed, shapes):
    out, off = [], 0
    flat = packed.reshape(-1)
    for shp in shapes:
        size = math.prod(shp)
        out.append(flat[off : off + size].reshape(shp))
        off += size + (-size) % (_PACK_ROWS * 128)
    return out


def kernel(x, l0_mix_norm_g, l0_w_in, l0_sc_conv_w, l0_w_out, l0_ffn_norm_g, l0_ffn_up, l0_ffn_conv_w, l0_ffn_down, l1_mix_norm_g, l1_w_in, l1_fox_b_f, l1_sg_w, l1_sg_b, l1_sg_norm_g, l1_w_out, l1_ffn_norm_g, l1_ffn_up, l1_ffn_conv_w, l1_ffn_down, final_norm_g, loss_target, m_l0_mix_norm_g, m_l0_w_in, m_l0_sc_conv_w, m_l0_w_out, m_l0_ffn_norm_g, m_l0_ffn_up, m_l0_ffn_conv_w, m_l0_ffn_down, m_l1_mix_norm_g, m_l1_w_in, m_l1_fox_b_f, m_l1_sg_w, m_l1_sg_b, m_l1_sg_norm_g, m_l1_w_out, m_l1_ffn_norm_g, m_l1_ffn_up, m_l1_ffn_conv_w, m_l1_ffn_down, m_final_norm_g, v_l0_mix_norm_g, v_l0_w_in, v_l0_sc_conv_w, v_l0_w_out, v_l0_ffn_norm_g, v_l0_ffn_up, v_l0_ffn_conv_w, v_l0_ffn_down, v_l1_mix_norm_g, v_l1_w_in, v_l1_fox_b_f, v_l1_sg_w, v_l1_sg_b, v_l1_sg_norm_g, v_l1_w_out, v_l1_ffn_norm_g, v_l1_ffn_up, v_l1_ffn_conv_w, v_l1_ffn_down, v_final_norm_g):
    given = dict(locals())
    w = {n: given[n] for n in _WEIGHTS}
    mom = {n: given["m_" + n] for n in _WEIGHTS}
    var = {n: given["v_" + n] for n in _WEIGHTS}
    xs, target = x[0], loss_target[0]
    S, D = xs.shape
    W = D // 2
    nh = W // HD
    cx, cy, cc = _place()
    me = 4 * cx + 2 * cy + cc

    wts = {"nb": NDEV, "F": l0_ffn_down.shape[0] * NDEV}
    for n in _SMALL:
        if n not in _CONV:
            wts[n] = w[n]
    gathered, loss_sum = {}, []

    def start_gather(names):
        srcs = [(w[n].T if n == "l1_w_in" else w[n]).astype(BF16) for n in names]
        taps = [w[c] for c in _CONV] if names[0] == _BIG[0] else []
        got = all_gather(srcs + taps, "gather_" + "_".join(names))
        for n, full in zip(names, got):
            if n == "l1_w_in":
                gathered[n] = full
            elif n in _ROW_SHARDED:
                wts[n] = full.reshape(-1, D)
            else:
                wts[n] = full.reshape(NDEV * D, -1)
        for c, full in zip(_CONV, got[len(names):] if taps else []):
            wts[c] = full.transpose(1, 0, 2).reshape(CONV_K, -1)

    def at(point, after, value):
        if point == "l1_w_in":
            got, after = lax.optimization_barrier((gathered[point], after))
            wts["l1_w_in_t"] = got.reshape(-1, D)
            wts["l1_w_f_t"] = jnp.pad(wts["l1_w_in_t"][5 * W :], ((0, 128 - nh), (0, 0)))
        elif point == "loss":
            total, after = lax.optimization_barrier((lax.psum(value[0, 0], ("x", "y", "c")), after))
            loss_sum.append(total)
        elif point == "small_ready":
            early = [n for n in _SMALL if n != _LAST_SMALL]
            gathered["small"] = all_gather([_pack([value[n] for n in early])], "gather_small_grads")[0]
        elif point == "small_done":
            after = update_small([n for n in _SMALL if n != _LAST_SMALL], gathered["small"], "small", after)
        return after

    out_g, out_d, out_m, out_v = {}, {}, {}, {}

    def update_small(names, all_terms, tag, after=None):
        shapes = [w[n].shape for n in names]
        full_shapes = [(CONV_K, NDEV * w[n].shape[1]) if n in _CONV else w[n].shape for n in names]
        grads = {}
        for n, t in zip(names, _unpack(sum_slots(all_terms, f"sum_{tag}_grads"), full_shapes)):
            if n in _CONV:
                cols = w[n].shape[1]
                t = lax.dynamic_slice_in_dim(t, me * cols, cols, axis=1)
            grads[n] = t
        res = adamw(
            _pack([w[n] for n in names]), _pack([mom[n] for n in names]), _pack([var[n] for n in names]),
            _pack([grads[n] for n in names])[None], f"adamw_{tag}",
        )
        if after is not None:
            res, after = lax.optimization_barrier((res, after))
        for dst, packed_out in zip((out_g, out_d, out_m, out_v), res):
            for n, t in zip(names, _unpack(packed_out, shapes)):
                dst[n] = t
        return after

    core = jnp.reshape(cc, (1,)).astype(jnp.int32)
    chip = jnp.reshape(2 * cx + cy, (1,)).astype(jnp.int32)
    pair_flying, chip_flying = [], []

    def tie(value, after):
        if after is None:
            return value, None
        return lax.optimization_barrier((value, after))

    def to_chips(after):
        n, flying = pair_flying.pop()
        landed = _split_wait(_pair_copies, *flying, f"reduce_pair_wait_{n}")
        summed = pair_sum(flying[2], landed, core, f"pair_sum_{n}")
        *flying, token = _split_start(_chip_copies, summed, summed.shape, 3, f"reduce_chips_{n}")
        token, after = tie(token, after)
        chip_flying.append((n, flying + [token]))
        return after

    def update(after, behind=None):
        n, flying = chip_flying.pop(0)
        if behind is not None:
            flying[4], _ = lax.optimization_barrier((flying[4], behind))
        landed = _split_wait(_chip_copies, *flying, f"reduce_chips_wait_{n}")
        turn = (lambda t: t.T) if n == "l1_w_in" else (lambda t: t)
        res = adamw_reduced(turn(w[n]), turn(mom[n]), turn(var[n]), flying[2], landed, chip, f"adamw_{n}")
        res, after = tie(res, after)
        out_g[n], out_d[n], out_m[n], out_v[n] = [turn(t) for t in res]
        return after, res[0]

    def on_grad(n, term, after):
        if n is None:
            return to_chips(after)
        if n in _ROW_SHARDED or n == "l1_w_in":
            term = term.reshape(NDEV, -1, D)
        else:
            term = term.reshape(NDEV, D, -1)
        term = term.reshape((4, 2) + term.shape[1:])
        *flying, token = _split_start(_pair_copies, term, term.shape[:1] + term.shape[2:], 4, f"reduce_pair_{n}")
        token, after = tie(token, after)
        if len(chip_flying) == UPDATE_LAG:
            after, _ = update(after)
        if pair_flying:
            after = to_chips(after)
        pair_flying.append((n, flying + [token]))
        return after

    for n in _BIG:
        start_gather([n])
    dx, g = local_step(xs, target, wts, at, on_grad)
    last = _pack([g[_LAST_SMALL]])
    *flying, done = _split_start(_direct_copies, last, (NDEV,) + last.shape, NDEV - 1, "gather_last_grad")
    while len(chip_flying) > 1:
        _, done = update(None, behind=done)
    landed = _split_wait(_direct_copies, *flying, done, "gather_last_grad_wait")
    update_small([_LAST_SMALL], lax.dynamic_update_slice(landed, last[None], (me, 0, 0)), "last")
    update(None, behind=out_g[_LAST_SMALL])
    loss = loss_sum[0]

    return (loss, dx[None], *[out_g[n] for n in _WEIGHTS], *[out_d[n] for n in _WEIGHTS],
            *[out_m[n] for n in _WEIGHTS], *[out_v[n] for n in _WEIGHTS])
```

```python
import functools
import math

import jax
import jax.numpy as jnp
from jax import lax
from jax.experimental import pallas as pl
from jax.experimental.pallas import tpu as pltpu
from jax.experimental.pallas import tpu_sc as plsc

F32 = jnp.float32
BF16 = jnp.bfloat16
HD = 128
EPS = 1e-6
CONV_K = 3
VMEM_LIMIT_BYTES = 48 << 20
NDEV = 8
MESH = pl.DeviceIdType.MESH

ADAM_LR = 0.001
ADAM_B1 = 0.9
ADAM_B2 = 0.999
ADAM_EPS = 1e-08
ADAM_WD = 0.01
ADAM_STEP = 10


def _cp(sem):
    return pltpu.CompilerParams(dimension_semantics=sem, vmem_limit_bytes=VMEM_LIMIT_BYTES)


def _pick(n, prefs):
    for p in prefs:
        if n % p == 0:
            return p
    return n


def _dot(a, b):
    return jnp.dot(a, b, preferred_element_type=F32)


def _dot_nt(a, b):
    return lax.dot_general(a, b, (((1,), (1,)), ((), ())), preferred_element_type=F32)


def _dot_tn(a, b):
    return lax.dot_general(a, b, (((0,), (0,)), ((), ())), preferred_element_type=F32)


def _split3(x):
    hi = x.astype(BF16)
    r = x - hi.astype(F32)
    mid = r.astype(BF16)
    lo = (r - mid.astype(F32)).astype(BF16)
    return hi, mid, lo


def _dot_ones_left(ones_bf16, x):
    hi, mid, lo = _split3(x)
    return _dot(ones_bf16, hi) + _dot(ones_bf16, mid) + _dot(ones_bf16, lo)


def _iota2(shape, axis):
    return lax.broadcasted_iota(jnp.int32, shape, axis)


def mm_nn(a, w2d, nb, name, out_dtype=BF16, res=None, tm=None, tn=None, tk=None, a_map=None, a_shape=None):
    M, K = a_shape or a.shape
    n = w2d.shape[1]
    assert w2d.shape[0] == nb * K or (nb == 1 and w2d.shape[0] > K)
    a_map = a_map or (lambda i, k: (i, k))
    tm = tm or _pick(M, (1024, 512, 256, 128))
    tn = tn or _pick(n, (1408, 1024, 768, 512, 256, 128))
    tk = tk or (K if K <= 2048 else _pick(K, (1408, 1024, 512, 256, 128)))
    nk, nt = K // tk, n // tn
    has_res = res is not None

    def body(*refs):
        if has_res:
            a_ref, w_ref, r_ref, o_ref = refs[:4]
        else:
            a_ref, w_ref, o_ref = refs[:3]
            r_ref = None
        part = _dot(a_ref[...], w_ref[...])

        def finish(acc):
            if r_ref is not None:
                acc = acc + r_ref[...].astype(F32)
            o_ref[...] = acc.astype(o_ref.dtype)

        if nk == 1:
            finish(part)
        else:
            acc_ref = refs[-1]
            k = pl.program_id(3)

            @pl.when(k == 0)
            def _():
                acc_ref[...] = part

            @pl.when(k > 0)
            def _():
                acc_ref[...] += part

            @pl.when(k == nk - 1)
            def _():
                finish(acc_ref[...])

    in_specs = [
        pl.BlockSpec((tm, tk), lambda i, j, t, k: a_map(i, k)),
        pl.BlockSpec((tk, tn), lambda i, j, t, k: (j * nk + k, t)),
    ]
    args = [a, w2d]
    out_spec = pl.BlockSpec((tm, tn), lambda i, j, t, k: (i, j * nt + t))
    if has_res:
        in_specs.append(out_spec)
        args.append(res)
    return pl.pallas_call(
        body,
        grid=(M // tm, nb, nt, nk),
        in_specs=in_specs,
        out_specs=out_spec,
        out_shape=jax.ShapeDtypeStruct((M, nb * n), out_dtype),
        scratch_shapes=[pltpu.VMEM((tm, tn), F32)] if nk > 1 else [],
        compiler_params=_cp(("parallel", "parallel", "parallel", "arbitrary")),
        name=name,
    )(*args)


def mm_nt(dy2d, w2d, nb, M, K, name, out_dtype=BF16, res=None, dy_maps=None, tm=None, tko=None, tn=None):
    n = w2d.shape[1]
    assert w2d.shape[0] == nb * K or (nb == 1 and w2d.shape[0] > K)
    tm = tm or _pick(M, (1024, 512, 256, 128))
    tko = tko or _pick(K, (1024, 512, 256, 128))
    tn = tn or _pick(n, (1408, 1024, 768, 512, 256, 128))
    nt, nko = n // tn, K // tko
    has_res = res is not None
    if dy_maps is None:
        dy_maps = [lambda i, j, t: (i, j * nt + t)]
    nd = len(dy_maps)
    td = tn // nd

    one_step = nb * nt == 1

    def body(*refs):
        d_refs, w_ref = refs[:nd], refs[nd]
        r_ref = refs[nd + 1] if has_res else None
        d = d_refs[0][...] if nd == 1 else jnp.concatenate([r[...] for r in d_refs], axis=1)
        part = _dot_nt(d, w_ref[...])
        if one_step:
            o_ref = refs[-1]
            if r_ref is not None:
                part = part + r_ref[...].astype(F32)
            o_ref[...] = part.astype(o_ref.dtype)
            return
        o_ref, acc_ref = refs[-2], refs[-1]
        j, t = pl.program_id(2), pl.program_id(3)
        first = jnp.logical_and(j == 0, t == 0)
        last = jnp.logical_and(j == nb - 1, t == nt - 1)

        @pl.when(first)
        def _():
            acc_ref[...] = part

        @pl.when(jnp.logical_not(first))
        def _():
            acc_ref[...] += part

        @pl.when(last)
        def _():
            acc = acc_ref[...]
            if r_ref is not None:
                acc = acc + r_ref[...].astype(F32)
            o_ref[...] = acc.astype(o_ref.dtype)

    in_specs = [pl.BlockSpec((tm, td), functools.partial(lambda f, i, ko, j, t: f(i, j, t), f)) for f in dy_maps]
    in_specs.append(pl.BlockSpec((tko, tn), lambda i, ko, j, t: (j * nko + ko, t)))
    args = [dy2d] * nd + [w2d]
    out_spec = pl.BlockSpec((tm, tko), lambda i, ko, j, t: (i, ko))
    if has_res:
        in_specs.append(out_spec)
        args.append(res)
    return pl.pallas_call(
        body,
        grid=(M // tm, nko, nb, nt),
        in_specs=in_specs,
        out_specs=out_spec,
        out_shape=jax.ShapeDtypeStruct((M, K), out_dtype),
        scratch_shapes=[] if one_step else [pltpu.VMEM((tm, tko), F32)],
        compiler_params=_cp(("parallel", "parallel", "arbitrary", "arbitrary")),
        name=name,
    )(*args)


def mm_tn(x, dy2d, nb, n, name, out_dtype=BF16, dy_maps=None, tko=None, tn=None, x_map=None, x_shape=None):
    S, K = x_shape or x.shape
    x_map = x_map or (lambda ko: (0, ko))
    tko = tko or _pick(K, (512, 256, 128))
    tn = tn or _pick(n, (1408, 1024, 768, 512, 256, 128))
    nt, nko = n // tn, K // tko
    if dy_maps is None:
        dy_maps = [lambda j, t: (0, j * nt + t)]
    nd = len(dy_maps)
    td = tn // nd

    def body(*refs):
        x_ref, d_refs, o_ref = refs[0], refs[1 : 1 + nd], refs[-1]
        d = d_refs[0][...] if nd == 1 else jnp.concatenate([r[...] for r in d_refs], axis=1)
        o_ref[...] = _dot_tn(x_ref[...], d).astype(o_ref.dtype)

    in_specs = [pl.BlockSpec((S, tko), lambda ko, j, t: x_map(ko))]
    in_specs += [pl.BlockSpec((S, td), functools.partial(lambda f, ko, j, t: f(j, t), f)) for f in dy_maps]
    return pl.pallas_call(
        body,
        grid=(nko, nb, nt),
        in_specs=in_specs,
        out_specs=pl.BlockSpec((tko, tn), lambda ko, j, t: (j * nko + ko, t)),
        out_shape=jax.ShapeDtypeStruct((nb * K, n), out_dtype),
        compiler_params=_cp(("parallel", "parallel", "parallel")),
        name=name,
    )(x, *([dy2d] * nd))


def rms_fwd(x, g, name):
    S, D = x.shape
    tm = _pick(S, (512, 256, 128))

    def body(x_ref, g_ref, o_ref):
        xf = x_ref[...]
        r = lax.rsqrt(jnp.mean(xf * xf, axis=-1, keepdims=True) + EPS)
        o_ref[...] = (xf * r * g_ref[...]).astype(o_ref.dtype)

    return pl.pallas_call(
        body,
        grid=(S // tm,),
        in_specs=[pl.BlockSpec((tm, D), lambda i: (i, 0)), pl.BlockSpec((1, D), lambda i: (0, 0))],
        out_specs=pl.BlockSpec((tm, D), lambda i: (i, 0)),
        out_shape=jax.ShapeDtypeStruct((S, D), BF16),
        compiler_params=_cp(("parallel",)),
        name=name,
    )(x, g.reshape(1, D))


def rms_bwd(x, g, dh, dres, name):
    S, D = x.shape
    tm = _pick(S, (256, 128))

    def body(x_ref, g_ref, dh_ref, dr_ref, dx_ref, dxb_ref, dg_ref):
        i = pl.program_id(0)
        xf = x_ref[...]
        dh = dh_ref[...].astype(F32)
        r = lax.rsqrt(jnp.mean(xf * xf, axis=-1, keepdims=True) + EPS)
        gy = dh * g_ref[...]
        proj = jnp.mean(gy * xf, axis=-1, keepdims=True)
        dx = dr_ref[...] + r * gy - xf * (r * r * r * proj)
        dx_ref[...] = dx
        dxb_ref[...] = dx.astype(BF16)
        dg = jnp.sum(dh * (xf * r), axis=0, keepdims=True)

        @pl.when(i == 0)
        def _():
            dg_ref[...] = dg

        @pl.when(i > 0)
        def _():
            dg_ref[...] += dg

    row = pl.BlockSpec((tm, D), lambda i: (i, 0))
    vec = pl.BlockSpec((1, D), lambda i: (0, 0))
    return pl.pallas_call(
        body,
        grid=(S // tm,),
        in_specs=[row, vec, row, row],
        out_specs=[row, row, vec],
        out_shape=[jax.ShapeDtypeStruct((S, D), F32), jax.ShapeDtypeStruct((S, D), BF16), jax.ShapeDtypeStruct((1, D), F32)],
        compiler_params=_cp(("arbitrary",)),
        name=name,
    )(x, g.reshape(1, D), dh, dres)


def loss_head(x, g, target, name):
    S, D = x.shape
    tm = _pick(S, (256, 128))

    def body(x_ref, g_ref, t_ref, dx_ref, dxb_ref, dg_ref, loss_ref):
        i = pl.program_id(0)
        xf = x_ref[...]
        gg = g_ref[...]
        r = lax.rsqrt(jnp.mean(xf * xf, axis=-1, keepdims=True) + EPS)
        xh = xf * r
        err = xh * gg - t_ref[...]
        part = (0.5 / D) * jnp.sum(err * err)
        dy = err * (1.0 / D)
        gy = dy * gg
        proj = jnp.mean(gy * xf, axis=-1, keepdims=True)
        dx = r * gy - xf * (r * r * r * proj)
        dx_ref[...] = dx
        dxb_ref[...] = dx.astype(BF16)
        dg = jnp.sum(dy * xh, axis=0, keepdims=True)
        lossb = jnp.full(loss_ref.shape, part, F32)

        @pl.when(i == 0)
        def _():
            dg_ref[...] = dg
            loss_ref[...] = lossb

        @pl.when(i > 0)
        def _():
            dg_ref[...] += dg
            loss_ref[...] += lossb

    row = pl.BlockSpec((tm, D), lambda i: (i, 0))
    vec = pl.BlockSpec((1, D), lambda i: (0, 0))
    return pl.pallas_call(
        body,
        grid=(S // tm,),
        in_specs=[row, vec, row],
        out_specs=[row, row, vec, pl.BlockSpec((8, 128), lambda i: (0, 0))],
        out_shape=[
            jax.ShapeDtypeStruct((S, D), F32),
            jax.ShapeDtypeStruct((S, D), BF16),
            jax.ShapeDtypeStruct((1, D), F32),
            jax.ShapeDtypeStruct((8, 128), F32),
        ],
        compiler_params=_cp(("arbitrary",)),
        name=name,
    )(x, g.reshape(1, D), target)


def _shift_down(s, k):
    if k == 0:
        return s
    return jnp.where(_iota2(s.shape, 0) >= k, pltpu.roll(s, k, axis=0), 0.0)


def _shift_up(s, k):
    if k == 0:
        return s
    n = s.shape[0]
    return jnp.where(_iota2(s.shape, 0) < n - k, pltpu.roll(s, n - k, axis=0), 0.0)


def _conv(s, w):
    return w[0:1] * _shift_down(s, 2) + w[1:2] * _shift_down(s, 1) + w[2:3] * s


def _conv_t(d, w):
    return w[2:3] * d + w[1:2] * _shift_up(d, 1) + w[0:1] * _shift_up(d, 2)


def _conv_dw(d, s):
    return [jnp.sum(d * _shift_down(s, CONV_K - 1 - k), axis=0, keepdims=True) for k in range(CONV_K)]


def sc_fwd(p, convw, cat, W, name):
    S = p.shape[0]
    tc = _pick(W, (256, 128))
    nc = W // tc

    def body(gb_ref, gc_ref, hi_ref, w_ref, cat_ref, o_ref):
        s = gc_ref[...].astype(F32) * hi_ref[...].astype(F32)
        o_ref[...] = (gb_ref[...].astype(F32) * _conv(s, w_ref[...])).astype(o_ref.dtype)

    col = lambda part: pl.BlockSpec((S, tc), lambda c: (0, part * nc + c))
    return pl.pallas_call(
        body,
        grid=(nc,),
        in_specs=[col(3), col(4), col(5), pl.BlockSpec((CONV_K, tc), lambda c: (0, c)), pl.BlockSpec(memory_space=pl.ANY)],
        out_specs=col(1),
        out_shape=jax.ShapeDtypeStruct(cat.shape, cat.dtype),
        input_output_aliases={4: 0},
        compiler_params=_cp(("parallel",)),
        name=name,
    )(p, p, p, convw, cat)


def sc_bwd(p, convw, dcat, dp, W, name):
    S = p.shape[0]
    tc = _pick(W, (256, 128))
    nc = W // tc

    def body(gb_ref, gc_ref, hi_ref, w_ref, do_ref, dp_in_ref, dp_ref, dw_ref):
        gb = gb_ref[...].astype(F32)
        gc = gc_ref[...].astype(F32)
        hi = hi_ref[...].astype(F32)
        w = w_ref[...]
        do = do_ref[...].astype(F32)
        s = gc * hi
        dcs = do * gb
        ds = _conv_t(dcs, w)
        dp_ref[0] = (do * _conv(s, w)).astype(dp_ref.dtype)
        dp_ref[1] = (ds * hi).astype(dp_ref.dtype)
        dp_ref[2] = (ds * gc).astype(dp_ref.dtype)
        for k, row in enumerate(_conv_dw(dcs, s)):
            dw_ref[k : k + 1, :] = row

    col = lambda part: pl.BlockSpec((S, tc), lambda c: (0, part * nc + c))
    return pl.pallas_call(
        body,
        grid=(nc,),
        in_specs=[
            col(3), col(4), col(5),
            pl.BlockSpec((CONV_K, tc), lambda c: (0, c)),
            pl.BlockSpec((S, tc), lambda c: (0, nc + c)),
            pl.BlockSpec(memory_space=pl.ANY),
        ],
        out_specs=[pl.BlockSpec((3, S, tc), lambda c: (1, 0, c)), pl.BlockSpec((CONV_K, tc), lambda c: (0, c))],
        out_shape=[jax.ShapeDtypeStruct(dp.shape, dp.dtype), jax.ShapeDtypeStruct((CONV_K, W), F32)],
        input_output_aliases={5: 0},
        compiler_params=_cp(("parallel",)),
        name=name,
    )(p, p, p, convw, dcat, dp)


def _silu_parts(a):
    sig = 1.0 / (1.0 + jnp.exp(-a))
    return a * sig, sig


def ffn_act_fwd(u, convw, F, name):
    S = u.shape[0]
    tc = _pick(F, (256, 128))
    nc = F // tc

    def body(ug_ref, uu_ref, wg_ref, wu_ref, o_ref):
        ag = _conv(ug_ref[...].astype(F32), wg_ref[...])
        au = _conv(uu_ref[...].astype(F32), wu_ref[...])
        o_ref[...] = (_silu_parts(ag)[0] * au).astype(o_ref.dtype)

    col = lambda half: pl.BlockSpec((S, tc), lambda c: (0, half * nc + c))
    wcol = lambda half: pl.BlockSpec((CONV_K, tc), lambda c: (0, half * nc + c))
    return pl.pallas_call(
        body,
        grid=(nc,),
        in_specs=[col(0), col(1), wcol(0), wcol(1)],
        out_specs=pl.BlockSpec((S, tc), lambda c: (0, c)),
        out_shape=jax.ShapeDtypeStruct((S, F), BF16),
        compiler_params=_cp(("parallel",)),
        name=name,
    )(u, u, convw, convw)


def ffn_act_bwd(u, convw, dact, F, name):
    S = u.shape[0]
    tc = _pick(F, (256, 128))
    nc = F // tc

    def body(ug_ref, uu_ref, wg_ref, wu_ref, da_ref, du_ref, dw_ref):
        ug = ug_ref[...].astype(F32)
        uu = uu_ref[...].astype(F32)
        wg = wg_ref[...]
        wu = wu_ref[...]
        da = da_ref[...].astype(F32)
        ag = _conv(ug, wg)
        au = _conv(uu, wu)
        sl, sig = _silu_parts(ag)
        dag = da * au * (sig * (1.0 + ag * (1.0 - sig)))
        dau = da * sl
        du_ref[0] = _conv_t(dag, wg).astype(du_ref.dtype)
        du_ref[1] = _conv_t(dau, wu).astype(du_ref.dtype)
        for k, (rg, ru) in enumerate(zip(_conv_dw(dag, ug), _conv_dw(dau, uu))):
            dw_ref[0, k : k + 1, :] = rg
            dw_ref[1, k : k + 1, :] = ru

    col = lambda half: pl.BlockSpec((S, tc), lambda c: (0, half * nc + c))
    wcol = lambda half: pl.BlockSpec((CONV_K, tc), lambda c: (0, half * nc + c))
    return pl.pallas_call(
        body,
        grid=(nc,),
        in_specs=[col(0), col(1), wcol(0), wcol(1), pl.BlockSpec((S, tc), lambda c: (0, c))],
        out_specs=[pl.BlockSpec((2, S, tc), lambda c: (0, 0, c)), pl.BlockSpec((2, CONV_K, tc), lambda c: (0, 0, c))],
        out_shape=[jax.ShapeDtypeStruct((2, S, F), BF16), jax.ShapeDtypeStruct((2, CONV_K, F), F32)],
        compiler_params=_cp(("parallel",)),
        name=name,
    )(u, u, convw, convw, dact)


def _softplus(z):
    return jnp.maximum(z, 0.0) + jnp.log(1.0 + jnp.exp(-jnp.abs(z)))


def _key_strip(S):
    return _pick(S, (512, 256, 128))


def _query_rows(S):
    tq = _pick(S, (512, 256, 128))
    assert _key_strip(S) % tq == 0
    return tq


def _split2(x):
    hi = x.astype(BF16)
    return hi, (x - hi.astype(F32)).astype(BF16)


def _block_sums(x, ones_bf16):
    hi, lo = _split2(x)
    return [
        _dot(hi[:, b * HD : (b + 1) * HD], ones_bf16) + _dot(lo[:, b * HD : (b + 1) * HD], ones_bf16)
        for b in range(x.shape[1] // HD)
    ]


def _strip_mask(shape, row0, off, strict):
    cols, rows = _iota2(shape, 1) + off, _iota2(shape, 0) + row0
    return cols < rows if strict else cols <= rows


def _sb_strip(q, ks, row0, off, run, su, masked):
    z = _dot_nt(q, ks) * (HD ** -0.5)
    sp = _softplus(z)
    mask = _strip_mask(z.shape, row0, off, True) if masked else None
    l = jnp.where(mask, -sp, 0.0) if masked else -sp
    within = _block_sums(l, su)
    later = [None] * len(within)
    for b in reversed(range(len(within))):
        later[b] = within[b] + run
        run = run + jnp.sum(l[:, b * HD : (b + 1) * HD], axis=1, keepdims=True)
    a = jnp.exp(z - sp + jnp.concatenate(later, axis=1))
    return z, (jnp.where(mask, a, 0.0) if masked else a), run


def sb_fwd(p, W, name):
    S = p.shape[0]
    TQ, TK = _query_rows(S), _key_strip(S)
    nh, nq = W // HD, S // TQ

    def body(q_ref, k_ref, v_ref, o_ref):
        i = pl.program_id(1)
        q = q_ref[...]
        su = (_iota2((HD, HD), 0) > _iota2((HD, HD), 1)).astype(BF16)
        last = (i * TQ + TQ - 1) // TK

        def strip(g, carry, masked):
            acc, run = carry
            off = pl.multiple_of(g * TK, TK)
            _, a, run = _sb_strip(q, k_ref[pl.ds(off, TK), :], i * TQ, off, run, su, masked)
            return acc + _dot(a.astype(BF16), v_ref[pl.ds(off, TK), :]), run

        carry = strip(last, (jnp.zeros((TQ, HD), F32), jnp.zeros((TQ, 1), F32)), True)
        acc, _ = lax.fori_loop(0, last, lambda gg, c: strip(last - 1 - gg, c, False), carry)
        o_ref[...] = acc.astype(o_ref.dtype)

    return pl.pallas_call(
        body,
        grid=(nh, nq),
        in_specs=[
            pl.BlockSpec((TQ, HD), lambda h, i: (i, h)),
            pl.BlockSpec((S, HD), lambda h, i: (0, nh + h)),
            pl.BlockSpec((S, HD), lambda h, i: (0, 2 * nh + h)),
        ],
        out_specs=pl.BlockSpec((TQ, HD), lambda h, i: (i, h)),
        out_shape=jax.ShapeDtypeStruct((S, 2 * W), BF16),
        compiler_params=_cp(("parallel", "arbitrary")),
        name=name,
    )(p, p, p)


def sb_bwd(p, dcat, W, name):
    S = p.shape[0]
    TQ, TK = _query_rows(S), _key_strip(S)
    nh, nq = W // HD, S // TQ
    scale = HD ** -0.5

    def body(q_ref, k_ref, v_ref, do_ref, dp_ref, dk_acc, dv_acc, e_scr, z_scr):
        i = pl.program_id(1)
        q = q_ref[...]
        do = do_ref[...]
        su = (_iota2((HD, HD), 0) > _iota2((HD, HD), 1)).astype(BF16)
        sl = (_iota2((HD, HD), 0) < _iota2((HD, HD), 1)).astype(BF16)
        last = (i * TQ + TQ - 1) // TK

        @pl.when(i == 0)
        def _():
            dk_acc[...] = jnp.zeros_like(dk_acc)
            dv_acc[...] = jnp.zeros_like(dv_acc)

        def pass_a(g, run, masked):
            off = pl.multiple_of(g * TK, TK)
            z, a, run = _sb_strip(q, k_ref[pl.ds(off, TK), :], i * TQ, off, run, su, masked)
            e_scr[g] = a * _dot_nt(do, v_ref[pl.ds(off, TK), :])
            z_scr[g] = z
            dv_acc[pl.ds(off, TK), :] += _dot_tn(a.astype(BF16), do)
            return run

        run = pass_a(last, jnp.zeros((TQ, 1), F32), True)
        lax.fori_loop(0, last, lambda gg, r: pass_a(last - 1 - gg, r, False), run)

        def pass_b(g, carry, masked):
            dq, run_e = carry
            off = pl.multiple_of(g * TK, TK)
            e = e_scr[g]
            z = z_scr[g]
            within = _block_sums(e, sl)
            before = []
            for b in range(len(within)):
                before.append(within[b] + run_e)
                run_e = run_e + jnp.sum(e[:, b * HD : (b + 1) * HD], axis=1, keepdims=True)
            sig = 1.0 / (1.0 + jnp.exp(-z))
            dz = e * (1.0 - sig) - jnp.concatenate(before, axis=1) * sig
            if masked:
                dz = jnp.where(_strip_mask(z.shape, i * TQ, off, True), dz, 0.0)
            dz = (dz * scale).astype(BF16)
            dq = dq + _dot(dz, k_ref[pl.ds(off, TK), :])
            dk_acc[pl.ds(off, TK), :] += _dot_tn(dz, q)
            return dq, run_e

        carry = lax.fori_loop(0, last, lambda g, c: pass_b(g, c, False), (jnp.zeros((TQ, HD), F32), jnp.zeros((TQ, 1), F32)))
        dq, _ = pass_b(last, carry, True)
        dp_ref[0, pl.ds(pl.multiple_of(i * TQ, TQ), TQ), :] = dq.astype(dp_ref.dtype)

        @pl.when(i == nq - 1)
        def _():
            dp_ref[1] = dk_acc[...].astype(dp_ref.dtype)
            dp_ref[2] = dv_acc[...].astype(dp_ref.dtype)

    return pl.pallas_call(
        body,
        grid=(nh, nq),
        in_specs=[
            pl.BlockSpec((TQ, HD), lambda h, i: (i, h)),
            pl.BlockSpec((S, HD), lambda h, i: (0, nh + h)),
            pl.BlockSpec((S, HD), lambda h, i: (0, 2 * nh + h)),
            pl.BlockSpec((TQ, HD), lambda h, i: (i, h)),
        ],
        out_specs=pl.BlockSpec((3, S, HD), lambda h, i: (0, 0, h)),
        out_shape=jax.ShapeDtypeStruct((6, S, W), BF16),
        scratch_shapes=[
            pltpu.VMEM((S, HD), F32),
            pltpu.VMEM((S, HD), F32),
            pltpu.VMEM((S // TK, TQ, TK), F32),
            pltpu.VMEM((S // TK, TQ, TK), F32),
        ],
        compiler_params=_cp(("parallel", "arbitrary")),
        name=name,
    )(p, p, p, dcat)


def fox_gate_fwd(f, b, name):
    S = f.shape[0]
    nq = S // HD

    def body(f_ref, b_ref, c_ref, run):
        i = pl.program_id(0)

        @pl.when(i == 0)
        def _():
            run[...] = jnp.zeros_like(run)

        lf = -_softplus(-(f_ref[...] + b_ref[...]))
        tri = (_iota2((HD, HD), 0) >= _iota2((HD, HD), 1)).astype(BF16)
        c_ref[...] = _dot_ones_left(tri, lf) + run[...]
        run[...] += jnp.sum(lf, axis=0, keepdims=True)

    return pl.pallas_call(
        body,
        grid=(nq,),
        in_specs=[pl.BlockSpec((HD, 128), lambda i: (i, 0)), pl.BlockSpec((1, 128), lambda i: (0, 0))],
        out_specs=pl.BlockSpec((HD, 128), lambda i: (i, 0)),
        out_shape=jax.ShapeDtypeStruct((S, 128), F32),
        scratch_shapes=[pltpu.VMEM((1, 128), F32)],
        compiler_params=_cp(("arbitrary",)),
        name=name,
    )(f, b)


def fox_gate_bwd(f, b, dc, name):
    S = f.shape[0]
    nq = S // HD

    def body(f_ref, b_ref, dc_ref, df_ref, db_ref, run):
        i = pl.program_id(0)

        @pl.when(i == 0)
        def _():
            run[...] = jnp.zeros_like(run)

        dc = dc_ref[...]
        tri = (_iota2((HD, HD), 0) <= _iota2((HD, HD), 1)).astype(BF16)
        dlf = _dot_ones_left(tri, dc) + run[...]
        run[...] += jnp.sum(dc, axis=0, keepdims=True)
        x = f_ref[...] + b_ref[...]
        df = dlf * (1.0 / (1.0 + jnp.exp(x)))
        df_ref[...] = df
        db = jnp.sum(df, axis=0, keepdims=True)

        @pl.when(i == 0)
        def _():
            db_ref[...] = db

        @pl.when(i > 0)
        def _():
            db_ref[...] += db

    rev = pl.BlockSpec((HD, 128), lambda i: (nq - 1 - i, 0))
    vec = pl.BlockSpec((1, 128), lambda i: (0, 0))
    return pl.pallas_call(
        body,
        grid=(nq,),
        in_specs=[rev, vec, rev],
        out_specs=[rev, vec],
        out_shape=[jax.ShapeDtypeStruct((S, 128), F32), jax.ShapeDtypeStruct((1, 128), F32)],
        scratch_shapes=[pltpu.VMEM((1, 128), F32)],
        compiler_params=_cp(("arbitrary",)),
        name=name,
    )(f, b, dc)


def _fox_logits(q, ks, ct, cs, row0, off, masked):
    s = _dot_nt(q, ks) * (HD ** -0.5) + (ct - cs)
    if not masked:
        return s, None
    mask = _strip_mask(s.shape, row0, off, False)
    return jnp.where(mask, s, -1e30), mask


def fox_fwd(p, ccol, crow, cat, W, name):
    S = p.shape[0]
    TQ, TK = _query_rows(S), _key_strip(S)
    nh, nq = W // HD, S // TQ

    def body(q_ref, k_ref, v_ref, cc_ref, cr_ref, cat_ref, o_ref, lse_ref):
        i = pl.program_id(1)
        q = q_ref[...]
        ct = cc_ref[0]

        def step(g, carry, masked):
            m, l, acc = carry
            off = pl.multiple_of(g * TK, TK)
            s, _ = _fox_logits(q, k_ref[pl.ds(off, TK), :], ct, cr_ref[0, pl.ds(g, 1), :], i * TQ, off, masked)
            m_new = jnp.maximum(m, jnp.max(s, axis=1, keepdims=True))
            alpha = jnp.exp(m - m_new)
            pr = jnp.exp(s - m_new)
            l = alpha * l + jnp.sum(pr, axis=1, keepdims=True)
            acc = alpha * acc + _dot(pr.astype(BF16), v_ref[pl.ds(off, TK), :])
            return m_new, l, acc

        init = (jnp.full((TQ, 1), -1e30, F32), jnp.zeros((TQ, 1), F32), jnp.zeros((TQ, HD), F32))
        last = (i * TQ + TQ - 1) // TK
        m, l, acc = step(last, lax.fori_loop(0, last, lambda g, c: step(g, c, False), init), True)
        o_ref[...] = (acc / l).astype(o_ref.dtype)
        lse_ref[0] = m + jnp.log(l)

    return pl.pallas_call(
        body,
        grid=(nh, nq),
        in_specs=[
            pl.BlockSpec((TQ, HD), lambda h, i: (i, 2 * nh + h)),
            pl.BlockSpec((S, HD), lambda h, i: (0, 3 * nh + h)),
            pl.BlockSpec((S, HD), lambda h, i: (0, 4 * nh + h)),
            pl.BlockSpec((1, TQ, 1), lambda h, i: (h, i, 0)),
            pl.BlockSpec((1, S // TK, TK), lambda h, i: (h, 0, 0)),
            pl.BlockSpec(memory_space=pl.ANY),
        ],
        out_specs=[pl.BlockSpec((TQ, HD), lambda h, i: (i, nh + h)), pl.BlockSpec((1, TQ, 1), lambda h, i: (h, i, 0))],
        out_shape=[jax.ShapeDtypeStruct(cat.shape, cat.dtype), jax.ShapeDtypeStruct((nh, S, 1), F32)],
        input_output_aliases={5: 0},
        compiler_params=_cp(("parallel", "arbitrary")),
        name=name,
    )(p, p, p, ccol, crow, cat)


def fox_bwd(p, ccol, crow, cat, lse, dcat, dp, W, name):
    S = p.shape[0]
    TQ, TK = _query_rows(S), _key_strip(S)
    nh, nq = W // HD, S // TQ
    scale = HD ** -0.5

    def body(q_ref, k_ref, v_ref, cc_ref, cr_ref, o_ref, lse_ref, do_ref, dp_in_ref, dp_ref, dcs_ref, dct_ref, dk_acc, dv_acc):
        i = pl.program_id(1)
        q = q_ref[...]
        do = do_ref[...]
        ct = cc_ref[0]
        lse_i = lse_ref[0]
        delta = jnp.sum(do.astype(F32) * o_ref[...].astype(F32), axis=1, keepdims=True)

        @pl.when(i == 0)
        def _():
            dk_acc[...] = jnp.zeros_like(dk_acc)
            dv_acc[...] = jnp.zeros_like(dv_acc)
            dcs_ref[...] = jnp.zeros_like(dcs_ref)

        def step(g, carry, masked):
            dq, dct = carry
            off = pl.multiple_of(g * TK, TK)
            ks = k_ref[pl.ds(off, TK), :]
            s, mask = _fox_logits(q, ks, ct, cr_ref[0, pl.ds(g, 1), :], i * TQ, off, masked)
            pr = jnp.where(mask, jnp.exp(s - lse_i), 0.0) if masked else jnp.exp(s - lse_i)
            ds = pr * (_dot_nt(do, v_ref[pl.ds(off, TK), :]) - delta)
            dv_acc[pl.ds(off, TK), :] += _dot_tn(pr.astype(BF16), do)
            dsb = (ds * scale).astype(BF16)
            dk_acc[pl.ds(off, TK), :] += _dot_tn(dsb, q)
            dcs_ref[0, pl.ds(g, 1), :] += jnp.sum(ds, axis=0, keepdims=True)
            return dq + _dot(dsb, ks), dct + jnp.sum(ds, axis=1, keepdims=True)

        last = (i * TQ + TQ - 1) // TK
        carry = lax.fori_loop(0, last, lambda g, c: step(g, c, False), (jnp.zeros((TQ, HD), F32), jnp.zeros((TQ, 1), F32)))
        dq, dct = step(last, carry, True)
        dp_ref[0, pl.ds(pl.multiple_of(i * TQ, TQ), TQ), :] = dq.astype(dp_ref.dtype)
        dct_ref[0] = dct

        @pl.when(i == nq - 1)
        def _():
            dp_ref[1] = dk_acc[...].astype(dp_ref.dtype)
            dp_ref[2] = dv_acc[...].astype(dp_ref.dtype)

    return pl.pallas_call(
        body,
        grid=(nh, nq),
        in_specs=[
            pl.BlockSpec((TQ, HD), lambda h, i: (i, 2 * nh + h)),
            pl.BlockSpec((S, HD), lambda h, i: (0, 3 * nh + h)),
            pl.BlockSpec((S, HD), lambda h, i: (0, 4 * nh + h)),
            pl.BlockSpec((1, TQ, 1), lambda h, i: (h, i, 0)),
            pl.BlockSpec((1, S // TK, TK), lambda h, i: (h, 0, 0)),
            pl.BlockSpec((TQ, HD), lambda h, i: (i, nh + h)),
            pl.BlockSpec((1, TQ, 1), lambda h, i: (h, i, 0)),
            pl.BlockSpec((TQ, HD), lambda h, i: (i, nh + h)),
            pl.BlockSpec(memory_space=pl.ANY),
        ],
        out_specs=[
            pl.BlockSpec((3, S, HD), lambda h, i: (1, 0, h)),
            pl.BlockSpec((1, S // TK, TK), lambda h, i: (h, 0, 0)),
            pl.BlockSpec((1, TQ, 1), lambda h, i: (h, i, 0)),
        ],
        out_shape=[
            jax.ShapeDtypeStruct(dp.shape, dp.dtype),
            jax.ShapeDtypeStruct((nh, S // TK, TK), F32),
            jax.ShapeDtypeStruct((nh, S, 1), F32),
        ],
        input_output_aliases={8: 0},
        scratch_shapes=[pltpu.VMEM((S, HD), F32), pltpu.VMEM((S, HD), F32)],
        compiler_params=_cp(("parallel", "arbitrary")),
        name=name,
    )(p, p, p, ccol, crow, cat, lse, dcat, dp)


_GELU_K = math.sqrt(2.0 / math.pi)
_GELU_C = 0.044715


def _gelu(x):
    return 0.5 * x * (1.0 + jnp.tanh(_GELU_K * (x + _GELU_C * x * x * x)))


def _gelu_grad(x):
    t = jnp.tanh(_GELU_K * (x + _GELU_C * x * x * x))
    return 0.5 * (1.0 + t) + 0.5 * x * (1.0 - t * t) * (_GELU_K * (1.0 + 3.0 * _GELU_C * x * x))


def _layernorm_parts(gv):
    xc = gv - jnp.mean(gv, axis=-1, keepdims=True)
    r = lax.rsqrt(jnp.mean(xc * xc, axis=-1, keepdims=True) + EPS)
    return xc * r, r


def sg_fwd(p, sg_w, sg_bt, sg_g, W, name):
    S = p.shape[0]
    G, nq = W // HD, S // HD

    def body(u_ref, v_ref, w_ref, bt_ref, g_ref, o_ref):
        xh, _ = _layernorm_parts(_gelu(v_ref[...].astype(F32)))
        vn = (xh * g_ref[...]).astype(BF16)
        tri = _iota2((HD, HD), 0) >= _iota2((HD, HD), 1)
        for gi in range(G):
            cols = slice(gi * HD, (gi + 1) * HD)
            wt = jnp.where(tri, w_ref[gi], 0.0).astype(BF16)
            mixed = _dot(wt, vn[:, cols]) + bt_ref[:, gi : gi + 1]
            o_ref[:, cols] = (_gelu(u_ref[:, cols].astype(F32)) * mixed).astype(o_ref.dtype)

    return pl.pallas_call(
        body,
        grid=(nq,),
        in_specs=[
            pl.BlockSpec((HD, W), lambda i: (i, 0)),
            pl.BlockSpec((HD, W), lambda i: (i, 1)),
            pl.BlockSpec((G, HD, HD), lambda i: (0, 0, 0)),
            pl.BlockSpec((HD, G), lambda i: (0, 0)),
            pl.BlockSpec((1, W), lambda i: (0, 0)),
        ],
        out_specs=pl.BlockSpec((HD, W), lambda i: (i, 0)),
        out_shape=jax.ShapeDtypeStruct((S, 2 * W), BF16),
        compiler_params=_cp(("parallel",)),
        name=name,
    )(p, p, sg_w, sg_bt, sg_g.reshape(1, W))


def sg_bwd(p, sg_w, sg_bt, sg_g, dcat, W, name):
    S = p.shape[0]
    G, nq = W // HD, S // HD

    def body(u_ref, v_ref, w_ref, bt_ref, g_ref, do_ref, dp_ref, dw_ref, dbt_ref, dg_ref, dvn_scr):
        i = pl.program_id(0)

        @pl.when(i == 0)
        def _():
            dw_ref[...] = jnp.zeros_like(dw_ref)
            dbt_ref[...] = jnp.zeros_like(dbt_ref)
            dg_ref[...] = jnp.zeros_like(dg_ref)

        v = v_ref[...].astype(F32)
        xh, r = _layernorm_parts(_gelu(v))
        gg = g_ref[...]
        vn = (xh * gg).astype(BF16)
        tri = _iota2((HD, HD), 0) >= _iota2((HD, HD), 1)
        for gi in range(G):
            cols = slice(gi * HD, (gi + 1) * HD)
            wt = jnp.where(tri, w_ref[gi], 0.0).astype(BF16)
            mixed = _dot(wt, vn[:, cols]) + bt_ref[:, gi : gi + 1]
            u = u_ref[:, cols].astype(F32)
            do = do_ref[:, cols].astype(F32)
            dp_ref[0, :, cols] = (do * mixed * _gelu_grad(u)).astype(dp_ref.dtype)
            dmix = do * _gelu(u)
            dmb = dmix.astype(BF16)
            dw_ref[gi] += jnp.where(tri, _dot_nt(dmb, vn[:, cols]), 0.0)
            dbt_ref[:, gi : gi + 1] += jnp.sum(dmix, axis=1, keepdims=True)
            dvn_scr[:, cols] = _dot_tn(wt, dmb)
        dvn = dvn_scr[...]
        dg_ref[...] += jnp.sum(dvn * xh, axis=0, keepdims=True)
        dxh = dvn * gg
        dgv = r * (dxh - jnp.mean(dxh, axis=-1, keepdims=True) - xh * jnp.mean(dxh * xh, axis=-1, keepdims=True))
        dp_ref[1] = (dgv * _gelu_grad(v)).astype(dp_ref.dtype)

    return pl.pallas_call(
        body,
        grid=(nq,),
        in_specs=[
            pl.BlockSpec((HD, W), lambda i: (i, 0)),
            pl.BlockSpec((HD, W), lambda i: (i, 1)),
            pl.BlockSpec((G, HD, HD), lambda i: (0, 0, 0)),
            pl.BlockSpec((HD, G), lambda i: (0, 0)),
            pl.BlockSpec((1, W), lambda i: (0, 0)),
            pl.BlockSpec((HD, W), lambda i: (i, 0)),
        ],
        out_specs=[
            pl.BlockSpec((2, HD, W), lambda i: (0, i, 0)),
            pl.BlockSpec((G, HD, HD), lambda i: (0, 0, 0)),
            pl.BlockSpec((HD, G), lambda i: (0, 0)),
            pl.BlockSpec((1, W), lambda i: (0, 0)),
        ],
        out_shape=[
            jax.ShapeDtypeStruct((6, S, W), BF16),
            jax.ShapeDtypeStruct((G, HD, HD), F32),
            jax.ShapeDtypeStruct((HD, G), F32),
            jax.ShapeDtypeStruct((1, W), F32),
        ],
        scratch_shapes=[pltpu.VMEM((HD, W), F32)],
        compiler_params=_cp(("arbitrary",)),
        name=name,
    )(p, p, sg_w, sg_bt, sg_g.reshape(1, W), dcat)


def local_step(x, target, wts, at, on_grad):
    S, D = x.shape
    W = D // 2
    nb, F = wts["nb"], wts["F"]
    g = {}

    def ffn_fwd(xin, l):
        h = rms_fwd(xin, wts[f"{l}_ffn_norm_g"], f"{l}_ffn_rms")
        u = mm_nn(h, wts[f"{l}_ffn_up"], nb, f"{l}_ffn_up_mm")
        act = ffn_act_fwd(u, wts[f"{l}_ffn_conv_w"], F, f"{l}_ffn_act")
        half_tile = _pick(S, (512, 256, 128))
        xout = mm_nn(act, wts[f"{l}_ffn_down"], 1, f"{l}_ffn_down_mm", out_dtype=F32, res=xin,
                     tm=half_tile, tn=_pick(D, (512, 256, 128)), tk=F)
        return xout, (xin, h, u, act)

    def ffn_bwd(dxout, dxoutb, saved, l):
        xin, h, u, act = saved
        dact = mm_nt(dxoutb, wts[f"{l}_ffn_down"], 1, S, F, f"{l}_ffn_down_dx", tko=_pick(F, (512, 256, 128)), tn=D)
        dact = on_grad(f"{l}_ffn_down", mm_tn(act, dxoutb, 1, D, f"{l}_ffn_down_dw", tn=D), dact)
        du, dcw = ffn_act_bwd(u, wts[f"{l}_ffn_conv_w"], dact, F, f"{l}_ffn_act_bwd")
        g[f"{l}_ffn_conv_w"] = jnp.concatenate([dcw[0], dcw[1]], axis=1)
        du2 = du.reshape(2 * S, F)
        n = wts[f"{l}_ffn_up"].shape[1]
        tn = _pick(n, (1408, 1024, 768, 512, 256, 128))
        per_half = F // tn
        nt = n // tn

        def up_block(i, j, t):
            vb = j * nt + t
            return vb // per_half, vb % per_half

        tm = _pick(S, (1024, 512, 256, 128))

        def nt_map(i, j, t):
            half, cb = up_block(i, j, t)
            return (half * (S // tm) + i, cb)

        def tn_map(j, t):
            half, cb = up_block(0, j, t)
            return (half, cb)

        dh = mm_nt(du2, wts[f"{l}_ffn_up"], nb, S, D, f"{l}_ffn_up_dx", dy_maps=[nt_map], tm=tm, tko=D, tn=tn)
        dh = on_grad(f"{l}_ffn_up", mm_tn(h, du2, nb, n, f"{l}_ffn_up_dw", dy_maps=[tn_map], tko=_pick(D, (1024, 512, 256, 128)), tn=tn), dh)
        dxin, dxinb, dg = rms_bwd(xin, wts[f"{l}_ffn_norm_g"], dh, dxout, f"{l}_ffn_rms_bwd")
        g[f"{l}_ffn_norm_g"] = dg
        return dxin, dxinb

    h0 = rms_fwd(x, wts["l0_mix_norm_g"], "l0_mix_rms")
    p0 = mm_nn(h0, wts["l0_w_in"], nb, "l0_w_in_mm")
    cat0 = sb_fwd(p0, W, "l0_sb_fwd")
    cat0 = sc_fwd(p0, wts["l0_sc_conv_w"], cat0, W, "l0_sc_fwd")
    x1 = mm_nn(cat0, wts["l0_w_out"], 1, "l0_w_out_mm", out_dtype=F32, res=x, tm=S, tn=_pick(D, (512, 256, 128)))
    x2, ffn0_saved = ffn_fwd(x1, "l0")

    x2 = at("l1_w_in", x2, None)
    nh = W // HD
    h2 = rms_fwd(x2, wts["l1_mix_norm_g"], "l1_mix_rms")
    p1 = mm_nt(h2, wts["l1_w_in_t"], 1, S, 5 * W, "l1_w_in_mm", tn=D)
    f = mm_nt(h2, wts["l1_w_f_t"], 1, S, 128, "l1_w_f_mm", out_dtype=F32, tn=D)
    bf = jnp.zeros((1, 128), F32).at[0, :nh].set(wts["l1_fox_b_f"])
    c = fox_gate_fwd(f, bf, "l1_fox_gate")
    c_heads = c[:, :nh].T
    ccol = c_heads[:, :, None]
    crow = c_heads.reshape(nh, S // _key_strip(S), _key_strip(S))
    sg_bt = wts["l1_sg_b"].T
    cat1 = sg_fwd(p1, wts["l1_sg_w"], sg_bt, wts["l1_sg_norm_g"], W, "l1_sg_fwd")
    cat1, lse = fox_fwd(p1, ccol, crow, cat1, W, "l1_fox_fwd")
    x3 = mm_nn(cat1, wts["l1_w_out"], 1, "l1_w_out_mm", out_dtype=F32, res=x2, tm=S, tn=_pick(D, (512, 256, 128)))
    x4, ffn1_saved = ffn_fwd(x3, "l1")

    dx4, dx4b, dgf, loss = loss_head(x4, wts["final_norm_g"], target, "loss_head")
    dx4b = at("loss", dx4b, loss)
    g["final_norm_g"] = dgf

    dx3, dx3b = ffn_bwd(dx4, dx4b, ffn1_saved, "l1")
    dcat1 = mm_nt(dx3b, wts["l1_w_out"], 1, S, D, "l1_w_out_dx", tn=D)
    dcat1 = on_grad("l1_w_out", mm_tn(cat1, dx3b, 1, D, "l1_w_out_dw", tn=D), dcat1)
    dp1, dsgw, dsgbt, dsgg = sg_bwd(p1, wts["l1_sg_w"], sg_bt, wts["l1_sg_norm_g"], dcat1, W, "l1_sg_bwd")
    dp1, dcs, dct = fox_bwd(p1, ccol, crow, cat1, lse, dcat1, dp1, W, "l1_fox_bwd")
    g["l1_sg_w"], g["l1_sg_b"], g["l1_sg_norm_g"] = dsgw, dsgbt.T, dsgg
    dc = jnp.zeros((S, 128), F32).at[:, :nh].set((dct[:, :, 0] - dcs.reshape(nh, S)).T)
    df, dbf = fox_gate_bwd(f, bf, dc, "l1_fox_gate_bwd")
    g["l1_fox_b_f"] = dbf[0, :nh]
    dfb = df.astype(BF16)
    tk1 = _pick(W, (1024, 512, 256, 128))
    tx1 = _pick(W, (512, 256, 128))
    tm1 = _pick(S, (1024, 512, 256, 128))
    part_of = lambda pt: pt + pt // 2 - pt // 4

    def a_map1(i, k):
        return (part_of(k // (W // tk1)) * (S // tm1) + i, k % (W // tk1))

    def x_map1(ko):
        return (part_of(ko // (W // tx1)), ko % (W // tx1))

    dp1_2d = dp1.reshape(6 * S, W)
    dw_main = mm_tn(dp1_2d, h2, 1, D, "l1_w_in_dw", tko=tx1, tn=D, x_map=x_map1, x_shape=(S, 5 * W))
    dw_f = mm_tn(dfb, h2, 1, D, "l1_w_f_dw", tn=D)
    dh2 = mm_nn(dfb, wts["l1_w_f_t"], 1, "l1_w_f_dx", out_dtype=F32)
    dh2 = mm_nn(dp1_2d, wts["l1_w_in_t"], 1, "l1_w_in_dx", res=dh2, tm=tm1, tk=tk1, a_map=a_map1, a_shape=(S, 5 * W))
    dh2 = on_grad("l1_w_in", jnp.concatenate([dw_main, dw_f[:nh]], axis=0), dh2)
    dx2, dx2b, dg = rms_bwd(x2, wts["l1_mix_norm_g"], dh2, dx3, "l1_mix_rms_bwd")
    g["l1_mix_norm_g"] = dg

    dx1, dx1b = ffn_bwd(dx2, dx2b, ffn0_saved, "l0")
    dcat0 = mm_nt(dx1b, wts["l0_w_out"], 1, S, D, "l0_w_out_dx", tn=D)
    dcat0 = on_grad("l0_w_out", mm_tn(cat0, dx1b, 1, D, "l0_w_out_dw", tn=D), dcat0)
    dp0 = sb_bwd(p0, dcat0, W, "l0_sb_bwd")
    dp0, dscw = sc_bwd(p0, wts["l0_sc_conv_w"], dcat0, dp0, W, "l0_sc_bwd")
    g["l0_sc_conv_w"] = dscw
    dp0 = at("small_ready", dp0, g)
    n0 = wts["l0_w_in"].shape[1]
    td0 = math.gcd(n0, W)
    nd0 = n0 // td0
    tm0 = _pick(S, (1024, 512, 256, 128))
    per_part0 = W // td0

    def nt_maps0(k):
        def f(i, j, t):
            vb = j * nd0 + k
            return ((vb // per_part0) * (S // tm0) + i, vb % per_part0)
        return f

    def tn_maps0(k):
        def f(j, t):
            vb = j * nd0 + k
            return (vb // per_part0, vb % per_part0)
        return f

    dp0_2d = dp0.reshape(6 * S, W)
    dw0 = mm_tn(h0, dp0_2d, nb, n0, "l0_w_in_dw", dy_maps=[tn_maps0(k) for k in range(nd0)], tko=_pick(D, (1024, 512, 256, 128)), tn=n0)
    dp0_2d = on_grad("l0_w_in", dw0, dp0_2d)
    dp0_2d = on_grad(None, None, dp0_2d)
    dh0 = mm_nt(dp0_2d, wts["l0_w_in"], nb, S, D, "l0_w_in_dx", dy_maps=[nt_maps0(k) for k in range(nd0)], tm=tm0, tko=D, tn=n0)
    dh0 = at("small_done", dh0, None)
    dx0, _, dg = rms_bwd(x, wts["l0_mix_norm_g"], dh0, dx1, "l0_mix_rms_bwd")
    g["l0_mix_norm_g"] = dg
    return dx0, g


GATHER_ID = 1


def _place():
    return lax.axis_index("x"), lax.axis_index("y"), lax.axis_index("c")


def _other_chips(x, y):
    return [(x, 1 - y), (1 - x, y), (1 - x, 1 - y)]


def _handshake(peers):
    barrier = pltpu.get_barrier_semaphore()
    for peer in peers:
        pl.semaphore_signal(barrier, inc=1, device_id=peer, device_id_type=MESH)
    pl.semaphore_wait(barrier, len(peers))


UPDATE_LAG = 2


def _on_sequencer(body, out_type, scratch_types, collective_id, name):
    return pl.kernel(
        body,
        out_type=out_type,
        mesh=plsc.ScalarSubcoreMesh(axis_name="seq", num_cores=1),
        scratch_types=scratch_types,
        compiler_params=pltpu.CompilerParams(collective_id=collective_id),
        name=name,
    )


def all_gather(arrs, name):
    n = len(arrs)

    def body(*refs):
        xs, outs = refs[:n], refs[n : 2 * n]
        send_sems, recv_sems, local_sems = refs[2 * n :]
        x, y, c = _place()
        me, sibling = (x, y, c), (x, y, 1 - c)
        chips = _other_chips(x, y)
        _handshake([sibling] + [(*chip, c) for chip in chips])

        def copy(a, k, block, to, src=None):
            px, py, pc = block
            dst = outs[a].at[4 * px + 2 * py + pc]
            return pltpu.make_async_remote_copy(
                src_ref=dst if src is None else src, dst_ref=dst,
                send_sem=send_sems.at[7 * a + k], recv_sem=recv_sems.at[7 * a + k], device_id=to, device_id_type=MESH,
            )

        mine = [pltpu.make_async_copy(xs[a], outs[a].at[4 * x + 2 * y + c], local_sems.at[a]) for a in range(n)]
        for cp in mine:
            cp.start()
        first = []
        for a in range(n):
            first.append(copy(a, 0, me, sibling, src=xs[a]))
            first += [copy(a, 1 + j, me, (*chip, c), src=xs[a]) for j, chip in enumerate(chips)]
        for cp in first:
            cp.start()
        passed = []
        for a in range(n):
            for j, chip in enumerate(chips):
                copy(a, 1 + j, (*chip, c), me).wait_recv()
                cp = copy(a, 4 + j, (*chip, c), sibling)
                cp.start()
                passed.append(cp)
        for a in range(n):
            copy(a, 0, sibling, me).wait_recv()
            for j, chip in enumerate(chips):
                copy(a, 4 + j, (*chip, 1 - c), me).wait_recv()
        for cp in first + passed:
            cp.wait_send()
        for cp in mine:
            cp.wait()

    out_type = [jax.ShapeDtypeStruct((NDEV,) + a.shape, a.dtype) for a in arrs]
    sems = [pltpu.SemaphoreType.DMA((7 * n,)), pltpu.SemaphoreType.DMA((7 * n,)), pltpu.SemaphoreType.DMA((n,))]
    return _on_sequencer(body, out_type, sems, GATHER_ID, name)(*arrs)


_IN_HBM = pl.BlockSpec(memory_space=pltpu.HBM)
_IN_SEM = pl.BlockSpec(memory_space=pltpu.SEMAPHORE)
_EFFECT = pltpu.SideEffectType.DATAFLOW_SIDE_EFFECTING


def _split_start(make_copies, src, land_shape, nsem, name):
    def body(src_ref, land_ref, send_sems, recv_sems, land_thru, token):
        for cp in make_copies(src_ref, land_ref, send_sems, recv_sems):
            cp.start()
        token[...] = jnp.zeros_like(token)

    send_sems, recv_sems, land_thru, token = pl.pallas_call(
        body,
        name=name,
        out_shape=(
            pltpu.SemaphoreType.DMA((nsem,)), pltpu.SemaphoreType.DMA((nsem,)),
            pltpu.HBM(land_shape, src.dtype), jax.ShapeDtypeStruct((8, 128), F32),
        ),
        in_specs=(_IN_HBM, _IN_HBM),
        out_specs=(_IN_SEM, _IN_SEM, _IN_HBM, pl.BlockSpec(memory_space=pltpu.VMEM)),
        input_output_aliases={1: 2},
        compiler_params=pltpu.CompilerParams(has_side_effects=_EFFECT),
    )(src, pltpu.with_memory_space_constraint(lax.empty(land_shape, src.dtype), pltpu.HBM))
    return send_sems, recv_sems, src, land_thru, token


def _split_wait(make_copies, send_sems, recv_sems, src_thru, land_thru, after, name):
    def body(src_ref, land_ref, send_sems, recv_sems, after_ref, land_out):
        for cp in make_copies(src_ref, land_ref, send_sems, recv_sems):
            cp.wait_send()
            cp.wait_recv()

    return pl.pallas_call(
        body,
        name=name,
        out_shape=pltpu.HBM(land_thru.shape, land_thru.dtype),
        in_specs=(_IN_HBM, _IN_HBM, _IN_SEM, _IN_SEM, pl.BlockSpec(memory_space=pl.ANY)),
        out_specs=_IN_HBM,
        input_output_aliases={1: 0},
        compiler_params=pltpu.CompilerParams(has_side_effects=_EFFECT),
    )(src_thru, land_thru, send_sems, recv_sems, after)


def _pair_copies(src_ref, land_ref, send_sems, recv_sems):
    x, y, c = _place()
    return [
        pltpu.make_async_remote_copy(
            src_ref=src_ref.at[k, 1 - c], dst_ref=land_ref.at[k],
            send_sem=send_sems.at[k], recv_sem=recv_sems.at[k], device_id=(x, y, 1 - c), device_id_type=MESH,
        )
        for k in range(4)
    ]


def _direct_copies(src_ref, land_ref, send_sems, recv_sems):
    x, y, c = _place()
    me = 4 * x + 2 * y + c
    copies = []
    for k in range(NDEV - 1):
        to = (me + k + 1) % NDEV
        copies.append(pltpu.make_async_remote_copy(
            src_ref=src_ref, dst_ref=land_ref.at[me], send_sem=send_sems.at[k], recv_sem=recv_sems.at[k],
            device_id=(to // 4, (to // 2) % 2, to % 2), device_id_type=MESH,
        ))
    return copies


def _chip_copies(src_ref, land_ref, send_sems, recv_sems):
    x, y, c = _place()
    return [
        pltpu.make_async_remote_copy(
            src_ref=src_ref.at[2 * px + py], dst_ref=land_ref.at[2 * x + y],
            send_sem=send_sems.at[j], recv_sem=recv_sems.at[j], device_id=(px, py, c), device_id_type=MESH,
        )
        for j, (px, py) in enumerate(_other_chips(x, y))
    ]


def _row_tile(R, C, max_elems):
    if R * C <= max_elems:
        return R
    best = None
    for tr in range(16, R, 16):
        if R % tr == 0 and tr * C <= max_elems:
            best = tr
    return best or R


def pair_sum(a42, land4, core, name):
    _, _, R, C = a42.shape
    tr = _row_tile(R, C, 1 << 20)

    def body(core_ref, a_ref, l_ref, o_ref):
        o_ref[...] = (a_ref[0].astype(F32) + l_ref[...].astype(F32)).astype(o_ref.dtype)

    return pl.pallas_call(
        body,
        grid_spec=pltpu.PrefetchScalarGridSpec(
            num_scalar_prefetch=1,
            grid=(4, R // tr),
            in_specs=[
                pl.BlockSpec((1, 1, tr, C), lambda k, r, core_ref: (k, core_ref[0], r, 0)),
                pl.BlockSpec((1, tr, C), lambda k, r, core_ref: (k, r, 0)),
            ],
            out_specs=pl.BlockSpec((1, tr, C), lambda k, r, core_ref: (k, r, 0)),
        ),
        out_shape=jax.ShapeDtypeStruct((4, R, C), BF16),
        compiler_params=_cp(("parallel", "parallel")),
        name=name,
    )(core, a42, land4)


def sum_slots(parts, name):
    P, R, C = parts.shape

    def body(p_ref, o_ref):
        acc = p_ref[0].astype(F32)
        for k in range(1, P):
            acc = acc + p_ref[k].astype(F32)
        o_ref[...] = acc

    tr = _row_tile(R, P * C, 1 << 21)
    return pl.pallas_call(
        body,
        grid=(R // tr,),
        in_specs=[pl.BlockSpec((P, tr, C), lambda r: (0, r, 0))],
        out_specs=pl.BlockSpec((tr, C), lambda r: (r, 0)),
        out_shape=jax.ShapeDtypeStruct((R, C), F32),
        compiler_params=_cp(("parallel",)),
        name=name,
    )(parts)


def adamw(w, m, v, parts, name):
    R, C = w.shape
    P = parts.shape[0]
    tr = _pick(R, (256, 128, 64, 32, 16, 8))
    c1 = 1.0 - ADAM_B1 ** ADAM_STEP
    c2 = 1.0 - ADAM_B2 ** ADAM_STEP

    def body(w_ref, m_ref, v_ref, p_ref, g_ref, d_ref, nm_ref, nv_ref):
        g = p_ref[0].astype(F32)
        for k in range(1, P):
            g = g + p_ref[k].astype(F32)
        nm = ADAM_B1 * m_ref[...] + (1.0 - ADAM_B1) * g
        nv = ADAM_B2 * v_ref[...] + (1.0 - ADAM_B2) * (g * g)
        g_ref[...] = g
        nm_ref[...] = nm
        nv_ref[...] = nv
        d_ref[...] = -ADAM_LR * ((nm / c1) / (jnp.sqrt(nv / c2) + ADAM_EPS) + ADAM_WD * w_ref[...])

    blk = pl.BlockSpec((tr, C), lambda r: (r, 0))
    shp = jax.ShapeDtypeStruct((R, C), F32)
    return pl.pallas_call(
        body,
        grid=(R // tr,),
        in_specs=[blk, blk, blk, pl.BlockSpec((P, tr, C), lambda r: (0, r, 0))],
        out_specs=[blk, blk, blk, blk],
        out_shape=[shp, shp, shp, shp],
        compiler_params=_cp(("parallel",)),
        name=name,
    )(w, m, v, parts)


def adamw_reduced(w, m, v, own, land, chip, name):
    R, C = w.shape
    if R % 8 == 0:
        tr, tc = _pick(R, (256, 128, 64, 32, 16, 8)), C
    else:
        tr, tc = R, _pick(C, (256, 128))
    c1 = 1.0 - ADAM_B1 ** ADAM_STEP
    c2 = 1.0 - ADAM_B2 ** ADAM_STEP

    def body(chip_ref, w_ref, m_ref, v_ref, own_ref, land_ref, g_ref, d_ref, nm_ref, nv_ref):
        mine = own_ref[0].astype(F32)
        g = None
        for k in range(4):
            term = jnp.where(chip_ref[0] == k, mine, land_ref[k].astype(F32))
            g = term if g is None else g + term
        nm = ADAM_B1 * m_ref[...] + (1.0 - ADAM_B1) * g
        nv = ADAM_B2 * v_ref[...] + (1.0 - ADAM_B2) * (g * g)
        g_ref[...] = g
        nm_ref[...] = nm
        nv_ref[...] = nv
        d_ref[...] = -ADAM_LR * ((nm / c1) / (jnp.sqrt(nv / c2) + ADAM_EPS) + ADAM_WD * w_ref[...])

    blk = pl.BlockSpec((tr, tc), lambda r, c, chip_ref: (r, c))
    shp = jax.ShapeDtypeStruct((R, C), F32)
    return pl.pallas_call(
        body,
        grid_spec=pltpu.PrefetchScalarGridSpec(
            num_scalar_prefetch=1,
            grid=(R // tr, C // tc),
            in_specs=[
                blk, blk, blk,
                pl.BlockSpec((1, tr, tc), lambda r, c, chip_ref: (chip_ref[0], r, c)),
                pl.BlockSpec((4, tr, tc), lambda r, c, chip_ref: (0, r, c)),
            ],
            out_specs=[blk, blk, blk, blk],
        ),
        out_shape=[shp, shp, shp, shp],
        compiler_params=_cp(("parallel", "parallel")),
        name=name,
    )(chip, w, m, v, own, land)


_WEIGHTS = [
    "l0_mix_norm_g", "l0_w_in", "l0_sc_conv_w", "l0_w_out", "l0_ffn_norm_g", "l0_ffn_up", "l0_ffn_conv_w", "l0_ffn_down",
    "l1_mix_norm_g", "l1_w_in", "l1_fox_b_f", "l1_sg_w", "l1_sg_b", "l1_sg_norm_g", "l1_w_out", "l1_ffn_norm_g",
    "l1_ffn_up", "l1_ffn_conv_w", "l1_ffn_down", "final_norm_g",
]
_ROW_SHARDED = ["l0_w_out", "l0_ffn_down", "l1_w_out", "l1_ffn_down"]
_BIG = ["l0_w_in", "l0_w_out", "l0_ffn_up", "l0_ffn_down", "l1_w_in", "l1_w_out", "l1_ffn_up", "l1_ffn_down"]
_CONV = ["l0_sc_conv_w", "l0_ffn_conv_w", "l1_ffn_conv_w"]
_SMALL = [n for n in _WEIGHTS if n not in _BIG]
_LAST_SMALL = "l0_mix_norm_g"
_PACK_ROWS = 8


def _pack(arrs):
    flat = []
    for a in arrs:
        v = a.reshape(-1).astype(F32)
        pad = (-v.shape[0]) % (_PACK_ROWS * 128)
        flat.append(jnp.pad(v, (0, pad)))
    return jnp.concatenate(flat).reshape(-1, 128)


def _unpack(packed, shapes):
    out, off = [], 0
    flat = packed.reshape(-1)
    for shp in shapes:
        size = math.prod(shp)
        out.append(flat[off : off + size].reshape(shp))
        off += size + (-size) % (_PACK_ROWS * 128)
    return out


def kernel(x, l0_mix_norm_g, l0_w_in, l0_sc_conv_w, l0_w_out, l0_ffn_norm_g, l0_ffn_up, l0_ffn_conv_w, l0_ffn_down, l1_mix_norm_g, l1_w_in, l1_fox_b_f, l1_sg_w, l1_sg_b, l1_sg_norm_g, l1_w_out, l1_ffn_norm_g, l1_ffn_up, l1_ffn_conv_w, l1_ffn_down, final_norm_g, loss_target, m_l0_mix_norm_g, m_l0_w_in, m_l0_sc_conv_w, m_l0_w_out, m_l0_ffn_norm_g, m_l0_ffn_up, m_l0_ffn_conv_w, m_l0_ffn_down, m_l1_mix_norm_g, m_l1_w_in, m_l1_fox_b_f, m_l1_sg_w, m_l1_sg_b, m_l1_sg_norm_g, m_l1_w_out, m_l1_ffn_norm_g, m_l1_ffn_up, m_l1_ffn_conv_w, m_l1_ffn_down, m_final_norm_g, v_l0_mix_norm_g, v_l0_w_in, v_l0_sc_conv_w, v_l0_w_out, v_l0_ffn_norm_g, v_l0_ffn_up, v_l0_ffn_conv_w, v_l0_ffn_down, v_l1_mix_norm_g, v_l1_w_in, v_l1_fox_b_f, v_l1_sg_w, v_l1_sg_b, v_l1_sg_norm_g, v_l1_w_out, v_l1_ffn_norm_g, v_l1_ffn_up, v_l1_ffn_conv_w, v_l1_ffn_down, v_final_norm_g):
    given = dict(locals())
    w = {n: given[n] for n in _WEIGHTS}
    mom = {n: given["m_" + n] for n in _WEIGHTS}
    var = {n: given["v_" + n] for n in _WEIGHTS}
    xs, target = x[0], loss_target[0]
    S, D = xs.shape
    W = D // 2
    nh = W // HD
    cx, cy, cc = _place()
    me = 4 * cx + 2 * cy + cc

    wts = {"nb": NDEV, "F": l0_ffn_down.shape[0] * NDEV}
    for n in _SMALL:
        if n not in _CONV:
            wts[n] = w[n]
    gathered, loss_sum = {}, []

    def start_gather(names):
        srcs = [(w[n].T if n == "l1_w_in" else w[n]).astype(BF16) for n in names]
        taps = [w[c] for c in _CONV] if names[0] == _BIG[0] else []
        got = all_gather(srcs + taps, "gather_" + "_".join(names))
        for n, full in zip(names, got):
            if n == "l1_w_in":
                gathered[n] = full
            elif n in _ROW_SHARDED:
                wts[n] = full.reshape(-1, D)
            else:
                wts[n] = full.reshape(NDEV * D, -1)
        for c, full in zip(_CONV, got[len(names):] if taps else []):
            wts[c] = full.transpose(1, 0, 2).reshape(CONV_K, -1)

    def at(point, after, value):
        if point == "l1_w_in":
            got, after = lax.optimization_barrier((gathered[point], after))
            wts["l1_w_in_t"] = got.reshape(-1, D)
            wts["l1_w_f_t"] = jnp.pad(wts["l1_w_in_t"][5 * W :], ((0, 128 - nh), (0, 0)))
        elif point == "loss":
            gathered["loss"] = value[0, :1]
        elif point == "small_ready":
            early = [n for n in _SMALL if n != _LAST_SMALL]
            gathered["small"] = all_gather([_pack([value[n] for n in early] + [gathered["loss"]])], "gather_small_grads")[0]
        elif point == "small_done":
            after = update_small([n for n in _SMALL if n != _LAST_SMALL], gathered["small"], "small", after, True)
        return after

    out_g, out_d, out_m, out_v = {}, {}, {}, {}

    def update_small(names, all_terms, tag, after=None, with_loss=False):
        shapes = [w[n].shape for n in names]
        full_shapes = [(CONV_K, NDEV * w[n].shape[1]) if n in _CONV else w[n].shape for n in names]
        summed = _unpack(sum_slots(all_terms, f"sum_{tag}_grads"), full_shapes + ([(1,)] if with_loss else []))
        if with_loss:
            loss_sum.append(summed[-1][0])
        grads = {}
        for n, t in zip(names, summed):
            if n in _CONV:
                cols = w[n].shape[1]
                t = lax.dynamic_slice_in_dim(t, me * cols, cols, axis=1)
            grads[n] = t
        res = adamw(
            _pack([w[n] for n in names]), _pack([mom[n] for n in names]), _pack([var[n] for n in names]),
            _pack([grads[n] for n in names])[None], f"adamw_{tag}",
        )
        if after is not None:
            res, after = lax.optimization_barrier((res, after))
        for dst, packed_out in zip((out_g, out_d, out_m, out_v), res):
            for n, t in zip(names, _unpack(packed_out, shapes)):
                dst[n] = t
        return after

    core = jnp.reshape(cc, (1,)).astype(jnp.int32)
    chip = jnp.reshape(2 * cx + cy, (1,)).astype(jnp.int32)
    pair_flying, chip_flying = [], []

    def tie(value, after):
        if after is None:
            return value, None
        return lax.optimization_barrier((value, after))

    def to_chips(after):
        n, flying = pair_flying.pop()
        landed = _split_wait(_pair_copies, *flying, f"reduce_pair_wait_{n}")
        summed = pair_sum(flying[2], landed, core, f"pair_sum_{n}")
        *flying, token = _split_start(_chip_copies, summed, summed.shape, 3, f"reduce_chips_{n}")
        token, after = tie(token, after)
        chip_flying.append((n, flying + [token]))
        return after

    def update(after, behind=None):
        n, flying = chip_flying.pop(0)
        if behind is not None:
            flying[4], _ = lax.optimization_barrier((flying[4], behind))
        landed = _split_wait(_chip_copies, *flying, f"reduce_chips_wait_{n}")
        turn = (lambda t: t.T) if n == "l1_w_in" else (lambda t: t)
        res = adamw_reduced(turn(w[n]), turn(mom[n]), turn(var[n]), flying[2], landed, chip, f"adamw_{n}")
        res, after = tie(res, after)
        out_g[n], out_d[n], out_m[n], out_v[n] = [turn(t) for t in res]
        return after, res[0]

    def on_grad(n, term, after):
        if n is None:
            return to_chips(after)
        if n in _ROW_SHARDED or n == "l1_w_in":
            term = term.reshape(NDEV, -1, D)
        else:
            term = term.reshape(NDEV, D, -1)
        term = term.reshape((4, 2) + term.shape[1:])
        *flying, token = _split_start(_pair_copies, term, term.shape[:1] + term.shape[2:], 4, f"reduce_pair_{n}")
        token, after = tie(token, after)
        if len(chip_flying) == UPDATE_LAG:
            after, _ = update(after)
        if pair_flying:
            after = to_chips(after)
        pair_flying.append((n, flying + [token]))
        return after

    for n in _BIG:
        start_gather([n])
    dx, g = local_step(xs, target, wts, at, on_grad)
    last = _pack([g[_LAST_SMALL]])
    *flying, done = _split_start(_direct_copies, last, (NDEV,) + last.shape, NDEV - 1, "gather_last_grad")
    while len(chip_flying) > 1:
        _, done = update(None, behind=done)
    landed = _split_wait(_direct_copies, *flying, done, "gather_last_grad_wait")
    update_small([_LAST_SMALL], lax.dynamic_update_slice(landed, last[None], (me, 0, 0)), "last")
    update(None, behind=out_g[_LAST_SMALL])
    loss = loss_sum[0]

    return (loss, dx[None], *[out_g[n] for n in _WEIGHTS], *[out_d[n] for n in _WEIGHTS],
            *[out_m[n] for n in _WEIGHTS], *[out_v[n] for n in _WEIGHTS])
```

```python
import functools
import math

import jax
import jax.numpy as jnp
from jax import lax
from jax.experimental import pallas as pl
from jax.experimental.pallas import tpu as pltpu
from jax.experimental.pallas import tpu_sc as plsc

F32 = jnp.float32
BF16 = jnp.bfloat16
HD = 128
EPS = 1e-6
CONV_K = 3
VMEM_LIMIT_BYTES = 48 << 20
NDEV = 8
MESH = pl.DeviceIdType.MESH

ADAM_LR = 0.001
ADAM_B1 = 0.9
ADAM_B2 = 0.999
ADAM_EPS = 1e-08
ADAM_WD = 0.01
ADAM_STEP = 10


def _cp(sem):
    return pltpu.CompilerParams(dimension_semantics=sem, vmem_limit_bytes=VMEM_LIMIT_BYTES)


def _pick(n, prefs):
    for p in prefs:
        if n % p == 0:
            return p
    return n


def _dot(a, b):
    return jnp.dot(a, b, preferred_element_type=F32)


def _dot_nt(a, b):
    return lax.dot_general(a, b, (((1,), (1,)), ((), ())), preferred_element_type=F32)


def _dot_tn(a, b):
    return lax.dot_general(a, b, (((0,), (0,)), ((), ())), preferred_element_type=F32)


def _split3(x):
    hi = x.astype(BF16)
    r = x - hi.astype(F32)
    mid = r.astype(BF16)
    lo = (r - mid.astype(F32)).astype(BF16)
    return hi, mid, lo


def _dot_ones_left(ones_bf16, x):
    hi, mid, lo = _split3(x)
    return _dot(ones_bf16, hi) + _dot(ones_bf16, mid) + _dot(ones_bf16, lo)


def _iota2(shape, axis):
    return lax.broadcasted_iota(jnp.int32, shape, axis)


def mm_nn(a, w2d, nb, name, out_dtype=BF16, res=None, tm=None, tn=None, tk=None, a_map=None, a_shape=None):
    M, K = a_shape or a.shape
    n = w2d.shape[1]
    assert w2d.shape[0] == nb * K or (nb == 1 and w2d.shape[0] > K)
    a_map = a_map or (lambda i, k: (i, k))
    tm = tm or _pick(M, (1024, 512, 256, 128))
    tn = tn or _pick(n, (1408, 1024, 768, 512, 256, 128))
    tk = tk or (K if K <= 2048 else _pick(K, (1408, 1024, 512, 256, 128)))
    nk, nt = K // tk, n // tn
    has_res = res is not None

    def body(*refs):
        if has_res:
            a_ref, w_ref, r_ref, o_ref = refs[:4]
        else:
            a_ref, w_ref, o_ref = refs[:3]
            r_ref = None
        part = _dot(a_ref[...], w_ref[...])

        def finish(acc):
            if r_ref is not None:
                acc = acc + r_ref[...].astype(F32)
            o_ref[...] = acc.astype(o_ref.dtype)

        if nk == 1:
            finish(part)
        else:
            acc_ref = refs[-1]
            k = pl.program_id(3)

            @pl.when(k == 0)
            def _():
                acc_ref[...] = part

            @pl.when(k > 0)
            def _():
                acc_ref[...] += part

            @pl.when(k == nk - 1)
            def _():
                finish(acc_ref[...])

    in_specs = [
        pl.BlockSpec((tm, tk), lambda i, j, t, k: a_map(i, k)),
        pl.BlockSpec((tk, tn), lambda i, j, t, k: (j * nk + k, t)),
    ]
    args = [a, w2d]
    out_spec = pl.BlockSpec((tm, tn), lambda i, j, t, k: (i, j * nt + t))
    if has_res:
        in_specs.append(out_spec)
        args.append(res)
    return pl.pallas_call(
        body,
        grid=(M // tm, nb, nt, nk),
        in_specs=in_specs,
        out_specs=out_spec,
        out_shape=jax.ShapeDtypeStruct((M, nb * n), out_dtype),
        scratch_shapes=[pltpu.VMEM((tm, tn), F32)] if nk > 1 else [],
        compiler_params=_cp(("parallel", "parallel", "parallel", "arbitrary")),
        name=name,
    )(*args)


def mm_nt(dy2d, w2d, nb, M, K, name, out_dtype=BF16, res=None, dy_maps=None, tm=None, tko=None, tn=None):
    n = w2d.shape[1]
    assert w2d.shape[0] == nb * K or (nb == 1 and w2d.shape[0] > K)
    tm = tm or _pick(M, (1024, 512, 256, 128))
    tko = tko or _pick(K, (1024, 512, 256, 128))
    tn = tn or _pick(n, (1408, 1024, 768, 512, 256, 128))
    nt, nko = n // tn, K // tko
    has_res = res is not None
    if dy_maps is None:
        dy_maps = [lambda i, j, t: (i, j * nt + t)]
    nd = len(dy_maps)
    td = tn // nd

    one_step = nb * nt == 1

    def body(*refs):
        d_refs, w_ref = refs[:nd], refs[nd]
        r_ref = refs[nd + 1] if has_res else None
        d = d_refs[0][...] if nd == 1 else jnp.concatenate([r[...] for r in d_refs], axis=1)
        part = _dot_nt(d, w_ref[...])
        if one_step:
            o_ref = refs[-1]
            if r_ref is not None:
                part = part + r_ref[...].astype(F32)
            o_ref[...] = part.astype(o_ref.dtype)
            return
        o_ref, acc_ref = refs[-2], refs[-1]
        j, t = pl.program_id(2), pl.program_id(3)
        first = jnp.logical_and(j == 0, t == 0)
        last = jnp.logical_and(j == nb - 1, t == nt - 1)

        @pl.when(first)
        def _():
            acc_ref[...] = part

        @pl.when(jnp.logical_not(first))
        def _():
            acc_ref[...] += part

        @pl.when(last)
        def _():
            acc = acc_ref[...]
            if r_ref is not None:
                acc = acc + r_ref[...].astype(F32)
            o_ref[...] = acc.astype(o_ref.dtype)

    in_specs = [pl.BlockSpec((tm, td), functools.partial(lambda f, i, ko, j, t: f(i, j, t), f)) for f in dy_maps]
    in_specs.append(pl.BlockSpec((tko, tn), lambda i, ko, j, t: (j * nko + ko, t)))
    args = [dy2d] * nd + [w2d]
    out_spec = pl.BlockSpec((tm, tko), lambda i, ko, j, t: (i, ko))
    if has_res:
        in_specs.append(out_spec)
        args.append(res)
    return pl.pallas_call(
        body,
        grid=(M // tm, nko, nb, nt),
        in_specs=in_specs,
        out_specs=out_spec,
        out_shape=jax.ShapeDtypeStruct((M, K), out_dtype),
        scratch_shapes=[] if one_step else [pltpu.VMEM((tm, tko), F32)],
        compiler_params=_cp(("parallel", "parallel", "arbitrary", "arbitrary")),
        name=name,
    )(*args)


def mm_tn(x, dy2d, nb, n, name, out_dtype=BF16, dy_maps=None, tko=None, tn=None, x_map=None, x_shape=None):
    S, K = x_shape or x.shape
    x_map = x_map or (lambda ko: (0, ko))
    tko = tko or _pick(K, (512, 256, 128))
    tn = tn or _pick(n, (1408, 1024, 768, 512, 256, 128))
    nt, nko = n // tn, K // tko
    if dy_maps is None:
        dy_maps = [lambda j, t: (0, j * nt + t)]
    nd = len(dy_maps)
    td = tn // nd

    def body(*refs):
        x_ref, d_refs, o_ref = refs[0], refs[1 : 1 + nd], refs[-1]
        d = d_refs[0][...] if nd == 1 else jnp.concatenate([r[...] for r in d_refs], axis=1)
        o_ref[...] = _dot_tn(x_ref[...], d).astype(o_ref.dtype)

    in_specs = [pl.BlockSpec((S, tko), lambda ko, j, t: x_map(ko))]
    in_specs += [pl.BlockSpec((S, td), functools.partial(lambda f, ko, j, t: f(j, t), f)) for f in dy_maps]
    return pl.pallas_call(
        body,
        grid=(nko, nb, nt),
        in_specs=in_specs,
        out_specs=pl.BlockSpec((tko, tn), lambda ko, j, t: (j * nko + ko, t)),
        out_shape=jax.ShapeDtypeStruct((nb * K, n), out_dtype),
        compiler_params=_cp(("parallel", "parallel", "parallel")),
        name=name,
    )(x, *([dy2d] * nd))


def rms_fwd(x, g, name):
    S, D = x.shape
    tm = _pick(S, (512, 256, 128))

    def body(x_ref, g_ref, o_ref):
        xf = x_ref[...]
        r = lax.rsqrt(jnp.mean(xf * xf, axis=-1, keepdims=True) + EPS)
        o_ref[...] = (xf * r * g_ref[...]).astype(o_ref.dtype)

    return pl.pallas_call(
        body,
        grid=(S // tm,),
        in_specs=[pl.BlockSpec((tm, D), lambda i: (i, 0)), pl.BlockSpec((1, D), lambda i: (0, 0))],
        out_specs=pl.BlockSpec((tm, D), lambda i: (i, 0)),
        out_shape=jax.ShapeDtypeStruct((S, D), BF16),
        compiler_params=_cp(("parallel",)),
        name=name,
    )(x, g.reshape(1, D))


def rms_bwd(x, g, dh, dres, name):
    S, D = x.shape
    tm = _pick(S, (256, 128))

    def body(x_ref, g_ref, dh_ref, dr_ref, dx_ref, dxb_ref, dg_ref):
        i = pl.program_id(0)
        xf = x_ref[...]
        dh = dh_ref[...].astype(F32)
        r = lax.rsqrt(jnp.mean(xf * xf, axis=-1, keepdims=True) + EPS)
        gy = dh * g_ref[...]
        proj = jnp.mean(gy * xf, axis=-1, keepdims=True)
        dx = dr_ref[...] + r * gy - xf * (r * r * r * proj)
        dx_ref[...] = dx
        dxb_ref[...] = dx.astype(BF16)
        dg = jnp.sum(dh * (xf * r), axis=0, keepdims=True)

        @pl.when(i == 0)
        def _():
            dg_ref[...] = dg

        @pl.when(i > 0)
        def _():
            dg_ref[...] += dg

    row = pl.BlockSpec((tm, D), lambda i: (i, 0))
    vec = pl.BlockSpec((1, D), lambda i: (0, 0))
    return pl.pallas_call(
        body,
        grid=(S // tm,),
        in_specs=[row, vec, row, row],
        out_specs=[row, row, vec],
        out_shape=[jax.ShapeDtypeStruct((S, D), F32), jax.ShapeDtypeStruct((S, D), BF16), jax.ShapeDtypeStruct((1, D), F32)],
        compiler_params=_cp(("arbitrary",)),
        name=name,
    )(x, g.reshape(1, D), dh, dres)


def loss_head(x, g, target, name):
    S, D = x.shape
    tm = _pick(S, (256, 128))

    def body(x_ref, g_ref, t_ref, dx_ref, dxb_ref, dg_ref, loss_ref):
        i = pl.program_id(0)
        xf = x_ref[...]
        gg = g_ref[...]
        r = lax.rsqrt(jnp.mean(xf * xf, axis=-1, keepdims=True) + EPS)
        xh = xf * r
        err = xh * gg - t_ref[...]
        part = (0.5 / D) * jnp.sum(err * err)
        dy = err * (1.0 / D)
        gy = dy * gg
        proj = jnp.mean(gy * xf, axis=-1, keepdims=True)
        dx = r * gy - xf * (r * r * r * proj)
        dx_ref[...] = dx
        dxb_ref[...] = dx.astype(BF16)
        dg = jnp.sum(dy * xh, axis=0, keepdims=True)
        lossb = jnp.full(loss_ref.shape, part, F32)

        @pl.when(i == 0)
        def _():
            dg_ref[...] = dg
            loss_ref[...] = lossb

        @pl.when(i > 0)
        def _():
            dg_ref[...] += dg
            loss_ref[...] += lossb

    row = pl.BlockSpec((tm, D), lambda i: (i, 0))
    vec = pl.BlockSpec((1, D), lambda i: (0, 0))
    return pl.pallas_call(
        body,
        grid=(S // tm,),
        in_specs=[row, vec, row],
        out_specs=[row, row, vec, pl.BlockSpec((8, 128), lambda i: (0, 0))],
        out_shape=[
            jax.ShapeDtypeStruct((S, D), F32),
            jax.ShapeDtypeStruct((S, D), BF16),
            jax.ShapeDtypeStruct((1, D), F32),
            jax.ShapeDtypeStruct((8, 128), F32),
        ],
        compiler_params=_cp(("arbitrary",)),
        name=name,
    )(x, g.reshape(1, D), target)


def _shift_down(s, k):
    if k == 0:
        return s
    return jnp.where(_iota2(s.shape, 0) >= k, pltpu.roll(s, k, axis=0), 0.0)


def _shift_up(s, k):
    if k == 0:
        return s
    n = s.shape[0]
    return jnp.where(_iota2(s.shape, 0) < n - k, pltpu.roll(s, n - k, axis=0), 0.0)


def _conv(s, w):
    return w[0:1] * _shift_down(s, 2) + w[1:2] * _shift_down(s, 1) + w[2:3] * s


def _conv_t(d, w):
    return w[2:3] * d + w[1:2] * _shift_up(d, 1) + w[0:1] * _shift_up(d, 2)


def _conv_dw(d, s):
    return [jnp.sum(d * _shift_down(s, CONV_K - 1 - k), axis=0, keepdims=True) for k in range(CONV_K)]


def sc_fwd(p, convw, cat, W, name):
    S = p.shape[0]
    tc = _pick(W, (256, 128))
    nc = W // tc

    def body(gb_ref, gc_ref, hi_ref, w_ref, cat_ref, o_ref):
        s = gc_ref[...].astype(F32) * hi_ref[...].astype(F32)
        o_ref[...] = (gb_ref[...].astype(F32) * _conv(s, w_ref[...])).astype(o_ref.dtype)

    col = lambda part: pl.BlockSpec((S, tc), lambda c: (0, part * nc + c))
    return pl.pallas_call(
        body,
        grid=(nc,),
        in_specs=[col(3), col(4), col(5), pl.BlockSpec((CONV_K, tc), lambda c: (0, c)), pl.BlockSpec(memory_space=pl.ANY)],
        out_specs=col(1),
        out_shape=jax.ShapeDtypeStruct(cat.shape, cat.dtype),
        input_output_aliases={4: 0},
        compiler_params=_cp(("parallel",)),
        name=name,
    )(p, p, p, convw, cat)


def sc_bwd(p, convw, dcat, dp, W, name):
    S = p.shape[0]
    tc = _pick(W, (256, 128))
    nc = W // tc

    def body(gb_ref, gc_ref, hi_ref, w_ref, do_ref, dp_in_ref, dp_ref, dw_ref):
        gb = gb_ref[...].astype(F32)
        gc = gc_ref[...].astype(F32)
        hi = hi_ref[...].astype(F32)
        w = w_ref[...]
        do = do_ref[...].astype(F32)
        s = gc * hi
        dcs = do * gb
        ds = _conv_t(dcs, w)
        dp_ref[0] = (do * _conv(s, w)).astype(dp_ref.dtype)
        dp_ref[1] = (ds * hi).astype(dp_ref.dtype)
        dp_ref[2] = (ds * gc).astype(dp_ref.dtype)
        for k, row in enumerate(_conv_dw(dcs, s)):
            dw_ref[k : k + 1, :] = row

    col = lambda part: pl.BlockSpec((S, tc), lambda c: (0, part * nc + c))
    return pl.pallas_call(
        body,
        grid=(nc,),
        in_specs=[
            col(3), col(4), col(5),
            pl.BlockSpec((CONV_K, tc), lambda c: (0, c)),
            pl.BlockSpec((S, tc), lambda c: (0, nc + c)),
            pl.BlockSpec(memory_space=pl.ANY),
        ],
        out_specs=[pl.BlockSpec((3, S, tc), lambda c: (1, 0, c)), pl.BlockSpec((CONV_K, tc), lambda c: (0, c))],
        out_shape=[jax.ShapeDtypeStruct(dp.shape, dp.dtype), jax.ShapeDtypeStruct((CONV_K, W), F32)],
        input_output_aliases={5: 0},
        compiler_params=_cp(("parallel",)),
        name=name,
    )(p, p, p, convw, dcat, dp)


def _silu_parts(a):
    sig = 1.0 / (1.0 + jnp.exp(-a))
    return a * sig, sig


def ffn_act_fwd(u, convw, F, name):
    S = u.shape[0]
    tc = _pick(F, (256, 128))
    nc = F // tc

    def body(ug_ref, uu_ref, wg_ref, wu_ref, o_ref):
        ag = _conv(ug_ref[...].astype(F32), wg_ref[...])
        au = _conv(uu_ref[...].astype(F32), wu_ref[...])
        o_ref[...] = (_silu_parts(ag)[0] * au).astype(o_ref.dtype)

    col = lambda half: pl.BlockSpec((S, tc), lambda c: (0, half * nc + c))
    wcol = lambda half: pl.BlockSpec((CONV_K, tc), lambda c: (0, half * nc + c))
    return pl.pallas_call(
        body,
        grid=(nc,),
        in_specs=[col(0), col(1), wcol(0), wcol(1)],
        out_specs=pl.BlockSpec((S, tc), lambda c: (0, c)),
        out_shape=jax.ShapeDtypeStruct((S, F), BF16),
        compiler_params=_cp(("parallel",)),
        name=name,
    )(u, u, convw, convw)


def ffn_act_bwd(u, convw, dact, F, name):
    S = u.shape[0]
    tc = _pick(F, (256, 128))
    nc = F // tc

    def body(ug_ref, uu_ref, wg_ref, wu_ref, da_ref, du_ref, dw_ref):
        ug = ug_ref[...].astype(F32)
        uu = uu_ref[...].astype(F32)
        wg = wg_ref[...]
        wu = wu_ref[...]
        da = da_ref[...].astype(F32)
        ag = _conv(ug, wg)
        au = _conv(uu, wu)
        sl, sig = _silu_parts(ag)
        dag = da * au * (sig * (1.0 + ag * (1.0 - sig)))
        dau = da * sl
        du_ref[0] = _conv_t(dag, wg).astype(du_ref.dtype)
        du_ref[1] = _conv_t(dau, wu).astype(du_ref.dtype)
        for k, (rg, ru) in enumerate(zip(_conv_dw(dag, ug), _conv_dw(dau, uu))):
            dw_ref[0, k : k + 1, :] = rg
            dw_ref[1, k : k + 1, :] = ru

    col = lambda half: pl.BlockSpec((S, tc), lambda c: (0, half * nc + c))
    wcol = lambda half: pl.BlockSpec((CONV_K, tc), lambda c: (0, half * nc + c))
    return pl.pallas_call(
        body,
        grid=(nc,),
        in_specs=[col(0), col(1), wcol(0), wcol(1), pl.BlockSpec((S, tc), lambda c: (0, c))],
        out_specs=[pl.BlockSpec((2, S, tc), lambda c: (0, 0, c)), pl.BlockSpec((2, CONV_K, tc), lambda c: (0, 0, c))],
        out_shape=[jax.ShapeDtypeStruct((2, S, F), BF16), jax.ShapeDtypeStruct((2, CONV_K, F), F32)],
        compiler_params=_cp(("parallel",)),
        name=name,
    )(u, u, convw, convw, dact)


def _softplus(z):
    return jnp.maximum(z, 0.0) + jnp.log(1.0 + jnp.exp(-jnp.abs(z)))


def _key_strip(S):
    return _pick(S, (512, 256, 128))


def _query_rows(S):
    tq = _pick(S, (512, 256, 128))
    assert _key_strip(S) % tq == 0
    return tq


def _split2(x):
    hi = x.astype(BF16)
    return hi, (x - hi.astype(F32)).astype(BF16)


def _block_sums(x, ones_bf16):
    hi, lo = _split2(x)
    return [
        _dot(hi[:, b * HD : (b + 1) * HD], ones_bf16) + _dot(lo[:, b * HD : (b + 1) * HD], ones_bf16)
        for b in range(x.shape[1] // HD)
    ]


def _strip_mask(shape, row0, off, strict):
    cols, rows = _iota2(shape, 1) + off, _iota2(shape, 0) + row0
    return cols < rows if strict else cols <= rows


def _sb_strip(q, ks, row0, off, run, su, masked):
    z = _dot_nt(q, ks) * (HD ** -0.5)
    sp = _softplus(z)
    mask = _strip_mask(z.shape, row0, off, True) if masked else None
    l = jnp.where(mask, -sp, 0.0) if masked else -sp
    within = _block_sums(l, su)
    later = [None] * len(within)
    for b in reversed(range(len(within))):
        later[b] = within[b] + run
        run = run + jnp.sum(l[:, b * HD : (b + 1) * HD], axis=1, keepdims=True)
    a = jnp.exp(z - sp + jnp.concatenate(later, axis=1))
    return z, (jnp.where(mask, a, 0.0) if masked else a), run


def sb_fwd(p, W, name):
    S = p.shape[0]
    TQ, TK = _query_rows(S), _key_strip(S)
    nh, nq = W // HD, S // TQ

    def body(q_ref, k_ref, v_ref, o_ref):
        i = pl.program_id(1)
        q = q_ref[...]
        su = (_iota2((HD, HD), 0) > _iota2((HD, HD), 1)).astype(BF16)
        last = (i * TQ + TQ - 1) // TK

        def strip(g, carry, masked):
            acc, run = carry
            off = pl.multiple_of(g * TK, TK)
            _, a, run = _sb_strip(q, k_ref[pl.ds(off, TK), :], i * TQ, off, run, su, masked)
            return acc + _dot(a.astype(BF16), v_ref[pl.ds(off, TK), :]), run

        carry = strip(last, (jnp.zeros((TQ, HD), F32), jnp.zeros((TQ, 1), F32)), True)
        acc, _ = lax.fori_loop(0, last, lambda gg, c: strip(last - 1 - gg, c, False), carry)
        o_ref[...] = acc.astype(o_ref.dtype)

    return pl.pallas_call(
        body,
        grid=(nh, nq),
        in_specs=[
            pl.BlockSpec((TQ, HD), lambda h, i: (i, h)),
            pl.BlockSpec((S, HD), lambda h, i: (0, nh + h)),
            pl.BlockSpec((S, HD), lambda h, i: (0, 2 * nh + h)),
        ],
        out_specs=pl.BlockSpec((TQ, HD), lambda h, i: (i, h)),
        out_shape=jax.ShapeDtypeStruct((S, 2 * W), BF16),
        compiler_params=_cp(("parallel", "arbitrary")),
        name=name,
    )(p, p, p)


def sb_bwd(p, dcat, W, name):
    S = p.shape[0]
    TQ, TK = _query_rows(S), _key_strip(S)
    nh, nq = W // HD, S // TQ
    scale = HD ** -0.5

    def body(q_ref, k_ref, v_ref, do_ref, dp_ref, dk_acc, dv_acc, e_scr, z_scr):
        i = pl.program_id(1)
        q = q_ref[...]
        do = do_ref[...]
        su = (_iota2((HD, HD), 0) > _iota2((HD, HD), 1)).astype(BF16)
        sl = (_iota2((HD, HD), 0) < _iota2((HD, HD), 1)).astype(BF16)
        last = (i * TQ + TQ - 1) // TK

        @pl.when(i == 0)
        def _():
            dk_acc[...] = jnp.zeros_like(dk_acc)
            dv_acc[...] = jnp.zeros_like(dv_acc)

        def pass_a(g, run, masked):
            off = pl.multiple_of(g * TK, TK)
            z, a, run = _sb_strip(q, k_ref[pl.ds(off, TK), :], i * TQ, off, run, su, masked)
            e_scr[g] = a * _dot_nt(do, v_ref[pl.ds(off, TK), :])
            z_scr[g] = z
            dv_acc[pl.ds(off, TK), :] += _dot_tn(a.astype(BF16), do)
            return run

        run = pass_a(last, jnp.zeros((TQ, 1), F32), True)
        lax.fori_loop(0, last, lambda gg, r: pass_a(last - 1 - gg, r, False), run)

        def pass_b(g, carry, masked):
            dq, run_e = carry
            off = pl.multiple_of(g * TK, TK)
            e = e_scr[g]
            z = z_scr[g]
            within = _block_sums(e, sl)
            before = []
            for b in range(len(within)):
                before.append(within[b] + run_e)
                run_e = run_e + jnp.sum(e[:, b * HD : (b + 1) * HD], axis=1, keepdims=True)
            sig = 1.0 / (1.0 + jnp.exp(-z))
            dz = e * (1.0 - sig) - jnp.concatenate(before, axis=1) * sig
            if masked:
                dz = jnp.where(_strip_mask(z.shape, i * TQ, off, True), dz, 0.0)
            dz = (dz * scale).astype(BF16)
            dq = dq + _dot(dz, k_ref[pl.ds(off, TK), :])
            dk_acc[pl.ds(off, TK), :] += _dot_tn(dz, q)
            return dq, run_e

        carry = lax.fori_loop(0, last, lambda g, c: pass_b(g, c, False), (jnp.zeros((TQ, HD), F32), jnp.zeros((TQ, 1), F32)))
        dq, _ = pass_b(last, carry, True)
        dp_ref[0, pl.ds(pl.multiple_of(i * TQ, TQ), TQ), :] = dq.astype(dp_ref.dtype)

        @pl.when(i == nq - 1)
        def _():
            dp_ref[1] = dk_acc[...].astype(dp_ref.dtype)
            dp_ref[2] = dv_acc[...].astype(dp_ref.dtype)

    return pl.pallas_call(
        body,
        grid=(nh, nq),
        in_specs=[
            pl.BlockSpec((TQ, HD), lambda h, i: (i, h)),
            pl.BlockSpec((S, HD), lambda h, i: (0, nh + h)),
            pl.BlockSpec((S, HD), lambda h, i: (0, 2 * nh + h)),
            pl.BlockSpec((TQ, HD), lambda h, i: (i, h)),
        ],
        out_specs=pl.BlockSpec((3, S, HD), lambda h, i: (0, 0, h)),
        out_shape=jax.ShapeDtypeStruct((6, S, W), BF16),
        scratch_shapes=[
            pltpu.VMEM((S, HD), F32),
            pltpu.VMEM((S, HD), F32),
            pltpu.VMEM((S // TK, TQ, TK), F32),
            pltpu.VMEM((S // TK, TQ, TK), F32),
        ],
        compiler_params=_cp(("parallel", "arbitrary")),
        name=name,
    )(p, p, p, dcat)


def fox_gate_fwd(f, b, name):
    S = f.shape[0]
    nq = S // HD

    def body(f_ref, b_ref, c_ref, run):
        i = pl.program_id(0)

        @pl.when(i == 0)
        def _():
            run[...] = jnp.zeros_like(run)

        lf = -_softplus(-(f_ref[...] + b_ref[...]))
        tri = (_iota2((HD, HD), 0) >= _iota2((HD, HD), 1)).astype(BF16)
        c_ref[...] = _dot_ones_left(tri, lf) + run[...]
        run[...] += jnp.sum(lf, axis=0, keepdims=True)

    return pl.pallas_call(
        body,
        grid=(nq,),
        in_specs=[pl.BlockSpec((HD, 128), lambda i: (i, 0)), pl.BlockSpec((1, 128), lambda i: (0, 0))],
        out_specs=pl.BlockSpec((HD, 128), lambda i: (i, 0)),
        out_shape=jax.ShapeDtypeStruct((S, 128), F32),
        scratch_shapes=[pltpu.VMEM((1, 128), F32)],
        compiler_params=_cp(("arbitrary",)),
        name=name,
    )(f, b)


def fox_gate_bwd(f, b, dc, name):
    S = f.shape[0]
    nq = S // HD

    def body(f_ref, b_ref, dc_ref, df_ref, db_ref, run):
        i = pl.program_id(0)

        @pl.when(i == 0)
        def _():
            run[...] = jnp.zeros_like(run)

        dc = dc_ref[...]
        tri = (_iota2((HD, HD), 0) <= _iota2((HD, HD), 1)).astype(BF16)
        dlf = _dot_ones_left(tri, dc) + run[...]
        run[...] += jnp.sum(dc, axis=0, keepdims=True)
        x = f_ref[...] + b_ref[...]
        df = dlf * (1.0 / (1.0 + jnp.exp(x)))
        df_ref[...] = df
        db = jnp.sum(df, axis=0, keepdims=True)

        @pl.when(i == 0)
        def _():
            db_ref[...] = db

        @pl.when(i > 0)
        def _():
            db_ref[...] += db

    rev = pl.BlockSpec((HD, 128), lambda i: (nq - 1 - i, 0))
    vec = pl.BlockSpec((1, 128), lambda i: (0, 0))
    return pl.pallas_call(
        body,
        grid=(nq,),
        in_specs=[rev, vec, rev],
        out_specs=[rev, vec],
        out_shape=[jax.ShapeDtypeStruct((S, 128), F32), jax.ShapeDtypeStruct((1, 128), F32)],
        scratch_shapes=[pltpu.VMEM((1, 128), F32)],
        compiler_params=_cp(("arbitrary",)),
        name=name,
    )(f, b, dc)


def _fox_logits(q, ks, ct, cs, row0, off, masked):
    s = _dot_nt(q, ks) * (HD ** -0.5) + (ct - cs)
    if not masked:
        return s, None
    mask = _strip_mask(s.shape, row0, off, False)
    return jnp.where(mask, s, -1e30), mask


def fox_fwd(p, ccol, crow, cat, W, name):
    S = p.shape[0]
    TQ, TK = _query_rows(S), _key_strip(S)
    nh, nq = W // HD, S // TQ

    def body(q_ref, k_ref, v_ref, cc_ref, cr_ref, cat_ref, o_ref, lse_ref):
        i = pl.program_id(1)
        q = q_ref[...]
        ct = cc_ref[0]

        def step(g, carry, masked):
            m, l, acc = carry
            off = pl.multiple_of(g * TK, TK)
            s, _ = _fox_logits(q, k_ref[pl.ds(off, TK), :], ct, cr_ref[0, pl.ds(g, 1), :], i * TQ, off, masked)
            m_new = jnp.maximum(m, jnp.max(s, axis=1, keepdims=True))
            alpha = jnp.exp(m - m_new)
            pr = jnp.exp(s - m_new)
            l = alpha * l + jnp.sum(pr, axis=1, keepdims=True)
            acc = alpha * acc + _dot(pr.astype(BF16), v_ref[pl.ds(off, TK), :])
            return m_new, l, acc

        init = (jnp.full((TQ, 1), -1e30, F32), jnp.zeros((TQ, 1), F32), jnp.zeros((TQ, HD), F32))
        last = (i * TQ + TQ - 1) // TK
        m, l, acc = step(last, lax.fori_loop(0, last, lambda g, c: step(g, c, False), init), True)
        o_ref[...] = (acc / l).astype(o_ref.dtype)
        lse_ref[0] = m + jnp.log(l)

    return pl.pallas_call(
        body,
        grid=(nh, nq),
        in_specs=[
            pl.BlockSpec((TQ, HD), lambda h, i: (i, 2 * nh + h)),
            pl.BlockSpec((S, HD), lambda h, i: (0, 3 * nh + h)),
            pl.BlockSpec((S, HD), lambda h, i: (0, 4 * nh + h)),
            pl.BlockSpec((1, TQ, 1), lambda h, i: (h, i, 0)),
            pl.BlockSpec((1, S // TK, TK), lambda h, i: (h, 0, 0)),
            pl.BlockSpec(memory_space=pl.ANY),
        ],
        out_specs=[pl.BlockSpec((TQ, HD), lambda h, i: (i, nh + h)), pl.BlockSpec((1, TQ, 1), lambda h, i: (h, i, 0))],
        out_shape=[jax.ShapeDtypeStruct(cat.shape, cat.dtype), jax.ShapeDtypeStruct((nh, S, 1), F32)],
        input_output_aliases={5: 0},
        compiler_params=_cp(("parallel", "arbitrary")),
        name=name,
    )(p, p, p, ccol, crow, cat)


def fox_bwd(p, ccol, crow, cat, lse, dcat, dp, W, name):
    S = p.shape[0]
    TQ, TK = _query_rows(S), _key_strip(S)
    nh, nq = W // HD, S // TQ
    scale = HD ** -0.5

    def body(q_ref, k_ref, v_ref, cc_ref, cr_ref, o_ref, lse_ref, do_ref, dp_in_ref, dp_ref, dcs_ref, dct_ref, dk_acc, dv_acc):
        i = pl.program_id(1)
        q = q_ref[...]
        do = do_ref[...]
        ct = cc_ref[0]
        lse_i = lse_ref[0]
        delta = jnp.sum(do.astype(F32) * o_ref[...].astype(F32), axis=1, keepdims=True)

        @pl.when(i == 0)
        def _():
            dk_acc[...] = jnp.zeros_like(dk_acc)
            dv_acc[...] = jnp.zeros_like(dv_acc)
            dcs_ref[...] = jnp.zeros_like(dcs_ref)

        def step(g, carry, masked):
            dq, dct = carry
            off = pl.multiple_of(g * TK, TK)
            ks = k_ref[pl.ds(off, TK), :]
            s, mask = _fox_logits(q, ks, ct, cr_ref[0, pl.ds(g, 1), :], i * TQ, off, masked)
            pr = jnp.where(mask, jnp.exp(s - lse_i), 0.0) if masked else jnp.exp(s - lse_i)
            ds = pr * (_dot_nt(do, v_ref[pl.ds(off, TK), :]) - delta)
            dv_acc[pl.ds(off, TK), :] += _dot_tn(pr.astype(BF16), do)
            dsb = (ds * scale).astype(BF16)
            dk_acc[pl.ds(off, TK), :] += _dot_tn(dsb, q)
            dcs_ref[0, pl.ds(g, 1), :] += jnp.sum(ds, axis=0, keepdims=True)
            return dq + _dot(dsb, ks), dct + jnp.sum(ds, axis=1, keepdims=True)

        last = (i * TQ + TQ - 1) // TK
        carry = lax.fori_loop(0, last, lambda g, c: step(g, c, False), (jnp.zeros((TQ, HD), F32), jnp.zeros((TQ, 1), F32)))
        dq, dct = step(last, carry, True)
        dp_ref[0, pl.ds(pl.multiple_of(i * TQ, TQ), TQ), :] = dq.astype(dp_ref.dtype)
        dct_ref[0] = dct

        @pl.when(i == nq - 1)
        def _():
            dp_ref[1] = dk_acc[...].astype(dp_ref.dtype)
            dp_ref[2] = dv_acc[...].astype(dp_ref.dtype)

    return pl.pallas_call(
        body,
        grid=(nh, nq),
        in_specs=[
            pl.BlockSpec((TQ, HD), lambda h, i: (i, 2 * nh + h)),
            pl.BlockSpec((S, HD), lambda h, i: (0, 3 * nh + h)),
            pl.BlockSpec((S, HD), lambda h, i: (0, 4 * nh + h)),
            pl.BlockSpec((1, TQ, 1), lambda h, i: (h, i, 0)),
            pl.BlockSpec((1, S // TK, TK), lambda h, i: (h, 0, 0)),
            pl.BlockSpec((TQ, HD), lambda h, i: (i, nh + h)),
            pl.BlockSpec((1, TQ, 1), lambda h, i: (h, i, 0)),
            pl.BlockSpec((TQ, HD), lambda h, i: (i, nh + h)),
            pl.BlockSpec(memory_space=pl.ANY),
        ],
        out_specs=[
            pl.BlockSpec((3, S, HD), lambda h, i: (1, 0, h)),
            pl.BlockSpec((1, S // TK, TK), lambda h, i: (h, 0, 0)),
            pl.BlockSpec((1, TQ, 1), lambda h, i: (h, i, 0)),
        ],
        out_shape=[
            jax.ShapeDtypeStruct(dp.shape, dp.dtype),
            jax.ShapeDtypeStruct((nh, S // TK, TK), F32),
            jax.ShapeDtypeStruct((nh, S, 1), F32),
        ],
        input_output_aliases={8: 0},
        scratch_shapes=[pltpu.VMEM((S, HD), F32), pltpu.VMEM((S, HD), F32)],
        compiler_params=_cp(("parallel", "arbitrary")),
        name=name,
    )(p, p, p, ccol, crow, cat, lse, dcat, dp)


_GELU_K = math.sqrt(2.0 / math.pi)
_GELU_C = 0.044715


def _gelu(x):
    return 0.5 * x * (1.0 + jnp.tanh(_GELU_K * (x + _GELU_C * x * x * x)))


def _gelu_grad(x):
    t = jnp.tanh(_GELU_K * (x + _GELU_C * x * x * x))
    return 0.5 * (1.0 + t) + 0.5 * x * (1.0 - t * t) * (_GELU_K * (1.0 + 3.0 * _GELU_C * x * x))


def _layernorm_parts(gv):
    xc = gv - jnp.mean(gv, axis=-1, keepdims=True)
    r = lax.rsqrt(jnp.mean(xc * xc, axis=-1, keepdims=True) + EPS)
    return xc * r, r


def sg_fwd(p, sg_w, sg_bt, sg_g, W, name):
    S = p.shape[0]
    G, nq = W // HD, S // HD

    def body(u_ref, v_ref, w_ref, bt_ref, g_ref, o_ref):
        xh, _ = _layernorm_parts(_gelu(v_ref[...].astype(F32)))
        vn = (xh * g_ref[...]).astype(BF16)
        tri = _iota2((HD, HD), 0) >= _iota2((HD, HD), 1)
        for gi in range(G):
            cols = slice(gi * HD, (gi + 1) * HD)
            wt = jnp.where(tri, w_ref[gi], 0.0).astype(BF16)
            mixed = _dot(wt, vn[:, cols]) + bt_ref[:, gi : gi + 1]
            o_ref[:, cols] = (_gelu(u_ref[:, cols].astype(F32)) * mixed).astype(o_ref.dtype)

    return pl.pallas_call(
        body,
        grid=(nq,),
        in_specs=[
            pl.BlockSpec((HD, W), lambda i: (i, 0)),
            pl.BlockSpec((HD, W), lambda i: (i, 1)),
            pl.BlockSpec((G, HD, HD), lambda i: (0, 0, 0)),
            pl.BlockSpec((HD, G), lambda i: (0, 0)),
            pl.BlockSpec((1, W), lambda i: (0, 0)),
        ],
        out_specs=pl.BlockSpec((HD, W), lambda i: (i, 0)),
        out_shape=jax.ShapeDtypeStruct((S, 2 * W), BF16),
        compiler_params=_cp(("parallel",)),
        name=name,
    )(p, p, sg_w, sg_bt, sg_g.reshape(1, W))


def sg_bwd(p, sg_w, sg_bt, sg_g, dcat, W, name):
    S = p.shape[0]
    G, nq = W // HD, S // HD

    def body(u_ref, v_ref, w_ref, bt_ref, g_ref, do_ref, dp_ref, dw_ref, dbt_ref, dg_ref, dvn_scr):
        i = pl.program_id(0)

        @pl.when(i == 0)
        def _():
            dw_ref[...] = jnp.zeros_like(dw_ref)
            dbt_ref[...] = jnp.zeros_like(dbt_ref)
            dg_ref[...] = jnp.zeros_like(dg_ref)

        v = v_ref[...].astype(F32)
        xh, r = _layernorm_parts(_gelu(v))
        gg = g_ref[...]
        vn = (xh * gg).astype(BF16)
        tri = _iota2((HD, HD), 0) >= _iota2((HD, HD), 1)
        for gi in range(G):
            cols = slice(gi * HD, (gi + 1) * HD)
            wt = jnp.where(tri, w_ref[gi], 0.0).astype(BF16)
            mixed = _dot(wt, vn[:, cols]) + bt_ref[:, gi : gi + 1]
            u = u_ref[:, cols].astype(F32)
            do = do_ref[:, cols].astype(F32)
            dp_ref[0, :, cols] = (do * mixed * _gelu_grad(u)).astype(dp_ref.dtype)
            dmix = do * _gelu(u)
            dmb = dmix.astype(BF16)
            dw_ref[gi] += jnp.where(tri, _dot_nt(dmb, vn[:, cols]), 0.0)
            dbt_ref[:, gi : gi + 1] += jnp.sum(dmix, axis=1, keepdims=True)
            dvn_scr[:, cols] = _dot_tn(wt, dmb)
        dvn = dvn_scr[...]
        dg_ref[...] += jnp.sum(dvn * xh, axis=0, keepdims=True)
        dxh = dvn * gg
        dgv = r * (dxh - jnp.mean(dxh, axis=-1, keepdims=True) - xh * jnp.mean(dxh * xh, axis=-1, keepdims=True))
        dp_ref[1] = (dgv * _gelu_grad(v)).astype(dp_ref.dtype)

    return pl.pallas_call(
        body,
        grid=(nq,),
        in_specs=[
            pl.BlockSpec((HD, W), lambda i: (i, 0)),
            pl.BlockSpec((HD, W), lambda i: (i, 1)),
            pl.BlockSpec((G, HD, HD), lambda i: (0, 0, 0)),
            pl.BlockSpec((HD, G), lambda i: (0, 0)),
            pl.BlockSpec((1, W), lambda i: (0, 0)),
            pl.BlockSpec((HD, W), lambda i: (i, 0)),
        ],
        out_specs=[
            pl.BlockSpec((2, HD, W), lambda i: (0, i, 0)),
            pl.BlockSpec((G, HD, HD), lambda i: (0, 0, 0)),
            pl.BlockSpec((HD, G), lambda i: (0, 0)),
            pl.BlockSpec((1, W), lambda i: (0, 0)),
        ],
        out_shape=[
            jax.ShapeDtypeStruct((6, S, W), BF16),
            jax.ShapeDtypeStruct((G, HD, HD), F32),
            jax.ShapeDtypeStruct((HD, G), F32),
            jax.ShapeDtypeStruct((1, W), F32),
        ],
        scratch_shapes=[pltpu.VMEM((HD, W), F32)],
        compiler_params=_cp(("arbitrary",)),
        name=name,
    )(p, p, sg_w, sg_bt, sg_g.reshape(1, W), dcat)


def local_step(x, target, wts, at, on_grad):
    S, D = x.shape
    W = D // 2
    nb, F = wts["nb"], wts["F"]
    g = {}

    def ffn_fwd(xin, l):
        h = rms_fwd(xin, wts[f"{l}_ffn_norm_g"], f"{l}_ffn_rms")
        u = mm_nn(h, wts[f"{l}_ffn_up"], nb, f"{l}_ffn_up_mm")
        act = ffn_act_fwd(u, wts[f"{l}_ffn_conv_w"], F, f"{l}_ffn_act")
        xout = mm_nn(act, wts[f"{l}_ffn_down"], 1, f"{l}_ffn_down_mm", out_dtype=F32, res=xin,
                     tm=_pick(S, (1024, 512, 256, 128)), tn=_pick(D, (512, 256, 128)), tk=F)
        return xout, (xin, h, u, act)

    def ffn_bwd(dxout, dxoutb, saved, l):
        xin, h, u, act = saved
        dact = mm_nt(dxoutb, wts[f"{l}_ffn_down"], 1, S, F, f"{l}_ffn_down_dx", tko=_pick(F, (512, 256, 128)), tn=D)
        dact = on_grad(f"{l}_ffn_down", mm_tn(act, dxoutb, 1, D, f"{l}_ffn_down_dw", tn=D), dact)
        du, dcw = ffn_act_bwd(u, wts[f"{l}_ffn_conv_w"], dact, F, f"{l}_ffn_act_bwd")
        g[f"{l}_ffn_conv_w"] = jnp.concatenate([dcw[0], dcw[1]], axis=1)
        du2 = du.reshape(2 * S, F)
        n = wts[f"{l}_ffn_up"].shape[1]
        tn = _pick(n, (1408, 1024, 768, 512, 256, 128))
        per_half = F // tn
        nt = n // tn

        def up_block(i, j, t):
            vb = j * nt + t
            return vb // per_half, vb % per_half

        tm = _pick(S, (1024, 512, 256, 128))

        def nt_map(i, j, t):
            half, cb = up_block(i, j, t)
            return (half * (S // tm) + i, cb)

        def tn_map(j, t):
            half, cb = up_block(0, j, t)
            return (half, cb)

        dh = mm_nt(du2, wts[f"{l}_ffn_up"], nb, S, D, f"{l}_ffn_up_dx", dy_maps=[nt_map], tm=tm, tko=D, tn=tn)
        dh = on_grad(f"{l}_ffn_up", mm_tn(h, du2, nb, n, f"{l}_ffn_up_dw", dy_maps=[tn_map], tko=_pick(D, (1024, 512, 256, 128)), tn=tn), dh)
        dxin, dxinb, dg = rms_bwd(xin, wts[f"{l}_ffn_norm_g"], dh, dxout, f"{l}_ffn_rms_bwd")
        g[f"{l}_ffn_norm_g"] = dg
        return dxin, dxinb

    h0 = rms_fwd(x, wts["l0_mix_norm_g"], "l0_mix_rms")
    p0 = mm_nn(h0, wts["l0_w_in"], nb, "l0_w_in_mm")
    cat0 = sb_fwd(p0, W, "l0_sb_fwd")
    cat0 = sc_fwd(p0, wts["l0_sc_conv_w"], cat0, W, "l0_sc_fwd")
    x1 = mm_nn(cat0, wts["l0_w_out"], 1, "l0_w_out_mm", out_dtype=F32, res=x, tm=S, tn=_pick(D, (512, 256, 128)))
    x2, ffn0_saved = ffn_fwd(x1, "l0")

    x2 = at("l1_w_in", x2, None)
    nh = W // HD
    h2 = rms_fwd(x2, wts["l1_mix_norm_g"], "l1_mix_rms")
    p1 = mm_nt(h2, wts["l1_w_in_t"], 1, S, 5 * W, "l1_w_in_mm", tn=D)
    f = mm_nt(h2, wts["l1_w_f_t"], 1, S, 128, "l1_w_f_mm", out_dtype=F32, tn=D)
    bf = jnp.zeros((1, 128), F32).at[0, :nh].set(wts["l1_fox_b_f"])
    c = fox_gate_fwd(f, bf, "l1_fox_gate")
    c_heads = c[:, :nh].T
    ccol = c_heads[:, :, None]
    crow = c_heads.reshape(nh, S // _key_strip(S), _key_strip(S))
    sg_bt = wts["l1_sg_b"].T
    cat1 = sg_fwd(p1, wts["l1_sg_w"], sg_bt, wts["l1_sg_norm_g"], W, "l1_sg_fwd")
    cat1, lse = fox_fwd(p1, ccol, crow, cat1, W, "l1_fox_fwd")
    x3 = mm_nn(cat1, wts["l1_w_out"], 1, "l1_w_out_mm", out_dtype=F32, res=x2, tm=S, tn=_pick(D, (512, 256, 128)))
    x4, ffn1_saved = ffn_fwd(x3, "l1")

    dx4, dx4b, dgf, loss = loss_head(x4, wts["final_norm_g"], target, "loss_head")
    dx4b = at("loss", dx4b, loss)
    g["final_norm_g"] = dgf

    dx3, dx3b = ffn_bwd(dx4, dx4b, ffn1_saved, "l1")
    dcat1 = mm_nt(dx3b, wts["l1_w_out"], 1, S, D, "l1_w_out_dx", tn=D)
    dcat1 = on_grad("l1_w_out", mm_tn(cat1, dx3b, 1, D, "l1_w_out_dw", tn=D), dcat1)
    dp1, dsgw, dsgbt, dsgg = sg_bwd(p1, wts["l1_sg_w"], sg_bt, wts["l1_sg_norm_g"], dcat1, W, "l1_sg_bwd")
    dp1, dcs, dct = fox_bwd(p1, ccol, crow, cat1, lse, dcat1, dp1, W, "l1_fox_bwd")
    g["l1_sg_w"], g["l1_sg_b"], g["l1_sg_norm_g"] = dsgw, dsgbt.T, dsgg
    dc = jnp.zeros((S, 128), F32).at[:, :nh].set((dct[:, :, 0] - dcs.reshape(nh, S)).T)
    df, dbf = fox_gate_bwd(f, bf, dc, "l1_fox_gate_bwd")
    g["l1_fox_b_f"] = dbf[0, :nh]
    dfb = df.astype(BF16)
    tk1 = _pick(W, (1024, 512, 256, 128))
    tx1 = _pick(W, (512, 256, 128))
    tm1 = _pick(S, (1024, 512, 256, 128))
    part_of = lambda pt: pt + pt // 2 - pt // 4

    def a_map1(i, k):
        return (part_of(k // (W // tk1)) * (S // tm1) + i, k % (W // tk1))

    def x_map1(ko):
        return (part_of(ko // (W // tx1)), ko % (W // tx1))

    dp1_2d = dp1.reshape(6 * S, W)
    dw_main = mm_tn(dp1_2d, h2, 1, D, "l1_w_in_dw", tko=tx1, tn=D, x_map=x_map1, x_shape=(S, 5 * W))
    dw_f = mm_tn(dfb, h2, 1, D, "l1_w_f_dw", tn=D)
    dh2 = mm_nn(dfb, wts["l1_w_f_t"], 1, "l1_w_f_dx", out_dtype=F32)
    dh2 = mm_nn(dp1_2d, wts["l1_w_in_t"], 1, "l1_w_in_dx", res=dh2, tm=tm1, tk=tk1, a_map=a_map1, a_shape=(S, 5 * W))
    dh2 = on_grad("l1_w_in", jnp.concatenate([dw_main, dw_f[:nh]], axis=0), dh2)
    dx2, dx2b, dg = rms_bwd(x2, wts["l1_mix_norm_g"], dh2, dx3, "l1_mix_rms_bwd")
    g["l1_mix_norm_g"] = dg

    dx1, dx1b = ffn_bwd(dx2, dx2b, ffn0_saved, "l0")
    dcat0 = mm_nt(dx1b, wts["l0_w_out"], 1, S, D, "l0_w_out_dx", tn=D)
    dcat0 = on_grad("l0_w_out", mm_tn(cat0, dx1b, 1, D, "l0_w_out_dw", tn=D), dcat0)
    dp0 = sb_bwd(p0, dcat0, W, "l0_sb_bwd")
    dp0, dscw = sc_bwd(p0, wts["l0_sc_conv_w"], dcat0, dp0, W, "l0_sc_bwd")
    g["l0_sc_conv_w"] = dscw
    dp0 = at("small_ready", dp0, g)
    n0 = wts["l0_w_in"].shape[1]
    td0 = math.gcd(n0, W)
    nd0 = n0 // td0
    tm0 = _pick(S, (1024, 512, 256, 128))
    per_part0 = W // td0

    def nt_maps0(k):
        def f(i, j, t):
            vb = j * nd0 + k
            return ((vb // per_part0) * (S // tm0) + i, vb % per_part0)
        return f

    def tn_maps0(k):
        def f(j, t):
            vb = j * nd0 + k
            return (vb // per_part0, vb % per_part0)
        return f

    dp0_2d = dp0.reshape(6 * S, W)
    dw0 = mm_tn(h0, dp0_2d, nb, n0, "l0_w_in_dw", dy_maps=[tn_maps0(k) for k in range(nd0)], tko=_pick(D, (1024, 512, 256, 128)), tn=n0)
    dp0_2d = on_grad("l0_w_in", dw0, dp0_2d)
    dp0_2d = on_grad(None, None, dp0_2d)
    dh0 = mm_nt(dp0_2d, wts["l0_w_in"], nb, S, D, "l0_w_in_dx", dy_maps=[nt_maps0(k) for k in range(nd0)], tm=tm0, tko=D, tn=n0)
    dh0 = at("small_done", dh0, None)
    dx0, _, dg = rms_bwd(x, wts["l0_mix_norm_g"], dh0, dx1, "l0_mix_rms_bwd")
    g["l0_mix_norm_g"] = dg
    return dx0, g


GATHER_ID = 1


def _place():
    return lax.axis_index("x"), lax.axis_index("y"), lax.axis_index("c")


def _other_chips(x, y):
    return [(x, 1 - y), (1 - x, y), (1 - x, 1 - y)]


def _handshake(peers):
    barrier = pltpu.get_barrier_semaphore()
    for peer in peers:
        pl.semaphore_signal(barrier, inc=1, device_id=peer, device_id_type=MESH)
    pl.semaphore_wait(barrier, len(peers))


UPDATE_LAG = 2


def _on_sequencer(body, out_type, scratch_types, collective_id, name):
    return pl.kernel(
        body,
        out_type=out_type,
        mesh=plsc.ScalarSubcoreMesh(axis_name="seq", num_cores=1),
        scratch_types=scratch_types,
        compiler_params=pltpu.CompilerParams(collective_id=collective_id),
        name=name,
    )


def all_gather(arrs, name):
    n = len(arrs)

    def body(*refs):
        xs, outs = refs[:n], refs[n : 2 * n]
        send_sems, recv_sems, local_sems = refs[2 * n :]
        x, y, c = _place()
        me, sibling = (x, y, c), (x, y, 1 - c)
        chips = _other_chips(x, y)
        _handshake([sibling] + [(*chip, c) for chip in chips])

        def copy(a, k, block, to, src=None):
            px, py, pc = block
            dst = outs[a].at[4 * px + 2 * py + pc]
            return pltpu.make_async_remote_copy(
                src_ref=dst if src is None else src, dst_ref=dst,
                send_sem=send_sems.at[7 * a + k], recv_sem=recv_sems.at[7 * a + k], device_id=to, device_id_type=MESH,
            )

        mine = [pltpu.make_async_copy(xs[a], outs[a].at[4 * x + 2 * y + c], local_sems.at[a]) for a in range(n)]
        for cp in mine:
            cp.start()
        first = []
        for a in range(n):
            first.append(copy(a, 0, me, sibling, src=xs[a]))
            first += [copy(a, 1 + j, me, (*chip, c), src=xs[a]) for j, chip in enumerate(chips)]
        for cp in first:
            cp.start()
        passed = []
        for a in range(n):
            for j, chip in enumerate(chips):
                copy(a, 1 + j, (*chip, c), me).wait_recv()
                cp = copy(a, 4 + j, (*chip, c), sibling)
                cp.start()
                passed.append(cp)
        for a in range(n):
            copy(a, 0, sibling, me).wait_recv()
            for j, chip in enumerate(chips):
                copy(a, 4 + j, (*chip, 1 - c), me).wait_recv()
        for cp in first + passed:
            cp.wait_send()
        for cp in mine:
            cp.wait()

    out_type = [jax.ShapeDtypeStruct((NDEV,) + a.shape, a.dtype) for a in arrs]
    sems = [pltpu.SemaphoreType.DMA((7 * n,)), pltpu.SemaphoreType.DMA((7 * n,)), pltpu.SemaphoreType.DMA((n,))]
    return _on_sequencer(body, out_type, sems, GATHER_ID, name)(*arrs)


_IN_HBM = pl.BlockSpec(memory_space=pltpu.HBM)
_IN_SEM = pl.BlockSpec(memory_space=pltpu.SEMAPHORE)
_EFFECT = pltpu.SideEffectType.DATAFLOW_SIDE_EFFECTING


def _split_start(make_copies, src, land_shape, nsem, name):
    def body(src_ref, land_ref, send_sems, recv_sems, land_thru, token):
        for cp in make_copies(src_ref, land_ref, send_sems, recv_sems):
            cp.start()
        token[...] = jnp.zeros_like(token)

    send_sems, recv_sems, land_thru, token = pl.pallas_call(
        body,
        name=name,
        out_shape=(
            pltpu.SemaphoreType.DMA((nsem,)), pltpu.SemaphoreType.DMA((nsem,)),
            pltpu.HBM(land_shape, src.dtype), jax.ShapeDtypeStruct((8, 128), F32),
        ),
        in_specs=(_IN_HBM, _IN_HBM),
        out_specs=(_IN_SEM, _IN_SEM, _IN_HBM, pl.BlockSpec(memory_space=pltpu.VMEM)),
        input_output_aliases={1: 2},
        compiler_params=pltpu.CompilerParams(has_side_effects=_EFFECT),
    )(src, pltpu.with_memory_space_constraint(lax.empty(land_shape, src.dtype), pltpu.HBM))
    return send_sems, recv_sems, src, land_thru, token


def _split_wait(make_copies, send_sems, recv_sems, src_thru, land_thru, after, name):
    def body(src_ref, land_ref, send_sems, recv_sems, after_ref, land_out):
        for cp in make_copies(src_ref, land_ref, send_sems, recv_sems):
            cp.wait_send()
            cp.wait_recv()

    return pl.pallas_call(
        body,
        name=name,
        out_shape=pltpu.HBM(land_thru.shape, land_thru.dtype),
        in_specs=(_IN_HBM, _IN_HBM, _IN_SEM, _IN_SEM, pl.BlockSpec(memory_space=pl.ANY)),
        out_specs=_IN_HBM,
        input_output_aliases={1: 0},
        compiler_params=pltpu.CompilerParams(has_side_effects=_EFFECT),
    )(src_thru, land_thru, send_sems, recv_sems, after)


def _pair_copies(src_ref, land_ref, send_sems, recv_sems):
    x, y, c = _place()
    return [
        pltpu.make_async_remote_copy(
            src_ref=src_ref.at[k, 1 - c], dst_ref=land_ref.at[k],
            send_sem=send_sems.at[k], recv_sem=recv_sems.at[k], device_id=(x, y, 1 - c), device_id_type=MESH,
        )
        for k in range(4)
    ]


def _direct_copies(src_ref, land_ref, send_sems, recv_sems):
    x, y, c = _place()
    me = 4 * x + 2 * y + c
    copies = []
    for k in range(NDEV - 1):
        to = (me + k + 1) % NDEV
        copies.append(pltpu.make_async_remote_copy(
            src_ref=src_ref, dst_ref=land_ref.at[me], send_sem=send_sems.at[k], recv_sem=recv_sems.at[k],
            device_id=(to // 4, (to // 2) % 2, to % 2), device_id_type=MESH,
        ))
    return copies


def _chip_copies(src_ref, land_ref, send_sems, recv_sems):
    x, y, c = _place()
    return [
        pltpu.make_async_remote_copy(
            src_ref=src_ref.at[2 * px + py], dst_ref=land_ref.at[2 * x + y],
            send_sem=send_sems.at[j], recv_sem=recv_sems.at[j], device_id=(px, py, c), device_id_type=MESH,
        )
        for j, (px, py) in enumerate(_other_chips(x, y))
    ]


def _row_tile(R, C, max_elems):
    if R * C <= max_elems:
        return R
    best = None
    for tr in range(16, R, 16):
        if R % tr == 0 and tr * C <= max_elems:
            best = tr
    return best or R


def pair_sum(a42, land4, core, name):
    _, _, R, C = a42.shape
    tr = _row_tile(R, C, 1 << 20)

    def body(core_ref, a_ref, l_ref, o_ref):
        o_ref[...] = (a_ref[0].astype(F32) + l_ref[...].astype(F32)).astype(o_ref.dtype)

    return pl.pallas_call(
        body,
        grid_spec=pltpu.PrefetchScalarGridSpec(
            num_scalar_prefetch=1,
            grid=(4, R // tr),
            in_specs=[
                pl.BlockSpec((1, 1, tr, C), lambda k, r, core_ref: (k, core_ref[0], r, 0)),
                pl.BlockSpec((1, tr, C), lambda k, r, core_ref: (k, r, 0)),
            ],
            out_specs=pl.BlockSpec((1, tr, C), lambda k, r, core_ref: (k, r, 0)),
        ),
        out_shape=jax.ShapeDtypeStruct((4, R, C), BF16),
        compiler_params=_cp(("parallel", "parallel")),
        name=name,
    )(core, a42, land4)


def sum_slots(parts, name):
    P, R, C = parts.shape

    def body(p_ref, o_ref):
        acc = p_ref[0].astype(F32)
        for k in range(1, P):
            acc = acc + p_ref[k].astype(F32)
        o_ref[...] = acc

    tr = _row_tile(R, P * C, 1 << 21)
    return pl.pallas_call(
        body,
        grid=(R // tr,),
        in_specs=[pl.BlockSpec((P, tr, C), lambda r: (0, r, 0))],
        out_specs=pl.BlockSpec((tr, C), lambda r: (r, 0)),
        out_shape=jax.ShapeDtypeStruct((R, C), F32),
        compiler_params=_cp(("parallel",)),
        name=name,
    )(parts)


def adamw(w, m, v, parts, name):
    R, C = w.shape
    P = parts.shape[0]
    tr = _pick(R, (256, 128, 64, 32, 16, 8))
    c1 = 1.0 - ADAM_B1 ** ADAM_STEP
    c2 = 1.0 - ADAM_B2 ** ADAM_STEP

    def body(w_ref, m_ref, v_ref, p_ref, g_ref, d_ref, nm_ref, nv_ref):
        g = p_ref[0].astype(F32)
        for k in range(1, P):
            g = g + p_ref[k].astype(F32)
        nm = ADAM_B1 * m_ref[...] + (1.0 - ADAM_B1) * g
        nv = ADAM_B2 * v_ref[...] + (1.0 - ADAM_B2) * (g * g)
        g_ref[...] = g
        nm_ref[...] = nm
        nv_ref[...] = nv
        d_ref[...] = -ADAM_LR * ((nm / c1) / (jnp.sqrt(nv / c2) + ADAM_EPS) + ADAM_WD * w_ref[...])

    blk = pl.BlockSpec((tr, C), lambda r: (r, 0))
    shp = jax.ShapeDtypeStruct((R, C), F32)
    return pl.pallas_call(
        body,
        grid=(R // tr,),
        in_specs=[blk, blk, blk, pl.BlockSpec((P, tr, C), lambda r: (0, r, 0))],
        out_specs=[blk, blk, blk, blk],
        out_shape=[shp, shp, shp, shp],
        compiler_params=_cp(("parallel",)),
        name=name,
    )(w, m, v, parts)


def adamw_reduced(w, m, v, own, land, chip, name):
    R, C = w.shape
    if R % 8 == 0:
        tr, tc = _pick(R, (256, 128, 64, 32, 16, 8)), C
    else:
        tr, tc = R, _pick(C, (256, 128))
    c1 = 1.0 - ADAM_B1 ** ADAM_STEP
    c2 = 1.0 - ADAM_B2 ** ADAM_STEP

    def body(chip_ref, w_ref, m_ref, v_ref, own_ref, land_ref, g_ref, d_ref, nm_ref, nv_ref):
        mine = own_ref[0].astype(F32)
        g = None
        for k in range(4):
            term = jnp.where(chip_ref[0] == k, mine, land_ref[k].astype(F32))
            g = term if g is None else g + term
        nm = ADAM_B1 * m_ref[...] + (1.0 - ADAM_B1) * g
        nv = ADAM_B2 * v_ref[...] + (1.0 - ADAM_B2) * (g * g)
        g_ref[...] = g
        nm_ref[...] = nm
        nv_ref[...] = nv
        d_ref[...] = -ADAM_LR * ((nm / c1) / (jnp.sqrt(nv / c2) + ADAM_EPS) + ADAM_WD * w_ref[...])

    blk = pl.BlockSpec((tr, tc), lambda r, c, chip_ref: (r, c))
    shp = jax.ShapeDtypeStruct((R, C), F32)
    return pl.pallas_call(
        body,
        grid_spec=pltpu.PrefetchScalarGridSpec(
            num_scalar_prefetch=1,
            grid=(R // tr, C // tc),
            in_specs=[
                blk, blk, blk,
                pl.BlockSpec((1, tr, tc), lambda r, c, chip_ref: (chip_ref[0], r, c)),
                pl.BlockSpec((4, tr, tc), lambda r, c, chip_ref: (0, r, c)),
            ],
            out_specs=[blk, blk, blk, blk],
        ),
        out_shape=[shp, shp, shp, shp],
        compiler_params=_cp(("parallel", "parallel")),
        name=name,
    )(chip, w, m, v, own, land)


_WEIGHTS = [
    "l0_mix_norm_g", "l0_w_in", "l0_sc_conv_w", "l0_w_out", "l0_ffn_norm_g", "l0_ffn_up", "l0_ffn_conv_w", "l0_ffn_down",
    "l1_mix_norm_g", "l1_w_in", "l1_fox_b_f", "l1_sg_w", "l1_sg_b", "l1_sg_norm_g", "l1_w_out", "l1_ffn_norm_g",
    "l1_ffn_up", "l1_ffn_conv_w", "l1_ffn_down", "final_norm_g",
]
_ROW_SHARDED = ["l0_w_out", "l0_ffn_down", "l1_w_out", "l1_ffn_down"]
_BIG = ["l0_w_in", "l0_w_out", "l0_ffn_up", "l0_ffn_down", "l1_w_in", "l1_w_out", "l1_ffn_up", "l1_ffn_down"]
_CONV = ["l0_sc_conv_w", "l0_ffn_conv_w", "l1_ffn_conv_w"]
_SMALL = [n for n in _WEIGHTS if n not in _BIG]
_LAST_SMALL = "l0_mix_norm_g"
_PACK_ROWS = 8


def _pack(arrs):
    flat = []
    for a in arrs:
        v = a.reshape(-1).astype(F32)
        pad = (-v.shape[0]) % (_PACK_ROWS * 128)
        flat.append(jnp.pad(v, (0, pad)))
    return jnp.concatenate(flat).reshape(-1, 128)


def _unpack(packed, shapes):
    out, off = [], 0
    flat = packed.reshape(-1)
    for shp in shapes:
        size = math.prod(shp)
        out.append(flat[off : off + size].reshape(shp))
        off += size + (-size) % (_PACK_ROWS * 128)
    return out


def kernel(x, l0_mix_norm_g, l0_w_in, l0_sc_conv_w, l0_w_out, l0_ffn_norm_g, l0_ffn_up, l0_ffn_conv_w, l0_ffn_down, l1_mix_norm_g, l1_w_in, l1_fox_b_f, l1_sg_w, l1_sg_b, l1_sg_norm_g, l1_w_out, l1_ffn_norm_g, l1_ffn_up, l1_ffn_conv_w, l1_ffn_down, final_norm_g, loss_target, m_l0_mix_norm_g, m_l0_w_in, m_l0_sc_conv_w, m_l0_w_out, m_l0_ffn_norm_g, m_l0_ffn_up, m_l0_ffn_conv_w, m_l0_ffn_down, m_l1_mix_norm_g, m_l1_w_in, m_l1_fox_b_f, m_l1_sg_w, m_l1_sg_b, m_l1_sg_norm_g, m_l1_w_out, m_l1_ffn_norm_g, m_l1_ffn_up, m_l1_ffn_conv_w, m_l1_ffn_down, m_final_norm_g, v_l0_mix_norm_g, v_l0_w_in, v_l0_sc_conv_w, v_l0_w_out, v_l0_ffn_norm_g, v_l0_ffn_up, v_l0_ffn_conv_w, v_l0_ffn_down, v_l1_mix_norm_g, v_l1_w_in, v_l1_fox_b_f, v_l1_sg_w, v_l1_sg_b, v_l1_sg_norm_g, v_l1_w_out, v_l1_ffn_norm_g, v_l1_ffn_up, v_l1_ffn_conv_w, v_l1_ffn_down, v_final_norm_g):
    given = dict(locals())
    w = {n: given[n] for n in _WEIGHTS}
    mom = {n: given["m_" + n] for n in _WEIGHTS}
    var = {n: given["v_" + n] for n in _WEIGHTS}
    xs, target = x[0], loss_target[0]
    S, D = xs.shape
    W = D // 2
    nh = W // HD
    cx, cy, cc = _place()
    me = 4 * cx + 2 * cy + cc

    wts = {"nb": NDEV, "F": l0_ffn_down.shape[0] * NDEV}
    for n in _SMALL:
        if n not in _CONV:
            wts[n] = w[n]
    gathered, loss_sum = {}, []

    def start_gather(names):
        srcs = [(w[n].T if n == "l1_w_in" else w[n]).astype(BF16) for n in names]
        taps = [w[c] for c in _CONV] if names[0] == _BIG[0] else []
        got = all_gather(srcs + taps, "gather_" + "_".join(names))
        for n, full in zip(names, got):
            if n == "l1_w_in":
                gathered[n] = full
            elif n in _ROW_SHARDED:
                wts[n] = full.reshape(-1, D)
            else:
                wts[n] = full.reshape(NDEV * D, -1)
        for c, full in zip(_CONV, got[len(names):] if taps else []):
            wts[c] = full.transpose(1, 0, 2).reshape(CONV_K, -1)

    def at(point, after, value):
        if point == "l1_w_in":
            got, after = lax.optimization_barrier((gathered[point], after))
            wts["l1_w_in_t"] = got.reshape(-1, D)
            wts["l1_w_f_t"] = jnp.pad(wts["l1_w_in_t"][5 * W :], ((0, 128 - nh), (0, 0)))
        elif point == "loss":
            gathered["loss"] = value[0, :1]
        elif point == "small_ready":
            early = [n for n in _SMALL if n != _LAST_SMALL]
            gathered["small"] = all_gather([_pack([value[n] for n in early] + [gathered["loss"]])], "gather_small_grads")[0]
        elif point == "small_done":
            after = update_small([n for n in _SMALL if n != _LAST_SMALL], gathered["small"], "small", after, True)
        return after

    out_g, out_d, out_m, out_v = {}, {}, {}, {}

    def update_small(names, all_terms, tag, after=None, with_loss=False):
        shapes = [w[n].shape for n in names]
        full_shapes = [(CONV_K, NDEV * w[n].shape[1]) if n in _CONV else w[n].shape for n in names]
        summed = _unpack(sum_slots(all_terms, f"sum_{tag}_grads"), full_shapes + ([(1,)] if with_loss else []))
        if with_loss:
            loss_sum.append(summed[-1][0])
        grads = {}
        for n, t in zip(names, summed):
            if n in _CONV:
                cols = w[n].shape[1]
                t = lax.dynamic_slice_in_dim(t, me * cols, cols, axis=1)
            grads[n] = t
        res = adamw(
            _pack([w[n] for n in names]), _pack([mom[n] for n in names]), _pack([var[n] for n in names]),
            _pack([grads[n] for n in names])[None], f"adamw_{tag}",
        )
        if after is not None:
            res, after = lax.optimization_barrier((res, after))
        for dst, packed_out in zip((out_g, out_d, out_m, out_v), res):
            for n, t in zip(names, _unpack(packed_out, shapes)):
                dst[n] = t
        return after

    core = jnp.reshape(cc, (1,)).astype(jnp.int32)
    chip = jnp.reshape(2 * cx + cy, (1,)).astype(jnp.int32)
    pair_flying, chip_flying = [], []

    def tie(value, after):
        if after is None:
            return value, None
        return lax.optimization_barrier((value, after))

    def to_chips(after):
        n, flying = pair_flying.pop()
        landed = _split_wait(_pair_copies, *flying, f"reduce_pair_wait_{n}")
        summed = pair_sum(flying[2], landed, core, f"pair_sum_{n}")
        *flying, token = _split_start(_chip_copies, summed, summed.shape, 3, f"reduce_chips_{n}")
        token, after = tie(token, after)
        chip_flying.append((n, flying + [token]))
        return after

    def update(after, behind=None):
        n, flying = chip_flying.pop(0)
        if behind is not None:
            flying[4], _ = lax.optimization_barrier((flying[4], behind))
        landed = _split_wait(_chip_copies, *flying, f"reduce_chips_wait_{n}")
        turn = (lambda t: t.T) if n == "l1_w_in" else (lambda t: t)
        res = adamw_reduced(turn(w[n]), turn(mom[n]), turn(var[n]), flying[2], landed, chip, f"adamw_{n}")
        res, after = tie(res, after)
        out_g[n], out_d[n], out_m[n], out_v[n] = [turn(t) for t in res]
        return after, res[0]

    def on_grad(n, term, after):
        if n is None:
            return to_chips(after)
        if n in _ROW_SHARDED or n == "l1_w_in":
            term = term.reshape(NDEV, -1, D)
        else:
            term = term.reshape(NDEV, D, -1)
        term = term.reshape((4, 2) + term.shape[1:])
        *flying, token = _split_start(_pair_copies, term, term.shape[:1] + term.shape[2:], 4, f"reduce_pair_{n}")
        token, after = tie(token, after)
        if len(chip_flying) == UPDATE_LAG:
            after, _ = update(after)
        if pair_flying:
            after = to_chips(after)
        pair_flying.append((n, flying + [token]))
        return after

    for n in _BIG:
        start_gather([n])
    dx, g = local_step(xs, target, wts, at, on_grad)
    last = _pack([g[_LAST_SMALL]])
    *flying, done = _split_start(_direct_copies, last, (NDEV,) + last.shape, NDEV - 1, "gather_last_grad")
    while len(chip_flying) > 1:
        _, done = update(None, behind=done)
    landed = _split_wait(_direct_copies, *flying, done, "gather_last_grad_wait")
    update_small([_LAST_SMALL], lax.dynamic_update_slice(landed, last[None], (me, 0, 0)), "last")
    update(None, behind=out_g[_LAST_SMALL])
    loss = loss_sum[0]

    return (loss, dx[None], *[out_g[n] for n in _WEIGHTS], *[out_d[n] for n in _WEIGHTS],
            *[out_m[n] for n in _WEIGHTS], *[out_v[n] for n in _WEIGHTS])
```

```python
import functools
import math

import jax
import jax.numpy as jnp
from jax import lax
from jax.experimental import pallas as pl
from jax.experimental.pallas import tpu as pltpu
from jax.experimental.pallas import tpu_sc as plsc

F32 = jnp.float32
BF16 = jnp.bfloat16
HD = 128
EPS = 1e-6
CONV_K = 3
VMEM_LIMIT_BYTES = 48 << 20
NDEV = 8
MESH = pl.DeviceIdType.MESH

ADAM_LR = 0.001
ADAM_B1 = 0.9
ADAM_B2 = 0.999
ADAM_EPS = 1e-08
ADAM_WD = 0.01
ADAM_STEP = 10


def _cp(sem):
    return pltpu.CompilerParams(dimension_semantics=sem, vmem_limit_bytes=VMEM_LIMIT_BYTES)


def _pick(n, prefs):
    for p in prefs:
        if n % p == 0:
            return p
    return n


def _dot(a, b):
    return jnp.dot(a, b, preferred_element_type=F32)


def _dot_nt(a, b):
    return lax.dot_general(a, b, (((1,), (1,)), ((), ())), preferred_element_type=F32)


def _dot_tn(a, b):
    return lax.dot_general(a, b, (((0,), (0,)), ((), ())), preferred_element_type=F32)


def _split3(x):
    hi = x.astype(BF16)
    r = x - hi.astype(F32)
    mid = r.astype(BF16)
    lo = (r - mid.astype(F32)).astype(BF16)
    return hi, mid, lo


def _dot_ones_left(ones_bf16, x):
    hi, mid, lo = _split3(x)
    return _dot(ones_bf16, hi) + _dot(ones_bf16, mid) + _dot(ones_bf16, lo)


def _iota2(shape, axis):
    return lax.broadcasted_iota(jnp.int32, shape, axis)


def mm_nn(a, w2d, nb, name, out_dtype=BF16, res=None, tm=None, tn=None, tk=None, a_map=None, a_shape=None):
    M, K = a_shape or a.shape
    n = w2d.shape[1]
    assert w2d.shape[0] == nb * K or (nb == 1 and w2d.shape[0] > K)
    a_map = a_map or (lambda i, k: (i, k))
    tm = tm or _pick(M, (1024, 512, 256, 128))
    tn = tn or _pick(n, (1408, 1024, 768, 512, 256, 128))
    tk = tk or (K if K <= 2048 else _pick(K, (1408, 1024, 512, 256, 128)))
    nk, nt = K // tk, n // tn
    has_res = res is not None

    def body(*refs):
        if has_res:
            a_ref, w_ref, r_ref, o_ref = refs[:4]
        else:
            a_ref, w_ref, o_ref = refs[:3]
            r_ref = None
        part = _dot(a_ref[...], w_ref[...])

        def finish(acc):
            if r_ref is not None:
                acc = acc + r_ref[...].astype(F32)
            o_ref[...] = acc.astype(o_ref.dtype)

        if nk == 1:
            finish(part)
        else:
            acc_ref = refs[-1]
            k = pl.program_id(3)

            @pl.when(k == 0)
            def _():
                acc_ref[...] = part

            @pl.when(k > 0)
            def _():
                acc_ref[...] += part

            @pl.when(k == nk - 1)
            def _():
                finish(acc_ref[...])

    in_specs = [
        pl.BlockSpec((tm, tk), lambda i, j, t, k: a_map(i, k)),
        pl.BlockSpec((tk, tn), lambda i, j, t, k: (j * nk + k, t)),
    ]
    args = [a, w2d]
    out_spec = pl.BlockSpec((tm, tn), lambda i, j, t, k: (i, j * nt + t))
    if has_res:
        in_specs.append(out_spec)
        args.append(res)
    return pl.pallas_call(
        body,
        grid=(M // tm, nb, nt, nk),
        in_specs=in_specs,
        out_specs=out_spec,
        out_shape=jax.ShapeDtypeStruct((M, nb * n), out_dtype),
        scratch_shapes=[pltpu.VMEM((tm, tn), F32)] if nk > 1 else [],
        compiler_params=_cp(("parallel", "parallel", "parallel", "arbitrary")),
        name=name,
    )(*args)


def mm_nt(dy2d, w2d, nb, M, K, name, out_dtype=BF16, res=None, dy_maps=None, tm=None, tko=None, tn=None):
    n = w2d.shape[1]
    assert w2d.shape[0] == nb * K or (nb == 1 and w2d.shape[0] > K)
    tm = tm or _pick(M, (1024, 512, 256, 128))
    tko = tko or _pick(K, (1024, 512, 256, 128))
    tn = tn or _pick(n, (1408, 1024, 768, 512, 256, 128))
    nt, nko = n // tn, K // tko
    has_res = res is not None
    if dy_maps is None:
        dy_maps = [lambda i, j, t: (i, j * nt + t)]
    nd = len(dy_maps)
    td = tn // nd

    one_step = nb * nt == 1

    def body(*refs):
        d_refs, w_ref = refs[:nd], refs[nd]
        r_ref = refs[nd + 1] if has_res else None
        d = d_refs[0][...] if nd == 1 else jnp.concatenate([r[...] for r in d_refs], axis=1)
        part = _dot_nt(d, w_ref[...])
        if one_step:
            o_ref = refs[-1]
            if r_ref is not None:
                part = part + r_ref[...].astype(F32)
            o_ref[...] = part.astype(o_ref.dtype)
            return
        o_ref, acc_ref = refs[-2], refs[-1]
        j, t = pl.program_id(2), pl.program_id(3)
        first = jnp.logical_and(j == 0, t == 0)
        last = jnp.logical_and(j == nb - 1, t == nt - 1)

        @pl.when(first)
        def _():
            acc_ref[...] = part

        @pl.when(jnp.logical_not(first))
        def _():
            acc_ref[...] += part

        @pl.when(last)
        def _():
            acc = acc_ref[...]
            if r_ref is not None:
                acc = acc + r_ref[...].astype(F32)
            o_ref[...] = acc.astype(o_ref.dtype)

    in_specs = [pl.BlockSpec((tm, td), functools.partial(lambda f, i, ko, j, t: f(i, j, t), f)) for f in dy_maps]
    in_specs.append(pl.BlockSpec((tko, tn), lambda i, ko, j, t: (j * nko + ko, t)))
    args = [dy2d] * nd + [w2d]
    out_spec = pl.BlockSpec((tm, tko), lambda i, ko, j, t: (i, ko))
    if has_res:
        in_specs.append(out_spec)
        args.append(res)
    return pl.pallas_call(
        body,
        grid=(M // tm, nko, nb, nt),
        in_specs=in_specs,
        out_specs=out_spec,
        out_shape=jax.ShapeDtypeStruct((M, K), out_dtype),
        scratch_shapes=[] if one_step else [pltpu.VMEM((tm, tko), F32)],
        compiler_params=_cp(("parallel", "parallel", "arbitrary", "arbitrary")),
        name=name,
    )(*args)


def mm_tn(x, dy2d, nb, n, name, out_dtype=BF16, dy_maps=None, tko=None, tn=None, x_map=None, x_shape=None):
    S, K = x_shape or x.shape
    x_map = x_map or (lambda ko: (0, ko))
    tko = tko or _pick(K, (512, 256, 128))
    tn = tn or _pick(n, (1408, 1024, 768, 512, 256, 128))
    nt, nko = n // tn, K // tko
    if dy_maps is None:
        dy_maps = [lambda j, t: (0, j * nt + t)]
    nd = len(dy_maps)
    td = tn // nd

    def body(*refs):
        x_ref, d_refs, o_ref = refs[0], refs[1 : 1 + nd], refs[-1]
        d = d_refs[0][...] if nd == 1 else jnp.concatenate([r[...] for r in d_refs], axis=1)
        o_ref[...] = _dot_tn(x_ref[...], d).astype(o_ref.dtype)

    in_specs = [pl.BlockSpec((S, tko), lambda ko, j, t: x_map(ko))]
    in_specs += [pl.BlockSpec((S, td), functools.partial(lambda f, ko, j, t: f(j, t), f)) for f in dy_maps]
    return pl.pallas_call(
        body,
        grid=(nko, nb, nt),
        in_specs=in_specs,
        out_specs=pl.BlockSpec((tko, tn), lambda ko, j, t: (j * nko + ko, t)),
        out_shape=jax.ShapeDtypeStruct((nb * K, n), out_dtype),
        compiler_params=_cp(("parallel", "parallel", "parallel")),
        name=name,
    )(x, *([dy2d] * nd))


def rms_fwd(x, g, name):
    S, D = x.shape
    tm = _pick(S, (512, 256, 128))

    def body(x_ref, g_ref, o_ref):
        xf = x_ref[...]
        r = lax.rsqrt(jnp.mean(xf * xf, axis=-1, keepdims=True) + EPS)
        o_ref[...] = (xf * r * g_ref[...]).astype(o_ref.dtype)

    return pl.pallas_call(
        body,
        grid=(S // tm,),
        in_specs=[pl.BlockSpec((tm, D), lambda i: (i, 0)), pl.BlockSpec((1, D), lambda i: (0, 0))],
        out_specs=pl.BlockSpec((tm, D), lambda i: (i, 0)),
        out_shape=jax.ShapeDtypeStruct((S, D), BF16),
        compiler_params=_cp(("parallel",)),
        name=name,
    )(x, g.reshape(1, D))


def rms_bwd(x, g, dh, dres, name):
    S, D = x.shape
    tm = _pick(S, (256, 128))

    def body(x_ref, g_ref, dh_ref, dr_ref, dx_ref, dxb_ref, dg_ref):
        i = pl.program_id(0)
        xf = x_ref[...]
        dh = dh_ref[...].astype(F32)
        r = lax.rsqrt(jnp.mean(xf * xf, axis=-1, keepdims=True) + EPS)
        gy = dh * g_ref[...]
        proj = jnp.mean(gy * xf, axis=-1, keepdims=True)
        dx = dr_ref[...] + r * gy - xf * (r * r * r * proj)
        dx_ref[...] = dx
        dxb_ref[...] = dx.astype(BF16)
        dg = jnp.sum(dh * (xf * r), axis=0, keepdims=True)

        @pl.when(i == 0)
        def _():
            dg_ref[...] = dg

        @pl.when(i > 0)
        def _():
            dg_ref[...] += dg

    row = pl.BlockSpec((tm, D), lambda i: (i, 0))
    vec = pl.BlockSpec((1, D), lambda i: (0, 0))
    return pl.pallas_call(
        body,
        grid=(S // tm,),
        in_specs=[row, vec, row, row],
        out_specs=[row, row, vec],
        out_shape=[jax.ShapeDtypeStruct((S, D), F32), jax.ShapeDtypeStruct((S, D), BF16), jax.ShapeDtypeStruct((1, D), F32)],
        compiler_params=_cp(("arbitrary",)),
        name=name,
    )(x, g.reshape(1, D), dh, dres)


def loss_head(x, g, target, name):
    S, D = x.shape
    tm = _pick(S, (256, 128))

    def body(x_ref, g_ref, t_ref, dx_ref, dxb_ref, dg_ref, loss_ref):
        i = pl.program_id(0)
        xf = x_ref[...]
        gg = g_ref[...]
        r = lax.rsqrt(jnp.mean(xf * xf, axis=-1, keepdims=True) + EPS)
        xh = xf * r
        err = xh * gg - t_ref[...]
        part = (0.5 / D) * jnp.sum(err * err)
        dy = err * (1.0 / D)
        gy = dy * gg
        proj = jnp.mean(gy * xf, axis=-1, keepdims=True)
        dx = r * gy - xf * (r * r * r * proj)
        dx_ref[...] = dx
        dxb_ref[...] = dx.astype(BF16)
        dg = jnp.sum(dy * xh, axis=0, keepdims=True)
        lossb = jnp.full(loss_ref.shape, part, F32)

        @pl.when(i == 0)
        def _():
            dg_ref[...] = dg
            loss_ref[...] = lossb

        @pl.when(i > 0)
        def _():
            dg_ref[...] += dg
            loss_ref[...] += lossb

    row = pl.BlockSpec((tm, D), lambda i: (i, 0))
    vec = pl.BlockSpec((1, D), lambda i: (0, 0))
    return pl.pallas_call(
        body,
        grid=(S // tm,),
        in_specs=[row, vec, row],
        out_specs=[row, row, vec, pl.BlockSpec((8, 128), lambda i: (0, 0))],
        out_shape=[
            jax.ShapeDtypeStruct((S, D), F32),
            jax.ShapeDtypeStruct((S, D), BF16),
            jax.ShapeDtypeStruct((1, D), F32),
            jax.ShapeDtypeStruct((8, 128), F32),
        ],
        compiler_params=_cp(("arbitrary",)),
        name=name,
    )(x, g.reshape(1, D), target)


def _shift_down(s, k):
    if k == 0:
        return s
    return jnp.where(_iota2(s.shape, 0) >= k, pltpu.roll(s, k, axis=0), 0.0)


def _shift_up(s, k):
    if k == 0:
        return s
    n = s.shape[0]
    return jnp.where(_iota2(s.shape, 0) < n - k, pltpu.roll(s, n - k, axis=0), 0.0)


def _conv(s, w):
    return w[0:1] * _shift_down(s, 2) + w[1:2] * _shift_down(s, 1) + w[2:3] * s


def _conv_t(d, w):
    return w[2:3] * d + w[1:2] * _shift_up(d, 1) + w[0:1] * _shift_up(d, 2)


def _conv_dw(d, s):
    return [jnp.sum(d * _shift_down(s, CONV_K - 1 - k), axis=0, keepdims=True) for k in range(CONV_K)]


def sc_fwd(p, convw, cat, W, name):
    S = p.shape[0]
    tc = _pick(W, (256, 128))
    nc = W // tc

    def body(gb_ref, gc_ref, hi_ref, w_ref, cat_ref, o_ref):
        s = gc_ref[...].astype(F32) * hi_ref[...].astype(F32)
        o_ref[...] = (gb_ref[...].astype(F32) * _conv(s, w_ref[...])).astype(o_ref.dtype)

    col = lambda part: pl.BlockSpec((S, tc), lambda c: (0, part * nc + c))
    return pl.pallas_call(
        body,
        grid=(nc,),
        in_specs=[col(3), col(4), col(5), pl.BlockSpec((CONV_K, tc), lambda c: (0, c)), pl.BlockSpec(memory_space=pl.ANY)],
        out_specs=col(1),
        out_shape=jax.ShapeDtypeStruct(cat.shape, cat.dtype),
        input_output_aliases={4: 0},
        compiler_params=_cp(("parallel",)),
        name=name,
    )(p, p, p, convw, cat)


def sc_bwd(p, convw, dcat, dp, W, name):
    S = p.shape[0]
    tc = _pick(W, (256, 128))
    nc = W // tc

    def body(gb_ref, gc_ref, hi_ref, w_ref, do_ref, dp_in_ref, dp_ref, dw_ref):
        gb = gb_ref[...].astype(F32)
        gc = gc_ref[...].astype(F32)
        hi = hi_ref[...].astype(F32)
        w = w_ref[...]
        do = do_ref[...].astype(F32)
        s = gc * hi
        dcs = do * gb
        ds = _conv_t(dcs, w)
        dp_ref[0] = (do * _conv(s, w)).astype(dp_ref.dtype)
        dp_ref[1] = (ds * hi).astype(dp_ref.dtype)
        dp_ref[2] = (ds * gc).astype(dp_ref.dtype)
        for k, row in enumerate(_conv_dw(dcs, s)):
            dw_ref[k : k + 1, :] = row

    col = lambda part: pl.BlockSpec((S, tc), lambda c: (0, part * nc + c))
    return pl.pallas_call(
        body,
        grid=(nc,),
        in_specs=[
            col(3), col(4), col(5),
            pl.BlockSpec((CONV_K, tc), lambda c: (0, c)),
            pl.BlockSpec((S, tc), lambda c: (0, nc + c)),
            pl.BlockSpec(memory_space=pl.ANY),
        ],
        out_specs=[pl.BlockSpec((3, S, tc), lambda c: (1, 0, c)), pl.BlockSpec((CONV_K, tc), lambda c: (0, c))],
        out_shape=[jax.ShapeDtypeStruct(dp.shape, dp.dtype), jax.ShapeDtypeStruct((CONV_K, W), F32)],
        input_output_aliases={5: 0},
        compiler_params=_cp(("parallel",)),
        name=name,
    )(p, p, p, convw, dcat, dp)


def _silu_parts(a):
    sig = 1.0 / (1.0 + jnp.exp(-a))
    return a * sig, sig


def ffn_act_fwd(u, convw, F, name):
    S = u.shape[0]
    tc = _pick(F, (256, 128))
    nc = F // tc

    def body(ug_ref, uu_ref, wg_ref, wu_ref, o_ref):
        ag = _conv(ug_ref[...].astype(F32), wg_ref[...])
        au = _conv(uu_ref[...].astype(F32), wu_ref[...])
        o_ref[...] = (_silu_parts(ag)[0] * au).astype(o_ref.dtype)

    col = lambda half: pl.BlockSpec((S, tc), lambda c: (0, half * nc + c))
    wcol = lambda half: pl.BlockSpec((CONV_K, tc), lambda c: (0, half * nc + c))
    return pl.pallas_call(
        body,
        grid=(nc,),
        in_specs=[col(0), col(1), wcol(0), wcol(1)],
        out_specs=pl.BlockSpec((S, tc), lambda c: (0, c)),
        out_shape=jax.ShapeDtypeStruct((S, F), BF16),
        compiler_params=_cp(("parallel",)),
        name=name,
    )(u, u, convw, convw)


def ffn_act_bwd(u, convw, dact, F, name):
    S = u.shape[0]
    tc = _pick(F, (256, 128))
    nc = F // tc

    def body(ug_ref, uu_ref, wg_ref, wu_ref, da_ref, du_ref, dw_ref):
        ug = ug_ref[...].astype(F32)
        uu = uu_ref[...].astype(F32)
        wg = wg_ref[...]
        wu = wu_ref[...]
        da = da_ref[...].astype(F32)
        ag = _conv(ug, wg)
        au = _conv(uu, wu)
        sl, sig = _silu_parts(ag)
        dag = da * au * (sig * (1.0 + ag * (1.0 - sig)))
        dau = da * sl
        du_ref[0] = _conv_t(dag, wg).astype(du_ref.dtype)
        du_ref[1] = _conv_t(dau, wu).astype(du_ref.dtype)
        for k, (rg, ru) in enumerate(zip(_conv_dw(dag, ug), _conv_dw(dau, uu))):
            dw_ref[0, k : k + 1, :] = rg
            dw_ref[1, k : k + 1, :] = ru

    col = lambda half: pl.BlockSpec((S, tc), lambda c: (0, half * nc + c))
    wcol = lambda half: pl.BlockSpec((CONV_K, tc), lambda c: (0, half * nc + c))
    return pl.pallas_call(
        body,
        grid=(nc,),
        in_specs=[col(0), col(1), wcol(0), wcol(1), pl.BlockSpec((S, tc), lambda c: (0, c))],
        out_specs=[pl.BlockSpec((2, S, tc), lambda c: (0, 0, c)), pl.BlockSpec((2, CONV_K, tc), lambda c: (0, 0, c))],
        out_shape=[jax.ShapeDtypeStruct((2, S, F), BF16), jax.ShapeDtypeStruct((2, CONV_K, F), F32)],
        compiler_params=_cp(("parallel",)),
        name=name,
    )(u, u, convw, convw, dact)


def _softplus(z):
    return jnp.maximum(z, 0.0) + jnp.log(1.0 + jnp.exp(-jnp.abs(z)))


def _key_strip(S):
    return _pick(S, (512, 256, 128))


def _query_rows(S):
    tq = _pick(S, (512, 256, 128))
    assert _key_strip(S) % tq == 0
    return tq


def _split2(x):
    hi = x.astype(BF16)
    return hi, (x - hi.astype(F32)).astype(BF16)


def _block_sums(x, ones_bf16):
    hi, lo = _split2(x)
    return [
        _dot(hi[:, b * HD : (b + 1) * HD], ones_bf16) + _dot(lo[:, b * HD : (b + 1) * HD], ones_bf16)
        for b in range(x.shape[1] // HD)
    ]


def _strip_mask(shape, row0, off, strict):
    cols, rows = _iota2(shape, 1) + off, _iota2(shape, 0) + row0
    return cols < rows if strict else cols <= rows


def _sb_strip(q, ks, row0, off, run, su, masked):
    z = _dot_nt(q, ks) * (HD ** -0.5)
    sp = _softplus(z)
    mask = _strip_mask(z.shape, row0, off, True) if masked else None
    l = jnp.where(mask, -sp, 0.0) if masked else -sp
    within = _block_sums(l, su)
    later = [None] * len(within)
    for b in reversed(range(len(within))):
        later[b] = within[b] + run
        run = run + jnp.sum(l[:, b * HD : (b + 1) * HD], axis=1, keepdims=True)
    a = jnp.exp(z - sp + jnp.concatenate(later, axis=1))
    return z, (jnp.where(mask, a, 0.0) if masked else a), run


def sb_fwd(p, W, name):
    S = p.shape[0]
    TQ, TK = _query_rows(S), _key_strip(S)
    nh, nq = W // HD, S // TQ

    def body(q_ref, k_ref, v_ref, o_ref):
        i = pl.program_id(1)
        q = q_ref[...]
        su = (_iota2((HD, HD), 0) > _iota2((HD, HD), 1)).astype(BF16)
        last = (i * TQ + TQ - 1) // TK

        def strip(g, carry, masked):
            acc, run = carry
            off = pl.multiple_of(g * TK, TK)
            _, a, run = _sb_strip(q, k_ref[pl.ds(off, TK), :], i * TQ, off, run, su, masked)
            return acc + _dot(a.astype(BF16), v_ref[pl.ds(off, TK), :]), run

        carry = strip(last, (jnp.zeros((TQ, HD), F32), jnp.zeros((TQ, 1), F32)), True)
        acc, _ = lax.fori_loop(0, last, lambda gg, c: strip(last - 1 - gg, c, False), carry)
        o_ref[...] = acc.astype(o_ref.dtype)

    return pl.pallas_call(
        body,
        grid=(nh, nq),
        in_specs=[
            pl.BlockSpec((TQ, HD), lambda h, i: (i, h)),
            pl.BlockSpec((S, HD), lambda h, i: (0, nh + h)),
            pl.BlockSpec((S, HD), lambda h, i: (0, 2 * nh + h)),
        ],
        out_specs=pl.BlockSpec((TQ, HD), lambda h, i: (i, h)),
        out_shape=jax.ShapeDtypeStruct((S, 2 * W), BF16),
        compiler_params=_cp(("parallel", "arbitrary")),
        name=name,
    )(p, p, p)


def sb_bwd(p, dcat, W, name):
    S = p.shape[0]
    TQ, TK = _query_rows(S), _key_strip(S)
    nh, nq = W // HD, S // TQ
    scale = HD ** -0.5

    def body(q_ref, k_ref, v_ref, do_ref, dp_ref, dk_acc, dv_acc, e_scr, z_scr):
        i = pl.program_id(1)
        q = q_ref[...]
        do = do_ref[...]
        su = (_iota2((HD, HD), 0) > _iota2((HD, HD), 1)).astype(BF16)
        sl = (_iota2((HD, HD), 0) < _iota2((HD, HD), 1)).astype(BF16)
        last = (i * TQ + TQ - 1) // TK

        @pl.when(i == 0)
        def _():
            dk_acc[...] = jnp.zeros_like(dk_acc)
            dv_acc[...] = jnp.zeros_like(dv_acc)

        def pass_a(g, run, masked):
            off = pl.multiple_of(g * TK, TK)
            z, a, run = _sb_strip(q, k_ref[pl.ds(off, TK), :], i * TQ, off, run, su, masked)
            e_scr[g] = a * _dot_nt(do, v_ref[pl.ds(off, TK), :])
            z_scr[g] = z
            dv_acc[pl.ds(off, TK), :] += _dot_tn(a.astype(BF16), do)
            return run

        run = pass_a(last, jnp.zeros((TQ, 1), F32), True)
        lax.fori_loop(0, last, lambda gg, r: pass_a(last - 1 - gg, r, False), run)

        def pass_b(g, carry, masked):
            dq, run_e = carry
            off = pl.multiple_of(g * TK, TK)
            e = e_scr[g]
            z = z_scr[g]
            within = _block_sums(e, sl)
            before = []
            for b in range(len(within)):
                before.append(within[b] + run_e)
                run_e = run_e + jnp.sum(e[:, b * HD : (b + 1) * HD], axis=1, keepdims=True)
            sig = 1.0 / (1.0 + jnp.exp(-z))
            dz = e * (1.0 - sig) - jnp.concatenate(before, axis=1) * sig
            if masked:
                dz = jnp.where(_strip_mask(z.shape, i * TQ, off, True), dz, 0.0)
            dz = (dz * scale).astype(BF16)
            dq = dq + _dot(dz, k_ref[pl.ds(off, TK), :])
            dk_acc[pl.ds(off, TK), :] += _dot_tn(dz, q)
            return dq, run_e

        carry = lax.fori_loop(0, last, lambda g, c: pass_b(g, c, False), (jnp.zeros((TQ, HD), F32), jnp.zeros((TQ, 1), F32)))
        dq, _ = pass_b(last, carry, True)
        dp_ref[0, pl.ds(pl.multiple_of(i * TQ, TQ), TQ), :] = dq.astype(dp_ref.dtype)

        @pl.when(i == nq - 1)
        def _():
            dp_ref[1] = dk_acc[...].astype(dp_ref.dtype)
            dp_ref[2] = dv_acc[...].astype(dp_ref.dtype)

    return pl.pallas_call(
        body,
        grid=(nh, nq),
        in_specs=[
            pl.BlockSpec((TQ, HD), lambda h, i: (i, h)),
            pl.BlockSpec((S, HD), lambda h, i: (0, nh + h)),
            pl.BlockSpec((S, HD), lambda h, i: (0, 2 * nh + h)),
            pl.BlockSpec((TQ, HD), lambda h, i: (i, h)),
        ],
        out_specs=pl.BlockSpec((3, S, HD), lambda h, i: (0, 0, h)),
        out_shape=jax.ShapeDtypeStruct((6, S, W), BF16),
        scratch_shapes=[
            pltpu.VMEM((S, HD), F32),
            pltpu.VMEM((S, HD), F32),
            pltpu.VMEM((S // TK, TQ, TK), F32),
            pltpu.VMEM((S // TK, TQ, TK), F32),
        ],
        compiler_params=_cp(("parallel", "arbitrary")),
        name=name,
    )(p, p, p, dcat)


def fox_gate_fwd(f, b, name):
    S = f.shape[0]
    nq = S // HD

    def body(f_ref, b_ref, c_ref, run):
        i = pl.program_id(0)

        @pl.when(i == 0)
        def _():
            run[...] = jnp.zeros_like(run)

        lf = -_softplus(-(f_ref[...] + b_ref[...]))
        tri = (_iota2((HD, HD), 0) >= _iota2((HD, HD), 1)).astype(BF16)
        c_ref[...] = _dot_ones_left(tri, lf) + run[...]
        run[...] += jnp.sum(lf, axis=0, keepdims=True)

    return pl.pallas_call(
        body,
        grid=(nq,),
        in_specs=[pl.BlockSpec((HD, 128), lambda i: (i, 0)), pl.BlockSpec((1, 128), lambda i: (0, 0))],
        out_specs=pl.BlockSpec((HD, 128), lambda i: (i, 0)),
        out_shape=jax.ShapeDtypeStruct((S, 128), F32),
        scratch_shapes=[pltpu.VMEM((1, 128), F32)],
        compiler_params=_cp(("arbitrary",)),
        name=name,
    )(f, b)


def fox_gate_bwd(f, b, dc, name):
    S = f.shape[0]
    nq = S // HD

    def body(f_ref, b_ref, dc_ref, df_ref, db_ref, run):
        i = pl.program_id(0)

        @pl.when(i == 0)
        def _():
            run[...] = jnp.zeros_like(run)

        dc = dc_ref[...]
        tri = (_iota2((HD, HD), 0) <= _iota2((HD, HD), 1)).astype(BF16)
        dlf = _dot_ones_left(tri, dc) + run[...]
        run[...] += jnp.sum(dc, axis=0, keepdims=True)
        x = f_ref[...] + b_ref[...]
        df = dlf * (1.0 / (1.0 + jnp.exp(x)))
        df_ref[...] = df
        db = jnp.sum(df, axis=0, keepdims=True)

        @pl.when(i == 0)
        def _():
            db_ref[...] = db

        @pl.when(i > 0)
        def _():
            db_ref[...] += db

    rev = pl.BlockSpec((HD, 128), lambda i: (nq - 1 - i, 0))
    vec = pl.BlockSpec((1, 128), lambda i: (0, 0))
    return pl.pallas_call(
        body,
        grid=(nq,),
        in_specs=[rev, vec, rev],
        out_specs=[rev, vec],
        out_shape=[jax.ShapeDtypeStruct((S, 128), F32), jax.ShapeDtypeStruct((1, 128), F32)],
        scratch_shapes=[pltpu.VMEM((1, 128), F32)],
        compiler_params=_cp(("arbitrary",)),
        name=name,
    )(f, b, dc)


def _fox_logits(q, ks, ct, cs, row0, off, masked):
    s = _dot_nt(q, ks) * (HD ** -0.5) + (ct - cs)
    if not masked:
        return s, None
    mask = _strip_mask(s.shape, row0, off, False)
    return jnp.where(mask, s, -1e30), mask


def fox_fwd(p, ccol, crow, cat, W, name):
    S = p.shape[0]
    TQ, TK = _query_rows(S), _key_strip(S)
    nh, nq = W // HD, S // TQ

    def body(q_ref, k_ref, v_ref, cc_ref, cr_ref, cat_ref, o_ref, lse_ref):
        i = pl.program_id(1)
        q = q_ref[...]
        ct = cc_ref[0]

        def step(g, carry, masked):
            m, l, acc = carry
            off = pl.multiple_of(g * TK, TK)
            s, _ = _fox_logits(q, k_ref[pl.ds(off, TK), :], ct, cr_ref[0, pl.ds(g, 1), :], i * TQ, off, masked)
            m_new = jnp.maximum(m, jnp.max(s, axis=1, keepdims=True))
            alpha = jnp.exp(m - m_new)
            pr = jnp.exp(s - m_new)
            l = alpha * l + jnp.sum(pr, axis=1, keepdims=True)
            acc = alpha * acc + _dot(pr.astype(BF16), v_ref[pl.ds(off, TK), :])
            return m_new, l, acc

        init = (jnp.full((TQ, 1), -1e30, F32), jnp.zeros((TQ, 1), F32), jnp.zeros((TQ, HD), F32))
        last = (i * TQ + TQ - 1) // TK
        m, l, acc = step(last, lax.fori_loop(0, last, lambda g, c: step(g, c, False), init), True)
        o_ref[...] = (acc / l).astype(o_ref.dtype)
        lse_ref[0] = m + jnp.log(l)

    return pl.pallas_call(
        body,
        grid=(nh, nq),
        in_specs=[
            pl.BlockSpec((TQ, HD), lambda h, i: (i, 2 * nh + h)),
            pl.BlockSpec((S, HD), lambda h, i: (0, 3 * nh + h)),
            pl.BlockSpec((S, HD), lambda h, i: (0, 4 * nh + h)),
            pl.BlockSpec((1, TQ, 1), lambda h, i: (h, i, 0)),
            pl.BlockSpec((1, S // TK, TK), lambda h, i: (h, 0, 0)),
            pl.BlockSpec(memory_space=pl.ANY),
        ],
        out_specs=[pl.BlockSpec((TQ, HD), lambda h, i: (i, nh + h)), pl.BlockSpec((1, TQ, 1), lambda h, i: (h, i, 0))],
        out_shape=[jax.ShapeDtypeStruct(cat.shape, cat.dtype), jax.ShapeDtypeStruct((nh, S, 1), F32)],
        input_output_aliases={5: 0},
        compiler_params=_cp(("parallel", "arbitrary")),
        name=name,
    )(p, p, p, ccol, crow, cat)


def fox_bwd(p, ccol, crow, cat, lse, dcat, dp, W, name):
    S = p.shape[0]
    TQ, TK = _query_rows(S), _key_strip(S)
    nh, nq = W // HD, S // TQ
    scale = HD ** -0.5

    def body(q_ref, k_ref, v_ref, cc_ref, cr_ref, o_ref, lse_ref, do_ref, dp_in_ref, dp_ref, dcs_ref, dct_ref, dk_acc, dv_acc):
        i = pl.program_id(1)
        q = q_ref[...]
        do = do_ref[...]
        ct = cc_ref[0]
        lse_i = lse_ref[0]
        delta = jnp.sum(do.astype(F32) * o_ref[...].astype(F32), axis=1, keepdims=True)

        @pl.when(i == 0)
        def _():
            dk_acc[...] = jnp.zeros_like(dk_acc)
            dv_acc[...] = jnp.zeros_like(dv_acc)
            dcs_ref[...] = jnp.zeros_like(dcs_ref)

        def step(g, carry, masked):
            dq, dct = carry
            off = pl.multiple_of(g * TK, TK)
            ks = k_ref[pl.ds(off, TK), :]
            s, mask = _fox_logits(q, ks, ct, cr_ref[0, pl.ds(g, 1), :], i * TQ, off, masked)
            pr = jnp.where(mask, jnp.exp(s - lse_i), 0.0) if masked else jnp.exp(s - lse_i)
            ds = pr * (_dot_nt(do, v_ref[pl.ds(off, TK), :]) - delta)
            dv_acc[pl.ds(off, TK), :] += _dot_tn(pr.astype(BF16), do)
            dsb = (ds * scale).astype(BF16)
            dk_acc[pl.ds(off, TK), :] += _dot_tn(dsb, q)
            dcs_ref[0, pl.ds(g, 1), :] += jnp.sum(ds, axis=0, keepdims=True)
            return dq + _dot(dsb, ks), dct + jnp.sum(ds, axis=1, keepdims=True)

        last = (i * TQ + TQ - 1) // TK
        carry = lax.fori_loop(0, last, lambda g, c: step(g, c, False), (jnp.zeros((TQ, HD), F32), jnp.zeros((TQ, 1), F32)))
        dq, dct = step(last, carry, True)
        dp_ref[0, pl.ds(pl.multiple_of(i * TQ, TQ), TQ), :] = dq.astype(dp_ref.dtype)
        dct_ref[0] = dct

        @pl.when(i == nq - 1)
        def _():
            dp_ref[1] = dk_acc[...].astype(dp_ref.dtype)
            dp_ref[2] = dv_acc[...].astype(dp_ref.dtype)

    return pl.pallas_call(
        body,
        grid=(nh, nq),
        in_specs=[
            pl.BlockSpec((TQ, HD), lambda h, i: (i, 2 * nh + h)),
            pl.BlockSpec((S, HD), lambda h, i: (0, 3 * nh + h)),
            pl.BlockSpec((S, HD), lambda h, i: (0, 4 * nh + h)),
            pl.BlockSpec((1, TQ, 1), lambda h, i: (h, i, 0)),
            pl.BlockSpec((1, S // TK, TK), lambda h, i: (h, 0, 0)),
            pl.BlockSpec((TQ, HD), lambda h, i: (i, nh + h)),
            pl.BlockSpec((1, TQ, 1), lambda h, i: (h, i, 0)),
            pl.BlockSpec((TQ, HD), lambda h, i: (i, nh + h)),
            pl.BlockSpec(memory_space=pl.ANY),
        ],
        out_specs=[
            pl.BlockSpec((3, S, HD), lambda h, i: (1, 0, h)),
            pl.BlockSpec((1, S // TK, TK), lambda h, i: (h, 0, 0)),
            pl.BlockSpec((1, TQ, 1), lambda h, i: (h, i, 0)),
        ],
        out_shape=[
            jax.ShapeDtypeStruct(dp.shape, dp.dtype),
            jax.ShapeDtypeStruct((nh, S // TK, TK), F32),
            jax.ShapeDtypeStruct((nh, S, 1), F32),
        ],
        input_output_aliases={8: 0},
        scratch_shapes=[pltpu.VMEM((S, HD), F32), pltpu.VMEM((S, HD), F32)],
        compiler_params=_cp(("parallel", "arbitrary")),
        name=name,
    )(p, p, p, ccol, crow, cat, lse, dcat, dp)


_GELU_K = math.sqrt(2.0 / math.pi)
_GELU_C = 0.044715


def _gelu(x):
    return 0.5 * x * (1.0 + jnp.tanh(_GELU_K * (x + _GELU_C * x * x * x)))


def _gelu_grad(x):
    t = jnp.tanh(_GELU_K * (x + _GELU_C * x * x * x))
    return 0.5 * (1.0 + t) + 0.5 * x * (1.0 - t * t) * (_GELU_K * (1.0 + 3.0 * _GELU_C * x * x))


def _layernorm_parts(gv):
    xc = gv - jnp.mean(gv, axis=-1, keepdims=True)
    r = lax.rsqrt(jnp.mean(xc * xc, axis=-1, keepdims=True) + EPS)
    return xc * r, r


def sg_fwd(p, sg_w, sg_bt, sg_g, W, name):
    S = p.shape[0]
    G, nq = W // HD, S // HD

    def body(u_ref, v_ref, w_ref, bt_ref, g_ref, o_ref):
        xh, _ = _layernorm_parts(_gelu(v_ref[...].astype(F32)))
        vn = (xh * g_ref[...]).astype(BF16)
        tri = _iota2((HD, HD), 0) >= _iota2((HD, HD), 1)
        for gi in range(G):
            cols = slice(gi * HD, (gi + 1) * HD)
            wt = jnp.where(tri, w_ref[gi], 0.0).astype(BF16)
            mixed = _dot(wt, vn[:, cols]) + bt_ref[:, gi : gi + 1]
            o_ref[:, cols] = (_gelu(u_ref[:, cols].astype(F32)) * mixed).astype(o_ref.dtype)

    return pl.pallas_call(
        body,
        grid=(nq,),
        in_specs=[
            pl.BlockSpec((HD, W), lambda i: (i, 0)),
            pl.BlockSpec((HD, W), lambda i: (i, 1)),
            pl.BlockSpec((G, HD, HD), lambda i: (0, 0, 0)),
            pl.BlockSpec((HD, G), lambda i: (0, 0)),
            pl.BlockSpec((1, W), lambda i: (0, 0)),
        ],
        out_specs=pl.BlockSpec((HD, W), lambda i: (i, 0)),
        out_shape=jax.ShapeDtypeStruct((S, 2 * W), BF16),
        compiler_params=_cp(("parallel",)),
        name=name,
    )(p, p, sg_w, sg_bt, sg_g.reshape(1, W))


def sg_bwd(p, sg_w, sg_bt, sg_g, dcat, W, name):
    S = p.shape[0]
    G, nq = W // HD, S // HD

    def body(u_ref, v_ref, w_ref, bt_ref, g_ref, do_ref, dp_ref, dw_ref, dbt_ref, dg_ref, dvn_scr):
        i = pl.program_id(0)

        @pl.when(i == 0)
        def _():
            dw_ref[...] = jnp.zeros_like(dw_ref)
            dbt_ref[...] = jnp.zeros_like(dbt_ref)
            dg_ref[...] = jnp.zeros_like(dg_ref)

        v = v_ref[...].astype(F32)
        xh, r = _layernorm_parts(_gelu(v))
        gg = g_ref[...]
        vn = (xh * gg).astype(BF16)
        tri = _iota2((HD, HD), 0) >= _iota2((HD, HD), 1)
        for gi in range(G):
            cols = slice(gi * HD, (gi + 1) * HD)
            wt = jnp.where(tri, w_ref[gi], 0.0).astype(BF16)
            mixed = _dot(wt, vn[:, cols]) + bt_ref[:, gi : gi + 1]
            u = u_ref[:, cols].astype(F32)
            do = do_ref[:, cols].astype(F32)
            dp_ref[0, :, cols] = (do * mixed * _gelu_grad(u)).astype(dp_ref.dtype)
            dmix = do * _gelu(u)
            dmb = dmix.astype(BF16)
            dw_ref[gi] += jnp.where(tri, _dot_nt(dmb, vn[:, cols]), 0.0)
            dbt_ref[:, gi : gi + 1] += jnp.sum(dmix, axis=1, keepdims=True)
            dvn_scr[:, cols] = _dot_tn(wt, dmb)
        dvn = dvn_scr[...]
        dg_ref[...] += jnp.sum(dvn * xh, axis=0, keepdims=True)
        dxh = dvn * gg
        dgv = r * (dxh - jnp.mean(dxh, axis=-1, keepdims=True) - xh * jnp.mean(dxh * xh, axis=-1, keepdims=True))
        dp_ref[1] = (dgv * _gelu_grad(v)).astype(dp_ref.dtype)

    return pl.pallas_call(
        body,
        grid=(nq,),
        in_specs=[
            pl.BlockSpec((HD, W), lambda i: (i, 0)),
            pl.BlockSpec((HD, W), lambda i: (i, 1)),
            pl.BlockSpec((G, HD, HD), lambda i: (0, 0, 0)),
            pl.BlockSpec((HD, G), lambda i: (0, 0)),
            pl.BlockSpec((1, W), lambda i: (0, 0)),
            pl.BlockSpec((HD, W), lambda i: (i, 0)),
        ],
        out_specs=[
            pl.BlockSpec((2, HD, W), lambda i: (0, i, 0)),
            pl.BlockSpec((G, HD, HD), lambda i: (0, 0, 0)),
            pl.BlockSpec((HD, G), lambda i: (0, 0)),
            pl.BlockSpec((1, W), lambda i: (0, 0)),
        ],
        out_shape=[
            jax.ShapeDtypeStruct((6, S, W), BF16),
            jax.ShapeDtypeStruct((G, HD, HD), F32),
            jax.ShapeDtypeStruct((HD, G), F32),
            jax.ShapeDtypeStruct((1, W), F32),
        ],
        scratch_shapes=[pltpu.VMEM((HD, W), F32)],
        compiler_params=_cp(("arbitrary",)),
        name=name,
    )(p, p, sg_w, sg_bt, sg_g.reshape(1, W), dcat)


def local_step(x, target, wts, at, on_grad):
    S, D = x.shape
    W = D // 2
    nb, F = wts["nb"], wts["F"]
    g = {}

    def ffn_fwd(xin, l):
        h = rms_fwd(xin, wts[f"{l}_ffn_norm_g"], f"{l}_ffn_rms")
        u = mm_nn(h, wts[f"{l}_ffn_up"], nb, f"{l}_ffn_up_mm")
        act = ffn_act_fwd(u, wts[f"{l}_ffn_conv_w"], F, f"{l}_ffn_act")
        xout = mm_nn(act, wts[f"{l}_ffn_down"], 1, f"{l}_ffn_down_mm", out_dtype=F32, res=xin,
                     tm=_pick(S, (1024, 512, 256, 128)), tn=_pick(D, (512, 256, 128)), tk=F)
        return xout, (xin, h, u, act)

    def ffn_bwd(dxout, dxoutb, saved, l):
        xin, h, u, act = saved
        dact = mm_nt(dxoutb, wts[f"{l}_ffn_down"], 1, S, F, f"{l}_ffn_down_dx", tko=_pick(F, (512, 256, 128)), tn=D)
        dact = on_grad(f"{l}_ffn_down", mm_tn(act, dxoutb, 1, D, f"{l}_ffn_down_dw", tn=D), dact)
        du, dcw = ffn_act_bwd(u, wts[f"{l}_ffn_conv_w"], dact, F, f"{l}_ffn_act_bwd")
        g[f"{l}_ffn_conv_w"] = jnp.concatenate([dcw[0], dcw[1]], axis=1)
        du2 = du.reshape(2 * S, F)
        n = wts[f"{l}_ffn_up"].shape[1]
        tn = _pick(n, (1408, 1024, 768, 512, 256, 128))
        per_half = F // tn
        nt = n // tn

        def up_block(i, j, t):
            vb = j * nt + t
            return vb // per_half, vb % per_half

        tm = _pick(S, (1024, 512, 256, 128))

        def nt_map(i, j, t):
            half, cb = up_block(i, j, t)
            return (half * (S // tm) + i, cb)

        def tn_map(j, t):
            half, cb = up_block(0, j, t)
            return (half, cb)

        dh = mm_nt(du2, wts[f"{l}_ffn_up"], nb, S, D, f"{l}_ffn_up_dx", dy_maps=[nt_map], tm=tm, tko=D, tn=tn)
        dh = on_grad(f"{l}_ffn_up", mm_tn(h, du2, nb, n, f"{l}_ffn_up_dw", dy_maps=[tn_map], tko=_pick(D, (1024, 512, 256, 128)), tn=tn), dh)
        dxin, dxinb, dg = rms_bwd(xin, wts[f"{l}_ffn_norm_g"], dh, dxout, f"{l}_ffn_rms_bwd")
        g[f"{l}_ffn_norm_g"] = dg
        return dxin, dxinb

    h0 = rms_fwd(x, wts["l0_mix_norm_g"], "l0_mix_rms")
    p0 = mm_nn(h0, wts["l0_w_in"], nb, "l0_w_in_mm")
    cat0 = sb_fwd(p0, W, "l0_sb_fwd")
    cat0 = sc_fwd(p0, wts["l0_sc_conv_w"], cat0, W, "l0_sc_fwd")
    x1 = mm_nn(cat0, wts["l0_w_out"], 1, "l0_w_out_mm", out_dtype=F32, res=x, tm=S, tn=_pick(D, (512, 256, 128)))
    x2, ffn0_saved = ffn_fwd(x1, "l0")

    x2 = at("l1_w_in", x2, None)
    nh = W // HD
    h2 = rms_fwd(x2, wts["l1_mix_norm_g"], "l1_mix_rms")
    p1 = mm_nt(h2, wts["l1_w_in_t"], 1, S, 5 * W, "l1_w_in_mm", tn=D)
    f = mm_nt(h2, wts["l1_w_f_t"], 1, S, 128, "l1_w_f_mm", out_dtype=F32, tn=D)
    bf = jnp.zeros((1, 128), F32).at[0, :nh].set(wts["l1_fox_b_f"])
    c = fox_gate_fwd(f, bf, "l1_fox_gate")
    c_heads = c[:, :nh].T
    ccol = c_heads[:, :, None]
    crow = c_heads.reshape(nh, S // _key_strip(S), _key_strip(S))
    sg_bt = wts["l1_sg_b"].T
    cat1 = sg_fwd(p1, wts["l1_sg_w"], sg_bt, wts["l1_sg_norm_g"], W, "l1_sg_fwd")
    cat1, lse = fox_fwd(p1, ccol, crow, cat1, W, "l1_fox_fwd")
    x3 = mm_nn(cat1, wts["l1_w_out"], 1, "l1_w_out_mm", out_dtype=F32, res=x2, tm=S, tn=_pick(D, (512, 256, 128)))
    x4, ffn1_saved = ffn_fwd(x3, "l1")

    dx4, dx4b, dgf, loss = loss_head(x4, wts["final_norm_g"], target, "loss_head")
    dx4b = at("loss", dx4b, loss)
    g["final_norm_g"] = dgf

    dx3, dx3b = ffn_bwd(dx4, dx4b, ffn1_saved, "l1")
    dcat1 = mm_nt(dx3b, wts["l1_w_out"], 1, S, D, "l1_w_out_dx", tn=D)
    dcat1 = on_grad("l1_w_out", mm_tn(cat1, dx3b, 1, D, "l1_w_out_dw", tn=D), dcat1)
    dp1, dsgw, dsgbt, dsgg = sg_bwd(p1, wts["l1_sg_w"], sg_bt, wts["l1_sg_norm_g"], dcat1, W, "l1_sg_bwd")
    dp1, dcs, dct = fox_bwd(p1, ccol, crow, cat1, lse, dcat1, dp1, W, "l1_fox_bwd")
    g["l1_sg_w"], g["l1_sg_b"], g["l1_sg_norm_g"] = dsgw, dsgbt.T, dsgg
    dc = jnp.zeros((S, 128), F32).at[:, :nh].set((dct[:, :, 0] - dcs.reshape(nh, S)).T)
    df, dbf = fox_gate_bwd(f, bf, dc, "l1_fox_gate_bwd")
    g["l1_fox_b_f"] = dbf[0, :nh]
    dfb = df.astype(BF16)
    tk1 = _pick(W, (1024, 512, 256, 128))
    tx1 = _pick(W, (512, 256, 128))
    tm1 = _pick(S, (1024, 512, 256, 128))
    part_of = lambda pt: pt + pt // 2 - pt // 4

    def a_map1(i, k):
        return (part_of(k // (W // tk1)) * (S // tm1) + i, k % (W // tk1))

    def x_map1(ko):
        return (part_of(ko // (W // tx1)), ko % (W // tx1))

    dp1_2d = dp1.reshape(6 * S, W)
    dw_main = mm_tn(dp1_2d, h2, 1, D, "l1_w_in_dw", tko=tx1, tn=D, x_map=x_map1, x_shape=(S, 5 * W))
    dw_f = mm_tn(dfb, h2, 1, D, "l1_w_f_dw", tn=D)
    dh2 = mm_nn(dfb, wts["l1_w_f_t"], 1, "l1_w_f_dx", out_dtype=F32)
    dh2 = mm_nn(dp1_2d, wts["l1_w_in_t"], 1, "l1_w_in_dx", res=dh2, tm=tm1, tk=tk1, a_map=a_map1, a_shape=(S, 5 * W))
    dh2 = on_grad("l1_w_in", jnp.concatenate([dw_main, dw_f[:nh]], axis=0), dh2)
    dx2, dx2b, dg = rms_bwd(x2, wts["l1_mix_norm_g"], dh2, dx3, "l1_mix_rms_bwd")
    g["l1_mix_norm_g"] = dg

    dx1, dx1b = ffn_bwd(dx2, dx2b, ffn0_saved, "l0")
    dcat0 = mm_nt(dx1b, wts["l0_w_out"], 1, S, D, "l0_w_out_dx", tn=D)
    dcat0 = on_grad("l0_w_out", mm_tn(cat0, dx1b, 1, D, "l0_w_out_dw", tn=D), dcat0)
    dp0 = sb_bwd(p0, dcat0, W, "l0_sb_bwd")
    dp0, dscw = sc_bwd(p0, wts["l0_sc_conv_w"], dcat0, dp0, W, "l0_sc_bwd")
    g["l0_sc_conv_w"] = dscw
    dp0 = at("small_ready", dp0, g)
    n0 = wts["l0_w_in"].shape[1]
    td0 = math.gcd(n0, W)
    nd0 = n0 // td0
    tm0 = _pick(S, (1024, 512, 256, 128))
    per_part0 = W // td0

    def nt_maps0(k):
        def f(i, j, t):
            vb = j * nd0 + k
            return ((vb // per_part0) * (S // tm0) + i, vb % per_part0)
        return f

    def tn_maps0(k):
        def f(j, t):
            vb = j * nd0 + k
            return (vb // per_part0, vb % per_part0)
        return f

    dp0_2d = dp0.reshape(6 * S, W)
    dw0 = mm_tn(h0, dp0_2d, nb, n0, "l0_w_in_dw", dy_maps=[tn_maps0(k) for k in range(nd0)], tko=_pick(D, (1024, 512, 256, 128)), tn=n0)
    dp0_2d = on_grad("l0_w_in", dw0, dp0_2d)
    dp0_2d = on_grad(None, None, dp0_2d)
    dh0 = mm_nt(dp0_2d, wts["l0_w_in"], nb, S, D, "l0_w_in_dx", dy_maps=[nt_maps0(k) for k in range(nd0)], tm=tm0, tko=D, tn=n0)
    dh0 = at("small_done", dh0, None)
    dx0, _, dg = rms_bwd(x, wts["l0_mix_norm_g"], dh0, dx1, "l0_mix_rms_bwd")
    g["l0_mix_norm_g"] = dg
    return dx0, g


GATHER_ID = 1


def _place():
    return lax.axis_index("x"), lax.axis_index("y"), lax.axis_index("c")


def _other_chips(x, y):
    return [(x, 1 - y), (1 - x, y), (1 - x, 1 - y)]


def _handshake(peers):
    barrier = pltpu.get_barrier_semaphore()
    for peer in peers:
        pl.semaphore_signal(barrier, inc=1, device_id=peer, device_id_type=MESH)
    pl.semaphore_wait(barrier, len(peers))


UPDATE_LAG = 2


def _on_sequencer(body, out_type, scratch_types, collective_id, name):
    return pl.kernel(
        body,
        out_type=out_type,
        mesh=plsc.ScalarSubcoreMesh(axis_name="seq", num_cores=1),
        scratch_types=scratch_types,
        compiler_params=pltpu.CompilerParams(collective_id=collective_id),
        name=name,
    )


def all_gather(arrs, name):
    n = len(arrs)

    def body(*refs):
        xs, outs = refs[:n], refs[n : 2 * n]
        send_sems, recv_sems, local_sems = refs[2 * n :]
        x, y, c = _place()
        me, sibling = (x, y, c), (x, y, 1 - c)
        chips = _other_chips(x, y)
        _handshake([sibling] + [(*chip, c) for chip in chips])

        def copy(a, k, block, to, src=None):
            px, py, pc = block
            dst = outs[a].at[4 * px + 2 * py + pc]
            return pltpu.make_async_remote_copy(
                src_ref=dst if src is None else src, dst_ref=dst,
                send_sem=send_sems.at[7 * a + k], recv_sem=recv_sems.at[7 * a + k], device_id=to, device_id_type=MESH,
            )

        mine = [pltpu.make_async_copy(xs[a], outs[a].at[4 * x + 2 * y + c], local_sems.at[a]) for a in range(n)]
        for cp in mine:
            cp.start()
        first = []
        for a in range(n):
            first.append(copy(a, 0, me, sibling, src=xs[a]))
            first += [copy(a, 1 + j, me, (*chip, c), src=xs[a]) for j, chip in enumerate(chips)]
        for cp in first:
            cp.start()
        passed = []
        for a in range(n):
            for j, chip in enumerate(chips):
                copy(a, 1 + j, (*chip, c), me).wait_recv()
                cp = copy(a, 4 + j, (*chip, c), sibling)
                cp.start()
                passed.append(cp)
        for a in range(n):
            copy(a, 0, sibling, me).wait_recv()
            for j, chip in enumerate(chips):
                copy(a, 4 + j, (*chip, 1 - c), me).wait_recv()
        for cp in first + passed:
            cp.wait_send()
        for cp in mine:
            cp.wait()

    out_type = [jax.ShapeDtypeStruct((NDEV,) + a.shape, a.dtype) for a in arrs]
    sems = [pltpu.SemaphoreType.DMA((7 * n,)), pltpu.SemaphoreType.DMA((7 * n,)), pltpu.SemaphoreType.DMA((n,))]
    return _on_sequencer(body, out_type, sems, GATHER_ID, name)(*arrs)


_IN_HBM = pl.BlockSpec(memory_space=pltpu.HBM)
_IN_SEM = pl.BlockSpec(memory_space=pltpu.SEMAPHORE)
_EFFECT = pltpu.SideEffectType.DATAFLOW_SIDE_EFFECTING


def _split_start(make_copies, src, land_shape, nsem, name):
    def body(src_ref, land_ref, send_sems, recv_sems, land_thru, token):
        for cp in make_copies(src_ref, land_ref, send_sems, recv_sems):
            cp.start()
        token[...] = jnp.zeros_like(token)

    send_sems, recv_sems, land_thru, token = pl.pallas_call(
        body,
        name=name,
        out_shape=(
            pltpu.SemaphoreType.DMA((nsem,)), pltpu.SemaphoreType.DMA((nsem,)),
            pltpu.HBM(land_shape, src.dtype), jax.ShapeDtypeStruct((8, 128), F32),
        ),
        in_specs=(_IN_HBM, _IN_HBM),
        out_specs=(_IN_SEM, _IN_SEM, _IN_HBM, pl.BlockSpec(memory_space=pltpu.VMEM)),
        input_output_aliases={1: 2},
        compiler_params=pltpu.CompilerParams(has_side_effects=_EFFECT),
    )(src, pltpu.with_memory_space_constraint(lax.empty(land_shape, src.dtype), pltpu.HBM))
    return send_sems, recv_sems, src, land_thru, token


def _split_wait(make_copies, send_sems, recv_sems, src_thru, land_thru, after, name):
    def body(src_ref, land_ref, send_sems, recv_sems, after_ref, land_out):
        for cp in make_copies(src_ref, land_ref, send_sems, recv_sems):
            cp.wait_send()
            cp.wait_recv()

    return pl.pallas_call(
        body,
        name=name,
        out_shape=pltpu.HBM(land_thru.shape, land_thru.dtype),
        in_specs=(_IN_HBM, _IN_HBM, _IN_SEM, _IN_SEM, pl.BlockSpec(memory_space=pl.ANY)),
        out_specs=_IN_HBM,
        input_output_aliases={1: 0},
        compiler_params=pltpu.CompilerParams(has_side_effects=_EFFECT),
    )(src_thru, land_thru, send_sems, recv_sems, after)


def _pair_copies(src_ref, land_ref, send_sems, recv_sems):
    x, y, c = _place()
    return [
        pltpu.make_async_remote_copy(
            src_ref=src_ref.at[k, 1 - c], dst_ref=land_ref.at[k],
            send_sem=send_sems.at[k], recv_sem=recv_sems.at[k], device_id=(x, y, 1 - c), device_id_type=MESH,
        )
        for k in range(4)
    ]


def _direct_copies(src_ref, land_ref, send_sems, recv_sems):
    x, y, c = _place()
    me = 4 * x + 2 * y + c
    copies = []
    for k in range(NDEV - 1):
        to = (me + k + 1) % NDEV
        copies.append(pltpu.make_async_remote_copy(
            src_ref=src_ref, dst_ref=land_ref.at[me], send_sem=send_sems.at[k], recv_sem=recv_sems.at[k],
            device_id=(to // 4, (to // 2) % 2, to % 2), device_id_type=MESH,
        ))
    return copies


def _chip_copies(src_ref, land_ref, send_sems, recv_sems):
    x, y, c = _place()
    return [
        pltpu.make_async_remote_copy(
            src_ref=src_ref.at[2 * px + py], dst_ref=land_ref.at[2 * x + y],
            send_sem=send_sems.at[j], recv_sem=recv_sems.at[j], device_id=(px, py, c), device_id_type=MESH,
        )
        for j, (px, py) in enumerate(_other_chips(x, y))
    ]


def _row_tile(R, C, max_elems):
    if R * C <= max_elems:
        return R
    best = None
    for tr in range(16, R, 16):
        if R % tr == 0 and tr * C <= max_elems:
            best = tr
    return best or R


def pair_sum(a42, land4, core, name):
    _, _, R, C = a42.shape
    tr = _row_tile(R, C, 1 << 21)

    def body(core_ref, a_ref, l_ref, o_ref):
        o_ref[...] = (a_ref[0].astype(F32) + l_ref[...].astype(F32)).astype(o_ref.dtype)

    return pl.pallas_call(
        body,
        grid_spec=pltpu.PrefetchScalarGridSpec(
            num_scalar_prefetch=1,
            grid=(4, R // tr),
            in_specs=[
                pl.BlockSpec((1, 1, tr, C), lambda k, r, core_ref: (k, core_ref[0], r, 0)),
                pl.BlockSpec((1, tr, C), lambda k, r, core_ref: (k, r, 0)),
            ],
            out_specs=pl.BlockSpec((1, tr, C), lambda k, r, core_ref: (k, r, 0)),
        ),
        out_shape=jax.ShapeDtypeStruct((4, R, C), BF16),
        compiler_params=_cp(("parallel", "parallel")),
        name=name,
    )(core, a42, land4)


def sum_slots(parts, name):
    P, R, C = parts.shape

    def body(p_ref, o_ref):
        acc = p_ref[0].astype(F32)
        for k in range(1, P):
            acc = acc + p_ref[k].astype(F32)
        o_ref[...] = acc

    tr = _row_tile(R, P * C, 1 << 21)
    return pl.pallas_call(
        body,
        grid=(R // tr,),
        in_specs=[pl.BlockSpec((P, tr, C), lambda r: (0, r, 0))],
        out_specs=pl.BlockSpec((tr, C), lambda r: (r, 0)),
        out_shape=jax.ShapeDtypeStruct((R, C), F32),
        compiler_params=_cp(("parallel",)),
        name=name,
    )(parts)


def adamw(w, m, v, parts, name):
    R, C = w.shape
    P = parts.shape[0]
    tr = _pick(R, (256, 128, 64, 32, 16, 8))
    c1 = 1.0 - ADAM_B1 ** ADAM_STEP
    c2 = 1.0 - ADAM_B2 ** ADAM_STEP

    def body(w_ref, m_ref, v_ref, p_ref, g_ref, d_ref, nm_ref, nv_ref):
        g = p_ref[0].astype(F32)
        for k in range(1, P):
            g = g + p_ref[k].astype(F32)
        nm = ADAM_B1 * m_ref[...] + (1.0 - ADAM_B1) * g
        nv = ADAM_B2 * v_ref[...] + (1.0 - ADAM_B2) * (g * g)
        g_ref[...] = g
        nm_ref[...] = nm
        nv_ref[...] = nv
        d_ref[...] = -ADAM_LR * ((nm / c1) / (jnp.sqrt(nv / c2) + ADAM_EPS) + ADAM_WD * w_ref[...])

    blk = pl.BlockSpec((tr, C), lambda r: (r, 0))
    shp = jax.ShapeDtypeStruct((R, C), F32)
    return pl.pallas_call(
        body,
        grid=(R // tr,),
        in_specs=[blk, blk, blk, pl.BlockSpec((P, tr, C), lambda r: (0, r, 0))],
        out_specs=[blk, blk, blk, blk],
        out_shape=[shp, shp, shp, shp],
        compiler_params=_cp(("parallel",)),
        name=name,
    )(w, m, v, parts)


def adamw_reduced(w, m, v, own, land, chip, name):
    R, C = w.shape
    if R % 8 == 0:
        tr, tc = _pick(R, (256, 128, 64, 32, 16, 8)), C
    else:
        tr, tc = R, _pick(C, (256, 128))
    c1 = 1.0 - ADAM_B1 ** ADAM_STEP
    c2 = 1.0 - ADAM_B2 ** ADAM_STEP

    def body(chip_ref, w_ref, m_ref, v_ref, own_ref, land_ref, g_ref, d_ref, nm_ref, nv_ref):
        mine = own_ref[0].astype(F32)
        g = None
        for k in range(4):
            term = jnp.where(chip_ref[0] == k, mine, land_ref[k].astype(F32))
            g = term if g is None else g + term
        nm = ADAM_B1 * m_ref[...] + (1.0 - ADAM_B1) * g
        nv = ADAM_B2 * v_ref[...] + (1.0 - ADAM_B2) * (g * g)
        g_ref[...] = g
        nm_ref[...] = nm
        nv_ref[...] = nv
        d_ref[...] = -ADAM_LR * ((nm / c1) / (jnp.sqrt(nv / c2) + ADAM_EPS) + ADAM_WD * w_ref[...])

    blk = pl.BlockSpec((tr, tc), lambda r, c, chip_ref: (r, c))
    shp = jax.ShapeDtypeStruct((R, C), F32)
    return pl.pallas_call(
        body,
        grid_spec=pltpu.PrefetchScalarGridSpec(
            num_scalar_prefetch=1,
            grid=(R // tr, C // tc),
            in_specs=[
                blk, blk, blk,
                pl.BlockSpec((1, tr, tc), lambda r, c, chip_ref: (chip_ref[0], r, c)),
                pl.BlockSpec((4, tr, tc), lambda r, c, chip_ref: (0, r, c)),
            ],
            out_specs=[blk, blk, blk, blk],
        ),
        out_shape=[shp, shp, shp, shp],
        compiler_params=_cp(("parallel", "parallel")),
        name=name,
    )(chip, w, m, v, own, land)


_WEIGHTS = [
    "l0_mix_norm_g", "l0_w_in", "l0_sc_conv_w", "l0_w_out", "l0_ffn_norm_g", "l0_ffn_up", "l0_ffn_conv_w", "l0_ffn_down",
    "l1_mix_norm_g", "l1_w_in", "l1_fox_b_f", "l1_sg_w", "l1_sg_b", "l1_sg_norm_g", "l1_w_out", "l1_ffn_norm_g",
    "l1_ffn_up", "l1_ffn_conv_w", "l1_ffn_down", "final_norm_g",
]
_ROW_SHARDED = ["l0_w_out", "l0_ffn_down", "l1_w_out", "l1_ffn_down"]
_BIG = ["l0_w_in", "l0_w_out", "l0_ffn_up", "l0_ffn_down", "l1_w_in", "l1_w_out", "l1_ffn_up", "l1_ffn_down"]
_CONV = ["l0_sc_conv_w", "l0_ffn_conv_w", "l1_ffn_conv_w"]
_SMALL = [n for n in _WEIGHTS if n not in _BIG]
_LAST_SMALL = "l0_mix_norm_g"
_PACK_ROWS = 8


def _pack(arrs):
    flat = []
    for a in arrs:
        v = a.reshape(-1).astype(F32)
        pad = (-v.shape[0]) % (_PACK_ROWS * 128)
        flat.append(jnp.pad(v, (0, pad)))
    return jnp.concatenate(flat).reshape(-1, 128)


def _unpack(packed, shapes):
    out, off = [], 0
    flat = packed.reshape(-1)
    for shp in shapes:
        size = math.prod(shp)
        out.append(flat[off : off + size].reshape(shp))
        off += size + (-size) % (_PACK_ROWS * 128)
    return out


def kernel(x, l0_mix_norm_g, l0_w_in, l0_sc_conv_w, l0_w_out, l0_ffn_norm_g, l0_ffn_up, l0_ffn_conv_w, l0_ffn_down, l1_mix_norm_g, l1_w_in, l1_fox_b_f, l1_sg_w, l1_sg_b, l1_sg_norm_g, l1_w_out, l1_ffn_norm_g, l1_ffn_up, l1_ffn_conv_w, l1_ffn_down, final_norm_g, loss_target, m_l0_mix_norm_g, m_l0_w_in, m_l0_sc_conv_w, m_l0_w_out, m_l0_ffn_norm_g, m_l0_ffn_up, m_l0_ffn_conv_w, m_l0_ffn_down, m_l1_mix_norm_g, m_l1_w_in, m_l1_fox_b_f, m_l1_sg_w, m_l1_sg_b, m_l1_sg_norm_g, m_l1_w_out, m_l1_ffn_norm_g, m_l1_ffn_up, m_l1_ffn_conv_w, m_l1_ffn_down, m_final_norm_g, v_l0_mix_norm_g, v_l0_w_in, v_l0_sc_conv_w, v_l0_w_out, v_l0_ffn_norm_g, v_l0_ffn_up, v_l0_ffn_conv_w, v_l0_ffn_down, v_l1_mix_norm_g, v_l1_w_in, v_l1_fox_b_f, v_l1_sg_w, v_l1_sg_b, v_l1_sg_norm_g, v_l1_w_out, v_l1_ffn_norm_g, v_l1_ffn_up, v_l1_ffn_conv_w, v_l1_ffn_down, v_final_norm_g):
    given = dict(locals())
    w = {n: given[n] for n in _WEIGHTS}
    mom = {n: given["m_" + n] for n in _WEIGHTS}
    var = {n: given["v_" + n] for n in _WEIGHTS}
    xs, target = x[0], loss_target[0]
    S, D = xs.shape
    W = D // 2
    nh = W // HD
    cx, cy, cc = _place()
    me = 4 * cx + 2 * cy + cc

    wts = {"nb": NDEV, "F": l0_ffn_down.shape[0] * NDEV}
    for n in _SMALL:
        if n not in _CONV:
            wts[n] = w[n]
    gathered, loss_sum = {}, []

    def start_gather(names):
        srcs = [(w[n].T if n == "l1_w_in" else w[n]).astype(BF16) for n in names]
        taps = [w[c] for c in _CONV] if names[0] == _BIG[0] else []
        got = all_gather(srcs + taps, "gather_" + "_".join(names))
        for n, full in zip(names, got):
            if n == "l1_w_in":
                gathered[n] = full
            elif n in _ROW_SHARDED:
                wts[n] = full.reshape(-1, D)
            else:
                wts[n] = full.reshape(NDEV * D, -1)
        for c, full in zip(_CONV, got[len(names):] if taps else []):
            wts[c] = full.transpose(1, 0, 2).reshape(CONV_K, -1)

    def at(point, after, value):
        if point == "l1_w_in":
            got, after = lax.optimization_barrier((gathered[point], after))
            wts["l1_w_in_t"] = got.reshape(-1, D)
            wts["l1_w_f_t"] = jnp.pad(wts["l1_w_in_t"][5 * W :], ((0, 128 - nh), (0, 0)))
        elif point == "loss":
            gathered["loss"] = value[0, :1]
        elif point == "small_ready":
            early = [n for n in _SMALL if n != _LAST_SMALL]
            gathered["small"] = all_gather([_pack([value[n] for n in early] + [gathered["loss"]])], "gather_small_grads")[0]
        elif point == "small_done":
            after = update_small([n for n in _SMALL if n != _LAST_SMALL], gathered["small"], "small", after, True)
        return after

    out_g, out_d, out_m, out_v = {}, {}, {}, {}

    def update_small(names, all_terms, tag, after=None, with_loss=False):
        shapes = [w[n].shape for n in names]
        full_shapes = [(CONV_K, NDEV * w[n].shape[1]) if n in _CONV else w[n].shape for n in names]
        summed = _unpack(sum_slots(all_terms, f"sum_{tag}_grads"), full_shapes + ([(1,)] if with_loss else []))
        if with_loss:
            loss_sum.append(summed[-1][0])
        grads = {}
        for n, t in zip(names, summed):
            if n in _CONV:
                cols = w[n].shape[1]
                t = lax.dynamic_slice_in_dim(t, me * cols, cols, axis=1)
            grads[n] = t
        res = adamw(
            _pack([w[n] for n in names]), _pack([mom[n] for n in names]), _pack([var[n] for n in names]),
            _pack([grads[n] for n in names])[None], f"adamw_{tag}",
        )
        if after is not None:
            res, after = lax.optimization_barrier((res, after))
        for dst, packed_out in zip((out_g, out_d, out_m, out_v), res):
            for n, t in zip(names, _unpack(packed_out, shapes)):
                dst[n] = t
        return after

    core = jnp.reshape(cc, (1,)).astype(jnp.int32)
    chip = jnp.reshape(2 * cx + cy, (1,)).astype(jnp.int32)
    pair_flying, chip_flying = [], []

    def tie(value, after):
        if after is None:
            return value, None
        return lax.optimization_barrier((value, after))

    def to_chips(after):
        n, flying = pair_flying.pop()
        landed = _split_wait(_pair_copies, *flying, f"reduce_pair_wait_{n}")
        summed = pair_sum(flying[2], landed, core, f"pair_sum_{n}")
        *flying, token = _split_start(_chip_copies, summed, summed.shape, 3, f"reduce_chips_{n}")
        token, after = tie(token, after)
        chip_flying.append((n, flying + [token]))
        return after

    def update(after, behind=None):
        n, flying = chip_flying.pop(0)
        if behind is not None:
            flying[4], _ = lax.optimization_barrier((flying[4], behind))
        landed = _split_wait(_chip_copies, *flying, f"reduce_chips_wait_{n}")
        turn = (lambda t: t.T) if n == "l1_w_in" else (lambda t: t)
        res = adamw_reduced(turn(w[n]), turn(mom[n]), turn(var[n]), flying[2], landed, chip, f"adamw_{n}")
        res, after = tie(res, after)
        out_g[n], out_d[n], out_m[n], out_v[n] = [turn(t) for t in res]
        return after, res[0]

    def on_grad(n, term, after):
        if n is None:
            return to_chips(after)
        if n in _ROW_SHARDED or n == "l1_w_in":
            term = term.reshape(NDEV, -1, D)
        else:
            term = term.reshape(NDEV, D, -1)
        term = term.reshape((4, 2) + term.shape[1:])
        *flying, token = _split_start(_pair_copies, term, term.shape[:1] + term.shape[2:], 4, f"reduce_pair_{n}")
        token, after = tie(token, after)
        if len(chip_flying) == UPDATE_LAG:
            after, _ = update(after)
        if pair_flying:
            after = to_chips(after)
        pair_flying.append((n, flying + [token]))
        return after

    for n in _BIG:
        start_gather([n])
    dx, g = local_step(xs, target, wts, at, on_grad)
    last = _pack([g[_LAST_SMALL]])
    *flying, done = _split_start(_direct_copies, last, (NDEV,) + last.shape, NDEV - 1, "gather_last_grad")
    while len(chip_flying) > 1:
        _, done = update(None, behind=done)
    landed = _split_wait(_direct_copies, *flying, done, "gather_last_grad_wait")
    update_small([_LAST_SMALL], lax.dynamic_update_slice(landed, last[None], (me, 0, 0)), "last")
    update(None, behind=out_g[_LAST_SMALL])
    loss = loss_sum[0]

    return (loss, dx[None], *[out_g[n] for n in _WEIGHTS], *[out_d[n] for n in _WEIGHTS],
            *[out_m[n] for n in _WEIGHTS], *[out_v[n] for n in _WEIGHTS])
```

```python
import functools
import math

import jax
import jax.numpy as jnp
from jax import lax
from jax.experimental import pallas as pl
from jax.experimental.pallas import tpu as pltpu
from jax.experimental.pallas import tpu_sc as plsc

F32 = jnp.float32
BF16 = jnp.bfloat16
HD = 128
EPS = 1e-6
CONV_K = 3
VMEM_LIMIT_BYTES = 48 << 20
NDEV = 8
MESH = pl.DeviceIdType.MESH

ADAM_LR = 0.001
ADAM_B1 = 0.9
ADAM_B2 = 0.999
ADAM_EPS = 1e-08
ADAM_WD = 0.01
ADAM_STEP = 10


def _cp(sem):
    return pltpu.CompilerParams(dimension_semantics=sem, vmem_limit_bytes=VMEM_LIMIT_BYTES)


def _pick(n, prefs):
    for p in prefs:
        if n % p == 0:
            return p
    return n


def _dot(a, b):
    return jnp.dot(a, b, preferred_element_type=F32)


def _dot_nt(a, b):
    return lax.dot_general(a, b, (((1,), (1,)), ((), ())), preferred_element_type=F32)


def _dot_tn(a, b):
    return lax.dot_general(a, b, (((0,), (0,)), ((), ())), preferred_element_type=F32)


def _split3(x):
    hi = x.astype(BF16)
    r = x - hi.astype(F32)
    mid = r.astype(BF16)
    lo = (r - mid.astype(F32)).astype(BF16)
    return hi, mid, lo


def _dot_ones_left(ones_bf16, x):
    hi, mid, lo = _split3(x)
    return _dot(ones_bf16, hi) + _dot(ones_bf16, mid) + _dot(ones_bf16, lo)


def _iota2(shape, axis):
    return lax.broadcasted_iota(jnp.int32, shape, axis)


def mm_nn(a, w2d, nb, name, out_dtype=BF16, res=None, tm=None, tn=None, tk=None, a_map=None, a_shape=None):
    M, K = a_shape or a.shape
    n = w2d.shape[1]
    assert w2d.shape[0] == nb * K or (nb == 1 and w2d.shape[0] > K)
    a_map = a_map or (lambda i, k: (i, k))
    tm = tm or _pick(M, (1024, 512, 256, 128))
    tn = tn or _pick(n, (1408, 1024, 768, 512, 256, 128))
    tk = tk or (K if K <= 2048 else _pick(K, (1408, 1024, 512, 256, 128)))
    nk, nt = K // tk, n // tn
    has_res = res is not None

    def body(*refs):
        if has_res:
            a_ref, w_ref, r_ref, o_ref = refs[:4]
        else:
            a_ref, w_ref, o_ref = refs[:3]
            r_ref = None
        part = _dot(a_ref[...], w_ref[...])

        def finish(acc):
            if r_ref is not None:
                acc = acc + r_ref[...].astype(F32)
            o_ref[...] = acc.astype(o_ref.dtype)

        if nk == 1:
            finish(part)
        else:
            acc_ref = refs[-1]
            k = pl.program_id(3)

            @pl.when(k == 0)
            def _():
                acc_ref[...] = part

            @pl.when(k > 0)
            def _():
                acc_ref[...] += part

            @pl.when(k == nk - 1)
            def _():
                finish(acc_ref[...])

    in_specs = [
        pl.BlockSpec((tm, tk), lambda i, j, t, k: a_map(i, k)),
        pl.BlockSpec((tk, tn), lambda i, j, t, k: (j * nk + k, t)),
    ]
    args = [a, w2d]
    out_spec = pl.BlockSpec((tm, tn), lambda i, j, t, k: (i, j * nt + t))
    if has_res:
        in_specs.append(out_spec)
        args.append(res)
    return pl.pallas_call(
        body,
        grid=(M // tm, nb, nt, nk),
        in_specs=in_specs,
        out_specs=out_spec,
        out_shape=jax.ShapeDtypeStruct((M, nb * n), out_dtype),
        scratch_shapes=[pltpu.VMEM((tm, tn), F32)] if nk > 1 else [],
        compiler_params=_cp(("parallel", "parallel", "parallel", "arbitrary")),
        name=name,
    )(*args)


def mm_nt(dy2d, w2d, nb, M, K, name, out_dtype=BF16, res=None, dy_maps=None, tm=None, tko=None, tn=None):
    n = w2d.shape[1]
    assert w2d.shape[0] == nb * K or (nb == 1 and w2d.shape[0] > K)
    tm = tm or _pick(M, (1024, 512, 256, 128))
    tko = tko or _pick(K, (1024, 512, 256, 128))
    tn = tn or _pick(n, (1408, 1024, 768, 512, 256, 128))
    nt, nko = n // tn, K // tko
    has_res = res is not None
    if dy_maps is None:
        dy_maps = [lambda i, j, t: (i, j * nt + t)]
    nd = len(dy_maps)
    td = tn // nd

    one_step = nb * nt == 1

    def body(*refs):
        d_refs, w_ref = refs[:nd], refs[nd]
        r_ref = refs[nd + 1] if has_res else None
        d = d_refs[0][...] if nd == 1 else jnp.concatenate([r[...] for r in d_refs], axis=1)
        part = _dot_nt(d, w_ref[...])
        if one_step:
            o_ref = refs[-1]
            if r_ref is not None:
                part = part + r_ref[...].astype(F32)
            o_ref[...] = part.astype(o_ref.dtype)
            return
        o_ref, acc_ref = refs[-2], refs[-1]
        j, t = pl.program_id(2), pl.program_id(3)
        first = jnp.logical_and(j == 0, t == 0)
        last = jnp.logical_and(j == nb - 1, t == nt - 1)

        @pl.when(first)
        def _():
            acc_ref[...] = part

        @pl.when(jnp.logical_not(first))
        def _():
            acc_ref[...] += part

        @pl.when(last)
        def _():
            acc = acc_ref[...]
            if r_ref is not None:
                acc = acc + r_ref[...].astype(F32)
            o_ref[...] = acc.astype(o_ref.dtype)

    in_specs = [pl.BlockSpec((tm, td), functools.partial(lambda f, i, ko, j, t: f(i, j, t), f)) for f in dy_maps]
    in_specs.append(pl.BlockSpec((tko, tn), lambda i, ko, j, t: (j * nko + ko, t)))
    args = [dy2d] * nd + [w2d]
    out_spec = pl.BlockSpec((tm, tko), lambda i, ko, j, t: (i, ko))
    if has_res:
        in_specs.append(out_spec)
        args.append(res)
    return pl.pallas_call(
        body,
        grid=(M // tm, nko, nb, nt),
        in_specs=in_specs,
        out_specs=out_spec,
        out_shape=jax.ShapeDtypeStruct((M, K), out_dtype),
        scratch_shapes=[] if one_step else [pltpu.VMEM((tm, tko), F32)],
        compiler_params=_cp(("parallel", "parallel", "arbitrary", "arbitrary")),
        name=name,
    )(*args)


def mm_tn(x, dy2d, nb, n, name, out_dtype=BF16, dy_maps=None, tko=None, tn=None, x_map=None, x_shape=None):
    S, K = x_shape or x.shape
    x_map = x_map or (lambda ko: (0, ko))
    tko = tko or _pick(K, (512, 256, 128))
    tn = tn or _pick(n, (1408, 1024, 768, 512, 256, 128))
    nt, nko = n // tn, K // tko
    if dy_maps is None:
        dy_maps = [lambda j, t: (0, j * nt + t)]
    nd = len(dy_maps)
    td = tn // nd

    def body(*refs):
        x_ref, d_refs, o_ref = refs[0], refs[1 : 1 + nd], refs[-1]
        d = d_refs[0][...] if nd == 1 else jnp.concatenate([r[...] for r in d_refs], axis=1)
        o_ref[...] = _dot_tn(x_ref[...], d).astype(o_ref.dtype)

    in_specs = [pl.BlockSpec((S, tko), lambda ko, j, t: x_map(ko))]
    in_specs += [pl.BlockSpec((S, td), functools.partial(lambda f, ko, j, t: f(j, t), f)) for f in dy_maps]
    return pl.pallas_call(
        body,
        grid=(nko, nb, nt),
        in_specs=in_specs,
        out_specs=pl.BlockSpec((tko, tn), lambda ko, j, t: (j * nko + ko, t)),
        out_shape=jax.ShapeDtypeStruct((nb * K, n), out_dtype),
        compiler_params=_cp(("parallel", "parallel", "parallel")),
        name=name,
    )(x, *([dy2d] * nd))


def rms_fwd(x, g, name):
    S, D = x.shape
    tm = _pick(S, (512, 256, 128))

    def body(x_ref, g_ref, o_ref):
        xf = x_ref[...]
        r = lax.rsqrt(jnp.mean(xf * xf, axis=-1, keepdims=True) + EPS)
        o_ref[...] = (xf * r * g_ref[...]).astype(o_ref.dtype)

    return pl.pallas_call(
        body,
        grid=(S // tm,),
        in_specs=[pl.BlockSpec((tm, D), lambda i: (i, 0)), pl.BlockSpec((1, D), lambda i: (0, 0))],
        out_specs=pl.BlockSpec((tm, D), lambda i: (i, 0)),
        out_shape=jax.ShapeDtypeStruct((S, D), BF16),
        compiler_params=_cp(("parallel",)),
        name=name,
    )(x, g.reshape(1, D))


def rms_bwd(x, g, dh, dres, name):
    S, D = x.shape
    tm = _pick(S, (256, 128))

    def body(x_ref, g_ref, dh_ref, dr_ref, dx_ref, dxb_ref, dg_ref):
        i = pl.program_id(0)
        xf = x_ref[...]
        dh = dh_ref[...].astype(F32)
        r = lax.rsqrt(jnp.mean(xf * xf, axis=-1, keepdims=True) + EPS)
        gy = dh * g_ref[...]
        proj = jnp.mean(gy * xf, axis=-1, keepdims=True)
        dx = dr_ref[...] + r * gy - xf * (r * r * r * proj)
        dx_ref[...] = dx
        dxb_ref[...] = dx.astype(BF16)
        dg = jnp.sum(dh * (xf * r), axis=0, keepdims=True)

        @pl.when(i == 0)
        def _():
            dg_ref[...] = dg

        @pl.when(i > 0)
        def _():
            dg_ref[...] += dg

    row = pl.BlockSpec((tm, D), lambda i: (i, 0))
    vec = pl.BlockSpec((1, D), lambda i: (0, 0))
    return pl.pallas_call(
        body,
        grid=(S // tm,),
        in_specs=[row, vec, row, row],
        out_specs=[row, row, vec],
        out_shape=[jax.ShapeDtypeStruct((S, D), F32), jax.ShapeDtypeStruct((S, D), BF16), jax.ShapeDtypeStruct((1, D), F32)],
        compiler_params=_cp(("arbitrary",)),
        name=name,
    )(x, g.reshape(1, D), dh, dres)


def loss_head(x, g, target, name):
    S, D = x.shape
    tm = _pick(S, (256, 128))

    def body(x_ref, g_ref, t_ref, dx_ref, dxb_ref, dg_ref, loss_ref):
        i = pl.program_id(0)
        xf = x_ref[...]
        gg = g_ref[...]
        r = lax.rsqrt(jnp.mean(xf * xf, axis=-1, keepdims=True) + EPS)
        xh = xf * r
        err = xh * gg - t_ref[...]
        part = (0.5 / D) * jnp.sum(err * err)
        dy = err * (1.0 / D)
        gy = dy * gg
        proj = jnp.mean(gy * xf, axis=-1, keepdims=True)
        dx = r * gy - xf * (r * r * r * proj)
        dx_ref[...] = dx
        dxb_ref[...] = dx.astype(BF16)
        dg = jnp.sum(dy * xh, axis=0, keepdims=True)
        lossb = jnp.full(loss_ref.shape, part, F32)

        @pl.when(i == 0)
        def _():
            dg_ref[...] = dg
            loss_ref[...] = lossb

        @pl.when(i > 0)
        def _():
            dg_ref[...] += dg
            loss_ref[...] += lossb

    row = pl.BlockSpec((tm, D), lambda i: (i, 0))
    vec = pl.BlockSpec((1, D), lambda i: (0, 0))
    return pl.pallas_call(
        body,
        grid=(S // tm,),
        in_specs=[row, vec, row],
        out_specs=[row, row, vec, pl.BlockSpec((8, 128), lambda i: (0, 0))],
        out_shape=[
            jax.ShapeDtypeStruct((S, D), F32),
            jax.ShapeDtypeStruct((S, D), BF16),
            jax.ShapeDtypeStruct((1, D), F32),
            jax.ShapeDtypeStruct((8, 128), F32),
        ],
        compiler_params=_cp(("arbitrary",)),
        name=name,
    )(x, g.reshape(1, D), target)


def _shift_down(s, k):
    if k == 0:
        return s
    return jnp.where(_iota2(s.shape, 0) >= k, pltpu.roll(s, k, axis=0), 0.0)


def _shift_up(s, k):
    if k == 0:
        return s
    n = s.shape[0]
    return jnp.where(_iota2(s.shape, 0) < n - k, pltpu.roll(s, n - k, axis=0), 0.0)


def _conv(s, w):
    return w[0:1] * _shift_down(s, 2) + w[1:2] * _shift_down(s, 1) + w[2:3] * s


def _conv_t(d, w):
    return w[2:3] * d + w[1:2] * _shift_up(d, 1) + w[0:1] * _shift_up(d, 2)


def _conv_dw(d, s):
    return [jnp.sum(d * _shift_down(s, CONV_K - 1 - k), axis=0, keepdims=True) for k in range(CONV_K)]


def sc_fwd(p, convw, cat, W, name):
    S = p.shape[0]
    tc = _pick(W, (256, 128))
    nc = W // tc

    def body(gb_ref, gc_ref, hi_ref, w_ref, cat_ref, o_ref):
        s = gc_ref[...].astype(F32) * hi_ref[...].astype(F32)
        o_ref[...] = (gb_ref[...].astype(F32) * _conv(s, w_ref[...])).astype(o_ref.dtype)

    col = lambda part: pl.BlockSpec((S, tc), lambda c: (0, part * nc + c))
    return pl.pallas_call(
        body,
        grid=(nc,),
        in_specs=[col(3), col(4), col(5), pl.BlockSpec((CONV_K, tc), lambda c: (0, c)), pl.BlockSpec(memory_space=pl.ANY)],
        out_specs=col(1),
        out_shape=jax.ShapeDtypeStruct(cat.shape, cat.dtype),
        input_output_aliases={4: 0},
        compiler_params=_cp(("parallel",)),
        name=name,
    )(p, p, p, convw, cat)


def sc_bwd(p, convw, dcat, dp, W, name):
    S = p.shape[0]
    tc = _pick(W, (256, 128))
    nc = W // tc

    def body(gb_ref, gc_ref, hi_ref, w_ref, do_ref, dp_in_ref, dp_ref, dw_ref):
        gb = gb_ref[...].astype(F32)
        gc = gc_ref[...].astype(F32)
        hi = hi_ref[...].astype(F32)
        w = w_ref[...]
        do = do_ref[...].astype(F32)
        s = gc * hi
        dcs = do * gb
        ds = _conv_t(dcs, w)
        dp_ref[0] = (do * _conv(s, w)).astype(dp_ref.dtype)
        dp_ref[1] = (ds * hi).astype(dp_ref.dtype)
        dp_ref[2] = (ds * gc).astype(dp_ref.dtype)
        for k, row in enumerate(_conv_dw(dcs, s)):
            dw_ref[k : k + 1, :] = row

    col = lambda part: pl.BlockSpec((S, tc), lambda c: (0, part * nc + c))
    return pl.pallas_call(
        body,
        grid=(nc,),
        in_specs=[
            col(3), col(4), col(5),
            pl.BlockSpec((CONV_K, tc), lambda c: (0, c)),
            pl.BlockSpec((S, tc), lambda c: (0, nc + c)),
            pl.BlockSpec(memory_space=pl.ANY),
        ],
        out_specs=[pl.BlockSpec((3, S, tc), lambda c: (1, 0, c)), pl.BlockSpec((CONV_K, tc), lambda c: (0, c))],
        out_shape=[jax.ShapeDtypeStruct(dp.shape, dp.dtype), jax.ShapeDtypeStruct((CONV_K, W), F32)],
        input_output_aliases={5: 0},
        compiler_params=_cp(("parallel",)),
        name=name,
    )(p, p, p, convw, dcat, dp)


def _silu_parts(a):
    sig = 1.0 / (1.0 + jnp.exp(-a))
    return a * sig, sig


def ffn_act_fwd(u, convw, F, name):
    S = u.shape[0]
    tc = _pick(F, (256, 128))
    nc = F // tc

    def body(ug_ref, uu_ref, wg_ref, wu_ref, o_ref):
        ag = _conv(ug_ref[...].astype(F32), wg_ref[...])
        au = _conv(uu_ref[...].astype(F32), wu_ref[...])
        o_ref[...] = (_silu_parts(ag)[0] * au).astype(o_ref.dtype)

    col = lambda half: pl.BlockSpec((S, tc), lambda c: (0, half * nc + c))
    wcol = lambda half: pl.BlockSpec((CONV_K, tc), lambda c: (0, half * nc + c))
    return pl.pallas_call(
        body,
        grid=(nc,),
        in_specs=[col(0), col(1), wcol(0), wcol(1)],
        out_specs=pl.BlockSpec((S, tc), lambda c: (0, c)),
        out_shape=jax.ShapeDtypeStruct((S, F), BF16),
        compiler_params=_cp(("parallel",)),
        name=name,
    )(u, u, convw, convw)


def ffn_act_bwd(u, convw, dact, F, name):
    S = u.shape[0]
    tc = _pick(F, (256, 128))
    nc = F // tc

    def body(ug_ref, uu_ref, wg_ref, wu_ref, da_ref, du_ref, dw_ref):
        ug = ug_ref[...].astype(F32)
        uu = uu_ref[...].astype(F32)
        wg = wg_ref[...]
        wu = wu_ref[...]
        da = da_ref[...].astype(F32)
        ag = _conv(ug, wg)
        au = _conv(uu, wu)
        sl, sig = _silu_parts(ag)
        dag = da * au * (sig * (1.0 + ag * (1.0 - sig)))
        dau = da * sl
        du_ref[0] = _conv_t(dag, wg).astype(du_ref.dtype)
        du_ref[1] = _conv_t(dau, wu).astype(du_ref.dtype)
        for k, (rg, ru) in enumerate(zip(_conv_dw(dag, ug), _conv_dw(dau, uu))):
            dw_ref[0, k : k + 1, :] = rg
            dw_ref[1, k : k + 1, :] = ru

    col = lambda half: pl.BlockSpec((S, tc), lambda c: (0, half * nc + c))
    wcol = lambda half: pl.BlockSpec((CONV_K, tc), lambda c: (0, half * nc + c))
    return pl.pallas_call(
        body,
        grid=(nc,),
        in_specs=[col(0), col(1), wcol(0), wcol(1), pl.BlockSpec((S, tc), lambda c: (0, c))],
        out_specs=[pl.BlockSpec((2, S, tc), lambda c: (0, 0, c)), pl.BlockSpec((2, CONV_K, tc), lambda c: (0, 0, c))],
        out_shape=[jax.ShapeDtypeStruct((2, S, F), BF16), jax.ShapeDtypeStruct((2, CONV_K, F), F32)],
        compiler_params=_cp(("parallel",)),
        name=name,
    )(u, u, convw, convw, dact)


def _softplus(z):
    return jnp.maximum(z, 0.0) + jnp.log(1.0 + jnp.exp(-jnp.abs(z)))


def _key_strip(S):
    return _pick(S, (512, 256, 128))


def _query_rows(S):
    tq = _pick(S, (512, 256, 128))
    assert _key_strip(S) % tq == 0
    return tq


def _split2(x):
    hi = x.astype(BF16)
    return hi, (x - hi.astype(F32)).astype(BF16)


def _block_sums(x, ones_bf16):
    hi, lo = _split2(x)
    return [
        _dot(hi[:, b * HD : (b + 1) * HD], ones_bf16) + _dot(lo[:, b * HD : (b + 1) * HD], ones_bf16)
        for b in range(x.shape[1] // HD)
    ]


def _strip_mask(shape, row0, off, strict):
    cols, rows = _iota2(shape, 1) + off, _iota2(shape, 0) + row0
    return cols < rows if strict else cols <= rows


def _sb_strip(q, ks, row0, off, run, su, masked):
    z = _dot_nt(q, ks) * (HD ** -0.5)
    sp = _softplus(z)
    mask = _strip_mask(z.shape, row0, off, True) if masked else None
    l = jnp.where(mask, -sp, 0.0) if masked else -sp
    within = _block_sums(l, su)
    later = [None] * len(within)
    for b in reversed(range(len(within))):
        later[b] = within[b] + run
        run = run + jnp.sum(l[:, b * HD : (b + 1) * HD], axis=1, keepdims=True)
    a = jnp.exp(z - sp + jnp.concatenate(later, axis=1))
    return z, (jnp.where(mask, a, 0.0) if masked else a), run


def sb_fwd(p, W, name):
    S = p.shape[0]
    TQ, TK = _query_rows(S), _key_strip(S)
    nh, nq = W // HD, S // TQ

    def body(q_ref, k_ref, v_ref, o_ref):
        i = pl.program_id(1)
        q = q_ref[...]
        su = (_iota2((HD, HD), 0) > _iota2((HD, HD), 1)).astype(BF16)
        last = (i * TQ + TQ - 1) // TK

        def strip(g, carry, masked):
            acc, run = carry
            off = pl.multiple_of(g * TK, TK)
            _, a, run = _sb_strip(q, k_ref[pl.ds(off, TK), :], i * TQ, off, run, su, masked)
            return acc + _dot(a.astype(BF16), v_ref[pl.ds(off, TK), :]), run

        carry = strip(last, (jnp.zeros((TQ, HD), F32), jnp.zeros((TQ, 1), F32)), True)
        acc, _ = lax.fori_loop(0, last, lambda gg, c: strip(last - 1 - gg, c, False), carry)
        o_ref[...] = acc.astype(o_ref.dtype)

    return pl.pallas_call(
        body,
        grid=(nh, nq),
        in_specs=[
            pl.BlockSpec((TQ, HD), lambda h, i: (i, h)),
            pl.BlockSpec((S, HD), lambda h, i: (0, nh + h)),
            pl.BlockSpec((S, HD), lambda h, i: (0, 2 * nh + h)),
        ],
        out_specs=pl.BlockSpec((TQ, HD), lambda h, i: (i, h)),
        out_shape=jax.ShapeDtypeStruct((S, 2 * W), BF16),
        compiler_params=_cp(("parallel", "arbitrary")),
        name=name,
    )(p, p, p)


def sb_bwd(p, dcat, W, name):
    S = p.shape[0]
    TQ, TK = _query_rows(S), _key_strip(S)
    nh, nq = W // HD, S // TQ
    scale = HD ** -0.5

    def body(q_ref, k_ref, v_ref, do_ref, dp_ref, dk_acc, dv_acc, e_scr, z_scr):
        i = pl.program_id(1)
        q = q_ref[...]
        do = do_ref[...]
        su = (_iota2((HD, HD), 0) > _iota2((HD, HD), 1)).astype(BF16)
        sl = (_iota2((HD, HD), 0) < _iota2((HD, HD), 1)).astype(BF16)
        last = (i * TQ + TQ - 1) // TK

        @pl.when(i == 0)
        def _():
            dk_acc[...] = jnp.zeros_like(dk_acc)
            dv_acc[...] = jnp.zeros_like(dv_acc)

        def pass_a(g, run, masked):
            off = pl.multiple_of(g * TK, TK)
            z, a, run = _sb_strip(q, k_ref[pl.ds(off, TK), :], i * TQ, off, run, su, masked)
            e_scr[g] = a * _dot_nt(do, v_ref[pl.ds(off, TK), :])
            z_scr[g] = z
            dv_acc[pl.ds(off, TK), :] += _dot_tn(a.astype(BF16), do)
            return run

        run = pass_a(last, jnp.zeros((TQ, 1), F32), True)
        lax.fori_loop(0, last, lambda gg, r: pass_a(last - 1 - gg, r, False), run)

        def pass_b(g, carry, masked):
            dq, run_e = carry
            off = pl.multiple_of(g * TK, TK)
            e = e_scr[g]
            z = z_scr[g]
            within = _block_sums(e, sl)
            before = []
            for b in range(len(within)):
                before.append(within[b] + run_e)
                run_e = run_e + jnp.sum(e[:, b * HD : (b + 1) * HD], axis=1, keepdims=True)
            sig = 1.0 / (1.0 + jnp.exp(-z))
            dz = e * (1.0 - sig) - jnp.concatenate(before, axis=1) * sig
            if masked:
                dz = jnp.where(_strip_mask(z.shape, i * TQ, off, True), dz, 0.0)
            dz = (dz * scale).astype(BF16)
            dq = dq + _dot(dz, k_ref[pl.ds(off, TK), :])
            dk_acc[pl.ds(off, TK), :] += _dot_tn(dz, q)
            return dq, run_e

        carry = lax.fori_loop(0, last, lambda g, c: pass_b(g, c, False), (jnp.zeros((TQ, HD), F32), jnp.zeros((TQ, 1), F32)))
        dq, _ = pass_b(last, carry, True)
        dp_ref[0, pl.ds(pl.multiple_of(i * TQ, TQ), TQ), :] = dq.astype(dp_ref.dtype)

        @pl.when(i == nq - 1)
        def _():
            dp_ref[1] = dk_acc[...].astype(dp_ref.dtype)
            dp_ref[2] = dv_acc[...].astype(dp_ref.dtype)

    return pl.pallas_call(
        body,
        grid=(nh, nq),
        in_specs=[
            pl.BlockSpec((TQ, HD), lambda h, i: (i, h)),
            pl.BlockSpec((S, HD), lambda h, i: (0, nh + h)),
            pl.BlockSpec((S, HD), lambda h, i: (0, 2 * nh + h)),
            pl.BlockSpec((TQ, HD), lambda h, i: (i, h)),
        ],
        out_specs=pl.BlockSpec((3, S, HD), lambda h, i: (0, 0, h)),
        out_shape=jax.ShapeDtypeStruct((6, S, W), BF16),
        scratch_shapes=[
            pltpu.VMEM((S, HD), F32),
            pltpu.VMEM((S, HD), F32),
            pltpu.VMEM((S // TK, TQ, TK), F32),
            pltpu.VMEM((S // TK, TQ, TK), F32),
        ],
        compiler_params=_cp(("parallel", "arbitrary")),
        name=name,
    )(p, p, p, dcat)


def fox_gate_fwd(f, b, name):
    S = f.shape[0]
    nq = S // HD

    def body(f_ref, b_ref, c_ref, run):
        i = pl.program_id(0)

        @pl.when(i == 0)
        def _():
            run[...] = jnp.zeros_like(run)

        lf = -_softplus(-(f_ref[...] + b_ref[...]))
        tri = (_iota2((HD, HD), 0) >= _iota2((HD, HD), 1)).astype(BF16)
        c_ref[...] = _dot_ones_left(tri, lf) + run[...]
        run[...] += jnp.sum(lf, axis=0, keepdims=True)

    return pl.pallas_call(
        body,
        grid=(nq,),
        in_specs=[pl.BlockSpec((HD, 128), lambda i: (i, 0)), pl.BlockSpec((1, 128), lambda i: (0, 0))],
        out_specs=pl.BlockSpec((HD, 128), lambda i: (i, 0)),
        out_shape=jax.ShapeDtypeStruct((S, 128), F32),
        scratch_shapes=[pltpu.VMEM((1, 128), F32)],
        compiler_params=_cp(("arbitrary",)),
        name=name,
    )(f, b)


def fox_gate_bwd(f, b, dc, name):
    S = f.shape[0]
    nq = S // HD

    def body(f_ref, b_ref, dc_ref, df_ref, db_ref, run):
        i = pl.program_id(0)

        @pl.when(i == 0)
        def _():
            run[...] = jnp.zeros_like(run)

        dc = dc_ref[...]
        tri = (_iota2((HD, HD), 0) <= _iota2((HD, HD), 1)).astype(BF16)
        dlf = _dot_ones_left(tri, dc) + run[...]
        run[...] += jnp.sum(dc, axis=0, keepdims=True)
        x = f_ref[...] + b_ref[...]
        df = dlf * (1.0 / (1.0 + jnp.exp(x)))
        df_ref[...] = df
        db = jnp.sum(df, axis=0, keepdims=True)

        @pl.when(i == 0)
        def _():
            db_ref[...] = db

        @pl.when(i > 0)
        def _():
            db_ref[...] += db

    rev = pl.BlockSpec((HD, 128), lambda i: (nq - 1 - i, 0))
    vec = pl.BlockSpec((1, 128), lambda i: (0, 0))
    return pl.pallas_call(
        body,
        grid=(nq,),
        in_specs=[rev, vec, rev],
        out_specs=[rev, vec],
        out_shape=[jax.ShapeDtypeStruct((S, 128), F32), jax.ShapeDtypeStruct((1, 128), F32)],
        scratch_shapes=[pltpu.VMEM((1, 128), F32)],
        compiler_params=_cp(("arbitrary",)),
        name=name,
    )(f, b, dc)


def _fox_logits(q, ks, ct, cs, row0, off, masked):
    s = _dot_nt(q, ks) * (HD ** -0.5) + (ct - cs)
    if not masked:
        return s, None
    mask = _strip_mask(s.shape, row0, off, False)
    return jnp.where(mask, s, -1e30), mask


def fox_fwd(p, ccol, crow, cat, W, name):
    S = p.shape[0]
    TQ, TK = _query_rows(S), _key_strip(S)
    nh, nq = W // HD, S // TQ

    def body(q_ref, k_ref, v_ref, cc_ref, cr_ref, cat_ref, o_ref, lse_ref):
        i = pl.program_id(1)
        q = q_ref[...]
        ct = cc_ref[0]

        def step(g, carry, masked):
            m, l, acc = carry
            off = pl.multiple_of(g * TK, TK)
            s, _ = _fox_logits(q, k_ref[pl.ds(off, TK), :], ct, cr_ref[0, pl.ds(g, 1), :], i * TQ, off, masked)
            m_new = jnp.maximum(m, jnp.max(s, axis=1, keepdims=True))
            alpha = jnp.exp(m - m_new)
            pr = jnp.exp(s - m_new)
            l = alpha * l + jnp.sum(pr, axis=1, keepdims=True)
            acc = alpha * acc + _dot(pr.astype(BF16), v_ref[pl.ds(off, TK), :])
            return m_new, l, acc

        init = (jnp.full((TQ, 1), -1e30, F32), jnp.zeros((TQ, 1), F32), jnp.zeros((TQ, HD), F32))
        last = (i * TQ + TQ - 1) // TK
        m, l, acc = step(last, lax.fori_loop(0, last, lambda g, c: step(g, c, False), init), True)
        o_ref[...] = (acc / l).astype(o_ref.dtype)
        lse_ref[0] = m + jnp.log(l)

    return pl.pallas_call(
        body,
        grid=(nh, nq),
        in_specs=[
            pl.BlockSpec((TQ, HD), lambda h, i: (i, 2 * nh + h)),
            pl.BlockSpec((S, HD), lambda h, i: (0, 3 * nh + h)),
            pl.BlockSpec((S, HD), lambda h, i: (0, 4 * nh + h)),
            pl.BlockSpec((1, TQ, 1), lambda h, i: (h, i, 0)),
            pl.BlockSpec((1, S // TK, TK), lambda h, i: (h, 0, 0)),
            pl.BlockSpec(memory_space=pl.ANY),
        ],
        out_specs=[pl.BlockSpec((TQ, HD), lambda h, i: (i, nh + h)), pl.BlockSpec((1, TQ, 1), lambda h, i: (h, i, 0))],
        out_shape=[jax.ShapeDtypeStruct(cat.shape, cat.dtype), jax.ShapeDtypeStruct((nh, S, 1), F32)],
        input_output_aliases={5: 0},
        compiler_params=_cp(("parallel", "arbitrary")),
        name=name,
    )(p, p, p, ccol, crow, cat)


def fox_bwd(p, ccol, crow, cat, lse, dcat, dp, W, name):
    S = p.shape[0]
    TQ, TK = _query_rows(S), _key_strip(S)
    nh, nq = W // HD, S // TQ
    scale = HD ** -0.5

    def body(q_ref, k_ref, v_ref, cc_ref, cr_ref, o_ref, lse_ref, do_ref, dp_in_ref, dp_ref, dcs_ref, dct_ref, dk_acc, dv_acc):
        i = pl.program_id(1)
        q = q_ref[...]
        do = do_ref[...]
        ct = cc_ref[0]
        lse_i = lse_ref[0]
        delta = jnp.sum(do.astype(F32) * o_ref[...].astype(F32), axis=1, keepdims=True)

        @pl.when(i == 0)
        def _():
            dk_acc[...] = jnp.zeros_like(dk_acc)
            dv_acc[...] = jnp.zeros_like(dv_acc)
            dcs_ref[...] = jnp.zeros_like(dcs_ref)

        def step(g, carry, masked):
            dq, dct = carry
            off = pl.multiple_of(g * TK, TK)
            ks = k_ref[pl.ds(off, TK), :]
            s, mask = _fox_logits(q, ks, ct, cr_ref[0, pl.ds(g, 1), :], i * TQ, off, masked)
            pr = jnp.where(mask, jnp.exp(s - lse_i), 0.0) if masked else jnp.exp(s - lse_i)
            ds = pr * (_dot_nt(do, v_ref[pl.ds(off, TK), :]) - delta)
            dv_acc[pl.ds(off, TK), :] += _dot_tn(pr.astype(BF16), do)
            dsb = (ds * scale).astype(BF16)
            dk_acc[pl.ds(off, TK), :] += _dot_tn(dsb, q)
            dcs_ref[0, pl.ds(g, 1), :] += jnp.sum(ds, axis=0, keepdims=True)
            return dq + _dot(dsb, ks), dct + jnp.sum(ds, axis=1, keepdims=True)

        last = (i * TQ + TQ - 1) // TK
        carry = lax.fori_loop(0, last, lambda g, c: step(g, c, False), (jnp.zeros((TQ, HD), F32), jnp.zeros((TQ, 1), F32)))
        dq, dct = step(last, carry, True)
        dp_ref[0, pl.ds(pl.multiple_of(i * TQ, TQ), TQ), :] = dq.astype(dp_ref.dtype)
        dct_ref[0] = dct

        @pl.when(i == nq - 1)
        def _():
            dp_ref[1] = dk_acc[...].astype(dp_ref.dtype)
            dp_ref[2] = dv_acc[...].astype(dp_ref.dtype)

    return pl.pallas_call(
        body,
        grid=(nh, nq),
        in_specs=[
            pl.BlockSpec((TQ, HD), lambda h, i: (i, 2 * nh + h)),
            pl.BlockSpec((S, HD), lambda h, i: (0, 3 * nh + h)),
            pl.BlockSpec((S, HD), lambda h, i: (0, 4 * nh + h)),
            pl.BlockSpec((1, TQ, 1), lambda h, i: (h, i, 0)),
            pl.BlockSpec((1, S // TK, TK), lambda h, i: (h, 0, 0)),
            pl.BlockSpec((TQ, HD), lambda h, i: (i, nh + h)),
            pl.BlockSpec((1, TQ, 1), lambda h, i: (h, i, 0)),
            pl.BlockSpec((TQ, HD), lambda h, i: (i, nh + h)),
            pl.BlockSpec(memory_space=pl.ANY),
        ],
        out_specs=[
            pl.BlockSpec((3, S, HD), lambda h, i: (1, 0, h)),
            pl.BlockSpec((1, S // TK, TK), lambda h, i: (h, 0, 0)),
            pl.BlockSpec((1, TQ, 1), lambda h, i: (h, i, 0)),
        ],
        out_shape=[
            jax.ShapeDtypeStruct(dp.shape, dp.dtype),
            jax.ShapeDtypeStruct((nh, S // TK, TK), F32),
            jax.ShapeDtypeStruct((nh, S, 1), F32),
        ],
        input_output_aliases={8: 0},
        scratch_shapes=[pltpu.VMEM((S, HD), F32), pltpu.VMEM((S, HD), F32)],
        compiler_params=_cp(("parallel", "arbitrary")),
        name=name,
    )(p, p, p, ccol, crow, cat, lse, dcat, dp)


_GELU_K = math.sqrt(2.0 / math.pi)
_GELU_C = 0.044715


def _gelu(x):
    return 0.5 * x * (1.0 + jnp.tanh(_GELU_K * (x + _GELU_C * x * x * x)))


def _gelu_grad(x):
    t = jnp.tanh(_GELU_K * (x + _GELU_C * x * x * x))
    return 0.5 * (1.0 + t) + 0.5 * x * (1.0 - t * t) * (_GELU_K * (1.0 + 3.0 * _GELU_C * x * x))


def _layernorm_parts(gv):
    xc = gv - jnp.mean(gv, axis=-1, keepdims=True)
    r = lax.rsqrt(jnp.mean(xc * xc, axis=-1, keepdims=True) + EPS)
    return xc * r, r


def sg_fwd(p, sg_w, sg_bt, sg_g, W, name):
    S = p.shape[0]
    G, nq = W // HD, S // HD

    def body(u_ref, v_ref, w_ref, bt_ref, g_ref, o_ref):
        xh, _ = _layernorm_parts(_gelu(v_ref[...].astype(F32)))
        vn = (xh * g_ref[...]).astype(BF16)
        tri = _iota2((HD, HD), 0) >= _iota2((HD, HD), 1)
        for gi in range(G):
            cols = slice(gi * HD, (gi + 1) * HD)
            wt = jnp.where(tri, w_ref[gi], 0.0).astype(BF16)
            mixed = _dot(wt, vn[:, cols]) + bt_ref[:, gi : gi + 1]
            o_ref[:, cols] = (_gelu(u_ref[:, cols].astype(F32)) * mixed).astype(o_ref.dtype)

    return pl.pallas_call(
        body,
        grid=(nq,),
        in_specs=[
            pl.BlockSpec((HD, W), lambda i: (i, 0)),
            pl.BlockSpec((HD, W), lambda i: (i, 1)),
            pl.BlockSpec((G, HD, HD), lambda i: (0, 0, 0)),
            pl.BlockSpec((HD, G), lambda i: (0, 0)),
            pl.BlockSpec((1, W), lambda i: (0, 0)),
        ],
        out_specs=pl.BlockSpec((HD, W), lambda i: (i, 0)),
        out_shape=jax.ShapeDtypeStruct((S, 2 * W), BF16),
        compiler_params=_cp(("parallel",)),
        name=name,
    )(p, p, sg_w, sg_bt, sg_g.reshape(1, W))


def sg_bwd(p, sg_w, sg_bt, sg_g, dcat, W, name):
    S = p.shape[0]
    G, nq = W // HD, S // HD

    def body(u_ref, v_ref, w_ref, bt_ref, g_ref, do_ref, dp_ref, dw_ref, dbt_ref, dg_ref, dvn_scr):
        i = pl.program_id(0)

        @pl.when(i == 0)
        def _():
            dw_ref[...] = jnp.zeros_like(dw_ref)
            dbt_ref[...] = jnp.zeros_like(dbt_ref)
            dg_ref[...] = jnp.zeros_like(dg_ref)

        v = v_ref[...].astype(F32)
        xh, r = _layernorm_parts(_gelu(v))
        gg = g_ref[...]
        vn = (xh * gg).astype(BF16)
        tri = _iota2((HD, HD), 0) >= _iota2((HD, HD), 1)
        for gi in range(G):
            cols = slice(gi * HD, (gi + 1) * HD)
            wt = jnp.where(tri, w_ref[gi], 0.0).astype(BF16)
            mixed = _dot(wt, vn[:, cols]) + bt_ref[:, gi : gi + 1]
            u = u_ref[:, cols].astype(F32)
            do = do_ref[:, cols].astype(F32)
            dp_ref[0, :, cols] = (do * mixed * _gelu_grad(u)).astype(dp_ref.dtype)
            dmix = do * _gelu(u)
            dmb = dmix.astype(BF16)
            dw_ref[gi] += jnp.where(tri, _dot_nt(dmb, vn[:, cols]), 0.0)
            dbt_ref[:, gi : gi + 1] += jnp.sum(dmix, axis=1, keepdims=True)
            dvn_scr[:, cols] = _dot_tn(wt, dmb)
        dvn = dvn_scr[...]
        dg_ref[...] += jnp.sum(dvn * xh, axis=0, keepdims=True)
        dxh = dvn * gg
        dgv = r * (dxh - jnp.mean(dxh, axis=-1, keepdims=True) - xh * jnp.mean(dxh * xh, axis=-1, keepdims=True))
        dp_ref[1] = (dgv * _gelu_grad(v)).astype(dp_ref.dtype)

    return pl.pallas_call(
        body,
        grid=(nq,),
        in_specs=[
            pl.BlockSpec((HD, W), lambda i: (i, 0)),
            pl.BlockSpec((HD, W), lambda i: (i, 1)),
            pl.BlockSpec((G, HD, HD), lambda i: (0, 0, 0)),
            pl.BlockSpec((HD, G), lambda i: (0, 0)),
            pl.BlockSpec((1, W), lambda i: (0, 0)),
            pl.BlockSpec((HD, W), lambda i: (i, 0)),
        ],
        out_specs=[
            pl.BlockSpec((2, HD, W), lambda i: (0, i, 0)),
            pl.BlockSpec((G, HD, HD), lambda i: (0, 0, 0)),
            pl.BlockSpec((HD, G), lambda i: (0, 0)),
            pl.BlockSpec((1, W), lambda i: (0, 0)),
        ],
        out_shape=[
            jax.ShapeDtypeStruct((6, S, W), BF16),
            jax.ShapeDtypeStruct((G, HD, HD), F32),
            jax.ShapeDtypeStruct((HD, G), F32),
            jax.ShapeDtypeStruct((1, W), F32),
        ],
        scratch_shapes=[pltpu.VMEM((HD, W), F32)],
        compiler_params=_cp(("arbitrary",)),
        name=name,
    )(p, p, sg_w, sg_bt, sg_g.reshape(1, W), dcat)


def local_step(x, target, wts, at, on_grad):
    S, D = x.shape
    W = D // 2
    nb, F = wts["nb"], wts["F"]
    g = {}

    def ffn_fwd(xin, l):
        h = rms_fwd(xin, wts[f"{l}_ffn_norm_g"], f"{l}_ffn_rms")
        u = mm_nn(h, wts[f"{l}_ffn_up"], nb, f"{l}_ffn_up_mm")
        act = ffn_act_fwd(u, wts[f"{l}_ffn_conv_w"], F, f"{l}_ffn_act")
        xout = mm_nn(act, wts[f"{l}_ffn_down"], 1, f"{l}_ffn_down_mm", out_dtype=F32, res=xin,
                     tm=_pick(S, (1024, 512, 256, 128)), tn=_pick(D, (512, 256, 128)), tk=F)
        return xout, (xin, h, u, act)

    def ffn_bwd(dxout, dxoutb, saved, l):
        xin, h, u, act = saved
        dact = mm_nt(dxoutb, wts[f"{l}_ffn_down"], 1, S, F, f"{l}_ffn_down_dx", tko=_pick(F, (512, 256, 128)), tn=D)
        dact = on_grad(f"{l}_ffn_down", mm_tn(act, dxoutb, 1, D, f"{l}_ffn_down_dw", tn=D), dact)
        du, dcw = ffn_act_bwd(u, wts[f"{l}_ffn_conv_w"], dact, F, f"{l}_ffn_act_bwd")
        g[f"{l}_ffn_conv_w"] = jnp.concatenate([dcw[0], dcw[1]], axis=1)
        du2 = du.reshape(2 * S, F)
        n = wts[f"{l}_ffn_up"].shape[1]
        tn = _pick(n, (1408, 1024, 768, 512, 256, 128))
        per_half = F // tn
        nt = n // tn

        def up_block(i, j, t):
            vb = j * nt + t
            return vb // per_half, vb % per_half

        tm = _pick(S, (1024, 512, 256, 128))

        def nt_map(i, j, t):
            half, cb = up_block(i, j, t)
            return (half * (S // tm) + i, cb)

        def tn_map(j, t):
            half, cb = up_block(0, j, t)
            return (half, cb)

        dh = mm_nt(du2, wts[f"{l}_ffn_up"], nb, S, D, f"{l}_ffn_up_dx", dy_maps=[nt_map], tm=tm, tko=D, tn=tn)
        dh = on_grad(f"{l}_ffn_up", mm_tn(h, du2, nb, n, f"{l}_ffn_up_dw", dy_maps=[tn_map], tko=_pick(D, (1024, 512, 256, 128)), tn=tn), dh)
        dxin, dxinb, dg = rms_bwd(xin, wts[f"{l}_ffn_norm_g"], dh, dxout, f"{l}_ffn_rms_bwd")
        g[f"{l}_ffn_norm_g"] = dg
        return dxin, dxinb

    h0 = rms_fwd(x, wts["l0_mix_norm_g"], "l0_mix_rms")
    p0 = mm_nn(h0, wts["l0_w_in"], nb, "l0_w_in_mm")
    cat0 = sb_fwd(p0, W, "l0_sb_fwd")
    cat0 = sc_fwd(p0, wts["l0_sc_conv_w"], cat0, W, "l0_sc_fwd")
    x1 = mm_nn(cat0, wts["l0_w_out"], 1, "l0_w_out_mm", out_dtype=F32, res=x, tm=S, tn=_pick(D, (512, 256, 128)))
    x2, ffn0_saved = ffn_fwd(x1, "l0")

    x2 = at("l1_w_in", x2, None)
    nh = W // HD
    h2 = rms_fwd(x2, wts["l1_mix_norm_g"], "l1_mix_rms")
    p1 = mm_nt(h2, wts["l1_w_in_t"], 1, S, 5 * W, "l1_w_in_mm", tn=D)
    f = mm_nt(h2, wts["l1_w_f_t"], 1, S, 128, "l1_w_f_mm", out_dtype=F32, tn=D)
    bf = jnp.zeros((1, 128), F32).at[0, :nh].set(wts["l1_fox_b_f"])
    c = fox_gate_fwd(f, bf, "l1_fox_gate")
    c_heads = c[:, :nh].T
    ccol = c_heads[:, :, None]
    crow = c_heads.reshape(nh, S // _key_strip(S), _key_strip(S))
    sg_bt = wts["l1_sg_b"].T
    cat1 = sg_fwd(p1, wts["l1_sg_w"], sg_bt, wts["l1_sg_norm_g"], W, "l1_sg_fwd")
    cat1, lse = fox_fwd(p1, ccol, crow, cat1, W, "l1_fox_fwd")
    x3 = mm_nn(cat1, wts["l1_w_out"], 1, "l1_w_out_mm", out_dtype=F32, res=x2, tm=S, tn=_pick(D, (512, 256, 128)))
    x4, ffn1_saved = ffn_fwd(x3, "l1")

    dx4, dx4b, dgf, loss = loss_head(x4, wts["final_norm_g"], target, "loss_head")
    dx4b = at("loss", dx4b, loss)
    g["final_norm_g"] = dgf

    dx3, dx3b = ffn_bwd(dx4, dx4b, ffn1_saved, "l1")
    dcat1 = mm_nt(dx3b, wts["l1_w_out"], 1, S, D, "l1_w_out_dx", tn=D)
    dcat1 = on_grad("l1_w_out", mm_tn(cat1, dx3b, 1, D, "l1_w_out_dw", tn=D), dcat1)
    dp1, dsgw, dsgbt, dsgg = sg_bwd(p1, wts["l1_sg_w"], sg_bt, wts["l1_sg_norm_g"], dcat1, W, "l1_sg_bwd")
    dp1, dcs, dct = fox_bwd(p1, ccol, crow, cat1, lse, dcat1, dp1, W, "l1_fox_bwd")
    g["l1_sg_w"], g["l1_sg_b"], g["l1_sg_norm_g"] = dsgw, dsgbt.T, dsgg
    dc = jnp.zeros((S, 128), F32).at[:, :nh].set((dct[:, :, 0] - dcs.reshape(nh, S)).T)
    df, dbf = fox_gate_bwd(f, bf, dc, "l1_fox_gate_bwd")
    g["l1_fox_b_f"] = dbf[0, :nh]
    dfb = df.astype(BF16)
    tk1 = _pick(W, (1024, 512, 256, 128))
    tx1 = _pick(W, (512, 256, 128))
    tm1 = _pick(S, (1024, 512, 256, 128))
    part_of = lambda pt: pt + pt // 2 - pt // 4

    def a_map1(i, k):
        return (part_of(k // (W // tk1)) * (S // tm1) + i, k % (W // tk1))

    def x_map1(ko):
        return (part_of(ko // (W // tx1)), ko % (W // tx1))

    dp1_2d = dp1.reshape(6 * S, W)
    dw_main = mm_tn(dp1_2d, h2, 1, D, "l1_w_in_dw", tko=tx1, tn=D, x_map=x_map1, x_shape=(S, 5 * W))
    dw_f = mm_tn(dfb, h2, 1, D, "l1_w_f_dw", tn=D)
    dh2 = mm_nn(dfb, wts["l1_w_f_t"], 1, "l1_w_f_dx", out_dtype=F32)
    dh2 = mm_nn(dp1_2d, wts["l1_w_in_t"], 1, "l1_w_in_dx", res=dh2, tm=tm1, tk=tk1, a_map=a_map1, a_shape=(S, 5 * W))
    dh2 = on_grad("l1_w_in", jnp.concatenate([dw_main, dw_f[:nh]], axis=0), dh2)
    dx2, dx2b, dg = rms_bwd(x2, wts["l1_mix_norm_g"], dh2, dx3, "l1_mix_rms_bwd")
    g["l1_mix_norm_g"] = dg

    dx1, dx1b = ffn_bwd(dx2, dx2b, ffn0_saved, "l0")
    dcat0 = mm_nt(dx1b, wts["l0_w_out"], 1, S, D, "l0_w_out_dx", tn=D)
    dcat0 = on_grad("l0_w_out", mm_tn(cat0, dx1b, 1, D, "l0_w_out_dw", tn=D), dcat0)
    dp0 = sb_bwd(p0, dcat0, W, "l0_sb_bwd")
    dp0, dscw = sc_bwd(p0, wts["l0_sc_conv_w"], dcat0, dp0, W, "l0_sc_bwd")
    g["l0_sc_conv_w"] = dscw
    dp0 = at("small_ready", dp0, g)
    n0 = wts["l0_w_in"].shape[1]
    td0 = math.gcd(n0, W)
    nd0 = n0 // td0
    tm0 = _pick(S, (1024, 512, 256, 128))
    per_part0 = W // td0

    def nt_maps0(k):
        def f(i, j, t):
            vb = j * nd0 + k
            return ((vb // per_part0) * (S // tm0) + i, vb % per_part0)
        return f

    def tn_maps0(k):
        def f(j, t):
            vb = j * nd0 + k
            return (vb // per_part0, vb % per_part0)
        return f

    dp0_2d = dp0.reshape(6 * S, W)
    dw0 = mm_tn(h0, dp0_2d, nb, n0, "l0_w_in_dw", dy_maps=[tn_maps0(k) for k in range(nd0)], tko=_pick(D, (1024, 512, 256, 128)), tn=n0)
    dp0_2d = on_grad("l0_w_in", dw0, dp0_2d)
    dp0_2d = on_grad(None, None, dp0_2d)
    dh0 = mm_nt(dp0_2d, wts["l0_w_in"], nb, S, D, "l0_w_in_dx", dy_maps=[nt_maps0(k) for k in range(nd0)], tm=tm0, tko=D, tn=n0)
    dh0 = at("small_done", dh0, None)
    dx0, _, dg = rms_bwd(x, wts["l0_mix_norm_g"], dh0, dx1, "l0_mix_rms_bwd")
    g["l0_mix_norm_g"] = dg
    return dx0, g


GATHER_ID = 1


def _place():
    return lax.axis_index("x"), lax.axis_index("y"), lax.axis_index("c")


def _other_chips(x, y):
    return [(x, 1 - y), (1 - x, y), (1 - x, 1 - y)]


def _handshake(peers):
    barrier = pltpu.get_barrier_semaphore()
    for peer in peers:
        pl.semaphore_signal(barrier, inc=1, device_id=peer, device_id_type=MESH)
    pl.semaphore_wait(barrier, len(peers))


UPDATE_LAG = 2


def _on_sequencer(body, out_type, scratch_types, collective_id, name):
    return pl.kernel(
        body,
        out_type=out_type,
        mesh=plsc.ScalarSubcoreMesh(axis_name="seq", num_cores=1),
        scratch_types=scratch_types,
        compiler_params=pltpu.CompilerParams(collective_id=collective_id),
        name=name,
    )


def all_gather(arrs, name):
    n = len(arrs)

    def body(*refs):
        xs, outs = refs[:n], refs[n : 2 * n]
        send_sems, recv_sems, local_sems = refs[2 * n :]
        x, y, c = _place()
        me, sibling = (x, y, c), (x, y, 1 - c)
        chips = _other_chips(x, y)
        _handshake([sibling] + [(*chip, c) for chip in chips])

        def copy(a, k, block, to, src=None):
            px, py, pc = block
            dst = outs[a].at[4 * px + 2 * py + pc]
            return pltpu.make_async_remote_copy(
                src_ref=dst if src is None else src, dst_ref=dst,
                send_sem=send_sems.at[7 * a + k], recv_sem=recv_sems.at[7 * a + k], device_id=to, device_id_type=MESH,
            )

        mine = [pltpu.make_async_copy(xs[a], outs[a].at[4 * x + 2 * y + c], local_sems.at[a]) for a in range(n)]
        for cp in mine:
            cp.start()
        first = []
        for a in range(n):
            first.append(copy(a, 0, me, sibling, src=xs[a]))
            first += [copy(a, 1 + j, me, (*chip, c), src=xs[a]) for j, chip in enumerate(chips)]
        for cp in first:
            cp.start()
        passed = []
        for a in range(n):
            for j, chip in enumerate(chips):
                copy(a, 1 + j, (*chip, c), me).wait_recv()
                cp = copy(a, 4 + j, (*chip, c), sibling)
                cp.start()
                passed.append(cp)
        for a in range(n):
            copy(a, 0, sibling, me).wait_recv()
            for j, chip in enumerate(chips):
                copy(a, 4 + j, (*chip, 1 - c), me).wait_recv()
        for cp in first + passed:
            cp.wait_send()
        for cp in mine:
            cp.wait()

    out_type = [jax.ShapeDtypeStruct((NDEV,) + a.shape, a.dtype) for a in arrs]
    sems = [pltpu.SemaphoreType.DMA((7 * n,)), pltpu.SemaphoreType.DMA((7 * n,)), pltpu.SemaphoreType.DMA((n,))]
    return _on_sequencer(body, out_type, sems, GATHER_ID, name)(*arrs)


_IN_HBM = pl.BlockSpec(memory_space=pltpu.HBM)
_IN_SEM = pl.BlockSpec(memory_space=pltpu.SEMAPHORE)
_EFFECT = pltpu.SideEffectType.DATAFLOW_SIDE_EFFECTING


def _split_start_many(jobs, name):
    nj = len(jobs)

    def body(*refs):
        ins, outs = refs[: 2 * nj], refs[2 * nj :]
        for q, job in enumerate(jobs):
            for cp in job[0](ins[2 * q], ins[2 * q + 1], outs[3 * q], outs[3 * q + 1]):
                cp.start()
        outs[-1][...] = jnp.zeros_like(outs[-1])

    out_shape, out_specs, operands, aliases = [], [], [], {}
    for q, (_, src, land_shape, nsem) in enumerate(jobs):
        out_shape += [pltpu.SemaphoreType.DMA((nsem,)), pltpu.SemaphoreType.DMA((nsem,)), pltpu.HBM(land_shape, src.dtype)]
        out_specs += [_IN_SEM, _IN_SEM, _IN_HBM]
        operands += [src, pltpu.with_memory_space_constraint(lax.empty(land_shape, src.dtype), pltpu.HBM)]
        aliases[2 * q + 1] = 3 * q + 2
    res = pl.pallas_call(
        body,
        name=name,
        out_shape=tuple(out_shape) + (jax.ShapeDtypeStruct((8, 128), F32),),
        in_specs=(_IN_HBM,) * (2 * nj),
        out_specs=tuple(out_specs) + (pl.BlockSpec(memory_space=pltpu.VMEM),),
        input_output_aliases=aliases,
        compiler_params=pltpu.CompilerParams(has_side_effects=_EFFECT),
    )(*operands)
    return [[res[3 * q], res[3 * q + 1], jobs[q][1], res[3 * q + 2]] for q in range(nj)], res[-1]


def _split_start(make_copies, src, land_shape, nsem, name):
    (flying,), token = _split_start_many([(make_copies, src, land_shape, nsem)], name)
    return (*flying, token)


def _split_wait(make_copies, send_sems, recv_sems, src_thru, land_thru, after, name):
    def body(src_ref, land_ref, send_sems, recv_sems, after_ref, land_out):
        for cp in make_copies(src_ref, land_ref, send_sems, recv_sems):
            cp.wait_send()
            cp.wait_recv()

    return pl.pallas_call(
        body,
        name=name,
        out_shape=pltpu.HBM(land_thru.shape, land_thru.dtype),
        in_specs=(_IN_HBM, _IN_HBM, _IN_SEM, _IN_SEM, pl.BlockSpec(memory_space=pl.ANY)),
        out_specs=_IN_HBM,
        input_output_aliases={1: 0},
        compiler_params=pltpu.CompilerParams(has_side_effects=_EFFECT),
    )(src_thru, land_thru, send_sems, recv_sems, after)


def _pair_copies(src_ref, land_ref, send_sems, recv_sems):
    x, y, c = _place()
    return [
        pltpu.make_async_remote_copy(
            src_ref=src_ref.at[k, 1 - c], dst_ref=land_ref.at[k],
            send_sem=send_sems.at[k], recv_sem=recv_sems.at[k], device_id=(x, y, 1 - c), device_id_type=MESH,
        )
        for k in range(4)
    ]


def _direct_copies(src_ref, land_ref, send_sems, recv_sems):
    x, y, c = _place()
    me = 4 * x + 2 * y + c
    copies = []
    for k in range(NDEV - 1):
        to = (me + k + 1) % NDEV
        copies.append(pltpu.make_async_remote_copy(
            src_ref=src_ref, dst_ref=land_ref.at[me], send_sem=send_sems.at[k], recv_sem=recv_sems.at[k],
            device_id=(to // 4, (to // 2) % 2, to % 2), device_id_type=MESH,
        ))
    return copies


def _chip_copies(src_ref, land_ref, send_sems, recv_sems):
    x, y, c = _place()
    return [
        pltpu.make_async_remote_copy(
            src_ref=src_ref.at[2 * px + py], dst_ref=land_ref.at[2 * x + y],
            send_sem=send_sems.at[j], recv_sem=recv_sems.at[j], device_id=(px, py, c), device_id_type=MESH,
        )
        for j, (px, py) in enumerate(_other_chips(x, y))
    ]


def _row_tile(R, C, max_elems):
    if R * C <= max_elems:
        return R
    best = None
    for tr in range(16, R, 16):
        if R % tr == 0 and tr * C <= max_elems:
            best = tr
    return best or R


def pair_sum(a42, land4, core, name):
    _, _, R, C = a42.shape
    tr = _row_tile(R, C, 1 << 20)

    def body(core_ref, a_ref, l_ref, o_ref):
        o_ref[...] = (a_ref[0].astype(F32) + l_ref[...].astype(F32)).astype(o_ref.dtype)

    return pl.pallas_call(
        body,
        grid_spec=pltpu.PrefetchScalarGridSpec(
            num_scalar_prefetch=1,
            grid=(4, R // tr),
            in_specs=[
                pl.BlockSpec((1, 1, tr, C), lambda k, r, core_ref: (k, core_ref[0], r, 0)),
                pl.BlockSpec((1, tr, C), lambda k, r, core_ref: (k, r, 0)),
            ],
            out_specs=pl.BlockSpec((1, tr, C), lambda k, r, core_ref: (k, r, 0)),
        ),
        out_shape=jax.ShapeDtypeStruct((4, R, C), BF16),
        compiler_params=_cp(("parallel", "parallel")),
        name=name,
    )(core, a42, land4)


def sum_slots(parts, name):
    P, R, C = parts.shape

    def body(p_ref, o_ref):
        acc = p_ref[0].astype(F32)
        for k in range(1, P):
            acc = acc + p_ref[k].astype(F32)
        o_ref[...] = acc

    tr = _row_tile(R, P * C, 1 << 21)
    return pl.pallas_call(
        body,
        grid=(R // tr,),
        in_specs=[pl.BlockSpec((P, tr, C), lambda r: (0, r, 0))],
        out_specs=pl.BlockSpec((tr, C), lambda r: (r, 0)),
        out_shape=jax.ShapeDtypeStruct((R, C), F32),
        compiler_params=_cp(("parallel",)),
        name=name,
    )(parts)


def adamw(w, m, v, parts, name):
    R, C = w.shape
    P = parts.shape[0]
    tr = _pick(R, (256, 128, 64, 32, 16, 8))
    c1 = 1.0 - ADAM_B1 ** ADAM_STEP
    c2 = 1.0 - ADAM_B2 ** ADAM_STEP

    def body(w_ref, m_ref, v_ref, p_ref, g_ref, d_ref, nm_ref, nv_ref):
        g = p_ref[0].astype(F32)
        for k in range(1, P):
            g = g + p_ref[k].astype(F32)
        nm = ADAM_B1 * m_ref[...] + (1.0 - ADAM_B1) * g
        nv = ADAM_B2 * v_ref[...] + (1.0 - ADAM_B2) * (g * g)
        g_ref[...] = g
        nm_ref[...] = nm
        nv_ref[...] = nv
        d_ref[...] = -ADAM_LR * ((nm / c1) / (jnp.sqrt(nv / c2) + ADAM_EPS) + ADAM_WD * w_ref[...])

    blk = pl.BlockSpec((tr, C), lambda r: (r, 0))
    shp = jax.ShapeDtypeStruct((R, C), F32)
    return pl.pallas_call(
        body,
        grid=(R // tr,),
        in_specs=[blk, blk, blk, pl.BlockSpec((P, tr, C), lambda r: (0, r, 0))],
        out_specs=[blk, blk, blk, blk],
        out_shape=[shp, shp, shp, shp],
        compiler_params=_cp(("parallel",)),
        name=name,
    )(w, m, v, parts)


def adamw_reduced(w, m, v, own, land, chip, name):
    R, C = w.shape
    if R % 8 == 0:
        tr, tc = _pick(R, (256, 128, 64, 32, 16, 8)), C
    else:
        tr, tc = R, _pick(C, (256, 128))
    c1 = 1.0 - ADAM_B1 ** ADAM_STEP
    c2 = 1.0 - ADAM_B2 ** ADAM_STEP

    def body(chip_ref, w_ref, m_ref, v_ref, own_ref, land_ref, g_ref, d_ref, nm_ref, nv_ref):
        mine = own_ref[0].astype(F32)
        g = None
        for k in range(4):
            term = jnp.where(chip_ref[0] == k, mine, land_ref[k].astype(F32))
            g = term if g is None else g + term
        nm = ADAM_B1 * m_ref[...] + (1.0 - ADAM_B1) * g
        nv = ADAM_B2 * v_ref[...] + (1.0 - ADAM_B2) * (g * g)
        g_ref[...] = g
        nm_ref[...] = nm
        nv_ref[...] = nv
        d_ref[...] = -ADAM_LR * ((nm / c1) / (jnp.sqrt(nv / c2) + ADAM_EPS) + ADAM_WD * w_ref[...])

    blk = pl.BlockSpec((tr, tc), lambda r, c, chip_ref: (r, c))
    shp = jax.ShapeDtypeStruct((R, C), F32)
    return pl.pallas_call(
        body,
        grid_spec=pltpu.PrefetchScalarGridSpec(
            num_scalar_prefetch=1,
            grid=(R // tr, C // tc),
            in_specs=[
                blk, blk, blk,
                pl.BlockSpec((1, tr, tc), lambda r, c, chip_ref: (chip_ref[0], r, c)),
                pl.BlockSpec((4, tr, tc), lambda r, c, chip_ref: (0, r, c)),
            ],
            out_specs=[blk, blk, blk, blk],
        ),
        out_shape=[shp, shp, shp, shp],
        compiler_params=_cp(("parallel", "parallel")),
        name=name,
    )(chip, w, m, v, own, land)


_WEIGHTS = [
    "l0_mix_norm_g", "l0_w_in", "l0_sc_conv_w", "l0_w_out", "l0_ffn_norm_g", "l0_ffn_up", "l0_ffn_conv_w", "l0_ffn_down",
    "l1_mix_norm_g", "l1_w_in", "l1_fox_b_f", "l1_sg_w", "l1_sg_b", "l1_sg_norm_g", "l1_w_out", "l1_ffn_norm_g",
    "l1_ffn_up", "l1_ffn_conv_w", "l1_ffn_down", "final_norm_g",
]
_ROW_SHARDED = ["l0_w_out", "l0_ffn_down", "l1_w_out", "l1_ffn_down"]
_BIG = ["l0_w_in", "l0_w_out", "l0_ffn_up", "l0_ffn_down", "l1_w_in", "l1_w_out", "l1_ffn_up", "l1_ffn_down"]
_CONV = ["l0_sc_conv_w", "l0_ffn_conv_w", "l1_ffn_conv_w"]
_SMALL = [n for n in _WEIGHTS if n not in _BIG]
_LAST_SMALL = "l0_mix_norm_g"
_PACK_ROWS = 8


def _pack(arrs):
    flat = []
    for a in arrs:
        v = a.reshape(-1).astype(F32)
        pad = (-v.shape[0]) % (_PACK_ROWS * 128)
        flat.append(jnp.pad(v, (0, pad)))
    return jnp.concatenate(flat).reshape(-1, 128)


def _unpack(packed, shapes):
    out, off = [], 0
    flat = packed.reshape(-1)
    for shp in shapes:
        size = math.prod(shp)
        out.append(flat[off : off + size].reshape(shp))
        off += size + (-size) % (_PACK_ROWS * 128)
    return out


def kernel(x, l0_mix_norm_g, l0_w_in, l0_sc_conv_w, l0_w_out, l0_ffn_norm_g, l0_ffn_up, l0_ffn_conv_w, l0_ffn_down, l1_mix_norm_g, l1_w_in, l1_fox_b_f, l1_sg_w, l1_sg_b, l1_sg_norm_g, l1_w_out, l1_ffn_norm_g, l1_ffn_up, l1_ffn_conv_w, l1_ffn_down, final_norm_g, loss_target, m_l0_mix_norm_g, m_l0_w_in, m_l0_sc_conv_w, m_l0_w_out, m_l0_ffn_norm_g, m_l0_ffn_up, m_l0_ffn_conv_w, m_l0_ffn_down, m_l1_mix_norm_g, m_l1_w_in, m_l1_fox_b_f, m_l1_sg_w, m_l1_sg_b, m_l1_sg_norm_g, m_l1_w_out, m_l1_ffn_norm_g, m_l1_ffn_up, m_l1_ffn_conv_w, m_l1_ffn_down, m_final_norm_g, v_l0_mix_norm_g, v_l0_w_in, v_l0_sc_conv_w, v_l0_w_out, v_l0_ffn_norm_g, v_l0_ffn_up, v_l0_ffn_conv_w, v_l0_ffn_down, v_l1_mix_norm_g, v_l1_w_in, v_l1_fox_b_f, v_l1_sg_w, v_l1_sg_b, v_l1_sg_norm_g, v_l1_w_out, v_l1_ffn_norm_g, v_l1_ffn_up, v_l1_ffn_conv_w, v_l1_ffn_down, v_final_norm_g):
    given = dict(locals())
    w = {n: given[n] for n in _WEIGHTS}
    mom = {n: given["m_" + n] for n in _WEIGHTS}
    var = {n: given["v_" + n] for n in _WEIGHTS}
    xs, target = x[0], loss_target[0]
    S, D = xs.shape
    W = D // 2
    nh = W // HD
    cx, cy, cc = _place()
    me = 4 * cx + 2 * cy + cc

    wts = {"nb": NDEV, "F": l0_ffn_down.shape[0] * NDEV}
    for n in _SMALL:
        if n not in _CONV:
            wts[n] = w[n]
    gathered, loss_sum = {}, []

    def start_gather(names):
        srcs = [(w[n].T if n == "l1_w_in" else w[n]).astype(BF16) for n in names]
        taps = [w[c] for c in _CONV] if names[0] == _BIG[0] else []
        got = all_gather(srcs + taps, "gather_" + "_".join(names))
        for n, full in zip(names, got):
            if n == "l1_w_in":
                gathered[n] = full
            elif n in _ROW_SHARDED:
                wts[n] = full.reshape(-1, D)
            else:
                wts[n] = full.reshape(NDEV * D, -1)
        for c, full in zip(_CONV, got[len(names):] if taps else []):
            wts[c] = full.transpose(1, 0, 2).reshape(CONV_K, -1)

    def at(point, after, value):
        if point == "l1_w_in":
            got, after = lax.optimization_barrier((gathered[point], after))
            wts["l1_w_in_t"] = got.reshape(-1, D)
            wts["l1_w_f_t"] = jnp.pad(wts["l1_w_in_t"][5 * W :], ((0, 128 - nh), (0, 0)))
        elif point == "loss":
            gathered["loss"] = value[0, :1]
        elif point == "small_ready":
            early = [n for n in _SMALL if n != _LAST_SMALL]
            gathered["small"] = all_gather([_pack([value[n] for n in early] + [gathered["loss"]])], "gather_small_grads")[0]
        elif point == "small_done":
            after = update_small([n for n in _SMALL if n != _LAST_SMALL], gathered["small"], "small", after, True)
        return after

    out_g, out_d, out_m, out_v = {}, {}, {}, {}

    def update_small(names, all_terms, tag, after=None, with_loss=False):
        shapes = [w[n].shape for n in names]
        full_shapes = [(CONV_K, NDEV * w[n].shape[1]) if n in _CONV else w[n].shape for n in names]
        summed = _unpack(sum_slots(all_terms, f"sum_{tag}_grads"), full_shapes + ([(1,)] if with_loss else []))
        if with_loss:
            loss_sum.append(summed[-1][0])
        grads = {}
        for n, t in zip(names, summed):
            if n in _CONV:
                cols = w[n].shape[1]
                t = lax.dynamic_slice_in_dim(t, me * cols, cols, axis=1)
            grads[n] = t
        res = adamw(
            _pack([w[n] for n in names]), _pack([mom[n] for n in names]), _pack([var[n] for n in names]),
            _pack([grads[n] for n in names])[None], f"adamw_{tag}",
        )
        if after is not None:
            res, after = lax.optimization_barrier((res, after))
        for dst, packed_out in zip((out_g, out_d, out_m, out_v), res):
            for n, t in zip(names, _unpack(packed_out, shapes)):
                dst[n] = t
        return after

    core = jnp.reshape(cc, (1,)).astype(jnp.int32)
    chip = jnp.reshape(2 * cx + cy, (1,)).astype(jnp.int32)
    pair_flying, chip_flying = [], []

    def tie(value, after):
        if after is None:
            return value, None
        return lax.optimization_barrier((value, after))

    def advance(after, new=None):
        jobs, names = [], []
        if pair_flying:
            n0, flying = pair_flying.pop()
            landed = _split_wait(_pair_copies, *flying, f"reduce_pair_wait_{n0}")
            summed = pair_sum(flying[2], landed, core, f"pair_sum_{n0}")
            jobs.append((_chip_copies, summed, summed.shape, 3))
            names.append(n0)
        if new is not None:
            jobs.append((_pair_copies, new[1], new[1].shape[:1] + new[1].shape[2:], 4))
            names.append(new[0])
        started, token = _split_start_many(jobs, "reduce_start_" + "_".join(names))
        token, after = tie(token, after)
        if new is not None:
            pair_flying.append((new[0], started.pop() + [token]))
        if started:
            chip_flying.append((names[0], started[0] + [token]))
        return after

    def update(after, behind=None):
        n, flying = chip_flying.pop(0)
        if behind is not None:
            flying[4], _ = lax.optimization_barrier((flying[4], behind))
        landed = _split_wait(_chip_copies, *flying, f"reduce_chips_wait_{n}")
        turn = (lambda t: t.T) if n == "l1_w_in" else (lambda t: t)
        res = adamw_reduced(turn(w[n]), turn(mom[n]), turn(var[n]), flying[2], landed, chip, f"adamw_{n}")
        res, after = tie(res, after)
        out_g[n], out_d[n], out_m[n], out_v[n] = [turn(t) for t in res]
        return after, res[0]

    def on_grad(n, term, after):
        if n is None:
            return advance(after)
        if n in _ROW_SHARDED or n == "l1_w_in":
            term = term.reshape(NDEV, -1, D)
        else:
            term = term.reshape(NDEV, D, -1)
        term = term.reshape((4, 2) + term.shape[1:])
        if len(chip_flying) == UPDATE_LAG:
            after, _ = update(after)
        return advance(after, (n, term))

    for n in _BIG:
        start_gather([n])
    dx, g = local_step(xs, target, wts, at, on_grad)
    last = _pack([g[_LAST_SMALL]])
    *flying, done = _split_start(_direct_copies, last, (NDEV,) + last.shape, NDEV - 1, "gather_last_grad")
    while len(chip_flying) > 1:
        _, done = update(None, behind=done)
    landed = _split_wait(_direct_copies, *flying, done, "gather_last_grad_wait")
    update_small([_LAST_SMALL], lax.dynamic_update_slice(landed, last[None], (me, 0, 0)), "last")
    update(None, behind=out_g[_LAST_SMALL])
    loss = loss_sum[0]

    return (loss, dx[None], *[out_g[n] for n in _WEIGHTS], *[out_d[n] for n in _WEIGHTS],
            *[out_m[n] for n in _WEIGHTS], *[out_v[n] for n in _WEIGHTS])
```

```python
import functools
import math

import jax
import jax.numpy as jnp
from jax import lax
from jax.experimental import pallas as pl
from jax.experimental.pallas import tpu as pltpu
from jax.experimental.pallas import tpu_sc as plsc

F32 = jnp.float32
BF16 = jnp.bfloat16
HD = 128
EPS = 1e-6
CONV_K = 3
VMEM_LIMIT_BYTES = 48 << 20
NDEV = 8
MESH = pl.DeviceIdType.MESH

ADAM_LR = 0.001
ADAM_B1 = 0.9
ADAM_B2 = 0.999
ADAM_EPS = 1e-08
ADAM_WD = 0.01
ADAM_STEP = 10


def _cp(sem):
    return pltpu.CompilerParams(dimension_semantics=sem, vmem_limit_bytes=VMEM_LIMIT_BYTES)


def _pick(n, prefs):
    for p in prefs:
        if n % p == 0:
            return p
    return n


def _dot(a, b):
    return jnp.dot(a, b, preferred_element_type=F32)


def _dot_nt(a, b):
    return lax.dot_general(a, b, (((1,), (1,)), ((), ())), preferred_element_type=F32)


def _dot_tn(a, b):
    return lax.dot_general(a, b, (((0,), (0,)), ((), ())), preferred_element_type=F32)


def _split3(x):
    hi = x.astype(BF16)
    r = x - hi.astype(F32)
    mid = r.astype(BF16)
    lo = (r - mid.astype(F32)).astype(BF16)
    return hi, mid, lo


def _dot_ones_left(ones_bf16, x):
    hi, mid, lo = _split3(x)
    return _dot(ones_bf16, hi) + _dot(ones_bf16, mid) + _dot(ones_bf16, lo)


def _iota2(shape, axis):
    return lax.broadcasted_iota(jnp.int32, shape, axis)


def mm_nn(a, w2d, nb, name, out_dtype=BF16, res=None, tm=None, tn=None, tk=None, a_map=None, a_shape=None):
    M, K = a_shape or a.shape
    n = w2d.shape[1]
    assert w2d.shape[0] == nb * K or (nb == 1 and w2d.shape[0] > K)
    a_map = a_map or (lambda i, k: (i, k))
    tm = tm or _pick(M, (1024, 512, 256, 128))
    tn = tn or _pick(n, (1408, 1024, 768, 512, 256, 128))
    tk = tk or (K if K <= 2048 else _pick(K, (1408, 1024, 512, 256, 128)))
    nk, nt = K // tk, n // tn
    has_res = res is not None

    def body(*refs):
        if has_res:
            a_ref, w_ref, r_ref, o_ref = refs[:4]
        else:
            a_ref, w_ref, o_ref = refs[:3]
            r_ref = None
        part = _dot(a_ref[...], w_ref[...])

        def finish(acc):
            if r_ref is not None:
                acc = acc + r_ref[...].astype(F32)
            o_ref[...] = acc.astype(o_ref.dtype)

        if nk == 1:
            finish(part)
        else:
            acc_ref = refs[-1]
            k = pl.program_id(3)

            @pl.when(k == 0)
            def _():
                acc_ref[...] = part

            @pl.when(k > 0)
            def _():
                acc_ref[...] += part

            @pl.when(k == nk - 1)
            def _():
                finish(acc_ref[...])

    in_specs = [
        pl.BlockSpec((tm, tk), lambda i, j, t, k: a_map(i, k)),
        pl.BlockSpec((tk, tn), lambda i, j, t, k: (j * nk + k, t)),
    ]
    args = [a, w2d]
    out_spec = pl.BlockSpec((tm, tn), lambda i, j, t, k: (i, j * nt + t))
    if has_res:
        in_specs.append(out_spec)
        args.append(res)
    return pl.pallas_call(
        body,
        grid=(M // tm, nb, nt, nk),
        in_specs=in_specs,
        out_specs=out_spec,
        out_shape=jax.ShapeDtypeStruct((M, nb * n), out_dtype),
        scratch_shapes=[pltpu.VMEM((tm, tn), F32)] if nk > 1 else [],
        compiler_params=_cp(("parallel", "parallel", "parallel", "arbitrary")),
        name=name,
    )(*args)


def mm_nt(dy2d, w2d, nb, M, K, name, out_dtype=BF16, res=None, dy_maps=None, tm=None, tko=None, tn=None):
    n = w2d.shape[1]
    assert w2d.shape[0] == nb * K or (nb == 1 and w2d.shape[0] > K)
    tm = tm or _pick(M, (1024, 512, 256, 128))
    tko = tko or _pick(K, (1024, 512, 256, 128))
    tn = tn or _pick(n, (1408, 1024, 768, 512, 256, 128))
    nt, nko = n // tn, K // tko
    has_res = res is not None
    if dy_maps is None:
        dy_maps = [lambda i, j, t: (i, j * nt + t)]
    nd = len(dy_maps)
    td = tn // nd

    one_step = nb * nt == 1

    def body(*refs):
        d_refs, w_ref = refs[:nd], refs[nd]
        r_ref = refs[nd + 1] if has_res else None
        d = d_refs[0][...] if nd == 1 else jnp.concatenate([r[...] for r in d_refs], axis=1)
        part = _dot_nt(d, w_ref[...])
        if one_step:
            o_ref = refs[-1]
            if r_ref is not None:
                part = part + r_ref[...].astype(F32)
            o_ref[...] = part.astype(o_ref.dtype)
            return
        o_ref, acc_ref = refs[-2], refs[-1]
        j, t = pl.program_id(2), pl.program_id(3)
        first = jnp.logical_and(j == 0, t == 0)
        last = jnp.logical_and(j == nb - 1, t == nt - 1)

        @pl.when(first)
        def _():
            acc_ref[...] = part

        @pl.when(jnp.logical_not(first))
        def _():
            acc_ref[...] += part

        @pl.when(last)
        def _():
            acc = acc_ref[...]
            if r_ref is not None:
                acc = acc + r_ref[...].astype(F32)
            o_ref[...] = acc.astype(o_ref.dtype)

    in_specs = [pl.BlockSpec((tm, td), functools.partial(lambda f, i, ko, j, t: f(i, j, t), f)) for f in dy_maps]
    in_specs.append(pl.BlockSpec((tko, tn), lambda i, ko, j, t: (j * nko + ko, t)))
    args = [dy2d] * nd + [w2d]
    out_spec = pl.BlockSpec((tm, tko), lambda i, ko, j, t: (i, ko))
    if has_res:
        in_specs.append(out_spec)
        args.append(res)
    return pl.pallas_call(
        body,
        grid=(M // tm, nko, nb, nt),
        in_specs=in_specs,
        out_specs=out_spec,
        out_shape=jax.ShapeDtypeStruct((M, K), out_dtype),
        scratch_shapes=[] if one_step else [pltpu.VMEM((tm, tko), F32)],
        compiler_params=_cp(("parallel", "parallel", "arbitrary", "arbitrary")),
        name=name,
    )(*args)


def mm_tn(x, dy2d, nb, n, name, out_dtype=BF16, dy_maps=None, tko=None, tn=None, x_map=None, x_shape=None):
    S, K = x_shape or x.shape
    x_map = x_map or (lambda ko: (0, ko))
    tko = tko or _pick(K, (512, 256, 128))
    tn = tn or _pick(n, (1408, 1024, 768, 512, 256, 128))
    nt, nko = n // tn, K // tko
    if dy_maps is None:
        dy_maps = [lambda j, t: (0, j * nt + t)]
    nd = len(dy_maps)
    td = tn // nd

    def body(*refs):
        x_ref, d_refs, o_ref = refs[0], refs[1 : 1 + nd], refs[-1]
        d = d_refs[0][...] if nd == 1 else jnp.concatenate([r[...] for r in d_refs], axis=1)
        o_ref[...] = _dot_tn(x_ref[...], d).astype(o_ref.dtype)

    in_specs = [pl.BlockSpec((S, tko), lambda ko, j, t: x_map(ko))]
    in_specs += [pl.BlockSpec((S, td), functools.partial(lambda f, ko, j, t: f(j, t), f)) for f in dy_maps]
    return pl.pallas_call(
        body,
        grid=(nko, nb, nt),
        in_specs=in_specs,
        out_specs=pl.BlockSpec((tko, tn), lambda ko, j, t: (j * nko + ko, t)),
        out_shape=jax.ShapeDtypeStruct((nb * K, n), out_dtype),
        compiler_params=_cp(("parallel", "parallel", "parallel")),
        name=name,
    )(x, *([dy2d] * nd))


def rms_fwd(x, g, name):
    S, D = x.shape
    tm = _pick(S, (512, 256, 128))

    def body(x_ref, g_ref, o_ref):
        xf = x_ref[...]
        r = lax.rsqrt(jnp.mean(xf * xf, axis=-1, keepdims=True) + EPS)
        o_ref[...] = (xf * r * g_ref[...]).astype(o_ref.dtype)

    return pl.pallas_call(
        body,
        grid=(S // tm,),
        in_specs=[pl.BlockSpec((tm, D), lambda i: (i, 0)), pl.BlockSpec((1, D), lambda i: (0, 0))],
        out_specs=pl.BlockSpec((tm, D), lambda i: (i, 0)),
        out_shape=jax.ShapeDtypeStruct((S, D), BF16),
        compiler_params=_cp(("parallel",)),
        name=name,
    )(x, g.reshape(1, D))


def rms_bwd(x, g, dh, dres, name):
    S, D = x.shape
    tm = _pick(S, (256, 128))

    def body(x_ref, g_ref, dh_ref, dr_ref, dx_ref, dxb_ref, dg_ref):
        i = pl.program_id(0)
        xf = x_ref[...]
        dh = dh_ref[...].astype(F32)
        r = lax.rsqrt(jnp.mean(xf * xf, axis=-1, keepdims=True) + EPS)
        gy = dh * g_ref[...]
        proj = jnp.mean(gy * xf, axis=-1, keepdims=True)
        dx = dr_ref[...] + r * gy - xf * (r * r * r * proj)
        dx_ref[...] = dx
        dxb_ref[...] = dx.astype(BF16)
        dg = jnp.sum(dh * (xf * r), axis=0, keepdims=True)

        @pl.when(i == 0)
        def _():
            dg_ref[...] = dg

        @pl.when(i > 0)
        def _():
            dg_ref[...] += dg

    row = pl.BlockSpec((tm, D), lambda i: (i, 0))
    vec = pl.BlockSpec((1, D), lambda i: (0, 0))
    return pl.pallas_call(
        body,
        grid=(S // tm,),
        in_specs=[row, vec, row, row],
        out_specs=[row, row, vec],
        out_shape=[jax.ShapeDtypeStruct((S, D), F32), jax.ShapeDtypeStruct((S, D), BF16), jax.ShapeDtypeStruct((1, D), F32)],
        compiler_params=_cp(("arbitrary",)),
        name=name,
    )(x, g.reshape(1, D), dh, dres)


def loss_head(x, g, target, name):
    S, D = x.shape
    tm = _pick(S, (256, 128))

    def body(x_ref, g_ref, t_ref, dx_ref, dxb_ref, dg_ref, loss_ref):
        i = pl.program_id(0)
        xf = x_ref[...]
        gg = g_ref[...]
        r = lax.rsqrt(jnp.mean(xf * xf, axis=-1, keepdims=True) + EPS)
        xh = xf * r
        err = xh * gg - t_ref[...]
        part = (0.5 / D) * jnp.sum(err * err)
        dy = err * (1.0 / D)
        gy = dy * gg
        proj = jnp.mean(gy * xf, axis=-1, keepdims=True)
        dx = r * gy - xf * (r * r * r * proj)
        dx_ref[...] = dx
        dxb_ref[...] = dx.astype(BF16)
        dg = jnp.sum(dy * xh, axis=0, keepdims=True)
        lossb = jnp.full(loss_ref.shape, part, F32)

        @pl.when(i == 0)
        def _():
            dg_ref[...] = dg
            loss_ref[...] = lossb

        @pl.when(i > 0)
        def _():
            dg_ref[...] += dg
            loss_ref[...] += lossb

    row = pl.BlockSpec((tm, D), lambda i: (i, 0))
    vec = pl.BlockSpec((1, D), lambda i: (0, 0))
    return pl.pallas_call(
        body,
        grid=(S // tm,),
        in_specs=[row, vec, row],
        out_specs=[row, row, vec, pl.BlockSpec((8, 128), lambda i: (0, 0))],
        out_shape=[
            jax.ShapeDtypeStruct((S, D), F32),
            jax.ShapeDtypeStruct((S, D), BF16),
            jax.ShapeDtypeStruct((1, D), F32),
            jax.ShapeDtypeStruct((8, 128), F32),
        ],
        compiler_params=_cp(("arbitrary",)),
        name=name,
    )(x, g.reshape(1, D), target)


def _shift_down(s, k):
    if k == 0:
        return s
    return jnp.where(_iota2(s.shape, 0) >= k, pltpu.roll(s, k, axis=0), 0.0)


def _shift_up(s, k):
    if k == 0:
        return s
    n = s.shape[0]
    return jnp.where(_iota2(s.shape, 0) < n - k, pltpu.roll(s, n - k, axis=0), 0.0)


def _conv(s, w):
    return w[0:1] * _shift_down(s, 2) + w[1:2] * _shift_down(s, 1) + w[2:3] * s


def _conv_t(d, w):
    return w[2:3] * d + w[1:2] * _shift_up(d, 1) + w[0:1] * _shift_up(d, 2)


def _conv_dw(d, s):
    return [jnp.sum(d * _shift_down(s, CONV_K - 1 - k), axis=0, keepdims=True) for k in range(CONV_K)]


def sc_fwd(p, convw, cat, W, name):
    S = p.shape[0]
    tc = _pick(W, (256, 128))
    nc = W // tc

    def body(gb_ref, gc_ref, hi_ref, w_ref, cat_ref, o_ref):
        s = gc_ref[...].astype(F32) * hi_ref[...].astype(F32)
        o_ref[...] = (gb_ref[...].astype(F32) * _conv(s, w_ref[...])).astype(o_ref.dtype)

    col = lambda part: pl.BlockSpec((S, tc), lambda c: (0, part * nc + c))
    return pl.pallas_call(
        body,
        grid=(nc,),
        in_specs=[col(3), col(4), col(5), pl.BlockSpec((CONV_K, tc), lambda c: (0, c)), pl.BlockSpec(memory_space=pl.ANY)],
        out_specs=col(1),
        out_shape=jax.ShapeDtypeStruct(cat.shape, cat.dtype),
        input_output_aliases={4: 0},
        compiler_params=_cp(("parallel",)),
        name=name,
    )(p, p, p, convw, cat)


def sc_bwd(p, convw, dcat, dp, W, name):
    S = p.shape[0]
    tc = _pick(W, (256, 128))
    nc = W // tc

    def body(gb_ref, gc_ref, hi_ref, w_ref, do_ref, dp_in_ref, dp_ref, dw_ref):
        gb = gb_ref[...].astype(F32)
        gc = gc_ref[...].astype(F32)
        hi = hi_ref[...].astype(F32)
        w = w_ref[...]
        do = do_ref[...].astype(F32)
        s = gc * hi
        dcs = do * gb
        ds = _conv_t(dcs, w)
        dp_ref[0] = (do * _conv(s, w)).astype(dp_ref.dtype)
        dp_ref[1] = (ds * hi).astype(dp_ref.dtype)
        dp_ref[2] = (ds * gc).astype(dp_ref.dtype)
        for k, row in enumerate(_conv_dw(dcs, s)):
            dw_ref[k : k + 1, :] = row

    col = lambda part: pl.BlockSpec((S, tc), lambda c: (0, part * nc + c))
    return pl.pallas_call(
        body,
        grid=(nc,),
        in_specs=[
            col(3), col(4), col(5),
            pl.BlockSpec((CONV_K, tc), lambda c: (0, c)),
            pl.BlockSpec((S, tc), lambda c: (0, nc + c)),
            pl.BlockSpec(memory_space=pl.ANY),
        ],
        out_specs=[pl.BlockSpec((3, S, tc), lambda c: (1, 0, c)), pl.BlockSpec((CONV_K, tc), lambda c: (0, c))],
        out_shape=[jax.ShapeDtypeStruct(dp.shape, dp.dtype), jax.ShapeDtypeStruct((CONV_K, W), F32)],
        input_output_aliases={5: 0},
        compiler_params=_cp(("parallel",)),
        name=name,
    )(p, p, p, convw, dcat, dp)


def _silu_parts(a):
    sig = 1.0 / (1.0 + jnp.exp(-a))
    return a * sig, sig


def ffn_act_fwd(u, convw, F, name):
    S = u.shape[0]
    tc = _pick(F, (256, 128))
    nc = F // tc

    def body(ug_ref, uu_ref, wg_ref, wu_ref, o_ref):
        ag = _conv(ug_ref[...].astype(F32), wg_ref[...])
        au = _conv(uu_ref[...].astype(F32), wu_ref[...])
        o_ref[...] = (_silu_parts(ag)[0] * au).astype(o_ref.dtype)

    col = lambda half: pl.BlockSpec((S, tc), lambda c: (0, half * nc + c))
    wcol = lambda half: pl.BlockSpec((CONV_K, tc), lambda c: (0, half * nc + c))
    return pl.pallas_call(
        body,
        grid=(nc,),
        in_specs=[col(0), col(1), wcol(0), wcol(1)],
        out_specs=pl.BlockSpec((S, tc), lambda c: (0, c)),
        out_shape=jax.ShapeDtypeStruct((S, F), BF16),
        compiler_params=_cp(("parallel",)),
        name=name,
    )(u, u, convw, convw)


def ffn_act_bwd(u, convw, dact, F, name):
    S = u.shape[0]
    tc = _pick(F, (256, 128))
    nc = F // tc

    def body(ug_ref, uu_ref, wg_ref, wu_ref, da_ref, du_ref, dw_ref):
        ug = ug_ref[...].astype(F32)
        uu = uu_ref[...].astype(F32)
        wg = wg_ref[...]
        wu = wu_ref[...]
        da = da_ref[...].astype(F32)
        ag = _conv(ug, wg)
        au = _conv(uu, wu)
        sl, sig = _silu_parts(ag)
        dag = da * au * (sig * (1.0 + ag * (1.0 - sig)))
        dau = da * sl
        du_ref[0] = _conv_t(dag, wg).astype(du_ref.dtype)
        du_ref[1] = _conv_t(dau, wu).astype(du_ref.dtype)
        for k, (rg, ru) in enumerate(zip(_conv_dw(dag, ug), _conv_dw(dau, uu))):
            dw_ref[0, k : k + 1, :] = rg
            dw_ref[1, k : k + 1, :] = ru

    col = lambda half: pl.BlockSpec((S, tc), lambda c: (0, half * nc + c))
    wcol = lambda half: pl.BlockSpec((CONV_K, tc), lambda c: (0, half * nc + c))
    return pl.pallas_call(
        body,
        grid=(nc,),
        in_specs=[col(0), col(1), wcol(0), wcol(1), pl.BlockSpec((S, tc), lambda c: (0, c))],
        out_specs=[pl.BlockSpec((2, S, tc), lambda c: (0, 0, c)), pl.BlockSpec((2, CONV_K, tc), lambda c: (0, 0, c))],
        out_shape=[jax.ShapeDtypeStruct((2, S, F), BF16), jax.ShapeDtypeStruct((2, CONV_K, F), F32)],
        compiler_params=_cp(("parallel",)),
        name=name,
    )(u, u, convw, convw, dact)


def _softplus(z):
    return jnp.maximum(z, 0.0) + jnp.log(1.0 + jnp.exp(-jnp.abs(z)))


def _key_strip(S):
    return _pick(S, (512, 256, 128))


def _query_rows(S):
    tq = _pick(S, (512, 256, 128))
    assert _key_strip(S) % tq == 0
    return tq


def _split2(x):
    hi = x.astype(BF16)
    return hi, (x - hi.astype(F32)).astype(BF16)


def _block_sums(x, ones_bf16):
    hi, lo = _split2(x)
    return [
        _dot(hi[:, b * HD : (b + 1) * HD], ones_bf16) + _dot(lo[:, b * HD : (b + 1) * HD], ones_bf16)
        for b in range(x.shape[1] // HD)
    ]


def _strip_mask(shape, row0, off, strict):
    cols, rows = _iota2(shape, 1) + off, _iota2(shape, 0) + row0
    return cols < rows if strict else cols <= rows


def _sb_strip(q, ks, row0, off, run, su, masked):
    z = _dot_nt(q, ks) * (HD ** -0.5)
    sp = _softplus(z)
    mask = _strip_mask(z.shape, row0, off, True) if masked else None
    l = jnp.where(mask, -sp, 0.0) if masked else -sp
    within = _block_sums(l, su)
    later = [None] * len(within)
    for b in reversed(range(len(within))):
        later[b] = within[b] + run
        run = run + jnp.sum(l[:, b * HD : (b + 1) * HD], axis=1, keepdims=True)
    a = jnp.exp(z - sp + jnp.concatenate(later, axis=1))
    return z, (jnp.where(mask, a, 0.0) if masked else a), run


def sb_fwd(p, W, name):
    S = p.shape[0]
    TQ, TK = _query_rows(S), _key_strip(S)
    nh, nq = W // HD, S // TQ

    def body(q_ref, k_ref, v_ref, o_ref):
        i = pl.program_id(1)
        q = q_ref[...]
        su = (_iota2((HD, HD), 0) > _iota2((HD, HD), 1)).astype(BF16)
        last = (i * TQ + TQ - 1) // TK

        def strip(g, carry, masked):
            acc, run = carry
            off = pl.multiple_of(g * TK, TK)
            _, a, run = _sb_strip(q, k_ref[pl.ds(off, TK), :], i * TQ, off, run, su, masked)
            return acc + _dot(a.astype(BF16), v_ref[pl.ds(off, TK), :]), run

        carry = strip(last, (jnp.zeros((TQ, HD), F32), jnp.zeros((TQ, 1), F32)), True)
        acc, _ = lax.fori_loop(0, last, lambda gg, c: strip(last - 1 - gg, c, False), carry)
        o_ref[...] = acc.astype(o_ref.dtype)

    return pl.pallas_call(
        body,
        grid=(nh, nq),
        in_specs=[
            pl.BlockSpec((TQ, HD), lambda h, i: (i, h)),
            pl.BlockSpec((S, HD), lambda h, i: (0, nh + h)),
            pl.BlockSpec((S, HD), lambda h, i: (0, 2 * nh + h)),
        ],
        out_specs=pl.BlockSpec((TQ, HD), lambda h, i: (i, h)),
        out_shape=jax.ShapeDtypeStruct((S, 2 * W), BF16),
        compiler_params=_cp(("parallel", "arbitrary")),
        name=name,
    )(p, p, p)


def sb_bwd(p, dcat, W, name):
    S = p.shape[0]
    TQ, TK = _query_rows(S), _key_strip(S)
    nh, nq = W // HD, S // TQ
    scale = HD ** -0.5

    def body(q_ref, k_ref, v_ref, do_ref, dp_ref, dk_acc, dv_acc, e_scr, z_scr):
        i = pl.program_id(1)
        q = q_ref[...]
        do = do_ref[...]
        su = (_iota2((HD, HD), 0) > _iota2((HD, HD), 1)).astype(BF16)
        sl = (_iota2((HD, HD), 0) < _iota2((HD, HD), 1)).astype(BF16)
        last = (i * TQ + TQ - 1) // TK

        @pl.when(i == 0)
        def _():
            dk_acc[...] = jnp.zeros_like(dk_acc)
            dv_acc[...] = jnp.zeros_like(dv_acc)

        def pass_a(g, run, masked):
            off = pl.multiple_of(g * TK, TK)
            z, a, run = _sb_strip(q, k_ref[pl.ds(off, TK), :], i * TQ, off, run, su, masked)
            e_scr[g] = a * _dot_nt(do, v_ref[pl.ds(off, TK), :])
            z_scr[g] = z
            dv_acc[pl.ds(off, TK), :] += _dot_tn(a.astype(BF16), do)
            return run

        run = pass_a(last, jnp.zeros((TQ, 1), F32), True)
        lax.fori_loop(0, last, lambda gg, r: pass_a(last - 1 - gg, r, False), run)

        def pass_b(g, carry, masked):
            dq, run_e = carry
            off = pl.multiple_of(g * TK, TK)
            e = e_scr[g]
            z = z_scr[g]
            within = _block_sums(e, sl)
            before = []
            for b in range(len(within)):
                before.append(within[b] + run_e)
                run_e = run_e + jnp.sum(e[:, b * HD : (b + 1) * HD], axis=1, keepdims=True)
            sig = 1.0 / (1.0 + jnp.exp(-z))
            dz = e * (1.0 - sig) - jnp.concatenate(before, axis=1) * sig
            if masked:
                dz = jnp.where(_strip_mask(z.shape, i * TQ, off, True), dz, 0.0)
            dz = (dz * scale).astype(BF16)
            dq = dq + _dot(dz, k_ref[pl.ds(off, TK), :])
            dk_acc[pl.ds(off, TK), :] += _dot_tn(dz, q)
            return dq, run_e

        carry = lax.fori_loop(0, last, lambda g, c: pass_b(g, c, False), (jnp.zeros((TQ, HD), F32), jnp.zeros((TQ, 1), F32)))
        dq, _ = pass_b(last, carry, True)
        dp_ref[0, pl.ds(pl.multiple_of(i * TQ, TQ), TQ), :] = dq.astype(dp_ref.dtype)

        @pl.when(i == nq - 1)
        def _():
            dp_ref[1] = dk_acc[...].astype(dp_ref.dtype)
            dp_ref[2] = dv_acc[...].astype(dp_ref.dtype)

    return pl.pallas_call(
        body,
        grid=(nh, nq),
        in_specs=[
            pl.BlockSpec((TQ, HD), lambda h, i: (i, h)),
            pl.BlockSpec((S, HD), lambda h, i: (0, nh + h)),
            pl.BlockSpec((S, HD), lambda h, i: (0, 2 * nh + h)),
            pl.BlockSpec((TQ, HD), lambda h, i: (i, h)),
        ],
        out_specs=pl.BlockSpec((3, S, HD), lambda h, i: (0, 0, h)),
        out_shape=jax.ShapeDtypeStruct((6, S, W), BF16),
        scratch_shapes=[
            pltpu.VMEM((S, HD), F32),
            pltpu.VMEM((S, HD), F32),
            pltpu.VMEM((S // TK, TQ, TK), F32),
            pltpu.VMEM((S // TK, TQ, TK), F32),
        ],
        compiler_params=_cp(("parallel", "arbitrary")),
        name=name,
    )(p, p, p, dcat)


def fox_gate_fwd(f, b, name):
    S = f.shape[0]
    nq = S // HD

    def body(f_ref, b_ref, c_ref, run):
        i = pl.program_id(0)

        @pl.when(i == 0)
        def _():
            run[...] = jnp.zeros_like(run)

        lf = -_softplus(-(f_ref[...] + b_ref[...]))
        tri = (_iota2((HD, HD), 0) >= _iota2((HD, HD), 1)).astype(BF16)
        c_ref[...] = _dot_ones_left(tri, lf) + run[...]
        run[...] += jnp.sum(lf, axis=0, keepdims=True)

    return pl.pallas_call(
        body,
        grid=(nq,),
        in_specs=[pl.BlockSpec((HD, 128), lambda i: (i, 0)), pl.BlockSpec((1, 128), lambda i: (0, 0))],
        out_specs=pl.BlockSpec((HD, 128), lambda i: (i, 0)),
        out_shape=jax.ShapeDtypeStruct((S, 128), F32),
        scratch_shapes=[pltpu.VMEM((1, 128), F32)],
        compiler_params=_cp(("arbitrary",)),
        name=name,
    )(f, b)


def fox_gate_bwd(f, b, dc, name):
    S = f.shape[0]
    nq = S // HD

    def body(f_ref, b_ref, dc_ref, df_ref, db_ref, run):
        i = pl.program_id(0)

        @pl.when(i == 0)
        def _():
            run[...] = jnp.zeros_like(run)

        dc = dc_ref[...]
        tri = (_iota2((HD, HD), 0) <= _iota2((HD, HD), 1)).astype(BF16)
        dlf = _dot_ones_left(tri, dc) + run[...]
        run[...] += jnp.sum(dc, axis=0, keepdims=True)
        x = f_ref[...] + b_ref[...]
        df = dlf * (1.0 / (1.0 + jnp.exp(x)))
        df_ref[...] = df
        db = jnp.sum(df, axis=0, keepdims=True)

        @pl.when(i == 0)
        def _():
            db_ref[...] = db

        @pl.when(i > 0)
        def _():
            db_ref[...] += db

    rev = pl.BlockSpec((HD, 128), lambda i: (nq - 1 - i, 0))
    vec = pl.BlockSpec((1, 128), lambda i: (0, 0))
    return pl.pallas_call(
        body,
        grid=(nq,),
        in_specs=[rev, vec, rev],
        out_specs=[rev, vec],
        out_shape=[jax.ShapeDtypeStruct((S, 128), F32), jax.ShapeDtypeStruct((1, 128), F32)],
        scratch_shapes=[pltpu.VMEM((1, 128), F32)],
        compiler_params=_cp(("arbitrary",)),
        name=name,
    )(f, b, dc)


def _fox_logits(q, ks, ct, cs, row0, off, masked):
    s = _dot_nt(q, ks) * (HD ** -0.5) + (ct - cs)
    if not masked:
        return s, None
    mask = _strip_mask(s.shape, row0, off, False)
    return jnp.where(mask, s, -1e30), mask


def fox_fwd(p, ccol, crow, cat, W, name):
    S = p.shape[0]
    TQ, TK = _query_rows(S), _key_strip(S)
    nh, nq = W // HD, S // TQ

    def body(q_ref, k_ref, v_ref, cc_ref, cr_ref, cat_ref, o_ref, lse_ref):
        i = pl.program_id(1)
        q = q_ref[...]
        ct = cc_ref[0]

        def step(g, carry, masked):
            m, l, acc = carry
            off = pl.multiple_of(g * TK, TK)
            s, _ = _fox_logits(q, k_ref[pl.ds(off, TK), :], ct, cr_ref[0, pl.ds(g, 1), :], i * TQ, off, masked)
            m_new = jnp.maximum(m, jnp.max(s, axis=1, keepdims=True))
            alpha = jnp.exp(m - m_new)
            pr = jnp.exp(s - m_new)
            l = alpha * l + jnp.sum(pr, axis=1, keepdims=True)
            acc = alpha * acc + _dot(pr.astype(BF16), v_ref[pl.ds(off, TK), :])
            return m_new, l, acc

        init = (jnp.full((TQ, 1), -1e30, F32), jnp.zeros((TQ, 1), F32), jnp.zeros((TQ, HD), F32))
        last = (i * TQ + TQ - 1) // TK
        m, l, acc = step(last, lax.fori_loop(0, last, lambda g, c: step(g, c, False), init), True)
        o_ref[...] = (acc / l).astype(o_ref.dtype)
        lse_ref[0] = m + jnp.log(l)

    return pl.pallas_call(
        body,
        grid=(nh, nq),
        in_specs=[
            pl.BlockSpec((TQ, HD), lambda h, i: (i, 2 * nh + h)),
            pl.BlockSpec((S, HD), lambda h, i: (0, 3 * nh + h)),
            pl.BlockSpec((S, HD), lambda h, i: (0, 4 * nh + h)),
            pl.BlockSpec((1, TQ, 1), lambda h, i: (h, i, 0)),
            pl.BlockSpec((1, S // TK, TK), lambda h, i: (h, 0, 0)),
            pl.BlockSpec(memory_space=pl.ANY),
        ],
        out_specs=[pl.BlockSpec((TQ, HD), lambda h, i: (i, nh + h)), pl.BlockSpec((1, TQ, 1), lambda h, i: (h, i, 0))],
        out_shape=[jax.ShapeDtypeStruct(cat.shape, cat.dtype), jax.ShapeDtypeStruct((nh, S, 1), F32)],
        input_output_aliases={5: 0},
        compiler_params=_cp(("parallel", "arbitrary")),
        name=name,
    )(p, p, p, ccol, crow, cat)


def fox_bwd(p, ccol, crow, cat, lse, dcat, dp, W, name):
    S = p.shape[0]
    TQ, TK = _query_rows(S), _key_strip(S)
    nh, nq = W // HD, S // TQ
    scale = HD ** -0.5

    def body(q_ref, k_ref, v_ref, cc_ref, cr_ref, o_ref, lse_ref, do_ref, dp_in_ref, dp_ref, dcs_ref, dct_ref, dk_acc, dv_acc):
        i = pl.program_id(1)
        q = q_ref[...]
        do = do_ref[...]
        ct = cc_ref[0]
        lse_i = lse_ref[0]
        delta = jnp.sum(do.astype(F32) * o_ref[...].astype(F32), axis=1, keepdims=True)

        @pl.when(i == 0)
        def _():
            dk_acc[...] = jnp.zeros_like(dk_acc)
            dv_acc[...] = jnp.zeros_like(dv_acc)
            dcs_ref[...] = jnp.zeros_like(dcs_ref)

        def step(g, carry, masked):
            dq, dct = carry
            off = pl.multiple_of(g * TK, TK)
            ks = k_ref[pl.ds(off, TK), :]
            s, mask = _fox_logits(q, ks, ct, cr_ref[0, pl.ds(g, 1), :], i * TQ, off, masked)
            pr = jnp.where(mask, jnp.exp(s - lse_i), 0.0) if masked else jnp.exp(s - lse_i)
            ds = pr * (_dot_nt(do, v_ref[pl.ds(off, TK), :]) - delta)
            dv_acc[pl.ds(off, TK), :] += _dot_tn(pr.astype(BF16), do)
            dsb = (ds * scale).astype(BF16)
            dk_acc[pl.ds(off, TK), :] += _dot_tn(dsb, q)
            dcs_ref[0, pl.ds(g, 1), :] += jnp.sum(ds, axis=0, keepdims=True)
            return dq + _dot(dsb, ks), dct + jnp.sum(ds, axis=1, keepdims=True)

        last = (i * TQ + TQ - 1) // TK
        carry = lax.fori_loop(0, last, lambda g, c: step(g, c, False), (jnp.zeros((TQ, HD), F32), jnp.zeros((TQ, 1), F32)))
        dq, dct = step(last, carry, True)
        dp_ref[0, pl.ds(pl.multiple_of(i * TQ, TQ), TQ), :] = dq.astype(dp_ref.dtype)
        dct_ref[0] = dct

        @pl.when(i == nq - 1)
        def _():
            dp_ref[1] = dk_acc[...].astype(dp_ref.dtype)
            dp_ref[2] = dv_acc[...].astype(dp_ref.dtype)

    return pl.pallas_call(
        body,
        grid=(nh, nq),
        in_specs=[
            pl.BlockSpec((TQ, HD), lambda h, i: (i, 2 * nh + h)),
            pl.BlockSpec((S, HD), lambda h, i: (0, 3 * nh + h)),
            pl.BlockSpec((S, HD), lambda h, i: (0, 4 * nh + h)),
            pl.BlockSpec((1, TQ, 1), lambda h, i: (h, i, 0)),
            pl.BlockSpec((1, S // TK, TK), lambda h, i: (h, 0, 0)),
            pl.BlockSpec((TQ, HD), lambda h, i: (i, nh + h)),
            pl.BlockSpec((1, TQ, 1), lambda h, i: (h, i, 0)),
            pl.BlockSpec((TQ, HD), lambda h, i: (i, nh + h)),
            pl.BlockSpec(memory_space=pl.ANY),
        ],
        out_specs=[
            pl.BlockSpec((3, S, HD), lambda h, i: (1, 0, h)),
            pl.BlockSpec((1, S // TK, TK), lambda h, i: (h, 0, 0)),
            pl.BlockSpec((1, TQ, 1), lambda h, i: (h, i, 0)),
        ],
        out_shape=[
            jax.ShapeDtypeStruct(dp.shape, dp.dtype),
            jax.ShapeDtypeStruct((nh, S // TK, TK), F32),
            jax.ShapeDtypeStruct((nh, S, 1), F32),
        ],
        input_output_aliases={8: 0},
        scratch_shapes=[pltpu.VMEM((S, HD), F32), pltpu.VMEM((S, HD), F32)],
        compiler_params=_cp(("parallel", "arbitrary")),
        name=name,
    )(p, p, p, ccol, crow, cat, lse, dcat, dp)


_GELU_K = math.sqrt(2.0 / math.pi)
_GELU_C = 0.044715


def _gelu(x):
    return 0.5 * x * (1.0 + jnp.tanh(_GELU_K * (x + _GELU_C * x * x * x)))


def _gelu_grad(x):
    t = jnp.tanh(_GELU_K * (x + _GELU_C * x * x * x))
    return 0.5 * (1.0 + t) + 0.5 * x * (1.0 - t * t) * (_GELU_K * (1.0 + 3.0 * _GELU_C * x * x))


def _layernorm_parts(gv):
    xc = gv - jnp.mean(gv, axis=-1, keepdims=True)
    r = lax.rsqrt(jnp.mean(xc * xc, axis=-1, keepdims=True) + EPS)
    return xc * r, r


def sg_fwd(p, sg_w, sg_bt, sg_g, W, name):
    S = p.shape[0]
    G, nq = W // HD, S // HD

    def body(u_ref, v_ref, w_ref, bt_ref, g_ref, o_ref):
        xh, _ = _layernorm_parts(_gelu(v_ref[...].astype(F32)))
        vn = (xh * g_ref[...]).astype(BF16)
        tri = _iota2((HD, HD), 0) >= _iota2((HD, HD), 1)
        for gi in range(G):
            cols = slice(gi * HD, (gi + 1) * HD)
            wt = jnp.where(tri, w_ref[gi], 0.0).astype(BF16)
            mixed = _dot(wt, vn[:, cols]) + bt_ref[:, gi : gi + 1]
            o_ref[:, cols] = (_gelu(u_ref[:, cols].astype(F32)) * mixed).astype(o_ref.dtype)

    return pl.pallas_call(
        body,
        grid=(nq,),
        in_specs=[
            pl.BlockSpec((HD, W), lambda i: (i, 0)),
            pl.BlockSpec((HD, W), lambda i: (i, 1)),
            pl.BlockSpec((G, HD, HD), lambda i: (0, 0, 0)),
            pl.BlockSpec((HD, G), lambda i: (0, 0)),
            pl.BlockSpec((1, W), lambda i: (0, 0)),
        ],
        out_specs=pl.BlockSpec((HD, W), lambda i: (i, 0)),
        out_shape=jax.ShapeDtypeStruct((S, 2 * W), BF16),
        compiler_params=_cp(("parallel",)),
        name=name,
    )(p, p, sg_w, sg_bt, sg_g.reshape(1, W))


def sg_bwd(p, sg_w, sg_bt, sg_g, dcat, W, name):
    S = p.shape[0]
    G, nq = W // HD, S // HD

    def body(u_ref, v_ref, w_ref, bt_ref, g_ref, do_ref, dp_ref, dw_ref, dbt_ref, dg_ref, dvn_scr):
        i = pl.program_id(0)

        @pl.when(i == 0)
        def _():
            dw_ref[...] = jnp.zeros_like(dw_ref)
            dbt_ref[...] = jnp.zeros_like(dbt_ref)
            dg_ref[...] = jnp.zeros_like(dg_ref)

        v = v_ref[...].astype(F32)
        xh, r = _layernorm_parts(_gelu(v))
        gg = g_ref[...]
        vn = (xh * gg).astype(BF16)
        tri = _iota2((HD, HD), 0) >= _iota2((HD, HD), 1)
        for gi in range(G):
            cols = slice(gi * HD, (gi + 1) * HD)
            wt = jnp.where(tri, w_ref[gi], 0.0).astype(BF16)
            mixed = _dot(wt, vn[:, cols]) + bt_ref[:, gi : gi + 1]
            u = u_ref[:, cols].astype(F32)
            do = do_ref[:, cols].astype(F32)
            dp_ref[0, :, cols] = (do * mixed * _gelu_grad(u)).astype(dp_ref.dtype)
            dmix = do * _gelu(u)
            dmb = dmix.astype(BF16)
            dw_ref[gi] += jnp.where(tri, _dot_nt(dmb, vn[:, cols]), 0.0)
            dbt_ref[:, gi : gi + 1] += jnp.sum(dmix, axis=1, keepdims=True)
            dvn_scr[:, cols] = _dot_tn(wt, dmb)
        dvn = dvn_scr[...]
        dg_ref[...] += jnp.sum(dvn * xh, axis=0, keepdims=True)
        dxh = dvn * gg
        dgv = r * (dxh - jnp.mean(dxh, axis=-1, keepdims=True) - xh * jnp.mean(dxh * xh, axis=-1, keepdims=True))
        dp_ref[1] = (dgv * _gelu_grad(v)).astype(dp_ref.dtype)

    return pl.pallas_call(
        body,
        grid=(nq,),
        in_specs=[
            pl.BlockSpec((HD, W), lambda i: (i, 0)),
            pl.BlockSpec((HD, W), lambda i: (i, 1)),
            pl.BlockSpec((G, HD, HD), lambda i: (0, 0, 0)),
            pl.BlockSpec((HD, G), lambda i: (0, 0)),
            pl.BlockSpec((1, W), lambda i: (0, 0)),
            pl.BlockSpec((HD, W), lambda i: (i, 0)),
        ],
        out_specs=[
            pl.BlockSpec((2, HD, W), lambda i: (0, i, 0)),
            pl.BlockSpec((G, HD, HD), lambda i: (0, 0, 0)),
            pl.BlockSpec((HD, G), lambda i: (0, 0)),
            pl.BlockSpec((1, W), lambda i: (0, 0)),
        ],
        out_shape=[
            jax.ShapeDtypeStruct((6, S, W), BF16),
            jax.ShapeDtypeStruct((G, HD, HD), F32),
            jax.ShapeDtypeStruct((HD, G), F32),
            jax.ShapeDtypeStruct((1, W), F32),
        ],
        scratch_shapes=[pltpu.VMEM((HD, W), F32)],
        compiler_params=_cp(("arbitrary",)),
        name=name,
    )(p, p, sg_w, sg_bt, sg_g.reshape(1, W), dcat)


def local_step(x, target, wts, at, on_grad):
    S, D = x.shape
    W = D // 2
    nb, F = wts["nb"], wts["F"]
    g = {}

    def ffn_fwd(xin, l):
        h = rms_fwd(xin, wts[f"{l}_ffn_norm_g"], f"{l}_ffn_rms")
        u = mm_nn(h, wts[f"{l}_ffn_up"], nb, f"{l}_ffn_up_mm")
        act = ffn_act_fwd(u, wts[f"{l}_ffn_conv_w"], F, f"{l}_ffn_act")
        xout = mm_nn(act, wts[f"{l}_ffn_down"], 1, f"{l}_ffn_down_mm", out_dtype=F32, res=xin,
                     tm=_pick(S, (1024, 512, 256, 128)), tn=_pick(D, (512, 256, 128)), tk=F)
        return xout, (xin, h, u, act)

    def ffn_bwd(dxout, dxoutb, saved, l):
        xin, h, u, act = saved
        dact = mm_nt(dxoutb, wts[f"{l}_ffn_down"], 1, S, F, f"{l}_ffn_down_dx", tko=_pick(F, (512, 256, 128)), tn=D)
        dact = on_grad(f"{l}_ffn_down", mm_tn(act, dxoutb, 1, D, f"{l}_ffn_down_dw", tn=D), dact)
        du, dcw = ffn_act_bwd(u, wts[f"{l}_ffn_conv_w"], dact, F, f"{l}_ffn_act_bwd")
        g[f"{l}_ffn_conv_w"] = jnp.concatenate([dcw[0], dcw[1]], axis=1)
        du2 = du.reshape(2 * S, F)
        n = wts[f"{l}_ffn_up"].shape[1]
        tn = _pick(n, (1408, 1024, 768, 512, 256, 128))
        per_half = F // tn
        nt = n // tn

        def up_block(i, j, t):
            vb = j * nt + t
            return vb // per_half, vb % per_half

        tm = _pick(S, (1024, 512, 256, 128))

        def nt_map(i, j, t):
            half, cb = up_block(i, j, t)
            return (half * (S // tm) + i, cb)

        def tn_map(j, t):
            half, cb = up_block(0, j, t)
            return (half, cb)

        dh = mm_nt(du2, wts[f"{l}_ffn_up"], nb, S, D, f"{l}_ffn_up_dx", dy_maps=[nt_map], tm=tm, tko=D, tn=tn)
        dh = on_grad(f"{l}_ffn_up", mm_tn(h, du2, nb, n, f"{l}_ffn_up_dw", dy_maps=[tn_map], tko=_pick(D, (1024, 512, 256, 128)), tn=tn), dh)
        dxin, dxinb, dg = rms_bwd(xin, wts[f"{l}_ffn_norm_g"], dh, dxout, f"{l}_ffn_rms_bwd")
        g[f"{l}_ffn_norm_g"] = dg
        return dxin, dxinb

    h0 = rms_fwd(x, wts["l0_mix_norm_g"], "l0_mix_rms")
    p0 = mm_nn(h0, wts["l0_w_in"], nb, "l0_w_in_mm")
    cat0 = sb_fwd(p0, W, "l0_sb_fwd")
    cat0 = sc_fwd(p0, wts["l0_sc_conv_w"], cat0, W, "l0_sc_fwd")
    x1 = mm_nn(cat0, wts["l0_w_out"], 1, "l0_w_out_mm", out_dtype=F32, res=x, tm=S, tn=_pick(D, (512, 256, 128)))
    x2, ffn0_saved = ffn_fwd(x1, "l0")

    x2 = at("l1_w_in", x2, None)
    nh = W // HD
    h2 = rms_fwd(x2, wts["l1_mix_norm_g"], "l1_mix_rms")
    p1 = mm_nt(h2, wts["l1_w_in_t"], 1, S, 5 * W, "l1_w_in_mm", tn=D)
    f = mm_nt(h2, wts["l1_w_f_t"], 1, S, 128, "l1_w_f_mm", out_dtype=F32, tn=D)
    bf = jnp.zeros((1, 128), F32).at[0, :nh].set(wts["l1_fox_b_f"])
    c = fox_gate_fwd(f, bf, "l1_fox_gate")
    c_heads = c[:, :nh].T
    ccol = c_heads[:, :, None]
    crow = c_heads.reshape(nh, S // _key_strip(S), _key_strip(S))
    sg_bt = wts["l1_sg_b"].T
    cat1 = sg_fwd(p1, wts["l1_sg_w"], sg_bt, wts["l1_sg_norm_g"], W, "l1_sg_fwd")
    cat1, lse = fox_fwd(p1, ccol, crow, cat1, W, "l1_fox_fwd")
    x3 = mm_nn(cat1, wts["l1_w_out"], 1, "l1_w_out_mm", out_dtype=F32, res=x2, tm=S, tn=_pick(D, (512, 256, 128)))
    x4, ffn1_saved = ffn_fwd(x3, "l1")

    dx4, dx4b, dgf, loss = loss_head(x4, wts["final_norm_g"], target, "loss_head")
    dx4b = at("loss", dx4b, loss)
    g["final_norm_g"] = dgf

    dx3, dx3b = ffn_bwd(dx4, dx4b, ffn1_saved, "l1")
    dcat1 = mm_nt(dx3b, wts["l1_w_out"], 1, S, D, "l1_w_out_dx", tn=D)
    dcat1 = on_grad("l1_w_out", mm_tn(cat1, dx3b, 1, D, "l1_w_out_dw", tn=D), dcat1)
    dp1, dsgw, dsgbt, dsgg = sg_bwd(p1, wts["l1_sg_w"], sg_bt, wts["l1_sg_norm_g"], dcat1, W, "l1_sg_bwd")
    dp1, dcs, dct = fox_bwd(p1, ccol, crow, cat1, lse, dcat1, dp1, W, "l1_fox_bwd")
    g["l1_sg_w"], g["l1_sg_b"], g["l1_sg_norm_g"] = dsgw, dsgbt.T, dsgg
    dc = jnp.zeros((S, 128), F32).at[:, :nh].set((dct[:, :, 0] - dcs.reshape(nh, S)).T)
    df, dbf = fox_gate_bwd(f, bf, dc, "l1_fox_gate_bwd")
    g["l1_fox_b_f"] = dbf[0, :nh]
    dfb = df.astype(BF16)
    tk1 = _pick(W, (1024, 512, 256, 128))
    tx1 = _pick(W, (512, 256, 128))
    tm1 = _pick(S, (1024, 512, 256, 128))
    part_of = lambda pt: pt + pt // 2 - pt // 4

    def a_map1(i, k):
        return (part_of(k // (W // tk1)) * (S // tm1) + i, k % (W // tk1))

    def x_map1(ko):
        return (part_of(ko // (W // tx1)), ko % (W // tx1))

    dp1_2d = dp1.reshape(6 * S, W)
    dw_main = mm_tn(dp1_2d, h2, 1, D, "l1_w_in_dw", tko=tx1, tn=D, x_map=x_map1, x_shape=(S, 5 * W))
    dw_f = mm_tn(dfb, h2, 1, D, "l1_w_f_dw", tn=D)
    dh2 = mm_nn(dfb, wts["l1_w_f_t"], 1, "l1_w_f_dx", out_dtype=F32)
    dh2 = mm_nn(dp1_2d, wts["l1_w_in_t"], 1, "l1_w_in_dx", res=dh2, tm=tm1, tk=tk1, a_map=a_map1, a_shape=(S, 5 * W))
    dh2 = on_grad("l1_w_in", jnp.concatenate([dw_main, dw_f[:nh]], axis=0), dh2)
    dx2, dx2b, dg = rms_bwd(x2, wts["l1_mix_norm_g"], dh2, dx3, "l1_mix_rms_bwd")
    g["l1_mix_norm_g"] = dg

    dx1, dx1b = ffn_bwd(dx2, dx2b, ffn0_saved, "l0")
    dcat0 = mm_nt(dx1b, wts["l0_w_out"], 1, S, D, "l0_w_out_dx", tn=D)
    dcat0 = on_grad("l0_w_out", mm_tn(cat0, dx1b, 1, D, "l0_w_out_dw", tn=D), dcat0)
    dp0 = sb_bwd(p0, dcat0, W, "l0_sb_bwd")
    dp0, dscw = sc_bwd(p0, wts["l0_sc_conv_w"], dcat0, dp0, W, "l0_sc_bwd")
    g["l0_sc_conv_w"] = dscw
    dp0 = at("small_ready", dp0, g)
    n0 = wts["l0_w_in"].shape[1]
    td0 = math.gcd(n0, W)
    nd0 = n0 // td0
    tm0 = _pick(S, (1024, 512, 256, 128))
    per_part0 = W // td0

    def nt_maps0(k):
        def f(i, j, t):
            vb = j * nd0 + k
            return ((vb // per_part0) * (S // tm0) + i, vb % per_part0)
        return f

    def tn_maps0(k):
        def f(j, t):
            vb = j * nd0 + k
            return (vb // per_part0, vb % per_part0)
        return f

    dp0_2d = dp0.reshape(6 * S, W)
    dw0 = mm_tn(h0, dp0_2d, nb, n0, "l0_w_in_dw", dy_maps=[tn_maps0(k) for k in range(nd0)], tko=_pick(D, (1024, 512, 256, 128)), tn=n0)
    dp0_2d = on_grad("l0_w_in", dw0, dp0_2d)
    dp0_2d = on_grad(None, None, dp0_2d)
    dh0 = mm_nt(dp0_2d, wts["l0_w_in"], nb, S, D, "l0_w_in_dx", dy_maps=[nt_maps0(k) for k in range(nd0)], tm=tm0, tko=D, tn=n0)
    dh0 = at("small_done", dh0, None)
    dx0, _, dg = rms_bwd(x, wts["l0_mix_norm_g"], dh0, dx1, "l0_mix_rms_bwd")
    g["l0_mix_norm_g"] = dg
    return dx0, g


GATHER_ID = 1


def _place():
    return lax.axis_index("x"), lax.axis_index("y"), lax.axis_index("c")


def _other_chips(x, y):
    return [(x, 1 - y), (1 - x, y), (1 - x, 1 - y)]


def _handshake(peers):
    barrier = pltpu.get_barrier_semaphore()
    for peer in peers:
        pl.semaphore_signal(barrier, inc=1, device_id=peer, device_id_type=MESH)
    pl.semaphore_wait(barrier, len(peers))


UPDATE_LAG = 2


def _on_sequencer(body, out_type, scratch_types, collective_id, name):
    return pl.kernel(
        body,
        out_type=out_type,
        mesh=plsc.ScalarSubcoreMesh(axis_name="seq", num_cores=1),
        scratch_types=scratch_types,
        compiler_params=pltpu.CompilerParams(collective_id=collective_id),
        name=name,
    )


def all_gather(arrs, name):
    n = len(arrs)

    def body(*refs):
        xs, outs = refs[:n], refs[n : 2 * n]
        send_sems, recv_sems, local_sems = refs[2 * n :]
        x, y, c = _place()
        me, sibling = (x, y, c), (x, y, 1 - c)
        chips = _other_chips(x, y)
        _handshake([sibling] + [(*chip, c) for chip in chips])

        def copy(a, k, block, to, src=None):
            px, py, pc = block
            dst = outs[a].at[4 * px + 2 * py + pc]
            return pltpu.make_async_remote_copy(
                src_ref=dst if src is None else src, dst_ref=dst,
                send_sem=send_sems.at[7 * a + k], recv_sem=recv_sems.at[7 * a + k], device_id=to, device_id_type=MESH,
            )

        mine = [pltpu.make_async_copy(xs[a], outs[a].at[4 * x + 2 * y + c], local_sems.at[a]) for a in range(n)]
        for cp in mine:
            cp.start()
        first = []
        for a in range(n):
            first.append(copy(a, 0, me, sibling, src=xs[a]))
            first += [copy(a, 1 + j, me, (*chip, c), src=xs[a]) for j, chip in enumerate(chips)]
        for cp in first:
            cp.start()
        passed = []
        for a in range(n):
            for j, chip in enumerate(chips):
                copy(a, 1 + j, (*chip, c), me).wait_recv()
                cp = copy(a, 4 + j, (*chip, c), sibling)
                cp.start()
                passed.append(cp)
        for a in range(n):
            copy(a, 0, sibling, me).wait_recv()
            for j, chip in enumerate(chips):
                copy(a, 4 + j, (*chip, 1 - c), me).wait_recv()
        for cp in first + passed:
            cp.wait_send()
        for cp in mine:
            cp.wait()

    out_type = [jax.ShapeDtypeStruct((NDEV,) + a.shape, a.dtype) for a in arrs]
    sems = [pltpu.SemaphoreType.DMA((7 * n,)), pltpu.SemaphoreType.DMA((7 * n,)), pltpu.SemaphoreType.DMA((n,))]
    return _on_sequencer(body, out_type, sems, GATHER_ID, name)(*arrs)


_IN_HBM = pl.BlockSpec(memory_space=pltpu.HBM)
_IN_SEM = pl.BlockSpec(memory_space=pltpu.SEMAPHORE)
_EFFECT = pltpu.SideEffectType.DATAFLOW_SIDE_EFFECTING


def _split_start_many(jobs, name):
    nj = len(jobs)

    def body(*refs):
        ins, outs = refs[: 2 * nj], refs[2 * nj :]
        for q, job in enumerate(jobs):
            for cp in job[0](ins[2 * q], ins[2 * q + 1], outs[3 * q], outs[3 * q + 1]):
                cp.start()
        outs[-1][...] = jnp.zeros_like(outs[-1])

    out_shape, out_specs, operands, aliases = [], [], [], {}
    for q, (_, src, land_shape, nsem) in enumerate(jobs):
        out_shape += [pltpu.SemaphoreType.DMA((nsem,)), pltpu.SemaphoreType.DMA((nsem,)), pltpu.HBM(land_shape, src.dtype)]
        out_specs += [_IN_SEM, _IN_SEM, _IN_HBM]
        operands += [src, pltpu.with_memory_space_constraint(lax.empty(land_shape, src.dtype), pltpu.HBM)]
        aliases[2 * q + 1] = 3 * q + 2
    res = pl.pallas_call(
        body,
        name=name,
        out_shape=tuple(out_shape) + (jax.ShapeDtypeStruct((8, 128), F32),),
        in_specs=(_IN_HBM,) * (2 * nj),
        out_specs=tuple(out_specs) + (pl.BlockSpec(memory_space=pltpu.VMEM),),
        input_output_aliases=aliases,
        compiler_params=pltpu.CompilerParams(has_side_effects=_EFFECT),
    )(*operands)
    return [[res[3 * q], res[3 * q + 1], jobs[q][1], res[3 * q + 2]] for q in range(nj)], res[-1]


def _split_start(make_copies, src, land_shape, nsem, name):
    (flying,), token = _split_start_many([(make_copies, src, land_shape, nsem)], name)
    return (*flying, token)


def _split_wait(make_copies, send_sems, recv_sems, src_thru, land_thru, after, name):
    def body(src_ref, land_ref, send_sems, recv_sems, after_ref, land_out):
        for cp in make_copies(src_ref, land_ref, send_sems, recv_sems):
            cp.wait_send()
            cp.wait_recv()

    return pl.pallas_call(
        body,
        name=name,
        out_shape=pltpu.HBM(land_thru.shape, land_thru.dtype),
        in_specs=(_IN_HBM, _IN_HBM, _IN_SEM, _IN_SEM, pl.BlockSpec(memory_space=pl.ANY)),
        out_specs=_IN_HBM,
        input_output_aliases={1: 0},
        compiler_params=pltpu.CompilerParams(has_side_effects=_EFFECT),
    )(src_thru, land_thru, send_sems, recv_sems, after)


def _pair_copies(src_ref, land_ref, send_sems, recv_sems):
    x, y, c = _place()
    return [
        pltpu.make_async_remote_copy(
            src_ref=src_ref.at[k, 1 - c], dst_ref=land_ref.at[k],
            send_sem=send_sems.at[k], recv_sem=recv_sems.at[k], device_id=(x, y, 1 - c), device_id_type=MESH,
        )
        for k in range(4)
    ]


def _direct_copies(src_ref, land_ref, send_sems, recv_sems):
    x, y, c = _place()
    me = 4 * x + 2 * y + c
    copies = []
    for k in range(NDEV - 1):
        to = (me + k + 1) % NDEV
        copies.append(pltpu.make_async_remote_copy(
            src_ref=src_ref, dst_ref=land_ref.at[me], send_sem=send_sems.at[k], recv_sem=recv_sems.at[k],
            device_id=(to // 4, (to // 2) % 2, to % 2), device_id_type=MESH,
        ))
    return copies


def _chip_copies(src_ref, land_ref, send_sems, recv_sems):
    x, y, c = _place()
    return [
        pltpu.make_async_remote_copy(
            src_ref=src_ref.at[2 * px + py], dst_ref=land_ref.at[2 * x + y],
            send_sem=send_sems.at[j], recv_sem=recv_sems.at[j], device_id=(px, py, c), device_id_type=MESH,
        )
        for j, (px, py) in enumerate(_other_chips(x, y))
    ]


def _row_tile(R, C, max_elems):
    if R * C <= max_elems:
        return R
    best = None
    for tr in range(16, R, 16):
        if R % tr == 0 and tr * C <= max_elems:
            best = tr
    return best or R


def pair_sum(a42, land4, core, name):
    _, _, R, C = a42.shape
    tr = _row_tile(R, C, 1 << 20)

    def body(core_ref, a_ref, l_ref, o_ref):
        o_ref[...] = (a_ref[0].astype(F32) + l_ref[...].astype(F32)).astype(o_ref.dtype)

    return pl.pallas_call(
        body,
        grid_spec=pltpu.PrefetchScalarGridSpec(
            num_scalar_prefetch=1,
            grid=(4, R // tr),
            in_specs=[
                pl.BlockSpec((1, 1, tr, C), lambda k, r, core_ref: (k, core_ref[0], r, 0)),
                pl.BlockSpec((1, tr, C), lambda k, r, core_ref: (k, r, 0)),
            ],
            out_specs=pl.BlockSpec((1, tr, C), lambda k, r, core_ref: (k, r, 0)),
        ),
        out_shape=jax.ShapeDtypeStruct((4, R, C), BF16),
        compiler_params=_cp(("parallel", "parallel")),
        name=name,
    )(core, a42, land4)


def sum_slots(parts, name):
    P, R, C = parts.shape

    def body(p_ref, o_ref):
        acc = p_ref[0].astype(F32)
        for k in range(1, P):
            acc = acc + p_ref[k].astype(F32)
        o_ref[...] = acc

    tr = _row_tile(R, P * C, 1 << 21)
    return pl.pallas_call(
        body,
        grid=(R // tr,),
        in_specs=[pl.BlockSpec((P, tr, C), lambda r: (0, r, 0))],
        out_specs=pl.BlockSpec((tr, C), lambda r: (r, 0)),
        out_shape=jax.ShapeDtypeStruct((R, C), F32),
        compiler_params=_cp(("parallel",)),
        name=name,
    )(parts)


def adamw(w, m, v, parts, name):
    R, C = w.shape
    P = parts.shape[0]
    tr = _pick(R, (256, 128, 64, 32, 16, 8))
    c1 = 1.0 - ADAM_B1 ** ADAM_STEP
    c2 = 1.0 - ADAM_B2 ** ADAM_STEP

    def body(w_ref, m_ref, v_ref, p_ref, g_ref, d_ref, nm_ref, nv_ref):
        g = p_ref[0].astype(F32)
        for k in range(1, P):
            g = g + p_ref[k].astype(F32)
        nm = ADAM_B1 * m_ref[...] + (1.0 - ADAM_B1) * g
        nv = ADAM_B2 * v_ref[...] + (1.0 - ADAM_B2) * (g * g)
        g_ref[...] = g
        nm_ref[...] = nm
        nv_ref[...] = nv
        d_ref[...] = -ADAM_LR * ((nm / c1) / (jnp.sqrt(nv / c2) + ADAM_EPS) + ADAM_WD * w_ref[...])

    blk = pl.BlockSpec((tr, C), lambda r: (r, 0))
    shp = jax.ShapeDtypeStruct((R, C), F32)
    return pl.pallas_call(
        body,
        grid=(R // tr,),
        in_specs=[blk, blk, blk, pl.BlockSpec((P, tr, C), lambda r: (0, r, 0))],
        out_specs=[blk, blk, blk, blk],
        out_shape=[shp, shp, shp, shp],
        compiler_params=_cp(("parallel",)),
        name=name,
    )(w, m, v, parts)


def adamw_reduced(w, m, v, own, land, chip, name):
    R, C = w.shape
    if R % 8 == 0:
        tr, tc = _pick(R, (256, 128, 64, 32, 16, 8)), C
    else:
        tr, tc = R, _pick(C, (256, 128))
    c1 = 1.0 - ADAM_B1 ** ADAM_STEP
    c2 = 1.0 - ADAM_B2 ** ADAM_STEP

    def body(chip_ref, w_ref, m_ref, v_ref, own_ref, land_ref, g_ref, d_ref, nm_ref, nv_ref):
        mine = own_ref[0].astype(F32)
        g = None
        for k in range(4):
            term = jnp.where(chip_ref[0] == k, mine, land_ref[k].astype(F32))
            g = term if g is None else g + term
        nm = ADAM_B1 * m_ref[...] + (1.0 - ADAM_B1) * g
        nv = ADAM_B2 * v_ref[...] + (1.0 - ADAM_B2) * (g * g)
        g_ref[...] = g
        nm_ref[...] = nm
        nv_ref[...] = nv
        d_ref[...] = -ADAM_LR * ((nm / c1) / (jnp.sqrt(nv / c2) + ADAM_EPS) + ADAM_WD * w_ref[...])

    blk = pl.BlockSpec((tr, tc), lambda r, c, chip_ref: (r, c))
    shp = jax.ShapeDtypeStruct((R, C), F32)
    return pl.pallas_call(
        body,
        grid_spec=pltpu.PrefetchScalarGridSpec(
            num_scalar_prefetch=1,
            grid=(R // tr, C // tc),
            in_specs=[
                blk, blk, blk,
                pl.BlockSpec((1, tr, tc), lambda r, c, chip_ref: (chip_ref[0], r, c)),
                pl.BlockSpec((4, tr, tc), lambda r, c, chip_ref: (0, r, c)),
            ],
            out_specs=[blk, blk, blk, blk],
        ),
        out_shape=[shp, shp, shp, shp],
        compiler_params=_cp(("parallel", "parallel")),
        name=name,
    )(chip, w, m, v, own, land)


_WEIGHTS = [
    "l0_mix_norm_g", "l0_w_in", "l0_sc_conv_w", "l0_w_out", "l0_ffn_norm_g", "l0_ffn_up", "l0_ffn_conv_w", "l0_ffn_down",
    "l1_mix_norm_g", "l1_w_in", "l1_fox_b_f", "l1_sg_w", "l1_sg_b", "l1_sg_norm_g", "l1_w_out", "l1_ffn_norm_g",
    "l1_ffn_up", "l1_ffn_conv_w", "l1_ffn_down", "final_norm_g",
]
_ROW_SHARDED = ["l0_w_out", "l0_ffn_down", "l1_w_out", "l1_ffn_down"]
_BIG = ["l0_w_in", "l0_w_out", "l0_ffn_up", "l0_ffn_down", "l1_w_in", "l1_w_out", "l1_ffn_up", "l1_ffn_down"]
_CONV = ["l0_sc_conv_w", "l0_ffn_conv_w", "l1_ffn_conv_w"]
_SMALL = [n for n in _WEIGHTS if n not in _BIG]
_LAST_SMALL = "l0_mix_norm_g"
_PACK_ROWS = 8


def _pack(arrs):
    flat = []
    for a in arrs:
        v = a.reshape(-1).astype(F32)
        pad = (-v.shape[0]) % (_PACK_ROWS * 128)
        flat.append(jnp.pad(v, (0, pad)))
    return jnp.concatenate(flat).reshape(-1, 128)


def _unpack(packed, shapes):
    out, off = [], 0
    flat = packed.reshape(-1)
    for shp in shapes:
        size = math.prod(shp)
        out.append(flat[off : off + size].reshape(shp))
        off += size + (-size) % (_PACK_ROWS * 128)
    return out


def kernel(x, l0_mix_norm_g, l0_w_in, l0_sc_conv_w, l0_w_out, l0_ffn_norm_g, l0_ffn_up, l0_ffn_conv_w, l0_ffn_down, l1_mix_norm_g, l1_w_in, l1_fox_b_f, l1_sg_w, l1_sg_b, l1_sg_norm_g, l1_w_out, l1_ffn_norm_g, l1_ffn_up, l1_ffn_conv_w, l1_ffn_down, final_norm_g, loss_target, m_l0_mix_norm_g, m_l0_w_in, m_l0_sc_conv_w, m_l0_w_out, m_l0_ffn_norm_g, m_l0_ffn_up, m_l0_ffn_conv_w, m_l0_ffn_down, m_l1_mix_norm_g, m_l1_w_in, m_l1_fox_b_f, m_l1_sg_w, m_l1_sg_b, m_l1_sg_norm_g, m_l1_w_out, m_l1_ffn_norm_g, m_l1_ffn_up, m_l1_ffn_conv_w, m_l1_ffn_down, m_final_norm_g, v_l0_mix_norm_g, v_l0_w_in, v_l0_sc_conv_w, v_l0_w_out, v_l0_ffn_norm_g, v_l0_ffn_up, v_l0_ffn_conv_w, v_l0_ffn_down, v_l1_mix_norm_g, v_l1_w_in, v_l1_fox_b_f, v_l1_sg_w, v_l1_sg_b, v_l1_sg_norm_g, v_l1_w_out, v_l1_ffn_norm_g, v_l1_ffn_up, v_l1_ffn_conv_w, v_l1_ffn_down, v_final_norm_g):
    given = dict(locals())
    w = {n: given[n] for n in _WEIGHTS}
    mom = {n: given["m_" + n] for n in _WEIGHTS}
    var = {n: given["v_" + n] for n in _WEIGHTS}
    xs, target = x[0], loss_target[0]
    S, D = xs.shape
    W = D // 2
    nh = W // HD
    cx, cy, cc = _place()
    me = 4 * cx + 2 * cy + cc

    wts = {"nb": NDEV, "F": l0_ffn_down.shape[0] * NDEV}
    for n in _SMALL:
        if n not in _CONV:
            wts[n] = w[n]
    gathered, loss_sum = {}, []

    def start_gather(names):
        srcs = [(w[n].T if n == "l1_w_in" else w[n]).astype(BF16) for n in names]
        taps = [w[c] for c in _CONV] if names[0] == _BIG[0] else []
        got = all_gather(srcs + taps, "gather_" + "_".join(names))
        for n, full in zip(names, got):
            if n == "l1_w_in":
                gathered[n] = full
            elif n in _ROW_SHARDED:
                wts[n] = full.reshape(-1, D)
            else:
                wts[n] = full.reshape(NDEV * D, -1)
        for c, full in zip(_CONV, got[len(names):] if taps else []):
            wts[c] = full.transpose(1, 0, 2).reshape(CONV_K, -1)

    def at(point, after, value):
        if point == "l1_w_in":
            got, after = lax.optimization_barrier((gathered[point], after))
            wts["l1_w_in_t"] = got.reshape(-1, D)
            wts["l1_w_f_t"] = jnp.pad(wts["l1_w_in_t"][5 * W :], ((0, 128 - nh), (0, 0)))
        elif point == "loss":
            gathered["loss"] = value[0, :1]
        elif point == "small_ready":
            early = [n for n in _SMALL if n != _LAST_SMALL]
            gathered["small"] = all_gather([_pack([value[n] for n in early] + [gathered["loss"]])], "gather_small_grads")[0]
        elif point == "small_done":
            after = update_small([n for n in _SMALL if n != _LAST_SMALL], gathered["small"], "small", after, True)
        return after

    out_g, out_d, out_m, out_v = {}, {}, {}, {}

    def update_small(names, all_terms, tag, after=None, with_loss=False):
        shapes = [w[n].shape for n in names]
        full_shapes = [(CONV_K, NDEV * w[n].shape[1]) if n in _CONV else w[n].shape for n in names]
        summed = _unpack(sum_slots(all_terms, f"sum_{tag}_grads"), full_shapes + ([(1,)] if with_loss else []))
        if with_loss:
            loss_sum.append(summed[-1][0])
        grads = {}
        for n, t in zip(names, summed):
            if n in _CONV:
                cols = w[n].shape[1]
                t = lax.dynamic_slice_in_dim(t, me * cols, cols, axis=1)
            grads[n] = t
        res = adamw(
            _pack([w[n] for n in names]), _pack([mom[n] for n in names]), _pack([var[n] for n in names]),
            _pack([grads[n] for n in names])[None], f"adamw_{tag}",
        )
        if after is not None:
            res, after = lax.optimization_barrier((res, after))
        for dst, packed_out in zip((out_g, out_d, out_m, out_v), res):
            for n, t in zip(names, _unpack(packed_out, shapes)):
                dst[n] = t
        return after

    core = jnp.reshape(cc, (1,)).astype(jnp.int32)
    chip = jnp.reshape(2 * cx + cy, (1,)).astype(jnp.int32)
    pair_flying, chip_flying = [], []

    def tie(value, after):
        if after is None:
            return value, None
        return lax.optimization_barrier((value, after))

    def advance(after, new=None):
        jobs, names = [], []
        if pair_flying:
            n0, flying = pair_flying.pop()
            landed = _split_wait(_pair_copies, *flying, f"reduce_pair_wait_{n0}")
            summed = pair_sum(flying[2], landed, core, f"pair_sum_{n0}")
            jobs.append((_chip_copies, summed, summed.shape, 3))
            names.append(n0)
        if new is not None:
            jobs.append((_pair_copies, new[1], new[1].shape[:1] + new[1].shape[2:], 4))
            names.append(new[0])
        started, token = _split_start_many(jobs, "reduce_start_" + "_".join(names))
        token, after = tie(token, after)
        if new is not None:
            pair_flying.append((new[0], started.pop() + [token]))
        if started:
            chip_flying.append((names[0], started[0] + [token]))
        return after

    def update(after, behind=None):
        n, flying = chip_flying.pop(0)
        if behind is not None:
            flying[4], _ = lax.optimization_barrier((flying[4], behind))
        landed = _split_wait(_chip_copies, *flying, f"reduce_chips_wait_{n}")
        turn = (lambda t: t.T) if n == "l1_w_in" else (lambda t: t)
        res = adamw_reduced(turn(w[n]), turn(mom[n]), turn(var[n]), flying[2], landed, chip, f"adamw_{n}")
        res, after = tie(res, after)
        out_g[n], out_d[n], out_m[n], out_v[n] = [turn(t) for t in res]
        return after, res[0]

    def on_grad(n, term, after):
        if n is None:
            return advance(after)
        if n in _ROW_SHARDED or n == "l1_w_in":
            term = term.reshape(NDEV, -1, D)
        else:
            term = term.reshape(NDEV, D, -1)
        term = term.reshape((4, 2) + term.shape[1:])
        after = advance(after, (n, term))
        if len(chip_flying) > UPDATE_LAG:
            after, _ = update(after)
        return after

    for n in _BIG:
        start_gather([n])
    dx, g = local_step(xs, target, wts, at, on_grad)
    last = _pack([g[_LAST_SMALL]])
    *flying, done = _split_start(_direct_copies, last, (NDEV,) + last.shape, NDEV - 1, "gather_last_grad")
    while len(chip_flying) > 1:
        _, done = update(None, behind=done)
    landed = _split_wait(_direct_copies, *flying, done, "gather_last_grad_wait")
    update_small([_LAST_SMALL], lax.dynamic_update_slice(landed, last[None], (me, 0, 0)), "last")
    update(None, behind=out_g[_LAST_SMALL])
    loss = loss_sum[0]

    return (loss, dx[None], *[out_g[n] for n in _WEIGHTS], *[out_d[n] for n in _WEIGHTS],
            *[out_m[n] for n in _WEIGHTS], *[out_v[n] for n in _WEIGHTS])
```

```python
import functools
import math

import jax
import jax.numpy as jnp
from jax import lax
from jax.experimental import pallas as pl
from jax.experimental.pallas import tpu as pltpu
from jax.experimental.pallas import tpu_sc as plsc

F32 = jnp.float32
BF16 = jnp.bfloat16
HD = 128
EPS = 1e-6
CONV_K = 3
VMEM_LIMIT_BYTES = 48 << 20
NDEV = 8
MESH = pl.DeviceIdType.MESH

ADAM_LR = 0.001
ADAM_B1 = 0.9
ADAM_B2 = 0.999
ADAM_EPS = 1e-08
ADAM_WD = 0.01
ADAM_STEP = 10


def _cp(sem):
    return pltpu.CompilerParams(dimension_semantics=sem, vmem_limit_bytes=VMEM_LIMIT_BYTES)


def _pick(n, prefs):
    for p in prefs:
        if n % p == 0:
            return p
    return n


def _dot(a, b):
    return jnp.dot(a, b, preferred_element_type=F32)


def _dot_nt(a, b):
    return lax.dot_general(a, b, (((1,), (1,)), ((), ())), preferred_element_type=F32)


def _dot_tn(a, b):
    return lax.dot_general(a, b, (((0,), (0,)), ((), ())), preferred_element_type=F32)


def _split3(x):
    hi = x.astype(BF16)
    r = x - hi.astype(F32)
    mid = r.astype(BF16)
    lo = (r - mid.astype(F32)).astype(BF16)
    return hi, mid, lo


def _dot_ones_left(ones_bf16, x):
    hi, mid, lo = _split3(x)
    return _dot(ones_bf16, hi) + _dot(ones_bf16, mid) + _dot(ones_bf16, lo)


def _iota2(shape, axis):
    return lax.broadcasted_iota(jnp.int32, shape, axis)


def mm_nn(a, w2d, nb, name, out_dtype=BF16, res=None, tm=None, tn=None, tk=None, a_map=None, a_shape=None):
    M, K = a_shape or a.shape
    n = w2d.shape[1]
    assert w2d.shape[0] == nb * K or (nb == 1 and w2d.shape[0] > K)
    a_map = a_map or (lambda i, k: (i, k))
    tm = tm or _pick(M, (1024, 512, 256, 128))
    tn = tn or _pick(n, (1408, 1024, 768, 512, 256, 128))
    tk = tk or (K if K <= 2048 else _pick(K, (1408, 1024, 512, 256, 128)))
    nk, nt = K // tk, n // tn
    has_res = res is not None

    def body(*refs):
        if has_res:
            a_ref, w_ref, r_ref, o_ref = refs[:4]
        else:
            a_ref, w_ref, o_ref = refs[:3]
            r_ref = None
        part = _dot(a_ref[...], w_ref[...])

        def finish(acc):
            if r_ref is not None:
                acc = acc + r_ref[...].astype(F32)
            o_ref[...] = acc.astype(o_ref.dtype)

        if nk == 1:
            finish(part)
        else:
            acc_ref = refs[-1]
            k = pl.program_id(3)

            @pl.when(k == 0)
            def _():
                acc_ref[...] = part

            @pl.when(k > 0)
            def _():
                acc_ref[...] += part

            @pl.when(k == nk - 1)
            def _():
                finish(acc_ref[...])

    in_specs = [
        pl.BlockSpec((tm, tk), lambda i, j, t, k: a_map(i, k)),
        pl.BlockSpec((tk, tn), lambda i, j, t, k: (j * nk + k, t)),
    ]
    args = [a, w2d]
    out_spec = pl.BlockSpec((tm, tn), lambda i, j, t, k: (i, j * nt + t))
    if has_res:
        in_specs.append(out_spec)
        args.append(res)
    return pl.pallas_call(
        body,
        grid=(M // tm, nb, nt, nk),
        in_specs=in_specs,
        out_specs=out_spec,
        out_shape=jax.ShapeDtypeStruct((M, nb * n), out_dtype),
        scratch_shapes=[pltpu.VMEM((tm, tn), F32)] if nk > 1 else [],
        compiler_params=_cp(("parallel", "parallel", "parallel", "arbitrary")),
        name=name,
    )(*args)


def mm_nt(dy2d, w2d, nb, M, K, name, out_dtype=BF16, res=None, dy_maps=None, tm=None, tko=None, tn=None):
    n = w2d.shape[1]
    assert w2d.shape[0] == nb * K or (nb == 1 and w2d.shape[0] > K)
    tm = tm or _pick(M, (1024, 512, 256, 128))
    tko = tko or _pick(K, (1024, 512, 256, 128))
    tn = tn or _pick(n, (1408, 1024, 768, 512, 256, 128))
    nt, nko = n // tn, K // tko
    has_res = res is not None
    if dy_maps is None:
        dy_maps = [lambda i, j, t: (i, j * nt + t)]
    nd = len(dy_maps)
    td = tn // nd

    one_step = nb * nt == 1

    def body(*refs):
        d_refs, w_ref = refs[:nd], refs[nd]
        r_ref = refs[nd + 1] if has_res else None
        d = d_refs[0][...] if nd == 1 else jnp.concatenate([r[...] for r in d_refs], axis=1)
        part = _dot_nt(d, w_ref[...])
        if one_step:
            o_ref = refs[-1]
            if r_ref is not None:
                part = part + r_ref[...].astype(F32)
            o_ref[...] = part.astype(o_ref.dtype)
            return
        o_ref, acc_ref = refs[-2], refs[-1]
        j, t = pl.program_id(2), pl.program_id(3)
        first = jnp.logical_and(j == 0, t == 0)
        last = jnp.logical_and(j == nb - 1, t == nt - 1)

        @pl.when(first)
        def _():
            acc_ref[...] = part

        @pl.when(jnp.logical_not(first))
        def _():
            acc_ref[...] += part

        @pl.when(last)
        def _():
            acc = acc_ref[...]
            if r_ref is not None:
                acc = acc + r_ref[...].astype(F32)
            o_ref[...] = acc.astype(o_ref.dtype)

    in_specs = [pl.BlockSpec((tm, td), functools.partial(lambda f, i, ko, j, t: f(i, j, t), f)) for f in dy_maps]
    in_specs.append(pl.BlockSpec((tko, tn), lambda i, ko, j, t: (j * nko + ko, t)))
    args = [dy2d] * nd + [w2d]
    out_spec = pl.BlockSpec((tm, tko), lambda i, ko, j, t: (i, ko))
    if has_res:
        in_specs.append(out_spec)
        args.append(res)
    return pl.pallas_call(
        body,
        grid=(M // tm, nko, nb, nt),
        in_specs=in_specs,
        out_specs=out_spec,
        out_shape=jax.ShapeDtypeStruct((M, K), out_dtype),
        scratch_shapes=[] if one_step else [pltpu.VMEM((tm, tko), F32)],
        compiler_params=_cp(("parallel", "parallel", "arbitrary", "arbitrary")),
        name=name,
    )(*args)


def mm_tn(x, dy2d, nb, n, name, out_dtype=BF16, dy_maps=None, tko=None, tn=None, x_map=None, x_shape=None):
    S, K = x_shape or x.shape
    x_map = x_map or (lambda ko: (0, ko))
    tko = tko or _pick(K, (512, 256, 128))
    tn = tn or _pick(n, (1408, 1024, 768, 512, 256, 128))
    nt, nko = n // tn, K // tko
    if dy_maps is None:
        dy_maps = [lambda j, t: (0, j * nt + t)]
    nd = len(dy_maps)
    td = tn // nd

    def body(*refs):
        x_ref, d_refs, o_ref = refs[0], refs[1 : 1 + nd], refs[-1]
        d = d_refs[0][...] if nd == 1 else jnp.concatenate([r[...] for r in d_refs], axis=1)
        o_ref[...] = _dot_tn(x_ref[...], d).astype(o_ref.dtype)

    in_specs = [pl.BlockSpec((S, tko), lambda ko, j, t: x_map(ko))]
    in_specs += [pl.BlockSpec((S, td), functools.partial(lambda f, ko, j, t: f(j, t), f)) for f in dy_maps]
    return pl.pallas_call(
        body,
        grid=(nko, nb, nt),
        in_specs=in_specs,
        out_specs=pl.BlockSpec((tko, tn), lambda ko, j, t: (j * nko + ko, t)),
        out_shape=jax.ShapeDtypeStruct((nb * K, n), out_dtype),
        compiler_params=_cp(("parallel", "parallel", "parallel")),
        name=name,
    )(x, *([dy2d] * nd))


def rms_fwd(x, g, name):
    S, D = x.shape
    tm = _pick(S, (512, 256, 128))

    def body(x_ref, g_ref, o_ref):
        xf = x_ref[...]
        r = lax.rsqrt(jnp.mean(xf * xf, axis=-1, keepdims=True) + EPS)
        o_ref[...] = (xf * r * g_ref[...]).astype(o_ref.dtype)

    return pl.pallas_call(
        body,
        grid=(S // tm,),
        in_specs=[pl.BlockSpec((tm, D), lambda i: (i, 0)), pl.BlockSpec((1, D), lambda i: (0, 0))],
        out_specs=pl.BlockSpec((tm, D), lambda i: (i, 0)),
        out_shape=jax.ShapeDtypeStruct((S, D), BF16),
        compiler_params=_cp(("parallel",)),
        name=name,
    )(x, g.reshape(1, D))


def rms_bwd(x, g, dh, dres, name):
    S, D = x.shape
    tm = _pick(S, (256, 128))

    def body(x_ref, g_ref, dh_ref, dr_ref, dx_ref, dxb_ref, dg_ref):
        i = pl.program_id(0)
        xf = x_ref[...]
        dh = dh_ref[...].astype(F32)
        r = lax.rsqrt(jnp.mean(xf * xf, axis=-1, keepdims=True) + EPS)
        gy = dh * g_ref[...]
        proj = jnp.mean(gy * xf, axis=-1, keepdims=True)
        dx = dr_ref[...] + r * gy - xf * (r * r * r * proj)
        dx_ref[...] = dx
        dxb_ref[...] = dx.astype(BF16)
        dg = jnp.sum(dh * (xf * r), axis=0, keepdims=True)

        @pl.when(i == 0)
        def _():
            dg_ref[...] = dg

        @pl.when(i > 0)
        def _():
            dg_ref[...] += dg

    row = pl.BlockSpec((tm, D), lambda i: (i, 0))
    vec = pl.BlockSpec((1, D), lambda i: (0, 0))
    return pl.pallas_call(
        body,
        grid=(S // tm,),
        in_specs=[row, vec, row, row],
        out_specs=[row, row, vec],
        out_shape=[jax.ShapeDtypeStruct((S, D), F32), jax.ShapeDtypeStruct((S, D), BF16), jax.ShapeDtypeStruct((1, D), F32)],
        compiler_params=_cp(("arbitrary",)),
        name=name,
    )(x, g.reshape(1, D), dh, dres)


def loss_head(x, g, target, name):
    S, D = x.shape
    tm = _pick(S, (256, 128))

    def body(x_ref, g_ref, t_ref, dx_ref, dxb_ref, dg_ref, loss_ref):
        i = pl.program_id(0)
        xf = x_ref[...]
        gg = g_ref[...]
        r = lax.rsqrt(jnp.mean(xf * xf, axis=-1, keepdims=True) + EPS)
        xh = xf * r
        err = xh * gg - t_ref[...]
        part = (0.5 / D) * jnp.sum(err * err)
        dy = err * (1.0 / D)
        gy = dy * gg
        proj = jnp.mean(gy * xf, axis=-1, keepdims=True)
        dx = r * gy - xf * (r * r * r * proj)
        dx_ref[...] = dx
        dxb_ref[...] = dx.astype(BF16)
        dg = jnp.sum(dy * xh, axis=0, keepdims=True)
        lossb = jnp.full(loss_ref.shape, part, F32)

        @pl.when(i == 0)
        def _():
            dg_ref[...] = dg
            loss_ref[...] = lossb

        @pl.when(i > 0)
        def _():
            dg_ref[...] += dg
            loss_ref[...] += lossb

    row = pl.BlockSpec((tm, D), lambda i: (i, 0))
    vec = pl.BlockSpec((1, D), lambda i: (0, 0))
    return pl.pallas_call(
        body,
        grid=(S // tm,),
        in_specs=[row, vec, row],
        out_specs=[row, row, vec, pl.BlockSpec((8, 128), lambda i: (0, 0))],
        out_shape=[
            jax.ShapeDtypeStruct((S, D), F32),
            jax.ShapeDtypeStruct((S, D), BF16),
            jax.ShapeDtypeStruct((1, D), F32),
            jax.ShapeDtypeStruct((8, 128), F32),
        ],
        compiler_params=_cp(("arbitrary",)),
        name=name,
    )(x, g.reshape(1, D), target)


def _shift_down(s, k):
    if k == 0:
        return s
    return jnp.where(_iota2(s.shape, 0) >= k, pltpu.roll(s, k, axis=0), 0.0)


def _shift_up(s, k):
    if k == 0:
        return s
    n = s.shape[0]
    return jnp.where(_iota2(s.shape, 0) < n - k, pltpu.roll(s, n - k, axis=0), 0.0)


def _conv(s, w):
    return w[0:1] * _shift_down(s, 2) + w[1:2] * _shift_down(s, 1) + w[2:3] * s


def _conv_t(d, w):
    return w[2:3] * d + w[1:2] * _shift_up(d, 1) + w[0:1] * _shift_up(d, 2)


def _conv_dw(d, s):
    return [jnp.sum(d * _shift_down(s, CONV_K - 1 - k), axis=0, keepdims=True) for k in range(CONV_K)]


def sc_fwd(p, convw, cat, W, name):
    S = p.shape[0]
    tc = _pick(W, (256, 128))
    nc = W // tc

    def body(gb_ref, gc_ref, hi_ref, w_ref, cat_ref, o_ref):
        s = gc_ref[...].astype(F32) * hi_ref[...].astype(F32)
        o_ref[...] = (gb_ref[...].astype(F32) * _conv(s, w_ref[...])).astype(o_ref.dtype)

    col = lambda part: pl.BlockSpec((S, tc), lambda c: (0, part * nc + c))
    return pl.pallas_call(
        body,
        grid=(nc,),
        in_specs=[col(3), col(4), col(5), pl.BlockSpec((CONV_K, tc), lambda c: (0, c)), pl.BlockSpec(memory_space=pl.ANY)],
        out_specs=col(1),
        out_shape=jax.ShapeDtypeStruct(cat.shape, cat.dtype),
        input_output_aliases={4: 0},
        compiler_params=_cp(("parallel",)),
        name=name,
    )(p, p, p, convw, cat)


def sc_bwd(p, convw, dcat, dp, W, name):
    S = p.shape[0]
    tc = _pick(W, (256, 128))
    nc = W // tc

    def body(gb_ref, gc_ref, hi_ref, w_ref, do_ref, dp_in_ref, dp_ref, dw_ref):
        gb = gb_ref[...].astype(F32)
        gc = gc_ref[...].astype(F32)
        hi = hi_ref[...].astype(F32)
        w = w_ref[...]
        do = do_ref[...].astype(F32)
        s = gc * hi
        dcs = do * gb
        ds = _conv_t(dcs, w)
        dp_ref[0] = (do * _conv(s, w)).astype(dp_ref.dtype)
        dp_ref[1] = (ds * hi).astype(dp_ref.dtype)
        dp_ref[2] = (ds * gc).astype(dp_ref.dtype)
        for k, row in enumerate(_conv_dw(dcs, s)):
            dw_ref[k : k + 1, :] = row

    col = lambda part: pl.BlockSpec((S, tc), lambda c: (0, part * nc + c))
    return pl.pallas_call(
        body,
        grid=(nc,),
        in_specs=[
            col(3), col(4), col(5),
            pl.BlockSpec((CONV_K, tc), lambda c: (0, c)),
            pl.BlockSpec((S, tc), lambda c: (0, nc + c)),
            pl.BlockSpec(memory_space=pl.ANY),
        ],
        out_specs=[pl.BlockSpec((3, S, tc), lambda c: (1, 0, c)), pl.BlockSpec((CONV_K, tc), lambda c: (0, c))],
        out_shape=[jax.ShapeDtypeStruct(dp.shape, dp.dtype), jax.ShapeDtypeStruct((CONV_K, W), F32)],
        input_output_aliases={5: 0},
        compiler_params=_cp(("parallel",)),
        name=name,
    )(p, p, p, convw, dcat, dp)


def _silu_parts(a):
    sig = 1.0 / (1.0 + jnp.exp(-a))
    return a * sig, sig


def ffn_act_fwd(u, convw, F, name):
    S = u.shape[0]
    tc = _pick(F, (256, 128))
    nc = F // tc

    def body(ug_ref, uu_ref, wg_ref, wu_ref, o_ref):
        ag = _conv(ug_ref[...].astype(F32), wg_ref[...])
        au = _conv(uu_ref[...].astype(F32), wu_ref[...])
        o_ref[...] = (_silu_parts(ag)[0] * au).astype(o_ref.dtype)

    col = lambda half: pl.BlockSpec((S, tc), lambda c: (0, half * nc + c))
    wcol = lambda half: pl.BlockSpec((CONV_K, tc), lambda c: (0, half * nc + c))
    return pl.pallas_call(
        body,
        grid=(nc,),
        in_specs=[col(0), col(1), wcol(0), wcol(1)],
        out_specs=pl.BlockSpec((S, tc), lambda c: (0, c)),
        out_shape=jax.ShapeDtypeStruct((S, F), BF16),
        compiler_params=_cp(("parallel",)),
        name=name,
    )(u, u, convw, convw)


def ffn_act_bwd(u, convw, dact, F, name):
    S = u.shape[0]
    tc = _pick(F, (256, 128))
    nc = F // tc
    CH = _pick(S, (256, 128))
    HALO = 16
    nch = S // CH

    def body(ug_ref, uu_ref, wg_ref, wu_ref, da_ref, du_ref, dw_ref, acc_ref):
        wg = wg_ref[...]
        wu = wu_ref[...]
        acc_ref[...] = jnp.zeros_like(acc_ref)

        def chunk(r0, first, last):
            lo, hi = (0 if first else HALO), (0 if last else HALO)
            n = CH + lo + hi
            rows = pl.ds(pl.multiple_of(r0 - lo, HALO), n)
            row = _iota2((n, tc), 0)

            def back(s, k):
                r = pltpu.roll(s, k, axis=0)
                return jnp.where(row >= k, r, 0.0) if first else r

            def ahead(s, k):
                r = pltpu.roll(s, n - k, axis=0)
                return jnp.where(row < n - k, r, 0.0) if last else r

            ug = ug_ref[rows, :].astype(F32)
            uu = uu_ref[rows, :].astype(F32)
            da = da_ref[rows, :].astype(F32)
            ug1, ug2, uu1, uu2 = back(ug, 1), back(ug, 2), back(uu, 1), back(uu, 2)
            ag = wg[0:1] * ug2 + wg[1:2] * ug1 + wg[2:3] * ug
            au = wu[0:1] * uu2 + wu[1:2] * uu1 + wu[2:3] * uu
            sl, sig = _silu_parts(ag)
            dag = da * au * (sig * (1.0 + ag * (1.0 - sig)))
            dau = da * sl
            dug = wg[2:3] * dag + wg[1:2] * ahead(dag, 1) + wg[0:1] * ahead(dag, 2)
            duu = wu[2:3] * dau + wu[1:2] * ahead(dau, 1) + wu[0:1] * ahead(dau, 2)
            out = pl.ds(pl.multiple_of(r0, HALO), CH)
            du_ref[0, out, :] = dug[lo : lo + CH].astype(du_ref.dtype)
            du_ref[1, out, :] = duu[lo : lo + CH].astype(du_ref.dtype)
            for k, (sg_, su_) in enumerate(((ug2, uu2), (ug1, uu1), (ug, uu))):
                acc_ref[k : k + 1, :] += jnp.sum((dag * sg_)[lo : lo + CH], axis=0, keepdims=True)
                acc_ref[CONV_K + k : CONV_K + k + 1, :] += jnp.sum((dau * su_)[lo : lo + CH], axis=0, keepdims=True)

        chunk(0, True, nch == 1)
        if nch > 2:
            def middle(c, carry):
                chunk(c * CH, False, False)
                return carry

            lax.fori_loop(1, nch - 1, middle, 0)
        if nch > 1:
            chunk((nch - 1) * CH, False, True)
        dw_ref[0] = acc_ref[0:CONV_K, :]
        dw_ref[1] = acc_ref[CONV_K : 2 * CONV_K, :]

    col = lambda half: pl.BlockSpec((S, tc), lambda c: (0, half * nc + c))
    wcol = lambda half: pl.BlockSpec((CONV_K, tc), lambda c: (0, half * nc + c))
    return pl.pallas_call(
        body,
        grid=(nc,),
        in_specs=[col(0), col(1), wcol(0), wcol(1), pl.BlockSpec((S, tc), lambda c: (0, c))],
        out_specs=[pl.BlockSpec((2, S, tc), lambda c: (0, 0, c)), pl.BlockSpec((2, CONV_K, tc), lambda c: (0, 0, c))],
        out_shape=[jax.ShapeDtypeStruct((2, S, F), BF16), jax.ShapeDtypeStruct((2, CONV_K, F), F32)],
        scratch_shapes=[pltpu.VMEM((8, tc), F32)],
        compiler_params=_cp(("parallel",)),
        name=name,
    )(u, u, convw, convw, dact)


def _softplus(z):
    return jnp.maximum(z, 0.0) + jnp.log(1.0 + jnp.exp(-jnp.abs(z)))


def _key_strip(S):
    return _pick(S, (512, 256, 128))


def _query_rows(S):
    tq = _pick(S, (512, 256, 128))
    assert _key_strip(S) % tq == 0
    return tq


def _split2(x):
    hi = x.astype(BF16)
    return hi, (x - hi.astype(F32)).astype(BF16)


def _block_sums(x, ones_bf16):
    hi, lo = _split2(x)
    return [
        _dot(hi[:, b * HD : (b + 1) * HD], ones_bf16) + _dot(lo[:, b * HD : (b + 1) * HD], ones_bf16)
        for b in range(x.shape[1] // HD)
    ]


def _strip_mask(shape, row0, off, strict):
    cols, rows = _iota2(shape, 1) + off, _iota2(shape, 0) + row0
    return cols < rows if strict else cols <= rows


def _sb_strip(q, ks, row0, off, run, su, masked):
    z = _dot_nt(q, ks) * (HD ** -0.5)
    sp = _softplus(z)
    mask = _strip_mask(z.shape, row0, off, True) if masked else None
    l = jnp.where(mask, -sp, 0.0) if masked else -sp
    within = _block_sums(l, su)
    later = [None] * len(within)
    for b in reversed(range(len(within))):
        later[b] = within[b] + run
        run = run + jnp.sum(l[:, b * HD : (b + 1) * HD], axis=1, keepdims=True)
    a = jnp.exp(z - sp + jnp.concatenate(later, axis=1))
    return z, (jnp.where(mask, a, 0.0) if masked else a), run


def sb_fwd(p, W, name):
    S = p.shape[0]
    TQ, TK = _query_rows(S), _key_strip(S)
    nh, nq = W // HD, S // TQ

    def body(q_ref, k_ref, v_ref, o_ref):
        i = pl.program_id(1)
        q = q_ref[...]
        su = (_iota2((HD, HD), 0) > _iota2((HD, HD), 1)).astype(BF16)
        last = (i * TQ + TQ - 1) // TK

        def strip(g, carry, masked):
            acc, run = carry
            off = pl.multiple_of(g * TK, TK)
            _, a, run = _sb_strip(q, k_ref[pl.ds(off, TK), :], i * TQ, off, run, su, masked)
            return acc + _dot(a.astype(BF16), v_ref[pl.ds(off, TK), :]), run

        carry = strip(last, (jnp.zeros((TQ, HD), F32), jnp.zeros((TQ, 1), F32)), True)
        acc, _ = lax.fori_loop(0, last, lambda gg, c: strip(last - 1 - gg, c, False), carry)
        o_ref[...] = acc.astype(o_ref.dtype)

    return pl.pallas_call(
        body,
        grid=(nh, nq),
        in_specs=[
            pl.BlockSpec((TQ, HD), lambda h, i: (i, h)),
            pl.BlockSpec((S, HD), lambda h, i: (0, nh + h)),
            pl.BlockSpec((S, HD), lambda h, i: (0, 2 * nh + h)),
        ],
        out_specs=pl.BlockSpec((TQ, HD), lambda h, i: (i, h)),
        out_shape=jax.ShapeDtypeStruct((S, 2 * W), BF16),
        compiler_params=_cp(("parallel", "arbitrary")),
        name=name,
    )(p, p, p)


def sb_bwd(p, dcat, W, name):
    S = p.shape[0]
    TQ, TK = _query_rows(S), _key_strip(S)
    nh, nq = W // HD, S // TQ
    scale = HD ** -0.5

    def body(q_ref, k_ref, v_ref, do_ref, dp_ref, dk_acc, dv_acc, e_scr, z_scr):
        i = pl.program_id(1)
        q = q_ref[...]
        do = do_ref[...]
        su = (_iota2((HD, HD), 0) > _iota2((HD, HD), 1)).astype(BF16)
        sl = (_iota2((HD, HD), 0) < _iota2((HD, HD), 1)).astype(BF16)
        last = (i * TQ + TQ - 1) // TK

        @pl.when(i == 0)
        def _():
            dk_acc[...] = jnp.zeros_like(dk_acc)
            dv_acc[...] = jnp.zeros_like(dv_acc)

        def pass_a(g, run, masked):
            off = pl.multiple_of(g * TK, TK)
            z, a, run = _sb_strip(q, k_ref[pl.ds(off, TK), :], i * TQ, off, run, su, masked)
            e_scr[g] = a * _dot_nt(do, v_ref[pl.ds(off, TK), :])
            z_scr[g] = z
            dv_acc[pl.ds(off, TK), :] += _dot_tn(a.astype(BF16), do)
            return run

        run = pass_a(last, jnp.zeros((TQ, 1), F32), True)
        lax.fori_loop(0, last, lambda gg, r: pass_a(last - 1 - gg, r, False), run)

        def pass_b(g, carry, masked):
            dq, run_e = carry
            off = pl.multiple_of(g * TK, TK)
            e = e_scr[g]
            z = z_scr[g]
            within = _block_sums(e, sl)
            before = []
            for b in range(len(within)):
                before.append(within[b] + run_e)
                run_e = run_e + jnp.sum(e[:, b * HD : (b + 1) * HD], axis=1, keepdims=True)
            sig = 1.0 / (1.0 + jnp.exp(-z))
            dz = e * (1.0 - sig) - jnp.concatenate(before, axis=1) * sig
            if masked:
                dz = jnp.where(_strip_mask(z.shape, i * TQ, off, True), dz, 0.0)
            dz = (dz * scale).astype(BF16)
            dq = dq + _dot(dz, k_ref[pl.ds(off, TK), :])
            dk_acc[pl.ds(off, TK), :] += _dot_tn(dz, q)
            return dq, run_e

        carry = lax.fori_loop(0, last, lambda g, c: pass_b(g, c, False), (jnp.zeros((TQ, HD), F32), jnp.zeros((TQ, 1), F32)))
        dq, _ = pass_b(last, carry, True)
        dp_ref[0, pl.ds(pl.multiple_of(i * TQ, TQ), TQ), :] = dq.astype(dp_ref.dtype)

        @pl.when(i == nq - 1)
        def _():
            dp_ref[1] = dk_acc[...].astype(dp_ref.dtype)
            dp_ref[2] = dv_acc[...].astype(dp_ref.dtype)

    return pl.pallas_call(
        body,
        grid=(nh, nq),
        in_specs=[
            pl.BlockSpec((TQ, HD), lambda h, i: (i, h)),
            pl.BlockSpec((S, HD), lambda h, i: (0, nh + h)),
            pl.BlockSpec((S, HD), lambda h, i: (0, 2 * nh + h)),
            pl.BlockSpec((TQ, HD), lambda h, i: (i, h)),
        ],
        out_specs=pl.BlockSpec((3, S, HD), lambda h, i: (0, 0, h)),
        out_shape=jax.ShapeDtypeStruct((6, S, W), BF16),
        scratch_shapes=[
            pltpu.VMEM((S, HD), F32),
            pltpu.VMEM((S, HD), F32),
            pltpu.VMEM((S // TK, TQ, TK), F32),
            pltpu.VMEM((S // TK, TQ, TK), F32),
        ],
        compiler_params=_cp(("parallel", "arbitrary")),
        name=name,
    )(p, p, p, dcat)


def fox_gate_fwd(f, b, name):
    S = f.shape[0]
    nq = S // HD

    def body(f_ref, b_ref, c_ref, run):
        i = pl.program_id(0)

        @pl.when(i == 0)
        def _():
            run[...] = jnp.zeros_like(run)

        lf = -_softplus(-(f_ref[...] + b_ref[...]))
        tri = (_iota2((HD, HD), 0) >= _iota2((HD, HD), 1)).astype(BF16)
        c_ref[...] = _dot_ones_left(tri, lf) + run[...]
        run[...] += jnp.sum(lf, axis=0, keepdims=True)

    return pl.pallas_call(
        body,
        grid=(nq,),
        in_specs=[pl.BlockSpec((HD, 128), lambda i: (i, 0)), pl.BlockSpec((1, 128), lambda i: (0, 0))],
        out_specs=pl.BlockSpec((HD, 128), lambda i: (i, 0)),
        out_shape=jax.ShapeDtypeStruct((S, 128), F32),
        scratch_shapes=[pltpu.VMEM((1, 128), F32)],
        compiler_params=_cp(("arbitrary",)),
        name=name,
    )(f, b)


def fox_gate_bwd(f, b, dc, name):
    S = f.shape[0]
    nq = S // HD

    def body(f_ref, b_ref, dc_ref, df_ref, db_ref, run):
        i = pl.program_id(0)

        @pl.when(i == 0)
        def _():
            run[...] = jnp.zeros_like(run)

        dc = dc_ref[...]
        tri = (_iota2((HD, HD), 0) <= _iota2((HD, HD), 1)).astype(BF16)
        dlf = _dot_ones_left(tri, dc) + run[...]
        run[...] += jnp.sum(dc, axis=0, keepdims=True)
        x = f_ref[...] + b_ref[...]
        df = dlf * (1.0 / (1.0 + jnp.exp(x)))
        df_ref[...] = df
        db = jnp.sum(df, axis=0, keepdims=True)

        @pl.when(i == 0)
        def _():
            db_ref[...] = db

        @pl.when(i > 0)
        def _():
            db_ref[...] += db

    rev = pl.BlockSpec((HD, 128), lambda i: (nq - 1 - i, 0))
    vec = pl.BlockSpec((1, 128), lambda i: (0, 0))
    return pl.pallas_call(
        body,
        grid=(nq,),
        in_specs=[rev, vec, rev],
        out_specs=[rev, vec],
        out_shape=[jax.ShapeDtypeStruct((S, 128), F32), jax.ShapeDtypeStruct((1, 128), F32)],
        scratch_shapes=[pltpu.VMEM((1, 128), F32)],
        compiler_params=_cp(("arbitrary",)),
        name=name,
    )(f, b, dc)


def _fox_logits(q, ks, ct, cs, row0, off, masked):
    s = _dot_nt(q, ks) * (HD ** -0.5) + (ct - cs)
    if not masked:
        return s, None
    mask = _strip_mask(s.shape, row0, off, False)
    return jnp.where(mask, s, -1e30), mask


def fox_fwd(p, ccol, crow, cat, W, name):
    S = p.shape[0]
    TQ, TK = _query_rows(S), _key_strip(S)
    nh, nq = W // HD, S // TQ

    def body(q_ref, k_ref, v_ref, cc_ref, cr_ref, cat_ref, o_ref, lse_ref):
        i = pl.program_id(1)
        q = q_ref[...]
        ct = cc_ref[0]

        def step(g, carry, masked):
            m, l, acc = carry
            off = pl.multiple_of(g * TK, TK)
            s, _ = _fox_logits(q, k_ref[pl.ds(off, TK), :], ct, cr_ref[0, pl.ds(g, 1), :], i * TQ, off, masked)
            m_new = jnp.maximum(m, jnp.max(s, axis=1, keepdims=True))
            alpha = jnp.exp(m - m_new)
            pr = jnp.exp(s - m_new)
            l = alpha * l + jnp.sum(pr, axis=1, keepdims=True)
            acc = alpha * acc + _dot(pr.astype(BF16), v_ref[pl.ds(off, TK), :])
            return m_new, l, acc

        init = (jnp.full((TQ, 1), -1e30, F32), jnp.zeros((TQ, 1), F32), jnp.zeros((TQ, HD), F32))
        last = (i * TQ + TQ - 1) // TK
        m, l, acc = step(last, lax.fori_loop(0, last, lambda g, c: step(g, c, False), init), True)
        o_ref[...] = (acc / l).astype(o_ref.dtype)
        lse_ref[0] = m + jnp.log(l)

    return pl.pallas_call(
        body,
        grid=(nh, nq),
        in_specs=[
            pl.BlockSpec((TQ, HD), lambda h, i: (i, 2 * nh + h)),
            pl.BlockSpec((S, HD), lambda h, i: (0, 3 * nh + h)),
            pl.BlockSpec((S, HD), lambda h, i: (0, 4 * nh + h)),
            pl.BlockSpec((1, TQ, 1), lambda h, i: (h, i, 0)),
            pl.BlockSpec((1, S // TK, TK), lambda h, i: (h, 0, 0)),
            pl.BlockSpec(memory_space=pl.ANY),
        ],
        out_specs=[pl.BlockSpec((TQ, HD), lambda h, i: (i, nh + h)), pl.BlockSpec((1, TQ, 1), lambda h, i: (h, i, 0))],
        out_shape=[jax.ShapeDtypeStruct(cat.shape, cat.dtype), jax.ShapeDtypeStruct((nh, S, 1), F32)],
        input_output_aliases={5: 0},
        compiler_params=_cp(("parallel", "arbitrary")),
        name=name,
    )(p, p, p, ccol, crow, cat)


def fox_bwd(p, ccol, crow, cat, lse, dcat, dp, W, name):
    S = p.shape[0]
    TQ, TK = _query_rows(S), _key_strip(S)
    nh, nq = W // HD, S // TQ
    scale = HD ** -0.5

    def body(q_ref, k_ref, v_ref, cc_ref, cr_ref, o_ref, lse_ref, do_ref, dp_in_ref, dp_ref, dcs_ref, dct_ref, dk_acc, dv_acc):
        i = pl.program_id(1)
        q = q_ref[...]
        do = do_ref[...]
        ct = cc_ref[0]
        lse_i = lse_ref[0]
        delta = jnp.sum(do.astype(F32) * o_ref[...].astype(F32), axis=1, keepdims=True)

        @pl.when(i == 0)
        def _():
            dk_acc[...] = jnp.zeros_like(dk_acc)
            dv_acc[...] = jnp.zeros_like(dv_acc)
            dcs_ref[...] = jnp.zeros_like(dcs_ref)

        def step(g, carry, masked):
            dq, dct = carry
            off = pl.multiple_of(g * TK, TK)
            ks = k_ref[pl.ds(off, TK), :]
            s, mask = _fox_logits(q, ks, ct, cr_ref[0, pl.ds(g, 1), :], i * TQ, off, masked)
            pr = jnp.where(mask, jnp.exp(s - lse_i), 0.0) if masked else jnp.exp(s - lse_i)
            ds = pr * (_dot_nt(do, v_ref[pl.ds(off, TK), :]) - delta)
            dv_acc[pl.ds(off, TK), :] += _dot_tn(pr.astype(BF16), do)
            dsb = (ds * scale).astype(BF16)
            dk_acc[pl.ds(off, TK), :] += _dot_tn(dsb, q)
            dcs_ref[0, pl.ds(g, 1), :] += jnp.sum(ds, axis=0, keepdims=True)
            return dq + _dot(dsb, ks), dct + jnp.sum(ds, axis=1, keepdims=True)

        last = (i * TQ + TQ - 1) // TK
        carry = lax.fori_loop(0, last, lambda g, c: step(g, c, False), (jnp.zeros((TQ, HD), F32), jnp.zeros((TQ, 1), F32)))
        dq, dct = step(last, carry, True)
        dp_ref[0, pl.ds(pl.multiple_of(i * TQ, TQ), TQ), :] = dq.astype(dp_ref.dtype)
        dct_ref[0] = dct

        @pl.when(i == nq - 1)
        def _():
            dp_ref[1] = dk_acc[...].astype(dp_ref.dtype)
            dp_ref[2] = dv_acc[...].astype(dp_ref.dtype)

    return pl.pallas_call(
        body,
        grid=(nh, nq),
        in_specs=[
            pl.BlockSpec((TQ, HD), lambda h, i: (i, 2 * nh + h)),
            pl.BlockSpec((S, HD), lambda h, i: (0, 3 * nh + h)),
            pl.BlockSpec((S, HD), lambda h, i: (0, 4 * nh + h)),
            pl.BlockSpec((1, TQ, 1), lambda h, i: (h, i, 0)),
            pl.BlockSpec((1, S // TK, TK), lambda h, i: (h, 0, 0)),
            pl.BlockSpec((TQ, HD), lambda h, i: (i, nh + h)),
            pl.BlockSpec((1, TQ, 1), lambda h, i: (h, i, 0)),
            pl.BlockSpec((TQ, HD), lambda h, i: (i, nh + h)),
            pl.BlockSpec(memory_space=pl.ANY),
        ],
        out_specs=[
            pl.BlockSpec((3, S, HD), lambda h, i: (1, 0, h)),
            pl.BlockSpec((1, S // TK, TK), lambda h, i: (h, 0, 0)),
            pl.BlockSpec((1, TQ, 1), lambda h, i: (h, i, 0)),
        ],
        out_shape=[
            jax.ShapeDtypeStruct(dp.shape, dp.dtype),
            jax.ShapeDtypeStruct((nh, S // TK, TK), F32),
            jax.ShapeDtypeStruct((nh, S, 1), F32),
        ],
        input_output_aliases={8: 0},
        scratch_shapes=[pltpu.VMEM((S, HD), F32), pltpu.VMEM((S, HD), F32)],
        compiler_params=_cp(("parallel", "arbitrary")),
        name=name,
    )(p, p, p, ccol, crow, cat, lse, dcat, dp)


_GELU_K = math.sqrt(2.0 / math.pi)
_GELU_C = 0.044715


def _gelu(x):
    return 0.5 * x * (1.0 + jnp.tanh(_GELU_K * (x + _GELU_C * x * x * x)))


def _gelu_grad(x):
    t = jnp.tanh(_GELU_K * (x + _GELU_C * x * x * x))
    return 0.5 * (1.0 + t) + 0.5 * x * (1.0 - t * t) * (_GELU_K * (1.0 + 3.0 * _GELU_C * x * x))


def _layernorm_parts(gv):
    xc = gv - jnp.mean(gv, axis=-1, keepdims=True)
    r = lax.rsqrt(jnp.mean(xc * xc, axis=-1, keepdims=True) + EPS)
    return xc * r, r


def sg_fwd(p, sg_w, sg_bt, sg_g, W, name):
    S = p.shape[0]
    G, nq = W // HD, S // HD

    def body(u_ref, v_ref, w_ref, bt_ref, g_ref, o_ref):
        xh, _ = _layernorm_parts(_gelu(v_ref[...].astype(F32)))
        vn = (xh * g_ref[...]).astype(BF16)
        tri = _iota2((HD, HD), 0) >= _iota2((HD, HD), 1)
        for gi in range(G):
            cols = slice(gi * HD, (gi + 1) * HD)
            wt = jnp.where(tri, w_ref[gi], 0.0).astype(BF16)
            mixed = _dot(wt, vn[:, cols]) + bt_ref[:, gi : gi + 1]
            o_ref[:, cols] = (_gelu(u_ref[:, cols].astype(F32)) * mixed).astype(o_ref.dtype)

    return pl.pallas_call(
        body,
        grid=(nq,),
        in_specs=[
            pl.BlockSpec((HD, W), lambda i: (i, 0)),
            pl.BlockSpec((HD, W), lambda i: (i, 1)),
            pl.BlockSpec((G, HD, HD), lambda i: (0, 0, 0)),
            pl.BlockSpec((HD, G), lambda i: (0, 0)),
            pl.BlockSpec((1, W), lambda i: (0, 0)),
        ],
        out_specs=pl.BlockSpec((HD, W), lambda i: (i, 0)),
        out_shape=jax.ShapeDtypeStruct((S, 2 * W), BF16),
        compiler_params=_cp(("parallel",)),
        name=name,
    )(p, p, sg_w, sg_bt, sg_g.reshape(1, W))


def sg_bwd(p, sg_w, sg_bt, sg_g, dcat, W, name):
    S = p.shape[0]
    G, nq = W // HD, S // HD

    def body(u_ref, v_ref, w_ref, bt_ref, g_ref, do_ref, dp_ref, dw_ref, dbt_ref, dg_ref, dvn_scr):
        i = pl.program_id(0)

        @pl.when(i == 0)
        def _():
            dw_ref[...] = jnp.zeros_like(dw_ref)
            dbt_ref[...] = jnp.zeros_like(dbt_ref)
            dg_ref[...] = jnp.zeros_like(dg_ref)

        v = v_ref[...].astype(F32)
        xh, r = _layernorm_parts(_gelu(v))
        gg = g_ref[...]
        vn = (xh * gg).astype(BF16)
        tri = _iota2((HD, HD), 0) >= _iota2((HD, HD), 1)
        for gi in range(G):
            cols = slice(gi * HD, (gi + 1) * HD)
            wt = jnp.where(tri, w_ref[gi], 0.0).astype(BF16)
            mixed = _dot(wt, vn[:, cols]) + bt_ref[:, gi : gi + 1]
            u = u_ref[:, cols].astype(F32)
            do = do_ref[:, cols].astype(F32)
            dp_ref[0, :, cols] = (do * mixed * _gelu_grad(u)).astype(dp_ref.dtype)
            dmix = do * _gelu(u)
            dmb = dmix.astype(BF16)
            dw_ref[gi] += jnp.where(tri, _dot_nt(dmb, vn[:, cols]), 0.0)
            dbt_ref[:, gi : gi + 1] += jnp.sum(dmix, axis=1, keepdims=True)
            dvn_scr[:, cols] = _dot_tn(wt, dmb)
        dvn = dvn_scr[...]
        dg_ref[...] += jnp.sum(dvn * xh, axis=0, keepdims=True)
        dxh = dvn * gg
        dgv = r * (dxh - jnp.mean(dxh, axis=-1, keepdims=True) - xh * jnp.mean(dxh * xh, axis=-1, keepdims=True))
        dp_ref[1] = (dgv * _gelu_grad(v)).astype(dp_ref.dtype)

    return pl.pallas_call(
        body,
        grid=(nq,),
        in_specs=[
            pl.BlockSpec((HD, W), lambda i: (i, 0)),
            pl.BlockSpec((HD, W), lambda i: (i, 1)),
            pl.BlockSpec((G, HD, HD), lambda i: (0, 0, 0)),
            pl.BlockSpec((HD, G), lambda i: (0, 0)),
            pl.BlockSpec((1, W), lambda i: (0, 0)),
            pl.BlockSpec((HD, W), lambda i: (i, 0)),
        ],
        out_specs=[
            pl.BlockSpec((2, HD, W), lambda i: (0, i, 0)),
            pl.BlockSpec((G, HD, HD), lambda i: (0, 0, 0)),
            pl.BlockSpec((HD, G), lambda i: (0, 0)),
            pl.BlockSpec((1, W), lambda i: (0, 0)),
        ],
        out_shape=[
            jax.ShapeDtypeStruct((6, S, W), BF16),
            jax.ShapeDtypeStruct((G, HD, HD), F32),
            jax.ShapeDtypeStruct((HD, G), F32),
            jax.ShapeDtypeStruct((1, W), F32),
        ],
        scratch_shapes=[pltpu.VMEM((HD, W), F32)],
        compiler_params=_cp(("arbitrary",)),
        name=name,
    )(p, p, sg_w, sg_bt, sg_g.reshape(1, W), dcat)


def local_step(x, target, wts, at, on_grad):
    S, D = x.shape
    W = D // 2
    nb, F = wts["nb"], wts["F"]
    g = {}

    def ffn_fwd(xin, l):
        h = rms_fwd(xin, wts[f"{l}_ffn_norm_g"], f"{l}_ffn_rms")
        u = mm_nn(h, wts[f"{l}_ffn_up"], nb, f"{l}_ffn_up_mm")
        act = ffn_act_fwd(u, wts[f"{l}_ffn_conv_w"], F, f"{l}_ffn_act")
        xout = mm_nn(act, wts[f"{l}_ffn_down"], 1, f"{l}_ffn_down_mm", out_dtype=F32, res=xin,
                     tm=_pick(S, (1024, 512, 256, 128)), tn=_pick(D, (512, 256, 128)), tk=F)
        return xout, (xin, h, u, act)

    def ffn_bwd(dxout, dxoutb, saved, l):
        xin, h, u, act = saved
        dact = mm_nt(dxoutb, wts[f"{l}_ffn_down"], 1, S, F, f"{l}_ffn_down_dx", tko=_pick(F, (512, 256, 128)), tn=D)
        dact = on_grad(f"{l}_ffn_down", mm_tn(act, dxoutb, 1, D, f"{l}_ffn_down_dw", tn=D), dact)
        du, dcw = ffn_act_bwd(u, wts[f"{l}_ffn_conv_w"], dact, F, f"{l}_ffn_act_bwd")
        g[f"{l}_ffn_conv_w"] = jnp.concatenate([dcw[0], dcw[1]], axis=1)
        du2 = du.reshape(2 * S, F)
        n = wts[f"{l}_ffn_up"].shape[1]
        tn = _pick(n, (1408, 1024, 768, 512, 256, 128))
        per_half = F // tn
        nt = n // tn

        def up_block(i, j, t):
            vb = j * nt + t
            return vb // per_half, vb % per_half

        tm = _pick(S, (1024, 512, 256, 128))

        def nt_map(i, j, t):
            half, cb = up_block(i, j, t)
            return (half * (S // tm) + i, cb)

        def tn_map(j, t):
            half, cb = up_block(0, j, t)
            return (half, cb)

        dh = mm_nt(du2, wts[f"{l}_ffn_up"], nb, S, D, f"{l}_ffn_up_dx", dy_maps=[nt_map], tm=tm, tko=D, tn=tn)
        dh = on_grad(f"{l}_ffn_up", mm_tn(h, du2, nb, n, f"{l}_ffn_up_dw", dy_maps=[tn_map], tko=_pick(D, (1024, 512, 256, 128)), tn=tn), dh)
        dxin, dxinb, dg = rms_bwd(xin, wts[f"{l}_ffn_norm_g"], dh, dxout, f"{l}_ffn_rms_bwd")
        g[f"{l}_ffn_norm_g"] = dg
        return dxin, dxinb

    h0 = rms_fwd(x, wts["l0_mix_norm_g"], "l0_mix_rms")
    p0 = mm_nn(h0, wts["l0_w_in"], nb, "l0_w_in_mm")
    cat0 = sb_fwd(p0, W, "l0_sb_fwd")
    cat0 = sc_fwd(p0, wts["l0_sc_conv_w"], cat0, W, "l0_sc_fwd")
    x1 = mm_nn(cat0, wts["l0_w_out"], 1, "l0_w_out_mm", out_dtype=F32, res=x, tm=S, tn=_pick(D, (512, 256, 128)))
    x2, ffn0_saved = ffn_fwd(x1, "l0")

    x2 = at("l1_w_in", x2, None)
    nh = W // HD
    h2 = rms_fwd(x2, wts["l1_mix_norm_g"], "l1_mix_rms")
    p1 = mm_nt(h2, wts["l1_w_in_t"], 1, S, 5 * W, "l1_w_in_mm", tn=D)
    f = mm_nt(h2, wts["l1_w_f_t"], 1, S, 128, "l1_w_f_mm", out_dtype=F32, tn=D)
    bf = jnp.zeros((1, 128), F32).at[0, :nh].set(wts["l1_fox_b_f"])
    c = fox_gate_fwd(f, bf, "l1_fox_gate")
    c_heads = c[:, :nh].T
    ccol = c_heads[:, :, None]
    crow = c_heads.reshape(nh, S // _key_strip(S), _key_strip(S))
    sg_bt = wts["l1_sg_b"].T
    cat1 = sg_fwd(p1, wts["l1_sg_w"], sg_bt, wts["l1_sg_norm_g"], W, "l1_sg_fwd")
    cat1, lse = fox_fwd(p1, ccol, crow, cat1, W, "l1_fox_fwd")
    x3 = mm_nn(cat1, wts["l1_w_out"], 1, "l1_w_out_mm", out_dtype=F32, res=x2, tm=S, tn=_pick(D, (512, 256, 128)))
    x4, ffn1_saved = ffn_fwd(x3, "l1")

    dx4, dx4b, dgf, loss = loss_head(x4, wts["final_norm_g"], target, "loss_head")
    dx4b = at("loss", dx4b, loss)
    g["final_norm_g"] = dgf

    dx3, dx3b = ffn_bwd(dx4, dx4b, ffn1_saved, "l1")
    dcat1 = mm_nt(dx3b, wts["l1_w_out"], 1, S, D, "l1_w_out_dx", tn=D)
    dcat1 = on_grad("l1_w_out", mm_tn(cat1, dx3b, 1, D, "l1_w_out_dw", tn=D), dcat1)
    dp1, dsgw, dsgbt, dsgg = sg_bwd(p1, wts["l1_sg_w"], sg_bt, wts["l1_sg_norm_g"], dcat1, W, "l1_sg_bwd")
    dp1, dcs, dct = fox_bwd(p1, ccol, crow, cat1, lse, dcat1, dp1, W, "l1_fox_bwd")
    g["l1_sg_w"], g["l1_sg_b"], g["l1_sg_norm_g"] = dsgw, dsgbt.T, dsgg
    dc = jnp.zeros((S, 128), F32).at[:, :nh].set((dct[:, :, 0] - dcs.reshape(nh, S)).T)
    df, dbf = fox_gate_bwd(f, bf, dc, "l1_fox_gate_bwd")
    g["l1_fox_b_f"] = dbf[0, :nh]
    dfb = df.astype(BF16)
    tk1 = _pick(W, (1024, 512, 256, 128))
    tx1 = _pick(W, (512, 256, 128))
    tm1 = _pick(S, (1024, 512, 256, 128))
    part_of = lambda pt: pt + pt // 2 - pt // 4

    def a_map1(i, k):
        return (part_of(k // (W // tk1)) * (S // tm1) + i, k % (W // tk1))

    def x_map1(ko):
        return (part_of(ko // (W // tx1)), ko % (W // tx1))

    dp1_2d = dp1.reshape(6 * S, W)
    dw_main = mm_tn(dp1_2d, h2, 1, D, "l1_w_in_dw", tko=tx1, tn=D, x_map=x_map1, x_shape=(S, 5 * W))
    dw_f = mm_tn(dfb, h2, 1, D, "l1_w_f_dw", tn=D)
    dh2 = mm_nn(dfb, wts["l1_w_f_t"], 1, "l1_w_f_dx", out_dtype=F32)
    dh2 = mm_nn(dp1_2d, wts["l1_w_in_t"], 1, "l1_w_in_dx", res=dh2, tm=tm1, tk=tk1, a_map=a_map1, a_shape=(S, 5 * W))
    dh2 = on_grad("l1_w_in", jnp.concatenate([dw_main, dw_f[:nh]], axis=0), dh2)
    dx2, dx2b, dg = rms_bwd(x2, wts["l1_mix_norm_g"], dh2, dx3, "l1_mix_rms_bwd")
    g["l1_mix_norm_g"] = dg

    dx1, dx1b = ffn_bwd(dx2, dx2b, ffn0_saved, "l0")
    dcat0 = mm_nt(dx1b, wts["l0_w_out"], 1, S, D, "l0_w_out_dx", tn=D)
    dcat0 = on_grad("l0_w_out", mm_tn(cat0, dx1b, 1, D, "l0_w_out_dw", tn=D), dcat0)
    dp0 = sb_bwd(p0, dcat0, W, "l0_sb_bwd")
    dp0, dscw = sc_bwd(p0, wts["l0_sc_conv_w"], dcat0, dp0, W, "l0_sc_bwd")
    g["l0_sc_conv_w"] = dscw
    dp0 = at("small_ready", dp0, g)
    n0 = wts["l0_w_in"].shape[1]
    td0 = math.gcd(n0, W)
    nd0 = n0 // td0
    tm0 = _pick(S, (1024, 512, 256, 128))
    per_part0 = W // td0

    def nt_maps0(k):
        def f(i, j, t):
            vb = j * nd0 + k
            return ((vb // per_part0) * (S // tm0) + i, vb % per_part0)
        return f

    def tn_maps0(k):
        def f(j, t):
            vb = j * nd0 + k
            return (vb // per_part0, vb % per_part0)
        return f

    dp0_2d = dp0.reshape(6 * S, W)
    dw0 = mm_tn(h0, dp0_2d, nb, n0, "l0_w_in_dw", dy_maps=[tn_maps0(k) for k in range(nd0)], tko=_pick(D, (1024, 512, 256, 128)), tn=n0)
    dp0_2d = on_grad("l0_w_in", dw0, dp0_2d)
    dp0_2d = on_grad(None, None, dp0_2d)
    dh0 = mm_nt(dp0_2d, wts["l0_w_in"], nb, S, D, "l0_w_in_dx", dy_maps=[nt_maps0(k) for k in range(nd0)], tm=tm0, tko=D, tn=n0)
    dh0 = at("small_done", dh0, None)
    dx0, _, dg = rms_bwd(x, wts["l0_mix_norm_g"], dh0, dx1, "l0_mix_rms_bwd")
    g["l0_mix_norm_g"] = dg
    return dx0, g


GATHER_ID = 1


def _place():
    return lax.axis_index("x"), lax.axis_index("y"), lax.axis_index("c")


def _other_chips(x, y):
    return [(x, 1 - y), (1 - x, y), (1 - x, 1 - y)]


def _handshake(peers):
    barrier = pltpu.get_barrier_semaphore()
    for peer in peers:
        pl.semaphore_signal(barrier, inc=1, device_id=peer, device_id_type=MESH)
    pl.semaphore_wait(barrier, len(peers))


UPDATE_LAG = 2


def _on_sequencer(body, out_type, scratch_types, collective_id, name):
    return pl.kernel(
        body,
        out_type=out_type,
        mesh=plsc.ScalarSubcoreMesh(axis_name="seq", num_cores=1),
        scratch_types=scratch_types,
        compiler_params=pltpu.CompilerParams(collective_id=collective_id),
        name=name,
    )


def all_gather(arrs, name):
    n = len(arrs)

    def body(*refs):
        xs, outs = refs[:n], refs[n : 2 * n]
        send_sems, recv_sems, local_sems = refs[2 * n :]
        x, y, c = _place()
        me, sibling = (x, y, c), (x, y, 1 - c)
        chips = _other_chips(x, y)
        _handshake([sibling] + [(*chip, c) for chip in chips])

        def copy(a, k, block, to, src=None):
            px, py, pc = block
            dst = outs[a].at[4 * px + 2 * py + pc]
            return pltpu.make_async_remote_copy(
                src_ref=dst if src is None else src, dst_ref=dst,
                send_sem=send_sems.at[7 * a + k], recv_sem=recv_sems.at[7 * a + k], device_id=to, device_id_type=MESH,
            )

        mine = [pltpu.make_async_copy(xs[a], outs[a].at[4 * x + 2 * y + c], local_sems.at[a]) for a in range(n)]
        for cp in mine:
            cp.start()
        first = []
        for a in range(n):
            first.append(copy(a, 0, me, sibling, src=xs[a]))
            first += [copy(a, 1 + j, me, (*chip, c), src=xs[a]) for j, chip in enumerate(chips)]
        for cp in first:
            cp.start()
        passed = []
        for a in range(n):
            for j, chip in enumerate(chips):
                copy(a, 1 + j, (*chip, c), me).wait_recv()
                cp = copy(a, 4 + j, (*chip, c), sibling)
                cp.start()
                passed.append(cp)
        for a in range(n):
            copy(a, 0, sibling, me).wait_recv()
            for j, chip in enumerate(chips):
                copy(a, 4 + j, (*chip, 1 - c), me).wait_recv()
        for cp in first + passed:
            cp.wait_send()
        for cp in mine:
            cp.wait()

    out_type = [jax.ShapeDtypeStruct((NDEV,) + a.shape, a.dtype) for a in arrs]
    sems = [pltpu.SemaphoreType.DMA((7 * n,)), pltpu.SemaphoreType.DMA((7 * n,)), pltpu.SemaphoreType.DMA((n,))]
    return _on_sequencer(body, out_type, sems, GATHER_ID, name)(*arrs)


_IN_HBM = pl.BlockSpec(memory_space=pltpu.HBM)
_IN_SEM = pl.BlockSpec(memory_space=pltpu.SEMAPHORE)
_EFFECT = pltpu.SideEffectType.DATAFLOW_SIDE_EFFECTING


def _split_start_many(jobs, name):
    nj = len(jobs)

    def body(*refs):
        ins, outs = refs[: 2 * nj], refs[2 * nj :]
        for q, job in enumerate(jobs):
            for cp in job[0](ins[2 * q], ins[2 * q + 1], outs[3 * q], outs[3 * q + 1]):
                cp.start()
        outs[-1][...] = jnp.zeros_like(outs[-1])

    out_shape, out_specs, operands, aliases = [], [], [], {}
    for q, (_, src, land_shape, nsem) in enumerate(jobs):
        out_shape += [pltpu.SemaphoreType.DMA((nsem,)), pltpu.SemaphoreType.DMA((nsem,)), pltpu.HBM(land_shape, src.dtype)]
        out_specs += [_IN_SEM, _IN_SEM, _IN_HBM]
        operands += [src, pltpu.with_memory_space_constraint(lax.empty(land_shape, src.dtype), pltpu.HBM)]
        aliases[2 * q + 1] = 3 * q + 2
    res = pl.pallas_call(
        body,
        name=name,
        out_shape=tuple(out_shape) + (jax.ShapeDtypeStruct((8, 128), F32),),
        in_specs=(_IN_HBM,) * (2 * nj),
        out_specs=tuple(out_specs) + (pl.BlockSpec(memory_space=pltpu.VMEM),),
        input_output_aliases=aliases,
        compiler_params=pltpu.CompilerParams(has_side_effects=_EFFECT),
    )(*operands)
    return [[res[3 * q], res[3 * q + 1], jobs[q][1], res[3 * q + 2]] for q in range(nj)], res[-1]


def _split_start(make_copies, src, land_shape, nsem, name):
    (flying,), token = _split_start_many([(make_copies, src, land_shape, nsem)], name)
    return (*flying, token)


def _split_wait(make_copies, send_sems, recv_sems, src_thru, land_thru, after, name):
    def body(src_ref, land_ref, send_sems, recv_sems, after_ref, land_out):
        for cp in make_copies(src_ref, land_ref, send_sems, recv_sems):
            cp.wait_send()
            cp.wait_recv()

    return pl.pallas_call(
        body,
        name=name,
        out_shape=pltpu.HBM(land_thru.shape, land_thru.dtype),
        in_specs=(_IN_HBM, _IN_HBM, _IN_SEM, _IN_SEM, pl.BlockSpec(memory_space=pl.ANY)),
        out_specs=_IN_HBM,
        input_output_aliases={1: 0},
        compiler_params=pltpu.CompilerParams(has_side_effects=_EFFECT),
    )(src_thru, land_thru, send_sems, recv_sems, after)


def _pair_copies(src_ref, land_ref, send_sems, recv_sems):
    x, y, c = _place()
    return [
        pltpu.make_async_remote_copy(
            src_ref=src_ref.at[k, 1 - c], dst_ref=land_ref.at[k],
            send_sem=send_sems.at[k], recv_sem=recv_sems.at[k], device_id=(x, y, 1 - c), device_id_type=MESH,
        )
        for k in range(4)
    ]


def _direct_copies(src_ref, land_ref, send_sems, recv_sems):
    x, y, c = _place()
    me = 4 * x + 2 * y + c
    copies = []
    for k in range(NDEV - 1):
        to = (me + k + 1) % NDEV
        copies.append(pltpu.make_async_remote_copy(
            src_ref=src_ref, dst_ref=land_ref.at[me], send_sem=send_sems.at[k], recv_sem=recv_sems.at[k],
            device_id=(to // 4, (to // 2) % 2, to % 2), device_id_type=MESH,
        ))
    return copies


def _chip_copies(src_ref, land_ref, send_sems, recv_sems):
    x, y, c = _place()
    return [
        pltpu.make_async_remote_copy(
            src_ref=src_ref.at[2 * px + py], dst_ref=land_ref.at[2 * x + y],
            send_sem=send_sems.at[j], recv_sem=recv_sems.at[j], device_id=(px, py, c), device_id_type=MESH,
        )
        for j, (px, py) in enumerate(_other_chips(x, y))
    ]


def _row_tile(R, C, max_elems):
    if R * C <= max_elems:
        return R
    best = None
    for tr in range(16, R, 16):
        if R % tr == 0 and tr * C <= max_elems:
            best = tr
    return best or R


def pair_sum(a42, land4, core, name):
    _, _, R, C = a42.shape
    tr = _row_tile(R, C, 1 << 20)

    def body(core_ref, a_ref, l_ref, o_ref):
        o_ref[...] = (a_ref[0].astype(F32) + l_ref[...].astype(F32)).astype(o_ref.dtype)

    return pl.pallas_call(
        body,
        grid_spec=pltpu.PrefetchScalarGridSpec(
            num_scalar_prefetch=1,
            grid=(4, R // tr),
            in_specs=[
                pl.BlockSpec((1, 1, tr, C), lambda k, r, core_ref: (k, core_ref[0], r, 0)),
                pl.BlockSpec((1, tr, C), lambda k, r, core_ref: (k, r, 0)),
            ],
            out_specs=pl.BlockSpec((1, tr, C), lambda k, r, core_ref: (k, r, 0)),
        ),
        out_shape=jax.ShapeDtypeStruct((4, R, C), BF16),
        compiler_params=_cp(("parallel", "parallel")),
        name=name,
    )(core, a42, land4)


def sum_slots(parts, name):
    P, R, C = parts.shape

    def body(p_ref, o_ref):
        acc = p_ref[0].astype(F32)
        for k in range(1, P):
            acc = acc + p_ref[k].astype(F32)
        o_ref[...] = acc

    tr = _row_tile(R, P * C, 1 << 21)
    return pl.pallas_call(
        body,
        grid=(R // tr,),
        in_specs=[pl.BlockSpec((P, tr, C), lambda r: (0, r, 0))],
        out_specs=pl.BlockSpec((tr, C), lambda r: (r, 0)),
        out_shape=jax.ShapeDtypeStruct((R, C), F32),
        compiler_params=_cp(("parallel",)),
        name=name,
    )(parts)


def adamw(w, m, v, parts, name):
    R, C = w.shape
    P = parts.shape[0]
    tr = _pick(R, (256, 128, 64, 32, 16, 8))
    c1 = 1.0 - ADAM_B1 ** ADAM_STEP
    c2 = 1.0 - ADAM_B2 ** ADAM_STEP

    def body(w_ref, m_ref, v_ref, p_ref, g_ref, d_ref, nm_ref, nv_ref):
        g = p_ref[0].astype(F32)
        for k in range(1, P):
            g = g + p_ref[k].astype(F32)
        nm = ADAM_B1 * m_ref[...] + (1.0 - ADAM_B1) * g
        nv = ADAM_B2 * v_ref[...] + (1.0 - ADAM_B2) * (g * g)
        g_ref[...] = g
        nm_ref[...] = nm
        nv_ref[...] = nv
        d_ref[...] = -ADAM_LR * ((nm / c1) / (jnp.sqrt(nv / c2) + ADAM_EPS) + ADAM_WD * w_ref[...])

    blk = pl.BlockSpec((tr, C), lambda r: (r, 0))
    shp = jax.ShapeDtypeStruct((R, C), F32)
    return pl.pallas_call(
        body,
        grid=(R // tr,),
        in_specs=[blk, blk, blk, pl.BlockSpec((P, tr, C), lambda r: (0, r, 0))],
        out_specs=[blk, blk, blk, blk],
        out_shape=[shp, shp, shp, shp],
        compiler_params=_cp(("parallel",)),
        name=name,
    )(w, m, v, parts)


def adamw_reduced(w, m, v, own, land, chip, name):
    R, C = w.shape
    if R % 8 == 0:
        tr, tc = _pick(R, (256, 128, 64, 32, 16, 8)), C
    else:
        tr, tc = R, _pick(C, (256, 128))
    c1 = 1.0 - ADAM_B1 ** ADAM_STEP
    c2 = 1.0 - ADAM_B2 ** ADAM_STEP

    def body(chip_ref, w_ref, m_ref, v_ref, own_ref, land_ref, g_ref, d_ref, nm_ref, nv_ref):
        mine = own_ref[0].astype(F32)
        g = None
        for k in range(4):
            term = jnp.where(chip_ref[0] == k, mine, land_ref[k].astype(F32))
            g = term if g is None else g + term
        nm = ADAM_B1 * m_ref[...] + (1.0 - ADAM_B1) * g
        nv = ADAM_B2 * v_ref[...] + (1.0 - ADAM_B2) * (g * g)
        g_ref[...] = g
        nm_ref[...] = nm
        nv_ref[...] = nv
        d_ref[...] = -ADAM_LR * ((nm / c1) / (jnp.sqrt(nv / c2) + ADAM_EPS) + ADAM_WD * w_ref[...])

    blk = pl.BlockSpec((tr, tc), lambda r, c, chip_ref: (r, c))
    shp = jax.ShapeDtypeStruct((R, C), F32)
    return pl.pallas_call(
        body,
        grid_spec=pltpu.PrefetchScalarGridSpec(
            num_scalar_prefetch=1,
            grid=(R // tr, C // tc),
            in_specs=[
                blk, blk, blk,
                pl.BlockSpec((1, tr, tc), lambda r, c, chip_ref: (chip_ref[0], r, c)),
                pl.BlockSpec((4, tr, tc), lambda r, c, chip_ref: (0, r, c)),
            ],
            out_specs=[blk, blk, blk, blk],
        ),
        out_shape=[shp, shp, shp, shp],
        compiler_params=_cp(("parallel", "parallel")),
        name=name,
    )(chip, w, m, v, own, land)


_WEIGHTS = [
    "l0_mix_norm_g", "l0_w_in", "l0_sc_conv_w", "l0_w_out", "l0_ffn_norm_g", "l0_ffn_up", "l0_ffn_conv_w", "l0_ffn_down",
    "l1_mix_norm_g", "l1_w_in", "l1_fox_b_f", "l1_sg_w", "l1_sg_b", "l1_sg_norm_g", "l1_w_out", "l1_ffn_norm_g",
    "l1_ffn_up", "l1_ffn_conv_w", "l1_ffn_down", "final_norm_g",
]
_ROW_SHARDED = ["l0_w_out", "l0_ffn_down", "l1_w_out", "l1_ffn_down"]
_BIG = ["l0_w_in", "l0_w_out", "l0_ffn_up", "l0_ffn_down", "l1_w_in", "l1_w_out", "l1_ffn_up", "l1_ffn_down"]
_CONV = ["l0_sc_conv_w", "l0_ffn_conv_w", "l1_ffn_conv_w"]
_SMALL = [n for n in _WEIGHTS if n not in _BIG]
_LAST_SMALL = "l0_mix_norm_g"
_PACK_ROWS = 8


def _pack(arrs):
    flat = []
    for a in arrs:
        v = a.reshape(-1).astype(F32)
        pad = (-v.shape[0]) % (_PACK_ROWS * 128)
        flat.append(jnp.pad(v, (0, pad)))
    return jnp.concatenate(flat).reshape(-1, 128)


def _unpack(packed, shapes):
    out, off = [], 0
    flat = packed.reshape(-1)
    for shp in shapes:
        size = math.prod(shp)
        out.append(flat[off : off + size].reshape(shp))
        off += size + (-size) % (_PACK_ROWS * 128)
    return out


def kernel(x, l0_mix_norm_g, l0_w_in, l0_sc_conv_w, l0_w_out, l0_ffn_norm_g, l0_ffn_up, l0_ffn_conv_w, l0_ffn_down, l1_mix_norm_g, l1_w_in, l1_fox_b_f, l1_sg_w, l1_sg_b, l1_sg_norm_g, l1_w_out, l1_ffn_norm_g, l1_ffn_up, l1_ffn_conv_w, l1_ffn_down, final_norm_g, loss_target, m_l0_mix_norm_g, m_l0_w_in, m_l0_sc_conv_w, m_l0_w_out, m_l0_ffn_norm_g, m_l0_ffn_up, m_l0_ffn_conv_w, m_l0_ffn_down, m_l1_mix_norm_g, m_l1_w_in, m_l1_fox_b_f, m_l1_sg_w, m_l1_sg_b, m_l1_sg_norm_g, m_l1_w_out, m_l1_ffn_norm_g, m_l1_ffn_up, m_l1_ffn_conv_w, m_l1_ffn_down, m_final_norm_g, v_l0_mix_norm_g, v_l0_w_in, v_l0_sc_conv_w, v_l0_w_out, v_l0_ffn_norm_g, v_l0_ffn_up, v_l0_ffn_conv_w, v_l0_ffn_down, v_l1_mix_norm_g, v_l1_w_in, v_l1_fox_b_f, v_l1_sg_w, v_l1_sg_b, v_l1_sg_norm_g, v_l1_w_out, v_l1_ffn_norm_g, v_l1_ffn_up, v_l1_ffn_conv_w, v_l1_ffn_down, v_final_norm_g):
    given = dict(locals())
    w = {n: given[n] for n in _WEIGHTS}
    mom = {n: given["m_" + n] for n in _WEIGHTS}
    var = {n: given["v_" + n] for n in _WEIGHTS}
    xs, target = x[0], loss_target[0]
    S, D = xs.shape
    W = D // 2
    nh = W // HD
    cx, cy, cc = _place()
    me = 4 * cx + 2 * cy + cc

    wts = {"nb": NDEV, "F": l0_ffn_down.shape[0] * NDEV}
    for n in _SMALL:
        if n not in _CONV:
            wts[n] = w[n]
    gathered, loss_sum = {}, []

    def start_gather(names):
        srcs = [(w[n].T if n == "l1_w_in" else w[n]).astype(BF16) for n in names]
        taps = [w[c] for c in _CONV] if names[0] == _BIG[0] else []
        got = all_gather(srcs + taps, "gather_" + "_".join(names))
        for n, full in zip(names, got):
            if n == "l1_w_in":
                gathered[n] = full
            elif n in _ROW_SHARDED:
                wts[n] = full.reshape(-1, D)
            else:
                wts[n] = full.reshape(NDEV * D, -1)
        for c, full in zip(_CONV, got[len(names):] if taps else []):
            wts[c] = full.transpose(1, 0, 2).reshape(CONV_K, -1)

    def at(point, after, value):
        if point == "l1_w_in":
            got, after = lax.optimization_barrier((gathered[point], after))
            wts["l1_w_in_t"] = got.reshape(-1, D)
            wts["l1_w_f_t"] = jnp.pad(wts["l1_w_in_t"][5 * W :], ((0, 128 - nh), (0, 0)))
        elif point == "loss":
            gathered["loss"] = value[0, :1]
        elif point == "small_ready":
            early = [n for n in _SMALL if n != _LAST_SMALL]
            gathered["small"] = all_gather([_pack([value[n] for n in early] + [gathered["loss"]])], "gather_small_grads")[0]
        elif point == "small_done":
            after = update_small([n for n in _SMALL if n != _LAST_SMALL], gathered["small"], "small", after, True)
        return after

    out_g, out_d, out_m, out_v = {}, {}, {}, {}

    def update_small(names, all_terms, tag, after=None, with_loss=False):
        shapes = [w[n].shape for n in names]
        full_shapes = [(CONV_K, NDEV * w[n].shape[1]) if n in _CONV else w[n].shape for n in names]
        summed = _unpack(sum_slots(all_terms, f"sum_{tag}_grads"), full_shapes + ([(1,)] if with_loss else []))
        if with_loss:
            loss_sum.append(summed[-1][0])
        grads = {}
        for n, t in zip(names, summed):
            if n in _CONV:
                cols = w[n].shape[1]
                t = lax.dynamic_slice_in_dim(t, me * cols, cols, axis=1)
            grads[n] = t
        res = adamw(
            _pack([w[n] for n in names]), _pack([mom[n] for n in names]), _pack([var[n] for n in names]),
            _pack([grads[n] for n in names])[None], f"adamw_{tag}",
        )
        if after is not None:
            res, after = lax.optimization_barrier((res, after))
        for dst, packed_out in zip((out_g, out_d, out_m, out_v), res):
            for n, t in zip(names, _unpack(packed_out, shapes)):
                dst[n] = t
        return after

    core = jnp.reshape(cc, (1,)).astype(jnp.int32)
    chip = jnp.reshape(2 * cx + cy, (1,)).astype(jnp.int32)
    pair_flying, chip_flying = [], []

    def tie(value, after):
        if after is None:
            return value, None
        return lax.optimization_barrier((value, after))

    def advance(after, new=None):
        jobs, names = [], []
        if pair_flying:
            n0, flying = pair_flying.pop()
            landed = _split_wait(_pair_copies, *flying, f"reduce_pair_wait_{n0}")
            summed = pair_sum(flying[2], landed, core, f"pair_sum_{n0}")
            jobs.append((_chip_copies, summed, summed.shape, 3))
            names.append(n0)
        if new is not None:
            jobs.append((_pair_copies, new[1], new[1].shape[:1] + new[1].shape[2:], 4))
            names.append(new[0])
        started, token = _split_start_many(jobs, "reduce_start_" + "_".join(names))
        token, after = tie(token, after)
        if new is not None:
            pair_flying.append((new[0], started.pop() + [token]))
        if started:
            chip_flying.append((names[0], started[0] + [token]))
        return after

    def update(after, behind=None):
        n, flying = chip_flying.pop(0)
        if behind is not None:
            flying[4], _ = lax.optimization_barrier((flying[4], behind))
        landed = _split_wait(_chip_copies, *flying, f"reduce_chips_wait_{n}")
        turn = (lambda t: t.T) if n == "l1_w_in" else (lambda t: t)
        res = adamw_reduced(turn(w[n]), turn(mom[n]), turn(var[n]), flying[2], landed, chip, f"adamw_{n}")
        res, after = tie(res, after)
        out_g[n], out_d[n], out_m[n], out_v[n] = [turn(t) for t in res]
        return after, res[0]

    def on_grad(n, term, after):
        if n is None:
            return advance(after)
        if n in _ROW_SHARDED or n == "l1_w_in":
            term = term.reshape(NDEV, -1, D)
        else:
            term = term.reshape(NDEV, D, -1)
        term = term.reshape((4, 2) + term.shape[1:])
        if len(chip_flying) == UPDATE_LAG:
            after, _ = update(after)
        return advance(after, (n, term))

    for n in _BIG:
        start_gather([n])
    dx, g = local_step(xs, target, wts, at, on_grad)
    last = _pack([g[_LAST_SMALL]])
    *flying, done = _split_start(_direct_copies, last, (NDEV,) + last.shape, NDEV - 1, "gather_last_grad")
    while len(chip_flying) > 1:
        _, done = update(None, behind=done)
    landed = _split_wait(_direct_copies, *flying, done, "gather_last_grad_wait")
    update_small([_LAST_SMALL], lax.dynamic_update_slice(landed, last[None], (me, 0, 0)), "last")
    update(None, behind=out_g[_LAST_SMALL])
    loss = loss_sum[0]

    return (loss, dx[None], *[out_g[n] for n in _WEIGHTS], *[out_d[n] for n in _WEIGHTS],
            *[out_m[n] for n in _WEIGHTS], *[out_v[n] for n in _WEIGHTS])
```

```python
import functools
import math

import jax
import jax.numpy as jnp
from jax import lax
from jax.experimental import pallas as pl
from jax.experimental.pallas import tpu as pltpu
from jax.experimental.pallas import tpu_sc as plsc

F32 = jnp.float32
BF16 = jnp.bfloat16
HD = 128
EPS = 1e-6
CONV_K = 3
VMEM_LIMIT_BYTES = 48 << 20
NDEV = 8
MESH = pl.DeviceIdType.MESH

ADAM_LR = 0.001
ADAM_B1 = 0.9
ADAM_B2 = 0.999
ADAM_EPS = 1e-08
ADAM_WD = 0.01
ADAM_STEP = 10


def _cp(sem):
    return pltpu.CompilerParams(dimension_semantics=sem, vmem_limit_bytes=VMEM_LIMIT_BYTES)


def _pick(n, prefs):
    for p in prefs:
        if n % p == 0:
            return p
    return n


def _dot(a, b):
    return jnp.dot(a, b, preferred_element_type=F32)


def _dot_nt(a, b):
    return lax.dot_general(a, b, (((1,), (1,)), ((), ())), preferred_element_type=F32)


def _dot_tn(a, b):
    return lax.dot_general(a, b, (((0,), (0,)), ((), ())), preferred_element_type=F32)


def _split3(x):
    hi = x.astype(BF16)
    r = x - hi.astype(F32)
    mid = r.astype(BF16)
    lo = (r - mid.astype(F32)).astype(BF16)
    return hi, mid, lo


def _dot_ones_left(ones_bf16, x):
    hi, mid, lo = _split3(x)
    return _dot(ones_bf16, hi) + _dot(ones_bf16, mid) + _dot(ones_bf16, lo)


def _iota2(shape, axis):
    return lax.broadcasted_iota(jnp.int32, shape, axis)


def mm_nn(a, w2d, nb, name, out_dtype=BF16, res=None, tm=None, tn=None, tk=None, a_map=None, a_shape=None):
    M, K = a_shape or a.shape
    n = w2d.shape[1]
    assert w2d.shape[0] == nb * K or (nb == 1 and w2d.shape[0] > K)
    a_map = a_map or (lambda i, k: (i, k))
    tm = tm or _pick(M, (1024, 512, 256, 128))
    tn = tn or _pick(n, (1408, 1024, 768, 512, 256, 128))
    tk = tk or (K if K <= 2048 else _pick(K, (1408, 1024, 512, 256, 128)))
    nk, nt = K // tk, n // tn
    has_res = res is not None

    def body(*refs):
        if has_res:
            a_ref, w_ref, r_ref, o_ref = refs[:4]
        else:
            a_ref, w_ref, o_ref = refs[:3]
            r_ref = None
        part = _dot(a_ref[...], w_ref[...])

        def finish(acc):
            if r_ref is not None:
                acc = acc + r_ref[...].astype(F32)
            o_ref[...] = acc.astype(o_ref.dtype)

        if nk == 1:
            finish(part)
        else:
            acc_ref = refs[-1]
            k = pl.program_id(3)

            @pl.when(k == 0)
            def _():
                acc_ref[...] = part

            @pl.when(k > 0)
            def _():
                acc_ref[...] += part

            @pl.when(k == nk - 1)
            def _():
                finish(acc_ref[...])

    in_specs = [
        pl.BlockSpec((tm, tk), lambda i, j, t, k: a_map(i, k)),
        pl.BlockSpec((tk, tn), lambda i, j, t, k: (j * nk + k, t)),
    ]
    args = [a, w2d]
    out_spec = pl.BlockSpec((tm, tn), lambda i, j, t, k: (i, j * nt + t))
    if has_res:
        in_specs.append(out_spec)
        args.append(res)
    return pl.pallas_call(
        body,
        grid=(M // tm, nb, nt, nk),
        in_specs=in_specs,
        out_specs=out_spec,
        out_shape=jax.ShapeDtypeStruct((M, nb * n), out_dtype),
        scratch_shapes=[pltpu.VMEM((tm, tn), F32)] if nk > 1 else [],
        compiler_params=_cp(("parallel", "parallel", "parallel", "arbitrary")),
        name=name,
    )(*args)


def mm_nt(dy2d, w2d, nb, M, K, name, out_dtype=BF16, res=None, dy_maps=None, tm=None, tko=None, tn=None):
    n = w2d.shape[1]
    assert w2d.shape[0] == nb * K or (nb == 1 and w2d.shape[0] > K)
    tm = tm or _pick(M, (1024, 512, 256, 128))
    tko = tko or _pick(K, (1024, 512, 256, 128))
    tn = tn or _pick(n, (1408, 1024, 768, 512, 256, 128))
    nt, nko = n // tn, K // tko
    has_res = res is not None
    if dy_maps is None:
        dy_maps = [lambda i, j, t: (i, j * nt + t)]
    nd = len(dy_maps)
    td = tn // nd

    one_step = nb * nt == 1

    def body(*refs):
        d_refs, w_ref = refs[:nd], refs[nd]
        r_ref = refs[nd + 1] if has_res else None
        d = d_refs[0][...] if nd == 1 else jnp.concatenate([r[...] for r in d_refs], axis=1)
        part = _dot_nt(d, w_ref[...])
        if one_step:
            o_ref = refs[-1]
            if r_ref is not None:
                part = part + r_ref[...].astype(F32)
            o_ref[...] = part.astype(o_ref.dtype)
            return
        o_ref, acc_ref = refs[-2], refs[-1]
        j, t = pl.program_id(2), pl.program_id(3)
        first = jnp.logical_and(j == 0, t == 0)
        last = jnp.logical_and(j == nb - 1, t == nt - 1)

        @pl.when(first)
        def _():
            acc_ref[...] = part

        @pl.when(jnp.logical_not(first))
        def _():
            acc_ref[...] += part

        @pl.when(last)
        def _():
            acc = acc_ref[...]
            if r_ref is not None:
                acc = acc + r_ref[...].astype(F32)
            o_ref[...] = acc.astype(o_ref.dtype)

    in_specs = [pl.BlockSpec((tm, td), functools.partial(lambda f, i, ko, j, t: f(i, j, t), f)) for f in dy_maps]
    in_specs.append(pl.BlockSpec((tko, tn), lambda i, ko, j, t: (j * nko + ko, t)))
    args = [dy2d] * nd + [w2d]
    out_spec = pl.BlockSpec((tm, tko), lambda i, ko, j, t: (i, ko))
    if has_res:
        in_specs.append(out_spec)
        args.append(res)
    return pl.pallas_call(
        body,
        grid=(M // tm, nko, nb, nt),
        in_specs=in_specs,
        out_specs=out_spec,
        out_shape=jax.ShapeDtypeStruct((M, K), out_dtype),
        scratch_shapes=[] if one_step else [pltpu.VMEM((tm, tko), F32)],
        compiler_params=_cp(("parallel", "parallel", "arbitrary", "arbitrary")),
        name=name,
    )(*args)


def mm_tn(x, dy2d, nb, n, name, out_dtype=BF16, dy_maps=None, tko=None, tn=None, x_map=None, x_shape=None):
    S, K = x_shape or x.shape
    x_map = x_map or (lambda ko: (0, ko))
    tko = tko or _pick(K, (512, 256, 128))
    tn = tn or _pick(n, (1408, 1024, 768, 512, 256, 128))
    nt, nko = n // tn, K // tko
    if dy_maps is None:
        dy_maps = [lambda j, t: (0, j * nt + t)]
    nd = len(dy_maps)
    td = tn // nd

    def body(*refs):
        x_ref, d_refs, o_ref = refs[0], refs[1 : 1 + nd], refs[-1]
        d = d_refs[0][...] if nd == 1 else jnp.concatenate([r[...] for r in d_refs], axis=1)
        o_ref[...] = _dot_tn(x_ref[...], d).astype(o_ref.dtype)

    in_specs = [pl.BlockSpec((S, tko), lambda ko, j, t: x_map(ko))]
    in_specs += [pl.BlockSpec((S, td), functools.partial(lambda f, ko, j, t: f(j, t), f)) for f in dy_maps]
    return pl.pallas_call(
        body,
        grid=(nko, nb, nt),
        in_specs=in_specs,
        out_specs=pl.BlockSpec((tko, tn), lambda ko, j, t: (j * nko + ko, t)),
        out_shape=jax.ShapeDtypeStruct((nb * K, n), out_dtype),
        compiler_params=_cp(("parallel", "parallel", "parallel")),
        name=name,
    )(x, *([dy2d] * nd))


def rms_fwd(x, g, name):
    S, D = x.shape
    tm = _pick(S, (512, 256, 128))

    def body(x_ref, g_ref, o_ref):
        xf = x_ref[...]
        r = lax.rsqrt(jnp.mean(xf * xf, axis=-1, keepdims=True) + EPS)
        o_ref[...] = (xf * r * g_ref[...]).astype(o_ref.dtype)

    return pl.pallas_call(
        body,
        grid=(S // tm,),
        in_specs=[pl.BlockSpec((tm, D), lambda i: (i, 0)), pl.BlockSpec((1, D), lambda i: (0, 0))],
        out_specs=pl.BlockSpec((tm, D), lambda i: (i, 0)),
        out_shape=jax.ShapeDtypeStruct((S, D), BF16),
        compiler_params=_cp(("parallel",)),
        name=name,
    )(x, g.reshape(1, D))


def rms_bwd(x, g, dh, dres, name):
    S, D = x.shape
    tm = _pick(S, (256, 128))

    def body(x_ref, g_ref, dh_ref, dr_ref, dx_ref, dxb_ref, dg_ref):
        i = pl.program_id(0)
        xf = x_ref[...]
        dh = dh_ref[...].astype(F32)
        r = lax.rsqrt(jnp.mean(xf * xf, axis=-1, keepdims=True) + EPS)
        gy = dh * g_ref[...]
        proj = jnp.mean(gy * xf, axis=-1, keepdims=True)
        dx = dr_ref[...] + r * gy - xf * (r * r * r * proj)
        dx_ref[...] = dx
        dxb_ref[...] = dx.astype(BF16)
        dg = jnp.sum(dh * (xf * r), axis=0, keepdims=True)

        @pl.when(i == 0)
        def _():
            dg_ref[...] = dg

        @pl.when(i > 0)
        def _():
            dg_ref[...] += dg

    row = pl.BlockSpec((tm, D), lambda i: (i, 0))
    vec = pl.BlockSpec((1, D), lambda i: (0, 0))
    return pl.pallas_call(
        body,
        grid=(S // tm,),
        in_specs=[row, vec, row, row],
        out_specs=[row, row, vec],
        out_shape=[jax.ShapeDtypeStruct((S, D), F32), jax.ShapeDtypeStruct((S, D), BF16), jax.ShapeDtypeStruct((1, D), F32)],
        compiler_params=_cp(("arbitrary",)),
        name=name,
    )(x, g.reshape(1, D), dh, dres)


def loss_head(x, g, target, name):
    S, D = x.shape
    tm = _pick(S, (256, 128))

    def body(x_ref, g_ref, t_ref, dx_ref, dxb_ref, dg_ref, loss_ref):
        i = pl.program_id(0)
        xf = x_ref[...]
        gg = g_ref[...]
        r = lax.rsqrt(jnp.mean(xf * xf, axis=-1, keepdims=True) + EPS)
        xh = xf * r
        err = xh * gg - t_ref[...]
        part = (0.5 / D) * jnp.sum(err * err)
        dy = err * (1.0 / D)
        gy = dy * gg
        proj = jnp.mean(gy * xf, axis=-1, keepdims=True)
        dx = r * gy - xf * (r * r * r * proj)
        dx_ref[...] = dx
        dxb_ref[...] = dx.astype(BF16)
        dg = jnp.sum(dy * xh, axis=0, keepdims=True)
        lossb = jnp.full(loss_ref.shape, part, F32)

        @pl.when(i == 0)
        def _():
            dg_ref[...] = dg
            loss_ref[...] = lossb

        @pl.when(i > 0)
        def _():
            dg_ref[...] += dg
            loss_ref[...] += lossb

    row = pl.BlockSpec((tm, D), lambda i: (i, 0))
    vec = pl.BlockSpec((1, D), lambda i: (0, 0))
    return pl.pallas_call(
        body,
        grid=(S // tm,),
        in_specs=[row, vec, row],
        out_specs=[row, row, vec, pl.BlockSpec((8, 128), lambda i: (0, 0))],
        out_shape=[
            jax.ShapeDtypeStruct((S, D), F32),
            jax.ShapeDtypeStruct((S, D), BF16),
            jax.ShapeDtypeStruct((1, D), F32),
            jax.ShapeDtypeStruct((8, 128), F32),
        ],
        compiler_params=_cp(("arbitrary",)),
        name=name,
    )(x, g.reshape(1, D), target)


def _shift_down(s, k):
    if k == 0:
        return s
    return jnp.where(_iota2(s.shape, 0) >= k, pltpu.roll(s, k, axis=0), 0.0)


def _shift_up(s, k):
    if k == 0:
        return s
    n = s.shape[0]
    return jnp.where(_iota2(s.shape, 0) < n - k, pltpu.roll(s, n - k, axis=0), 0.0)


def _conv(s, w):
    return w[0:1] * _shift_down(s, 2) + w[1:2] * _shift_down(s, 1) + w[2:3] * s


def _conv_t(d, w):
    return w[2:3] * d + w[1:2] * _shift_up(d, 1) + w[0:1] * _shift_up(d, 2)


def _conv_dw(d, s):
    return [jnp.sum(d * _shift_down(s, CONV_K - 1 - k), axis=0, keepdims=True) for k in range(CONV_K)]


def sc_fwd(p, convw, cat, W, name):
    S = p.shape[0]
    tc = _pick(W, (256, 128))
    nc = W // tc

    def body(gb_ref, gc_ref, hi_ref, w_ref, cat_ref, o_ref):
        s = gc_ref[...].astype(F32) * hi_ref[...].astype(F32)
        o_ref[...] = (gb_ref[...].astype(F32) * _conv(s, w_ref[...])).astype(o_ref.dtype)

    col = lambda part: pl.BlockSpec((S, tc), lambda c: (0, part * nc + c))
    return pl.pallas_call(
        body,
        grid=(nc,),
        in_specs=[col(3), col(4), col(5), pl.BlockSpec((CONV_K, tc), lambda c: (0, c)), pl.BlockSpec(memory_space=pl.ANY)],
        out_specs=col(1),
        out_shape=jax.ShapeDtypeStruct(cat.shape, cat.dtype),
        input_output_aliases={4: 0},
        compiler_params=_cp(("parallel",)),
        name=name,
    )(p, p, p, convw, cat)


def sc_bwd(p, convw, dcat, dp, W, name):
    S = p.shape[0]
    tc = _pick(W, (256, 128))
    nc = W // tc

    def body(gb_ref, gc_ref, hi_ref, w_ref, do_ref, dp_in_ref, dp_ref, dw_ref):
        gb = gb_ref[...].astype(F32)
        gc = gc_ref[...].astype(F32)
        hi = hi_ref[...].astype(F32)
        w = w_ref[...]
        do = do_ref[...].astype(F32)
        s = gc * hi
        dcs = do * gb
        ds = _conv_t(dcs, w)
        dp_ref[0] = (do * _conv(s, w)).astype(dp_ref.dtype)
        dp_ref[1] = (ds * hi).astype(dp_ref.dtype)
        dp_ref[2] = (ds * gc).astype(dp_ref.dtype)
        for k, row in enumerate(_conv_dw(dcs, s)):
            dw_ref[k : k + 1, :] = row

    col = lambda part: pl.BlockSpec((S, tc), lambda c: (0, part * nc + c))
    return pl.pallas_call(
        body,
        grid=(nc,),
        in_specs=[
            col(3), col(4), col(5),
            pl.BlockSpec((CONV_K, tc), lambda c: (0, c)),
            pl.BlockSpec((S, tc), lambda c: (0, nc + c)),
            pl.BlockSpec(memory_space=pl.ANY),
        ],
        out_specs=[pl.BlockSpec((3, S, tc), lambda c: (1, 0, c)), pl.BlockSpec((CONV_K, tc), lambda c: (0, c))],
        out_shape=[jax.ShapeDtypeStruct(dp.shape, dp.dtype), jax.ShapeDtypeStruct((CONV_K, W), F32)],
        input_output_aliases={5: 0},
        compiler_params=_cp(("parallel",)),
        name=name,
    )(p, p, p, convw, dcat, dp)


def _silu_parts(a):
    sig = 1.0 / (1.0 + jnp.exp(-a))
    return a * sig, sig


def ffn_act_fwd(u, convw, F, name):
    S = u.shape[0]
    tc = _pick(F, (256, 128))
    nc = F // tc

    def body(ug_ref, uu_ref, wg_ref, wu_ref, o_ref):
        ag = _conv(ug_ref[...].astype(F32), wg_ref[...])
        au = _conv(uu_ref[...].astype(F32), wu_ref[...])
        o_ref[...] = (_silu_parts(ag)[0] * au).astype(o_ref.dtype)

    col = lambda half: pl.BlockSpec((S, tc), lambda c: (0, half * nc + c))
    wcol = lambda half: pl.BlockSpec((CONV_K, tc), lambda c: (0, half * nc + c))
    return pl.pallas_call(
        body,
        grid=(nc,),
        in_specs=[col(0), col(1), wcol(0), wcol(1)],
        out_specs=pl.BlockSpec((S, tc), lambda c: (0, c)),
        out_shape=jax.ShapeDtypeStruct((S, F), BF16),
        compiler_params=_cp(("parallel",)),
        name=name,
    )(u, u, convw, convw)


def ffn_act_bwd(u, convw, dact, F, name):
    S = u.shape[0]
    tc = _pick(F, (128,))
    nc = F // tc

    def body(ug_ref, uu_ref, wg_ref, wu_ref, da_ref, du_ref, dw_ref):
        ug = ug_ref[...].astype(F32)
        uu = uu_ref[...].astype(F32)
        wg = wg_ref[...]
        wu = wu_ref[...]
        da = da_ref[...].astype(F32)
        ag = _conv(ug, wg)
        au = _conv(uu, wu)
        sl, sig = _silu_parts(ag)
        dag = da * au * (sig * (1.0 + ag * (1.0 - sig)))
        dau = da * sl
        du_ref[0] = _conv_t(dag, wg).astype(du_ref.dtype)
        du_ref[1] = _conv_t(dau, wu).astype(du_ref.dtype)
        for k, (rg, ru) in enumerate(zip(_conv_dw(dag, ug), _conv_dw(dau, uu))):
            dw_ref[0, k : k + 1, :] = rg
            dw_ref[1, k : k + 1, :] = ru

    col = lambda half: pl.BlockSpec((S, tc), lambda c: (0, half * nc + c))
    wcol = lambda half: pl.BlockSpec((CONV_K, tc), lambda c: (0, half * nc + c))
    return pl.pallas_call(
        body,
        grid=(nc,),
        in_specs=[col(0), col(1), wcol(0), wcol(1), pl.BlockSpec((S, tc), lambda c: (0, c))],
        out_specs=[pl.BlockSpec((2, S, tc), lambda c: (0, 0, c)), pl.BlockSpec((2, CONV_K, tc), lambda c: (0, 0, c))],
        out_shape=[jax.ShapeDtypeStruct((2, S, F), BF16), jax.ShapeDtypeStruct((2, CONV_K, F), F32)],
        compiler_params=_cp(("parallel",)),
        name=name,
    )(u, u, convw, convw, dact)


def _softplus(z):
    return jnp.maximum(z, 0.0) + jnp.log(1.0 + jnp.exp(-jnp.abs(z)))


def _key_strip(S):
    return _pick(S, (512, 256, 128))


def _query_rows(S):
    tq = _pick(S, (512, 256, 128))
    assert _key_strip(S) % tq == 0
    return tq


def _split2(x):
    hi = x.astype(BF16)
    return hi, (x - hi.astype(F32)).astype(BF16)


def _block_sums(x, ones_bf16):
    hi, lo = _split2(x)
    return [
        _dot(hi[:, b * HD : (b + 1) * HD], ones_bf16) + _dot(lo[:, b * HD : (b + 1) * HD], ones_bf16)
        for b in range(x.shape[1] // HD)
    ]


def _strip_mask(shape, row0, off, strict):
    cols, rows = _iota2(shape, 1) + off, _iota2(shape, 0) + row0
    return cols < rows if strict else cols <= rows


def _sb_strip(q, ks, row0, off, run, su, masked):
    z = _dot_nt(q, ks) * (HD ** -0.5)
    sp = _softplus(z)
    mask = _strip_mask(z.shape, row0, off, True) if masked else None
    l = jnp.where(mask, -sp, 0.0) if masked else -sp
    within = _block_sums(l, su)
    later = [None] * len(within)
    for b in reversed(range(len(within))):
        later[b] = within[b] + run
        run = run + jnp.sum(l[:, b * HD : (b + 1) * HD], axis=1, keepdims=True)
    a = jnp.exp(z - sp + jnp.concatenate(later, axis=1))
    return z, (jnp.where(mask, a, 0.0) if masked else a), run


def sb_fwd(p, W, name):
    S = p.shape[0]
    TQ, TK = _query_rows(S), _key_strip(S)
    nh, nq = W // HD, S // TQ

    def body(q_ref, k_ref, v_ref, o_ref):
        i = pl.program_id(1)
        q = q_ref[...]
        su = (_iota2((HD, HD), 0) > _iota2((HD, HD), 1)).astype(BF16)
        last = (i * TQ + TQ - 1) // TK

        def strip(g, carry, masked):
            acc, run = carry
            off = pl.multiple_of(g * TK, TK)
            _, a, run = _sb_strip(q, k_ref[pl.ds(off, TK), :], i * TQ, off, run, su, masked)
            return acc + _dot(a.astype(BF16), v_ref[pl.ds(off, TK), :]), run

        carry = strip(last, (jnp.zeros((TQ, HD), F32), jnp.zeros((TQ, 1), F32)), True)
        acc, _ = lax.fori_loop(0, last, lambda gg, c: strip(last - 1 - gg, c, False), carry)
        o_ref[...] = acc.astype(o_ref.dtype)

    return pl.pallas_call(
        body,
        grid=(nh, nq),
        in_specs=[
            pl.BlockSpec((TQ, HD), lambda h, i: (i, h)),
            pl.BlockSpec((S, HD), lambda h, i: (0, nh + h)),
            pl.BlockSpec((S, HD), lambda h, i: (0, 2 * nh + h)),
        ],
        out_specs=pl.BlockSpec((TQ, HD), lambda h, i: (i, h)),
        out_shape=jax.ShapeDtypeStruct((S, 2 * W), BF16),
        compiler_params=_cp(("parallel", "arbitrary")),
        name=name,
    )(p, p, p)


def sb_bwd(p, dcat, W, name):
    S = p.shape[0]
    TQ, TK = _query_rows(S), _key_strip(S)
    nh, nq = W // HD, S // TQ
    scale = HD ** -0.5

    def body(q_ref, k_ref, v_ref, do_ref, dp_ref, dk_acc, dv_acc, e_scr, z_scr):
        i = pl.program_id(1)
        q = q_ref[...]
        do = do_ref[...]
        su = (_iota2((HD, HD), 0) > _iota2((HD, HD), 1)).astype(BF16)
        sl = (_iota2((HD, HD), 0) < _iota2((HD, HD), 1)).astype(BF16)
        last = (i * TQ + TQ - 1) // TK

        @pl.when(i == 0)
        def _():
            dk_acc[...] = jnp.zeros_like(dk_acc)
            dv_acc[...] = jnp.zeros_like(dv_acc)

        def pass_a(g, run, masked):
            off = pl.multiple_of(g * TK, TK)
            z, a, run = _sb_strip(q, k_ref[pl.ds(off, TK), :], i * TQ, off, run, su, masked)
            e_scr[g] = a * _dot_nt(do, v_ref[pl.ds(off, TK), :])
            z_scr[g] = z
            dv_acc[pl.ds(off, TK), :] += _dot_tn(a.astype(BF16), do)
            return run

        run = pass_a(last, jnp.zeros((TQ, 1), F32), True)
        lax.fori_loop(0, last, lambda gg, r: pass_a(last - 1 - gg, r, False), run)

        def pass_b(g, carry, masked):
            dq, run_e = carry
            off = pl.multiple_of(g * TK, TK)
            e = e_scr[g]
            z = z_scr[g]
            within = _block_sums(e, sl)
            before = []
            for b in range(len(within)):
                before.append(within[b] + run_e)
                run_e = run_e + jnp.sum(e[:, b * HD : (b + 1) * HD], axis=1, keepdims=True)
            sig = 1.0 / (1.0 + jnp.exp(-z))
            dz = e * (1.0 - sig) - jnp.concatenate(before, axis=1) * sig
            if masked:
                dz = jnp.where(_strip_mask(z.shape, i * TQ, off, True), dz, 0.0)
            dz = (dz * scale).astype(BF16)
            dq = dq + _dot(dz, k_ref[pl.ds(off, TK), :])
            dk_acc[pl.ds(off, TK), :] += _dot_tn(dz, q)
            return dq, run_e

        carry = lax.fori_loop(0, last, lambda g, c: pass_b(g, c, False), (jnp.zeros((TQ, HD), F32), jnp.zeros((TQ, 1), F32)))
        dq, _ = pass_b(last, carry, True)
        dp_ref[0, pl.ds(pl.multiple_of(i * TQ, TQ), TQ), :] = dq.astype(dp_ref.dtype)

        @pl.when(i == nq - 1)
        def _():
            dp_ref[1] = dk_acc[...].astype(dp_ref.dtype)
            dp_ref[2] = dv_acc[...].astype(dp_ref.dtype)

    return pl.pallas_call(
        body,
        grid=(nh, nq),
        in_specs=[
            pl.BlockSpec((TQ, HD), lambda h, i: (i, h)),
            pl.BlockSpec((S, HD), lambda h, i: (0, nh + h)),
            pl.BlockSpec((S, HD), lambda h, i: (0, 2 * nh + h)),
            pl.BlockSpec((TQ, HD), lambda h, i: (i, h)),
        ],
        out_specs=pl.BlockSpec((3, S, HD), lambda h, i: (0, 0, h)),
        out_shape=jax.ShapeDtypeStruct((6, S, W), BF16),
        scratch_shapes=[
            pltpu.VMEM((S, HD), F32),
            pltpu.VMEM((S, HD), F32),
            pltpu.VMEM((S // TK, TQ, TK), F32),
            pltpu.VMEM((S // TK, TQ, TK), F32),
        ],
        compiler_params=_cp(("parallel", "arbitrary")),
        name=name,
    )(p, p, p, dcat)


def fox_gate_fwd(f, b, name):
    S = f.shape[0]
    nq = S // HD

    def body(f_ref, b_ref, c_ref, run):
        i = pl.program_id(0)

        @pl.when(i == 0)
        def _():
            run[...] = jnp.zeros_like(run)

        lf = -_softplus(-(f_ref[...] + b_ref[...]))
        tri = (_iota2((HD, HD), 0) >= _iota2((HD, HD), 1)).astype(BF16)
        c_ref[...] = _dot_ones_left(tri, lf) + run[...]
        run[...] += jnp.sum(lf, axis=0, keepdims=True)

    return pl.pallas_call(
        body,
        grid=(nq,),
        in_specs=[pl.BlockSpec((HD, 128), lambda i: (i, 0)), pl.BlockSpec((1, 128), lambda i: (0, 0))],
        out_specs=pl.BlockSpec((HD, 128), lambda i: (i, 0)),
        out_shape=jax.ShapeDtypeStruct((S, 128), F32),
        scratch_shapes=[pltpu.VMEM((1, 128), F32)],
        compiler_params=_cp(("arbitrary",)),
        name=name,
    )(f, b)


def fox_gate_bwd(f, b, dc, name):
    S = f.shape[0]
    nq = S // HD

    def body(f_ref, b_ref, dc_ref, df_ref, db_ref, run):
        i = pl.program_id(0)

        @pl.when(i == 0)
        def _():
            run[...] = jnp.zeros_like(run)

        dc = dc_ref[...]
        tri = (_iota2((HD, HD), 0) <= _iota2((HD, HD), 1)).astype(BF16)
        dlf = _dot_ones_left(tri, dc) + run[...]
        run[...] += jnp.sum(dc, axis=0, keepdims=True)
        x = f_ref[...] + b_ref[...]
        df = dlf * (1.0 / (1.0 + jnp.exp(x)))
        df_ref[...] = df
        db = jnp.sum(df, axis=0, keepdims=True)

        @pl.when(i == 0)
        def _():
            db_ref[...] = db

        @pl.when(i > 0)
        def _():
            db_ref[...] += db

    rev = pl.BlockSpec((HD, 128), lambda i: (nq - 1 - i, 0))
    vec = pl.BlockSpec((1, 128), lambda i: (0, 0))
    return pl.pallas_call(
        body,
        grid=(nq,),
        in_specs=[rev, vec, rev],
        out_specs=[rev, vec],
        out_shape=[jax.ShapeDtypeStruct((S, 128), F32), jax.ShapeDtypeStruct((1, 128), F32)],
        scratch_shapes=[pltpu.VMEM((1, 128), F32)],
        compiler_params=_cp(("arbitrary",)),
        name=name,
    )(f, b, dc)


def _fox_logits(q, ks, ct, cs, row0, off, masked):
    s = _dot_nt(q, ks) * (HD ** -0.5) + (ct - cs)
    if not masked:
        return s, None
    mask = _strip_mask(s.shape, row0, off, False)
    return jnp.where(mask, s, -1e30), mask


def fox_fwd(p, ccol, crow, cat, W, name):
    S = p.shape[0]
    TQ, TK = _query_rows(S), _key_strip(S)
    nh, nq = W // HD, S // TQ

    def body(q_ref, k_ref, v_ref, cc_ref, cr_ref, cat_ref, o_ref, lse_ref):
        i = pl.program_id(1)
        q = q_ref[...]
        ct = cc_ref[0]

        def step(g, carry, masked):
            m, l, acc = carry
            off = pl.multiple_of(g * TK, TK)
            s, _ = _fox_logits(q, k_ref[pl.ds(off, TK), :], ct, cr_ref[0, pl.ds(g, 1), :], i * TQ, off, masked)
            m_new = jnp.maximum(m, jnp.max(s, axis=1, keepdims=True))
            alpha = jnp.exp(m - m_new)
            pr = jnp.exp(s - m_new)
            l = alpha * l + jnp.sum(pr, axis=1, keepdims=True)
            acc = alpha * acc + _dot(pr.astype(BF16), v_ref[pl.ds(off, TK), :])
            return m_new, l, acc

        init = (jnp.full((TQ, 1), -1e30, F32), jnp.zeros((TQ, 1), F32), jnp.zeros((TQ, HD), F32))
        last = (i * TQ + TQ - 1) // TK
        m, l, acc = step(last, lax.fori_loop(0, last, lambda g, c: step(g, c, False), init), True)
        o_ref[...] = (acc / l).astype(o_ref.dtype)
        lse_ref[0] = m + jnp.log(l)

    return pl.pallas_call(
        body,
        grid=(nh, nq),
        in_specs=[
            pl.BlockSpec((TQ, HD), lambda h, i: (i, 2 * nh + h)),
            pl.BlockSpec((S, HD), lambda h, i: (0, 3 * nh + h)),
            pl.BlockSpec((S, HD), lambda h, i: (0, 4 * nh + h)),
            pl.BlockSpec((1, TQ, 1), lambda h, i: (h, i, 0)),
            pl.BlockSpec((1, S // TK, TK), lambda h, i: (h, 0, 0)),
            pl.BlockSpec(memory_space=pl.ANY),
        ],
        out_specs=[pl.BlockSpec((TQ, HD), lambda h, i: (i, nh + h)), pl.BlockSpec((1, TQ, 1), lambda h, i: (h, i, 0))],
        out_shape=[jax.ShapeDtypeStruct(cat.shape, cat.dtype), jax.ShapeDtypeStruct((nh, S, 1), F32)],
        input_output_aliases={5: 0},
        compiler_params=_cp(("parallel", "arbitrary")),
        name=name,
    )(p, p, p, ccol, crow, cat)


def fox_bwd(p, ccol, crow, cat, lse, dcat, dp, W, name):
    S = p.shape[0]
    TQ, TK = _query_rows(S), _key_strip(S)
    nh, nq = W // HD, S // TQ
    scale = HD ** -0.5

    def body(q_ref, k_ref, v_ref, cc_ref, cr_ref, o_ref, lse_ref, do_ref, dp_in_ref, dp_ref, dcs_ref, dct_ref, dk_acc, dv_acc):
        i = pl.program_id(1)
        q = q_ref[...]
        do = do_ref[...]
        ct = cc_ref[0]
        lse_i = lse_ref[0]
        delta = jnp.sum(do.astype(F32) * o_ref[...].astype(F32), axis=1, keepdims=True)

        @pl.when(i == 0)
        def _():
            dk_acc[...] = jnp.zeros_like(dk_acc)
            dv_acc[...] = jnp.zeros_like(dv_acc)
            dcs_ref[...] = jnp.zeros_like(dcs_ref)

        def step(g, carry, masked):
            dq, dct = carry
            off = pl.multiple_of(g * TK, TK)
            ks = k_ref[pl.ds(off, TK), :]
            s, mask = _fox_logits(q, ks, ct, cr_ref[0, pl.ds(g, 1), :], i * TQ, off, masked)
            pr = jnp.where(mask, jnp.exp(s - lse_i), 0.0) if masked else jnp.exp(s - lse_i)
            ds = pr * (_dot_nt(do, v_ref[pl.ds(off, TK), :]) - delta)
            dv_acc[pl.ds(off, TK), :] += _dot_tn(pr.astype(BF16), do)
            dsb = (ds * scale).astype(BF16)
            dk_acc[pl.ds(off, TK), :] += _dot_tn(dsb, q)
            dcs_ref[0, pl.ds(g, 1), :] += jnp.sum(ds, axis=0, keepdims=True)
            return dq + _dot(dsb, ks), dct + jnp.sum(ds, axis=1, keepdims=True)

        last = (i * TQ + TQ - 1) // TK
        carry = lax.fori_loop(0, last, lambda g, c: step(g, c, False), (jnp.zeros((TQ, HD), F32), jnp.zeros((TQ, 1), F32)))
        dq, dct = step(last, carry, True)
        dp_ref[0, pl.ds(pl.multiple_of(i * TQ, TQ), TQ), :] = dq.astype(dp_ref.dtype)
        dct_ref[0] = dct

        @pl.when(i == nq - 1)
        def _():
            dp_ref[1] = dk_acc[...].astype(dp_ref.dtype)
            dp_ref[2] = dv_acc[...].astype(dp_ref.dtype)

    return pl.pallas_call(
        body,
        grid=(nh, nq),
        in_specs=[
            pl.BlockSpec((TQ, HD), lambda h, i: (i, 2 * nh + h)),
            pl.BlockSpec((S, HD), lambda h, i: (0, 3 * nh + h)),
            pl.BlockSpec((S, HD), lambda h, i: (0, 4 * nh + h)),
            pl.BlockSpec((1, TQ, 1), lambda h, i: (h, i, 0)),
            pl.BlockSpec((1, S // TK, TK), lambda h, i: (h, 0, 0)),
            pl.BlockSpec((TQ, HD), lambda h, i: (i, nh + h)),
            pl.BlockSpec((1, TQ, 1), lambda h, i: (h, i, 0)),
            pl.BlockSpec((TQ, HD), lambda h, i: (i, nh + h)),
            pl.BlockSpec(memory_space=pl.ANY),
        ],
        out_specs=[
            pl.BlockSpec((3, S, HD), lambda h, i: (1, 0, h)),
            pl.BlockSpec((1, S // TK, TK), lambda h, i: (h, 0, 0)),
            pl.BlockSpec((1, TQ, 1), lambda h, i: (h, i, 0)),
        ],
        out_shape=[
            jax.ShapeDtypeStruct(dp.shape, dp.dtype),
            jax.ShapeDtypeStruct((nh, S // TK, TK), F32),
            jax.ShapeDtypeStruct((nh, S, 1), F32),
        ],
        input_output_aliases={8: 0},
        scratch_shapes=[pltpu.VMEM((S, HD), F32), pltpu.VMEM((S, HD), F32)],
        compiler_params=_cp(("parallel", "arbitrary")),
        name=name,
    )(p, p, p, ccol, crow, cat, lse, dcat, dp)


_GELU_K = math.sqrt(2.0 / math.pi)
_GELU_C = 0.044715


def _gelu(x):
    return 0.5 * x * (1.0 + jnp.tanh(_GELU_K * (x + _GELU_C * x * x * x)))


def _gelu_grad(x):
    t = jnp.tanh(_GELU_K * (x + _GELU_C * x * x * x))
    return 0.5 * (1.0 + t) + 0.5 * x * (1.0 - t * t) * (_GELU_K * (1.0 + 3.0 * _GELU_C * x * x))


def _layernorm_parts(gv):
    xc = gv - jnp.mean(gv, axis=-1, keepdims=True)
    r = lax.rsqrt(jnp.mean(xc * xc, axis=-1, keepdims=True) + EPS)
    return xc * r, r


def sg_fwd(p, sg_w, sg_bt, sg_g, W, name):
    S = p.shape[0]
    G, nq = W // HD, S // HD

    def body(u_ref, v_ref, w_ref, bt_ref, g_ref, o_ref):
        xh, _ = _layernorm_parts(_gelu(v_ref[...].astype(F32)))
        vn = (xh * g_ref[...]).astype(BF16)
        tri = _iota2((HD, HD), 0) >= _iota2((HD, HD), 1)
        for gi in range(G):
            cols = slice(gi * HD, (gi + 1) * HD)
            wt = jnp.where(tri, w_ref[gi], 0.0).astype(BF16)
            mixed = _dot(wt, vn[:, cols]) + bt_ref[:, gi : gi + 1]
            o_ref[:, cols] = (_gelu(u_ref[:, cols].astype(F32)) * mixed).astype(o_ref.dtype)

    return pl.pallas_call(
        body,
        grid=(nq,),
        in_specs=[
            pl.BlockSpec((HD, W), lambda i: (i, 0)),
            pl.BlockSpec((HD, W), lambda i: (i, 1)),
            pl.BlockSpec((G, HD, HD), lambda i: (0, 0, 0)),
            pl.BlockSpec((HD, G), lambda i: (0, 0)),
            pl.BlockSpec((1, W), lambda i: (0, 0)),
        ],
        out_specs=pl.BlockSpec((HD, W), lambda i: (i, 0)),
        out_shape=jax.ShapeDtypeStruct((S, 2 * W), BF16),
        compiler_params=_cp(("parallel",)),
        name=name,
    )(p, p, sg_w, sg_bt, sg_g.reshape(1, W))


def sg_bwd(p, sg_w, sg_bt, sg_g, dcat, W, name):
    S = p.shape[0]
    G, nq = W // HD, S // HD

    def body(u_ref, v_ref, w_ref, bt_ref, g_ref, do_ref, dp_ref, dw_ref, dbt_ref, dg_ref, dvn_scr):
        i = pl.program_id(0)

        @pl.when(i == 0)
        def _():
            dw_ref[...] = jnp.zeros_like(dw_ref)
            dbt_ref[...] = jnp.zeros_like(dbt_ref)
            dg_ref[...] = jnp.zeros_like(dg_ref)

        v = v_ref[...].astype(F32)
        xh, r = _layernorm_parts(_gelu(v))
        gg = g_ref[...]
        vn = (xh * gg).astype(BF16)
        tri = _iota2((HD, HD), 0) >= _iota2((HD, HD), 1)
        for gi in range(G):
            cols = slice(gi * HD, (gi + 1) * HD)
            wt = jnp.where(tri, w_ref[gi], 0.0).astype(BF16)
            mixed = _dot(wt, vn[:, cols]) + bt_ref[:, gi : gi + 1]
            u = u_ref[:, cols].astype(F32)
            do = do_ref[:, cols].astype(F32)
            dp_ref[0, :, cols] = (do * mixed * _gelu_grad(u)).astype(dp_ref.dtype)
            dmix = do * _gelu(u)
            dmb = dmix.astype(BF16)
            dw_ref[gi] += jnp.where(tri, _dot_nt(dmb, vn[:, cols]), 0.0)
            dbt_ref[:, gi : gi + 1] += jnp.sum(dmix, axis=1, keepdims=True)
            dvn_scr[:, cols] = _dot_tn(wt, dmb)
        dvn = dvn_scr[...]
        dg_ref[...] += jnp.sum(dvn * xh, axis=0, keepdims=True)
        dxh = dvn * gg
        dgv = r * (dxh - jnp.mean(dxh, axis=-1, keepdims=True) - xh * jnp.mean(dxh * xh, axis=-1, keepdims=True))
        dp_ref[1] = (dgv * _gelu_grad(v)).astype(dp_ref.dtype)

    return pl.pallas_call(
        body,
        grid=(nq,),
        in_specs=[
            pl.BlockSpec((HD, W), lambda i: (i, 0)),
            pl.BlockSpec((HD, W), lambda i: (i, 1)),
            pl.BlockSpec((G, HD, HD), lambda i: (0, 0, 0)),
            pl.BlockSpec((HD, G), lambda i: (0, 0)),
            pl.BlockSpec((1, W), lambda i: (0, 0)),
            pl.BlockSpec((HD, W), lambda i: (i, 0)),
        ],
        out_specs=[
            pl.BlockSpec((2, HD, W), lambda i: (0, i, 0)),
            pl.BlockSpec((G, HD, HD), lambda i: (0, 0, 0)),
            pl.BlockSpec((HD, G), lambda i: (0, 0)),
            pl.BlockSpec((1, W), lambda i: (0, 0)),
        ],
        out_shape=[
            jax.ShapeDtypeStruct((6, S, W), BF16),
            jax.ShapeDtypeStruct((G, HD, HD), F32),
            jax.ShapeDtypeStruct((HD, G), F32),
            jax.ShapeDtypeStruct((1, W), F32),
        ],
        scratch_shapes=[pltpu.VMEM((HD, W), F32)],
        compiler_params=_cp(("arbitrary",)),
        name=name,
    )(p, p, sg_w, sg_bt, sg_g.reshape(1, W), dcat)


def local_step(x, target, wts, at, on_grad):
    S, D = x.shape
    W = D // 2
    nb, F = wts["nb"], wts["F"]
    g = {}

    def ffn_fwd(xin, l):
        h = rms_fwd(xin, wts[f"{l}_ffn_norm_g"], f"{l}_ffn_rms")
        u = mm_nn(h, wts[f"{l}_ffn_up"], nb, f"{l}_ffn_up_mm")
        act = ffn_act_fwd(u, wts[f"{l}_ffn_conv_w"], F, f"{l}_ffn_act")
        xout = mm_nn(act, wts[f"{l}_ffn_down"], 1, f"{l}_ffn_down_mm", out_dtype=F32, res=xin,
                     tm=_pick(S, (1024, 512, 256, 128)), tn=_pick(D, (512, 256, 128)), tk=F)
        return xout, (xin, h, u, act)

    def ffn_bwd(dxout, dxoutb, saved, l):
        xin, h, u, act = saved
        dact = mm_nt(dxoutb, wts[f"{l}_ffn_down"], 1, S, F, f"{l}_ffn_down_dx", tko=_pick(F, (512, 256, 128)), tn=D)
        dact = on_grad(f"{l}_ffn_down", mm_tn(act, dxoutb, 1, D, f"{l}_ffn_down_dw", tn=D), dact)
        du, dcw = ffn_act_bwd(u, wts[f"{l}_ffn_conv_w"], dact, F, f"{l}_ffn_act_bwd")
        g[f"{l}_ffn_conv_w"] = jnp.concatenate([dcw[0], dcw[1]], axis=1)
        du2 = du.reshape(2 * S, F)
        n = wts[f"{l}_ffn_up"].shape[1]
        tn = _pick(n, (1408, 1024, 768, 512, 256, 128))
        per_half = F // tn
        nt = n // tn

        def up_block(i, j, t):
            vb = j * nt + t
            return vb // per_half, vb % per_half

        tm = _pick(S, (1024, 512, 256, 128))

        def nt_map(i, j, t):
            half, cb = up_block(i, j, t)
            return (half * (S // tm) + i, cb)

        def tn_map(j, t):
            half, cb = up_block(0, j, t)
            return (half, cb)

        dh = mm_nt(du2, wts[f"{l}_ffn_up"], nb, S, D, f"{l}_ffn_up_dx", dy_maps=[nt_map], tm=tm, tko=D, tn=tn)
        dh = on_grad(f"{l}_ffn_up", mm_tn(h, du2, nb, n, f"{l}_ffn_up_dw", dy_maps=[tn_map], tko=_pick(D, (1024, 512, 256, 128)), tn=tn), dh)
        dxin, dxinb, dg = rms_bwd(xin, wts[f"{l}_ffn_norm_g"], dh, dxout, f"{l}_ffn_rms_bwd")
        g[f"{l}_ffn_norm_g"] = dg
        return dxin, dxinb

    h0 = rms_fwd(x, wts["l0_mix_norm_g"], "l0_mix_rms")
    p0 = mm_nn(h0, wts["l0_w_in"], nb, "l0_w_in_mm")
    cat0 = sb_fwd(p0, W, "l0_sb_fwd")
    cat0 = sc_fwd(p0, wts["l0_sc_conv_w"], cat0, W, "l0_sc_fwd")
    x1 = mm_nn(cat0, wts["l0_w_out"], 1, "l0_w_out_mm", out_dtype=F32, res=x, tm=S, tn=_pick(D, (512, 256, 128)))
    x2, ffn0_saved = ffn_fwd(x1, "l0")

    x2 = at("l1_w_in", x2, None)
    nh = W // HD
    h2 = rms_fwd(x2, wts["l1_mix_norm_g"], "l1_mix_rms")
    p1 = mm_nt(h2, wts["l1_w_in_t"], 1, S, 5 * W, "l1_w_in_mm", tn=D)
    f = mm_nt(h2, wts["l1_w_f_t"], 1, S, 128, "l1_w_f_mm", out_dtype=F32, tn=D)
    bf = jnp.zeros((1, 128), F32).at[0, :nh].set(wts["l1_fox_b_f"])
    c = fox_gate_fwd(f, bf, "l1_fox_gate")
    c_heads = c[:, :nh].T
    ccol = c_heads[:, :, None]
    crow = c_heads.reshape(nh, S // _key_strip(S), _key_strip(S))
    sg_bt = wts["l1_sg_b"].T
    cat1 = sg_fwd(p1, wts["l1_sg_w"], sg_bt, wts["l1_sg_norm_g"], W, "l1_sg_fwd")
    cat1, lse = fox_fwd(p1, ccol, crow, cat1, W, "l1_fox_fwd")
    x3 = mm_nn(cat1, wts["l1_w_out"], 1, "l1_w_out_mm", out_dtype=F32, res=x2, tm=S, tn=_pick(D, (512, 256, 128)))
    x4, ffn1_saved = ffn_fwd(x3, "l1")

    dx4, dx4b, dgf, loss = loss_head(x4, wts["final_norm_g"], target, "loss_head")
    dx4b = at("loss", dx4b, loss)
    g["final_norm_g"] = dgf

    dx3, dx3b = ffn_bwd(dx4, dx4b, ffn1_saved, "l1")
    dcat1 = mm_nt(dx3b, wts["l1_w_out"], 1, S, D, "l1_w_out_dx", tn=D)
    dcat1 = on_grad("l1_w_out", mm_tn(cat1, dx3b, 1, D, "l1_w_out_dw", tn=D), dcat1)
    dp1, dsgw, dsgbt, dsgg = sg_bwd(p1, wts["l1_sg_w"], sg_bt, wts["l1_sg_norm_g"], dcat1, W, "l1_sg_bwd")
    dp1, dcs, dct = fox_bwd(p1, ccol, crow, cat1, lse, dcat1, dp1, W, "l1_fox_bwd")
    g["l1_sg_w"], g["l1_sg_b"], g["l1_sg_norm_g"] = dsgw, dsgbt.T, dsgg
    dc = jnp.zeros((S, 128), F32).at[:, :nh].set((dct[:, :, 0] - dcs.reshape(nh, S)).T)
    df, dbf = fox_gate_bwd(f, bf, dc, "l1_fox_gate_bwd")
    g["l1_fox_b_f"] = dbf[0, :nh]
    dfb = df.astype(BF16)
    tk1 = _pick(W, (1024, 512, 256, 128))
    tx1 = _pick(W, (512, 256, 128))
    tm1 = _pick(S, (1024, 512, 256, 128))
    part_of = lambda pt: pt + pt // 2 - pt // 4

    def a_map1(i, k):
        return (part_of(k // (W // tk1)) * (S // tm1) + i, k % (W // tk1))

    def x_map1(ko):
        return (part_of(ko // (W // tx1)), ko % (W // tx1))

    dp1_2d = dp1.reshape(6 * S, W)
    dw_main = mm_tn(dp1_2d, h2, 1, D, "l1_w_in_dw", tko=tx1, tn=D, x_map=x_map1, x_shape=(S, 5 * W))
    dw_f = mm_tn(dfb, h2, 1, D, "l1_w_f_dw", tn=D)
    dh2 = mm_nn(dfb, wts["l1_w_f_t"], 1, "l1_w_f_dx", out_dtype=F32)
    dh2 = mm_nn(dp1_2d, wts["l1_w_in_t"], 1, "l1_w_in_dx", res=dh2, tm=tm1, tk=tk1, a_map=a_map1, a_shape=(S, 5 * W))
    dh2 = on_grad("l1_w_in", jnp.concatenate([dw_main, dw_f[:nh]], axis=0), dh2)
    dx2, dx2b, dg = rms_bwd(x2, wts["l1_mix_norm_g"], dh2, dx3, "l1_mix_rms_bwd")
    g["l1_mix_norm_g"] = dg

    dx1, dx1b = ffn_bwd(dx2, dx2b, ffn0_saved, "l0")
    dcat0 = mm_nt(dx1b, wts["l0_w_out"], 1, S, D, "l0_w_out_dx", tn=D)
    dcat0 = on_grad("l0_w_out", mm_tn(cat0, dx1b, 1, D, "l0_w_out_dw", tn=D), dcat0)
    dp0 = sb_bwd(p0, dcat0, W, "l0_sb_bwd")
    dp0, dscw = sc_bwd(p0, wts["l0_sc_conv_w"], dcat0, dp0, W, "l0_sc_bwd")
    g["l0_sc_conv_w"] = dscw
    dp0 = at("small_ready", dp0, g)
    n0 = wts["l0_w_in"].shape[1]
    td0 = math.gcd(n0, W)
    nd0 = n0 // td0
    tm0 = _pick(S, (1024, 512, 256, 128))
    per_part0 = W // td0

    def nt_maps0(k):
        def f(i, j, t):
            vb = j * nd0 + k
            return ((vb // per_part0) * (S // tm0) + i, vb % per_part0)
        return f

    def tn_maps0(k):
        def f(j, t):
            vb = j * nd0 + k
            return (vb // per_part0, vb % per_part0)
        return f

    dp0_2d = dp0.reshape(6 * S, W)
    dw0 = mm_tn(h0, dp0_2d, nb, n0, "l0_w_in_dw", dy_maps=[tn_maps0(k) for k in range(nd0)], tko=_pick(D, (1024, 512, 256, 128)), tn=n0)
    dp0_2d = on_grad("l0_w_in", dw0, dp0_2d)
    dp0_2d = on_grad(None, None, dp0_2d)
    dh0 = mm_nt(dp0_2d, wts["l0_w_in"], nb, S, D, "l0_w_in_dx", dy_maps=[nt_maps0(k) for k in range(nd0)], tm=tm0, tko=D, tn=n0)
    dh0 = at("small_done", dh0, None)
    dx0, _, dg = rms_bwd(x, wts["l0_mix_norm_g"], dh0, dx1, "l0_mix_rms_bwd")
    g["l0_mix_norm_g"] = dg
    return dx0, g


GATHER_ID = 1


def _place():
    return lax.axis_index("x"), lax.axis_index("y"), lax.axis_index("c")


def _other_chips(x, y):
    return [(x, 1 - y), (1 - x, y), (1 - x, 1 - y)]


def _handshake(peers):
    barrier = pltpu.get_barrier_semaphore()
    for peer in peers:
        pl.semaphore_signal(barrier, inc=1, device_id=peer, device_id_type=MESH)
    pl.semaphore_wait(barrier, len(peers))


UPDATE_LAG = 2


def _on_sequencer(body, out_type, scratch_types, collective_id, name):
    return pl.kernel(
        body,
        out_type=out_type,
        mesh=plsc.ScalarSubcoreMesh(axis_name="seq", num_cores=1),
        scratch_types=scratch_types,
        compiler_params=pltpu.CompilerParams(collective_id=collective_id),
        name=name,
    )


def all_gather(arrs, name):
    n = len(arrs)

    def body(*refs):
        xs, outs = refs[:n], refs[n : 2 * n]
        send_sems, recv_sems, local_sems = refs[2 * n :]
        x, y, c = _place()
        me, sibling = (x, y, c), (x, y, 1 - c)
        chips = _other_chips(x, y)
        _handshake([sibling] + [(*chip, c) for chip in chips])

        def copy(a, k, block, to, src=None):
            px, py, pc = block
            dst = outs[a].at[4 * px + 2 * py + pc]
            return pltpu.make_async_remote_copy(
                src_ref=dst if src is None else src, dst_ref=dst,
                send_sem=send_sems.at[7 * a + k], recv_sem=recv_sems.at[7 * a + k], device_id=to, device_id_type=MESH,
            )

        mine = [pltpu.make_async_copy(xs[a], outs[a].at[4 * x + 2 * y + c], local_sems.at[a]) for a in range(n)]
        for cp in mine:
            cp.start()
        first = []
        for a in range(n):
            first.append(copy(a, 0, me, sibling, src=xs[a]))
            first += [copy(a, 1 + j, me, (*chip, c), src=xs[a]) for j, chip in enumerate(chips)]
        for cp in first:
            cp.start()
        passed = []
        for a in range(n):
            for j, chip in enumerate(chips):
                copy(a, 1 + j, (*chip, c), me).wait_recv()
                cp = copy(a, 4 + j, (*chip, c), sibling)
                cp.start()
                passed.append(cp)
        for a in range(n):
            copy(a, 0, sibling, me).wait_recv()
            for j, chip in enumerate(chips):
                copy(a, 4 + j, (*chip, 1 - c), me).wait_recv()
        for cp in first + passed:
            cp.wait_send()
        for cp in mine:
            cp.wait()

    out_type = [jax.ShapeDtypeStruct((NDEV,) + a.shape, a.dtype) for a in arrs]
    sems = [pltpu.SemaphoreType.DMA((7 * n,)), pltpu.SemaphoreType.DMA((7 * n,)), pltpu.SemaphoreType.DMA((n,))]
    return _on_sequencer(body, out_type, sems, GATHER_ID, name)(*arrs)


_IN_HBM = pl.BlockSpec(memory_space=pltpu.HBM)
_IN_SEM = pl.BlockSpec(memory_space=pltpu.SEMAPHORE)
_EFFECT = pltpu.SideEffectType.DATAFLOW_SIDE_EFFECTING


def _split_start_many(jobs, name):
    nj = len(jobs)

    def body(*refs):
        ins, outs = refs[: 2 * nj], refs[2 * nj :]
        for q, job in enumerate(jobs):
            for cp in job[0](ins[2 * q], ins[2 * q + 1], outs[3 * q], outs[3 * q + 1]):
                cp.start()
        outs[-1][...] = jnp.zeros_like(outs[-1])

    out_shape, out_specs, operands, aliases = [], [], [], {}
    for q, (_, src, land_shape, nsem) in enumerate(jobs):
        out_shape += [pltpu.SemaphoreType.DMA((nsem,)), pltpu.SemaphoreType.DMA((nsem,)), pltpu.HBM(land_shape, src.dtype)]
        out_specs += [_IN_SEM, _IN_SEM, _IN_HBM]
        operands += [src, pltpu.with_memory_space_constraint(lax.empty(land_shape, src.dtype), pltpu.HBM)]
        aliases[2 * q + 1] = 3 * q + 2
    res = pl.pallas_call(
        body,
        name=name,
        out_shape=tuple(out_shape) + (jax.ShapeDtypeStruct((8, 128), F32),),
        in_specs=(_IN_HBM,) * (2 * nj),
        out_specs=tuple(out_specs) + (pl.BlockSpec(memory_space=pltpu.VMEM),),
        input_output_aliases=aliases,
        compiler_params=pltpu.CompilerParams(has_side_effects=_EFFECT),
    )(*operands)
    return [[res[3 * q], res[3 * q + 1], jobs[q][1], res[3 * q + 2]] for q in range(nj)], res[-1]


def _split_start(make_copies, src, land_shape, nsem, name):
    (flying,), token = _split_start_many([(make_copies, src, land_shape, nsem)], name)
    return (*flying, token)


def _split_wait(make_copies, send_sems, recv_sems, src_thru, land_thru, after, name):
    def body(src_ref, land_ref, send_sems, recv_sems, after_ref, land_out):
        for cp in make_copies(src_ref, land_ref, send_sems, recv_sems):
            cp.wait_send()
            cp.wait_recv()

    return pl.pallas_call(
        body,
        name=name,
        out_shape=pltpu.HBM(land_thru.shape, land_thru.dtype),
        in_specs=(_IN_HBM, _IN_HBM, _IN_SEM, _IN_SEM, pl.BlockSpec(memory_space=pl.ANY)),
        out_specs=_IN_HBM,
        input_output_aliases={1: 0},
        compiler_params=pltpu.CompilerParams(has_side_effects=_EFFECT),
    )(src_thru, land_thru, send_sems, recv_sems, after)


def _pair_copies(src_ref, land_ref, send_sems, recv_sems):
    x, y, c = _place()
    return [
        pltpu.make_async_remote_copy(
            src_ref=src_ref.at[k, 1 - c], dst_ref=land_ref.at[k],
            send_sem=send_sems.at[k], recv_sem=recv_sems.at[k], device_id=(x, y, 1 - c), device_id_type=MESH,
        )
        for k in range(4)
    ]


def _direct_copies(src_ref, land_ref, send_sems, recv_sems):
    x, y, c = _place()
    me = 4 * x + 2 * y + c
    copies = []
    for k in range(NDEV - 1):
        to = (me + k + 1) % NDEV
        copies.append(pltpu.make_async_remote_copy(
            src_ref=src_ref, dst_ref=land_ref.at[me], send_sem=send_sems.at[k], recv_sem=recv_sems.at[k],
            device_id=(to // 4, (to // 2) % 2, to % 2), device_id_type=MESH,
        ))
    return copies


def _chip_copies(src_ref, land_ref, send_sems, recv_sems):
    x, y, c = _place()
    return [
        pltpu.make_async_remote_copy(
            src_ref=src_ref.at[2 * px + py], dst_ref=land_ref.at[2 * x + y],
            send_sem=send_sems.at[j], recv_sem=recv_sems.at[j], device_id=(px, py, c), device_id_type=MESH,
        )
        for j, (px, py) in enumerate(_other_chips(x, y))
    ]


def _row_tile(R, C, max_elems):
    if R * C <= max_elems:
        return R
    best = None
    for tr in range(16, R, 16):
        if R % tr == 0 and tr * C <= max_elems:
            best = tr
    return best or R


def pair_sum(a42, land4, core, name):
    _, _, R, C = a42.shape
    tr = _row_tile(R, C, 1 << 20)

    def body(core_ref, a_ref, l_ref, o_ref):
        o_ref[...] = (a_ref[0].astype(F32) + l_ref[...].astype(F32)).astype(o_ref.dtype)

    return pl.pallas_call(
        body,
        grid_spec=pltpu.PrefetchScalarGridSpec(
            num_scalar_prefetch=1,
            grid=(4, R // tr),
            in_specs=[
                pl.BlockSpec((1, 1, tr, C), lambda k, r, core_ref: (k, core_ref[0], r, 0)),
                pl.BlockSpec((1, tr, C), lambda k, r, core_ref: (k, r, 0)),
            ],
            out_specs=pl.BlockSpec((1, tr, C), lambda k, r, core_ref: (k, r, 0)),
        ),
        out_shape=jax.ShapeDtypeStruct((4, R, C), BF16),
        compiler_params=_cp(("parallel", "parallel")),
        name=name,
    )(core, a42, land4)


def sum_slots(parts, name):
    P, R, C = parts.shape

    def body(p_ref, o_ref):
        acc = p_ref[0].astype(F32)
        for k in range(1, P):
            acc = acc + p_ref[k].astype(F32)
        o_ref[...] = acc

    tr = _row_tile(R, P * C, 1 << 21)
    return pl.pallas_call(
        body,
        grid=(R // tr,),
        in_specs=[pl.BlockSpec((P, tr, C), lambda r: (0, r, 0))],
        out_specs=pl.BlockSpec((tr, C), lambda r: (r, 0)),
        out_shape=jax.ShapeDtypeStruct((R, C), F32),
        compiler_params=_cp(("parallel",)),
        name=name,
    )(parts)


def adamw(w, m, v, parts, name):
    R, C = w.shape
    P = parts.shape[0]
    tr = _pick(R, (256, 128, 64, 32, 16, 8))
    c1 = 1.0 - ADAM_B1 ** ADAM_STEP
    c2 = 1.0 - ADAM_B2 ** ADAM_STEP

    def body(w_ref, m_ref, v_ref, p_ref, g_ref, d_ref, nm_ref, nv_ref):
        g = p_ref[0].astype(F32)
        for k in range(1, P):
            g = g + p_ref[k].astype(F32)
        nm = ADAM_B1 * m_ref[...] + (1.0 - ADAM_B1) * g
        nv = ADAM_B2 * v_ref[...] + (1.0 - ADAM_B2) * (g * g)
        g_ref[...] = g
        nm_ref[...] = nm
        nv_ref[...] = nv
        d_ref[...] = -ADAM_LR * ((nm / c1) / (jnp.sqrt(nv / c2) + ADAM_EPS) + ADAM_WD * w_ref[...])

    blk = pl.BlockSpec((tr, C), lambda r: (r, 0))
    shp = jax.ShapeDtypeStruct((R, C), F32)
    return pl.pallas_call(
        body,
        grid=(R // tr,),
        in_specs=[blk, blk, blk, pl.BlockSpec((P, tr, C), lambda r: (0, r, 0))],
        out_specs=[blk, blk, blk, blk],
        out_shape=[shp, shp, shp, shp],
        compiler_params=_cp(("parallel",)),
        name=name,
    )(w, m, v, parts)


def adamw_reduced(w, m, v, own, land, chip, name):
    R, C = w.shape
    if R % 8 == 0:
        tr, tc = _pick(R, (256, 128, 64, 32, 16, 8)), C
    else:
        tr, tc = R, _pick(C, (256, 128))
    c1 = 1.0 - ADAM_B1 ** ADAM_STEP
    c2 = 1.0 - ADAM_B2 ** ADAM_STEP

    def body(chip_ref, w_ref, m_ref, v_ref, own_ref, land_ref, g_ref, d_ref, nm_ref, nv_ref):
        mine = own_ref[0].astype(F32)
        g = None
        for k in range(4):
            term = jnp.where(chip_ref[0] == k, mine, land_ref[k].astype(F32))
            g = term if g is None else g + term
        nm = ADAM_B1 * m_ref[...] + (1.0 - ADAM_B1) * g
        nv = ADAM_B2 * v_ref[...] + (1.0 - ADAM_B2) * (g * g)
        g_ref[...] = g
        nm_ref[...] = nm
        nv_ref[...] = nv
        d_ref[...] = -ADAM_LR * ((nm / c1) / (jnp.sqrt(nv / c2) + ADAM_EPS) + ADAM_WD * w_ref[...])

    blk = pl.BlockSpec((tr, tc), lambda r, c, chip_ref: (r, c))
    shp = jax.ShapeDtypeStruct((R, C), F32)
    return pl.pallas_call(
        body,
        grid_spec=pltpu.PrefetchScalarGridSpec(
            num_scalar_prefetch=1,
            grid=(R // tr, C // tc),
            in_specs=[
                blk, blk, blk,
                pl.BlockSpec((1, tr, tc), lambda r, c, chip_ref: (chip_ref[0], r, c)),
                pl.BlockSpec((4, tr, tc), lambda r, c, chip_ref: (0, r, c)),
            ],
            out_specs=[blk, blk, blk, blk],
        ),
        out_shape=[shp, shp, shp, shp],
        compiler_params=_cp(("parallel", "parallel")),
        name=name,
    )(chip, w, m, v, own, land)


_WEIGHTS = [
    "l0_mix_norm_g", "l0_w_in", "l0_sc_conv_w", "l0_w_out", "l0_ffn_norm_g", "l0_ffn_up", "l0_ffn_conv_w", "l0_ffn_down",
    "l1_mix_norm_g", "l1_w_in", "l1_fox_b_f", "l1_sg_w", "l1_sg_b", "l1_sg_norm_g", "l1_w_out", "l1_ffn_norm_g",
    "l1_ffn_up", "l1_ffn_conv_w", "l1_ffn_down", "final_norm_g",
]
_ROW_SHARDED = ["l0_w_out", "l0_ffn_down", "l1_w_out", "l1_ffn_down"]
_BIG = ["l0_w_in", "l0_w_out", "l0_ffn_up", "l0_ffn_down", "l1_w_in", "l1_w_out", "l1_ffn_up", "l1_ffn_down"]
_CONV = ["l0_sc_conv_w", "l0_ffn_conv_w", "l1_ffn_conv_w"]
_SMALL = [n for n in _WEIGHTS if n not in _BIG]
_LAST_SMALL = "l0_mix_norm_g"
_PACK_ROWS = 8


def _pack(arrs):
    flat = []
    for a in arrs:
        v = a.reshape(-1).astype(F32)
        pad = (-v.shape[0]) % (_PACK_ROWS * 128)
        flat.append(jnp.pad(v, (0, pad)))
    return jnp.concatenate(flat).reshape(-1, 128)


def _unpack(packed, shapes):
    out, off = [], 0
    flat = packed.reshape(-1)
    for shp in shapes:
        size = math.prod(shp)
        out.append(flat[off : off + size].reshape(shp))
        off += size + (-size) % (_PACK_ROWS * 128)
    return out


def kernel(x, l0_mix_norm_g, l0_w_in, l0_sc_conv_w, l0_w_out, l0_ffn_norm_g, l0_ffn_up, l0_ffn_conv_w, l0_ffn_down, l1_mix_norm_g, l1_w_in, l1_fox_b_f, l1_sg_w, l1_sg_b, l1_sg_norm_g, l1_w_out, l1_ffn_norm_g, l1_ffn_up, l1_ffn_conv_w, l1_ffn_down, final_norm_g, loss_target, m_l0_mix_norm_g, m_l0_w_in, m_l0_sc_conv_w, m_l0_w_out, m_l0_ffn_norm_g, m_l0_ffn_up, m_l0_ffn_conv_w, m_l0_ffn_down, m_l1_mix_norm_g, m_l1_w_in, m_l1_fox_b_f, m_l1_sg_w, m_l1_sg_b, m_l1_sg_norm_g, m_l1_w_out, m_l1_ffn_norm_g, m_l1_ffn_up, m_l1_ffn_conv_w, m_l1_ffn_down, m_final_norm_g, v_l0_mix_norm_g, v_l0_w_in, v_l0_sc_conv_w, v_l0_w_out, v_l0_ffn_norm_g, v_l0_ffn_up, v_l0_ffn_conv_w, v_l0_ffn_down, v_l1_mix_norm_g, v_l1_w_in, v_l1_fox_b_f, v_l1_sg_w, v_l1_sg_b, v_l1_sg_norm_g, v_l1_w_out, v_l1_ffn_norm_g, v_l1_ffn_up, v_l1_ffn_conv_w, v_l1_ffn_down, v_final_norm_g):
    given = dict(locals())
    w = {n: given[n] for n in _WEIGHTS}
    mom = {n: given["m_" + n] for n in _WEIGHTS}
    var = {n: given["v_" + n] for n in _WEIGHTS}
    xs, target = x[0], loss_target[0]
    S, D = xs.shape
    W = D // 2
    nh = W // HD
    cx, cy, cc = _place()
    me = 4 * cx + 2 * cy + cc

    wts = {"nb": NDEV, "F": l0_ffn_down.shape[0] * NDEV}
    for n in _SMALL:
        if n not in _CONV:
            wts[n] = w[n]
    gathered, loss_sum = {}, []

    def start_gather(names):
        srcs = [(w[n].T if n == "l1_w_in" else w[n]).astype(BF16) for n in names]
        taps = [w[c] for c in _CONV] if names[0] == _BIG[0] else []
        got = all_gather(srcs + taps, "gather_" + "_".join(names))
        for n, full in zip(names, got):
            if n == "l1_w_in":
                gathered[n] = full
            elif n in _ROW_SHARDED:
                wts[n] = full.reshape(-1, D)
            else:
                wts[n] = full.reshape(NDEV * D, -1)
        for c, full in zip(_CONV, got[len(names):] if taps else []):
            wts[c] = full.transpose(1, 0, 2).reshape(CONV_K, -1)

    def at(point, after, value):
        if point == "l1_w_in":
            got, after = lax.optimization_barrier((gathered[point], after))
            wts["l1_w_in_t"] = got.reshape(-1, D)
            wts["l1_w_f_t"] = jnp.pad(wts["l1_w_in_t"][5 * W :], ((0, 128 - nh), (0, 0)))
        elif point == "loss":
            gathered["loss"] = value[0, :1]
        elif point == "small_ready":
            early = [n for n in _SMALL if n != _LAST_SMALL]
            gathered["small"] = all_gather([_pack([value[n] for n in early] + [gathered["loss"]])], "gather_small_grads")[0]
        elif point == "small_done":
            after = update_small([n for n in _SMALL if n != _LAST_SMALL], gathered["small"], "small", after, True)
        return after

    out_g, out_d, out_m, out_v = {}, {}, {}, {}

    def update_small(names, all_terms, tag, after=None, with_loss=False):
        shapes = [w[n].shape for n in names]
        full_shapes = [(CONV_K, NDEV * w[n].shape[1]) if n in _CONV else w[n].shape for n in names]
        summed = _unpack(sum_slots(all_terms, f"sum_{tag}_grads"), full_shapes + ([(1,)] if with_loss else []))
        if with_loss:
            loss_sum.append(summed[-1][0])
        grads = {}
        for n, t in zip(names, summed):
            if n in _CONV:
                cols = w[n].shape[1]
                t = lax.dynamic_slice_in_dim(t, me * cols, cols, axis=1)
            grads[n] = t
        res = adamw(
            _pack([w[n] for n in names]), _pack([mom[n] for n in names]), _pack([var[n] for n in names]),
            _pack([grads[n] for n in names])[None], f"adamw_{tag}",
        )
        if after is not None:
            res, after = lax.optimization_barrier((res, after))
        for dst, packed_out in zip((out_g, out_d, out_m, out_v), res):
            for n, t in zip(names, _unpack(packed_out, shapes)):
                dst[n] = t
        return after

    core = jnp.reshape(cc, (1,)).astype(jnp.int32)
    chip = jnp.reshape(2 * cx + cy, (1,)).astype(jnp.int32)
    pair_flying, chip_flying = [], []

    def tie(value, after):
        if after is None:
            return value, None
        return lax.optimization_barrier((value, after))

    def advance(after, new=None):
        jobs, names = [], []
        if pair_flying:
            n0, flying = pair_flying.pop()
            landed = _split_wait(_pair_copies, *flying, f"reduce_pair_wait_{n0}")
            summed = pair_sum(flying[2], landed, core, f"pair_sum_{n0}")
            jobs.append((_chip_copies, summed, summed.shape, 3))
            names.append(n0)
        if new is not None:
            jobs.append((_pair_copies, new[1], new[1].shape[:1] + new[1].shape[2:], 4))
            names.append(new[0])
        started, token = _split_start_many(jobs, "reduce_start_" + "_".join(names))
        token, after = tie(token, after)
        if new is not None:
            pair_flying.append((new[0], started.pop() + [token]))
        if started:
            chip_flying.append((names[0], started[0] + [token]))
        return after

    def update(after, behind=None):
        n, flying = chip_flying.pop(0)
        if behind is not None:
            flying[4], _ = lax.optimization_barrier((flying[4], behind))
        landed = _split_wait(_chip_copies, *flying, f"reduce_chips_wait_{n}")
        turn = (lambda t: t.T) if n == "l1_w_in" else (lambda t: t)
        res = adamw_reduced(turn(w[n]), turn(mom[n]), turn(var[n]), flying[2], landed, chip, f"adamw_{n}")
        res, after = tie(res, after)
        out_g[n], out_d[n], out_m[n], out_v[n] = [turn(t) for t in res]
        return after, res[0]

    def on_grad(n, term, after):
        if n is None:
            return advance(after)
        if n in _ROW_SHARDED or n == "l1_w_in":
            term = term.reshape(NDEV, -1, D)
        else:
            term = term.reshape(NDEV, D, -1)
        term = term.reshape((4, 2) + term.shape[1:])
        if len(chip_flying) == UPDATE_LAG:
            after, _ = update(after)
        return advance(after, (n, term))

    for n in _BIG:
        start_gather([n])
    dx, g = local_step(xs, target, wts, at, on_grad)
    last = _pack([g[_LAST_SMALL]])
    *flying, done = _split_start(_direct_copies, last, (NDEV,) + last.shape, NDEV - 1, "gather_last_grad")
    while len(chip_flying) > 1:
        _, done = update(None, behind=done)
    landed = _split_wait(_direct_copies, *flying, done, "gather_last_grad_wait")
    update_small([_LAST_SMALL], lax.dynamic_update_slice(landed, last[None], (me, 0, 0)), "last")
    update(None, behind=out_g[_LAST_SMALL])
    loss = loss_sum[0]

    return (loss, dx[None], *[out_g[n] for n in _WEIGHTS], *[out_d[n] for n in _WEIGHTS],
            *[out_m[n] for n in _WEIGHTS], *[out_v[n] for n in _WEIGHTS])
```
